```python
import math
import jax, jax.numpy as jnp
from jax import lax
import numpy as np

D_MODEL = 1024
BATCH = 8
SEQ = 4096
DEPTH = 1

N_Q_HEADS = 8
N_KV_HEADS = 2
HEAD_DIM = 64
ATTN_WIDTH = N_Q_HEADS * HEAD_DIM
KV_WIDTH = N_KV_HEADS * HEAD_DIM
WINDOW = 128
BLOCK = 128
ROPE_DIM = HEAD_DIM // 4
ROPE_THETA = 500000.0
SSM_WIDTH = D_MODEL // 2
SSM_GROUP = 16
N_SSM_GROUPS = SSM_WIDTH // SSM_GROUP
SSM_STATE = 64
N_DIRS = 2
DT_MIN = 1e-3
DT_MAX = 1e-1
MIX_WIDTH = ATTN_WIDTH + SSM_WIDTH
IN_WIDTH = ATTN_WIDTH + 2 * KV_WIDTH + SSM_WIDTH
D_FF = 2816
CONV_WIDTH = 3
EPS = 1e-6

kernel_name = "hymba_s5_swa_convffn_encoder"

F32 = jnp.float32


def rms_norm(x, g):
    xf = x.astype(F32)
    y = xf * lax.rsqrt(jnp.mean(xf * xf, axis=-1, keepdims=True) + EPS)
    return (y * g.astype(F32)).astype(x.dtype)


def partial_rope(t, pos):
    half = ROPE_DIM // 2
    inv_freq = jnp.power(ROPE_THETA, -jnp.arange(half, dtype=F32) / half)
    ang = pos.astype(F32)[:, None] * inv_freq[None, :]
    cos = jnp.cos(ang)[None, :, None, :]
    sin = jnp.sin(ang)[None, :, None, :]
    tf = t.astype(F32)
    t1 = tf[..., :half]
    t2 = tf[..., half:ROPE_DIM]
    rest = tf[..., ROPE_DIM:]
    out = jnp.concatenate([t1 * cos - t2 * sin, t2 * cos + t1 * sin, rest], axis=-1)
    return out.astype(t.dtype)


def window_attention(q, k, v, sink):
    b, l = q.shape[0], q.shape[1]
    nb = l // BLOCK
    grp = N_Q_HEADS // N_KV_HEADS
    qb = q.astype(F32).reshape(b, nb, BLOCK, N_KV_HEADS, grp, HEAD_DIM)

    def band(t):
        tp = jnp.pad(t.astype(F32), ((0, 0), (BLOCK, BLOCK), (0, 0), (0, 0)))
        tb = tp.reshape(b, nb + 2, BLOCK, N_KV_HEADS, HEAD_DIM)
        return jnp.concatenate([tb[:, :-2], tb[:, 1:-1], tb[:, 2:]], axis=2)

    kw = band(k)
    vw = band(v)
    s = jnp.einsum('bnqhgd,bnkhd->bnhgqk', qb, kw) * (HEAD_DIM ** -0.5)
    qpos = jnp.arange(nb)[:, None] * BLOCK + jnp.arange(BLOCK)[None, :]
    kpos = (jnp.arange(nb)[:, None] - 1) * BLOCK + jnp.arange(3 * BLOCK)[None, :]
    rel = kpos[:, None, :] - qpos[:, :, None]
    valid = (jnp.abs(rel) <= WINDOW) & (kpos[:, None, :] >= 0) & (kpos[:, None, :] < l)
    s = jnp.where(valid[None, :, None, None], s, -jnp.inf)
    sink_l = sink.astype(F32).reshape(N_KV_HEADS, grp)[None, None, :, :, None, None]
    m = jnp.maximum(jnp.max(s, axis=-1, keepdims=True), sink_l)
    p = jnp.exp(s - m)
    denom = jnp.sum(p, axis=-1, keepdims=True) + jnp.exp(sink_l - m)
    o = jnp.einsum('bnhgqk,bnkhd->bnqhgd', p / denom, vw)
    return o.reshape(b, l, ATTN_WIDTH)


def _scan_op(e1, e2):
    a1, x1 = e1
    a2, x2 = e2
    return a1 * a2, a2 * x1 + x2


def s5_bidirectional(u, a_re, a_im, log_step, b_re, b_im, c_re, c_im, d_skip):
    bsz, l = u.shape[0], u.shape[1]
    ug = u.astype(F32).reshape(bsz, l, N_SSM_GROUPS, SSM_GROUP)
    lam = lax.complex(a_re.astype(F32), a_im.astype(F32))
    step = jnp.exp(log_step.astype(F32))[..., None]
    lam_bar = jnp.exp(lam * step)
    bmat = lax.complex(b_re.astype(F32), b_im.astype(F32))
    b_bar = ((lam_bar - 1.0) / lam)[..., None] * bmat
    cmat = lax.complex(c_re.astype(F32), c_im.astype(F32))
    y = d_skip.astype(F32)[None, None] * ug
    for direction, rev in ((0, False), (1, True)):
        bu = jnp.einsum('blgc,gpc->blgp', ug, b_bar[direction])
        a = jnp.broadcast_to(lam_bar[direction][None, None], bu.shape)
        _, states = lax.associative_scan(_scan_op, (a, bu), reverse=rev, axis=1)
        y = y + jnp.real(jnp.einsum('gcp,blgp->blgc', cmat[direction], states))
    return y.reshape(bsz, l, SSM_WIDTH)


def depthwise_conv(t, w, bias):
    c = t.shape[-1]
    pad = CONV_WIDTH // 2
    out = lax.conv_general_dilated(t, w[:, None, :].astype(t.dtype), window_strides=(1,),
                                   padding=((pad, pad),), dimension_numbers=('NWC', 'WIO', 'NWC'),
                                   feature_group_count=c)
    return out + bias.astype(out.dtype)


def _fwd_setup_inputs(seed: int = 0) -> dict:
    key = jax.random.key(seed)
    ks = jax.random.split(key, 24)
    G, P, C = N_SSM_GROUPS, SSM_STATE, SSM_GROUP
    nrm = lambda k, shape: jax.random.normal(k, shape, dtype=F32)
    x = nrm(ks[0], (BATCH, SEQ, D_MODEL))
    norm_mix_g = 1.0 + 0.02 * nrm(ks[1], (DEPTH, D_MODEL))
    w_in = nrm(ks[2], (DEPTH, D_MODEL, IN_WIDTH)) * D_MODEL ** -0.5
    n_idx = jnp.arange(P, dtype=F32)
    a_re = -0.5 + 0.01 * nrm(ks[3], (DEPTH, N_DIRS, G, P))
    a_im = math.pi * n_idx + 0.01 * nrm(ks[4], (DEPTH, N_DIRS, G, P))
    log_step = jax.random.uniform(ks[5], (DEPTH, N_DIRS, G), dtype=F32,
                                  minval=math.log(DT_MIN), maxval=math.log(DT_MAX))
    b_re = nrm(ks[6], (DEPTH, N_DIRS, G, P, C)) * (2.0 * C) ** -0.5
    b_im = nrm(ks[7], (DEPTH, N_DIRS, G, P, C)) * (2.0 * C) ** -0.5
    c_re = nrm(ks[8], (DEPTH, N_DIRS, G, C, P)) * P ** -0.5
    c_im = nrm(ks[9], (DEPTH, N_DIRS, G, C, P)) * P ** -0.5
    d_skip = 0.5 * nrm(ks[10], (DEPTH, G, C))
    w_glu = nrm(ks[11], (DEPTH, SSM_WIDTH, SSM_WIDTH)) * SSM_WIDTH ** -0.5
    sink = 0.5 * nrm(ks[12], (DEPTH, N_Q_HEADS))
    norm_attn_g = 1.0 + 0.02 * nrm(ks[13], (DEPTH, ATTN_WIDTH))
    norm_ssm_g = 1.0 + 0.02 * nrm(ks[14], (DEPTH, SSM_WIDTH))
    w_out = nrm(ks[15], (DEPTH, MIX_WIDTH, D_MODEL)) * MIX_WIDTH ** -0.5
    norm_ffn_g = 1.0 + 0.02 * nrm(ks[16], (DEPTH, D_MODEL))
    w_up = nrm(ks[17], (DEPTH, D_MODEL, 2 * D_FF)) * D_MODEL ** -0.5
    conv_w = nrm(ks[18], (DEPTH, CONV_WIDTH, 2 * D_FF)) * CONV_WIDTH ** -0.5
    conv_b = 0.02 * nrm(ks[19], (DEPTH, 2 * D_FF))
    w_down = nrm(ks[20], (DEPTH, D_FF, D_MODEL)) * D_FF ** -0.5
    norm_final_g = 1.0 + 0.02 * nrm(ks[21], (D_MODEL,))
    return {"x": x, "norm_mix_g": norm_mix_g, "w_in": w_in, "a_re": a_re, "a_im": a_im,
            "log_step": log_step, "b_re": b_re, "b_im": b_im, "c_re": c_re, "c_im": c_im,
            "d_skip": d_skip, "w_glu": w_glu, "sink": sink, "norm_attn_g": norm_attn_g,
            "norm_ssm_g": norm_ssm_g, "w_out": w_out, "norm_ffn_g": norm_ffn_g, "w_up": w_up,
            "conv_w": conv_w, "conv_b": conv_b, "w_down": w_down, "norm_final_g": norm_final_g}


def _fwd_reference(x, norm_mix_g, w_in, a_re, a_im, log_step, b_re, b_im, c_re, c_im, d_skip, w_glu,
              sink, norm_attn_g, norm_ssm_g, w_out, norm_ffn_g, w_up, conv_w, conv_b, w_down,
              norm_final_g):
    bsz, l = x.shape[0], x.shape[1]
    pos = jnp.arange(l)
    for i in range(DEPTH):
        h = rms_norm(x, norm_mix_g[i])
        proj = h @ w_in[i]
        q = proj[..., :ATTN_WIDTH].reshape(bsz, l, N_Q_HEADS, HEAD_DIM)
        k = proj[..., ATTN_WIDTH:ATTN_WIDTH + KV_WIDTH].reshape(bsz, l, N_KV_HEADS, HEAD_DIM)
        v = proj[..., ATTN_WIDTH + KV_WIDTH:ATTN_WIDTH + 2 * KV_WIDTH].reshape(bsz, l, N_KV_HEADS, HEAD_DIM)
        u = proj[..., ATTN_WIDTH + 2 * KV_WIDTH:]
        q = partial_rope(q, pos)
        k = partial_rope(k, pos)
        attn = window_attention(q, k, v, sink[i])
        ys = s5_bidirectional(u, a_re[i], a_im[i], log_step[i], b_re[i], b_im[i],
                              c_re[i], c_im[i], d_skip[i])
        ys = jax.nn.gelu(ys, approximate=False)
        ys = ys * jax.nn.sigmoid(ys @ w_glu[i].astype(F32))
        mixed = jnp.concatenate([rms_norm(attn, norm_attn_g[i]), rms_norm(ys, norm_ssm_g[i])], axis=-1)
        x = x + (mixed.astype(x.dtype) @ w_out[i]).astype(x.dtype)
        h = rms_norm(x, norm_ffn_g[i])
        up = depthwise_conv(h @ w_up[i], conv_w[i], conv_b[i])
        gate = up[..., :D_FF]
        val = up[..., D_FF:]
        x = x + ((jax.nn.silu(gate) * val) @ w_down[i]).astype(x.dtype)
    return rms_norm(x, norm_final_g)


import jax as _jax
import jax.numpy as _jnp

TWIN_FORMAT = 'train_step'
FWD_PARAMS = ['x', 'norm_mix_g', 'w_in', 'a_re', 'a_im', 'log_step', 'b_re', 'b_im', 'c_re', 'c_im', 'd_skip', 'w_glu', 'sink', 'norm_attn_g', 'norm_ssm_g', 'w_out', 'norm_ffn_g', 'w_up', 'conv_w', 'conv_b', 'w_down', 'norm_final_g']
TWIN_WEIGHTS = ['norm_mix_g', 'w_in', 'a_re', 'a_im', 'log_step', 'b_re', 'b_im', 'c_re', 'c_im', 'd_skip', 'w_glu', 'sink', 'norm_attn_g', 'norm_ssm_g', 'w_out', 'norm_ffn_g', 'w_up', 'conv_w', 'conv_b', 'w_down', 'norm_final_g']
TWIN_DIFF_INPUT = 'x'
TWIN_INPUTS = ['x', 'norm_mix_g', 'w_in', 'a_re', 'a_im', 'log_step', 'b_re', 'b_im', 'c_re', 'c_im', 'd_skip', 'w_glu', 'sink', 'norm_attn_g', 'norm_ssm_g', 'w_out', 'norm_ffn_g', 'w_up', 'conv_w', 'conv_b', 'w_down', 'norm_final_g', 'loss_target', 'm_norm_mix_g', 'm_w_in', 'm_a_re', 'm_a_im', 'm_log_step', 'm_b_re', 'm_b_im', 'm_c_re', 'm_c_im', 'm_d_skip', 'm_w_glu', 'm_sink', 'm_norm_attn_g', 'm_norm_ssm_g', 'm_w_out', 'm_norm_ffn_g', 'm_w_up', 'm_conv_w', 'm_conv_b', 'm_w_down', 'm_norm_final_g', 'v_norm_mix_g', 'v_w_in', 'v_a_re', 'v_a_im', 'v_log_step', 'v_b_re', 'v_b_im', 'v_c_re', 'v_c_im', 'v_d_skip', 'v_w_glu', 'v_sink', 'v_norm_attn_g', 'v_norm_ssm_g', 'v_w_out', 'v_norm_ffn_g', 'v_w_up', 'v_conv_w', 'v_conv_b', 'v_w_down', 'v_norm_final_g']
TWIN_OUTPUTS = ['loss', 'grad_x', 'grad_norm_mix_g', 'grad_w_in', 'grad_a_re', 'grad_a_im', 'grad_log_step', 'grad_b_re', 'grad_b_im', 'grad_c_re', 'grad_c_im', 'grad_d_skip', 'grad_w_glu', 'grad_sink', 'grad_norm_attn_g', 'grad_norm_ssm_g', 'grad_w_out', 'grad_norm_ffn_g', 'grad_w_up', 'grad_conv_w', 'grad_conv_b', 'grad_w_down', 'grad_norm_final_g', 'delta_norm_mix_g', 'delta_w_in', 'delta_a_re', 'delta_a_im', 'delta_log_step', 'delta_b_re', 'delta_b_im', 'delta_c_re', 'delta_c_im', 'delta_d_skip', 'delta_w_glu', 'delta_sink', 'delta_norm_attn_g', 'delta_norm_ssm_g', 'delta_w_out', 'delta_norm_ffn_g', 'delta_w_up', 'delta_conv_w', 'delta_conv_b', 'delta_w_down', 'delta_norm_final_g', 'new_m_norm_mix_g', 'new_m_w_in', 'new_m_a_re', 'new_m_a_im', 'new_m_log_step', 'new_m_b_re', 'new_m_b_im', 'new_m_c_re', 'new_m_c_im', 'new_m_d_skip', 'new_m_w_glu', 'new_m_sink', 'new_m_norm_attn_g', 'new_m_norm_ssm_g', 'new_m_w_out', 'new_m_norm_ffn_g', 'new_m_w_up', 'new_m_conv_w', 'new_m_conv_b', 'new_m_w_down', 'new_m_norm_final_g', 'new_v_norm_mix_g', 'new_v_w_in', 'new_v_a_re', 'new_v_a_im', 'new_v_log_step', 'new_v_b_re', 'new_v_b_im', 'new_v_c_re', 'new_v_c_im', 'new_v_d_skip', 'new_v_w_glu', 'new_v_sink', 'new_v_norm_attn_g', 'new_v_norm_ssm_g', 'new_v_w_out', 'new_v_norm_ffn_g', 'new_v_w_up', 'new_v_conv_w', 'new_v_conv_b', 'new_v_w_down', 'new_v_norm_final_g']
TWIN_LEAF_KINDS = {'loss': 'loss', 'grad_x': 'grad_x', 'grad_norm_mix_g': 'grad_w', 'grad_w_in': 'grad_w', 'grad_a_re': 'grad_w', 'grad_a_im': 'grad_w', 'grad_log_step': 'grad_w', 'grad_b_re': 'grad_w', 'grad_b_im': 'grad_w', 'grad_c_re': 'grad_w', 'grad_c_im': 'grad_w', 'grad_d_skip': 'grad_w', 'grad_w_glu': 'grad_w', 'grad_sink': 'grad_w', 'grad_norm_attn_g': 'grad_w', 'grad_norm_ssm_g': 'grad_w', 'grad_w_out': 'grad_w', 'grad_norm_ffn_g': 'grad_w', 'grad_w_up': 'grad_w', 'grad_conv_w': 'grad_w', 'grad_conv_b': 'grad_w', 'grad_w_down': 'grad_w', 'grad_norm_final_g': 'grad_w', 'delta_norm_mix_g': 'delta_w', 'delta_w_in': 'delta_w', 'delta_a_re': 'delta_w', 'delta_a_im': 'delta_w', 'delta_log_step': 'delta_w', 'delta_b_re': 'delta_w', 'delta_b_im': 'delta_w', 'delta_c_re': 'delta_w', 'delta_c_im': 'delta_w', 'delta_d_skip': 'delta_w', 'delta_w_glu': 'delta_w', 'delta_sink': 'delta_w', 'delta_norm_attn_g': 'delta_w', 'delta_norm_ssm_g': 'delta_w', 'delta_w_out': 'delta_w', 'delta_norm_ffn_g': 'delta_w', 'delta_w_up': 'delta_w', 'delta_conv_w': 'delta_w', 'delta_conv_b': 'delta_w', 'delta_w_down': 'delta_w', 'delta_norm_final_g': 'delta_w', 'new_m_norm_mix_g': 'new_m', 'new_m_w_in': 'new_m', 'new_m_a_re': 'new_m', 'new_m_a_im': 'new_m', 'new_m_log_step': 'new_m', 'new_m_b_re': 'new_m', 'new_m_b_im': 'new_m', 'new_m_c_re': 'new_m', 'new_m_c_im': 'new_m', 'new_m_d_skip': 'new_m', 'new_m_w_glu': 'new_m', 'new_m_sink': 'new_m', 'new_m_norm_attn_g': 'new_m', 'new_m_norm_ssm_g': 'new_m', 'new_m_w_out': 'new_m', 'new_m_norm_ffn_g': 'new_m', 'new_m_w_up': 'new_m', 'new_m_conv_w': 'new_m', 'new_m_conv_b': 'new_m', 'new_m_w_down': 'new_m', 'new_m_norm_final_g': 'new_m', 'new_v_norm_mix_g': 'new_v', 'new_v_w_in': 'new_v', 'new_v_a_re': 'new_v', 'new_v_a_im': 'new_v', 'new_v_log_step': 'new_v', 'new_v_b_re': 'new_v', 'new_v_b_im': 'new_v', 'new_v_c_re': 'new_v', 'new_v_c_im': 'new_v', 'new_v_d_skip': 'new_v', 'new_v_w_glu': 'new_v', 'new_v_sink': 'new_v', 'new_v_norm_attn_g': 'new_v', 'new_v_norm_ssm_g': 'new_v', 'new_v_w_out': 'new_v', 'new_v_norm_ffn_g': 'new_v', 'new_v_w_up': 'new_v', 'new_v_conv_w': 'new_v', 'new_v_conv_b': 'new_v', 'new_v_w_down': 'new_v', 'new_v_norm_final_g': 'new_v'}


def _forward(args):
    return _fwd_reference(*[args[k] for k in FWD_PARAMS])


def _output_shape():
    def fwd():
        inp = _fwd_setup_inputs(0)
        return _fwd_reference(*[inp[k] for k in FWD_PARAMS])
    out = _jax.eval_shape(fwd)
    return out.shape, out.dtype

N_MICROBATCH = 1
ADAM_LR = 0.001
ADAM_B1 = 0.9
ADAM_B2 = 0.999
ADAM_EPS = 1e-08
ADAM_WD = 0.01
ADAM_STEP = 10
PER_EXAMPLE_BATCH_AXIS = {'x': 0, 'loss_target': 0}
SHARED_INPUTS = []
_WEIGHT_DTYPES = {'norm_mix_g': _jnp.float32, 'w_in': _jnp.float32, 'a_re': _jnp.float32, 'a_im': _jnp.float32, 'log_step': _jnp.float32, 'b_re': _jnp.float32, 'b_im': _jnp.float32, 'c_re': _jnp.float32, 'c_im': _jnp.float32, 'd_skip': _jnp.float32, 'w_glu': _jnp.float32, 'sink': _jnp.float32, 'norm_attn_g': _jnp.float32, 'norm_ssm_g': _jnp.float32, 'w_out': _jnp.float32, 'norm_ffn_g': _jnp.float32, 'w_up': _jnp.float32, 'conv_w': _jnp.float32, 'conv_b': _jnp.float32, 'w_down': _jnp.float32, 'norm_final_g': _jnp.float32}
MOMENT_SCALE = {'norm_mix_g': 2.191902e-01, 'w_in': 1.740062e-01, 'a_re': 2.049251e-02, 'a_im': 1.839727e-02, 'log_step': 8.356995e+00, 'b_re': 1.274983e-02, 'b_im': 1.286561e-02, 'c_re': 1.836186e-02, 'c_im': 1.783418e-02, 'd_skip': 2.904266e-01, 'w_glu': 2.110848e-02, 'sink': 2.617174e-03, 'norm_attn_g': 1.436272e-01, 'norm_ssm_g': 1.446507e-01, 'w_out': 1.396038e-01, 'norm_ffn_g': 9.641647e-02, 'w_up': 4.232713e-02, 'conv_w': 4.327661e-02, 'conv_b': 4.362053e-02, 'w_down': 6.920310e-02, 'norm_final_g': 3.204666e+01}


def _to_microbatches(a, axis):
    t = _jnp.moveaxis(a, axis, 0)
    t = t.reshape((N_MICROBATCH, t.shape[0] // N_MICROBATCH) + t.shape[1:])
    return _jnp.moveaxis(t, 1, axis + 1)


def setup_inputs(seed: int = 0) -> dict:
    inp = _fwd_setup_inputs(seed)
    key = _jax.random.fold_in(_jax.random.key(seed), 7919)
    shape, _ = _output_shape()
    out = dict(inp)
    out["loss_target"] = _jax.random.normal(_jax.random.fold_in(key, 0), shape, _jnp.float32)
    for i, name in enumerate(TWIN_WEIGHTS):
        w = inp[name].astype(_jnp.float32)
        if MOMENT_SCALE is None:
            s = _jnp.sqrt(_jnp.mean(_jnp.square(w)) + 1e-30)
        else:
            s = MOMENT_SCALE[name]
        km, kv = _jax.random.split(_jax.random.fold_in(key, i + 1))
        out[name] = w
        out["m_" + name] = s * _jax.random.normal(km, w.shape, _jnp.float32)
        out["v_" + name] = (s * s) * _jax.random.uniform(kv, w.shape, _jnp.float32, 0.5, 1.5)
    if N_MICROBATCH > 1:
        for name, axis in PER_EXAMPLE_BATCH_AXIS.items():
            out[name] = _to_microbatches(out[name], axis)
    return {'x': out['x'], 'norm_mix_g': out['norm_mix_g'], 'w_in': out['w_in'], 'a_re': out['a_re'], 'a_im': out['a_im'], 'log_step': out['log_step'], 'b_re': out['b_re'], 'b_im': out['b_im'], 'c_re': out['c_re'], 'c_im': out['c_im'], 'd_skip': out['d_skip'], 'w_glu': out['w_glu'], 'sink': out['sink'], 'norm_attn_g': out['norm_attn_g'], 'norm_ssm_g': out['norm_ssm_g'], 'w_out': out['w_out'], 'norm_ffn_g': out['norm_ffn_g'], 'w_up': out['w_up'], 'conv_w': out['conv_w'], 'conv_b': out['conv_b'], 'w_down': out['w_down'], 'norm_final_g': out['norm_final_g'], 'loss_target': out['loss_target'], 'm_norm_mix_g': out['m_norm_mix_g'], 'm_w_in': out['m_w_in'], 'm_a_re': out['m_a_re'], 'm_a_im': out['m_a_im'], 'm_log_step': out['m_log_step'], 'm_b_re': out['m_b_re'], 'm_b_im': out['m_b_im'], 'm_c_re': out['m_c_re'], 'm_c_im': out['m_c_im'], 'm_d_skip': out['m_d_skip'], 'm_w_glu': out['m_w_glu'], 'm_sink': out['m_sink'], 'm_norm_attn_g': out['m_norm_attn_g'], 'm_norm_ssm_g': out['m_norm_ssm_g'], 'm_w_out': out['m_w_out'], 'm_norm_ffn_g': out['m_norm_ffn_g'], 'm_w_up': out['m_w_up'], 'm_conv_w': out['m_conv_w'], 'm_conv_b': out['m_conv_b'], 'm_w_down': out['m_w_down'], 'm_norm_final_g': out['m_norm_final_g'], 'v_norm_mix_g': out['v_norm_mix_g'], 'v_w_in': out['v_w_in'], 'v_a_re': out['v_a_re'], 'v_a_im': out['v_a_im'], 'v_log_step': out['v_log_step'], 'v_b_re': out['v_b_re'], 'v_b_im': out['v_b_im'], 'v_c_re': out['v_c_re'], 'v_c_im': out['v_c_im'], 'v_d_skip': out['v_d_skip'], 'v_w_glu': out['v_w_glu'], 'v_sink': out['v_sink'], 'v_norm_attn_g': out['v_norm_attn_g'], 'v_norm_ssm_g': out['v_norm_ssm_g'], 'v_w_out': out['v_w_out'], 'v_norm_ffn_g': out['v_norm_ffn_g'], 'v_w_up': out['v_w_up'], 'v_conv_w': out['v_conv_w'], 'v_conv_b': out['v_conv_b'], 'v_w_down': out['v_w_down'], 'v_norm_final_g': out['v_norm_final_g']}


def _loss(weights, diff, rest, loss_target):
    with _jax.named_scope("forward"):
        args = {**rest, TWIN_DIFF_INPUT: diff, **{k: w.astype(_WEIGHT_DTYPES[k]) for k, w in weights.items()}}
        y = _forward(args)
    with _jax.named_scope("loss_head"):
        err = _jnp.square(y.astype(_jnp.float32) - loss_target)
        return 0.5 * _jnp.sum(_jnp.mean(err, axis=-1)) if err.ndim else 0.5 * err


def _adamw(w, g, m, v):
    m = ADAM_B1 * m + (1.0 - ADAM_B1) * g
    v = ADAM_B2 * v + (1.0 - ADAM_B2) * _jnp.square(g)
    m_hat = m / (1.0 - ADAM_B1 ** ADAM_STEP)
    v_hat = v / (1.0 - ADAM_B2 ** ADAM_STEP)
    delta = -ADAM_LR * (m_hat / (_jnp.sqrt(v_hat) + ADAM_EPS) + ADAM_WD * w)
    return delta, m, v


def reference(x, norm_mix_g, w_in, a_re, a_im, log_step, b_re, b_im, c_re, c_im, d_skip, w_glu, sink, norm_attn_g, norm_ssm_g, w_out, norm_ffn_g, w_up, conv_w, conv_b, w_down, norm_final_g, loss_target, m_norm_mix_g, m_w_in, m_a_re, m_a_im, m_log_step, m_b_re, m_b_im, m_c_re, m_c_im, m_d_skip, m_w_glu, m_sink, m_norm_attn_g, m_norm_ssm_g, m_w_out, m_norm_ffn_g, m_w_up, m_conv_w, m_conv_b, m_w_down, m_norm_final_g, v_norm_mix_g, v_w_in, v_a_re, v_a_im, v_log_step, v_b_re, v_b_im, v_c_re, v_c_im, v_d_skip, v_w_glu, v_sink, v_norm_attn_g, v_norm_ssm_g, v_w_out, v_norm_ffn_g, v_w_up, v_conv_w, v_conv_b, v_w_down, v_norm_final_g):
    given = dict(x=x, norm_mix_g=norm_mix_g, w_in=w_in, a_re=a_re, a_im=a_im, log_step=log_step, b_re=b_re, b_im=b_im, c_re=c_re, c_im=c_im, d_skip=d_skip, w_glu=w_glu, sink=sink, norm_attn_g=norm_attn_g, norm_ssm_g=norm_ssm_g, w_out=w_out, norm_ffn_g=norm_ffn_g, w_up=w_up, conv_w=conv_w, conv_b=conv_b, w_down=w_down, norm_final_g=norm_final_g, loss_target=loss_target, m_norm_mix_g=m_norm_mix_g, m_w_in=m_w_in, m_a_re=m_a_re, m_a_im=m_a_im, m_log_step=m_log_step, m_b_re=m_b_re, m_b_im=m_b_im, m_c_re=m_c_re, m_c_im=m_c_im, m_d_skip=m_d_skip, m_w_glu=m_w_glu, m_sink=m_sink, m_norm_attn_g=m_norm_attn_g, m_norm_ssm_g=m_norm_ssm_g, m_w_out=m_w_out, m_norm_ffn_g=m_norm_ffn_g, m_w_up=m_w_up, m_conv_w=m_conv_w, m_conv_b=m_conv_b, m_w_down=m_w_down, m_norm_final_g=m_norm_final_g, v_norm_mix_g=v_norm_mix_g, v_w_in=v_w_in, v_a_re=v_a_re, v_a_im=v_a_im, v_log_step=v_log_step, v_b_re=v_b_re, v_b_im=v_b_im, v_c_re=v_c_re, v_c_im=v_c_im, v_d_skip=v_d_skip, v_w_glu=v_w_glu, v_sink=v_sink, v_norm_attn_g=v_norm_attn_g, v_norm_ssm_g=v_norm_ssm_g, v_w_out=v_w_out, v_norm_ffn_g=v_norm_ffn_g, v_w_up=v_w_up, v_conv_w=v_conv_w, v_conv_b=v_conv_b, v_w_down=v_w_down, v_norm_final_g=v_norm_final_g)
    weights = {n: given[n] for n in TWIN_WEIGHTS}
    shared = {n: given[n] for n in SHARED_INPUTS}
    per_example = {n: given[n] for n in ['x']}
    grad_fn = _jax.value_and_grad(_loss, argnums=(0, 1))

    def one_microbatch(ex, loss_target):
        ex = dict(ex)
        diff = ex.pop(TWIN_DIFF_INPUT)
        return grad_fn(weights, diff, {**shared, **ex}, loss_target)

    if N_MICROBATCH == 1:
        loss, (grad_w, grad_x) = one_microbatch(per_example, given["loss_target"])
    else:
        def body(carry, xs):
            loss_sum, grad_sum = carry
            l_k, (gw_k, gx_k) = one_microbatch(xs[0], xs[1])
            with _jax.named_scope("update"):
                return (loss_sum + l_k, _jax.tree.map(_jnp.add, grad_sum, gw_k)), gx_k

        init = (_jnp.zeros((), _jnp.float32), _jax.tree.map(_jnp.zeros_like, weights))
        (loss, grad_w), grad_x = _jax.lax.scan(body, init, (per_example, given["loss_target"]))
    with _jax.named_scope("update"):
        delta_w, new_m, new_v = {}, {}, {}
        for n in TWIN_WEIGHTS:
            delta_w[n], new_m[n], new_v[n] = _adamw(weights[n], grad_w[n], given["m_" + n], given["v_" + n])
    return (loss, grad_x, *[grad_w[n] for n in TWIN_WEIGHTS], *[delta_w[n] for n in TWIN_WEIGHTS],
            *[new_m[n] for n in TWIN_WEIGHTS], *[new_v[n] for n in TWIN_WEIGHTS])
```

```python
import functools
import math

import jax
import jax.numpy as jnp
from jax import lax
from jax.experimental import pallas as pl
from jax.experimental.pallas import tpu as pltpu

F32 = jnp.float32
BF16 = jnp.bfloat16

D_MODEL = 1024
N_Q_HEADS = 8
N_KV_HEADS = 2
HEAD_DIM = 64
ATTN_WIDTH = 512
KV_WIDTH = 128
QKV_WIDTH = ATTN_WIDTH + 2 * KV_WIDTH
WINDOW = 128
BLOCK = 128
ROPE_DIM = 16
ROPE_THETA = 500000.0
SSM_WIDTH = 512
SSM_GROUP = 16
N_SSM_GROUPS = 32
SSM_STATE = 64
IN_WIDTH = 1280
D_FF = 2816
EPS = 1e-6
ADAM_LR = 0.001
ADAM_B1 = 0.9
ADAM_B2 = 0.999
ADAM_EPS = 1e-08
ADAM_WD = 0.01
ADAM_STEP = 10

VMEM_BYTES_V7X = 64 * 1024 * 1024
SUBLANES = 8
LANES = 128
SSM_CB = 4
SSM_CH = 128
SSM_ST = 512
N_SEG = SUBLANES

NN = (((1,), (0,)), ((), ()))
NT = (((1,), (1,)), ((), ()))
TN = (((0,), (0,)), ((), ()))


def _params(sem=None, vmem_mb=48):
    return pltpu.CompilerParams(dimension_semantics=sem, vmem_limit_bytes=vmem_mb * 1024 * 1024)


def _dg(a, b, dims):
    return lax.dot_general(a, b, dims, preferred_element_type=F32)


def _split(a):
    hi = a.astype(BF16)
    lo = (a - hi.astype(F32)).astype(BF16)
    return hi, lo


def _dot3w(a, b_hi, b_lo, dims):
    a_hi, a_lo = _split(a)
    return _dg(a_hi, b_hi, dims) + _dg(a_hi, b_lo, dims) + _dg(a_lo, b_hi, dims)


def _dot3(a, b, dims):
    b_hi, b_lo = _split(b)
    return _dot3w(a, b_hi, b_lo, dims)


def _sigmoid(x):
    return 1.0 / (1.0 + jnp.exp(-x))


_SQRT_HALF = 0.7071067811865476
_INV_SQRT_2PI = 0.3989422804014327


def _gelu(x):
    return 0.5 * x * (1.0 + lax.erf(x * _SQRT_HALF))


def _gelu_grad(x):
    return 0.5 * (1.0 + lax.erf(x * _SQRT_HALF)) + x * (_INV_SQRT_2PI * jnp.exp(-0.5 * x * x))


def _mm_nn(a, b, tm, tn, out_dtype, name, res=None):
    m, k = a.shape
    n = b.shape[1]

    def body(*refs):
        if res is None:
            a_ref, b_ref, o_ref = refs
            o_ref[...] = _dg(a_ref[...], b_ref[...], NN).astype(out_dtype)
        else:
            a_ref, b_ref, r_ref, o_ref = refs
            o_ref[...] = (r_ref[...] + _dg(a_ref[...], b_ref[...], NN)).astype(out_dtype)

    in_specs = [pl.BlockSpec((tm, k), lambda i, j: (i, 0)), pl.BlockSpec((k, tn), lambda i, j: (0, j))]
    args = [a, b]
    if res is not None:
        in_specs.append(pl.BlockSpec((tm, tn), lambda i, j: (i, j)))
        args.append(res)
    return pl.pallas_call(
        body, grid=(m // tm, n // tn), in_specs=in_specs,
        out_specs=pl.BlockSpec((tm, tn), lambda i, j: (i, j)),
        out_shape=jax.ShapeDtypeStruct((m, n), out_dtype), name=name,
        compiler_params=_params(("parallel", "parallel")),
    )(*args)


def _mm_nt(a, b, tm, tn, out_dtype, name):
    m, k = a.shape
    n = b.shape[0]

    def body(a_ref, b_ref, o_ref):
        o_ref[...] = _dg(a_ref[...], b_ref[...], NT).astype(out_dtype)

    return pl.pallas_call(
        body, grid=(m // tm, n // tn),
        in_specs=[pl.BlockSpec((tm, k), lambda i, j: (i, 0)), pl.BlockSpec((tn, k), lambda i, j: (j, 0))],
        out_specs=pl.BlockSpec((tm, tn), lambda i, j: (i, j)),
        out_shape=jax.ShapeDtypeStruct((m, n), out_dtype), name=name,
        compiler_params=_params(("parallel", "parallel")),
    )(a, b)


def _mm_tn(a, b, tm, tn, name):
    k, m = a.shape
    n = b.shape[1]

    def body(a_ref, b_ref, o_ref):
        o_ref[...] = _dg(a_ref[...], b_ref[...], TN)

    return pl.pallas_call(
        body, grid=(m // tm, n // tn),
        in_specs=[pl.BlockSpec((k, tm), lambda i, j: (0, i)), pl.BlockSpec((k, tn), lambda i, j: (0, j))],
        out_specs=pl.BlockSpec((tm, tn), lambda i, j: (i, j)),
        out_shape=jax.ShapeDtypeStruct((m, n), F32), name=name,
        compiler_params=_params(("parallel", "parallel")),
    )(a, b)


TM_EW = 256


def _rms_fwd(x, g, name):
    l, d = x.shape

    def body(x_ref, g_ref, h_ref):
        xv = x_ref[...]
        r = lax.rsqrt(jnp.mean(xv * xv, axis=-1, keepdims=True) + EPS)
        h_ref[...] = (xv * r * g_ref[...]).astype(BF16)

    return pl.pallas_call(
        body, grid=(l // TM_EW,),
        in_specs=[pl.BlockSpec((TM_EW, d), lambda i: (i, 0)), pl.BlockSpec((1, d), lambda i: (0, 0))],
        out_specs=pl.BlockSpec((TM_EW, d), lambda i: (i, 0)),
        out_shape=jax.ShapeDtypeStruct((l, d), BF16), name=name,
        compiler_params=_params(("parallel",)),
    )(x, g)


def _rms_bwd_vals(xv, gv, dy):
    r = lax.rsqrt(jnp.mean(xv * xv, axis=-1, keepdims=True) + EPS)
    xh = xv * r
    dxh = dy * gv
    dx = r * (dxh - xh * jnp.mean(dxh * xh, axis=-1, keepdims=True))
    return dx, dy * xh


def _rms_bwd(x, g, dy, res, name):
    l, d = x.shape

    def body(x_ref, g_ref, dy_ref, res_ref, dx_ref, dxb_ref, dg_ref):
        dx, dgr = _rms_bwd_vals(x_ref[...], g_ref[...], dy_ref[...])
        dx = dx + res_ref[...]
        dx_ref[...] = dx
        dxb_ref[...] = dx.astype(BF16)

        @pl.when(pl.program_id(0) == 0)
        def _():
            dg_ref[...] = jnp.zeros_like(dg_ref)

        dg_ref[...] += jnp.sum(dgr, axis=0, keepdims=True)

    row = pl.BlockSpec((TM_EW, d), lambda i: (i, 0))
    vec = pl.BlockSpec((1, d), lambda i: (0, 0))
    return pl.pallas_call(
        body, grid=(l // TM_EW,), in_specs=[row, vec, row, row], out_specs=[row, row, vec],
        out_shape=[jax.ShapeDtypeStruct((l, d), F32), jax.ShapeDtypeStruct((l, d), BF16),
                   jax.ShapeDtypeStruct((1, d), F32)],
        name=name, compiler_params=_params(("arbitrary",)),
    )(x, g, dy, res)


def _final_loss(x2, g, target):
    l, d = x2.shape

    def body(x_ref, g_ref, t_ref, loss_ref, dx_ref, dxb_ref, dg_ref):
        xv = x_ref[...]
        gv = g_ref[...]
        r = lax.rsqrt(jnp.mean(xv * xv, axis=-1, keepdims=True) + EPS)
        xh = xv * r
        e = xh * gv - t_ref[...]
        part = jnp.sum(jnp.sum(e * e, axis=1, keepdims=True), axis=0, keepdims=True) * (0.5 / d)
        dy = e * (1.0 / d)
        dxh = dy * gv
        dx = r * (dxh - xh * jnp.mean(dxh * xh, axis=-1, keepdims=True))
        dx_ref[...] = dx
        dxb_ref[...] = dx.astype(BF16)

        @pl.when(pl.program_id(0) == 0)
        def _():
            dg_ref[...] = jnp.zeros_like(dg_ref)
            loss_ref[...] = jnp.zeros_like(loss_ref)

        dg_ref[...] += jnp.sum(dy * xh, axis=0, keepdims=True)
        loss_ref[...] += part

    row = pl.BlockSpec((TM_EW, d), lambda i: (i, 0))
    vec = pl.BlockSpec((1, d), lambda i: (0, 0))
    one = pl.BlockSpec((1, 1), lambda i: (0, 0))
    return pl.pallas_call(
        body, grid=(l // TM_EW,), in_specs=[row, vec, row], out_specs=[one, row, row, vec],
        out_shape=[jax.ShapeDtypeStruct((1, 1), F32), jax.ShapeDtypeStruct((l, d), F32),
                   jax.ShapeDtypeStruct((l, d), BF16), jax.ShapeDtypeStruct((1, d), F32)],
        name="final_loss", compiler_params=_params(("arbitrary",)),
    )(x2, g, target)


def _mix_fwd(attn, ys, g_attn, g_ssm):
    l, w = attn.shape

    def body(a_ref, y_ref, ga_ref, gs_ref, o_ref):
        for src, gr, off in ((a_ref, ga_ref, 0), (y_ref, gs_ref, w)):
            xv = src[...]
            r = lax.rsqrt(jnp.mean(xv * xv, axis=-1, keepdims=True) + EPS)
            o_ref[:, off:off + w] = (xv * r * gr[...]).astype(BF16)

    row = pl.BlockSpec((TM_EW, w), lambda i: (i, 0))
    vec = pl.BlockSpec((1, w), lambda i: (0, 0))
    return pl.pallas_call(
        body, grid=(l // TM_EW,), in_specs=[row, row, vec, vec],
        out_specs=pl.BlockSpec((TM_EW, 2 * w), lambda i: (i, 0)),
        out_shape=jax.ShapeDtypeStruct((l, 2 * w), BF16), name="mix_fwd",
        compiler_params=_params(("parallel",)),
    )(attn, ys, g_attn, g_ssm)


def _mix_bwd(attn, ys, g_attn, g_ssm, dmixed):
    l, w = attn.shape

    def body(a_ref, y_ref, ga_ref, gs_ref, dm_ref, da_ref, dy_ref, dga_ref, dgs_ref):
        @pl.when(pl.program_id(0) == 0)
        def _():
            dga_ref[...] = jnp.zeros_like(dga_ref)
            dgs_ref[...] = jnp.zeros_like(dgs_ref)

        for src, gr, off, dst, dgr in ((a_ref, ga_ref, 0, da_ref, dga_ref), (y_ref, gs_ref, w, dy_ref, dgs_ref)):
            dx, dg_rows = _rms_bwd_vals(src[...], gr[...], dm_ref[:, off:off + w])
            dst[...] = dx
            dgr[...] += jnp.sum(dg_rows, axis=0, keepdims=True)

    row = pl.BlockSpec((TM_EW, w), lambda i: (i, 0))
    vec = pl.BlockSpec((1, w), lambda i: (0, 0))
    return pl.pallas_call(
        body, grid=(l // TM_EW,),
        in_specs=[row, row, vec, vec, pl.BlockSpec((TM_EW, 2 * w), lambda i: (i, 0))],
        out_specs=[row, row, vec, vec],
        out_shape=[jax.ShapeDtypeStruct((l, w), F32), jax.ShapeDtypeStruct((l, w), F32),
                   jax.ShapeDtypeStruct((1, w), F32), jax.ShapeDtypeStruct((1, w), F32)],
        name="mix_bwd", compiler_params=_params(("arbitrary",)),
    )(attn, ys, g_attn, g_ssm, dmixed)


def _rope_tables(l):
    half = ROPE_DIM // 2
    inv_freq = jnp.power(ROPE_THETA, -jnp.arange(half, dtype=F32) / half)
    ang = jnp.arange(l, dtype=F32)[:, None] * inv_freq[None, :]
    cos, sin = jnp.cos(ang), jnp.sin(ang)
    ones = jnp.ones((l, HEAD_DIM - ROPE_DIM), F32)
    zeros = jnp.zeros((l, HEAD_DIM - ROPE_DIM), F32)
    zh = jnp.zeros((l, half), F32)
    c = jnp.concatenate([cos, cos, ones], axis=1)
    s_lo = jnp.concatenate([-sin, zh, zeros], axis=1)
    s_hi = jnp.concatenate([zh, sin, zeros], axis=1)
    return tuple(jnp.tile(t, (1, LANES // HEAD_DIM)) for t in (c, s_lo, s_hi))


def _rope_fwd(proj, tabs):
    l = proj.shape[0]
    nq = ATTN_WIDTH // LANES

    def body(p_ref, c_ref, lo_ref, hi_ref, o_ref):
        c, lo, hi = c_ref[...], lo_ref[...], hi_ref[...]
        for blk in range(nq + 1):
            t = p_ref[:, blk * LANES:(blk + 1) * LANES]
            rot = t * c + pltpu.roll(t, LANES - 8, 1) * lo + pltpu.roll(t, 8, 1) * hi
            o_ref[:, blk * LANES:(blk + 1) * LANES] = rot.astype(BF16)
        o_ref[:, (nq + 1) * LANES:] = p_ref[:, (nq + 1) * LANES:].astype(BF16)

    tab = pl.BlockSpec((TM_EW, LANES), lambda i: (i, 0))
    return pl.pallas_call(
        body, grid=(l // TM_EW,),
        in_specs=[pl.BlockSpec((TM_EW, QKV_WIDTH), lambda i: (i, 0)), tab, tab, tab],
        out_specs=pl.BlockSpec((TM_EW, QKV_WIDTH), lambda i: (i, 0)),
        out_shape=jax.ShapeDtypeStruct((l, QKV_WIDTH), BF16), name="rope_fwd",
        compiler_params=_params(("parallel",)),
    )(proj, *tabs)


def _rope_bwd(dqkv, du, tabs):
    l = dqkv.shape[0]
    nq = ATTN_WIDTH // LANES

    def body(d_ref, du_ref, c_ref, lo_ref, hi_ref, o_ref):
        c, lo, hi = c_ref[...], lo_ref[...], hi_ref[...]
        for blk in range(nq + 1):
            t = d_ref[:, blk * LANES:(blk + 1) * LANES]
            g = t * c + pltpu.roll(t * lo, 8, 1) + pltpu.roll(t * hi, LANES - 8, 1)
            o_ref[:, blk * LANES:(blk + 1) * LANES] = g.astype(BF16)
        o_ref[:, (nq + 1) * LANES:QKV_WIDTH] = d_ref[:, (nq + 1) * LANES:].astype(BF16)
        o_ref[:, QKV_WIDTH:] = du_ref[...].astype(BF16)

    tab = pl.BlockSpec((TM_EW, LANES), lambda i: (i, 0))
    return pl.pallas_call(
        body, grid=(l // TM_EW,),
        in_specs=[pl.BlockSpec((TM_EW, QKV_WIDTH), lambda i: (i, 0)),
                  pl.BlockSpec((TM_EW, SSM_WIDTH), lambda i: (i, 0)), tab, tab, tab],
        out_specs=pl.BlockSpec((TM_EW, IN_WIDTH), lambda i: (i, 0)),
        out_shape=jax.ShapeDtypeStruct((l, IN_WIDTH), BF16), name="rope_bwd",
        compiler_params=_params(("parallel",)),
    )(dqkv, du, *tabs)


_Q_COLS = ATTN_WIDTH // LANES
_SCALE = HEAD_DIM ** -0.5
_NEG = -1e30


def _window_specs(nb, width, col):
    return [
        pl.BlockSpec((BLOCK, width), lambda n: (jnp.maximum(n - 1, 0), col)),
        pl.BlockSpec((BLOCK, width), lambda n: (n, col)),
        pl.BlockSpec((BLOCK, width), lambda n: (jnp.minimum(n + 1, nb - 1), col)),
    ]


def _attn_fwd(qkv, sink):
    l = qkv.shape[0]
    nb = l // BLOCK
    grp = N_Q_HEADS // N_KV_HEADS

    def body(sink_ref, q_ref, k0, k1, k2, v0, v1, v2, o_ref, lse_ref):
        n = pl.program_id(0)
        q = q_ref[...]
        kw = jnp.concatenate([k0[...], k1[...], k2[...]], axis=0)
        vw = jnp.concatenate([v0[...], v1[...], v2[...]], axis=0)
        row = lax.broadcasted_iota(jnp.int32, (BLOCK, 3 * BLOCK), 0)
        col = lax.broadcasted_iota(jnp.int32, (BLOCK, 3 * BLOCK), 1)
        rel = col - BLOCK - row
        valid = (jnp.abs(rel) <= WINDOW)
        valid &= jnp.logical_not((n == 0) & (col < BLOCK))
        valid &= jnp.logical_not((n == nb - 1) & (col >= 2 * BLOCK))
        for h in range(N_Q_HEADS):
            hk = h // grp
            qh = q[:, h * HEAD_DIM:(h + 1) * HEAD_DIM]
            kh = kw[:, hk * HEAD_DIM:(hk + 1) * HEAD_DIM]
            vh = vw[:, hk * HEAD_DIM:(hk + 1) * HEAD_DIM]
            s = jnp.where(valid, _dg(qh, kh, NT) * _SCALE, _NEG)
            sk = sink_ref[0, h]
            m = jnp.maximum(jnp.max(s, axis=1, keepdims=True), sk)
            p = jnp.exp(s - m)
            denom = jnp.sum(p, axis=1, keepdims=True) + jnp.exp(sk - m)
            o = _dg((p / denom).astype(BF16), vh, NN)
            o_ref[:, h * HEAD_DIM:(h + 1) * HEAD_DIM] = o
            lse_ref[:, h:h + 1] = m + jnp.log(denom)

    return pl.pallas_call(
        body, grid=(nb,),
        in_specs=[pl.BlockSpec(memory_space=pltpu.SMEM),
                  pl.BlockSpec((BLOCK, ATTN_WIDTH), lambda n: (n, 0))]
        + _window_specs(nb, KV_WIDTH, _Q_COLS) + _window_specs(nb, KV_WIDTH, _Q_COLS + 1),
        out_specs=[pl.BlockSpec((BLOCK, ATTN_WIDTH), lambda n: (n, 0)),
                   pl.BlockSpec((BLOCK, N_Q_HEADS), lambda n: (n, 0))],
        out_shape=[jax.ShapeDtypeStruct((l, ATTN_WIDTH), F32), jax.ShapeDtypeStruct((l, N_Q_HEADS), F32)],
        name="attn_fwd", compiler_params=_params(("parallel",)),
    )(sink, qkv, qkv, qkv, qkv, qkv, qkv, qkv)


def _attn_bwd(qkv, attn, dattn, lse, sink):
    l = qkv.shape[0]
    nb = l // BLOCK
    grp = N_Q_HEADS // N_KV_HEADS

    def body(sink_ref, q0, q1, q2, k0, k1, k2, v0, v1, v2, o0, o1, o2, d0, d1, d2,
             l0, l1, l2, dqkv_ref, dsink_ref):
        n = pl.program_id(0)
        first, last = n == 0, n == nb - 1

        @pl.when(first)
        def _():
            dsink_ref[...] = jnp.zeros_like(dsink_ref)

        cat = lambda a, b, c: jnp.concatenate([a[...], b[...], c[...]], axis=0)
        qw, kw, vw = cat(q0, q1, q2), cat(k0, k1, k2), cat(v0, v1, v2)
        dow = cat(d0, d1, d2)
        prodw = cat(o0, o1, o2) * dow
        lsew = cat(l0, l1, l2)
        dob = dow.astype(BF16)
        q, k, v, do, lse_n = q1[...], k1[...], v1[...], dob[BLOCK:2 * BLOCK], l1[...]

        row = lax.broadcasted_iota(jnp.int32, (BLOCK, 3 * BLOCK), 0)
        col = lax.broadcasted_iota(jnp.int32, (BLOCK, 3 * BLOCK), 1)
        valid_q = jnp.abs(col - BLOCK - row) <= WINDOW
        valid_q &= jnp.logical_not(first & (col < BLOCK))
        valid_q &= jnp.logical_not(last & (col >= 2 * BLOCK))
        rowk = lax.broadcasted_iota(jnp.int32, (3 * BLOCK, BLOCK), 0)
        colk = lax.broadcasted_iota(jnp.int32, (3 * BLOCK, BLOCK), 1)
        valid_k = jnp.abs(colk + BLOCK - rowk) <= WINDOW
        valid_k &= jnp.logical_not(first & (rowk < BLOCK))
        valid_k &= jnp.logical_not(last & (rowk >= 2 * BLOCK))

        dsink_parts = []
        for hk in range(N_KV_HEADS):
            ksl = slice(hk * HEAD_DIM, (hk + 1) * HEAD_DIM)
            dk = jnp.zeros((BLOCK, HEAD_DIM), F32)
            dv = jnp.zeros((BLOCK, HEAD_DIM), F32)
            for g in range(grp):
                h = hk * grp + g
                hsl = slice(h * HEAD_DIM, (h + 1) * HEAD_DIM)
                deltaw = jnp.sum(prodw[:, hsl], axis=1, keepdims=True)
                delta = deltaw[BLOCK:2 * BLOCK]
                s = jnp.where(valid_q, _dg(q[:, hsl], kw[:, ksl], NT) * _SCALE, _NEG)
                p = jnp.exp(s - lse_n[:, h:h + 1])
                dp = _dg(do[:, hsl], vw[:, ksl], NT)
                ds = (p * (dp - delta) * _SCALE).astype(BF16)
                dqkv_ref[:, hsl] = _dg(ds, kw[:, ksl], NN)
                dsink_parts.append(jnp.sum(jnp.exp(sink_ref[0, h] - lse_n[:, h:h + 1]) * delta,
                                           axis=0, keepdims=True))
                s2 = jnp.where(valid_k, _dg(qw[:, hsl], k[:, ksl], NT) * _SCALE, _NEG)
                p2 = jnp.where(valid_k, jnp.exp(s2 - lsew[:, h:h + 1]), 0.0)
                dv += _dg(p2.astype(BF16), dob[:, hsl], TN)
                dp2 = _dg(dob[:, hsl], v[:, ksl], NT)
                ds2 = (p2 * (dp2 - deltaw) * _SCALE).astype(BF16)
                dk += _dg(ds2, qw[:, hsl], TN)
            dqkv_ref[:, ATTN_WIDTH + hk * HEAD_DIM:ATTN_WIDTH + (hk + 1) * HEAD_DIM] = dk
            dqkv_ref[:, ATTN_WIDTH + KV_WIDTH + hk * HEAD_DIM:ATTN_WIDTH + KV_WIDTH + (hk + 1) * HEAD_DIM] = dv
        dsink_ref[...] -= jnp.concatenate(dsink_parts, axis=1)

    return pl.pallas_call(
        body, grid=(nb,),
        in_specs=[pl.BlockSpec(memory_space=pltpu.SMEM)]
        + _window_specs(nb, ATTN_WIDTH, 0)
        + _window_specs(nb, KV_WIDTH, _Q_COLS) + _window_specs(nb, KV_WIDTH, _Q_COLS + 1)
        + _window_specs(nb, ATTN_WIDTH, 0) + _window_specs(nb, ATTN_WIDTH, 0)
        + _window_specs(nb, N_Q_HEADS, 0),
        out_specs=[pl.BlockSpec((BLOCK, QKV_WIDTH), lambda n: (n, 0)),
                   pl.BlockSpec((1, N_Q_HEADS), lambda n: (0, 0))],
        out_shape=[jax.ShapeDtypeStruct((l, QKV_WIDTH), F32), jax.ShapeDtypeStruct((1, N_Q_HEADS), F32)],
        name="attn_bwd", compiler_params=_params(("arbitrary",)),
    )(sink, qkv, qkv, qkv, qkv, qkv, qkv, qkv, qkv, qkv, attn, attn, attn,
      dattn, dattn, dattn, lse, lse, lse)


def _ssm_disc(a_re, a_im, log_step, b_re, b_im):
    step = jnp.exp(log_step)[..., None]
    mag = jnp.exp(a_re * step)
    lb_re, lb_im = mag * jnp.cos(a_im * step), mag * jnp.sin(a_im * step)
    nr, ni = lb_re - 1.0, lb_im
    den = a_re * a_re + a_im * a_im
    f_re = ((nr * a_re + ni * a_im) / den)[..., None]
    f_im = ((ni * a_re - nr * a_im) / den)[..., None]
    return lb_re, lb_im, f_re * b_re - f_im * b_im, f_re * b_im + f_im * b_re


def _ssm_pack(lb_re, lb_im, bb_re, bb_im, c_re, c_im):
    eye = jnp.eye(SSM_CH // SSM_GROUP, dtype=F32)
    ng = SSM_CH // SSM_GROUP

    def diag_b(bb):
        t = bb.reshape(2, SSM_CB, ng, SSM_STATE, SSM_GROUP)
        return jnp.einsum('dkgpc,gh->dkgchp', t, eye).reshape(2, SSM_CB, SSM_CH, SSM_ST)

    def diag_c(cc):
        t = cc.reshape(2, SSM_CB, ng, SSM_GROUP, SSM_STATE)
        return jnp.einsum('dkgcp,gh->dkhpgc', t, eye).reshape(2, SSM_CB, SSM_ST, SSM_CH)

    bcat = jnp.concatenate([diag_b(bb_re), diag_b(bb_im)], axis=-1)
    ccat = jnp.concatenate([diag_c(c_re), -diag_c(c_im)], axis=-2)
    lam_re = lb_re.reshape(2, SSM_CB, 1, SSM_ST)
    lam_im = lb_im.reshape(2, SSM_CB, 1, SSM_ST)
    return bcat, ccat, lam_re, lam_im


def _ssm_unpack(dbcat, dccat, dlam_re, dlam_im):
    ng = SSM_CH // SSM_GROUP
    eye = jnp.eye(ng, dtype=F32)

    def undiag_b(t):
        t = t.reshape(2, SSM_CB, ng, SSM_GROUP, ng, SSM_STATE)
        return jnp.einsum('dkgchp,gh->dkgpc', t, eye).reshape(2, N_SSM_GROUPS, SSM_STATE, SSM_GROUP)

    def undiag_c(t):
        t = t.reshape(2, SSM_CB, ng, SSM_STATE, ng, SSM_GROUP)
        return jnp.einsum('dkhpgc,gh->dkgcp', t, eye).reshape(2, N_SSM_GROUPS, SSM_GROUP, SSM_STATE)

    dbb_re, dbb_im = undiag_b(dbcat[..., :SSM_ST]), undiag_b(dbcat[..., SSM_ST:])
    dc_re, dc_im = undiag_c(dccat[:, :, :SSM_ST]), -undiag_c(dccat[:, :, SSM_ST:])
    shape = (2, N_SSM_GROUPS, SSM_STATE)
    return dlam_re.reshape(shape), dlam_im.reshape(shape), dbb_re, dbb_im, dc_re, dc_im


def _to_segments(t):
    l, w = t.shape
    return t.reshape(N_SEG, l // N_SEG, w).transpose(1, 0, 2).reshape(l, w)


def _from_segments(t):
    l, w = t.shape
    return t.reshape(l // N_SEG, N_SEG, w).transpose(1, 0, 2).reshape(l, w)


def _cfma(ar, ai, xr, xi, br, bi):
    return ar * xr - ai * xi + br, ar * xi + ai * xr + bi


def _scan_segments(xs_ref, ar, ai, rev, nj, prev_ref=None):
    shape = (N_SEG, SSM_ST)
    ar = jnp.broadcast_to(ar, shape)
    ai = jnp.broadcast_to(ai, shape)
    zero = jnp.zeros(shape, F32)
    re_cols, im_cols = pl.ds(0, SSM_ST), pl.ds(SSM_ST, SSM_ST)

    def rows_of(jj):
        j = jnp.where(rev, nj - 1 - jj, jj)
        return j, pl.ds(pl.multiple_of(j * N_SEG, N_SEG), N_SEG)

    def pass1(jj, carry):
        _, rows = rows_of(jj)
        return _cfma(ar, ai, carry[0], carry[1], xs_ref[rows, re_cols], xs_ref[rows, im_cols])

    end_r, end_i = lax.fori_loop(0, nj, pass1, (zero, zero))

    pr, pi = ar, ai
    for _ in range(int(math.log2(nj))):
        pr, pi = pr * pr - pi * pi, 2.0 * pr * pi
    seg = lax.broadcasted_iota(jnp.int32, shape, 0)

    def chain(shift, keep):
        ir, ii = zero, zero
        for _ in range(N_SEG - 1):
            tr, ti = _cfma(pr, pi, ir, ii, end_r, end_i)
            ir = jnp.where(keep, pltpu.roll(tr, shift, 0), 0.0)
            ii = jnp.where(keep, pltpu.roll(ti, shift, 0), 0.0)
        return ir, ii

    up_r, up_i = chain(1, seg >= 1)
    dn_r, dn_i = chain(N_SEG - 1, seg <= N_SEG - 2)
    init_r, init_i = jnp.where(rev, dn_r, up_r), jnp.where(rev, dn_i, up_i)

    def pass2(jj, carry):
        j, rows = rows_of(jj)
        nr, ni = _cfma(ar, ai, carry[0], carry[1], xs_ref[rows, re_cols], xs_ref[rows, im_cols])
        xs_ref[rows, re_cols] = nr
        xs_ref[rows, im_cols] = ni
        if prev_ref is None:
            return nr, ni
        jp = jnp.where(rev, j - 1, j + 1)
        inside = (jp >= 0) & (jp < nj)
        prow = pl.ds(pl.multiple_of(jnp.clip(jp, 0, nj - 1) * N_SEG, N_SEG), N_SEG)
        xr, xi = prev_ref[prow, re_cols], prev_ref[prow, im_cols]
        f = jnp.where(inside, 1.0, 0.0)
        return nr, ni, carry[2] + f * (nr * xr + ni * xi), carry[3] + f * (ni * xr - nr * xi)

    if prev_ref is None:
        lax.fori_loop(0, nj, pass2, (init_r, init_i))
        return init_r, init_i, None, None
    _, _, acc_r, acc_i = lax.fori_loop(0, nj, pass2, (init_r, init_i, zero, zero))
    return init_r, init_i, acc_r, acc_i


SSM_RC = 256


def _ssm_specs(l):
    act = pl.BlockSpec((l, SSM_CH), lambda k, d: (0, k))
    bmat = pl.BlockSpec((None, None, SSM_CH, 2 * SSM_ST), lambda k, d: (d, k, 0, 0))
    cmat = pl.BlockSpec((None, None, 2 * SSM_ST, SSM_CH), lambda k, d: (d, k, 0, 0))
    lam = pl.BlockSpec((None, None, 1, SSM_ST), lambda k, d: (d, k, 0, 0))
    return act, bmat, cmat, lam


def _ssm_fwd(u_seg, bcat, ccat, lam_re, lam_im):
    l = u_seg.shape[0]
    nj = l // N_SEG
    b_hi, b_lo = _split(bcat)
    c_hi, c_lo = _split(ccat)

    def body(u_ref, bh_ref, bl_ref, ch_ref, cl_ref, lr_ref, li_ref, y_ref, xs_ref):
        d = pl.program_id(1)

        def bu_chunk(i, _):
            rows = pl.ds(pl.multiple_of(i * SSM_RC, SSM_RC), SSM_RC)
            xs_ref[rows, :] = _dot3w(u_ref[rows, :], bh_ref[...], bl_ref[...], NN)
            return 0

        lax.fori_loop(0, l // SSM_RC, bu_chunk, 0)
        _scan_segments(xs_ref, lr_ref[...], li_ref[...], d == 1, nj)

        def y_chunk(i, _):
            rows = pl.ds(pl.multiple_of(i * SSM_RC, SSM_RC), SSM_RC)
            yv = _dot3w(xs_ref[rows, :], ch_ref[...], cl_ref[...], NN)

            @pl.when(d == 0)
            def _():
                y_ref[rows, :] = yv

            @pl.when(d == 1)
            def _():
                y_ref[rows, :] += yv

            return 0

        lax.fori_loop(0, l // SSM_RC, y_chunk, 0)

    act, bmat, cmat, lam = _ssm_specs(l)
    return pl.pallas_call(
        body, grid=(SSM_CB, 2), in_specs=[act, bmat, bmat, cmat, cmat, lam, lam], out_specs=act,
        out_shape=jax.ShapeDtypeStruct((l, SSM_WIDTH), F32),
        scratch_shapes=[pltpu.VMEM((l, 2 * SSM_ST), F32)],
        name="ssm_fwd", compiler_params=_params(("parallel", "arbitrary"), vmem_mb=56),
    )(u_seg, b_hi, b_lo, c_hi, c_lo, lam_re, lam_im)


def _ssm_bwd(u_seg, dy_seg, bcat, ccat, lam_re, lam_im):
    l = u_seg.shape[0]
    nj = l // N_SEG
    b_hi, b_lo = _split(bcat)
    c_hi, c_lo = _split(ccat)

    def body(u_ref, dy_ref, bh_ref, bl_ref, ch_ref, cl_ref, lr_ref, li_ref,
             du_ref, db_ref, dc_ref, dlr_ref, dli_ref, xs_ref, gs_ref):
        d = pl.program_id(1)
        rev = d == 1
        ar, ai = lr_ref[...], li_ref[...]

        def chunk1(i, _):
            rows = pl.ds(pl.multiple_of(i * SSM_RC, SSM_RC), SSM_RC)
            xs_ref[rows, :] = _dot3w(u_ref[rows, :], bh_ref[...], bl_ref[...], NN)
            gs_ref[rows, :] = _dot3w(dy_ref[rows, :], ch_ref[...], cl_ref[...], NT)
            return 0

        lax.fori_loop(0, l // SSM_RC, chunk1, 0)
        init_r, init_i, _, _ = _scan_segments(xs_ref, ar, ai, rev, nj)
        _, _, acc_r, acc_i = _scan_segments(gs_ref, ar, -ai, jnp.logical_not(rev), nj, prev_ref=xs_ref)
        jb = jnp.where(rev, nj - 1, 0)
        brow = pl.ds(pl.multiple_of(jb * N_SEG, N_SEG), N_SEG)
        gr, gi = gs_ref[brow, pl.ds(0, SSM_ST)], gs_ref[brow, pl.ds(SSM_ST, SSM_ST)]
        acc_r = acc_r + gr * init_r + gi * init_i
        acc_i = acc_i + gi * init_r - gr * init_i
        dlr_ref[...] = jnp.sum(acc_r, axis=0, keepdims=True)
        dli_ref[...] = jnp.sum(acc_i, axis=0, keepdims=True)

        db_ref[...] = jnp.zeros_like(db_ref)
        dc_ref[...] = jnp.zeros_like(dc_ref)

        def chunk2(i, _):
            rows = pl.ds(pl.multiple_of(i * SSM_RC, SSM_RC), SSM_RC)
            g = gs_ref[rows, :]
            dc_ref[...] += _dot3(xs_ref[rows, :], dy_ref[rows, :], TN)
            db_ref[...] += _dot3(u_ref[rows, :], g, TN)
            duv = _dot3w(g, bh_ref[...], bl_ref[...], NT)

            @pl.when(d == 0)
            def _():
                du_ref[rows, :] = duv

            @pl.when(d == 1)
            def _():
                du_ref[rows, :] += duv

            return 0

        lax.fori_loop(0, l // SSM_RC, chunk2, 0)

    act, bmat, cmat, lam = _ssm_specs(l)
    return pl.pallas_call(
        body, grid=(SSM_CB, 2), in_specs=[act, act, bmat, bmat, cmat, cmat, lam, lam],
        out_specs=[act, bmat, cmat, lam, lam],
        out_shape=[jax.ShapeDtypeStruct((l, SSM_WIDTH), F32),
                   jax.ShapeDtypeStruct(bcat.shape, F32), jax.ShapeDtypeStruct(ccat.shape, F32),
                   jax.ShapeDtypeStruct(lam_re.shape, F32), jax.ShapeDtypeStruct(lam_im.shape, F32)],
        scratch_shapes=[pltpu.VMEM((l, 2 * SSM_ST), F32), pltpu.VMEM((l, 2 * SSM_ST), F32)],
        name="ssm_bwd", compiler_params=_params(("parallel", "arbitrary"), vmem_mb=60),
    )(u_seg, dy_seg, b_hi, b_lo, c_hi, c_lo, lam_re, lam_im)


def _glu_fwd(y_ssm, u, d_skip, w_glu):
    l, w = u.shape

    def body(y_ref, u_ref, d_ref, w_ref, pre_ref, s_ref, ys_ref):
        pre = y_ref[...] + d_ref[...] * u_ref[...]
        z = _gelu(pre)
        s = _dg(z.astype(BF16), w_ref[...], NN)
        pre_ref[...] = pre
        s_ref[...] = s
        ys_ref[...] = z * _sigmoid(s)

    row = pl.BlockSpec((TM_EW, w), lambda i: (i, 0))
    out = jax.ShapeDtypeStruct((l, w), F32)
    return pl.pallas_call(
        body, grid=(l // TM_EW,),
        in_specs=[row, row, pl.BlockSpec((1, w), lambda i: (0, 0)), pl.BlockSpec((w, w), lambda i: (0, 0))],
        out_specs=[row, row, row], out_shape=[out, out, out], name="glu_fwd",
        compiler_params=_params(("parallel",)),
    )(y_ssm, u, d_skip, w_glu)


def _glu_bwd(pre, s, dys, u, d_skip, w_glu):
    l, w = u.shape

    def body(pre_ref, s_ref, dys_ref, u_ref, d_ref, w_ref, dpre_ref, z_ref, ds_ref, dd_ref):
        pre, dys = pre_ref[...], dys_ref[...]
        z = _gelu(pre)
        sig = _sigmoid(s_ref[...])
        ds = (dys * z * sig * (1.0 - sig)).astype(BF16)
        dz = dys * sig + _dg(ds, w_ref[...], NT)
        dpre = dz * _gelu_grad(pre)
        dpre_ref[...] = dpre
        z_ref[...] = z.astype(BF16)
        ds_ref[...] = ds

        @pl.when(pl.program_id(0) == 0)
        def _():
            dd_ref[...] = jnp.zeros_like(dd_ref)

        dd_ref[...] += jnp.sum(dpre * u_ref[...], axis=0, keepdims=True)

    row = pl.BlockSpec((TM_EW, w), lambda i: (i, 0))
    vec = pl.BlockSpec((1, w), lambda i: (0, 0))
    return pl.pallas_call(
        body, grid=(l // TM_EW,),
        in_specs=[row, row, row, row, vec, pl.BlockSpec((w, w), lambda i: (0, 0))],
        out_specs=[row, row, row, vec],
        out_shape=[jax.ShapeDtypeStruct((l, w), F32), jax.ShapeDtypeStruct((l, w), BF16),
                   jax.ShapeDtypeStruct((l, w), BF16), jax.ShapeDtypeStruct((1, w), F32)],
        name="glu_bwd", compiler_params=_params(("arbitrary",)),
    )(pre, s, dys, u, d_skip, w_glu)


TM_CV = 256
TC_CV = 256
HALO = SUBLANES


def _conv_specs(l, col0):
    per = TM_CV // HALO
    nh = l // HALO
    off = col0 // TC_CV
    return [
        pl.BlockSpec((HALO, TC_CV), lambda j, i: (jnp.maximum(i * per - 1, 0), j + off)),
        pl.BlockSpec((TM_CV, TC_CV), lambda j, i: (i, j + off)),
        pl.BlockSpec((HALO, TC_CV), lambda j, i: (jnp.minimum((i + 1) * per, nh - 1), j + off)),
    ]


def _ext(prev_ref, mid_ref, next_ref, first, last):
    p = jnp.where(first, 0.0, prev_ref[...])
    n = jnp.where(last, 0.0, next_ref[...])
    return jnp.concatenate([p, mid_ref[...], n], axis=0)


def _shift_dn(t):
    return pltpu.roll(t, 1, 0)


def _shift_up(t):
    return pltpu.roll(t, t.shape[0] - 1, 0)


def _conv3(e, w_ref, b_ref):
    return w_ref[0:1, :] * _shift_dn(e) + w_ref[1:2, :] * e + w_ref[2:3, :] * _shift_up(e) + b_ref[...]


def _convffn_fwd(up_pre, conv_w, conv_b):
    l = up_pre.shape[0]
    ni = l // TM_CV
    wspec = lambda off: pl.BlockSpec((3, TC_CV), lambda j, i: (0, j + off))
    bspec = lambda off: pl.BlockSpec((1, TC_CV), lambda j, i: (0, j + off))
    voff = D_FF // TC_CV

    def body(gp, gm, gn, vp, vm, vn, wg, bg, wv, bv, o_ref):
        i = pl.program_id(1)
        first, last = i == 0, i == ni - 1
        gate = _conv3(_ext(gp, gm, gn, first, last), wg, bg)[HALO:HALO + TM_CV]
        val = _conv3(_ext(vp, vm, vn, first, last), wv, bv)[HALO:HALO + TM_CV]
        o_ref[...] = (gate * _sigmoid(gate) * val).astype(BF16)

    return pl.pallas_call(
        body, grid=(D_FF // TC_CV, ni),
        in_specs=_conv_specs(l, 0) + _conv_specs(l, D_FF) + [wspec(0), bspec(0), wspec(voff), bspec(voff)],
        out_specs=pl.BlockSpec((TM_CV, TC_CV), lambda j, i: (i, j)),
        out_shape=jax.ShapeDtypeStruct((l, D_FF), BF16), name="convffn_fwd",
        compiler_params=_params(("parallel", "parallel")),
    )(up_pre, up_pre, up_pre, up_pre, up_pre, up_pre, conv_w, conv_b, conv_w, conv_b)


def _convffn_bwd(up_pre, dact, conv_w, conv_b):
    l = up_pre.shape[0]
    ni = l // TM_CV
    wspec = lambda off: pl.BlockSpec((3, TC_CV), lambda j, i: (0, j + off))
    bspec = lambda off: pl.BlockSpec((1, TC_CV), lambda j, i: (0, j + off))
    voff = D_FF // TC_CV

    def body(gp, gm, gn, vp, vm, vn, dp, dm, dn, wg, bg, wv, bv, dgate_ref, dval_ref, pg_ref, pv_ref):
        i = pl.program_id(1)
        first, last = i == 0, i == ni - 1
        ge, ve, de = _ext(gp, gm, gn, first, last), _ext(vp, vm, vn, first, last), _ext(dp, dm, dn, first, last)
        gate, val = _conv3(ge, wg, bg), _conv3(ve, wv, bv)
        sig = _sigmoid(gate)
        silu = gate * sig
        dgate = de * val * (sig + silu * (1.0 - sig))
        dval = de * silu
        mid = slice(HALO, HALO + TM_CV)
        rid = lax.broadcasted_iota(jnp.int32, (SUBLANES, TC_CV), 0)

        @pl.when(i == 0)
        def _():
            pg_ref[...] = jnp.zeros_like(pg_ref)
            pv_ref[...] = jnp.zeros_like(pv_ref)

        for dup, e, w_ref, out_ref, p_ref in ((dgate, ge, wg, dgate_ref, pg_ref), (dval, ve, wv, dval_ref, pv_ref)):
            dpre = w_ref[0:1, :] * _shift_up(dup) + w_ref[1:2, :] * dup + w_ref[2:3, :] * _shift_dn(dup)
            out_ref[...] = dpre[mid].astype(BF16)
            dm_ = dup[mid]
            sums = [jnp.sum(dm_ * _shift_dn(e)[mid], axis=0, keepdims=True),
                    jnp.sum(dm_ * e[mid], axis=0, keepdims=True),
                    jnp.sum(dm_ * _shift_up(e)[mid], axis=0, keepdims=True),
                    jnp.sum(dm_, axis=0, keepdims=True)]
            acc = jnp.zeros((SUBLANES, TC_CV), F32)
            for k, sk in enumerate(sums):
                acc = jnp.where(rid == k, sk, acc)
            p_ref[...] += acc

    tile = pl.BlockSpec((TM_CV, TC_CV), lambda j, i: (i, j))
    par = pl.BlockSpec((SUBLANES, TC_CV), lambda j, i: (0, j))
    dgate, dval, pg, pv = pl.pallas_call(
        body, grid=(D_FF // TC_CV, ni),
        in_specs=_conv_specs(l, 0) + _conv_specs(l, D_FF) + _conv_specs(l, 0)
        + [wspec(0), bspec(0), wspec(voff), bspec(voff)],
        out_specs=[tile, tile, par, par],
        out_shape=[jax.ShapeDtypeStruct((l, D_FF), BF16), jax.ShapeDtypeStruct((l, D_FF), BF16),
                   jax.ShapeDtypeStruct((SUBLANES, D_FF), F32), jax.ShapeDtypeStruct((SUBLANES, D_FF), F32)],
        name="convffn_bwd", compiler_params=_params(("parallel", "arbitrary")),
    )(up_pre, up_pre, up_pre, up_pre, up_pre, up_pre, dact, dact, dact, conv_w, conv_b, conv_w, conv_b)
    return jnp.concatenate([dgate, dval], axis=1), jnp.concatenate([pg, pv], axis=1)


def _local_step(x, target, wb, sp):
    l = x.shape[0]
    tabs = _rope_tables(l)
    disc = _ssm_disc(sp["a_re"], sp["a_im"], sp["log_step"], sp["b_re"], sp["b_im"])
    bcat, ccat, lam_re, lam_im = _ssm_pack(*disc, sp["c_re"], sp["c_im"])
    d_skip = sp["d_skip"].reshape(1, SSM_WIDTH)

    h = _rms_fwd(x, sp["norm_mix_g"], "rms_mix")
    proj = _mm_nn(h, wb["w_in"], 512, IN_WIDTH, F32, "mm_in")
    qkv = _rope_fwd(proj, tabs)
    attn, lse = _attn_fwd(qkv, sp["sink"])
    u = proj[:, QKV_WIDTH:]
    u_seg = _to_segments(u)
    y_ssm = _from_segments(_ssm_fwd(u_seg, bcat, ccat, lam_re, lam_im))
    pre, s_glu, ys = _glu_fwd(y_ssm, u, d_skip, wb["w_glu"])
    mixed = _mix_fwd(attn, ys, sp["norm_attn_g"], sp["norm_ssm_g"])
    x1 = _mm_nn(mixed, wb["w_out"], 512, 512, F32, "mm_out", res=x)
    h2 = _rms_fwd(x1, sp["norm_ffn_g"], "rms_ffn")
    up_pre = _mm_nn(h2, wb["w_up"], 512, 512, F32, "mm_up")
    act = _convffn_fwd(up_pre, sp["conv_w"], sp["conv_b"])
    x2 = _mm_nn(act, wb["w_down"], 512, 512, F32, "mm_down", res=x1)
    loss, dx2, dx2b, d_final_g = _final_loss(x2, sp["norm_final_g"].reshape(1, D_MODEL), target)

    g = {"norm_final_g": d_final_g.reshape(D_MODEL)}
    dact = _mm_nt(dx2b, wb["w_down"], 512, D_FF // 2, F32, "mm_down_dx")
    g["w_down"] = _mm_tn(act, dx2b, 256, 512, "mm_down_dw")
    dup_pre, conv_par = _convffn_bwd(up_pre, dact, sp["conv_w"], sp["conv_b"])
    g["conv_w"], g["conv_b"] = conv_par[0:3], conv_par[3:4]
    g["w_up"] = _mm_tn(h2, dup_pre, 512, 512, "mm_up_dw")
    dh2 = _mm_nt(dup_pre, wb["w_up"], 256, 512, F32, "mm_up_dx")
    dx1, dx1b, g["norm_ffn_g"] = _rms_bwd(x1, sp["norm_ffn_g"], dh2, dx2, "rms_ffn_bwd")
    dmixed = _mm_nt(dx1b, wb["w_out"], 512, 512, F32, "mm_out_dx")
    g["w_out"] = _mm_tn(mixed, dx1b, 512, 512, "mm_out_dw")
    dattn, dys, g["norm_attn_g"], g["norm_ssm_g"] = _mix_bwd(attn, ys, sp["norm_attn_g"], sp["norm_ssm_g"], dmixed)
    dpre, zb, dsb, dd = _glu_bwd(pre, s_glu, dys, u, d_skip, wb["w_glu"])
    g["d_skip"] = dd.reshape(N_SSM_GROUPS, SSM_GROUP)
    g["w_glu"] = _mm_tn(zb, dsb, 512, 512, "mm_glu_dw")
    du_seg, dbcat, dccat, dlam_re, dlam_im = _ssm_bwd(u_seg, _to_segments(dpre), bcat, ccat, lam_re, lam_im)
    dlb_re, dlb_im, dbb_re, dbb_im, g["c_re"], g["c_im"] = _ssm_unpack(dbcat, dccat, dlam_re, dlam_im)
    _, disc_vjp = jax.vjp(_ssm_disc, sp["a_re"], sp["a_im"], sp["log_step"], sp["b_re"], sp["b_im"])
    g["a_re"], g["a_im"], g["log_step"], g["b_re"], g["b_im"] = disc_vjp((dlb_re, dlb_im, dbb_re, dbb_im))
    du = _from_segments(du_seg) + dpre * d_skip
    dqkv, dsink = _attn_bwd(qkv, attn, dattn, lse, sp["sink"])
    g["sink"] = dsink
    dproj = _rope_bwd(dqkv, du, tabs)
    g["w_in"] = _mm_tn(h, dproj, 512, IN_WIDTH, "mm_in_dw")
    dh = _mm_nt(dproj, wb["w_in"], 512, 512, F32, "mm_in_dx")
    grad_x, _, g["norm_mix_g"] = _rms_bwd(x, sp["norm_mix_g"], dh, dx1, "rms_mix_bwd")
    return loss, grad_x, g


MESH = pl.DeviceIdType.MESH
ANY = pl.BlockSpec(memory_space=pl.ANY)


def _place():
    x, y, c = lax.axis_index("x"), lax.axis_index("y"), lax.axis_index("c")
    chips = [(1 - x, y), (x, 1 - y), (1 - x, 1 - y)]
    return x, y, c, chips


def _chip_index(px, py):
    return 2 * px + py


def _cast_bf16(w, name):
    r, c = w.shape
    tr = r if r <= 512 else r // 2

    def body(w_ref, o_ref):
        o_ref[...] = w_ref[...].astype(BF16)

    spec = pl.BlockSpec((tr, c), lambda i: (i, 0))
    return pl.pallas_call(body, grid=(r // tr,), in_specs=[spec], out_specs=spec,
                          out_shape=jax.ShapeDtypeStruct((r, c), BF16), name=name,
                          compiler_params=_params(("parallel",)))(w)


def _gather_weights(shards):
    nw = len(shards)

    def body(*refs):
        w_refs, o_refs = refs[:nw], refs[nw:2 * nw]
        send_sems, recv_sems, local_sems = refs[2 * nw:]
        x, y, c, chips = _place()
        mine = _chip_index(x, y)
        sibling = (x, y, 1 - c)

        def half(ref, chip, which):
            hr = ref.shape[-2] // 2
            return ref.at[chip, pl.ds(which * hr, hr), :]

        def src_half(ref, which):
            hr = ref.shape[-2] // 2
            return ref.at[pl.ds(which * hr, hr), :]

        def copy(wi, k, src, dst, to):
            return pltpu.make_async_remote_copy(src_ref=src, dst_ref=dst, send_sem=send_sems.at[wi, k],
                                                recv_sem=recv_sems.at[wi, k], device_id=to, device_id_type=MESH)

        local, sent = [], []
        for wi in range(nw):
            cp = pltpu.make_async_copy(w_refs[wi], o_refs[wi].at[mine], local_sems.at[wi])
            cp.start()
            local.append(cp)
            for j, chip in enumerate(chips):
                cp = copy(wi, j, src_half(w_refs[wi], c), half(o_refs[wi], mine, c), (*chip, c))
                cp.start()
                sent.append(cp)
        for wi in range(nw):
            for j, chip in enumerate(chips):
                got = half(o_refs[wi], _chip_index(*chip), c)
                copy(wi, j, got, got, (*chip, c)).wait_recv()
                cp = copy(wi, 3 + j, got, got, sibling)
                cp.start()
                sent.append(cp)
        for wi in range(nw):
            for j, chip in enumerate(chips):
                got = half(o_refs[wi], _chip_index(*chip), 1 - c)
                copy(wi, 3 + j, got, got, sibling).wait_recv()
        for cp in sent:
            cp.wait_send()
        for cp in local:
            cp.wait()

    return pl.pallas_call(
        body, in_specs=[ANY] * nw, out_specs=[ANY] * nw,
        out_shape=[jax.ShapeDtypeStruct((4, *s.shape), s.dtype) for s in shards],
        scratch_shapes=[pltpu.SemaphoreType.DMA((nw, 6)), pltpu.SemaphoreType.DMA((nw, 6)),
                        pltpu.SemaphoreType.DMA((nw,))],
        name="gather_weights",
    )(*shards)


def _pair_exchange(grads):
    na = len(grads)

    def body(*refs):
        g_refs, o_refs = refs[:na], refs[na:2 * na]
        send_sems, recv_sems, local_sems = refs[2 * na:]
        x, y, c, _ = _place()
        cps = []
        for ai in range(na):
            hr = g_refs[ai].shape[1] // 2
            keep = g_refs[ai].at[:, pl.ds(c * hr, hr), :]
            give = g_refs[ai].at[:, pl.ds((1 - c) * hr, hr), :]
            lc = pltpu.make_async_copy(keep, o_refs[ai].at[0], local_sems.at[ai])
            lc.start()
            rc = pltpu.make_async_remote_copy(src_ref=give, dst_ref=o_refs[ai].at[1], send_sem=send_sems.at[ai],
                                              recv_sem=recv_sems.at[ai], device_id=(x, y, 1 - c),
                                              device_id_type=MESH)
            rc.start()
            cps.append((lc, rc))
        for lc, rc in cps:
            lc.wait()
            rc.wait()

    return pl.pallas_call(
        body, in_specs=[ANY] * na, out_specs=[ANY] * na,
        out_shape=[jax.ShapeDtypeStruct((2, 4, g.shape[1] // 2, g.shape[2]), F32) for g in grads],
        scratch_shapes=[pltpu.SemaphoreType.DMA((na,)), pltpu.SemaphoreType.DMA((na,)),
                        pltpu.SemaphoreType.DMA((na,))],
        name="pair_exchange",
    )(*grads)


def _chip_exchange(sums):
    na = len(sums)

    def body(*refs):
        s_refs, o_refs = refs[:na], refs[na:2 * na]
        send_sems, recv_sems, local_sems = refs[2 * na:]
        x, y, c, chips = _place()
        mine = _chip_index(x, y)
        cps = []
        for ai in range(na):
            lc = pltpu.make_async_copy(s_refs[ai].at[mine], o_refs[ai].at[mine], local_sems.at[ai])
            lc.start()
            cps.append(lc)
            for j, chip in enumerate(chips):
                rc = pltpu.make_async_remote_copy(
                    src_ref=s_refs[ai].at[_chip_index(*chip)], dst_ref=o_refs[ai].at[mine],
                    send_sem=send_sems.at[ai, j], recv_sem=recv_sems.at[ai, j],
                    device_id=(*chip, c), device_id_type=MESH)
                rc.start()
                cps.append(rc)
        for cp in cps:
            cp.wait()

    return pl.pallas_call(
        body, in_specs=[ANY] * na, out_specs=[ANY] * na,
        out_shape=[jax.ShapeDtypeStruct(s.shape, F32) for s in sums],
        scratch_shapes=[pltpu.SemaphoreType.DMA((na, 3)), pltpu.SemaphoreType.DMA((na, 3)),
                        pltpu.SemaphoreType.DMA((na,))],
        name="chip_exchange",
    )(*sums)


def _final_exchange(halves, small):
    nh = len(halves)

    def body(*refs):
        h_refs, s_ref = refs[:nh], refs[nh]
        o_refs, so_ref = refs[nh + 1:2 * nh + 1], refs[2 * nh + 1]
        send_sems, recv_sems, local_sems, ssend_sems, srecv_sems = refs[2 * nh + 2:]
        x, y, c, _ = _place()
        me = 4 * x + 2 * y + c
        cps = []
        for hi in range(nh):
            hr = h_refs[hi].shape[0]
            rows = o_refs[hi].at[pl.ds(c * hr, hr), :]
            lc = pltpu.make_async_copy(h_refs[hi], rows, local_sems.at[hi])
            rc = pltpu.make_async_remote_copy(src_ref=h_refs[hi], dst_ref=rows, send_sem=send_sems.at[hi],
                                              recv_sem=recv_sems.at[hi], device_id=(x, y, 1 - c),
                                              device_id_type=MESH)
            lc.start()
            rc.start()
            cps += [lc, rc]
        lc = pltpu.make_async_copy(s_ref, so_ref.at[me], local_sems.at[nh])
        lc.start()
        cps.append(lc)
        for r in range(1, 8):
            fx, fy, fc = (r >> 2) & 1, (r >> 1) & 1, r & 1
            peer = (1 - x if fx else x, 1 - y if fy else y, 1 - c if fc else c)
            rc = pltpu.make_async_remote_copy(src_ref=s_ref, dst_ref=so_ref.at[me], send_sem=ssend_sems.at[r - 1],
                                              recv_sem=srecv_sems.at[r - 1], device_id=peer, device_id_type=MESH)
            rc.start()
            cps.append(rc)
        for cp in cps:
            cp.wait()

    return pl.pallas_call(
        body, in_specs=[ANY] * (nh + 1), out_specs=[ANY] * (nh + 1),
        out_shape=[jax.ShapeDtypeStruct((2 * h.shape[0], h.shape[1]), F32) for h in halves]
        + [jax.ShapeDtypeStruct((8, *small.shape), F32)],
        scratch_shapes=[pltpu.SemaphoreType.DMA((nh,)), pltpu.SemaphoreType.DMA((nh,)),
                        pltpu.SemaphoreType.DMA((nh + 1,)), pltpu.SemaphoreType.DMA((7,)),
                        pltpu.SemaphoreType.DMA((7,))],
        name="final_exchange",
    )(*halves, small)


def _sum_leading(t, name):
    n = t.shape[0]
    shape = t.shape[1:]
    t3 = t.reshape(n, -1, shape[-1])
    rows, cols = t3.shape[1], t3.shape[2]
    tr = rows
    for cand in (512, 448, 384, 352, 320, 256, 192, 176, 160, 128, 64, 40, 32, 16, 8):
        if rows % cand == 0:
            tr = cand
            break

    def body(t_ref, o_ref):
        acc = t_ref[0]
        for k in range(1, n):
            acc = acc + t_ref[k]
        o_ref[...] = acc

    out = pl.pallas_call(
        body, grid=(rows // tr,), in_specs=[pl.BlockSpec((n, tr, cols), lambda i: (0, i, 0))],
        out_specs=pl.BlockSpec((tr, cols), lambda i: (i, 0)),
        out_shape=jax.ShapeDtypeStruct((rows, cols), F32), name=name,
        compiler_params=_params(("parallel",)),
    )(t3)
    return out.reshape(shape)


def _adamw(w, g, m, v, name):
    shape = w.shape
    n = w.size
    if w.ndim >= 2 and shape[-1] >= LANES:
        two_d = (n // shape[-1], shape[-1])
    elif n % LANES == 0:
        two_d = (n // LANES, LANES)
    else:
        two_d = (1, n)
    r, c = two_d
    tr = r
    for cand in (512, 256, 176, 128, 64):
        if r > cand and r % cand == 0 and cand * c <= 256 * 1024:
            tr = cand
            break
    c1 = 1.0 - ADAM_B1 ** ADAM_STEP
    c2 = 1.0 - ADAM_B2 ** ADAM_STEP

    def body(w_ref, g_ref, m_ref, v_ref, d_ref, nm_ref, nv_ref):
        gv = g_ref[...]
        nm = ADAM_B1 * m_ref[...] + (1.0 - ADAM_B1) * gv
        nv = ADAM_B2 * v_ref[...] + (1.0 - ADAM_B2) * (gv * gv)
        d_ref[...] = -ADAM_LR * ((nm / c1) / (jnp.sqrt(nv / c2) + ADAM_EPS) + ADAM_WD * w_ref[...])
        nm_ref[...] = nm
        nv_ref[...] = nv

    spec = pl.BlockSpec((tr, c), lambda i: (i, 0))
    out = jax.ShapeDtypeStruct((r, c), F32)
    d, nm, nv = pl.pallas_call(
        body, grid=(r // tr,), in_specs=[spec] * 4, out_specs=[spec] * 3, out_shape=[out] * 3, name=name,
        compiler_params=_params(("parallel",)),
    )(w.reshape(two_d), g.reshape(two_d), m.reshape(two_d), v.reshape(two_d))
    return d.reshape(shape), nm.reshape(shape), nv.reshape(shape)


BIG = ("w_in", "w_glu", "w_out", "w_up", "w_down")
WEIGHTS = ("norm_mix_g", "w_in", "a_re", "a_im", "log_step", "b_re", "b_im", "c_re", "c_im", "d_skip", "w_glu",
           "sink", "norm_attn_g", "norm_ssm_g", "w_out", "norm_ffn_g", "w_up", "conv_w", "conv_b", "w_down",
           "norm_final_g")
SMALL = ("norm_mix_g", "a_re", "a_im", "log_step", "b_re", "b_im", "c_re", "c_im", "d_skip", "sink",
         "norm_attn_g", "norm_ssm_g", "norm_ffn_g", "conv_w", "conv_b", "norm_final_g")
SMALL_ROWS = 40
N_DEV = 8


def _full_matrices(gathered):
    w_in, w_glu, w_out, w_up, w_down = gathered
    cols = lambda t: t.transpose(1, 0, 2).reshape(t.shape[1], 4 * t.shape[2])
    rows = lambda t: t.reshape(4 * t.shape[1], t.shape[2])
    return {"w_in": cols(w_in), "w_glu": rows(w_glu), "w_out": rows(w_out), "w_up": cols(w_up),
            "w_down": rows(w_down)}


def _by_owner(name, g):
    if name in ("w_in", "w_up"):
        return g.reshape(g.shape[0], 4, g.shape[1] // 4).transpose(1, 0, 2)
    return g.reshape(4, g.shape[0] // 4, g.shape[1])


def kernel(x, norm_mix_g, w_in, a_re, a_im, log_step, b_re, b_im, c_re, c_im, d_skip, w_glu, sink, norm_attn_g, norm_ssm_g, w_out, norm_ffn_g, w_up, conv_w, conv_b, w_down, norm_final_g, loss_target, m_norm_mix_g, m_w_in, m_a_re, m_a_im, m_log_step, m_b_re, m_b_im, m_c_re, m_c_im, m_d_skip, m_w_glu, m_sink, m_norm_attn_g, m_norm_ssm_g, m_w_out, m_norm_ffn_g, m_w_up, m_conv_w, m_conv_b, m_w_down, m_norm_final_g, v_norm_mix_g, v_w_in, v_a_re, v_a_im, v_log_step, v_b_re, v_b_im, v_c_re, v_c_im, v_d_skip, v_w_glu, v_sink, v_norm_attn_g, v_norm_ssm_g, v_w_out, v_norm_ffn_g, v_w_up, v_conv_w, v_conv_b, v_w_down, v_norm_final_g):
    given = dict(locals())
    w = {n: given[n] for n in WEIGHTS}
    m = {n: given["m_" + n] for n in WEIGHTS}
    v = {n: given["v_" + n] for n in WEIGHTS}
    xy = 2 * lax.axis_index("x") + lax.axis_index("y")

    shards = [_cast_bf16(w[n][0], "cast_" + n) for n in BIG]
    conv_rows = jnp.pad(w["conv_w"][0], ((0, 2 * SUBLANES - 3), (0, 0)))
    *gathered, conv_all = _gather_weights(shards + [conv_rows])
    wb = _full_matrices(gathered)

    sp = {n: w[n][0] for n in ("a_re", "a_im", "log_step", "b_re", "b_im", "c_re", "c_im", "d_skip",
                               "norm_mix_g", "norm_attn_g", "norm_ssm_g", "norm_ffn_g", "sink", "conv_b")}
    for n in ("norm_mix_g", "norm_attn_g", "norm_ssm_g", "norm_ffn_g", "sink", "conv_b"):
        sp[n] = sp[n].reshape(1, -1)
    sp["conv_w"] = conv_all[:, :3].transpose(1, 0, 2).reshape(3, 2 * D_FF)
    sp["norm_final_g"] = w["norm_final_g"]
    loss, grad_x, g = _local_step(x[0], loss_target[0], wb, sp)

    flat = jnp.concatenate([g[n].reshape(-1) for n in SMALL])
    pad = N_DEV * SMALL_ROWS * D_MODEL - flat.shape[0]
    small = jnp.concatenate([flat, jnp.zeros((pad,), F32)]).reshape(4, 2 * SMALL_ROWS, D_MODEL)
    by_owner = [_by_owner(n, g[n]) for n in BIG] + [small]
    pairs = _pair_exchange(by_owner)
    chip_sums = [_sum_leading(p, "pair_sum_%d" % i) for i, p in enumerate(pairs)]
    landed = _chip_exchange(chip_sums)
    halves = [_sum_leading(t, "chip_sum_%d" % i) for i, t in enumerate(landed)]
    *shard_grads, small_all = _final_exchange(halves[:-1], halves[-1])
    grads = dict(zip(BIG, shard_grads))
    flat = small_all.reshape(-1)
    off = 0
    for n in SMALL:
        shape = (3, 4 * w[n].shape[-1]) if n == "conv_w" else w[n].shape[1:] if n != "norm_final_g" else w[n].shape
        size = math.prod(shape)
        grads[n] = flat[off:off + size].reshape(shape)
        off += size
    cw = w["conv_w"].shape[-1]
    grads["conv_w"] = lax.dynamic_slice_in_dim(grads["conv_w"], xy * cw, cw, axis=1)

    outs_g, outs_d, outs_m, outs_v = [], [], [], []
    for n in WEIGHTS:
        gn = grads[n].reshape(w[n].shape)
        dn, mn, vn = _adamw(w[n], gn, m[n], v[n], "adamw_" + n)
        outs_g.append(gn)
        outs_d.append(dn)
        outs_m.append(mn)
        outs_v.append(vn)
    loss = lax.psum(loss[0, 0], ("x", "y", "c"))
    return (loss, grad_x[None], *outs_g, *outs_d, *outs_m, *outs_v)
```

```python
import functools
import math

import jax
import jax.numpy as jnp
from jax import lax
from jax.experimental import pallas as pl
from jax.experimental.pallas import tpu as pltpu

F32 = jnp.float32
BF16 = jnp.bfloat16

D_MODEL = 1024
N_Q_HEADS = 8
N_KV_HEADS = 2
HEAD_DIM = 64
ATTN_WIDTH = 512
KV_WIDTH = 128
QKV_WIDTH = ATTN_WIDTH + 2 * KV_WIDTH
WINDOW = 128
BLOCK = 128
ROPE_DIM = 16
ROPE_THETA = 500000.0
SSM_WIDTH = 512
SSM_GROUP = 16
N_SSM_GROUPS = 32
SSM_STATE = 64
IN_WIDTH = 1280
D_FF = 2816
EPS = 1e-6
ADAM_LR = 0.001
ADAM_B1 = 0.9
ADAM_B2 = 0.999
ADAM_EPS = 1e-08
ADAM_WD = 0.01
ADAM_STEP = 10

VMEM_BYTES_V7X = 64 * 1024 * 1024
SUBLANES = 8
LANES = 128
SSM_CB = 4
SSM_CH = 128
SSM_ST = 512
N_SEG = SUBLANES

NN = (((1,), (0,)), ((), ()))
NT = (((1,), (1,)), ((), ()))
TN = (((0,), (0,)), ((), ()))


def _params(sem=None, vmem_mb=48):
    return pltpu.CompilerParams(dimension_semantics=sem, vmem_limit_bytes=vmem_mb * 1024 * 1024)


def _dg(a, b, dims):
    return lax.dot_general(a, b, dims, preferred_element_type=F32)


def _split(a):
    hi = a.astype(BF16)
    lo = (a - hi.astype(F32)).astype(BF16)
    return hi, lo


def _dot3w(a, b_hi, b_lo, dims):
    a_hi, a_lo = _split(a)
    return _dg(a_hi, b_hi, dims) + _dg(a_hi, b_lo, dims) + _dg(a_lo, b_hi, dims)


def _dot3(a, b, dims):
    b_hi, b_lo = _split(b)
    return _dot3w(a, b_hi, b_lo, dims)


def _sigmoid(x):
    return 1.0 / (1.0 + jnp.exp(-x))


_SQRT_HALF = 0.7071067811865476
_INV_SQRT_2PI = 0.3989422804014327


def _gelu(x):
    return 0.5 * x * (1.0 + lax.erf(x * _SQRT_HALF))


def _gelu_grad(x):
    return 0.5 * (1.0 + lax.erf(x * _SQRT_HALF)) + x * (_INV_SQRT_2PI * jnp.exp(-0.5 * x * x))


def _mm_nn(a, b, tm, tn, out_dtype, name, res=None):
    m, k = a.shape
    n = b.shape[1]

    def body(*refs):
        if res is None:
            a_ref, b_ref, o_ref = refs
            o_ref[...] = _dg(a_ref[...], b_ref[...], NN).astype(out_dtype)
        else:
            a_ref, b_ref, r_ref, o_ref = refs
            o_ref[...] = (r_ref[...] + _dg(a_ref[...], b_ref[...], NN)).astype(out_dtype)

    in_specs = [pl.BlockSpec((tm, k), lambda i, j: (i, 0)), pl.BlockSpec((k, tn), lambda i, j: (0, j))]
    args = [a, b]
    if res is not None:
        in_specs.append(pl.BlockSpec((tm, tn), lambda i, j: (i, j)))
        args.append(res)
    return pl.pallas_call(
        body, grid=(m // tm, n // tn), in_specs=in_specs,
        out_specs=pl.BlockSpec((tm, tn), lambda i, j: (i, j)),
        out_shape=jax.ShapeDtypeStruct((m, n), out_dtype), name=name,
        compiler_params=_params(("parallel", "parallel")),
    )(*args)


def _mm_nt(a, b, tm, tn, out_dtype, name):
    m, k = a.shape
    n = b.shape[0]

    def body(a_ref, b_ref, o_ref):
        o_ref[...] = _dg(a_ref[...], b_ref[...], NT).astype(out_dtype)

    return pl.pallas_call(
        body, grid=(m // tm, n // tn),
        in_specs=[pl.BlockSpec((tm, k), lambda i, j: (i, 0)), pl.BlockSpec((tn, k), lambda i, j: (j, 0))],
        out_specs=pl.BlockSpec((tm, tn), lambda i, j: (i, j)),
        out_shape=jax.ShapeDtypeStruct((m, n), out_dtype), name=name,
        compiler_params=_params(("parallel", "parallel")),
    )(a, b)


def _mm_tn(a, b, tm, tn, name):
    k, m = a.shape
    n = b.shape[1]

    def body(a_ref, b_ref, o_ref):
        o_ref[...] = _dg(a_ref[...], b_ref[...], TN)

    return pl.pallas_call(
        body, grid=(m // tm, n // tn),
        in_specs=[pl.BlockSpec((k, tm), lambda i, j: (0, i)), pl.BlockSpec((k, tn), lambda i, j: (0, j))],
        out_specs=pl.BlockSpec((tm, tn), lambda i, j: (i, j)),
        out_shape=jax.ShapeDtypeStruct((m, n), F32), name=name,
        compiler_params=_params(("parallel", "parallel")),
    )(a, b)


TM_EW = 256


def _rms_fwd(x, g, name):
    l, d = x.shape

    def body(x_ref, g_ref, h_ref):
        xv = x_ref[...]
        r = lax.rsqrt(jnp.mean(xv * xv, axis=-1, keepdims=True) + EPS)
        h_ref[...] = (xv * r * g_ref[...]).astype(BF16)

    return pl.pallas_call(
        body, grid=(l // TM_EW,),
        in_specs=[pl.BlockSpec((TM_EW, d), lambda i: (i, 0)), pl.BlockSpec((1, d), lambda i: (0, 0))],
        out_specs=pl.BlockSpec((TM_EW, d), lambda i: (i, 0)),
        out_shape=jax.ShapeDtypeStruct((l, d), BF16), name=name,
        compiler_params=_params(("parallel",)),
    )(x, g)


def _rms_bwd_vals(xv, gv, dy):
    r = lax.rsqrt(jnp.mean(xv * xv, axis=-1, keepdims=True) + EPS)
    xh = xv * r
    dxh = dy * gv
    dx = r * (dxh - xh * jnp.mean(dxh * xh, axis=-1, keepdims=True))
    return dx, dy * xh


def _rms_bwd(x, g, dy, res, name):
    l, d = x.shape

    def body(x_ref, g_ref, dy_ref, res_ref, dx_ref, dxb_ref, dg_ref):
        dx, dgr = _rms_bwd_vals(x_ref[...], g_ref[...], dy_ref[...])
        dx = dx + res_ref[...]
        dx_ref[...] = dx
        dxb_ref[...] = dx.astype(BF16)

        @pl.when(pl.program_id(0) == 0)
        def _():
            dg_ref[...] = jnp.zeros_like(dg_ref)

        dg_ref[...] += jnp.sum(dgr, axis=0, keepdims=True)

    row = pl.BlockSpec((TM_EW, d), lambda i: (i, 0))
    vec = pl.BlockSpec((1, d), lambda i: (0, 0))
    return pl.pallas_call(
        body, grid=(l // TM_EW,), in_specs=[row, vec, row, row], out_specs=[row, row, vec],
        out_shape=[jax.ShapeDtypeStruct((l, d), F32), jax.ShapeDtypeStruct((l, d), BF16),
                   jax.ShapeDtypeStruct((1, d), F32)],
        name=name, compiler_params=_params(("arbitrary",)),
    )(x, g, dy, res)


def _final_loss(x2, g, target):
    l, d = x2.shape

    def body(x_ref, g_ref, t_ref, loss_ref, dx_ref, dxb_ref, dg_ref):
        xv = x_ref[...]
        gv = g_ref[...]
        r = lax.rsqrt(jnp.mean(xv * xv, axis=-1, keepdims=True) + EPS)
        xh = xv * r
        e = xh * gv - t_ref[...]
        part = jnp.sum(jnp.sum(e * e, axis=1, keepdims=True), axis=0, keepdims=True) * (0.5 / d)
        dy = e * (1.0 / d)
        dxh = dy * gv
        dx = r * (dxh - xh * jnp.mean(dxh * xh, axis=-1, keepdims=True))
        dx_ref[...] = dx
        dxb_ref[...] = dx.astype(BF16)

        @pl.when(pl.program_id(0) == 0)
        def _():
            dg_ref[...] = jnp.zeros_like(dg_ref)
            loss_ref[...] = jnp.zeros_like(loss_ref)

        dg_ref[...] += jnp.sum(dy * xh, axis=0, keepdims=True)
        loss_ref[...] += part

    row = pl.BlockSpec((TM_EW, d), lambda i: (i, 0))
    vec = pl.BlockSpec((1, d), lambda i: (0, 0))
    one = pl.BlockSpec((1, 1), lambda i: (0, 0))
    return pl.pallas_call(
        body, grid=(l // TM_EW,), in_specs=[row, vec, row], out_specs=[one, row, row, vec],
        out_shape=[jax.ShapeDtypeStruct((1, 1), F32), jax.ShapeDtypeStruct((l, d), F32),
                   jax.ShapeDtypeStruct((l, d), BF16), jax.ShapeDtypeStruct((1, d), F32)],
        name="final_loss", compiler_params=_params(("arbitrary",)),
    )(x2, g, target)


def _mix_fwd(attn, ys, g_attn, g_ssm):
    l, w = attn.shape

    def body(a_ref, y_ref, ga_ref, gs_ref, o_ref):
        for src, gr, off in ((a_ref, ga_ref, 0), (y_ref, gs_ref, w)):
            xv = src[...]
            r = lax.rsqrt(jnp.mean(xv * xv, axis=-1, keepdims=True) + EPS)
            o_ref[:, off:off + w] = (xv * r * gr[...]).astype(BF16)

    row = pl.BlockSpec((TM_EW, w), lambda i: (i, 0))
    vec = pl.BlockSpec((1, w), lambda i: (0, 0))
    return pl.pallas_call(
        body, grid=(l // TM_EW,), in_specs=[row, row, vec, vec],
        out_specs=pl.BlockSpec((TM_EW, 2 * w), lambda i: (i, 0)),
        out_shape=jax.ShapeDtypeStruct((l, 2 * w), BF16), name="mix_fwd",
        compiler_params=_params(("parallel",)),
    )(attn, ys, g_attn, g_ssm)


def _mix_bwd(attn, ys, g_attn, g_ssm, dmixed):
    l, w = attn.shape

    def body(a_ref, y_ref, ga_ref, gs_ref, dm_ref, da_ref, dy_ref, dga_ref, dgs_ref):
        @pl.when(pl.program_id(0) == 0)
        def _():
            dga_ref[...] = jnp.zeros_like(dga_ref)
            dgs_ref[...] = jnp.zeros_like(dgs_ref)

        for src, gr, off, dst, dgr in ((a_ref, ga_ref, 0, da_ref, dga_ref), (y_ref, gs_ref, w, dy_ref, dgs_ref)):
            dx, dg_rows = _rms_bwd_vals(src[...], gr[...], dm_ref[:, off:off + w])
            dst[...] = dx
            dgr[...] += jnp.sum(dg_rows, axis=0, keepdims=True)

    row = pl.BlockSpec((TM_EW, w), lambda i: (i, 0))
    vec = pl.BlockSpec((1, w), lambda i: (0, 0))
    return pl.pallas_call(
        body, grid=(l // TM_EW,),
        in_specs=[row, row, vec, vec, pl.BlockSpec((TM_EW, 2 * w), lambda i: (i, 0))],
        out_specs=[row, row, vec, vec],
        out_shape=[jax.ShapeDtypeStruct((l, w), F32), jax.ShapeDtypeStruct((l, w), F32),
                   jax.ShapeDtypeStruct((1, w), F32), jax.ShapeDtypeStruct((1, w), F32)],
        name="mix_bwd", compiler_params=_params(("arbitrary",)),
    )(attn, ys, g_attn, g_ssm, dmixed)


def _rope_tables(l):
    half = ROPE_DIM // 2
    inv_freq = jnp.power(ROPE_THETA, -jnp.arange(half, dtype=F32) / half)
    ang = jnp.arange(l, dtype=F32)[:, None] * inv_freq[None, :]
    cos, sin = jnp.cos(ang), jnp.sin(ang)
    ones = jnp.ones((l, HEAD_DIM - ROPE_DIM), F32)
    zeros = jnp.zeros((l, HEAD_DIM - ROPE_DIM), F32)
    zh = jnp.zeros((l, half), F32)
    c = jnp.concatenate([cos, cos, ones], axis=1)
    s_lo = jnp.concatenate([-sin, zh, zeros], axis=1)
    s_hi = jnp.concatenate([zh, sin, zeros], axis=1)
    return tuple(jnp.tile(t, (1, LANES // HEAD_DIM)) for t in (c, s_lo, s_hi))


def _rope_fwd(proj, tabs):
    l = proj.shape[0]
    nq = ATTN_WIDTH // LANES

    def body(p_ref, c_ref, lo_ref, hi_ref, o_ref):
        c, lo, hi = c_ref[...], lo_ref[...], hi_ref[...]
        for blk in range(nq + 1):
            t = p_ref[:, blk * LANES:(blk + 1) * LANES]
            rot = t * c + pltpu.roll(t, LANES - 8, 1) * lo + pltpu.roll(t, 8, 1) * hi
            o_ref[:, blk * LANES:(blk + 1) * LANES] = rot.astype(BF16)
        o_ref[:, (nq + 1) * LANES:] = p_ref[:, (nq + 1) * LANES:].astype(BF16)

    tab = pl.BlockSpec((TM_EW, LANES), lambda i: (i, 0))
    return pl.pallas_call(
        body, grid=(l // TM_EW,),
        in_specs=[pl.BlockSpec((TM_EW, QKV_WIDTH), lambda i: (i, 0)), tab, tab, tab],
        out_specs=pl.BlockSpec((TM_EW, QKV_WIDTH), lambda i: (i, 0)),
        out_shape=jax.ShapeDtypeStruct((l, QKV_WIDTH), BF16), name="rope_fwd",
        compiler_params=_params(("parallel",)),
    )(proj, *tabs)


def _rope_bwd(dqkv, du, tabs):
    l = dqkv.shape[0]
    nq = ATTN_WIDTH // LANES

    def body(d_ref, du_ref, c_ref, lo_ref, hi_ref, o_ref):
        c, lo, hi = c_ref[...], lo_ref[...], hi_ref[...]
        for blk in range(nq + 1):
            t = d_ref[:, blk * LANES:(blk + 1) * LANES]
            g = t * c + pltpu.roll(t * lo, 8, 1) + pltpu.roll(t * hi, LANES - 8, 1)
            o_ref[:, blk * LANES:(blk + 1) * LANES] = g.astype(BF16)
        o_ref[:, (nq + 1) * LANES:QKV_WIDTH] = d_ref[:, (nq + 1) * LANES:].astype(BF16)
        o_ref[:, QKV_WIDTH:] = du_ref[...].astype(BF16)

    tab = pl.BlockSpec((TM_EW, LANES), lambda i: (i, 0))
    return pl.pallas_call(
        body, grid=(l // TM_EW,),
        in_specs=[pl.BlockSpec((TM_EW, QKV_WIDTH), lambda i: (i, 0)),
                  pl.BlockSpec((TM_EW, SSM_WIDTH), lambda i: (i, 0)), tab, tab, tab],
        out_specs=pl.BlockSpec((TM_EW, IN_WIDTH), lambda i: (i, 0)),
        out_shape=jax.ShapeDtypeStruct((l, IN_WIDTH), BF16), name="rope_bwd",
        compiler_params=_params(("parallel",)),
    )(dqkv, du, *tabs)


_Q_COLS = ATTN_WIDTH // LANES
_SCALE = HEAD_DIM ** -0.5
_NEG = -1e30


def _window_specs(nb, width, col):
    return [
        pl.BlockSpec((BLOCK, width), lambda n: (jnp.maximum(n - 1, 0), col)),
        pl.BlockSpec((BLOCK, width), lambda n: (n, col)),
        pl.BlockSpec((BLOCK, width), lambda n: (jnp.minimum(n + 1, nb - 1), col)),
    ]


def _attn_fwd(qkv, sink):
    l = qkv.shape[0]
    nb = l // BLOCK
    grp = N_Q_HEADS // N_KV_HEADS

    def body(sink_ref, q_ref, k0, k1, k2, v0, v1, v2, o_ref, lse_ref):
        n = pl.program_id(0)
        q = q_ref[...]
        kw = jnp.concatenate([k0[...], k1[...], k2[...]], axis=0)
        vw = jnp.concatenate([v0[...], v1[...], v2[...]], axis=0)
        row = lax.broadcasted_iota(jnp.int32, (BLOCK, 3 * BLOCK), 0)
        col = lax.broadcasted_iota(jnp.int32, (BLOCK, 3 * BLOCK), 1)
        rel = col - BLOCK - row
        valid = (jnp.abs(rel) <= WINDOW)
        valid &= jnp.logical_not((n == 0) & (col < BLOCK))
        valid &= jnp.logical_not((n == nb - 1) & (col >= 2 * BLOCK))
        for h in range(N_Q_HEADS):
            hk = h // grp
            qh = q[:, h * HEAD_DIM:(h + 1) * HEAD_DIM]
            kh = kw[:, hk * HEAD_DIM:(hk + 1) * HEAD_DIM]
            vh = vw[:, hk * HEAD_DIM:(hk + 1) * HEAD_DIM]
            s = jnp.where(valid, _dg(qh, kh, NT) * _SCALE, _NEG)
            sk = sink_ref[0, h]
            m = jnp.maximum(jnp.max(s, axis=1, keepdims=True), sk)
            p = jnp.exp(s - m)
            denom = jnp.sum(p, axis=1, keepdims=True) + jnp.exp(sk - m)
            o = _dg((p / denom).astype(BF16), vh, NN)
            o_ref[:, h * HEAD_DIM:(h + 1) * HEAD_DIM] = o
            lse_ref[:, h:h + 1] = m + jnp.log(denom)

    return pl.pallas_call(
        body, grid=(nb,),
        in_specs=[pl.BlockSpec(memory_space=pltpu.SMEM),
                  pl.BlockSpec((BLOCK, ATTN_WIDTH), lambda n: (n, 0))]
        + _window_specs(nb, KV_WIDTH, _Q_COLS) + _window_specs(nb, KV_WIDTH, _Q_COLS + 1),
        out_specs=[pl.BlockSpec((BLOCK, ATTN_WIDTH), lambda n: (n, 0)),
                   pl.BlockSpec((BLOCK, N_Q_HEADS), lambda n: (n, 0))],
        out_shape=[jax.ShapeDtypeStruct((l, ATTN_WIDTH), F32), jax.ShapeDtypeStruct((l, N_Q_HEADS), F32)],
        name="attn_fwd", compiler_params=_params(("parallel",)),
    )(sink, qkv, qkv, qkv, qkv, qkv, qkv, qkv)


def _attn_bwd(qkv, attn, dattn, lse, sink):
    l = qkv.shape[0]
    nb = l // BLOCK
    grp = N_Q_HEADS // N_KV_HEADS

    def body(sink_ref, q0, q1, q2, k0, k1, k2, v0, v1, v2, o0, o1, o2, d0, d1, d2,
             l0, l1, l2, dqkv_ref, dsink_ref):
        n = pl.program_id(0)
        first, last = n == 0, n == nb - 1

        @pl.when(first)
        def _():
            dsink_ref[...] = jnp.zeros_like(dsink_ref)

        cat = lambda a, b, c: jnp.concatenate([a[...], b[...], c[...]], axis=0)
        qw, kw, vw = cat(q0, q1, q2), cat(k0, k1, k2), cat(v0, v1, v2)
        dow = cat(d0, d1, d2)
        prodw = cat(o0, o1, o2) * dow
        lsew = cat(l0, l1, l2)
        dob = dow.astype(BF16)
        q, k, v, do, lse_n = q1[...], k1[...], v1[...], dob[BLOCK:2 * BLOCK], l1[...]

        row = lax.broadcasted_iota(jnp.int32, (BLOCK, 3 * BLOCK), 0)
        col = lax.broadcasted_iota(jnp.int32, (BLOCK, 3 * BLOCK), 1)
        valid_q = jnp.abs(col - BLOCK - row) <= WINDOW
        valid_q &= jnp.logical_not(first & (col < BLOCK))
        valid_q &= jnp.logical_not(last & (col >= 2 * BLOCK))
        rowk = lax.broadcasted_iota(jnp.int32, (3 * BLOCK, BLOCK), 0)
        colk = lax.broadcasted_iota(jnp.int32, (3 * BLOCK, BLOCK), 1)
        valid_k = jnp.abs(colk + BLOCK - rowk) <= WINDOW
        valid_k &= jnp.logical_not(first & (rowk < BLOCK))
        valid_k &= jnp.logical_not(last & (rowk >= 2 * BLOCK))

        dsink_parts = []
        for hk in range(N_KV_HEADS):
            ksl = slice(hk * HEAD_DIM, (hk + 1) * HEAD_DIM)
            dk = jnp.zeros((BLOCK, HEAD_DIM), F32)
            dv = jnp.zeros((BLOCK, HEAD_DIM), F32)
            for g in range(grp):
                h = hk * grp + g
                hsl = slice(h * HEAD_DIM, (h + 1) * HEAD_DIM)
                deltaw = jnp.sum(prodw[:, hsl], axis=1, keepdims=True)
                delta = deltaw[BLOCK:2 * BLOCK]
                s = jnp.where(valid_q, _dg(q[:, hsl], kw[:, ksl], NT) * _SCALE, _NEG)
                p = jnp.exp(s - lse_n[:, h:h + 1])
                dp = _dg(do[:, hsl], vw[:, ksl], NT)
                ds = (p * (dp - delta) * _SCALE).astype(BF16)
                dqkv_ref[:, hsl] = _dg(ds, kw[:, ksl], NN)
                dsink_parts.append(jnp.sum(jnp.exp(sink_ref[0, h] - lse_n[:, h:h + 1]) * delta,
                                           axis=0, keepdims=True))
                s2 = jnp.where(valid_k, _dg(qw[:, hsl], k[:, ksl], NT) * _SCALE, _NEG)
                p2 = jnp.where(valid_k, jnp.exp(s2 - lsew[:, h:h + 1]), 0.0)
                dv += _dg(p2.astype(BF16), dob[:, hsl], TN)
                dp2 = _dg(dob[:, hsl], v[:, ksl], NT)
                ds2 = (p2 * (dp2 - deltaw) * _SCALE).astype(BF16)
                dk += _dg(ds2, qw[:, hsl], TN)
            dqkv_ref[:, ATTN_WIDTH + hk * HEAD_DIM:ATTN_WIDTH + (hk + 1) * HEAD_DIM] = dk
            dqkv_ref[:, ATTN_WIDTH + KV_WIDTH + hk * HEAD_DIM:ATTN_WIDTH + KV_WIDTH + (hk + 1) * HEAD_DIM] = dv
        dsink_ref[...] -= jnp.concatenate(dsink_parts, axis=1)

    return pl.pallas_call(
        body, grid=(nb,),
        in_specs=[pl.BlockSpec(memory_space=pltpu.SMEM)]
        + _window_specs(nb, ATTN_WIDTH, 0)
        + _window_specs(nb, KV_WIDTH, _Q_COLS) + _window_specs(nb, KV_WIDTH, _Q_COLS + 1)
        + _window_specs(nb, ATTN_WIDTH, 0) + _window_specs(nb, ATTN_WIDTH, 0)
        + _window_specs(nb, N_Q_HEADS, 0),
        out_specs=[pl.BlockSpec((BLOCK, QKV_WIDTH), lambda n: (n, 0)),
                   pl.BlockSpec((1, N_Q_HEADS), lambda n: (0, 0))],
        out_shape=[jax.ShapeDtypeStruct((l, QKV_WIDTH), F32), jax.ShapeDtypeStruct((1, N_Q_HEADS), F32)],
        name="attn_bwd", compiler_params=_params(("arbitrary",)),
    )(sink, qkv, qkv, qkv, qkv, qkv, qkv, qkv, qkv, qkv, attn, attn, attn,
      dattn, dattn, dattn, lse, lse, lse)


def _ssm_disc(a_re, a_im, log_step, b_re, b_im):
    step = jnp.exp(log_step)[..., None]
    mag = jnp.exp(a_re * step)
    lb_re, lb_im = mag * jnp.cos(a_im * step), mag * jnp.sin(a_im * step)
    nr, ni = lb_re - 1.0, lb_im
    den = a_re * a_re + a_im * a_im
    f_re = ((nr * a_re + ni * a_im) / den)[..., None]
    f_im = ((ni * a_re - nr * a_im) / den)[..., None]
    return lb_re, lb_im, f_re * b_re - f_im * b_im, f_re * b_im + f_im * b_re


def _ssm_pack(lb_re, lb_im, bb_re, bb_im, c_re, c_im):
    eye = jnp.eye(SSM_CH // SSM_GROUP, dtype=F32)
    ng = SSM_CH // SSM_GROUP

    def diag_b(bb):
        t = bb.reshape(2, SSM_CB, ng, SSM_STATE, SSM_GROUP)
        return jnp.einsum('dkgpc,gh->dkgchp', t, eye).reshape(2, SSM_CB, SSM_CH, SSM_ST)

    def diag_c(cc):
        t = cc.reshape(2, SSM_CB, ng, SSM_GROUP, SSM_STATE)
        return jnp.einsum('dkgcp,gh->dkhpgc', t, eye).reshape(2, SSM_CB, SSM_ST, SSM_CH)

    bcat = jnp.concatenate([diag_b(bb_re), diag_b(bb_im)], axis=-1)
    ccat = jnp.concatenate([diag_c(c_re), -diag_c(c_im)], axis=-2)
    lam_re = lb_re.reshape(2, SSM_CB, 1, SSM_ST)
    lam_im = lb_im.reshape(2, SSM_CB, 1, SSM_ST)
    return bcat, ccat, lam_re, lam_im


def _ssm_unpack(dbcat, dccat, dlam_re, dlam_im):
    ng = SSM_CH // SSM_GROUP
    eye = jnp.eye(ng, dtype=F32)

    def undiag_b(t):
        t = t.reshape(2, SSM_CB, ng, SSM_GROUP, ng, SSM_STATE)
        return jnp.einsum('dkgchp,gh->dkgpc', t, eye).reshape(2, N_SSM_GROUPS, SSM_STATE, SSM_GROUP)

    def undiag_c(t):
        t = t.reshape(2, SSM_CB, ng, SSM_STATE, ng, SSM_GROUP)
        return jnp.einsum('dkhpgc,gh->dkgcp', t, eye).reshape(2, N_SSM_GROUPS, SSM_GROUP, SSM_STATE)

    dbb_re, dbb_im = undiag_b(dbcat[..., :SSM_ST]), undiag_b(dbcat[..., SSM_ST:])
    dc_re, dc_im = undiag_c(dccat[:, :, :SSM_ST]), -undiag_c(dccat[:, :, SSM_ST:])
    shape = (2, N_SSM_GROUPS, SSM_STATE)
    return dlam_re.reshape(shape), dlam_im.reshape(shape), dbb_re, dbb_im, dc_re, dc_im


def _to_segments(t):
    l, w = t.shape
    return t.reshape(N_SEG, l // N_SEG, w).transpose(1, 0, 2).reshape(l, w)


def _from_segments(t):
    l, w = t.shape
    return t.reshape(l // N_SEG, N_SEG, w).transpose(1, 0, 2).reshape(l, w)


def _cfma(ar, ai, xr, xi, br, bi):
    return ar * xr - ai * xi + br, ar * xi + ai * xr + bi


def _scan_segments(xs_ref, ar, ai, rev, nj, prev_ref=None):
    shape = (N_SEG, SSM_ST)
    ar = jnp.broadcast_to(ar, shape)
    ai = jnp.broadcast_to(ai, shape)
    zero = jnp.zeros(shape, F32)
    re_cols, im_cols = pl.ds(0, SSM_ST), pl.ds(SSM_ST, SSM_ST)

    def rows_of(jj):
        j = jnp.where(rev, nj - 1 - jj, jj)
        return j, pl.ds(pl.multiple_of(j * N_SEG, N_SEG), N_SEG)

    def pass1(jj, carry):
        _, rows = rows_of(jj)
        return _cfma(ar, ai, carry[0], carry[1], xs_ref[rows, re_cols], xs_ref[rows, im_cols])

    end_r, end_i = lax.fori_loop(0, nj, pass1, (zero, zero))

    pr, pi = ar, ai
    for _ in range(int(math.log2(nj))):
        pr, pi = pr * pr - pi * pi, 2.0 * pr * pi
    seg = lax.broadcasted_iota(jnp.int32, shape, 0)

    def chain(shift, keep):
        ir, ii = zero, zero
        for _ in range(N_SEG - 1):
            tr, ti = _cfma(pr, pi, ir, ii, end_r, end_i)
            ir = jnp.where(keep, pltpu.roll(tr, shift, 0), 0.0)
            ii = jnp.where(keep, pltpu.roll(ti, shift, 0), 0.0)
        return ir, ii

    up_r, up_i = chain(1, seg >= 1)
    dn_r, dn_i = chain(N_SEG - 1, seg <= N_SEG - 2)
    init_r, init_i = jnp.where(rev, dn_r, up_r), jnp.where(rev, dn_i, up_i)

    def pass2(jj, carry):
        j, rows = rows_of(jj)
        nr, ni = _cfma(ar, ai, carry[0], carry[1], xs_ref[rows, re_cols], xs_ref[rows, im_cols])
        xs_ref[rows, re_cols] = nr
        xs_ref[rows, im_cols] = ni
        if prev_ref is None:
            return nr, ni
        jp = jnp.where(rev, j - 1, j + 1)
        inside = (jp >= 0) & (jp < nj)
        prow = pl.ds(pl.multiple_of(jnp.clip(jp, 0, nj - 1) * N_SEG, N_SEG), N_SEG)
        xr, xi = prev_ref[prow, re_cols], prev_ref[prow, im_cols]
        f = jnp.where(inside, 1.0, 0.0)
        return nr, ni, carry[2] + f * (nr * xr + ni * xi), carry[3] + f * (ni * xr - nr * xi)

    if prev_ref is None:
        lax.fori_loop(0, nj, pass2, (init_r, init_i))
        return init_r, init_i, None, None
    _, _, acc_r, acc_i = lax.fori_loop(0, nj, pass2, (init_r, init_i, zero, zero))
    return init_r, init_i, acc_r, acc_i


SSM_RC = 256


def _ssm_specs(l):
    act = pl.BlockSpec((l, SSM_CH), lambda k, d: (0, k))
    bmat = pl.BlockSpec((None, None, SSM_CH, 2 * SSM_ST), lambda k, d: (d, k, 0, 0))
    cmat = pl.BlockSpec((None, None, 2 * SSM_ST, SSM_CH), lambda k, d: (d, k, 0, 0))
    lam = pl.BlockSpec((None, None, 1, SSM_ST), lambda k, d: (d, k, 0, 0))
    return act, bmat, cmat, lam


def _ssm_fwd(u_seg, bcat, ccat, lam_re, lam_im):
    l = u_seg.shape[0]
    nj = l // N_SEG
    b_hi, b_lo = _split(bcat)
    c_hi, c_lo = _split(ccat)

    def body(u_ref, bh_ref, bl_ref, ch_ref, cl_ref, lr_ref, li_ref, y_ref, xs_ref):
        d = pl.program_id(1)

        def bu_chunk(i, _):
            rows = pl.ds(pl.multiple_of(i * SSM_RC, SSM_RC), SSM_RC)
            xs_ref[rows, :] = _dot3w(u_ref[rows, :], bh_ref[...], bl_ref[...], NN)
            return 0

        lax.fori_loop(0, l // SSM_RC, bu_chunk, 0)
        _scan_segments(xs_ref, lr_ref[...], li_ref[...], d == 1, nj)

        def y_chunk(i, _):
            rows = pl.ds(pl.multiple_of(i * SSM_RC, SSM_RC), SSM_RC)
            yv = _dot3w(xs_ref[rows, :], ch_ref[...], cl_ref[...], NN)

            @pl.when(d == 0)
            def _():
                y_ref[rows, :] = yv

            @pl.when(d == 1)
            def _():
                y_ref[rows, :] += yv

            return 0

        lax.fori_loop(0, l // SSM_RC, y_chunk, 0)

    act, bmat, cmat, lam = _ssm_specs(l)
    return pl.pallas_call(
        body, grid=(SSM_CB, 2), in_specs=[act, bmat, bmat, cmat, cmat, lam, lam], out_specs=act,
        out_shape=jax.ShapeDtypeStruct((l, SSM_WIDTH), F32),
        scratch_shapes=[pltpu.VMEM((l, 2 * SSM_ST), F32)],
        name="ssm_fwd", compiler_params=_params(("parallel", "arbitrary"), vmem_mb=56),
    )(u_seg, b_hi, b_lo, c_hi, c_lo, lam_re, lam_im)


def _ssm_bwd(u_seg, dy_seg, bcat, ccat, lam_re, lam_im):
    l = u_seg.shape[0]
    nj = l // N_SEG
    b_hi, b_lo = _split(bcat)
    c_hi, c_lo = _split(ccat)

    def body(u_ref, dy_ref, bh_ref, bl_ref, ch_ref, cl_ref, lr_ref, li_ref,
             du_ref, db_ref, dc_ref, dlr_ref, dli_ref, xs_ref, gs_ref):
        d = pl.program_id(1)
        rev = d == 1
        ar, ai = lr_ref[...], li_ref[...]

        def chunk1(i, _):
            rows = pl.ds(pl.multiple_of(i * SSM_RC, SSM_RC), SSM_RC)
            xs_ref[rows, :] = _dot3w(u_ref[rows, :], bh_ref[...], bl_ref[...], NN)
            gs_ref[rows, :] = _dot3w(dy_ref[rows, :], ch_ref[...], cl_ref[...], NT)
            return 0

        lax.fori_loop(0, l // SSM_RC, chunk1, 0)
        init_r, init_i, _, _ = _scan_segments(xs_ref, ar, ai, rev, nj)
        _, _, acc_r, acc_i = _scan_segments(gs_ref, ar, -ai, jnp.logical_not(rev), nj, prev_ref=xs_ref)
        jb = jnp.where(rev, nj - 1, 0)
        brow = pl.ds(pl.multiple_of(jb * N_SEG, N_SEG), N_SEG)
        gr, gi = gs_ref[brow, pl.ds(0, SSM_ST)], gs_ref[brow, pl.ds(SSM_ST, SSM_ST)]
        acc_r = acc_r + gr * init_r + gi * init_i
        acc_i = acc_i + gi * init_r - gr * init_i
        dlr_ref[...] = jnp.sum(acc_r, axis=0, keepdims=True)
        dli_ref[...] = jnp.sum(acc_i, axis=0, keepdims=True)

        db_ref[...] = jnp.zeros_like(db_ref)
        dc_ref[...] = jnp.zeros_like(dc_ref)

        def chunk2(i, _):
            rows = pl.ds(pl.multiple_of(i * SSM_RC, SSM_RC), SSM_RC)
            g = gs_ref[rows, :]
            dc_ref[...] += _dot3(xs_ref[rows, :], dy_ref[rows, :], TN)
            db_ref[...] += _dot3(u_ref[rows, :], g, TN)
            duv = _dot3w(g, bh_ref[...], bl_ref[...], NT)

            @pl.when(d == 0)
            def _():
                du_ref[rows, :] = duv

            @pl.when(d == 1)
            def _():
                du_ref[rows, :] += duv

            return 0

        lax.fori_loop(0, l // SSM_RC, chunk2, 0)

    act, bmat, cmat, lam = _ssm_specs(l)
    return pl.pallas_call(
        body, grid=(SSM_CB, 2), in_specs=[act, act, bmat, bmat, cmat, cmat, lam, lam],
        out_specs=[act, bmat, cmat, lam, lam],
        out_shape=[jax.ShapeDtypeStruct((l, SSM_WIDTH), F32),
                   jax.ShapeDtypeStruct(bcat.shape, F32), jax.ShapeDtypeStruct(ccat.shape, F32),
                   jax.ShapeDtypeStruct(lam_re.shape, F32), jax.ShapeDtypeStruct(lam_im.shape, F32)],
        scratch_shapes=[pltpu.VMEM((l, 2 * SSM_ST), F32), pltpu.VMEM((l, 2 * SSM_ST), F32)],
        name="ssm_bwd", compiler_params=_params(("parallel", "arbitrary"), vmem_mb=60),
    )(u_seg, dy_seg, b_hi, b_lo, c_hi, c_lo, lam_re, lam_im)


def _glu_fwd(y_ssm, u, d_skip, w_glu):
    l, w = u.shape

    def body(y_ref, u_ref, d_ref, w_ref, pre_ref, s_ref, ys_ref):
        pre = y_ref[...] + d_ref[...] * u_ref[...]
        z = _gelu(pre)
        s = _dg(z.astype(BF16), w_ref[...], NN)
        pre_ref[...] = pre
        s_ref[...] = s
        ys_ref[...] = z * _sigmoid(s)

    row = pl.BlockSpec((TM_EW, w), lambda i: (i, 0))
    out = jax.ShapeDtypeStruct((l, w), F32)
    return pl.pallas_call(
        body, grid=(l // TM_EW,),
        in_specs=[row, row, pl.BlockSpec((1, w), lambda i: (0, 0)), pl.BlockSpec((w, w), lambda i: (0, 0))],
        out_specs=[row, row, row], out_shape=[out, out, out], name="glu_fwd",
        compiler_params=_params(("parallel",)),
    )(y_ssm, u, d_skip, w_glu)


def _glu_bwd(pre, s, dys, u, d_skip, w_glu):
    l, w = u.shape

    def body(pre_ref, s_ref, dys_ref, u_ref, d_ref, w_ref, dpre_ref, z_ref, ds_ref, dd_ref):
        pre, dys = pre_ref[...], dys_ref[...]
        z = _gelu(pre)
        sig = _sigmoid(s_ref[...])
        ds = (dys * z * sig * (1.0 - sig)).astype(BF16)
        dz = dys * sig + _dg(ds, w_ref[...], NT)
        dpre = dz * _gelu_grad(pre)
        dpre_ref[...] = dpre
        z_ref[...] = z.astype(BF16)
        ds_ref[...] = ds

        @pl.when(pl.program_id(0) == 0)
        def _():
            dd_ref[...] = jnp.zeros_like(dd_ref)

        dd_ref[...] += jnp.sum(dpre * u_ref[...], axis=0, keepdims=True)

    row = pl.BlockSpec((TM_EW, w), lambda i: (i, 0))
    vec = pl.BlockSpec((1, w), lambda i: (0, 0))
    return pl.pallas_call(
        body, grid=(l // TM_EW,),
        in_specs=[row, row, row, row, vec, pl.BlockSpec((w, w), lambda i: (0, 0))],
        out_specs=[row, row, row, vec],
        out_shape=[jax.ShapeDtypeStruct((l, w), F32), jax.ShapeDtypeStruct((l, w), BF16),
                   jax.ShapeDtypeStruct((l, w), BF16), jax.ShapeDtypeStruct((1, w), F32)],
        name="glu_bwd", compiler_params=_params(("arbitrary",)),
    )(pre, s, dys, u, d_skip, w_glu)


TM_CV = 256
TC_CV = 256
HALO = SUBLANES


def _conv_specs(l, col0):
    per = TM_CV // HALO
    nh = l // HALO
    off = col0 // TC_CV
    return [
        pl.BlockSpec((HALO, TC_CV), lambda j, i: (jnp.maximum(i * per - 1, 0), j + off)),
        pl.BlockSpec((TM_CV, TC_CV), lambda j, i: (i, j + off)),
        pl.BlockSpec((HALO, TC_CV), lambda j, i: (jnp.minimum((i + 1) * per, nh - 1), j + off)),
    ]


def _ext(prev_ref, mid_ref, next_ref, first, last):
    p = jnp.where(first, 0.0, prev_ref[...])
    n = jnp.where(last, 0.0, next_ref[...])
    return jnp.concatenate([p, mid_ref[...], n], axis=0)


def _shift_dn(t):
    return pltpu.roll(t, 1, 0)


def _shift_up(t):
    return pltpu.roll(t, t.shape[0] - 1, 0)


def _conv3(e, w_ref, b_ref):
    return w_ref[0:1, :] * _shift_dn(e) + w_ref[1:2, :] * e + w_ref[2:3, :] * _shift_up(e) + b_ref[...]


def _convffn_fwd(up_pre, conv_w, conv_b):
    l = up_pre.shape[0]
    ni = l // TM_CV
    wspec = lambda off: pl.BlockSpec((3, TC_CV), lambda j, i: (0, j + off))
    bspec = lambda off: pl.BlockSpec((1, TC_CV), lambda j, i: (0, j + off))
    voff = D_FF // TC_CV

    def body(gp, gm, gn, vp, vm, vn, wg, bg, wv, bv, o_ref):
        i = pl.program_id(1)
        first, last = i == 0, i == ni - 1
        gate = _conv3(_ext(gp, gm, gn, first, last), wg, bg)[HALO:HALO + TM_CV]
        val = _conv3(_ext(vp, vm, vn, first, last), wv, bv)[HALO:HALO + TM_CV]
        o_ref[...] = (gate * _sigmoid(gate) * val).astype(BF16)

    return pl.pallas_call(
        body, grid=(D_FF // TC_CV, ni),
        in_specs=_conv_specs(l, 0) + _conv_specs(l, D_FF) + [wspec(0), bspec(0), wspec(voff), bspec(voff)],
        out_specs=pl.BlockSpec((TM_CV, TC_CV), lambda j, i: (i, j)),
        out_shape=jax.ShapeDtypeStruct((l, D_FF), BF16), name="convffn_fwd",
        compiler_params=_params(("parallel", "parallel")),
    )(up_pre, up_pre, up_pre, up_pre, up_pre, up_pre, conv_w, conv_b, conv_w, conv_b)


def _convffn_bwd(up_pre, dact, conv_w, conv_b):
    l = up_pre.shape[0]
    ni = l // TM_CV
    wspec = lambda off: pl.BlockSpec((3, TC_CV), lambda j, i: (0, j + off))
    bspec = lambda off: pl.BlockSpec((1, TC_CV), lambda j, i: (0, j + off))
    voff = D_FF // TC_CV

    def body(gp, gm, gn, vp, vm, vn, dp, dm, dn, wg, bg, wv, bv, dgate_ref, dval_ref, pg_ref, pv_ref):
        i = pl.program_id(1)
        first, last = i == 0, i == ni - 1
        ge, ve, de = _ext(gp, gm, gn, first, last), _ext(vp, vm, vn, first, last), _ext(dp, dm, dn, first, last)
        gate, val = _conv3(ge, wg, bg), _conv3(ve, wv, bv)
        sig = _sigmoid(gate)
        silu = gate * sig
        dgate = de * val * (sig + silu * (1.0 - sig))
        dval = de * silu
        mid = slice(HALO, HALO + TM_CV)
        rid = lax.broadcasted_iota(jnp.int32, (SUBLANES, TC_CV), 0)

        @pl.when(i == 0)
        def _():
            pg_ref[...] = jnp.zeros_like(pg_ref)
            pv_ref[...] = jnp.zeros_like(pv_ref)

        for dup, e, w_ref, out_ref, p_ref in ((dgate, ge, wg, dgate_ref, pg_ref), (dval, ve, wv, dval_ref, pv_ref)):
            dpre = w_ref[0:1, :] * _shift_up(dup) + w_ref[1:2, :] * dup + w_ref[2:3, :] * _shift_dn(dup)
            out_ref[...] = dpre[mid].astype(BF16)
            dm_ = dup[mid]
            sums = [jnp.sum(dm_ * _shift_dn(e)[mid], axis=0, keepdims=True),
                    jnp.sum(dm_ * e[mid], axis=0, keepdims=True),
                    jnp.sum(dm_ * _shift_up(e)[mid], axis=0, keepdims=True),
                    jnp.sum(dm_, axis=0, keepdims=True)]
            acc = jnp.zeros((SUBLANES, TC_CV), F32)
            for k, sk in enumerate(sums):
                acc = jnp.where(rid == k, sk, acc)
            p_ref[...] += acc

    tile = pl.BlockSpec((TM_CV, TC_CV), lambda j, i: (i, j))
    par = pl.BlockSpec((SUBLANES, TC_CV), lambda j, i: (0, j))
    dgate, dval, pg, pv = pl.pallas_call(
        body, grid=(D_FF // TC_CV, ni),
        in_specs=_conv_specs(l, 0) + _conv_specs(l, D_FF) + _conv_specs(l, 0)
        + [wspec(0), bspec(0), wspec(voff), bspec(voff)],
        out_specs=[tile, tile, par, par],
        out_shape=[jax.ShapeDtypeStruct((l, D_FF), BF16), jax.ShapeDtypeStruct((l, D_FF), BF16),
                   jax.ShapeDtypeStruct((SUBLANES, D_FF), F32), jax.ShapeDtypeStruct((SUBLANES, D_FF), F32)],
        name="convffn_bwd", compiler_params=_params(("parallel", "arbitrary")),
    )(up_pre, up_pre, up_pre, up_pre, up_pre, up_pre, dact, dact, dact, conv_w, conv_b, conv_w, conv_b)
    return jnp.concatenate([dgate, dval], axis=1), jnp.concatenate([pg, pv], axis=1)


def _local_step(x, target, wb, sp):
    l = x.shape[0]
    tabs = _rope_tables(l)
    disc = _ssm_disc(sp["a_re"], sp["a_im"], sp["log_step"], sp["b_re"], sp["b_im"])
    bcat, ccat, lam_re, lam_im = _ssm_pack(*disc, sp["c_re"], sp["c_im"])
    d_skip = sp["d_skip"].reshape(1, SSM_WIDTH)

    h = _rms_fwd(x, sp["norm_mix_g"], "rms_mix")
    proj = _mm_nn(h, wb["w_in"], 512, IN_WIDTH, F32, "mm_in")
    qkv = _rope_fwd(proj, tabs)
    attn, lse = _attn_fwd(qkv, sp["sink"])
    u = proj[:, QKV_WIDTH:]
    u_seg = _to_segments(u)
    y_ssm = _from_segments(_ssm_fwd(u_seg, bcat, ccat, lam_re, lam_im))
    pre, s_glu, ys = _glu_fwd(y_ssm, u, d_skip, wb["w_glu"])
    mixed = _mix_fwd(attn, ys, sp["norm_attn_g"], sp["norm_ssm_g"])
    x1 = _mm_nn(mixed, wb["w_out"], 512, 512, F32, "mm_out", res=x)
    h2 = _rms_fwd(x1, sp["norm_ffn_g"], "rms_ffn")
    up_pre = _mm_nn(h2, wb["w_up"], 512, 512, F32, "mm_up")
    act = _convffn_fwd(up_pre, sp["conv_w"], sp["conv_b"])
    x2 = _mm_nn(act, wb["w_down"], 512, 512, F32, "mm_down", res=x1)
    loss, dx2, dx2b, d_final_g = _final_loss(x2, sp["norm_final_g"].reshape(1, D_MODEL), target)

    g = {"norm_final_g": d_final_g.reshape(D_MODEL)}
    dact = _mm_nt(dx2b, wb["w_down"], 512, D_FF // 2, F32, "mm_down_dx")
    g["w_down"] = _mm_tn(act, dx2b, 256, 512, "mm_down_dw")
    dup_pre, conv_par = _convffn_bwd(up_pre, dact, sp["conv_w"], sp["conv_b"])
    g["conv_w"], g["conv_b"] = conv_par[0:3], conv_par[3:4]
    g["w_up"] = _mm_tn(h2, dup_pre, 512, 512, "mm_up_dw")
    dh2 = _mm_nt(dup_pre, wb["w_up"], 256, 512, F32, "mm_up_dx")
    dx1, dx1b, g["norm_ffn_g"] = _rms_bwd(x1, sp["norm_ffn_g"], dh2, dx2, "rms_ffn_bwd")
    dmixed = _mm_nt(dx1b, wb["w_out"], 512, 512, F32, "mm_out_dx")
    g["w_out"] = _mm_tn(mixed, dx1b, 512, 512, "mm_out_dw")
    dattn, dys, g["norm_attn_g"], g["norm_ssm_g"] = _mix_bwd(attn, ys, sp["norm_attn_g"], sp["norm_ssm_g"], dmixed)
    dpre, zb, dsb, dd = _glu_bwd(pre, s_glu, dys, u, d_skip, wb["w_glu"])
    g["d_skip"] = dd.reshape(N_SSM_GROUPS, SSM_GROUP)
    g["w_glu"] = _mm_tn(zb, dsb, 512, 512, "mm_glu_dw")
    du_seg, dbcat, dccat, dlam_re, dlam_im = _ssm_bwd(u_seg, _to_segments(dpre), bcat, ccat, lam_re, lam_im)
    dlb_re, dlb_im, dbb_re, dbb_im, g["c_re"], g["c_im"] = _ssm_unpack(dbcat, dccat, dlam_re, dlam_im)
    _, disc_vjp = jax.vjp(_ssm_disc, sp["a_re"], sp["a_im"], sp["log_step"], sp["b_re"], sp["b_im"])
    g["a_re"], g["a_im"], g["log_step"], g["b_re"], g["b_im"] = disc_vjp((dlb_re, dlb_im, dbb_re, dbb_im))
    du = _from_segments(du_seg) + dpre * d_skip
    dqkv, dsink = _attn_bwd(qkv, attn, dattn, lse, sp["sink"])
    g["sink"] = dsink
    dproj = _rope_bwd(dqkv, du, tabs)
    g["w_in"] = _mm_tn(h, dproj, 512, IN_WIDTH, "mm_in_dw")
    dh = _mm_nt(dproj, wb["w_in"], 512, 512, F32, "mm_in_dx")
    grad_x, _, g["norm_mix_g"] = _rms_bwd(x, sp["norm_mix_g"], dh, dx1, "rms_mix_bwd")
    return loss, grad_x, g


MESH = pl.DeviceIdType.MESH
ANY = pl.BlockSpec(memory_space=pl.ANY)


def _place():
    x, y, c = lax.axis_index("x"), lax.axis_index("y"), lax.axis_index("c")
    chips = [(1 - x, y), (x, 1 - y), (1 - x, 1 - y)]
    return x, y, c, chips


def _chip_index(px, py):
    return 2 * px + py


CHUNK_BYTES = 256 * 1024
MAX_CHUNKS = 16


def _row_chunks(rows, row_bytes, align):
    n = max(1, min(MAX_CHUNKS, (rows * row_bytes) // CHUNK_BYTES))
    per = -(-rows // n)
    per = -(-per // align) * align
    return [(r0, min(per, rows - r0)) for r0 in range(0, rows, per)]


def _align_of(dtype):
    return SUBLANES * 4 // jnp.dtype(dtype).itemsize


def _remote(src, dst, send_sem, recv_sem, to):
    return pltpu.make_async_remote_copy(src_ref=src, dst_ref=dst, send_sem=send_sem, recv_sem=recv_sem,
                                        device_id=to, device_id_type=MESH)


def _cast_bf16(w, name):
    r, c = w.shape
    tr = r if r <= 512 else r // 2

    def body(w_ref, o_ref):
        o_ref[...] = w_ref[...].astype(BF16)

    spec = pl.BlockSpec((tr, c), lambda i: (i, 0))
    return pl.pallas_call(body, grid=(r // tr,), in_specs=[spec], out_specs=spec,
                          out_shape=jax.ShapeDtypeStruct((r, c), BF16), name=name,
                          compiler_params=_params(("parallel",)))(w)


def _gather_weights(shards):
    nw = len(shards)

    def body(*refs):
        w_refs, o_refs = refs[:nw], refs[nw:2 * nw]
        send_sems, recv_sems, local_sems = refs[2 * nw:]
        x, y, c, chips = _place()
        mine = _chip_index(x, y)
        sibling = (x, y, 1 - c)

        def rows_of(ref, chip, r0, nr):
            return ref.at[chip, pl.ds(r0, nr), :]

        def copy(wi, k, src, dst, to):
            return _remote(src, dst, send_sems.at[wi, k], recv_sems.at[wi, k], to)

        geo = []
        for wi in range(nw):
            rows, cols = w_refs[wi].shape
            row_bytes = cols * jnp.dtype(w_refs[wi].dtype).itemsize
            align = _align_of(w_refs[wi].dtype)
            geo.append((rows // 2, _row_chunks(rows // 2, row_bytes, align), _row_chunks(rows, row_bytes, align)))

        for wi in range(nw):
            hr, half_chunks, all_chunks = geo[wi]
            for r0, nr in all_chunks:
                pltpu.make_async_copy(w_refs[wi].at[pl.ds(r0, nr), :], rows_of(o_refs[wi], mine, r0, nr),
                                      local_sems.at[wi]).start()
            for j, chip in enumerate(chips):
                for r0, nr in half_chunks:
                    copy(wi, j, w_refs[wi].at[pl.ds(c * hr + r0, nr), :],
                         rows_of(o_refs[wi], mine, c * hr + r0, nr), (*chip, c)).start()
        for wi in range(nw):
            hr, half_chunks, _ = geo[wi]
            for j, chip in enumerate(chips):
                got = rows_of(o_refs[wi], _chip_index(*chip), c * hr, hr)
                copy(wi, j, got, got, (*chip, c)).wait_recv()
                for r0, nr in half_chunks:
                    piece = rows_of(o_refs[wi], _chip_index(*chip), c * hr + r0, nr)
                    copy(wi, 3 + j, piece, piece, sibling).start()
        for wi in range(nw):
            hr = geo[wi][0]
            for j, chip in enumerate(chips):
                got = rows_of(o_refs[wi], _chip_index(*chip), (1 - c) * hr, hr)
                copy(wi, 3 + j, got, got, sibling).wait_recv()
        for wi in range(nw):
            hr = geo[wi][0]
            sent = rows_of(o_refs[wi], mine, c * hr, hr)
            for k in range(6):
                copy(wi, k, sent, sent, sibling).wait_send()
            pltpu.make_async_copy(w_refs[wi], o_refs[wi].at[mine], local_sems.at[wi]).wait()

    return pl.pallas_call(
        body, in_specs=[ANY] * nw, out_specs=[ANY] * nw,
        out_shape=[jax.ShapeDtypeStruct((4, *s.shape), s.dtype) for s in shards],
        scratch_shapes=[pltpu.SemaphoreType.DMA((nw, 6)), pltpu.SemaphoreType.DMA((nw, 6)),
                        pltpu.SemaphoreType.DMA((nw,))],
        name="gather_weights",
    )(*shards)


def _pair_exchange(grads):
    na = len(grads)

    def body(*refs):
        g_refs, o_refs = refs[:na], refs[na:2 * na]
        send_sems, recv_sems, local_sems = refs[2 * na:]
        x, y, c, _ = _place()
        sibling = (x, y, 1 - c)
        for ai in range(na):
            _, rows, cols = g_refs[ai].shape
            hr = rows // 2
            for k in range(4):
                for r0, nr in _row_chunks(hr, cols * 4, SUBLANES):
                    pltpu.make_async_copy(g_refs[ai].at[k, pl.ds(c * hr + r0, nr), :],
                                          o_refs[ai].at[0, k, pl.ds(r0, nr), :], local_sems.at[ai]).start()
                    _remote(g_refs[ai].at[k, pl.ds((1 - c) * hr + r0, nr), :],
                            o_refs[ai].at[1, k, pl.ds(r0, nr), :], send_sems.at[ai], recv_sems.at[ai],
                            sibling).start()
        for ai in range(na):
            pltpu.make_async_copy(o_refs[ai].at[1], o_refs[ai].at[0], local_sems.at[ai]).wait()
            _remote(o_refs[ai].at[0], o_refs[ai].at[1], send_sems.at[ai], recv_sems.at[ai], sibling).wait()

    return pl.pallas_call(
        body, in_specs=[ANY] * na, out_specs=[ANY] * na,
        out_shape=[jax.ShapeDtypeStruct((2, 4, g.shape[1] // 2, g.shape[2]), F32) for g in grads],
        scratch_shapes=[pltpu.SemaphoreType.DMA((na,)), pltpu.SemaphoreType.DMA((na,)),
                        pltpu.SemaphoreType.DMA((na,))],
        name="pair_exchange",
    )(*grads)


def _chip_exchange(sums):
    na = len(sums)

    def body(*refs):
        s_refs, o_refs = refs[:na], refs[na:2 * na]
        send_sems, recv_sems, local_sems = refs[2 * na:]
        x, y, c, chips = _place()
        mine = _chip_index(x, y)
        for ai in range(na):
            _, rows, cols = s_refs[ai].shape
            for r0, nr in _row_chunks(rows, cols * 4, SUBLANES):
                pltpu.make_async_copy(s_refs[ai].at[mine, pl.ds(r0, nr), :], o_refs[ai].at[mine, pl.ds(r0, nr), :],
                                      local_sems.at[ai]).start()
                for j, chip in enumerate(chips):
                    _remote(s_refs[ai].at[_chip_index(*chip), pl.ds(r0, nr), :],
                            o_refs[ai].at[mine, pl.ds(r0, nr), :], send_sems.at[ai, j], recv_sems.at[ai, j],
                            (*chip, c)).start()
        for ai in range(na):
            pltpu.make_async_copy(s_refs[ai].at[mine], o_refs[ai].at[mine], local_sems.at[ai]).wait()
            for j, chip in enumerate(chips):
                _remote(s_refs[ai].at[mine], o_refs[ai].at[mine], send_sems.at[ai, j], recv_sems.at[ai, j],
                        (*chip, c)).wait()

    return pl.pallas_call(
        body, in_specs=[ANY] * na, out_specs=[ANY] * na,
        out_shape=[jax.ShapeDtypeStruct(s.shape, F32) for s in sums],
        scratch_shapes=[pltpu.SemaphoreType.DMA((na, 3)), pltpu.SemaphoreType.DMA((na, 3)),
                        pltpu.SemaphoreType.DMA((na,))],
        name="chip_exchange",
    )(*sums)


def _final_exchange(halves, small):
    nh = len(halves)

    def body(*refs):
        h_refs, s_ref = refs[:nh], refs[nh]
        o_refs, so_ref = refs[nh + 1:2 * nh + 1], refs[2 * nh + 1]
        send_sems, recv_sems, local_sems, ssend_sems, srecv_sems = refs[2 * nh + 2:]
        x, y, c, _ = _place()
        me = 4 * x + 2 * y + c
        sibling = (x, y, 1 - c)
        for hi in range(nh):
            hr, cols = h_refs[hi].shape
            for r0, nr in _row_chunks(hr, cols * 4, SUBLANES):
                src = h_refs[hi].at[pl.ds(r0, nr), :]
                dst = o_refs[hi].at[pl.ds(c * hr + r0, nr), :]
                pltpu.make_async_copy(src, dst, local_sems.at[hi]).start()
                _remote(src, dst, send_sems.at[hi], recv_sems.at[hi], sibling).start()
        small_cps = [pltpu.make_async_copy(s_ref, so_ref.at[me], local_sems.at[nh])]
        for r in range(1, 8):
            fx, fy, fc = (r >> 2) & 1, (r >> 1) & 1, r & 1
            peer = (1 - x if fx else x, 1 - y if fy else y, 1 - c if fc else c)
            small_cps.append(_remote(s_ref, so_ref.at[me], ssend_sems.at[r - 1], srecv_sems.at[r - 1], peer))
        for cp in small_cps:
            cp.start()
        for hi in range(nh):
            hr = h_refs[hi].shape[0]
            rows = o_refs[hi].at[pl.ds(c * hr, hr), :]
            pltpu.make_async_copy(h_refs[hi], rows, local_sems.at[hi]).wait()
            _remote(h_refs[hi], rows, send_sems.at[hi], recv_sems.at[hi], sibling).wait()
        for cp in small_cps:
            cp.wait()

    return pl.pallas_call(
        body, in_specs=[ANY] * (nh + 1), out_specs=[ANY] * (nh + 1),
        out_shape=[jax.ShapeDtypeStruct((2 * h.shape[0], h.shape[1]), F32) for h in halves]
        + [jax.ShapeDtypeStruct((8, *small.shape), F32)],
        scratch_shapes=[pltpu.SemaphoreType.DMA((nh,)), pltpu.SemaphoreType.DMA((nh,)),
                        pltpu.SemaphoreType.DMA((nh + 1,)), pltpu.SemaphoreType.DMA((7,)),
                        pltpu.SemaphoreType.DMA((7,))],
        name="final_exchange",
    )(*halves, small)


def _sum_leading(t, name):
    n = t.shape[0]
    shape = t.shape[1:]
    t3 = t.reshape(n, -1, shape[-1])
    rows, cols = t3.shape[1], t3.shape[2]
    tr = rows
    for cand in (512, 448, 384, 352, 320, 256, 192, 176, 160, 128, 64, 40, 32, 16, 8):
        if rows % cand == 0:
            tr = cand
            break

    def body(t_ref, o_ref):
        acc = t_ref[0]
        for k in range(1, n):
            acc = acc + t_ref[k]
        o_ref[...] = acc

    out = pl.pallas_call(
        body, grid=(rows // tr,), in_specs=[pl.BlockSpec((n, tr, cols), lambda i: (0, i, 0))],
        out_specs=pl.BlockSpec((tr, cols), lambda i: (i, 0)),
        out_shape=jax.ShapeDtypeStruct((rows, cols), F32), name=name,
        compiler_params=_params(("parallel",)),
    )(t3)
    return out.reshape(shape)


def _adamw(w, g, m, v, name):
    shape = w.shape
    n = w.size
    if w.ndim >= 2 and shape[-1] >= LANES:
        two_d = (n // shape[-1], shape[-1])
    elif n % LANES == 0:
        two_d = (n // LANES, LANES)
    else:
        two_d = (1, n)
    r, c = two_d
    tr = r
    for cand in (512, 256, 176, 128, 64):
        if r > cand and r % cand == 0 and cand * c <= 256 * 1024:
            tr = cand
            break
    c1 = 1.0 - ADAM_B1 ** ADAM_STEP
    c2 = 1.0 - ADAM_B2 ** ADAM_STEP

    def body(w_ref, g_ref, m_ref, v_ref, d_ref, nm_ref, nv_ref):
        gv = g_ref[...]
        nm = ADAM_B1 * m_ref[...] + (1.0 - ADAM_B1) * gv
        nv = ADAM_B2 * v_ref[...] + (1.0 - ADAM_B2) * (gv * gv)
        d_ref[...] = -ADAM_LR * ((nm / c1) / (jnp.sqrt(nv / c2) + ADAM_EPS) + ADAM_WD * w_ref[...])
        nm_ref[...] = nm
        nv_ref[...] = nv

    spec = pl.BlockSpec((tr, c), lambda i: (i, 0))
    out = jax.ShapeDtypeStruct((r, c), F32)
    d, nm, nv = pl.pallas_call(
        body, grid=(r // tr,), in_specs=[spec] * 4, out_specs=[spec] * 3, out_shape=[out] * 3, name=name,
        compiler_params=_params(("parallel",)),
    )(w.reshape(two_d), g.reshape(two_d), m.reshape(two_d), v.reshape(two_d))
    return d.reshape(shape), nm.reshape(shape), nv.reshape(shape)


BIG = ("w_in", "w_glu", "w_out", "w_up", "w_down")
WEIGHTS = ("norm_mix_g", "w_in", "a_re", "a_im", "log_step", "b_re", "b_im", "c_re", "c_im", "d_skip", "w_glu",
           "sink", "norm_attn_g", "norm_ssm_g", "w_out", "norm_ffn_g", "w_up", "conv_w", "conv_b", "w_down",
           "norm_final_g")
SMALL = ("norm_mix_g", "a_re", "a_im", "log_step", "b_re", "b_im", "c_re", "c_im", "d_skip", "sink",
         "norm_attn_g", "norm_ssm_g", "norm_ffn_g", "conv_w", "conv_b", "norm_final_g")
SMALL_ROWS = 40
N_DEV = 8


def _full_matrices(gathered):
    w_in, w_glu, w_out, w_up, w_down = gathered
    cols = lambda t: t.transpose(1, 0, 2).reshape(t.shape[1], 4 * t.shape[2])
    rows = lambda t: t.reshape(4 * t.shape[1], t.shape[2])
    return {"w_in": cols(w_in), "w_glu": rows(w_glu), "w_out": rows(w_out), "w_up": cols(w_up),
            "w_down": rows(w_down)}


def _by_owner(name, g):
    if name in ("w_in", "w_up"):
        return g.reshape(g.shape[0], 4, g.shape[1] // 4).transpose(1, 0, 2)
    return g.reshape(4, g.shape[0] // 4, g.shape[1])


def kernel(x, norm_mix_g, w_in, a_re, a_im, log_step, b_re, b_im, c_re, c_im, d_skip, w_glu, sink, norm_attn_g, norm_ssm_g, w_out, norm_ffn_g, w_up, conv_w, conv_b, w_down, norm_final_g, loss_target, m_norm_mix_g, m_w_in, m_a_re, m_a_im, m_log_step, m_b_re, m_b_im, m_c_re, m_c_im, m_d_skip, m_w_glu, m_sink, m_norm_attn_g, m_norm_ssm_g, m_w_out, m_norm_ffn_g, m_w_up, m_conv_w, m_conv_b, m_w_down, m_norm_final_g, v_norm_mix_g, v_w_in, v_a_re, v_a_im, v_log_step, v_b_re, v_b_im, v_c_re, v_c_im, v_d_skip, v_w_glu, v_sink, v_norm_attn_g, v_norm_ssm_g, v_w_out, v_norm_ffn_g, v_w_up, v_conv_w, v_conv_b, v_w_down, v_norm_final_g):
    given = dict(locals())
    w = {n: given[n] for n in WEIGHTS}
    m = {n: given["m_" + n] for n in WEIGHTS}
    v = {n: given["v_" + n] for n in WEIGHTS}
    xy = 2 * lax.axis_index("x") + lax.axis_index("y")

    shards = [_cast_bf16(w[n][0], "cast_" + n) for n in BIG]
    conv_rows = jnp.pad(w["conv_w"][0], ((0, 2 * SUBLANES - 3), (0, 0)))
    *gathered, conv_all = _gather_weights(shards + [conv_rows])
    wb = _full_matrices(gathered)

    sp = {n: w[n][0] for n in ("a_re", "a_im", "log_step", "b_re", "b_im", "c_re", "c_im", "d_skip",
                               "norm_mix_g", "norm_attn_g", "norm_ssm_g", "norm_ffn_g", "sink", "conv_b")}
    for n in ("norm_mix_g", "norm_attn_g", "norm_ssm_g", "norm_ffn_g", "sink", "conv_b"):
        sp[n] = sp[n].reshape(1, -1)
    sp["conv_w"] = conv_all[:, :3].transpose(1, 0, 2).reshape(3, 2 * D_FF)
    sp["norm_final_g"] = w["norm_final_g"]
    loss, grad_x, g = _local_step(x[0], loss_target[0], wb, sp)

    flat = jnp.concatenate([g[n].reshape(-1) for n in SMALL])
    pad = N_DEV * SMALL_ROWS * D_MODEL - flat.shape[0]
    small = jnp.concatenate([flat, jnp.zeros((pad,), F32)]).reshape(4, 2 * SMALL_ROWS, D_MODEL)
    by_owner = [_by_owner(n, g[n]) for n in BIG] + [small]
    pairs = _pair_exchange(by_owner)
    chip_sums = [_sum_leading(p, "pair_sum_%d" % i) for i, p in enumerate(pairs)]
    landed = _chip_exchange(chip_sums)
    halves = [_sum_leading(t, "chip_sum_%d" % i) for i, t in enumerate(landed)]
    *shard_grads, small_all = _final_exchange(halves[:-1], halves[-1])
    grads = dict(zip(BIG, shard_grads))
    flat = small_all.reshape(-1)
    off = 0
    for n in SMALL:
        shape = (3, 4 * w[n].shape[-1]) if n == "conv_w" else w[n].shape[1:] if n != "norm_final_g" else w[n].shape
        size = math.prod(shape)
        grads[n] = flat[off:off + size].reshape(shape)
        off += size
    cw = w["conv_w"].shape[-1]
    grads["conv_w"] = lax.dynamic_slice_in_dim(grads["conv_w"], xy * cw, cw, axis=1)

    outs_g, outs_d, outs_m, outs_v = [], [], [], []
    for n in WEIGHTS:
        gn = grads[n].reshape(w[n].shape)
        dn, mn, vn = _adamw(w[n], gn, m[n], v[n], "adamw_" + n)
        outs_g.append(gn)
        outs_d.append(dn)
        outs_m.append(mn)
        outs_v.append(vn)
    loss = lax.psum(loss[0, 0], ("x", "y", "c"))
    return (loss, grad_x[None], *outs_g, *outs_d, *outs_m, *outs_v)
```

```python
import functools
import math

import jax
import jax.numpy as jnp
from jax import lax
from jax.experimental import pallas as pl
from jax.experimental.pallas import tpu as pltpu

F32 = jnp.float32
BF16 = jnp.bfloat16

D_MODEL = 1024
N_Q_HEADS = 8
N_KV_HEADS = 2
HEAD_DIM = 64
ATTN_WIDTH = 512
KV_WIDTH = 128
QKV_WIDTH = ATTN_WIDTH + 2 * KV_WIDTH
WINDOW = 128
BLOCK = 128
ROPE_DIM = 16
ROPE_THETA = 500000.0
SSM_WIDTH = 512
SSM_GROUP = 16
N_SSM_GROUPS = 32
SSM_STATE = 64
IN_WIDTH = 1280
D_FF = 2816
EPS = 1e-6
ADAM_LR = 0.001
ADAM_B1 = 0.9
ADAM_B2 = 0.999
ADAM_EPS = 1e-08
ADAM_WD = 0.01
ADAM_STEP = 10

VMEM_BYTES_V7X = 64 * 1024 * 1024
SUBLANES = 8
LANES = 128
SSM_CB = 4
SSM_CH = 128
SSM_ST = 512
N_SEG = SUBLANES

NN = (((1,), (0,)), ((), ()))
NT = (((1,), (1,)), ((), ()))
TN = (((0,), (0,)), ((), ()))


def _params(sem=None, vmem_mb=48):
    return pltpu.CompilerParams(dimension_semantics=sem, vmem_limit_bytes=vmem_mb * 1024 * 1024)


def _dg(a, b, dims):
    return lax.dot_general(a, b, dims, preferred_element_type=F32)


def _split(a):
    hi = a.astype(BF16)
    lo = (a - hi.astype(F32)).astype(BF16)
    return hi, lo


def _dot3w(a, b_hi, b_lo, dims):
    a_hi, a_lo = _split(a)
    return _dg(a_hi, b_hi, dims) + _dg(a_hi, b_lo, dims) + _dg(a_lo, b_hi, dims)


def _dot3(a, b, dims):
    b_hi, b_lo = _split(b)
    return _dot3w(a, b_hi, b_lo, dims)


def _sigmoid(x):
    return 1.0 / (1.0 + jnp.exp(-x))


_SQRT_HALF = 0.7071067811865476
_INV_SQRT_2PI = 0.3989422804014327


def _gelu(x):
    return 0.5 * x * (1.0 + lax.erf(x * _SQRT_HALF))


def _gelu_grad(x):
    return 0.5 * (1.0 + lax.erf(x * _SQRT_HALF)) + x * (_INV_SQRT_2PI * jnp.exp(-0.5 * x * x))


def _mm_nn(a, b, tm, tn, out_dtype, name, res=None):
    m, k = a.shape
    n = b.shape[1]

    def body(*refs):
        if res is None:
            a_ref, b_ref, o_ref = refs
            o_ref[...] = _dg(a_ref[...], b_ref[...], NN).astype(out_dtype)
        else:
            a_ref, b_ref, r_ref, o_ref = refs
            o_ref[...] = (r_ref[...] + _dg(a_ref[...], b_ref[...], NN)).astype(out_dtype)

    in_specs = [pl.BlockSpec((tm, k), lambda i, j: (i, 0)), pl.BlockSpec((k, tn), lambda i, j: (0, j))]
    args = [a, b]
    if res is not None:
        in_specs.append(pl.BlockSpec((tm, tn), lambda i, j: (i, j)))
        args.append(res)
    return pl.pallas_call(
        body, grid=(m // tm, n // tn), in_specs=in_specs,
        out_specs=pl.BlockSpec((tm, tn), lambda i, j: (i, j)),
        out_shape=jax.ShapeDtypeStruct((m, n), out_dtype), name=name,
        compiler_params=_params(("parallel", "parallel")),
    )(*args)


def _mm_nt(a, b, tm, tn, out_dtype, name):
    m, k = a.shape
    n = b.shape[0]

    def body(a_ref, b_ref, o_ref):
        o_ref[...] = _dg(a_ref[...], b_ref[...], NT).astype(out_dtype)

    return pl.pallas_call(
        body, grid=(m // tm, n // tn),
        in_specs=[pl.BlockSpec((tm, k), lambda i, j: (i, 0)), pl.BlockSpec((tn, k), lambda i, j: (j, 0))],
        out_specs=pl.BlockSpec((tm, tn), lambda i, j: (i, j)),
        out_shape=jax.ShapeDtypeStruct((m, n), out_dtype), name=name,
        compiler_params=_params(("parallel", "parallel")),
    )(a, b)


def _mm_tn(a, b, tm, tn, name):
    k, m = a.shape
    n = b.shape[1]

    def body(a_ref, b_ref, o_ref):
        o_ref[...] = _dg(a_ref[...], b_ref[...], TN)

    return pl.pallas_call(
        body, grid=(m // tm, n // tn),
        in_specs=[pl.BlockSpec((k, tm), lambda i, j: (0, i)), pl.BlockSpec((k, tn), lambda i, j: (0, j))],
        out_specs=pl.BlockSpec((tm, tn), lambda i, j: (i, j)),
        out_shape=jax.ShapeDtypeStruct((m, n), F32), name=name,
        compiler_params=_params(("parallel", "parallel")),
    )(a, b)


TM_EW = 256


def _rms_fwd(x, g, name):
    l, d = x.shape

    def body(x_ref, g_ref, h_ref):
        xv = x_ref[...]
        r = lax.rsqrt(jnp.mean(xv * xv, axis=-1, keepdims=True) + EPS)
        h_ref[...] = (xv * r * g_ref[...]).astype(BF16)

    return pl.pallas_call(
        body, grid=(l // TM_EW,),
        in_specs=[pl.BlockSpec((TM_EW, d), lambda i: (i, 0)), pl.BlockSpec((1, d), lambda i: (0, 0))],
        out_specs=pl.BlockSpec((TM_EW, d), lambda i: (i, 0)),
        out_shape=jax.ShapeDtypeStruct((l, d), BF16), name=name,
        compiler_params=_params(("parallel",)),
    )(x, g)


def _rms_bwd_vals(xv, gv, dy):
    r = lax.rsqrt(jnp.mean(xv * xv, axis=-1, keepdims=True) + EPS)
    xh = xv * r
    dxh = dy * gv
    dx = r * (dxh - xh * jnp.mean(dxh * xh, axis=-1, keepdims=True))
    return dx, dy * xh


def _rms_bwd(x, g, dy, res, name):
    l, d = x.shape

    def body(x_ref, g_ref, dy_ref, res_ref, dx_ref, dxb_ref, dg_ref):
        dx, dgr = _rms_bwd_vals(x_ref[...], g_ref[...], dy_ref[...])
        dx = dx + res_ref[...]
        dx_ref[...] = dx
        dxb_ref[...] = dx.astype(BF16)

        @pl.when(pl.program_id(0) == 0)
        def _():
            dg_ref[...] = jnp.zeros_like(dg_ref)

        dg_ref[...] += jnp.sum(dgr, axis=0, keepdims=True)

    row = pl.BlockSpec((TM_EW, d), lambda i: (i, 0))
    vec = pl.BlockSpec((1, d), lambda i: (0, 0))
    return pl.pallas_call(
        body, grid=(l // TM_EW,), in_specs=[row, vec, row, row], out_specs=[row, row, vec],
        out_shape=[jax.ShapeDtypeStruct((l, d), F32), jax.ShapeDtypeStruct((l, d), BF16),
                   jax.ShapeDtypeStruct((1, d), F32)],
        name=name, compiler_params=_params(("arbitrary",)),
    )(x, g, dy, res)


def _final_loss(x2, g, target):
    l, d = x2.shape

    def body(x_ref, g_ref, t_ref, loss_ref, dx_ref, dxb_ref, dg_ref):
        xv = x_ref[...]
        gv = g_ref[...]
        r = lax.rsqrt(jnp.mean(xv * xv, axis=-1, keepdims=True) + EPS)
        xh = xv * r
        e = xh * gv - t_ref[...]
        part = jnp.sum(jnp.sum(e * e, axis=1, keepdims=True), axis=0, keepdims=True) * (0.5 / d)
        dy = e * (1.0 / d)
        dxh = dy * gv
        dx = r * (dxh - xh * jnp.mean(dxh * xh, axis=-1, keepdims=True))
        dx_ref[...] = dx
        dxb_ref[...] = dx.astype(BF16)

        @pl.when(pl.program_id(0) == 0)
        def _():
            dg_ref[...] = jnp.zeros_like(dg_ref)
            loss_ref[...] = jnp.zeros_like(loss_ref)

        dg_ref[...] += jnp.sum(dy * xh, axis=0, keepdims=True)
        loss_ref[...] += part

    row = pl.BlockSpec((TM_EW, d), lambda i: (i, 0))
    vec = pl.BlockSpec((1, d), lambda i: (0, 0))
    one = pl.BlockSpec((1, 1), lambda i: (0, 0))
    return pl.pallas_call(
        body, grid=(l // TM_EW,), in_specs=[row, vec, row], out_specs=[one, row, row, vec],
        out_shape=[jax.ShapeDtypeStruct((1, 1), F32), jax.ShapeDtypeStruct((l, d), F32),
                   jax.ShapeDtypeStruct((l, d), BF16), jax.ShapeDtypeStruct((1, d), F32)],
        name="final_loss", compiler_params=_params(("arbitrary",)),
    )(x2, g, target)


def _mix_fwd(attn, ys, g_attn, g_ssm):
    l, w = attn.shape

    def body(a_ref, y_ref, ga_ref, gs_ref, o_ref):
        for src, gr, off in ((a_ref, ga_ref, 0), (y_ref, gs_ref, w)):
            xv = src[...]
            r = lax.rsqrt(jnp.mean(xv * xv, axis=-1, keepdims=True) + EPS)
            o_ref[:, off:off + w] = (xv * r * gr[...]).astype(BF16)

    row = pl.BlockSpec((TM_EW, w), lambda i: (i, 0))
    vec = pl.BlockSpec((1, w), lambda i: (0, 0))
    return pl.pallas_call(
        body, grid=(l // TM_EW,), in_specs=[row, row, vec, vec],
        out_specs=pl.BlockSpec((TM_EW, 2 * w), lambda i: (i, 0)),
        out_shape=jax.ShapeDtypeStruct((l, 2 * w), BF16), name="mix_fwd",
        compiler_params=_params(("parallel",)),
    )(attn, ys, g_attn, g_ssm)


def _mix_bwd(attn, ys, g_attn, g_ssm, dmixed):
    l, w = attn.shape

    def body(a_ref, y_ref, ga_ref, gs_ref, dm_ref, da_ref, dy_ref, dga_ref, dgs_ref):
        @pl.when(pl.program_id(0) == 0)
        def _():
            dga_ref[...] = jnp.zeros_like(dga_ref)
            dgs_ref[...] = jnp.zeros_like(dgs_ref)

        for src, gr, off, dst, dgr in ((a_ref, ga_ref, 0, da_ref, dga_ref), (y_ref, gs_ref, w, dy_ref, dgs_ref)):
            dx, dg_rows = _rms_bwd_vals(src[...], gr[...], dm_ref[:, off:off + w])
            dst[...] = dx
            dgr[...] += jnp.sum(dg_rows, axis=0, keepdims=True)

    row = pl.BlockSpec((TM_EW, w), lambda i: (i, 0))
    vec = pl.BlockSpec((1, w), lambda i: (0, 0))
    return pl.pallas_call(
        body, grid=(l // TM_EW,),
        in_specs=[row, row, vec, vec, pl.BlockSpec((TM_EW, 2 * w), lambda i: (i, 0))],
        out_specs=[row, row, vec, vec],
        out_shape=[jax.ShapeDtypeStruct((l, w), F32), jax.ShapeDtypeStruct((l, w), F32),
                   jax.ShapeDtypeStruct((1, w), F32), jax.ShapeDtypeStruct((1, w), F32)],
        name="mix_bwd", compiler_params=_params(("arbitrary",)),
    )(attn, ys, g_attn, g_ssm, dmixed)


def _rope_tables(l):
    half = ROPE_DIM // 2
    inv_freq = jnp.power(ROPE_THETA, -jnp.arange(half, dtype=F32) / half)
    ang = jnp.arange(l, dtype=F32)[:, None] * inv_freq[None, :]
    cos, sin = jnp.cos(ang), jnp.sin(ang)
    ones = jnp.ones((l, HEAD_DIM - ROPE_DIM), F32)
    zeros = jnp.zeros((l, HEAD_DIM - ROPE_DIM), F32)
    zh = jnp.zeros((l, half), F32)
    c = jnp.concatenate([cos, cos, ones], axis=1)
    s_lo = jnp.concatenate([-sin, zh, zeros], axis=1)
    s_hi = jnp.concatenate([zh, sin, zeros], axis=1)
    return tuple(jnp.tile(t, (1, LANES // HEAD_DIM)) for t in (c, s_lo, s_hi))


def _rope_fwd(proj, tabs):
    l = proj.shape[0]
    nq = ATTN_WIDTH // LANES

    def body(p_ref, c_ref, lo_ref, hi_ref, o_ref):
        c, lo, hi = c_ref[...], lo_ref[...], hi_ref[...]
        for blk in range(nq + 1):
            t = p_ref[:, blk * LANES:(blk + 1) * LANES]
            rot = t * c + pltpu.roll(t, LANES - 8, 1) * lo + pltpu.roll(t, 8, 1) * hi
            o_ref[:, blk * LANES:(blk + 1) * LANES] = rot.astype(BF16)
        o_ref[:, (nq + 1) * LANES:] = p_ref[:, (nq + 1) * LANES:].astype(BF16)

    tab = pl.BlockSpec((TM_EW, LANES), lambda i: (i, 0))
    return pl.pallas_call(
        body, grid=(l // TM_EW,),
        in_specs=[pl.BlockSpec((TM_EW, QKV_WIDTH), lambda i: (i, 0)), tab, tab, tab],
        out_specs=pl.BlockSpec((TM_EW, QKV_WIDTH), lambda i: (i, 0)),
        out_shape=jax.ShapeDtypeStruct((l, QKV_WIDTH), BF16), name="rope_fwd",
        compiler_params=_params(("parallel",)),
    )(proj, *tabs)


def _rope_bwd(dqkv, du, tabs):
    l = dqkv.shape[0]
    nq = ATTN_WIDTH // LANES

    def body(d_ref, du_ref, c_ref, lo_ref, hi_ref, o_ref):
        c, lo, hi = c_ref[...], lo_ref[...], hi_ref[...]
        for blk in range(nq + 1):
            t = d_ref[:, blk * LANES:(blk + 1) * LANES]
            g = t * c + pltpu.roll(t * lo, 8, 1) + pltpu.roll(t * hi, LANES - 8, 1)
            o_ref[:, blk * LANES:(blk + 1) * LANES] = g.astype(BF16)
        o_ref[:, (nq + 1) * LANES:QKV_WIDTH] = d_ref[:, (nq + 1) * LANES:].astype(BF16)
        o_ref[:, QKV_WIDTH:] = du_ref[...].astype(BF16)

    tab = pl.BlockSpec((TM_EW, LANES), lambda i: (i, 0))
    return pl.pallas_call(
        body, grid=(l // TM_EW,),
        in_specs=[pl.BlockSpec((TM_EW, QKV_WIDTH), lambda i: (i, 0)),
                  pl.BlockSpec((TM_EW, SSM_WIDTH), lambda i: (i, 0)), tab, tab, tab],
        out_specs=pl.BlockSpec((TM_EW, IN_WIDTH), lambda i: (i, 0)),
        out_shape=jax.ShapeDtypeStruct((l, IN_WIDTH), BF16), name="rope_bwd",
        compiler_params=_params(("parallel",)),
    )(dqkv, du, *tabs)


_Q_COLS = ATTN_WIDTH // LANES
_SCALE = HEAD_DIM ** -0.5
_NEG = -1e30


def _window_specs(nb, width, col):
    return [
        pl.BlockSpec((BLOCK, width), lambda n: (jnp.maximum(n - 1, 0), col)),
        pl.BlockSpec((BLOCK, width), lambda n: (n, col)),
        pl.BlockSpec((BLOCK, width), lambda n: (jnp.minimum(n + 1, nb - 1), col)),
    ]


def _attn_fwd(qkv, sink):
    l = qkv.shape[0]
    nb = l // BLOCK
    grp = N_Q_HEADS // N_KV_HEADS

    def body(sink_ref, q_ref, k0, k1, k2, v0, v1, v2, o_ref, lse_ref):
        n = pl.program_id(0)
        q = q_ref[...]
        kw = jnp.concatenate([k0[...], k1[...], k2[...]], axis=0)
        vw = jnp.concatenate([v0[...], v1[...], v2[...]], axis=0)
        row = lax.broadcasted_iota(jnp.int32, (BLOCK, 3 * BLOCK), 0)
        col = lax.broadcasted_iota(jnp.int32, (BLOCK, 3 * BLOCK), 1)
        rel = col - BLOCK - row
        valid = (jnp.abs(rel) <= WINDOW)
        valid &= jnp.logical_not((n == 0) & (col < BLOCK))
        valid &= jnp.logical_not((n == nb - 1) & (col >= 2 * BLOCK))
        for h in range(N_Q_HEADS):
            hk = h // grp
            qh = q[:, h * HEAD_DIM:(h + 1) * HEAD_DIM]
            kh = kw[:, hk * HEAD_DIM:(hk + 1) * HEAD_DIM]
            vh = vw[:, hk * HEAD_DIM:(hk + 1) * HEAD_DIM]
            s = jnp.where(valid, _dg(qh, kh, NT) * _SCALE, _NEG)
            sk = sink_ref[0, h]
            m = jnp.maximum(jnp.max(s, axis=1, keepdims=True), sk)
            p = jnp.exp(s - m)
            denom = jnp.sum(p, axis=1, keepdims=True) + jnp.exp(sk - m)
            o = _dg((p / denom).astype(BF16), vh, NN)
            o_ref[:, h * HEAD_DIM:(h + 1) * HEAD_DIM] = o
            lse_ref[:, h:h + 1] = m + jnp.log(denom)

    return pl.pallas_call(
        body, grid=(nb,),
        in_specs=[pl.BlockSpec(memory_space=pltpu.SMEM),
                  pl.BlockSpec((BLOCK, ATTN_WIDTH), lambda n: (n, 0))]
        + _window_specs(nb, KV_WIDTH, _Q_COLS) + _window_specs(nb, KV_WIDTH, _Q_COLS + 1),
        out_specs=[pl.BlockSpec((BLOCK, ATTN_WIDTH), lambda n: (n, 0)),
                   pl.BlockSpec((BLOCK, N_Q_HEADS), lambda n: (n, 0))],
        out_shape=[jax.ShapeDtypeStruct((l, ATTN_WIDTH), F32), jax.ShapeDtypeStruct((l, N_Q_HEADS), F32)],
        name="attn_fwd", compiler_params=_params(("parallel",)),
    )(sink, qkv, qkv, qkv, qkv, qkv, qkv, qkv)


def _attn_bwd(qkv, attn, dattn, lse, sink):
    l = qkv.shape[0]
    nb = l // BLOCK
    grp = N_Q_HEADS // N_KV_HEADS

    def body(sink_ref, q0, q1, q2, k0, k1, k2, v0, v1, v2, o0, o1, o2, d0, d1, d2,
             l0, l1, l2, dqkv_ref, dsink_ref):
        n = pl.program_id(0)
        first, last = n == 0, n == nb - 1

        @pl.when(first)
        def _():
            dsink_ref[...] = jnp.zeros_like(dsink_ref)

        cat = lambda a, b, c: jnp.concatenate([a[...], b[...], c[...]], axis=0)
        qw, kw, vw = cat(q0, q1, q2), cat(k0, k1, k2), cat(v0, v1, v2)
        dow = cat(d0, d1, d2)
        prodw = cat(o0, o1, o2) * dow
        lsew = cat(l0, l1, l2)
        dob = dow.astype(BF16)
        q, k, v, do, lse_n = q1[...], k1[...], v1[...], dob[BLOCK:2 * BLOCK], l1[...]

        row = lax.broadcasted_iota(jnp.int32, (BLOCK, 3 * BLOCK), 0)
        col = lax.broadcasted_iota(jnp.int32, (BLOCK, 3 * BLOCK), 1)
        valid_q = jnp.abs(col - BLOCK - row) <= WINDOW
        valid_q &= jnp.logical_not(first & (col < BLOCK))
        valid_q &= jnp.logical_not(last & (col >= 2 * BLOCK))
        rowk = lax.broadcasted_iota(jnp.int32, (3 * BLOCK, BLOCK), 0)
        colk = lax.broadcasted_iota(jnp.int32, (3 * BLOCK, BLOCK), 1)
        valid_k = jnp.abs(colk + BLOCK - rowk) <= WINDOW
        valid_k &= jnp.logical_not(first & (rowk < BLOCK))
        valid_k &= jnp.logical_not(last & (rowk >= 2 * BLOCK))

        dsink_parts = []
        for hk in range(N_KV_HEADS):
            ksl = slice(hk * HEAD_DIM, (hk + 1) * HEAD_DIM)
            dk = jnp.zeros((BLOCK, HEAD_DIM), F32)
            dv = jnp.zeros((BLOCK, HEAD_DIM), F32)
            for g in range(grp):
                h = hk * grp + g
                hsl = slice(h * HEAD_DIM, (h + 1) * HEAD_DIM)
                deltaw = jnp.sum(prodw[:, hsl], axis=1, keepdims=True)
                delta = deltaw[BLOCK:2 * BLOCK]
                s = jnp.where(valid_q, _dg(q[:, hsl], kw[:, ksl], NT) * _SCALE, _NEG)
                p = jnp.exp(s - lse_n[:, h:h + 1])
                dp = _dg(do[:, hsl], vw[:, ksl], NT)
                ds = (p * (dp - delta) * _SCALE).astype(BF16)
                dqkv_ref[:, hsl] = _dg(ds, kw[:, ksl], NN)
                dsink_parts.append(jnp.sum(jnp.exp(sink_ref[0, h] - lse_n[:, h:h + 1]) * delta,
                                           axis=0, keepdims=True))
                s2 = jnp.where(valid_k, _dg(qw[:, hsl], k[:, ksl], NT) * _SCALE, _NEG)
                p2 = jnp.where(valid_k, jnp.exp(s2 - lsew[:, h:h + 1]), 0.0)
                dv += _dg(p2.astype(BF16), dob[:, hsl], TN)
                dp2 = _dg(dob[:, hsl], v[:, ksl], NT)
                ds2 = (p2 * (dp2 - deltaw) * _SCALE).astype(BF16)
                dk += _dg(ds2, qw[:, hsl], TN)
            dqkv_ref[:, ATTN_WIDTH + hk * HEAD_DIM:ATTN_WIDTH + (hk + 1) * HEAD_DIM] = dk
            dqkv_ref[:, ATTN_WIDTH + KV_WIDTH + hk * HEAD_DIM:ATTN_WIDTH + KV_WIDTH + (hk + 1) * HEAD_DIM] = dv
        dsink_ref[...] -= jnp.concatenate(dsink_parts, axis=1)

    return pl.pallas_call(
        body, grid=(nb,),
        in_specs=[pl.BlockSpec(memory_space=pltpu.SMEM)]
        + _window_specs(nb, ATTN_WIDTH, 0)
        + _window_specs(nb, KV_WIDTH, _Q_COLS) + _window_specs(nb, KV_WIDTH, _Q_COLS + 1)
        + _window_specs(nb, ATTN_WIDTH, 0) + _window_specs(nb, ATTN_WIDTH, 0)
        + _window_specs(nb, N_Q_HEADS, 0),
        out_specs=[pl.BlockSpec((BLOCK, QKV_WIDTH), lambda n: (n, 0)),
                   pl.BlockSpec((1, N_Q_HEADS), lambda n: (0, 0))],
        out_shape=[jax.ShapeDtypeStruct((l, QKV_WIDTH), F32), jax.ShapeDtypeStruct((1, N_Q_HEADS), F32)],
        name="attn_bwd", compiler_params=_params(("arbitrary",)),
    )(sink, qkv, qkv, qkv, qkv, qkv, qkv, qkv, qkv, qkv, attn, attn, attn,
      dattn, dattn, dattn, lse, lse, lse)


def _ssm_disc(a_re, a_im, log_step, b_re, b_im):
    step = jnp.exp(log_step)[..., None]
    mag = jnp.exp(a_re * step)
    lb_re, lb_im = mag * jnp.cos(a_im * step), mag * jnp.sin(a_im * step)
    nr, ni = lb_re - 1.0, lb_im
    den = a_re * a_re + a_im * a_im
    f_re = ((nr * a_re + ni * a_im) / den)[..., None]
    f_im = ((ni * a_re - nr * a_im) / den)[..., None]
    return lb_re, lb_im, f_re * b_re - f_im * b_im, f_re * b_im + f_im * b_re


def _ssm_pack(lb_re, lb_im, bb_re, bb_im, c_re, c_im):
    eye = jnp.eye(SSM_CH // SSM_GROUP, dtype=F32)
    ng = SSM_CH // SSM_GROUP

    def diag_b(bb):
        t = bb.reshape(2, SSM_CB, ng, SSM_STATE, SSM_GROUP)
        return jnp.einsum('dkgpc,gh->dkgchp', t, eye).reshape(2, SSM_CB, SSM_CH, SSM_ST)

    def diag_c(cc):
        t = cc.reshape(2, SSM_CB, ng, SSM_GROUP, SSM_STATE)
        return jnp.einsum('dkgcp,gh->dkhpgc', t, eye).reshape(2, SSM_CB, SSM_ST, SSM_CH)

    bcat = jnp.concatenate([diag_b(bb_re), diag_b(bb_im)], axis=-1)
    ccat = jnp.concatenate([diag_c(c_re), -diag_c(c_im)], axis=-2)
    lam_re = lb_re.reshape(2, SSM_CB, 1, SSM_ST)
    lam_im = lb_im.reshape(2, SSM_CB, 1, SSM_ST)
    return bcat, ccat, lam_re, lam_im


def _ssm_unpack(dbcat, dccat, dlam_re, dlam_im):
    ng = SSM_CH // SSM_GROUP
    eye = jnp.eye(ng, dtype=F32)

    def undiag_b(t):
        t = t.reshape(2, SSM_CB, ng, SSM_GROUP, ng, SSM_STATE)
        return jnp.einsum('dkgchp,gh->dkgpc', t, eye).reshape(2, N_SSM_GROUPS, SSM_STATE, SSM_GROUP)

    def undiag_c(t):
        t = t.reshape(2, SSM_CB, ng, SSM_STATE, ng, SSM_GROUP)
        return jnp.einsum('dkhpgc,gh->dkgcp', t, eye).reshape(2, N_SSM_GROUPS, SSM_GROUP, SSM_STATE)

    dbb_re, dbb_im = undiag_b(dbcat[..., :SSM_ST]), undiag_b(dbcat[..., SSM_ST:])
    dc_re, dc_im = undiag_c(dccat[:, :, :SSM_ST]), -undiag_c(dccat[:, :, SSM_ST:])
    shape = (2, N_SSM_GROUPS, SSM_STATE)
    return dlam_re.reshape(shape), dlam_im.reshape(shape), dbb_re, dbb_im, dc_re, dc_im


def _to_segments(t):
    l, w = t.shape
    return t.reshape(N_SEG, l // N_SEG, w).transpose(1, 0, 2).reshape(l, w)


def _from_segments(t):
    l, w = t.shape
    return t.reshape(l // N_SEG, N_SEG, w).transpose(1, 0, 2).reshape(l, w)


def _cfma(ar, ai, xr, xi, br, bi):
    return ar * xr - ai * xi + br, ar * xi + ai * xr + bi


def _scan_segments(xs_ref, ar, ai, rev, nj, prev_ref=None):
    shape = (N_SEG, SSM_ST)
    ar = jnp.broadcast_to(ar, shape)
    ai = jnp.broadcast_to(ai, shape)
    zero = jnp.zeros(shape, F32)
    re_cols, im_cols = pl.ds(0, SSM_ST), pl.ds(SSM_ST, SSM_ST)

    def rows_of(jj):
        j = jnp.where(rev, nj - 1 - jj, jj)
        return j, pl.ds(pl.multiple_of(j * N_SEG, N_SEG), N_SEG)

    def pass1(jj, carry):
        _, rows = rows_of(jj)
        return _cfma(ar, ai, carry[0], carry[1], xs_ref[rows, re_cols], xs_ref[rows, im_cols])

    end_r, end_i = lax.fori_loop(0, nj, pass1, (zero, zero))

    pr, pi = ar, ai
    for _ in range(int(math.log2(nj))):
        pr, pi = pr * pr - pi * pi, 2.0 * pr * pi
    seg = lax.broadcasted_iota(jnp.int32, shape, 0)

    def chain(shift, keep):
        ir, ii = zero, zero
        for _ in range(N_SEG - 1):
            tr, ti = _cfma(pr, pi, ir, ii, end_r, end_i)
            ir = jnp.where(keep, pltpu.roll(tr, shift, 0), 0.0)
            ii = jnp.where(keep, pltpu.roll(ti, shift, 0), 0.0)
        return ir, ii

    up_r, up_i = chain(1, seg >= 1)
    dn_r, dn_i = chain(N_SEG - 1, seg <= N_SEG - 2)
    init_r, init_i = jnp.where(rev, dn_r, up_r), jnp.where(rev, dn_i, up_i)

    def pass2(jj, carry):
        j, rows = rows_of(jj)
        nr, ni = _cfma(ar, ai, carry[0], carry[1], xs_ref[rows, re_cols], xs_ref[rows, im_cols])
        xs_ref[rows, re_cols] = nr
        xs_ref[rows, im_cols] = ni
        if prev_ref is None:
            return nr, ni
        jp = jnp.where(rev, j - 1, j + 1)
        inside = (jp >= 0) & (jp < nj)
        prow = pl.ds(pl.multiple_of(jnp.clip(jp, 0, nj - 1) * N_SEG, N_SEG), N_SEG)
        xr, xi = prev_ref[prow, re_cols], prev_ref[prow, im_cols]
        f = jnp.where(inside, 1.0, 0.0)
        return nr, ni, carry[2] + f * (nr * xr + ni * xi), carry[3] + f * (ni * xr - nr * xi)

    if prev_ref is None:
        lax.fori_loop(0, nj, pass2, (init_r, init_i))
        return init_r, init_i, None, None
    _, _, acc_r, acc_i = lax.fori_loop(0, nj, pass2, (init_r, init_i, zero, zero))
    return init_r, init_i, acc_r, acc_i


SSM_RC = 256


def _ssm_specs(l):
    act = pl.BlockSpec((l, SSM_CH), lambda k, d: (0, k))
    bmat = pl.BlockSpec((None, None, SSM_CH, 2 * SSM_ST), lambda k, d: (d, k, 0, 0))
    cmat = pl.BlockSpec((None, None, 2 * SSM_ST, SSM_CH), lambda k, d: (d, k, 0, 0))
    lam = pl.BlockSpec((None, None, 1, SSM_ST), lambda k, d: (d, k, 0, 0))
    return act, bmat, cmat, lam


def _ssm_fwd(u_seg, bcat, ccat, lam_re, lam_im):
    l = u_seg.shape[0]
    nj = l // N_SEG
    b_hi, b_lo = _split(bcat)
    c_hi, c_lo = _split(ccat)

    def body(u_ref, bh_ref, bl_ref, ch_ref, cl_ref, lr_ref, li_ref, y_ref, xs_ref):
        d = pl.program_id(1)

        def bu_chunk(i, _):
            rows = pl.ds(pl.multiple_of(i * SSM_RC, SSM_RC), SSM_RC)
            xs_ref[rows, :] = _dot3w(u_ref[rows, :], bh_ref[...], bl_ref[...], NN)
            return 0

        lax.fori_loop(0, l // SSM_RC, bu_chunk, 0)
        _scan_segments(xs_ref, lr_ref[...], li_ref[...], d == 1, nj)

        def y_chunk(i, _):
            rows = pl.ds(pl.multiple_of(i * SSM_RC, SSM_RC), SSM_RC)
            yv = _dot3w(xs_ref[rows, :], ch_ref[...], cl_ref[...], NN)

            @pl.when(d == 0)
            def _():
                y_ref[rows, :] = yv

            @pl.when(d == 1)
            def _():
                y_ref[rows, :] += yv

            return 0

        lax.fori_loop(0, l // SSM_RC, y_chunk, 0)

    act, bmat, cmat, lam = _ssm_specs(l)
    return pl.pallas_call(
        body, grid=(SSM_CB, 2), in_specs=[act, bmat, bmat, cmat, cmat, lam, lam], out_specs=act,
        out_shape=jax.ShapeDtypeStruct((l, SSM_WIDTH), F32),
        scratch_shapes=[pltpu.VMEM((l, 2 * SSM_ST), F32)],
        name="ssm_fwd", compiler_params=_params(("parallel", "arbitrary"), vmem_mb=56),
    )(u_seg, b_hi, b_lo, c_hi, c_lo, lam_re, lam_im)


def _ssm_bwd(u_seg, dy_seg, bcat, ccat, lam_re, lam_im):
    l = u_seg.shape[0]
    nj = l // N_SEG
    b_hi, b_lo = _split(bcat)
    c_hi, c_lo = _split(ccat)

    def body(u_ref, dy_ref, bh_ref, bl_ref, ch_ref, cl_ref, lr_ref, li_ref,
             du_ref, db_ref, dc_ref, dlr_ref, dli_ref, xs_ref, gs_ref):
        d = pl.program_id(1)
        rev = d == 1
        ar, ai = lr_ref[...], li_ref[...]

        def chunk1(i, _):
            rows = pl.ds(pl.multiple_of(i * SSM_RC, SSM_RC), SSM_RC)
            xs_ref[rows, :] = _dot3w(u_ref[rows, :], bh_ref[...], bl_ref[...], NN)
            gs_ref[rows, :] = _dot3w(dy_ref[rows, :], ch_ref[...], cl_ref[...], NT)
            return 0

        lax.fori_loop(0, l // SSM_RC, chunk1, 0)
        init_r, init_i, _, _ = _scan_segments(xs_ref, ar, ai, rev, nj)
        _, _, acc_r, acc_i = _scan_segments(gs_ref, ar, -ai, jnp.logical_not(rev), nj, prev_ref=xs_ref)
        jb = jnp.where(rev, nj - 1, 0)
        brow = pl.ds(pl.multiple_of(jb * N_SEG, N_SEG), N_SEG)
        gr, gi = gs_ref[brow, pl.ds(0, SSM_ST)], gs_ref[brow, pl.ds(SSM_ST, SSM_ST)]
        acc_r = acc_r + gr * init_r + gi * init_i
        acc_i = acc_i + gi * init_r - gr * init_i
        dlr_ref[...] = jnp.sum(acc_r, axis=0, keepdims=True)
        dli_ref[...] = jnp.sum(acc_i, axis=0, keepdims=True)

        db_ref[...] = jnp.zeros_like(db_ref)
        dc_ref[...] = jnp.zeros_like(dc_ref)

        def chunk2(i, _):
            rows = pl.ds(pl.multiple_of(i * SSM_RC, SSM_RC), SSM_RC)
            g = gs_ref[rows, :]
            dc_ref[...] += _dot3(xs_ref[rows, :], dy_ref[rows, :], TN)
            db_ref[...] += _dot3(u_ref[rows, :], g, TN)
            duv = _dot3w(g, bh_ref[...], bl_ref[...], NT)

            @pl.when(d == 0)
            def _():
                du_ref[rows, :] = duv

            @pl.when(d == 1)
            def _():
                du_ref[rows, :] += duv

            return 0

        lax.fori_loop(0, l // SSM_RC, chunk2, 0)

    act, bmat, cmat, lam = _ssm_specs(l)
    return pl.pallas_call(
        body, grid=(SSM_CB, 2), in_specs=[act, act, bmat, bmat, cmat, cmat, lam, lam],
        out_specs=[act, bmat, cmat, lam, lam],
        out_shape=[jax.ShapeDtypeStruct((l, SSM_WIDTH), F32),
                   jax.ShapeDtypeStruct(bcat.shape, F32), jax.ShapeDtypeStruct(ccat.shape, F32),
                   jax.ShapeDtypeStruct(lam_re.shape, F32), jax.ShapeDtypeStruct(lam_im.shape, F32)],
        scratch_shapes=[pltpu.VMEM((l, 2 * SSM_ST), F32), pltpu.VMEM((l, 2 * SSM_ST), F32)],
        name="ssm_bwd", compiler_params=_params(("parallel", "arbitrary"), vmem_mb=60),
    )(u_seg, dy_seg, b_hi, b_lo, c_hi, c_lo, lam_re, lam_im)


def _glu_fwd(y_ssm, u, d_skip, w_glu):
    l, w = u.shape

    def body(y_ref, u_ref, d_ref, w_ref, pre_ref, s_ref, ys_ref):
        pre = y_ref[...] + d_ref[...] * u_ref[...]
        z = _gelu(pre)
        s = _dg(z.astype(BF16), w_ref[...], NN)
        pre_ref[...] = pre
        s_ref[...] = s
        ys_ref[...] = z * _sigmoid(s)

    row = pl.BlockSpec((TM_EW, w), lambda i: (i, 0))
    out = jax.ShapeDtypeStruct((l, w), F32)
    return pl.pallas_call(
        body, grid=(l // TM_EW,),
        in_specs=[row, row, pl.BlockSpec((1, w), lambda i: (0, 0)), pl.BlockSpec((w, w), lambda i: (0, 0))],
        out_specs=[row, row, row], out_shape=[out, out, out], name="glu_fwd",
        compiler_params=_params(("parallel",)),
    )(y_ssm, u, d_skip, w_glu)


def _glu_bwd(pre, s, dys, u, d_skip, w_glu):
    l, w = u.shape

    def body(pre_ref, s_ref, dys_ref, u_ref, d_ref, w_ref, dpre_ref, z_ref, ds_ref, dd_ref):
        pre, dys = pre_ref[...], dys_ref[...]
        z = _gelu(pre)
        sig = _sigmoid(s_ref[...])
        ds = (dys * z * sig * (1.0 - sig)).astype(BF16)
        dz = dys * sig + _dg(ds, w_ref[...], NT)
        dpre = dz * _gelu_grad(pre)
        dpre_ref[...] = dpre
        z_ref[...] = z.astype(BF16)
        ds_ref[...] = ds

        @pl.when(pl.program_id(0) == 0)
        def _():
            dd_ref[...] = jnp.zeros_like(dd_ref)

        dd_ref[...] += jnp.sum(dpre * u_ref[...], axis=0, keepdims=True)

    row = pl.BlockSpec((TM_EW, w), lambda i: (i, 0))
    vec = pl.BlockSpec((1, w), lambda i: (0, 0))
    return pl.pallas_call(
        body, grid=(l // TM_EW,),
        in_specs=[row, row, row, row, vec, pl.BlockSpec((w, w), lambda i: (0, 0))],
        out_specs=[row, row, row, vec],
        out_shape=[jax.ShapeDtypeStruct((l, w), F32), jax.ShapeDtypeStruct((l, w), BF16),
                   jax.ShapeDtypeStruct((l, w), BF16), jax.ShapeDtypeStruct((1, w), F32)],
        name="glu_bwd", compiler_params=_params(("arbitrary",)),
    )(pre, s, dys, u, d_skip, w_glu)


TM_CV = 256
TC_CV = 256
HALO = SUBLANES


def _conv_specs(l, col0):
    per = TM_CV // HALO
    nh = l // HALO
    off = col0 // TC_CV
    return [
        pl.BlockSpec((HALO, TC_CV), lambda j, i: (jnp.maximum(i * per - 1, 0), j + off)),
        pl.BlockSpec((TM_CV, TC_CV), lambda j, i: (i, j + off)),
        pl.BlockSpec((HALO, TC_CV), lambda j, i: (jnp.minimum((i + 1) * per, nh - 1), j + off)),
    ]


def _ext(prev_ref, mid_ref, next_ref, first, last):
    p = jnp.where(first, 0.0, prev_ref[...])
    n = jnp.where(last, 0.0, next_ref[...])
    return jnp.concatenate([p, mid_ref[...], n], axis=0)


def _shift_dn(t):
    return pltpu.roll(t, 1, 0)


def _shift_up(t):
    return pltpu.roll(t, t.shape[0] - 1, 0)


def _conv3(e, w_ref, b_ref):
    return w_ref[0:1, :] * _shift_dn(e) + w_ref[1:2, :] * e + w_ref[2:3, :] * _shift_up(e) + b_ref[...]


def _convffn_fwd(up_pre, conv_w, conv_b):
    l = up_pre.shape[0]
    ni = l // TM_CV
    wspec = lambda off: pl.BlockSpec((3, TC_CV), lambda j, i: (0, j + off))
    bspec = lambda off: pl.BlockSpec((1, TC_CV), lambda j, i: (0, j + off))
    voff = D_FF // TC_CV

    def body(gp, gm, gn, vp, vm, vn, wg, bg, wv, bv, o_ref):
        i = pl.program_id(1)
        first, last = i == 0, i == ni - 1
        gate = _conv3(_ext(gp, gm, gn, first, last), wg, bg)[HALO:HALO + TM_CV]
        val = _conv3(_ext(vp, vm, vn, first, last), wv, bv)[HALO:HALO + TM_CV]
        o_ref[...] = (gate * _sigmoid(gate) * val).astype(BF16)

    return pl.pallas_call(
        body, grid=(D_FF // TC_CV, ni),
        in_specs=_conv_specs(l, 0) + _conv_specs(l, D_FF) + [wspec(0), bspec(0), wspec(voff), bspec(voff)],
        out_specs=pl.BlockSpec((TM_CV, TC_CV), lambda j, i: (i, j)),
        out_shape=jax.ShapeDtypeStruct((l, D_FF), BF16), name="convffn_fwd",
        compiler_params=_params(("parallel", "parallel")),
    )(up_pre, up_pre, up_pre, up_pre, up_pre, up_pre, conv_w, conv_b, conv_w, conv_b)


def _convffn_bwd(up_pre, dact, conv_w, conv_b):
    l = up_pre.shape[0]
    ni = l // TM_CV
    wspec = lambda off: pl.BlockSpec((3, TC_CV), lambda j, i: (0, j + off))
    bspec = lambda off: pl.BlockSpec((1, TC_CV), lambda j, i: (0, j + off))
    voff = D_FF // TC_CV

    def body(gp, gm, gn, vp, vm, vn, dp, dm, dn, wg, bg, wv, bv, dgate_ref, dval_ref, pg_ref, pv_ref):
        i = pl.program_id(1)
        first, last = i == 0, i == ni - 1
        ge, ve, de = _ext(gp, gm, gn, first, last), _ext(vp, vm, vn, first, last), _ext(dp, dm, dn, first, last)
        gate, val = _conv3(ge, wg, bg), _conv3(ve, wv, bv)
        sig = _sigmoid(gate)
        silu = gate * sig
        dgate = de * val * (sig + silu * (1.0 - sig))
        dval = de * silu
        mid = slice(HALO, HALO + TM_CV)
        rid = lax.broadcasted_iota(jnp.int32, (SUBLANES, TC_CV), 0)

        @pl.when(i == 0)
        def _():
            pg_ref[...] = jnp.zeros_like(pg_ref)
            pv_ref[...] = jnp.zeros_like(pv_ref)

        for dup, e, w_ref, out_ref, p_ref in ((dgate, ge, wg, dgate_ref, pg_ref), (dval, ve, wv, dval_ref, pv_ref)):
            dpre = w_ref[0:1, :] * _shift_up(dup) + w_ref[1:2, :] * dup + w_ref[2:3, :] * _shift_dn(dup)
            out_ref[...] = dpre[mid].astype(BF16)
            dm_ = dup[mid]
            sums = [jnp.sum(dm_ * _shift_dn(e)[mid], axis=0, keepdims=True),
                    jnp.sum(dm_ * e[mid], axis=0, keepdims=True),
                    jnp.sum(dm_ * _shift_up(e)[mid], axis=0, keepdims=True),
                    jnp.sum(dm_, axis=0, keepdims=True)]
            acc = jnp.zeros((SUBLANES, TC_CV), F32)
            for k, sk in enumerate(sums):
                acc = jnp.where(rid == k, sk, acc)
            p_ref[...] += acc

    tile = pl.BlockSpec((TM_CV, TC_CV), lambda j, i: (i, j))
    par = pl.BlockSpec((SUBLANES, TC_CV), lambda j, i: (0, j))
    dgate, dval, pg, pv = pl.pallas_call(
        body, grid=(D_FF // TC_CV, ni),
        in_specs=_conv_specs(l, 0) + _conv_specs(l, D_FF) + _conv_specs(l, 0)
        + [wspec(0), bspec(0), wspec(voff), bspec(voff)],
        out_specs=[tile, tile, par, par],
        out_shape=[jax.ShapeDtypeStruct((l, D_FF), BF16), jax.ShapeDtypeStruct((l, D_FF), BF16),
                   jax.ShapeDtypeStruct((SUBLANES, D_FF), F32), jax.ShapeDtypeStruct((SUBLANES, D_FF), F32)],
        name="convffn_bwd", compiler_params=_params(("parallel", "arbitrary")),
    )(up_pre, up_pre, up_pre, up_pre, up_pre, up_pre, dact, dact, dact, conv_w, conv_b, conv_w, conv_b)
    return jnp.concatenate([dgate, dval], axis=1), jnp.concatenate([pg, pv], axis=1)


def _local_step(x, target, wb, sp):
    l = x.shape[0]
    tabs = _rope_tables(l)
    disc = _ssm_disc(sp["a_re"], sp["a_im"], sp["log_step"], sp["b_re"], sp["b_im"])
    bcat, ccat, lam_re, lam_im = _ssm_pack(*disc, sp["c_re"], sp["c_im"])
    d_skip = sp["d_skip"].reshape(1, SSM_WIDTH)

    h = _rms_fwd(x, sp["norm_mix_g"], "rms_mix")
    proj = _mm_nn(h, wb["w_in"], 512, IN_WIDTH, F32, "mm_in")
    qkv = _rope_fwd(proj, tabs)
    attn, lse = _attn_fwd(qkv, sp["sink"])
    u = proj[:, QKV_WIDTH:]
    u_seg = _to_segments(u)
    y_ssm = _from_segments(_ssm_fwd(u_seg, bcat, ccat, lam_re, lam_im))
    pre, s_glu, ys = _glu_fwd(y_ssm, u, d_skip, wb["w_glu"])
    mixed = _mix_fwd(attn, ys, sp["norm_attn_g"], sp["norm_ssm_g"])
    x1 = _mm_nn(mixed, wb["w_out"], 512, 512, F32, "mm_out", res=x)
    h2 = _rms_fwd(x1, sp["norm_ffn_g"], "rms_ffn")
    up_pre = _mm_nn(h2, wb["w_up"], 512, 512, F32, "mm_up")
    act = _convffn_fwd(up_pre, sp["conv_w"], sp["conv_b"])
    x2 = _mm_nn(act, wb["w_down"], 512, 512, F32, "mm_down", res=x1)
    loss, dx2, dx2b, d_final_g = _final_loss(x2, sp["norm_final_g"].reshape(1, D_MODEL), target)

    g = {"norm_final_g": d_final_g.reshape(D_MODEL)}
    dact = _mm_nt(dx2b, wb["w_down"], 512, D_FF // 2, F32, "mm_down_dx")
    g["w_down"] = _mm_tn(act, dx2b, 256, 512, "mm_down_dw")
    dup_pre, conv_par = _convffn_bwd(up_pre, dact, sp["conv_w"], sp["conv_b"])
    g["conv_w"], g["conv_b"] = conv_par[0:3], conv_par[3:4]
    g["w_up"] = _mm_tn(h2, dup_pre, 512, 512, "mm_up_dw")
    dh2 = _mm_nt(dup_pre, wb["w_up"], 256, 512, F32, "mm_up_dx")
    dx1, dx1b, g["norm_ffn_g"] = _rms_bwd(x1, sp["norm_ffn_g"], dh2, dx2, "rms_ffn_bwd")
    dmixed = _mm_nt(dx1b, wb["w_out"], 512, 512, F32, "mm_out_dx")
    g["w_out"] = _mm_tn(mixed, dx1b, 512, 512, "mm_out_dw")
    dattn, dys, g["norm_attn_g"], g["norm_ssm_g"] = _mix_bwd(attn, ys, sp["norm_attn_g"], sp["norm_ssm_g"], dmixed)
    dpre, zb, dsb, dd = _glu_bwd(pre, s_glu, dys, u, d_skip, wb["w_glu"])
    g["d_skip"] = dd.reshape(N_SSM_GROUPS, SSM_GROUP)
    g["w_glu"] = _mm_tn(zb, dsb, 512, 512, "mm_glu_dw")
    du_seg, dbcat, dccat, dlam_re, dlam_im = _ssm_bwd(u_seg, _to_segments(dpre), bcat, ccat, lam_re, lam_im)
    dlb_re, dlb_im, dbb_re, dbb_im, g["c_re"], g["c_im"] = _ssm_unpack(dbcat, dccat, dlam_re, dlam_im)
    _, disc_vjp = jax.vjp(_ssm_disc, sp["a_re"], sp["a_im"], sp["log_step"], sp["b_re"], sp["b_im"])
    g["a_re"], g["a_im"], g["log_step"], g["b_re"], g["b_im"] = disc_vjp((dlb_re, dlb_im, dbb_re, dbb_im))
    du = _from_segments(du_seg) + dpre * d_skip
    dqkv, dsink = _attn_bwd(qkv, attn, dattn, lse, sp["sink"])
    g["sink"] = dsink
    dproj = _rope_bwd(dqkv, du, tabs)
    g["w_in"] = _mm_tn(h, dproj, 512, IN_WIDTH, "mm_in_dw")
    dh = _mm_nt(dproj, wb["w_in"], 512, 512, F32, "mm_in_dx")
    grad_x, _, g["norm_mix_g"] = _rms_bwd(x, sp["norm_mix_g"], dh, dx1, "rms_mix_bwd")
    return loss, grad_x, g


MESH = pl.DeviceIdType.MESH
ANY = pl.BlockSpec(memory_space=pl.ANY)


def _place():
    x, y, c = lax.axis_index("x"), lax.axis_index("y"), lax.axis_index("c")
    chips = [(1 - x, y), (x, 1 - y), (1 - x, 1 - y)]
    return x, y, c, chips


def _chip_index(px, py):
    return 2 * px + py


CHUNK_BYTES = 256 * 1024
MAX_CHUNKS = 16


def _row_chunks(rows, row_bytes, align):
    n = max(1, min(MAX_CHUNKS, (rows * row_bytes) // CHUNK_BYTES))
    per = -(-rows // n)
    per = -(-per // align) * align
    return [(r0, min(per, rows - r0)) for r0 in range(0, rows, per)]


def _align_of(dtype):
    return SUBLANES * 4 // jnp.dtype(dtype).itemsize


def _remote(src, dst, send_sem, recv_sem, to):
    return pltpu.make_async_remote_copy(src_ref=src, dst_ref=dst, send_sem=send_sem, recv_sem=recv_sem,
                                        device_id=to, device_id_type=MESH)


def _cast_bf16(w, name):
    r, c = w.shape
    tr = r if r <= 512 else r // 2

    def body(w_ref, o_ref):
        o_ref[...] = w_ref[...].astype(BF16)

    spec = pl.BlockSpec((tr, c), lambda i: (i, 0))
    return pl.pallas_call(body, grid=(r // tr,), in_specs=[spec], out_specs=spec,
                          out_shape=jax.ShapeDtypeStruct((r, c), BF16), name=name,
                          compiler_params=_params(("parallel",)))(w)


def _gather_weights(shards):
    nw = len(shards)

    def body(*refs):
        w_refs, o_refs = refs[:nw], refs[nw:2 * nw]
        send_sems, recv_sems = refs[2 * nw:]
        x, y, c, chips = _place()
        mine = _chip_index(x, y)
        sibling = (x, y, 1 - c)

        def rows_of(ref, chip, r0, nr):
            return ref.at[chip, pl.ds(r0, nr), :]

        def copy(wi, k, src, dst, to):
            return _remote(src, dst, send_sems.at[wi, k], recv_sems.at[wi, k], to)

        geo = []
        for wi in range(nw):
            rows, cols = w_refs[wi].shape
            row_bytes = cols * jnp.dtype(w_refs[wi].dtype).itemsize
            geo.append((rows // 2, _row_chunks(rows // 2, row_bytes, _align_of(w_refs[wi].dtype))))

        for wi in range(nw):
            hr, half_chunks = geo[wi]
            for j, chip in enumerate(chips):
                for r0, nr in half_chunks:
                    copy(wi, j, w_refs[wi].at[pl.ds(c * hr + r0, nr), :],
                         rows_of(o_refs[wi], mine, c * hr + r0, nr), (*chip, c)).start()
        for wi in range(nw):
            hr, half_chunks = geo[wi]
            for j, chip in enumerate(chips):
                got = rows_of(o_refs[wi], _chip_index(*chip), c * hr, hr)
                copy(wi, j, got, got, (*chip, c)).wait_recv()
                for r0, nr in half_chunks:
                    piece = rows_of(o_refs[wi], _chip_index(*chip), c * hr + r0, nr)
                    copy(wi, 3 + j, piece, piece, sibling).start()
        for wi in range(nw):
            hr = geo[wi][0]
            for j, chip in enumerate(chips):
                got = rows_of(o_refs[wi], _chip_index(*chip), (1 - c) * hr, hr)
                copy(wi, 3 + j, got, got, sibling).wait_recv()
        for wi in range(nw):
            hr = geo[wi][0]
            sent = rows_of(o_refs[wi], mine, c * hr, hr)
            for k in range(6):
                copy(wi, k, sent, sent, sibling).wait_send()

    outs = pl.pallas_call(
        body, in_specs=[ANY] * nw, out_specs=[ANY] * nw,
        out_shape=[jax.ShapeDtypeStruct((4, *s.shape), s.dtype) for s in shards],
        scratch_shapes=[pltpu.SemaphoreType.DMA((nw, 6)), pltpu.SemaphoreType.DMA((nw, 6))],
        name="gather_weights",
    )(*shards)
    mine = _chip_index(lax.axis_index("x"), lax.axis_index("y"))
    return [lax.dynamic_update_slice(o, s[None], (mine, 0, 0)) for o, s in zip(outs, shards)]


def _pair_exchange(grads):
    na = len(grads)

    def body(*refs):
        g_refs, o_refs = refs[:na], refs[na:2 * na]
        send_sems, recv_sems = refs[2 * na:]
        x, y, c, _ = _place()
        sibling = (x, y, 1 - c)
        for ai in range(na):
            _, rows, cols = g_refs[ai].shape
            hr = rows // 2
            for k in range(4):
                for r0, nr in _row_chunks(hr, cols * 4, SUBLANES):
                    _remote(g_refs[ai].at[k, pl.ds((1 - c) * hr + r0, nr), :], o_refs[ai].at[k, pl.ds(r0, nr), :],
                            send_sems.at[ai], recv_sems.at[ai], sibling).start()
        for ai in range(na):
            _remote(o_refs[ai], o_refs[ai], send_sems.at[ai], recv_sems.at[ai], sibling).wait()

    return pl.pallas_call(
        body, in_specs=[ANY] * na, out_specs=[ANY] * na,
        out_shape=[jax.ShapeDtypeStruct((4, g.shape[1] // 2, g.shape[2]), F32) for g in grads],
        scratch_shapes=[pltpu.SemaphoreType.DMA((na,)), pltpu.SemaphoreType.DMA((na,))],
        name="pair_exchange",
    )(*grads)


def _row_tile(rows, cols, align):
    best = align
    for cand in range(align, rows + 1, align):
        if rows % cand == 0 and cand * cols <= 256 * 1024:
            best = cand
    return best


def _pair_sum(g, got, place, transit, name):
    _, rows, cols = g.shape
    hr = rows // 2
    tr = _row_tile(hr, cols, _align_of(transit))
    nt = hr // tr

    def body(p_ref, g_ref, r_ref, s_ref, own_ref):
        total = g_ref[...] + r_ref[...]
        s_ref[...] = total.astype(transit)

        @pl.when(pl.program_id(1) == p_ref[1])
        def _():
            own_ref[...] = total

    grid_spec = pltpu.PrefetchScalarGridSpec(
        num_scalar_prefetch=1, grid=(nt, 4),
        in_specs=[pl.BlockSpec((None, tr, cols), lambda i, k, p: (k, p[0] * nt + i, 0)),
                  pl.BlockSpec((None, tr, cols), lambda i, k, p: (k, i, 0))],
        out_specs=[pl.BlockSpec((None, tr, cols), lambda i, k, p: (k, i, 0)),
                   pl.BlockSpec((tr, cols), lambda i, k, p: (i, 0))])
    return pl.pallas_call(
        body, grid_spec=grid_spec,
        out_shape=[jax.ShapeDtypeStruct((4, hr, cols), transit), jax.ShapeDtypeStruct((hr, cols), F32)],
        name=name, compiler_params=_params(("parallel", "arbitrary")),
    )(place, g, got)


def _chip_exchange(sums):
    na = len(sums)

    def body(*refs):
        s_refs, o_refs = refs[:na], refs[na:2 * na]
        send_sems, recv_sems = refs[2 * na:]
        x, y, c, chips = _place()
        for ai in range(na):
            _, rows, cols = s_refs[ai].shape
            row_bytes = cols * jnp.dtype(s_refs[ai].dtype).itemsize
            for r0, nr in _row_chunks(rows, row_bytes, _align_of(s_refs[ai].dtype)):
                for j, chip in enumerate(chips):
                    _remote(s_refs[ai].at[_chip_index(*chip), pl.ds(r0, nr), :], o_refs[ai].at[j, pl.ds(r0, nr), :],
                            send_sems.at[ai, j], recv_sems.at[ai, j], (*chip, c)).start()
        for ai in range(na):
            for j, chip in enumerate(chips):
                _remote(o_refs[ai].at[j], o_refs[ai].at[j], send_sems.at[ai, j], recv_sems.at[ai, j],
                        (*chip, c)).wait()

    return pl.pallas_call(
        body, in_specs=[ANY] * na, out_specs=[ANY] * na,
        out_shape=[jax.ShapeDtypeStruct((3, *s.shape[1:]), s.dtype) for s in sums],
        scratch_shapes=[pltpu.SemaphoreType.DMA((na, 3)), pltpu.SemaphoreType.DMA((na, 3))],
        name="chip_exchange",
    )(*sums)


def _chip_sum(own, landed, name):
    hr, cols = own.shape
    tr = _row_tile(hr, cols, _align_of(landed.dtype))

    def body(o_ref, l_ref, f_ref):
        acc = o_ref[...]
        for j in range(3):
            acc = acc + l_ref[j].astype(F32)
        f_ref[...] = acc

    return pl.pallas_call(
        body, grid=(hr // tr,),
        in_specs=[pl.BlockSpec((tr, cols), lambda i: (i, 0)), pl.BlockSpec((3, tr, cols), lambda i: (0, i, 0))],
        out_specs=pl.BlockSpec((tr, cols), lambda i: (i, 0)),
        out_shape=jax.ShapeDtypeStruct((hr, cols), F32), name=name,
        compiler_params=_params(("parallel",)),
    )(own, landed)


def _final_exchange(halves, small):
    nh = len(halves)

    def body(*refs):
        h_refs, s_ref = refs[:nh], refs[nh]
        o_refs, so_ref = refs[nh + 1:2 * nh + 1], refs[2 * nh + 1]
        send_sems, recv_sems, local_sem, ssend_sems, srecv_sems = refs[2 * nh + 2:]
        x, y, c, _ = _place()
        me = 4 * x + 2 * y + c
        sibling = (x, y, 1 - c)
        for hi in range(nh):
            hr, cols = h_refs[hi].shape
            for r0, nr in _row_chunks(hr, cols * 4, SUBLANES):
                _remote(h_refs[hi].at[pl.ds(r0, nr), :], o_refs[hi].at[pl.ds(r0, nr), :],
                        send_sems.at[hi], recv_sems.at[hi], sibling).start()
        small_cps = [pltpu.make_async_copy(s_ref, so_ref.at[me], local_sem)]
        for r in range(1, 8):
            fx, fy, fc = (r >> 2) & 1, (r >> 1) & 1, r & 1
            peer = (1 - x if fx else x, 1 - y if fy else y, 1 - c if fc else c)
            small_cps.append(_remote(s_ref, so_ref.at[me], ssend_sems.at[r - 1], srecv_sems.at[r - 1], peer))
        for cp in small_cps:
            cp.start()
        for hi in range(nh):
            _remote(h_refs[hi], o_refs[hi], send_sems.at[hi], recv_sems.at[hi], sibling).wait()
        for cp in small_cps:
            cp.wait()

    return pl.pallas_call(
        body, in_specs=[ANY] * (nh + 1), out_specs=[ANY] * (nh + 1),
        out_shape=[jax.ShapeDtypeStruct(h.shape, F32) for h in halves]
        + [jax.ShapeDtypeStruct((8, *small.shape), F32)],
        scratch_shapes=[pltpu.SemaphoreType.DMA((nh,)), pltpu.SemaphoreType.DMA((nh,)),
                        pltpu.SemaphoreType.DMA, pltpu.SemaphoreType.DMA((7,)), pltpu.SemaphoreType.DMA((7,))],
        name="final_exchange",
    )(*halves, small)


def _adamw(w, g, m, v, name):
    shape = w.shape
    n = w.size
    if w.ndim >= 2 and shape[-1] >= LANES:
        two_d = (n // shape[-1], shape[-1])
    elif n % LANES == 0:
        two_d = (n // LANES, LANES)
    else:
        two_d = (1, n)
    r, c = two_d
    tr = r
    for cand in (512, 256, 176, 128, 64):
        if r > cand and r % cand == 0 and cand * c <= 256 * 1024:
            tr = cand
            break
    c1 = 1.0 - ADAM_B1 ** ADAM_STEP
    c2 = 1.0 - ADAM_B2 ** ADAM_STEP

    def body(w_ref, g_ref, m_ref, v_ref, d_ref, nm_ref, nv_ref):
        gv = g_ref[...]
        nm = ADAM_B1 * m_ref[...] + (1.0 - ADAM_B1) * gv
        nv = ADAM_B2 * v_ref[...] + (1.0 - ADAM_B2) * (gv * gv)
        d_ref[...] = -ADAM_LR * ((nm / c1) / (jnp.sqrt(nv / c2) + ADAM_EPS) + ADAM_WD * w_ref[...])
        nm_ref[...] = nm
        nv_ref[...] = nv

    spec = pl.BlockSpec((tr, c), lambda i: (i, 0))
    out = jax.ShapeDtypeStruct((r, c), F32)
    d, nm, nv = pl.pallas_call(
        body, grid=(r // tr,), in_specs=[spec] * 4, out_specs=[spec] * 3, out_shape=[out] * 3, name=name,
        compiler_params=_params(("parallel",)),
    )(w.reshape(two_d), g.reshape(two_d), m.reshape(two_d), v.reshape(two_d))
    return d.reshape(shape), nm.reshape(shape), nv.reshape(shape)


BIG = ("w_in", "w_glu", "w_out", "w_up", "w_down")
WEIGHTS = ("norm_mix_g", "w_in", "a_re", "a_im", "log_step", "b_re", "b_im", "c_re", "c_im", "d_skip", "w_glu",
           "sink", "norm_attn_g", "norm_ssm_g", "w_out", "norm_ffn_g", "w_up", "conv_w", "conv_b", "w_down",
           "norm_final_g")
SMALL = ("norm_mix_g", "a_re", "a_im", "log_step", "b_re", "b_im", "c_re", "c_im", "d_skip", "sink",
         "norm_attn_g", "norm_ssm_g", "norm_ffn_g", "conv_w", "conv_b", "norm_final_g")
SMALL_ROWS = 40
N_DEV = 8


def _full_matrices(gathered):
    w_in, w_glu, w_out, w_up, w_down = gathered
    cols = lambda t: t.transpose(1, 0, 2).reshape(t.shape[1], 4 * t.shape[2])
    rows = lambda t: t.reshape(4 * t.shape[1], t.shape[2])
    return {"w_in": cols(w_in), "w_glu": rows(w_glu), "w_out": rows(w_out), "w_up": cols(w_up),
            "w_down": rows(w_down)}


def _by_owner(name, g):
    if name in ("w_in", "w_up"):
        return g.reshape(g.shape[0], 4, g.shape[1] // 4).transpose(1, 0, 2)
    return g.reshape(4, g.shape[0] // 4, g.shape[1])


def kernel(x, norm_mix_g, w_in, a_re, a_im, log_step, b_re, b_im, c_re, c_im, d_skip, w_glu, sink, norm_attn_g, norm_ssm_g, w_out, norm_ffn_g, w_up, conv_w, conv_b, w_down, norm_final_g, loss_target, m_norm_mix_g, m_w_in, m_a_re, m_a_im, m_log_step, m_b_re, m_b_im, m_c_re, m_c_im, m_d_skip, m_w_glu, m_sink, m_norm_attn_g, m_norm_ssm_g, m_w_out, m_norm_ffn_g, m_w_up, m_conv_w, m_conv_b, m_w_down, m_norm_final_g, v_norm_mix_g, v_w_in, v_a_re, v_a_im, v_log_step, v_b_re, v_b_im, v_c_re, v_c_im, v_d_skip, v_w_glu, v_sink, v_norm_attn_g, v_norm_ssm_g, v_w_out, v_norm_ffn_g, v_w_up, v_conv_w, v_conv_b, v_w_down, v_norm_final_g):
    given = dict(locals())
    w = {n: given[n] for n in WEIGHTS}
    m = {n: given["m_" + n] for n in WEIGHTS}
    v = {n: given["v_" + n] for n in WEIGHTS}
    xy = 2 * lax.axis_index("x") + lax.axis_index("y")

    shards = [_cast_bf16(w[n][0], "cast_" + n) for n in BIG]
    conv_rows = jnp.pad(w["conv_w"][0], ((0, 2 * SUBLANES - 3), (0, 0)))
    *gathered, conv_all = _gather_weights(shards + [conv_rows])
    wb = _full_matrices(gathered)

    sp = {n: w[n][0] for n in ("a_re", "a_im", "log_step", "b_re", "b_im", "c_re", "c_im", "d_skip",
                               "norm_mix_g", "norm_attn_g", "norm_ssm_g", "norm_ffn_g", "sink", "conv_b")}
    for n in ("norm_mix_g", "norm_attn_g", "norm_ssm_g", "norm_ffn_g", "sink", "conv_b"):
        sp[n] = sp[n].reshape(1, -1)
    sp["conv_w"] = conv_all[:, :3].transpose(1, 0, 2).reshape(3, 2 * D_FF)
    sp["norm_final_g"] = w["norm_final_g"]
    loss, grad_x, g = _local_step(x[0], loss_target[0], wb, sp)

    flat = jnp.concatenate([g[n].reshape(-1) for n in SMALL])
    pad = N_DEV * SMALL_ROWS * D_MODEL - flat.shape[0]
    small = jnp.concatenate([flat, jnp.zeros((pad,), F32)]).reshape(4, 2 * SMALL_ROWS, D_MODEL)
    by_owner = [_by_owner(n, g[n]) for n in BIG] + [small]
    core = lax.axis_index("c")
    place = jnp.stack([core, xy]).astype(jnp.int32)
    got = _pair_exchange(by_owner)
    transit = [BF16] * len(BIG) + [F32]
    chip_sums, own_sums = zip(*[_pair_sum(a, b, place, t, "pair_sum_%d" % i)
                                for i, (a, b, t) in enumerate(zip(by_owner, got, transit))])
    landed = _chip_exchange(list(chip_sums))
    halves = [_chip_sum(o, t, "chip_sum_%d" % i) for i, (o, t) in enumerate(zip(own_sums, landed))]
    *others, small_all = _final_exchange(halves[:-1], halves[-1])
    grads = {n: jnp.concatenate([jnp.where(core == 0, h, o), jnp.where(core == 0, o, h)], axis=0)
             for n, h, o in zip(BIG, halves, others)}
    flat = small_all.reshape(-1)
    off = 0
    for n in SMALL:
        shape = (3, 4 * w[n].shape[-1]) if n == "conv_w" else w[n].shape[1:] if n != "norm_final_g" else w[n].shape
        size = math.prod(shape)
        grads[n] = flat[off:off + size].reshape(shape)
        off += size
    cw = w["conv_w"].shape[-1]
    grads["conv_w"] = lax.dynamic_slice_in_dim(grads["conv_w"], xy * cw, cw, axis=1)

    outs_g, outs_d, outs_m, outs_v = [], [], [], []
    for n in WEIGHTS:
        gn = grads[n].reshape(w[n].shape)
        dn, mn, vn = _adamw(w[n], gn, m[n], v[n], "adamw_" + n)
        outs_g.append(gn)
        outs_d.append(dn)
        outs_m.append(mn)
        outs_v.append(vn)
    loss = lax.psum(loss[0, 0], ("x", "y", "c"))
    return (loss, grad_x[None], *outs_g, *outs_d, *outs_m, *outs_v)
```

```python
import functools
import math

import jax
import jax.numpy as jnp
from jax import lax
from jax.experimental import pallas as pl
from jax.experimental.pallas import tpu as pltpu

F32 = jnp.float32
BF16 = jnp.bfloat16

D_MODEL = 1024
N_Q_HEADS = 8
N_KV_HEADS = 2
HEAD_DIM = 64
ATTN_WIDTH = 512
KV_WIDTH = 128
QKV_WIDTH = ATTN_WIDTH + 2 * KV_WIDTH
WINDOW = 128
BLOCK = 128
ROPE_DIM = 16
ROPE_THETA = 500000.0
SSM_WIDTH = 512
SSM_GROUP = 16
N_SSM_GROUPS = 32
SSM_STATE = 64
IN_WIDTH = 1280
D_FF = 2816
EPS = 1e-6
ADAM_LR = 0.001
ADAM_B1 = 0.9
ADAM_B2 = 0.999
ADAM_EPS = 1e-08
ADAM_WD = 0.01
ADAM_STEP = 10

VMEM_BYTES_V7X = 64 * 1024 * 1024
SUBLANES = 8
LANES = 128
SSM_CB = 4
SSM_CH = 128
SSM_ST = 512
N_SEG = SUBLANES

NN = (((1,), (0,)), ((), ()))
NT = (((1,), (1,)), ((), ()))
TN = (((0,), (0,)), ((), ()))


def _params(sem=None, vmem_mb=48):
    return pltpu.CompilerParams(dimension_semantics=sem, vmem_limit_bytes=vmem_mb * 1024 * 1024)


def _dg(a, b, dims):
    return lax.dot_general(a, b, dims, preferred_element_type=F32)


def _sigmoid(x):
    return 1.0 / (1.0 + jnp.exp(-x))


_SQRT_HALF = 0.7071067811865476
_INV_SQRT_2PI = 0.3989422804014327


def _gelu(x):
    return 0.5 * x * (1.0 + lax.erf(x * _SQRT_HALF))


def _gelu_grad(x):
    return 0.5 * (1.0 + lax.erf(x * _SQRT_HALF)) + x * (_INV_SQRT_2PI * jnp.exp(-0.5 * x * x))


def _mm_nn(a, b, tm, tn, out_dtype, name, res=None):
    m, k = a.shape
    n = b.shape[1]

    def body(*refs):
        if res is None:
            a_ref, b_ref, o_ref = refs
            o_ref[...] = _dg(a_ref[...], b_ref[...], NN).astype(out_dtype)
        else:
            a_ref, b_ref, r_ref, o_ref = refs
            o_ref[...] = (r_ref[...] + _dg(a_ref[...], b_ref[...], NN)).astype(out_dtype)

    in_specs = [pl.BlockSpec((tm, k), lambda i, j: (i, 0)), pl.BlockSpec((k, tn), lambda i, j: (0, j))]
    args = [a, b]
    if res is not None:
        in_specs.append(pl.BlockSpec((tm, tn), lambda i, j: (i, j)))
        args.append(res)
    return pl.pallas_call(
        body, grid=(m // tm, n // tn), in_specs=in_specs,
        out_specs=pl.BlockSpec((tm, tn), lambda i, j: (i, j)),
        out_shape=jax.ShapeDtypeStruct((m, n), out_dtype), name=name,
        compiler_params=_params(("parallel", "parallel")),
    )(*args)


def _mm_nt(a, b, tm, tn, out_dtype, name):
    m, k = a.shape
    n = b.shape[0]

    def body(a_ref, b_ref, o_ref):
        o_ref[...] = _dg(a_ref[...], b_ref[...], NT).astype(out_dtype)

    return pl.pallas_call(
        body, grid=(m // tm, n // tn),
        in_specs=[pl.BlockSpec((tm, k), lambda i, j: (i, 0)), pl.BlockSpec((tn, k), lambda i, j: (j, 0))],
        out_specs=pl.BlockSpec((tm, tn), lambda i, j: (i, j)),
        out_shape=jax.ShapeDtypeStruct((m, n), out_dtype), name=name,
        compiler_params=_params(("parallel", "parallel")),
    )(a, b)


def _mm_tn(a, b, tm, tn, name):
    k, m = a.shape
    n = b.shape[1]

    def body(a_ref, b_ref, o_ref):
        o_ref[...] = _dg(a_ref[...], b_ref[...], TN)

    return pl.pallas_call(
        body, grid=(m // tm, n // tn),
        in_specs=[pl.BlockSpec((k, tm), lambda i, j: (0, i)), pl.BlockSpec((k, tn), lambda i, j: (0, j))],
        out_specs=pl.BlockSpec((tm, tn), lambda i, j: (i, j)),
        out_shape=jax.ShapeDtypeStruct((m, n), F32), name=name,
        compiler_params=_params(("parallel", "parallel")),
    )(a, b)


TM_EW = 256


def _rms_fwd(x, g, name):
    l, d = x.shape

    def body(x_ref, g_ref, h_ref):
        xv = x_ref[...]
        r = lax.rsqrt(jnp.mean(xv * xv, axis=-1, keepdims=True) + EPS)
        h_ref[...] = (xv * r * g_ref[...]).astype(BF16)

    return pl.pallas_call(
        body, grid=(l // TM_EW,),
        in_specs=[pl.BlockSpec((TM_EW, d), lambda i: (i, 0)), pl.BlockSpec((1, d), lambda i: (0, 0))],
        out_specs=pl.BlockSpec((TM_EW, d), lambda i: (i, 0)),
        out_shape=jax.ShapeDtypeStruct((l, d), BF16), name=name,
        compiler_params=_params(("parallel",)),
    )(x, g)


def _rms_bwd_vals(xv, gv, dy):
    r = lax.rsqrt(jnp.mean(xv * xv, axis=-1, keepdims=True) + EPS)
    xh = xv * r
    dxh = dy * gv
    dx = r * (dxh - xh * jnp.mean(dxh * xh, axis=-1, keepdims=True))
    return dx, dy * xh


def _rms_bwd(x, g, dy, res, name):
    l, d = x.shape

    def body(x_ref, g_ref, dy_ref, res_ref, dx_ref, dxb_ref, dg_ref):
        dx, dgr = _rms_bwd_vals(x_ref[...], g_ref[...], dy_ref[...])
        dx = dx + res_ref[...]
        dx_ref[...] = dx
        dxb_ref[...] = dx.astype(BF16)

        @pl.when(pl.program_id(0) == 0)
        def _():
            dg_ref[...] = jnp.zeros_like(dg_ref)

        dg_ref[...] += jnp.sum(dgr, axis=0, keepdims=True)

    row = pl.BlockSpec((TM_EW, d), lambda i: (i, 0))
    vec = pl.BlockSpec((1, d), lambda i: (0, 0))
    return pl.pallas_call(
        body, grid=(l // TM_EW,), in_specs=[row, vec, row, row], out_specs=[row, row, vec],
        out_shape=[jax.ShapeDtypeStruct((l, d), F32), jax.ShapeDtypeStruct((l, d), BF16),
                   jax.ShapeDtypeStruct((1, d), F32)],
        name=name, compiler_params=_params(("arbitrary",)),
    )(x, g, dy, res)


def _final_loss(x2, g, target):
    l, d = x2.shape

    def body(x_ref, g_ref, t_ref, loss_ref, dx_ref, dxb_ref, dg_ref):
        xv = x_ref[...]
        gv = g_ref[...]
        r = lax.rsqrt(jnp.mean(xv * xv, axis=-1, keepdims=True) + EPS)
        xh = xv * r
        e = xh * gv - t_ref[...]
        part = jnp.sum(jnp.sum(e * e, axis=1, keepdims=True), axis=0, keepdims=True) * (0.5 / d)
        dy = e * (1.0 / d)
        dxh = dy * gv
        dx = r * (dxh - xh * jnp.mean(dxh * xh, axis=-1, keepdims=True))
        dx_ref[...] = dx
        dxb_ref[...] = dx.astype(BF16)

        @pl.when(pl.program_id(0) == 0)
        def _():
            dg_ref[...] = jnp.zeros_like(dg_ref)
            loss_ref[...] = jnp.zeros_like(loss_ref)

        dg_ref[...] += jnp.sum(dy * xh, axis=0, keepdims=True)
        loss_ref[...] += part

    row = pl.BlockSpec((TM_EW, d), lambda i: (i, 0))
    vec = pl.BlockSpec((1, d), lambda i: (0, 0))
    one = pl.BlockSpec((1, 1), lambda i: (0, 0))
    return pl.pallas_call(
        body, grid=(l // TM_EW,), in_specs=[row, vec, row], out_specs=[one, row, row, vec],
        out_shape=[jax.ShapeDtypeStruct((1, 1), F32), jax.ShapeDtypeStruct((l, d), F32),
                   jax.ShapeDtypeStruct((l, d), BF16), jax.ShapeDtypeStruct((1, d), F32)],
        name="final_loss", compiler_params=_params(("arbitrary",)),
    )(x2, g, target)


def _mix_fwd(attn, ys, g_attn, g_ssm):
    l, w = attn.shape

    def body(a_ref, y_ref, ga_ref, gs_ref, o_ref):
        for src, gr, off in ((a_ref, ga_ref, 0), (y_ref, gs_ref, w)):
            xv = src[...]
            r = lax.rsqrt(jnp.mean(xv * xv, axis=-1, keepdims=True) + EPS)
            o_ref[:, off:off + w] = (xv * r * gr[...]).astype(BF16)

    row = pl.BlockSpec((TM_EW, w), lambda i: (i, 0))
    vec = pl.BlockSpec((1, w), lambda i: (0, 0))
    return pl.pallas_call(
        body, grid=(l // TM_EW,), in_specs=[row, row, vec, vec],
        out_specs=pl.BlockSpec((TM_EW, 2 * w), lambda i: (i, 0)),
        out_shape=jax.ShapeDtypeStruct((l, 2 * w), BF16), name="mix_fwd",
        compiler_params=_params(("parallel",)),
    )(attn, ys, g_attn, g_ssm)


def _mix_bwd(attn, ys, g_attn, g_ssm, dmixed):
    l, w = attn.shape

    def body(a_ref, y_ref, ga_ref, gs_ref, dm_ref, da_ref, dy_ref, dga_ref, dgs_ref):
        @pl.when(pl.program_id(0) == 0)
        def _():
            dga_ref[...] = jnp.zeros_like(dga_ref)
            dgs_ref[...] = jnp.zeros_like(dgs_ref)

        for src, gr, off, dst, dgr in ((a_ref, ga_ref, 0, da_ref, dga_ref), (y_ref, gs_ref, w, dy_ref, dgs_ref)):
            dx, dg_rows = _rms_bwd_vals(src[...], gr[...], dm_ref[:, off:off + w])
            dst[...] = dx
            dgr[...] += jnp.sum(dg_rows, axis=0, keepdims=True)

    row = pl.BlockSpec((TM_EW, w), lambda i: (i, 0))
    vec = pl.BlockSpec((1, w), lambda i: (0, 0))
    return pl.pallas_call(
        body, grid=(l // TM_EW,),
        in_specs=[row, row, vec, vec, pl.BlockSpec((TM_EW, 2 * w), lambda i: (i, 0))],
        out_specs=[row, row, vec, vec],
        out_shape=[jax.ShapeDtypeStruct((l, w), F32), jax.ShapeDtypeStruct((l, w), F32),
                   jax.ShapeDtypeStruct((1, w), F32), jax.ShapeDtypeStruct((1, w), F32)],
        name="mix_bwd", compiler_params=_params(("arbitrary",)),
    )(attn, ys, g_attn, g_ssm, dmixed)


def _rope_tables(l):
    half = ROPE_DIM // 2
    inv_freq = jnp.power(ROPE_THETA, -jnp.arange(half, dtype=F32) / half)
    ang = jnp.arange(l, dtype=F32)[:, None] * inv_freq[None, :]
    cos, sin = jnp.cos(ang), jnp.sin(ang)
    ones = jnp.ones((l, HEAD_DIM - ROPE_DIM), F32)
    zeros = jnp.zeros((l, HEAD_DIM - ROPE_DIM), F32)
    zh = jnp.zeros((l, half), F32)
    c = jnp.concatenate([cos, cos, ones], axis=1)
    s_lo = jnp.concatenate([-sin, zh, zeros], axis=1)
    s_hi = jnp.concatenate([zh, sin, zeros], axis=1)
    return tuple(jnp.tile(t, (1, LANES // HEAD_DIM)) for t in (c, s_lo, s_hi))


def _rope_fwd(proj, tabs):
    l = proj.shape[0]
    nq = ATTN_WIDTH // LANES

    def body(p_ref, c_ref, lo_ref, hi_ref, o_ref):
        c, lo, hi = c_ref[...], lo_ref[...], hi_ref[...]
        for blk in range(nq + 1):
            t = p_ref[:, blk * LANES:(blk + 1) * LANES]
            rot = t * c + pltpu.roll(t, LANES - 8, 1) * lo + pltpu.roll(t, 8, 1) * hi
            o_ref[:, blk * LANES:(blk + 1) * LANES] = rot.astype(BF16)
        o_ref[:, (nq + 1) * LANES:] = p_ref[:, (nq + 1) * LANES:].astype(BF16)

    tab = pl.BlockSpec((TM_EW, LANES), lambda i: (i, 0))
    return pl.pallas_call(
        body, grid=(l // TM_EW,),
        in_specs=[pl.BlockSpec((TM_EW, QKV_WIDTH), lambda i: (i, 0)), tab, tab, tab],
        out_specs=pl.BlockSpec((TM_EW, QKV_WIDTH), lambda i: (i, 0)),
        out_shape=jax.ShapeDtypeStruct((l, QKV_WIDTH), BF16), name="rope_fwd",
        compiler_params=_params(("parallel",)),
    )(proj, *tabs)


def _rope_bwd(dqkv, du, tabs):
    l = dqkv.shape[0]
    nq = ATTN_WIDTH // LANES

    def body(d_ref, du_ref, c_ref, lo_ref, hi_ref, o_ref):
        c, lo, hi = c_ref[...], lo_ref[...], hi_ref[...]
        for blk in range(nq + 1):
            t = d_ref[:, blk * LANES:(blk + 1) * LANES]
            g = t * c + pltpu.roll(t * lo, 8, 1) + pltpu.roll(t * hi, LANES - 8, 1)
            o_ref[:, blk * LANES:(blk + 1) * LANES] = g.astype(BF16)
        o_ref[:, (nq + 1) * LANES:QKV_WIDTH] = d_ref[:, (nq + 1) * LANES:].astype(BF16)
        o_ref[:, QKV_WIDTH:] = du_ref[...].astype(BF16)

    tab = pl.BlockSpec((TM_EW, LANES), lambda i: (i, 0))
    return pl.pallas_call(
        body, grid=(l // TM_EW,),
        in_specs=[pl.BlockSpec((TM_EW, QKV_WIDTH), lambda i: (i, 0)),
                  pl.BlockSpec((TM_EW, SSM_WIDTH), lambda i: (i, 0)), tab, tab, tab],
        out_specs=pl.BlockSpec((TM_EW, IN_WIDTH), lambda i: (i, 0)),
        out_shape=jax.ShapeDtypeStruct((l, IN_WIDTH), BF16), name="rope_bwd",
        compiler_params=_params(("parallel",)),
    )(dqkv, du, *tabs)


_Q_COLS = ATTN_WIDTH // LANES
_SCALE = HEAD_DIM ** -0.5
_NEG = -1e30


def _window_specs(nb, width, col):
    return [
        pl.BlockSpec((BLOCK, width), lambda n: (jnp.maximum(n - 1, 0), col)),
        pl.BlockSpec((BLOCK, width), lambda n: (n, col)),
        pl.BlockSpec((BLOCK, width), lambda n: (jnp.minimum(n + 1, nb - 1), col)),
    ]


def _stacked_sink(sink_ref, heads):
    rid = lax.broadcasted_iota(jnp.int32, (len(heads) * BLOCK, 1), 0)
    sk = jnp.full(rid.shape, sink_ref[0, heads[-1]], F32)
    for g in range(len(heads) - 2, -1, -1):
        sk = jnp.where(rid < (g + 1) * BLOCK, sink_ref[0, heads[g]], sk)
    return sk


def _attn_fwd(qkv, sink):
    l = qkv.shape[0]
    nb = l // BLOCK
    grp = N_Q_HEADS // N_KV_HEADS

    def body(sink_ref, q_ref, k0, k1, k2, v0, v1, v2, o_ref, lse_ref):
        n = pl.program_id(0)
        q = q_ref[...]
        kw = jnp.concatenate([k0[...], k1[...], k2[...]], axis=0)
        vw = jnp.concatenate([v0[...], v1[...], v2[...]], axis=0)
        row = lax.broadcasted_iota(jnp.int32, (grp * BLOCK, 3 * BLOCK), 0)
        col = lax.broadcasted_iota(jnp.int32, (grp * BLOCK, 3 * BLOCK), 1)
        valid = jnp.abs(col - BLOCK - (row & (BLOCK - 1))) <= WINDOW
        valid &= jnp.logical_not((n == 0) & (col < BLOCK))
        valid &= jnp.logical_not((n == nb - 1) & (col >= 2 * BLOCK))
        for hk in range(N_KV_HEADS):
            heads = range(hk * grp, (hk + 1) * grp)
            qs = jnp.concatenate([q[:, h * HEAD_DIM:(h + 1) * HEAD_DIM] for h in heads], axis=0)
            kh = kw[:, hk * HEAD_DIM:(hk + 1) * HEAD_DIM]
            vh = vw[:, hk * HEAD_DIM:(hk + 1) * HEAD_DIM]
            s = jnp.where(valid, _dg(qs, kh, NT) * _SCALE, _NEG)
            sk = _stacked_sink(sink_ref, heads)
            m = jnp.maximum(jnp.max(s, axis=1, keepdims=True), sk)
            p = jnp.exp(s - m)
            denom = jnp.sum(p, axis=1, keepdims=True) + jnp.exp(sk - m)
            o = _dg((p / denom).astype(BF16), vh, NN)
            lse = m + jnp.log(denom)
            for g, h in enumerate(heads):
                o_ref[:, h * HEAD_DIM:(h + 1) * HEAD_DIM] = o[g * BLOCK:(g + 1) * BLOCK]
                lse_ref[:, h:h + 1] = lse[g * BLOCK:(g + 1) * BLOCK]

    return pl.pallas_call(
        body, grid=(nb,),
        in_specs=[pl.BlockSpec(memory_space=pltpu.SMEM),
                  pl.BlockSpec((BLOCK, ATTN_WIDTH), lambda n: (n, 0))]
        + _window_specs(nb, KV_WIDTH, _Q_COLS) + _window_specs(nb, KV_WIDTH, _Q_COLS + 1),
        out_specs=[pl.BlockSpec((BLOCK, ATTN_WIDTH), lambda n: (n, 0)),
                   pl.BlockSpec((BLOCK, N_Q_HEADS), lambda n: (n, 0))],
        out_shape=[jax.ShapeDtypeStruct((l, ATTN_WIDTH), F32), jax.ShapeDtypeStruct((l, N_Q_HEADS), F32)],
        name="attn_fwd", compiler_params=_params(("parallel",)),
    )(sink, qkv, qkv, qkv, qkv, qkv, qkv, qkv)


def _attn_bwd(qkv, attn, dattn, lse, sink):
    l = qkv.shape[0]
    nb = l // BLOCK
    grp = N_Q_HEADS // N_KV_HEADS

    def body(sink_ref, q0, q1, q2, k0, k1, k2, v0, v1, v2, o0, o1, o2, d0, d1, d2,
             l0, l1, l2, dqkv_ref, dsink_ref):
        n = pl.program_id(0)
        first, last = n == 0, n == nb - 1

        @pl.when(first)
        def _():
            dsink_ref[...] = jnp.zeros_like(dsink_ref)

        cat = lambda a, b, c: jnp.concatenate([a[...], b[...], c[...]], axis=0)
        qw, kw, vw = cat(q0, q1, q2), cat(k0, k1, k2), cat(v0, v1, v2)
        dow = cat(d0, d1, d2)
        prodw = cat(o0, o1, o2) * dow
        lsew = cat(l0, l1, l2)
        dob = dow.astype(BF16)
        win = 3 * BLOCK
        mid = slice(BLOCK, 2 * BLOCK)

        row = lax.broadcasted_iota(jnp.int32, (grp * BLOCK, win), 0)
        col = lax.broadcasted_iota(jnp.int32, (grp * BLOCK, win), 1)
        valid_q = jnp.abs(col - BLOCK - (row & (BLOCK - 1))) <= WINDOW
        valid_q &= jnp.logical_not(first & (col < BLOCK))
        valid_q &= jnp.logical_not(last & (col >= 2 * BLOCK))
        rowk = lax.broadcasted_iota(jnp.int32, (grp * win, BLOCK), 0)
        colk = lax.broadcasted_iota(jnp.int32, (grp * win, BLOCK), 1)
        for g in range(1, grp):
            rowk = jnp.where(rowk >= win, rowk - win, rowk)
        valid_k = jnp.abs(colk + BLOCK - rowk) <= WINDOW
        valid_k &= jnp.logical_not(first & (rowk < BLOCK))
        valid_k &= jnp.logical_not(last & (rowk >= 2 * BLOCK))

        dsink_parts = []
        for hk in range(N_KV_HEADS):
            heads = range(hk * grp, (hk + 1) * grp)
            ksl = slice(hk * HEAD_DIM, (hk + 1) * HEAD_DIM)
            hsl = [slice(h * HEAD_DIM, (h + 1) * HEAD_DIM) for h in heads]
            stack = lambda parts: jnp.concatenate(parts, axis=0)
            qws = stack([qw[:, s_] for s_ in hsl])
            dows = stack([dob[:, s_] for s_ in hsl])
            deltaws = stack([jnp.sum(prodw[:, s_], axis=1, keepdims=True) for s_ in hsl])
            lsews = stack([lsew[:, h:h + 1] for h in heads])
            of_block = lambda t: stack([t[g * win + BLOCK:g * win + 2 * BLOCK] for g in range(grp)])
            qs, dos, deltas, lses = of_block(qws), of_block(dows), of_block(deltaws), of_block(lsews)
            kh, vh = kw[:, ksl], vw[:, ksl]
            s = jnp.where(valid_q, _dg(qs, kh, NT) * _SCALE, _NEG)
            p = jnp.exp(s - lses)
            dp = _dg(dos, vh, NT)
            ds = (p * (dp - deltas) * _SCALE).astype(BF16)
            dq = _dg(ds, kh, NN)
            sink_rows = jnp.exp(_stacked_sink(sink_ref, heads) - lses) * deltas
            for g, h in enumerate(heads):
                dqkv_ref[:, hsl[g]] = dq[g * BLOCK:(g + 1) * BLOCK]
                dsink_parts.append(jnp.sum(sink_rows[g * BLOCK:(g + 1) * BLOCK], axis=0, keepdims=True))
            s2 = jnp.where(valid_k, _dg(qws, kh[mid], NT) * _SCALE, _NEG)
            p2 = jnp.exp(s2 - lsews)
            dv = _dg(p2.astype(BF16), dows, TN)
            dp2 = _dg(dows, vh[mid], NT)
            ds2 = (p2 * (dp2 - deltaws) * _SCALE).astype(BF16)
            dk = _dg(ds2, qws, TN)
            dqkv_ref[:, ATTN_WIDTH + hk * HEAD_DIM:ATTN_WIDTH + (hk + 1) * HEAD_DIM] = dk
            dqkv_ref[:, ATTN_WIDTH + KV_WIDTH + hk * HEAD_DIM:ATTN_WIDTH + KV_WIDTH + (hk + 1) * HEAD_DIM] = dv
        dsink_ref[...] -= jnp.concatenate(dsink_parts, axis=1)

    return pl.pallas_call(
        body, grid=(nb,),
        in_specs=[pl.BlockSpec(memory_space=pltpu.SMEM)]
        + _window_specs(nb, ATTN_WIDTH, 0)
        + _window_specs(nb, KV_WIDTH, _Q_COLS) + _window_specs(nb, KV_WIDTH, _Q_COLS + 1)
        + _window_specs(nb, ATTN_WIDTH, 0) + _window_specs(nb, ATTN_WIDTH, 0)
        + _window_specs(nb, N_Q_HEADS, 0),
        out_specs=[pl.BlockSpec((BLOCK, QKV_WIDTH), lambda n: (n, 0)),
                   pl.BlockSpec((1, N_Q_HEADS), lambda n: (0, 0))],
        out_shape=[jax.ShapeDtypeStruct((l, QKV_WIDTH), F32), jax.ShapeDtypeStruct((1, N_Q_HEADS), F32)],
        name="attn_bwd", compiler_params=_params(("arbitrary",)),
    )(sink, qkv, qkv, qkv, qkv, qkv, qkv, qkv, qkv, qkv, attn, attn, attn,
      dattn, dattn, dattn, lse, lse, lse)


def _ssm_disc(a_re, a_im, log_step, b_re, b_im):
    step = jnp.exp(log_step)[..., None]
    mag = jnp.exp(a_re * step)
    lb_re, lb_im = mag * jnp.cos(a_im * step), mag * jnp.sin(a_im * step)
    nr, ni = lb_re - 1.0, lb_im
    den = a_re * a_re + a_im * a_im
    f_re = ((nr * a_re + ni * a_im) / den)[..., None]
    f_im = ((ni * a_re - nr * a_im) / den)[..., None]
    return lb_re, lb_im, f_re * b_re - f_im * b_im, f_re * b_im + f_im * b_re


def _ssm_pack(lb_re, lb_im, bb_re, bb_im, c_re, c_im):
    eye = jnp.eye(SSM_CH // SSM_GROUP, dtype=F32)
    ng = SSM_CH // SSM_GROUP

    def diag_b(bb):
        t = bb.reshape(2, SSM_CB, ng, SSM_STATE, SSM_GROUP)
        return jnp.einsum('dkgpc,gh->dkgchp', t, eye).reshape(2, SSM_CB, SSM_CH, SSM_ST)

    def diag_c(cc):
        t = cc.reshape(2, SSM_CB, ng, SSM_GROUP, SSM_STATE)
        return jnp.einsum('dkgcp,gh->dkhpgc', t, eye).reshape(2, SSM_CB, SSM_ST, SSM_CH)

    bcat = jnp.concatenate([diag_b(bb_re), diag_b(bb_im)], axis=-1)
    ccat = jnp.concatenate([diag_c(c_re), -diag_c(c_im)], axis=-2)
    lam_re = lb_re.reshape(2, SSM_CB, 1, SSM_ST)
    lam_im = lb_im.reshape(2, SSM_CB, 1, SSM_ST)
    return bcat, ccat, lam_re, lam_im


def _ssm_unpack(dbcat, dccat, dlam_re, dlam_im):
    ng = SSM_CH // SSM_GROUP
    eye = jnp.eye(ng, dtype=F32)

    def undiag_b(t):
        t = t.reshape(2, SSM_CB, ng, SSM_GROUP, ng, SSM_STATE)
        return jnp.einsum('dkgchp,gh->dkgpc', t, eye).reshape(2, N_SSM_GROUPS, SSM_STATE, SSM_GROUP)

    def undiag_c(t):
        t = t.reshape(2, SSM_CB, ng, SSM_STATE, ng, SSM_GROUP)
        return jnp.einsum('dkhpgc,gh->dkgcp', t, eye).reshape(2, N_SSM_GROUPS, SSM_GROUP, SSM_STATE)

    dbb_re, dbb_im = undiag_b(dbcat[..., :SSM_ST]), undiag_b(dbcat[..., SSM_ST:])
    dc_re, dc_im = undiag_c(dccat[:, :, :SSM_ST]), -undiag_c(dccat[:, :, SSM_ST:])
    shape = (2, N_SSM_GROUPS, SSM_STATE)
    return dlam_re.reshape(shape), dlam_im.reshape(shape), dbb_re, dbb_im, dc_re, dc_im


def _to_segments(t):
    l, w = t.shape
    return t.reshape(N_SEG, l // N_SEG, w).transpose(1, 0, 2).reshape(l, w)


def _from_segments(t):
    l, w = t.shape
    return t.reshape(l // N_SEG, N_SEG, w).transpose(1, 0, 2).reshape(l, w)


def _cfma(ar, ai, xr, xi, br, bi):
    return ar * xr - ai * xi + br, ar * xi + ai * xr + bi


def _scan_segments(xs_ref, ar, ai, rev, nj, prev_ref=None):
    shape = (N_SEG, SSM_ST)
    ar = jnp.broadcast_to(ar, shape)
    ai = jnp.broadcast_to(ai, shape)
    zero = jnp.zeros(shape, F32)
    re_cols, im_cols = pl.ds(0, SSM_ST), pl.ds(SSM_ST, SSM_ST)

    def rows_of(jj):
        j = jnp.where(rev, nj - 1 - jj, jj)
        return j, pl.ds(pl.multiple_of(j * N_SEG, N_SEG), N_SEG)

    def pass1(jj, carry):
        _, rows = rows_of(jj)
        return _cfma(ar, ai, carry[0], carry[1], xs_ref[rows, re_cols], xs_ref[rows, im_cols])

    end_r, end_i = lax.fori_loop(0, nj, pass1, (zero, zero))

    pr, pi = ar, ai
    for _ in range(int(math.log2(nj))):
        pr, pi = pr * pr - pi * pi, 2.0 * pr * pi
    seg = lax.broadcasted_iota(jnp.int32, shape, 0)

    def chain(shift, keep):
        ir, ii = zero, zero
        for _ in range(N_SEG - 1):
            tr, ti = _cfma(pr, pi, ir, ii, end_r, end_i)
            ir = jnp.where(keep, pltpu.roll(tr, shift, 0), 0.0)
            ii = jnp.where(keep, pltpu.roll(ti, shift, 0), 0.0)
        return ir, ii

    up_r, up_i = chain(1, seg >= 1)
    dn_r, dn_i = chain(N_SEG - 1, seg <= N_SEG - 2)
    init_r, init_i = jnp.where(rev, dn_r, up_r), jnp.where(rev, dn_i, up_i)

    def pass2(jj, carry):
        j, rows = rows_of(jj)
        nr, ni = _cfma(ar, ai, carry[0], carry[1], xs_ref[rows, re_cols], xs_ref[rows, im_cols])
        xs_ref[rows, re_cols] = nr
        xs_ref[rows, im_cols] = ni
        if prev_ref is None:
            return nr, ni
        jp = jnp.where(rev, j - 1, j + 1)
        inside = (jp >= 0) & (jp < nj)
        prow = pl.ds(pl.multiple_of(jnp.clip(jp, 0, nj - 1) * N_SEG, N_SEG), N_SEG)
        xr, xi = prev_ref[prow, re_cols], prev_ref[prow, im_cols]
        f = jnp.where(inside, 1.0, 0.0)
        return nr, ni, carry[2] + f * (nr * xr + ni * xi), carry[3] + f * (ni * xr - nr * xi)

    if prev_ref is None:
        lax.fori_loop(0, nj, pass2, (init_r, init_i))
        return init_r, init_i, None, None
    _, _, acc_r, acc_i = lax.fori_loop(0, nj, pass2, (init_r, init_i, zero, zero))
    return init_r, init_i, acc_r, acc_i


SSM_RC = 256


def _ssm_specs(l):
    act = pl.BlockSpec((l, SSM_CH), lambda k, d: (0, k))
    bmat = pl.BlockSpec((None, None, SSM_CH, 2 * SSM_ST), lambda k, d: (d, k, 0, 0))
    cmat = pl.BlockSpec((None, None, 2 * SSM_ST, SSM_CH), lambda k, d: (d, k, 0, 0))
    lam = pl.BlockSpec((None, None, 1, SSM_ST), lambda k, d: (d, k, 0, 0))
    return act, bmat, cmat, lam


def _ssm_fwd(u_seg, bcat, ccat, lam_re, lam_im):
    l = u_seg.shape[0]
    nj = l // N_SEG

    def body(u_ref, b_ref, c_ref, lr_ref, li_ref, y_ref, xs_ref):
        d = pl.program_id(1)

        def bu_chunk(i, _):
            rows = pl.ds(pl.multiple_of(i * SSM_RC, SSM_RC), SSM_RC)
            xs_ref[rows, :] = _dg(u_ref[rows, :], b_ref[...], NN)
            return 0

        lax.fori_loop(0, l // SSM_RC, bu_chunk, 0)
        _scan_segments(xs_ref, lr_ref[...], li_ref[...], d == 1, nj)

        def y_chunk(i, _):
            rows = pl.ds(pl.multiple_of(i * SSM_RC, SSM_RC), SSM_RC)
            yv = _dg(xs_ref[rows, :].astype(BF16), c_ref[...], NN)

            @pl.when(d == 0)
            def _():
                y_ref[rows, :] = yv

            @pl.when(d == 1)
            def _():
                y_ref[rows, :] += yv

            return 0

        lax.fori_loop(0, l // SSM_RC, y_chunk, 0)

    act, bmat, cmat, lam = _ssm_specs(l)
    return pl.pallas_call(
        body, grid=(SSM_CB, 2), in_specs=[act, bmat, cmat, lam, lam], out_specs=act,
        out_shape=jax.ShapeDtypeStruct((l, SSM_WIDTH), F32),
        scratch_shapes=[pltpu.VMEM((l, 2 * SSM_ST), F32)],
        name="ssm_fwd", compiler_params=_params(("parallel", "arbitrary"), vmem_mb=56),
    )(u_seg, bcat.astype(BF16), ccat.astype(BF16), lam_re, lam_im)


def _ssm_bwd(u_seg, dy_seg, bcat, ccat, lam_re, lam_im):
    l = u_seg.shape[0]
    nj = l // N_SEG

    def body(u_ref, dy_ref, b_ref, c_ref, lr_ref, li_ref,
             du_ref, db_ref, dc_ref, dlr_ref, dli_ref, xs_ref, gs_ref):
        d = pl.program_id(1)
        rev = d == 1
        ar, ai = lr_ref[...], li_ref[...]

        def chunk1(i, _):
            rows = pl.ds(pl.multiple_of(i * SSM_RC, SSM_RC), SSM_RC)
            xs_ref[rows, :] = _dg(u_ref[rows, :], b_ref[...], NN)
            gs_ref[rows, :] = _dg(dy_ref[rows, :], c_ref[...], NT)
            return 0

        lax.fori_loop(0, l // SSM_RC, chunk1, 0)
        init_r, init_i, _, _ = _scan_segments(xs_ref, ar, ai, rev, nj)
        _, _, acc_r, acc_i = _scan_segments(gs_ref, ar, -ai, jnp.logical_not(rev), nj, prev_ref=xs_ref)
        jb = jnp.where(rev, nj - 1, 0)
        brow = pl.ds(pl.multiple_of(jb * N_SEG, N_SEG), N_SEG)
        gr, gi = gs_ref[brow, pl.ds(0, SSM_ST)], gs_ref[brow, pl.ds(SSM_ST, SSM_ST)]
        acc_r = acc_r + gr * init_r + gi * init_i
        acc_i = acc_i + gi * init_r - gr * init_i
        dlr_ref[...] = jnp.sum(acc_r, axis=0, keepdims=True)
        dli_ref[...] = jnp.sum(acc_i, axis=0, keepdims=True)

        db_ref[...] = jnp.zeros_like(db_ref)
        dc_ref[...] = jnp.zeros_like(dc_ref)

        def chunk2(i, _):
            rows = pl.ds(pl.multiple_of(i * SSM_RC, SSM_RC), SSM_RC)
            g = gs_ref[rows, :].astype(BF16)
            dc_ref[...] += _dg(xs_ref[rows, :].astype(BF16), dy_ref[rows, :], TN)
            db_ref[...] += _dg(u_ref[rows, :], g, TN)
            duv = _dg(g, b_ref[...], NT)

            @pl.when(d == 0)
            def _():
                du_ref[rows, :] = duv

            @pl.when(d == 1)
            def _():
                du_ref[rows, :] += duv

            return 0

        lax.fori_loop(0, l // SSM_RC, chunk2, 0)

    act, bmat, cmat, lam = _ssm_specs(l)
    return pl.pallas_call(
        body, grid=(SSM_CB, 2), in_specs=[act, act, bmat, cmat, lam, lam],
        out_specs=[act, bmat, cmat, lam, lam],
        out_shape=[jax.ShapeDtypeStruct((l, SSM_WIDTH), F32),
                   jax.ShapeDtypeStruct(bcat.shape, F32), jax.ShapeDtypeStruct(ccat.shape, F32),
                   jax.ShapeDtypeStruct(lam_re.shape, F32), jax.ShapeDtypeStruct(lam_im.shape, F32)],
        scratch_shapes=[pltpu.VMEM((l, 2 * SSM_ST), F32), pltpu.VMEM((l, 2 * SSM_ST), F32)],
        name="ssm_bwd", compiler_params=_params(("parallel", "arbitrary"), vmem_mb=60),
    )(u_seg, dy_seg, bcat.astype(BF16), ccat.astype(BF16), lam_re, lam_im)


def _glu_fwd(y_ssm, u, d_skip, w_glu):
    l, w = u.shape

    def body(y_ref, u_ref, d_ref, w_ref, pre_ref, s_ref, ys_ref):
        pre = y_ref[...] + d_ref[...] * u_ref[...]
        z = _gelu(pre)
        s = _dg(z.astype(BF16), w_ref[...], NN)
        pre_ref[...] = pre
        s_ref[...] = s
        ys_ref[...] = z * _sigmoid(s)

    row = pl.BlockSpec((TM_EW, w), lambda i: (i, 0))
    out = jax.ShapeDtypeStruct((l, w), F32)
    return pl.pallas_call(
        body, grid=(l // TM_EW,),
        in_specs=[row, row, pl.BlockSpec((1, w), lambda i: (0, 0)), pl.BlockSpec((w, w), lambda i: (0, 0))],
        out_specs=[row, row, row], out_shape=[out, out, out], name="glu_fwd",
        compiler_params=_params(("parallel",)),
    )(y_ssm, u, d_skip, w_glu)


def _glu_bwd(pre, s, dys, u, d_skip, w_glu):
    l, w = u.shape

    def body(pre_ref, s_ref, dys_ref, u_ref, d_ref, w_ref, dpre_ref, z_ref, ds_ref, dd_ref):
        pre, dys = pre_ref[...], dys_ref[...]
        z = _gelu(pre)
        sig = _sigmoid(s_ref[...])
        ds = (dys * z * sig * (1.0 - sig)).astype(BF16)
        dz = dys * sig + _dg(ds, w_ref[...], NT)
        dpre = dz * _gelu_grad(pre)
        dpre_ref[...] = dpre
        z_ref[...] = z.astype(BF16)
        ds_ref[...] = ds

        @pl.when(pl.program_id(0) == 0)
        def _():
            dd_ref[...] = jnp.zeros_like(dd_ref)

        dd_ref[...] += jnp.sum(dpre * u_ref[...], axis=0, keepdims=True)

    row = pl.BlockSpec((TM_EW, w), lambda i: (i, 0))
    vec = pl.BlockSpec((1, w), lambda i: (0, 0))
    return pl.pallas_call(
        body, grid=(l // TM_EW,),
        in_specs=[row, row, row, row, vec, pl.BlockSpec((w, w), lambda i: (0, 0))],
        out_specs=[row, row, row, vec],
        out_shape=[jax.ShapeDtypeStruct((l, w), F32), jax.ShapeDtypeStruct((l, w), BF16),
                   jax.ShapeDtypeStruct((l, w), BF16), jax.ShapeDtypeStruct((1, w), F32)],
        name="glu_bwd", compiler_params=_params(("arbitrary",)),
    )(pre, s, dys, u, d_skip, w_glu)


TM_CV = 512
TC_CV = 256
HALO = SUBLANES


def _conv_specs(l, col0):
    per = TM_CV // HALO
    nh = l // HALO
    off = col0 // TC_CV
    return [
        pl.BlockSpec((HALO, TC_CV), lambda j, i: (jnp.maximum(i * per - 1, 0), j + off)),
        pl.BlockSpec((TM_CV, TC_CV), lambda j, i: (i, j + off)),
        pl.BlockSpec((HALO, TC_CV), lambda j, i: (jnp.minimum((i + 1) * per, nh - 1), j + off)),
    ]


def _ext(prev_ref, mid_ref, next_ref, first, last):
    p = jnp.where(first, 0.0, prev_ref[...])
    n = jnp.where(last, 0.0, next_ref[...])
    return jnp.concatenate([p, mid_ref[...], n], axis=0)


def _shift_dn(t):
    return pltpu.roll(t, 1, 0)


def _shift_up(t):
    return pltpu.roll(t, t.shape[0] - 1, 0)


def _conv3(e, w_ref, b_ref):
    return w_ref[0:1, :] * _shift_dn(e) + w_ref[1:2, :] * e + w_ref[2:3, :] * _shift_up(e) + b_ref[...]


def _convffn_fwd(up_pre, conv_w, conv_b):
    l = up_pre.shape[0]
    ni = l // TM_CV
    wspec = lambda off: pl.BlockSpec((3, TC_CV), lambda j, i: (0, j + off))
    bspec = lambda off: pl.BlockSpec((1, TC_CV), lambda j, i: (0, j + off))
    voff = D_FF // TC_CV

    def body(gp, gm, gn, vp, vm, vn, wg, bg, wv, bv, o_ref):
        i = pl.program_id(1)
        first, last = i == 0, i == ni - 1
        gate = _conv3(_ext(gp, gm, gn, first, last), wg, bg)[HALO:HALO + TM_CV]
        val = _conv3(_ext(vp, vm, vn, first, last), wv, bv)[HALO:HALO + TM_CV]
        o_ref[...] = (gate * _sigmoid(gate) * val).astype(BF16)

    return pl.pallas_call(
        body, grid=(D_FF // TC_CV, ni),
        in_specs=_conv_specs(l, 0) + _conv_specs(l, D_FF) + [wspec(0), bspec(0), wspec(voff), bspec(voff)],
        out_specs=pl.BlockSpec((TM_CV, TC_CV), lambda j, i: (i, j)),
        out_shape=jax.ShapeDtypeStruct((l, D_FF), BF16), name="convffn_fwd",
        compiler_params=_params(("parallel", "parallel")),
    )(up_pre, up_pre, up_pre, up_pre, up_pre, up_pre, conv_w, conv_b, conv_w, conv_b)


def _convffn_bwd(up_pre, dact, conv_w, conv_b):
    l = up_pre.shape[0]
    ni = l // TM_CV
    wspec = lambda off: pl.BlockSpec((3, TC_CV), lambda j, i: (0, j + off))
    bspec = lambda off: pl.BlockSpec((1, TC_CV), lambda j, i: (0, j + off))
    voff = D_FF // TC_CV

    def body(gp, gm, gn, vp, vm, vn, dp, dm, dn, wg, bg, wv, bv, dgate_ref, dval_ref, pg_ref, pv_ref):
        i = pl.program_id(1)
        first, last = i == 0, i == ni - 1
        ge, ve, de = _ext(gp, gm, gn, first, last), _ext(vp, vm, vn, first, last), _ext(dp, dm, dn, first, last)
        gate, val = _conv3(ge, wg, bg), _conv3(ve, wv, bv)
        sig = _sigmoid(gate)
        silu = gate * sig
        dgate = de * val * (sig + silu * (1.0 - sig))
        dval = de * silu
        mid = slice(HALO, HALO + TM_CV)
        rid = lax.broadcasted_iota(jnp.int32, (SUBLANES, TC_CV), 0)

        @pl.when(i == 0)
        def _():
            pg_ref[...] = jnp.zeros_like(pg_ref)
            pv_ref[...] = jnp.zeros_like(pv_ref)

        for dup, e, w_ref, out_ref, p_ref in ((dgate, ge, wg, dgate_ref, pg_ref), (dval, ve, wv, dval_ref, pv_ref)):
            dpre = w_ref[0:1, :] * _shift_up(dup) + w_ref[1:2, :] * dup + w_ref[2:3, :] * _shift_dn(dup)
            out_ref[...] = dpre[mid].astype(BF16)
            dm_ = dup[mid]
            sums = [jnp.sum(dm_ * _shift_dn(e)[mid], axis=0, keepdims=True),
                    jnp.sum(dm_ * e[mid], axis=0, keepdims=True),
                    jnp.sum(dm_ * _shift_up(e)[mid], axis=0, keepdims=True),
                    jnp.sum(dm_, axis=0, keepdims=True)]
            acc = jnp.zeros((SUBLANES, TC_CV), F32)
            for k, sk in enumerate(sums):
                acc = jnp.where(rid == k, sk, acc)
            p_ref[...] += acc

    tile = pl.BlockSpec((TM_CV, TC_CV), lambda j, i: (i, j))
    par = pl.BlockSpec((SUBLANES, TC_CV), lambda j, i: (0, j))
    dgate, dval, pg, pv = pl.pallas_call(
        body, grid=(D_FF // TC_CV, ni),
        in_specs=_conv_specs(l, 0) + _conv_specs(l, D_FF) + _conv_specs(l, 0)
        + [wspec(0), bspec(0), wspec(voff), bspec(voff)],
        out_specs=[tile, tile, par, par],
        out_shape=[jax.ShapeDtypeStruct((l, D_FF), BF16), jax.ShapeDtypeStruct((l, D_FF), BF16),
                   jax.ShapeDtypeStruct((SUBLANES, D_FF), F32), jax.ShapeDtypeStruct((SUBLANES, D_FF), F32)],
        name="convffn_bwd", compiler_params=_params(("parallel", "arbitrary")),
    )(up_pre, up_pre, up_pre, up_pre, up_pre, up_pre, dact, dact, dact, conv_w, conv_b, conv_w, conv_b)
    return jnp.concatenate([dgate, dval], axis=1), jnp.concatenate([pg, pv], axis=1)


def _local_step(x, target, wb, sp):
    l = x.shape[0]
    tabs = _rope_tables(l)
    disc = _ssm_disc(sp["a_re"], sp["a_im"], sp["log_step"], sp["b_re"], sp["b_im"])
    bcat, ccat, lam_re, lam_im = _ssm_pack(*disc, sp["c_re"], sp["c_im"])
    d_skip = sp["d_skip"].reshape(1, SSM_WIDTH)

    big = min(l, 1024)
    h = _rms_fwd(x, sp["norm_mix_g"], "rms_mix")
    proj = _mm_nn(h, wb["w_in"], big, IN_WIDTH, F32, "mm_in")
    qkv = _rope_fwd(proj, tabs)
    attn, lse = _attn_fwd(qkv, sp["sink"])
    u = proj[:, QKV_WIDTH:]
    u_seg = _to_segments(u).astype(BF16)
    y_ssm = _from_segments(_ssm_fwd(u_seg, bcat, ccat, lam_re, lam_im))
    pre, s_glu, ys = _glu_fwd(y_ssm, u, d_skip, wb["w_glu"])
    mixed = _mix_fwd(attn, ys, sp["norm_attn_g"], sp["norm_ssm_g"])
    x1 = _mm_nn(mixed, wb["w_out"], big, 1024, F32, "mm_out", res=x)
    h2 = _rms_fwd(x1, sp["norm_ffn_g"], "rms_ffn")
    up_pre = _mm_nn(h2, wb["w_up"], big, D_FF // 2, F32, "mm_up")
    act = _convffn_fwd(up_pre, sp["conv_w"], sp["conv_b"])
    x2 = _mm_nn(act, wb["w_down"], big, 512, F32, "mm_down", res=x1)
    loss, dx2, dx2b, d_final_g = _final_loss(x2, sp["norm_final_g"].reshape(1, D_MODEL), target)

    g = {"norm_final_g": d_final_g.reshape(D_MODEL)}
    dact = _mm_nt(dx2b, wb["w_down"], big, D_FF // 2, F32, "mm_down_dx")
    g["w_down"] = _mm_tn(act, dx2b, D_FF // 2, 512, "mm_down_dw")
    dup_pre, conv_par = _convffn_bwd(up_pre, dact, sp["conv_w"], sp["conv_b"])
    g["conv_w"], g["conv_b"] = conv_par[0:3], conv_par[3:4]
    g["w_up"] = _mm_tn(h2, dup_pre, 1024, 512, "mm_up_dw")
    dh2 = _mm_nt(dup_pre, wb["w_up"], 512, 1024, F32, "mm_up_dx")
    dx1, dx1b, g["norm_ffn_g"] = _rms_bwd(x1, sp["norm_ffn_g"], dh2, dx2, "rms_ffn_bwd")
    dmixed = _mm_nt(dx1b, wb["w_out"], big, 1024, F32, "mm_out_dx")
    g["w_out"] = _mm_tn(mixed, dx1b, 1024, 1024, "mm_out_dw")
    dattn, dys, g["norm_attn_g"], g["norm_ssm_g"] = _mix_bwd(attn, ys, sp["norm_attn_g"], sp["norm_ssm_g"], dmixed)
    dpre, zb, dsb, dd = _glu_bwd(pre, s_glu, dys, u, d_skip, wb["w_glu"])
    g["d_skip"] = dd.reshape(N_SSM_GROUPS, SSM_GROUP)
    g["w_glu"] = _mm_tn(zb, dsb, 512, 512, "mm_glu_dw")
    du_seg, dbcat, dccat, dlam_re, dlam_im = _ssm_bwd(u_seg, _to_segments(dpre).astype(BF16), bcat, ccat,
                                                      lam_re, lam_im)
    dlb_re, dlb_im, dbb_re, dbb_im, g["c_re"], g["c_im"] = _ssm_unpack(dbcat, dccat, dlam_re, dlam_im)
    _, disc_vjp = jax.vjp(_ssm_disc, sp["a_re"], sp["a_im"], sp["log_step"], sp["b_re"], sp["b_im"])
    g["a_re"], g["a_im"], g["log_step"], g["b_re"], g["b_im"] = disc_vjp((dlb_re, dlb_im, dbb_re, dbb_im))
    du = _from_segments(du_seg) + dpre * d_skip
    dqkv, dsink = _attn_bwd(qkv, attn, dattn, lse, sp["sink"])
    g["sink"] = dsink
    dproj = _rope_bwd(dqkv, du, tabs)
    g["w_in"] = _mm_tn(h, dproj, 512, IN_WIDTH, "mm_in_dw")
    dh = _mm_nt(dproj, wb["w_in"], 512, 512, F32, "mm_in_dx")
    grad_x, _, g["norm_mix_g"] = _rms_bwd(x, sp["norm_mix_g"], dh, dx1, "rms_mix_bwd")
    return loss, grad_x, g


MESH = pl.DeviceIdType.MESH
ANY = pl.BlockSpec(memory_space=pl.ANY)


def _place():
    x, y, c = lax.axis_index("x"), lax.axis_index("y"), lax.axis_index("c")
    chips = [(1 - x, y), (x, 1 - y), (1 - x, 1 - y)]
    return x, y, c, chips


def _chip_index(px, py):
    return 2 * px + py


CHUNK_BYTES = 256 * 1024
MAX_CHUNKS = 16


def _row_chunks(rows, row_bytes, align):
    n = max(1, min(MAX_CHUNKS, (rows * row_bytes) // CHUNK_BYTES))
    per = -(-rows // n)
    per = -(-per // align) * align
    return [(r0, min(per, rows - r0)) for r0 in range(0, rows, per)]


def _align_of(dtype):
    return SUBLANES * 4 // jnp.dtype(dtype).itemsize


def _remote(src, dst, send_sem, recv_sem, to):
    return pltpu.make_async_remote_copy(src_ref=src, dst_ref=dst, send_sem=send_sem, recv_sem=recv_sem,
                                        device_id=to, device_id_type=MESH)


def _cast_bf16(w, name):
    r, c = w.shape
    tr = r if r <= 512 else r // 2

    def body(w_ref, o_ref):
        o_ref[...] = w_ref[...].astype(BF16)

    spec = pl.BlockSpec((tr, c), lambda i: (i, 0))
    return pl.pallas_call(body, grid=(r // tr,), in_specs=[spec], out_specs=spec,
                          out_shape=jax.ShapeDtypeStruct((r, c), BF16), name=name,
                          compiler_params=_params(("parallel",)))(w)


def _gather_weights(shards):
    nw = len(shards)

    def body(*refs):
        w_refs, o_refs = refs[:nw], refs[nw:2 * nw]
        send_sems, recv_sems = refs[2 * nw:]
        x, y, c, chips = _place()
        mine = _chip_index(x, y)
        sibling = (x, y, 1 - c)

        def rows_of(ref, chip, r0, nr):
            return ref.at[chip, pl.ds(r0, nr), :]

        def copy(wi, k, src, dst, to):
            return _remote(src, dst, send_sems.at[wi, k], recv_sems.at[wi, k], to)

        geo = []
        for wi in range(nw):
            rows, cols = w_refs[wi].shape
            row_bytes = cols * jnp.dtype(w_refs[wi].dtype).itemsize
            geo.append((rows // 2, _row_chunks(rows // 2, row_bytes, _align_of(w_refs[wi].dtype))))

        for wi in range(nw):
            hr, half_chunks = geo[wi]
            for j, chip in enumerate(chips):
                for r0, nr in half_chunks:
                    copy(wi, j, w_refs[wi].at[pl.ds(c * hr + r0, nr), :],
                         rows_of(o_refs[wi], mine, c * hr + r0, nr), (*chip, c)).start()
        for wi in range(nw):
            hr, half_chunks = geo[wi]
            for j, chip in enumerate(chips):
                got = rows_of(o_refs[wi], _chip_index(*chip), c * hr, hr)
                copy(wi, j, got, got, (*chip, c)).wait_recv()
                for r0, nr in half_chunks:
                    piece = rows_of(o_refs[wi], _chip_index(*chip), c * hr + r0, nr)
                    copy(wi, 3 + j, piece, piece, sibling).start()
        for wi in range(nw):
            hr = geo[wi][0]
            for j, chip in enumerate(chips):
                got = rows_of(o_refs[wi], _chip_index(*chip), (1 - c) * hr, hr)
                copy(wi, 3 + j, got, got, sibling).wait_recv()
        for wi in range(nw):
            hr = geo[wi][0]
            sent = rows_of(o_refs[wi], mine, c * hr, hr)
            for k in range(6):
                copy(wi, k, sent, sent, sibling).wait_send()

    outs = pl.pallas_call(
        body, in_specs=[ANY] * nw, out_specs=[ANY] * nw,
        out_shape=[jax.ShapeDtypeStruct((4, *s.shape), s.dtype) for s in shards],
        scratch_shapes=[pltpu.SemaphoreType.DMA((nw, 6)), pltpu.SemaphoreType.DMA((nw, 6))],
        name="gather_weights",
    )(*shards)
    mine = _chip_index(lax.axis_index("x"), lax.axis_index("y"))
    return [lax.dynamic_update_slice(o, s[None], (mine, 0, 0)) for o, s in zip(outs, shards)]


def _pair_exchange(grads):
    na = len(grads)

    def body(*refs):
        g_refs, o_refs = refs[:na], refs[na:2 * na]
        send_sems, recv_sems = refs[2 * na:]
        x, y, c, _ = _place()
        sibling = (x, y, 1 - c)
        for ai in range(na):
            _, rows, cols = g_refs[ai].shape
            hr = rows // 2
            for k in range(4):
                for r0, nr in _row_chunks(hr, cols * 4, SUBLANES):
                    _remote(g_refs[ai].at[k, pl.ds((1 - c) * hr + r0, nr), :], o_refs[ai].at[k, pl.ds(r0, nr), :],
                            send_sems.at[ai], recv_sems.at[ai], sibling).start()
        for ai in range(na):
            _remote(o_refs[ai], o_refs[ai], send_sems.at[ai], recv_sems.at[ai], sibling).wait()

    return pl.pallas_call(
        body, in_specs=[ANY] * na, out_specs=[ANY] * na,
        out_shape=[jax.ShapeDtypeStruct((4, g.shape[1] // 2, g.shape[2]), F32) for g in grads],
        scratch_shapes=[pltpu.SemaphoreType.DMA((na,)), pltpu.SemaphoreType.DMA((na,))],
        name="pair_exchange",
    )(*grads)


def _row_tile(rows, cols, align):
    best = align
    for cand in range(align, rows + 1, align):
        if rows % cand == 0 and cand * cols <= 256 * 1024:
            best = cand
    return best


def _pair_sum(g, got, place, transit, name):
    _, rows, cols = g.shape
    hr = rows // 2
    tr = _row_tile(hr, cols, _align_of(transit))
    nt = hr // tr

    def body(p_ref, g_ref, r_ref, s_ref, own_ref):
        total = g_ref[...] + r_ref[...]
        s_ref[...] = total.astype(transit)

        @pl.when(pl.program_id(1) == p_ref[1])
        def _():
            own_ref[...] = total

    grid_spec = pltpu.PrefetchScalarGridSpec(
        num_scalar_prefetch=1, grid=(nt, 4),
        in_specs=[pl.BlockSpec((None, tr, cols), lambda i, k, p: (k, p[0] * nt + i, 0)),
                  pl.BlockSpec((None, tr, cols), lambda i, k, p: (k, i, 0))],
        out_specs=[pl.BlockSpec((None, tr, cols), lambda i, k, p: (k, i, 0)),
                   pl.BlockSpec((tr, cols), lambda i, k, p: (i, 0))])
    return pl.pallas_call(
        body, grid_spec=grid_spec,
        out_shape=[jax.ShapeDtypeStruct((4, hr, cols), transit), jax.ShapeDtypeStruct((hr, cols), F32)],
        name=name, compiler_params=_params(("parallel", "arbitrary")),
    )(place, g, got)


def _chip_exchange(sums):
    na = len(sums)

    def body(*refs):
        s_refs, o_refs = refs[:na], refs[na:2 * na]
        send_sems, recv_sems = refs[2 * na:]
        x, y, c, chips = _place()
        for ai in range(na):
            _, rows, cols = s_refs[ai].shape
            row_bytes = cols * jnp.dtype(s_refs[ai].dtype).itemsize
            for r0, nr in _row_chunks(rows, row_bytes, _align_of(s_refs[ai].dtype)):
                for j, chip in enumerate(chips):
                    _remote(s_refs[ai].at[_chip_index(*chip), pl.ds(r0, nr), :], o_refs[ai].at[j, pl.ds(r0, nr), :],
                            send_sems.at[ai, j], recv_sems.at[ai, j], (*chip, c)).start()
        for ai in range(na):
            for j, chip in enumerate(chips):
                _remote(o_refs[ai].at[j], o_refs[ai].at[j], send_sems.at[ai, j], recv_sems.at[ai, j],
                        (*chip, c)).wait()

    return pl.pallas_call(
        body, in_specs=[ANY] * na, out_specs=[ANY] * na,
        out_shape=[jax.ShapeDtypeStruct((3, *s.shape[1:]), s.dtype) for s in sums],
        scratch_shapes=[pltpu.SemaphoreType.DMA((na, 3)), pltpu.SemaphoreType.DMA((na, 3))],
        name="chip_exchange",
    )(*sums)


def _chip_sum(own, landed, name):
    hr, cols = own.shape
    tr = _row_tile(hr, cols, _align_of(landed.dtype))

    def body(o_ref, l_ref, f_ref):
        acc = o_ref[...]
        for j in range(3):
            acc = acc + l_ref[j].astype(F32)
        f_ref[...] = acc

    return pl.pallas_call(
        body, grid=(hr // tr,),
        in_specs=[pl.BlockSpec((tr, cols), lambda i: (i, 0)), pl.BlockSpec((3, tr, cols), lambda i: (0, i, 0))],
        out_specs=pl.BlockSpec((tr, cols), lambda i: (i, 0)),
        out_shape=jax.ShapeDtypeStruct((hr, cols), F32), name=name,
        compiler_params=_params(("parallel",)),
    )(own, landed)


def _final_exchange(halves, small):
    nh = len(halves)

    def body(*refs):
        h_refs, s_ref = refs[:nh], refs[nh]
        o_refs, so_ref = refs[nh + 1:2 * nh + 1], refs[2 * nh + 1]
        send_sems, recv_sems, local_sem, ssend_sems, srecv_sems = refs[2 * nh + 2:]
        x, y, c, _ = _place()
        me = 4 * x + 2 * y + c
        sibling = (x, y, 1 - c)
        for hi in range(nh):
            hr, cols = h_refs[hi].shape
            for r0, nr in _row_chunks(hr, cols * 4, SUBLANES):
                _remote(h_refs[hi].at[pl.ds(r0, nr), :], o_refs[hi].at[pl.ds(r0, nr), :],
                        send_sems.at[hi], recv_sems.at[hi], sibling).start()
        small_cps = [pltpu.make_async_copy(s_ref, so_ref.at[me], local_sem)]
        for r in range(1, 8):
            fx, fy, fc = (r >> 2) & 1, (r >> 1) & 1, r & 1
            peer = (1 - x if fx else x, 1 - y if fy else y, 1 - c if fc else c)
            small_cps.append(_remote(s_ref, so_ref.at[me], ssend_sems.at[r - 1], srecv_sems.at[r - 1], peer))
        for cp in small_cps:
            cp.start()
        for hi in range(nh):
            _remote(h_refs[hi], o_refs[hi], send_sems.at[hi], recv_sems.at[hi], sibling).wait()
        for cp in small_cps:
            cp.wait()

    return pl.pallas_call(
        body, in_specs=[ANY] * (nh + 1), out_specs=[ANY] * (nh + 1),
        out_shape=[jax.ShapeDtypeStruct(h.shape, F32) for h in halves]
        + [jax.ShapeDtypeStruct((8, *small.shape), F32)],
        scratch_shapes=[pltpu.SemaphoreType.DMA((nh,)), pltpu.SemaphoreType.DMA((nh,)),
                        pltpu.SemaphoreType.DMA, pltpu.SemaphoreType.DMA((7,)), pltpu.SemaphoreType.DMA((7,))],
        name="final_exchange",
    )(*halves, small)


def _adamw(w, g, m, v, name):
    shape = w.shape
    n = w.size
    if w.ndim >= 2 and shape[-1] >= LANES:
        two_d = (n // shape[-1], shape[-1])
    elif n % LANES == 0:
        two_d = (n // LANES, LANES)
    else:
        two_d = (1, n)
    r, c = two_d
    tr = r
    for cand in (512, 256, 176, 128, 64):
        if r > cand and r % cand == 0 and cand * c <= 256 * 1024:
            tr = cand
            break
    c1 = 1.0 - ADAM_B1 ** ADAM_STEP
    c2 = 1.0 - ADAM_B2 ** ADAM_STEP

    def body(w_ref, g_ref, m_ref, v_ref, d_ref, nm_ref, nv_ref):
        gv = g_ref[...]
        nm = ADAM_B1 * m_ref[...] + (1.0 - ADAM_B1) * gv
        nv = ADAM_B2 * v_ref[...] + (1.0 - ADAM_B2) * (gv * gv)
        d_ref[...] = -ADAM_LR * ((nm / c1) / (jnp.sqrt(nv / c2) + ADAM_EPS) + ADAM_WD * w_ref[...])
        nm_ref[...] = nm
        nv_ref[...] = nv

    spec = pl.BlockSpec((tr, c), lambda i: (i, 0))
    out = jax.ShapeDtypeStruct((r, c), F32)
    d, nm, nv = pl.pallas_call(
        body, grid=(r // tr,), in_specs=[spec] * 4, out_specs=[spec] * 3, out_shape=[out] * 3, name=name,
        compiler_params=_params(("parallel",)),
    )(w.reshape(two_d), g.reshape(two_d), m.reshape(two_d), v.reshape(two_d))
    return d.reshape(shape), nm.reshape(shape), nv.reshape(shape)


BIG = ("w_in", "w_glu", "w_out", "w_up", "w_down")
WEIGHTS = ("norm_mix_g", "w_in", "a_re", "a_im", "log_step", "b_re", "b_im", "c_re", "c_im", "d_skip", "w_glu",
           "sink", "norm_attn_g", "norm_ssm_g", "w_out", "norm_ffn_g", "w_up", "conv_w", "conv_b", "w_down",
           "norm_final_g")
SMALL = ("norm_mix_g", "a_re", "a_im", "log_step", "b_re", "b_im", "c_re", "c_im", "d_skip", "sink",
         "norm_attn_g", "norm_ssm_g", "norm_ffn_g", "conv_w", "conv_b", "norm_final_g")
SMALL_ROWS = 40
N_DEV = 8


def _full_matrices(gathered):
    w_in, w_glu, w_out, w_up, w_down = gathered
    cols = lambda t: t.transpose(1, 0, 2).reshape(t.shape[1], 4 * t.shape[2])
    rows = lambda t: t.reshape(4 * t.shape[1], t.shape[2])
    return {"w_in": cols(w_in), "w_glu": rows(w_glu), "w_out": rows(w_out), "w_up": cols(w_up),
            "w_down": rows(w_down)}


def _by_owner(name, g):
    if name in ("w_in", "w_up"):
        return g.reshape(g.shape[0], 4, g.shape[1] // 4).transpose(1, 0, 2)
    return g.reshape(4, g.shape[0] // 4, g.shape[1])


def kernel(x, norm_mix_g, w_in, a_re, a_im, log_step, b_re, b_im, c_re, c_im, d_skip, w_glu, sink, norm_attn_g, norm_ssm_g, w_out, norm_ffn_g, w_up, conv_w, conv_b, w_down, norm_final_g, loss_target, m_norm_mix_g, m_w_in, m_a_re, m_a_im, m_log_step, m_b_re, m_b_im, m_c_re, m_c_im, m_d_skip, m_w_glu, m_sink, m_norm_attn_g, m_norm_ssm_g, m_w_out, m_norm_ffn_g, m_w_up, m_conv_w, m_conv_b, m_w_down, m_norm_final_g, v_norm_mix_g, v_w_in, v_a_re, v_a_im, v_log_step, v_b_re, v_b_im, v_c_re, v_c_im, v_d_skip, v_w_glu, v_sink, v_norm_attn_g, v_norm_ssm_g, v_w_out, v_norm_ffn_g, v_w_up, v_conv_w, v_conv_b, v_w_down, v_norm_final_g):
    given = dict(locals())
    w = {n: given[n] for n in WEIGHTS}
    m = {n: given["m_" + n] for n in WEIGHTS}
    v = {n: given["v_" + n] for n in WEIGHTS}
    xy = 2 * lax.axis_index("x") + lax.axis_index("y")

    shards = [_cast_bf16(w[n][0], "cast_" + n) for n in BIG]
    conv_rows = jnp.pad(w["conv_w"][0], ((0, 2 * SUBLANES - 3), (0, 0)))
    *gathered, conv_all = _gather_weights(shards + [conv_rows])
    wb = _full_matrices(gathered)

    sp = {n: w[n][0] for n in ("a_re", "a_im", "log_step", "b_re", "b_im", "c_re", "c_im", "d_skip",
                               "norm_mix_g", "norm_attn_g", "norm_ssm_g", "norm_ffn_g", "sink", "conv_b")}
    for n in ("norm_mix_g", "norm_attn_g", "norm_ssm_g", "norm_ffn_g", "sink", "conv_b"):
        sp[n] = sp[n].reshape(1, -1)
    sp["conv_w"] = conv_all[:, :3].transpose(1, 0, 2).reshape(3, 2 * D_FF)
    sp["norm_final_g"] = w["norm_final_g"]
    loss, grad_x, g = _local_step(x[0], loss_target[0], wb, sp)

    flat = jnp.concatenate([g[n].reshape(-1) for n in SMALL])
    pad = N_DEV * SMALL_ROWS * D_MODEL - flat.shape[0]
    small = jnp.concatenate([flat, jnp.zeros((pad,), F32)]).reshape(4, 2 * SMALL_ROWS, D_MODEL)
    by_owner = [_by_owner(n, g[n]) for n in BIG] + [small]
    core = lax.axis_index("c")
    place = jnp.stack([core, xy]).astype(jnp.int32)
    got = _pair_exchange(by_owner)
    transit = [BF16] * len(BIG) + [F32]
    chip_sums, own_sums = zip(*[_pair_sum(a, b, place, t, "pair_sum_%d" % i)
                                for i, (a, b, t) in enumerate(zip(by_owner, got, transit))])
    landed = _chip_exchange(list(chip_sums))
    halves = [_chip_sum(o, t, "chip_sum_%d" % i) for i, (o, t) in enumerate(zip(own_sums, landed))]
    *others, small_all = _final_exchange(halves[:-1], halves[-1])
    grads = {n: jnp.concatenate([jnp.where(core == 0, h, o), jnp.where(core == 0, o, h)], axis=0)
             for n, h, o in zip(BIG, halves, others)}
    flat = small_all.reshape(-1)
    off = 0
    for n in SMALL:
        shape = (3, 4 * w[n].shape[-1]) if n == "conv_w" else w[n].shape[1:] if n != "norm_final_g" else w[n].shape
        size = math.prod(shape)
        grads[n] = flat[off:off + size].reshape(shape)
        off += size
    cw = w["conv_w"].shape[-1]
    grads["conv_w"] = lax.dynamic_slice_in_dim(grads["conv_w"], xy * cw, cw, axis=1)

    outs_g, outs_d, outs_m, outs_v = [], [], [], []
    for n in WEIGHTS:
        gn = grads[n].reshape(w[n].shape)
        dn, mn, vn = _adamw(w[n], gn, m[n], v[n], "adamw_" + n)
        outs_g.append(gn)
        outs_d.append(dn)
        outs_m.append(mn)
        outs_v.append(vn)
    loss = lax.psum(loss[0, 0], ("x", "y", "c"))
    return (loss, grad_x[None], *outs_g, *outs_d, *outs_m, *outs_v)
```

```python
import functools
import math

import jax
import jax.numpy as jnp
from jax import lax
from jax.experimental import pallas as pl
from jax.experimental.pallas import tpu as pltpu

F32 = jnp.float32
BF16 = jnp.bfloat16

D_MODEL = 1024
N_Q_HEADS = 8
N_KV_HEADS = 2
HEAD_DIM = 64
ATTN_WIDTH = 512
KV_WIDTH = 128
QKV_WIDTH = ATTN_WIDTH + 2 * KV_WIDTH
WINDOW = 128
BLOCK = 128
ROPE_DIM = 16
ROPE_THETA = 500000.0
SSM_WIDTH = 512
SSM_GROUP = 16
N_SSM_GROUPS = 32
SSM_STATE = 64
IN_WIDTH = 1280
D_FF = 2816
EPS = 1e-6
ADAM_LR = 0.001
ADAM_B1 = 0.9
ADAM_B2 = 0.999
ADAM_EPS = 1e-08
ADAM_WD = 0.01
ADAM_STEP = 10

VMEM_BYTES_V7X = 64 * 1024 * 1024
SUBLANES = 8
LANES = 128
SSM_CB = 4
SSM_CH = 128
SSM_ST = 512
N_SEG = SUBLANES

NN = (((1,), (0,)), ((), ()))
NT = (((1,), (1,)), ((), ()))
TN = (((0,), (0,)), ((), ()))


def _params(sem=None, vmem_mb=48):
    return pltpu.CompilerParams(dimension_semantics=sem, vmem_limit_bytes=vmem_mb * 1024 * 1024)


def _dg(a, b, dims):
    return lax.dot_general(a, b, dims, preferred_element_type=F32)


def _sigmoid(x):
    return 1.0 / (1.0 + jnp.exp(-x))


_SQRT_HALF = 0.7071067811865476
_INV_SQRT_2PI = 0.3989422804014327


def _gelu(x):
    return 0.5 * x * (1.0 + lax.erf(x * _SQRT_HALF))


def _gelu_grad(x):
    return 0.5 * (1.0 + lax.erf(x * _SQRT_HALF)) + x * (_INV_SQRT_2PI * jnp.exp(-0.5 * x * x))


def _mm_nn(a, b, tm, tn, out_dtype, name, res=None):
    m, k = a.shape
    n = b.shape[1]

    def body(*refs):
        if res is None:
            a_ref, b_ref, o_ref = refs
            o_ref[...] = _dg(a_ref[...], b_ref[...], NN).astype(out_dtype)
        else:
            a_ref, b_ref, r_ref, o_ref = refs
            o_ref[...] = (r_ref[...] + _dg(a_ref[...], b_ref[...], NN)).astype(out_dtype)

    in_specs = [pl.BlockSpec((tm, k), lambda i, j: (i, 0)), pl.BlockSpec((k, tn), lambda i, j: (0, j))]
    args = [a, b]
    if res is not None:
        in_specs.append(pl.BlockSpec((tm, tn), lambda i, j: (i, j)))
        args.append(res)
    return pl.pallas_call(
        body, grid=(m // tm, n // tn), in_specs=in_specs,
        out_specs=pl.BlockSpec((tm, tn), lambda i, j: (i, j)),
        out_shape=jax.ShapeDtypeStruct((m, n), out_dtype), name=name,
        compiler_params=_params(("parallel", "parallel")),
    )(*args)


def _mm_nt(a, b, tm, tn, out_dtype, name):
    m, k = a.shape
    n = b.shape[0]

    def body(a_ref, b_ref, o_ref):
        o_ref[...] = _dg(a_ref[...], b_ref[...], NT).astype(out_dtype)

    return pl.pallas_call(
        body, grid=(m // tm, n // tn),
        in_specs=[pl.BlockSpec((tm, k), lambda i, j: (i, 0)), pl.BlockSpec((tn, k), lambda i, j: (j, 0))],
        out_specs=pl.BlockSpec((tm, tn), lambda i, j: (i, j)),
        out_shape=jax.ShapeDtypeStruct((m, n), out_dtype), name=name,
        compiler_params=_params(("parallel", "parallel")),
    )(a, b)


def _mm_tn(a, b, tm, tn, name):
    k, m = a.shape
    n = b.shape[1]

    def body(a_ref, b_ref, o_ref):
        o_ref[...] = _dg(a_ref[...], b_ref[...], TN)

    return pl.pallas_call(
        body, grid=(m // tm, n // tn),
        in_specs=[pl.BlockSpec((k, tm), lambda i, j: (0, i)), pl.BlockSpec((k, tn), lambda i, j: (0, j))],
        out_specs=pl.BlockSpec((tm, tn), lambda i, j: (i, j)),
        out_shape=jax.ShapeDtypeStruct((m, n), F32), name=name,
        compiler_params=_params(("parallel", "parallel")),
    )(a, b)


def _mm_nn_cols(a, b4, tm, name):
    m, k = a.shape
    s, _, n = b4.shape

    def body(a_ref, b_ref, o_ref):
        o_ref[...] = _dg(a_ref[...], b_ref[...], NN)

    return pl.pallas_call(
        body, grid=(m // tm, s),
        in_specs=[pl.BlockSpec((tm, k), lambda i, j: (i, 0)), pl.BlockSpec((None, k, n), lambda i, j: (j, 0, 0))],
        out_specs=pl.BlockSpec((tm, n), lambda i, j: (i, j)),
        out_shape=jax.ShapeDtypeStruct((m, s * n), F32), name=name,
        compiler_params=_params(("parallel", "parallel")),
    )(a, b4)


def _mm_nt_cols(a, b4, tm, name):
    m = a.shape[0]
    s, k, n = b4.shape

    def body(a_ref, b_ref, o_ref):
        part = _dg(a_ref[...], b_ref[...], NT)

        @pl.when(pl.program_id(1) == 0)
        def _():
            o_ref[...] = part

        @pl.when(pl.program_id(1) > 0)
        def _():
            o_ref[...] += part

    return pl.pallas_call(
        body, grid=(m // tm, s),
        in_specs=[pl.BlockSpec((tm, n), lambda i, j: (i, j)), pl.BlockSpec((None, k, n), lambda i, j: (j, 0, 0))],
        out_specs=pl.BlockSpec((tm, k), lambda i, j: (i, 0)),
        out_shape=jax.ShapeDtypeStruct((m, k), F32), name=name,
        compiler_params=_params(("parallel", "arbitrary")),
    )(a, b4)


def _mm_tn_cols(a, b, s, tm, name):
    k, m = a.shape
    n = b.shape[1] // s

    def body(a_ref, b_ref, o_ref):
        o_ref[...] = _dg(a_ref[...], b_ref[...], TN)

    return pl.pallas_call(
        body, grid=(s, m // tm),
        in_specs=[pl.BlockSpec((k, tm), lambda j, i: (0, i)), pl.BlockSpec((k, n), lambda j, i: (0, j))],
        out_specs=pl.BlockSpec((None, tm, n), lambda j, i: (j, i, 0)),
        out_shape=jax.ShapeDtypeStruct((s, m, n), F32), name=name,
        compiler_params=_params(("parallel", "parallel")),
    )(a, b)


TM_EW = 256


def _rms_fwd(x, g, name):
    l, d = x.shape

    def body(x_ref, g_ref, h_ref):
        xv = x_ref[...]
        r = lax.rsqrt(jnp.mean(xv * xv, axis=-1, keepdims=True) + EPS)
        h_ref[...] = (xv * r * g_ref[...]).astype(BF16)

    return pl.pallas_call(
        body, grid=(l // TM_EW,),
        in_specs=[pl.BlockSpec((TM_EW, d), lambda i: (i, 0)), pl.BlockSpec((1, d), lambda i: (0, 0))],
        out_specs=pl.BlockSpec((TM_EW, d), lambda i: (i, 0)),
        out_shape=jax.ShapeDtypeStruct((l, d), BF16), name=name,
        compiler_params=_params(("parallel",)),
    )(x, g)


def _rms_bwd_vals(xv, gv, dy):
    r = lax.rsqrt(jnp.mean(xv * xv, axis=-1, keepdims=True) + EPS)
    xh = xv * r
    dxh = dy * gv
    dx = r * (dxh - xh * jnp.mean(dxh * xh, axis=-1, keepdims=True))
    return dx, dy * xh


def _rms_bwd(x, g, dy, res, name):
    l, d = x.shape

    def body(x_ref, g_ref, dy_ref, res_ref, dx_ref, dxb_ref, dg_ref):
        dx, dgr = _rms_bwd_vals(x_ref[...], g_ref[...], dy_ref[...])
        dx = dx + res_ref[...]
        dx_ref[...] = dx
        dxb_ref[...] = dx.astype(BF16)

        @pl.when(pl.program_id(0) == 0)
        def _():
            dg_ref[...] = jnp.zeros_like(dg_ref)

        dg_ref[...] += jnp.sum(dgr, axis=0, keepdims=True)

    row = pl.BlockSpec((TM_EW, d), lambda i: (i, 0))
    vec = pl.BlockSpec((1, d), lambda i: (0, 0))
    return pl.pallas_call(
        body, grid=(l // TM_EW,), in_specs=[row, vec, row, row], out_specs=[row, row, vec],
        out_shape=[jax.ShapeDtypeStruct((l, d), F32), jax.ShapeDtypeStruct((l, d), BF16),
                   jax.ShapeDtypeStruct((1, d), F32)],
        name=name, compiler_params=_params(("arbitrary",)),
    )(x, g, dy, res)


def _final_loss(x2, g, target):
    l, d = x2.shape

    def body(x_ref, g_ref, t_ref, loss_ref, dx_ref, dxb_ref, dg_ref):
        xv = x_ref[...]
        gv = g_ref[...]
        r = lax.rsqrt(jnp.mean(xv * xv, axis=-1, keepdims=True) + EPS)
        xh = xv * r
        e = xh * gv - t_ref[...]
        part = jnp.sum(jnp.sum(e * e, axis=1, keepdims=True), axis=0, keepdims=True) * (0.5 / d)
        dy = e * (1.0 / d)
        dxh = dy * gv
        dx = r * (dxh - xh * jnp.mean(dxh * xh, axis=-1, keepdims=True))
        dx_ref[...] = dx
        dxb_ref[...] = dx.astype(BF16)

        @pl.when(pl.program_id(0) == 0)
        def _():
            dg_ref[...] = jnp.zeros_like(dg_ref)
            loss_ref[...] = jnp.zeros_like(loss_ref)

        dg_ref[...] += jnp.sum(dy * xh, axis=0, keepdims=True)
        loss_ref[...] += part

    row = pl.BlockSpec((TM_EW, d), lambda i: (i, 0))
    vec = pl.BlockSpec((1, d), lambda i: (0, 0))
    one = pl.BlockSpec((1, 1), lambda i: (0, 0))
    return pl.pallas_call(
        body, grid=(l // TM_EW,), in_specs=[row, vec, row], out_specs=[one, row, row, vec],
        out_shape=[jax.ShapeDtypeStruct((1, 1), F32), jax.ShapeDtypeStruct((l, d), F32),
                   jax.ShapeDtypeStruct((l, d), BF16), jax.ShapeDtypeStruct((1, d), F32)],
        name="final_loss", compiler_params=_params(("arbitrary",)),
    )(x2, g, target)


def _mix_fwd(attn, ys, g_attn, g_ssm):
    l, w = attn.shape

    def body(a_ref, y_ref, ga_ref, gs_ref, o_ref):
        for src, gr, off in ((a_ref, ga_ref, 0), (y_ref, gs_ref, w)):
            xv = src[...]
            r = lax.rsqrt(jnp.mean(xv * xv, axis=-1, keepdims=True) + EPS)
            o_ref[:, off:off + w] = (xv * r * gr[...]).astype(BF16)

    row = pl.BlockSpec((TM_EW, w), lambda i: (i, 0))
    vec = pl.BlockSpec((1, w), lambda i: (0, 0))
    return pl.pallas_call(
        body, grid=(l // TM_EW,), in_specs=[row, row, vec, vec],
        out_specs=pl.BlockSpec((TM_EW, 2 * w), lambda i: (i, 0)),
        out_shape=jax.ShapeDtypeStruct((l, 2 * w), BF16), name="mix_fwd",
        compiler_params=_params(("parallel",)),
    )(attn, ys, g_attn, g_ssm)


def _mix_bwd(attn, ys, g_attn, g_ssm, dmixed):
    l, w = attn.shape

    def body(a_ref, y_ref, ga_ref, gs_ref, dm_ref, da_ref, dy_ref, dga_ref, dgs_ref):
        @pl.when(pl.program_id(0) == 0)
        def _():
            dga_ref[...] = jnp.zeros_like(dga_ref)
            dgs_ref[...] = jnp.zeros_like(dgs_ref)

        for src, gr, off, dst, dgr in ((a_ref, ga_ref, 0, da_ref, dga_ref), (y_ref, gs_ref, w, dy_ref, dgs_ref)):
            dx, dg_rows = _rms_bwd_vals(src[...], gr[...], dm_ref[:, off:off + w])
            dst[...] = dx
            dgr[...] += jnp.sum(dg_rows, axis=0, keepdims=True)

    row = pl.BlockSpec((TM_EW, w), lambda i: (i, 0))
    vec = pl.BlockSpec((1, w), lambda i: (0, 0))
    return pl.pallas_call(
        body, grid=(l // TM_EW,),
        in_specs=[row, row, vec, vec, pl.BlockSpec((TM_EW, 2 * w), lambda i: (i, 0))],
        out_specs=[row, row, vec, vec],
        out_shape=[jax.ShapeDtypeStruct((l, w), F32), jax.ShapeDtypeStruct((l, w), F32),
                   jax.ShapeDtypeStruct((1, w), F32), jax.ShapeDtypeStruct((1, w), F32)],
        name="mix_bwd", compiler_params=_params(("arbitrary",)),
    )(attn, ys, g_attn, g_ssm, dmixed)


def _rope_tables(l):
    half = ROPE_DIM // 2
    inv_freq = jnp.power(ROPE_THETA, -jnp.arange(half, dtype=F32) / half)
    ang = jnp.arange(l, dtype=F32)[:, None] * inv_freq[None, :]
    cos, sin = jnp.cos(ang), jnp.sin(ang)
    ones = jnp.ones((l, HEAD_DIM - ROPE_DIM), F32)
    zeros = jnp.zeros((l, HEAD_DIM - ROPE_DIM), F32)
    zh = jnp.zeros((l, half), F32)
    c = jnp.concatenate([cos, cos, ones], axis=1)
    s_lo = jnp.concatenate([-sin, zh, zeros], axis=1)
    s_hi = jnp.concatenate([zh, sin, zeros], axis=1)
    return tuple(jnp.tile(t, (1, LANES // HEAD_DIM)) for t in (c, s_lo, s_hi))


def _rope_fwd(proj, tabs):
    l = proj.shape[0]
    nq = ATTN_WIDTH // LANES

    def body(p_ref, c_ref, lo_ref, hi_ref, o_ref):
        c, lo, hi = c_ref[...], lo_ref[...], hi_ref[...]
        for blk in range(nq + 1):
            t = p_ref[:, blk * LANES:(blk + 1) * LANES]
            rot = t * c + pltpu.roll(t, LANES - 8, 1) * lo + pltpu.roll(t, 8, 1) * hi
            o_ref[:, blk * LANES:(blk + 1) * LANES] = rot.astype(BF16)
        o_ref[:, (nq + 1) * LANES:] = p_ref[:, (nq + 1) * LANES:].astype(BF16)

    tab = pl.BlockSpec((TM_EW, LANES), lambda i: (i, 0))
    return pl.pallas_call(
        body, grid=(l // TM_EW,),
        in_specs=[pl.BlockSpec((TM_EW, QKV_WIDTH), lambda i: (i, 0)), tab, tab, tab],
        out_specs=pl.BlockSpec((TM_EW, QKV_WIDTH), lambda i: (i, 0)),
        out_shape=jax.ShapeDtypeStruct((l, QKV_WIDTH), BF16), name="rope_fwd",
        compiler_params=_params(("parallel",)),
    )(proj, *tabs)


def _rope_bwd(dqkv, du_ssm, dpre, d_skip, tabs):
    l = dqkv.shape[0]
    nq = ATTN_WIDTH // LANES

    def body(d_ref, du_ref, dpre_ref, ds_ref, c_ref, lo_ref, hi_ref, o_ref):
        c, lo, hi = c_ref[...], lo_ref[...], hi_ref[...]
        for blk in range(nq + 1):
            t = d_ref[:, blk * LANES:(blk + 1) * LANES]
            g = t * c + pltpu.roll(t * lo, 8, 1) + pltpu.roll(t * hi, LANES - 8, 1)
            o_ref[:, blk * LANES:(blk + 1) * LANES] = g.astype(BF16)
        o_ref[:, (nq + 1) * LANES:QKV_WIDTH] = d_ref[:, (nq + 1) * LANES:].astype(BF16)
        o_ref[:, QKV_WIDTH:] = (du_ref[...] + dpre_ref[...] * ds_ref[...]).astype(BF16)

    tab = pl.BlockSpec((TM_EW, LANES), lambda i: (i, 0))
    wide = pl.BlockSpec((TM_EW, SSM_WIDTH), lambda i: (i, 0))
    return pl.pallas_call(
        body, grid=(l // TM_EW,),
        in_specs=[pl.BlockSpec((TM_EW, QKV_WIDTH), lambda i: (i, 0)), wide, wide,
                  pl.BlockSpec((1, SSM_WIDTH), lambda i: (0, 0)), tab, tab, tab],
        out_specs=pl.BlockSpec((TM_EW, IN_WIDTH), lambda i: (i, 0)),
        out_shape=jax.ShapeDtypeStruct((l, IN_WIDTH), BF16), name="rope_bwd",
        compiler_params=_params(("parallel",)),
    )(dqkv, du_ssm, dpre, d_skip, *tabs)


_Q_COLS = ATTN_WIDTH // LANES
_SCALE = HEAD_DIM ** -0.5
_NEG = -1e30


def _window_specs(nb, width, col):
    return [
        pl.BlockSpec((BLOCK, width), lambda n: (jnp.maximum(n - 1, 0), col)),
        pl.BlockSpec((BLOCK, width), lambda n: (n, col)),
        pl.BlockSpec((BLOCK, width), lambda n: (jnp.minimum(n + 1, nb - 1), col)),
    ]


def _stacked_sink(sink_ref, heads):
    rid = lax.broadcasted_iota(jnp.int32, (len(heads) * BLOCK, 1), 0)
    sk = jnp.full(rid.shape, sink_ref[0, heads[-1]], F32)
    for g in range(len(heads) - 2, -1, -1):
        sk = jnp.where(rid < (g + 1) * BLOCK, sink_ref[0, heads[g]], sk)
    return sk


def _attn_fwd(qkv, sink):
    l = qkv.shape[0]
    nb = l // BLOCK
    grp = N_Q_HEADS // N_KV_HEADS

    def body(sink_ref, q_ref, k0, k1, k2, v0, v1, v2, o_ref, lse_ref):
        n = pl.program_id(0)
        q = q_ref[...]
        kw = jnp.concatenate([k0[...], k1[...], k2[...]], axis=0)
        vw = jnp.concatenate([v0[...], v1[...], v2[...]], axis=0)
        row = lax.broadcasted_iota(jnp.int32, (grp * BLOCK, 3 * BLOCK), 0)
        col = lax.broadcasted_iota(jnp.int32, (grp * BLOCK, 3 * BLOCK), 1)
        valid = jnp.abs(col - BLOCK - (row & (BLOCK - 1))) <= WINDOW
        valid &= jnp.logical_not((n == 0) & (col < BLOCK))
        valid &= jnp.logical_not((n == nb - 1) & (col >= 2 * BLOCK))
        for hk in range(N_KV_HEADS):
            heads = range(hk * grp, (hk + 1) * grp)
            qs = jnp.concatenate([q[:, h * HEAD_DIM:(h + 1) * HEAD_DIM] for h in heads], axis=0)
            kh = kw[:, hk * HEAD_DIM:(hk + 1) * HEAD_DIM]
            vh = vw[:, hk * HEAD_DIM:(hk + 1) * HEAD_DIM]
            s = jnp.where(valid, _dg(qs, kh, NT) * _SCALE, _NEG)
            sk = _stacked_sink(sink_ref, heads)
            m = jnp.maximum(jnp.max(s, axis=1, keepdims=True), sk)
            p = jnp.exp(s - m)
            denom = jnp.sum(p, axis=1, keepdims=True) + jnp.exp(sk - m)
            o = _dg((p / denom).astype(BF16), vh, NN)
            lse = m + jnp.log(denom)
            for g, h in enumerate(heads):
                o_ref[:, h * HEAD_DIM:(h + 1) * HEAD_DIM] = o[g * BLOCK:(g + 1) * BLOCK]
                lse_ref[:, h:h + 1] = lse[g * BLOCK:(g + 1) * BLOCK]

    return pl.pallas_call(
        body, grid=(nb,),
        in_specs=[pl.BlockSpec(memory_space=pltpu.SMEM),
                  pl.BlockSpec((BLOCK, ATTN_WIDTH), lambda n: (n, 0))]
        + _window_specs(nb, KV_WIDTH, _Q_COLS) + _window_specs(nb, KV_WIDTH, _Q_COLS + 1),
        out_specs=[pl.BlockSpec((BLOCK, ATTN_WIDTH), lambda n: (n, 0)),
                   pl.BlockSpec((BLOCK, N_Q_HEADS), lambda n: (n, 0))],
        out_shape=[jax.ShapeDtypeStruct((l, ATTN_WIDTH), F32), jax.ShapeDtypeStruct((l, N_Q_HEADS), F32)],
        name="attn_fwd", compiler_params=_params(("parallel",)),
    )(sink, qkv, qkv, qkv, qkv, qkv, qkv, qkv)


def _attn_bwd(qkv, attn, dattn, lse, sink):
    l = qkv.shape[0]
    nb = l // BLOCK
    grp = N_Q_HEADS // N_KV_HEADS

    def body(sink_ref, q0, q1, q2, k0, k1, k2, v0, v1, v2, o0, o1, o2, d0, d1, d2,
             l0, l1, l2, dqkv_ref, dsink_ref):
        n = pl.program_id(0)
        first, last = n == 0, n == nb - 1

        @pl.when(first)
        def _():
            dsink_ref[...] = jnp.zeros_like(dsink_ref)

        cat = lambda a, b, c: jnp.concatenate([a[...], b[...], c[...]], axis=0)
        qw, kw, vw = cat(q0, q1, q2), cat(k0, k1, k2), cat(v0, v1, v2)
        dow = cat(d0, d1, d2)
        prodw = cat(o0, o1, o2) * dow
        lsew = cat(l0, l1, l2)
        dob = dow.astype(BF16)
        win = 3 * BLOCK
        mid = slice(BLOCK, 2 * BLOCK)

        row = lax.broadcasted_iota(jnp.int32, (grp * BLOCK, win), 0)
        col = lax.broadcasted_iota(jnp.int32, (grp * BLOCK, win), 1)
        valid_q = jnp.abs(col - BLOCK - (row & (BLOCK - 1))) <= WINDOW
        valid_q &= jnp.logical_not(first & (col < BLOCK))
        valid_q &= jnp.logical_not(last & (col >= 2 * BLOCK))
        rowk = lax.broadcasted_iota(jnp.int32, (grp * win, BLOCK), 0)
        colk = lax.broadcasted_iota(jnp.int32, (grp * win, BLOCK), 1)
        for g in range(1, grp):
            rowk = jnp.where(rowk >= win, rowk - win, rowk)
        valid_k = jnp.abs(colk + BLOCK - rowk) <= WINDOW
        valid_k &= jnp.logical_not(first & (rowk < BLOCK))
        valid_k &= jnp.logical_not(last & (rowk >= 2 * BLOCK))

        dsink_parts = []
        for hk in range(N_KV_HEADS):
            heads = range(hk * grp, (hk + 1) * grp)
            ksl = slice(hk * HEAD_DIM, (hk + 1) * HEAD_DIM)
            hsl = [slice(h * HEAD_DIM, (h + 1) * HEAD_DIM) for h in heads]
            stack = lambda parts: jnp.concatenate(parts, axis=0)
            qws = stack([qw[:, s_] for s_ in hsl])
            dows = stack([dob[:, s_] for s_ in hsl])
            deltaws = stack([jnp.sum(prodw[:, s_], axis=1, keepdims=True) for s_ in hsl])
            lsews = stack([lsew[:, h:h + 1] for h in heads])
            of_block = lambda t: stack([t[g * win + BLOCK:g * win + 2 * BLOCK] for g in range(grp)])
            qs, dos, deltas, lses = of_block(qws), of_block(dows), of_block(deltaws), of_block(lsews)
            kh, vh = kw[:, ksl], vw[:, ksl]
            s = jnp.where(valid_q, _dg(qs, kh, NT) * _SCALE, _NEG)
            p = jnp.exp(s - lses)
            dp = _dg(dos, vh, NT)
            ds = (p * (dp - deltas) * _SCALE).astype(BF16)
            dq = _dg(ds, kh, NN)
            sink_rows = jnp.exp(_stacked_sink(sink_ref, heads) - lses) * deltas
            for g, h in enumerate(heads):
                dqkv_ref[:, hsl[g]] = dq[g * BLOCK:(g + 1) * BLOCK]
                dsink_parts.append(jnp.sum(sink_rows[g * BLOCK:(g + 1) * BLOCK], axis=0, keepdims=True))
            s2 = jnp.where(valid_k, _dg(qws, kh[mid], NT) * _SCALE, _NEG)
            p2 = jnp.exp(s2 - lsews)
            dv = _dg(p2.astype(BF16), dows, TN)
            dp2 = _dg(dows, vh[mid], NT)
            ds2 = (p2 * (dp2 - deltaws) * _SCALE).astype(BF16)
            dk = _dg(ds2, qws, TN)
            dqkv_ref[:, ATTN_WIDTH + hk * HEAD_DIM:ATTN_WIDTH + (hk + 1) * HEAD_DIM] = dk
            dqkv_ref[:, ATTN_WIDTH + KV_WIDTH + hk * HEAD_DIM:ATTN_WIDTH + KV_WIDTH + (hk + 1) * HEAD_DIM] = dv
        dsink_ref[...] -= jnp.concatenate(dsink_parts, axis=1)

    return pl.pallas_call(
        body, grid=(nb,),
        in_specs=[pl.BlockSpec(memory_space=pltpu.SMEM)]
        + _window_specs(nb, ATTN_WIDTH, 0)
        + _window_specs(nb, KV_WIDTH, _Q_COLS) + _window_specs(nb, KV_WIDTH, _Q_COLS + 1)
        + _window_specs(nb, ATTN_WIDTH, 0) + _window_specs(nb, ATTN_WIDTH, 0)
        + _window_specs(nb, N_Q_HEADS, 0),
        out_specs=[pl.BlockSpec((BLOCK, QKV_WIDTH), lambda n: (n, 0)),
                   pl.BlockSpec((1, N_Q_HEADS), lambda n: (0, 0))],
        out_shape=[jax.ShapeDtypeStruct((l, QKV_WIDTH), F32), jax.ShapeDtypeStruct((1, N_Q_HEADS), F32)],
        name="attn_bwd", compiler_params=_params(("arbitrary",)),
    )(sink, qkv, qkv, qkv, qkv, qkv, qkv, qkv, qkv, qkv, attn, attn, attn,
      dattn, dattn, dattn, lse, lse, lse)


def _ssm_disc(a_re, a_im, log_step, b_re, b_im):
    step = jnp.exp(log_step)[..., None]
    mag = jnp.exp(a_re * step)
    lb_re, lb_im = mag * jnp.cos(a_im * step), mag * jnp.sin(a_im * step)
    nr, ni = lb_re - 1.0, lb_im
    den = a_re * a_re + a_im * a_im
    f_re = ((nr * a_re + ni * a_im) / den)[..., None]
    f_im = ((ni * a_re - nr * a_im) / den)[..., None]
    return lb_re, lb_im, f_re * b_re - f_im * b_im, f_re * b_im + f_im * b_re


def _ssm_pack(lb_re, lb_im, bb_re, bb_im, c_re, c_im):
    eye = jnp.eye(SSM_CH // SSM_GROUP, dtype=F32)
    ng = SSM_CH // SSM_GROUP

    def diag_b(bb):
        t = bb.reshape(2, SSM_CB, ng, SSM_STATE, SSM_GROUP)
        return jnp.einsum('dkgpc,gh->dkgchp', t, eye).reshape(2, SSM_CB, SSM_CH, SSM_ST)

    def diag_c(cc):
        t = cc.reshape(2, SSM_CB, ng, SSM_GROUP, SSM_STATE)
        return jnp.einsum('dkgcp,gh->dkhpgc', t, eye).reshape(2, SSM_CB, SSM_ST, SSM_CH)

    bcat = jnp.concatenate([diag_b(bb_re), diag_b(bb_im)], axis=-1)
    ccat = jnp.concatenate([diag_c(c_re), -diag_c(c_im)], axis=-2)
    lam_re = lb_re.reshape(2, SSM_CB, 1, SSM_ST)
    lam_im = lb_im.reshape(2, SSM_CB, 1, SSM_ST)
    return bcat, ccat, lam_re, lam_im


def _ssm_unpack(dbcat, dccat, dlam_re, dlam_im):
    ng = SSM_CH // SSM_GROUP
    eye = jnp.eye(ng, dtype=F32)

    def undiag_b(t):
        t = t.reshape(2, SSM_CB, ng, SSM_GROUP, ng, SSM_STATE)
        return jnp.einsum('dkgchp,gh->dkgpc', t, eye).reshape(2, N_SSM_GROUPS, SSM_STATE, SSM_GROUP)

    def undiag_c(t):
        t = t.reshape(2, SSM_CB, ng, SSM_STATE, ng, SSM_GROUP)
        return jnp.einsum('dkhpgc,gh->dkgcp', t, eye).reshape(2, N_SSM_GROUPS, SSM_GROUP, SSM_STATE)

    dbb_re, dbb_im = undiag_b(dbcat[..., :SSM_ST]), undiag_b(dbcat[..., SSM_ST:])
    dc_re, dc_im = undiag_c(dccat[:, :, :SSM_ST]), -undiag_c(dccat[:, :, SSM_ST:])
    shape = (2, N_SSM_GROUPS, SSM_STATE)
    return dlam_re.reshape(shape), dlam_im.reshape(shape), dbb_re, dbb_im, dc_re, dc_im


def _to_segments(t):
    l, w = t.shape
    return t.reshape(N_SEG, l // N_SEG, w).transpose(1, 0, 2).reshape(l, w)


def _from_segments(t):
    l, w = t.shape
    return t.reshape(l // N_SEG, N_SEG, w).transpose(1, 0, 2).reshape(l, w)


SCAN_UNROLL = 4


def _cfma(ar, ai, xr, xi, br, bi):
    return ar * xr - ai * xi + br, ar * xi + ai * xr + bi


def _scan_segments(xs_ref, ar, ai, rev, nj, prev_ref=None):
    shape = (N_SEG, SSM_ST)
    ar = jnp.broadcast_to(ar, shape)
    ai = jnp.broadcast_to(ai, shape)
    zero = jnp.zeros(shape, F32)
    re_cols, im_cols = pl.ds(0, SSM_ST), pl.ds(SSM_ST, SSM_ST)

    def rows_of(jj):
        j = jnp.where(rev, nj - 1 - jj, jj)
        return j, pl.ds(pl.multiple_of(j * N_SEG, N_SEG), N_SEG)

    def steps(step, init, last_step=None):
        def outer(o, carry):
            for k in range(SCAN_UNROLL):
                carry = step(o * SCAN_UNROLL + k, carry)
            return carry

        carry = lax.fori_loop(0, nj // SCAN_UNROLL - 1, outer, init)
        for jj in range(nj - SCAN_UNROLL, nj):
            carry = (last_step if last_step is not None and jj == nj - 1 else step)(jj, carry)
        return carry

    def pass1(jj, carry):
        _, rows = rows_of(jj)
        return _cfma(ar, ai, carry[0], carry[1], xs_ref[rows, re_cols], xs_ref[rows, im_cols])

    end_r, end_i = steps(pass1, (zero, zero))

    pr, pi = ar, ai
    for _ in range(int(math.log2(nj))):
        pr, pi = pr * pr - pi * pi, 2.0 * pr * pi
    seg = lax.broadcasted_iota(jnp.int32, shape, 0)

    def chain(shift, keep):
        ir, ii = zero, zero
        for _ in range(N_SEG - 1):
            tr, ti = _cfma(pr, pi, ir, ii, end_r, end_i)
            ir = jnp.where(keep, pltpu.roll(tr, shift, 0), 0.0)
            ii = jnp.where(keep, pltpu.roll(ti, shift, 0), 0.0)
        return ir, ii

    up_r, up_i = chain(1, seg >= 1)
    dn_r, dn_i = chain(N_SEG - 1, seg <= N_SEG - 2)
    init_r, init_i = jnp.where(rev, dn_r, up_r), jnp.where(rev, dn_i, up_i)

    def pass2(jj, carry):
        j, rows = rows_of(jj)
        nr, ni = _cfma(ar, ai, carry[0], carry[1], xs_ref[rows, re_cols], xs_ref[rows, im_cols])
        xs_ref[rows, re_cols] = nr
        xs_ref[rows, im_cols] = ni
        return (j, nr, ni) + tuple(carry[2:])

    def pass2_plain(jj, carry):
        return pass2(jj, carry)[1:]

    def pass2_sums(jj, carry):
        j, nr, ni, acc_r, acc_i = pass2(jj, carry)
        jp = jnp.where(rev, j - 1, j + 1)
        prow = pl.ds(pl.multiple_of(jp * N_SEG, N_SEG), N_SEG)
        xr, xi = prev_ref[prow, re_cols], prev_ref[prow, im_cols]
        return nr, ni, acc_r + (nr * xr + ni * xi), acc_i + (ni * xr - nr * xi)

    if prev_ref is None:
        steps(pass2_plain, (init_r, init_i))
        return init_r, init_i, None, None
    _, _, acc_r, acc_i = steps(pass2_sums, (init_r, init_i, zero, zero), last_step=pass2_plain)
    return init_r, init_i, acc_r, acc_i


SSM_RC = 256


def _ssm_specs(l):
    act = pl.BlockSpec((l, SSM_CH), lambda k, d: (0, k))
    bmat = pl.BlockSpec((None, None, SSM_CH, 2 * SSM_ST), lambda k, d: (d, k, 0, 0))
    cmat = pl.BlockSpec((None, None, 2 * SSM_ST, SSM_CH), lambda k, d: (d, k, 0, 0))
    lam = pl.BlockSpec((None, None, 1, SSM_ST), lambda k, d: (d, k, 0, 0))
    return act, bmat, cmat, lam


def _ssm_fwd(u_seg, bcat, ccat, lam_re, lam_im):
    l = u_seg.shape[0]
    nj = l // N_SEG

    def body(u_ref, b_ref, c_ref, lr_ref, li_ref, y_ref, xs_ref):
        d = pl.program_id(1)

        def bu_chunk(i, _):
            rows = pl.ds(pl.multiple_of(i * SSM_RC, SSM_RC), SSM_RC)
            xs_ref[rows, :] = _dg(u_ref[rows, :], b_ref[...], NN)
            return 0

        lax.fori_loop(0, l // SSM_RC, bu_chunk, 0)
        _scan_segments(xs_ref, lr_ref[...], li_ref[...], d == 1, nj)

        def y_chunk(i, _):
            rows = pl.ds(pl.multiple_of(i * SSM_RC, SSM_RC), SSM_RC)
            yv = _dg(xs_ref[rows, :].astype(BF16), c_ref[...], NN)

            @pl.when(d == 0)
            def _():
                y_ref[rows, :] = yv

            @pl.when(d == 1)
            def _():
                y_ref[rows, :] += yv

            return 0

        lax.fori_loop(0, l // SSM_RC, y_chunk, 0)

    act, bmat, cmat, lam = _ssm_specs(l)
    return pl.pallas_call(
        body, grid=(SSM_CB, 2), in_specs=[act, bmat, cmat, lam, lam], out_specs=act,
        out_shape=jax.ShapeDtypeStruct((l, SSM_WIDTH), F32),
        scratch_shapes=[pltpu.VMEM((l, 2 * SSM_ST), F32)],
        name="ssm_fwd", compiler_params=_params(("parallel", "arbitrary"), vmem_mb=56),
    )(u_seg, bcat.astype(BF16), ccat.astype(BF16), lam_re, lam_im)


def _ssm_bwd(u_seg, dy_seg, bcat, ccat, lam_re, lam_im):
    l = u_seg.shape[0]
    nj = l // N_SEG

    def body(u_ref, dy_ref, b_ref, c_ref, lr_ref, li_ref,
             du_ref, db_ref, dc_ref, dlr_ref, dli_ref, xs_ref, gs_ref):
        d = pl.program_id(1)
        rev = d == 1
        ar, ai = lr_ref[...], li_ref[...]

        def chunk1(i, _):
            rows = pl.ds(pl.multiple_of(i * SSM_RC, SSM_RC), SSM_RC)
            xs_ref[rows, :] = _dg(u_ref[rows, :], b_ref[...], NN)
            gs_ref[rows, :] = _dg(dy_ref[rows, :], c_ref[...], NT)
            return 0

        lax.fori_loop(0, l // SSM_RC, chunk1, 0)
        init_r, init_i, _, _ = _scan_segments(xs_ref, ar, ai, rev, nj)
        _, _, acc_r, acc_i = _scan_segments(gs_ref, ar, -ai, jnp.logical_not(rev), nj, prev_ref=xs_ref)
        jb = jnp.where(rev, nj - 1, 0)
        brow = pl.ds(pl.multiple_of(jb * N_SEG, N_SEG), N_SEG)
        gr, gi = gs_ref[brow, pl.ds(0, SSM_ST)], gs_ref[brow, pl.ds(SSM_ST, SSM_ST)]
        acc_r = acc_r + gr * init_r + gi * init_i
        acc_i = acc_i + gi * init_r - gr * init_i
        dlr_ref[...] = jnp.sum(acc_r, axis=0, keepdims=True)
        dli_ref[...] = jnp.sum(acc_i, axis=0, keepdims=True)

        db_ref[...] = jnp.zeros_like(db_ref)
        dc_ref[...] = jnp.zeros_like(dc_ref)

        def chunk2(i, _):
            rows = pl.ds(pl.multiple_of(i * SSM_RC, SSM_RC), SSM_RC)
            g = gs_ref[rows, :].astype(BF16)
            dc_ref[...] += _dg(xs_ref[rows, :].astype(BF16), dy_ref[rows, :], TN)
            db_ref[...] += _dg(u_ref[rows, :], g, TN)
            duv = _dg(g, b_ref[...], NT)

            @pl.when(d == 0)
            def _():
                du_ref[rows, :] = duv

            @pl.when(d == 1)
            def _():
                du_ref[rows, :] += duv

            return 0

        lax.fori_loop(0, l // SSM_RC, chunk2, 0)

    act, bmat, cmat, lam = _ssm_specs(l)
    return pl.pallas_call(
        body, grid=(SSM_CB, 2), in_specs=[act, act, bmat, cmat, lam, lam],
        out_specs=[act, bmat, cmat, lam, lam],
        out_shape=[jax.ShapeDtypeStruct((l, SSM_WIDTH), F32),
                   jax.ShapeDtypeStruct(bcat.shape, F32), jax.ShapeDtypeStruct(ccat.shape, F32),
                   jax.ShapeDtypeStruct(lam_re.shape, F32), jax.ShapeDtypeStruct(lam_im.shape, F32)],
        scratch_shapes=[pltpu.VMEM((l, 2 * SSM_ST), F32), pltpu.VMEM((l, 2 * SSM_ST), F32)],
        name="ssm_bwd", compiler_params=_params(("parallel", "arbitrary"), vmem_mb=60),
    )(u_seg, dy_seg, bcat.astype(BF16), ccat.astype(BF16), lam_re, lam_im)


def _glu_fwd(y_ssm, u, d_skip, w_glu):
    l, w = u.shape

    def body(y_ref, u_ref, d_ref, w_ref, pre_ref, s_ref, ys_ref):
        pre = y_ref[...] + d_ref[...] * u_ref[...]
        z = _gelu(pre)
        s = _dg(z.astype(BF16), w_ref[...], NN)
        pre_ref[...] = pre
        s_ref[...] = s
        ys_ref[...] = z * _sigmoid(s)

    row = pl.BlockSpec((TM_EW, w), lambda i: (i, 0))
    out = jax.ShapeDtypeStruct((l, w), F32)
    return pl.pallas_call(
        body, grid=(l // TM_EW,),
        in_specs=[row, row, pl.BlockSpec((1, w), lambda i: (0, 0)), pl.BlockSpec((w, w), lambda i: (0, 0))],
        out_specs=[row, row, row], out_shape=[out, out, out], name="glu_fwd",
        compiler_params=_params(("parallel",)),
    )(y_ssm, u, d_skip, w_glu)


def _glu_bwd(pre, s, dys, u, d_skip, w_glu):
    l, w = u.shape

    def body(pre_ref, s_ref, dys_ref, u_ref, d_ref, w_ref, dpre_ref, z_ref, ds_ref, dd_ref):
        pre, dys = pre_ref[...], dys_ref[...]
        z = _gelu(pre)
        sig = _sigmoid(s_ref[...])
        ds = (dys * z * sig * (1.0 - sig)).astype(BF16)
        dz = dys * sig + _dg(ds, w_ref[...], NT)
        dpre = dz * _gelu_grad(pre)
        dpre_ref[...] = dpre
        z_ref[...] = z.astype(BF16)
        ds_ref[...] = ds

        @pl.when(pl.program_id(0) == 0)
        def _():
            dd_ref[...] = jnp.zeros_like(dd_ref)

        dd_ref[...] += jnp.sum(dpre * u_ref[...], axis=0, keepdims=True)

    row = pl.BlockSpec((TM_EW, w), lambda i: (i, 0))
    vec = pl.BlockSpec((1, w), lambda i: (0, 0))
    return pl.pallas_call(
        body, grid=(l // TM_EW,),
        in_specs=[row, row, row, row, vec, pl.BlockSpec((w, w), lambda i: (0, 0))],
        out_specs=[row, row, row, vec],
        out_shape=[jax.ShapeDtypeStruct((l, w), F32), jax.ShapeDtypeStruct((l, w), BF16),
                   jax.ShapeDtypeStruct((l, w), BF16), jax.ShapeDtypeStruct((1, w), F32)],
        name="glu_bwd", compiler_params=_params(("arbitrary",)),
    )(pre, s, dys, u, d_skip, w_glu)


TM_CV = 512
TC_CV = 256
HALO = SUBLANES


def _conv_specs(l, col0):
    per = TM_CV // HALO
    nh = l // HALO
    off = col0 // TC_CV
    return [
        pl.BlockSpec((HALO, TC_CV), lambda j, i: (jnp.maximum(i * per - 1, 0), j + off)),
        pl.BlockSpec((TM_CV, TC_CV), lambda j, i: (i, j + off)),
        pl.BlockSpec((HALO, TC_CV), lambda j, i: (jnp.minimum((i + 1) * per, nh - 1), j + off)),
    ]


def _ext(prev_ref, mid_ref, next_ref, first, last):
    p = jnp.where(first, 0.0, prev_ref[...])
    n = jnp.where(last, 0.0, next_ref[...])
    return jnp.concatenate([p, mid_ref[...], n], axis=0)


def _shift_dn(t):
    return pltpu.roll(t, 1, 0)


def _shift_up(t):
    return pltpu.roll(t, t.shape[0] - 1, 0)


def _conv3(e, w_ref, b_ref):
    return w_ref[0:1, :] * _shift_dn(e) + w_ref[1:2, :] * e + w_ref[2:3, :] * _shift_up(e) + b_ref[...]


def _convffn_fwd(up_pre, conv_w, conv_b):
    l = up_pre.shape[0]
    ni = l // TM_CV
    wspec = lambda off: pl.BlockSpec((3, TC_CV), lambda j, i: (0, j + off))
    bspec = lambda off: pl.BlockSpec((1, TC_CV), lambda j, i: (0, j + off))
    voff = D_FF // TC_CV

    def body(gp, gm, gn, vp, vm, vn, wg, bg, wv, bv, o_ref):
        i = pl.program_id(1)
        first, last = i == 0, i == ni - 1
        gate = _conv3(_ext(gp, gm, gn, first, last), wg, bg)[HALO:HALO + TM_CV]
        val = _conv3(_ext(vp, vm, vn, first, last), wv, bv)[HALO:HALO + TM_CV]
        o_ref[...] = (gate * _sigmoid(gate) * val).astype(BF16)

    return pl.pallas_call(
        body, grid=(D_FF // TC_CV, ni),
        in_specs=_conv_specs(l, 0) + _conv_specs(l, D_FF) + [wspec(0), bspec(0), wspec(voff), bspec(voff)],
        out_specs=pl.BlockSpec((TM_CV, TC_CV), lambda j, i: (i, j)),
        out_shape=jax.ShapeDtypeStruct((l, D_FF), BF16), name="convffn_fwd",
        compiler_params=_params(("parallel", "parallel")),
    )(up_pre, up_pre, up_pre, up_pre, up_pre, up_pre, conv_w, conv_b, conv_w, conv_b)


def _convffn_bwd(up_pre, dact, conv_w, conv_b):
    l = up_pre.shape[0]
    ni = l // TM_CV
    wspec = lambda off: pl.BlockSpec((3, TC_CV), lambda j, i: (0, j + off))
    bspec = lambda off: pl.BlockSpec((1, TC_CV), lambda j, i: (0, j + off))
    voff = D_FF // TC_CV

    def body(gp, gm, gn, vp, vm, vn, dp, dm, dn, wg, bg, wv, bv, dgate_ref, dval_ref, pg_ref, pv_ref):
        i = pl.program_id(1)
        first, last = i == 0, i == ni - 1
        ge, ve, de = _ext(gp, gm, gn, first, last), _ext(vp, vm, vn, first, last), _ext(dp, dm, dn, first, last)
        gate, val = _conv3(ge, wg, bg), _conv3(ve, wv, bv)
        sig = _sigmoid(gate)
        silu = gate * sig
        dgate = de * val * (sig + silu * (1.0 - sig))
        dval = de * silu
        mid = slice(HALO, HALO + TM_CV)
        rid = lax.broadcasted_iota(jnp.int32, (SUBLANES, TC_CV), 0)

        @pl.when(i == 0)
        def _():
            pg_ref[...] = jnp.zeros_like(pg_ref)
            pv_ref[...] = jnp.zeros_like(pv_ref)

        for dup, e, w_ref, out_ref, p_ref in ((dgate, ge, wg, dgate_ref, pg_ref), (dval, ve, wv, dval_ref, pv_ref)):
            dpre = w_ref[0:1, :] * _shift_up(dup) + w_ref[1:2, :] * dup + w_ref[2:3, :] * _shift_dn(dup)
            out_ref[...] = dpre[mid].astype(BF16)
            dm_ = dup[mid]
            sums = [jnp.sum(dm_ * _shift_dn(e)[mid], axis=0, keepdims=True),
                    jnp.sum(dm_ * e[mid], axis=0, keepdims=True),
                    jnp.sum(dm_ * _shift_up(e)[mid], axis=0, keepdims=True),
                    jnp.sum(dm_, axis=0, keepdims=True)]
            acc = jnp.zeros((SUBLANES, TC_CV), F32)
            for k, sk in enumerate(sums):
                acc = jnp.where(rid == k, sk, acc)
            p_ref[...] += acc

    tile = pl.BlockSpec((TM_CV, TC_CV), lambda j, i: (i, j))
    par = pl.BlockSpec((SUBLANES, TC_CV), lambda j, i: (0, j))
    dgate, dval, pg, pv = pl.pallas_call(
        body, grid=(D_FF // TC_CV, ni),
        in_specs=_conv_specs(l, 0) + _conv_specs(l, D_FF) + _conv_specs(l, 0)
        + [wspec(0), bspec(0), wspec(voff), bspec(voff)],
        out_specs=[tile, tile, par, par],
        out_shape=[jax.ShapeDtypeStruct((l, D_FF), BF16), jax.ShapeDtypeStruct((l, D_FF), BF16),
                   jax.ShapeDtypeStruct((SUBLANES, D_FF), F32), jax.ShapeDtypeStruct((SUBLANES, D_FF), F32)],
        name="convffn_bwd", compiler_params=_params(("parallel", "arbitrary")),
    )(up_pre, up_pre, up_pre, up_pre, up_pre, up_pre, dact, dact, dact, conv_w, conv_b, conv_w, conv_b)
    return jnp.concatenate([dgate, dval], axis=1), jnp.concatenate([pg, pv], axis=1)


def _local_step(x, target, wb, sp):
    l = x.shape[0]
    tabs = _rope_tables(l)
    disc = _ssm_disc(sp["a_re"], sp["a_im"], sp["log_step"], sp["b_re"], sp["b_im"])
    bcat, ccat, lam_re, lam_im = _ssm_pack(*disc, sp["c_re"], sp["c_im"])
    d_skip = sp["d_skip"].reshape(1, SSM_WIDTH)

    big = min(l, 1024)
    h = _rms_fwd(x, sp["norm_mix_g"], "rms_mix")
    proj = _mm_nn(h, wb["w_in"], big, IN_WIDTH, F32, "mm_in")
    qkv = _rope_fwd(proj, tabs)
    attn, lse = _attn_fwd(qkv, sp["sink"])
    u = proj[:, QKV_WIDTH:]
    u_seg = _to_segments(u).astype(BF16)
    y_ssm = _from_segments(_ssm_fwd(u_seg, bcat, ccat, lam_re, lam_im))
    pre, s_glu, ys = _glu_fwd(y_ssm, u, d_skip, wb["w_glu"])
    mixed = _mix_fwd(attn, ys, sp["norm_attn_g"], sp["norm_ssm_g"])
    x1 = _mm_nn(mixed, wb["w_out"], big, 1024, F32, "mm_out", res=x)
    h2 = _rms_fwd(x1, sp["norm_ffn_g"], "rms_ffn")
    up_pre = _mm_nn_cols(h2, wb["w_up"], big, "mm_up")
    act = _convffn_fwd(up_pre, sp["conv_w"], sp["conv_b"])
    x2 = _mm_nn(act, wb["w_down"], big, 512, F32, "mm_down", res=x1)
    loss, dx2, dx2b, d_final_g = _final_loss(x2, sp["norm_final_g"].reshape(1, D_MODEL), target)

    g = {"norm_final_g": d_final_g.reshape(D_MODEL)}
    dact = _mm_nt(dx2b, wb["w_down"], big, D_FF // 2, F32, "mm_down_dx")
    g["w_down"] = _mm_tn(act, dx2b, D_FF // 2, 512, "mm_down_dw")
    dup_pre, conv_par = _convffn_bwd(up_pre, dact, sp["conv_w"], sp["conv_b"])
    g["conv_w"], g["conv_b"] = conv_par[0:3], conv_par[3:4]
    g["w_up"] = _mm_tn_cols(h2, dup_pre, wb["w_up"].shape[0], 512, "mm_up_dw")
    dh2 = _mm_nt_cols(dup_pre, wb["w_up"], 512, "mm_up_dx")
    dx1, dx1b, g["norm_ffn_g"] = _rms_bwd(x1, sp["norm_ffn_g"], dh2, dx2, "rms_ffn_bwd")
    dmixed = _mm_nt(dx1b, wb["w_out"], big, 1024, F32, "mm_out_dx")
    g["w_out"] = _mm_tn(mixed, dx1b, 1024, 1024, "mm_out_dw")
    dattn, dys, g["norm_attn_g"], g["norm_ssm_g"] = _mix_bwd(attn, ys, sp["norm_attn_g"], sp["norm_ssm_g"], dmixed)
    dpre, zb, dsb, dd = _glu_bwd(pre, s_glu, dys, u, d_skip, wb["w_glu"])
    g["d_skip"] = dd.reshape(N_SSM_GROUPS, SSM_GROUP)
    g["w_glu"] = _mm_tn(zb, dsb, 512, 512, "mm_glu_dw")
    du_seg, dbcat, dccat, dlam_re, dlam_im = _ssm_bwd(u_seg, _to_segments(dpre).astype(BF16), bcat, ccat,
                                                      lam_re, lam_im)
    dlb_re, dlb_im, dbb_re, dbb_im, g["c_re"], g["c_im"] = _ssm_unpack(dbcat, dccat, dlam_re, dlam_im)
    _, disc_vjp = jax.vjp(_ssm_disc, sp["a_re"], sp["a_im"], sp["log_step"], sp["b_re"], sp["b_im"])
    g["a_re"], g["a_im"], g["log_step"], g["b_re"], g["b_im"] = disc_vjp((dlb_re, dlb_im, dbb_re, dbb_im))
    dqkv, dsink = _attn_bwd(qkv, attn, dattn, lse, sp["sink"])
    g["sink"] = dsink
    dproj = _rope_bwd(dqkv, _from_segments(du_seg), dpre, d_skip, tabs)
    g["w_in"] = _mm_tn(h, dproj, 512, IN_WIDTH, "mm_in_dw")
    dh = _mm_nt(dproj, wb["w_in"], 512, 512, F32, "mm_in_dx")
    grad_x, _, g["norm_mix_g"] = _rms_bwd(x, sp["norm_mix_g"], dh, dx1, "rms_mix_bwd")
    return loss, grad_x, g


MESH = pl.DeviceIdType.MESH
ANY = pl.BlockSpec(memory_space=pl.ANY)


def _place():
    x, y, c = lax.axis_index("x"), lax.axis_index("y"), lax.axis_index("c")
    chips = [(1 - x, y), (x, 1 - y), (1 - x, 1 - y)]
    return x, y, c, chips


def _chip_index(px, py):
    return 2 * px + py


CHUNK_BYTES = 256 * 1024
MAX_CHUNKS = 16


def _row_chunks(rows, row_bytes, align):
    n = max(1, min(MAX_CHUNKS, (rows * row_bytes) // CHUNK_BYTES))
    per = -(-rows // n)
    per = -(-per // align) * align
    return [(r0, min(per, rows - r0)) for r0 in range(0, rows, per)]


def _align_of(dtype):
    return SUBLANES * 4 // jnp.dtype(dtype).itemsize


def _remote(src, dst, send_sem, recv_sem, to):
    return pltpu.make_async_remote_copy(src_ref=src, dst_ref=dst, send_sem=send_sem, recv_sem=recv_sem,
                                        device_id=to, device_id_type=MESH)


CAST_ROWS = 64


def _gather_weights(shards, dtypes):
    nw = len(shards)

    def body(*refs):
        w_refs, o_refs = refs[:nw], refs[nw:2 * nw]
        send_sems, recv_sems, in_sems, out_sems = refs[2 * nw:2 * nw + 4]
        raw, cast = refs[2 * nw + 4:3 * nw + 4], refs[3 * nw + 4:]
        x, y, c, chips = _place()
        mine = _chip_index(x, y)
        sibling = (x, y, 1 - c)

        def rows_of(ref, chip, r0, nr):
            return ref.at[chip, pl.ds(r0, nr), :]

        def copy(wi, k, src, dst, to):
            return _remote(src, dst, send_sems.at[wi, k], recv_sems.at[wi, k], to)

        geo = []
        for wi in range(nw):
            rows, cols = w_refs[wi].shape
            row_bytes = cols * jnp.dtype(dtypes[wi]).itemsize
            geo.append((rows // 2, _row_chunks(rows // 2, row_bytes, _align_of(dtypes[wi]))))

        stage_in = [pltpu.make_async_copy(w_refs[wi], raw[wi], in_sems.at[wi]) for wi in range(nw)]
        for cp in stage_in:
            cp.start()
        staged = [raw[wi] if dtypes[wi] == w_refs[wi].dtype else cast[wi] for wi in range(nw)]
        stage_out = []
        for wi in range(nw):
            stage_in[wi].wait()
            if staged[wi] is not raw[wi]:
                def cast_rows(i, _, wi=wi):
                    rows = pl.ds(pl.multiple_of(i * CAST_ROWS, CAST_ROWS), CAST_ROWS)
                    cast[wi][rows, :] = raw[wi][rows, :].astype(dtypes[wi])
                    return 0

                lax.fori_loop(0, w_refs[wi].shape[0] // CAST_ROWS, cast_rows, 0)
            cp = pltpu.make_async_copy(staged[wi], o_refs[wi].at[mine], out_sems.at[wi])
            cp.start()
            stage_out.append(cp)

        for wi in range(nw):
            hr, half_chunks = geo[wi]
            for j, chip in enumerate(chips):
                for r0, nr in half_chunks:
                    copy(wi, j, staged[wi].at[pl.ds(c * hr + r0, nr), :],
                         rows_of(o_refs[wi], mine, c * hr + r0, nr), (*chip, c)).start()
        for wi in range(nw):
            hr, half_chunks = geo[wi]
            for j, chip in enumerate(chips):
                got = rows_of(o_refs[wi], _chip_index(*chip), c * hr, hr)
                copy(wi, j, got, got, (*chip, c)).wait_recv()
                for r0, nr in half_chunks:
                    piece = rows_of(o_refs[wi], _chip_index(*chip), c * hr + r0, nr)
                    copy(wi, 3 + j, piece, piece, sibling).start()
        for wi in range(nw):
            hr = geo[wi][0]
            for j, chip in enumerate(chips):
                got = rows_of(o_refs[wi], _chip_index(*chip), (1 - c) * hr, hr)
                copy(wi, 3 + j, got, got, sibling).wait_recv()
        for wi in range(nw):
            hr = geo[wi][0]
            sent = rows_of(o_refs[wi], mine, c * hr, hr)
            for k in range(6):
                copy(wi, k, sent, sent, sibling).wait_send()
            stage_out[wi].wait()

    return pl.pallas_call(
        body, in_specs=[ANY] * nw, out_specs=[ANY] * nw,
        out_shape=[jax.ShapeDtypeStruct((4, *s.shape), t) for s, t in zip(shards, dtypes)],
        scratch_shapes=[pltpu.SemaphoreType.DMA((nw, 6)), pltpu.SemaphoreType.DMA((nw, 6)),
                        pltpu.SemaphoreType.DMA((nw,)), pltpu.SemaphoreType.DMA((nw,))]
        + [pltpu.VMEM(s.shape, s.dtype) for s in shards] + [pltpu.VMEM(s.shape, t) for s, t in zip(shards, dtypes)],
        name="gather_weights", compiler_params=_params(vmem_mb=40),
    )(*shards)


def _pair_exchange(grads):
    na = len(grads)

    def body(*refs):
        g_refs, o_refs = refs[:na], refs[na:2 * na]
        send_sems, recv_sems = refs[2 * na:]
        x, y, c, _ = _place()
        sibling = (x, y, 1 - c)
        for ai in range(na):
            _, rows, cols = g_refs[ai].shape
            hr = rows // 2
            for k in range(4):
                for r0, nr in _row_chunks(hr, cols * 4, SUBLANES):
                    _remote(g_refs[ai].at[k, pl.ds((1 - c) * hr + r0, nr), :], o_refs[ai].at[k, pl.ds(r0, nr), :],
                            send_sems.at[ai], recv_sems.at[ai], sibling).start()
        for ai in range(na):
            _remote(o_refs[ai], o_refs[ai], send_sems.at[ai], recv_sems.at[ai], sibling).wait()

    return pl.pallas_call(
        body, in_specs=[ANY] * na, out_specs=[ANY] * na,
        out_shape=[jax.ShapeDtypeStruct((4, g.shape[1] // 2, g.shape[2]), F32) for g in grads],
        scratch_shapes=[pltpu.SemaphoreType.DMA((na,)), pltpu.SemaphoreType.DMA((na,))],
        name="pair_exchange",
    )(*grads)


def _row_tile(rows, cols, align):
    best = align
    for cand in range(align, rows + 1, align):
        if rows % cand == 0 and cand * cols <= 256 * 1024:
            best = cand
    return best


def _pair_sum(g, got, place, transit, name):
    _, rows, cols = g.shape
    hr = rows // 2
    tr = _row_tile(hr, cols, _align_of(transit))
    nt = hr // tr

    def body(p_ref, g_ref, r_ref, s_ref, own_ref):
        total = g_ref[...] + r_ref[...]
        s_ref[...] = total.astype(transit)

        @pl.when(pl.program_id(1) == p_ref[1])
        def _():
            own_ref[...] = total

    grid_spec = pltpu.PrefetchScalarGridSpec(
        num_scalar_prefetch=1, grid=(nt, 4),
        in_specs=[pl.BlockSpec((None, tr, cols), lambda i, k, p: (k, p[0] * nt + i, 0)),
                  pl.BlockSpec((None, tr, cols), lambda i, k, p: (k, i, 0))],
        out_specs=[pl.BlockSpec((None, tr, cols), lambda i, k, p: (k, i, 0)),
                   pl.BlockSpec((tr, cols), lambda i, k, p: (i, 0))])
    return pl.pallas_call(
        body, grid_spec=grid_spec,
        out_shape=[jax.ShapeDtypeStruct((4, hr, cols), transit), jax.ShapeDtypeStruct((hr, cols), F32)],
        name=name, compiler_params=_params(("parallel", "arbitrary")),
    )(place, g, got)


def _chip_exchange(sums):
    na = len(sums)

    def body(*refs):
        s_refs, o_refs = refs[:na], refs[na:2 * na]
        send_sems, recv_sems = refs[2 * na:]
        x, y, c, chips = _place()
        for ai in range(na):
            _, rows, cols = s_refs[ai].shape
            row_bytes = cols * jnp.dtype(s_refs[ai].dtype).itemsize
            for r0, nr in _row_chunks(rows, row_bytes, _align_of(s_refs[ai].dtype)):
                for j, chip in enumerate(chips):
                    _remote(s_refs[ai].at[_chip_index(*chip), pl.ds(r0, nr), :], o_refs[ai].at[j, pl.ds(r0, nr), :],
                            send_sems.at[ai, j], recv_sems.at[ai, j], (*chip, c)).start()
        for ai in range(na):
            for j, chip in enumerate(chips):
                _remote(o_refs[ai].at[j], o_refs[ai].at[j], send_sems.at[ai, j], recv_sems.at[ai, j],
                        (*chip, c)).wait()

    return pl.pallas_call(
        body, in_specs=[ANY] * na, out_specs=[ANY] * na,
        out_shape=[jax.ShapeDtypeStruct((3, *s.shape[1:]), s.dtype) for s in sums],
        scratch_shapes=[pltpu.SemaphoreType.DMA((na, 3)), pltpu.SemaphoreType.DMA((na, 3))],
        name="chip_exchange",
    )(*sums)


def _chip_sum(own, landed, name):
    hr, cols = own.shape
    tr = _row_tile(hr, cols, _align_of(landed.dtype))

    def body(o_ref, l_ref, f_ref):
        acc = o_ref[...]
        for j in range(3):
            acc = acc + l_ref[j].astype(F32)
        f_ref[...] = acc

    return pl.pallas_call(
        body, grid=(hr // tr,),
        in_specs=[pl.BlockSpec((tr, cols), lambda i: (i, 0)), pl.BlockSpec((3, tr, cols), lambda i: (0, i, 0))],
        out_specs=pl.BlockSpec((tr, cols), lambda i: (i, 0)),
        out_shape=jax.ShapeDtypeStruct((hr, cols), F32), name=name,
        compiler_params=_params(("parallel",)),
    )(own, landed)


def _final_exchange(halves, small):
    nh = len(halves)

    def body(*refs):
        h_refs, s_ref = refs[:nh], refs[nh]
        o_refs, so_ref = refs[nh + 1:2 * nh + 1], refs[2 * nh + 1]
        send_sems, recv_sems, local_sem, ssend_sems, srecv_sems = refs[2 * nh + 2:]
        x, y, c, _ = _place()
        me = 4 * x + 2 * y + c
        sibling = (x, y, 1 - c)
        for hi in range(nh):
            hr, cols = h_refs[hi].shape
            for r0, nr in _row_chunks(hr, cols * 4, SUBLANES):
                _remote(h_refs[hi].at[pl.ds(r0, nr), :], o_refs[hi].at[pl.ds(r0, nr), :],
                        send_sems.at[hi], recv_sems.at[hi], sibling).start()
        small_cps = [pltpu.make_async_copy(s_ref, so_ref.at[me], local_sem)]
        for r in range(1, 8):
            fx, fy, fc = (r >> 2) & 1, (r >> 1) & 1, r & 1
            peer = (1 - x if fx else x, 1 - y if fy else y, 1 - c if fc else c)
            small_cps.append(_remote(s_ref, so_ref.at[me], ssend_sems.at[r - 1], srecv_sems.at[r - 1], peer))
        for cp in small_cps:
            cp.start()
        for hi in range(nh):
            _remote(h_refs[hi], o_refs[hi], send_sems.at[hi], recv_sems.at[hi], sibling).wait()
        for cp in small_cps:
            cp.wait()

    return pl.pallas_call(
        body, in_specs=[ANY] * (nh + 1), out_specs=[ANY] * (nh + 1),
        out_shape=[jax.ShapeDtypeStruct(h.shape, F32) for h in halves]
        + [jax.ShapeDtypeStruct((8, *small.shape), F32)],
        scratch_shapes=[pltpu.SemaphoreType.DMA((nh,)), pltpu.SemaphoreType.DMA((nh,)),
                        pltpu.SemaphoreType.DMA, pltpu.SemaphoreType.DMA((7,)), pltpu.SemaphoreType.DMA((7,))],
        name="final_exchange",
    )(*halves, small)


def _adamw(w, g, m, v, name):
    shape = w.shape
    n = w.size
    if w.ndim >= 2 and shape[-1] >= LANES:
        two_d = (n // shape[-1], shape[-1])
    elif n % LANES == 0:
        two_d = (n // LANES, LANES)
    else:
        two_d = (1, n)
    r, c = two_d
    tr = r
    for cand in (512, 256, 176, 128, 64):
        if r > cand and r % cand == 0 and cand * c <= 256 * 1024:
            tr = cand
            break
    c1 = 1.0 - ADAM_B1 ** ADAM_STEP
    c2 = 1.0 - ADAM_B2 ** ADAM_STEP

    def body(w_ref, g_ref, m_ref, v_ref, d_ref, nm_ref, nv_ref):
        gv = g_ref[...]
        nm = ADAM_B1 * m_ref[...] + (1.0 - ADAM_B1) * gv
        nv = ADAM_B2 * v_ref[...] + (1.0 - ADAM_B2) * (gv * gv)
        d_ref[...] = -ADAM_LR * ((nm / c1) / (jnp.sqrt(nv / c2) + ADAM_EPS) + ADAM_WD * w_ref[...])
        nm_ref[...] = nm
        nv_ref[...] = nv

    spec = pl.BlockSpec((tr, c), lambda i: (i, 0))
    out = jax.ShapeDtypeStruct((r, c), F32)
    d, nm, nv = pl.pallas_call(
        body, grid=(r // tr,), in_specs=[spec] * 4, out_specs=[spec] * 3, out_shape=[out] * 3, name=name,
        compiler_params=_params(("parallel",)),
    )(w.reshape(two_d), g.reshape(two_d), m.reshape(two_d), v.reshape(two_d))
    return d.reshape(shape), nm.reshape(shape), nv.reshape(shape)


def _adamw_many(ws, gs, ms, vs, name):
    n = len(ws)
    c1 = 1.0 - ADAM_B1 ** ADAM_STEP
    c2 = 1.0 - ADAM_B2 ** ADAM_STEP

    def body(*refs):
        w_refs, g_refs, m_refs, v_refs = (refs[k * n:(k + 1) * n] for k in range(4))
        d_refs, nm_refs, nv_refs = (refs[(4 + k) * n:(5 + k) * n] for k in range(3))
        for i in range(n):
            gv = g_refs[i][...]
            nm = ADAM_B1 * m_refs[i][...] + (1.0 - ADAM_B1) * gv
            nv = ADAM_B2 * v_refs[i][...] + (1.0 - ADAM_B2) * (gv * gv)
            d_refs[i][...] = -ADAM_LR * ((nm / c1) / (jnp.sqrt(nv / c2) + ADAM_EPS) + ADAM_WD * w_refs[i][...])
            nm_refs[i][...] = nm
            nv_refs[i][...] = nv

    vmem = pl.BlockSpec(memory_space=pltpu.VMEM)
    shapes = [jax.ShapeDtypeStruct(t.shape, F32) for t in ws]
    outs = pl.pallas_call(body, in_specs=[vmem] * (4 * n), out_specs=[vmem] * (3 * n), out_shape=shapes * 3,
                          name=name, compiler_params=_params(vmem_mb=56))(*ws, *gs, *ms, *vs)
    return outs[:n], outs[n:2 * n], outs[2 * n:]


BIG = ("w_in", "w_glu", "w_out", "w_up", "w_down")
WEIGHTS = ("norm_mix_g", "w_in", "a_re", "a_im", "log_step", "b_re", "b_im", "c_re", "c_im", "d_skip", "w_glu",
           "sink", "norm_attn_g", "norm_ssm_g", "w_out", "norm_ffn_g", "w_up", "conv_w", "conv_b", "w_down",
           "norm_final_g")
SMALL = ("norm_mix_g", "a_re", "a_im", "log_step", "b_re", "b_im", "c_re", "c_im", "d_skip", "sink",
         "norm_attn_g", "norm_ssm_g", "norm_ffn_g", "conv_w", "conv_b", "norm_final_g")
SMALL_ROWS = 40
N_DEV = 8


def _full_matrices(gathered):
    w_in, w_glu, w_out, w_up, w_down = gathered
    cols = lambda t: t.transpose(1, 0, 2).reshape(t.shape[1], 4 * t.shape[2])
    rows = lambda t: t.reshape(4 * t.shape[1], t.shape[2])
    return {"w_in": cols(w_in), "w_glu": rows(w_glu), "w_out": rows(w_out), "w_up": w_up, "w_down": rows(w_down)}


def _by_owner(name, g):
    if name == "w_up":
        return g
    if name == "w_in":
        return g.reshape(g.shape[0], 4, g.shape[1] // 4).transpose(1, 0, 2)
    return g.reshape(4, g.shape[0] // 4, g.shape[1])


def kernel(x, norm_mix_g, w_in, a_re, a_im, log_step, b_re, b_im, c_re, c_im, d_skip, w_glu, sink, norm_attn_g, norm_ssm_g, w_out, norm_ffn_g, w_up, conv_w, conv_b, w_down, norm_final_g, loss_target, m_norm_mix_g, m_w_in, m_a_re, m_a_im, m_log_step, m_b_re, m_b_im, m_c_re, m_c_im, m_d_skip, m_w_glu, m_sink, m_norm_attn_g, m_norm_ssm_g, m_w_out, m_norm_ffn_g, m_w_up, m_conv_w, m_conv_b, m_w_down, m_norm_final_g, v_norm_mix_g, v_w_in, v_a_re, v_a_im, v_log_step, v_b_re, v_b_im, v_c_re, v_c_im, v_d_skip, v_w_glu, v_sink, v_norm_attn_g, v_norm_ssm_g, v_w_out, v_norm_ffn_g, v_w_up, v_conv_w, v_conv_b, v_w_down, v_norm_final_g):
    given = dict(locals())
    w = {n: given[n] for n in WEIGHTS}
    m = {n: given["m_" + n] for n in WEIGHTS}
    v = {n: given["v_" + n] for n in WEIGHTS}
    xy = 2 * lax.axis_index("x") + lax.axis_index("y")

    conv_rows = jnp.pad(w["conv_w"][0], ((0, 2 * SUBLANES - 3), (0, 0)))
    *gathered, conv_all = _gather_weights([w[n][0] for n in BIG] + [conv_rows], [BF16] * len(BIG) + [F32])
    wb = _full_matrices(gathered)

    sp = {n: w[n][0] for n in ("a_re", "a_im", "log_step", "b_re", "b_im", "c_re", "c_im", "d_skip",
                               "norm_mix_g", "norm_attn_g", "norm_ssm_g", "norm_ffn_g", "sink", "conv_b")}
    for n in ("norm_mix_g", "norm_attn_g", "norm_ssm_g", "norm_ffn_g", "sink", "conv_b"):
        sp[n] = sp[n].reshape(1, -1)
    sp["conv_w"] = conv_all[:, :3].transpose(1, 0, 2).reshape(3, 2 * D_FF)
    sp["norm_final_g"] = w["norm_final_g"]
    loss, grad_x, g = _local_step(x[0], loss_target[0], wb, sp)

    flat = jnp.concatenate([g[n].reshape(-1) for n in SMALL] + [loss.reshape(-1)])
    pad = N_DEV * SMALL_ROWS * D_MODEL - flat.shape[0]
    small = jnp.concatenate([flat, jnp.zeros((pad,), F32)]).reshape(4, 2 * SMALL_ROWS, D_MODEL)
    by_owner = [_by_owner(n, g[n]) for n in BIG] + [small]
    core = lax.axis_index("c")
    place = jnp.stack([core, xy]).astype(jnp.int32)
    got = _pair_exchange(by_owner)
    transit = [BF16] * len(BIG) + [F32]
    chip_sums, own_sums = zip(*[_pair_sum(a, b, place, t, "pair_sum_%d" % i)
                                for i, (a, b, t) in enumerate(zip(by_owner, got, transit))])
    landed = _chip_exchange(list(chip_sums))
    halves = [_chip_sum(o, t, "chip_sum_%d" % i) for i, (o, t) in enumerate(zip(own_sums, landed))]
    *others, small_all = _final_exchange(halves[:-1], halves[-1])
    grads = {n: jnp.concatenate([jnp.where(core == 0, h, o), jnp.where(core == 0, o, h)], axis=0)
             for n, h, o in zip(BIG, halves, others)}
    flat = small_all.reshape(-1)
    off = 0
    for n in SMALL:
        shape = (3, 4 * w[n].shape[-1]) if n == "conv_w" else w[n].shape[1:] if n != "norm_final_g" else w[n].shape
        size = math.prod(shape)
        grads[n] = flat[off:off + size].reshape(shape)
        off += size
    loss = flat[off]
    cw = w["conv_w"].shape[-1]
    grads["conv_w"] = lax.dynamic_slice_in_dim(grads["conv_w"], xy * cw, cw, axis=1)
    grads = {n: grads[n].reshape(w[n].shape) for n in WEIGHTS}

    delta, new_m, new_v = {}, {}, {}
    for n in BIG:
        delta[n], new_m[n], new_v[n] = _adamw(w[n], grads[n], m[n], v[n], "adamw_" + n)
    for group, name in ((("b_re", "b_im"), "adamw_b"), (tuple(n for n in SMALL if n not in ("b_re", "b_im")), "adamw_small")):
        row = lambda t: t.reshape(1, -1) if t.ndim == 1 else t
        d_, m_, v_ = _adamw_many(*[[row(t[n]) for n in group] for t in (w, grads, m, v)], name)
        for n, dn, mn, vn in zip(group, d_, m_, v_):
            delta[n], new_m[n], new_v[n] = (t.reshape(w[n].shape) for t in (dn, mn, vn))
    return (loss, grad_x[None], *[grads[n] for n in WEIGHTS], *[delta[n] for n in WEIGHTS],
            *[new_m[n] for n in WEIGHTS], *[new_v[n] for n in WEIGHTS])
```

```python
import functools
import math

import jax
import jax.numpy as jnp
from jax import lax
from jax.experimental import pallas as pl
from jax.experimental.pallas import tpu as pltpu

F32 = jnp.float32
BF16 = jnp.bfloat16

D_MODEL = 1024
N_Q_HEADS = 8
N_KV_HEADS = 2
HEAD_DIM = 64
ATTN_WIDTH = 512
KV_WIDTH = 128
QKV_WIDTH = ATTN_WIDTH + 2 * KV_WIDTH
WINDOW = 128
BLOCK = 128
ROPE_DIM = 16
ROPE_THETA = 500000.0
SSM_WIDTH = 512
SSM_GROUP = 16
N_SSM_GROUPS = 32
SSM_STATE = 64
IN_WIDTH = 1280
D_FF = 2816
EPS = 1e-6
ADAM_LR = 0.001
ADAM_B1 = 0.9
ADAM_B2 = 0.999
ADAM_EPS = 1e-08
ADAM_WD = 0.01
ADAM_STEP = 10

VMEM_BYTES_V7X = 64 * 1024 * 1024
SUBLANES = 8
LANES = 128
SSM_CB = 4
SSM_CH = 128
SSM_ST = 512
N_SEG = SUBLANES

NN = (((1,), (0,)), ((), ()))
NT = (((1,), (1,)), ((), ()))
TN = (((0,), (0,)), ((), ()))


def _params(sem=None, vmem_mb=48):
    return pltpu.CompilerParams(dimension_semantics=sem, vmem_limit_bytes=vmem_mb * 1024 * 1024)


def _dg(a, b, dims):
    return lax.dot_general(a, b, dims, preferred_element_type=F32)


def _sigmoid(x):
    return 1.0 / (1.0 + jnp.exp(-x))


_SQRT_HALF = 0.7071067811865476
_INV_SQRT_2PI = 0.3989422804014327


def _gelu(x):
    return 0.5 * x * (1.0 + lax.erf(x * _SQRT_HALF))


def _gelu_grad(x):
    return 0.5 * (1.0 + lax.erf(x * _SQRT_HALF)) + x * (_INV_SQRT_2PI * jnp.exp(-0.5 * x * x))


def _mm_nn(a, b, tm, tn, out_dtype, name, res=None):
    m, k = a.shape
    n = b.shape[1]

    def body(*refs):
        if res is None:
            a_ref, b_ref, o_ref = refs
            o_ref[...] = _dg(a_ref[...], b_ref[...], NN).astype(out_dtype)
        else:
            a_ref, b_ref, r_ref, o_ref = refs
            o_ref[...] = (r_ref[...] + _dg(a_ref[...], b_ref[...], NN)).astype(out_dtype)

    in_specs = [pl.BlockSpec((tm, k), lambda i, j: (i, 0)), pl.BlockSpec((k, tn), lambda i, j: (0, j))]
    args = [a, b]
    if res is not None:
        in_specs.append(pl.BlockSpec((tm, tn), lambda i, j: (i, j)))
        args.append(res)
    return pl.pallas_call(
        body, grid=(m // tm, n // tn), in_specs=in_specs,
        out_specs=pl.BlockSpec((tm, tn), lambda i, j: (i, j)),
        out_shape=jax.ShapeDtypeStruct((m, n), out_dtype), name=name,
        compiler_params=_params(("parallel", "parallel")),
    )(*args)


def _mm_nt(a, b, tm, tn, out_dtype, name):
    m, k = a.shape
    n = b.shape[0]

    def body(a_ref, b_ref, o_ref):
        o_ref[...] = _dg(a_ref[...], b_ref[...], NT).astype(out_dtype)

    return pl.pallas_call(
        body, grid=(m // tm, n // tn),
        in_specs=[pl.BlockSpec((tm, k), lambda i, j: (i, 0)), pl.BlockSpec((tn, k), lambda i, j: (j, 0))],
        out_specs=pl.BlockSpec((tm, tn), lambda i, j: (i, j)),
        out_shape=jax.ShapeDtypeStruct((m, n), out_dtype), name=name,
        compiler_params=_params(("parallel", "parallel")),
    )(a, b)


def _mm_tn(a, b, tm, tn, name):
    k, m = a.shape
    n = b.shape[1]

    def body(a_ref, b_ref, o_ref):
        o_ref[...] = _dg(a_ref[...], b_ref[...], TN)

    return pl.pallas_call(
        body, grid=(m // tm, n // tn),
        in_specs=[pl.BlockSpec((k, tm), lambda i, j: (0, i)), pl.BlockSpec((k, tn), lambda i, j: (0, j))],
        out_specs=pl.BlockSpec((tm, tn), lambda i, j: (i, j)),
        out_shape=jax.ShapeDtypeStruct((m, n), F32), name=name,
        compiler_params=_params(("parallel", "parallel")),
    )(a, b)


def _mm_split(a, b, tm, split, name):
    m, k = a.shape
    n = b.shape[1]

    def body(a_ref, b_ref, lo_ref, hi_ref):
        out = _dg(a_ref[...], b_ref[...], NN)
        lo_ref[...] = out[:, :split]
        hi_ref[...] = out[:, split:]

    return pl.pallas_call(
        body, grid=(m // tm,),
        in_specs=[pl.BlockSpec((tm, k), lambda i: (i, 0)), pl.BlockSpec((k, n), lambda i: (0, 0))],
        out_specs=[pl.BlockSpec((tm, split), lambda i: (i, 0)), pl.BlockSpec((tm, n - split), lambda i: (i, 0))],
        out_shape=[jax.ShapeDtypeStruct((m, split), F32), jax.ShapeDtypeStruct((m, n - split), F32)], name=name,
        compiler_params=_params(("parallel",)),
    )(a, b)


def _mm_nn_cols(a, b4, tm, name):
    m, k = a.shape
    s, _, n = b4.shape

    def body(a_ref, b_ref, o_ref):
        o_ref[...] = _dg(a_ref[...], b_ref[...], NN)

    return pl.pallas_call(
        body, grid=(m // tm, s),
        in_specs=[pl.BlockSpec((tm, k), lambda i, j: (i, 0)), pl.BlockSpec((None, k, n), lambda i, j: (j, 0, 0))],
        out_specs=pl.BlockSpec((tm, n), lambda i, j: (i, j)),
        out_shape=jax.ShapeDtypeStruct((m, s * n), F32), name=name,
        compiler_params=_params(("parallel", "parallel")),
    )(a, b4)


def _mm_nt_cols(a2, b4, tm, name):
    h, m, wide = a2.shape
    s, k, n = b4.shape
    per = s // h

    def body(a_ref, b_ref, o_ref):
        acc = None
        for j in range(s):
            part = _dg(a_ref[j // per, :, (j % per) * n:(j % per + 1) * n], b_ref[j], NT)
            acc = part if acc is None else acc + part
        o_ref[...] = acc

    return pl.pallas_call(
        body, grid=(m // tm,),
        in_specs=[pl.BlockSpec((h, tm, wide), lambda i: (0, i, 0)), pl.BlockSpec((s, k, n), lambda i: (0, 0, 0))],
        out_specs=pl.BlockSpec((tm, k), lambda i: (i, 0)),
        out_shape=jax.ShapeDtypeStruct((m, k), F32), name=name,
        compiler_params=_params(("parallel",)),
    )(a2, b4)


def _mm_tn_cols(a, b2, s, tm, name):
    k, m = a.shape
    h, _, wide = b2.shape
    per = s // h
    n = wide // per

    def body(a_ref, b_ref, o_ref):
        o_ref[...] = _dg(a_ref[...], b_ref[...], TN)

    return pl.pallas_call(
        body, grid=(s, m // tm),
        in_specs=[pl.BlockSpec((k, tm), lambda j, i: (0, i)),
                  pl.BlockSpec((None, k, n), lambda j, i: (j // per, 0, j % per))],
        out_specs=pl.BlockSpec((None, tm, n), lambda j, i: (j, i, 0)),
        out_shape=jax.ShapeDtypeStruct((s, m, n), F32), name=name,
        compiler_params=_params(("parallel", "parallel")),
    )(a, b2)


TM_EW = 256


def _rms_fwd(x, g, name):
    l, d = x.shape

    def body(x_ref, g_ref, h_ref):
        xv = x_ref[...]
        r = lax.rsqrt(jnp.mean(xv * xv, axis=-1, keepdims=True) + EPS)
        h_ref[...] = (xv * r * g_ref[...]).astype(BF16)

    return pl.pallas_call(
        body, grid=(l // TM_EW,),
        in_specs=[pl.BlockSpec((TM_EW, d), lambda i: (i, 0)), pl.BlockSpec((1, d), lambda i: (0, 0))],
        out_specs=pl.BlockSpec((TM_EW, d), lambda i: (i, 0)),
        out_shape=jax.ShapeDtypeStruct((l, d), BF16), name=name,
        compiler_params=_params(("parallel",)),
    )(x, g)


def _rms_bwd_vals(xv, gv, dy):
    r = lax.rsqrt(jnp.mean(xv * xv, axis=-1, keepdims=True) + EPS)
    xh = xv * r
    dxh = dy * gv
    dx = r * (dxh - xh * jnp.mean(dxh * xh, axis=-1, keepdims=True))
    return dx, dy * xh


def _rms_bwd(x, g, dy, res, name):
    l, d = x.shape

    def body(x_ref, g_ref, dy_ref, res_ref, dx_ref, dxb_ref, dg_ref):
        dx, dgr = _rms_bwd_vals(x_ref[...], g_ref[...], dy_ref[...])
        dx = dx + res_ref[...]
        dx_ref[...] = dx
        dxb_ref[...] = dx.astype(BF16)

        @pl.when(pl.program_id(0) == 0)
        def _():
            dg_ref[...] = jnp.zeros_like(dg_ref)

        dg_ref[...] += jnp.sum(dgr, axis=0, keepdims=True)

    row = pl.BlockSpec((TM_EW, d), lambda i: (i, 0))
    vec = pl.BlockSpec((1, d), lambda i: (0, 0))
    return pl.pallas_call(
        body, grid=(l // TM_EW,), in_specs=[row, vec, row, row], out_specs=[row, row, vec],
        out_shape=[jax.ShapeDtypeStruct((l, d), F32), jax.ShapeDtypeStruct((l, d), BF16),
                   jax.ShapeDtypeStruct((1, d), F32)],
        name=name, compiler_params=_params(("arbitrary",)),
    )(x, g, dy, res)


def _final_loss(x2, g, target):
    l, d = x2.shape

    def body(x_ref, g_ref, t_ref, loss_ref, dx_ref, dxb_ref, dg_ref):
        xv = x_ref[...]
        gv = g_ref[...]
        r = lax.rsqrt(jnp.mean(xv * xv, axis=-1, keepdims=True) + EPS)
        xh = xv * r
        e = xh * gv - t_ref[...]
        part = jnp.sum(jnp.sum(e * e, axis=1, keepdims=True), axis=0, keepdims=True) * (0.5 / d)
        dy = e * (1.0 / d)
        dxh = dy * gv
        dx = r * (dxh - xh * jnp.mean(dxh * xh, axis=-1, keepdims=True))
        dx_ref[...] = dx
        dxb_ref[...] = dx.astype(BF16)

        @pl.when(pl.program_id(0) == 0)
        def _():
            dg_ref[...] = jnp.zeros_like(dg_ref)
            loss_ref[...] = jnp.zeros_like(loss_ref)

        dg_ref[...] += jnp.sum(dy * xh, axis=0, keepdims=True)
        loss_ref[...] += part

    row = pl.BlockSpec((TM_EW, d), lambda i: (i, 0))
    vec = pl.BlockSpec((1, d), lambda i: (0, 0))
    one = pl.BlockSpec((1, 1), lambda i: (0, 0))
    return pl.pallas_call(
        body, grid=(l // TM_EW,), in_specs=[row, vec, row], out_specs=[one, row, row, vec],
        out_shape=[jax.ShapeDtypeStruct((1, 1), F32), jax.ShapeDtypeStruct((l, d), F32),
                   jax.ShapeDtypeStruct((l, d), BF16), jax.ShapeDtypeStruct((1, d), F32)],
        name="final_loss", compiler_params=_params(("arbitrary",)),
    )(x2, g, target)


def _mix_fwd(attn, ys, g_attn, g_ssm):
    l, w = attn.shape

    def body(a_ref, y_ref, ga_ref, gs_ref, o_ref):
        for src, gr, off in ((a_ref, ga_ref, 0), (y_ref, gs_ref, w)):
            xv = src[...]
            r = lax.rsqrt(jnp.mean(xv * xv, axis=-1, keepdims=True) + EPS)
            o_ref[:, off:off + w] = (xv * r * gr[...]).astype(BF16)

    row = pl.BlockSpec((TM_EW, w), lambda i: (i, 0))
    vec = pl.BlockSpec((1, w), lambda i: (0, 0))
    return pl.pallas_call(
        body, grid=(l // TM_EW,), in_specs=[row, row, vec, vec],
        out_specs=pl.BlockSpec((TM_EW, 2 * w), lambda i: (i, 0)),
        out_shape=jax.ShapeDtypeStruct((l, 2 * w), BF16), name="mix_fwd",
        compiler_params=_params(("parallel",)),
    )(attn, ys, g_attn, g_ssm)


def _mix_bwd(attn, ys, g_attn, g_ssm, dmixed):
    l, w = attn.shape

    def body(a_ref, y_ref, ga_ref, gs_ref, dm_ref, da_ref, dy_ref, dga_ref, dgs_ref):
        @pl.when(pl.program_id(0) == 0)
        def _():
            dga_ref[...] = jnp.zeros_like(dga_ref)
            dgs_ref[...] = jnp.zeros_like(dgs_ref)

        for src, gr, off, dst, dgr in ((a_ref, ga_ref, 0, da_ref, dga_ref), (y_ref, gs_ref, w, dy_ref, dgs_ref)):
            dx, dg_rows = _rms_bwd_vals(src[...], gr[...], dm_ref[:, off:off + w])
            dst[...] = dx
            dgr[...] += jnp.sum(dg_rows, axis=0, keepdims=True)

    row = pl.BlockSpec((TM_EW, w), lambda i: (i, 0))
    vec = pl.BlockSpec((1, w), lambda i: (0, 0))
    return pl.pallas_call(
        body, grid=(l // TM_EW,),
        in_specs=[row, row, vec, vec, pl.BlockSpec((TM_EW, 2 * w), lambda i: (i, 0))],
        out_specs=[row, row, vec, vec],
        out_shape=[jax.ShapeDtypeStruct((l, w), F32), jax.ShapeDtypeStruct((l, w), F32),
                   jax.ShapeDtypeStruct((1, w), F32), jax.ShapeDtypeStruct((1, w), F32)],
        name="mix_bwd", compiler_params=_params(("arbitrary",)),
    )(attn, ys, g_attn, g_ssm, dmixed)


def _rope_tables(l):
    half = ROPE_DIM // 2
    inv_freq = jnp.power(ROPE_THETA, -jnp.arange(half, dtype=F32) / half)
    ang = jnp.arange(l, dtype=F32)[:, None] * inv_freq[None, :]
    cos, sin = jnp.cos(ang), jnp.sin(ang)
    ones = jnp.ones((l, HEAD_DIM - ROPE_DIM), F32)
    zeros = jnp.zeros((l, HEAD_DIM - ROPE_DIM), F32)
    zh = jnp.zeros((l, half), F32)
    c = jnp.concatenate([cos, cos, ones], axis=1)
    s_lo = jnp.concatenate([-sin, zh, zeros], axis=1)
    s_hi = jnp.concatenate([zh, sin, zeros], axis=1)
    return tuple(jnp.tile(t, (1, LANES // HEAD_DIM)) for t in (c, s_lo, s_hi))


def _rope_fwd(proj, tabs):
    l = proj.shape[0]
    nq = ATTN_WIDTH // LANES

    def body(p_ref, c_ref, lo_ref, hi_ref, o_ref):
        c, lo, hi = c_ref[...], lo_ref[...], hi_ref[...]
        for blk in range(nq + 1):
            t = p_ref[:, blk * LANES:(blk + 1) * LANES]
            rot = t * c + pltpu.roll(t, LANES - 8, 1) * lo + pltpu.roll(t, 8, 1) * hi
            o_ref[:, blk * LANES:(blk + 1) * LANES] = rot.astype(BF16)
        o_ref[:, (nq + 1) * LANES:] = p_ref[:, (nq + 1) * LANES:].astype(BF16)

    tab = pl.BlockSpec((TM_EW, LANES), lambda i: (i, 0))
    return pl.pallas_call(
        body, grid=(l // TM_EW,),
        in_specs=[pl.BlockSpec((TM_EW, QKV_WIDTH), lambda i: (i, 0)), tab, tab, tab],
        out_specs=pl.BlockSpec((TM_EW, QKV_WIDTH), lambda i: (i, 0)),
        out_shape=jax.ShapeDtypeStruct((l, QKV_WIDTH), BF16), name="rope_fwd",
        compiler_params=_params(("parallel",)),
    )(proj, *tabs)


def _rope_bwd(dqkv, du_ssm, dpre, d_skip, tabs):
    l = dqkv.shape[0]
    nq = ATTN_WIDTH // LANES

    def body(d_ref, du_ref, dpre_ref, ds_ref, c_ref, lo_ref, hi_ref, o_ref):
        c, lo, hi = c_ref[...], lo_ref[...], hi_ref[...]
        for blk in range(nq + 1):
            t = d_ref[:, blk * LANES:(blk + 1) * LANES]
            g = t * c + pltpu.roll(t * lo, 8, 1) + pltpu.roll(t * hi, LANES - 8, 1)
            o_ref[:, blk * LANES:(blk + 1) * LANES] = g.astype(BF16)
        o_ref[:, (nq + 1) * LANES:QKV_WIDTH] = d_ref[:, (nq + 1) * LANES:].astype(BF16)
        o_ref[:, QKV_WIDTH:] = (du_ref[...] + dpre_ref[...] * ds_ref[...]).astype(BF16)

    tab = pl.BlockSpec((TM_EW, LANES), lambda i: (i, 0))
    wide = pl.BlockSpec((TM_EW, SSM_WIDTH), lambda i: (i, 0))
    return pl.pallas_call(
        body, grid=(l // TM_EW,),
        in_specs=[pl.BlockSpec((TM_EW, QKV_WIDTH), lambda i: (i, 0)), wide, wide,
                  pl.BlockSpec((1, SSM_WIDTH), lambda i: (0, 0)), tab, tab, tab],
        out_specs=pl.BlockSpec((TM_EW, IN_WIDTH), lambda i: (i, 0)),
        out_shape=jax.ShapeDtypeStruct((l, IN_WIDTH), BF16), name="rope_bwd",
        compiler_params=_params(("parallel",)),
    )(dqkv, du_ssm, dpre, d_skip, *tabs)


_Q_COLS = ATTN_WIDTH // LANES
_SCALE = HEAD_DIM ** -0.5
_NEG = -1e30


def _window_specs(nb, width, col):
    return [
        pl.BlockSpec((BLOCK, width), lambda n: (jnp.maximum(n - 1, 0), col)),
        pl.BlockSpec((BLOCK, width), lambda n: (n, col)),
        pl.BlockSpec((BLOCK, width), lambda n: (jnp.minimum(n + 1, nb - 1), col)),
    ]


def _stacked_sink(sink_ref, heads):
    rid = lax.broadcasted_iota(jnp.int32, (len(heads) * BLOCK, 1), 0)
    sk = jnp.full(rid.shape, sink_ref[0, heads[-1]], F32)
    for g in range(len(heads) - 2, -1, -1):
        sk = jnp.where(rid < (g + 1) * BLOCK, sink_ref[0, heads[g]], sk)
    return sk


def _attn_fwd(qkv, sink):
    l = qkv.shape[0]
    nb = l // BLOCK
    grp = N_Q_HEADS // N_KV_HEADS

    def body(sink_ref, q_ref, k0, k1, k2, v0, v1, v2, o_ref, lse_ref):
        n = pl.program_id(0)
        q = q_ref[...]
        kw = jnp.concatenate([k0[...], k1[...], k2[...]], axis=0)
        vw = jnp.concatenate([v0[...], v1[...], v2[...]], axis=0)
        row = lax.broadcasted_iota(jnp.int32, (grp * BLOCK, 3 * BLOCK), 0)
        col = lax.broadcasted_iota(jnp.int32, (grp * BLOCK, 3 * BLOCK), 1)
        valid = jnp.abs(col - BLOCK - (row & (BLOCK - 1))) <= WINDOW
        valid &= jnp.logical_not((n == 0) & (col < BLOCK))
        valid &= jnp.logical_not((n == nb - 1) & (col >= 2 * BLOCK))
        for hk in range(N_KV_HEADS):
            heads = range(hk * grp, (hk + 1) * grp)
            qs = jnp.concatenate([q[:, h * HEAD_DIM:(h + 1) * HEAD_DIM] for h in heads], axis=0)
            kh = kw[:, hk * HEAD_DIM:(hk + 1) * HEAD_DIM]
            vh = vw[:, hk * HEAD_DIM:(hk + 1) * HEAD_DIM]
            s = jnp.where(valid, _dg(qs, kh, NT) * _SCALE, _NEG)
            sk = _stacked_sink(sink_ref, heads)
            m = jnp.maximum(jnp.max(s, axis=1, keepdims=True), sk)
            p = jnp.exp(s - m)
            denom = jnp.sum(p, axis=1, keepdims=True) + jnp.exp(sk - m)
            o = _dg((p / denom).astype(BF16), vh, NN)
            lse = m + jnp.log(denom)
            for g, h in enumerate(heads):
                o_ref[:, h * HEAD_DIM:(h + 1) * HEAD_DIM] = o[g * BLOCK:(g + 1) * BLOCK]
                lse_ref[:, h:h + 1] = lse[g * BLOCK:(g + 1) * BLOCK]

    return pl.pallas_call(
        body, grid=(nb,),
        in_specs=[pl.BlockSpec(memory_space=pltpu.SMEM),
                  pl.BlockSpec((BLOCK, ATTN_WIDTH), lambda n: (n, 0))]
        + _window_specs(nb, KV_WIDTH, _Q_COLS) + _window_specs(nb, KV_WIDTH, _Q_COLS + 1),
        out_specs=[pl.BlockSpec((BLOCK, ATTN_WIDTH), lambda n: (n, 0)),
                   pl.BlockSpec((BLOCK, N_Q_HEADS), lambda n: (n, 0))],
        out_shape=[jax.ShapeDtypeStruct((l, ATTN_WIDTH), F32), jax.ShapeDtypeStruct((l, N_Q_HEADS), F32)],
        name="attn_fwd", compiler_params=_params(("parallel",)),
    )(sink, qkv, qkv, qkv, qkv, qkv, qkv, qkv)


def _attn_bwd(qkv, attn, dattn, lse, sink):
    l = qkv.shape[0]
    nb = l // BLOCK
    grp = N_Q_HEADS // N_KV_HEADS

    def body(sink_ref, q0, q1, q2, k0, k1, k2, v0, v1, v2, o0, o1, o2, d0, d1, d2,
             l0, l1, l2, dqkv_ref, dsink_ref):
        n = pl.program_id(0)
        first, last = n == 0, n == nb - 1

        @pl.when(first)
        def _():
            dsink_ref[...] = jnp.zeros_like(dsink_ref)

        cat = lambda a, b, c: jnp.concatenate([a[...], b[...], c[...]], axis=0)
        qw, kw, vw = cat(q0, q1, q2), cat(k0, k1, k2), cat(v0, v1, v2)
        dow = cat(d0, d1, d2)
        prodw = cat(o0, o1, o2) * dow
        lsew = cat(l0, l1, l2)
        dob = dow.astype(BF16)
        win = 3 * BLOCK
        mid = slice(BLOCK, 2 * BLOCK)

        row = lax.broadcasted_iota(jnp.int32, (grp * BLOCK, win), 0)
        col = lax.broadcasted_iota(jnp.int32, (grp * BLOCK, win), 1)
        valid_q = jnp.abs(col - BLOCK - (row & (BLOCK - 1))) <= WINDOW
        valid_q &= jnp.logical_not(first & (col < BLOCK))
        valid_q &= jnp.logical_not(last & (col >= 2 * BLOCK))
        rowk = lax.broadcasted_iota(jnp.int32, (grp * win, BLOCK), 0)
        colk = lax.broadcasted_iota(jnp.int32, (grp * win, BLOCK), 1)
        for g in range(1, grp):
            rowk = jnp.where(rowk >= win, rowk - win, rowk)
        valid_k = jnp.abs(colk + BLOCK - rowk) <= WINDOW
        valid_k &= jnp.logical_not(first & (rowk < BLOCK))
        valid_k &= jnp.logical_not(last & (rowk >= 2 * BLOCK))

        dsink_parts = []
        for hk in range(N_KV_HEADS):
            heads = range(hk * grp, (hk + 1) * grp)
            ksl = slice(hk * HEAD_DIM, (hk + 1) * HEAD_DIM)
            hsl = [slice(h * HEAD_DIM, (h + 1) * HEAD_DIM) for h in heads]
            stack = lambda parts: jnp.concatenate(parts, axis=0)
            qws = stack([qw[:, s_] for s_ in hsl])
            dows = stack([dob[:, s_] for s_ in hsl])
            deltaws = stack([jnp.sum(prodw[:, s_], axis=1, keepdims=True) for s_ in hsl])
            lsews = stack([lsew[:, h:h + 1] for h in heads])
            of_block = lambda t: stack([t[g * win + BLOCK:g * win + 2 * BLOCK] for g in range(grp)])
            qs, dos, deltas, lses = of_block(qws), of_block(dows), of_block(deltaws), of_block(lsews)
            kh, vh = kw[:, ksl], vw[:, ksl]
            s = jnp.where(valid_q, _dg(qs, kh, NT) * _SCALE, _NEG)
            p = jnp.exp(s - lses)
            dp = _dg(dos, vh, NT)
            ds = (p * (dp - deltas) * _SCALE).astype(BF16)
            dq = _dg(ds, kh, NN)
            sink_rows = jnp.exp(_stacked_sink(sink_ref, heads) - lses) * deltas
            for g, h in enumerate(heads):
                dqkv_ref[:, hsl[g]] = dq[g * BLOCK:(g + 1) * BLOCK]
                dsink_parts.append(jnp.sum(sink_rows[g * BLOCK:(g + 1) * BLOCK], axis=0, keepdims=True))
            s2 = jnp.where(valid_k, _dg(qws, kh[mid], NT) * _SCALE, _NEG)
            p2 = jnp.exp(s2 - lsews)
            dv = _dg(p2.astype(BF16), dows, TN)
            dp2 = _dg(dows, vh[mid], NT)
            ds2 = (p2 * (dp2 - deltaws) * _SCALE).astype(BF16)
            dk = _dg(ds2, qws, TN)
            dqkv_ref[:, ATTN_WIDTH + hk * HEAD_DIM:ATTN_WIDTH + (hk + 1) * HEAD_DIM] = dk
            dqkv_ref[:, ATTN_WIDTH + KV_WIDTH + hk * HEAD_DIM:ATTN_WIDTH + KV_WIDTH + (hk + 1) * HEAD_DIM] = dv
        dsink_ref[...] -= jnp.concatenate(dsink_parts, axis=1)

    return pl.pallas_call(
        body, grid=(nb,),
        in_specs=[pl.BlockSpec(memory_space=pltpu.SMEM)]
        + _window_specs(nb, ATTN_WIDTH, 0)
        + _window_specs(nb, KV_WIDTH, _Q_COLS) + _window_specs(nb, KV_WIDTH, _Q_COLS + 1)
        + _window_specs(nb, ATTN_WIDTH, 0) + _window_specs(nb, ATTN_WIDTH, 0)
        + _window_specs(nb, N_Q_HEADS, 0),
        out_specs=[pl.BlockSpec((BLOCK, QKV_WIDTH), lambda n: (n, 0)),
                   pl.BlockSpec((1, N_Q_HEADS), lambda n: (0, 0))],
        out_shape=[jax.ShapeDtypeStruct((l, QKV_WIDTH), F32), jax.ShapeDtypeStruct((1, N_Q_HEADS), F32)],
        name="attn_bwd", compiler_params=_params(("arbitrary",)),
    )(sink, qkv, qkv, qkv, qkv, qkv, qkv, qkv, qkv, qkv, attn, attn, attn,
      dattn, dattn, dattn, lse, lse, lse)


def _ssm_disc(a_re, a_im, log_step, b_re, b_im):
    step = jnp.exp(log_step)[..., None]
    mag = jnp.exp(a_re * step)
    lb_re, lb_im = mag * jnp.cos(a_im * step), mag * jnp.sin(a_im * step)
    nr, ni = lb_re - 1.0, lb_im
    den = a_re * a_re + a_im * a_im
    f_re = ((nr * a_re + ni * a_im) / den)[..., None]
    f_im = ((ni * a_re - nr * a_im) / den)[..., None]
    return lb_re, lb_im, f_re * b_re - f_im * b_im, f_re * b_im + f_im * b_re


def _ssm_pack(lb_re, lb_im, bb_re, bb_im, c_re, c_im):
    eye = jnp.eye(SSM_CH // SSM_GROUP, dtype=F32)
    ng = SSM_CH // SSM_GROUP

    def diag_b(bb):
        t = bb.reshape(2, SSM_CB, ng, SSM_STATE, SSM_GROUP)
        return jnp.einsum('dkgpc,gh->dkgchp', t, eye).reshape(2, SSM_CB, SSM_CH, SSM_ST)

    def diag_c(cc):
        t = cc.reshape(2, SSM_CB, ng, SSM_GROUP, SSM_STATE)
        return jnp.einsum('dkgcp,gh->dkhpgc', t, eye).reshape(2, SSM_CB, SSM_ST, SSM_CH)

    bcat = jnp.concatenate([diag_b(bb_re), diag_b(bb_im)], axis=-1)
    ccat = jnp.concatenate([diag_c(c_re), -diag_c(c_im)], axis=-2)
    lam_re = lb_re.reshape(2, SSM_CB, 1, SSM_ST)
    lam_im = lb_im.reshape(2, SSM_CB, 1, SSM_ST)
    return bcat, ccat, lam_re, lam_im


def _ssm_unpack(dbcat, dccat, dlam_re, dlam_im):
    ng = SSM_CH // SSM_GROUP
    eye = jnp.eye(ng, dtype=F32)

    def undiag_b(t):
        t = t.reshape(2, SSM_CB, ng, SSM_GROUP, ng, SSM_STATE)
        return jnp.einsum('dkgchp,gh->dkgpc', t, eye).reshape(2, N_SSM_GROUPS, SSM_STATE, SSM_GROUP)

    def undiag_c(t):
        t = t.reshape(2, SSM_CB, ng, SSM_STATE, ng, SSM_GROUP)
        return jnp.einsum('dkhpgc,gh->dkgcp', t, eye).reshape(2, N_SSM_GROUPS, SSM_GROUP, SSM_STATE)

    dbb_re, dbb_im = undiag_b(dbcat[..., :SSM_ST]), undiag_b(dbcat[..., SSM_ST:])
    dc_re, dc_im = undiag_c(dccat[:, :, :SSM_ST]), -undiag_c(dccat[:, :, SSM_ST:])
    shape = (2, N_SSM_GROUPS, SSM_STATE)
    return dlam_re.reshape(shape), dlam_im.reshape(shape), dbb_re, dbb_im, dc_re, dc_im


def _to_segments(t):
    l, w = t.shape
    return t.reshape(N_SEG, l // N_SEG, w).transpose(1, 0, 2).reshape(l, w)


def _from_segments(t):
    l, w = t.shape
    return t.reshape(l // N_SEG, N_SEG, w).transpose(1, 0, 2).reshape(l, w)


SCAN_UNROLL = 4


def _cfma(ar, ai, xr, xi, br, bi):
    return ar * xr - ai * xi + br, ar * xi + ai * xr + bi


def _scan_segments(xs_ref, ar, ai, rev, nj, prev_ref=None, before_sums=None):
    shape = (N_SEG, SSM_ST)
    ar = jnp.broadcast_to(ar, shape)
    ai = jnp.broadcast_to(ai, shape)
    zero = jnp.zeros(shape, F32)
    re_cols, im_cols = pl.ds(0, SSM_ST), pl.ds(SSM_ST, SSM_ST)

    def rows_of(jj):
        j = jnp.where(rev, nj - 1 - jj, jj)
        return j, pl.ds(pl.multiple_of(j * N_SEG, N_SEG), N_SEG)

    def steps(step, init, last_step=None):
        def outer(o, carry):
            for k in range(SCAN_UNROLL):
                carry = step(o * SCAN_UNROLL + k, carry)
            return carry

        carry = lax.fori_loop(0, nj // SCAN_UNROLL - 1, outer, init)
        for jj in range(nj - SCAN_UNROLL, nj):
            carry = (last_step if last_step is not None and jj == nj - 1 else step)(jj, carry)
        return carry

    def pass1(jj, carry):
        _, rows = rows_of(jj)
        return _cfma(ar, ai, carry[0], carry[1], xs_ref[rows, re_cols], xs_ref[rows, im_cols])

    end_r, end_i = steps(pass1, (zero, zero))

    pr, pi = ar, ai
    for _ in range(int(math.log2(nj))):
        pr, pi = pr * pr - pi * pi, 2.0 * pr * pi
    seg = lax.broadcasted_iota(jnp.int32, shape, 0)

    def chain(shift, keep):
        ir, ii = zero, zero
        for _ in range(N_SEG - 1):
            tr, ti = _cfma(pr, pi, ir, ii, end_r, end_i)
            ir = jnp.where(keep, pltpu.roll(tr, shift, 0), 0.0)
            ii = jnp.where(keep, pltpu.roll(ti, shift, 0), 0.0)
        return ir, ii

    up_r, up_i = chain(1, seg >= 1)
    dn_r, dn_i = chain(N_SEG - 1, seg <= N_SEG - 2)
    init_r, init_i = jnp.where(rev, dn_r, up_r), jnp.where(rev, dn_i, up_i)

    def pass2(jj, carry):
        j, rows = rows_of(jj)
        nr, ni = _cfma(ar, ai, carry[0], carry[1], xs_ref[rows, re_cols], xs_ref[rows, im_cols])
        xs_ref[rows, re_cols] = nr
        xs_ref[rows, im_cols] = ni
        return (j, nr, ni) + tuple(carry[2:])

    def pass2_plain(jj, carry):
        return pass2(jj, carry)[1:]

    def pass2_sums(jj, carry):
        j, nr, ni, acc_r, acc_i = pass2(jj, carry)
        jp = jnp.where(rev, j - 1, j + 1)
        prow = pl.ds(pl.multiple_of(jp * N_SEG, N_SEG), N_SEG)
        xr, xi = prev_ref[prow, re_cols], prev_ref[prow, im_cols]
        return nr, ni, acc_r + (nr * xr + ni * xi), acc_i + (ni * xr - nr * xi)

    if prev_ref is None:
        steps(pass2_plain, (init_r, init_i))
        return init_r, init_i, None, None
    if before_sums is not None:
        before_sums()
    _, _, acc_r, acc_i = steps(pass2_sums, (init_r, init_i, zero, zero), last_step=pass2_plain)
    return init_r, init_i, acc_r, acc_i


SSM_RC = 256


def _ssm_specs(l):
    act = pl.BlockSpec((l, SSM_CH), lambda k, d: (0, k))
    bmat = pl.BlockSpec((None, None, SSM_CH, 2 * SSM_ST), lambda k, d: (d, k, 0, 0))
    cmat = pl.BlockSpec((None, None, 2 * SSM_ST, SSM_CH), lambda k, d: (d, k, 0, 0))
    lam = pl.BlockSpec((None, None, 1, SSM_ST), lambda k, d: (d, k, 0, 0))
    return act, bmat, cmat, lam


def _ssm_fwd(u_seg, bcat, ccat, lam_re, lam_im):
    l = u_seg.shape[0]
    nj = l // N_SEG

    def body(u_ref, b_ref, c_ref, lr_ref, li_ref, y_ref, keep_ref, xs_ref, keep_sem):
        k, d = pl.program_id(0), pl.program_id(1)

        def bu_chunk(i, _):
            rows = pl.ds(pl.multiple_of(i * SSM_RC, SSM_RC), SSM_RC)
            xs_ref[rows, :] = _dg(u_ref[rows, :], b_ref[...], NN)
            return 0

        lax.fori_loop(0, l // SSM_RC, bu_chunk, 0)
        _scan_segments(xs_ref, lr_ref[...], li_ref[...], d == 1, nj)
        keep = pltpu.make_async_copy(xs_ref, keep_ref.at[d, k], keep_sem)
        keep.start()

        def y_chunk(i, _):
            rows = pl.ds(pl.multiple_of(i * SSM_RC, SSM_RC), SSM_RC)
            yv = _dg(xs_ref[rows, :].astype(BF16), c_ref[...], NN)

            @pl.when(d == 0)
            def _():
                y_ref[rows, :] = yv

            @pl.when(d == 1)
            def _():
                y_ref[rows, :] += yv

            return 0

        lax.fori_loop(0, l // SSM_RC, y_chunk, 0)
        keep.wait()

    act, bmat, cmat, lam = _ssm_specs(l)
    return pl.pallas_call(
        body, grid=(SSM_CB, 2), in_specs=[act, bmat, cmat, lam, lam], out_specs=[act, ANY],
        out_shape=[jax.ShapeDtypeStruct((l, SSM_WIDTH), F32),
                   jax.ShapeDtypeStruct((2, SSM_CB, l, 2 * SSM_ST), F32)],
        scratch_shapes=[pltpu.VMEM((l, 2 * SSM_ST), F32), pltpu.SemaphoreType.DMA],
        name="ssm_fwd", compiler_params=_params(("parallel", "arbitrary"), vmem_mb=56),
    )(u_seg, bcat.astype(BF16), ccat.astype(BF16), lam_re, lam_im)


def _ssm_bwd(u_seg, dy_seg, states, bcat, ccat, lam_re, lam_im):
    l = u_seg.shape[0]
    nj = l // N_SEG

    rc2 = min(l, 2 * SSM_RC)

    def body(u_ref, dy_ref, keep_ref, b_ref, c_ref, lr_ref, li_ref,
             du_ref, db_ref, dc_ref, dlr_ref, dli_ref, xs_ref, gs_ref, keep_sem):
        k, d = pl.program_id(0), pl.program_id(1)
        rev = d == 1
        ar, ai = lr_ref[...], li_ref[...]
        fetch = pltpu.make_async_copy(keep_ref.at[d, k], xs_ref, keep_sem)
        fetch.start()

        def chunk1(i, _):
            rows = pl.ds(pl.multiple_of(i * SSM_RC, SSM_RC), SSM_RC)
            gs_ref[rows, :] = _dg(dy_ref[rows, :], c_ref[...], NT)
            return 0

        lax.fori_loop(0, l // SSM_RC, chunk1, 0)
        _, _, acc_r, acc_i = _scan_segments(gs_ref, ar, -ai, jnp.logical_not(rev), nj, prev_ref=xs_ref,
                                            before_sums=fetch.wait)
        seg = lax.broadcasted_iota(jnp.int32, (N_SEG, SSM_ST), 0)
        jb = jnp.where(rev, nj - 1, 0)
        brow = pl.ds(pl.multiple_of(jb * N_SEG, N_SEG), N_SEG)
        erow = pl.ds(pl.multiple_of((nj - 1 - jb) * N_SEG, N_SEG), N_SEG)
        re_cols, im_cols = pl.ds(0, SSM_ST), pl.ds(SSM_ST, SSM_ST)

        def before(t):
            up = jnp.where(seg >= 1, pltpu.roll(t, 1, 0), 0.0)
            down = jnp.where(seg <= N_SEG - 2, pltpu.roll(t, N_SEG - 1, 0), 0.0)
            return jnp.where(rev, down, up)

        init_r, init_i = before(xs_ref[erow, re_cols]), before(xs_ref[erow, im_cols])
        gr, gi = gs_ref[brow, re_cols], gs_ref[brow, im_cols]
        acc_r = acc_r + gr * init_r + gi * init_i
        acc_i = acc_i + gi * init_r - gr * init_i
        dlr_ref[...] = jnp.sum(acc_r, axis=0, keepdims=True)
        dli_ref[...] = jnp.sum(acc_i, axis=0, keepdims=True)

        db_ref[...] = jnp.zeros_like(db_ref)
        dc_ref[...] = jnp.zeros_like(dc_ref)

        def chunk2(i, _):
            rows = pl.ds(pl.multiple_of(i * rc2, rc2), rc2)
            g = gs_ref[rows, :].astype(BF16)
            dc_ref[...] += _dg(xs_ref[rows, :].astype(BF16), dy_ref[rows, :], TN)
            db_ref[...] += _dg(u_ref[rows, :], g, TN)
            duv = _dg(g, b_ref[...], NT)

            @pl.when(d == 0)
            def _():
                du_ref[rows, :] = duv

            @pl.when(d == 1)
            def _():
                du_ref[rows, :] += duv

            return 0

        lax.fori_loop(0, l // rc2, chunk2, 0)

    act, bmat, cmat, lam = _ssm_specs(l)
    return pl.pallas_call(
        body, grid=(SSM_CB, 2), in_specs=[act, act, ANY, bmat, cmat, lam, lam],
        out_specs=[act, bmat, cmat, lam, lam],
        out_shape=[jax.ShapeDtypeStruct((l, SSM_WIDTH), F32),
                   jax.ShapeDtypeStruct(bcat.shape, F32), jax.ShapeDtypeStruct(ccat.shape, F32),
                   jax.ShapeDtypeStruct(lam_re.shape, F32), jax.ShapeDtypeStruct(lam_im.shape, F32)],
        scratch_shapes=[pltpu.VMEM((l, 2 * SSM_ST), F32), pltpu.VMEM((l, 2 * SSM_ST), F32),
                        pltpu.SemaphoreType.DMA],
        name="ssm_bwd", compiler_params=_params(("parallel", "arbitrary"), vmem_mb=56),
    )(u_seg, dy_seg, states, bcat.astype(BF16), ccat.astype(BF16), lam_re, lam_im)


def _glu_fwd(y_ssm, u, d_skip, w_glu):
    l, w = u.shape

    def body(y_ref, u_ref, d_ref, w_ref, pre_ref, s_ref, ys_ref):
        pre = y_ref[...] + d_ref[...] * u_ref[...]
        z = _gelu(pre)
        s = _dg(z.astype(BF16), w_ref[...], NN)
        pre_ref[...] = pre
        s_ref[...] = s
        ys_ref[...] = z * _sigmoid(s)

    row = pl.BlockSpec((TM_EW, w), lambda i: (i, 0))
    out = jax.ShapeDtypeStruct((l, w), F32)
    return pl.pallas_call(
        body, grid=(l // TM_EW,),
        in_specs=[row, row, pl.BlockSpec((1, w), lambda i: (0, 0)), pl.BlockSpec((w, w), lambda i: (0, 0))],
        out_specs=[row, row, row], out_shape=[out, out, out], name="glu_fwd",
        compiler_params=_params(("parallel",)),
    )(y_ssm, u, d_skip, w_glu)


def _glu_bwd(pre, s, dys, u, d_skip, w_glu):
    l, w = u.shape

    def body(pre_ref, s_ref, dys_ref, u_ref, d_ref, w_ref, dpre_ref, z_ref, ds_ref, dd_ref):
        pre, dys = pre_ref[...], dys_ref[...]
        z = _gelu(pre)
        sig = _sigmoid(s_ref[...])
        ds = (dys * z * sig * (1.0 - sig)).astype(BF16)
        dz = dys * sig + _dg(ds, w_ref[...], NT)
        dpre = dz * _gelu_grad(pre)
        dpre_ref[...] = dpre
        z_ref[...] = z.astype(BF16)
        ds_ref[...] = ds

        @pl.when(pl.program_id(0) == 0)
        def _():
            dd_ref[...] = jnp.zeros_like(dd_ref)

        dd_ref[...] += jnp.sum(dpre * u_ref[...], axis=0, keepdims=True)

    row = pl.BlockSpec((TM_EW, w), lambda i: (i, 0))
    vec = pl.BlockSpec((1, w), lambda i: (0, 0))
    return pl.pallas_call(
        body, grid=(l // TM_EW,),
        in_specs=[row, row, row, row, vec, pl.BlockSpec((w, w), lambda i: (0, 0))],
        out_specs=[row, row, row, vec],
        out_shape=[jax.ShapeDtypeStruct((l, w), F32), jax.ShapeDtypeStruct((l, w), BF16),
                   jax.ShapeDtypeStruct((l, w), BF16), jax.ShapeDtypeStruct((1, w), F32)],
        name="glu_bwd", compiler_params=_params(("arbitrary",)),
    )(pre, s, dys, u, d_skip, w_glu)


TM_CV = 512
TC_CV = 256
HALO = SUBLANES


def _conv_specs(l, col0):
    per = TM_CV // HALO
    nh = l // HALO
    off = col0 // TC_CV
    return [
        pl.BlockSpec((HALO, TC_CV), lambda j, i: (jnp.maximum(i * per - 1, 0), j + off)),
        pl.BlockSpec((TM_CV, TC_CV), lambda j, i: (i, j + off)),
        pl.BlockSpec((HALO, TC_CV), lambda j, i: (jnp.minimum((i + 1) * per, nh - 1), j + off)),
    ]


def _ext(prev_ref, mid_ref, next_ref, first, last):
    p = jnp.where(first, 0.0, prev_ref[...])
    n = jnp.where(last, 0.0, next_ref[...])
    return jnp.concatenate([p, mid_ref[...], n], axis=0)


def _shift_dn(t):
    return pltpu.roll(t, 1, 0)


def _shift_up(t):
    return pltpu.roll(t, t.shape[0] - 1, 0)


def _conv3(e, w_ref, b_ref):
    return w_ref[0:1, :] * _shift_dn(e) + w_ref[1:2, :] * e + w_ref[2:3, :] * _shift_up(e) + b_ref[...]


def _convffn_fwd(up_pre, conv_w, conv_b):
    l = up_pre.shape[0]
    ni = l // TM_CV
    wspec = lambda off: pl.BlockSpec((3, TC_CV), lambda j, i: (0, j + off))
    bspec = lambda off: pl.BlockSpec((1, TC_CV), lambda j, i: (0, j + off))
    voff = D_FF // TC_CV

    def body(gp, gm, gn, vp, vm, vn, wg, bg, wv, bv, o_ref):
        i = pl.program_id(1)
        first, last = i == 0, i == ni - 1
        gate = _conv3(_ext(gp, gm, gn, first, last), wg, bg)[HALO:HALO + TM_CV]
        val = _conv3(_ext(vp, vm, vn, first, last), wv, bv)[HALO:HALO + TM_CV]
        o_ref[...] = (gate * _sigmoid(gate) * val).astype(BF16)

    return pl.pallas_call(
        body, grid=(D_FF // TC_CV, ni),
        in_specs=_conv_specs(l, 0) + _conv_specs(l, D_FF) + [wspec(0), bspec(0), wspec(voff), bspec(voff)],
        out_specs=pl.BlockSpec((TM_CV, TC_CV), lambda j, i: (i, j)),
        out_shape=jax.ShapeDtypeStruct((l, D_FF), BF16), name="convffn_fwd",
        compiler_params=_params(("parallel", "parallel")),
    )(up_pre, up_pre, up_pre, up_pre, up_pre, up_pre, conv_w, conv_b, conv_w, conv_b)


def _convffn_bwd(up_pre, dact, conv_w, conv_b):
    l = up_pre.shape[0]
    ni = l // TM_CV
    wspec = lambda off: pl.BlockSpec((3, TC_CV), lambda j, i: (0, j + off))
    bspec = lambda off: pl.BlockSpec((1, TC_CV), lambda j, i: (0, j + off))
    voff = D_FF // TC_CV

    def body(gp, gm, gn, vp, vm, vn, dp, dm, dn, wg, bg, wv, bv, dup_ref, pg_ref, pv_ref):
        i = pl.program_id(1)
        first, last = i == 0, i == ni - 1
        ge, ve, de = _ext(gp, gm, gn, first, last), _ext(vp, vm, vn, first, last), _ext(dp, dm, dn, first, last)
        gate, val = _conv3(ge, wg, bg), _conv3(ve, wv, bv)
        sig = _sigmoid(gate)
        silu = gate * sig
        dgate = de * val * (sig + silu * (1.0 - sig))
        dval = de * silu
        mid = slice(HALO, HALO + TM_CV)
        rid = lax.broadcasted_iota(jnp.int32, (SUBLANES, TC_CV), 0)

        @pl.when(i == 0)
        def _():
            pg_ref[...] = jnp.zeros_like(pg_ref)
            pv_ref[...] = jnp.zeros_like(pv_ref)

        for half, (dup, e, w_ref, p_ref) in enumerate(((dgate, ge, wg, pg_ref), (dval, ve, wv, pv_ref))):
            dpre = w_ref[0:1, :] * _shift_up(dup) + w_ref[1:2, :] * dup + w_ref[2:3, :] * _shift_dn(dup)
            dup_ref[half] = dpre[mid].astype(BF16)
            dm_ = dup[mid]
            sums = [jnp.sum(dm_ * _shift_dn(e)[mid], axis=0, keepdims=True),
                    jnp.sum(dm_ * e[mid], axis=0, keepdims=True),
                    jnp.sum(dm_ * _shift_up(e)[mid], axis=0, keepdims=True),
                    jnp.sum(dm_, axis=0, keepdims=True)]
            acc = jnp.zeros((SUBLANES, TC_CV), F32)
            for k, sk in enumerate(sums):
                acc = jnp.where(rid == k, sk, acc)
            p_ref[...] += acc

    par = pl.BlockSpec((SUBLANES, TC_CV), lambda j, i: (0, j))
    dup, pg, pv = pl.pallas_call(
        body, grid=(D_FF // TC_CV, ni),
        in_specs=_conv_specs(l, 0) + _conv_specs(l, D_FF) + _conv_specs(l, 0)
        + [wspec(0), bspec(0), wspec(voff), bspec(voff)],
        out_specs=[pl.BlockSpec((2, TM_CV, TC_CV), lambda j, i: (0, i, j)), par, par],
        out_shape=[jax.ShapeDtypeStruct((2, l, D_FF), BF16),
                   jax.ShapeDtypeStruct((SUBLANES, D_FF), F32), jax.ShapeDtypeStruct((SUBLANES, D_FF), F32)],
        name="convffn_bwd", compiler_params=_params(("parallel", "arbitrary")),
    )(up_pre, up_pre, up_pre, up_pre, up_pre, up_pre, dact, dact, dact, conv_w, conv_b, conv_w, conv_b)
    return dup, jnp.concatenate([pg, pv], axis=1)


def _local_step(x, target, wb, sp):
    l = x.shape[0]
    tabs = _rope_tables(l)
    disc = _ssm_disc(sp["a_re"], sp["a_im"], sp["log_step"], sp["b_re"], sp["b_im"])
    bcat, ccat, lam_re, lam_im = _ssm_pack(*disc, sp["c_re"], sp["c_im"])
    d_skip = sp["d_skip"].reshape(1, SSM_WIDTH)

    big = min(l, 1024)
    h = _rms_fwd(x, sp["norm_mix_g"], "rms_mix")
    proj, u = _mm_split(h, wb["w_in"], big, QKV_WIDTH, "mm_in")
    qkv = _rope_fwd(proj, tabs)
    attn, lse = _attn_fwd(qkv, sp["sink"])
    u_seg = _to_segments(u).astype(BF16)
    y_seg, states = _ssm_fwd(u_seg, bcat, ccat, lam_re, lam_im)
    y_ssm = _from_segments(y_seg)
    pre, s_glu, ys = _glu_fwd(y_ssm, u, d_skip, wb["w_glu"])
    mixed = _mix_fwd(attn, ys, sp["norm_attn_g"], sp["norm_ssm_g"])
    x1 = _mm_nn(mixed, wb["w_out"], big, 1024, F32, "mm_out", res=x)
    h2 = _rms_fwd(x1, sp["norm_ffn_g"], "rms_ffn")
    up_pre = _mm_nn_cols(h2, wb["w_up"], big, "mm_up")
    act = _convffn_fwd(up_pre, sp["conv_w"], sp["conv_b"])
    x2 = _mm_nn(act, wb["w_down"], big, 512, F32, "mm_down", res=x1)
    loss, dx2, dx2b, d_final_g = _final_loss(x2, sp["norm_final_g"].reshape(1, D_MODEL), target)

    g = {"norm_final_g": d_final_g.reshape(D_MODEL)}
    dact = _mm_nt(dx2b, wb["w_down"], big, D_FF // 2, F32, "mm_down_dx")
    g["w_down"] = _mm_tn(act, dx2b, D_FF // 2, 512, "mm_down_dw")
    dup_pre, conv_par = _convffn_bwd(up_pre, dact, sp["conv_w"], sp["conv_b"])
    g["conv_w"], g["conv_b"] = conv_par[0:3], conv_par[3:4]
    g["w_up"] = _mm_tn_cols(h2, dup_pre, wb["w_up"].shape[0], 512, "mm_up_dw")
    dh2 = _mm_nt_cols(dup_pre, wb["w_up"], 512, "mm_up_dx")
    dx1, dx1b, g["norm_ffn_g"] = _rms_bwd(x1, sp["norm_ffn_g"], dh2, dx2, "rms_ffn_bwd")
    dmixed = _mm_nt(dx1b, wb["w_out"], big, 1024, F32, "mm_out_dx")
    g["w_out"] = _mm_tn(mixed, dx1b, 1024, 1024, "mm_out_dw")
    dattn, dys, g["norm_attn_g"], g["norm_ssm_g"] = _mix_bwd(attn, ys, sp["norm_attn_g"], sp["norm_ssm_g"], dmixed)
    dpre, zb, dsb, dd = _glu_bwd(pre, s_glu, dys, u, d_skip, wb["w_glu"])
    g["d_skip"] = dd.reshape(N_SSM_GROUPS, SSM_GROUP)
    g["w_glu"] = _mm_tn(zb, dsb, 512, 512, "mm_glu_dw")
    du_seg, dbcat, dccat, dlam_re, dlam_im = _ssm_bwd(u_seg, _to_segments(dpre).astype(BF16), states, bcat, ccat,
                                                      lam_re, lam_im)
    dlb_re, dlb_im, dbb_re, dbb_im, g["c_re"], g["c_im"] = _ssm_unpack(dbcat, dccat, dlam_re, dlam_im)
    _, disc_vjp = jax.vjp(_ssm_disc, sp["a_re"], sp["a_im"], sp["log_step"], sp["b_re"], sp["b_im"])
    g["a_re"], g["a_im"], g["log_step"], g["b_re"], g["b_im"] = disc_vjp((dlb_re, dlb_im, dbb_re, dbb_im))
    dqkv, dsink = _attn_bwd(qkv, attn, dattn, lse, sp["sink"])
    g["sink"] = dsink
    dproj = _rope_bwd(dqkv, _from_segments(du_seg), dpre, d_skip, tabs)
    g["w_in"] = _mm_tn(h, dproj, 512, IN_WIDTH, "mm_in_dw")
    dh = _mm_nt(dproj, wb["w_in"], 512, 512, F32, "mm_in_dx")
    grad_x, _, g["norm_mix_g"] = _rms_bwd(x, sp["norm_mix_g"], dh, dx1, "rms_mix_bwd")
    return loss, grad_x, g


MESH = pl.DeviceIdType.MESH
ANY = pl.BlockSpec(memory_space=pl.ANY)


def _place():
    x, y, c = lax.axis_index("x"), lax.axis_index("y"), lax.axis_index("c")
    chips = [(1 - x, y), (x, 1 - y), (1 - x, 1 - y)]
    return x, y, c, chips


def _chip_index(px, py):
    return 2 * px + py


CHUNK_BYTES = 256 * 1024
MAX_CHUNKS = 16


def _row_chunks(rows, row_bytes, align):
    n = max(1, min(MAX_CHUNKS, (rows * row_bytes) // CHUNK_BYTES))
    per = -(-rows // n)
    per = -(-per // align) * align
    return [(r0, min(per, rows - r0)) for r0 in range(0, rows, per)]


def _align_of(dtype):
    return SUBLANES * 4 // jnp.dtype(dtype).itemsize


def _remote(src, dst, send_sem, recv_sem, to):
    return pltpu.make_async_remote_copy(src_ref=src, dst_ref=dst, send_sem=send_sem, recv_sem=recv_sem,
                                        device_id=to, device_id_type=MESH)


CAST_ROWS = 64


def _gather_weights(shards, dtypes):
    nw = len(shards)

    def body(*refs):
        w_refs, o_refs = refs[:nw], refs[nw:2 * nw]
        send_sems, recv_sems, in_sems, out_sems = refs[2 * nw:2 * nw + 4]
        raw, cast = refs[2 * nw + 4:3 * nw + 4], refs[3 * nw + 4:]
        x, y, c, chips = _place()
        mine = _chip_index(x, y)
        sibling = (x, y, 1 - c)

        def rows_of(ref, chip, r0, nr):
            return ref.at[chip, pl.ds(r0, nr), :]

        def copy(wi, k, src, dst, to):
            return _remote(src, dst, send_sems.at[wi, k], recv_sems.at[wi, k], to)

        geo = []
        for wi in range(nw):
            rows, cols = w_refs[wi].shape
            row_bytes = cols * jnp.dtype(dtypes[wi]).itemsize
            geo.append((rows // 2, _row_chunks(rows // 2, row_bytes, _align_of(dtypes[wi]))))

        stage_in = [pltpu.make_async_copy(w_refs[wi], raw[wi], in_sems.at[wi]) for wi in range(nw)]
        for cp in stage_in:
            cp.start()
        staged = [raw[wi] if dtypes[wi] == w_refs[wi].dtype else cast[wi] for wi in range(nw)]
        stage_out = []
        for wi in range(nw):
            stage_in[wi].wait()
            if staged[wi] is not raw[wi]:
                def cast_rows(i, _, wi=wi):
                    rows = pl.ds(pl.multiple_of(i * CAST_ROWS, CAST_ROWS), CAST_ROWS)
                    cast[wi][rows, :] = raw[wi][rows, :].astype(dtypes[wi])
                    return 0

                lax.fori_loop(0, w_refs[wi].shape[0] // CAST_ROWS, cast_rows, 0)
            cp = pltpu.make_async_copy(staged[wi], o_refs[wi].at[mine], out_sems.at[wi])
            cp.start()
            stage_out.append(cp)

        for wi in range(nw):
            hr, half_chunks = geo[wi]
            for j, chip in enumerate(chips):
                for r0, nr in half_chunks:
                    copy(wi, j, staged[wi].at[pl.ds(c * hr + r0, nr), :],
                         rows_of(o_refs[wi], mine, c * hr + r0, nr), (*chip, c)).start()
        for wi in range(nw):
            hr, half_chunks = geo[wi]
            for j, chip in enumerate(chips):
                got = rows_of(o_refs[wi], _chip_index(*chip), c * hr, hr)
                copy(wi, j, got, got, (*chip, c)).wait_recv()
                for r0, nr in half_chunks:
                    piece = rows_of(o_refs[wi], _chip_index(*chip), c * hr + r0, nr)
                    copy(wi, 3 + j, piece, piece, sibling).start()
        for wi in range(nw):
            hr = geo[wi][0]
            for j, chip in enumerate(chips):
                got = rows_of(o_refs[wi], _chip_index(*chip), (1 - c) * hr, hr)
                copy(wi, 3 + j, got, got, sibling).wait_recv()
        for wi in range(nw):
            hr = geo[wi][0]
            sent = rows_of(o_refs[wi], mine, c * hr, hr)
            for k in range(6):
                copy(wi, k, sent, sent, sibling).wait_send()
            stage_out[wi].wait()

    return pl.pallas_call(
        body, in_specs=[ANY] * nw, out_specs=[ANY] * nw,
        out_shape=[jax.ShapeDtypeStruct((4, *s.shape), t) for s, t in zip(shards, dtypes)],
        scratch_shapes=[pltpu.SemaphoreType.DMA((nw, 6)), pltpu.SemaphoreType.DMA((nw, 6)),
                        pltpu.SemaphoreType.DMA((nw,)), pltpu.SemaphoreType.DMA((nw,))]
        + [pltpu.VMEM(s.shape, s.dtype) for s in shards] + [pltpu.VMEM(s.shape, t) for s, t in zip(shards, dtypes)],
        name="gather_weights", compiler_params=_params(vmem_mb=40),
    )(*shards)


def _pair_exchange(grads):
    na = len(grads)

    def body(*refs):
        g_refs, o_refs = refs[:na], refs[na:2 * na]
        send_sems, recv_sems = refs[2 * na:]
        x, y, c, _ = _place()
        sibling = (x, y, 1 - c)
        for ai in range(na):
            _, rows, cols = g_refs[ai].shape
            hr = rows // 2
            for k in range(4):
                for r0, nr in _row_chunks(hr, cols * 4, SUBLANES):
                    _remote(g_refs[ai].at[k, pl.ds((1 - c) * hr + r0, nr), :], o_refs[ai].at[k, pl.ds(r0, nr), :],
                            send_sems.at[ai], recv_sems.at[ai], sibling).start()
        for ai in range(na):
            _remote(o_refs[ai], o_refs[ai], send_sems.at[ai], recv_sems.at[ai], sibling).wait()

    return pl.pallas_call(
        body, in_specs=[ANY] * na, out_specs=[ANY] * na,
        out_shape=[jax.ShapeDtypeStruct((4, g.shape[1] // 2, g.shape[2]), F32) for g in grads],
        scratch_shapes=[pltpu.SemaphoreType.DMA((na,)), pltpu.SemaphoreType.DMA((na,))],
        name="pair_exchange",
    )(*grads)


def _row_tile(rows, cols, align):
    best = align
    for cand in range(align, rows + 1, align):
        if rows % cand == 0 and cand * cols <= 256 * 1024:
            best = cand
    return best


def _pair_sum(g, got, place, transit, name):
    _, rows, cols = g.shape
    hr = rows // 2
    tr = _row_tile(hr, cols, _align_of(transit))
    nt = hr // tr

    def body(p_ref, g_ref, r_ref, s_ref, own_ref):
        total = g_ref[...] + r_ref[...]
        s_ref[...] = total.astype(transit)

        @pl.when(pl.program_id(1) == p_ref[1])
        def _():
            own_ref[...] = total

    grid_spec = pltpu.PrefetchScalarGridSpec(
        num_scalar_prefetch=1, grid=(nt, 4),
        in_specs=[pl.BlockSpec((None, tr, cols), lambda i, k, p: (k, p[0] * nt + i, 0)),
                  pl.BlockSpec((None, tr, cols), lambda i, k, p: (k, i, 0))],
        out_specs=[pl.BlockSpec((None, tr, cols), lambda i, k, p: (k, i, 0)),
                   pl.BlockSpec((tr, cols), lambda i, k, p: (i, 0))])
    return pl.pallas_call(
        body, grid_spec=grid_spec,
        out_shape=[jax.ShapeDtypeStruct((4, hr, cols), transit), jax.ShapeDtypeStruct((hr, cols), F32)],
        name=name, compiler_params=_params(("parallel", "arbitrary")),
    )(place, g, got)


def _chip_exchange(sums):
    na = len(sums)

    def body(*refs):
        s_refs, o_refs = refs[:na], refs[na:2 * na]
        send_sems, recv_sems = refs[2 * na:]
        x, y, c, chips = _place()
        for ai in range(na):
            _, rows, cols = s_refs[ai].shape
            row_bytes = cols * jnp.dtype(s_refs[ai].dtype).itemsize
            for r0, nr in _row_chunks(rows, row_bytes, _align_of(s_refs[ai].dtype)):
                for j, chip in enumerate(chips):
                    _remote(s_refs[ai].at[_chip_index(*chip), pl.ds(r0, nr), :], o_refs[ai].at[j, pl.ds(r0, nr), :],
                            send_sems.at[ai, j], recv_sems.at[ai, j], (*chip, c)).start()
        for ai in range(na):
            for j, chip in enumerate(chips):
                _remote(o_refs[ai].at[j], o_refs[ai].at[j], send_sems.at[ai, j], recv_sems.at[ai, j],
                        (*chip, c)).wait()

    return pl.pallas_call(
        body, in_specs=[ANY] * na, out_specs=[ANY] * na,
        out_shape=[jax.ShapeDtypeStruct((3, *s.shape[1:]), s.dtype) for s in sums],
        scratch_shapes=[pltpu.SemaphoreType.DMA((na, 3)), pltpu.SemaphoreType.DMA((na, 3))],
        name="chip_exchange",
    )(*sums)


def _chip_sum(own, landed, name):
    hr, cols = own.shape
    tr = _row_tile(hr, cols, _align_of(landed.dtype))

    def body(o_ref, l_ref, f_ref):
        acc = o_ref[...]
        for j in range(3):
            acc = acc + l_ref[j].astype(F32)
        f_ref[...] = acc

    return pl.pallas_call(
        body, grid=(hr // tr,),
        in_specs=[pl.BlockSpec((tr, cols), lambda i: (i, 0)), pl.BlockSpec((3, tr, cols), lambda i: (0, i, 0))],
        out_specs=pl.BlockSpec((tr, cols), lambda i: (i, 0)),
        out_shape=jax.ShapeDtypeStruct((hr, cols), F32), name=name,
        compiler_params=_params(("parallel",)),
    )(own, landed)


def _final_exchange(halves, small):
    nh = len(halves)

    def body(*refs):
        h_refs, s_ref = refs[:nh], refs[nh]
        o_refs, so_ref = refs[nh + 1:2 * nh + 1], refs[2 * nh + 1]
        send_sems, recv_sems, local_sem, ssend_sems, srecv_sems = refs[2 * nh + 2:]
        x, y, c, _ = _place()
        me = 4 * x + 2 * y + c
        sibling = (x, y, 1 - c)
        for hi in range(nh):
            hr, cols = h_refs[hi].shape
            for r0, nr in _row_chunks(hr, cols * 4, SUBLANES):
                _remote(h_refs[hi].at[pl.ds(r0, nr), :], o_refs[hi].at[pl.ds(r0, nr), :],
                        send_sems.at[hi], recv_sems.at[hi], sibling).start()
        small_cps = [pltpu.make_async_copy(s_ref, so_ref.at[me], local_sem)]
        for r in range(1, 8):
            fx, fy, fc = (r >> 2) & 1, (r >> 1) & 1, r & 1
            peer = (1 - x if fx else x, 1 - y if fy else y, 1 - c if fc else c)
            small_cps.append(_remote(s_ref, so_ref.at[me], ssend_sems.at[r - 1], srecv_sems.at[r - 1], peer))
        for cp in small_cps:
            cp.start()
        for hi in range(nh):
            _remote(h_refs[hi], o_refs[hi], send_sems.at[hi], recv_sems.at[hi], sibling).wait()
        for cp in small_cps:
            cp.wait()

    return pl.pallas_call(
        body, in_specs=[ANY] * (nh + 1), out_specs=[ANY] * (nh + 1),
        out_shape=[jax.ShapeDtypeStruct(h.shape, F32) for h in halves]
        + [jax.ShapeDtypeStruct((8, *small.shape), F32)],
        scratch_shapes=[pltpu.SemaphoreType.DMA((nh,)), pltpu.SemaphoreType.DMA((nh,)),
                        pltpu.SemaphoreType.DMA, pltpu.SemaphoreType.DMA((7,)), pltpu.SemaphoreType.DMA((7,))],
        name="final_exchange",
    )(*halves, small)


def _adamw(w, g, m, v, name):
    shape = w.shape
    n = w.size
    if w.ndim >= 2 and shape[-1] >= LANES:
        two_d = (n // shape[-1], shape[-1])
    elif n % LANES == 0:
        two_d = (n // LANES, LANES)
    else:
        two_d = (1, n)
    r, c = two_d
    tr = r
    for cand in (512, 256, 176, 128, 64):
        if r > cand and r % cand == 0 and cand * c <= 256 * 1024:
            tr = cand
            break
    c1 = 1.0 - ADAM_B1 ** ADAM_STEP
    c2 = 1.0 - ADAM_B2 ** ADAM_STEP

    def body(w_ref, g_ref, m_ref, v_ref, d_ref, nm_ref, nv_ref):
        gv = g_ref[...]
        nm = ADAM_B1 * m_ref[...] + (1.0 - ADAM_B1) * gv
        nv = ADAM_B2 * v_ref[...] + (1.0 - ADAM_B2) * (gv * gv)
        d_ref[...] = -ADAM_LR * ((nm / c1) / (jnp.sqrt(nv / c2) + ADAM_EPS) + ADAM_WD * w_ref[...])
        nm_ref[...] = nm
        nv_ref[...] = nv

    spec = pl.BlockSpec((tr, c), lambda i: (i, 0))
    out = jax.ShapeDtypeStruct((r, c), F32)
    d, nm, nv = pl.pallas_call(
        body, grid=(r // tr,), in_specs=[spec] * 4, out_specs=[spec] * 3, out_shape=[out] * 3, name=name,
        compiler_params=_params(("parallel",)),
    )(w.reshape(two_d), g.reshape(two_d), m.reshape(two_d), v.reshape(two_d))
    return d.reshape(shape), nm.reshape(shape), nv.reshape(shape)


def _adamw_many(ws, gs, ms, vs, name):
    n = len(ws)
    c1 = 1.0 - ADAM_B1 ** ADAM_STEP
    c2 = 1.0 - ADAM_B2 ** ADAM_STEP

    def body(*refs):
        w_refs, g_refs, m_refs, v_refs = (refs[k * n:(k + 1) * n] for k in range(4))
        d_refs, nm_refs, nv_refs = (refs[(4 + k) * n:(5 + k) * n] for k in range(3))
        for i in range(n):
            gv = g_refs[i][...]
            nm = ADAM_B1 * m_refs[i][...] + (1.0 - ADAM_B1) * gv
            nv = ADAM_B2 * v_refs[i][...] + (1.0 - ADAM_B2) * (gv * gv)
            d_refs[i][...] = -ADAM_LR * ((nm / c1) / (jnp.sqrt(nv / c2) + ADAM_EPS) + ADAM_WD * w_refs[i][...])
            nm_refs[i][...] = nm
            nv_refs[i][...] = nv

    vmem = pl.BlockSpec(memory_space=pltpu.VMEM)
    shapes = [jax.ShapeDtypeStruct(t.shape, F32) for t in ws]
    outs = pl.pallas_call(body, in_specs=[vmem] * (4 * n), out_specs=[vmem] * (3 * n), out_shape=shapes * 3,
                          name=name, compiler_params=_params(vmem_mb=56))(*ws, *gs, *ms, *vs)
    return outs[:n], outs[n:2 * n], outs[2 * n:]


BIG = ("w_in", "w_glu", "w_out", "w_up", "w_down")
WEIGHTS = ("norm_mix_g", "w_in", "a_re", "a_im", "log_step", "b_re", "b_im", "c_re", "c_im", "d_skip", "w_glu",
           "sink", "norm_attn_g", "norm_ssm_g", "w_out", "norm_ffn_g", "w_up", "conv_w", "conv_b", "w_down",
           "norm_final_g")
SMALL = ("norm_mix_g", "a_re", "a_im", "log_step", "b_re", "b_im", "c_re", "c_im", "d_skip", "sink",
         "norm_attn_g", "norm_ssm_g", "norm_ffn_g", "conv_w", "conv_b", "norm_final_g")
SMALL_ROWS = 40
N_DEV = 8


def _full_matrices(gathered):
    w_in, w_glu, w_out, w_up, w_down = gathered
    cols = lambda t: t.transpose(1, 0, 2).reshape(t.shape[1], 4 * t.shape[2])
    rows = lambda t: t.reshape(4 * t.shape[1], t.shape[2])
    return {"w_in": cols(w_in), "w_glu": rows(w_glu), "w_out": rows(w_out), "w_up": w_up, "w_down": rows(w_down)}


def _by_owner(name, g):
    if name == "w_up":
        return g
    if name == "w_in":
        return g.reshape(g.shape[0], 4, g.shape[1] // 4).transpose(1, 0, 2)
    return g.reshape(4, g.shape[0] // 4, g.shape[1])


def kernel(x, norm_mix_g, w_in, a_re, a_im, log_step, b_re, b_im, c_re, c_im, d_skip, w_glu, sink, norm_attn_g, norm_ssm_g, w_out, norm_ffn_g, w_up, conv_w, conv_b, w_down, norm_final_g, loss_target, m_norm_mix_g, m_w_in, m_a_re, m_a_im, m_log_step, m_b_re, m_b_im, m_c_re, m_c_im, m_d_skip, m_w_glu, m_sink, m_norm_attn_g, m_norm_ssm_g, m_w_out, m_norm_ffn_g, m_w_up, m_conv_w, m_conv_b, m_w_down, m_norm_final_g, v_norm_mix_g, v_w_in, v_a_re, v_a_im, v_log_step, v_b_re, v_b_im, v_c_re, v_c_im, v_d_skip, v_w_glu, v_sink, v_norm_attn_g, v_norm_ssm_g, v_w_out, v_norm_ffn_g, v_w_up, v_conv_w, v_conv_b, v_w_down, v_norm_final_g):
    given = dict(locals())
    w = {n: given[n] for n in WEIGHTS}
    m = {n: given["m_" + n] for n in WEIGHTS}
    v = {n: given["v_" + n] for n in WEIGHTS}
    xy = 2 * lax.axis_index("x") + lax.axis_index("y")

    conv_rows = jnp.pad(w["conv_w"][0], ((0, 2 * SUBLANES - 3), (0, 0)))
    *gathered, conv_all = _gather_weights([w[n][0] for n in BIG] + [conv_rows], [BF16] * len(BIG) + [F32])
    wb = _full_matrices(gathered)

    sp = {n: w[n][0] for n in ("a_re", "a_im", "log_step", "b_re", "b_im", "c_re", "c_im", "d_skip",
                               "norm_mix_g", "norm_attn_g", "norm_ssm_g", "norm_ffn_g", "sink", "conv_b")}
    for n in ("norm_mix_g", "norm_attn_g", "norm_ssm_g", "norm_ffn_g", "sink", "conv_b"):
        sp[n] = sp[n].reshape(1, -1)
    sp["conv_w"] = conv_all[:, :3].transpose(1, 0, 2).reshape(3, 2 * D_FF)
    sp["norm_final_g"] = w["norm_final_g"]
    loss, grad_x, g = _local_step(x[0], loss_target[0], wb, sp)

    flat = jnp.concatenate([g[n].reshape(-1) for n in SMALL] + [loss.reshape(-1)])
    pad = N_DEV * SMALL_ROWS * D_MODEL - flat.shape[0]
    small = jnp.concatenate([flat, jnp.zeros((pad,), F32)]).reshape(4, 2 * SMALL_ROWS, D_MODEL)
    by_owner = [_by_owner(n, g[n]) for n in BIG] + [small]
    core = lax.axis_index("c")
    place = jnp.stack([core, xy]).astype(jnp.int32)
    got = _pair_exchange(by_owner)
    transit = [BF16] * len(BIG) + [F32]
    chip_sums, own_sums = zip(*[_pair_sum(a, b, place, t, "pair_sum_%d" % i)
                                for i, (a, b, t) in enumerate(zip(by_owner, got, transit))])
    landed = _chip_exchange(list(chip_sums))
    halves = [_chip_sum(o, t, "chip_sum_%d" % i) for i, (o, t) in enumerate(zip(own_sums, landed))]
    *others, small_all = _final_exchange(halves[:-1], halves[-1])
    grads = {n: jnp.concatenate([jnp.where(core == 0, h, o), jnp.where(core == 0, o, h)], axis=0)
             for n, h, o in zip(BIG, halves, others)}
    flat = small_all.reshape(-1)
    off = 0
    for n in SMALL:
        shape = (3, 4 * w[n].shape[-1]) if n == "conv_w" else w[n].shape[1:] if n != "norm_final_g" else w[n].shape
        size = math.prod(shape)
        grads[n] = flat[off:off + size].reshape(shape)
        off += size
    loss = flat[off]
    cw = w["conv_w"].shape[-1]
    grads["conv_w"] = lax.dynamic_slice_in_dim(grads["conv_w"], xy * cw, cw, axis=1)
    grads = {n: grads[n].reshape(w[n].shape) for n in WEIGHTS}

    delta, new_m, new_v = {}, {}, {}
    for n in BIG:
        delta[n], new_m[n], new_v[n] = _adamw(w[n], grads[n], m[n], v[n], "adamw_" + n)
    for group, name in ((("b_re", "b_im"), "adamw_b"), (tuple(n for n in SMALL if n not in ("b_re", "b_im")), "adamw_small")):
        row = lambda t: t.reshape(1, -1) if t.ndim == 1 else t
        d_, m_, v_ = _adamw_many(*[[row(t[n]) for n in group] for t in (w, grads, m, v)], name)
        for n, dn, mn, vn in zip(group, d_, m_, v_):
            delta[n], new_m[n], new_v[n] = (t.reshape(w[n].shape) for t in (dn, mn, vn))
    return (loss, grad_x[None], *[grads[n] for n in WEIGHTS], *[delta[n] for n in WEIGHTS],
            *[new_m[n] for n in WEIGHTS], *[new_v[n] for n in WEIGHTS])
```

```python
import functools
import math

import jax
import jax.numpy as jnp
from jax import lax
from jax.experimental import pallas as pl
from jax.experimental.pallas import tpu as pltpu

F32 = jnp.float32
BF16 = jnp.bfloat16

D_MODEL = 1024
N_Q_HEADS = 8
N_KV_HEADS = 2
HEAD_DIM = 64
ATTN_WIDTH = 512
KV_WIDTH = 128
QKV_WIDTH = ATTN_WIDTH + 2 * KV_WIDTH
WINDOW = 128
BLOCK = 128
ROPE_DIM = 16
ROPE_THETA = 500000.0
SSM_WIDTH = 512
SSM_GROUP = 16
N_SSM_GROUPS = 32
SSM_STATE = 64
IN_WIDTH = 1280
D_FF = 2816
EPS = 1e-6
ADAM_LR = 0.001
ADAM_B1 = 0.9
ADAM_B2 = 0.999
ADAM_EPS = 1e-08
ADAM_WD = 0.01
ADAM_STEP = 10

VMEM_BYTES_V7X = 64 * 1024 * 1024
SUBLANES = 8
LANES = 128
SSM_CB = 4
SSM_CH = 128
SSM_ST = 512
N_SEG = SUBLANES

NN = (((1,), (0,)), ((), ()))
NT = (((1,), (1,)), ((), ()))
TN = (((0,), (0,)), ((), ()))


def _params(sem=None, vmem_mb=48):
    return pltpu.CompilerParams(dimension_semantics=sem, vmem_limit_bytes=vmem_mb * 1024 * 1024)


def _dg(a, b, dims):
    return lax.dot_general(a, b, dims, preferred_element_type=F32)


def _sigmoid(x):
    return 1.0 / (1.0 + jnp.exp(-x))


_SQRT_HALF = 0.7071067811865476
_INV_SQRT_2PI = 0.3989422804014327


def _gelu(x):
    return 0.5 * x * (1.0 + lax.erf(x * _SQRT_HALF))


def _gelu_grad(x):
    return 0.5 * (1.0 + lax.erf(x * _SQRT_HALF)) + x * (_INV_SQRT_2PI * jnp.exp(-0.5 * x * x))


def _mm_nn(a, b, tm, tn, out_dtype, name, res=None):
    m, k = a.shape
    n = b.shape[1]

    def body(*refs):
        if res is None:
            a_ref, b_ref, o_ref = refs
            o_ref[...] = _dg(a_ref[...], b_ref[...], NN).astype(out_dtype)
        else:
            a_ref, b_ref, r_ref, o_ref = refs
            o_ref[...] = (r_ref[...] + _dg(a_ref[...], b_ref[...], NN)).astype(out_dtype)

    in_specs = [pl.BlockSpec((tm, k), lambda i, j: (i, 0)), pl.BlockSpec((k, tn), lambda i, j: (0, j))]
    args = [a, b]
    if res is not None:
        in_specs.append(pl.BlockSpec((tm, tn), lambda i, j: (i, j)))
        args.append(res)
    return pl.pallas_call(
        body, grid=(m // tm, n // tn), in_specs=in_specs,
        out_specs=pl.BlockSpec((tm, tn), lambda i, j: (i, j)),
        out_shape=jax.ShapeDtypeStruct((m, n), out_dtype), name=name,
        compiler_params=_params(("parallel", "parallel")),
    )(*args)


def _mm_nt(a, b, tm, tn, out_dtype, name):
    m, k = a.shape
    n = b.shape[0]

    def body(a_ref, b_ref, o_ref):
        o_ref[...] = _dg(a_ref[...], b_ref[...], NT).astype(out_dtype)

    return pl.pallas_call(
        body, grid=(m // tm, n // tn),
        in_specs=[pl.BlockSpec((tm, k), lambda i, j: (i, 0)), pl.BlockSpec((tn, k), lambda i, j: (j, 0))],
        out_specs=pl.BlockSpec((tm, tn), lambda i, j: (i, j)),
        out_shape=jax.ShapeDtypeStruct((m, n), out_dtype), name=name,
        compiler_params=_params(("parallel", "parallel")),
    )(a, b)


def _mm_tn(a, b, tm, tn, name):
    k, m = a.shape
    n = b.shape[1]

    def body(a_ref, b_ref, o_ref):
        o_ref[...] = _dg(a_ref[...], b_ref[...], TN)

    return pl.pallas_call(
        body, grid=(m // tm, n // tn),
        in_specs=[pl.BlockSpec((k, tm), lambda i, j: (0, i)), pl.BlockSpec((k, tn), lambda i, j: (0, j))],
        out_specs=pl.BlockSpec((tm, tn), lambda i, j: (i, j)),
        out_shape=jax.ShapeDtypeStruct((m, n), F32), name=name,
        compiler_params=_params(("parallel", "parallel")),
    )(a, b)


def _mm_split(a, b, tm, split, name):
    m, k = a.shape
    n = b.shape[1]

    def body(a_ref, b_ref, lo_ref, hi_ref):
        out = _dg(a_ref[...], b_ref[...], NN)
        lo_ref[...] = out[:, :split]
        hi_ref[...] = out[:, split:]

    return pl.pallas_call(
        body, grid=(m // tm,),
        in_specs=[pl.BlockSpec((tm, k), lambda i: (i, 0)), pl.BlockSpec((k, n), lambda i: (0, 0))],
        out_specs=[pl.BlockSpec((tm, split), lambda i: (i, 0)), pl.BlockSpec((tm, n - split), lambda i: (i, 0))],
        out_shape=[jax.ShapeDtypeStruct((m, split), F32), jax.ShapeDtypeStruct((m, n - split), F32)], name=name,
        compiler_params=_params(("parallel",)),
    )(a, b)


def _mm_nn_cols(a, b4, tm, name):
    m, k = a.shape
    s, _, n = b4.shape

    def body(a_ref, b_ref, o_ref):
        o_ref[...] = _dg(a_ref[...], b_ref[...], NN)

    return pl.pallas_call(
        body, grid=(m // tm, s),
        in_specs=[pl.BlockSpec((tm, k), lambda i, j: (i, 0)), pl.BlockSpec((None, k, n), lambda i, j: (j, 0, 0))],
        out_specs=pl.BlockSpec((tm, n), lambda i, j: (i, j)),
        out_shape=jax.ShapeDtypeStruct((m, s * n), F32), name=name,
        compiler_params=_params(("parallel", "parallel")),
    )(a, b4)


def _mm_nt_cols(a2, b4, tm, name):
    h, m, wide = a2.shape
    s, k, n = b4.shape
    per = s // h

    def body(a_ref, b_ref, o_ref):
        acc = None
        for j in range(s):
            part = _dg(a_ref[j // per, :, (j % per) * n:(j % per + 1) * n], b_ref[j], NT)
            acc = part if acc is None else acc + part
        o_ref[...] = acc

    return pl.pallas_call(
        body, grid=(m // tm,),
        in_specs=[pl.BlockSpec((h, tm, wide), lambda i: (0, i, 0)), pl.BlockSpec((s, k, n), lambda i: (0, 0, 0))],
        out_specs=pl.BlockSpec((tm, k), lambda i: (i, 0)),
        out_shape=jax.ShapeDtypeStruct((m, k), F32), name=name,
        compiler_params=_params(("parallel",)),
    )(a2, b4)


def _mm_tn_cols(a, b2, s, tm, name):
    k, m = a.shape
    h, _, wide = b2.shape
    per = s // h
    n = wide // per

    def body(a_ref, b_ref, o_ref):
        o_ref[...] = _dg(a_ref[...], b_ref[...], TN)

    return pl.pallas_call(
        body, grid=(s, m // tm),
        in_specs=[pl.BlockSpec((k, tm), lambda j, i: (0, i)),
                  pl.BlockSpec((None, k, n), lambda j, i: (j // per, 0, j % per))],
        out_specs=pl.BlockSpec((None, tm, n), lambda j, i: (j, i, 0)),
        out_shape=jax.ShapeDtypeStruct((s, m, n), F32), name=name,
        compiler_params=_params(("parallel", "parallel")),
    )(a, b2)


TM_EW = 256


def _rms_fwd(x, g, name):
    l, d = x.shape

    def body(x_ref, g_ref, h_ref):
        xv = x_ref[...]
        r = lax.rsqrt(jnp.mean(xv * xv, axis=-1, keepdims=True) + EPS)
        h_ref[...] = (xv * r * g_ref[...]).astype(BF16)

    return pl.pallas_call(
        body, grid=(l // TM_EW,),
        in_specs=[pl.BlockSpec((TM_EW, d), lambda i: (i, 0)), pl.BlockSpec((1, d), lambda i: (0, 0))],
        out_specs=pl.BlockSpec((TM_EW, d), lambda i: (i, 0)),
        out_shape=jax.ShapeDtypeStruct((l, d), BF16), name=name,
        compiler_params=_params(("parallel",)),
    )(x, g)


def _rms_bwd_vals(xv, gv, dy):
    r = lax.rsqrt(jnp.mean(xv * xv, axis=-1, keepdims=True) + EPS)
    xh = xv * r
    dxh = dy * gv
    dx = r * (dxh - xh * jnp.mean(dxh * xh, axis=-1, keepdims=True))
    return dx, dy * xh


def _rms_bwd(x, g, dy, res, name):
    l, d = x.shape

    def body(x_ref, g_ref, dy_ref, res_ref, dx_ref, dxb_ref, dg_ref):
        dx, dgr = _rms_bwd_vals(x_ref[...], g_ref[...], dy_ref[...])
        dx = dx + res_ref[...]
        dx_ref[...] = dx
        dxb_ref[...] = dx.astype(BF16)

        @pl.when(pl.program_id(0) == 0)
        def _():
            dg_ref[...] = jnp.zeros_like(dg_ref)

        dg_ref[...] += jnp.sum(dgr, axis=0, keepdims=True)

    row = pl.BlockSpec((TM_EW, d), lambda i: (i, 0))
    vec = pl.BlockSpec((1, d), lambda i: (0, 0))
    return pl.pallas_call(
        body, grid=(l // TM_EW,), in_specs=[row, vec, row, row], out_specs=[row, row, vec],
        out_shape=[jax.ShapeDtypeStruct((l, d), F32), jax.ShapeDtypeStruct((l, d), BF16),
                   jax.ShapeDtypeStruct((1, d), F32)],
        name=name, compiler_params=_params(("arbitrary",)),
    )(x, g, dy, res)


def _final_loss(x2, g, target):
    l, d = x2.shape

    def body(x_ref, g_ref, t_ref, loss_ref, dx_ref, dxb_ref, dg_ref):
        xv = x_ref[...]
        gv = g_ref[...]
        r = lax.rsqrt(jnp.mean(xv * xv, axis=-1, keepdims=True) + EPS)
        xh = xv * r
        e = xh * gv - t_ref[...]
        part = jnp.sum(jnp.sum(e * e, axis=1, keepdims=True), axis=0, keepdims=True) * (0.5 / d)
        dy = e * (1.0 / d)
        dxh = dy * gv
        dx = r * (dxh - xh * jnp.mean(dxh * xh, axis=-1, keepdims=True))
        dx_ref[...] = dx
        dxb_ref[...] = dx.astype(BF16)

        @pl.when(pl.program_id(0) == 0)
        def _():
            dg_ref[...] = jnp.zeros_like(dg_ref)
            loss_ref[...] = jnp.zeros_like(loss_ref)

        dg_ref[...] += jnp.sum(dy * xh, axis=0, keepdims=True)
        loss_ref[...] += part

    row = pl.BlockSpec((TM_EW, d), lambda i: (i, 0))
    vec = pl.BlockSpec((1, d), lambda i: (0, 0))
    one = pl.BlockSpec((1, 1), lambda i: (0, 0))
    return pl.pallas_call(
        body, grid=(l // TM_EW,), in_specs=[row, vec, row], out_specs=[one, row, row, vec],
        out_shape=[jax.ShapeDtypeStruct((1, 1), F32), jax.ShapeDtypeStruct((l, d), F32),
                   jax.ShapeDtypeStruct((l, d), BF16), jax.ShapeDtypeStruct((1, d), F32)],
        name="final_loss", compiler_params=_params(("arbitrary",)),
    )(x2, g, target)


def _mix_fwd(attn, ys, g_attn, g_ssm):
    l, w = attn.shape

    def body(a_ref, y_ref, ga_ref, gs_ref, o_ref):
        for src, gr, off in ((a_ref, ga_ref, 0), (y_ref, gs_ref, w)):
            xv = src[...]
            r = lax.rsqrt(jnp.mean(xv * xv, axis=-1, keepdims=True) + EPS)
            o_ref[:, off:off + w] = (xv * r * gr[...]).astype(BF16)

    row = pl.BlockSpec((TM_EW, w), lambda i: (i, 0))
    vec = pl.BlockSpec((1, w), lambda i: (0, 0))
    return pl.pallas_call(
        body, grid=(l // TM_EW,), in_specs=[row, row, vec, vec],
        out_specs=pl.BlockSpec((TM_EW, 2 * w), lambda i: (i, 0)),
        out_shape=jax.ShapeDtypeStruct((l, 2 * w), BF16), name="mix_fwd",
        compiler_params=_params(("parallel",)),
    )(attn, ys, g_attn, g_ssm)


def _mix_bwd(attn, ys, g_attn, g_ssm, dmixed):
    l, w = attn.shape

    def body(a_ref, y_ref, ga_ref, gs_ref, dm_ref, da_ref, dy_ref, dga_ref, dgs_ref):
        @pl.when(pl.program_id(0) == 0)
        def _():
            dga_ref[...] = jnp.zeros_like(dga_ref)
            dgs_ref[...] = jnp.zeros_like(dgs_ref)

        for src, gr, off, dst, dgr in ((a_ref, ga_ref, 0, da_ref, dga_ref), (y_ref, gs_ref, w, dy_ref, dgs_ref)):
            dx, dg_rows = _rms_bwd_vals(src[...], gr[...], dm_ref[:, off:off + w])
            dst[...] = dx
            dgr[...] += jnp.sum(dg_rows, axis=0, keepdims=True)

    row = pl.BlockSpec((TM_EW, w), lambda i: (i, 0))
    vec = pl.BlockSpec((1, w), lambda i: (0, 0))
    return pl.pallas_call(
        body, grid=(l // TM_EW,),
        in_specs=[row, row, vec, vec, pl.BlockSpec((TM_EW, 2 * w), lambda i: (i, 0))],
        out_specs=[row, row, vec, vec],
        out_shape=[jax.ShapeDtypeStruct((l, w), F32), jax.ShapeDtypeStruct((l, w), F32),
                   jax.ShapeDtypeStruct((1, w), F32), jax.ShapeDtypeStruct((1, w), F32)],
        name="mix_bwd", compiler_params=_params(("arbitrary",)),
    )(attn, ys, g_attn, g_ssm, dmixed)


def _rope_tables(l):
    half = ROPE_DIM // 2
    inv_freq = jnp.power(ROPE_THETA, -jnp.arange(half, dtype=F32) / half)
    ang = jnp.arange(l, dtype=F32)[:, None] * inv_freq[None, :]
    cos, sin = jnp.cos(ang), jnp.sin(ang)
    ones = jnp.ones((l, HEAD_DIM - ROPE_DIM), F32)
    zeros = jnp.zeros((l, HEAD_DIM - ROPE_DIM), F32)
    zh = jnp.zeros((l, half), F32)
    c = jnp.concatenate([cos, cos, ones], axis=1)
    s_lo = jnp.concatenate([-sin, zh, zeros], axis=1)
    s_hi = jnp.concatenate([zh, sin, zeros], axis=1)
    return tuple(jnp.tile(t, (1, LANES // HEAD_DIM)) for t in (c, s_lo, s_hi))


def _rope_fwd(proj, tabs):
    l = proj.shape[0]
    nq = ATTN_WIDTH // LANES

    def body(p_ref, c_ref, lo_ref, hi_ref, o_ref):
        c, lo, hi = c_ref[...], lo_ref[...], hi_ref[...]
        for blk in range(nq + 1):
            t = p_ref[:, blk * LANES:(blk + 1) * LANES]
            rot = t * c + pltpu.roll(t, LANES - 8, 1) * lo + pltpu.roll(t, 8, 1) * hi
            o_ref[:, blk * LANES:(blk + 1) * LANES] = rot.astype(BF16)
        o_ref[:, (nq + 1) * LANES:] = p_ref[:, (nq + 1) * LANES:].astype(BF16)

    tab = pl.BlockSpec((TM_EW, LANES), lambda i: (i, 0))
    return pl.pallas_call(
        body, grid=(l // TM_EW,),
        in_specs=[pl.BlockSpec((TM_EW, QKV_WIDTH), lambda i: (i, 0)), tab, tab, tab],
        out_specs=pl.BlockSpec((TM_EW, QKV_WIDTH), lambda i: (i, 0)),
        out_shape=jax.ShapeDtypeStruct((l, QKV_WIDTH), BF16), name="rope_fwd",
        compiler_params=_params(("parallel",)),
    )(proj, *tabs)


def _rope_bwd(dqkv, du_ssm, dpre, d_skip, tabs):
    l = dqkv.shape[0]
    nq = ATTN_WIDTH // LANES

    def body(d_ref, du_ref, dpre_ref, ds_ref, c_ref, lo_ref, hi_ref, o_ref):
        c, lo, hi = c_ref[...], lo_ref[...], hi_ref[...]
        for blk in range(nq + 1):
            t = d_ref[:, blk * LANES:(blk + 1) * LANES]
            g = t * c + pltpu.roll(t * lo, 8, 1) + pltpu.roll(t * hi, LANES - 8, 1)
            o_ref[:, blk * LANES:(blk + 1) * LANES] = g.astype(BF16)
        o_ref[:, (nq + 1) * LANES:QKV_WIDTH] = d_ref[:, (nq + 1) * LANES:].astype(BF16)
        o_ref[:, QKV_WIDTH:] = (du_ref[...] + dpre_ref[...] * ds_ref[...]).astype(BF16)

    tab = pl.BlockSpec((TM_EW, LANES), lambda i: (i, 0))
    wide = pl.BlockSpec((TM_EW, SSM_WIDTH), lambda i: (i, 0))
    return pl.pallas_call(
        body, grid=(l // TM_EW,),
        in_specs=[pl.BlockSpec((TM_EW, QKV_WIDTH), lambda i: (i, 0)), wide, wide,
                  pl.BlockSpec((1, SSM_WIDTH), lambda i: (0, 0)), tab, tab, tab],
        out_specs=pl.BlockSpec((TM_EW, IN_WIDTH), lambda i: (i, 0)),
        out_shape=jax.ShapeDtypeStruct((l, IN_WIDTH), BF16), name="rope_bwd",
        compiler_params=_params(("parallel",)),
    )(dqkv, du_ssm, dpre, d_skip, *tabs)


_Q_COLS = ATTN_WIDTH // LANES
_SCALE = HEAD_DIM ** -0.5
_NEG = -1e30


def _window_specs(nb, width, col):
    return [
        pl.BlockSpec((BLOCK, width), lambda n: (jnp.maximum(n - 1, 0), col)),
        pl.BlockSpec((BLOCK, width), lambda n: (n, col)),
        pl.BlockSpec((BLOCK, width), lambda n: (jnp.minimum(n + 1, nb - 1), col)),
    ]


def _stacked_sink(sink_ref, heads):
    rid = lax.broadcasted_iota(jnp.int32, (len(heads) * BLOCK, 1), 0)
    sk = jnp.full(rid.shape, sink_ref[0, heads[-1]], F32)
    for g in range(len(heads) - 2, -1, -1):
        sk = jnp.where(rid < (g + 1) * BLOCK, sink_ref[0, heads[g]], sk)
    return sk


def _attn_fwd(qkv, sink):
    l = qkv.shape[0]
    nb = l // BLOCK
    grp = N_Q_HEADS // N_KV_HEADS

    def body(sink_ref, q_ref, k0, k1, k2, v0, v1, v2, o_ref, lse_ref):
        n = pl.program_id(0)
        q = q_ref[...]
        kw = jnp.concatenate([k0[...], k1[...], k2[...]], axis=0)
        vw = jnp.concatenate([v0[...], v1[...], v2[...]], axis=0)
        row = lax.broadcasted_iota(jnp.int32, (grp * BLOCK, 3 * BLOCK), 0)
        col = lax.broadcasted_iota(jnp.int32, (grp * BLOCK, 3 * BLOCK), 1)
        valid = jnp.abs(col - BLOCK - (row & (BLOCK - 1))) <= WINDOW
        valid &= jnp.logical_not((n == 0) & (col < BLOCK))
        valid &= jnp.logical_not((n == nb - 1) & (col >= 2 * BLOCK))
        for hk in range(N_KV_HEADS):
            heads = range(hk * grp, (hk + 1) * grp)
            qs = jnp.concatenate([q[:, h * HEAD_DIM:(h + 1) * HEAD_DIM] for h in heads], axis=0)
            kh = kw[:, hk * HEAD_DIM:(hk + 1) * HEAD_DIM]
            vh = vw[:, hk * HEAD_DIM:(hk + 1) * HEAD_DIM]
            s = jnp.where(valid, _dg(qs, kh, NT) * _SCALE, _NEG)
            sk = _stacked_sink(sink_ref, heads)
            m = jnp.maximum(jnp.max(s, axis=1, keepdims=True), sk)
            p = jnp.exp(s - m)
            denom = jnp.sum(p, axis=1, keepdims=True) + jnp.exp(sk - m)
            o = _dg((p / denom).astype(BF16), vh, NN)
            lse = m + jnp.log(denom)
            for g, h in enumerate(heads):
                o_ref[:, h * HEAD_DIM:(h + 1) * HEAD_DIM] = o[g * BLOCK:(g + 1) * BLOCK]
                lse_ref[:, h:h + 1] = lse[g * BLOCK:(g + 1) * BLOCK]

    return pl.pallas_call(
        body, grid=(nb,),
        in_specs=[pl.BlockSpec(memory_space=pltpu.SMEM),
                  pl.BlockSpec((BLOCK, ATTN_WIDTH), lambda n: (n, 0))]
        + _window_specs(nb, KV_WIDTH, _Q_COLS) + _window_specs(nb, KV_WIDTH, _Q_COLS + 1),
        out_specs=[pl.BlockSpec((BLOCK, ATTN_WIDTH), lambda n: (n, 0)),
                   pl.BlockSpec((BLOCK, N_Q_HEADS), lambda n: (n, 0))],
        out_shape=[jax.ShapeDtypeStruct((l, ATTN_WIDTH), F32), jax.ShapeDtypeStruct((l, N_Q_HEADS), F32)],
        name="attn_fwd", compiler_params=_params(("parallel",)),
    )(sink, qkv, qkv, qkv, qkv, qkv, qkv, qkv)


def _attn_bwd(qkv, attn, dattn, lse, sink):
    l = qkv.shape[0]
    nb = l // BLOCK
    grp = N_Q_HEADS // N_KV_HEADS

    def body(sink_ref, q0, q1, q2, k0, k1, k2, v0, v1, v2, o0, o1, o2, d0, d1, d2,
             l0, l1, l2, dqkv_ref, dsink_ref):
        n = pl.program_id(0)
        first, last = n == 0, n == nb - 1

        @pl.when(first)
        def _():
            dsink_ref[...] = jnp.zeros_like(dsink_ref)

        cat = lambda a, b, c: jnp.concatenate([a[...], b[...], c[...]], axis=0)
        qw, kw, vw = cat(q0, q1, q2), cat(k0, k1, k2), cat(v0, v1, v2)
        dow = cat(d0, d1, d2)
        prodw = cat(o0, o1, o2) * dow
        lsew = cat(l0, l1, l2)
        dob = dow.astype(BF16)
        win = 3 * BLOCK
        mid = slice(BLOCK, 2 * BLOCK)

        row = lax.broadcasted_iota(jnp.int32, (grp * BLOCK, win), 0)
        col = lax.broadcasted_iota(jnp.int32, (grp * BLOCK, win), 1)
        valid_q = jnp.abs(col - BLOCK - (row & (BLOCK - 1))) <= WINDOW
        valid_q &= jnp.logical_not(first & (col < BLOCK))
        valid_q &= jnp.logical_not(last & (col >= 2 * BLOCK))
        rowk = lax.broadcasted_iota(jnp.int32, (grp * win, BLOCK), 0)
        colk = lax.broadcasted_iota(jnp.int32, (grp * win, BLOCK), 1)
        for g in range(1, grp):
            rowk = jnp.where(rowk >= win, rowk - win, rowk)
        valid_k = jnp.abs(colk + BLOCK - rowk) <= WINDOW
        valid_k &= jnp.logical_not(first & (rowk < BLOCK))
        valid_k &= jnp.logical_not(last & (rowk >= 2 * BLOCK))

        dsink_parts = []
        for hk in range(N_KV_HEADS):
            heads = range(hk * grp, (hk + 1) * grp)
            ksl = slice(hk * HEAD_DIM, (hk + 1) * HEAD_DIM)
            hsl = [slice(h * HEAD_DIM, (h + 1) * HEAD_DIM) for h in heads]
            stack = lambda parts: jnp.concatenate(parts, axis=0)
            qws = stack([qw[:, s_] for s_ in hsl])
            dows = stack([dob[:, s_] for s_ in hsl])
            deltaws = stack([jnp.sum(prodw[:, s_], axis=1, keepdims=True) for s_ in hsl])
            lsews = stack([lsew[:, h:h + 1] for h in heads])
            of_block = lambda t: stack([t[g * win + BLOCK:g * win + 2 * BLOCK] for g in range(grp)])
            qs, dos, deltas, lses = of_block(qws), of_block(dows), of_block(deltaws), of_block(lsews)
            kh, vh = kw[:, ksl], vw[:, ksl]
            s = jnp.where(valid_q, _dg(qs, kh, NT) * _SCALE, _NEG)
            p = jnp.exp(s - lses)
            dp = _dg(dos, vh, NT)
            ds = (p * (dp - deltas) * _SCALE).astype(BF16)
            dq = _dg(ds, kh, NN)
            sink_rows = jnp.exp(_stacked_sink(sink_ref, heads) - lses) * deltas
            for g, h in enumerate(heads):
                dqkv_ref[:, hsl[g]] = dq[g * BLOCK:(g + 1) * BLOCK]
                dsink_parts.append(jnp.sum(sink_rows[g * BLOCK:(g + 1) * BLOCK], axis=0, keepdims=True))
            s2 = jnp.where(valid_k, _dg(qws, kh[mid], NT) * _SCALE, _NEG)
            p2 = jnp.exp(s2 - lsews)
            dv = _dg(p2.astype(BF16), dows, TN)
            dp2 = _dg(dows, vh[mid], NT)
            ds2 = (p2 * (dp2 - deltaws) * _SCALE).astype(BF16)
            dk = _dg(ds2, qws, TN)
            dqkv_ref[:, ATTN_WIDTH + hk * HEAD_DIM:ATTN_WIDTH + (hk + 1) * HEAD_DIM] = dk
            dqkv_ref[:, ATTN_WIDTH + KV_WIDTH + hk * HEAD_DIM:ATTN_WIDTH + KV_WIDTH + (hk + 1) * HEAD_DIM] = dv
        dsink_ref[...] -= jnp.concatenate(dsink_parts, axis=1)

    return pl.pallas_call(
        body, grid=(nb,),
        in_specs=[pl.BlockSpec(memory_space=pltpu.SMEM)]
        + _window_specs(nb, ATTN_WIDTH, 0)
        + _window_specs(nb, KV_WIDTH, _Q_COLS) + _window_specs(nb, KV_WIDTH, _Q_COLS + 1)
        + _window_specs(nb, ATTN_WIDTH, 0) + _window_specs(nb, ATTN_WIDTH, 0)
        + _window_specs(nb, N_Q_HEADS, 0),
        out_specs=[pl.BlockSpec((BLOCK, QKV_WIDTH), lambda n: (n, 0)),
                   pl.BlockSpec((1, N_Q_HEADS), lambda n: (0, 0))],
        out_shape=[jax.ShapeDtypeStruct((l, QKV_WIDTH), F32), jax.ShapeDtypeStruct((1, N_Q_HEADS), F32)],
        name="attn_bwd", compiler_params=_params(("arbitrary",)),
    )(sink, qkv, qkv, qkv, qkv, qkv, qkv, qkv, qkv, qkv, attn, attn, attn,
      dattn, dattn, dattn, lse, lse, lse)


def _ssm_disc(a_re, a_im, log_step, b_re, b_im):
    step = jnp.exp(log_step)[..., None]
    mag = jnp.exp(a_re * step)
    lb_re, lb_im = mag * jnp.cos(a_im * step), mag * jnp.sin(a_im * step)
    nr, ni = lb_re - 1.0, lb_im
    den = a_re * a_re + a_im * a_im
    f_re = ((nr * a_re + ni * a_im) / den)[..., None]
    f_im = ((ni * a_re - nr * a_im) / den)[..., None]
    return lb_re, lb_im, f_re * b_re - f_im * b_im, f_re * b_im + f_im * b_re


def _ssm_pack(lb_re, lb_im, bb_re, bb_im, c_re, c_im):
    eye = jnp.eye(SSM_CH // SSM_GROUP, dtype=F32)
    ng = SSM_CH // SSM_GROUP

    def diag_b(bb):
        t = bb.reshape(2, SSM_CB, ng, SSM_STATE, SSM_GROUP)
        return jnp.einsum('dkgpc,gh->dkgchp', t, eye).reshape(2, SSM_CB, SSM_CH, SSM_ST)

    def diag_c(cc):
        t = cc.reshape(2, SSM_CB, ng, SSM_GROUP, SSM_STATE)
        return jnp.einsum('dkgcp,gh->dkhpgc', t, eye).reshape(2, SSM_CB, SSM_ST, SSM_CH)

    bcat = jnp.concatenate([diag_b(bb_re), diag_b(bb_im)], axis=-1)
    ccat = jnp.concatenate([diag_c(c_re), -diag_c(c_im)], axis=-2)
    lam_re = lb_re.reshape(2, SSM_CB, 1, SSM_ST)
    lam_im = lb_im.reshape(2, SSM_CB, 1, SSM_ST)
    return bcat, ccat, lam_re, lam_im


def _ssm_unpack(dbcat, dccat, dlam_re, dlam_im):
    ng = SSM_CH // SSM_GROUP
    eye = jnp.eye(ng, dtype=F32)

    def undiag_b(t):
        t = t.reshape(2, SSM_CB, ng, SSM_GROUP, ng, SSM_STATE)
        return jnp.einsum('dkgchp,gh->dkgpc', t, eye).reshape(2, N_SSM_GROUPS, SSM_STATE, SSM_GROUP)

    def undiag_c(t):
        t = t.reshape(2, SSM_CB, ng, SSM_STATE, ng, SSM_GROUP)
        return jnp.einsum('dkhpgc,gh->dkgcp', t, eye).reshape(2, N_SSM_GROUPS, SSM_GROUP, SSM_STATE)

    dbb_re, dbb_im = undiag_b(dbcat[..., :SSM_ST]), undiag_b(dbcat[..., SSM_ST:])
    dc_re, dc_im = undiag_c(dccat[:, :, :SSM_ST]), -undiag_c(dccat[:, :, SSM_ST:])
    shape = (2, N_SSM_GROUPS, SSM_STATE)
    return dlam_re.reshape(shape), dlam_im.reshape(shape), dbb_re, dbb_im, dc_re, dc_im


def _to_segments(t):
    l, w = t.shape
    return t.reshape(N_SEG, l // N_SEG, w).transpose(1, 0, 2).reshape(l, w)


def _from_segments(t):
    l, w = t.shape
    return t.reshape(l // N_SEG, N_SEG, w).transpose(1, 0, 2).reshape(l, w)


SCAN_UNROLL = 4


def _cfma(ar, ai, xr, xi, br, bi):
    return ar * xr - ai * xi + br, ar * xi + ai * xr + bi


def _scan_segments(xs_ref, ar, ai, rev, nj, prev_ref=None, before_sums=None):
    shape = (N_SEG, SSM_ST)
    ar = jnp.broadcast_to(ar, shape)
    ai = jnp.broadcast_to(ai, shape)
    zero = jnp.zeros(shape, F32)
    re_cols, im_cols = pl.ds(0, SSM_ST), pl.ds(SSM_ST, SSM_ST)

    def rows_of(jj):
        j = jnp.where(rev, nj - 1 - jj, jj)
        return j, pl.ds(pl.multiple_of(j * N_SEG, N_SEG), N_SEG)

    def steps(step, init, last_step=None):
        def outer(o, carry):
            for k in range(SCAN_UNROLL):
                carry = step(o * SCAN_UNROLL + k, carry)
            return carry

        carry = lax.fori_loop(0, nj // SCAN_UNROLL - 1, outer, init)
        for jj in range(nj - SCAN_UNROLL, nj):
            carry = (last_step if last_step is not None and jj == nj - 1 else step)(jj, carry)
        return carry

    def pass1(jj, carry):
        _, rows = rows_of(jj)
        return _cfma(ar, ai, carry[0], carry[1], xs_ref[rows, re_cols], xs_ref[rows, im_cols])

    end_r, end_i = steps(pass1, (zero, zero))

    pr, pi = ar, ai
    for _ in range(int(math.log2(nj))):
        pr, pi = pr * pr - pi * pi, 2.0 * pr * pi
    seg = lax.broadcasted_iota(jnp.int32, shape, 0)

    def chain(shift, keep):
        ir, ii = zero, zero
        for _ in range(N_SEG - 1):
            tr, ti = _cfma(pr, pi, ir, ii, end_r, end_i)
            ir = jnp.where(keep, pltpu.roll(tr, shift, 0), 0.0)
            ii = jnp.where(keep, pltpu.roll(ti, shift, 0), 0.0)
        return ir, ii

    up_r, up_i = chain(1, seg >= 1)
    dn_r, dn_i = chain(N_SEG - 1, seg <= N_SEG - 2)
    init_r, init_i = jnp.where(rev, dn_r, up_r), jnp.where(rev, dn_i, up_i)

    def pass2(jj, carry):
        j, rows = rows_of(jj)
        nr, ni = _cfma(ar, ai, carry[0], carry[1], xs_ref[rows, re_cols], xs_ref[rows, im_cols])
        xs_ref[rows, re_cols] = nr
        xs_ref[rows, im_cols] = ni
        return (j, nr, ni) + tuple(carry[2:])

    def pass2_plain(jj, carry):
        return pass2(jj, carry)[1:]

    def pass2_sums(jj, carry):
        j, nr, ni, acc_r, acc_i = pass2(jj, carry)
        jp = jnp.where(rev, j - 1, j + 1)
        prow = pl.ds(pl.multiple_of(jp * N_SEG, N_SEG), N_SEG)
        xr, xi = prev_ref[prow, re_cols], prev_ref[prow, im_cols]
        return nr, ni, acc_r + (nr * xr + ni * xi), acc_i + (ni * xr - nr * xi)

    if prev_ref is None:
        steps(pass2_plain, (init_r, init_i))
        return init_r, init_i, None, None
    if before_sums is not None:
        before_sums()
    _, _, acc_r, acc_i = steps(pass2_sums, (init_r, init_i, zero, zero), last_step=pass2_plain)
    return init_r, init_i, acc_r, acc_i


SSM_RC = 256


def _ssm_specs(l):
    act = pl.BlockSpec((l, SSM_CH), lambda k, d: (0, k))
    bmat = pl.BlockSpec((None, None, SSM_CH, 2 * SSM_ST), lambda k, d: (d, k, 0, 0))
    cmat = pl.BlockSpec((None, None, 2 * SSM_ST, SSM_CH), lambda k, d: (d, k, 0, 0))
    lam = pl.BlockSpec((None, None, 1, SSM_ST), lambda k, d: (d, k, 0, 0))
    return act, bmat, cmat, lam


def _ssm_fwd(u_seg, bcat, ccat, lam_re, lam_im):
    l = u_seg.shape[0]
    nj = l // N_SEG

    def body(u_ref, b_ref, c_ref, lr_ref, li_ref, y_ref, keep_ref, xs_ref, keep_sem):
        k, d = pl.program_id(0), pl.program_id(1)

        def bu_chunk(i, _):
            rows = pl.ds(pl.multiple_of(i * SSM_RC, SSM_RC), SSM_RC)
            xs_ref[rows, :] = _dg(u_ref[rows, :], b_ref[...], NN)
            return 0

        lax.fori_loop(0, l // SSM_RC, bu_chunk, 0)
        _scan_segments(xs_ref, lr_ref[...], li_ref[...], d == 1, nj)
        keep = pltpu.make_async_copy(xs_ref, keep_ref.at[d, k], keep_sem)
        keep.start()

        def y_chunk(i, _):
            rows = pl.ds(pl.multiple_of(i * SSM_RC, SSM_RC), SSM_RC)
            yv = _dg(xs_ref[rows, :].astype(BF16), c_ref[...], NN)

            @pl.when(d == 0)
            def _():
                y_ref[rows, :] = yv

            @pl.when(d == 1)
            def _():
                y_ref[rows, :] += yv

            return 0

        lax.fori_loop(0, l // SSM_RC, y_chunk, 0)
        keep.wait()

    act, bmat, cmat, lam = _ssm_specs(l)
    return pl.pallas_call(
        body, grid=(SSM_CB, 2), in_specs=[act, bmat, cmat, lam, lam], out_specs=[act, ANY],
        out_shape=[jax.ShapeDtypeStruct((l, SSM_WIDTH), F32),
                   jax.ShapeDtypeStruct((2, SSM_CB, l, 2 * SSM_ST), F32)],
        scratch_shapes=[pltpu.VMEM((l, 2 * SSM_ST), F32), pltpu.SemaphoreType.DMA],
        name="ssm_fwd", compiler_params=_params(("parallel", "arbitrary"), vmem_mb=56),
    )(u_seg, bcat.astype(BF16), ccat.astype(BF16), lam_re, lam_im)


def _ssm_bwd(u_seg, dy_seg, states, bcat, ccat, lam_re, lam_im):
    l = u_seg.shape[0]
    nj = l // N_SEG

    rc2 = min(l, 2 * SSM_RC)

    def body(u_ref, dy_ref, keep_ref, b_ref, c_ref, lr_ref, li_ref,
             du_ref, db_ref, dc_ref, dlr_ref, dli_ref, xs_ref, gs_ref, keep_sem):
        k, d = pl.program_id(0), pl.program_id(1)
        rev = d == 1
        ar, ai = lr_ref[...], li_ref[...]
        fetch = pltpu.make_async_copy(keep_ref.at[d, k], xs_ref, keep_sem)
        fetch.start()

        def chunk1(i, _):
            rows = pl.ds(pl.multiple_of(i * SSM_RC, SSM_RC), SSM_RC)
            gs_ref[rows, :] = _dg(dy_ref[rows, :], c_ref[...], NT)
            return 0

        lax.fori_loop(0, l // SSM_RC, chunk1, 0)
        _, _, acc_r, acc_i = _scan_segments(gs_ref, ar, -ai, jnp.logical_not(rev), nj, prev_ref=xs_ref,
                                            before_sums=fetch.wait)
        seg = lax.broadcasted_iota(jnp.int32, (N_SEG, SSM_ST), 0)
        jb = jnp.where(rev, nj - 1, 0)
        brow = pl.ds(pl.multiple_of(jb * N_SEG, N_SEG), N_SEG)
        erow = pl.ds(pl.multiple_of((nj - 1 - jb) * N_SEG, N_SEG), N_SEG)
        re_cols, im_cols = pl.ds(0, SSM_ST), pl.ds(SSM_ST, SSM_ST)

        def before(t):
            up = jnp.where(seg >= 1, pltpu.roll(t, 1, 0), 0.0)
            down = jnp.where(seg <= N_SEG - 2, pltpu.roll(t, N_SEG - 1, 0), 0.0)
            return jnp.where(rev, down, up)

        init_r, init_i = before(xs_ref[erow, re_cols]), before(xs_ref[erow, im_cols])
        gr, gi = gs_ref[brow, re_cols], gs_ref[brow, im_cols]
        acc_r = acc_r + gr * init_r + gi * init_i
        acc_i = acc_i + gi * init_r - gr * init_i
        dlr_ref[...] = jnp.sum(acc_r, axis=0, keepdims=True)
        dli_ref[...] = jnp.sum(acc_i, axis=0, keepdims=True)

        db_ref[...] = jnp.zeros_like(db_ref)
        dc_ref[...] = jnp.zeros_like(dc_ref)

        def chunk2(i, _):
            rows = pl.ds(pl.multiple_of(i * rc2, rc2), rc2)
            g = gs_ref[rows, :].astype(BF16)
            dc_ref[...] += _dg(xs_ref[rows, :].astype(BF16), dy_ref[rows, :], TN)
            db_ref[...] += _dg(u_ref[rows, :], g, TN)
            duv = _dg(g, b_ref[...], NT)

            @pl.when(d == 0)
            def _():
                du_ref[rows, :] = duv

            @pl.when(d == 1)
            def _():
                du_ref[rows, :] += duv

            return 0

        lax.fori_loop(0, l // rc2, chunk2, 0)

    act, bmat, cmat, lam = _ssm_specs(l)
    return pl.pallas_call(
        body, grid=(SSM_CB, 2), in_specs=[act, act, ANY, bmat, cmat, lam, lam],
        out_specs=[act, bmat, cmat, lam, lam],
        out_shape=[jax.ShapeDtypeStruct((l, SSM_WIDTH), F32),
                   jax.ShapeDtypeStruct(bcat.shape, F32), jax.ShapeDtypeStruct(ccat.shape, F32),
                   jax.ShapeDtypeStruct(lam_re.shape, F32), jax.ShapeDtypeStruct(lam_im.shape, F32)],
        scratch_shapes=[pltpu.VMEM((l, 2 * SSM_ST), F32), pltpu.VMEM((l, 2 * SSM_ST), F32),
                        pltpu.SemaphoreType.DMA],
        name="ssm_bwd", compiler_params=_params(("parallel", "arbitrary"), vmem_mb=56),
    )(u_seg, dy_seg, states, bcat.astype(BF16), ccat.astype(BF16), lam_re, lam_im)


def _glu_fwd(y_ssm, u, d_skip, w_glu):
    l, w = u.shape

    def body(y_ref, u_ref, d_ref, w_ref, pre_ref, s_ref, ys_ref):
        pre = y_ref[...] + d_ref[...] * u_ref[...]
        z = _gelu(pre)
        s = _dg(z.astype(BF16), w_ref[...], NN)
        pre_ref[...] = pre
        s_ref[...] = s
        ys_ref[...] = z * _sigmoid(s)

    row = pl.BlockSpec((TM_EW, w), lambda i: (i, 0))
    out = jax.ShapeDtypeStruct((l, w), F32)
    return pl.pallas_call(
        body, grid=(l // TM_EW,),
        in_specs=[row, row, pl.BlockSpec((1, w), lambda i: (0, 0)), pl.BlockSpec((w, w), lambda i: (0, 0))],
        out_specs=[row, row, row], out_shape=[out, out, out], name="glu_fwd",
        compiler_params=_params(("parallel",)),
    )(y_ssm, u, d_skip, w_glu)


def _glu_bwd(pre, s, dys, u, d_skip, w_glu):
    l, w = u.shape

    def body(pre_ref, s_ref, dys_ref, u_ref, d_ref, w_ref, dpre_ref, z_ref, ds_ref, dd_ref):
        pre, dys = pre_ref[...], dys_ref[...]
        z = _gelu(pre)
        sig = _sigmoid(s_ref[...])
        ds = (dys * z * sig * (1.0 - sig)).astype(BF16)
        dz = dys * sig + _dg(ds, w_ref[...], NT)
        dpre = dz * _gelu_grad(pre)
        dpre_ref[...] = dpre
        z_ref[...] = z.astype(BF16)
        ds_ref[...] = ds

        @pl.when(pl.program_id(0) == 0)
        def _():
            dd_ref[...] = jnp.zeros_like(dd_ref)

        dd_ref[...] += jnp.sum(dpre * u_ref[...], axis=0, keepdims=True)

    row = pl.BlockSpec((TM_EW, w), lambda i: (i, 0))
    vec = pl.BlockSpec((1, w), lambda i: (0, 0))
    return pl.pallas_call(
        body, grid=(l // TM_EW,),
        in_specs=[row, row, row, row, vec, pl.BlockSpec((w, w), lambda i: (0, 0))],
        out_specs=[row, row, row, vec],
        out_shape=[jax.ShapeDtypeStruct((l, w), F32), jax.ShapeDtypeStruct((l, w), BF16),
                   jax.ShapeDtypeStruct((l, w), BF16), jax.ShapeDtypeStruct((1, w), F32)],
        name="glu_bwd", compiler_params=_params(("arbitrary",)),
    )(pre, s, dys, u, d_skip, w_glu)


TM_CV = 512
TC_CV = 256
HALO = SUBLANES


def _conv_specs(l, col0):
    per = TM_CV // HALO
    nh = l // HALO
    off = col0 // TC_CV
    return [
        pl.BlockSpec((HALO, TC_CV), lambda j, i: (jnp.maximum(i * per - 1, 0), j + off)),
        pl.BlockSpec((TM_CV, TC_CV), lambda j, i: (i, j + off)),
        pl.BlockSpec((HALO, TC_CV), lambda j, i: (jnp.minimum((i + 1) * per, nh - 1), j + off)),
    ]


def _ext(prev_ref, mid_ref, next_ref, first, last):
    p = jnp.where(first, 0.0, prev_ref[...])
    n = jnp.where(last, 0.0, next_ref[...])
    return jnp.concatenate([p, mid_ref[...], n], axis=0)


def _shift_dn(t):
    return pltpu.roll(t, 1, 0)


def _shift_up(t):
    return pltpu.roll(t, t.shape[0] - 1, 0)


def _conv3(e, w_ref, b_ref):
    return w_ref[0:1, :] * _shift_dn(e) + w_ref[1:2, :] * e + w_ref[2:3, :] * _shift_up(e) + b_ref[...]


def _convffn_fwd(up_pre, conv_w, conv_b):
    l = up_pre.shape[0]
    ni = l // TM_CV
    wspec = lambda off: pl.BlockSpec((3, TC_CV), lambda j, i: (0, j + off))
    bspec = lambda off: pl.BlockSpec((1, TC_CV), lambda j, i: (0, j + off))
    voff = D_FF // TC_CV

    def body(gp, gm, gn, vp, vm, vn, wg, bg, wv, bv, o_ref):
        i = pl.program_id(1)
        first, last = i == 0, i == ni - 1
        gate = _conv3(_ext(gp, gm, gn, first, last), wg, bg)[HALO:HALO + TM_CV]
        val = _conv3(_ext(vp, vm, vn, first, last), wv, bv)[HALO:HALO + TM_CV]
        o_ref[...] = (gate * _sigmoid(gate) * val).astype(BF16)

    return pl.pallas_call(
        body, grid=(D_FF // TC_CV, ni),
        in_specs=_conv_specs(l, 0) + _conv_specs(l, D_FF) + [wspec(0), bspec(0), wspec(voff), bspec(voff)],
        out_specs=pl.BlockSpec((TM_CV, TC_CV), lambda j, i: (i, j)),
        out_shape=jax.ShapeDtypeStruct((l, D_FF), BF16), name="convffn_fwd",
        compiler_params=_params(("parallel", "parallel")),
    )(up_pre, up_pre, up_pre, up_pre, up_pre, up_pre, conv_w, conv_b, conv_w, conv_b)


def _convffn_bwd(up_pre, dact, conv_w, conv_b):
    l = up_pre.shape[0]
    ni = l // TM_CV
    wspec = lambda off: pl.BlockSpec((3, TC_CV), lambda j, i: (0, j + off))
    bspec = lambda off: pl.BlockSpec((1, TC_CV), lambda j, i: (0, j + off))
    voff = D_FF // TC_CV

    def body(gp, gm, gn, vp, vm, vn, dp, dm, dn, wg, bg, wv, bv, dup_ref, pg_ref, pv_ref):
        i = pl.program_id(1)
        first, last = i == 0, i == ni - 1
        ge, ve, de = _ext(gp, gm, gn, first, last), _ext(vp, vm, vn, first, last), _ext(dp, dm, dn, first, last)
        gate, val = _conv3(ge, wg, bg), _conv3(ve, wv, bv)
        sig = _sigmoid(gate)
        silu = gate * sig
        dgate = de * val * (sig + silu * (1.0 - sig))
        dval = de * silu
        mid = slice(HALO, HALO + TM_CV)
        rid = lax.broadcasted_iota(jnp.int32, (SUBLANES, TC_CV), 0)

        @pl.when(i == 0)
        def _():
            pg_ref[...] = jnp.zeros_like(pg_ref)
            pv_ref[...] = jnp.zeros_like(pv_ref)

        for half, (dup, e, w_ref, p_ref) in enumerate(((dgate, ge, wg, pg_ref), (dval, ve, wv, pv_ref))):
            dpre = w_ref[0:1, :] * _shift_up(dup) + w_ref[1:2, :] * dup + w_ref[2:3, :] * _shift_dn(dup)
            dup_ref[half] = dpre[mid].astype(BF16)
            dm_ = dup[mid]
            sums = [jnp.sum(dm_ * _shift_dn(e)[mid], axis=0, keepdims=True),
                    jnp.sum(dm_ * e[mid], axis=0, keepdims=True),
                    jnp.sum(dm_ * _shift_up(e)[mid], axis=0, keepdims=True),
                    jnp.sum(dm_, axis=0, keepdims=True)]
            acc = jnp.zeros((SUBLANES, TC_CV), F32)
            for k, sk in enumerate(sums):
                acc = jnp.where(rid == k, sk, acc)
            p_ref[...] += acc

    par = pl.BlockSpec((SUBLANES, TC_CV), lambda j, i: (0, j))
    dup, pg, pv = pl.pallas_call(
        body, grid=(D_FF // TC_CV, ni),
        in_specs=_conv_specs(l, 0) + _conv_specs(l, D_FF) + _conv_specs(l, 0)
        + [wspec(0), bspec(0), wspec(voff), bspec(voff)],
        out_specs=[pl.BlockSpec((2, TM_CV, TC_CV), lambda j, i: (0, i, j)), par, par],
        out_shape=[jax.ShapeDtypeStruct((2, l, D_FF), BF16),
                   jax.ShapeDtypeStruct((SUBLANES, D_FF), F32), jax.ShapeDtypeStruct((SUBLANES, D_FF), F32)],
        name="convffn_bwd", compiler_params=_params(("parallel", "arbitrary")),
    )(up_pre, up_pre, up_pre, up_pre, up_pre, up_pre, dact, dact, dact, conv_w, conv_b, conv_w, conv_b)
    return dup, jnp.concatenate([pg, pv], axis=1)


def _local_step(x, target, wb, sp, late_weights=None):
    l = x.shape[0]
    tabs = _rope_tables(l)
    disc = _ssm_disc(sp["a_re"], sp["a_im"], sp["log_step"], sp["b_re"], sp["b_im"])
    bcat, ccat, lam_re, lam_im = _ssm_pack(*disc, sp["c_re"], sp["c_im"])
    d_skip = sp["d_skip"].reshape(1, SSM_WIDTH)

    big = min(l, 1024)
    h = _rms_fwd(x, sp["norm_mix_g"], "rms_mix")
    proj, u = _mm_split(h, wb["w_in"], big, QKV_WIDTH, "mm_in")
    qkv = _rope_fwd(proj, tabs)
    attn, lse = _attn_fwd(qkv, sp["sink"])
    u_seg = _to_segments(u).astype(BF16)
    y_seg, states = _ssm_fwd(u_seg, bcat, ccat, lam_re, lam_im)
    y_ssm = _from_segments(y_seg)
    pre, s_glu, ys = _glu_fwd(y_ssm, u, d_skip, wb["w_glu"])
    mixed = _mix_fwd(attn, ys, sp["norm_attn_g"], sp["norm_ssm_g"])
    x1 = _mm_nn(mixed, wb["w_out"], big, 1024, F32, "mm_out", res=x)
    h2 = _rms_fwd(x1, sp["norm_ffn_g"], "rms_ffn")
    if late_weights is not None:
        wb = dict(wb, **late_weights(h2))
    up_pre = _mm_nn_cols(h2, wb["w_up"], big, "mm_up")
    act = _convffn_fwd(up_pre, sp["conv_w"], sp["conv_b"])
    x2 = _mm_nn(act, wb["w_down"], big, 512, F32, "mm_down", res=x1)
    loss, dx2, dx2b, d_final_g = _final_loss(x2, sp["norm_final_g"].reshape(1, D_MODEL), target)

    g = {"norm_final_g": d_final_g.reshape(D_MODEL)}
    dact = _mm_nt(dx2b, wb["w_down"], big, D_FF // 2, F32, "mm_down_dx")
    g["w_down"] = _mm_tn(act, dx2b, D_FF // 2, 512, "mm_down_dw")
    dup_pre, conv_par = _convffn_bwd(up_pre, dact, sp["conv_w"], sp["conv_b"])
    g["conv_w"], g["conv_b"] = conv_par[0:3], conv_par[3:4]
    g["w_up"] = _mm_tn_cols(h2, dup_pre, wb["w_up"].shape[0], 512, "mm_up_dw")
    dh2 = _mm_nt_cols(dup_pre, wb["w_up"], 512, "mm_up_dx")
    dx1, dx1b, g["norm_ffn_g"] = _rms_bwd(x1, sp["norm_ffn_g"], dh2, dx2, "rms_ffn_bwd")
    dmixed = _mm_nt(dx1b, wb["w_out"], big, 1024, F32, "mm_out_dx")
    g["w_out"] = _mm_tn(mixed, dx1b, 1024, 1024, "mm_out_dw")
    dattn, dys, g["norm_attn_g"], g["norm_ssm_g"] = _mix_bwd(attn, ys, sp["norm_attn_g"], sp["norm_ssm_g"], dmixed)
    dpre, zb, dsb, dd = _glu_bwd(pre, s_glu, dys, u, d_skip, wb["w_glu"])
    g["d_skip"] = dd.reshape(N_SSM_GROUPS, SSM_GROUP)
    g["w_glu"] = _mm_tn(zb, dsb, 512, 512, "mm_glu_dw")
    du_seg, dbcat, dccat, dlam_re, dlam_im = _ssm_bwd(u_seg, _to_segments(dpre).astype(BF16), states, bcat, ccat,
                                                      lam_re, lam_im)
    dlb_re, dlb_im, dbb_re, dbb_im, g["c_re"], g["c_im"] = _ssm_unpack(dbcat, dccat, dlam_re, dlam_im)
    _, disc_vjp = jax.vjp(_ssm_disc, sp["a_re"], sp["a_im"], sp["log_step"], sp["b_re"], sp["b_im"])
    g["a_re"], g["a_im"], g["log_step"], g["b_re"], g["b_im"] = disc_vjp((dlb_re, dlb_im, dbb_re, dbb_im))
    dqkv, dsink = _attn_bwd(qkv, attn, dattn, lse, sp["sink"])
    g["sink"] = dsink
    dproj = _rope_bwd(dqkv, _from_segments(du_seg), dpre, d_skip, tabs)
    g["w_in"] = _mm_tn(h, dproj, 512, IN_WIDTH, "mm_in_dw")
    dh = _mm_nt(dproj, wb["w_in"], 512, 512, F32, "mm_in_dx")
    grad_x, _, g["norm_mix_g"] = _rms_bwd(x, sp["norm_mix_g"], dh, dx1, "rms_mix_bwd")
    return loss, grad_x, g


MESH = pl.DeviceIdType.MESH
ANY = pl.BlockSpec(memory_space=pl.ANY)


def _place():
    x, y, c = lax.axis_index("x"), lax.axis_index("y"), lax.axis_index("c")
    chips = [(1 - x, y), (x, 1 - y), (1 - x, 1 - y)]
    return x, y, c, chips


def _chip_index(px, py):
    return 2 * px + py


CHUNK_BYTES = 256 * 1024
MAX_CHUNKS = 16


def _row_chunks(rows, row_bytes, align):
    n = max(1, min(MAX_CHUNKS, (rows * row_bytes) // CHUNK_BYTES))
    per = -(-rows // n)
    per = -(-per // align) * align
    return [(r0, min(per, rows - r0)) for r0 in range(0, rows, per)]


def _align_of(dtype):
    return SUBLANES * 4 // jnp.dtype(dtype).itemsize


def _remote(src, dst, send_sem, recv_sem, to):
    return pltpu.make_async_remote_copy(src_ref=src, dst_ref=dst, send_sem=send_sem, recv_sem=recv_sem,
                                        device_id=to, device_id_type=MESH)


CAST_ROWS = 64


def _gather_weights(shards, dtypes):
    nw = len(shards)

    def body(*refs):
        w_refs, o_refs = refs[:nw], refs[nw:2 * nw]
        send_sems, recv_sems, in_sems, out_sems = refs[2 * nw:2 * nw + 4]
        raw, cast = refs[2 * nw + 4:3 * nw + 4], refs[3 * nw + 4:]
        x, y, c, chips = _place()
        mine = _chip_index(x, y)
        sibling = (x, y, 1 - c)

        def rows_of(ref, chip, r0, nr):
            return ref.at[chip, pl.ds(r0, nr), :]

        def copy(wi, k, src, dst, to):
            return _remote(src, dst, send_sems.at[wi, k], recv_sems.at[wi, k], to)

        geo = []
        for wi in range(nw):
            rows, cols = w_refs[wi].shape
            row_bytes = cols * jnp.dtype(dtypes[wi]).itemsize
            geo.append((rows // 2, _row_chunks(rows // 2, row_bytes, _align_of(dtypes[wi]))))

        stage_in = [pltpu.make_async_copy(w_refs[wi], raw[wi], in_sems.at[wi]) for wi in range(nw)]
        for cp in stage_in:
            cp.start()
        staged = [raw[wi] if dtypes[wi] == w_refs[wi].dtype else cast[wi] for wi in range(nw)]
        stage_out = []
        for wi in range(nw):
            stage_in[wi].wait()
            if staged[wi] is not raw[wi]:
                def cast_rows(i, _, wi=wi):
                    rows = pl.ds(pl.multiple_of(i * CAST_ROWS, CAST_ROWS), CAST_ROWS)
                    cast[wi][rows, :] = raw[wi][rows, :].astype(dtypes[wi])
                    return 0

                lax.fori_loop(0, w_refs[wi].shape[0] // CAST_ROWS, cast_rows, 0)
            cp = pltpu.make_async_copy(staged[wi], o_refs[wi].at[mine], out_sems.at[wi])
            cp.start()
            stage_out.append(cp)

        for wi in range(nw):
            hr, half_chunks = geo[wi]
            for j, chip in enumerate(chips):
                for r0, nr in half_chunks:
                    copy(wi, j, staged[wi].at[pl.ds(c * hr + r0, nr), :],
                         rows_of(o_refs[wi], mine, c * hr + r0, nr), (*chip, c)).start()
        for wi in range(nw):
            hr, half_chunks = geo[wi]
            for j, chip in enumerate(chips):
                got = rows_of(o_refs[wi], _chip_index(*chip), c * hr, hr)
                copy(wi, j, got, got, (*chip, c)).wait_recv()
                for r0, nr in half_chunks:
                    piece = rows_of(o_refs[wi], _chip_index(*chip), c * hr + r0, nr)
                    copy(wi, 3 + j, piece, piece, sibling).start()
        for wi in range(nw):
            hr = geo[wi][0]
            for j, chip in enumerate(chips):
                got = rows_of(o_refs[wi], _chip_index(*chip), (1 - c) * hr, hr)
                copy(wi, 3 + j, got, got, sibling).wait_recv()
        for wi in range(nw):
            hr = geo[wi][0]
            sent = rows_of(o_refs[wi], mine, c * hr, hr)
            for k in range(6):
                copy(wi, k, sent, sent, sibling).wait_send()
            stage_out[wi].wait()

    return pl.pallas_call(
        body, in_specs=[ANY] * nw, out_specs=[ANY] * nw,
        out_shape=[jax.ShapeDtypeStruct((4, *s.shape), t) for s, t in zip(shards, dtypes)],
        scratch_shapes=[pltpu.SemaphoreType.DMA((nw, 6)), pltpu.SemaphoreType.DMA((nw, 6)),
                        pltpu.SemaphoreType.DMA((nw,)), pltpu.SemaphoreType.DMA((nw,))]
        + [pltpu.VMEM(s.shape, s.dtype) for s in shards] + [pltpu.VMEM(s.shape, t) for s, t in zip(shards, dtypes)],
        name="gather_weights", compiler_params=_params(vmem_mb=40),
    )(*shards)


HBM = pl.BlockSpec(memory_space=pltpu.HBM)
SEM = pl.BlockSpec(memory_space=pltpu.SEMAPHORE)
EFFECT = pltpu.SideEffectType.DATAFLOW_SIDE_EFFECTING


def _cast_place(w, place, dtype, name):
    rows, cols = w.shape
    tr = _row_tile(rows, cols, _align_of(dtype))

    def body(p_ref, w_ref, o_ref):
        del p_ref
        o_ref[...] = w_ref[...].astype(dtype)

    grid_spec = pltpu.PrefetchScalarGridSpec(
        num_scalar_prefetch=1, grid=(rows // tr,),
        in_specs=[pl.BlockSpec((tr, cols), lambda i, p: (i, 0))],
        out_specs=pl.BlockSpec((None, tr, cols), lambda i, p: (p[1], i, 0)))
    return pl.pallas_call(body, grid_spec=grid_spec, out_shape=jax.ShapeDtypeStruct((4, rows, cols), dtype),
                          name=name, compiler_params=_params(("parallel",)))(place, w)


def _spread_start(lands):
    n = len(lands)

    def body(*refs):
        land_refs, send_sems, recv_sems = refs[:n], refs[n:2 * n], refs[2 * n:3 * n]
        token = refs[4 * n]
        x, y, c, chips = _place()
        mine = _chip_index(x, y)
        for a in range(n):
            _, rows, cols = land_refs[a].shape
            hr = rows // 2
            row_bytes = cols * jnp.dtype(land_refs[a].dtype).itemsize
            for r0, nr in _row_chunks(hr, row_bytes, _align_of(land_refs[a].dtype)):
                piece = land_refs[a].at[mine, pl.ds(c * hr + r0, nr), :]
                for chip in chips:
                    for core in (0, 1):
                        _remote(piece, piece, send_sems[a], recv_sems[a], (*chip, core)).start()
        token[...] = jnp.zeros_like(token)

    dma = pltpu.SemaphoreType.DMA(())
    outs = pl.pallas_call(
        body, name="spread_start",
        out_shape=[dma] * (2 * n) + [pltpu.HBM(t.shape, t.dtype) for t in lands]
        + [jax.ShapeDtypeStruct((SUBLANES, LANES), F32)],
        in_specs=[HBM] * n, out_specs=[SEM] * (2 * n) + [HBM] * n + [pl.BlockSpec(memory_space=pltpu.VMEM)],
        input_output_aliases={a: 2 * n + a for a in range(n)},
        compiler_params=pltpu.CompilerParams(has_side_effects=EFFECT),
    )(*[pltpu.with_memory_space_constraint(t, pltpu.HBM) for t in lands])
    return outs[:n], outs[n:2 * n], outs[2 * n:3 * n], outs[3 * n]


def _spread_wait(send_sems, recv_sems, flying, after):
    n = len(flying)

    def body(*refs):
        land_refs, sends, recvs = refs[:n], refs[n:2 * n], refs[2 * n:3 * n]
        x, y, c, _ = _place()
        for a in range(n):
            three = land_refs[a].at[pl.ds(0, 3)]
            cp = _remote(three, three, sends[a], recvs[a], (x, y, 1 - c))
            cp.wait_send()
            cp.wait_recv()

    return pl.pallas_call(
        body, name="spread_wait", out_shape=[pltpu.HBM(t.shape, t.dtype) for t in flying],
        in_specs=[HBM] * n + [SEM] * (2 * n) + [ANY], out_specs=[HBM] * n,
        input_output_aliases={a: a for a in range(n)},
        compiler_params=pltpu.CompilerParams(has_side_effects=EFFECT),
    )(*flying, *send_sems, *recv_sems, after)


def _pair_exchange(grads):
    na = len(grads)

    def body(*refs):
        g_refs, o_refs = refs[:na], refs[na:2 * na]
        send_sems, recv_sems = refs[2 * na:]
        x, y, c, _ = _place()
        sibling = (x, y, 1 - c)
        for ai in range(na):
            _, rows, cols = g_refs[ai].shape
            hr = rows // 2
            for k in range(4):
                for r0, nr in _row_chunks(hr, cols * 4, SUBLANES):
                    _remote(g_refs[ai].at[k, pl.ds((1 - c) * hr + r0, nr), :], o_refs[ai].at[k, pl.ds(r0, nr), :],
                            send_sems.at[ai], recv_sems.at[ai], sibling).start()
        for ai in range(na):
            _remote(o_refs[ai], o_refs[ai], send_sems.at[ai], recv_sems.at[ai], sibling).wait()

    return pl.pallas_call(
        body, in_specs=[ANY] * na, out_specs=[ANY] * na,
        out_shape=[jax.ShapeDtypeStruct((4, g.shape[1] // 2, g.shape[2]), F32) for g in grads],
        scratch_shapes=[pltpu.SemaphoreType.DMA((na,)), pltpu.SemaphoreType.DMA((na,))],
        name="pair_exchange",
    )(*grads)


def _row_tile(rows, cols, align):
    best = align
    for cand in range(align, rows + 1, align):
        if rows % cand == 0 and cand * cols <= 256 * 1024:
            best = cand
    return best


def _pair_sum(g, got, place, transit, name):
    _, rows, cols = g.shape
    hr = rows // 2
    tr = _row_tile(hr, cols, _align_of(transit))
    nt = hr // tr

    def body(p_ref, g_ref, r_ref, s_ref, own_ref):
        total = g_ref[...] + r_ref[...]
        s_ref[...] = total.astype(transit)

        @pl.when(pl.program_id(1) == p_ref[1])
        def _():
            own_ref[...] = total

    grid_spec = pltpu.PrefetchScalarGridSpec(
        num_scalar_prefetch=1, grid=(nt, 4),
        in_specs=[pl.BlockSpec((None, tr, cols), lambda i, k, p: (k, p[0] * nt + i, 0)),
                  pl.BlockSpec((None, tr, cols), lambda i, k, p: (k, i, 0))],
        out_specs=[pl.BlockSpec((None, tr, cols), lambda i, k, p: (k, i, 0)),
                   pl.BlockSpec((tr, cols), lambda i, k, p: (i, 0))])
    return pl.pallas_call(
        body, grid_spec=grid_spec,
        out_shape=[jax.ShapeDtypeStruct((4, hr, cols), transit), jax.ShapeDtypeStruct((hr, cols), F32)],
        name=name, compiler_params=_params(("parallel", "arbitrary")),
    )(place, g, got)


def _chip_exchange(sums):
    na = len(sums)

    def body(*refs):
        s_refs, o_refs = refs[:na], refs[na:2 * na]
        send_sems, recv_sems = refs[2 * na:]
        x, y, c, chips = _place()
        for ai in range(na):
            _, rows, cols = s_refs[ai].shape
            row_bytes = cols * jnp.dtype(s_refs[ai].dtype).itemsize
            for r0, nr in _row_chunks(rows, row_bytes, _align_of(s_refs[ai].dtype)):
                for j, chip in enumerate(chips):
                    _remote(s_refs[ai].at[_chip_index(*chip), pl.ds(r0, nr), :], o_refs[ai].at[j, pl.ds(r0, nr), :],
                            send_sems.at[ai, j], recv_sems.at[ai, j], (*chip, c)).start()
        for ai in range(na):
            for j, chip in enumerate(chips):
                _remote(o_refs[ai].at[j], o_refs[ai].at[j], send_sems.at[ai, j], recv_sems.at[ai, j],
                        (*chip, c)).wait()

    return pl.pallas_call(
        body, in_specs=[ANY] * na, out_specs=[ANY] * na,
        out_shape=[jax.ShapeDtypeStruct((3, *s.shape[1:]), s.dtype) for s in sums],
        scratch_shapes=[pltpu.SemaphoreType.DMA((na, 3)), pltpu.SemaphoreType.DMA((na, 3))],
        name="chip_exchange",
    )(*sums)


def _chip_sum(own, landed, name):
    hr, cols = own.shape
    tr = _row_tile(hr, cols, _align_of(landed.dtype))

    def body(o_ref, l_ref, f_ref):
        acc = o_ref[...]
        for j in range(3):
            acc = acc + l_ref[j].astype(F32)
        f_ref[...] = acc

    return pl.pallas_call(
        body, grid=(hr // tr,),
        in_specs=[pl.BlockSpec((tr, cols), lambda i: (i, 0)), pl.BlockSpec((3, tr, cols), lambda i: (0, i, 0))],
        out_specs=pl.BlockSpec((tr, cols), lambda i: (i, 0)),
        out_shape=jax.ShapeDtypeStruct((hr, cols), F32), name=name,
        compiler_params=_params(("parallel",)),
    )(own, landed)


def _final_exchange(halves, small):
    nh = len(halves)

    def body(*refs):
        h_refs, s_ref = refs[:nh], refs[nh]
        o_refs, so_ref = refs[nh + 1:2 * nh + 1], refs[2 * nh + 1]
        send_sems, recv_sems, local_sem, ssend_sems, srecv_sems = refs[2 * nh + 2:]
        x, y, c, _ = _place()
        me = 4 * x + 2 * y + c
        sibling = (x, y, 1 - c)
        for hi in range(nh):
            hr, cols = h_refs[hi].shape
            for r0, nr in _row_chunks(hr, cols * 4, SUBLANES):
                _remote(h_refs[hi].at[pl.ds(r0, nr), :], o_refs[hi].at[pl.ds(r0, nr), :],
                        send_sems.at[hi], recv_sems.at[hi], sibling).start()
        small_cps = [pltpu.make_async_copy(s_ref, so_ref.at[me], local_sem)]
        for r in range(1, 8):
            fx, fy, fc = (r >> 2) & 1, (r >> 1) & 1, r & 1
            peer = (1 - x if fx else x, 1 - y if fy else y, 1 - c if fc else c)
            small_cps.append(_remote(s_ref, so_ref.at[me], ssend_sems.at[r - 1], srecv_sems.at[r - 1], peer))
        for cp in small_cps:
            cp.start()
        for hi in range(nh):
            _remote(h_refs[hi], o_refs[hi], send_sems.at[hi], recv_sems.at[hi], sibling).wait()
        for cp in small_cps:
            cp.wait()

    return pl.pallas_call(
        body, in_specs=[ANY] * (nh + 1), out_specs=[ANY] * (nh + 1),
        out_shape=[jax.ShapeDtypeStruct(h.shape, F32) for h in halves]
        + [jax.ShapeDtypeStruct((8, *small.shape), F32)],
        scratch_shapes=[pltpu.SemaphoreType.DMA((nh,)), pltpu.SemaphoreType.DMA((nh,)),
                        pltpu.SemaphoreType.DMA, pltpu.SemaphoreType.DMA((7,)), pltpu.SemaphoreType.DMA((7,))],
        name="final_exchange",
    )(*halves, small)


def _adamw(w, g, m, v, name):
    shape = w.shape
    n = w.size
    if w.ndim >= 2 and shape[-1] >= LANES:
        two_d = (n // shape[-1], shape[-1])
    elif n % LANES == 0:
        two_d = (n // LANES, LANES)
    else:
        two_d = (1, n)
    r, c = two_d
    tr = r
    for cand in (512, 256, 176, 128, 64):
        if r > cand and r % cand == 0 and cand * c <= 256 * 1024:
            tr = cand
            break
    c1 = 1.0 - ADAM_B1 ** ADAM_STEP
    c2 = 1.0 - ADAM_B2 ** ADAM_STEP

    def body(w_ref, g_ref, m_ref, v_ref, d_ref, nm_ref, nv_ref):
        gv = g_ref[...]
        nm = ADAM_B1 * m_ref[...] + (1.0 - ADAM_B1) * gv
        nv = ADAM_B2 * v_ref[...] + (1.0 - ADAM_B2) * (gv * gv)
        d_ref[...] = -ADAM_LR * ((nm / c1) / (jnp.sqrt(nv / c2) + ADAM_EPS) + ADAM_WD * w_ref[...])
        nm_ref[...] = nm
        nv_ref[...] = nv

    spec = pl.BlockSpec((tr, c), lambda i: (i, 0))
    out = jax.ShapeDtypeStruct((r, c), F32)
    d, nm, nv = pl.pallas_call(
        body, grid=(r // tr,), in_specs=[spec] * 4, out_specs=[spec] * 3, out_shape=[out] * 3, name=name,
        compiler_params=_params(("parallel",)),
    )(w.reshape(two_d), g.reshape(two_d), m.reshape(two_d), v.reshape(two_d))
    return d.reshape(shape), nm.reshape(shape), nv.reshape(shape)


def _adamw_many(ws, gs, ms, vs, name):
    n = len(ws)
    c1 = 1.0 - ADAM_B1 ** ADAM_STEP
    c2 = 1.0 - ADAM_B2 ** ADAM_STEP

    def body(*refs):
        w_refs, g_refs, m_refs, v_refs = (refs[k * n:(k + 1) * n] for k in range(4))
        d_refs, nm_refs, nv_refs = (refs[(4 + k) * n:(5 + k) * n] for k in range(3))
        for i in range(n):
            gv = g_refs[i][...]
            nm = ADAM_B1 * m_refs[i][...] + (1.0 - ADAM_B1) * gv
            nv = ADAM_B2 * v_refs[i][...] + (1.0 - ADAM_B2) * (gv * gv)
            d_refs[i][...] = -ADAM_LR * ((nm / c1) / (jnp.sqrt(nv / c2) + ADAM_EPS) + ADAM_WD * w_refs[i][...])
            nm_refs[i][...] = nm
            nv_refs[i][...] = nv

    vmem = pl.BlockSpec(memory_space=pltpu.VMEM)
    shapes = [jax.ShapeDtypeStruct(t.shape, F32) for t in ws]
    outs = pl.pallas_call(body, in_specs=[vmem] * (4 * n), out_specs=[vmem] * (3 * n), out_shape=shapes * 3,
                          name=name, compiler_params=_params(vmem_mb=56))(*ws, *gs, *ms, *vs)
    return outs[:n], outs[n:2 * n], outs[2 * n:]


BIG = ("w_in", "w_glu", "w_out", "w_up", "w_down")
WEIGHTS = ("norm_mix_g", "w_in", "a_re", "a_im", "log_step", "b_re", "b_im", "c_re", "c_im", "d_skip", "w_glu",
           "sink", "norm_attn_g", "norm_ssm_g", "w_out", "norm_ffn_g", "w_up", "conv_w", "conv_b", "w_down",
           "norm_final_g")
SMALL = ("norm_mix_g", "a_re", "a_im", "log_step", "b_re", "b_im", "c_re", "c_im", "d_skip", "sink",
         "norm_attn_g", "norm_ssm_g", "norm_ffn_g", "conv_w", "conv_b", "norm_final_g")
SMALL_ROWS = 40
N_DEV = 8


def _by_owner(name, g):
    if name == "w_up":
        return g
    if name == "w_in":
        return g.reshape(g.shape[0], 4, g.shape[1] // 4).transpose(1, 0, 2)
    return g.reshape(4, g.shape[0] // 4, g.shape[1])


def kernel(x, norm_mix_g, w_in, a_re, a_im, log_step, b_re, b_im, c_re, c_im, d_skip, w_glu, sink, norm_attn_g, norm_ssm_g, w_out, norm_ffn_g, w_up, conv_w, conv_b, w_down, norm_final_g, loss_target, m_norm_mix_g, m_w_in, m_a_re, m_a_im, m_log_step, m_b_re, m_b_im, m_c_re, m_c_im, m_d_skip, m_w_glu, m_sink, m_norm_attn_g, m_norm_ssm_g, m_w_out, m_norm_ffn_g, m_w_up, m_conv_w, m_conv_b, m_w_down, m_norm_final_g, v_norm_mix_g, v_w_in, v_a_re, v_a_im, v_log_step, v_b_re, v_b_im, v_c_re, v_c_im, v_d_skip, v_w_glu, v_sink, v_norm_attn_g, v_norm_ssm_g, v_w_out, v_norm_ffn_g, v_w_up, v_conv_w, v_conv_b, v_w_down, v_norm_final_g):
    given = dict(locals())
    w = {n: given[n] for n in WEIGHTS}
    m = {n: given["m_" + n] for n in WEIGHTS}
    v = {n: given["v_" + n] for n in WEIGHTS}
    xy = 2 * lax.axis_index("x") + lax.axis_index("y")

    core = lax.axis_index("c")
    place = jnp.stack([core, xy]).astype(jnp.int32)

    conv_rows = jnp.pad(w["conv_w"][0], ((0, 2 * SUBLANES - 3), (0, 0)))
    early = ("w_in", "w_glu", "w_out")
    *gathered, conv_all = _gather_weights([w[n][0] for n in early] + [conv_rows], [BF16] * len(early) + [F32])
    late = ("w_up", "w_down")
    send_sems, recv_sems, flying, token = _spread_start([_cast_place(w[n][0], place, BF16, "cast_" + n) for n in late])
    rows = lambda t: t.reshape(4 * t.shape[1], t.shape[2])
    wb = {"w_in": gathered[0].transpose(1, 0, 2).reshape(D_MODEL, IN_WIDTH), "w_glu": rows(gathered[1]),
          "w_out": rows(gathered[2])}

    def late_weights(after):
        w_up4, w_down4 = _spread_wait(send_sems, recv_sems, flying, after)
        return {"w_up": w_up4, "w_down": rows(w_down4)}

    sp = {n: w[n][0] for n in ("a_re", "a_im", "log_step", "b_re", "b_im", "c_re", "c_im", "d_skip",
                               "norm_mix_g", "norm_attn_g", "norm_ssm_g", "norm_ffn_g", "sink", "conv_b")}
    for n in ("norm_mix_g", "norm_attn_g", "norm_ssm_g", "norm_ffn_g", "sink", "conv_b"):
        sp[n] = sp[n].reshape(1, -1)
    sp["norm_mix_g"] = sp["norm_mix_g"] + token[:1, :1]
    sp["conv_w"] = conv_all[:, :3].transpose(1, 0, 2).reshape(3, 2 * D_FF)
    sp["norm_final_g"] = w["norm_final_g"]
    loss, grad_x, g = _local_step(x[0], loss_target[0], wb, sp, late_weights)

    flat = jnp.concatenate([g[n].reshape(-1) for n in SMALL] + [loss.reshape(-1)])
    pad = N_DEV * SMALL_ROWS * D_MODEL - flat.shape[0]
    small = jnp.concatenate([flat, jnp.zeros((pad,), F32)]).reshape(4, 2 * SMALL_ROWS, D_MODEL)
    by_owner = [_by_owner(n, g[n]) for n in BIG] + [small]
    got = _pair_exchange(by_owner)
    transit = [BF16] * len(BIG) + [F32]
    chip_sums, own_sums = zip(*[_pair_sum(a, b, place, t, "pair_sum_%d" % i)
                                for i, (a, b, t) in enumerate(zip(by_owner, got, transit))])
    landed = _chip_exchange(list(chip_sums))
    halves = [_chip_sum(o, t, "chip_sum_%d" % i) for i, (o, t) in enumerate(zip(own_sums, landed))]
    *others, small_all = _final_exchange(halves[:-1], halves[-1])
    grads = {n: jnp.concatenate([jnp.where(core == 0, h, o), jnp.where(core == 0, o, h)], axis=0)
             for n, h, o in zip(BIG, halves, others)}
    flat = small_all.reshape(-1)
    off = 0
    for n in SMALL:
        shape = (3, 4 * w[n].shape[-1]) if n == "conv_w" else w[n].shape[1:] if n != "norm_final_g" else w[n].shape
        size = math.prod(shape)
        grads[n] = flat[off:off + size].reshape(shape)
        off += size
    loss = flat[off]
    cw = w["conv_w"].shape[-1]
    grads["conv_w"] = lax.dynamic_slice_in_dim(grads["conv_w"], xy * cw, cw, axis=1)
    grads = {n: grads[n].reshape(w[n].shape) for n in WEIGHTS}

    delta, new_m, new_v = {}, {}, {}
    for n in BIG:
        delta[n], new_m[n], new_v[n] = _adamw(w[n], grads[n], m[n], v[n], "adamw_" + n)
    for group, name in ((("b_re", "b_im"), "adamw_b"), (tuple(n for n in SMALL if n not in ("b_re", "b_im")), "adamw_small")):
        row = lambda t: t.reshape(1, -1) if t.ndim == 1 else t
        d_, m_, v_ = _adamw_many(*[[row(t[n]) for n in group] for t in (w, grads, m, v)], name)
        for n, dn, mn, vn in zip(group, d_, m_, v_):
            delta[n], new_m[n], new_v[n] = (t.reshape(w[n].shape) for t in (dn, mn, vn))
    return (loss, grad_x[None], *[grads[n] for n in WEIGHTS], *[delta[n] for n in WEIGHTS],
            *[new_m[n] for n in WEIGHTS], *[new_v[n] for n in WEIGHTS])
```

```python
import functools
import math

import jax
import jax.numpy as jnp
from jax import lax
from jax.experimental import pallas as pl
from jax.experimental.pallas import tpu as pltpu

F32 = jnp.float32
BF16 = jnp.bfloat16

D_MODEL = 1024
N_Q_HEADS = 8
N_KV_HEADS = 2
HEAD_DIM = 64
ATTN_WIDTH = 512
KV_WIDTH = 128
QKV_WIDTH = ATTN_WIDTH + 2 * KV_WIDTH
WINDOW = 128
BLOCK = 128
ROPE_DIM = 16
ROPE_THETA = 500000.0
SSM_WIDTH = 512
SSM_GROUP = 16
N_SSM_GROUPS = 32
SSM_STATE = 64
IN_WIDTH = 1280
D_FF = 2816
EPS = 1e-6
ADAM_LR = 0.001
ADAM_B1 = 0.9
ADAM_B2 = 0.999
ADAM_EPS = 1e-08
ADAM_WD = 0.01
ADAM_STEP = 10

VMEM_BYTES_V7X = 64 * 1024 * 1024
SUBLANES = 8
LANES = 128
SSM_CB = 4
SSM_CH = 128
SSM_ST = 512
N_SEG = SUBLANES

NN = (((1,), (0,)), ((), ()))
NT = (((1,), (1,)), ((), ()))
TN = (((0,), (0,)), ((), ()))


def _params(sem=None, vmem_mb=48):
    return pltpu.CompilerParams(dimension_semantics=sem, vmem_limit_bytes=vmem_mb * 1024 * 1024)


def _dg(a, b, dims):
    return lax.dot_general(a, b, dims, preferred_element_type=F32)


def _sigmoid(x):
    return 1.0 / (1.0 + jnp.exp(-x))


_SQRT_HALF = 0.7071067811865476
_INV_SQRT_2PI = 0.3989422804014327


def _gelu(x):
    return 0.5 * x * (1.0 + lax.erf(x * _SQRT_HALF))


def _gelu_grad(x):
    return 0.5 * (1.0 + lax.erf(x * _SQRT_HALF)) + x * (_INV_SQRT_2PI * jnp.exp(-0.5 * x * x))


def _mm_nn(a, b, tm, tn, out_dtype, name, res=None):
    m, k = a.shape
    n = b.shape[1]

    def body(*refs):
        if res is None:
            a_ref, b_ref, o_ref = refs
            o_ref[...] = _dg(a_ref[...], b_ref[...], NN).astype(out_dtype)
        else:
            a_ref, b_ref, r_ref, o_ref = refs
            o_ref[...] = (r_ref[...] + _dg(a_ref[...], b_ref[...], NN)).astype(out_dtype)

    in_specs = [pl.BlockSpec((tm, k), lambda i, j: (i, 0)), pl.BlockSpec((k, tn), lambda i, j: (0, j))]
    args = [a, b]
    if res is not None:
        in_specs.append(pl.BlockSpec((tm, tn), lambda i, j: (i, j)))
        args.append(res)
    return pl.pallas_call(
        body, grid=(m // tm, n // tn), in_specs=in_specs,
        out_specs=pl.BlockSpec((tm, tn), lambda i, j: (i, j)),
        out_shape=jax.ShapeDtypeStruct((m, n), out_dtype), name=name,
        compiler_params=_params(("parallel", "parallel")),
    )(*args)


def _mm_nt(a, b, tm, tn, out_dtype, name):
    m, k = a.shape
    n = b.shape[0]

    def body(a_ref, b_ref, o_ref):
        o_ref[...] = _dg(a_ref[...], b_ref[...], NT).astype(out_dtype)

    return pl.pallas_call(
        body, grid=(m // tm, n // tn),
        in_specs=[pl.BlockSpec((tm, k), lambda i, j: (i, 0)), pl.BlockSpec((tn, k), lambda i, j: (j, 0))],
        out_specs=pl.BlockSpec((tm, tn), lambda i, j: (i, j)),
        out_shape=jax.ShapeDtypeStruct((m, n), out_dtype), name=name,
        compiler_params=_params(("parallel", "parallel")),
    )(a, b)


def _mm_tn(a, b, tm, tn, name):
    k, m = a.shape
    n = b.shape[1]

    def body(a_ref, b_ref, o_ref):
        o_ref[...] = _dg(a_ref[...], b_ref[...], TN)

    return pl.pallas_call(
        body, grid=(m // tm, n // tn),
        in_specs=[pl.BlockSpec((k, tm), lambda i, j: (0, i)), pl.BlockSpec((k, tn), lambda i, j: (0, j))],
        out_specs=pl.BlockSpec((tm, tn), lambda i, j: (i, j)),
        out_shape=jax.ShapeDtypeStruct((m, n), F32), name=name,
        compiler_params=_params(("parallel", "parallel")),
    )(a, b)


def _mm_split(a, b, tm, split, name):
    m, k = a.shape
    n = b.shape[1]

    def body(a_ref, b_ref, lo_ref, hi_ref):
        out = _dg(a_ref[...], b_ref[...], NN)
        lo_ref[...] = out[:, :split]
        hi_ref[...] = out[:, split:]

    return pl.pallas_call(
        body, grid=(m // tm,),
        in_specs=[pl.BlockSpec((tm, k), lambda i: (i, 0)), pl.BlockSpec((k, n), lambda i: (0, 0))],
        out_specs=[pl.BlockSpec((tm, split), lambda i: (i, 0)), pl.BlockSpec((tm, n - split), lambda i: (i, 0))],
        out_shape=[jax.ShapeDtypeStruct((m, split), F32), jax.ShapeDtypeStruct((m, n - split), F32)], name=name,
        compiler_params=_params(("parallel",)),
    )(a, b)


def _mm_nn_cols(a, b4, tm, name):
    m, k = a.shape
    s, _, n = b4.shape

    def body(a_ref, b_ref, o_ref):
        o_ref[...] = _dg(a_ref[...], b_ref[...], NN)

    return pl.pallas_call(
        body, grid=(m // tm, s),
        in_specs=[pl.BlockSpec((tm, k), lambda i, j: (i, 0)), pl.BlockSpec((None, k, n), lambda i, j: (j, 0, 0))],
        out_specs=pl.BlockSpec((tm, n), lambda i, j: (i, j)),
        out_shape=jax.ShapeDtypeStruct((m, s * n), F32), name=name,
        compiler_params=_params(("parallel", "parallel")),
    )(a, b4)


def _mm_nt_cols(a2, b4, tm, name):
    h, m, wide = a2.shape
    s, k, n = b4.shape
    per = s // h

    def body(a_ref, b_ref, o_ref):
        acc = None
        for j in range(s):
            part = _dg(a_ref[j // per, :, (j % per) * n:(j % per + 1) * n], b_ref[j], NT)
            acc = part if acc is None else acc + part
        o_ref[...] = acc

    return pl.pallas_call(
        body, grid=(m // tm,),
        in_specs=[pl.BlockSpec((h, tm, wide), lambda i: (0, i, 0)), pl.BlockSpec((s, k, n), lambda i: (0, 0, 0))],
        out_specs=pl.BlockSpec((tm, k), lambda i: (i, 0)),
        out_shape=jax.ShapeDtypeStruct((m, k), F32), name=name,
        compiler_params=_params(("parallel",)),
    )(a2, b4)


def _mm_tn_cols(a, b2, s, tm, name):
    k, m = a.shape
    h, _, wide = b2.shape
    per = s // h
    n = wide // per

    def body(a_ref, b_ref, o_ref):
        o_ref[...] = _dg(a_ref[...], b_ref[...], TN)

    return pl.pallas_call(
        body, grid=(s, m // tm),
        in_specs=[pl.BlockSpec((k, tm), lambda j, i: (0, i)),
                  pl.BlockSpec((None, k, n), lambda j, i: (j // per, 0, j % per))],
        out_specs=pl.BlockSpec((None, tm, n), lambda j, i: (j, i, 0)),
        out_shape=jax.ShapeDtypeStruct((s, m, n), F32), name=name,
        compiler_params=_params(("parallel", "parallel")),
    )(a, b2)


TM_EW = 256


def _rms_fwd(x, g, name):
    l, d = x.shape

    def body(x_ref, g_ref, h_ref):
        xv = x_ref[...]
        r = lax.rsqrt(jnp.mean(xv * xv, axis=-1, keepdims=True) + EPS)
        h_ref[...] = (xv * r * g_ref[...]).astype(BF16)

    return pl.pallas_call(
        body, grid=(l // TM_EW,),
        in_specs=[pl.BlockSpec((TM_EW, d), lambda i: (i, 0)), pl.BlockSpec((1, d), lambda i: (0, 0))],
        out_specs=pl.BlockSpec((TM_EW, d), lambda i: (i, 0)),
        out_shape=jax.ShapeDtypeStruct((l, d), BF16), name=name,
        compiler_params=_params(("parallel",)),
    )(x, g)


def _rms_bwd_vals(xv, gv, dy):
    r = lax.rsqrt(jnp.mean(xv * xv, axis=-1, keepdims=True) + EPS)
    xh = xv * r
    dxh = dy * gv
    dx = r * (dxh - xh * jnp.mean(dxh * xh, axis=-1, keepdims=True))
    return dx, dy * xh


def _rms_bwd(x, g, dy, res, name):
    l, d = x.shape

    def body(x_ref, g_ref, dy_ref, res_ref, dx_ref, dxb_ref, dg_ref):
        dx, dgr = _rms_bwd_vals(x_ref[...], g_ref[...], dy_ref[...])
        dx = dx + res_ref[...]
        dx_ref[...] = dx
        dxb_ref[...] = dx.astype(BF16)

        @pl.when(pl.program_id(0) == 0)
        def _():
            dg_ref[...] = jnp.zeros_like(dg_ref)

        dg_ref[...] += jnp.sum(dgr, axis=0, keepdims=True)

    row = pl.BlockSpec((TM_EW, d), lambda i: (i, 0))
    vec = pl.BlockSpec((1, d), lambda i: (0, 0))
    return pl.pallas_call(
        body, grid=(l // TM_EW,), in_specs=[row, vec, row, row], out_specs=[row, row, vec],
        out_shape=[jax.ShapeDtypeStruct((l, d), F32), jax.ShapeDtypeStruct((l, d), BF16),
                   jax.ShapeDtypeStruct((1, d), F32)],
        name=name, compiler_params=_params(("arbitrary",)),
    )(x, g, dy, res)


def _final_loss(x2, g, target):
    l, d = x2.shape

    def body(x_ref, g_ref, t_ref, loss_ref, dx_ref, dxb_ref, dg_ref):
        xv = x_ref[...]
        gv = g_ref[...]
        r = lax.rsqrt(jnp.mean(xv * xv, axis=-1, keepdims=True) + EPS)
        xh = xv * r
        e = xh * gv - t_ref[...]
        part = jnp.sum(jnp.sum(e * e, axis=1, keepdims=True), axis=0, keepdims=True) * (0.5 / d)
        dy = e * (1.0 / d)
        dxh = dy * gv
        dx = r * (dxh - xh * jnp.mean(dxh * xh, axis=-1, keepdims=True))
        dx_ref[...] = dx
        dxb_ref[...] = dx.astype(BF16)

        @pl.when(pl.program_id(0) == 0)
        def _():
            dg_ref[...] = jnp.zeros_like(dg_ref)
            loss_ref[...] = jnp.zeros_like(loss_ref)

        dg_ref[...] += jnp.sum(dy * xh, axis=0, keepdims=True)
        loss_ref[...] += part

    row = pl.BlockSpec((TM_EW, d), lambda i: (i, 0))
    vec = pl.BlockSpec((1, d), lambda i: (0, 0))
    one = pl.BlockSpec((1, 1), lambda i: (0, 0))
    return pl.pallas_call(
        body, grid=(l // TM_EW,), in_specs=[row, vec, row], out_specs=[one, row, row, vec],
        out_shape=[jax.ShapeDtypeStruct((1, 1), F32), jax.ShapeDtypeStruct((l, d), F32),
                   jax.ShapeDtypeStruct((l, d), BF16), jax.ShapeDtypeStruct((1, d), F32)],
        name="final_loss", compiler_params=_params(("arbitrary",)),
    )(x2, g, target)


def _mix_fwd(attn, ys, g_attn, g_ssm):
    l, w = attn.shape

    def body(a_ref, y_ref, ga_ref, gs_ref, o_ref):
        for src, gr, off in ((a_ref, ga_ref, 0), (y_ref, gs_ref, w)):
            xv = src[...]
            r = lax.rsqrt(jnp.mean(xv * xv, axis=-1, keepdims=True) + EPS)
            o_ref[:, off:off + w] = (xv * r * gr[...]).astype(BF16)

    row = pl.BlockSpec((TM_EW, w), lambda i: (i, 0))
    vec = pl.BlockSpec((1, w), lambda i: (0, 0))
    return pl.pallas_call(
        body, grid=(l // TM_EW,), in_specs=[row, row, vec, vec],
        out_specs=pl.BlockSpec((TM_EW, 2 * w), lambda i: (i, 0)),
        out_shape=jax.ShapeDtypeStruct((l, 2 * w), BF16), name="mix_fwd",
        compiler_params=_params(("parallel",)),
    )(attn, ys, g_attn, g_ssm)


def _mix_bwd(attn, ys, g_attn, g_ssm, dmixed):
    l, w = attn.shape

    def body(a_ref, y_ref, ga_ref, gs_ref, dm_ref, da_ref, dy_ref, dga_ref, dgs_ref):
        @pl.when(pl.program_id(0) == 0)
        def _():
            dga_ref[...] = jnp.zeros_like(dga_ref)
            dgs_ref[...] = jnp.zeros_like(dgs_ref)

        for src, gr, off, dst, dgr in ((a_ref, ga_ref, 0, da_ref, dga_ref), (y_ref, gs_ref, w, dy_ref, dgs_ref)):
            dx, dg_rows = _rms_bwd_vals(src[...], gr[...], dm_ref[:, off:off + w])
            dst[...] = dx
            dgr[...] += jnp.sum(dg_rows, axis=0, keepdims=True)

    row = pl.BlockSpec((TM_EW, w), lambda i: (i, 0))
    vec = pl.BlockSpec((1, w), lambda i: (0, 0))
    return pl.pallas_call(
        body, grid=(l // TM_EW,),
        in_specs=[row, row, vec, vec, pl.BlockSpec((TM_EW, 2 * w), lambda i: (i, 0))],
        out_specs=[row, row, vec, vec],
        out_shape=[jax.ShapeDtypeStruct((l, w), F32), jax.ShapeDtypeStruct((l, w), F32),
                   jax.ShapeDtypeStruct((1, w), F32), jax.ShapeDtypeStruct((1, w), F32)],
        name="mix_bwd", compiler_params=_params(("arbitrary",)),
    )(attn, ys, g_attn, g_ssm, dmixed)


def _rope_tables(l):
    half = ROPE_DIM // 2
    inv_freq = jnp.power(ROPE_THETA, -jnp.arange(half, dtype=F32) / half)
    ang = jnp.arange(l, dtype=F32)[:, None] * inv_freq[None, :]
    cos, sin = jnp.cos(ang), jnp.sin(ang)
    ones = jnp.ones((l, HEAD_DIM - ROPE_DIM), F32)
    zeros = jnp.zeros((l, HEAD_DIM - ROPE_DIM), F32)
    zh = jnp.zeros((l, half), F32)
    c = jnp.concatenate([cos, cos, ones], axis=1)
    s_lo = jnp.concatenate([-sin, zh, zeros], axis=1)
    s_hi = jnp.concatenate([zh, sin, zeros], axis=1)
    return tuple(jnp.tile(t, (1, LANES // HEAD_DIM)) for t in (c, s_lo, s_hi))


def _rope_fwd(proj, tabs):
    l = proj.shape[0]
    nq = ATTN_WIDTH // LANES

    def body(p_ref, c_ref, lo_ref, hi_ref, o_ref):
        c, lo, hi = c_ref[...], lo_ref[...], hi_ref[...]
        for blk in range(nq + 1):
            t = p_ref[:, blk * LANES:(blk + 1) * LANES]
            rot = t * c + pltpu.roll(t, LANES - 8, 1) * lo + pltpu.roll(t, 8, 1) * hi
            o_ref[:, blk * LANES:(blk + 1) * LANES] = rot.astype(BF16)
        o_ref[:, (nq + 1) * LANES:] = p_ref[:, (nq + 1) * LANES:].astype(BF16)

    tab = pl.BlockSpec((TM_EW, LANES), lambda i: (i, 0))
    return pl.pallas_call(
        body, grid=(l // TM_EW,),
        in_specs=[pl.BlockSpec((TM_EW, QKV_WIDTH), lambda i: (i, 0)), tab, tab, tab],
        out_specs=pl.BlockSpec((TM_EW, QKV_WIDTH), lambda i: (i, 0)),
        out_shape=jax.ShapeDtypeStruct((l, QKV_WIDTH), BF16), name="rope_fwd",
        compiler_params=_params(("parallel",)),
    )(proj, *tabs)


def _rope_bwd(dqkv, du_ssm, dpre, d_skip, tabs):
    l = dqkv.shape[0]
    nq = ATTN_WIDTH // LANES

    def body(d_ref, du_ref, dpre_ref, ds_ref, c_ref, lo_ref, hi_ref, o_ref):
        c, lo, hi = c_ref[...], lo_ref[...], hi_ref[...]
        for blk in range(nq + 1):
            t = d_ref[:, blk * LANES:(blk + 1) * LANES]
            g = t * c + pltpu.roll(t * lo, 8, 1) + pltpu.roll(t * hi, LANES - 8, 1)
            o_ref[:, blk * LANES:(blk + 1) * LANES] = g.astype(BF16)
        o_ref[:, (nq + 1) * LANES:QKV_WIDTH] = d_ref[:, (nq + 1) * LANES:].astype(BF16)
        o_ref[:, QKV_WIDTH:] = (du_ref[...] + dpre_ref[...] * ds_ref[...]).astype(BF16)

    tab = pl.BlockSpec((TM_EW, LANES), lambda i: (i, 0))
    wide = pl.BlockSpec((TM_EW, SSM_WIDTH), lambda i: (i, 0))
    return pl.pallas_call(
        body, grid=(l // TM_EW,),
        in_specs=[pl.BlockSpec((TM_EW, QKV_WIDTH), lambda i: (i, 0)), wide, wide,
                  pl.BlockSpec((1, SSM_WIDTH), lambda i: (0, 0)), tab, tab, tab],
        out_specs=pl.BlockSpec((TM_EW, IN_WIDTH), lambda i: (i, 0)),
        out_shape=jax.ShapeDtypeStruct((l, IN_WIDTH), BF16), name="rope_bwd",
        compiler_params=_params(("parallel",)),
    )(dqkv, du_ssm, dpre, d_skip, *tabs)


_Q_COLS = ATTN_WIDTH // LANES
_SCALE = HEAD_DIM ** -0.5
_NEG = -1e30


def _window_specs(nb, width, col):
    return [
        pl.BlockSpec((BLOCK, width), lambda n: (jnp.maximum(n - 1, 0), col)),
        pl.BlockSpec((BLOCK, width), lambda n: (n, col)),
        pl.BlockSpec((BLOCK, width), lambda n: (jnp.minimum(n + 1, nb - 1), col)),
    ]


def _stacked_sink(sink_ref, heads):
    rid = lax.broadcasted_iota(jnp.int32, (len(heads) * BLOCK, 1), 0)
    sk = jnp.full(rid.shape, sink_ref[0, heads[-1]], F32)
    for g in range(len(heads) - 2, -1, -1):
        sk = jnp.where(rid < (g + 1) * BLOCK, sink_ref[0, heads[g]], sk)
    return sk


def _attn_fwd(qkv, sink):
    l = qkv.shape[0]
    nb = l // BLOCK
    grp = N_Q_HEADS // N_KV_HEADS

    def body(sink_ref, q_ref, k0, k1, k2, v0, v1, v2, o_ref, lse_ref):
        n = pl.program_id(0)
        q = q_ref[...]
        kw = jnp.concatenate([k0[...], k1[...], k2[...]], axis=0)
        vw = jnp.concatenate([v0[...], v1[...], v2[...]], axis=0)
        row = lax.broadcasted_iota(jnp.int32, (grp * BLOCK, 3 * BLOCK), 0)
        col = lax.broadcasted_iota(jnp.int32, (grp * BLOCK, 3 * BLOCK), 1)
        valid = jnp.abs(col - BLOCK - (row & (BLOCK - 1))) <= WINDOW
        valid &= jnp.logical_not((n == 0) & (col < BLOCK))
        valid &= jnp.logical_not((n == nb - 1) & (col >= 2 * BLOCK))
        for hk in range(N_KV_HEADS):
            heads = range(hk * grp, (hk + 1) * grp)
            qs = jnp.concatenate([q[:, h * HEAD_DIM:(h + 1) * HEAD_DIM] for h in heads], axis=0)
            kh = kw[:, hk * HEAD_DIM:(hk + 1) * HEAD_DIM]
            vh = vw[:, hk * HEAD_DIM:(hk + 1) * HEAD_DIM]
            s = jnp.where(valid, _dg(qs, kh, NT) * _SCALE, _NEG)
            sk = _stacked_sink(sink_ref, heads)
            m = jnp.maximum(jnp.max(s, axis=1, keepdims=True), sk)
            p = jnp.exp(s - m)
            denom = jnp.sum(p, axis=1, keepdims=True) + jnp.exp(sk - m)
            o = _dg((p / denom).astype(BF16), vh, NN)
            lse = m + jnp.log(denom)
            for g, h in enumerate(heads):
                o_ref[:, h * HEAD_DIM:(h + 1) * HEAD_DIM] = o[g * BLOCK:(g + 1) * BLOCK]
                lse_ref[:, h:h + 1] = lse[g * BLOCK:(g + 1) * BLOCK]

    return pl.pallas_call(
        body, grid=(nb,),
        in_specs=[pl.BlockSpec(memory_space=pltpu.SMEM),
                  pl.BlockSpec((BLOCK, ATTN_WIDTH), lambda n: (n, 0))]
        + _window_specs(nb, KV_WIDTH, _Q_COLS) + _window_specs(nb, KV_WIDTH, _Q_COLS + 1),
        out_specs=[pl.BlockSpec((BLOCK, ATTN_WIDTH), lambda n: (n, 0)),
                   pl.BlockSpec((BLOCK, N_Q_HEADS), lambda n: (n, 0))],
        out_shape=[jax.ShapeDtypeStruct((l, ATTN_WIDTH), F32), jax.ShapeDtypeStruct((l, N_Q_HEADS), F32)],
        name="attn_fwd", compiler_params=_params(("parallel",)),
    )(sink, qkv, qkv, qkv, qkv, qkv, qkv, qkv)


def _attn_bwd(qkv, attn, dattn, lse, sink):
    l = qkv.shape[0]
    nb = l // BLOCK
    grp = N_Q_HEADS // N_KV_HEADS

    def body(sink_ref, q0, q1, q2, k0, k1, k2, v0, v1, v2, o0, o1, o2, d0, d1, d2,
             l0, l1, l2, dqkv_ref, dsink_ref):
        n = pl.program_id(0)
        first, last = n == 0, n == nb - 1

        @pl.when(first)
        def _():
            dsink_ref[...] = jnp.zeros_like(dsink_ref)

        cat = lambda a, b, c: jnp.concatenate([a[...], b[...], c[...]], axis=0)
        qw, kw, vw = cat(q0, q1, q2), cat(k0, k1, k2), cat(v0, v1, v2)
        dow = cat(d0, d1, d2)
        prodw = cat(o0, o1, o2) * dow
        lsew = cat(l0, l1, l2)
        dob = dow.astype(BF16)
        win = 3 * BLOCK
        mid = slice(BLOCK, 2 * BLOCK)

        row = lax.broadcasted_iota(jnp.int32, (grp * BLOCK, win), 0)
        col = lax.broadcasted_iota(jnp.int32, (grp * BLOCK, win), 1)
        valid_q = jnp.abs(col - BLOCK - (row & (BLOCK - 1))) <= WINDOW
        valid_q &= jnp.logical_not(first & (col < BLOCK))
        valid_q &= jnp.logical_not(last & (col >= 2 * BLOCK))
        rowk = lax.broadcasted_iota(jnp.int32, (grp * win, BLOCK), 0)
        colk = lax.broadcasted_iota(jnp.int32, (grp * win, BLOCK), 1)
        for g in range(1, grp):
            rowk = jnp.where(rowk >= win, rowk - win, rowk)
        valid_k = jnp.abs(colk + BLOCK - rowk) <= WINDOW
        valid_k &= jnp.logical_not(first & (rowk < BLOCK))
        valid_k &= jnp.logical_not(last & (rowk >= 2 * BLOCK))

        dsink_parts = []
        for hk in range(N_KV_HEADS):
            heads = range(hk * grp, (hk + 1) * grp)
            ksl = slice(hk * HEAD_DIM, (hk + 1) * HEAD_DIM)
            hsl = [slice(h * HEAD_DIM, (h + 1) * HEAD_DIM) for h in heads]
            stack = lambda parts: jnp.concatenate(parts, axis=0)
            qws = stack([qw[:, s_] for s_ in hsl])
            dows = stack([dob[:, s_] for s_ in hsl])
            deltaws = stack([jnp.sum(prodw[:, s_], axis=1, keepdims=True) for s_ in hsl])
            lsews = stack([lsew[:, h:h + 1] for h in heads])
            of_block = lambda t: stack([t[g * win + BLOCK:g * win + 2 * BLOCK] for g in range(grp)])
            qs, dos, deltas, lses = of_block(qws), of_block(dows), of_block(deltaws), of_block(lsews)
            kh, vh = kw[:, ksl], vw[:, ksl]
            s = jnp.where(valid_q, _dg(qs, kh, NT) * _SCALE, _NEG)
            p = jnp.exp(s - lses)
            dp = _dg(dos, vh, NT)
            ds = (p * (dp - deltas) * _SCALE).astype(BF16)
            dq = _dg(ds, kh, NN)
            sink_rows = jnp.exp(_stacked_sink(sink_ref, heads) - lses) * deltas
            for g, h in enumerate(heads):
                dqkv_ref[:, hsl[g]] = dq[g * BLOCK:(g + 1) * BLOCK]
                dsink_parts.append(jnp.sum(sink_rows[g * BLOCK:(g + 1) * BLOCK], axis=0, keepdims=True))
            s2 = jnp.where(valid_k, _dg(qws, kh[mid], NT) * _SCALE, _NEG)
            p2 = jnp.exp(s2 - lsews)
            dv = _dg(p2.astype(BF16), dows, TN)
            dp2 = _dg(dows, vh[mid], NT)
            ds2 = (p2 * (dp2 - deltaws) * _SCALE).astype(BF16)
            dk = _dg(ds2, qws, TN)
            dqkv_ref[:, ATTN_WIDTH + hk * HEAD_DIM:ATTN_WIDTH + (hk + 1) * HEAD_DIM] = dk
            dqkv_ref[:, ATTN_WIDTH + KV_WIDTH + hk * HEAD_DIM:ATTN_WIDTH + KV_WIDTH + (hk + 1) * HEAD_DIM] = dv
        dsink_ref[...] -= jnp.concatenate(dsink_parts, axis=1)

    return pl.pallas_call(
        body, grid=(nb,),
        in_specs=[pl.BlockSpec(memory_space=pltpu.SMEM)]
        + _window_specs(nb, ATTN_WIDTH, 0)
        + _window_specs(nb, KV_WIDTH, _Q_COLS) + _window_specs(nb, KV_WIDTH, _Q_COLS + 1)
        + _window_specs(nb, ATTN_WIDTH, 0) + _window_specs(nb, ATTN_WIDTH, 0)
        + _window_specs(nb, N_Q_HEADS, 0),
        out_specs=[pl.BlockSpec((BLOCK, QKV_WIDTH), lambda n: (n, 0)),
                   pl.BlockSpec((1, N_Q_HEADS), lambda n: (0, 0))],
        out_shape=[jax.ShapeDtypeStruct((l, QKV_WIDTH), F32), jax.ShapeDtypeStruct((1, N_Q_HEADS), F32)],
        name="attn_bwd", compiler_params=_params(("arbitrary",)),
    )(sink, qkv, qkv, qkv, qkv, qkv, qkv, qkv, qkv, qkv, attn, attn, attn,
      dattn, dattn, dattn, lse, lse, lse)


def _ssm_disc(a_re, a_im, log_step, b_re, b_im):
    step = jnp.exp(log_step)[..., None]
    mag = jnp.exp(a_re * step)
    lb_re, lb_im = mag * jnp.cos(a_im * step), mag * jnp.sin(a_im * step)
    nr, ni = lb_re - 1.0, lb_im
    den = a_re * a_re + a_im * a_im
    f_re = ((nr * a_re + ni * a_im) / den)[..., None]
    f_im = ((ni * a_re - nr * a_im) / den)[..., None]
    return lb_re, lb_im, f_re * b_re - f_im * b_im, f_re * b_im + f_im * b_re


def _ssm_pack(lb_re, lb_im, bb_re, bb_im, c_re, c_im):
    eye = jnp.eye(SSM_CH // SSM_GROUP, dtype=F32)
    ng = SSM_CH // SSM_GROUP

    def diag_b(bb):
        t = bb.reshape(2, SSM_CB, ng, SSM_STATE, SSM_GROUP)
        return jnp.einsum('dkgpc,gh->dkgchp', t, eye).reshape(2, SSM_CB, SSM_CH, SSM_ST)

    def diag_c(cc):
        t = cc.reshape(2, SSM_CB, ng, SSM_GROUP, SSM_STATE)
        return jnp.einsum('dkgcp,gh->dkhpgc', t, eye).reshape(2, SSM_CB, SSM_ST, SSM_CH)

    bcat = jnp.concatenate([diag_b(bb_re), diag_b(bb_im)], axis=-1)
    ccat = jnp.concatenate([diag_c(c_re), -diag_c(c_im)], axis=-2)
    lam_re = lb_re.reshape(2, SSM_CB, 1, SSM_ST)
    lam_im = lb_im.reshape(2, SSM_CB, 1, SSM_ST)
    return bcat, ccat, lam_re, lam_im


def _ssm_unpack(dbcat, dccat, dlam_re, dlam_im):
    ng = SSM_CH // SSM_GROUP
    eye = jnp.eye(ng, dtype=F32)

    def undiag_b(t):
        t = t.reshape(2, SSM_CB, ng, SSM_GROUP, ng, SSM_STATE)
        return jnp.einsum('dkgchp,gh->dkgpc', t, eye).reshape(2, N_SSM_GROUPS, SSM_STATE, SSM_GROUP)

    def undiag_c(t):
        t = t.reshape(2, SSM_CB, ng, SSM_STATE, ng, SSM_GROUP)
        return jnp.einsum('dkhpgc,gh->dkgcp', t, eye).reshape(2, N_SSM_GROUPS, SSM_GROUP, SSM_STATE)

    dbb_re, dbb_im = undiag_b(dbcat[..., :SSM_ST]), undiag_b(dbcat[..., SSM_ST:])
    dc_re, dc_im = undiag_c(dccat[:, :, :SSM_ST]), -undiag_c(dccat[:, :, SSM_ST:])
    shape = (2, N_SSM_GROUPS, SSM_STATE)
    return dlam_re.reshape(shape), dlam_im.reshape(shape), dbb_re, dbb_im, dc_re, dc_im


def _to_segments(t):
    l, w = t.shape
    return t.reshape(N_SEG, l // N_SEG, w).transpose(1, 0, 2).reshape(l, w)


def _from_segments(t):
    l, w = t.shape
    return t.reshape(l // N_SEG, N_SEG, w).transpose(1, 0, 2).reshape(l, w)


SCAN_UNROLL = 4


def _cfma(ar, ai, xr, xi, br, bi):
    return ar * xr - ai * xi + br, ar * xi + ai * xr + bi


def _scan_segments(xs_ref, ar, ai, rev, nj, prev_ref=None, before_sums=None):
    shape = (N_SEG, SSM_ST)
    ar = jnp.broadcast_to(ar, shape)
    ai = jnp.broadcast_to(ai, shape)
    zero = jnp.zeros(shape, F32)
    re_cols, im_cols = pl.ds(0, SSM_ST), pl.ds(SSM_ST, SSM_ST)

    def rows_of(jj):
        j = jnp.where(rev, nj - 1 - jj, jj)
        return j, pl.ds(pl.multiple_of(j * N_SEG, N_SEG), N_SEG)

    def steps(step, init, last_step=None):
        def outer(o, carry):
            for k in range(SCAN_UNROLL):
                carry = step(o * SCAN_UNROLL + k, carry)
            return carry

        carry = lax.fori_loop(0, nj // SCAN_UNROLL - 1, outer, init)
        for jj in range(nj - SCAN_UNROLL, nj):
            carry = (last_step if last_step is not None and jj == nj - 1 else step)(jj, carry)
        return carry

    def pass1(jj, carry):
        _, rows = rows_of(jj)
        return _cfma(ar, ai, carry[0], carry[1], xs_ref[rows, re_cols], xs_ref[rows, im_cols])

    end_r, end_i = steps(pass1, (zero, zero))

    pr, pi = ar, ai
    for _ in range(int(math.log2(nj))):
        pr, pi = pr * pr - pi * pi, 2.0 * pr * pi
    seg = lax.broadcasted_iota(jnp.int32, shape, 0)

    def chain(shift, keep):
        ir, ii = zero, zero
        for _ in range(N_SEG - 1):
            tr, ti = _cfma(pr, pi, ir, ii, end_r, end_i)
            ir = jnp.where(keep, pltpu.roll(tr, shift, 0), 0.0)
            ii = jnp.where(keep, pltpu.roll(ti, shift, 0), 0.0)
        return ir, ii

    up_r, up_i = chain(1, seg >= 1)
    dn_r, dn_i = chain(N_SEG - 1, seg <= N_SEG - 2)
    init_r, init_i = jnp.where(rev, dn_r, up_r), jnp.where(rev, dn_i, up_i)

    def pass2(jj, carry):
        j, rows = rows_of(jj)
        nr, ni = _cfma(ar, ai, carry[0], carry[1], xs_ref[rows, re_cols], xs_ref[rows, im_cols])
        xs_ref[rows, re_cols] = nr
        xs_ref[rows, im_cols] = ni
        return (j, nr, ni) + tuple(carry[2:])

    def pass2_plain(jj, carry):
        return pass2(jj, carry)[1:]

    def pass2_sums(jj, carry):
        j, nr, ni, acc_r, acc_i = pass2(jj, carry)
        jp = jnp.where(rev, j - 1, j + 1)
        prow = pl.ds(pl.multiple_of(jp * N_SEG, N_SEG), N_SEG)
        xr, xi = prev_ref[prow, re_cols], prev_ref[prow, im_cols]
        return nr, ni, acc_r + (nr * xr + ni * xi), acc_i + (ni * xr - nr * xi)

    if prev_ref is None:
        steps(pass2_plain, (init_r, init_i))
        return init_r, init_i, None, None
    if before_sums is not None:
        before_sums()
    _, _, acc_r, acc_i = steps(pass2_sums, (init_r, init_i, zero, zero), last_step=pass2_plain)
    return init_r, init_i, acc_r, acc_i


SSM_RC = 256


def _ssm_specs(l):
    act = pl.BlockSpec((l, SSM_CH), lambda k, d: (0, k))
    bmat = pl.BlockSpec((None, None, SSM_CH, 2 * SSM_ST), lambda k, d: (d, k, 0, 0))
    cmat = pl.BlockSpec((None, None, 2 * SSM_ST, SSM_CH), lambda k, d: (d, k, 0, 0))
    lam = pl.BlockSpec((None, None, 1, SSM_ST), lambda k, d: (d, k, 0, 0))
    return act, bmat, cmat, lam


def _ssm_fwd(u_seg, bcat, ccat, lam_re, lam_im):
    l = u_seg.shape[0]
    nj = l // N_SEG

    def body(u_ref, b_ref, c_ref, lr_ref, li_ref, y_ref, keep_ref, xs_ref, keep_sem):
        k, d = pl.program_id(0), pl.program_id(1)

        def bu_chunk(i, _):
            rows = pl.ds(pl.multiple_of(i * SSM_RC, SSM_RC), SSM_RC)
            xs_ref[rows, :] = _dg(u_ref[rows, :], b_ref[...], NN)
            return 0

        lax.fori_loop(0, l // SSM_RC, bu_chunk, 0)
        _scan_segments(xs_ref, lr_ref[...], li_ref[...], d == 1, nj)
        keep = pltpu.make_async_copy(xs_ref, keep_ref.at[d, k], keep_sem)
        keep.start()

        def y_chunk(i, _):
            rows = pl.ds(pl.multiple_of(i * SSM_RC, SSM_RC), SSM_RC)
            yv = _dg(xs_ref[rows, :].astype(BF16), c_ref[...], NN)

            @pl.when(d == 0)
            def _():
                y_ref[rows, :] = yv

            @pl.when(d == 1)
            def _():
                y_ref[rows, :] += yv

            return 0

        lax.fori_loop(0, l // SSM_RC, y_chunk, 0)
        keep.wait()

    act, bmat, cmat, lam = _ssm_specs(l)
    return pl.pallas_call(
        body, grid=(SSM_CB, 2), in_specs=[act, bmat, cmat, lam, lam], out_specs=[act, ANY],
        out_shape=[jax.ShapeDtypeStruct((l, SSM_WIDTH), F32),
                   jax.ShapeDtypeStruct((2, SSM_CB, l, 2 * SSM_ST), F32)],
        scratch_shapes=[pltpu.VMEM((l, 2 * SSM_ST), F32), pltpu.SemaphoreType.DMA],
        name="ssm_fwd", compiler_params=_params(("parallel", "arbitrary"), vmem_mb=56),
    )(u_seg, bcat.astype(BF16), ccat.astype(BF16), lam_re, lam_im)


def _ssm_bwd(u_seg, dy_seg, states, bcat, ccat, lam_re, lam_im):
    l = u_seg.shape[0]
    nj = l // N_SEG

    rc2 = min(l, 2 * SSM_RC)

    def body(u_ref, dy_ref, keep_ref, b_ref, c_ref, lr_ref, li_ref,
             du_ref, db_ref, dc_ref, dlr_ref, dli_ref, xs_ref, gs_ref, keep_sem):
        k, d = pl.program_id(0), pl.program_id(1)
        rev = d == 1
        ar, ai = lr_ref[...], li_ref[...]
        fetch = pltpu.make_async_copy(keep_ref.at[d, k], xs_ref, keep_sem)
        fetch.start()

        def chunk1(i, _):
            rows = pl.ds(pl.multiple_of(i * SSM_RC, SSM_RC), SSM_RC)
            gs_ref[rows, :] = _dg(dy_ref[rows, :], c_ref[...], NT)
            return 0

        lax.fori_loop(0, l // SSM_RC, chunk1, 0)
        _, _, acc_r, acc_i = _scan_segments(gs_ref, ar, -ai, jnp.logical_not(rev), nj, prev_ref=xs_ref,
                                            before_sums=fetch.wait)
        seg = lax.broadcasted_iota(jnp.int32, (N_SEG, SSM_ST), 0)
        jb = jnp.where(rev, nj - 1, 0)
        brow = pl.ds(pl.multiple_of(jb * N_SEG, N_SEG), N_SEG)
        erow = pl.ds(pl.multiple_of((nj - 1 - jb) * N_SEG, N_SEG), N_SEG)
        re_cols, im_cols = pl.ds(0, SSM_ST), pl.ds(SSM_ST, SSM_ST)

        def before(t):
            up = jnp.where(seg >= 1, pltpu.roll(t, 1, 0), 0.0)
            down = jnp.where(seg <= N_SEG - 2, pltpu.roll(t, N_SEG - 1, 0), 0.0)
            return jnp.where(rev, down, up)

        init_r, init_i = before(xs_ref[erow, re_cols]), before(xs_ref[erow, im_cols])
        gr, gi = gs_ref[brow, re_cols], gs_ref[brow, im_cols]
        acc_r = acc_r + gr * init_r + gi * init_i
        acc_i = acc_i + gi * init_r - gr * init_i
        dlr_ref[...] = jnp.sum(acc_r, axis=0, keepdims=True)
        dli_ref[...] = jnp.sum(acc_i, axis=0, keepdims=True)

        db_ref[...] = jnp.zeros_like(db_ref)
        dc_ref[...] = jnp.zeros_like(dc_ref)

        def chunk2(i, _):
            rows = pl.ds(pl.multiple_of(i * rc2, rc2), rc2)
            g = gs_ref[rows, :].astype(BF16)
            dc_ref[...] += _dg(xs_ref[rows, :].astype(BF16), dy_ref[rows, :], TN)
            db_ref[...] += _dg(u_ref[rows, :], g, TN)
            duv = _dg(g, b_ref[...], NT)

            @pl.when(d == 0)
            def _():
                du_ref[rows, :] = duv

            @pl.when(d == 1)
            def _():
                du_ref[rows, :] += duv

            return 0

        lax.fori_loop(0, l // rc2, chunk2, 0)

    act, bmat, cmat, lam = _ssm_specs(l)
    return pl.pallas_call(
        body, grid=(SSM_CB, 2), in_specs=[act, act, ANY, bmat, cmat, lam, lam],
        out_specs=[act, bmat, cmat, lam, lam],
        out_shape=[jax.ShapeDtypeStruct((l, SSM_WIDTH), F32),
                   jax.ShapeDtypeStruct(bcat.shape, F32), jax.ShapeDtypeStruct(ccat.shape, F32),
                   jax.ShapeDtypeStruct(lam_re.shape, F32), jax.ShapeDtypeStruct(lam_im.shape, F32)],
        scratch_shapes=[pltpu.VMEM((l, 2 * SSM_ST), F32), pltpu.VMEM((l, 2 * SSM_ST), F32),
                        pltpu.SemaphoreType.DMA],
        name="ssm_bwd", compiler_params=_params(("parallel", "arbitrary"), vmem_mb=56),
    )(u_seg, dy_seg, states, bcat.astype(BF16), ccat.astype(BF16), lam_re, lam_im)


def _glu_fwd(y_ssm, u, d_skip, w_glu):
    l, w = u.shape

    def body(y_ref, u_ref, d_ref, w_ref, pre_ref, s_ref, ys_ref):
        pre = y_ref[...] + d_ref[...] * u_ref[...]
        z = _gelu(pre)
        s = _dg(z.astype(BF16), w_ref[...], NN)
        pre_ref[...] = pre
        s_ref[...] = s
        ys_ref[...] = z * _sigmoid(s)

    row = pl.BlockSpec((TM_EW, w), lambda i: (i, 0))
    out = jax.ShapeDtypeStruct((l, w), F32)
    return pl.pallas_call(
        body, grid=(l // TM_EW,),
        in_specs=[row, row, pl.BlockSpec((1, w), lambda i: (0, 0)), pl.BlockSpec((w, w), lambda i: (0, 0))],
        out_specs=[row, row, row], out_shape=[out, out, out], name="glu_fwd",
        compiler_params=_params(("parallel",)),
    )(y_ssm, u, d_skip, w_glu)


def _glu_bwd(pre, s, dys, u, d_skip, w_glu):
    l, w = u.shape

    def body(pre_ref, s_ref, dys_ref, u_ref, d_ref, w_ref, dpre_ref, z_ref, ds_ref, dd_ref):
        pre, dys = pre_ref[...], dys_ref[...]
        z = _gelu(pre)
        sig = _sigmoid(s_ref[...])
        ds = (dys * z * sig * (1.0 - sig)).astype(BF16)
        dz = dys * sig + _dg(ds, w_ref[...], NT)
        dpre = dz * _gelu_grad(pre)
        dpre_ref[...] = dpre
        z_ref[...] = z.astype(BF16)
        ds_ref[...] = ds

        @pl.when(pl.program_id(0) == 0)
        def _():
            dd_ref[...] = jnp.zeros_like(dd_ref)

        dd_ref[...] += jnp.sum(dpre * u_ref[...], axis=0, keepdims=True)

    row = pl.BlockSpec((TM_EW, w), lambda i: (i, 0))
    vec = pl.BlockSpec((1, w), lambda i: (0, 0))
    return pl.pallas_call(
        body, grid=(l // TM_EW,),
        in_specs=[row, row, row, row, vec, pl.BlockSpec((w, w), lambda i: (0, 0))],
        out_specs=[row, row, row, vec],
        out_shape=[jax.ShapeDtypeStruct((l, w), F32), jax.ShapeDtypeStruct((l, w), BF16),
                   jax.ShapeDtypeStruct((l, w), BF16), jax.ShapeDtypeStruct((1, w), F32)],
        name="glu_bwd", compiler_params=_params(("arbitrary",)),
    )(pre, s, dys, u, d_skip, w_glu)


TM_CV = 512
TC_CV = 256
HALO = SUBLANES


def _conv_specs(l, col0):
    per = TM_CV // HALO
    nh = l // HALO
    off = col0 // TC_CV
    return [
        pl.BlockSpec((HALO, TC_CV), lambda j, i: (jnp.maximum(i * per - 1, 0), j + off)),
        pl.BlockSpec((TM_CV, TC_CV), lambda j, i: (i, j + off)),
        pl.BlockSpec((HALO, TC_CV), lambda j, i: (jnp.minimum((i + 1) * per, nh - 1), j + off)),
    ]


def _ext(prev_ref, mid_ref, next_ref, first, last):
    p = jnp.where(first, 0.0, prev_ref[...])
    n = jnp.where(last, 0.0, next_ref[...])
    return jnp.concatenate([p, mid_ref[...], n], axis=0)


def _shift_dn(t):
    return pltpu.roll(t, 1, 0)


def _shift_up(t):
    return pltpu.roll(t, t.shape[0] - 1, 0)


def _conv3(e, w_ref, b_ref):
    return w_ref[0:1, :] * _shift_dn(e) + w_ref[1:2, :] * e + w_ref[2:3, :] * _shift_up(e) + b_ref[...]


def _convffn_fwd(up_pre, conv_w, conv_b):
    l = up_pre.shape[0]
    ni = l // TM_CV
    wspec = lambda off: pl.BlockSpec((3, TC_CV), lambda j, i: (0, j + off))
    bspec = lambda off: pl.BlockSpec((1, TC_CV), lambda j, i: (0, j + off))
    voff = D_FF // TC_CV

    def body(gp, gm, gn, vp, vm, vn, wg, bg, wv, bv, o_ref):
        i = pl.program_id(1)
        first, last = i == 0, i == ni - 1
        gate = _conv3(_ext(gp, gm, gn, first, last), wg, bg)[HALO:HALO + TM_CV]
        val = _conv3(_ext(vp, vm, vn, first, last), wv, bv)[HALO:HALO + TM_CV]
        o_ref[...] = (gate * _sigmoid(gate) * val).astype(BF16)

    return pl.pallas_call(
        body, grid=(D_FF // TC_CV, ni),
        in_specs=_conv_specs(l, 0) + _conv_specs(l, D_FF) + [wspec(0), bspec(0), wspec(voff), bspec(voff)],
        out_specs=pl.BlockSpec((TM_CV, TC_CV), lambda j, i: (i, j)),
        out_shape=jax.ShapeDtypeStruct((l, D_FF), BF16), name="convffn_fwd",
        compiler_params=_params(("parallel", "parallel")),
    )(up_pre, up_pre, up_pre, up_pre, up_pre, up_pre, conv_w, conv_b, conv_w, conv_b)


def _convffn_bwd(up_pre, dact, conv_w, conv_b):
    l = up_pre.shape[0]
    ni = l // TM_CV
    wspec = lambda off: pl.BlockSpec((3, TC_CV), lambda j, i: (0, j + off))
    bspec = lambda off: pl.BlockSpec((1, TC_CV), lambda j, i: (0, j + off))
    voff = D_FF // TC_CV

    def body(gp, gm, gn, vp, vm, vn, dp, dm, dn, wg, bg, wv, bv, dup_ref, pg_ref, pv_ref):
        i = pl.program_id(1)
        first, last = i == 0, i == ni - 1
        ge, ve, de = _ext(gp, gm, gn, first, last), _ext(vp, vm, vn, first, last), _ext(dp, dm, dn, first, last)
        gate, val = _conv3(ge, wg, bg), _conv3(ve, wv, bv)
        sig = _sigmoid(gate)
        silu = gate * sig
        dgate = de * val * (sig + silu * (1.0 - sig))
        dval = de * silu
        mid = slice(HALO, HALO + TM_CV)
        rid = lax.broadcasted_iota(jnp.int32, (SUBLANES, TC_CV), 0)

        @pl.when(i == 0)
        def _():
            pg_ref[...] = jnp.zeros_like(pg_ref)
            pv_ref[...] = jnp.zeros_like(pv_ref)

        for half, (dup, e, w_ref, p_ref) in enumerate(((dgate, ge, wg, pg_ref), (dval, ve, wv, pv_ref))):
            dpre = w_ref[0:1, :] * _shift_up(dup) + w_ref[1:2, :] * dup + w_ref[2:3, :] * _shift_dn(dup)
            dup_ref[half] = dpre[mid].astype(BF16)
            dm_ = dup[mid]
            sums = [jnp.sum(dm_ * _shift_dn(e)[mid], axis=0, keepdims=True),
                    jnp.sum(dm_ * e[mid], axis=0, keepdims=True),
                    jnp.sum(dm_ * _shift_up(e)[mid], axis=0, keepdims=True),
                    jnp.sum(dm_, axis=0, keepdims=True)]
            acc = jnp.zeros((SUBLANES, TC_CV), F32)
            for k, sk in enumerate(sums):
                acc = jnp.where(rid == k, sk, acc)
            p_ref[...] += acc

    par = pl.BlockSpec((SUBLANES, TC_CV), lambda j, i: (0, j))
    dup, pg, pv = pl.pallas_call(
        body, grid=(D_FF // TC_CV, ni),
        in_specs=_conv_specs(l, 0) + _conv_specs(l, D_FF) + _conv_specs(l, 0)
        + [wspec(0), bspec(0), wspec(voff), bspec(voff)],
        out_specs=[pl.BlockSpec((2, TM_CV, TC_CV), lambda j, i: (0, i, j)), par, par],
        out_shape=[jax.ShapeDtypeStruct((2, l, D_FF), BF16),
                   jax.ShapeDtypeStruct((SUBLANES, D_FF), F32), jax.ShapeDtypeStruct((SUBLANES, D_FF), F32)],
        name="convffn_bwd", compiler_params=_params(("parallel", "arbitrary")),
    )(up_pre, up_pre, up_pre, up_pre, up_pre, up_pre, dact, dact, dact, conv_w, conv_b, conv_w, conv_b)
    return dup, jnp.concatenate([pg, pv], axis=1)


def _local_step(x, target, wb, sp, late_weights=None, ffn_grads_ready=None, ffn_grads_next=None):
    l = x.shape[0]
    tabs = _rope_tables(l)
    disc = _ssm_disc(sp["a_re"], sp["a_im"], sp["log_step"], sp["b_re"], sp["b_im"])
    bcat, ccat, lam_re, lam_im = _ssm_pack(*disc, sp["c_re"], sp["c_im"])
    d_skip = sp["d_skip"].reshape(1, SSM_WIDTH)

    big = min(l, 1024)
    h = _rms_fwd(x, sp["norm_mix_g"], "rms_mix")
    proj, u = _mm_split(h, wb["w_in"], big, QKV_WIDTH, "mm_in")
    qkv = _rope_fwd(proj, tabs)
    attn, lse = _attn_fwd(qkv, sp["sink"])
    u_seg = _to_segments(u).astype(BF16)
    y_seg, states = _ssm_fwd(u_seg, bcat, ccat, lam_re, lam_im)
    y_ssm = _from_segments(y_seg)
    pre, s_glu, ys = _glu_fwd(y_ssm, u, d_skip, wb["w_glu"])
    mixed = _mix_fwd(attn, ys, sp["norm_attn_g"], sp["norm_ssm_g"])
    x1 = _mm_nn(mixed, wb["w_out"], big, 1024, F32, "mm_out", res=x)
    h2 = _rms_fwd(x1, sp["norm_ffn_g"], "rms_ffn")
    if late_weights is not None:
        wb = dict(wb, **late_weights(h2))
    up_pre = _mm_nn_cols(h2, wb["w_up"], big, "mm_up")
    act = _convffn_fwd(up_pre, sp["conv_w"], sp["conv_b"])
    x2 = _mm_nn(act, wb["w_down"], big, 512, F32, "mm_down", res=x1)
    loss, dx2, dx2b, d_final_g = _final_loss(x2, sp["norm_final_g"].reshape(1, D_MODEL), target)

    g = {"norm_final_g": d_final_g.reshape(D_MODEL)}
    dact = _mm_nt(dx2b, wb["w_down"], big, D_FF // 2, F32, "mm_down_dx")
    g["w_down"] = _mm_tn(act, dx2b, D_FF // 2, 512, "mm_down_dw")
    dup_pre, conv_par = _convffn_bwd(up_pre, dact, sp["conv_w"], sp["conv_b"])
    g["conv_w"], g["conv_b"] = conv_par[0:3], conv_par[3:4]
    g["w_up"] = _mm_tn_cols(h2, dup_pre, wb["w_up"].shape[0], 512, "mm_up_dw")
    zero = ffn_grads_ready(g["w_up"], g["w_down"]) if ffn_grads_ready is not None else 0.0
    dh2 = _mm_nt_cols(dup_pre, wb["w_up"], 512, "mm_up_dx")
    dx1, dx1b, g["norm_ffn_g"] = _rms_bwd(x1, sp["norm_ffn_g"] + zero, dh2, dx2, "rms_ffn_bwd")
    dmixed = _mm_nt(dx1b, wb["w_out"], big, 1024, F32, "mm_out_dx")
    g["w_out"] = _mm_tn(mixed, dx1b, 1024, 1024, "mm_out_dw")
    zero = ffn_grads_next(dmixed) if ffn_grads_next is not None else 0.0
    dattn, dys, g["norm_attn_g"], g["norm_ssm_g"] = _mix_bwd(attn, ys, sp["norm_attn_g"] + zero, sp["norm_ssm_g"],
                                                            dmixed)
    dpre, zb, dsb, dd = _glu_bwd(pre, s_glu, dys, u, d_skip, wb["w_glu"])
    g["d_skip"] = dd.reshape(N_SSM_GROUPS, SSM_GROUP)
    g["w_glu"] = _mm_tn(zb, dsb, 512, 512, "mm_glu_dw")
    du_seg, dbcat, dccat, dlam_re, dlam_im = _ssm_bwd(u_seg, _to_segments(dpre).astype(BF16), states, bcat, ccat,
                                                      lam_re, lam_im)
    dlb_re, dlb_im, dbb_re, dbb_im, g["c_re"], g["c_im"] = _ssm_unpack(dbcat, dccat, dlam_re, dlam_im)
    _, disc_vjp = jax.vjp(_ssm_disc, sp["a_re"], sp["a_im"], sp["log_step"], sp["b_re"], sp["b_im"])
    g["a_re"], g["a_im"], g["log_step"], g["b_re"], g["b_im"] = disc_vjp((dlb_re, dlb_im, dbb_re, dbb_im))
    dqkv, dsink = _attn_bwd(qkv, attn, dattn, lse, sp["sink"])
    g["sink"] = dsink
    dproj = _rope_bwd(dqkv, _from_segments(du_seg), dpre, d_skip, tabs)
    g["w_in"] = _mm_tn(h, dproj, 512, IN_WIDTH, "mm_in_dw")
    dh = _mm_nt(dproj, wb["w_in"], 512, 512, F32, "mm_in_dx")
    grad_x, _, g["norm_mix_g"] = _rms_bwd(x, sp["norm_mix_g"], dh, dx1, "rms_mix_bwd")
    return loss, grad_x, g


MESH = pl.DeviceIdType.MESH
ANY = pl.BlockSpec(memory_space=pl.ANY)


def _place():
    x, y, c = lax.axis_index("x"), lax.axis_index("y"), lax.axis_index("c")
    chips = [(1 - x, y), (x, 1 - y), (1 - x, 1 - y)]
    return x, y, c, chips


def _chip_index(px, py):
    return 2 * px + py


CHUNK_BYTES = 256 * 1024
MAX_CHUNKS = 16


def _row_chunks(rows, row_bytes, align):
    n = max(1, min(MAX_CHUNKS, (rows * row_bytes) // CHUNK_BYTES))
    per = -(-rows // n)
    per = -(-per // align) * align
    return [(r0, min(per, rows - r0)) for r0 in range(0, rows, per)]


def _align_of(dtype):
    return SUBLANES * 4 // jnp.dtype(dtype).itemsize


def _remote(src, dst, send_sem, recv_sem, to):
    return pltpu.make_async_remote_copy(src_ref=src, dst_ref=dst, send_sem=send_sem, recv_sem=recv_sem,
                                        device_id=to, device_id_type=MESH)


CAST_ROWS = 64


def _gather_weights(shards, dtypes):
    nw = len(shards)

    def body(*refs):
        w_refs, o_refs = refs[:nw], refs[nw:2 * nw]
        send_sems, recv_sems, in_sems, out_sems = refs[2 * nw:2 * nw + 4]
        raw, cast = refs[2 * nw + 4:3 * nw + 4], refs[3 * nw + 4:]
        x, y, c, chips = _place()
        mine = _chip_index(x, y)
        sibling = (x, y, 1 - c)

        def rows_of(ref, chip, r0, nr):
            return ref.at[chip, pl.ds(r0, nr), :]

        def copy(wi, k, src, dst, to):
            return _remote(src, dst, send_sems.at[wi, k], recv_sems.at[wi, k], to)

        geo = []
        for wi in range(nw):
            rows, cols = w_refs[wi].shape
            row_bytes = cols * jnp.dtype(dtypes[wi]).itemsize
            geo.append((rows // 2, _row_chunks(rows // 2, row_bytes, _align_of(dtypes[wi]))))

        stage_in = [pltpu.make_async_copy(w_refs[wi], raw[wi], in_sems.at[wi]) for wi in range(nw)]
        for cp in stage_in:
            cp.start()
        staged = [raw[wi] if dtypes[wi] == w_refs[wi].dtype else cast[wi] for wi in range(nw)]
        stage_out = []
        for wi in range(nw):
            stage_in[wi].wait()
            if staged[wi] is not raw[wi]:
                def cast_rows(i, _, wi=wi):
                    rows = pl.ds(pl.multiple_of(i * CAST_ROWS, CAST_ROWS), CAST_ROWS)
                    cast[wi][rows, :] = raw[wi][rows, :].astype(dtypes[wi])
                    return 0

                lax.fori_loop(0, w_refs[wi].shape[0] // CAST_ROWS, cast_rows, 0)
            cp = pltpu.make_async_copy(staged[wi], o_refs[wi].at[mine], out_sems.at[wi])
            cp.start()
            stage_out.append(cp)

        for wi in range(nw):
            hr, half_chunks = geo[wi]
            for j, chip in enumerate(chips):
                for r0, nr in half_chunks:
                    copy(wi, j, staged[wi].at[pl.ds(c * hr + r0, nr), :],
                         rows_of(o_refs[wi], mine, c * hr + r0, nr), (*chip, c)).start()
        for wi in range(nw):
            hr, half_chunks = geo[wi]
            for j, chip in enumerate(chips):
                got = rows_of(o_refs[wi], _chip_index(*chip), c * hr, hr)
                copy(wi, j, got, got, (*chip, c)).wait_recv()
                for r0, nr in half_chunks:
                    piece = rows_of(o_refs[wi], _chip_index(*chip), c * hr + r0, nr)
                    copy(wi, 3 + j, piece, piece, sibling).start()
        for wi in range(nw):
            hr = geo[wi][0]
            for j, chip in enumerate(chips):
                got = rows_of(o_refs[wi], _chip_index(*chip), (1 - c) * hr, hr)
                copy(wi, 3 + j, got, got, sibling).wait_recv()
        for wi in range(nw):
            hr = geo[wi][0]
            sent = rows_of(o_refs[wi], mine, c * hr, hr)
            for k in range(6):
                copy(wi, k, sent, sent, sibling).wait_send()
            stage_out[wi].wait()

    return pl.pallas_call(
        body, in_specs=[ANY] * nw, out_specs=[ANY] * nw,
        out_shape=[jax.ShapeDtypeStruct((4, *s.shape), t) for s, t in zip(shards, dtypes)],
        scratch_shapes=[pltpu.SemaphoreType.DMA((nw, 6)), pltpu.SemaphoreType.DMA((nw, 6)),
                        pltpu.SemaphoreType.DMA((nw,)), pltpu.SemaphoreType.DMA((nw,))]
        + [pltpu.VMEM(s.shape, s.dtype) for s in shards] + [pltpu.VMEM(s.shape, t) for s, t in zip(shards, dtypes)],
        name="gather_weights", compiler_params=_params(vmem_mb=40),
    )(*shards)


HBM = pl.BlockSpec(memory_space=pltpu.HBM)
SEM = pl.BlockSpec(memory_space=pltpu.SEMAPHORE)
EFFECT = pltpu.SideEffectType.DATAFLOW_SIDE_EFFECTING


def _cast_place(w, place, dtype, name):
    rows, cols = w.shape
    tr = _row_tile(rows, cols, _align_of(dtype))

    def body(p_ref, w_ref, o_ref):
        del p_ref
        o_ref[...] = w_ref[...].astype(dtype)

    grid_spec = pltpu.PrefetchScalarGridSpec(
        num_scalar_prefetch=1, grid=(rows // tr,),
        in_specs=[pl.BlockSpec((tr, cols), lambda i, p: (i, 0))],
        out_specs=pl.BlockSpec((None, tr, cols), lambda i, p: (p[1], i, 0)))
    return pl.pallas_call(body, grid_spec=grid_spec, out_shape=jax.ShapeDtypeStruct((4, rows, cols), dtype),
                          name=name, compiler_params=_params(("parallel",)))(place, w)


def _split_start(name, arrays, n_pairs, issue):
    n = len(arrays)

    def body(*refs):
        issue(refs[:n], refs[n:n + n_pairs], refs[n + n_pairs:n + 2 * n_pairs])
        token = refs[2 * n + 2 * n_pairs]
        token[...] = jnp.zeros_like(token)

    dma = pltpu.SemaphoreType.DMA(())
    outs = pl.pallas_call(
        body, name=name,
        out_shape=[dma] * (2 * n_pairs) + [pltpu.HBM(t.shape, t.dtype) for t in arrays]
        + [jax.ShapeDtypeStruct((SUBLANES, LANES), F32)],
        in_specs=[HBM] * n, out_specs=[SEM] * (2 * n_pairs) + [HBM] * n + [pl.BlockSpec(memory_space=pltpu.VMEM)],
        input_output_aliases={a: 2 * n_pairs + a for a in range(n)},
        compiler_params=pltpu.CompilerParams(has_side_effects=EFFECT),
    )(*[pltpu.with_memory_space_constraint(t, pltpu.HBM) for t in arrays])
    return outs[:n_pairs], outs[n_pairs:2 * n_pairs], outs[2 * n_pairs:2 * n_pairs + n], outs[-1]


def _split_wait(name, send_sems, recv_sems, flying, sizes, after):
    n, n_pairs = len(flying), len(send_sems)

    def body(*refs):
        x, y, c, _ = _place()
        for k, ref in enumerate(sizes(refs[:n])):
            cp = _remote(ref, ref, refs[n + k], refs[n + n_pairs + k], (x, y, 1 - c))
            cp.wait_send()
            cp.wait_recv()

    return pl.pallas_call(
        body, name=name, out_shape=[pltpu.HBM(t.shape, t.dtype) for t in flying],
        in_specs=[HBM] * n + [SEM] * (2 * n_pairs) + [ANY], out_specs=[HBM] * n,
        input_output_aliases={a: a for a in range(n)},
        compiler_params=pltpu.CompilerParams(has_side_effects=EFFECT),
    )(*flying, *send_sems, *recv_sems, after)


def _spread_start(lands):
    def issue(land_refs, send_sems, recv_sems):
        x, y, c, chips = _place()
        mine = _chip_index(x, y)
        for a, land in enumerate(land_refs):
            _, rows, cols = land.shape
            hr = rows // 2
            row_bytes = cols * jnp.dtype(land.dtype).itemsize
            for r0, nr in _row_chunks(hr, row_bytes, _align_of(land.dtype)):
                piece = land.at[mine, pl.ds(c * hr + r0, nr), :]
                for chip in chips:
                    for core in (0, 1):
                        _remote(piece, piece, send_sems[a], recv_sems[a], (*chip, core)).start()

    return _split_start("spread_start", lands, len(lands), issue)


def _spread_wait(send_sems, recv_sems, flying, after):
    return _split_wait("spread_wait", send_sems, recv_sems, flying,
                       lambda refs: [r.at[pl.ds(0, 3)] for r in refs], after)


def _pair_start(grads):
    n = len(grads)
    zones = [lax.empty((4, g.shape[1] // 2, g.shape[2]), F32) for g in grads]

    def issue(refs, send_sems, recv_sems):
        x, y, c, _ = _place()
        for a in range(n):
            g_ref, z_ref = refs[a], refs[n + a]
            _, rows, cols = g_ref.shape
            hr = rows // 2
            for k in range(4):
                for r0, nr in _row_chunks(hr, cols * 4, SUBLANES):
                    _remote(g_ref.at[k, pl.ds((1 - c) * hr + r0, nr), :], z_ref.at[k, pl.ds(r0, nr), :],
                            send_sems[a], recv_sems[a], (x, y, 1 - c)).start()

    return _split_start("pair_start", list(grads) + zones, n, issue)


def _pair_wait(send_sems, recv_sems, flying, after):
    n = len(flying) // 2
    out = _split_wait("pair_wait", send_sems, recv_sems, flying, lambda refs: list(refs[n:]), after)
    return out[:n], out[n:]


def _chip_start(sums):
    n = len(sums)
    zones = [lax.empty((3, *s.shape[1:]), s.dtype) for s in sums]

    def issue(refs, send_sems, recv_sems):
        x, y, c, chips = _place()
        for a in range(n):
            s_ref, z_ref = refs[a], refs[n + a]
            _, rows, cols = s_ref.shape
            row_bytes = cols * jnp.dtype(s_ref.dtype).itemsize
            for r0, nr in _row_chunks(rows, row_bytes, _align_of(s_ref.dtype)):
                for j, chip in enumerate(chips):
                    _remote(s_ref.at[_chip_index(*chip), pl.ds(r0, nr), :], z_ref.at[j, pl.ds(r0, nr), :],
                            send_sems[a], recv_sems[a], (*chip, c)).start()

    return _split_start("chip_start", list(sums) + zones, n, issue)


def _chip_wait(send_sems, recv_sems, flying, after):
    n = len(flying) // 2
    return _split_wait("chip_wait", send_sems, recv_sems, flying, lambda refs: list(refs[n:]), after)[n:]


def _pair_exchange(grads):
    na = len(grads)

    def body(*refs):
        g_refs, o_refs = refs[:na], refs[na:2 * na]
        send_sems, recv_sems = refs[2 * na:]
        x, y, c, _ = _place()
        sibling = (x, y, 1 - c)
        for ai in range(na):
            _, rows, cols = g_refs[ai].shape
            hr = rows // 2
            for k in range(4):
                for r0, nr in _row_chunks(hr, cols * 4, SUBLANES):
                    _remote(g_refs[ai].at[k, pl.ds((1 - c) * hr + r0, nr), :], o_refs[ai].at[k, pl.ds(r0, nr), :],
                            send_sems.at[ai], recv_sems.at[ai], sibling).start()
        for ai in range(na):
            _remote(o_refs[ai], o_refs[ai], send_sems.at[ai], recv_sems.at[ai], sibling).wait()

    return pl.pallas_call(
        body, in_specs=[ANY] * na, out_specs=[ANY] * na,
        out_shape=[jax.ShapeDtypeStruct((4, g.shape[1] // 2, g.shape[2]), F32) for g in grads],
        scratch_shapes=[pltpu.SemaphoreType.DMA((na,)), pltpu.SemaphoreType.DMA((na,))],
        name="pair_exchange",
    )(*grads)


def _row_tile(rows, cols, align):
    best = align
    for cand in range(align, rows + 1, align):
        if rows % cand == 0 and cand * cols <= 256 * 1024:
            best = cand
    return best


def _pair_sum(g, got, place, transit, name):
    _, rows, cols = g.shape
    hr = rows // 2
    tr = _row_tile(hr, cols, _align_of(transit))
    nt = hr // tr

    def body(p_ref, g_ref, r_ref, s_ref, own_ref):
        total = g_ref[...] + r_ref[...]
        s_ref[...] = total.astype(transit)

        @pl.when(pl.program_id(1) == p_ref[1])
        def _():
            own_ref[...] = total

    grid_spec = pltpu.PrefetchScalarGridSpec(
        num_scalar_prefetch=1, grid=(nt, 4),
        in_specs=[pl.BlockSpec((None, tr, cols), lambda i, k, p: (k, p[0] * nt + i, 0)),
                  pl.BlockSpec((None, tr, cols), lambda i, k, p: (k, i, 0))],
        out_specs=[pl.BlockSpec((None, tr, cols), lambda i, k, p: (k, i, 0)),
                   pl.BlockSpec((tr, cols), lambda i, k, p: (i, 0))])
    return pl.pallas_call(
        body, grid_spec=grid_spec,
        out_shape=[jax.ShapeDtypeStruct((4, hr, cols), transit), jax.ShapeDtypeStruct((hr, cols), F32)],
        name=name, compiler_params=_params(("parallel", "arbitrary")),
    )(place, g, got)


def _chip_exchange(sums):
    na = len(sums)

    def body(*refs):
        s_refs, o_refs = refs[:na], refs[na:2 * na]
        send_sems, recv_sems = refs[2 * na:]
        x, y, c, chips = _place()
        for ai in range(na):
            _, rows, cols = s_refs[ai].shape
            row_bytes = cols * jnp.dtype(s_refs[ai].dtype).itemsize
            for r0, nr in _row_chunks(rows, row_bytes, _align_of(s_refs[ai].dtype)):
                for j, chip in enumerate(chips):
                    _remote(s_refs[ai].at[_chip_index(*chip), pl.ds(r0, nr), :], o_refs[ai].at[j, pl.ds(r0, nr), :],
                            send_sems.at[ai, j], recv_sems.at[ai, j], (*chip, c)).start()
        for ai in range(na):
            for j, chip in enumerate(chips):
                _remote(o_refs[ai].at[j], o_refs[ai].at[j], send_sems.at[ai, j], recv_sems.at[ai, j],
                        (*chip, c)).wait()

    return pl.pallas_call(
        body, in_specs=[ANY] * na, out_specs=[ANY] * na,
        out_shape=[jax.ShapeDtypeStruct((3, *s.shape[1:]), s.dtype) for s in sums],
        scratch_shapes=[pltpu.SemaphoreType.DMA((na, 3)), pltpu.SemaphoreType.DMA((na, 3))],
        name="chip_exchange",
    )(*sums)


def _chip_sum(own, landed, name):
    hr, cols = own.shape
    tr = _row_tile(hr, cols, _align_of(landed.dtype))

    def body(o_ref, l_ref, f_ref):
        acc = o_ref[...]
        for j in range(3):
            acc = acc + l_ref[j].astype(F32)
        f_ref[...] = acc

    return pl.pallas_call(
        body, grid=(hr // tr,),
        in_specs=[pl.BlockSpec((tr, cols), lambda i: (i, 0)), pl.BlockSpec((3, tr, cols), lambda i: (0, i, 0))],
        out_specs=pl.BlockSpec((tr, cols), lambda i: (i, 0)),
        out_shape=jax.ShapeDtypeStruct((hr, cols), F32), name=name,
        compiler_params=_params(("parallel",)),
    )(own, landed)


def _final_exchange(halves, small):
    nh = len(halves)

    def body(*refs):
        h_refs, s_ref = refs[:nh], refs[nh]
        o_refs, so_ref = refs[nh + 1:2 * nh + 1], refs[2 * nh + 1]
        send_sems, recv_sems, local_sem, ssend_sems, srecv_sems = refs[2 * nh + 2:]
        x, y, c, _ = _place()
        me = 4 * x + 2 * y + c
        sibling = (x, y, 1 - c)
        for hi in range(nh):
            hr, cols = h_refs[hi].shape
            for r0, nr in _row_chunks(hr, cols * 4, SUBLANES):
                _remote(h_refs[hi].at[pl.ds(r0, nr), :], o_refs[hi].at[pl.ds(r0, nr), :],
                        send_sems.at[hi], recv_sems.at[hi], sibling).start()
        small_cps = [pltpu.make_async_copy(s_ref, so_ref.at[me], local_sem)]
        for r in range(1, 8):
            fx, fy, fc = (r >> 2) & 1, (r >> 1) & 1, r & 1
            peer = (1 - x if fx else x, 1 - y if fy else y, 1 - c if fc else c)
            small_cps.append(_remote(s_ref, so_ref.at[me], ssend_sems.at[r - 1], srecv_sems.at[r - 1], peer))
        for cp in small_cps:
            cp.start()
        for hi in range(nh):
            _remote(h_refs[hi], o_refs[hi], send_sems.at[hi], recv_sems.at[hi], sibling).wait()
        for cp in small_cps:
            cp.wait()

    return pl.pallas_call(
        body, in_specs=[ANY] * (nh + 1), out_specs=[ANY] * (nh + 1),
        out_shape=[jax.ShapeDtypeStruct(h.shape, F32) for h in halves]
        + [jax.ShapeDtypeStruct((8, *small.shape), F32)],
        scratch_shapes=[pltpu.SemaphoreType.DMA((nh,)), pltpu.SemaphoreType.DMA((nh,)),
                        pltpu.SemaphoreType.DMA, pltpu.SemaphoreType.DMA((7,)), pltpu.SemaphoreType.DMA((7,))],
        name="final_exchange",
    )(*halves, small)


def _adamw(w, g, m, v, name):
    shape = w.shape
    n = w.size
    if w.ndim >= 2 and shape[-1] >= LANES:
        two_d = (n // shape[-1], shape[-1])
    elif n % LANES == 0:
        two_d = (n // LANES, LANES)
    else:
        two_d = (1, n)
    r, c = two_d
    tr = r
    for cand in (512, 256, 176, 128, 64):
        if r > cand and r % cand == 0 and cand * c <= 256 * 1024:
            tr = cand
            break
    c1 = 1.0 - ADAM_B1 ** ADAM_STEP
    c2 = 1.0 - ADAM_B2 ** ADAM_STEP

    def body(w_ref, g_ref, m_ref, v_ref, d_ref, nm_ref, nv_ref):
        gv = g_ref[...]
        nm = ADAM_B1 * m_ref[...] + (1.0 - ADAM_B1) * gv
        nv = ADAM_B2 * v_ref[...] + (1.0 - ADAM_B2) * (gv * gv)
        d_ref[...] = -ADAM_LR * ((nm / c1) / (jnp.sqrt(nv / c2) + ADAM_EPS) + ADAM_WD * w_ref[...])
        nm_ref[...] = nm
        nv_ref[...] = nv

    spec = pl.BlockSpec((tr, c), lambda i: (i, 0))
    out = jax.ShapeDtypeStruct((r, c), F32)
    d, nm, nv = pl.pallas_call(
        body, grid=(r // tr,), in_specs=[spec] * 4, out_specs=[spec] * 3, out_shape=[out] * 3, name=name,
        compiler_params=_params(("parallel",)),
    )(w.reshape(two_d), g.reshape(two_d), m.reshape(two_d), v.reshape(two_d))
    return d.reshape(shape), nm.reshape(shape), nv.reshape(shape)


def _adamw_many(ws, gs, ms, vs, name):
    n = len(ws)
    c1 = 1.0 - ADAM_B1 ** ADAM_STEP
    c2 = 1.0 - ADAM_B2 ** ADAM_STEP

    def body(*refs):
        w_refs, g_refs, m_refs, v_refs = (refs[k * n:(k + 1) * n] for k in range(4))
        d_refs, nm_refs, nv_refs = (refs[(4 + k) * n:(5 + k) * n] for k in range(3))
        for i in range(n):
            gv = g_refs[i][...]
            nm = ADAM_B1 * m_refs[i][...] + (1.0 - ADAM_B1) * gv
            nv = ADAM_B2 * v_refs[i][...] + (1.0 - ADAM_B2) * (gv * gv)
            d_refs[i][...] = -ADAM_LR * ((nm / c1) / (jnp.sqrt(nv / c2) + ADAM_EPS) + ADAM_WD * w_refs[i][...])
            nm_refs[i][...] = nm
            nv_refs[i][...] = nv

    vmem = pl.BlockSpec(memory_space=pltpu.VMEM)
    shapes = [jax.ShapeDtypeStruct(t.shape, F32) for t in ws]
    outs = pl.pallas_call(body, in_specs=[vmem] * (4 * n), out_specs=[vmem] * (3 * n), out_shape=shapes * 3,
                          name=name, compiler_params=_params(vmem_mb=56))(*ws, *gs, *ms, *vs)
    return outs[:n], outs[n:2 * n], outs[2 * n:]


BIG = ("w_in", "w_glu", "w_out", "w_up", "w_down")
WEIGHTS = ("norm_mix_g", "w_in", "a_re", "a_im", "log_step", "b_re", "b_im", "c_re", "c_im", "d_skip", "w_glu",
           "sink", "norm_attn_g", "norm_ssm_g", "w_out", "norm_ffn_g", "w_up", "conv_w", "conv_b", "w_down",
           "norm_final_g")
SMALL = ("norm_mix_g", "a_re", "a_im", "log_step", "b_re", "b_im", "c_re", "c_im", "d_skip", "sink",
         "norm_attn_g", "norm_ssm_g", "norm_ffn_g", "conv_w", "conv_b", "norm_final_g")
SMALL_ROWS = 40
N_DEV = 8


def _by_owner(name, g):
    if name == "w_up":
        return g
    if name == "w_in":
        return g.reshape(g.shape[0], 4, g.shape[1] // 4).transpose(1, 0, 2)
    return g.reshape(4, g.shape[0] // 4, g.shape[1])


def kernel(x, norm_mix_g, w_in, a_re, a_im, log_step, b_re, b_im, c_re, c_im, d_skip, w_glu, sink, norm_attn_g, norm_ssm_g, w_out, norm_ffn_g, w_up, conv_w, conv_b, w_down, norm_final_g, loss_target, m_norm_mix_g, m_w_in, m_a_re, m_a_im, m_log_step, m_b_re, m_b_im, m_c_re, m_c_im, m_d_skip, m_w_glu, m_sink, m_norm_attn_g, m_norm_ssm_g, m_w_out, m_norm_ffn_g, m_w_up, m_conv_w, m_conv_b, m_w_down, m_norm_final_g, v_norm_mix_g, v_w_in, v_a_re, v_a_im, v_log_step, v_b_re, v_b_im, v_c_re, v_c_im, v_d_skip, v_w_glu, v_sink, v_norm_attn_g, v_norm_ssm_g, v_w_out, v_norm_ffn_g, v_w_up, v_conv_w, v_conv_b, v_w_down, v_norm_final_g):
    given = dict(locals())
    w = {n: given[n] for n in WEIGHTS}
    m = {n: given["m_" + n] for n in WEIGHTS}
    v = {n: given["v_" + n] for n in WEIGHTS}
    xy = 2 * lax.axis_index("x") + lax.axis_index("y")

    core = lax.axis_index("c")
    place = jnp.stack([core, xy]).astype(jnp.int32)

    conv_rows = jnp.pad(w["conv_w"][0], ((0, 2 * SUBLANES - 3), (0, 0)))
    early = ("w_in", "w_glu", "w_out")
    *gathered, conv_all = _gather_weights([w[n][0] for n in early] + [conv_rows], [BF16] * len(early) + [F32])
    late = ("w_up", "w_down")
    send_sems, recv_sems, flying, token = _spread_start([_cast_place(w[n][0], place, BF16, "cast_" + n) for n in late])
    rows = lambda t: t.reshape(4 * t.shape[1], t.shape[2])
    wb = {"w_in": gathered[0].transpose(1, 0, 2).reshape(D_MODEL, IN_WIDTH), "w_glu": rows(gathered[1]),
          "w_out": rows(gathered[2])}

    def late_weights(after):
        w_up4, w_down4 = _spread_wait(send_sems, recv_sems, flying, after)
        return {"w_up": w_up4, "w_down": rows(w_down4)}

    sp = {n: w[n][0] for n in ("a_re", "a_im", "log_step", "b_re", "b_im", "c_re", "c_im", "d_skip",
                               "norm_mix_g", "norm_attn_g", "norm_ssm_g", "norm_ffn_g", "sink", "conv_b")}
    for n in ("norm_mix_g", "norm_attn_g", "norm_ssm_g", "norm_ffn_g", "sink", "conv_b"):
        sp[n] = sp[n].reshape(1, -1)
    sp["norm_mix_g"] = sp["norm_mix_g"] + token[:1, :1]
    sp["conv_w"] = conv_all[:, :3].transpose(1, 0, 2).reshape(3, 2 * D_FF)
    sp["norm_final_g"] = w["norm_final_g"]
    flight = {}

    def ffn_grads_ready(dw_up, dw_down):
        *flight["pair"], token = _pair_start([dw_up, _by_owner("w_down", dw_down)])
        return token[:1, :1]

    def ffn_grads_next(after):
        mine, got = _pair_wait(*flight["pair"], after)
        sums, flight["own"] = zip(*[_pair_sum(a, b, place, BF16, "pair_sum_" + n) for n, a, b in zip(late, mine, got)])
        *flight["chip"], token = _chip_start(list(sums))
        return token[:1, :1]

    loss, grad_x, g = _local_step(x[0], loss_target[0], wb, sp, late_weights, ffn_grads_ready, ffn_grads_next)

    flat = jnp.concatenate([g[n].reshape(-1) for n in SMALL] + [loss.reshape(-1)])
    pad = N_DEV * SMALL_ROWS * D_MODEL - flat.shape[0]
    small = jnp.concatenate([flat, jnp.zeros((pad,), F32)]).reshape(4, 2 * SMALL_ROWS, D_MODEL)
    by_owner = [_by_owner(n, g[n]) for n in early] + [small]
    got = _pair_exchange(by_owner)
    transit = [BF16] * len(early) + [F32]
    chip_sums, own_sums = zip(*[_pair_sum(a, b, place, t, "pair_sum_" + n)
                                for n, a, b, t in zip(early + ("small",), by_owner, got, transit)])
    landed = _chip_exchange(list(chip_sums))
    halves = {n: _chip_sum(o, t, "chip_sum_" + n) for n, o, t in zip(early + ("small",), own_sums, landed)}
    late_landed = _chip_wait(*flight["chip"], grad_x)
    for n, o, t in zip(late, flight["own"], late_landed):
        halves[n] = _chip_sum(o, t, "chip_sum_" + n)
    *others, small_all = _final_exchange([halves[n] for n in BIG], halves["small"])
    grads = {n: jnp.concatenate([jnp.where(core == 0, halves[n], o), jnp.where(core == 0, o, halves[n])], axis=0)
             for n, o in zip(BIG, others)}
    flat = small_all.reshape(-1)
    off = 0
    for n in SMALL:
        shape = (3, 4 * w[n].shape[-1]) if n == "conv_w" else w[n].shape[1:] if n != "norm_final_g" else w[n].shape
        size = math.prod(shape)
        grads[n] = flat[off:off + size].reshape(shape)
        off += size
    loss = flat[off]
    cw = w["conv_w"].shape[-1]
    grads["conv_w"] = lax.dynamic_slice_in_dim(grads["conv_w"], xy * cw, cw, axis=1)
    grads = {n: grads[n].reshape(w[n].shape) for n in WEIGHTS}

    delta, new_m, new_v = {}, {}, {}
    for n in BIG:
        delta[n], new_m[n], new_v[n] = _adamw(w[n], grads[n], m[n], v[n], "adamw_" + n)
    for group, name in ((("b_re", "b_im"), "adamw_b"), (tuple(n for n in SMALL if n not in ("b_re", "b_im")), "adamw_small")):
        row = lambda t: t.reshape(1, -1) if t.ndim == 1 else t
        d_, m_, v_ = _adamw_many(*[[row(t[n]) for n in group] for t in (w, grads, m, v)], name)
        for n, dn, mn, vn in zip(group, d_, m_, v_):
            delta[n], new_m[n], new_v[n] = (t.reshape(w[n].shape) for t in (dn, mn, vn))
    return (loss, grad_x[None], *[grads[n] for n in WEIGHTS], *[delta[n] for n in WEIGHTS],
            *[new_m[n] for n in WEIGHTS], *[new_v[n] for n in WEIGHTS])
```

```python
import functools
import math

import jax
import jax.numpy as jnp
from jax import lax
from jax.experimental import pallas as pl
from jax.experimental.pallas import tpu as pltpu

F32 = jnp.float32
BF16 = jnp.bfloat16

D_MODEL = 1024
N_Q_HEADS = 8
N_KV_HEADS = 2
HEAD_DIM = 64
ATTN_WIDTH = 512
KV_WIDTH = 128
QKV_WIDTH = ATTN_WIDTH + 2 * KV_WIDTH
WINDOW = 128
BLOCK = 128
ROPE_DIM = 16
ROPE_THETA = 500000.0
SSM_WIDTH = 512
SSM_GROUP = 16
N_SSM_GROUPS = 32
SSM_STATE = 64
IN_WIDTH = 1280
D_FF = 2816
EPS = 1e-6
ADAM_LR = 0.001
ADAM_B1 = 0.9
ADAM_B2 = 0.999
ADAM_EPS = 1e-08
ADAM_WD = 0.01
ADAM_STEP = 10

VMEM_BYTES_V7X = 64 * 1024 * 1024
SUBLANES = 8
LANES = 128
SSM_CB = 4
SSM_CH = 128
SSM_ST = 512
N_SEG = SUBLANES

NN = (((1,), (0,)), ((), ()))
NT = (((1,), (1,)), ((), ()))
TN = (((0,), (0,)), ((), ()))


def _params(sem=None, vmem_mb=48):
    return pltpu.CompilerParams(dimension_semantics=sem, vmem_limit_bytes=vmem_mb * 1024 * 1024)


def _dg(a, b, dims):
    return lax.dot_general(a, b, dims, preferred_element_type=F32)


def _sigmoid(x):
    return 1.0 / (1.0 + jnp.exp(-x))


_SQRT_HALF = 0.7071067811865476
_INV_SQRT_2PI = 0.3989422804014327


def _gelu(x):
    return 0.5 * x * (1.0 + lax.erf(x * _SQRT_HALF))


def _gelu_grad(x):
    return 0.5 * (1.0 + lax.erf(x * _SQRT_HALF)) + x * (_INV_SQRT_2PI * jnp.exp(-0.5 * x * x))


def _mm_nn(a, b, tm, tn, out_dtype, name, res=None):
    m, k = a.shape
    n = b.shape[1]

    def body(*refs):
        if res is None:
            a_ref, b_ref, o_ref = refs
            o_ref[...] = _dg(a_ref[...], b_ref[...], NN).astype(out_dtype)
        else:
            a_ref, b_ref, r_ref, o_ref = refs
            o_ref[...] = (r_ref[...] + _dg(a_ref[...], b_ref[...], NN)).astype(out_dtype)

    in_specs = [pl.BlockSpec((tm, k), lambda i, j: (i, 0)), pl.BlockSpec((k, tn), lambda i, j: (0, j))]
    args = [a, b]
    if res is not None:
        in_specs.append(pl.BlockSpec((tm, tn), lambda i, j: (i, j)))
        args.append(res)
    return pl.pallas_call(
        body, grid=(m // tm, n // tn), in_specs=in_specs,
        out_specs=pl.BlockSpec((tm, tn), lambda i, j: (i, j)),
        out_shape=jax.ShapeDtypeStruct((m, n), out_dtype), name=name,
        compiler_params=_params(("parallel", "parallel")),
    )(*args)


def _mm_nt(a, b, tm, tn, out_dtype, name):
    m, k = a.shape
    n = b.shape[0]

    def body(a_ref, b_ref, o_ref):
        o_ref[...] = _dg(a_ref[...], b_ref[...], NT).astype(out_dtype)

    return pl.pallas_call(
        body, grid=(m // tm, n // tn),
        in_specs=[pl.BlockSpec((tm, k), lambda i, j: (i, 0)), pl.BlockSpec((tn, k), lambda i, j: (j, 0))],
        out_specs=pl.BlockSpec((tm, tn), lambda i, j: (i, j)),
        out_shape=jax.ShapeDtypeStruct((m, n), out_dtype), name=name,
        compiler_params=_params(("parallel", "parallel")),
    )(a, b)


def _mm_tn(a, b, tm, tn, name):
    k, m = a.shape
    n = b.shape[1]

    def body(a_ref, b_ref, o_ref):
        o_ref[...] = _dg(a_ref[...], b_ref[...], TN)

    return pl.pallas_call(
        body, grid=(m // tm, n // tn),
        in_specs=[pl.BlockSpec((k, tm), lambda i, j: (0, i)), pl.BlockSpec((k, tn), lambda i, j: (0, j))],
        out_specs=pl.BlockSpec((tm, tn), lambda i, j: (i, j)),
        out_shape=jax.ShapeDtypeStruct((m, n), F32), name=name,
        compiler_params=_params(("parallel", "parallel")),
    )(a, b)


def _mm_split(a, b, tm, split, name):
    m, k = a.shape
    n = b.shape[1]

    def body(a_ref, b_ref, lo_ref, hi_ref):
        out = _dg(a_ref[...], b_ref[...], NN)
        lo_ref[...] = out[:, :split]
        hi_ref[...] = out[:, split:]

    return pl.pallas_call(
        body, grid=(m // tm,),
        in_specs=[pl.BlockSpec((tm, k), lambda i: (i, 0)), pl.BlockSpec((k, n), lambda i: (0, 0))],
        out_specs=[pl.BlockSpec((tm, split), lambda i: (i, 0)), pl.BlockSpec((tm, n - split), lambda i: (i, 0))],
        out_shape=[jax.ShapeDtypeStruct((m, split), F32), jax.ShapeDtypeStruct((m, n - split), F32)], name=name,
        compiler_params=_params(("parallel",)),
    )(a, b)


def _mm_nn_cols(a, b4, tm, name):
    m, k = a.shape
    s, _, n = b4.shape

    def body(a_ref, b_ref, o_ref):
        o_ref[...] = _dg(a_ref[...], b_ref[...], NN)

    return pl.pallas_call(
        body, grid=(m // tm, s),
        in_specs=[pl.BlockSpec((tm, k), lambda i, j: (i, 0)), pl.BlockSpec((None, k, n), lambda i, j: (j, 0, 0))],
        out_specs=pl.BlockSpec((tm, n), lambda i, j: (i, j)),
        out_shape=jax.ShapeDtypeStruct((m, s * n), F32), name=name,
        compiler_params=_params(("parallel", "parallel")),
    )(a, b4)


def _mm_nt_cols(a2, b4, tm, name):
    h, m, wide = a2.shape
    s, k, n = b4.shape
    per = s // h

    def body(a_ref, b_ref, o_ref):
        acc = None
        for j in range(s):
            part = _dg(a_ref[j // per, :, (j % per) * n:(j % per + 1) * n], b_ref[j], NT)
            acc = part if acc is None else acc + part
        o_ref[...] = acc

    return pl.pallas_call(
        body, grid=(m // tm,),
        in_specs=[pl.BlockSpec((h, tm, wide), lambda i: (0, i, 0)), pl.BlockSpec((s, k, n), lambda i: (0, 0, 0))],
        out_specs=pl.BlockSpec((tm, k), lambda i: (i, 0)),
        out_shape=jax.ShapeDtypeStruct((m, k), F32), name=name,
        compiler_params=_params(("parallel",)),
    )(a2, b4)


def _mm_tn_cols(a, b2, s, tm, name):
    k, m = a.shape
    h, _, wide = b2.shape
    per = s // h
    n = wide // per

    def body(a_ref, b_ref, o_ref):
        o_ref[...] = _dg(a_ref[...], b_ref[...], TN)

    return pl.pallas_call(
        body, grid=(s, m // tm),
        in_specs=[pl.BlockSpec((k, tm), lambda j, i: (0, i)),
                  pl.BlockSpec((None, k, n), lambda j, i: (j // per, 0, j % per))],
        out_specs=pl.BlockSpec((None, tm, n), lambda j, i: (j, i, 0)),
        out_shape=jax.ShapeDtypeStruct((s, m, n), F32), name=name,
        compiler_params=_params(("parallel", "parallel")),
    )(a, b2)


TM_EW = 256


def _rms_fwd(x, g, name):
    l, d = x.shape

    def body(x_ref, g_ref, h_ref):
        xv = x_ref[...]
        r = lax.rsqrt(jnp.mean(xv * xv, axis=-1, keepdims=True) + EPS)
        h_ref[...] = (xv * r * g_ref[...]).astype(BF16)

    return pl.pallas_call(
        body, grid=(l // TM_EW,),
        in_specs=[pl.BlockSpec((TM_EW, d), lambda i: (i, 0)), pl.BlockSpec((1, d), lambda i: (0, 0))],
        out_specs=pl.BlockSpec((TM_EW, d), lambda i: (i, 0)),
        out_shape=jax.ShapeDtypeStruct((l, d), BF16), name=name,
        compiler_params=_params(("parallel",)),
    )(x, g)


def _rms_bwd_vals(xv, gv, dy):
    r = lax.rsqrt(jnp.mean(xv * xv, axis=-1, keepdims=True) + EPS)
    xh = xv * r
    dxh = dy * gv
    dx = r * (dxh - xh * jnp.mean(dxh * xh, axis=-1, keepdims=True))
    return dx, dy * xh


def _rms_bwd(x, g, dy, res, name):
    l, d = x.shape

    def body(x_ref, g_ref, dy_ref, res_ref, dx_ref, dxb_ref, dg_ref):
        dx, dgr = _rms_bwd_vals(x_ref[...], g_ref[...], dy_ref[...])
        dx = dx + res_ref[...]
        dx_ref[...] = dx
        dxb_ref[...] = dx.astype(BF16)

        @pl.when(pl.program_id(0) == 0)
        def _():
            dg_ref[...] = jnp.zeros_like(dg_ref)

        dg_ref[...] += jnp.sum(dgr, axis=0, keepdims=True)

    row = pl.BlockSpec((TM_EW, d), lambda i: (i, 0))
    vec = pl.BlockSpec((1, d), lambda i: (0, 0))
    return pl.pallas_call(
        body, grid=(l // TM_EW,), in_specs=[row, vec, row, row], out_specs=[row, row, vec],
        out_shape=[jax.ShapeDtypeStruct((l, d), F32), jax.ShapeDtypeStruct((l, d), BF16),
                   jax.ShapeDtypeStruct((1, d), F32)],
        name=name, compiler_params=_params(("arbitrary",)),
    )(x, g, dy, res)


def _final_loss(x2, g, target):
    l, d = x2.shape

    def body(x_ref, g_ref, t_ref, loss_ref, dx_ref, dxb_ref, dg_ref):
        xv = x_ref[...]
        gv = g_ref[...]
        r = lax.rsqrt(jnp.mean(xv * xv, axis=-1, keepdims=True) + EPS)
        xh = xv * r
        e = xh * gv - t_ref[...]
        part = jnp.sum(jnp.sum(e * e, axis=1, keepdims=True), axis=0, keepdims=True) * (0.5 / d)
        dy = e * (1.0 / d)
        dxh = dy * gv
        dx = r * (dxh - xh * jnp.mean(dxh * xh, axis=-1, keepdims=True))
        dx_ref[...] = dx
        dxb_ref[...] = dx.astype(BF16)

        @pl.when(pl.program_id(0) == 0)
        def _():
            dg_ref[...] = jnp.zeros_like(dg_ref)
            loss_ref[...] = jnp.zeros_like(loss_ref)

        dg_ref[...] += jnp.sum(dy * xh, axis=0, keepdims=True)
        loss_ref[...] += part

    row = pl.BlockSpec((TM_EW, d), lambda i: (i, 0))
    vec = pl.BlockSpec((1, d), lambda i: (0, 0))
    one = pl.BlockSpec((1, 1), lambda i: (0, 0))
    return pl.pallas_call(
        body, grid=(l // TM_EW,), in_specs=[row, vec, row], out_specs=[one, row, row, vec],
        out_shape=[jax.ShapeDtypeStruct((1, 1), F32), jax.ShapeDtypeStruct((l, d), F32),
                   jax.ShapeDtypeStruct((l, d), BF16), jax.ShapeDtypeStruct((1, d), F32)],
        name="final_loss", compiler_params=_params(("arbitrary",)),
    )(x2, g, target)


def _mix_fwd(attn, ys, g_attn, g_ssm):
    l, w = attn.shape

    def body(a_ref, y_ref, ga_ref, gs_ref, o_ref):
        for src, gr, off in ((a_ref, ga_ref, 0), (y_ref, gs_ref, w)):
            xv = src[...]
            r = lax.rsqrt(jnp.mean(xv * xv, axis=-1, keepdims=True) + EPS)
            o_ref[:, off:off + w] = (xv * r * gr[...]).astype(BF16)

    row = pl.BlockSpec((TM_EW, w), lambda i: (i, 0))
    vec = pl.BlockSpec((1, w), lambda i: (0, 0))
    return pl.pallas_call(
        body, grid=(l // TM_EW,), in_specs=[row, row, vec, vec],
        out_specs=pl.BlockSpec((TM_EW, 2 * w), lambda i: (i, 0)),
        out_shape=jax.ShapeDtypeStruct((l, 2 * w), BF16), name="mix_fwd",
        compiler_params=_params(("parallel",)),
    )(attn, ys, g_attn, g_ssm)


def _mix_bwd(attn, ys, g_attn, g_ssm, dmixed):
    l, w = attn.shape

    def body(a_ref, y_ref, ga_ref, gs_ref, dm_ref, da_ref, dy_ref, dga_ref, dgs_ref):
        @pl.when(pl.program_id(0) == 0)
        def _():
            dga_ref[...] = jnp.zeros_like(dga_ref)
            dgs_ref[...] = jnp.zeros_like(dgs_ref)

        for src, gr, off, dst, dgr in ((a_ref, ga_ref, 0, da_ref, dga_ref), (y_ref, gs_ref, w, dy_ref, dgs_ref)):
            dx, dg_rows = _rms_bwd_vals(src[...], gr[...], dm_ref[:, off:off + w])
            dst[...] = dx
            dgr[...] += jnp.sum(dg_rows, axis=0, keepdims=True)

    row = pl.BlockSpec((TM_EW, w), lambda i: (i, 0))
    vec = pl.BlockSpec((1, w), lambda i: (0, 0))
    return pl.pallas_call(
        body, grid=(l // TM_EW,),
        in_specs=[row, row, vec, vec, pl.BlockSpec((TM_EW, 2 * w), lambda i: (i, 0))],
        out_specs=[row, row, vec, vec],
        out_shape=[jax.ShapeDtypeStruct((l, w), F32), jax.ShapeDtypeStruct((l, w), F32),
                   jax.ShapeDtypeStruct((1, w), F32), jax.ShapeDtypeStruct((1, w), F32)],
        name="mix_bwd", compiler_params=_params(("arbitrary",)),
    )(attn, ys, g_attn, g_ssm, dmixed)


def _rope_tables(l):
    half = ROPE_DIM // 2
    inv_freq = jnp.power(ROPE_THETA, -jnp.arange(half, dtype=F32) / half)
    ang = jnp.arange(l, dtype=F32)[:, None] * inv_freq[None, :]
    cos, sin = jnp.cos(ang), jnp.sin(ang)
    ones = jnp.ones((l, HEAD_DIM - ROPE_DIM), F32)
    zeros = jnp.zeros((l, HEAD_DIM - ROPE_DIM), F32)
    zh = jnp.zeros((l, half), F32)
    c = jnp.concatenate([cos, cos, ones], axis=1)
    s_lo = jnp.concatenate([-sin, zh, zeros], axis=1)
    s_hi = jnp.concatenate([zh, sin, zeros], axis=1)
    return tuple(jnp.tile(t, (1, LANES // HEAD_DIM)) for t in (c, s_lo, s_hi))


def _rope_fwd(proj, tabs):
    l = proj.shape[0]
    nq = ATTN_WIDTH // LANES

    def body(p_ref, c_ref, lo_ref, hi_ref, o_ref):
        c, lo, hi = c_ref[...], lo_ref[...], hi_ref[...]
        for blk in range(nq + 1):
            t = p_ref[:, blk * LANES:(blk + 1) * LANES]
            rot = t * c + pltpu.roll(t, LANES - 8, 1) * lo + pltpu.roll(t, 8, 1) * hi
            o_ref[:, blk * LANES:(blk + 1) * LANES] = rot.astype(BF16)
        o_ref[:, (nq + 1) * LANES:] = p_ref[:, (nq + 1) * LANES:].astype(BF16)

    tab = pl.BlockSpec((TM_EW, LANES), lambda i: (i, 0))
    return pl.pallas_call(
        body, grid=(l // TM_EW,),
        in_specs=[pl.BlockSpec((TM_EW, QKV_WIDTH), lambda i: (i, 0)), tab, tab, tab],
        out_specs=pl.BlockSpec((TM_EW, QKV_WIDTH), lambda i: (i, 0)),
        out_shape=jax.ShapeDtypeStruct((l, QKV_WIDTH), BF16), name="rope_fwd",
        compiler_params=_params(("parallel",)),
    )(proj, *tabs)


def _rope_bwd(dqkv, du_ssm, dpre, d_skip, tabs):
    l = dqkv.shape[0]
    nq = ATTN_WIDTH // LANES

    def body(d_ref, du_ref, dpre_ref, ds_ref, c_ref, lo_ref, hi_ref, o_ref):
        c, lo, hi = c_ref[...], lo_ref[...], hi_ref[...]
        for blk in range(nq + 1):
            t = d_ref[:, blk * LANES:(blk + 1) * LANES]
            g = t * c + pltpu.roll(t * lo, 8, 1) + pltpu.roll(t * hi, LANES - 8, 1)
            o_ref[:, blk * LANES:(blk + 1) * LANES] = g.astype(BF16)
        o_ref[:, (nq + 1) * LANES:QKV_WIDTH] = d_ref[:, (nq + 1) * LANES:].astype(BF16)
        o_ref[:, QKV_WIDTH:] = (du_ref[...] + dpre_ref[...] * ds_ref[...]).astype(BF16)

    tab = pl.BlockSpec((TM_EW, LANES), lambda i: (i, 0))
    wide = pl.BlockSpec((TM_EW, SSM_WIDTH), lambda i: (i, 0))
    return pl.pallas_call(
        body, grid=(l // TM_EW,),
        in_specs=[pl.BlockSpec((TM_EW, QKV_WIDTH), lambda i: (i, 0)), wide, wide,
                  pl.BlockSpec((1, SSM_WIDTH), lambda i: (0, 0)), tab, tab, tab],
        out_specs=pl.BlockSpec((TM_EW, IN_WIDTH), lambda i: (i, 0)),
        out_shape=jax.ShapeDtypeStruct((l, IN_WIDTH), BF16), name="rope_bwd",
        compiler_params=_params(("parallel",)),
    )(dqkv, du_ssm, dpre, d_skip, *tabs)


_Q_COLS = ATTN_WIDTH // LANES
_SCALE = HEAD_DIM ** -0.5
_NEG = -1e30


def _window_specs(nb, width, col):
    return [
        pl.BlockSpec((BLOCK, width), lambda n: (jnp.maximum(n - 1, 0), col)),
        pl.BlockSpec((BLOCK, width), lambda n: (n, col)),
        pl.BlockSpec((BLOCK, width), lambda n: (jnp.minimum(n + 1, nb - 1), col)),
    ]


def _stacked_sink(sink_ref, heads):
    rid = lax.broadcasted_iota(jnp.int32, (len(heads) * BLOCK, 1), 0)
    sk = jnp.full(rid.shape, sink_ref[0, heads[-1]], F32)
    for g in range(len(heads) - 2, -1, -1):
        sk = jnp.where(rid < (g + 1) * BLOCK, sink_ref[0, heads[g]], sk)
    return sk


def _attn_fwd(qkv, sink):
    l = qkv.shape[0]
    nb = l // BLOCK
    grp = N_Q_HEADS // N_KV_HEADS

    def body(sink_ref, q_ref, k0, k1, k2, v0, v1, v2, o_ref, lse_ref):
        n = pl.program_id(0)
        q = q_ref[...]
        kw = jnp.concatenate([k0[...], k1[...], k2[...]], axis=0)
        vw = jnp.concatenate([v0[...], v1[...], v2[...]], axis=0)
        row = lax.broadcasted_iota(jnp.int32, (grp * BLOCK, 3 * BLOCK), 0)
        col = lax.broadcasted_iota(jnp.int32, (grp * BLOCK, 3 * BLOCK), 1)
        valid = jnp.abs(col - BLOCK - (row & (BLOCK - 1))) <= WINDOW
        valid &= jnp.logical_not((n == 0) & (col < BLOCK))
        valid &= jnp.logical_not((n == nb - 1) & (col >= 2 * BLOCK))
        for hk in range(N_KV_HEADS):
            heads = range(hk * grp, (hk + 1) * grp)
            qs = jnp.concatenate([q[:, h * HEAD_DIM:(h + 1) * HEAD_DIM] for h in heads], axis=0)
            kh = kw[:, hk * HEAD_DIM:(hk + 1) * HEAD_DIM]
            vh = vw[:, hk * HEAD_DIM:(hk + 1) * HEAD_DIM]
            s = jnp.where(valid, _dg(qs, kh, NT) * _SCALE, _NEG)
            sk = _stacked_sink(sink_ref, heads)
            m = jnp.maximum(jnp.max(s, axis=1, keepdims=True), sk)
            p = jnp.exp(s - m)
            denom = jnp.sum(p, axis=1, keepdims=True) + jnp.exp(sk - m)
            o = _dg((p / denom).astype(BF16), vh, NN)
            lse = m + jnp.log(denom)
            for g, h in enumerate(heads):
                o_ref[:, h * HEAD_DIM:(h + 1) * HEAD_DIM] = o[g * BLOCK:(g + 1) * BLOCK]
                lse_ref[:, h:h + 1] = lse[g * BLOCK:(g + 1) * BLOCK]

    return pl.pallas_call(
        body, grid=(nb,),
        in_specs=[pl.BlockSpec(memory_space=pltpu.SMEM),
                  pl.BlockSpec((BLOCK, ATTN_WIDTH), lambda n: (n, 0))]
        + _window_specs(nb, KV_WIDTH, _Q_COLS) + _window_specs(nb, KV_WIDTH, _Q_COLS + 1),
        out_specs=[pl.BlockSpec((BLOCK, ATTN_WIDTH), lambda n: (n, 0)),
                   pl.BlockSpec((BLOCK, N_Q_HEADS), lambda n: (n, 0))],
        out_shape=[jax.ShapeDtypeStruct((l, ATTN_WIDTH), F32), jax.ShapeDtypeStruct((l, N_Q_HEADS), F32)],
        name="attn_fwd", compiler_params=_params(("parallel",)),
    )(sink, qkv, qkv, qkv, qkv, qkv, qkv, qkv)


def _attn_bwd(qkv, attn, dattn, lse, sink):
    l = qkv.shape[0]
    nb = l // BLOCK
    grp = N_Q_HEADS // N_KV_HEADS

    def body(sink_ref, q0, q1, q2, k0, k1, k2, v0, v1, v2, o0, o1, o2, d0, d1, d2,
             l0, l1, l2, dqkv_ref, dsink_ref):
        n = pl.program_id(0)
        first, last = n == 0, n == nb - 1

        @pl.when(first)
        def _():
            dsink_ref[...] = jnp.zeros_like(dsink_ref)

        cat = lambda a, b, c: jnp.concatenate([a[...], b[...], c[...]], axis=0)
        qw, kw, vw = cat(q0, q1, q2), cat(k0, k1, k2), cat(v0, v1, v2)
        dow = cat(d0, d1, d2)
        prodw = cat(o0, o1, o2) * dow
        lsew = cat(l0, l1, l2)
        dob = dow.astype(BF16)
        win = 3 * BLOCK
        mid = slice(BLOCK, 2 * BLOCK)

        row = lax.broadcasted_iota(jnp.int32, (grp * BLOCK, win), 0)
        col = lax.broadcasted_iota(jnp.int32, (grp * BLOCK, win), 1)
        valid_q = jnp.abs(col - BLOCK - (row & (BLOCK - 1))) <= WINDOW
        valid_q &= jnp.logical_not(first & (col < BLOCK))
        valid_q &= jnp.logical_not(last & (col >= 2 * BLOCK))
        rowk = lax.broadcasted_iota(jnp.int32, (grp * win, BLOCK), 0)
        colk = lax.broadcasted_iota(jnp.int32, (grp * win, BLOCK), 1)
        for g in range(1, grp):
            rowk = jnp.where(rowk >= win, rowk - win, rowk)
        valid_k = jnp.abs(colk + BLOCK - rowk) <= WINDOW
        valid_k &= jnp.logical_not(first & (rowk < BLOCK))
        valid_k &= jnp.logical_not(last & (rowk >= 2 * BLOCK))

        dsink_parts = []
        for hk in range(N_KV_HEADS):
            heads = range(hk * grp, (hk + 1) * grp)
            ksl = slice(hk * HEAD_DIM, (hk + 1) * HEAD_DIM)
            hsl = [slice(h * HEAD_DIM, (h + 1) * HEAD_DIM) for h in heads]
            stack = lambda parts: jnp.concatenate(parts, axis=0)
            qws = stack([qw[:, s_] for s_ in hsl])
            dows = stack([dob[:, s_] for s_ in hsl])
            deltaws = stack([jnp.sum(prodw[:, s_], axis=1, keepdims=True) for s_ in hsl])
            lsews = stack([lsew[:, h:h + 1] for h in heads])
            of_block = lambda t: stack([t[g * win + BLOCK:g * win + 2 * BLOCK] for g in range(grp)])
            qs, dos, deltas, lses = of_block(qws), of_block(dows), of_block(deltaws), of_block(lsews)
            kh, vh = kw[:, ksl], vw[:, ksl]
            s = jnp.where(valid_q, _dg(qs, kh, NT) * _SCALE, _NEG)
            p = jnp.exp(s - lses)
            dp = _dg(dos, vh, NT)
            ds = (p * (dp - deltas) * _SCALE).astype(BF16)
            dq = _dg(ds, kh, NN)
            sink_rows = jnp.exp(_stacked_sink(sink_ref, heads) - lses) * deltas
            for g, h in enumerate(heads):
                dqkv_ref[:, hsl[g]] = dq[g * BLOCK:(g + 1) * BLOCK]
                dsink_parts.append(jnp.sum(sink_rows[g * BLOCK:(g + 1) * BLOCK], axis=0, keepdims=True))
            s2 = jnp.where(valid_k, _dg(qws, kh[mid], NT) * _SCALE, _NEG)
            p2 = jnp.exp(s2 - lsews)
            dv = _dg(p2.astype(BF16), dows, TN)
            dp2 = _dg(dows, vh[mid], NT)
            ds2 = (p2 * (dp2 - deltaws) * _SCALE).astype(BF16)
            dk = _dg(ds2, qws, TN)
            dqkv_ref[:, ATTN_WIDTH + hk * HEAD_DIM:ATTN_WIDTH + (hk + 1) * HEAD_DIM] = dk
            dqkv_ref[:, ATTN_WIDTH + KV_WIDTH + hk * HEAD_DIM:ATTN_WIDTH + KV_WIDTH + (hk + 1) * HEAD_DIM] = dv
        dsink_ref[...] -= jnp.concatenate(dsink_parts, axis=1)

    return pl.pallas_call(
        body, grid=(nb,),
        in_specs=[pl.BlockSpec(memory_space=pltpu.SMEM)]
        + _window_specs(nb, ATTN_WIDTH, 0)
        + _window_specs(nb, KV_WIDTH, _Q_COLS) + _window_specs(nb, KV_WIDTH, _Q_COLS + 1)
        + _window_specs(nb, ATTN_WIDTH, 0) + _window_specs(nb, ATTN_WIDTH, 0)
        + _window_specs(nb, N_Q_HEADS, 0),
        out_specs=[pl.BlockSpec((BLOCK, QKV_WIDTH), lambda n: (n, 0)),
                   pl.BlockSpec((1, N_Q_HEADS), lambda n: (0, 0))],
        out_shape=[jax.ShapeDtypeStruct((l, QKV_WIDTH), F32), jax.ShapeDtypeStruct((1, N_Q_HEADS), F32)],
        name="attn_bwd", compiler_params=_params(("arbitrary",)),
    )(sink, qkv, qkv, qkv, qkv, qkv, qkv, qkv, qkv, qkv, attn, attn, attn,
      dattn, dattn, dattn, lse, lse, lse)


def _ssm_disc(a_re, a_im, log_step, b_re, b_im):
    step = jnp.exp(log_step)[..., None]
    mag = jnp.exp(a_re * step)
    lb_re, lb_im = mag * jnp.cos(a_im * step), mag * jnp.sin(a_im * step)
    nr, ni = lb_re - 1.0, lb_im
    den = a_re * a_re + a_im * a_im
    f_re = ((nr * a_re + ni * a_im) / den)[..., None]
    f_im = ((ni * a_re - nr * a_im) / den)[..., None]
    return lb_re, lb_im, f_re * b_re - f_im * b_im, f_re * b_im + f_im * b_re


def _ssm_pack(lb_re, lb_im, bb_re, bb_im, c_re, c_im):
    eye = jnp.eye(SSM_CH // SSM_GROUP, dtype=F32)
    ng = SSM_CH // SSM_GROUP

    def diag_b(bb):
        t = bb.reshape(2, SSM_CB, ng, SSM_STATE, SSM_GROUP)
        return jnp.einsum('dkgpc,gh->dkgchp', t, eye).reshape(2, SSM_CB, SSM_CH, SSM_ST)

    def diag_c(cc):
        t = cc.reshape(2, SSM_CB, ng, SSM_GROUP, SSM_STATE)
        return jnp.einsum('dkgcp,gh->dkhpgc', t, eye).reshape(2, SSM_CB, SSM_ST, SSM_CH)

    bcat = jnp.concatenate([diag_b(bb_re), diag_b(bb_im)], axis=-1)
    ccat = jnp.concatenate([diag_c(c_re), -diag_c(c_im)], axis=-2)
    lam_re = lb_re.reshape(2, SSM_CB, 1, SSM_ST)
    lam_im = lb_im.reshape(2, SSM_CB, 1, SSM_ST)
    return bcat, ccat, lam_re, lam_im


def _ssm_unpack(dbcat, dccat, dlam_re, dlam_im):
    ng = SSM_CH // SSM_GROUP
    eye = jnp.eye(ng, dtype=F32)

    def undiag_b(t):
        t = t.reshape(2, SSM_CB, ng, SSM_GROUP, ng, SSM_STATE)
        return jnp.einsum('dkgchp,gh->dkgpc', t, eye).reshape(2, N_SSM_GROUPS, SSM_STATE, SSM_GROUP)

    def undiag_c(t):
        t = t.reshape(2, SSM_CB, ng, SSM_STATE, ng, SSM_GROUP)
        return jnp.einsum('dkhpgc,gh->dkgcp', t, eye).reshape(2, N_SSM_GROUPS, SSM_GROUP, SSM_STATE)

    dbb_re, dbb_im = undiag_b(dbcat[..., :SSM_ST]), undiag_b(dbcat[..., SSM_ST:])
    dc_re, dc_im = undiag_c(dccat[:, :, :SSM_ST]), -undiag_c(dccat[:, :, SSM_ST:])
    shape = (2, N_SSM_GROUPS, SSM_STATE)
    return dlam_re.reshape(shape), dlam_im.reshape(shape), dbb_re, dbb_im, dc_re, dc_im


def _to_segments(t):
    l, w = t.shape
    return t.reshape(N_SEG, l // N_SEG, w).transpose(1, 0, 2).reshape(l, w)


def _from_segments(t):
    l, w = t.shape
    return t.reshape(l // N_SEG, N_SEG, w).transpose(1, 0, 2).reshape(l, w)


SCAN_UNROLL = 4


def _cfma(ar, ai, xr, xi, br, bi):
    return ar * xr - ai * xi + br, ar * xi + ai * xr + bi


def _scan_segments(xs_ref, ar, ai, rev, nj, prev_ref=None, before_sums=None):
    shape = (N_SEG, SSM_ST)
    ar = jnp.broadcast_to(ar, shape)
    ai = jnp.broadcast_to(ai, shape)
    zero = jnp.zeros(shape, F32)
    re_cols, im_cols = pl.ds(0, SSM_ST), pl.ds(SSM_ST, SSM_ST)

    def rows_of(jj):
        j = jnp.where(rev, nj - 1 - jj, jj)
        return j, pl.ds(pl.multiple_of(j * N_SEG, N_SEG), N_SEG)

    def steps(step, init, last_step=None):
        def outer(o, carry):
            for k in range(SCAN_UNROLL):
                carry = step(o * SCAN_UNROLL + k, carry)
            return carry

        carry = lax.fori_loop(0, nj // SCAN_UNROLL - 1, outer, init)
        for jj in range(nj - SCAN_UNROLL, nj):
            carry = (last_step if last_step is not None and jj == nj - 1 else step)(jj, carry)
        return carry

    def pass1(jj, carry):
        _, rows = rows_of(jj)
        return _cfma(ar, ai, carry[0], carry[1], xs_ref[rows, re_cols], xs_ref[rows, im_cols])

    end_r, end_i = steps(pass1, (zero, zero))

    pr, pi = ar, ai
    for _ in range(int(math.log2(nj))):
        pr, pi = pr * pr - pi * pi, 2.0 * pr * pi
    seg = lax.broadcasted_iota(jnp.int32, shape, 0)

    def chain(shift, keep):
        ir, ii = zero, zero
        for _ in range(N_SEG - 1):
            tr, ti = _cfma(pr, pi, ir, ii, end_r, end_i)
            ir = jnp.where(keep, pltpu.roll(tr, shift, 0), 0.0)
            ii = jnp.where(keep, pltpu.roll(ti, shift, 0), 0.0)
        return ir, ii

    up_r, up_i = chain(1, seg >= 1)
    dn_r, dn_i = chain(N_SEG - 1, seg <= N_SEG - 2)
    init_r, init_i = jnp.where(rev, dn_r, up_r), jnp.where(rev, dn_i, up_i)

    def pass2(jj, carry):
        j, rows = rows_of(jj)
        nr, ni = _cfma(ar, ai, carry[0], carry[1], xs_ref[rows, re_cols], xs_ref[rows, im_cols])
        xs_ref[rows, re_cols] = nr
        xs_ref[rows, im_cols] = ni
        return (j, nr, ni) + tuple(carry[2:])

    def pass2_plain(jj, carry):
        return pass2(jj, carry)[1:]

    def pass2_sums(jj, carry):
        j, nr, ni, acc_r, acc_i = pass2(jj, carry)
        jp = jnp.where(rev, j - 1, j + 1)
        prow = pl.ds(pl.multiple_of(jp * N_SEG, N_SEG), N_SEG)
        xr, xi = prev_ref[prow, re_cols], prev_ref[prow, im_cols]
        return nr, ni, acc_r + (nr * xr + ni * xi), acc_i + (ni * xr - nr * xi)

    if prev_ref is None:
        steps(pass2_plain, (init_r, init_i))
        return init_r, init_i, None, None
    if before_sums is not None:
        before_sums()
    _, _, acc_r, acc_i = steps(pass2_sums, (init_r, init_i, zero, zero), last_step=pass2_plain)
    return init_r, init_i, acc_r, acc_i


SSM_RC = 256


def _ssm_specs(l):
    act = pl.BlockSpec((l, SSM_CH), lambda k, d: (0, k))
    bmat = pl.BlockSpec((None, None, SSM_CH, 2 * SSM_ST), lambda k, d: (d, k, 0, 0))
    cmat = pl.BlockSpec((None, None, 2 * SSM_ST, SSM_CH), lambda k, d: (d, k, 0, 0))
    lam = pl.BlockSpec((None, None, 1, SSM_ST), lambda k, d: (d, k, 0, 0))
    return act, bmat, cmat, lam


def _ssm_fwd(u_seg, bcat, ccat, lam_re, lam_im):
    l = u_seg.shape[0]
    nj = l // N_SEG

    def body(u_ref, b_ref, c_ref, lr_ref, li_ref, y_ref, keep_ref, xs_ref, keep_sem):
        k, d = pl.program_id(0), pl.program_id(1)

        def bu_chunk(i, _):
            rows = pl.ds(pl.multiple_of(i * SSM_RC, SSM_RC), SSM_RC)
            xs_ref[rows, :] = _dg(u_ref[rows, :], b_ref[...], NN)
            return 0

        lax.fori_loop(0, l // SSM_RC, bu_chunk, 0)
        _scan_segments(xs_ref, lr_ref[...], li_ref[...], d == 1, nj)
        keep = pltpu.make_async_copy(xs_ref, keep_ref.at[d, k], keep_sem)
        keep.start()

        def y_chunk(i, _):
            rows = pl.ds(pl.multiple_of(i * SSM_RC, SSM_RC), SSM_RC)
            yv = _dg(xs_ref[rows, :].astype(BF16), c_ref[...], NN)

            @pl.when(d == 0)
            def _():
                y_ref[rows, :] = yv

            @pl.when(d == 1)
            def _():
                y_ref[rows, :] += yv

            return 0

        lax.fori_loop(0, l // SSM_RC, y_chunk, 0)
        keep.wait()

    act, bmat, cmat, lam = _ssm_specs(l)
    return pl.pallas_call(
        body, grid=(SSM_CB, 2), in_specs=[act, bmat, cmat, lam, lam], out_specs=[act, ANY],
        out_shape=[jax.ShapeDtypeStruct((l, SSM_WIDTH), F32),
                   jax.ShapeDtypeStruct((2, SSM_CB, l, 2 * SSM_ST), F32)],
        scratch_shapes=[pltpu.VMEM((l, 2 * SSM_ST), F32), pltpu.SemaphoreType.DMA],
        name="ssm_fwd", compiler_params=_params(("parallel", "arbitrary"), vmem_mb=56),
    )(u_seg, bcat.astype(BF16), ccat.astype(BF16), lam_re, lam_im)


def _ssm_bwd(u_seg, dy_seg, states, bcat, ccat, lam_re, lam_im):
    l = u_seg.shape[0]
    nj = l // N_SEG

    rc2 = min(l, 2 * SSM_RC)

    def body(u_ref, dy_ref, keep_ref, b_ref, c_ref, lr_ref, li_ref,
             du_ref, db_ref, dc_ref, dlr_ref, dli_ref, xs_ref, gs_ref, keep_sem):
        k, d = pl.program_id(0), pl.program_id(1)
        rev = d == 1
        ar, ai = lr_ref[...], li_ref[...]
        fetch = pltpu.make_async_copy(keep_ref.at[d, k], xs_ref, keep_sem)
        fetch.start()

        def chunk1(i, _):
            rows = pl.ds(pl.multiple_of(i * SSM_RC, SSM_RC), SSM_RC)
            gs_ref[rows, :] = _dg(dy_ref[rows, :], c_ref[...], NT)
            return 0

        lax.fori_loop(0, l // SSM_RC, chunk1, 0)
        _, _, acc_r, acc_i = _scan_segments(gs_ref, ar, -ai, jnp.logical_not(rev), nj, prev_ref=xs_ref,
                                            before_sums=fetch.wait)
        seg = lax.broadcasted_iota(jnp.int32, (N_SEG, SSM_ST), 0)
        jb = jnp.where(rev, nj - 1, 0)
        brow = pl.ds(pl.multiple_of(jb * N_SEG, N_SEG), N_SEG)
        erow = pl.ds(pl.multiple_of((nj - 1 - jb) * N_SEG, N_SEG), N_SEG)
        re_cols, im_cols = pl.ds(0, SSM_ST), pl.ds(SSM_ST, SSM_ST)

        def before(t):
            up = jnp.where(seg >= 1, pltpu.roll(t, 1, 0), 0.0)
            down = jnp.where(seg <= N_SEG - 2, pltpu.roll(t, N_SEG - 1, 0), 0.0)
            return jnp.where(rev, down, up)

        init_r, init_i = before(xs_ref[erow, re_cols]), before(xs_ref[erow, im_cols])
        gr, gi = gs_ref[brow, re_cols], gs_ref[brow, im_cols]
        acc_r = acc_r + gr * init_r + gi * init_i
        acc_i = acc_i + gi * init_r - gr * init_i
        dlr_ref[...] = jnp.sum(acc_r, axis=0, keepdims=True)
        dli_ref[...] = jnp.sum(acc_i, axis=0, keepdims=True)

        db_ref[...] = jnp.zeros_like(db_ref)
        dc_ref[...] = jnp.zeros_like(dc_ref)

        def chunk2(i, _):
            rows = pl.ds(pl.multiple_of(i * rc2, rc2), rc2)
            g = gs_ref[rows, :].astype(BF16)
            dc_ref[...] += _dg(xs_ref[rows, :].astype(BF16), dy_ref[rows, :], TN)
            db_ref[...] += _dg(u_ref[rows, :], g, TN)
            duv = _dg(g, b_ref[...], NT)

            @pl.when(d == 0)
            def _():
                du_ref[rows, :] = duv

            @pl.when(d == 1)
            def _():
                du_ref[rows, :] += duv

            return 0

        lax.fori_loop(0, l // rc2, chunk2, 0)

    act, bmat, cmat, lam = _ssm_specs(l)
    return pl.pallas_call(
        body, grid=(SSM_CB, 2), in_specs=[act, act, ANY, bmat, cmat, lam, lam],
        out_specs=[act, bmat, cmat, lam, lam],
        out_shape=[jax.ShapeDtypeStruct((l, SSM_WIDTH), F32),
                   jax.ShapeDtypeStruct(bcat.shape, F32), jax.ShapeDtypeStruct(ccat.shape, F32),
                   jax.ShapeDtypeStruct(lam_re.shape, F32), jax.ShapeDtypeStruct(lam_im.shape, F32)],
        scratch_shapes=[pltpu.VMEM((l, 2 * SSM_ST), F32), pltpu.VMEM((l, 2 * SSM_ST), F32),
                        pltpu.SemaphoreType.DMA],
        name="ssm_bwd", compiler_params=_params(("parallel", "arbitrary"), vmem_mb=56),
    )(u_seg, dy_seg, states, bcat.astype(BF16), ccat.astype(BF16), lam_re, lam_im)


def _glu_fwd(y_ssm, u, d_skip, w_glu):
    l, w = u.shape

    def body(y_ref, u_ref, d_ref, w_ref, pre_ref, s_ref, ys_ref):
        pre = y_ref[...] + d_ref[...] * u_ref[...]
        z = _gelu(pre)
        s = _dg(z.astype(BF16), w_ref[...], NN)
        pre_ref[...] = pre
        s_ref[...] = s
        ys_ref[...] = z * _sigmoid(s)

    row = pl.BlockSpec((TM_EW, w), lambda i: (i, 0))
    out = jax.ShapeDtypeStruct((l, w), F32)
    return pl.pallas_call(
        body, grid=(l // TM_EW,),
        in_specs=[row, row, pl.BlockSpec((1, w), lambda i: (0, 0)), pl.BlockSpec((w, w), lambda i: (0, 0))],
        out_specs=[row, row, row], out_shape=[out, out, out], name="glu_fwd",
        compiler_params=_params(("parallel",)),
    )(y_ssm, u, d_skip, w_glu)


def _glu_bwd(pre, s, dys, u, d_skip, w_glu):
    l, w = u.shape

    def body(pre_ref, s_ref, dys_ref, u_ref, d_ref, w_ref, dpre_ref, z_ref, ds_ref, dd_ref):
        pre, dys = pre_ref[...], dys_ref[...]
        z = _gelu(pre)
        sig = _sigmoid(s_ref[...])
        ds = (dys * z * sig * (1.0 - sig)).astype(BF16)
        dz = dys * sig + _dg(ds, w_ref[...], NT)
        dpre = dz * _gelu_grad(pre)
        dpre_ref[...] = dpre
        z_ref[...] = z.astype(BF16)
        ds_ref[...] = ds

        @pl.when(pl.program_id(0) == 0)
        def _():
            dd_ref[...] = jnp.zeros_like(dd_ref)

        dd_ref[...] += jnp.sum(dpre * u_ref[...], axis=0, keepdims=True)

    row = pl.BlockSpec((TM_EW, w), lambda i: (i, 0))
    vec = pl.BlockSpec((1, w), lambda i: (0, 0))
    return pl.pallas_call(
        body, grid=(l // TM_EW,),
        in_specs=[row, row, row, row, vec, pl.BlockSpec((w, w), lambda i: (0, 0))],
        out_specs=[row, row, row, vec],
        out_shape=[jax.ShapeDtypeStruct((l, w), F32), jax.ShapeDtypeStruct((l, w), BF16),
                   jax.ShapeDtypeStruct((l, w), BF16), jax.ShapeDtypeStruct((1, w), F32)],
        name="glu_bwd", compiler_params=_params(("arbitrary",)),
    )(pre, s, dys, u, d_skip, w_glu)


TM_CV = 512
TC_CV = 256
HALO = SUBLANES


def _conv_specs(l, col0):
    per = TM_CV // HALO
    nh = l // HALO
    off = col0 // TC_CV
    return [
        pl.BlockSpec((HALO, TC_CV), lambda j, i: (jnp.maximum(i * per - 1, 0), j + off)),
        pl.BlockSpec((TM_CV, TC_CV), lambda j, i: (i, j + off)),
        pl.BlockSpec((HALO, TC_CV), lambda j, i: (jnp.minimum((i + 1) * per, nh - 1), j + off)),
    ]


def _ext(prev_ref, mid_ref, next_ref, first, last):
    p = jnp.where(first, 0.0, prev_ref[...])
    n = jnp.where(last, 0.0, next_ref[...])
    return jnp.concatenate([p, mid_ref[...], n], axis=0)


def _shift_dn(t):
    return pltpu.roll(t, 1, 0)


def _shift_up(t):
    return pltpu.roll(t, t.shape[0] - 1, 0)


def _conv3(e, w_ref, b_ref):
    return w_ref[0:1, :] * _shift_dn(e) + w_ref[1:2, :] * e + w_ref[2:3, :] * _shift_up(e) + b_ref[...]


def _convffn_fwd(up_pre, conv_w, conv_b):
    l = up_pre.shape[0]
    ni = l // TM_CV
    wspec = lambda off: pl.BlockSpec((3, TC_CV), lambda j, i: (0, j + off))
    bspec = lambda off: pl.BlockSpec((1, TC_CV), lambda j, i: (0, j + off))
    voff = D_FF // TC_CV

    def body(gp, gm, gn, vp, vm, vn, wg, bg, wv, bv, o_ref):
        i = pl.program_id(1)
        first, last = i == 0, i == ni - 1
        gate = _conv3(_ext(gp, gm, gn, first, last), wg, bg)[HALO:HALO + TM_CV]
        val = _conv3(_ext(vp, vm, vn, first, last), wv, bv)[HALO:HALO + TM_CV]
        o_ref[...] = (gate * _sigmoid(gate) * val).astype(BF16)

    return pl.pallas_call(
        body, grid=(D_FF // TC_CV, ni),
        in_specs=_conv_specs(l, 0) + _conv_specs(l, D_FF) + [wspec(0), bspec(0), wspec(voff), bspec(voff)],
        out_specs=pl.BlockSpec((TM_CV, TC_CV), lambda j, i: (i, j)),
        out_shape=jax.ShapeDtypeStruct((l, D_FF), BF16), name="convffn_fwd",
        compiler_params=_params(("parallel", "parallel")),
    )(up_pre, up_pre, up_pre, up_pre, up_pre, up_pre, conv_w, conv_b, conv_w, conv_b)


def _convffn_bwd(up_pre, dact, conv_w, conv_b):
    l = up_pre.shape[0]
    ni = l // TM_CV
    wspec = lambda off: pl.BlockSpec((3, TC_CV), lambda j, i: (0, j + off))
    bspec = lambda off: pl.BlockSpec((1, TC_CV), lambda j, i: (0, j + off))
    voff = D_FF // TC_CV

    def body(gp, gm, gn, vp, vm, vn, dp, dm, dn, wg, bg, wv, bv, dup_ref, pg_ref, pv_ref):
        i = pl.program_id(1)
        first, last = i == 0, i == ni - 1
        ge, ve, de = _ext(gp, gm, gn, first, last), _ext(vp, vm, vn, first, last), _ext(dp, dm, dn, first, last)
        gate, val = _conv3(ge, wg, bg), _conv3(ve, wv, bv)
        sig = _sigmoid(gate)
        silu = gate * sig
        dgate = de * val * (sig + silu * (1.0 - sig))
        dval = de * silu
        mid = slice(HALO, HALO + TM_CV)
        rid = lax.broadcasted_iota(jnp.int32, (SUBLANES, TC_CV), 0)

        @pl.when(i == 0)
        def _():
            pg_ref[...] = jnp.zeros_like(pg_ref)
            pv_ref[...] = jnp.zeros_like(pv_ref)

        for half, (dup, e, w_ref, p_ref) in enumerate(((dgate, ge, wg, pg_ref), (dval, ve, wv, pv_ref))):
            dpre = w_ref[0:1, :] * _shift_up(dup) + w_ref[1:2, :] * dup + w_ref[2:3, :] * _shift_dn(dup)
            dup_ref[half] = dpre[mid].astype(BF16)
            dm_ = dup[mid]
            sums = [jnp.sum(dm_ * _shift_dn(e)[mid], axis=0, keepdims=True),
                    jnp.sum(dm_ * e[mid], axis=0, keepdims=True),
                    jnp.sum(dm_ * _shift_up(e)[mid], axis=0, keepdims=True),
                    jnp.sum(dm_, axis=0, keepdims=True)]
            acc = jnp.zeros((SUBLANES, TC_CV), F32)
            for k, sk in enumerate(sums):
                acc = jnp.where(rid == k, sk, acc)
            p_ref[...] += acc

    par = pl.BlockSpec((SUBLANES, TC_CV), lambda j, i: (0, j))
    dup, pg, pv = pl.pallas_call(
        body, grid=(D_FF // TC_CV, ni),
        in_specs=_conv_specs(l, 0) + _conv_specs(l, D_FF) + _conv_specs(l, 0)
        + [wspec(0), bspec(0), wspec(voff), bspec(voff)],
        out_specs=[pl.BlockSpec((2, TM_CV, TC_CV), lambda j, i: (0, i, j)), par, par],
        out_shape=[jax.ShapeDtypeStruct((2, l, D_FF), BF16),
                   jax.ShapeDtypeStruct((SUBLANES, D_FF), F32), jax.ShapeDtypeStruct((SUBLANES, D_FF), F32)],
        name="convffn_bwd", compiler_params=_params(("parallel", "arbitrary")),
    )(up_pre, up_pre, up_pre, up_pre, up_pre, up_pre, dact, dact, dact, conv_w, conv_b, conv_w, conv_b)
    return dup, jnp.concatenate([pg, pv], axis=1)


def _local_step(x, target, wb, sp, late_weights=None, ffn_grads_ready=None, ffn_grads_next=None):
    l = x.shape[0]
    tabs = _rope_tables(l)
    disc = _ssm_disc(sp["a_re"], sp["a_im"], sp["log_step"], sp["b_re"], sp["b_im"])
    bcat, ccat, lam_re, lam_im = _ssm_pack(*disc, sp["c_re"], sp["c_im"])
    d_skip = sp["d_skip"].reshape(1, SSM_WIDTH)

    big = min(l, 1024)
    h = _rms_fwd(x, sp["norm_mix_g"], "rms_mix")
    proj, u = _mm_split(h, wb["w_in"], big, QKV_WIDTH, "mm_in")
    qkv = _rope_fwd(proj, tabs)
    attn, lse = _attn_fwd(qkv, sp["sink"])
    u_seg = _to_segments(u).astype(BF16)
    y_seg, states = _ssm_fwd(u_seg, bcat, ccat, lam_re, lam_im)
    y_ssm = _from_segments(y_seg)
    pre, s_glu, ys = _glu_fwd(y_ssm, u, d_skip, wb["w_glu"])
    mixed = _mix_fwd(attn, ys, sp["norm_attn_g"], sp["norm_ssm_g"])
    x1 = _mm_nn(mixed, wb["w_out"], big, 1024, F32, "mm_out", res=x)
    h2 = _rms_fwd(x1, sp["norm_ffn_g"], "rms_ffn")
    if late_weights is not None:
        wb = dict(wb, **late_weights(h2))
    up_pre = _mm_nn_cols(h2, wb["w_up"], big, "mm_up")
    act = _convffn_fwd(up_pre, sp["conv_w"], sp["conv_b"])
    x2 = _mm_nn(act, wb["w_down"], big, 512, F32, "mm_down", res=x1)
    loss, dx2, dx2b, d_final_g = _final_loss(x2, sp["norm_final_g"].reshape(1, D_MODEL), target)

    g = {"norm_final_g": d_final_g.reshape(D_MODEL)}
    dact = _mm_nt(dx2b, wb["w_down"], big, D_FF // 2, F32, "mm_down_dx")
    g["w_down"] = _mm_tn(act, dx2b, D_FF // 2, 512, "mm_down_dw")
    dup_pre, conv_par = _convffn_bwd(up_pre, dact, sp["conv_w"], sp["conv_b"])
    g["conv_w"], g["conv_b"] = conv_par[0:3], conv_par[3:4]
    g["w_up"] = _mm_tn_cols(h2, dup_pre, wb["w_up"].shape[0], 512, "mm_up_dw")
    zero = ffn_grads_ready(g["w_up"], g["w_down"]) if ffn_grads_ready is not None else 0.0
    dh2 = _mm_nt_cols(dup_pre, wb["w_up"], 512, "mm_up_dx")
    dx1, dx1b, g["norm_ffn_g"] = _rms_bwd(x1, sp["norm_ffn_g"] + zero, dh2, dx2, "rms_ffn_bwd")
    dmixed = _mm_nt(dx1b, wb["w_out"], big, 1024, F32, "mm_out_dx")
    g["w_out"] = _mm_tn(mixed, dx1b, 1024, 1024, "mm_out_dw")
    zero = ffn_grads_next(dmixed) if ffn_grads_next is not None else 0.0
    dattn, dys, g["norm_attn_g"], g["norm_ssm_g"] = _mix_bwd(attn, ys, sp["norm_attn_g"] + zero, sp["norm_ssm_g"],
                                                            dmixed)
    dpre, zb, dsb, dd = _glu_bwd(pre, s_glu, dys, u, d_skip, wb["w_glu"])
    g["d_skip"] = dd.reshape(N_SSM_GROUPS, SSM_GROUP)
    g["w_glu"] = _mm_tn(zb, dsb, 512, 512, "mm_glu_dw")
    du_seg, dbcat, dccat, dlam_re, dlam_im = _ssm_bwd(u_seg, _to_segments(dpre).astype(BF16), states, bcat, ccat,
                                                      lam_re, lam_im)
    dlb_re, dlb_im, dbb_re, dbb_im, g["c_re"], g["c_im"] = _ssm_unpack(dbcat, dccat, dlam_re, dlam_im)
    _, disc_vjp = jax.vjp(_ssm_disc, sp["a_re"], sp["a_im"], sp["log_step"], sp["b_re"], sp["b_im"])
    g["a_re"], g["a_im"], g["log_step"], g["b_re"], g["b_im"] = disc_vjp((dlb_re, dlb_im, dbb_re, dbb_im))
    dqkv, dsink = _attn_bwd(qkv, attn, dattn, lse, sp["sink"])
    g["sink"] = dsink
    dproj = _rope_bwd(dqkv, _from_segments(du_seg), dpre, d_skip, tabs)
    g["w_in"] = _mm_tn(h, dproj, 512, IN_WIDTH, "mm_in_dw")
    dh = _mm_nt(dproj, wb["w_in"], 512, 512, F32, "mm_in_dx")
    grad_x, _, g["norm_mix_g"] = _rms_bwd(x, sp["norm_mix_g"], dh, dx1, "rms_mix_bwd")
    return loss, grad_x, g


MESH = pl.DeviceIdType.MESH
ANY = pl.BlockSpec(memory_space=pl.ANY)


def _place():
    x, y, c = lax.axis_index("x"), lax.axis_index("y"), lax.axis_index("c")
    chips = [(1 - x, y), (x, 1 - y), (1 - x, 1 - y)]
    return x, y, c, chips


def _chip_index(px, py):
    return 2 * px + py


CHUNK_BYTES = 256 * 1024
MAX_CHUNKS = 16


def _row_chunks(rows, row_bytes, align):
    n = max(1, min(MAX_CHUNKS, (rows * row_bytes) // CHUNK_BYTES))
    per = -(-rows // n)
    per = -(-per // align) * align
    return [(r0, min(per, rows - r0)) for r0 in range(0, rows, per)]


def _align_of(dtype):
    return SUBLANES * 4 // jnp.dtype(dtype).itemsize


def _remote(src, dst, send_sem, recv_sem, to):
    return pltpu.make_async_remote_copy(src_ref=src, dst_ref=dst, send_sem=send_sem, recv_sem=recv_sem,
                                        device_id=to, device_id_type=MESH)


CAST_ROWS = 64


def _gather_weights(shards, dtypes):
    nw = len(shards)

    def body(*refs):
        w_refs, o_refs = refs[:nw], refs[nw:2 * nw]
        send_sems, recv_sems, in_sems, out_sems = refs[2 * nw:2 * nw + 4]
        raw, cast = refs[2 * nw + 4:3 * nw + 4], refs[3 * nw + 4:]
        x, y, c, chips = _place()
        mine = _chip_index(x, y)
        sibling = (x, y, 1 - c)

        def rows_of(ref, chip, r0, nr):
            return ref.at[chip, pl.ds(r0, nr), :]

        def copy(wi, k, src, dst, to):
            return _remote(src, dst, send_sems.at[wi, k], recv_sems.at[wi, k], to)

        geo = []
        for wi in range(nw):
            rows, cols = w_refs[wi].shape
            row_bytes = cols * jnp.dtype(dtypes[wi]).itemsize
            geo.append((rows // 2, _row_chunks(rows // 2, row_bytes, _align_of(dtypes[wi]))))

        stage_in = [pltpu.make_async_copy(w_refs[wi], raw[wi], in_sems.at[wi]) for wi in range(nw)]
        for cp in stage_in:
            cp.start()
        staged = [raw[wi] if dtypes[wi] == w_refs[wi].dtype else cast[wi] for wi in range(nw)]
        stage_out = []
        for wi in range(nw):
            stage_in[wi].wait()
            if staged[wi] is not raw[wi]:
                def cast_rows(i, _, wi=wi):
                    rows = pl.ds(pl.multiple_of(i * CAST_ROWS, CAST_ROWS), CAST_ROWS)
                    cast[wi][rows, :] = raw[wi][rows, :].astype(dtypes[wi])
                    return 0

                lax.fori_loop(0, w_refs[wi].shape[0] // CAST_ROWS, cast_rows, 0)
            cp = pltpu.make_async_copy(staged[wi], o_refs[wi].at[mine], out_sems.at[wi])
            cp.start()
            stage_out.append(cp)

        for wi in range(nw):
            hr, half_chunks = geo[wi]
            for j, chip in enumerate(chips):
                for r0, nr in half_chunks:
                    copy(wi, j, staged[wi].at[pl.ds(c * hr + r0, nr), :],
                         rows_of(o_refs[wi], mine, c * hr + r0, nr), (*chip, c)).start()
        for wi in range(nw):
            hr, half_chunks = geo[wi]
            for j, chip in enumerate(chips):
                got = rows_of(o_refs[wi], _chip_index(*chip), c * hr, hr)
                copy(wi, j, got, got, (*chip, c)).wait_recv()
                for r0, nr in half_chunks:
                    piece = rows_of(o_refs[wi], _chip_index(*chip), c * hr + r0, nr)
                    copy(wi, 3 + j, piece, piece, sibling).start()
        for wi in range(nw):
            hr = geo[wi][0]
            for j, chip in enumerate(chips):
                got = rows_of(o_refs[wi], _chip_index(*chip), (1 - c) * hr, hr)
                copy(wi, 3 + j, got, got, sibling).wait_recv()
        for wi in range(nw):
            hr = geo[wi][0]
            sent = rows_of(o_refs[wi], mine, c * hr, hr)
            for k in range(6):
                copy(wi, k, sent, sent, sibling).wait_send()
            stage_out[wi].wait()

    return pl.pallas_call(
        body, in_specs=[ANY] * nw, out_specs=[ANY] * nw,
        out_shape=[jax.ShapeDtypeStruct((4, *s.shape), t) for s, t in zip(shards, dtypes)],
        scratch_shapes=[pltpu.SemaphoreType.DMA((nw, 6)), pltpu.SemaphoreType.DMA((nw, 6)),
                        pltpu.SemaphoreType.DMA((nw,)), pltpu.SemaphoreType.DMA((nw,))]
        + [pltpu.VMEM(s.shape, s.dtype) for s in shards] + [pltpu.VMEM(s.shape, t) for s, t in zip(shards, dtypes)],
        name="gather_weights", compiler_params=_params(vmem_mb=40),
    )(*shards)


HBM = pl.BlockSpec(memory_space=pltpu.HBM)
SEM = pl.BlockSpec(memory_space=pltpu.SEMAPHORE)
EFFECT = pltpu.SideEffectType.DATAFLOW_SIDE_EFFECTING


def _cast_place(w, place, dtype, after, name):
    rows, cols = w.shape
    tr = _row_tile(rows, cols, _align_of(dtype))

    def body(p_ref, w_ref, after_ref, o_ref):
        del p_ref, after_ref
        o_ref[...] = w_ref[...].astype(dtype)

    grid_spec = pltpu.PrefetchScalarGridSpec(
        num_scalar_prefetch=1, grid=(rows // tr,),
        in_specs=[pl.BlockSpec((tr, cols), lambda i, p: (i, 0)), ANY],
        out_specs=pl.BlockSpec((None, tr, cols), lambda i, p: (p[1], i, 0)))
    return pl.pallas_call(body, grid_spec=grid_spec, out_shape=jax.ShapeDtypeStruct((4, rows, cols), dtype),
                          name=name, compiler_params=_params(("parallel",)))(place, w, after)


def _split_start(name, arrays, n_pairs, issue):
    n = len(arrays)

    def body(*refs):
        issue(refs[:n], refs[n:n + n_pairs], refs[n + n_pairs:n + 2 * n_pairs])
        token = refs[2 * n + 2 * n_pairs]
        token[...] = jnp.zeros_like(token)

    dma = pltpu.SemaphoreType.DMA(())
    outs = pl.pallas_call(
        body, name=name,
        out_shape=[dma] * (2 * n_pairs) + [pltpu.HBM(t.shape, t.dtype) for t in arrays]
        + [jax.ShapeDtypeStruct((SUBLANES, LANES), F32)],
        in_specs=[HBM] * n, out_specs=[SEM] * (2 * n_pairs) + [HBM] * n + [pl.BlockSpec(memory_space=pltpu.VMEM)],
        input_output_aliases={a: 2 * n_pairs + a for a in range(n)},
        compiler_params=pltpu.CompilerParams(has_side_effects=EFFECT),
    )(*[pltpu.with_memory_space_constraint(t, pltpu.HBM) for t in arrays])
    return outs[:n_pairs], outs[n_pairs:2 * n_pairs], outs[2 * n_pairs:2 * n_pairs + n], outs[-1]


def _split_wait(name, send_sems, recv_sems, flying, sizes, after):
    n, n_pairs = len(flying), len(send_sems)

    def body(*refs):
        x, y, c, _ = _place()
        for k, ref in enumerate(sizes(refs[:n])):
            cp = _remote(ref, ref, refs[n + k], refs[n + n_pairs + k], (x, y, 1 - c))
            cp.wait_send()
            cp.wait_recv()

    return pl.pallas_call(
        body, name=name, out_shape=[pltpu.HBM(t.shape, t.dtype) for t in flying],
        in_specs=[HBM] * n + [SEM] * (2 * n_pairs) + [ANY], out_specs=[HBM] * n,
        input_output_aliases={a: a for a in range(n)},
        compiler_params=pltpu.CompilerParams(has_side_effects=EFFECT),
    )(*flying, *send_sems, *recv_sems, after)


def _spread_start(lands):
    def issue(land_refs, send_sems, recv_sems):
        x, y, c, chips = _place()
        mine = _chip_index(x, y)
        for a, land in enumerate(land_refs):
            _, rows, cols = land.shape
            hr = rows // 2
            row_bytes = cols * jnp.dtype(land.dtype).itemsize
            for r0, nr in _row_chunks(hr, row_bytes, _align_of(land.dtype)):
                piece = land.at[mine, pl.ds(c * hr + r0, nr), :]
                for chip in chips:
                    for core in (0, 1):
                        _remote(piece, piece, send_sems[a], recv_sems[a], (*chip, core)).start()

    return _split_start("spread_start", lands, len(lands), issue)


def _spread_wait(send_sems, recv_sems, flying, after):
    return _split_wait("spread_wait", send_sems, recv_sems, flying,
                       lambda refs: [r.at[pl.ds(0, 3)] for r in refs], after)


def _pair_start(grads):
    n = len(grads)
    zones = [lax.empty((4, g.shape[1] // 2, g.shape[2]), F32) for g in grads]

    def issue(refs, send_sems, recv_sems):
        x, y, c, _ = _place()
        for a in range(n):
            g_ref, z_ref = refs[a], refs[n + a]
            _, rows, cols = g_ref.shape
            hr = rows // 2
            for k in range(4):
                for r0, nr in _row_chunks(hr, cols * 4, SUBLANES):
                    _remote(g_ref.at[k, pl.ds((1 - c) * hr + r0, nr), :], z_ref.at[k, pl.ds(r0, nr), :],
                            send_sems[a], recv_sems[a], (x, y, 1 - c)).start()

    return _split_start("pair_start", list(grads) + zones, n, issue)


def _pair_wait(send_sems, recv_sems, flying, after):
    n = len(flying) // 2
    out = _split_wait("pair_wait", send_sems, recv_sems, flying, lambda refs: list(refs[n:]), after)
    return out[:n], out[n:]


def _chip_start(sums):
    n = len(sums)
    zones = [lax.empty((3, *s.shape[1:]), s.dtype) for s in sums]

    def issue(refs, send_sems, recv_sems):
        x, y, c, chips = _place()
        for a in range(n):
            s_ref, z_ref = refs[a], refs[n + a]
            _, rows, cols = s_ref.shape
            row_bytes = cols * jnp.dtype(s_ref.dtype).itemsize
            for r0, nr in _row_chunks(rows, row_bytes, _align_of(s_ref.dtype)):
                for j, chip in enumerate(chips):
                    _remote(s_ref.at[_chip_index(*chip), pl.ds(r0, nr), :], z_ref.at[j, pl.ds(r0, nr), :],
                            send_sems[a], recv_sems[a], (*chip, c)).start()

    return _split_start("chip_start", list(sums) + zones, n, issue)


def _chip_wait(send_sems, recv_sems, flying, after):
    n = len(flying) // 2
    return _split_wait("chip_wait", send_sems, recv_sems, flying, lambda refs: list(refs[n:]), after)[n:]


def _pair_exchange(grads):
    na = len(grads)

    def body(*refs):
        g_refs, o_refs = refs[:na], refs[na:2 * na]
        send_sems, recv_sems = refs[2 * na:]
        x, y, c, _ = _place()
        sibling = (x, y, 1 - c)
        for ai in range(na):
            _, rows, cols = g_refs[ai].shape
            hr = rows // 2
            for k in range(4):
                for r0, nr in _row_chunks(hr, cols * 4, SUBLANES):
                    _remote(g_refs[ai].at[k, pl.ds((1 - c) * hr + r0, nr), :], o_refs[ai].at[k, pl.ds(r0, nr), :],
                            send_sems.at[ai], recv_sems.at[ai], sibling).start()
        for ai in range(na):
            _remote(o_refs[ai], o_refs[ai], send_sems.at[ai], recv_sems.at[ai], sibling).wait()

    return pl.pallas_call(
        body, in_specs=[ANY] * na, out_specs=[ANY] * na,
        out_shape=[jax.ShapeDtypeStruct((4, g.shape[1] // 2, g.shape[2]), F32) for g in grads],
        scratch_shapes=[pltpu.SemaphoreType.DMA((na,)), pltpu.SemaphoreType.DMA((na,))],
        name="pair_exchange",
    )(*grads)


def _row_tile(rows, cols, align):
    best = align
    for cand in range(align, rows + 1, align):
        if rows % cand == 0 and cand * cols <= 256 * 1024:
            best = cand
    return best


def _pair_sum(g, got, place, transit, name):
    _, rows, cols = g.shape
    hr = rows // 2
    tr = _row_tile(hr, cols, _align_of(transit))
    nt = hr // tr

    def body(p_ref, g_ref, r_ref, s_ref, own_ref):
        total = g_ref[...] + r_ref[...]
        s_ref[...] = total.astype(transit)

        @pl.when(pl.program_id(1) == p_ref[1])
        def _():
            own_ref[...] = total

    grid_spec = pltpu.PrefetchScalarGridSpec(
        num_scalar_prefetch=1, grid=(nt, 4),
        in_specs=[pl.BlockSpec((None, tr, cols), lambda i, k, p: (k, p[0] * nt + i, 0)),
                  pl.BlockSpec((None, tr, cols), lambda i, k, p: (k, i, 0))],
        out_specs=[pl.BlockSpec((None, tr, cols), lambda i, k, p: (k, i, 0)),
                   pl.BlockSpec((tr, cols), lambda i, k, p: (i, 0))])
    return pl.pallas_call(
        body, grid_spec=grid_spec,
        out_shape=[jax.ShapeDtypeStruct((4, hr, cols), transit), jax.ShapeDtypeStruct((hr, cols), F32)],
        name=name, compiler_params=_params(("parallel", "arbitrary")),
    )(place, g, got)


def _chip_exchange(sums):
    na = len(sums)

    def body(*refs):
        s_refs, o_refs = refs[:na], refs[na:2 * na]
        send_sems, recv_sems = refs[2 * na:]
        x, y, c, chips = _place()
        for ai in range(na):
            _, rows, cols = s_refs[ai].shape
            row_bytes = cols * jnp.dtype(s_refs[ai].dtype).itemsize
            for r0, nr in _row_chunks(rows, row_bytes, _align_of(s_refs[ai].dtype)):
                for j, chip in enumerate(chips):
                    _remote(s_refs[ai].at[_chip_index(*chip), pl.ds(r0, nr), :], o_refs[ai].at[j, pl.ds(r0, nr), :],
                            send_sems.at[ai, j], recv_sems.at[ai, j], (*chip, c)).start()
        for ai in range(na):
            for j, chip in enumerate(chips):
                _remote(o_refs[ai].at[j], o_refs[ai].at[j], send_sems.at[ai, j], recv_sems.at[ai, j],
                        (*chip, c)).wait()

    return pl.pallas_call(
        body, in_specs=[ANY] * na, out_specs=[ANY] * na,
        out_shape=[jax.ShapeDtypeStruct((3, *s.shape[1:]), s.dtype) for s in sums],
        scratch_shapes=[pltpu.SemaphoreType.DMA((na, 3)), pltpu.SemaphoreType.DMA((na, 3))],
        name="chip_exchange",
    )(*sums)


def _chip_sum(own, landed, name):
    hr, cols = own.shape
    tr = _row_tile(hr, cols, _align_of(landed.dtype))

    def body(o_ref, l_ref, f_ref):
        acc = o_ref[...]
        for j in range(3):
            acc = acc + l_ref[j].astype(F32)
        f_ref[...] = acc

    return pl.pallas_call(
        body, grid=(hr // tr,),
        in_specs=[pl.BlockSpec((tr, cols), lambda i: (i, 0)), pl.BlockSpec((3, tr, cols), lambda i: (0, i, 0))],
        out_specs=pl.BlockSpec((tr, cols), lambda i: (i, 0)),
        out_shape=jax.ShapeDtypeStruct((hr, cols), F32), name=name,
        compiler_params=_params(("parallel",)),
    )(own, landed)


def _final_exchange(halves, small):
    nh = len(halves)

    def body(*refs):
        h_refs, s_ref = refs[:nh], refs[nh]
        o_refs, so_ref = refs[nh + 1:2 * nh + 1], refs[2 * nh + 1]
        send_sems, recv_sems, local_sem, ssend_sems, srecv_sems = refs[2 * nh + 2:]
        x, y, c, _ = _place()
        me = 4 * x + 2 * y + c
        sibling = (x, y, 1 - c)
        for hi in range(nh):
            hr, cols = h_refs[hi].shape
            for r0, nr in _row_chunks(hr, cols * 4, SUBLANES):
                _remote(h_refs[hi].at[pl.ds(r0, nr), :], o_refs[hi].at[pl.ds(r0, nr), :],
                        send_sems.at[hi], recv_sems.at[hi], sibling).start()
        small_cps = [pltpu.make_async_copy(s_ref, so_ref.at[me], local_sem)]
        for r in range(1, 8):
            fx, fy, fc = (r >> 2) & 1, (r >> 1) & 1, r & 1
            peer = (1 - x if fx else x, 1 - y if fy else y, 1 - c if fc else c)
            small_cps.append(_remote(s_ref, so_ref.at[me], ssend_sems.at[r - 1], srecv_sems.at[r - 1], peer))
        for cp in small_cps:
            cp.start()
        for hi in range(nh):
            _remote(h_refs[hi], o_refs[hi], send_sems.at[hi], recv_sems.at[hi], sibling).wait()
        for cp in small_cps:
            cp.wait()

    return pl.pallas_call(
        body, in_specs=[ANY] * (nh + 1), out_specs=[ANY] * (nh + 1),
        out_shape=[jax.ShapeDtypeStruct(h.shape, F32) for h in halves]
        + [jax.ShapeDtypeStruct((8, *small.shape), F32)],
        scratch_shapes=[pltpu.SemaphoreType.DMA((nh,)), pltpu.SemaphoreType.DMA((nh,)),
                        pltpu.SemaphoreType.DMA, pltpu.SemaphoreType.DMA((7,)), pltpu.SemaphoreType.DMA((7,))],
        name="final_exchange",
    )(*halves, small)


def _adamw(w, g, m, v, name):
    shape = w.shape
    n = w.size
    if w.ndim >= 2 and shape[-1] >= LANES:
        two_d = (n // shape[-1], shape[-1])
    elif n % LANES == 0:
        two_d = (n // LANES, LANES)
    else:
        two_d = (1, n)
    r, c = two_d
    tr = r
    for cand in (512, 256, 176, 128, 64):
        if r > cand and r % cand == 0 and cand * c <= 256 * 1024:
            tr = cand
            break
    c1 = 1.0 - ADAM_B1 ** ADAM_STEP
    c2 = 1.0 - ADAM_B2 ** ADAM_STEP

    def body(w_ref, g_ref, m_ref, v_ref, d_ref, nm_ref, nv_ref):
        gv = g_ref[...]
        nm = ADAM_B1 * m_ref[...] + (1.0 - ADAM_B1) * gv
        nv = ADAM_B2 * v_ref[...] + (1.0 - ADAM_B2) * (gv * gv)
        d_ref[...] = -ADAM_LR * ((nm / c1) / (jnp.sqrt(nv / c2) + ADAM_EPS) + ADAM_WD * w_ref[...])
        nm_ref[...] = nm
        nv_ref[...] = nv

    spec = pl.BlockSpec((tr, c), lambda i: (i, 0))
    out = jax.ShapeDtypeStruct((r, c), F32)
    d, nm, nv = pl.pallas_call(
        body, grid=(r // tr,), in_specs=[spec] * 4, out_specs=[spec] * 3, out_shape=[out] * 3, name=name,
        compiler_params=_params(("parallel",)),
    )(w.reshape(two_d), g.reshape(two_d), m.reshape(two_d), v.reshape(two_d))
    return d.reshape(shape), nm.reshape(shape), nv.reshape(shape)


def _adamw_many(ws, gs, ms, vs, name):
    n = len(ws)
    c1 = 1.0 - ADAM_B1 ** ADAM_STEP
    c2 = 1.0 - ADAM_B2 ** ADAM_STEP

    def body(*refs):
        w_refs, g_refs, m_refs, v_refs = (refs[k * n:(k + 1) * n] for k in range(4))
        d_refs, nm_refs, nv_refs = (refs[(4 + k) * n:(5 + k) * n] for k in range(3))
        for i in range(n):
            gv = g_refs[i][...]
            nm = ADAM_B1 * m_refs[i][...] + (1.0 - ADAM_B1) * gv
            nv = ADAM_B2 * v_refs[i][...] + (1.0 - ADAM_B2) * (gv * gv)
            d_refs[i][...] = -ADAM_LR * ((nm / c1) / (jnp.sqrt(nv / c2) + ADAM_EPS) + ADAM_WD * w_refs[i][...])
            nm_refs[i][...] = nm
            nv_refs[i][...] = nv

    vmem = pl.BlockSpec(memory_space=pltpu.VMEM)
    shapes = [jax.ShapeDtypeStruct(t.shape, F32) for t in ws]
    outs = pl.pallas_call(body, in_specs=[vmem] * (4 * n), out_specs=[vmem] * (3 * n), out_shape=shapes * 3,
                          name=name, compiler_params=_params(vmem_mb=56))(*ws, *gs, *ms, *vs)
    return outs[:n], outs[n:2 * n], outs[2 * n:]


BIG = ("w_in", "w_glu", "w_out", "w_up", "w_down")
WEIGHTS = ("norm_mix_g", "w_in", "a_re", "a_im", "log_step", "b_re", "b_im", "c_re", "c_im", "d_skip", "w_glu",
           "sink", "norm_attn_g", "norm_ssm_g", "w_out", "norm_ffn_g", "w_up", "conv_w", "conv_b", "w_down",
           "norm_final_g")
SMALL = ("norm_mix_g", "a_re", "a_im", "log_step", "b_re", "b_im", "c_re", "c_im", "d_skip", "sink",
         "norm_attn_g", "norm_ssm_g", "norm_ffn_g", "conv_w", "conv_b", "norm_final_g")
SMALL_ROWS = 40
N_DEV = 8


def _by_owner(name, g):
    if name == "w_up":
        return g
    if name == "w_in":
        return g.reshape(g.shape[0], 4, g.shape[1] // 4).transpose(1, 0, 2)
    return g.reshape(4, g.shape[0] // 4, g.shape[1])


def kernel(x, norm_mix_g, w_in, a_re, a_im, log_step, b_re, b_im, c_re, c_im, d_skip, w_glu, sink, norm_attn_g, norm_ssm_g, w_out, norm_ffn_g, w_up, conv_w, conv_b, w_down, norm_final_g, loss_target, m_norm_mix_g, m_w_in, m_a_re, m_a_im, m_log_step, m_b_re, m_b_im, m_c_re, m_c_im, m_d_skip, m_w_glu, m_sink, m_norm_attn_g, m_norm_ssm_g, m_w_out, m_norm_ffn_g, m_w_up, m_conv_w, m_conv_b, m_w_down, m_norm_final_g, v_norm_mix_g, v_w_in, v_a_re, v_a_im, v_log_step, v_b_re, v_b_im, v_c_re, v_c_im, v_d_skip, v_w_glu, v_sink, v_norm_attn_g, v_norm_ssm_g, v_w_out, v_norm_ffn_g, v_w_up, v_conv_w, v_conv_b, v_w_down, v_norm_final_g):
    given = dict(locals())
    w = {n: given[n] for n in WEIGHTS}
    m = {n: given["m_" + n] for n in WEIGHTS}
    v = {n: given["v_" + n] for n in WEIGHTS}
    xy = 2 * lax.axis_index("x") + lax.axis_index("y")

    core = lax.axis_index("c")
    place = jnp.stack([core, xy]).astype(jnp.int32)

    conv_rows = jnp.pad(w["conv_w"][0], ((0, 2 * SUBLANES - 3), (0, 0)))
    early = ("w_in", "w_glu", "w_out")
    *gathered, conv_all = _gather_weights([w[n][0] for n in early] + [conv_rows], [BF16] * len(early) + [F32])
    late = ("w_up", "w_down")
    send_sems, recv_sems, flying, token = _spread_start(
        [_cast_place(w[n][0], place, BF16, conv_all, "cast_" + n) for n in late])
    rows = lambda t: t.reshape(4 * t.shape[1], t.shape[2])
    wb = {"w_in": gathered[0].transpose(1, 0, 2).reshape(D_MODEL, IN_WIDTH), "w_glu": rows(gathered[1]),
          "w_out": rows(gathered[2])}

    def late_weights(after):
        w_up4, w_down4 = _spread_wait(send_sems, recv_sems, flying, after)
        return {"w_up": w_up4, "w_down": rows(w_down4)}

    sp = {n: w[n][0] for n in ("a_re", "a_im", "log_step", "b_re", "b_im", "c_re", "c_im", "d_skip",
                               "norm_mix_g", "norm_attn_g", "norm_ssm_g", "norm_ffn_g", "sink", "conv_b")}
    for n in ("norm_mix_g", "norm_attn_g", "norm_ssm_g", "norm_ffn_g", "sink", "conv_b"):
        sp[n] = sp[n].reshape(1, -1)
    sp["norm_mix_g"] = sp["norm_mix_g"] + token[:1, :1]
    sp["conv_w"] = conv_all[:, :3].transpose(1, 0, 2).reshape(3, 2 * D_FF)
    sp["norm_final_g"] = w["norm_final_g"]
    flight = {}

    def ffn_grads_ready(dw_up, dw_down):
        *flight["pair"], token = _pair_start([dw_up, _by_owner("w_down", dw_down)])
        return token[:1, :1]

    def ffn_grads_next(after):
        mine, got = _pair_wait(*flight["pair"], after)
        sums, flight["own"] = zip(*[_pair_sum(a, b, place, BF16, "pair_sum_" + n) for n, a, b in zip(late, mine, got)])
        *flight["chip"], token = _chip_start(list(sums))
        return token[:1, :1]

    loss, grad_x, g = _local_step(x[0], loss_target[0], wb, sp, late_weights, ffn_grads_ready, ffn_grads_next)

    flat = jnp.concatenate([g[n].reshape(-1) for n in SMALL] + [loss.reshape(-1)])
    pad = N_DEV * SMALL_ROWS * D_MODEL - flat.shape[0]
    small = jnp.concatenate([flat, jnp.zeros((pad,), F32)]).reshape(4, 2 * SMALL_ROWS, D_MODEL)
    by_owner = [_by_owner(n, g[n]) for n in early] + [small]
    got = _pair_exchange(by_owner)
    transit = [BF16] * len(early) + [F32]
    chip_sums, own_sums = zip(*[_pair_sum(a, b, place, t, "pair_sum_" + n)
                                for n, a, b, t in zip(early + ("small",), by_owner, got, transit)])
    landed = _chip_exchange(list(chip_sums))
    halves = {n: _chip_sum(o, t, "chip_sum_" + n) for n, o, t in zip(early + ("small",), own_sums, landed)}
    late_landed = _chip_wait(*flight["chip"], grad_x)
    for n, o, t in zip(late, flight["own"], late_landed):
        halves[n] = _chip_sum(o, t, "chip_sum_" + n)
    *others, small_all = _final_exchange([halves[n] for n in BIG], halves["small"])
    grads = {n: jnp.concatenate([jnp.where(core == 0, halves[n], o), jnp.where(core == 0, o, halves[n])], axis=0)
             for n, o in zip(BIG, others)}
    flat = small_all.reshape(-1)
    off = 0
    for n in SMALL:
        shape = (3, 4 * w[n].shape[-1]) if n == "conv_w" else w[n].shape[1:] if n != "norm_final_g" else w[n].shape
        size = math.prod(shape)
        grads[n] = flat[off:off + size].reshape(shape)
        off += size
    loss = flat[off]
    cw = w["conv_w"].shape[-1]
    grads["conv_w"] = lax.dynamic_slice_in_dim(grads["conv_w"], xy * cw, cw, axis=1)
    grads = {n: grads[n].reshape(w[n].shape) for n in WEIGHTS}

    delta, new_m, new_v = {}, {}, {}
    for n in BIG:
        delta[n], new_m[n], new_v[n] = _adamw(w[n], grads[n], m[n], v[n], "adamw_" + n)
    for group, name in ((("b_re", "b_im"), "adamw_b"), (tuple(n for n in SMALL if n not in ("b_re", "b_im")), "adamw_small")):
        row = lambda t: t.reshape(1, -1) if t.ndim == 1 else t
        d_, m_, v_ = _adamw_many(*[[row(t[n]) for n in group] for t in (w, grads, m, v)], name)
        for n, dn, mn, vn in zip(group, d_, m_, v_):
            delta[n], new_m[n], new_v[n] = (t.reshape(w[n].shape) for t in (dn, mn, vn))
    return (loss, grad_x[None], *[grads[n] for n in WEIGHTS], *[delta[n] for n in WEIGHTS],
            *[new_m[n] for n in WEIGHTS], *[new_v[n] for n in WEIGHTS])
```

```python
import functools
import math

import jax
import jax.numpy as jnp
from jax import lax
from jax.experimental import pallas as pl
from jax.experimental.pallas import tpu as pltpu

F32 = jnp.float32
BF16 = jnp.bfloat16

D_MODEL = 1024
N_Q_HEADS = 8
N_KV_HEADS = 2
HEAD_DIM = 64
ATTN_WIDTH = 512
KV_WIDTH = 128
QKV_WIDTH = ATTN_WIDTH + 2 * KV_WIDTH
WINDOW = 128
BLOCK = 128
ROPE_DIM = 16
ROPE_THETA = 500000.0
SSM_WIDTH = 512
SSM_GROUP = 16
N_SSM_GROUPS = 32
SSM_STATE = 64
IN_WIDTH = 1280
D_FF = 2816
EPS = 1e-6
ADAM_LR = 0.001
ADAM_B1 = 0.9
ADAM_B2 = 0.999
ADAM_EPS = 1e-08
ADAM_WD = 0.01
ADAM_STEP = 10

VMEM_BYTES_V7X = 64 * 1024 * 1024
SUBLANES = 8
LANES = 128
SSM_CB = 4
SSM_CH = 128
SSM_ST = 512
N_SEG = SUBLANES

NN = (((1,), (0,)), ((), ()))
NT = (((1,), (1,)), ((), ()))
TN = (((0,), (0,)), ((), ()))


def _params(sem=None, vmem_mb=48):
    return pltpu.CompilerParams(dimension_semantics=sem, vmem_limit_bytes=vmem_mb * 1024 * 1024)


def _dg(a, b, dims):
    return lax.dot_general(a, b, dims, preferred_element_type=F32)


def _sigmoid(x):
    return 1.0 / (1.0 + jnp.exp(-x))


_SQRT_HALF = 0.7071067811865476
_INV_SQRT_2PI = 0.3989422804014327


def _gelu(x):
    return 0.5 * x * (1.0 + lax.erf(x * _SQRT_HALF))


def _gelu_grad(x):
    return 0.5 * (1.0 + lax.erf(x * _SQRT_HALF)) + x * (_INV_SQRT_2PI * jnp.exp(-0.5 * x * x))


def _mm_nt(a, b, tm, tn, out_dtype, name):
    m, k = a.shape
    n = b.shape[0]

    def body(a_ref, b_ref, o_ref):
        o_ref[...] = _dg(a_ref[...], b_ref[...], NT).astype(out_dtype)

    return pl.pallas_call(
        body, grid=(m // tm, n // tn),
        in_specs=[pl.BlockSpec((tm, k), lambda i, j: (i, 0)), pl.BlockSpec((tn, k), lambda i, j: (j, 0))],
        out_specs=pl.BlockSpec((tm, tn), lambda i, j: (i, j)),
        out_shape=jax.ShapeDtypeStruct((m, n), out_dtype), name=name,
        compiler_params=_params(("parallel", "parallel")),
    )(a, b)


def _mm_tn(a, b, tm, tn, name):
    k, m = a.shape
    n = b.shape[1]

    def body(a_ref, b_ref, o_ref):
        o_ref[...] = _dg(a_ref[...], b_ref[...], TN)

    return pl.pallas_call(
        body, grid=(m // tm, n // tn),
        in_specs=[pl.BlockSpec((k, tm), lambda i, j: (0, i)), pl.BlockSpec((k, tn), lambda i, j: (0, j))],
        out_specs=pl.BlockSpec((tm, tn), lambda i, j: (i, j)),
        out_shape=jax.ShapeDtypeStruct((m, n), F32), name=name,
        compiler_params=_params(("parallel", "parallel")),
    )(a, b)


def _mm_nn_cols(a, b4, tm, name):
    m, k = a.shape
    s, _, n = b4.shape

    def body(a_ref, b_ref, o_ref):
        o_ref[...] = _dg(a_ref[...], b_ref[...], NN)

    return pl.pallas_call(
        body, grid=(m // tm, s),
        in_specs=[pl.BlockSpec((tm, k), lambda i, j: (i, 0)), pl.BlockSpec((None, k, n), lambda i, j: (j, 0, 0))],
        out_specs=pl.BlockSpec((tm, n), lambda i, j: (i, j)),
        out_shape=jax.ShapeDtypeStruct((m, s * n), F32), name=name,
        compiler_params=_params(("parallel", "parallel")),
    )(a, b4)


def _mm_tn_cols(a, b2, s, tm, name):
    k, m = a.shape
    h, _, wide = b2.shape
    per = s // h
    n = wide // per

    def body(a_ref, b_ref, o_ref):
        o_ref[...] = _dg(a_ref[...], b_ref[...], TN)

    return pl.pallas_call(
        body, grid=(s, m // tm),
        in_specs=[pl.BlockSpec((k, tm), lambda j, i: (0, i)),
                  pl.BlockSpec((None, k, n), lambda j, i: (j // per, 0, j % per))],
        out_specs=pl.BlockSpec((None, tm, n), lambda j, i: (j, i, 0)),
        out_shape=jax.ShapeDtypeStruct((s, m, n), F32), name=name,
        compiler_params=_params(("parallel", "parallel")),
    )(a, b2)


TM_EW = 256


def _rms_bwd_vals(xv, gv, dy):
    r = lax.rsqrt(jnp.mean(xv * xv, axis=-1, keepdims=True) + EPS)
    xh = xv * r
    dxh = dy * gv
    dx = r * (dxh - xh * jnp.mean(dxh * xh, axis=-1, keepdims=True))
    return dx, dy * xh


TM_FUSED = 256


def _rms_vals(xv, gv):
    return xv * lax.rsqrt(jnp.mean(xv * xv, axis=-1, keepdims=True) + EPS) * gv


def _rms_mm_split(x, g, w, split, name):
    l, d = x.shape
    n = w.shape[1]

    def body(x_ref, g_ref, w_ref, h_ref, lo_ref, hi_ref):
        h = _rms_vals(x_ref[...], g_ref[...]).astype(BF16)
        h_ref[...] = h
        out = _dg(h, w_ref[...], NN)
        lo_ref[...] = out[:, :split]
        hi_ref[...] = out[:, split:]

    row = lambda width: pl.BlockSpec((TM_FUSED, width), lambda i: (i, 0))
    return pl.pallas_call(
        body, grid=(l // TM_FUSED,),
        in_specs=[row(d), pl.BlockSpec((1, d), lambda i: (0, 0)), pl.BlockSpec((d, n), lambda i: (0, 0))],
        out_specs=[row(d), row(split), row(n - split)],
        out_shape=[jax.ShapeDtypeStruct((l, d), BF16), jax.ShapeDtypeStruct((l, split), F32),
                   jax.ShapeDtypeStruct((l, n - split), F32)],
        name=name, compiler_params=_params(("parallel",)),
    )(x, g, w)


def _mm_res_rms(a, b, res, g, name):
    l, k = a.shape
    d = b.shape[1]

    def body(a_ref, b_ref, r_ref, g_ref, x_ref, h_ref):
        xv = r_ref[...] + _dg(a_ref[...], b_ref[...], NN)
        x_ref[...] = xv
        h_ref[...] = _rms_vals(xv, g_ref[...]).astype(BF16)

    row = lambda width: pl.BlockSpec((TM_FUSED, width), lambda i: (i, 0))
    return pl.pallas_call(
        body, grid=(l // TM_FUSED,),
        in_specs=[row(k), pl.BlockSpec((k, d), lambda i: (0, 0)), row(d), pl.BlockSpec((1, d), lambda i: (0, 0))],
        out_specs=[row(d), row(d)],
        out_shape=[jax.ShapeDtypeStruct((l, d), F32), jax.ShapeDtypeStruct((l, d), BF16)],
        name=name, compiler_params=_params(("parallel",)),
    )(a, b, res, g)


def _mm_res_loss(a, b, res, g, target):
    l, k = a.shape
    d = b.shape[1]

    def body(a_ref, b_ref, r_ref, g_ref, t_ref, loss_ref, dx_ref, dxb_ref, dg_ref):
        xv = r_ref[...] + _dg(a_ref[...], b_ref[...], NN)
        gv = g_ref[...]
        r = lax.rsqrt(jnp.mean(xv * xv, axis=-1, keepdims=True) + EPS)
        xh = xv * r
        e = xh * gv - t_ref[...]
        part = jnp.sum(jnp.sum(e * e, axis=1, keepdims=True), axis=0, keepdims=True) * (0.5 / d)
        dy = e * (1.0 / d)
        dxh = dy * gv
        dx = r * (dxh - xh * jnp.mean(dxh * xh, axis=-1, keepdims=True))
        dx_ref[...] = dx
        dxb_ref[...] = dx.astype(BF16)

        @pl.when(pl.program_id(0) == 0)
        def _():
            dg_ref[...] = jnp.zeros_like(dg_ref)
            loss_ref[...] = jnp.zeros_like(loss_ref)

        dg_ref[...] += jnp.sum(dy * xh, axis=0, keepdims=True)
        loss_ref[...] += part

    row = lambda width: pl.BlockSpec((TM_FUSED, width), lambda i: (i, 0))
    vec = pl.BlockSpec((1, d), lambda i: (0, 0))
    return pl.pallas_call(
        body, grid=(l // TM_FUSED,),
        in_specs=[row(k), pl.BlockSpec((k, d), lambda i: (0, 0)), row(d), vec, row(d)],
        out_specs=[pl.BlockSpec((1, 1), lambda i: (0, 0)), row(d), row(d), vec],
        out_shape=[jax.ShapeDtypeStruct((1, 1), F32), jax.ShapeDtypeStruct((l, d), F32),
                   jax.ShapeDtypeStruct((l, d), BF16), jax.ShapeDtypeStruct((1, d), F32)],
        name="mm_down_loss", compiler_params=_params(("arbitrary",)),
    )(a, b, res, g, target)


def _mm_rms_bwd(a, b, a_spec, b_spec, matmul, x, g, res, name):
    l, d = x.shape

    def body(a_ref, b_ref, x_ref, g_ref, res_ref, dx_ref, dxb_ref, dg_ref):
        dx, dgr = _rms_bwd_vals(x_ref[...], g_ref[...], matmul(a_ref, b_ref))
        dx = dx + res_ref[...]
        dx_ref[...] = dx
        dxb_ref[...] = dx.astype(BF16)

        @pl.when(pl.program_id(0) == 0)
        def _():
            dg_ref[...] = jnp.zeros_like(dg_ref)

        dg_ref[...] += jnp.sum(dgr, axis=0, keepdims=True)

    row = pl.BlockSpec((TM_FUSED, d), lambda i: (i, 0))
    vec = pl.BlockSpec((1, d), lambda i: (0, 0))
    return pl.pallas_call(
        body, grid=(l // TM_FUSED,), in_specs=[a_spec, b_spec, row, vec, row], out_specs=[row, row, vec],
        out_shape=[jax.ShapeDtypeStruct((l, d), F32), jax.ShapeDtypeStruct((l, d), BF16),
                   jax.ShapeDtypeStruct((1, d), F32)],
        name=name, compiler_params=_params(("arbitrary",)),
    )(a, b, x, g, res)


def _mm_nt_rms_bwd(a, b, x, g, res, name):
    return _mm_rms_bwd(a, b, pl.BlockSpec((TM_FUSED, a.shape[1]), lambda i: (i, 0)),
                       pl.BlockSpec(b.shape, lambda i: (0, 0)),
                       lambda a_ref, b_ref: _dg(a_ref[...], b_ref[...], NT), x, g, res, name)


def _mm_cols_rms_bwd(a2, b4, x, g, res, name):
    h, _, wide = a2.shape
    s, _, n = b4.shape
    per = s // h

    def matmul(a_ref, b_ref):
        acc = None
        for j in range(s):
            part = _dg(a_ref[j // per, :, (j % per) * n:(j % per + 1) * n], b_ref[j], NT)
            acc = part if acc is None else acc + part
        return acc

    return _mm_rms_bwd(a2, b4, pl.BlockSpec((h, TM_FUSED, wide), lambda i: (0, i, 0)),
                       pl.BlockSpec(b4.shape, lambda i: (0, 0, 0)), matmul, x, g, res, name)


def _mix_fwd(attn, ys, g_attn, g_ssm):
    l, w = attn.shape

    def body(a_ref, y_ref, ga_ref, gs_ref, o_ref):
        for src, gr, off in ((a_ref, ga_ref, 0), (y_ref, gs_ref, w)):
            xv = src[...]
            r = lax.rsqrt(jnp.mean(xv * xv, axis=-1, keepdims=True) + EPS)
            o_ref[:, off:off + w] = (xv * r * gr[...]).astype(BF16)

    row = pl.BlockSpec((TM_EW, w), lambda i: (i, 0))
    vec = pl.BlockSpec((1, w), lambda i: (0, 0))
    return pl.pallas_call(
        body, grid=(l // TM_EW,), in_specs=[row, row, vec, vec],
        out_specs=pl.BlockSpec((TM_EW, 2 * w), lambda i: (i, 0)),
        out_shape=jax.ShapeDtypeStruct((l, 2 * w), BF16), name="mix_fwd",
        compiler_params=_params(("parallel",)),
    )(attn, ys, g_attn, g_ssm)


def _mix_bwd(attn, ys, g_attn, g_ssm, dmixed):
    l, w = attn.shape

    def body(a_ref, y_ref, ga_ref, gs_ref, dm_ref, da_ref, dy_ref, dga_ref, dgs_ref):
        @pl.when(pl.program_id(0) == 0)
        def _():
            dga_ref[...] = jnp.zeros_like(dga_ref)
            dgs_ref[...] = jnp.zeros_like(dgs_ref)

        for src, gr, off, dst, dgr in ((a_ref, ga_ref, 0, da_ref, dga_ref), (y_ref, gs_ref, w, dy_ref, dgs_ref)):
            dx, dg_rows = _rms_bwd_vals(src[...], gr[...], dm_ref[:, off:off + w])
            dst[...] = dx
            dgr[...] += jnp.sum(dg_rows, axis=0, keepdims=True)

    row = pl.BlockSpec((TM_EW, w), lambda i: (i, 0))
    vec = pl.BlockSpec((1, w), lambda i: (0, 0))
    return pl.pallas_call(
        body, grid=(l // TM_EW,),
        in_specs=[row, row, vec, vec, pl.BlockSpec((TM_EW, 2 * w), lambda i: (i, 0))],
        out_specs=[row, row, vec, vec],
        out_shape=[jax.ShapeDtypeStruct((l, w), F32), jax.ShapeDtypeStruct((l, w), F32),
                   jax.ShapeDtypeStruct((1, w), F32), jax.ShapeDtypeStruct((1, w), F32)],
        name="mix_bwd", compiler_params=_params(("arbitrary",)),
    )(attn, ys, g_attn, g_ssm, dmixed)


def _rope_tables(l):
    half = ROPE_DIM // 2
    inv_freq = jnp.power(ROPE_THETA, -jnp.arange(half, dtype=F32) / half)
    ang = jnp.arange(l, dtype=F32)[:, None] * inv_freq[None, :]
    cos, sin = jnp.cos(ang), jnp.sin(ang)
    ones = jnp.ones((l, HEAD_DIM - ROPE_DIM), F32)
    zeros = jnp.zeros((l, HEAD_DIM - ROPE_DIM), F32)
    zh = jnp.zeros((l, half), F32)
    c = jnp.concatenate([cos, cos, ones], axis=1)
    s_lo = jnp.concatenate([-sin, zh, zeros], axis=1)
    s_hi = jnp.concatenate([zh, sin, zeros], axis=1)
    return tuple(jnp.tile(t, (1, LANES // HEAD_DIM)) for t in (c, s_lo, s_hi))


def _rope_fwd(proj, tabs):
    l = proj.shape[0]
    nq = ATTN_WIDTH // LANES

    def body(p_ref, c_ref, lo_ref, hi_ref, o_ref):
        c, lo, hi = c_ref[...], lo_ref[...], hi_ref[...]
        for blk in range(nq + 1):
            t = p_ref[:, blk * LANES:(blk + 1) * LANES]
            rot = t * c + pltpu.roll(t, LANES - 8, 1) * lo + pltpu.roll(t, 8, 1) * hi
            o_ref[:, blk * LANES:(blk + 1) * LANES] = rot.astype(BF16)
        o_ref[:, (nq + 1) * LANES:] = p_ref[:, (nq + 1) * LANES:].astype(BF16)

    tab = pl.BlockSpec((TM_EW, LANES), lambda i: (i, 0))
    return pl.pallas_call(
        body, grid=(l // TM_EW,),
        in_specs=[pl.BlockSpec((TM_EW, QKV_WIDTH), lambda i: (i, 0)), tab, tab, tab],
        out_specs=pl.BlockSpec((TM_EW, QKV_WIDTH), lambda i: (i, 0)),
        out_shape=jax.ShapeDtypeStruct((l, QKV_WIDTH), BF16), name="rope_fwd",
        compiler_params=_params(("parallel",)),
    )(proj, *tabs)


def _rope_bwd(dqkv, du_ssm, dpre, d_skip, tabs):
    l = dqkv.shape[0]
    nq = ATTN_WIDTH // LANES

    def body(d_ref, du_ref, dpre_ref, ds_ref, c_ref, lo_ref, hi_ref, o_ref):
        c, lo, hi = c_ref[...], lo_ref[...], hi_ref[...]
        for blk in range(nq + 1):
            t = d_ref[:, blk * LANES:(blk + 1) * LANES]
            g = t * c + pltpu.roll(t * lo, 8, 1) + pltpu.roll(t * hi, LANES - 8, 1)
            o_ref[:, blk * LANES:(blk + 1) * LANES] = g.astype(BF16)
        o_ref[:, (nq + 1) * LANES:QKV_WIDTH] = d_ref[:, (nq + 1) * LANES:].astype(BF16)
        o_ref[:, QKV_WIDTH:] = (du_ref[...] + dpre_ref[...] * ds_ref[...]).astype(BF16)

    tab = pl.BlockSpec((TM_EW, LANES), lambda i: (i, 0))
    wide = pl.BlockSpec((TM_EW, SSM_WIDTH), lambda i: (i, 0))
    return pl.pallas_call(
        body, grid=(l // TM_EW,),
        in_specs=[pl.BlockSpec((TM_EW, QKV_WIDTH), lambda i: (i, 0)), wide, wide,
                  pl.BlockSpec((1, SSM_WIDTH), lambda i: (0, 0)), tab, tab, tab],
        out_specs=pl.BlockSpec((TM_EW, IN_WIDTH), lambda i: (i, 0)),
        out_shape=jax.ShapeDtypeStruct((l, IN_WIDTH), BF16), name="rope_bwd",
        compiler_params=_params(("parallel",)),
    )(dqkv, du_ssm, dpre, d_skip, *tabs)


_Q_COLS = ATTN_WIDTH // LANES
_SCALE = HEAD_DIM ** -0.5
_NEG = -1e30


def _window_specs(nb, width, col):
    return [
        pl.BlockSpec((BLOCK, width), lambda n: (jnp.maximum(n - 1, 0), col)),
        pl.BlockSpec((BLOCK, width), lambda n: (n, col)),
        pl.BlockSpec((BLOCK, width), lambda n: (jnp.minimum(n + 1, nb - 1), col)),
    ]


def _stacked_sink(sink_ref, heads):
    rid = lax.broadcasted_iota(jnp.int32, (len(heads) * BLOCK, 1), 0)
    sk = jnp.full(rid.shape, sink_ref[0, heads[-1]], F32)
    for g in range(len(heads) - 2, -1, -1):
        sk = jnp.where(rid < (g + 1) * BLOCK, sink_ref[0, heads[g]], sk)
    return sk


def _attn_fwd(qkv, sink):
    l = qkv.shape[0]
    nb = l // BLOCK
    grp = N_Q_HEADS // N_KV_HEADS

    def body(sink_ref, q_ref, k0, k1, k2, v0, v1, v2, o_ref, lse_ref):
        n = pl.program_id(0)
        q = q_ref[...]
        kw = jnp.concatenate([k0[...], k1[...], k2[...]], axis=0)
        vw = jnp.concatenate([v0[...], v1[...], v2[...]], axis=0)
        row = lax.broadcasted_iota(jnp.int32, (grp * BLOCK, 3 * BLOCK), 0)
        col = lax.broadcasted_iota(jnp.int32, (grp * BLOCK, 3 * BLOCK), 1)
        valid = jnp.abs(col - BLOCK - (row & (BLOCK - 1))) <= WINDOW
        valid &= jnp.logical_not((n == 0) & (col < BLOCK))
        valid &= jnp.logical_not((n == nb - 1) & (col >= 2 * BLOCK))
        for hk in range(N_KV_HEADS):
            heads = range(hk * grp, (hk + 1) * grp)
            qs = jnp.concatenate([q[:, h * HEAD_DIM:(h + 1) * HEAD_DIM] for h in heads], axis=0)
            kh = kw[:, hk * HEAD_DIM:(hk + 1) * HEAD_DIM]
            vh = vw[:, hk * HEAD_DIM:(hk + 1) * HEAD_DIM]
            s = jnp.where(valid, _dg(qs, kh, NT) * _SCALE, _NEG)
            sk = _stacked_sink(sink_ref, heads)
            m = jnp.maximum(jnp.max(s, axis=1, keepdims=True), sk)
            p = jnp.exp(s - m)
            denom = jnp.sum(p, axis=1, keepdims=True) + jnp.exp(sk - m)
            o = _dg((p / denom).astype(BF16), vh, NN)
            lse = m + jnp.log(denom)
            for g, h in enumerate(heads):
                o_ref[:, h * HEAD_DIM:(h + 1) * HEAD_DIM] = o[g * BLOCK:(g + 1) * BLOCK]
                lse_ref[:, h:h + 1] = lse[g * BLOCK:(g + 1) * BLOCK]

    return pl.pallas_call(
        body, grid=(nb,),
        in_specs=[pl.BlockSpec(memory_space=pltpu.SMEM),
                  pl.BlockSpec((BLOCK, ATTN_WIDTH), lambda n: (n, 0))]
        + _window_specs(nb, KV_WIDTH, _Q_COLS) + _window_specs(nb, KV_WIDTH, _Q_COLS + 1),
        out_specs=[pl.BlockSpec((BLOCK, ATTN_WIDTH), lambda n: (n, 0)),
                   pl.BlockSpec((BLOCK, N_Q_HEADS), lambda n: (n, 0))],
        out_shape=[jax.ShapeDtypeStruct((l, ATTN_WIDTH), F32), jax.ShapeDtypeStruct((l, N_Q_HEADS), F32)],
        name="attn_fwd", compiler_params=_params(("parallel",)),
    )(sink, qkv, qkv, qkv, qkv, qkv, qkv, qkv)


def _attn_bwd(qkv, attn, dattn, lse, sink):
    l = qkv.shape[0]
    nb = l // BLOCK
    grp = N_Q_HEADS // N_KV_HEADS

    def body(sink_ref, q0, q1, q2, k0, k1, k2, v0, v1, v2, o0, o1, o2, d0, d1, d2,
             l0, l1, l2, dqkv_ref, dsink_ref):
        n = pl.program_id(0)
        first, last = n == 0, n == nb - 1

        @pl.when(first)
        def _():
            dsink_ref[...] = jnp.zeros_like(dsink_ref)

        cat = lambda a, b, c: jnp.concatenate([a[...], b[...], c[...]], axis=0)
        qw, kw, vw = cat(q0, q1, q2), cat(k0, k1, k2), cat(v0, v1, v2)
        dow = cat(d0, d1, d2)
        prodw = cat(o0, o1, o2) * dow
        lsew = cat(l0, l1, l2)
        dob = dow.astype(BF16)
        win = 3 * BLOCK
        mid = slice(BLOCK, 2 * BLOCK)

        row = lax.broadcasted_iota(jnp.int32, (grp * BLOCK, win), 0)
        col = lax.broadcasted_iota(jnp.int32, (grp * BLOCK, win), 1)
        valid_q = jnp.abs(col - BLOCK - (row & (BLOCK - 1))) <= WINDOW
        valid_q &= jnp.logical_not(first & (col < BLOCK))
        valid_q &= jnp.logical_not(last & (col >= 2 * BLOCK))
        rowk = lax.broadcasted_iota(jnp.int32, (grp * win, BLOCK), 0)
        colk = lax.broadcasted_iota(jnp.int32, (grp * win, BLOCK), 1)
        for g in range(1, grp):
            rowk = jnp.where(rowk >= win, rowk - win, rowk)
        valid_k = jnp.abs(colk + BLOCK - rowk) <= WINDOW
        valid_k &= jnp.logical_not(first & (rowk < BLOCK))
        valid_k &= jnp.logical_not(last & (rowk >= 2 * BLOCK))

        dsink_parts = []
        for hk in range(N_KV_HEADS):
            heads = range(hk * grp, (hk + 1) * grp)
            ksl = slice(hk * HEAD_DIM, (hk + 1) * HEAD_DIM)
            hsl = [slice(h * HEAD_DIM, (h + 1) * HEAD_DIM) for h in heads]
            stack = lambda parts: jnp.concatenate(parts, axis=0)
            qws = stack([qw[:, s_] for s_ in hsl])
            dows = stack([dob[:, s_] for s_ in hsl])
            deltaws = stack([jnp.sum(prodw[:, s_], axis=1, keepdims=True) for s_ in hsl])
            lsews = stack([lsew[:, h:h + 1] for h in heads])
            of_block = lambda t: stack([t[g * win + BLOCK:g * win + 2 * BLOCK] for g in range(grp)])
            qs, dos, deltas, lses = of_block(qws), of_block(dows), of_block(deltaws), of_block(lsews)
            kh, vh = kw[:, ksl], vw[:, ksl]
            s = jnp.where(valid_q, _dg(qs, kh, NT) * _SCALE, _NEG)
            p = jnp.exp(s - lses)
            dp = _dg(dos, vh, NT)
            ds = (p * (dp - deltas) * _SCALE).astype(BF16)
            dq = _dg(ds, kh, NN)
            sink_rows = jnp.exp(_stacked_sink(sink_ref, heads) - lses) * deltas
            for g, h in enumerate(heads):
                dqkv_ref[:, hsl[g]] = dq[g * BLOCK:(g + 1) * BLOCK]
                dsink_parts.append(jnp.sum(sink_rows[g * BLOCK:(g + 1) * BLOCK], axis=0, keepdims=True))
            s2 = jnp.where(valid_k, _dg(qws, kh[mid], NT) * _SCALE, _NEG)
            p2 = jnp.exp(s2 - lsews)
            dv = _dg(p2.astype(BF16), dows, TN)
            dp2 = _dg(dows, vh[mid], NT)
            ds2 = (p2 * (dp2 - deltaws) * _SCALE).astype(BF16)
            dk = _dg(ds2, qws, TN)
            dqkv_ref[:, ATTN_WIDTH + hk * HEAD_DIM:ATTN_WIDTH + (hk + 1) * HEAD_DIM] = dk
            dqkv_ref[:, ATTN_WIDTH + KV_WIDTH + hk * HEAD_DIM:ATTN_WIDTH + KV_WIDTH + (hk + 1) * HEAD_DIM] = dv
        dsink_ref[...] -= jnp.concatenate(dsink_parts, axis=1)

    return pl.pallas_call(
        body, grid=(nb,),
        in_specs=[pl.BlockSpec(memory_space=pltpu.SMEM)]
        + _window_specs(nb, ATTN_WIDTH, 0)
        + _window_specs(nb, KV_WIDTH, _Q_COLS) + _window_specs(nb, KV_WIDTH, _Q_COLS + 1)
        + _window_specs(nb, ATTN_WIDTH, 0) + _window_specs(nb, ATTN_WIDTH, 0)
        + _window_specs(nb, N_Q_HEADS, 0),
        out_specs=[pl.BlockSpec((BLOCK, QKV_WIDTH), lambda n: (n, 0)),
                   pl.BlockSpec((1, N_Q_HEADS), lambda n: (0, 0))],
        out_shape=[jax.ShapeDtypeStruct((l, QKV_WIDTH), F32), jax.ShapeDtypeStruct((1, N_Q_HEADS), F32)],
        name="attn_bwd", compiler_params=_params(("arbitrary",)),
    )(sink, qkv, qkv, qkv, qkv, qkv, qkv, qkv, qkv, qkv, attn, attn, attn,
      dattn, dattn, dattn, lse, lse, lse)


def _ssm_disc(a_re, a_im, log_step, b_re, b_im):
    step = jnp.exp(log_step)[..., None]
    mag = jnp.exp(a_re * step)
    lb_re, lb_im = mag * jnp.cos(a_im * step), mag * jnp.sin(a_im * step)
    nr, ni = lb_re - 1.0, lb_im
    den = a_re * a_re + a_im * a_im
    f_re = ((nr * a_re + ni * a_im) / den)[..., None]
    f_im = ((ni * a_re - nr * a_im) / den)[..., None]
    return lb_re, lb_im, f_re * b_re - f_im * b_im, f_re * b_im + f_im * b_re


def _ssm_pack(lb_re, lb_im, bb_re, bb_im, c_re, c_im):
    eye = jnp.eye(SSM_CH // SSM_GROUP, dtype=F32)
    ng = SSM_CH // SSM_GROUP

    def diag_b(bb):
        t = bb.reshape(2, SSM_CB, ng, SSM_STATE, SSM_GROUP)
        return jnp.einsum('dkgpc,gh->dkgchp', t, eye).reshape(2, SSM_CB, SSM_CH, SSM_ST)

    def diag_c(cc):
        t = cc.reshape(2, SSM_CB, ng, SSM_GROUP, SSM_STATE)
        return jnp.einsum('dkgcp,gh->dkhpgc', t, eye).reshape(2, SSM_CB, SSM_ST, SSM_CH)

    bcat = jnp.concatenate([diag_b(bb_re), diag_b(bb_im)], axis=-1)
    ccat = jnp.concatenate([diag_c(c_re), -diag_c(c_im)], axis=-2)
    lam_re = lb_re.reshape(2, SSM_CB, 1, SSM_ST)
    lam_im = lb_im.reshape(2, SSM_CB, 1, SSM_ST)
    return bcat, ccat, lam_re, lam_im


def _ssm_unpack(dbcat, dccat, dlam_re, dlam_im):
    ng = SSM_CH // SSM_GROUP
    eye = jnp.eye(ng, dtype=F32)

    def undiag_b(t):
        t = t.reshape(2, SSM_CB, ng, SSM_GROUP, ng, SSM_STATE)
        return jnp.einsum('dkgchp,gh->dkgpc', t, eye).reshape(2, N_SSM_GROUPS, SSM_STATE, SSM_GROUP)

    def undiag_c(t):
        t = t.reshape(2, SSM_CB, ng, SSM_STATE, ng, SSM_GROUP)
        return jnp.einsum('dkhpgc,gh->dkgcp', t, eye).reshape(2, N_SSM_GROUPS, SSM_GROUP, SSM_STATE)

    dbb_re, dbb_im = undiag_b(dbcat[..., :SSM_ST]), undiag_b(dbcat[..., SSM_ST:])
    dc_re, dc_im = undiag_c(dccat[:, :, :SSM_ST]), -undiag_c(dccat[:, :, SSM_ST:])
    shape = (2, N_SSM_GROUPS, SSM_STATE)
    return dlam_re.reshape(shape), dlam_im.reshape(shape), dbb_re, dbb_im, dc_re, dc_im


def _to_segments(t):
    l, w = t.shape
    return t.reshape(N_SEG, l // N_SEG, w).transpose(1, 0, 2).reshape(l, w)


def _from_segments(t):
    l, w = t.shape
    return t.reshape(l // N_SEG, N_SEG, w).transpose(1, 0, 2).reshape(l, w)


SCAN_UNROLL = 4


def _cfma(ar, ai, xr, xi, br, bi):
    return ar * xr - ai * xi + br, ar * xi + ai * xr + bi


def _scan_segments(xs_ref, ar, ai, rev, nj, prev_ref=None, before_sums=None):
    shape = (N_SEG, SSM_ST)
    ar = jnp.broadcast_to(ar, shape)
    ai = jnp.broadcast_to(ai, shape)
    zero = jnp.zeros(shape, F32)
    re_cols, im_cols = pl.ds(0, SSM_ST), pl.ds(SSM_ST, SSM_ST)

    def rows_of(jj):
        j = jnp.where(rev, nj - 1 - jj, jj)
        return j, pl.ds(pl.multiple_of(j * N_SEG, N_SEG), N_SEG)

    def steps(step, init, last_step=None):
        def outer(o, carry):
            for k in range(SCAN_UNROLL):
                carry = step(o * SCAN_UNROLL + k, carry)
            return carry

        carry = lax.fori_loop(0, nj // SCAN_UNROLL - 1, outer, init)
        for jj in range(nj - SCAN_UNROLL, nj):
            carry = (last_step if last_step is not None and jj == nj - 1 else step)(jj, carry)
        return carry

    def pass1(jj, carry):
        _, rows = rows_of(jj)
        return _cfma(ar, ai, carry[0], carry[1], xs_ref[rows, re_cols], xs_ref[rows, im_cols])

    end_r, end_i = steps(pass1, (zero, zero))

    pr, pi = ar, ai
    for _ in range(int(math.log2(nj))):
        pr, pi = pr * pr - pi * pi, 2.0 * pr * pi
    seg = lax.broadcasted_iota(jnp.int32, shape, 0)

    def chain(shift, keep):
        ir, ii = zero, zero
        for _ in range(N_SEG - 1):
            tr, ti = _cfma(pr, pi, ir, ii, end_r, end_i)
            ir = jnp.where(keep, pltpu.roll(tr, shift, 0), 0.0)
            ii = jnp.where(keep, pltpu.roll(ti, shift, 0), 0.0)
        return ir, ii

    up_r, up_i = chain(1, seg >= 1)
    dn_r, dn_i = chain(N_SEG - 1, seg <= N_SEG - 2)
    init_r, init_i = jnp.where(rev, dn_r, up_r), jnp.where(rev, dn_i, up_i)

    def pass2(jj, carry):
        j, rows = rows_of(jj)
        nr, ni = _cfma(ar, ai, carry[0], carry[1], xs_ref[rows, re_cols], xs_ref[rows, im_cols])
        xs_ref[rows, re_cols] = nr
        xs_ref[rows, im_cols] = ni
        return (j, nr, ni) + tuple(carry[2:])

    def pass2_plain(jj, carry):
        return pass2(jj, carry)[1:]

    def pass2_sums(jj, carry):
        j, nr, ni, acc_r, acc_i = pass2(jj, carry)
        jp = jnp.where(rev, j - 1, j + 1)
        prow = pl.ds(pl.multiple_of(jp * N_SEG, N_SEG), N_SEG)
        xr, xi = prev_ref[prow, re_cols], prev_ref[prow, im_cols]
        return nr, ni, acc_r + (nr * xr + ni * xi), acc_i + (ni * xr - nr * xi)

    if prev_ref is None:
        steps(pass2_plain, (init_r, init_i))
        return init_r, init_i, None, None
    if before_sums is not None:
        before_sums()
    _, _, acc_r, acc_i = steps(pass2_sums, (init_r, init_i, zero, zero), last_step=pass2_plain)
    return init_r, init_i, acc_r, acc_i


SSM_RC = 256


def _ssm_specs(l):
    act = pl.BlockSpec((l, SSM_CH), lambda k, d: (0, k))
    bmat = pl.BlockSpec((None, None, SSM_CH, 2 * SSM_ST), lambda k, d: (d, k, 0, 0))
    cmat = pl.BlockSpec((None, None, 2 * SSM_ST, SSM_CH), lambda k, d: (d, k, 0, 0))
    lam = pl.BlockSpec((None, None, 1, SSM_ST), lambda k, d: (d, k, 0, 0))
    return act, bmat, cmat, lam


def _ssm_fwd(u_seg, bcat, ccat, lam_re, lam_im):
    l = u_seg.shape[0]
    nj = l // N_SEG

    def body(u_ref, b_ref, c_ref, lr_ref, li_ref, y_ref, keep_ref, xs_ref, keep_sem):
        k, d = pl.program_id(0), pl.program_id(1)

        def bu_chunk(i, _):
            rows = pl.ds(pl.multiple_of(i * SSM_RC, SSM_RC), SSM_RC)
            xs_ref[rows, :] = _dg(u_ref[rows, :], b_ref[...], NN)
            return 0

        lax.fori_loop(0, l // SSM_RC, bu_chunk, 0)
        _scan_segments(xs_ref, lr_ref[...], li_ref[...], d == 1, nj)
        keep = pltpu.make_async_copy(xs_ref, keep_ref.at[d, k], keep_sem)
        keep.start()

        def y_chunk(i, _):
            rows = pl.ds(pl.multiple_of(i * SSM_RC, SSM_RC), SSM_RC)
            yv = _dg(xs_ref[rows, :].astype(BF16), c_ref[...], NN)

            @pl.when(d == 0)
            def _():
                y_ref[rows, :] = yv

            @pl.when(d == 1)
            def _():
                y_ref[rows, :] += yv

            return 0

        lax.fori_loop(0, l // SSM_RC, y_chunk, 0)
        keep.wait()

    act, bmat, cmat, lam = _ssm_specs(l)
    return pl.pallas_call(
        body, grid=(SSM_CB, 2), in_specs=[act, bmat, cmat, lam, lam], out_specs=[act, ANY],
        out_shape=[jax.ShapeDtypeStruct((l, SSM_WIDTH), F32),
                   jax.ShapeDtypeStruct((2, SSM_CB, l, 2 * SSM_ST), F32)],
        scratch_shapes=[pltpu.VMEM((l, 2 * SSM_ST), F32), pltpu.SemaphoreType.DMA],
        name="ssm_fwd", compiler_params=_params(("parallel", "arbitrary"), vmem_mb=56),
    )(u_seg, bcat.astype(BF16), ccat.astype(BF16), lam_re, lam_im)


def _ssm_bwd(u_seg, dy_seg, states, bcat, ccat, lam_re, lam_im):
    l = u_seg.shape[0]
    nj = l // N_SEG

    rc2 = min(l, 2 * SSM_RC)

    def body(u_ref, dy_ref, keep_ref, b_ref, c_ref, lr_ref, li_ref,
             du_ref, db_ref, dc_ref, dlr_ref, dli_ref, xs_ref, gs_ref, keep_sem):
        k, d = pl.program_id(0), pl.program_id(1)
        rev = d == 1
        ar, ai = lr_ref[...], li_ref[...]
        fetch = pltpu.make_async_copy(keep_ref.at[d, k], xs_ref, keep_sem)
        fetch.start()

        def chunk1(i, _):
            rows = pl.ds(pl.multiple_of(i * SSM_RC, SSM_RC), SSM_RC)
            gs_ref[rows, :] = _dg(dy_ref[rows, :], c_ref[...], NT)
            return 0

        lax.fori_loop(0, l // SSM_RC, chunk1, 0)
        _, _, acc_r, acc_i = _scan_segments(gs_ref, ar, -ai, jnp.logical_not(rev), nj, prev_ref=xs_ref,
                                            before_sums=fetch.wait)
        seg = lax.broadcasted_iota(jnp.int32, (N_SEG, SSM_ST), 0)
        jb = jnp.where(rev, nj - 1, 0)
        brow = pl.ds(pl.multiple_of(jb * N_SEG, N_SEG), N_SEG)
        erow = pl.ds(pl.multiple_of((nj - 1 - jb) * N_SEG, N_SEG), N_SEG)
        re_cols, im_cols = pl.ds(0, SSM_ST), pl.ds(SSM_ST, SSM_ST)

        def before(t):
            up = jnp.where(seg >= 1, pltpu.roll(t, 1, 0), 0.0)
            down = jnp.where(seg <= N_SEG - 2, pltpu.roll(t, N_SEG - 1, 0), 0.0)
            return jnp.where(rev, down, up)

        init_r, init_i = before(xs_ref[erow, re_cols]), before(xs_ref[erow, im_cols])
        gr, gi = gs_ref[brow, re_cols], gs_ref[brow, im_cols]
        acc_r = acc_r + gr * init_r + gi * init_i
        acc_i = acc_i + gi * init_r - gr * init_i
        dlr_ref[...] = jnp.sum(acc_r, axis=0, keepdims=True)
        dli_ref[...] = jnp.sum(acc_i, axis=0, keepdims=True)

        db_ref[...] = jnp.zeros_like(db_ref)
        dc_ref[...] = jnp.zeros_like(dc_ref)

        def chunk2(i, _):
            rows = pl.ds(pl.multiple_of(i * rc2, rc2), rc2)
            g = gs_ref[rows, :].astype(BF16)
            dc_ref[...] += _dg(xs_ref[rows, :].astype(BF16), dy_ref[rows, :], TN)
            db_ref[...] += _dg(u_ref[rows, :], g, TN)
            duv = _dg(g, b_ref[...], NT)

            @pl.when(d == 0)
            def _():
                du_ref[rows, :] = duv

            @pl.when(d == 1)
            def _():
                du_ref[rows, :] += duv

            return 0

        lax.fori_loop(0, l // rc2, chunk2, 0)

    act, bmat, cmat, lam = _ssm_specs(l)
    return pl.pallas_call(
        body, grid=(SSM_CB, 2), in_specs=[act, act, ANY, bmat, cmat, lam, lam],
        out_specs=[act, bmat, cmat, lam, lam],
        out_shape=[jax.ShapeDtypeStruct((l, SSM_WIDTH), F32),
                   jax.ShapeDtypeStruct(bcat.shape, F32), jax.ShapeDtypeStruct(ccat.shape, F32),
                   jax.ShapeDtypeStruct(lam_re.shape, F32), jax.ShapeDtypeStruct(lam_im.shape, F32)],
        scratch_shapes=[pltpu.VMEM((l, 2 * SSM_ST), F32), pltpu.VMEM((l, 2 * SSM_ST), F32),
                        pltpu.SemaphoreType.DMA],
        name="ssm_bwd", compiler_params=_params(("parallel", "arbitrary"), vmem_mb=56),
    )(u_seg, dy_seg, states, bcat.astype(BF16), ccat.astype(BF16), lam_re, lam_im)


def _glu_fwd(y_ssm, u, d_skip, w_glu):
    l, w = u.shape

    def body(y_ref, u_ref, d_ref, w_ref, pre_ref, s_ref, ys_ref):
        pre = y_ref[...] + d_ref[...] * u_ref[...]
        z = _gelu(pre)
        s = _dg(z.astype(BF16), w_ref[...], NN)
        pre_ref[...] = pre
        s_ref[...] = s
        ys_ref[...] = z * _sigmoid(s)

    row = pl.BlockSpec((TM_EW, w), lambda i: (i, 0))
    out = jax.ShapeDtypeStruct((l, w), F32)
    return pl.pallas_call(
        body, grid=(l // TM_EW,),
        in_specs=[row, row, pl.BlockSpec((1, w), lambda i: (0, 0)), pl.BlockSpec((w, w), lambda i: (0, 0))],
        out_specs=[row, row, row], out_shape=[out, out, out], name="glu_fwd",
        compiler_params=_params(("parallel",)),
    )(y_ssm, u, d_skip, w_glu)


def _glu_bwd(pre, s, dys, u, d_skip, w_glu):
    l, w = u.shape

    def body(pre_ref, s_ref, dys_ref, u_ref, d_ref, w_ref, dpre_ref, z_ref, ds_ref, dd_ref):
        pre, dys = pre_ref[...], dys_ref[...]
        z = _gelu(pre)
        sig = _sigmoid(s_ref[...])
        ds = (dys * z * sig * (1.0 - sig)).astype(BF16)
        dz = dys * sig + _dg(ds, w_ref[...], NT)
        dpre = dz * _gelu_grad(pre)
        dpre_ref[...] = dpre
        z_ref[...] = z.astype(BF16)
        ds_ref[...] = ds

        @pl.when(pl.program_id(0) == 0)
        def _():
            dd_ref[...] = jnp.zeros_like(dd_ref)

        dd_ref[...] += jnp.sum(dpre * u_ref[...], axis=0, keepdims=True)

    row = pl.BlockSpec((TM_EW, w), lambda i: (i, 0))
    vec = pl.BlockSpec((1, w), lambda i: (0, 0))
    return pl.pallas_call(
        body, grid=(l // TM_EW,),
        in_specs=[row, row, row, row, vec, pl.BlockSpec((w, w), lambda i: (0, 0))],
        out_specs=[row, row, row, vec],
        out_shape=[jax.ShapeDtypeStruct((l, w), F32), jax.ShapeDtypeStruct((l, w), BF16),
                   jax.ShapeDtypeStruct((l, w), BF16), jax.ShapeDtypeStruct((1, w), F32)],
        name="glu_bwd", compiler_params=_params(("arbitrary",)),
    )(pre, s, dys, u, d_skip, w_glu)


TM_CV = 512
TC_CV = 256
TM_CF = 256
TC_CF = D_FF // 2
HALO = SUBLANES


def _conv_specs(l, col0, tm=TM_CV, tc=TC_CV):
    per = tm // HALO
    nh = l // HALO
    off = col0 // tc
    return [
        pl.BlockSpec((HALO, tc), lambda j, i: (jnp.maximum(i * per - 1, 0), j + off)),
        pl.BlockSpec((tm, tc), lambda j, i: (i, j + off)),
        pl.BlockSpec((HALO, tc), lambda j, i: (jnp.minimum((i + 1) * per, nh - 1), j + off)),
    ]


def _ext(prev_ref, mid_ref, next_ref, first, last):
    p = jnp.where(first, 0.0, prev_ref[...])
    n = jnp.where(last, 0.0, next_ref[...])
    return jnp.concatenate([p, mid_ref[...], n], axis=0)


def _shift_dn(t):
    return pltpu.roll(t, 1, 0)


def _shift_up(t):
    return pltpu.roll(t, t.shape[0] - 1, 0)


def _conv3(e, w_ref, b_ref):
    return w_ref[0:1, :] * _shift_dn(e) + w_ref[1:2, :] * e + w_ref[2:3, :] * _shift_up(e) + b_ref[...]


def _convffn_fwd(up_pre, conv_w, conv_b):
    l = up_pre.shape[0]
    tm, tc = TM_CF, TC_CF
    ni = l // tm
    wspec = lambda off: pl.BlockSpec((3, tc), lambda j, i: (0, j + off))
    bspec = lambda off: pl.BlockSpec((1, tc), lambda j, i: (0, j + off))
    voff = D_FF // tc

    def body(gp, gm, gn, vp, vm, vn, wg, bg, wv, bv, o_ref):
        i = pl.program_id(1)
        first, last = i == 0, i == ni - 1
        gate = _conv3(_ext(gp, gm, gn, first, last), wg, bg)[HALO:HALO + tm]
        val = _conv3(_ext(vp, vm, vn, first, last), wv, bv)[HALO:HALO + tm]
        o_ref[...] = (gate * _sigmoid(gate) * val).astype(BF16)

    return pl.pallas_call(
        body, grid=(D_FF // tc, ni),
        in_specs=_conv_specs(l, 0, tm, tc) + _conv_specs(l, D_FF, tm, tc)
        + [wspec(0), bspec(0), wspec(voff), bspec(voff)],
        out_specs=pl.BlockSpec((tm, tc), lambda j, i: (i, j)),
        out_shape=jax.ShapeDtypeStruct((l, D_FF), BF16), name="convffn_fwd",
        compiler_params=_params(("parallel", "parallel")),
    )(up_pre, up_pre, up_pre, up_pre, up_pre, up_pre, conv_w, conv_b, conv_w, conv_b)


def _convffn_bwd(up_pre, dact, conv_w, conv_b):
    l = up_pre.shape[0]
    ni = l // TM_CV
    wspec = lambda off: pl.BlockSpec((3, TC_CV), lambda j, i: (0, j + off))
    bspec = lambda off: pl.BlockSpec((1, TC_CV), lambda j, i: (0, j + off))
    voff = D_FF // TC_CV

    def body(gp, gm, gn, vp, vm, vn, dp, dm, dn, wg, bg, wv, bv, dup_ref, pg_ref, pv_ref):
        i = pl.program_id(1)
        first, last = i == 0, i == ni - 1
        ge, ve, de = _ext(gp, gm, gn, first, last), _ext(vp, vm, vn, first, last), _ext(dp, dm, dn, first, last)
        gate, val = _conv3(ge, wg, bg), _conv3(ve, wv, bv)
        sig = _sigmoid(gate)
        silu = gate * sig
        dgate = de * val * (sig + silu * (1.0 - sig))
        dval = de * silu
        mid = slice(HALO, HALO + TM_CV)
        rid = lax.broadcasted_iota(jnp.int32, (SUBLANES, TC_CV), 0)

        @pl.when(i == 0)
        def _():
            pg_ref[...] = jnp.zeros_like(pg_ref)
            pv_ref[...] = jnp.zeros_like(pv_ref)

        for half, (dup, e, w_ref, p_ref) in enumerate(((dgate, ge, wg, pg_ref), (dval, ve, wv, pv_ref))):
            dpre = w_ref[0:1, :] * _shift_up(dup) + w_ref[1:2, :] * dup + w_ref[2:3, :] * _shift_dn(dup)
            dup_ref[half] = dpre[mid].astype(BF16)
            dm_ = dup[mid]
            sums = [jnp.sum(dm_ * _shift_dn(e)[mid], axis=0, keepdims=True),
                    jnp.sum(dm_ * e[mid], axis=0, keepdims=True),
                    jnp.sum(dm_ * _shift_up(e)[mid], axis=0, keepdims=True),
                    jnp.sum(dm_, axis=0, keepdims=True)]
            acc = jnp.zeros((SUBLANES, TC_CV), F32)
            for k, sk in enumerate(sums):
                acc = jnp.where(rid == k, sk, acc)
            p_ref[...] += acc

    par = pl.BlockSpec((SUBLANES, TC_CV), lambda j, i: (0, j))
    dup, pg, pv = pl.pallas_call(
        body, grid=(D_FF // TC_CV, ni),
        in_specs=_conv_specs(l, 0) + _conv_specs(l, D_FF) + _conv_specs(l, 0)
        + [wspec(0), bspec(0), wspec(voff), bspec(voff)],
        out_specs=[pl.BlockSpec((2, TM_CV, TC_CV), lambda j, i: (0, i, j)), par, par],
        out_shape=[jax.ShapeDtypeStruct((2, l, D_FF), BF16),
                   jax.ShapeDtypeStruct((SUBLANES, D_FF), F32), jax.ShapeDtypeStruct((SUBLANES, D_FF), F32)],
        name="convffn_bwd", compiler_params=_params(("parallel", "arbitrary")),
    )(up_pre, up_pre, up_pre, up_pre, up_pre, up_pre, dact, dact, dact, conv_w, conv_b, conv_w, conv_b)
    return dup, jnp.concatenate([pg, pv], axis=1)


def _local_step(x, target, wb, sp, late_weights=None, ffn_grads_ready=None, ffn_grads_next=None):
    l = x.shape[0]
    tabs = _rope_tables(l)
    disc = _ssm_disc(sp["a_re"], sp["a_im"], sp["log_step"], sp["b_re"], sp["b_im"])
    bcat, ccat, lam_re, lam_im = _ssm_pack(*disc, sp["c_re"], sp["c_im"])
    d_skip = sp["d_skip"].reshape(1, SSM_WIDTH)

    big = min(l, 1024)
    h, proj, u = _rms_mm_split(x, sp["norm_mix_g"], wb["w_in"], QKV_WIDTH, "mm_in")
    qkv = _rope_fwd(proj, tabs)
    attn, lse = _attn_fwd(qkv, sp["sink"])
    u_seg = _to_segments(u).astype(BF16)
    y_seg, states = _ssm_fwd(u_seg, bcat, ccat, lam_re, lam_im)
    y_ssm = _from_segments(y_seg)
    pre, s_glu, ys = _glu_fwd(y_ssm, u, d_skip, wb["w_glu"])
    mixed = _mix_fwd(attn, ys, sp["norm_attn_g"], sp["norm_ssm_g"])
    x1, h2 = _mm_res_rms(mixed, wb["w_out"], x, sp["norm_ffn_g"], "mm_out")
    if late_weights is not None:
        wb = dict(wb, **late_weights(h2))
    up_pre = _mm_nn_cols(h2, wb["w_up"], big, "mm_up")
    act = _convffn_fwd(up_pre, sp["conv_w"], sp["conv_b"])
    loss, dx2, dx2b, d_final_g = _mm_res_loss(act, wb["w_down"], x1, sp["norm_final_g"].reshape(1, D_MODEL), target)

    g = {"norm_final_g": d_final_g.reshape(D_MODEL)}
    dact = _mm_nt(dx2b, wb["w_down"], big, D_FF // 2, F32, "mm_down_dx")
    g["w_down"] = _mm_tn(act, dx2b, D_FF // 2, 512, "mm_down_dw")
    dup_pre, conv_par = _convffn_bwd(up_pre, dact, sp["conv_w"], sp["conv_b"])
    g["conv_w"], g["conv_b"] = conv_par[0:3], conv_par[3:4]
    g["w_up"] = _mm_tn_cols(h2, dup_pre, wb["w_up"].shape[0], 512, "mm_up_dw")
    zero = ffn_grads_ready(g["w_up"], g["w_down"]) if ffn_grads_ready is not None else 0.0
    dx1, dx1b, g["norm_ffn_g"] = _mm_cols_rms_bwd(dup_pre, wb["w_up"], x1, sp["norm_ffn_g"] + zero, dx2, "mm_up_dx")
    dmixed = _mm_nt(dx1b, wb["w_out"], big, 1024, F32, "mm_out_dx")
    g["w_out"] = _mm_tn(mixed, dx1b, 1024, 1024, "mm_out_dw")
    zero = ffn_grads_next(dmixed) if ffn_grads_next is not None else 0.0
    dattn, dys, g["norm_attn_g"], g["norm_ssm_g"] = _mix_bwd(attn, ys, sp["norm_attn_g"] + zero, sp["norm_ssm_g"],
                                                            dmixed)
    dpre, zb, dsb, dd = _glu_bwd(pre, s_glu, dys, u, d_skip, wb["w_glu"])
    g["d_skip"] = dd.reshape(N_SSM_GROUPS, SSM_GROUP)
    g["w_glu"] = _mm_tn(zb, dsb, 512, 512, "mm_glu_dw")
    du_seg, dbcat, dccat, dlam_re, dlam_im = _ssm_bwd(u_seg, _to_segments(dpre).astype(BF16), states, bcat, ccat,
                                                      lam_re, lam_im)
    dlb_re, dlb_im, dbb_re, dbb_im, g["c_re"], g["c_im"] = _ssm_unpack(dbcat, dccat, dlam_re, dlam_im)
    _, disc_vjp = jax.vjp(_ssm_disc, sp["a_re"], sp["a_im"], sp["log_step"], sp["b_re"], sp["b_im"])
    g["a_re"], g["a_im"], g["log_step"], g["b_re"], g["b_im"] = disc_vjp((dlb_re, dlb_im, dbb_re, dbb_im))
    dqkv, dsink = _attn_bwd(qkv, attn, dattn, lse, sp["sink"])
    g["sink"] = dsink
    dproj = _rope_bwd(dqkv, _from_segments(du_seg), dpre, d_skip, tabs)
    g["w_in"] = _mm_tn(h, dproj, 512, IN_WIDTH, "mm_in_dw")
    grad_x, _, g["norm_mix_g"] = _mm_nt_rms_bwd(dproj, wb["w_in"], x, sp["norm_mix_g"], dx1, "mm_in_dx")
    return loss, grad_x, g


MESH = pl.DeviceIdType.MESH
ANY = pl.BlockSpec(memory_space=pl.ANY)


def _place():
    x, y, c = lax.axis_index("x"), lax.axis_index("y"), lax.axis_index("c")
    chips = [(1 - x, y), (x, 1 - y), (1 - x, 1 - y)]
    return x, y, c, chips


def _chip_index(px, py):
    return 2 * px + py


CHUNK_BYTES = 256 * 1024
MAX_CHUNKS = 16


def _row_chunks(rows, row_bytes, align):
    n = max(1, min(MAX_CHUNKS, (rows * row_bytes) // CHUNK_BYTES))
    per = -(-rows // n)
    per = -(-per // align) * align
    return [(r0, min(per, rows - r0)) for r0 in range(0, rows, per)]


def _align_of(dtype):
    return SUBLANES * 4 // jnp.dtype(dtype).itemsize


def _remote(src, dst, send_sem, recv_sem, to):
    return pltpu.make_async_remote_copy(src_ref=src, dst_ref=dst, send_sem=send_sem, recv_sem=recv_sem,
                                        device_id=to, device_id_type=MESH)


CAST_ROWS = 64


def _gather_weights(shards, dtypes):
    nw = len(shards)

    def body(*refs):
        w_refs, o_refs = refs[:nw], refs[nw:2 * nw]
        send_sems, recv_sems, in_sems, out_sems = refs[2 * nw:2 * nw + 4]
        raw, cast = refs[2 * nw + 4:3 * nw + 4], refs[3 * nw + 4:]
        x, y, c, chips = _place()
        mine = _chip_index(x, y)
        sibling = (x, y, 1 - c)

        def rows_of(ref, chip, r0, nr):
            return ref.at[chip, pl.ds(r0, nr), :]

        def copy(wi, k, src, dst, to):
            return _remote(src, dst, send_sems.at[wi, k], recv_sems.at[wi, k], to)

        geo = []
        for wi in range(nw):
            rows, cols = w_refs[wi].shape
            row_bytes = cols * jnp.dtype(dtypes[wi]).itemsize
            geo.append((rows // 2, _row_chunks(rows // 2, row_bytes, _align_of(dtypes[wi]))))

        stage_in = [pltpu.make_async_copy(w_refs[wi], raw[wi], in_sems.at[wi]) for wi in range(nw)]
        for cp in stage_in:
            cp.start()
        staged = [raw[wi] if dtypes[wi] == w_refs[wi].dtype else cast[wi] for wi in range(nw)]
        stage_out = []
        for wi in range(nw):
            stage_in[wi].wait()
            if staged[wi] is not raw[wi]:
                def cast_rows(i, _, wi=wi):
                    rows = pl.ds(pl.multiple_of(i * CAST_ROWS, CAST_ROWS), CAST_ROWS)
                    cast[wi][rows, :] = raw[wi][rows, :].astype(dtypes[wi])
                    return 0

                lax.fori_loop(0, w_refs[wi].shape[0] // CAST_ROWS, cast_rows, 0)
            cp = pltpu.make_async_copy(staged[wi], o_refs[wi].at[mine], out_sems.at[wi])
            cp.start()
            stage_out.append(cp)

        for wi in range(nw):
            hr, half_chunks = geo[wi]
            for j, chip in enumerate(chips):
                for r0, nr in half_chunks:
                    copy(wi, j, staged[wi].at[pl.ds(c * hr + r0, nr), :],
                         rows_of(o_refs[wi], mine, c * hr + r0, nr), (*chip, c)).start()
        for wi in range(nw):
            hr, half_chunks = geo[wi]
            for j, chip in enumerate(chips):
                got = rows_of(o_refs[wi], _chip_index(*chip), c * hr, hr)
                copy(wi, j, got, got, (*chip, c)).wait_recv()
                for r0, nr in half_chunks:
                    piece = rows_of(o_refs[wi], _chip_index(*chip), c * hr + r0, nr)
                    copy(wi, 3 + j, piece, piece, sibling).start()
        for wi in range(nw):
            hr = geo[wi][0]
            for j, chip in enumerate(chips):
                got = rows_of(o_refs[wi], _chip_index(*chip), (1 - c) * hr, hr)
                copy(wi, 3 + j, got, got, sibling).wait_recv()
        for wi in range(nw):
            hr = geo[wi][0]
            sent = rows_of(o_refs[wi], mine, c * hr, hr)
            for k in range(6):
                copy(wi, k, sent, sent, sibling).wait_send()
            stage_out[wi].wait()

    return pl.pallas_call(
        body, in_specs=[ANY] * nw, out_specs=[ANY] * nw,
        out_shape=[jax.ShapeDtypeStruct((4, *s.shape), t) for s, t in zip(shards, dtypes)],
        scratch_shapes=[pltpu.SemaphoreType.DMA((nw, 6)), pltpu.SemaphoreType.DMA((nw, 6)),
                        pltpu.SemaphoreType.DMA((nw,)), pltpu.SemaphoreType.DMA((nw,))]
        + [pltpu.VMEM(s.shape, s.dtype) for s in shards] + [pltpu.VMEM(s.shape, t) for s, t in zip(shards, dtypes)],
        name="gather_weights", compiler_params=_params(vmem_mb=40),
    )(*shards)


HBM = pl.BlockSpec(memory_space=pltpu.HBM)
SEM = pl.BlockSpec(memory_space=pltpu.SEMAPHORE)
EFFECT = pltpu.SideEffectType.DATAFLOW_SIDE_EFFECTING


def _cast_place(w, place, dtype, after, name):
    rows, cols = w.shape
    tr = _row_tile(rows, cols, _align_of(dtype))

    def body(p_ref, w_ref, after_ref, o_ref):
        del p_ref, after_ref
        o_ref[...] = w_ref[...].astype(dtype)

    grid_spec = pltpu.PrefetchScalarGridSpec(
        num_scalar_prefetch=1, grid=(rows // tr,),
        in_specs=[pl.BlockSpec((tr, cols), lambda i, p: (i, 0)), ANY],
        out_specs=pl.BlockSpec((None, tr, cols), lambda i, p: (p[1], i, 0)))
    return pl.pallas_call(body, grid_spec=grid_spec, out_shape=jax.ShapeDtypeStruct((4, rows, cols), dtype),
                          name=name, compiler_params=_params(("parallel",)))(place, w, after)


def _split_start(name, arrays, n_pairs, issue):
    n = len(arrays)

    def body(*refs):
        issue(refs[:n], refs[n:n + n_pairs], refs[n + n_pairs:n + 2 * n_pairs])
        token = refs[2 * n + 2 * n_pairs]
        token[...] = jnp.zeros_like(token)

    dma = pltpu.SemaphoreType.DMA(())
    outs = pl.pallas_call(
        body, name=name,
        out_shape=[dma] * (2 * n_pairs) + [pltpu.HBM(t.shape, t.dtype) for t in arrays]
        + [jax.ShapeDtypeStruct((SUBLANES, LANES), F32)],
        in_specs=[HBM] * n, out_specs=[SEM] * (2 * n_pairs) + [HBM] * n + [pl.BlockSpec(memory_space=pltpu.VMEM)],
        input_output_aliases={a: 2 * n_pairs + a for a in range(n)},
        compiler_params=pltpu.CompilerParams(has_side_effects=EFFECT),
    )(*[pltpu.with_memory_space_constraint(t, pltpu.HBM) for t in arrays])
    return outs[:n_pairs], outs[n_pairs:2 * n_pairs], outs[2 * n_pairs:2 * n_pairs + n], outs[-1]


def _split_wait(name, send_sems, recv_sems, flying, sizes, after):
    n, n_pairs = len(flying), len(send_sems)

    def body(*refs):
        x, y, c, _ = _place()
        for k, ref in enumerate(sizes(refs[:n])):
            cp = _remote(ref, ref, refs[n + k], refs[n + n_pairs + k], (x, y, 1 - c))
            cp.wait_send()
            cp.wait_recv()

    return pl.pallas_call(
        body, name=name, out_shape=[pltpu.HBM(t.shape, t.dtype) for t in flying],
        in_specs=[HBM] * n + [SEM] * (2 * n_pairs) + [ANY], out_specs=[HBM] * n,
        input_output_aliases={a: a for a in range(n)},
        compiler_params=pltpu.CompilerParams(has_side_effects=EFFECT),
    )(*flying, *send_sems, *recv_sems, after)


def _spread_start(lands):
    def issue(land_refs, send_sems, recv_sems):
        x, y, c, chips = _place()
        mine = _chip_index(x, y)
        for a, land in enumerate(land_refs):
            _, rows, cols = land.shape
            hr = rows // 2
            row_bytes = cols * jnp.dtype(land.dtype).itemsize
            for r0, nr in _row_chunks(hr, row_bytes, _align_of(land.dtype)):
                piece = land.at[mine, pl.ds(c * hr + r0, nr), :]
                for chip in chips:
                    for core in (0, 1):
                        _remote(piece, piece, send_sems[a], recv_sems[a], (*chip, core)).start()

    return _split_start("spread_start", lands, len(lands), issue)


def _spread_wait(send_sems, recv_sems, flying, after):
    return _split_wait("spread_wait", send_sems, recv_sems, flying,
                       lambda refs: [r.at[pl.ds(0, 3)] for r in refs], after)


def _pair_start(grads):
    n = len(grads)
    zones = [lax.empty((4, g.shape[1] // 2, g.shape[2]), F32) for g in grads]

    def issue(refs, send_sems, recv_sems):
        x, y, c, _ = _place()
        for a in range(n):
            g_ref, z_ref = refs[a], refs[n + a]
            _, rows, cols = g_ref.shape
            hr = rows // 2
            for k in range(4):
                for r0, nr in _row_chunks(hr, cols * 4, SUBLANES):
                    _remote(g_ref.at[k, pl.ds((1 - c) * hr + r0, nr), :], z_ref.at[k, pl.ds(r0, nr), :],
                            send_sems[a], recv_sems[a], (x, y, 1 - c)).start()

    return _split_start("pair_start", list(grads) + zones, n, issue)


def _pair_wait(send_sems, recv_sems, flying, after):
    n = len(flying) // 2
    out = _split_wait("pair_wait", send_sems, recv_sems, flying, lambda refs: list(refs[n:]), after)
    return out[:n], out[n:]


def _chip_start(sums):
    n = len(sums)
    zones = [lax.empty((3, *s.shape[1:]), s.dtype) for s in sums]

    def issue(refs, send_sems, recv_sems):
        x, y, c, chips = _place()
        for a in range(n):
            s_ref, z_ref = refs[a], refs[n + a]
            _, rows, cols = s_ref.shape
            row_bytes = cols * jnp.dtype(s_ref.dtype).itemsize
            for r0, nr in _row_chunks(rows, row_bytes, _align_of(s_ref.dtype)):
                for j, chip in enumerate(chips):
                    _remote(s_ref.at[_chip_index(*chip), pl.ds(r0, nr), :], z_ref.at[j, pl.ds(r0, nr), :],
                            send_sems[a], recv_sems[a], (*chip, c)).start()

    return _split_start("chip_start", list(sums) + zones, n, issue)


def _chip_wait(send_sems, recv_sems, flying, after):
    n = len(flying) // 2
    return _split_wait("chip_wait", send_sems, recv_sems, flying, lambda refs: list(refs[n:]), after)[n:]


def _pair_exchange(grads):
    na = len(grads)

    def body(*refs):
        g_refs, o_refs = refs[:na], refs[na:2 * na]
        send_sems, recv_sems = refs[2 * na:]
        x, y, c, _ = _place()
        sibling = (x, y, 1 - c)
        for ai in range(na):
            _, rows, cols = g_refs[ai].shape
            hr = rows // 2
            for k in range(4):
                for r0, nr in _row_chunks(hr, cols * 4, SUBLANES):
                    _remote(g_refs[ai].at[k, pl.ds((1 - c) * hr + r0, nr), :], o_refs[ai].at[k, pl.ds(r0, nr), :],
                            send_sems.at[ai], recv_sems.at[ai], sibling).start()
        for ai in range(na):
            _remote(o_refs[ai], o_refs[ai], send_sems.at[ai], recv_sems.at[ai], sibling).wait()

    return pl.pallas_call(
        body, in_specs=[ANY] * na, out_specs=[ANY] * na,
        out_shape=[jax.ShapeDtypeStruct((4, g.shape[1] // 2, g.shape[2]), F32) for g in grads],
        scratch_shapes=[pltpu.SemaphoreType.DMA((na,)), pltpu.SemaphoreType.DMA((na,))],
        name="pair_exchange",
    )(*grads)


def _row_tile(rows, cols, align):
    best = align
    for cand in range(align, rows + 1, align):
        if rows % cand == 0 and cand * cols <= 256 * 1024:
            best = cand
    return best


def _pair_sum(g, got, place, transit, name):
    _, rows, cols = g.shape
    hr = rows // 2
    tr = _row_tile(hr, cols, _align_of(transit))
    nt = hr // tr

    def body(p_ref, g_ref, r_ref, s_ref, own_ref):
        total = g_ref[...] + r_ref[...]
        s_ref[...] = total.astype(transit)

        @pl.when(pl.program_id(1) == p_ref[1])
        def _():
            own_ref[...] = total

    grid_spec = pltpu.PrefetchScalarGridSpec(
        num_scalar_prefetch=1, grid=(nt, 4),
        in_specs=[pl.BlockSpec((None, tr, cols), lambda i, k, p: (k, p[0] * nt + i, 0)),
                  pl.BlockSpec((None, tr, cols), lambda i, k, p: (k, i, 0))],
        out_specs=[pl.BlockSpec((None, tr, cols), lambda i, k, p: (k, i, 0)),
                   pl.BlockSpec((tr, cols), lambda i, k, p: (i, 0))])
    return pl.pallas_call(
        body, grid_spec=grid_spec,
        out_shape=[jax.ShapeDtypeStruct((4, hr, cols), transit), jax.ShapeDtypeStruct((hr, cols), F32)],
        name=name, compiler_params=_params(("parallel", "arbitrary")),
    )(place, g, got)


def _chip_exchange(sums):
    na = len(sums)

    def body(*refs):
        s_refs, o_refs = refs[:na], refs[na:2 * na]
        send_sems, recv_sems = refs[2 * na:]
        x, y, c, chips = _place()
        for ai in range(na):
            _, rows, cols = s_refs[ai].shape
            row_bytes = cols * jnp.dtype(s_refs[ai].dtype).itemsize
            for r0, nr in _row_chunks(rows, row_bytes, _align_of(s_refs[ai].dtype)):
                for j, chip in enumerate(chips):
                    _remote(s_refs[ai].at[_chip_index(*chip), pl.ds(r0, nr), :], o_refs[ai].at[j, pl.ds(r0, nr), :],
                            send_sems.at[ai, j], recv_sems.at[ai, j], (*chip, c)).start()
        for ai in range(na):
            for j, chip in enumerate(chips):
                _remote(o_refs[ai].at[j], o_refs[ai].at[j], send_sems.at[ai, j], recv_sems.at[ai, j],
                        (*chip, c)).wait()

    return pl.pallas_call(
        body, in_specs=[ANY] * na, out_specs=[ANY] * na,
        out_shape=[jax.ShapeDtypeStruct((3, *s.shape[1:]), s.dtype) for s in sums],
        scratch_shapes=[pltpu.SemaphoreType.DMA((na, 3)), pltpu.SemaphoreType.DMA((na, 3))],
        name="chip_exchange",
    )(*sums)


def _chip_sum(own, landed, name):
    hr, cols = own.shape
    tr = _row_tile(hr, cols, _align_of(landed.dtype))

    def body(o_ref, l_ref, f_ref):
        acc = o_ref[...]
        for j in range(3):
            acc = acc + l_ref[j].astype(F32)
        f_ref[...] = acc

    return pl.pallas_call(
        body, grid=(hr // tr,),
        in_specs=[pl.BlockSpec((tr, cols), lambda i: (i, 0)), pl.BlockSpec((3, tr, cols), lambda i: (0, i, 0))],
        out_specs=pl.BlockSpec((tr, cols), lambda i: (i, 0)),
        out_shape=jax.ShapeDtypeStruct((hr, cols), F32), name=name,
        compiler_params=_params(("parallel",)),
    )(own, landed)


def _final_exchange(halves, small):
    nh = len(halves)

    def body(*refs):
        h_refs, s_ref = refs[:nh], refs[nh]
        o_refs, so_ref = refs[nh + 1:2 * nh + 1], refs[2 * nh + 1]
        send_sems, recv_sems, local_sem, ssend_sems, srecv_sems = refs[2 * nh + 2:]
        x, y, c, _ = _place()
        me = 4 * x + 2 * y + c
        sibling = (x, y, 1 - c)
        for hi in range(nh):
            hr, cols = h_refs[hi].shape
            for r0, nr in _row_chunks(hr, cols * 4, SUBLANES):
                _remote(h_refs[hi].at[pl.ds(r0, nr), :], o_refs[hi].at[pl.ds(r0, nr), :],
                        send_sems.at[hi], recv_sems.at[hi], sibling).start()
        small_cps = [pltpu.make_async_copy(s_ref, so_ref.at[me], local_sem)]
        for r in range(1, 8):
            fx, fy, fc = (r >> 2) & 1, (r >> 1) & 1, r & 1
            peer = (1 - x if fx else x, 1 - y if fy else y, 1 - c if fc else c)
            small_cps.append(_remote(s_ref, so_ref.at[me], ssend_sems.at[r - 1], srecv_sems.at[r - 1], peer))
        for cp in small_cps:
            cp.start()
        for hi in range(nh):
            _remote(h_refs[hi], o_refs[hi], send_sems.at[hi], recv_sems.at[hi], sibling).wait()
        for cp in small_cps:
            cp.wait()

    return pl.pallas_call(
        body, in_specs=[ANY] * (nh + 1), out_specs=[ANY] * (nh + 1),
        out_shape=[jax.ShapeDtypeStruct(h.shape, F32) for h in halves]
        + [jax.ShapeDtypeStruct((8, *small.shape), F32)],
        scratch_shapes=[pltpu.SemaphoreType.DMA((nh,)), pltpu.SemaphoreType.DMA((nh,)),
                        pltpu.SemaphoreType.DMA, pltpu.SemaphoreType.DMA((7,)), pltpu.SemaphoreType.DMA((7,))],
        name="final_exchange",
    )(*halves, small)


def _adamw(w, g, m, v, name):
    shape = w.shape
    n = w.size
    if w.ndim >= 2 and shape[-1] >= LANES:
        two_d = (n // shape[-1], shape[-1])
    elif n % LANES == 0:
        two_d = (n // LANES, LANES)
    else:
        two_d = (1, n)
    r, c = two_d
    tr = r
    for cand in (512, 256, 176, 128, 64):
        if r > cand and r % cand == 0 and cand * c <= 256 * 1024:
            tr = cand
            break
    c1 = 1.0 - ADAM_B1 ** ADAM_STEP
    c2 = 1.0 - ADAM_B2 ** ADAM_STEP

    def body(w_ref, g_ref, m_ref, v_ref, d_ref, nm_ref, nv_ref):
        gv = g_ref[...]
        nm = ADAM_B1 * m_ref[...] + (1.0 - ADAM_B1) * gv
        nv = ADAM_B2 * v_ref[...] + (1.0 - ADAM_B2) * (gv * gv)
        d_ref[...] = -ADAM_LR * ((nm / c1) / (jnp.sqrt(nv / c2) + ADAM_EPS) + ADAM_WD * w_ref[...])
        nm_ref[...] = nm
        nv_ref[...] = nv

    spec = pl.BlockSpec((tr, c), lambda i: (i, 0))
    out = jax.ShapeDtypeStruct((r, c), F32)
    d, nm, nv = pl.pallas_call(
        body, grid=(r // tr,), in_specs=[spec] * 4, out_specs=[spec] * 3, out_shape=[out] * 3, name=name,
        compiler_params=_params(("parallel",)),
    )(w.reshape(two_d), g.reshape(two_d), m.reshape(two_d), v.reshape(two_d))
    return d.reshape(shape), nm.reshape(shape), nv.reshape(shape)


def _adamw_many(ws, gs, ms, vs, name):
    n = len(ws)
    c1 = 1.0 - ADAM_B1 ** ADAM_STEP
    c2 = 1.0 - ADAM_B2 ** ADAM_STEP

    def body(*refs):
        w_refs, g_refs, m_refs, v_refs = (refs[k * n:(k + 1) * n] for k in range(4))
        d_refs, nm_refs, nv_refs = (refs[(4 + k) * n:(5 + k) * n] for k in range(3))
        for i in range(n):
            gv = g_refs[i][...]
            nm = ADAM_B1 * m_refs[i][...] + (1.0 - ADAM_B1) * gv
            nv = ADAM_B2 * v_refs[i][...] + (1.0 - ADAM_B2) * (gv * gv)
            d_refs[i][...] = -ADAM_LR * ((nm / c1) / (jnp.sqrt(nv / c2) + ADAM_EPS) + ADAM_WD * w_refs[i][...])
            nm_refs[i][...] = nm
            nv_refs[i][...] = nv

    vmem = pl.BlockSpec(memory_space=pltpu.VMEM)
    shapes = [jax.ShapeDtypeStruct(t.shape, F32) for t in ws]
    outs = pl.pallas_call(body, in_specs=[vmem] * (4 * n), out_specs=[vmem] * (3 * n), out_shape=shapes * 3,
                          name=name, compiler_params=_params(vmem_mb=56))(*ws, *gs, *ms, *vs)
    return outs[:n], outs[n:2 * n], outs[2 * n:]


BIG = ("w_in", "w_glu", "w_out", "w_up", "w_down")
WEIGHTS = ("norm_mix_g", "w_in", "a_re", "a_im", "log_step", "b_re", "b_im", "c_re", "c_im", "d_skip", "w_glu",
           "sink", "norm_attn_g", "norm_ssm_g", "w_out", "norm_ffn_g", "w_up", "conv_w", "conv_b", "w_down",
           "norm_final_g")
SMALL = ("norm_mix_g", "a_re", "a_im", "log_step", "b_re", "b_im", "c_re", "c_im", "d_skip", "sink",
         "norm_attn_g", "norm_ssm_g", "norm_ffn_g", "conv_w", "conv_b", "norm_final_g")
SMALL_ROWS = 40
N_DEV = 8


def _by_owner(name, g):
    if name == "w_up":
        return g
    if name == "w_in":
        return g.reshape(g.shape[0], 4, g.shape[1] // 4).transpose(1, 0, 2)
    return g.reshape(4, g.shape[0] // 4, g.shape[1])


def kernel(x, norm_mix_g, w_in, a_re, a_im, log_step, b_re, b_im, c_re, c_im, d_skip, w_glu, sink, norm_attn_g, norm_ssm_g, w_out, norm_ffn_g, w_up, conv_w, conv_b, w_down, norm_final_g, loss_target, m_norm_mix_g, m_w_in, m_a_re, m_a_im, m_log_step, m_b_re, m_b_im, m_c_re, m_c_im, m_d_skip, m_w_glu, m_sink, m_norm_attn_g, m_norm_ssm_g, m_w_out, m_norm_ffn_g, m_w_up, m_conv_w, m_conv_b, m_w_down, m_norm_final_g, v_norm_mix_g, v_w_in, v_a_re, v_a_im, v_log_step, v_b_re, v_b_im, v_c_re, v_c_im, v_d_skip, v_w_glu, v_sink, v_norm_attn_g, v_norm_ssm_g, v_w_out, v_norm_ffn_g, v_w_up, v_conv_w, v_conv_b, v_w_down, v_norm_final_g):
    given = dict(locals())
    w = {n: given[n] for n in WEIGHTS}
    m = {n: given["m_" + n] for n in WEIGHTS}
    v = {n: given["v_" + n] for n in WEIGHTS}
    xy = 2 * lax.axis_index("x") + lax.axis_index("y")

    core = lax.axis_index("c")
    place = jnp.stack([core, xy]).astype(jnp.int32)

    conv_rows = jnp.pad(w["conv_w"][0], ((0, 2 * SUBLANES - 3), (0, 0)))
    early = ("w_in", "w_glu", "w_out")
    *gathered, conv_all = _gather_weights([w[n][0] for n in early] + [conv_rows], [BF16] * len(early) + [F32])
    late = ("w_up", "w_down")
    send_sems, recv_sems, flying, token = _spread_start(
        [_cast_place(w[n][0], place, BF16, conv_all, "cast_" + n) for n in late])
    rows = lambda t: t.reshape(4 * t.shape[1], t.shape[2])
    wb = {"w_in": gathered[0].transpose(1, 0, 2).reshape(D_MODEL, IN_WIDTH), "w_glu": rows(gathered[1]),
          "w_out": rows(gathered[2])}

    def late_weights(after):
        w_up4, w_down4 = _spread_wait(send_sems, recv_sems, flying, after)
        return {"w_up": w_up4, "w_down": rows(w_down4)}

    sp = {n: w[n][0] for n in ("a_re", "a_im", "log_step", "b_re", "b_im", "c_re", "c_im", "d_skip",
                               "norm_mix_g", "norm_attn_g", "norm_ssm_g", "norm_ffn_g", "sink", "conv_b")}
    for n in ("norm_mix_g", "norm_attn_g", "norm_ssm_g", "norm_ffn_g", "sink", "conv_b"):
        sp[n] = sp[n].reshape(1, -1)
    sp["norm_mix_g"] = sp["norm_mix_g"] + token[:1, :1]
    sp["conv_w"] = conv_all[:, :3].transpose(1, 0, 2).reshape(3, 2 * D_FF)
    sp["norm_final_g"] = w["norm_final_g"]
    flight = {}

    def ffn_grads_ready(dw_up, dw_down):
        *flight["pair"], token = _pair_start([dw_up, _by_owner("w_down", dw_down)])
        return token[:1, :1]

    def ffn_grads_next(after):
        mine, got = _pair_wait(*flight["pair"], after)
        sums, flight["own"] = zip(*[_pair_sum(a, b, place, BF16, "pair_sum_" + n) for n, a, b in zip(late, mine, got)])
        *flight["chip"], token = _chip_start(list(sums))
        return token[:1, :1]

    loss, grad_x, g = _local_step(x[0], loss_target[0], wb, sp, late_weights, ffn_grads_ready, ffn_grads_next)

    flat = jnp.concatenate([g[n].reshape(-1) for n in SMALL] + [loss.reshape(-1)])
    pad = N_DEV * SMALL_ROWS * D_MODEL - flat.shape[0]
    small = jnp.concatenate([flat, jnp.zeros((pad,), F32)]).reshape(4, 2 * SMALL_ROWS, D_MODEL)
    by_owner = [_by_owner(n, g[n]) for n in early] + [small]
    got = _pair_exchange(by_owner)
    transit = [BF16] * len(early) + [F32]
    chip_sums, own_sums = zip(*[_pair_sum(a, b, place, t, "pair_sum_" + n)
                                for n, a, b, t in zip(early + ("small",), by_owner, got, transit)])
    landed = _chip_exchange(list(chip_sums))
    halves = {n: _chip_sum(o, t, "chip_sum_" + n) for n, o, t in zip(early + ("small",), own_sums, landed)}
    late_landed = _chip_wait(*flight["chip"], grad_x)
    for n, o, t in zip(late, flight["own"], late_landed):
        halves[n] = _chip_sum(o, t, "chip_sum_" + n)
    *others, small_all = _final_exchange([halves[n] for n in BIG], halves["small"])
    grads = {n: jnp.concatenate([jnp.where(core == 0, halves[n], o), jnp.where(core == 0, o, halves[n])], axis=0)
             for n, o in zip(BIG, others)}
    flat = small_all.reshape(-1)
    off = 0
    for n in SMALL:
        shape = (3, 4 * w[n].shape[-1]) if n == "conv_w" else w[n].shape[1:] if n != "norm_final_g" else w[n].shape
        size = math.prod(shape)
        grads[n] = flat[off:off + size].reshape(shape)
        off += size
    loss = flat[off]
    cw = w["conv_w"].shape[-1]
    grads["conv_w"] = lax.dynamic_slice_in_dim(grads["conv_w"], xy * cw, cw, axis=1)
    grads = {n: grads[n].reshape(w[n].shape) for n in WEIGHTS}

    delta, new_m, new_v = {}, {}, {}
    for n in BIG:
        delta[n], new_m[n], new_v[n] = _adamw(w[n], grads[n], m[n], v[n], "adamw_" + n)
    for group, name in ((("b_re", "b_im"), "adamw_b"), (tuple(n for n in SMALL if n not in ("b_re", "b_im")), "adamw_small")):
        row = lambda t: t.reshape(1, -1) if t.ndim == 1 else t
        d_, m_, v_ = _adamw_many(*[[row(t[n]) for n in group] for t in (w, grads, m, v)], name)
        for n, dn, mn, vn in zip(group, d_, m_, v_):
            delta[n], new_m[n], new_v[n] = (t.reshape(w[n].shape) for t in (dn, mn, vn))
    return (loss, grad_x[None], *[grads[n] for n in WEIGHTS], *[delta[n] for n in WEIGHTS],
            *[new_m[n] for n in WEIGHTS], *[new_v[n] for n in WEIGHTS])
```

```python
import functools
import math

import jax
import jax.numpy as jnp
from jax import lax
from jax.experimental import pallas as pl
from jax.experimental.pallas import tpu as pltpu

F32 = jnp.float32
BF16 = jnp.bfloat16

D_MODEL = 1024
N_Q_HEADS = 8
N_KV_HEADS = 2
HEAD_DIM = 64
ATTN_WIDTH = 512
KV_WIDTH = 128
QKV_WIDTH = ATTN_WIDTH + 2 * KV_WIDTH
WINDOW = 128
BLOCK = 128
ROPE_DIM = 16
ROPE_THETA = 500000.0
SSM_WIDTH = 512
SSM_GROUP = 16
N_SSM_GROUPS = 32
SSM_STATE = 64
IN_WIDTH = 1280
D_FF = 2816
EPS = 1e-6
ADAM_LR = 0.001
ADAM_B1 = 0.9
ADAM_B2 = 0.999
ADAM_EPS = 1e-08
ADAM_WD = 0.01
ADAM_STEP = 10

VMEM_BYTES_V7X = 64 * 1024 * 1024
SUBLANES = 8
LANES = 128
SSM_CB = 4
SSM_CH = 128
SSM_ST = 512
N_SEG = SUBLANES

NN = (((1,), (0,)), ((), ()))
NT = (((1,), (1,)), ((), ()))
TN = (((0,), (0,)), ((), ()))


def _params(sem=None, vmem_mb=48):
    return pltpu.CompilerParams(dimension_semantics=sem, vmem_limit_bytes=vmem_mb * 1024 * 1024)


def _dg(a, b, dims):
    return lax.dot_general(a, b, dims, preferred_element_type=F32)


def _sigmoid(x):
    return 1.0 / (1.0 + jnp.exp(-x))


_SQRT_HALF = 0.7071067811865476
_INV_SQRT_2PI = 0.3989422804014327


def _gelu(x):
    return 0.5 * x * (1.0 + lax.erf(x * _SQRT_HALF))


def _gelu_grad(x):
    return 0.5 * (1.0 + lax.erf(x * _SQRT_HALF)) + x * (_INV_SQRT_2PI * jnp.exp(-0.5 * x * x))


def _mm_nt(a, b, tm, tn, out_dtype, name):
    m, k = a.shape
    n = b.shape[0]

    def body(a_ref, b_ref, o_ref):
        o_ref[...] = _dg(a_ref[...], b_ref[...], NT).astype(out_dtype)

    return pl.pallas_call(
        body, grid=(m // tm, n // tn),
        in_specs=[pl.BlockSpec((tm, k), lambda i, j: (i, 0)), pl.BlockSpec((tn, k), lambda i, j: (j, 0))],
        out_specs=pl.BlockSpec((tm, tn), lambda i, j: (i, j)),
        out_shape=jax.ShapeDtypeStruct((m, n), out_dtype), name=name,
        compiler_params=_params(("parallel", "parallel")),
    )(a, b)


def _mm_tn(a, b, tm, tn, name):
    k, m = a.shape
    n = b.shape[1]

    def body(a_ref, b_ref, o_ref):
        o_ref[...] = _dg(a_ref[...], b_ref[...], TN)

    return pl.pallas_call(
        body, grid=(m // tm, n // tn),
        in_specs=[pl.BlockSpec((k, tm), lambda i, j: (0, i)), pl.BlockSpec((k, tn), lambda i, j: (0, j))],
        out_specs=pl.BlockSpec((tm, tn), lambda i, j: (i, j)),
        out_shape=jax.ShapeDtypeStruct((m, n), F32), name=name,
        compiler_params=_params(("parallel", "parallel")),
    )(a, b)


def _mm_nn_cols(a, b4, tm, name):
    m, k = a.shape
    s, _, n = b4.shape

    def body(a_ref, b_ref, o_ref):
        o_ref[...] = _dg(a_ref[...], b_ref[...], NN)

    return pl.pallas_call(
        body, grid=(m // tm, s),
        in_specs=[pl.BlockSpec((tm, k), lambda i, j: (i, 0)), pl.BlockSpec((None, k, n), lambda i, j: (j, 0, 0))],
        out_specs=pl.BlockSpec((tm, n), lambda i, j: (i, j)),
        out_shape=jax.ShapeDtypeStruct((m, s * n), F32), name=name,
        compiler_params=_params(("parallel", "parallel")),
    )(a, b4)


def _mm_tn_cols(a, b2, s, tm, name):
    k, m = a.shape
    h, _, wide = b2.shape
    per = s // h
    n = wide // per

    def body(a_ref, b_ref, o_ref):
        o_ref[...] = _dg(a_ref[...], b_ref[...], TN)

    return pl.pallas_call(
        body, grid=(s, m // tm),
        in_specs=[pl.BlockSpec((k, tm), lambda j, i: (0, i)),
                  pl.BlockSpec((None, k, n), lambda j, i: (j // per, 0, j % per))],
        out_specs=pl.BlockSpec((None, tm, n), lambda j, i: (j, i, 0)),
        out_shape=jax.ShapeDtypeStruct((s, m, n), F32), name=name,
        compiler_params=_params(("parallel", "parallel")),
    )(a, b2)


TM_EW = 256


def _rms_bwd_vals(xv, gv, dy):
    r = lax.rsqrt(jnp.mean(xv * xv, axis=-1, keepdims=True) + EPS)
    xh = xv * r
    dxh = dy * gv
    dx = r * (dxh - xh * jnp.mean(dxh * xh, axis=-1, keepdims=True))
    return dx, dy * xh


TM_FUSED = 256


def _rms_vals(xv, gv):
    return xv * lax.rsqrt(jnp.mean(xv * xv, axis=-1, keepdims=True) + EPS) * gv


def _rms_mm_split(x, g, w, split, name):
    l, d = x.shape
    n = w.shape[1]

    def body(x_ref, g_ref, w_ref, h_ref, lo_ref, hi_ref):
        h = _rms_vals(x_ref[...], g_ref[...]).astype(BF16)
        h_ref[...] = h
        out = _dg(h, w_ref[...], NN)
        lo_ref[...] = out[:, :split]
        hi_ref[...] = out[:, split:]

    row = lambda width: pl.BlockSpec((TM_FUSED, width), lambda i: (i, 0))
    return pl.pallas_call(
        body, grid=(l // TM_FUSED,),
        in_specs=[row(d), pl.BlockSpec((1, d), lambda i: (0, 0)), pl.BlockSpec((d, n), lambda i: (0, 0))],
        out_specs=[row(d), row(split), row(n - split)],
        out_shape=[jax.ShapeDtypeStruct((l, d), BF16), jax.ShapeDtypeStruct((l, split), F32),
                   jax.ShapeDtypeStruct((l, n - split), F32)],
        name=name, compiler_params=_params(("parallel",)),
    )(x, g, w)


def _mm_res_rms(a, b, res, g, name):
    l, k = a.shape
    d = b.shape[1]

    def body(a_ref, b_ref, r_ref, g_ref, x_ref, h_ref):
        xv = r_ref[...] + _dg(a_ref[...], b_ref[...], NN)
        x_ref[...] = xv
        h_ref[...] = _rms_vals(xv, g_ref[...]).astype(BF16)

    row = lambda width: pl.BlockSpec((TM_FUSED, width), lambda i: (i, 0))
    return pl.pallas_call(
        body, grid=(l // TM_FUSED,),
        in_specs=[row(k), pl.BlockSpec((k, d), lambda i: (0, 0)), row(d), pl.BlockSpec((1, d), lambda i: (0, 0))],
        out_specs=[row(d), row(d)],
        out_shape=[jax.ShapeDtypeStruct((l, d), F32), jax.ShapeDtypeStruct((l, d), BF16)],
        name=name, compiler_params=_params(("parallel",)),
    )(a, b, res, g)


def _mm_res_loss(a, b, res, g, target):
    l, k = a.shape
    d = b.shape[1]

    def body(a_ref, b_ref, r_ref, g_ref, t_ref, loss_ref, dx_ref, dxb_ref, dg_ref):
        xv = r_ref[...] + _dg(a_ref[...], b_ref[...], NN)
        gv = g_ref[...]
        r = lax.rsqrt(jnp.mean(xv * xv, axis=-1, keepdims=True) + EPS)
        xh = xv * r
        e = xh * gv - t_ref[...]
        part = jnp.sum(jnp.sum(e * e, axis=1, keepdims=True), axis=0, keepdims=True) * (0.5 / d)
        dy = e * (1.0 / d)
        dxh = dy * gv
        dx = r * (dxh - xh * jnp.mean(dxh * xh, axis=-1, keepdims=True))
        dx_ref[...] = dx
        dxb_ref[...] = dx.astype(BF16)

        @pl.when(pl.program_id(0) == 0)
        def _():
            dg_ref[...] = jnp.zeros_like(dg_ref)
            loss_ref[...] = jnp.zeros_like(loss_ref)

        dg_ref[...] += jnp.sum(dy * xh, axis=0, keepdims=True)
        loss_ref[...] += part

    row = lambda width: pl.BlockSpec((TM_FUSED, width), lambda i: (i, 0))
    vec = pl.BlockSpec((1, d), lambda i: (0, 0))
    return pl.pallas_call(
        body, grid=(l // TM_FUSED,),
        in_specs=[row(k), pl.BlockSpec((k, d), lambda i: (0, 0)), row(d), vec, row(d)],
        out_specs=[pl.BlockSpec((1, 1), lambda i: (0, 0)), row(d), row(d), vec],
        out_shape=[jax.ShapeDtypeStruct((1, 1), F32), jax.ShapeDtypeStruct((l, d), F32),
                   jax.ShapeDtypeStruct((l, d), BF16), jax.ShapeDtypeStruct((1, d), F32)],
        name="mm_down_loss", compiler_params=_params(("arbitrary",)),
    )(a, b, res, g, target)


def _mm_rms_bwd(a, b, a_spec, b_spec, matmul, x, g, res, name):
    l, d = x.shape

    def body(a_ref, b_ref, x_ref, g_ref, res_ref, dx_ref, dxb_ref, dg_ref):
        dx, dgr = _rms_bwd_vals(x_ref[...], g_ref[...], matmul(a_ref, b_ref))
        dx = dx + res_ref[...]
        dx_ref[...] = dx
        dxb_ref[...] = dx.astype(BF16)

        @pl.when(pl.program_id(0) == 0)
        def _():
            dg_ref[...] = jnp.zeros_like(dg_ref)

        dg_ref[...] += jnp.sum(dgr, axis=0, keepdims=True)

    row = pl.BlockSpec((TM_FUSED, d), lambda i: (i, 0))
    vec = pl.BlockSpec((1, d), lambda i: (0, 0))
    return pl.pallas_call(
        body, grid=(l // TM_FUSED,), in_specs=[a_spec, b_spec, row, vec, row], out_specs=[row, row, vec],
        out_shape=[jax.ShapeDtypeStruct((l, d), F32), jax.ShapeDtypeStruct((l, d), BF16),
                   jax.ShapeDtypeStruct((1, d), F32)],
        name=name, compiler_params=_params(("arbitrary",)),
    )(a, b, x, g, res)


def _mm_nt_rms_bwd(a, b, x, g, res, name):
    return _mm_rms_bwd(a, b, pl.BlockSpec((TM_FUSED, a.shape[1]), lambda i: (i, 0)),
                       pl.BlockSpec(b.shape, lambda i: (0, 0)),
                       lambda a_ref, b_ref: _dg(a_ref[...], b_ref[...], NT), x, g, res, name)


def _mm_cols_rms_bwd(a2, b4, x, g, res, name):
    h, _, wide = a2.shape
    s, _, n = b4.shape
    per = s // h

    def matmul(a_ref, b_ref):
        acc = None
        for j in range(s):
            part = _dg(a_ref[j // per, :, (j % per) * n:(j % per + 1) * n], b_ref[j], NT)
            acc = part if acc is None else acc + part
        return acc

    return _mm_rms_bwd(a2, b4, pl.BlockSpec((h, TM_FUSED, wide), lambda i: (0, i, 0)),
                       pl.BlockSpec(b4.shape, lambda i: (0, 0, 0)), matmul, x, g, res, name)


def _mix_fwd(attn, ys, g_attn, g_ssm):
    l, w = attn.shape

    def body(a_ref, y_ref, ga_ref, gs_ref, o_ref):
        for src, gr, off in ((a_ref, ga_ref, 0), (y_ref, gs_ref, w)):
            xv = src[...]
            r = lax.rsqrt(jnp.mean(xv * xv, axis=-1, keepdims=True) + EPS)
            o_ref[:, off:off + w] = (xv * r * gr[...]).astype(BF16)

    row = pl.BlockSpec((TM_EW, w), lambda i: (i, 0))
    vec = pl.BlockSpec((1, w), lambda i: (0, 0))
    return pl.pallas_call(
        body, grid=(l // TM_EW,), in_specs=[row, row, vec, vec],
        out_specs=pl.BlockSpec((TM_EW, 2 * w), lambda i: (i, 0)),
        out_shape=jax.ShapeDtypeStruct((l, 2 * w), BF16), name="mix_fwd",
        compiler_params=_params(("parallel",)),
    )(attn, ys, g_attn, g_ssm)


def _mix_bwd(attn, ys, g_attn, g_ssm, dmixed):
    l, w = attn.shape

    def body(a_ref, y_ref, ga_ref, gs_ref, dm_ref, da_ref, dy_ref, dga_ref, dgs_ref):
        @pl.when(pl.program_id(0) == 0)
        def _():
            dga_ref[...] = jnp.zeros_like(dga_ref)
            dgs_ref[...] = jnp.zeros_like(dgs_ref)

        for src, gr, off, dst, dgr in ((a_ref, ga_ref, 0, da_ref, dga_ref), (y_ref, gs_ref, w, dy_ref, dgs_ref)):
            dx, dg_rows = _rms_bwd_vals(src[...], gr[...], dm_ref[:, off:off + w])
            dst[...] = dx
            dgr[...] += jnp.sum(dg_rows, axis=0, keepdims=True)

    row = pl.BlockSpec((TM_EW, w), lambda i: (i, 0))
    vec = pl.BlockSpec((1, w), lambda i: (0, 0))
    return pl.pallas_call(
        body, grid=(l // TM_EW,),
        in_specs=[row, row, vec, vec, pl.BlockSpec((TM_EW, 2 * w), lambda i: (i, 0))],
        out_specs=[row, row, vec, vec],
        out_shape=[jax.ShapeDtypeStruct((l, w), F32), jax.ShapeDtypeStruct((l, w), F32),
                   jax.ShapeDtypeStruct((1, w), F32), jax.ShapeDtypeStruct((1, w), F32)],
        name="mix_bwd", compiler_params=_params(("arbitrary",)),
    )(attn, ys, g_attn, g_ssm, dmixed)


def _rope_tables(l):
    half = ROPE_DIM // 2
    inv_freq = jnp.power(ROPE_THETA, -jnp.arange(half, dtype=F32) / half)
    ang = jnp.arange(l, dtype=F32)[:, None] * inv_freq[None, :]
    cos, sin = jnp.cos(ang), jnp.sin(ang)
    ones = jnp.ones((l, HEAD_DIM - ROPE_DIM), F32)
    zeros = jnp.zeros((l, HEAD_DIM - ROPE_DIM), F32)
    zh = jnp.zeros((l, half), F32)
    c = jnp.concatenate([cos, cos, ones], axis=1)
    s_lo = jnp.concatenate([-sin, zh, zeros], axis=1)
    s_hi = jnp.concatenate([zh, sin, zeros], axis=1)
    return tuple(jnp.tile(t, (1, LANES // HEAD_DIM)) for t in (c, s_lo, s_hi))


def _rope_fwd(proj, tabs):
    l = proj.shape[0]
    nq = ATTN_WIDTH // LANES

    def body(p_ref, c_ref, lo_ref, hi_ref, o_ref):
        c, lo, hi = c_ref[...], lo_ref[...], hi_ref[...]
        for blk in range(nq + 1):
            t = p_ref[:, blk * LANES:(blk + 1) * LANES]
            rot = t * c + pltpu.roll(t, LANES - 8, 1) * lo + pltpu.roll(t, 8, 1) * hi
            o_ref[:, blk * LANES:(blk + 1) * LANES] = rot.astype(BF16)
        o_ref[:, (nq + 1) * LANES:] = p_ref[:, (nq + 1) * LANES:].astype(BF16)

    tab = pl.BlockSpec((TM_EW, LANES), lambda i: (i, 0))
    return pl.pallas_call(
        body, grid=(l // TM_EW,),
        in_specs=[pl.BlockSpec((TM_EW, QKV_WIDTH), lambda i: (i, 0)), tab, tab, tab],
        out_specs=pl.BlockSpec((TM_EW, QKV_WIDTH), lambda i: (i, 0)),
        out_shape=jax.ShapeDtypeStruct((l, QKV_WIDTH), BF16), name="rope_fwd",
        compiler_params=_params(("parallel",)),
    )(proj, *tabs)


def _rope_bwd(dq, dkv, du_ssm, dpre, d_skip, tabs):
    l = dq.shape[0]
    nq = ATTN_WIDTH // LANES

    def body(dq_ref, dkv_ref, du_ref, dpre_ref, ds_ref, c_ref, lo_ref, hi_ref, o_ref):
        c, lo, hi = c_ref[...], lo_ref[...], hi_ref[...]
        for blk in range(nq + 1):
            t = dq_ref[:, blk * LANES:(blk + 1) * LANES] if blk < nq else dkv_ref[:, :KV_WIDTH]
            g = t * c + pltpu.roll(t * lo, 8, 1) + pltpu.roll(t * hi, LANES - 8, 1)
            o_ref[:, blk * LANES:(blk + 1) * LANES] = g.astype(BF16)
        o_ref[:, (nq + 1) * LANES:QKV_WIDTH] = dkv_ref[:, KV_WIDTH:].astype(BF16)
        o_ref[:, QKV_WIDTH:] = (du_ref[...] + dpre_ref[...] * ds_ref[...]).astype(BF16)

    tab = pl.BlockSpec((TM_EW, LANES), lambda i: (i, 0))
    wide = pl.BlockSpec((TM_EW, SSM_WIDTH), lambda i: (i, 0))
    return pl.pallas_call(
        body, grid=(l // TM_EW,),
        in_specs=[wide, pl.BlockSpec((TM_EW, 2 * KV_WIDTH), lambda i: (i, 0)), wide, wide,
                  pl.BlockSpec((1, SSM_WIDTH), lambda i: (0, 0)), tab, tab, tab],
        out_specs=pl.BlockSpec((TM_EW, IN_WIDTH), lambda i: (i, 0)),
        out_shape=jax.ShapeDtypeStruct((l, IN_WIDTH), BF16), name="rope_bwd",
        compiler_params=_params(("parallel",)),
    )(dq, dkv, du_ssm, dpre, d_skip, *tabs)


_Q_COLS = ATTN_WIDTH // LANES
_SCALE = HEAD_DIM ** -0.5
_NEG = -1e30


def _window_specs(nb, width, col):
    return [
        pl.BlockSpec((BLOCK, width), lambda n: (jnp.maximum(n - 1, 0), col)),
        pl.BlockSpec((BLOCK, width), lambda n: (n, col)),
        pl.BlockSpec((BLOCK, width), lambda n: (jnp.minimum(n + 1, nb - 1), col)),
    ]


def _stacked_sink(sink_ref, heads):
    rid = lax.broadcasted_iota(jnp.int32, (len(heads) * BLOCK, 1), 0)
    sk = jnp.full(rid.shape, sink_ref[0, heads[-1]], F32)
    for g in range(len(heads) - 2, -1, -1):
        sk = jnp.where(rid < (g + 1) * BLOCK, sink_ref[0, heads[g]], sk)
    return sk


def _attn_fwd(qkv, sink):
    l = qkv.shape[0]
    nb = l // BLOCK
    grp = N_Q_HEADS // N_KV_HEADS

    def body(sink_ref, q_ref, k0, k1, k2, v0, v1, v2, o_ref, lse_ref):
        n = pl.program_id(0)
        q = q_ref[...]
        kw = jnp.concatenate([k0[...], k1[...], k2[...]], axis=0)
        vw = jnp.concatenate([v0[...], v1[...], v2[...]], axis=0)
        row = lax.broadcasted_iota(jnp.int32, (grp * BLOCK, 3 * BLOCK), 0)
        col = lax.broadcasted_iota(jnp.int32, (grp * BLOCK, 3 * BLOCK), 1)
        valid = jnp.abs(col - BLOCK - (row & (BLOCK - 1))) <= WINDOW
        valid &= jnp.logical_not((n == 0) & (col < BLOCK))
        valid &= jnp.logical_not((n == nb - 1) & (col >= 2 * BLOCK))
        for hk in range(N_KV_HEADS):
            heads = range(hk * grp, (hk + 1) * grp)
            qs = jnp.concatenate([q[:, h * HEAD_DIM:(h + 1) * HEAD_DIM] for h in heads], axis=0)
            kh = kw[:, hk * HEAD_DIM:(hk + 1) * HEAD_DIM]
            vh = vw[:, hk * HEAD_DIM:(hk + 1) * HEAD_DIM]
            s = jnp.where(valid, _dg(qs, kh, NT) * _SCALE, _NEG)
            sk = _stacked_sink(sink_ref, heads)
            m = jnp.maximum(jnp.max(s, axis=1, keepdims=True), sk)
            p = jnp.exp(s - m)
            denom = jnp.sum(p, axis=1, keepdims=True) + jnp.exp(sk - m)
            o = _dg((p / denom).astype(BF16), vh, NN)
            lse = m + jnp.log(denom)
            for g, h in enumerate(heads):
                o_ref[:, h * HEAD_DIM:(h + 1) * HEAD_DIM] = o[g * BLOCK:(g + 1) * BLOCK]
                lse_ref[:, h:h + 1] = lse[g * BLOCK:(g + 1) * BLOCK]

    return pl.pallas_call(
        body, grid=(nb,),
        in_specs=[pl.BlockSpec(memory_space=pltpu.SMEM),
                  pl.BlockSpec((BLOCK, ATTN_WIDTH), lambda n: (n, 0))]
        + _window_specs(nb, KV_WIDTH, _Q_COLS) + _window_specs(nb, KV_WIDTH, _Q_COLS + 1),
        out_specs=[pl.BlockSpec((BLOCK, ATTN_WIDTH), lambda n: (n, 0)),
                   pl.BlockSpec((BLOCK, N_Q_HEADS), lambda n: (n, 0))],
        out_shape=[jax.ShapeDtypeStruct((l, ATTN_WIDTH), F32), jax.ShapeDtypeStruct((l, N_Q_HEADS), F32)],
        name="attn_fwd", compiler_params=_params(("parallel",)),
    )(sink, qkv, qkv, qkv, qkv, qkv, qkv, qkv)


def _attn_bwd(qkv, attn, dattn, lse, sink):
    l = qkv.shape[0]
    nb = l // BLOCK
    grp = N_Q_HEADS // N_KV_HEADS
    win = 3 * BLOCK

    def body(sink_ref, q_ref, k0, k1, k2, v0, v1, v2, o_ref, d_ref, l_ref, dq_ref, dkv_ref, dsink_ref, ring_ref):
        n = pl.program_id(0)

        @pl.when(n == 0)
        def _():
            dsink_ref[...] = jnp.zeros_like(dsink_ref)
            ring_ref[...] = jnp.zeros_like(ring_ref)

        @pl.when(n < nb)
        def _():
            first, last = n == 0, n == nb - 1
            cat = lambda a, b, c: jnp.concatenate([a[...], b[...], c[...]], axis=0)
            q, kw, vw = q_ref[...], cat(k0, k1, k2), cat(v0, v1, v2)
            dov = d_ref[...]
            prod = o_ref[...] * dov
            dob = dov.astype(BF16)
            lse = l_ref[...]
            row = lax.broadcasted_iota(jnp.int32, (grp * BLOCK, win), 0)
            col = lax.broadcasted_iota(jnp.int32, (grp * BLOCK, win), 1)
            valid = jnp.abs(col - BLOCK - (row & (BLOCK - 1))) <= WINDOW
            valid &= jnp.logical_not(first & (col < BLOCK))
            valid &= jnp.logical_not(last & (col >= 2 * BLOCK))

            dsink_parts, dks, dvs = [], [], []
            for hk in range(N_KV_HEADS):
                heads = range(hk * grp, (hk + 1) * grp)
                ksl = slice(hk * HEAD_DIM, (hk + 1) * HEAD_DIM)
                hsl = [slice(h * HEAD_DIM, (h + 1) * HEAD_DIM) for h in heads]
                stack = lambda parts: jnp.concatenate(parts, axis=0)
                qs = stack([q[:, s_] for s_ in hsl])
                dos = stack([dob[:, s_] for s_ in hsl])
                deltas = stack([jnp.sum(prod[:, s_], axis=1, keepdims=True) for s_ in hsl])
                lses = stack([lse[:, h:h + 1] for h in heads])
                kh, vh = kw[:, ksl], vw[:, ksl]
                s = jnp.where(valid, _dg(qs, kh, NT) * _SCALE, _NEG)
                p = jnp.exp(s - lses)
                dp = _dg(dos, vh, NT)
                ds = (p * (dp - deltas) * _SCALE).astype(BF16)
                dq = _dg(ds, kh, NN)
                sink_rows = jnp.exp(_stacked_sink(sink_ref, heads) - lses) * deltas
                for g in range(grp):
                    dq_ref[:, hsl[g]] = dq[g * BLOCK:(g + 1) * BLOCK]
                    dsink_parts.append(jnp.sum(sink_rows[g * BLOCK:(g + 1) * BLOCK], axis=0, keepdims=True))
                dks.append(_dg(ds, qs, TN))
                dvs.append(_dg(p.astype(BF16), dos, TN))
            dsink_ref[...] -= jnp.concatenate(dsink_parts, axis=1)
            part = jnp.concatenate(dks + dvs, axis=1)
            ring_ref[(n + 2) % 3] += part[0:BLOCK]
            ring_ref[n % 3] += part[BLOCK:2 * BLOCK]
            ring_ref[(n + 1) % 3] = part[2 * BLOCK:]

        @pl.when(n >= 1)
        def _():
            dkv_ref[...] = ring_ref[(n + 2) % 3]

    centre = lambda n: jnp.minimum(n, nb - 1)
    window = lambda width, col: [
        pl.BlockSpec((BLOCK, width), lambda n: (jnp.maximum(centre(n) - 1, 0), col)),
        pl.BlockSpec((BLOCK, width), lambda n: (centre(n), col)),
        pl.BlockSpec((BLOCK, width), lambda n: (jnp.minimum(centre(n) + 1, nb - 1), col))]
    own = lambda width: pl.BlockSpec((BLOCK, width), lambda n: (centre(n), 0))
    return pl.pallas_call(
        body, grid=(nb + 1,),
        in_specs=[pl.BlockSpec(memory_space=pltpu.SMEM), own(ATTN_WIDTH)]
        + window(KV_WIDTH, _Q_COLS) + window(KV_WIDTH, _Q_COLS + 1)
        + [own(ATTN_WIDTH), own(ATTN_WIDTH), own(N_Q_HEADS)],
        out_specs=[own(ATTN_WIDTH), pl.BlockSpec((BLOCK, 2 * KV_WIDTH), lambda n: (jnp.maximum(n - 1, 0), 0)),
                   pl.BlockSpec((1, N_Q_HEADS), lambda n: (0, 0))],
        out_shape=[jax.ShapeDtypeStruct((l, ATTN_WIDTH), F32), jax.ShapeDtypeStruct((l, 2 * KV_WIDTH), F32),
                   jax.ShapeDtypeStruct((1, N_Q_HEADS), F32)],
        scratch_shapes=[pltpu.VMEM((3, BLOCK, 2 * KV_WIDTH), F32)],
        name="attn_bwd", compiler_params=_params(("arbitrary",)),
    )(sink, qkv, qkv, qkv, qkv, qkv, qkv, qkv, attn, dattn, lse)


def _ssm_disc(a_re, a_im, log_step, b_re, b_im):
    step = jnp.exp(log_step)[..., None]
    mag = jnp.exp(a_re * step)
    lb_re, lb_im = mag * jnp.cos(a_im * step), mag * jnp.sin(a_im * step)
    nr, ni = lb_re - 1.0, lb_im
    den = a_re * a_re + a_im * a_im
    f_re = ((nr * a_re + ni * a_im) / den)[..., None]
    f_im = ((ni * a_re - nr * a_im) / den)[..., None]
    return lb_re, lb_im, f_re * b_re - f_im * b_im, f_re * b_im + f_im * b_re


def _ssm_pack(lb_re, lb_im, bb_re, bb_im, c_re, c_im):
    eye = jnp.eye(SSM_CH // SSM_GROUP, dtype=F32)
    ng = SSM_CH // SSM_GROUP

    def diag_b(bb):
        t = bb.reshape(2, SSM_CB, ng, SSM_STATE, SSM_GROUP)
        return jnp.einsum('dkgpc,gh->dkgchp', t, eye).reshape(2, SSM_CB, SSM_CH, SSM_ST)

    def diag_c(cc):
        t = cc.reshape(2, SSM_CB, ng, SSM_GROUP, SSM_STATE)
        return jnp.einsum('dkgcp,gh->dkhpgc', t, eye).reshape(2, SSM_CB, SSM_ST, SSM_CH)

    bcat = jnp.concatenate([diag_b(bb_re), diag_b(bb_im)], axis=-1)
    ccat = jnp.concatenate([diag_c(c_re), -diag_c(c_im)], axis=-2)
    lam_re = lb_re.reshape(2, SSM_CB, 1, SSM_ST)
    lam_im = lb_im.reshape(2, SSM_CB, 1, SSM_ST)
    return bcat, ccat, lam_re, lam_im


def _ssm_unpack(dbcat, dccat, dlam_re, dlam_im):
    ng = SSM_CH // SSM_GROUP
    eye = jnp.eye(ng, dtype=F32)

    def undiag_b(t):
        t = t.reshape(2, SSM_CB, ng, SSM_GROUP, ng, SSM_STATE)
        return jnp.einsum('dkgchp,gh->dkgpc', t, eye).reshape(2, N_SSM_GROUPS, SSM_STATE, SSM_GROUP)

    def undiag_c(t):
        t = t.reshape(2, SSM_CB, ng, SSM_STATE, ng, SSM_GROUP)
        return jnp.einsum('dkhpgc,gh->dkgcp', t, eye).reshape(2, N_SSM_GROUPS, SSM_GROUP, SSM_STATE)

    dbb_re, dbb_im = undiag_b(dbcat[..., :SSM_ST]), undiag_b(dbcat[..., SSM_ST:])
    dc_re, dc_im = undiag_c(dccat[:, :, :SSM_ST]), -undiag_c(dccat[:, :, SSM_ST:])
    shape = (2, N_SSM_GROUPS, SSM_STATE)
    return dlam_re.reshape(shape), dlam_im.reshape(shape), dbb_re, dbb_im, dc_re, dc_im


def _to_segments(t):
    l, w = t.shape
    return t.reshape(N_SEG, l // N_SEG, w).transpose(1, 0, 2).reshape(l, w)


def _from_segments(t):
    l, w = t.shape
    return t.reshape(l // N_SEG, N_SEG, w).transpose(1, 0, 2).reshape(l, w)


SSM_RC = 256
SSM_JC = SSM_RC // N_SEG
_RE, _IM = pl.ds(0, SSM_ST), pl.ds(SSM_ST, SSM_ST)


def _cfma(ar, ai, xr, xi, br, bi):
    return ar * xr - ai * xi + br, ar * xi + ai * xr + bi


def _chunk_rows(ci, rev, nc):
    start = jnp.where(rev, (nc - 1 - ci) * SSM_RC, ci * SSM_RC)
    return pl.ds(pl.multiple_of(start, SSM_RC), SSM_RC)


def _scan_chunk(src, dst, ar, ai, rev, nj, ci, carry, prev_ref=None):
    def rows_of(staged, j, k):
        at = jnp.where(rev, SSM_JC - 1 - k, k) if staged else j
        return pl.ds(pl.multiple_of(at * N_SEG, N_SEG), N_SEG)

    for k in range(SSM_JC):
        jj = ci * SSM_JC + k
        j = jnp.where(rev, nj - 1 - jj, jj)
        rows = rows_of(src[1], j, k)
        nr, ni = _cfma(ar, ai, carry[0], carry[1], src[0][rows, _RE], src[0][rows, _IM])
        if dst is not None:
            rows = rows_of(dst[1], j, k)
            dst[0][rows, _RE] = nr
            dst[0][rows, _IM] = ni
        if prev_ref is None:
            carry = (nr, ni)
            continue
        jp = jnp.where(rev, j - 1, j + 1)
        if k == SSM_JC - 1:
            inside = jnp.where((jp >= 0) & (jp < nj), 1.0, 0.0)
            jp = jnp.clip(jp, 0, nj - 1)
        prow = pl.ds(pl.multiple_of(jp * N_SEG, N_SEG), N_SEG)
        xr, xi = prev_ref[prow, _RE], prev_ref[prow, _IM]
        sr, si = nr * xr + ni * xi, ni * xr - nr * xi
        if k == SSM_JC - 1:
            sr, si = inside * sr, inside * si
        carry = (nr, ni, carry[2] + sr, carry[3] + si)
    return carry


def _segment_inits(ar, ai, end_r, end_i, rev, nj):
    pr, pi = ar, ai
    for _ in range(int(math.log2(nj))):
        pr, pi = pr * pr - pi * pi, 2.0 * pr * pi
    seg = lax.broadcasted_iota(jnp.int32, end_r.shape, 0)
    zero = jnp.zeros_like(end_r)

    def chain(shift, keep):
        ir, ii = zero, zero
        for _ in range(N_SEG - 1):
            tr, ti = _cfma(pr, pi, ir, ii, end_r, end_i)
            ir = jnp.where(keep, pltpu.roll(tr, shift, 0), 0.0)
            ii = jnp.where(keep, pltpu.roll(ti, shift, 0), 0.0)
        return ir, ii

    up_r, up_i = chain(1, seg >= 1)
    dn_r, dn_i = chain(N_SEG - 1, seg <= N_SEG - 2)
    return jnp.where(rev, dn_r, up_r), jnp.where(rev, dn_i, up_i)


def _ssm_specs(l):
    act = pl.BlockSpec((l, SSM_CH), lambda k, d: (0, k))
    by_dir = pl.BlockSpec((None, l, SSM_CH), lambda k, d: (d, 0, k))
    bmat = pl.BlockSpec((None, None, SSM_CH, 2 * SSM_ST), lambda k, d: (d, k, 0, 0))
    cmat = pl.BlockSpec((None, None, 2 * SSM_ST, SSM_CH), lambda k, d: (d, k, 0, 0))
    lam = pl.BlockSpec((None, None, 1, SSM_ST), lambda k, d: (d, k, 0, 0))
    return act, by_dir, bmat, cmat, lam


def _ssm_fwd(u_seg, bcat, ccat, lam_re, lam_im):
    l = u_seg.shape[0]
    nj = l // N_SEG
    nc = l // SSM_RC

    def body(u_ref, b_ref, c_ref, lr_ref, li_ref, y_ref, keep_ref, xs_ref, stage0, stage1, keep_sem):
        k, d = pl.program_id(0), pl.program_id(1)
        rev = d == 1
        shape = (N_SEG, SSM_ST)
        ar, ai = jnp.broadcast_to(lr_ref[...], shape), jnp.broadcast_to(li_ref[...], shape)
        zero = jnp.zeros(shape, F32)

        def inputs(ci, stage):
            rows = _chunk_rows(ci, rev, nc)
            bu = _dg(u_ref[rows, :], b_ref[...], NN)
            stage[...] = bu
            xs_ref[rows, :] = bu

        def first(stage, ci, carry):
            return _scan_chunk((stage, True), None, ar, ai, rev, nj, ci, carry)

        def first_pass(t, carry):
            inputs(2 * t + 1, stage1)
            carry = first(stage0, 2 * t, carry)
            inputs(2 * t + 2, stage0)
            return first(stage1, 2 * t + 1, carry)

        inputs(0, stage0)
        carry = lax.fori_loop(0, nc // 2 - 1, first_pass, (zero, zero))
        inputs(nc - 1, stage1)
        carry = first(stage0, nc - 2, carry)
        end_r, end_i = first(stage1, nc - 1, carry)
        init = _segment_inits(ar, ai, end_r, end_i, rev, nj)

        def outputs(ci):
            rows = _chunk_rows(ci, rev, nc)
            y_ref[rows, :] = _dg(xs_ref[rows, :].astype(BF16), c_ref[...], NN)
            pltpu.make_async_copy(xs_ref.at[rows], keep_ref.at[d, k, rows], keep_sem).start()

        def second(ci, carry):
            return _scan_chunk((xs_ref, False), (xs_ref, False), ar, ai, rev, nj, ci, carry)

        def second_pass(ci, carry):
            outputs(ci - 1)
            return second(ci, carry)

        lax.fori_loop(1, nc, second_pass, second(0, init))
        outputs(nc - 1)
        pltpu.make_async_copy(xs_ref, keep_ref.at[d, k], keep_sem).wait()

    act, by_dir, bmat, cmat, lam = _ssm_specs(l)
    return pl.pallas_call(
        body, grid=(SSM_CB, 2), in_specs=[act, bmat, cmat, lam, lam], out_specs=[by_dir, ANY],
        out_shape=[jax.ShapeDtypeStruct((2, l, SSM_WIDTH), F32),
                   jax.ShapeDtypeStruct((2, SSM_CB, l, 2 * SSM_ST), F32)],
        scratch_shapes=[pltpu.VMEM((l, 2 * SSM_ST), F32), pltpu.VMEM((SSM_RC, 2 * SSM_ST), F32),
                        pltpu.VMEM((SSM_RC, 2 * SSM_ST), F32), pltpu.SemaphoreType.DMA],
        name="ssm_fwd", compiler_params=_params(("parallel", "arbitrary"), vmem_mb=56),
    )(u_seg, bcat.astype(BF16), ccat.astype(BF16), lam_re, lam_im)


def _ssm_bwd(u_seg, dy_seg, states, bcat, ccat, lam_re, lam_im):
    l = u_seg.shape[0]
    nj = l // N_SEG
    nc = l // SSM_RC

    def body(u_ref, dy_ref, keep_ref, b_ref, c_ref, lr_ref, li_ref,
             du_ref, db_ref, dc_ref, dlr_ref, dli_ref, xs_ref, gs_ref, stage0, stage1, keep_sem):
        k, d = pl.program_id(0), pl.program_id(1)
        rev = d == 1
        back = jnp.logical_not(rev)
        shape = (N_SEG, SSM_ST)
        ar, ai = jnp.broadcast_to(lr_ref[...], shape), -jnp.broadcast_to(li_ref[...], shape)
        zero = jnp.zeros(shape, F32)
        fetch = pltpu.make_async_copy(keep_ref.at[d, k], xs_ref, keep_sem)
        fetch.start()

        def inputs(ci, stage):
            rows = _chunk_rows(ci, back, nc)
            dx = _dg(dy_ref[rows, :], c_ref[...], NT)
            stage[...] = dx
            gs_ref[rows, :] = dx

        def first(stage, ci, carry):
            return _scan_chunk((stage, True), None, ar, ai, back, nj, ci, carry)

        def first_pass(t, carry):
            inputs(2 * t + 1, stage1)
            carry = first(stage0, 2 * t, carry)
            inputs(2 * t + 2, stage0)
            return first(stage1, 2 * t + 1, carry)

        inputs(0, stage0)
        carry = lax.fori_loop(0, nc // 2 - 1, first_pass, (zero, zero))
        inputs(nc - 1, stage1)
        carry = first(stage0, nc - 2, carry)
        end_r, end_i = first(stage1, nc - 1, carry)
        init = _segment_inits(ar, ai, end_r, end_i, back, nj)
        fetch.wait()
        db_ref[...] = jnp.zeros_like(db_ref)
        dc_ref[...] = jnp.zeros_like(dc_ref)

        def outputs(ci, stage):
            rows = _chunk_rows(ci, back, nc)
            g = stage[...].astype(BF16)
            dc_ref[...] += _dg(xs_ref[rows, :].astype(BF16), dy_ref[rows, :], TN)
            db_ref[...] += _dg(u_ref[rows, :], g, TN)
            du_ref[rows, :] = _dg(g, b_ref[...], NT)

        def second(ci, stage, carry):
            return _scan_chunk((gs_ref, False), (stage, True), ar, ai, back, nj, ci, carry, prev_ref=xs_ref)

        def second_pass(t, carry):
            outputs(2 * t, stage0)
            carry = second(2 * t + 1, stage1, carry)
            outputs(2 * t + 1, stage1)
            return second(2 * t + 2, stage0, carry)

        carry = lax.fori_loop(0, nc // 2 - 1, second_pass, second(0, stage0, init + (zero, zero)))
        outputs(nc - 2, stage0)
        gr, gi, acc_r, acc_i = second(nc - 1, stage1, carry)
        outputs(nc - 1, stage1)

        seg = lax.broadcasted_iota(jnp.int32, shape, 0)
        jb = jnp.where(rev, nj - 1, 0)
        erow = pl.ds(pl.multiple_of((nj - 1 - jb) * N_SEG, N_SEG), N_SEG)

        def before(t):
            up = jnp.where(seg >= 1, pltpu.roll(t, 1, 0), 0.0)
            down = jnp.where(seg <= N_SEG - 2, pltpu.roll(t, N_SEG - 1, 0), 0.0)
            return jnp.where(rev, down, up)

        init_r, init_i = before(xs_ref[erow, _RE]), before(xs_ref[erow, _IM])
        acc_r = acc_r + gr * init_r + gi * init_i
        acc_i = acc_i + gi * init_r - gr * init_i
        dlr_ref[...] = jnp.sum(acc_r, axis=0, keepdims=True)
        dli_ref[...] = jnp.sum(acc_i, axis=0, keepdims=True)

    act, by_dir, bmat, cmat, lam = _ssm_specs(l)
    return pl.pallas_call(
        body, grid=(SSM_CB, 2), in_specs=[act, act, ANY, bmat, cmat, lam, lam],
        out_specs=[by_dir, bmat, cmat, lam, lam],
        out_shape=[jax.ShapeDtypeStruct((2, l, SSM_WIDTH), F32),
                   jax.ShapeDtypeStruct(bcat.shape, F32), jax.ShapeDtypeStruct(ccat.shape, F32),
                   jax.ShapeDtypeStruct(lam_re.shape, F32), jax.ShapeDtypeStruct(lam_im.shape, F32)],
        scratch_shapes=[pltpu.VMEM((l, 2 * SSM_ST), F32), pltpu.VMEM((l, 2 * SSM_ST), F32),
                        pltpu.VMEM((SSM_RC, 2 * SSM_ST), F32), pltpu.VMEM((SSM_RC, 2 * SSM_ST), F32),
                        pltpu.SemaphoreType.DMA],
        name="ssm_bwd", compiler_params=_params(("parallel", "arbitrary"), vmem_mb=58),
    )(u_seg, dy_seg, states, bcat.astype(BF16), ccat.astype(BF16), lam_re, lam_im)


def _glu_fwd(y_ssm, u, d_skip, w_glu):
    l, w = u.shape

    def body(y_ref, u_ref, d_ref, w_ref, pre_ref, s_ref, ys_ref):
        pre = y_ref[...] + d_ref[...] * u_ref[...]
        z = _gelu(pre)
        s = _dg(z.astype(BF16), w_ref[...], NN)
        pre_ref[...] = pre
        s_ref[...] = s
        ys_ref[...] = z * _sigmoid(s)

    row = pl.BlockSpec((TM_EW, w), lambda i: (i, 0))
    out = jax.ShapeDtypeStruct((l, w), F32)
    return pl.pallas_call(
        body, grid=(l // TM_EW,),
        in_specs=[row, row, pl.BlockSpec((1, w), lambda i: (0, 0)), pl.BlockSpec((w, w), lambda i: (0, 0))],
        out_specs=[row, row, row], out_shape=[out, out, out], name="glu_fwd",
        compiler_params=_params(("parallel",)),
    )(y_ssm, u, d_skip, w_glu)


def _glu_bwd(pre, s, dys, u, d_skip, w_glu):
    l, w = u.shape

    def body(pre_ref, s_ref, dys_ref, u_ref, d_ref, w_ref, dpre_ref, z_ref, ds_ref, dd_ref):
        pre, dys = pre_ref[...], dys_ref[...]
        z = _gelu(pre)
        sig = _sigmoid(s_ref[...])
        ds = (dys * z * sig * (1.0 - sig)).astype(BF16)
        dz = dys * sig + _dg(ds, w_ref[...], NT)
        dpre = dz * _gelu_grad(pre)
        dpre_ref[...] = dpre
        z_ref[...] = z.astype(BF16)
        ds_ref[...] = ds

        @pl.when(pl.program_id(0) == 0)
        def _():
            dd_ref[...] = jnp.zeros_like(dd_ref)

        dd_ref[...] += jnp.sum(dpre * u_ref[...], axis=0, keepdims=True)

    row = pl.BlockSpec((TM_EW, w), lambda i: (i, 0))
    vec = pl.BlockSpec((1, w), lambda i: (0, 0))
    return pl.pallas_call(
        body, grid=(l // TM_EW,),
        in_specs=[row, row, row, row, vec, pl.BlockSpec((w, w), lambda i: (0, 0))],
        out_specs=[row, row, row, vec],
        out_shape=[jax.ShapeDtypeStruct((l, w), F32), jax.ShapeDtypeStruct((l, w), BF16),
                   jax.ShapeDtypeStruct((l, w), BF16), jax.ShapeDtypeStruct((1, w), F32)],
        name="glu_bwd", compiler_params=_params(("arbitrary",)),
    )(pre, s, dys, u, d_skip, w_glu)


TM_CV = 512
TC_CV = 256
TM_CF = 256
TC_CF = D_FF // 2
HALO = SUBLANES


def _conv_specs(l, col0, tm=TM_CV, tc=TC_CV):
    per = tm // HALO
    nh = l // HALO
    off = col0 // tc
    return [
        pl.BlockSpec((HALO, tc), lambda j, i: (jnp.maximum(i * per - 1, 0), j + off)),
        pl.BlockSpec((tm, tc), lambda j, i: (i, j + off)),
        pl.BlockSpec((HALO, tc), lambda j, i: (jnp.minimum((i + 1) * per, nh - 1), j + off)),
    ]


def _ext(prev_ref, mid_ref, next_ref, first, last):
    p = jnp.where(first, 0.0, prev_ref[...])
    n = jnp.where(last, 0.0, next_ref[...])
    return jnp.concatenate([p, mid_ref[...], n], axis=0)


def _shift_dn(t):
    return pltpu.roll(t, 1, 0)


def _shift_up(t):
    return pltpu.roll(t, t.shape[0] - 1, 0)


def _conv3(e, w_ref, b_ref):
    return w_ref[0:1, :] * _shift_dn(e) + w_ref[1:2, :] * e + w_ref[2:3, :] * _shift_up(e) + b_ref[...]


def _convffn_fwd(up_pre, conv_w, conv_b):
    l = up_pre.shape[0]
    tm, tc = TM_CF, TC_CF
    ni = l // tm
    wspec = lambda off: pl.BlockSpec((3, tc), lambda j, i: (0, j + off))
    bspec = lambda off: pl.BlockSpec((1, tc), lambda j, i: (0, j + off))
    voff = D_FF // tc

    def body(gp, gm, gn, vp, vm, vn, wg, bg, wv, bv, o_ref):
        i = pl.program_id(1)
        first, last = i == 0, i == ni - 1
        gate = _conv3(_ext(gp, gm, gn, first, last), wg, bg)[HALO:HALO + tm]
        val = _conv3(_ext(vp, vm, vn, first, last), wv, bv)[HALO:HALO + tm]
        o_ref[...] = (gate * _sigmoid(gate) * val).astype(BF16)

    return pl.pallas_call(
        body, grid=(D_FF // tc, ni),
        in_specs=_conv_specs(l, 0, tm, tc) + _conv_specs(l, D_FF, tm, tc)
        + [wspec(0), bspec(0), wspec(voff), bspec(voff)],
        out_specs=pl.BlockSpec((tm, tc), lambda j, i: (i, j)),
        out_shape=jax.ShapeDtypeStruct((l, D_FF), BF16), name="convffn_fwd",
        compiler_params=_params(("parallel", "parallel")),
    )(up_pre, up_pre, up_pre, up_pre, up_pre, up_pre, conv_w, conv_b, conv_w, conv_b)


def _convffn_bwd(up_pre, dact, conv_w, conv_b):
    l = up_pre.shape[0]
    ni = l // TM_CV
    wspec = lambda off: pl.BlockSpec((3, TC_CV), lambda j, i: (0, j + off))
    bspec = lambda off: pl.BlockSpec((1, TC_CV), lambda j, i: (0, j + off))
    voff = D_FF // TC_CV

    def body(gp, gm, gn, vp, vm, vn, dp, dm, dn, wg, bg, wv, bv, dup_ref, pg_ref, pv_ref):
        i = pl.program_id(1)
        first, last = i == 0, i == ni - 1
        ge, ve, de = _ext(gp, gm, gn, first, last), _ext(vp, vm, vn, first, last), _ext(dp, dm, dn, first, last)
        gate, val = _conv3(ge, wg, bg), _conv3(ve, wv, bv)
        sig = _sigmoid(gate)
        silu = gate * sig
        dgate = de * val * (sig + silu * (1.0 - sig))
        dval = de * silu
        mid = slice(HALO, HALO + TM_CV)
        rid = lax.broadcasted_iota(jnp.int32, (SUBLANES, TC_CV), 0)

        @pl.when(i == 0)
        def _():
            pg_ref[...] = jnp.zeros_like(pg_ref)
            pv_ref[...] = jnp.zeros_like(pv_ref)

        for half, (dup, e, w_ref, p_ref) in enumerate(((dgate, ge, wg, pg_ref), (dval, ve, wv, pv_ref))):
            dpre = w_ref[0:1, :] * _shift_up(dup) + w_ref[1:2, :] * dup + w_ref[2:3, :] * _shift_dn(dup)
            dup_ref[half] = dpre[mid].astype(BF16)
            dm_ = dup[mid]
            sums = [jnp.sum(dm_ * _shift_dn(e)[mid], axis=0, keepdims=True),
                    jnp.sum(dm_ * e[mid], axis=0, keepdims=True),
                    jnp.sum(dm_ * _shift_up(e)[mid], axis=0, keepdims=True),
                    jnp.sum(dm_, axis=0, keepdims=True)]
            acc = jnp.zeros((SUBLANES, TC_CV), F32)
            for k, sk in enumerate(sums):
                acc = jnp.where(rid == k, sk, acc)
            p_ref[...] += acc

    par = pl.BlockSpec((SUBLANES, TC_CV), lambda j, i: (0, j))
    dup, pg, pv = pl.pallas_call(
        body, grid=(D_FF // TC_CV, ni),
        in_specs=_conv_specs(l, 0) + _conv_specs(l, D_FF) + _conv_specs(l, 0)
        + [wspec(0), bspec(0), wspec(voff), bspec(voff)],
        out_specs=[pl.BlockSpec((2, TM_CV, TC_CV), lambda j, i: (0, i, j)), par, par],
        out_shape=[jax.ShapeDtypeStruct((2, l, D_FF), BF16),
                   jax.ShapeDtypeStruct((SUBLANES, D_FF), F32), jax.ShapeDtypeStruct((SUBLANES, D_FF), F32)],
        name="convffn_bwd", compiler_params=_params(("parallel", "arbitrary")),
    )(up_pre, up_pre, up_pre, up_pre, up_pre, up_pre, dact, dact, dact, conv_w, conv_b, conv_w, conv_b)
    return dup, jnp.concatenate([pg, pv], axis=1)


def _local_step(x, target, wb, sp, late_weights=None, ffn_grads_ready=None, ffn_grads_next=None):
    l = x.shape[0]
    tabs = _rope_tables(l)
    disc = _ssm_disc(sp["a_re"], sp["a_im"], sp["log_step"], sp["b_re"], sp["b_im"])
    bcat, ccat, lam_re, lam_im = _ssm_pack(*disc, sp["c_re"], sp["c_im"])
    d_skip = sp["d_skip"].reshape(1, SSM_WIDTH)

    big = min(l, 1024)
    h, proj, u = _rms_mm_split(x, sp["norm_mix_g"], wb["w_in"], QKV_WIDTH, "mm_in")
    qkv = _rope_fwd(proj, tabs)
    attn, lse = _attn_fwd(qkv, sp["sink"])
    u_seg = _to_segments(u).astype(BF16)
    y_seg, states = _ssm_fwd(u_seg, bcat, ccat, lam_re, lam_im)
    y_ssm = _from_segments(y_seg[0] + y_seg[1])
    pre, s_glu, ys = _glu_fwd(y_ssm, u, d_skip, wb["w_glu"])
    mixed = _mix_fwd(attn, ys, sp["norm_attn_g"], sp["norm_ssm_g"])
    x1, h2 = _mm_res_rms(mixed, wb["w_out"], x, sp["norm_ffn_g"], "mm_out")
    if late_weights is not None:
        wb = dict(wb, **late_weights(h2))
    up_pre = _mm_nn_cols(h2, wb["w_up"], big, "mm_up")
    act = _convffn_fwd(up_pre, sp["conv_w"], sp["conv_b"])
    loss, dx2, dx2b, d_final_g = _mm_res_loss(act, wb["w_down"], x1, sp["norm_final_g"].reshape(1, D_MODEL), target)

    g = {"norm_final_g": d_final_g.reshape(D_MODEL)}
    dact = _mm_nt(dx2b, wb["w_down"], big, D_FF // 2, F32, "mm_down_dx")
    g["w_down"] = _mm_tn(act, dx2b, D_FF // 2, 512, "mm_down_dw")
    dup_pre, conv_par = _convffn_bwd(up_pre, dact, sp["conv_w"], sp["conv_b"])
    g["conv_w"], g["conv_b"] = conv_par[0:3], conv_par[3:4]
    g["w_up"] = _mm_tn_cols(h2, dup_pre, wb["w_up"].shape[0], 512, "mm_up_dw")
    zero = ffn_grads_ready(g["w_up"], g["w_down"]) if ffn_grads_ready is not None else 0.0
    dx1, dx1b, g["norm_ffn_g"] = _mm_cols_rms_bwd(dup_pre, wb["w_up"], x1, sp["norm_ffn_g"] + zero, dx2, "mm_up_dx")
    dmixed = _mm_nt(dx1b, wb["w_out"], big, 1024, F32, "mm_out_dx")
    g["w_out"] = _mm_tn(mixed, dx1b, 1024, 1024, "mm_out_dw")
    zero = ffn_grads_next(dmixed) if ffn_grads_next is not None else 0.0
    dattn, dys, g["norm_attn_g"], g["norm_ssm_g"] = _mix_bwd(attn, ys, sp["norm_attn_g"] + zero, sp["norm_ssm_g"],
                                                            dmixed)
    dpre, zb, dsb, dd = _glu_bwd(pre, s_glu, dys, u, d_skip, wb["w_glu"])
    g["d_skip"] = dd.reshape(N_SSM_GROUPS, SSM_GROUP)
    g["w_glu"] = _mm_tn(zb, dsb, 512, 512, "mm_glu_dw")
    du_seg, dbcat, dccat, dlam_re, dlam_im = _ssm_bwd(u_seg, _to_segments(dpre).astype(BF16), states, bcat, ccat,
                                                      lam_re, lam_im)
    dlb_re, dlb_im, dbb_re, dbb_im, g["c_re"], g["c_im"] = _ssm_unpack(dbcat, dccat, dlam_re, dlam_im)
    _, disc_vjp = jax.vjp(_ssm_disc, sp["a_re"], sp["a_im"], sp["log_step"], sp["b_re"], sp["b_im"])
    g["a_re"], g["a_im"], g["log_step"], g["b_re"], g["b_im"] = disc_vjp((dlb_re, dlb_im, dbb_re, dbb_im))
    dq, dkv, g["sink"] = _attn_bwd(qkv, attn, dattn, lse, sp["sink"])
    dproj = _rope_bwd(dq, dkv, _from_segments(du_seg[0] + du_seg[1]), dpre, d_skip, tabs)
    g["w_in"] = _mm_tn(h, dproj, 512, IN_WIDTH, "mm_in_dw")
    grad_x, _, g["norm_mix_g"] = _mm_nt_rms_bwd(dproj, wb["w_in"], x, sp["norm_mix_g"], dx1, "mm_in_dx")
    return loss, grad_x, g


MESH = pl.DeviceIdType.MESH
ANY = pl.BlockSpec(memory_space=pl.ANY)


def _place():
    x, y, c = lax.axis_index("x"), lax.axis_index("y"), lax.axis_index("c")
    chips = [(1 - x, y), (x, 1 - y), (1 - x, 1 - y)]
    return x, y, c, chips


def _chip_index(px, py):
    return 2 * px + py


CHUNK_BYTES = 256 * 1024
MAX_CHUNKS = 16


def _row_chunks(rows, row_bytes, align):
    n = max(1, min(MAX_CHUNKS, (rows * row_bytes) // CHUNK_BYTES))
    per = -(-rows // n)
    per = -(-per // align) * align
    return [(r0, min(per, rows - r0)) for r0 in range(0, rows, per)]


def _align_of(dtype):
    return SUBLANES * 4 // jnp.dtype(dtype).itemsize


def _remote(src, dst, send_sem, recv_sem, to):
    return pltpu.make_async_remote_copy(src_ref=src, dst_ref=dst, send_sem=send_sem, recv_sem=recv_sem,
                                        device_id=to, device_id_type=MESH)


CAST_ROWS = 64


def _gather_weights(shards, dtypes):
    nw = len(shards)

    def body(*refs):
        w_refs, o_refs = refs[:nw], refs[nw:2 * nw]
        send_sems, recv_sems, in_sems, out_sems = refs[2 * nw:2 * nw + 4]
        raw, cast = refs[2 * nw + 4:3 * nw + 4], refs[3 * nw + 4:]
        x, y, c, chips = _place()
        mine = _chip_index(x, y)
        sibling = (x, y, 1 - c)

        def rows_of(ref, chip, r0, nr):
            return ref.at[chip, pl.ds(r0, nr), :]

        def copy(wi, k, src, dst, to):
            return _remote(src, dst, send_sems.at[wi, k], recv_sems.at[wi, k], to)

        geo = []
        for wi in range(nw):
            rows, cols = w_refs[wi].shape
            row_bytes = cols * jnp.dtype(dtypes[wi]).itemsize
            geo.append((rows // 2, _row_chunks(rows // 2, row_bytes, _align_of(dtypes[wi]))))

        stage_in = [pltpu.make_async_copy(w_refs[wi], raw[wi], in_sems.at[wi]) for wi in range(nw)]
        for cp in stage_in:
            cp.start()
        staged = [raw[wi] if dtypes[wi] == w_refs[wi].dtype else cast[wi] for wi in range(nw)]
        stage_out = []
        for wi in range(nw):
            stage_in[wi].wait()
            if staged[wi] is not raw[wi]:
                def cast_rows(i, _, wi=wi):
                    rows = pl.ds(pl.multiple_of(i * CAST_ROWS, CAST_ROWS), CAST_ROWS)
                    cast[wi][rows, :] = raw[wi][rows, :].astype(dtypes[wi])
                    return 0

                lax.fori_loop(0, w_refs[wi].shape[0] // CAST_ROWS, cast_rows, 0)
            cp = pltpu.make_async_copy(staged[wi], o_refs[wi].at[mine], out_sems.at[wi])
            cp.start()
            stage_out.append(cp)

        for wi in range(nw):
            hr, half_chunks = geo[wi]
            for j, chip in enumerate(chips):
                for r0, nr in half_chunks:
                    copy(wi, j, staged[wi].at[pl.ds(c * hr + r0, nr), :],
                         rows_of(o_refs[wi], mine, c * hr + r0, nr), (*chip, c)).start()
        for wi in range(nw):
            hr, half_chunks = geo[wi]
            for j, chip in enumerate(chips):
                got = rows_of(o_refs[wi], _chip_index(*chip), c * hr, hr)
                copy(wi, j, got, got, (*chip, c)).wait_recv()
                for r0, nr in half_chunks:
                    piece = rows_of(o_refs[wi], _chip_index(*chip), c * hr + r0, nr)
                    copy(wi, 3 + j, piece, piece, sibling).start()
        for wi in range(nw):
            hr = geo[wi][0]
            for j, chip in enumerate(chips):
                got = rows_of(o_refs[wi], _chip_index(*chip), (1 - c) * hr, hr)
                copy(wi, 3 + j, got, got, sibling).wait_recv()
        for wi in range(nw):
            hr = geo[wi][0]
            sent = rows_of(o_refs[wi], mine, c * hr, hr)
            for k in range(6):
                copy(wi, k, sent, sent, sibling).wait_send()
            stage_out[wi].wait()

    return pl.pallas_call(
        body, in_specs=[ANY] * nw, out_specs=[ANY] * nw,
        out_shape=[jax.ShapeDtypeStruct((4, *s.shape), t) for s, t in zip(shards, dtypes)],
        scratch_shapes=[pltpu.SemaphoreType.DMA((nw, 6)), pltpu.SemaphoreType.DMA((nw, 6)),
                        pltpu.SemaphoreType.DMA((nw,)), pltpu.SemaphoreType.DMA((nw,))]
        + [pltpu.VMEM(s.shape, s.dtype) for s in shards] + [pltpu.VMEM(s.shape, t) for s, t in zip(shards, dtypes)],
        name="gather_weights", compiler_params=_params(vmem_mb=40),
    )(*shards)


HBM = pl.BlockSpec(memory_space=pltpu.HBM)
SEM = pl.BlockSpec(memory_space=pltpu.SEMAPHORE)
EFFECT = pltpu.SideEffectType.DATAFLOW_SIDE_EFFECTING


def _cast_place(w, place, dtype, after, name):
    rows, cols = w.shape
    tr = _row_tile(rows, cols, _align_of(dtype))

    def body(p_ref, w_ref, after_ref, o_ref):
        del p_ref, after_ref
        o_ref[...] = w_ref[...].astype(dtype)

    grid_spec = pltpu.PrefetchScalarGridSpec(
        num_scalar_prefetch=1, grid=(rows // tr,),
        in_specs=[pl.BlockSpec((tr, cols), lambda i, p: (i, 0)), ANY],
        out_specs=pl.BlockSpec((None, tr, cols), lambda i, p: (p[1], i, 0)))
    return pl.pallas_call(body, grid_spec=grid_spec, out_shape=jax.ShapeDtypeStruct((4, rows, cols), dtype),
                          name=name, compiler_params=_params(("parallel",)))(place, w, after)


def _split_start(name, arrays, n_pairs, issue):
    n = len(arrays)

    def body(*refs):
        issue(refs[:n], refs[n:n + n_pairs], refs[n + n_pairs:n + 2 * n_pairs])
        token = refs[2 * n + 2 * n_pairs]
        token[...] = jnp.zeros_like(token)

    dma = pltpu.SemaphoreType.DMA(())
    outs = pl.pallas_call(
        body, name=name,
        out_shape=[dma] * (2 * n_pairs) + [pltpu.HBM(t.shape, t.dtype) for t in arrays]
        + [jax.ShapeDtypeStruct((SUBLANES, LANES), F32)],
        in_specs=[HBM] * n, out_specs=[SEM] * (2 * n_pairs) + [HBM] * n + [pl.BlockSpec(memory_space=pltpu.VMEM)],
        input_output_aliases={a: 2 * n_pairs + a for a in range(n)},
        compiler_params=pltpu.CompilerParams(has_side_effects=EFFECT),
    )(*[pltpu.with_memory_space_constraint(t, pltpu.HBM) for t in arrays])
    return outs[:n_pairs], outs[n_pairs:2 * n_pairs], outs[2 * n_pairs:2 * n_pairs + n], outs[-1]


def _split_wait(name, send_sems, recv_sems, flying, sizes, after):
    n, n_pairs = len(flying), len(send_sems)

    def body(*refs):
        x, y, c, _ = _place()
        for k, ref in enumerate(sizes(refs[:n])):
            cp = _remote(ref, ref, refs[n + k], refs[n + n_pairs + k], (x, y, 1 - c))
            cp.wait_send()
            cp.wait_recv()

    return pl.pallas_call(
        body, name=name, out_shape=[pltpu.HBM(t.shape, t.dtype) for t in flying],
        in_specs=[HBM] * n + [SEM] * (2 * n_pairs) + [ANY], out_specs=[HBM] * n,
        input_output_aliases={a: a for a in range(n)},
        compiler_params=pltpu.CompilerParams(has_side_effects=EFFECT),
    )(*flying, *send_sems, *recv_sems, after)


def _spread_start(lands):
    def issue(land_refs, send_sems, recv_sems):
        x, y, c, chips = _place()
        mine = _chip_index(x, y)
        for a, land in enumerate(land_refs):
            _, rows, cols = land.shape
            hr = rows // 2
            row_bytes = cols * jnp.dtype(land.dtype).itemsize
            for r0, nr in _row_chunks(hr, row_bytes, _align_of(land.dtype)):
                piece = land.at[mine, pl.ds(c * hr + r0, nr), :]
                for chip in chips:
                    for core in (0, 1):
                        _remote(piece, piece, send_sems[a], recv_sems[a], (*chip, core)).start()

    return _split_start("spread_start", lands, len(lands), issue)


def _spread_wait(send_sems, recv_sems, flying, after):
    return _split_wait("spread_wait", send_sems, recv_sems, flying,
                       lambda refs: [r.at[pl.ds(0, 3)] for r in refs], after)


def _pair_start(grads):
    n = len(grads)
    zones = [lax.empty((4, g.shape[1] // 2, g.shape[2]), F32) for g in grads]

    def issue(refs, send_sems, recv_sems):
        x, y, c, _ = _place()
        for a in range(n):
            g_ref, z_ref = refs[a], refs[n + a]
            _, rows, cols = g_ref.shape
            hr = rows // 2
            for k in range(4):
                for r0, nr in _row_chunks(hr, cols * 4, SUBLANES):
                    _remote(g_ref.at[k, pl.ds((1 - c) * hr + r0, nr), :], z_ref.at[k, pl.ds(r0, nr), :],
                            send_sems[a], recv_sems[a], (x, y, 1 - c)).start()

    return _split_start("pair_start", list(grads) + zones, n, issue)


def _pair_wait(send_sems, recv_sems, flying, after):
    n = len(flying) // 2
    out = _split_wait("pair_wait", send_sems, recv_sems, flying, lambda refs: list(refs[n:]), after)
    return out[:n], out[n:]


def _chip_start(sums):
    n = len(sums)
    zones = [lax.empty((3, *s.shape[1:]), s.dtype) for s in sums]

    def issue(refs, send_sems, recv_sems):
        x, y, c, chips = _place()
        for a in range(n):
            s_ref, z_ref = refs[a], refs[n + a]
            _, rows, cols = s_ref.shape
            row_bytes = cols * jnp.dtype(s_ref.dtype).itemsize
            for r0, nr in _row_chunks(rows, row_bytes, _align_of(s_ref.dtype)):
                for j, chip in enumerate(chips):
                    _remote(s_ref.at[_chip_index(*chip), pl.ds(r0, nr), :], z_ref.at[j, pl.ds(r0, nr), :],
                            send_sems[a], recv_sems[a], (*chip, c)).start()

    return _split_start("chip_start", list(sums) + zones, n, issue)


def _chip_wait(send_sems, recv_sems, flying, after):
    n = len(flying) // 2
    return _split_wait("chip_wait", send_sems, recv_sems, flying, lambda refs: list(refs[n:]), after)[n:]


def _pair_exchange(grads):
    na = len(grads)

    def body(*refs):
        g_refs, o_refs = refs[:na], refs[na:2 * na]
        send_sems, recv_sems = refs[2 * na:]
        x, y, c, _ = _place()
        sibling = (x, y, 1 - c)
        for ai in range(na):
            _, rows, cols = g_refs[ai].shape
            hr = rows // 2
            for k in range(4):
                for r0, nr in _row_chunks(hr, cols * 4, SUBLANES):
                    _remote(g_refs[ai].at[k, pl.ds((1 - c) * hr + r0, nr), :], o_refs[ai].at[k, pl.ds(r0, nr), :],
                            send_sems.at[ai], recv_sems.at[ai], sibling).start()
        for ai in range(na):
            _remote(o_refs[ai], o_refs[ai], send_sems.at[ai], recv_sems.at[ai], sibling).wait()

    return pl.pallas_call(
        body, in_specs=[ANY] * na, out_specs=[ANY] * na,
        out_shape=[jax.ShapeDtypeStruct((4, g.shape[1] // 2, g.shape[2]), F32) for g in grads],
        scratch_shapes=[pltpu.SemaphoreType.DMA((na,)), pltpu.SemaphoreType.DMA((na,))],
        name="pair_exchange",
    )(*grads)


def _row_tile(rows, cols, align):
    best = align
    for cand in range(align, rows + 1, align):
        if rows % cand == 0 and cand * cols <= 256 * 1024:
            best = cand
    return best


def _pair_sum(g, got, place, transit, name):
    _, rows, cols = g.shape
    hr = rows // 2
    tr = _row_tile(hr, cols, _align_of(transit))
    nt = hr // tr

    def body(p_ref, g_ref, r_ref, s_ref, own_ref):
        total = g_ref[...] + r_ref[...]
        s_ref[...] = total.astype(transit)

        @pl.when(pl.program_id(1) == p_ref[1])
        def _():
            own_ref[...] = total

    grid_spec = pltpu.PrefetchScalarGridSpec(
        num_scalar_prefetch=1, grid=(nt, 4),
        in_specs=[pl.BlockSpec((None, tr, cols), lambda i, k, p: (k, p[0] * nt + i, 0)),
                  pl.BlockSpec((None, tr, cols), lambda i, k, p: (k, i, 0))],
        out_specs=[pl.BlockSpec((None, tr, cols), lambda i, k, p: (k, i, 0)),
                   pl.BlockSpec((tr, cols), lambda i, k, p: (i, 0))])
    return pl.pallas_call(
        body, grid_spec=grid_spec,
        out_shape=[jax.ShapeDtypeStruct((4, hr, cols), transit), jax.ShapeDtypeStruct((hr, cols), F32)],
        name=name, compiler_params=_params(("parallel", "arbitrary")),
    )(place, g, got)


def _chip_exchange(sums):
    na = len(sums)

    def body(*refs):
        s_refs, o_refs = refs[:na], refs[na:2 * na]
        send_sems, recv_sems = refs[2 * na:]
        x, y, c, chips = _place()
        for ai in range(na):
            _, rows, cols = s_refs[ai].shape
            row_bytes = cols * jnp.dtype(s_refs[ai].dtype).itemsize
            for r0, nr in _row_chunks(rows, row_bytes, _align_of(s_refs[ai].dtype)):
                for j, chip in enumerate(chips):
                    _remote(s_refs[ai].at[_chip_index(*chip), pl.ds(r0, nr), :], o_refs[ai].at[j, pl.ds(r0, nr), :],
                            send_sems.at[ai, j], recv_sems.at[ai, j], (*chip, c)).start()
        for ai in range(na):
            for j, chip in enumerate(chips):
                _remote(o_refs[ai].at[j], o_refs[ai].at[j], send_sems.at[ai, j], recv_sems.at[ai, j],
                        (*chip, c)).wait()

    return pl.pallas_call(
        body, in_specs=[ANY] * na, out_specs=[ANY] * na,
        out_shape=[jax.ShapeDtypeStruct((3, *s.shape[1:]), s.dtype) for s in sums],
        scratch_shapes=[pltpu.SemaphoreType.DMA((na, 3)), pltpu.SemaphoreType.DMA((na, 3))],
        name="chip_exchange",
    )(*sums)


def _chip_sum(own, landed, name):
    hr, cols = own.shape
    tr = _row_tile(hr, cols, _align_of(landed.dtype))

    def body(o_ref, l_ref, f_ref):
        acc = o_ref[...]
        for j in range(3):
            acc = acc + l_ref[j].astype(F32)
        f_ref[...] = acc

    return pl.pallas_call(
        body, grid=(hr // tr,),
        in_specs=[pl.BlockSpec((tr, cols), lambda i: (i, 0)), pl.BlockSpec((3, tr, cols), lambda i: (0, i, 0))],
        out_specs=pl.BlockSpec((tr, cols), lambda i: (i, 0)),
        out_shape=jax.ShapeDtypeStruct((hr, cols), F32), name=name,
        compiler_params=_params(("parallel",)),
    )(own, landed)


def _final_exchange(halves, small):
    nh = len(halves)

    def body(*refs):
        h_refs, s_ref = refs[:nh], refs[nh]
        o_refs, so_ref = refs[nh + 1:2 * nh + 1], refs[2 * nh + 1]
        send_sems, recv_sems, local_sem, ssend_sems, srecv_sems = refs[2 * nh + 2:]
        x, y, c, _ = _place()
        me = 4 * x + 2 * y + c
        sibling = (x, y, 1 - c)
        for hi in range(nh):
            hr, cols = h_refs[hi].shape
            for r0, nr in _row_chunks(hr, cols * 4, SUBLANES):
                _remote(h_refs[hi].at[pl.ds(r0, nr), :], o_refs[hi].at[pl.ds(r0, nr), :],
                        send_sems.at[hi], recv_sems.at[hi], sibling).start()
        small_cps = [pltpu.make_async_copy(s_ref, so_ref.at[me], local_sem)]
        for r in range(1, 8):
            fx, fy, fc = (r >> 2) & 1, (r >> 1) & 1, r & 1
            peer = (1 - x if fx else x, 1 - y if fy else y, 1 - c if fc else c)
            small_cps.append(_remote(s_ref, so_ref.at[me], ssend_sems.at[r - 1], srecv_sems.at[r - 1], peer))
        for cp in small_cps:
            cp.start()
        for hi in range(nh):
            _remote(h_refs[hi], o_refs[hi], send_sems.at[hi], recv_sems.at[hi], sibling).wait()
        for cp in small_cps:
            cp.wait()

    return pl.pallas_call(
        body, in_specs=[ANY] * (nh + 1), out_specs=[ANY] * (nh + 1),
        out_shape=[jax.ShapeDtypeStruct(h.shape, F32) for h in halves]
        + [jax.ShapeDtypeStruct((8, *small.shape), F32)],
        scratch_shapes=[pltpu.SemaphoreType.DMA((nh,)), pltpu.SemaphoreType.DMA((nh,)),
                        pltpu.SemaphoreType.DMA, pltpu.SemaphoreType.DMA((7,)), pltpu.SemaphoreType.DMA((7,))],
        name="final_exchange",
    )(*halves, small)


def _adamw(w, g, m, v, name):
    shape = w.shape
    n = w.size
    if w.ndim >= 2 and shape[-1] >= LANES:
        two_d = (n // shape[-1], shape[-1])
    elif n % LANES == 0:
        two_d = (n // LANES, LANES)
    else:
        two_d = (1, n)
    r, c = two_d
    tr = r
    for cand in (512, 256, 176, 128, 64):
        if r > cand and r % cand == 0 and cand * c <= 256 * 1024:
            tr = cand
            break
    c1 = 1.0 - ADAM_B1 ** ADAM_STEP
    c2 = 1.0 - ADAM_B2 ** ADAM_STEP

    def body(w_ref, g_ref, m_ref, v_ref, d_ref, nm_ref, nv_ref):
        gv = g_ref[...]
        nm = ADAM_B1 * m_ref[...] + (1.0 - ADAM_B1) * gv
        nv = ADAM_B2 * v_ref[...] + (1.0 - ADAM_B2) * (gv * gv)
        d_ref[...] = -ADAM_LR * ((nm / c1) / (jnp.sqrt(nv / c2) + ADAM_EPS) + ADAM_WD * w_ref[...])
        nm_ref[...] = nm
        nv_ref[...] = nv

    spec = pl.BlockSpec((tr, c), lambda i: (i, 0))
    out = jax.ShapeDtypeStruct((r, c), F32)
    d, nm, nv = pl.pallas_call(
        body, grid=(r // tr,), in_specs=[spec] * 4, out_specs=[spec] * 3, out_shape=[out] * 3, name=name,
        compiler_params=_params(("parallel",)),
    )(w.reshape(two_d), g.reshape(two_d), m.reshape(two_d), v.reshape(two_d))
    return d.reshape(shape), nm.reshape(shape), nv.reshape(shape)


def _adamw_many(ws, gs, ms, vs, name):
    n = len(ws)
    c1 = 1.0 - ADAM_B1 ** ADAM_STEP
    c2 = 1.0 - ADAM_B2 ** ADAM_STEP

    def body(*refs):
        w_refs, g_refs, m_refs, v_refs = (refs[k * n:(k + 1) * n] for k in range(4))
        d_refs, nm_refs, nv_refs = (refs[(4 + k) * n:(5 + k) * n] for k in range(3))
        for i in range(n):
            gv = g_refs[i][...]
            nm = ADAM_B1 * m_refs[i][...] + (1.0 - ADAM_B1) * gv
            nv = ADAM_B2 * v_refs[i][...] + (1.0 - ADAM_B2) * (gv * gv)
            d_refs[i][...] = -ADAM_LR * ((nm / c1) / (jnp.sqrt(nv / c2) + ADAM_EPS) + ADAM_WD * w_refs[i][...])
            nm_refs[i][...] = nm
            nv_refs[i][...] = nv

    vmem = pl.BlockSpec(memory_space=pltpu.VMEM)
    shapes = [jax.ShapeDtypeStruct(t.shape, F32) for t in ws]
    outs = pl.pallas_call(body, in_specs=[vmem] * (4 * n), out_specs=[vmem] * (3 * n), out_shape=shapes * 3,
                          name=name, compiler_params=_params(vmem_mb=56))(*ws, *gs, *ms, *vs)
    return outs[:n], outs[n:2 * n], outs[2 * n:]


BIG = ("w_in", "w_glu", "w_out", "w_up", "w_down")
WEIGHTS = ("norm_mix_g", "w_in", "a_re", "a_im", "log_step", "b_re", "b_im", "c_re", "c_im", "d_skip", "w_glu",
           "sink", "norm_attn_g", "norm_ssm_g", "w_out", "norm_ffn_g", "w_up", "conv_w", "conv_b", "w_down",
           "norm_final_g")
SMALL = ("norm_mix_g", "a_re", "a_im", "log_step", "b_re", "b_im", "c_re", "c_im", "d_skip", "sink",
         "norm_attn_g", "norm_ssm_g", "norm_ffn_g", "conv_w", "conv_b", "norm_final_g")
SMALL_ROWS = 40
N_DEV = 8


def _by_owner(name, g):
    if name == "w_up":
        return g
    if name == "w_in":
        return g.reshape(g.shape[0], 4, g.shape[1] // 4).transpose(1, 0, 2)
    return g.reshape(4, g.shape[0] // 4, g.shape[1])


def kernel(x, norm_mix_g, w_in, a_re, a_im, log_step, b_re, b_im, c_re, c_im, d_skip, w_glu, sink, norm_attn_g, norm_ssm_g, w_out, norm_ffn_g, w_up, conv_w, conv_b, w_down, norm_final_g, loss_target, m_norm_mix_g, m_w_in, m_a_re, m_a_im, m_log_step, m_b_re, m_b_im, m_c_re, m_c_im, m_d_skip, m_w_glu, m_sink, m_norm_attn_g, m_norm_ssm_g, m_w_out, m_norm_ffn_g, m_w_up, m_conv_w, m_conv_b, m_w_down, m_norm_final_g, v_norm_mix_g, v_w_in, v_a_re, v_a_im, v_log_step, v_b_re, v_b_im, v_c_re, v_c_im, v_d_skip, v_w_glu, v_sink, v_norm_attn_g, v_norm_ssm_g, v_w_out, v_norm_ffn_g, v_w_up, v_conv_w, v_conv_b, v_w_down, v_norm_final_g):
    given = dict(locals())
    w = {n: given[n] for n in WEIGHTS}
    m = {n: given["m_" + n] for n in WEIGHTS}
    v = {n: given["v_" + n] for n in WEIGHTS}
    xy = 2 * lax.axis_index("x") + lax.axis_index("y")

    core = lax.axis_index("c")
    place = jnp.stack([core, xy]).astype(jnp.int32)

    conv_rows = jnp.pad(w["conv_w"][0], ((0, 2 * SUBLANES - 3), (0, 0)))
    early = ("w_in", "w_glu", "w_out")
    *gathered, conv_all = _gather_weights([w[n][0] for n in early] + [conv_rows], [BF16] * len(early) + [F32])
    late = ("w_up", "w_down")
    send_sems, recv_sems, flying, token = _spread_start(
        [_cast_place(w[n][0], place, BF16, conv_all, "cast_" + n) for n in late])
    rows = lambda t: t.reshape(4 * t.shape[1], t.shape[2])
    wb = {"w_in": gathered[0].transpose(1, 0, 2).reshape(D_MODEL, IN_WIDTH), "w_glu": rows(gathered[1]),
          "w_out": rows(gathered[2])}

    def late_weights(after):
        w_up4, w_down4 = _spread_wait(send_sems, recv_sems, flying, after)
        return {"w_up": w_up4, "w_down": rows(w_down4)}

    sp = {n: w[n][0] for n in ("a_re", "a_im", "log_step", "b_re", "b_im", "c_re", "c_im", "d_skip",
                               "norm_mix_g", "norm_attn_g", "norm_ssm_g", "norm_ffn_g", "sink", "conv_b")}
    for n in ("norm_mix_g", "norm_attn_g", "norm_ssm_g", "norm_ffn_g", "sink", "conv_b"):
        sp[n] = sp[n].reshape(1, -1)
    sp["norm_mix_g"] = sp["norm_mix_g"] + token[:1, :1]
    sp["conv_w"] = conv_all[:, :3].transpose(1, 0, 2).reshape(3, 2 * D_FF)
    sp["norm_final_g"] = w["norm_final_g"]
    flight = {}

    def ffn_grads_ready(dw_up, dw_down):
        *flight["pair"], token = _pair_start([dw_up, _by_owner("w_down", dw_down)])
        return token[:1, :1]

    def ffn_grads_next(after):
        mine, got = _pair_wait(*flight["pair"], after)
        sums, flight["own"] = zip(*[_pair_sum(a, b, place, BF16, "pair_sum_" + n) for n, a, b in zip(late, mine, got)])
        *flight["chip"], token = _chip_start(list(sums))
        return token[:1, :1]

    loss, grad_x, g = _local_step(x[0], loss_target[0], wb, sp, late_weights, ffn_grads_ready, ffn_grads_next)

    flat = jnp.concatenate([g[n].reshape(-1) for n in SMALL] + [loss.reshape(-1)])
    pad = N_DEV * SMALL_ROWS * D_MODEL - flat.shape[0]
    small = jnp.concatenate([flat, jnp.zeros((pad,), F32)]).reshape(4, 2 * SMALL_ROWS, D_MODEL)
    by_owner = [_by_owner(n, g[n]) for n in early] + [small]
    got = _pair_exchange(by_owner)
    transit = [BF16] * len(early) + [F32]
    chip_sums, own_sums = zip(*[_pair_sum(a, b, place, t, "pair_sum_" + n)
                                for n, a, b, t in zip(early + ("small",), by_owner, got, transit)])
    landed = _chip_exchange(list(chip_sums))
    halves = {n: _chip_sum(o, t, "chip_sum_" + n) for n, o, t in zip(early + ("small",), own_sums, landed)}
    late_landed = _chip_wait(*flight["chip"], grad_x)
    for n, o, t in zip(late, flight["own"], late_landed):
        halves[n] = _chip_sum(o, t, "chip_sum_" + n)
    *others, small_all = _final_exchange([halves[n] for n in BIG], halves["small"])
    grads = {n: jnp.concatenate([jnp.where(core == 0, halves[n], o), jnp.where(core == 0, o, halves[n])], axis=0)
             for n, o in zip(BIG, others)}
    flat = small_all.reshape(-1)
    off = 0
    for n in SMALL:
        shape = (3, 4 * w[n].shape[-1]) if n == "conv_w" else w[n].shape[1:] if n != "norm_final_g" else w[n].shape
        size = math.prod(shape)
        grads[n] = flat[off:off + size].reshape(shape)
        off += size
    loss = flat[off]
    cw = w["conv_w"].shape[-1]
    grads["conv_w"] = lax.dynamic_slice_in_dim(grads["conv_w"], xy * cw, cw, axis=1)
    grads = {n: grads[n].reshape(w[n].shape) for n in WEIGHTS}

    delta, new_m, new_v = {}, {}, {}
    for n in BIG:
        delta[n], new_m[n], new_v[n] = _adamw(w[n], grads[n], m[n], v[n], "adamw_" + n)
    for group, name in ((("b_re", "b_im"), "adamw_b"), (tuple(n for n in SMALL if n not in ("b_re", "b_im")), "adamw_small")):
        row = lambda t: t.reshape(1, -1) if t.ndim == 1 else t
        d_, m_, v_ = _adamw_many(*[[row(t[n]) for n in group] for t in (w, grads, m, v)], name)
        for n, dn, mn, vn in zip(group, d_, m_, v_):
            delta[n], new_m[n], new_v[n] = (t.reshape(w[n].shape) for t in (dn, mn, vn))
    return (loss, grad_x[None], *[grads[n] for n in WEIGHTS], *[delta[n] for n in WEIGHTS],
            *[new_m[n] for n in WEIGHTS], *[new_v[n] for n in WEIGHTS])
```

```python
import functools
import math

import jax
import jax.numpy as jnp
from jax import lax
from jax.experimental import pallas as pl
from jax.experimental.pallas import tpu as pltpu

F32 = jnp.float32
BF16 = jnp.bfloat16

D_MODEL = 1024
N_Q_HEADS = 8
N_KV_HEADS = 2
HEAD_DIM = 64
ATTN_WIDTH = 512
KV_WIDTH = 128
QKV_WIDTH = ATTN_WIDTH + 2 * KV_WIDTH
WINDOW = 128
BLOCK = 128
ROPE_DIM = 16
ROPE_THETA = 500000.0
SSM_WIDTH = 512
SSM_GROUP = 16
N_SSM_GROUPS = 32
SSM_STATE = 64
IN_WIDTH = 1280
D_FF = 2816
EPS = 1e-6
ADAM_LR = 0.001
ADAM_B1 = 0.9
ADAM_B2 = 0.999
ADAM_EPS = 1e-08
ADAM_WD = 0.01
ADAM_STEP = 10

VMEM_BYTES_V7X = 64 * 1024 * 1024
SUBLANES = 8
LANES = 128
SSM_CB = 4
SSM_CH = 128
SSM_ST = 512
N_SEG = SUBLANES

NN = (((1,), (0,)), ((), ()))
NT = (((1,), (1,)), ((), ()))
TN = (((0,), (0,)), ((), ()))


def _params(sem=None, vmem_mb=48):
    return pltpu.CompilerParams(dimension_semantics=sem, vmem_limit_bytes=vmem_mb * 1024 * 1024)


def _dg(a, b, dims):
    return lax.dot_general(a, b, dims, preferred_element_type=F32)


def _sigmoid(x):
    return 1.0 / (1.0 + jnp.exp(-x))


_SQRT_HALF = 0.7071067811865476
_INV_SQRT_2PI = 0.3989422804014327


def _gelu(x):
    return 0.5 * x * (1.0 + lax.erf(x * _SQRT_HALF))


def _gelu_grad(x):
    return 0.5 * (1.0 + lax.erf(x * _SQRT_HALF)) + x * (_INV_SQRT_2PI * jnp.exp(-0.5 * x * x))


def _mm_nt(a, b, tm, tn, out_dtype, name):
    m, k = a.shape
    n = b.shape[0]

    def body(a_ref, b_ref, o_ref):
        o_ref[...] = _dg(a_ref[...], b_ref[...], NT).astype(out_dtype)

    return pl.pallas_call(
        body, grid=(m // tm, n // tn),
        in_specs=[pl.BlockSpec((tm, k), lambda i, j: (i, 0)), pl.BlockSpec((tn, k), lambda i, j: (j, 0))],
        out_specs=pl.BlockSpec((tm, tn), lambda i, j: (i, j)),
        out_shape=jax.ShapeDtypeStruct((m, n), out_dtype), name=name,
        compiler_params=_params(("parallel", "parallel")),
    )(a, b)


def _mm_tn(a, b, tm, tn, name):
    k, m = a.shape
    n = b.shape[1]

    def body(a_ref, b_ref, o_ref):
        o_ref[...] = _dg(a_ref[...], b_ref[...], TN)

    return pl.pallas_call(
        body, grid=(m // tm, n // tn),
        in_specs=[pl.BlockSpec((k, tm), lambda i, j: (0, i)), pl.BlockSpec((k, tn), lambda i, j: (0, j))],
        out_specs=pl.BlockSpec((tm, tn), lambda i, j: (i, j)),
        out_shape=jax.ShapeDtypeStruct((m, n), F32), name=name,
        compiler_params=_params(("parallel", "parallel")),
    )(a, b)


def _mm_nn_cols(a, b4, tm, name):
    m, k = a.shape
    s, _, n = b4.shape

    def body(a_ref, b_ref, o_ref):
        o_ref[...] = _dg(a_ref[...], b_ref[...], NN)

    return pl.pallas_call(
        body, grid=(m // tm, s),
        in_specs=[pl.BlockSpec((tm, k), lambda i, j: (i, 0)), pl.BlockSpec((None, k, n), lambda i, j: (j, 0, 0))],
        out_specs=pl.BlockSpec((tm, n), lambda i, j: (i, j)),
        out_shape=jax.ShapeDtypeStruct((m, s * n), F32), name=name,
        compiler_params=_params(("parallel", "parallel")),
    )(a, b4)


def _mm_tn_cols(a, b2, s, tm, name):
    k, m = a.shape
    h, _, wide = b2.shape
    per = s // h
    n = wide // per

    def body(a_ref, b_ref, o_ref):
        o_ref[...] = _dg(a_ref[...], b_ref[...], TN)

    return pl.pallas_call(
        body, grid=(s, m // tm),
        in_specs=[pl.BlockSpec((k, tm), lambda j, i: (0, i)),
                  pl.BlockSpec((None, k, n), lambda j, i: (j // per, 0, j % per))],
        out_specs=pl.BlockSpec((None, tm, n), lambda j, i: (j, i, 0)),
        out_shape=jax.ShapeDtypeStruct((s, m, n), F32), name=name,
        compiler_params=_params(("parallel", "parallel")),
    )(a, b2)


TM_EW = 256


def _rms_bwd_vals(xv, gv, dy):
    r = lax.rsqrt(jnp.mean(xv * xv, axis=-1, keepdims=True) + EPS)
    xh = xv * r
    dxh = dy * gv
    dx = r * (dxh - xh * jnp.mean(dxh * xh, axis=-1, keepdims=True))
    return dx, dy * xh


TM_FUSED = 256


def _rms_vals(xv, gv):
    return xv * lax.rsqrt(jnp.mean(xv * xv, axis=-1, keepdims=True) + EPS) * gv


def _rms_mm_split(x, g, w, split, name):
    l, d = x.shape
    n = w.shape[1]

    def body(x_ref, g_ref, w_ref, h_ref, lo_ref, hi_ref):
        h = _rms_vals(x_ref[...], g_ref[...]).astype(BF16)
        h_ref[...] = h
        out = _dg(h, w_ref[...], NN)
        lo_ref[...] = out[:, :split]
        hi_ref[...] = out[:, split:]

    row = lambda width: pl.BlockSpec((TM_FUSED, width), lambda i: (i, 0))
    return pl.pallas_call(
        body, grid=(l // TM_FUSED,),
        in_specs=[row(d), pl.BlockSpec((1, d), lambda i: (0, 0)), pl.BlockSpec((d, n), lambda i: (0, 0))],
        out_specs=[row(d), row(split), row(n - split)],
        out_shape=[jax.ShapeDtypeStruct((l, d), BF16), jax.ShapeDtypeStruct((l, split), F32),
                   jax.ShapeDtypeStruct((l, n - split), F32)],
        name=name, compiler_params=_params(("parallel",)),
    )(x, g, w)


def _mm_res_rms(a, b, res, g, name):
    l, k = a.shape
    d = b.shape[1]

    def body(a_ref, b_ref, r_ref, g_ref, x_ref, h_ref):
        xv = r_ref[...] + _dg(a_ref[...], b_ref[...], NN)
        x_ref[...] = xv
        h_ref[...] = _rms_vals(xv, g_ref[...]).astype(BF16)

    row = lambda width: pl.BlockSpec((TM_FUSED, width), lambda i: (i, 0))
    return pl.pallas_call(
        body, grid=(l // TM_FUSED,),
        in_specs=[row(k), pl.BlockSpec((k, d), lambda i: (0, 0)), row(d), pl.BlockSpec((1, d), lambda i: (0, 0))],
        out_specs=[row(d), row(d)],
        out_shape=[jax.ShapeDtypeStruct((l, d), F32), jax.ShapeDtypeStruct((l, d), BF16)],
        name=name, compiler_params=_params(("parallel",)),
    )(a, b, res, g)


def _mm_res_loss(a, b, res, g, target):
    l, k = a.shape
    d = b.shape[1]

    def body(a_ref, b_ref, r_ref, g_ref, t_ref, loss_ref, dx_ref, dxb_ref, dg_ref):
        xv = r_ref[...] + _dg(a_ref[...], b_ref[...], NN)
        gv = g_ref[...]
        r = lax.rsqrt(jnp.mean(xv * xv, axis=-1, keepdims=True) + EPS)
        xh = xv * r
        e = xh * gv - t_ref[...]
        part = jnp.sum(jnp.sum(e * e, axis=1, keepdims=True), axis=0, keepdims=True) * (0.5 / d)
        dy = e * (1.0 / d)
        dxh = dy * gv
        dx = r * (dxh - xh * jnp.mean(dxh * xh, axis=-1, keepdims=True))
        dx_ref[...] = dx
        dxb_ref[...] = dx.astype(BF16)

        @pl.when(pl.program_id(0) == 0)
        def _():
            dg_ref[...] = jnp.zeros_like(dg_ref)
            loss_ref[...] = jnp.zeros_like(loss_ref)

        dg_ref[...] += jnp.sum(dy * xh, axis=0, keepdims=True)
        loss_ref[...] += part

    row = lambda width: pl.BlockSpec((TM_FUSED, width), lambda i: (i, 0))
    vec = pl.BlockSpec((1, d), lambda i: (0, 0))
    return pl.pallas_call(
        body, grid=(l // TM_FUSED,),
        in_specs=[row(k), pl.BlockSpec((k, d), lambda i: (0, 0)), row(d), vec, row(d)],
        out_specs=[pl.BlockSpec((1, 1), lambda i: (0, 0)), row(d), row(d), vec],
        out_shape=[jax.ShapeDtypeStruct((1, 1), F32), jax.ShapeDtypeStruct((l, d), F32),
                   jax.ShapeDtypeStruct((l, d), BF16), jax.ShapeDtypeStruct((1, d), F32)],
        name="mm_down_loss", compiler_params=_params(("arbitrary",)),
    )(a, b, res, g, target)


def _mm_rms_bwd(a, b, a_spec, b_spec, matmul, x, g, res, name):
    l, d = x.shape

    def body(a_ref, b_ref, x_ref, g_ref, res_ref, dx_ref, dxb_ref, dg_ref):
        dx, dgr = _rms_bwd_vals(x_ref[...], g_ref[...], matmul(a_ref, b_ref))
        dx = dx + res_ref[...]
        dx_ref[...] = dx
        dxb_ref[...] = dx.astype(BF16)

        @pl.when(pl.program_id(0) == 0)
        def _():
            dg_ref[...] = jnp.zeros_like(dg_ref)

        dg_ref[...] += jnp.sum(dgr, axis=0, keepdims=True)

    row = pl.BlockSpec((TM_FUSED, d), lambda i: (i, 0))
    vec = pl.BlockSpec((1, d), lambda i: (0, 0))
    return pl.pallas_call(
        body, grid=(l // TM_FUSED,), in_specs=[a_spec, b_spec, row, vec, row], out_specs=[row, row, vec],
        out_shape=[jax.ShapeDtypeStruct((l, d), F32), jax.ShapeDtypeStruct((l, d), BF16),
                   jax.ShapeDtypeStruct((1, d), F32)],
        name=name, compiler_params=_params(("arbitrary",)),
    )(a, b, x, g, res)


def _mm_nt_rms_bwd(a, b, x, g, res, name):
    return _mm_rms_bwd(a, b, pl.BlockSpec((TM_FUSED, a.shape[1]), lambda i: (i, 0)),
                       pl.BlockSpec(b.shape, lambda i: (0, 0)),
                       lambda a_ref, b_ref: _dg(a_ref[...], b_ref[...], NT), x, g, res, name)


def _mm_cols_rms_bwd(a2, b4, x, g, res, name):
    h, _, wide = a2.shape
    s, _, n = b4.shape
    per = s // h

    def matmul(a_ref, b_ref):
        acc = None
        for j in range(s):
            part = _dg(a_ref[j // per, :, (j % per) * n:(j % per + 1) * n], b_ref[j], NT)
            acc = part if acc is None else acc + part
        return acc

    return _mm_rms_bwd(a2, b4, pl.BlockSpec((h, TM_FUSED, wide), lambda i: (0, i, 0)),
                       pl.BlockSpec(b4.shape, lambda i: (0, 0, 0)), matmul, x, g, res, name)


def _mix_fwd(attn, ys, g_attn, g_ssm):
    l, w = attn.shape

    def body(a_ref, y_ref, ga_ref, gs_ref, o_ref):
        for src, gr, off in ((a_ref, ga_ref, 0), (y_ref, gs_ref, w)):
            xv = src[...]
            r = lax.rsqrt(jnp.mean(xv * xv, axis=-1, keepdims=True) + EPS)
            o_ref[:, off:off + w] = (xv * r * gr[...]).astype(BF16)

    row = pl.BlockSpec((TM_EW, w), lambda i: (i, 0))
    vec = pl.BlockSpec((1, w), lambda i: (0, 0))
    return pl.pallas_call(
        body, grid=(l // TM_EW,), in_specs=[row, row, vec, vec],
        out_specs=pl.BlockSpec((TM_EW, 2 * w), lambda i: (i, 0)),
        out_shape=jax.ShapeDtypeStruct((l, 2 * w), BF16), name="mix_fwd",
        compiler_params=_params(("parallel",)),
    )(attn, ys, g_attn, g_ssm)


def _mix_bwd(attn, ys, g_attn, g_ssm, dmixed):
    l, w = attn.shape

    def body(a_ref, y_ref, ga_ref, gs_ref, dm_ref, da_ref, dy_ref, dga_ref, dgs_ref):
        @pl.when(pl.program_id(0) == 0)
        def _():
            dga_ref[...] = jnp.zeros_like(dga_ref)
            dgs_ref[...] = jnp.zeros_like(dgs_ref)

        for src, gr, off, dst, dgr in ((a_ref, ga_ref, 0, da_ref, dga_ref), (y_ref, gs_ref, w, dy_ref, dgs_ref)):
            dx, dg_rows = _rms_bwd_vals(src[...], gr[...], dm_ref[:, off:off + w])
            dst[...] = dx
            dgr[...] += jnp.sum(dg_rows, axis=0, keepdims=True)

    row = pl.BlockSpec((TM_EW, w), lambda i: (i, 0))
    vec = pl.BlockSpec((1, w), lambda i: (0, 0))
    return pl.pallas_call(
        body, grid=(l // TM_EW,),
        in_specs=[row, row, vec, vec, pl.BlockSpec((TM_EW, 2 * w), lambda i: (i, 0))],
        out_specs=[row, row, vec, vec],
        out_shape=[jax.ShapeDtypeStruct((l, w), F32), jax.ShapeDtypeStruct((l, w), F32),
                   jax.ShapeDtypeStruct((1, w), F32), jax.ShapeDtypeStruct((1, w), F32)],
        name="mix_bwd", compiler_params=_params(("arbitrary",)),
    )(attn, ys, g_attn, g_ssm, dmixed)


def _rope_tables(l):
    half = ROPE_DIM // 2
    inv_freq = jnp.power(ROPE_THETA, -jnp.arange(half, dtype=F32) / half)
    ang = jnp.arange(l, dtype=F32)[:, None] * inv_freq[None, :]
    cos, sin = jnp.cos(ang), jnp.sin(ang)
    ones = jnp.ones((l, HEAD_DIM - ROPE_DIM), F32)
    zeros = jnp.zeros((l, HEAD_DIM - ROPE_DIM), F32)
    zh = jnp.zeros((l, half), F32)
    c = jnp.concatenate([cos, cos, ones], axis=1)
    s_lo = jnp.concatenate([-sin, zh, zeros], axis=1)
    s_hi = jnp.concatenate([zh, sin, zeros], axis=1)
    return tuple(jnp.tile(t, (1, LANES // HEAD_DIM)) for t in (c, s_lo, s_hi))


def _rope_fwd(proj, tabs):
    l = proj.shape[0]
    nq = ATTN_WIDTH // LANES

    def body(p_ref, c_ref, lo_ref, hi_ref, o_ref):
        c, lo, hi = c_ref[...], lo_ref[...], hi_ref[...]
        for blk in range(nq + 1):
            t = p_ref[:, blk * LANES:(blk + 1) * LANES]
            rot = t * c + pltpu.roll(t, LANES - 8, 1) * lo + pltpu.roll(t, 8, 1) * hi
            o_ref[:, blk * LANES:(blk + 1) * LANES] = rot.astype(BF16)
        o_ref[:, (nq + 1) * LANES:] = p_ref[:, (nq + 1) * LANES:].astype(BF16)

    tab = pl.BlockSpec((TM_EW, LANES), lambda i: (i, 0))
    return pl.pallas_call(
        body, grid=(l // TM_EW,),
        in_specs=[pl.BlockSpec((TM_EW, QKV_WIDTH), lambda i: (i, 0)), tab, tab, tab],
        out_specs=pl.BlockSpec((TM_EW, QKV_WIDTH), lambda i: (i, 0)),
        out_shape=jax.ShapeDtypeStruct((l, QKV_WIDTH), BF16), name="rope_fwd",
        compiler_params=_params(("parallel",)),
    )(proj, *tabs)


def _rope_bwd(dq, dkv, du_ssm, dpre, d_skip, tabs):
    l = dq.shape[0]
    nq = ATTN_WIDTH // LANES

    def body(dq_ref, dkv_ref, du_ref, dpre_ref, ds_ref, c_ref, lo_ref, hi_ref, o_ref):
        c, lo, hi = c_ref[...], lo_ref[...], hi_ref[...]
        for blk in range(nq + 1):
            t = dq_ref[:, blk * LANES:(blk + 1) * LANES] if blk < nq else dkv_ref[:, :KV_WIDTH]
            g = t * c + pltpu.roll(t * lo, 8, 1) + pltpu.roll(t * hi, LANES - 8, 1)
            o_ref[:, blk * LANES:(blk + 1) * LANES] = g.astype(BF16)
        o_ref[:, (nq + 1) * LANES:QKV_WIDTH] = dkv_ref[:, KV_WIDTH:].astype(BF16)
        o_ref[:, QKV_WIDTH:] = (du_ref[...] + dpre_ref[...] * ds_ref[...]).astype(BF16)

    tab = pl.BlockSpec((TM_EW, LANES), lambda i: (i, 0))
    wide = pl.BlockSpec((TM_EW, SSM_WIDTH), lambda i: (i, 0))
    return pl.pallas_call(
        body, grid=(l // TM_EW,),
        in_specs=[wide, pl.BlockSpec((TM_EW, 2 * KV_WIDTH), lambda i: (i, 0)), wide, wide,
                  pl.BlockSpec((1, SSM_WIDTH), lambda i: (0, 0)), tab, tab, tab],
        out_specs=pl.BlockSpec((TM_EW, IN_WIDTH), lambda i: (i, 0)),
        out_shape=jax.ShapeDtypeStruct((l, IN_WIDTH), BF16), name="rope_bwd",
        compiler_params=_params(("parallel",)),
    )(dq, dkv, du_ssm, dpre, d_skip, *tabs)


_Q_COLS = ATTN_WIDTH // LANES
_SCALE = HEAD_DIM ** -0.5
_NEG = -1e30


def _window_specs(nb, width, col):
    return [
        pl.BlockSpec((BLOCK, width), lambda n: (jnp.maximum(n - 1, 0), col)),
        pl.BlockSpec((BLOCK, width), lambda n: (n, col)),
        pl.BlockSpec((BLOCK, width), lambda n: (jnp.minimum(n + 1, nb - 1), col)),
    ]


def _stacked_sink(sink_ref, heads):
    rid = lax.broadcasted_iota(jnp.int32, (len(heads) * BLOCK, 1), 0)
    sk = jnp.full(rid.shape, sink_ref[0, heads[-1]], F32)
    for g in range(len(heads) - 2, -1, -1):
        sk = jnp.where(rid < (g + 1) * BLOCK, sink_ref[0, heads[g]], sk)
    return sk


def _attn_fwd(qkv, sink):
    l = qkv.shape[0]
    nb = l // BLOCK
    grp = N_Q_HEADS // N_KV_HEADS

    def body(sink_ref, q_ref, k0, k1, k2, v0, v1, v2, o_ref, lse_ref):
        n = pl.program_id(0)
        q = q_ref[...]
        kw = jnp.concatenate([k0[...], k1[...], k2[...]], axis=0)
        vw = jnp.concatenate([v0[...], v1[...], v2[...]], axis=0)
        row = lax.broadcasted_iota(jnp.int32, (grp * BLOCK, 3 * BLOCK), 0)
        col = lax.broadcasted_iota(jnp.int32, (grp * BLOCK, 3 * BLOCK), 1)
        valid = jnp.abs(col - BLOCK - (row & (BLOCK - 1))) <= WINDOW
        valid &= jnp.logical_not((n == 0) & (col < BLOCK))
        valid &= jnp.logical_not((n == nb - 1) & (col >= 2 * BLOCK))
        for hk in range(N_KV_HEADS):
            heads = range(hk * grp, (hk + 1) * grp)
            qs = jnp.concatenate([q[:, h * HEAD_DIM:(h + 1) * HEAD_DIM] for h in heads], axis=0)
            kh = kw[:, hk * HEAD_DIM:(hk + 1) * HEAD_DIM]
            vh = vw[:, hk * HEAD_DIM:(hk + 1) * HEAD_DIM]
            s = jnp.where(valid, _dg(qs, kh, NT) * _SCALE, _NEG)
            sk = _stacked_sink(sink_ref, heads)
            m = jnp.maximum(jnp.max(s, axis=1, keepdims=True), sk)
            p = jnp.exp(s - m)
            denom = jnp.sum(p, axis=1, keepdims=True) + jnp.exp(sk - m)
            o = _dg((p / denom).astype(BF16), vh, NN)
            lse = m + jnp.log(denom)
            for g, h in enumerate(heads):
                o_ref[:, h * HEAD_DIM:(h + 1) * HEAD_DIM] = o[g * BLOCK:(g + 1) * BLOCK]
                lse_ref[:, h:h + 1] = lse[g * BLOCK:(g + 1) * BLOCK]

    return pl.pallas_call(
        body, grid=(nb,),
        in_specs=[pl.BlockSpec(memory_space=pltpu.SMEM),
                  pl.BlockSpec((BLOCK, ATTN_WIDTH), lambda n: (n, 0))]
        + _window_specs(nb, KV_WIDTH, _Q_COLS) + _window_specs(nb, KV_WIDTH, _Q_COLS + 1),
        out_specs=[pl.BlockSpec((BLOCK, ATTN_WIDTH), lambda n: (n, 0)),
                   pl.BlockSpec((BLOCK, N_Q_HEADS), lambda n: (n, 0))],
        out_shape=[jax.ShapeDtypeStruct((l, ATTN_WIDTH), F32), jax.ShapeDtypeStruct((l, N_Q_HEADS), F32)],
        name="attn_fwd", compiler_params=_params(("parallel",)),
    )(sink, qkv, qkv, qkv, qkv, qkv, qkv, qkv)


def _attn_bwd(qkv, attn, dattn, lse, sink):
    l = qkv.shape[0]
    nb = l // BLOCK
    grp = N_Q_HEADS // N_KV_HEADS
    win = 3 * BLOCK

    def body(sink_ref, q_ref, k0, k1, k2, v0, v1, v2, o_ref, d_ref, l_ref, dq_ref, dkv_ref, dsink_ref, ring_ref):
        n = pl.program_id(0)

        @pl.when(n == 0)
        def _():
            dsink_ref[...] = jnp.zeros_like(dsink_ref)
            ring_ref[...] = jnp.zeros_like(ring_ref)

        @pl.when(n < nb)
        def _():
            first, last = n == 0, n == nb - 1
            cat = lambda a, b, c: jnp.concatenate([a[...], b[...], c[...]], axis=0)
            q, kw, vw = q_ref[...], cat(k0, k1, k2), cat(v0, v1, v2)
            dov = d_ref[...]
            prod = o_ref[...] * dov
            dob = dov.astype(BF16)
            lse = l_ref[...]
            row = lax.broadcasted_iota(jnp.int32, (grp * BLOCK, win), 0)
            col = lax.broadcasted_iota(jnp.int32, (grp * BLOCK, win), 1)
            valid = jnp.abs(col - BLOCK - (row & (BLOCK - 1))) <= WINDOW
            valid &= jnp.logical_not(first & (col < BLOCK))
            valid &= jnp.logical_not(last & (col >= 2 * BLOCK))

            dsink_parts, dks, dvs = [], [], []
            for hk in range(N_KV_HEADS):
                heads = range(hk * grp, (hk + 1) * grp)
                ksl = slice(hk * HEAD_DIM, (hk + 1) * HEAD_DIM)
                hsl = [slice(h * HEAD_DIM, (h + 1) * HEAD_DIM) for h in heads]
                stack = lambda parts: jnp.concatenate(parts, axis=0)
                qs = stack([q[:, s_] for s_ in hsl])
                dos = stack([dob[:, s_] for s_ in hsl])
                deltas = stack([jnp.sum(prod[:, s_], axis=1, keepdims=True) for s_ in hsl])
                lses = stack([lse[:, h:h + 1] for h in heads])
                kh, vh = kw[:, ksl], vw[:, ksl]
                s = jnp.where(valid, _dg(qs, kh, NT) * _SCALE, _NEG)
                p = jnp.exp(s - lses)
                dp = _dg(dos, vh, NT)
                ds = (p * (dp - deltas) * _SCALE).astype(BF16)
                dq = _dg(ds, kh, NN)
                sink_rows = jnp.exp(_stacked_sink(sink_ref, heads) - lses) * deltas
                for g in range(grp):
                    dq_ref[:, hsl[g]] = dq[g * BLOCK:(g + 1) * BLOCK]
                    dsink_parts.append(jnp.sum(sink_rows[g * BLOCK:(g + 1) * BLOCK], axis=0, keepdims=True))
                dks.append(_dg(ds, qs, TN))
                dvs.append(_dg(p.astype(BF16), dos, TN))
            dsink_ref[...] -= jnp.concatenate(dsink_parts, axis=1)
            part = jnp.concatenate(dks + dvs, axis=1)
            ring_ref[(n + 2) % 3] += part[0:BLOCK]
            ring_ref[n % 3] += part[BLOCK:2 * BLOCK]
            ring_ref[(n + 1) % 3] = part[2 * BLOCK:]

        @pl.when(n >= 1)
        def _():
            dkv_ref[...] = ring_ref[(n + 2) % 3]

    centre = lambda n: jnp.minimum(n, nb - 1)
    window = lambda width, col: [
        pl.BlockSpec((BLOCK, width), lambda n: (jnp.maximum(centre(n) - 1, 0), col)),
        pl.BlockSpec((BLOCK, width), lambda n: (centre(n), col)),
        pl.BlockSpec((BLOCK, width), lambda n: (jnp.minimum(centre(n) + 1, nb - 1), col))]
    own = lambda width: pl.BlockSpec((BLOCK, width), lambda n: (centre(n), 0))
    return pl.pallas_call(
        body, grid=(nb + 1,),
        in_specs=[pl.BlockSpec(memory_space=pltpu.SMEM), own(ATTN_WIDTH)]
        + window(KV_WIDTH, _Q_COLS) + window(KV_WIDTH, _Q_COLS + 1)
        + [own(ATTN_WIDTH), own(ATTN_WIDTH), own(N_Q_HEADS)],
        out_specs=[own(ATTN_WIDTH), pl.BlockSpec((BLOCK, 2 * KV_WIDTH), lambda n: (jnp.maximum(n - 1, 0), 0)),
                   pl.BlockSpec((1, N_Q_HEADS), lambda n: (0, 0))],
        out_shape=[jax.ShapeDtypeStruct((l, ATTN_WIDTH), F32), jax.ShapeDtypeStruct((l, 2 * KV_WIDTH), F32),
                   jax.ShapeDtypeStruct((1, N_Q_HEADS), F32)],
        scratch_shapes=[pltpu.VMEM((3, BLOCK, 2 * KV_WIDTH), F32)],
        name="attn_bwd", compiler_params=_params(("arbitrary",)),
    )(sink, qkv, qkv, qkv, qkv, qkv, qkv, qkv, attn, dattn, lse)


def _ssm_disc(a_re, a_im, log_step, b_re, b_im):
    step = jnp.exp(log_step)[..., None]
    mag = jnp.exp(a_re * step)
    lb_re, lb_im = mag * jnp.cos(a_im * step), mag * jnp.sin(a_im * step)
    nr, ni = lb_re - 1.0, lb_im
    den = a_re * a_re + a_im * a_im
    f_re = ((nr * a_re + ni * a_im) / den)[..., None]
    f_im = ((ni * a_re - nr * a_im) / den)[..., None]
    return lb_re, lb_im, f_re * b_re - f_im * b_im, f_re * b_im + f_im * b_re


def _ssm_pack(lb_re, lb_im, bb_re, bb_im, c_re, c_im):
    eye = jnp.eye(SSM_CH // SSM_GROUP, dtype=F32)
    ng = SSM_CH // SSM_GROUP

    def diag_b(bb):
        t = bb.reshape(2, SSM_CB, ng, SSM_STATE, SSM_GROUP)
        return jnp.einsum('dkgpc,gh->dkgchp', t, eye).reshape(2, SSM_CB, SSM_CH, SSM_ST)

    def diag_c(cc):
        t = cc.reshape(2, SSM_CB, ng, SSM_GROUP, SSM_STATE)
        return jnp.einsum('dkgcp,gh->dkhpgc', t, eye).reshape(2, SSM_CB, SSM_ST, SSM_CH)

    bcat = jnp.concatenate([diag_b(bb_re), diag_b(bb_im)], axis=-1)
    ccat = jnp.concatenate([diag_c(c_re), -diag_c(c_im)], axis=-2)
    lam_re = lb_re.reshape(2, SSM_CB, 1, SSM_ST)
    lam_im = lb_im.reshape(2, SSM_CB, 1, SSM_ST)
    return bcat, ccat, lam_re, lam_im


def _ssm_unpack(dbcat, dccat, dlam_re, dlam_im):
    ng = SSM_CH // SSM_GROUP
    eye = jnp.eye(ng, dtype=F32)

    def undiag_b(t):
        t = t.reshape(2, SSM_CB, ng, SSM_GROUP, ng, SSM_STATE)
        return jnp.einsum('dkgchp,gh->dkgpc', t, eye).reshape(2, N_SSM_GROUPS, SSM_STATE, SSM_GROUP)

    def undiag_c(t):
        t = t.reshape(2, SSM_CB, ng, SSM_STATE, ng, SSM_GROUP)
        return jnp.einsum('dkhpgc,gh->dkgcp', t, eye).reshape(2, N_SSM_GROUPS, SSM_GROUP, SSM_STATE)

    dbb_re, dbb_im = undiag_b(dbcat[..., :SSM_ST]), undiag_b(dbcat[..., SSM_ST:])
    dc_re, dc_im = undiag_c(dccat[:, :, :SSM_ST]), -undiag_c(dccat[:, :, SSM_ST:])
    shape = (2, N_SSM_GROUPS, SSM_STATE)
    return dlam_re.reshape(shape), dlam_im.reshape(shape), dbb_re, dbb_im, dc_re, dc_im


def _to_segments(t):
    l, w = t.shape
    return t.reshape(N_SEG, l // N_SEG, w).transpose(1, 0, 2).reshape(l, w)


def _from_segments(t):
    l, w = t.shape
    return t.reshape(l // N_SEG, N_SEG, w).transpose(1, 0, 2).reshape(l, w)


SSM_RC = 256
SSM_JC = SSM_RC // N_SEG
_RE, _IM = pl.ds(0, SSM_ST), pl.ds(SSM_ST, SSM_ST)


def _cfma(ar, ai, xr, xi, br, bi):
    return ar * xr - ai * xi + br, ar * xi + ai * xr + bi


def _chunk_rows(ci, rev, nc):
    start = jnp.where(rev, (nc - 1 - ci) * SSM_RC, ci * SSM_RC)
    return pl.ds(pl.multiple_of(start, SSM_RC), SSM_RC)


def _scan_chunk(src, dst, ar, ai, rev, nj, ci, carry, prev_ref=None):
    def rows_of(staged, j, k):
        at = jnp.where(rev, SSM_JC - 1 - k, k) if staged else j
        return pl.ds(pl.multiple_of(at * N_SEG, N_SEG), N_SEG)

    for k in range(SSM_JC):
        jj = ci * SSM_JC + k
        j = jnp.where(rev, nj - 1 - jj, jj)
        rows = rows_of(src[1], j, k)
        nr, ni = _cfma(ar, ai, carry[0], carry[1], src[0][rows, _RE], src[0][rows, _IM])
        if dst is not None:
            rows = rows_of(dst[1], j, k)
            dst[0][rows, _RE] = nr
            dst[0][rows, _IM] = ni
        if prev_ref is None:
            carry = (nr, ni)
            continue
        jp = jnp.where(rev, j - 1, j + 1)
        if k == SSM_JC - 1:
            inside = jnp.where((jp >= 0) & (jp < nj), 1.0, 0.0)
            jp = jnp.clip(jp, 0, nj - 1)
        prow = pl.ds(pl.multiple_of(jp * N_SEG, N_SEG), N_SEG)
        xr, xi = prev_ref[prow, _RE], prev_ref[prow, _IM]
        sr, si = nr * xr + ni * xi, ni * xr - nr * xi
        if k == SSM_JC - 1:
            sr, si = inside * sr, inside * si
        carry = (nr, ni, carry[2] + sr, carry[3] + si)
    return carry


def _segment_inits(ar, ai, end_r, end_i, rev, nj):
    pr, pi = ar, ai
    for _ in range(int(math.log2(nj))):
        pr, pi = pr * pr - pi * pi, 2.0 * pr * pi
    seg = lax.broadcasted_iota(jnp.int32, end_r.shape, 0)
    zero = jnp.zeros_like(end_r)

    def chain(shift, keep):
        ir, ii = zero, zero
        for _ in range(N_SEG - 1):
            tr, ti = _cfma(pr, pi, ir, ii, end_r, end_i)
            ir = jnp.where(keep, pltpu.roll(tr, shift, 0), 0.0)
            ii = jnp.where(keep, pltpu.roll(ti, shift, 0), 0.0)
        return ir, ii

    up_r, up_i = chain(1, seg >= 1)
    dn_r, dn_i = chain(N_SEG - 1, seg <= N_SEG - 2)
    return jnp.where(rev, dn_r, up_r), jnp.where(rev, dn_i, up_i)


def _ssm_specs(l):
    act = pl.BlockSpec((l, SSM_CH), lambda k, d: (0, k))
    bmat = pl.BlockSpec((None, None, SSM_CH, 2 * SSM_ST), lambda k, d: (d, k, 0, 0))
    cmat = pl.BlockSpec((None, None, 2 * SSM_ST, SSM_CH), lambda k, d: (d, k, 0, 0))
    lam = pl.BlockSpec((None, None, 1, SSM_ST), lambda k, d: (d, k, 0, 0))
    return act, bmat, cmat, lam


def _ssm_fwd(u_seg, bcat, ccat, lam_re, lam_im):
    l = u_seg.shape[0]
    nj = l // N_SEG
    nc = l // SSM_RC

    def body(u_ref, b_ref, c_ref, lr_ref, li_ref, y_ref, keep_ref, xs_ref, stage0, stage1, keep_sem):
        k, d = pl.program_id(0), pl.program_id(1)
        rev = d == 1
        shape = (N_SEG, SSM_ST)
        ar, ai = jnp.broadcast_to(lr_ref[...], shape), jnp.broadcast_to(li_ref[...], shape)
        zero = jnp.zeros(shape, F32)

        def inputs(ci, stage):
            rows = _chunk_rows(ci, rev, nc)
            bu = _dg(u_ref[rows, :], b_ref[...], NN)
            stage[...] = bu
            xs_ref[rows, :] = bu

        def first(stage, ci, carry):
            return _scan_chunk((stage, True), None, ar, ai, rev, nj, ci, carry)

        def first_pass(t, carry):
            inputs(2 * t + 1, stage1)
            carry = first(stage0, 2 * t, carry)
            inputs(2 * t + 2, stage0)
            return first(stage1, 2 * t + 1, carry)

        inputs(0, stage0)
        carry = lax.fori_loop(0, nc // 2 - 1, first_pass, (zero, zero))
        inputs(nc - 1, stage1)
        carry = first(stage0, nc - 2, carry)
        end_r, end_i = first(stage1, nc - 1, carry)
        init = _segment_inits(ar, ai, end_r, end_i, rev, nj)

        @pl.when(d == 0)
        def _():
            y_ref[...] = jnp.zeros_like(y_ref)

        def outputs(ci):
            rows = _chunk_rows(ci, rev, nc)
            y_ref[rows, :] += _dg(xs_ref[rows, :].astype(BF16), c_ref[...], NN)
            pltpu.make_async_copy(xs_ref.at[rows], keep_ref.at[d, k, rows], keep_sem).start()

        def second(ci, carry):
            return _scan_chunk((xs_ref, False), (xs_ref, False), ar, ai, rev, nj, ci, carry)

        def second_pass(ci, carry):
            outputs(ci - 1)
            return second(ci, carry)

        lax.fori_loop(1, nc, second_pass, second(0, init))
        outputs(nc - 1)
        pltpu.make_async_copy(xs_ref, keep_ref.at[d, k], keep_sem).wait()

    act, bmat, cmat, lam = _ssm_specs(l)
    return pl.pallas_call(
        body, grid=(SSM_CB, 2), in_specs=[act, bmat, cmat, lam, lam], out_specs=[act, ANY],
        out_shape=[jax.ShapeDtypeStruct((l, SSM_WIDTH), F32),
                   jax.ShapeDtypeStruct((2, SSM_CB, l, 2 * SSM_ST), F32)],
        scratch_shapes=[pltpu.VMEM((l, 2 * SSM_ST), F32), pltpu.VMEM((SSM_RC, 2 * SSM_ST), F32),
                        pltpu.VMEM((SSM_RC, 2 * SSM_ST), F32), pltpu.SemaphoreType.DMA],
        name="ssm_fwd", compiler_params=_params(("parallel", "arbitrary"), vmem_mb=56),
    )(u_seg, bcat.astype(BF16), ccat.astype(BF16), lam_re, lam_im)


def _ssm_bwd(u_seg, dy_seg, states, bcat, ccat, lam_re, lam_im):
    l = u_seg.shape[0]
    nj = l // N_SEG
    nc = l // SSM_RC

    def body(u_ref, dy_ref, keep_ref, b_ref, c_ref, lr_ref, li_ref,
             du_ref, db_ref, dc_ref, dlr_ref, dli_ref, xs_ref, gs_ref, stage0, stage1, keep_sem):
        k, d = pl.program_id(0), pl.program_id(1)
        rev = d == 1
        back = jnp.logical_not(rev)
        shape = (N_SEG, SSM_ST)
        ar, ai = jnp.broadcast_to(lr_ref[...], shape), -jnp.broadcast_to(li_ref[...], shape)
        zero = jnp.zeros(shape, F32)
        fetch = pltpu.make_async_copy(keep_ref.at[d, k], xs_ref, keep_sem)
        fetch.start()

        def inputs(ci, stage):
            rows = _chunk_rows(ci, back, nc)
            dx = _dg(dy_ref[rows, :], c_ref[...], NT)
            stage[...] = dx
            gs_ref[rows, :] = dx

        def first(stage, ci, carry):
            return _scan_chunk((stage, True), None, ar, ai, back, nj, ci, carry)

        def first_pass(t, carry):
            inputs(2 * t + 1, stage1)
            carry = first(stage0, 2 * t, carry)
            inputs(2 * t + 2, stage0)
            return first(stage1, 2 * t + 1, carry)

        inputs(0, stage0)
        carry = lax.fori_loop(0, nc // 2 - 1, first_pass, (zero, zero))
        inputs(nc - 1, stage1)
        carry = first(stage0, nc - 2, carry)
        end_r, end_i = first(stage1, nc - 1, carry)
        init = _segment_inits(ar, ai, end_r, end_i, back, nj)
        fetch.wait()
        db_ref[...] = jnp.zeros_like(db_ref)
        dc_ref[...] = jnp.zeros_like(dc_ref)

        @pl.when(d == 0)
        def _():
            du_ref[...] = jnp.zeros_like(du_ref)

        def outputs(ci, stage):
            rows = _chunk_rows(ci, back, nc)
            g = stage[...].astype(BF16)
            dc_ref[...] += _dg(xs_ref[rows, :].astype(BF16), dy_ref[rows, :], TN)
            db_ref[...] += _dg(u_ref[rows, :], g, TN)
            du_ref[rows, :] += _dg(g, b_ref[...], NT)

        def second(ci, stage, carry):
            return _scan_chunk((gs_ref, False), (stage, True), ar, ai, back, nj, ci, carry, prev_ref=xs_ref)

        def second_pass(t, carry):
            outputs(2 * t, stage0)
            carry = second(2 * t + 1, stage1, carry)
            outputs(2 * t + 1, stage1)
            return second(2 * t + 2, stage0, carry)

        carry = lax.fori_loop(0, nc // 2 - 1, second_pass, second(0, stage0, init + (zero, zero)))
        outputs(nc - 2, stage0)
        gr, gi, acc_r, acc_i = second(nc - 1, stage1, carry)
        outputs(nc - 1, stage1)

        seg = lax.broadcasted_iota(jnp.int32, shape, 0)
        jb = jnp.where(rev, nj - 1, 0)
        erow = pl.ds(pl.multiple_of((nj - 1 - jb) * N_SEG, N_SEG), N_SEG)

        def before(t):
            up = jnp.where(seg >= 1, pltpu.roll(t, 1, 0), 0.0)
            down = jnp.where(seg <= N_SEG - 2, pltpu.roll(t, N_SEG - 1, 0), 0.0)
            return jnp.where(rev, down, up)

        init_r, init_i = before(xs_ref[erow, _RE]), before(xs_ref[erow, _IM])
        acc_r = acc_r + gr * init_r + gi * init_i
        acc_i = acc_i + gi * init_r - gr * init_i
        dlr_ref[...] = jnp.sum(acc_r, axis=0, keepdims=True)
        dli_ref[...] = jnp.sum(acc_i, axis=0, keepdims=True)

    act, bmat, cmat, lam = _ssm_specs(l)
    return pl.pallas_call(
        body, grid=(SSM_CB, 2), in_specs=[act, act, ANY, bmat, cmat, lam, lam],
        out_specs=[act, bmat, cmat, lam, lam],
        out_shape=[jax.ShapeDtypeStruct((l, SSM_WIDTH), F32),
                   jax.ShapeDtypeStruct(bcat.shape, F32), jax.ShapeDtypeStruct(ccat.shape, F32),
                   jax.ShapeDtypeStruct(lam_re.shape, F32), jax.ShapeDtypeStruct(lam_im.shape, F32)],
        scratch_shapes=[pltpu.VMEM((l, 2 * SSM_ST), F32), pltpu.VMEM((l, 2 * SSM_ST), F32),
                        pltpu.VMEM((SSM_RC, 2 * SSM_ST), F32), pltpu.VMEM((SSM_RC, 2 * SSM_ST), F32),
                        pltpu.SemaphoreType.DMA],
        name="ssm_bwd", compiler_params=_params(("parallel", "arbitrary"), vmem_mb=58),
    )(u_seg, dy_seg, states, bcat.astype(BF16), ccat.astype(BF16), lam_re, lam_im)


def _glu_fwd(y_ssm, u, d_skip, w_glu):
    l, w = u.shape

    def body(y_ref, u_ref, d_ref, w_ref, pre_ref, s_ref, ys_ref):
        pre = y_ref[...] + d_ref[...] * u_ref[...]
        z = _gelu(pre)
        s = _dg(z.astype(BF16), w_ref[...], NN)
        pre_ref[...] = pre
        s_ref[...] = s
        ys_ref[...] = z * _sigmoid(s)

    row = pl.BlockSpec((TM_EW, w), lambda i: (i, 0))
    out = jax.ShapeDtypeStruct((l, w), F32)
    return pl.pallas_call(
        body, grid=(l // TM_EW,),
        in_specs=[row, row, pl.BlockSpec((1, w), lambda i: (0, 0)), pl.BlockSpec((w, w), lambda i: (0, 0))],
        out_specs=[row, row, row], out_shape=[out, out, out], name="glu_fwd",
        compiler_params=_params(("parallel",)),
    )(y_ssm, u, d_skip, w_glu)


def _glu_bwd(pre, s, dys, u, d_skip, w_glu):
    l, w = u.shape

    def body(pre_ref, s_ref, dys_ref, u_ref, d_ref, w_ref, dpre_ref, z_ref, ds_ref, dd_ref):
        pre, dys = pre_ref[...], dys_ref[...]
        z = _gelu(pre)
        sig = _sigmoid(s_ref[...])
        ds = (dys * z * sig * (1.0 - sig)).astype(BF16)
        dz = dys * sig + _dg(ds, w_ref[...], NT)
        dpre = dz * _gelu_grad(pre)
        dpre_ref[...] = dpre
        z_ref[...] = z.astype(BF16)
        ds_ref[...] = ds

        @pl.when(pl.program_id(0) == 0)
        def _():
            dd_ref[...] = jnp.zeros_like(dd_ref)

        dd_ref[...] += jnp.sum(dpre * u_ref[...], axis=0, keepdims=True)

    row = pl.BlockSpec((TM_EW, w), lambda i: (i, 0))
    vec = pl.BlockSpec((1, w), lambda i: (0, 0))
    return pl.pallas_call(
        body, grid=(l // TM_EW,),
        in_specs=[row, row, row, row, vec, pl.BlockSpec((w, w), lambda i: (0, 0))],
        out_specs=[row, row, row, vec],
        out_shape=[jax.ShapeDtypeStruct((l, w), F32), jax.ShapeDtypeStruct((l, w), BF16),
                   jax.ShapeDtypeStruct((l, w), BF16), jax.ShapeDtypeStruct((1, w), F32)],
        name="glu_bwd", compiler_params=_params(("arbitrary",)),
    )(pre, s, dys, u, d_skip, w_glu)


TM_CV = 512
TC_CV = 256
TM_CF = 256
TC_CF = D_FF // 2
HALO = SUBLANES


def _conv_specs(l, col0, tm=TM_CV, tc=TC_CV):
    per = tm // HALO
    nh = l // HALO
    off = col0 // tc
    return [
        pl.BlockSpec((HALO, tc), lambda j, i: (jnp.maximum(i * per - 1, 0), j + off)),
        pl.BlockSpec((tm, tc), lambda j, i: (i, j + off)),
        pl.BlockSpec((HALO, tc), lambda j, i: (jnp.minimum((i + 1) * per, nh - 1), j + off)),
    ]


def _ext(prev_ref, mid_ref, next_ref, first, last):
    p = jnp.where(first, 0.0, prev_ref[...])
    n = jnp.where(last, 0.0, next_ref[...])
    return jnp.concatenate([p, mid_ref[...], n], axis=0)


def _shift_dn(t):
    return pltpu.roll(t, 1, 0)


def _shift_up(t):
    return pltpu.roll(t, t.shape[0] - 1, 0)


def _conv3(e, w_ref, b_ref):
    return w_ref[0:1, :] * _shift_dn(e) + w_ref[1:2, :] * e + w_ref[2:3, :] * _shift_up(e) + b_ref[...]


def _convffn_fwd(up_pre, conv_w, conv_b):
    l = up_pre.shape[0]
    tm, tc = TM_CF, TC_CF
    ni = l // tm
    wspec = lambda off: pl.BlockSpec((3, tc), lambda j, i: (0, j + off))
    bspec = lambda off: pl.BlockSpec((1, tc), lambda j, i: (0, j + off))
    voff = D_FF // tc

    def body(gp, gm, gn, vp, vm, vn, wg, bg, wv, bv, o_ref):
        i = pl.program_id(1)
        first, last = i == 0, i == ni - 1
        gate = _conv3(_ext(gp, gm, gn, first, last), wg, bg)[HALO:HALO + tm]
        val = _conv3(_ext(vp, vm, vn, first, last), wv, bv)[HALO:HALO + tm]
        o_ref[...] = (gate * _sigmoid(gate) * val).astype(BF16)

    return pl.pallas_call(
        body, grid=(D_FF // tc, ni),
        in_specs=_conv_specs(l, 0, tm, tc) + _conv_specs(l, D_FF, tm, tc)
        + [wspec(0), bspec(0), wspec(voff), bspec(voff)],
        out_specs=pl.BlockSpec((tm, tc), lambda j, i: (i, j)),
        out_shape=jax.ShapeDtypeStruct((l, D_FF), BF16), name="convffn_fwd",
        compiler_params=_params(("parallel", "parallel")),
    )(up_pre, up_pre, up_pre, up_pre, up_pre, up_pre, conv_w, conv_b, conv_w, conv_b)


HALO_B = 2 * SUBLANES


def _convffn_bwd(up_pre, dx2b, w_down, conv_w, conv_b):
    l = up_pre.shape[0]
    ni = l // TM_CV
    d = dx2b.shape[1]
    wspec = lambda off: pl.BlockSpec((3, TC_CV), lambda i, j: (0, j + off))
    bspec = lambda off: pl.BlockSpec((1, TC_CV), lambda i, j: (0, j + off))
    voff = D_FF // TC_CV
    swap = lambda spec: pl.BlockSpec(spec.block_shape, lambda i, j, f=spec.index_map: f(j, i))
    per, nh = TM_CV // HALO_B, l // HALO_B
    dx_specs = [pl.BlockSpec((HALO_B, d), lambda i, j: (jnp.maximum(i * per - 1, 0), 0)),
                pl.BlockSpec((TM_CV, d), lambda i, j: (i, 0)),
                pl.BlockSpec((HALO_B, d), lambda i, j: (jnp.minimum((i + 1) * per, nh - 1), 0))]

    def body(gp, gm, gn, vp, vm, vn, xp, xm, xn, wd, wg, bg, wv, bv, dup_ref, pg_ref, pv_ref):
        i = pl.program_id(0)
        first, last = i == 0, i == ni - 1
        ge, ve = _ext(gp, gm, gn, first, last), _ext(vp, vm, vn, first, last)
        zero = jnp.zeros((HALO_B, d), BF16)
        dx = jnp.concatenate([jnp.where(first, zero, xp[...]), xm[...], jnp.where(last, zero, xn[...])], axis=0)
        de = _dg(dx, wd[...], NT)[HALO_B - HALO:HALO_B + TM_CV + HALO]
        gate, val = _conv3(ge, wg, bg), _conv3(ve, wv, bv)
        sig = _sigmoid(gate)
        silu = gate * sig
        dgate = de * val * (sig + silu * (1.0 - sig))
        dval = de * silu
        mid = slice(HALO, HALO + TM_CV)
        rid = lax.broadcasted_iota(jnp.int32, (SUBLANES, TC_CV), 0)
        for half, (dup, e, w_ref, p_ref) in enumerate(((dgate, ge, wg, pg_ref), (dval, ve, wv, pv_ref))):
            dpre = w_ref[0:1, :] * _shift_up(dup) + w_ref[1:2, :] * dup + w_ref[2:3, :] * _shift_dn(dup)
            dup_ref[half] = dpre[mid].astype(BF16)
            dm_ = dup[mid]
            sums = [jnp.sum(dm_ * _shift_dn(e)[mid], axis=0, keepdims=True),
                    jnp.sum(dm_ * e[mid], axis=0, keepdims=True),
                    jnp.sum(dm_ * _shift_up(e)[mid], axis=0, keepdims=True),
                    jnp.sum(dm_, axis=0, keepdims=True)]
            acc = jnp.zeros((SUBLANES, TC_CV), F32)
            for k, sk in enumerate(sums):
                acc = jnp.where(rid == k, sk, acc)
            p_ref[...] = acc

    par = pl.BlockSpec((None, SUBLANES, TC_CV), lambda i, j: (i, 0, j))
    dup, pg, pv = pl.pallas_call(
        body, grid=(ni, D_FF // TC_CV),
        in_specs=[swap(s) for s in _conv_specs(l, 0) + _conv_specs(l, D_FF)] + dx_specs
        + [pl.BlockSpec((TC_CV, d), lambda i, j: (j, 0)), wspec(0), bspec(0), wspec(voff), bspec(voff)],
        out_specs=[pl.BlockSpec((2, TM_CV, TC_CV), lambda i, j: (0, i, j)), par, par],
        out_shape=[jax.ShapeDtypeStruct((2, l, D_FF), BF16),
                   jax.ShapeDtypeStruct((ni, SUBLANES, D_FF), F32), jax.ShapeDtypeStruct((ni, SUBLANES, D_FF), F32)],
        name="convffn_bwd", compiler_params=_params(("parallel", "parallel")),
    )(up_pre, up_pre, up_pre, up_pre, up_pre, up_pre, dx2b, dx2b, dx2b, w_down, conv_w, conv_b, conv_w, conv_b)
    return dup, jnp.concatenate([jnp.sum(pg, axis=0), jnp.sum(pv, axis=0)], axis=1)


def _local_step(x, target, wb, sp, late_weights=None, ffn_grads_ready=None, ffn_grads_next=None):
    l = x.shape[0]
    tabs = _rope_tables(l)
    disc = _ssm_disc(sp["a_re"], sp["a_im"], sp["log_step"], sp["b_re"], sp["b_im"])
    bcat, ccat, lam_re, lam_im = _ssm_pack(*disc, sp["c_re"], sp["c_im"])
    d_skip = sp["d_skip"].reshape(1, SSM_WIDTH)

    big = min(l, 1024)
    h, proj, u = _rms_mm_split(x, sp["norm_mix_g"], wb["w_in"], QKV_WIDTH, "mm_in")
    qkv = _rope_fwd(proj, tabs)
    attn, lse = _attn_fwd(qkv, sp["sink"])
    u_seg = _to_segments(u).astype(BF16)
    y_seg, states = _ssm_fwd(u_seg, bcat, ccat, lam_re, lam_im)
    y_ssm = _from_segments(y_seg)
    pre, s_glu, ys = _glu_fwd(y_ssm, u, d_skip, wb["w_glu"])
    mixed = _mix_fwd(attn, ys, sp["norm_attn_g"], sp["norm_ssm_g"])
    x1, h2 = _mm_res_rms(mixed, wb["w_out"], x, sp["norm_ffn_g"], "mm_out")
    if late_weights is not None:
        wb = dict(wb, **late_weights(h2))
    up_pre = _mm_nn_cols(h2, wb["w_up"], big, "mm_up")
    act = _convffn_fwd(up_pre, sp["conv_w"], sp["conv_b"])
    loss, dx2, dx2b, d_final_g = _mm_res_loss(act, wb["w_down"], x1, sp["norm_final_g"].reshape(1, D_MODEL), target)

    g = {"norm_final_g": d_final_g.reshape(D_MODEL)}
    g["w_down"] = _mm_tn(act, dx2b, D_FF // 2, 512, "mm_down_dw")
    dup_pre, conv_par = _convffn_bwd(up_pre, dx2b, wb["w_down"], sp["conv_w"], sp["conv_b"])
    g["conv_w"], g["conv_b"] = conv_par[0:3], conv_par[3:4]
    g["w_up"] = _mm_tn_cols(h2, dup_pre, wb["w_up"].shape[0], 512, "mm_up_dw")
    zero = ffn_grads_ready(g["w_up"], g["w_down"]) if ffn_grads_ready is not None else 0.0
    dx1, dx1b, g["norm_ffn_g"] = _mm_cols_rms_bwd(dup_pre, wb["w_up"], x1, sp["norm_ffn_g"] + zero, dx2, "mm_up_dx")
    dmixed = _mm_nt(dx1b, wb["w_out"], big, 1024, F32, "mm_out_dx")
    g["w_out"] = _mm_tn(mixed, dx1b, 1024, 1024, "mm_out_dw")
    zero = ffn_grads_next(dmixed) if ffn_grads_next is not None else 0.0
    dattn, dys, g["norm_attn_g"], g["norm_ssm_g"] = _mix_bwd(attn, ys, sp["norm_attn_g"] + zero, sp["norm_ssm_g"],
                                                            dmixed)
    dpre, zb, dsb, dd = _glu_bwd(pre, s_glu, dys, u, d_skip, wb["w_glu"])
    g["d_skip"] = dd.reshape(N_SSM_GROUPS, SSM_GROUP)
    g["w_glu"] = _mm_tn(zb, dsb, 512, 512, "mm_glu_dw")
    du_seg, dbcat, dccat, dlam_re, dlam_im = _ssm_bwd(u_seg, _to_segments(dpre).astype(BF16), states, bcat, ccat,
                                                      lam_re, lam_im)
    dlb_re, dlb_im, dbb_re, dbb_im, g["c_re"], g["c_im"] = _ssm_unpack(dbcat, dccat, dlam_re, dlam_im)
    _, disc_vjp = jax.vjp(_ssm_disc, sp["a_re"], sp["a_im"], sp["log_step"], sp["b_re"], sp["b_im"])
    g["a_re"], g["a_im"], g["log_step"], g["b_re"], g["b_im"] = disc_vjp((dlb_re, dlb_im, dbb_re, dbb_im))
    dq, dkv, g["sink"] = _attn_bwd(qkv, attn, dattn, lse, sp["sink"])
    dproj = _rope_bwd(dq, dkv, _from_segments(du_seg), dpre, d_skip, tabs)
    g["w_in"] = _mm_tn(h, dproj, 512, IN_WIDTH, "mm_in_dw")
    grad_x, _, g["norm_mix_g"] = _mm_nt_rms_bwd(dproj, wb["w_in"], x, sp["norm_mix_g"], dx1, "mm_in_dx")
    return loss, grad_x, g


MESH = pl.DeviceIdType.MESH
ANY = pl.BlockSpec(memory_space=pl.ANY)


def _place():
    x, y, c = lax.axis_index("x"), lax.axis_index("y"), lax.axis_index("c")
    chips = [(1 - x, y), (x, 1 - y), (1 - x, 1 - y)]
    return x, y, c, chips


def _chip_index(px, py):
    return 2 * px + py


CHUNK_BYTES = 256 * 1024
MAX_CHUNKS = 16


def _row_chunks(rows, row_bytes, align):
    n = max(1, min(MAX_CHUNKS, (rows * row_bytes) // CHUNK_BYTES))
    per = -(-rows // n)
    per = -(-per // align) * align
    return [(r0, min(per, rows - r0)) for r0 in range(0, rows, per)]


def _align_of(dtype):
    return SUBLANES * 4 // jnp.dtype(dtype).itemsize


def _remote(src, dst, send_sem, recv_sem, to):
    return pltpu.make_async_remote_copy(src_ref=src, dst_ref=dst, send_sem=send_sem, recv_sem=recv_sem,
                                        device_id=to, device_id_type=MESH)


CAST_ROWS = 64


def _gather_weights(shards, dtypes):
    nw = len(shards)

    def body(*refs):
        w_refs, o_refs = refs[:nw], refs[nw:2 * nw]
        send_sems, recv_sems, in_sems, out_sems = refs[2 * nw:2 * nw + 4]
        raw, cast = refs[2 * nw + 4:3 * nw + 4], refs[3 * nw + 4:]
        x, y, c, chips = _place()
        mine = _chip_index(x, y)
        sibling = (x, y, 1 - c)

        def rows_of(ref, chip, r0, nr):
            return ref.at[chip, pl.ds(r0, nr), :]

        def copy(wi, k, src, dst, to):
            return _remote(src, dst, send_sems.at[wi, k], recv_sems.at[wi, k], to)

        geo = []
        for wi in range(nw):
            rows, cols = w_refs[wi].shape
            row_bytes = cols * jnp.dtype(dtypes[wi]).itemsize
            geo.append((rows // 2, _row_chunks(rows // 2, row_bytes, _align_of(dtypes[wi]))))

        stage_in = [pltpu.make_async_copy(w_refs[wi], raw[wi], in_sems.at[wi]) for wi in range(nw)]
        for cp in stage_in:
            cp.start()
        staged = [raw[wi] if dtypes[wi] == w_refs[wi].dtype else cast[wi] for wi in range(nw)]
        stage_out = []
        for wi in range(nw):
            stage_in[wi].wait()
            if staged[wi] is not raw[wi]:
                def cast_rows(i, _, wi=wi):
                    rows = pl.ds(pl.multiple_of(i * CAST_ROWS, CAST_ROWS), CAST_ROWS)
                    cast[wi][rows, :] = raw[wi][rows, :].astype(dtypes[wi])
                    return 0

                lax.fori_loop(0, w_refs[wi].shape[0] // CAST_ROWS, cast_rows, 0)
            cp = pltpu.make_async_copy(staged[wi], o_refs[wi].at[mine], out_sems.at[wi])
            cp.start()
            stage_out.append(cp)

        for wi in range(nw):
            hr, half_chunks = geo[wi]
            for j, chip in enumerate(chips):
                for r0, nr in half_chunks:
                    copy(wi, j, staged[wi].at[pl.ds(c * hr + r0, nr), :],
                         rows_of(o_refs[wi], mine, c * hr + r0, nr), (*chip, c)).start()
        for wi in range(nw):
            hr, half_chunks = geo[wi]
            for j, chip in enumerate(chips):
                got = rows_of(o_refs[wi], _chip_index(*chip), c * hr, hr)
                copy(wi, j, got, got, (*chip, c)).wait_recv()
                for r0, nr in half_chunks:
                    piece = rows_of(o_refs[wi], _chip_index(*chip), c * hr + r0, nr)
                    copy(wi, 3 + j, piece, piece, sibling).start()
        for wi in range(nw):
            hr = geo[wi][0]
            for j, chip in enumerate(chips):
                got = rows_of(o_refs[wi], _chip_index(*chip), (1 - c) * hr, hr)
                copy(wi, 3 + j, got, got, sibling).wait_recv()
        for wi in range(nw):
            hr = geo[wi][0]
            sent = rows_of(o_refs[wi], mine, c * hr, hr)
            for k in range(6):
                copy(wi, k, sent, sent, sibling).wait_send()
            stage_out[wi].wait()

    return pl.pallas_call(
        body, in_specs=[ANY] * nw, out_specs=[ANY] * nw,
        out_shape=[jax.ShapeDtypeStruct((4, *s.shape), t) for s, t in zip(shards, dtypes)],
        scratch_shapes=[pltpu.SemaphoreType.DMA((nw, 6)), pltpu.SemaphoreType.DMA((nw, 6)),
                        pltpu.SemaphoreType.DMA((nw,)), pltpu.SemaphoreType.DMA((nw,))]
        + [pltpu.VMEM(s.shape, s.dtype) for s in shards] + [pltpu.VMEM(s.shape, t) for s, t in zip(shards, dtypes)],
        name="gather_weights", compiler_params=_params(vmem_mb=40),
    )(*shards)


HBM = pl.BlockSpec(memory_space=pltpu.HBM)
SEM = pl.BlockSpec(memory_space=pltpu.SEMAPHORE)
EFFECT = pltpu.SideEffectType.DATAFLOW_SIDE_EFFECTING


def _cast_place(w, place, dtype, after, name):
    rows, cols = w.shape
    tr = _row_tile(rows, cols, _align_of(dtype))

    def body(p_ref, w_ref, after_ref, o_ref):
        del p_ref, after_ref
        o_ref[...] = w_ref[...].astype(dtype)

    grid_spec = pltpu.PrefetchScalarGridSpec(
        num_scalar_prefetch=1, grid=(rows // tr,),
        in_specs=[pl.BlockSpec((tr, cols), lambda i, p: (i, 0)), ANY],
        out_specs=pl.BlockSpec((None, tr, cols), lambda i, p: (p[1], i, 0)))
    return pl.pallas_call(body, grid_spec=grid_spec, out_shape=jax.ShapeDtypeStruct((4, rows, cols), dtype),
                          name=name, compiler_params=_params(("parallel",)))(place, w, after)


def _split_start(name, arrays, n_pairs, issue):
    n = len(arrays)

    def body(*refs):
        issue(refs[:n], refs[n:n + n_pairs], refs[n + n_pairs:n + 2 * n_pairs])
        token = refs[2 * n + 2 * n_pairs]
        token[...] = jnp.zeros_like(token)

    dma = pltpu.SemaphoreType.DMA(())
    outs = pl.pallas_call(
        body, name=name,
        out_shape=[dma] * (2 * n_pairs) + [pltpu.HBM(t.shape, t.dtype) for t in arrays]
        + [jax.ShapeDtypeStruct((SUBLANES, LANES), F32)],
        in_specs=[HBM] * n, out_specs=[SEM] * (2 * n_pairs) + [HBM] * n + [pl.BlockSpec(memory_space=pltpu.VMEM)],
        input_output_aliases={a: 2 * n_pairs + a for a in range(n)},
        compiler_params=pltpu.CompilerParams(has_side_effects=EFFECT),
    )(*[pltpu.with_memory_space_constraint(t, pltpu.HBM) for t in arrays])
    return outs[:n_pairs], outs[n_pairs:2 * n_pairs], outs[2 * n_pairs:2 * n_pairs + n], outs[-1]


def _split_wait(name, send_sems, recv_sems, flying, sizes, after):
    n, n_pairs = len(flying), len(send_sems)

    def body(*refs):
        x, y, c, _ = _place()
        for k, ref in enumerate(sizes(refs[:n])):
            cp = _remote(ref, ref, refs[n + k], refs[n + n_pairs + k], (x, y, 1 - c))
            cp.wait_send()
            cp.wait_recv()

    return pl.pallas_call(
        body, name=name, out_shape=[pltpu.HBM(t.shape, t.dtype) for t in flying],
        in_specs=[HBM] * n + [SEM] * (2 * n_pairs) + [ANY], out_specs=[HBM] * n,
        input_output_aliases={a: a for a in range(n)},
        compiler_params=pltpu.CompilerParams(has_side_effects=EFFECT),
    )(*flying, *send_sems, *recv_sems, after)


def _spread_start(lands):
    def issue(land_refs, send_sems, recv_sems):
        x, y, c, chips = _place()
        mine = _chip_index(x, y)
        for a, land in enumerate(land_refs):
            _, rows, cols = land.shape
            hr = rows // 2
            row_bytes = cols * jnp.dtype(land.dtype).itemsize
            for r0, nr in _row_chunks(hr, row_bytes, _align_of(land.dtype)):
                piece = land.at[mine, pl.ds(c * hr + r0, nr), :]
                for chip in chips:
                    for core in (0, 1):
                        _remote(piece, piece, send_sems[a], recv_sems[a], (*chip, core)).start()

    return _split_start("spread_start", lands, len(lands), issue)


def _spread_wait(send_sems, recv_sems, flying, after):
    return _split_wait("spread_wait", send_sems, recv_sems, flying,
                       lambda refs: [r.at[pl.ds(0, 3)] for r in refs], after)


def _pair_start(grads):
    n = len(grads)
    zones = [lax.empty((4, g.shape[1] // 2, g.shape[2]), F32) for g in grads]

    def issue(refs, send_sems, recv_sems):
        x, y, c, _ = _place()
        for a in range(n):
            g_ref, z_ref = refs[a], refs[n + a]
            _, rows, cols = g_ref.shape
            hr = rows // 2
            for k in range(4):
                for r0, nr in _row_chunks(hr, cols * 4, SUBLANES):
                    _remote(g_ref.at[k, pl.ds((1 - c) * hr + r0, nr), :], z_ref.at[k, pl.ds(r0, nr), :],
                            send_sems[a], recv_sems[a], (x, y, 1 - c)).start()

    return _split_start("pair_start", list(grads) + zones, n, issue)


def _pair_wait(send_sems, recv_sems, flying, after):
    n = len(flying) // 2
    out = _split_wait("pair_wait", send_sems, recv_sems, flying, lambda refs: list(refs[n:]), after)
    return out[:n], out[n:]


def _chip_start(sums):
    n = len(sums)
    zones = [lax.empty((3, *s.shape[1:]), s.dtype) for s in sums]

    def issue(refs, send_sems, recv_sems):
        x, y, c, chips = _place()
        for a in range(n):
            s_ref, z_ref = refs[a], refs[n + a]
            _, rows, cols = s_ref.shape
            row_bytes = cols * jnp.dtype(s_ref.dtype).itemsize
            for r0, nr in _row_chunks(rows, row_bytes, _align_of(s_ref.dtype)):
                for j, chip in enumerate(chips):
                    _remote(s_ref.at[_chip_index(*chip), pl.ds(r0, nr), :], z_ref.at[j, pl.ds(r0, nr), :],
                            send_sems[a], recv_sems[a], (*chip, c)).start()

    return _split_start("chip_start", list(sums) + zones, n, issue)


def _chip_wait(send_sems, recv_sems, flying, after):
    n = len(flying) // 2
    return _split_wait("chip_wait", send_sems, recv_sems, flying, lambda refs: list(refs[n:]), after)[n:]


def _pair_exchange(grads):
    na = len(grads)

    def body(*refs):
        g_refs, o_refs = refs[:na], refs[na:2 * na]
        send_sems, recv_sems = refs[2 * na:]
        x, y, c, _ = _place()
        sibling = (x, y, 1 - c)
        for ai in range(na):
            _, rows, cols = g_refs[ai].shape
            hr = rows // 2
            for k in range(4):
                for r0, nr in _row_chunks(hr, cols * 4, SUBLANES):
                    _remote(g_refs[ai].at[k, pl.ds((1 - c) * hr + r0, nr), :], o_refs[ai].at[k, pl.ds(r0, nr), :],
                            send_sems.at[ai], recv_sems.at[ai], sibling).start()
        for ai in range(na):
            _remote(o_refs[ai], o_refs[ai], send_sems.at[ai], recv_sems.at[ai], sibling).wait()

    return pl.pallas_call(
        body, in_specs=[ANY] * na, out_specs=[ANY] * na,
        out_shape=[jax.ShapeDtypeStruct((4, g.shape[1] // 2, g.shape[2]), F32) for g in grads],
        scratch_shapes=[pltpu.SemaphoreType.DMA((na,)), pltpu.SemaphoreType.DMA((na,))],
        name="pair_exchange",
    )(*grads)


def _row_tile(rows, cols, align):
    best = align
    for cand in range(align, rows + 1, align):
        if rows % cand == 0 and cand * cols <= 256 * 1024:
            best = cand
    return best


def _pair_sum(g, got, place, transit, name):
    _, rows, cols = g.shape
    hr = rows // 2
    tr = _row_tile(hr, cols, _align_of(transit))
    nt = hr // tr

    def body(p_ref, g_ref, r_ref, s_ref, own_ref):
        total = g_ref[...] + r_ref[...]
        s_ref[...] = total.astype(transit)

        @pl.when(pl.program_id(1) == p_ref[1])
        def _():
            own_ref[...] = total

    grid_spec = pltpu.PrefetchScalarGridSpec(
        num_scalar_prefetch=1, grid=(nt, 4),
        in_specs=[pl.BlockSpec((None, tr, cols), lambda i, k, p: (k, p[0] * nt + i, 0)),
                  pl.BlockSpec((None, tr, cols), lambda i, k, p: (k, i, 0))],
        out_specs=[pl.BlockSpec((None, tr, cols), lambda i, k, p: (k, i, 0)),
                   pl.BlockSpec((tr, cols), lambda i, k, p: (i, 0))])
    return pl.pallas_call(
        body, grid_spec=grid_spec,
        out_shape=[jax.ShapeDtypeStruct((4, hr, cols), transit), jax.ShapeDtypeStruct((hr, cols), F32)],
        name=name, compiler_params=_params(("parallel", "arbitrary")),
    )(place, g, got)


def _chip_exchange(sums):
    na = len(sums)

    def body(*refs):
        s_refs, o_refs = refs[:na], refs[na:2 * na]
        send_sems, recv_sems = refs[2 * na:]
        x, y, c, chips = _place()
        for ai in range(na):
            _, rows, cols = s_refs[ai].shape
            row_bytes = cols * jnp.dtype(s_refs[ai].dtype).itemsize
            for r0, nr in _row_chunks(rows, row_bytes, _align_of(s_refs[ai].dtype)):
                for j, chip in enumerate(chips):
                    _remote(s_refs[ai].at[_chip_index(*chip), pl.ds(r0, nr), :], o_refs[ai].at[j, pl.ds(r0, nr), :],
                            send_sems.at[ai, j], recv_sems.at[ai, j], (*chip, c)).start()
        for ai in range(na):
            for j, chip in enumerate(chips):
                _remote(o_refs[ai].at[j], o_refs[ai].at[j], send_sems.at[ai, j], recv_sems.at[ai, j],
                        (*chip, c)).wait()

    return pl.pallas_call(
        body, in_specs=[ANY] * na, out_specs=[ANY] * na,
        out_shape=[jax.ShapeDtypeStruct((3, *s.shape[1:]), s.dtype) for s in sums],
        scratch_shapes=[pltpu.SemaphoreType.DMA((na, 3)), pltpu.SemaphoreType.DMA((na, 3))],
        name="chip_exchange",
    )(*sums)


def _chip_sum(own, landed, name):
    hr, cols = own.shape
    tr = _row_tile(hr, cols, _align_of(landed.dtype))

    def body(o_ref, l_ref, f_ref):
        acc = o_ref[...]
        for j in range(3):
            acc = acc + l_ref[j].astype(F32)
        f_ref[...] = acc

    return pl.pallas_call(
        body, grid=(hr // tr,),
        in_specs=[pl.BlockSpec((tr, cols), lambda i: (i, 0)), pl.BlockSpec((3, tr, cols), lambda i: (0, i, 0))],
        out_specs=pl.BlockSpec((tr, cols), lambda i: (i, 0)),
        out_shape=jax.ShapeDtypeStruct((hr, cols), F32), name=name,
        compiler_params=_params(("parallel",)),
    )(own, landed)


def _final_exchange(halves, small):
    nh = len(halves)

    def body(*refs):
        h_refs, s_ref = refs[:nh], refs[nh]
        o_refs, so_ref = refs[nh + 1:2 * nh + 1], refs[2 * nh + 1]
        send_sems, recv_sems, local_sem, ssend_sems, srecv_sems = refs[2 * nh + 2:]
        x, y, c, _ = _place()
        me = 4 * x + 2 * y + c
        sibling = (x, y, 1 - c)
        for hi in range(nh):
            hr, cols = h_refs[hi].shape
            for r0, nr in _row_chunks(hr, cols * 4, SUBLANES):
                _remote(h_refs[hi].at[pl.ds(r0, nr), :], o_refs[hi].at[pl.ds(r0, nr), :],
                        send_sems.at[hi], recv_sems.at[hi], sibling).start()
        small_cps = [pltpu.make_async_copy(s_ref, so_ref.at[me], local_sem)]
        for r in range(1, 8):
            fx, fy, fc = (r >> 2) & 1, (r >> 1) & 1, r & 1
            peer = (1 - x if fx else x, 1 - y if fy else y, 1 - c if fc else c)
            small_cps.append(_remote(s_ref, so_ref.at[me], ssend_sems.at[r - 1], srecv_sems.at[r - 1], peer))
        for cp in small_cps:
            cp.start()
        for hi in range(nh):
            _remote(h_refs[hi], o_refs[hi], send_sems.at[hi], recv_sems.at[hi], sibling).wait()
        for cp in small_cps:
            cp.wait()

    return pl.pallas_call(
        body, in_specs=[ANY] * (nh + 1), out_specs=[ANY] * (nh + 1),
        out_shape=[jax.ShapeDtypeStruct(h.shape, F32) for h in halves]
        + [jax.ShapeDtypeStruct((8, *small.shape), F32)],
        scratch_shapes=[pltpu.SemaphoreType.DMA((nh,)), pltpu.SemaphoreType.DMA((nh,)),
                        pltpu.SemaphoreType.DMA, pltpu.SemaphoreType.DMA((7,)), pltpu.SemaphoreType.DMA((7,))],
        name="final_exchange",
    )(*halves, small)


def _adamw(w, g, m, v, name):
    shape = w.shape
    n = w.size
    if w.ndim >= 2 and shape[-1] >= LANES:
        two_d = (n // shape[-1], shape[-1])
    elif n % LANES == 0:
        two_d = (n // LANES, LANES)
    else:
        two_d = (1, n)
    r, c = two_d
    tr = r
    for cand in (512, 256, 176, 128, 64):
        if r > cand and r % cand == 0 and cand * c <= 256 * 1024:
            tr = cand
            break
    c1 = 1.0 - ADAM_B1 ** ADAM_STEP
    c2 = 1.0 - ADAM_B2 ** ADAM_STEP

    def body(w_ref, g_ref, m_ref, v_ref, d_ref, nm_ref, nv_ref):
        gv = g_ref[...]
        nm = ADAM_B1 * m_ref[...] + (1.0 - ADAM_B1) * gv
        nv = ADAM_B2 * v_ref[...] + (1.0 - ADAM_B2) * (gv * gv)
        d_ref[...] = -ADAM_LR * ((nm / c1) / (jnp.sqrt(nv / c2) + ADAM_EPS) + ADAM_WD * w_ref[...])
        nm_ref[...] = nm
        nv_ref[...] = nv

    spec = pl.BlockSpec((tr, c), lambda i: (i, 0))
    out = jax.ShapeDtypeStruct((r, c), F32)
    d, nm, nv = pl.pallas_call(
        body, grid=(r // tr,), in_specs=[spec] * 4, out_specs=[spec] * 3, out_shape=[out] * 3, name=name,
        compiler_params=_params(("parallel",)),
    )(w.reshape(two_d), g.reshape(two_d), m.reshape(two_d), v.reshape(two_d))
    return d.reshape(shape), nm.reshape(shape), nv.reshape(shape)


def _adamw_many(ws, gs, ms, vs, name):
    n = len(ws)
    c1 = 1.0 - ADAM_B1 ** ADAM_STEP
    c2 = 1.0 - ADAM_B2 ** ADAM_STEP

    def body(*refs):
        w_refs, g_refs, m_refs, v_refs = (refs[k * n:(k + 1) * n] for k in range(4))
        d_refs, nm_refs, nv_refs = (refs[(4 + k) * n:(5 + k) * n] for k in range(3))
        for i in range(n):
            gv = g_refs[i][...]
            nm = ADAM_B1 * m_refs[i][...] + (1.0 - ADAM_B1) * gv
            nv = ADAM_B2 * v_refs[i][...] + (1.0 - ADAM_B2) * (gv * gv)
            d_refs[i][...] = -ADAM_LR * ((nm / c1) / (jnp.sqrt(nv / c2) + ADAM_EPS) + ADAM_WD * w_refs[i][...])
            nm_refs[i][...] = nm
            nv_refs[i][...] = nv

    vmem = pl.BlockSpec(memory_space=pltpu.VMEM)
    shapes = [jax.ShapeDtypeStruct(t.shape, F32) for t in ws]
    outs = pl.pallas_call(body, in_specs=[vmem] * (4 * n), out_specs=[vmem] * (3 * n), out_shape=shapes * 3,
                          name=name, compiler_params=_params(vmem_mb=56))(*ws, *gs, *ms, *vs)
    return outs[:n], outs[n:2 * n], outs[2 * n:]


BIG = ("w_in", "w_glu", "w_out", "w_up", "w_down")
WEIGHTS = ("norm_mix_g", "w_in", "a_re", "a_im", "log_step", "b_re", "b_im", "c_re", "c_im", "d_skip", "w_glu",
           "sink", "norm_attn_g", "norm_ssm_g", "w_out", "norm_ffn_g", "w_up", "conv_w", "conv_b", "w_down",
           "norm_final_g")
SMALL = ("norm_mix_g", "a_re", "a_im", "log_step", "b_re", "b_im", "c_re", "c_im", "d_skip", "sink",
         "norm_attn_g", "norm_ssm_g", "norm_ffn_g", "conv_w", "conv_b", "norm_final_g")
SMALL_ROWS = 40
N_DEV = 8


def _by_owner(name, g):
    if name == "w_up":
        return g
    if name == "w_in":
        return g.reshape(g.shape[0], 4, g.shape[1] // 4).transpose(1, 0, 2)
    return g.reshape(4, g.shape[0] // 4, g.shape[1])


def kernel(x, norm_mix_g, w_in, a_re, a_im, log_step, b_re, b_im, c_re, c_im, d_skip, w_glu, sink, norm_attn_g, norm_ssm_g, w_out, norm_ffn_g, w_up, conv_w, conv_b, w_down, norm_final_g, loss_target, m_norm_mix_g, m_w_in, m_a_re, m_a_im, m_log_step, m_b_re, m_b_im, m_c_re, m_c_im, m_d_skip, m_w_glu, m_sink, m_norm_attn_g, m_norm_ssm_g, m_w_out, m_norm_ffn_g, m_w_up, m_conv_w, m_conv_b, m_w_down, m_norm_final_g, v_norm_mix_g, v_w_in, v_a_re, v_a_im, v_log_step, v_b_re, v_b_im, v_c_re, v_c_im, v_d_skip, v_w_glu, v_sink, v_norm_attn_g, v_norm_ssm_g, v_w_out, v_norm_ffn_g, v_w_up, v_conv_w, v_conv_b, v_w_down, v_norm_final_g):
    given = dict(locals())
    w = {n: given[n] for n in WEIGHTS}
    m = {n: given["m_" + n] for n in WEIGHTS}
    v = {n: given["v_" + n] for n in WEIGHTS}
    xy = 2 * lax.axis_index("x") + lax.axis_index("y")

    core = lax.axis_index("c")
    place = jnp.stack([core, xy]).astype(jnp.int32)

    conv_rows = jnp.pad(w["conv_w"][0], ((0, 2 * SUBLANES - 3), (0, 0)))
    early = ("w_in", "w_glu", "w_out")
    *gathered, conv_all = _gather_weights([w[n][0] for n in early] + [conv_rows], [BF16] * len(early) + [F32])
    late = ("w_up", "w_down")
    send_sems, recv_sems, flying, token = _spread_start(
        [_cast_place(w[n][0], place, BF16, conv_all, "cast_" + n) for n in late])
    rows = lambda t: t.reshape(4 * t.shape[1], t.shape[2])
    wb = {"w_in": gathered[0].transpose(1, 0, 2).reshape(D_MODEL, IN_WIDTH), "w_glu": rows(gathered[1]),
          "w_out": rows(gathered[2])}

    def late_weights(after):
        w_up4, w_down4 = _spread_wait(send_sems, recv_sems, flying, after)
        return {"w_up": w_up4, "w_down": rows(w_down4)}

    sp = {n: w[n][0] for n in ("a_re", "a_im", "log_step", "b_re", "b_im", "c_re", "c_im", "d_skip",
                               "norm_mix_g", "norm_attn_g", "norm_ssm_g", "norm_ffn_g", "sink", "conv_b")}
    for n in ("norm_mix_g", "norm_attn_g", "norm_ssm_g", "norm_ffn_g", "sink", "conv_b"):
        sp[n] = sp[n].reshape(1, -1)
    sp["norm_mix_g"] = sp["norm_mix_g"] + token[:1, :1]
    sp["conv_w"] = conv_all[:, :3].transpose(1, 0, 2).reshape(3, 2 * D_FF)
    sp["norm_final_g"] = w["norm_final_g"]
    flight = {}

    def ffn_grads_ready(dw_up, dw_down):
        *flight["pair"], token = _pair_start([dw_up, _by_owner("w_down", dw_down)])
        return token[:1, :1]

    def ffn_grads_next(after):
        mine, got = _pair_wait(*flight["pair"], after)
        sums, flight["own"] = zip(*[_pair_sum(a, b, place, BF16, "pair_sum_" + n) for n, a, b in zip(late, mine, got)])
        *flight["chip"], token = _chip_start(list(sums))
        return token[:1, :1]

    loss, grad_x, g = _local_step(x[0], loss_target[0], wb, sp, late_weights, ffn_grads_ready, ffn_grads_next)

    flat = jnp.concatenate([g[n].reshape(-1) for n in SMALL] + [loss.reshape(-1)])
    pad = N_DEV * SMALL_ROWS * D_MODEL - flat.shape[0]
    small = jnp.concatenate([flat, jnp.zeros((pad,), F32)]).reshape(4, 2 * SMALL_ROWS, D_MODEL)
    by_owner = [_by_owner(n, g[n]) for n in early] + [small]
    got = _pair_exchange(by_owner)
    transit = [BF16] * len(early) + [F32]
    chip_sums, own_sums = zip(*[_pair_sum(a, b, place, t, "pair_sum_" + n)
                                for n, a, b, t in zip(early + ("small",), by_owner, got, transit)])
    landed = _chip_exchange(list(chip_sums))
    halves = {n: _chip_sum(o, t, "chip_sum_" + n) for n, o, t in zip(early + ("small",), own_sums, landed)}
    late_landed = _chip_wait(*flight["chip"], grad_x)
    for n, o, t in zip(late, flight["own"], late_landed):
        halves[n] = _chip_sum(o, t, "chip_sum_" + n)
    *others, small_all = _final_exchange([halves[n] for n in BIG], halves["small"])
    grads = {n: jnp.concatenate([jnp.where(core == 0, halves[n], o), jnp.where(core == 0, o, halves[n])], axis=0)
             for n, o in zip(BIG, others)}
    flat = small_all.reshape(-1)
    off = 0
    for n in SMALL:
        shape = (3, 4 * w[n].shape[-1]) if n == "conv_w" else w[n].shape[1:] if n != "norm_final_g" else w[n].shape
        size = math.prod(shape)
        grads[n] = flat[off:off + size].reshape(shape)
        off += size
    loss = flat[off]
    cw = w["conv_w"].shape[-1]
    grads["conv_w"] = lax.dynamic_slice_in_dim(grads["conv_w"], xy * cw, cw, axis=1)
    grads = {n: grads[n].reshape(w[n].shape) for n in WEIGHTS}

    delta, new_m, new_v = {}, {}, {}
    for n in BIG:
        delta[n], new_m[n], new_v[n] = _adamw(w[n], grads[n], m[n], v[n], "adamw_" + n)
    for group, name in ((("b_re", "b_im"), "adamw_b"), (tuple(n for n in SMALL if n not in ("b_re", "b_im")), "adamw_small")):
        row = lambda t: t.reshape(1, -1) if t.ndim == 1 else t
        d_, m_, v_ = _adamw_many(*[[row(t[n]) for n in group] for t in (w, grads, m, v)], name)
        for n, dn, mn, vn in zip(group, d_, m_, v_):
            delta[n], new_m[n], new_v[n] = (t.reshape(w[n].shape) for t in (dn, mn, vn))
    return (loss, grad_x[None], *[grads[n] for n in WEIGHTS], *[delta[n] for n in WEIGHTS],
            *[new_m[n] for n in WEIGHTS], *[new_v[n] for n in WEIGHTS])
```

```python
import functools
import math

import jax
import jax.numpy as jnp
from jax import lax
from jax.experimental import pallas as pl
from jax.experimental.pallas import tpu as pltpu

F32 = jnp.float32
BF16 = jnp.bfloat16

D_MODEL = 1024
N_Q_HEADS = 8
N_KV_HEADS = 2
HEAD_DIM = 64
ATTN_WIDTH = 512
KV_WIDTH = 128
QKV_WIDTH = ATTN_WIDTH + 2 * KV_WIDTH
WINDOW = 128
BLOCK = 128
ROPE_DIM = 16
ROPE_THETA = 500000.0
SSM_WIDTH = 512
SSM_GROUP = 16
N_SSM_GROUPS = 32
SSM_STATE = 64
IN_WIDTH = 1280
D_FF = 2816
EPS = 1e-6
ADAM_LR = 0.001
ADAM_B1 = 0.9
ADAM_B2 = 0.999
ADAM_EPS = 1e-08
ADAM_WD = 0.01
ADAM_STEP = 10

VMEM_BYTES_V7X = 64 * 1024 * 1024
SUBLANES = 8
LANES = 128
SSM_CB = 4
SSM_CH = 128
SSM_ST = 512
N_SEG = SUBLANES

NN = (((1,), (0,)), ((), ()))
NT = (((1,), (1,)), ((), ()))
TN = (((0,), (0,)), ((), ()))


def _params(sem=None, vmem_mb=48):
    return pltpu.CompilerParams(dimension_semantics=sem, vmem_limit_bytes=vmem_mb * 1024 * 1024)


def _dg(a, b, dims):
    return lax.dot_general(a, b, dims, preferred_element_type=F32)


def _sigmoid(x):
    return 1.0 / (1.0 + jnp.exp(-x))


_SQRT_HALF = 0.7071067811865476
_INV_SQRT_2PI = 0.3989422804014327


def _gelu(x):
    return 0.5 * x * (1.0 + lax.erf(x * _SQRT_HALF))


def _gelu_grad(x):
    return 0.5 * (1.0 + lax.erf(x * _SQRT_HALF)) + x * (_INV_SQRT_2PI * jnp.exp(-0.5 * x * x))


def _mm_nt(a, b, tm, tn, out_dtype, name):
    m, k = a.shape
    n = b.shape[0]

    def body(a_ref, b_ref, o_ref):
        o_ref[...] = _dg(a_ref[...], b_ref[...], NT).astype(out_dtype)

    return pl.pallas_call(
        body, grid=(m // tm, n // tn),
        in_specs=[pl.BlockSpec((tm, k), lambda i, j: (i, 0)), pl.BlockSpec((tn, k), lambda i, j: (j, 0))],
        out_specs=pl.BlockSpec((tm, tn), lambda i, j: (i, j)),
        out_shape=jax.ShapeDtypeStruct((m, n), out_dtype), name=name,
        compiler_params=_params(("parallel", "parallel")),
    )(a, b)


def _mm_tn(a, b, tm, tn, name):
    k, m = a.shape
    n = b.shape[1]

    def body(a_ref, b_ref, o_ref):
        o_ref[...] = _dg(a_ref[...], b_ref[...], TN)

    return pl.pallas_call(
        body, grid=(m // tm, n // tn),
        in_specs=[pl.BlockSpec((k, tm), lambda i, j: (0, i)), pl.BlockSpec((k, tn), lambda i, j: (0, j))],
        out_specs=pl.BlockSpec((tm, tn), lambda i, j: (i, j)),
        out_shape=jax.ShapeDtypeStruct((m, n), F32), name=name,
        compiler_params=_params(("parallel", "parallel")),
    )(a, b)


def _mm_nn_cols(a, b4, tm, name):
    m, k = a.shape
    s, _, n = b4.shape

    def body(a_ref, b_ref, o_ref):
        o_ref[...] = _dg(a_ref[...], b_ref[...], NN)

    return pl.pallas_call(
        body, grid=(m // tm, s),
        in_specs=[pl.BlockSpec((tm, k), lambda i, j: (i, 0)), pl.BlockSpec((None, k, n), lambda i, j: (j, 0, 0))],
        out_specs=pl.BlockSpec((tm, n), lambda i, j: (i, j)),
        out_shape=jax.ShapeDtypeStruct((m, s * n), F32), name=name,
        compiler_params=_params(("parallel", "parallel")),
    )(a, b4)


def _mm_tn_cols(a, b2, s, tm, name):
    k, m = a.shape
    h, _, wide = b2.shape
    per = s // h
    n = wide // per

    def body(a_ref, b_ref, o_ref):
        o_ref[...] = _dg(a_ref[...], b_ref[...], TN)

    return pl.pallas_call(
        body, grid=(s, m // tm),
        in_specs=[pl.BlockSpec((k, tm), lambda j, i: (0, i)),
                  pl.BlockSpec((None, k, n), lambda j, i: (j // per, 0, j % per))],
        out_specs=pl.BlockSpec((None, tm, n), lambda j, i: (j, i, 0)),
        out_shape=jax.ShapeDtypeStruct((s, m, n), F32), name=name,
        compiler_params=_params(("parallel", "parallel")),
    )(a, b2)


TM_EW = 256


def _rms_bwd_vals(xv, gv, dy):
    r = lax.rsqrt(jnp.mean(xv * xv, axis=-1, keepdims=True) + EPS)
    xh = xv * r
    dxh = dy * gv
    dx = r * (dxh - xh * jnp.mean(dxh * xh, axis=-1, keepdims=True))
    return dx, dy * xh


TM_FUSED = 256


def _rms_vals(xv, gv):
    return xv * lax.rsqrt(jnp.mean(xv * xv, axis=-1, keepdims=True) + EPS) * gv


def _rms_mm_split(x, g, wt, split, name):
    l, d = x.shape
    n = wt.shape[0]

    def body(x_ref, g_ref, w_ref, h_ref, lo_ref, hi_ref):
        h = _rms_vals(x_ref[...], g_ref[...]).astype(BF16)
        h_ref[...] = h
        out = _dg(h, w_ref[...], NT)
        lo_ref[...] = out[:, :split]
        hi_ref[...] = out[:, split:]

    row = lambda width: pl.BlockSpec((TM_FUSED, width), lambda i: (i, 0))
    return pl.pallas_call(
        body, grid=(l // TM_FUSED,),
        in_specs=[row(d), pl.BlockSpec((1, d), lambda i: (0, 0)), pl.BlockSpec((n, d), lambda i: (0, 0))],
        out_specs=[row(d), row(split), row(n - split)],
        out_shape=[jax.ShapeDtypeStruct((l, d), BF16), jax.ShapeDtypeStruct((l, split), F32),
                   jax.ShapeDtypeStruct((l, n - split), F32)],
        name=name, compiler_params=_params(("parallel",)),
    )(x, g, wt)


def _mm_res_rms(a, b, res, g, name):
    l, k = a.shape
    d = b.shape[1]

    def body(a_ref, b_ref, r_ref, g_ref, x_ref, h_ref):
        xv = r_ref[...] + _dg(a_ref[...], b_ref[...], NN)
        x_ref[...] = xv
        h_ref[...] = _rms_vals(xv, g_ref[...]).astype(BF16)

    row = lambda width: pl.BlockSpec((TM_FUSED, width), lambda i: (i, 0))
    return pl.pallas_call(
        body, grid=(l // TM_FUSED,),
        in_specs=[row(k), pl.BlockSpec((k, d), lambda i: (0, 0)), row(d), pl.BlockSpec((1, d), lambda i: (0, 0))],
        out_specs=[row(d), row(d)],
        out_shape=[jax.ShapeDtypeStruct((l, d), F32), jax.ShapeDtypeStruct((l, d), BF16)],
        name=name, compiler_params=_params(("parallel",)),
    )(a, b, res, g)


def _mm_res_loss(a, b, res, g, target):
    l, k = a.shape
    d = b.shape[1]

    def body(a_ref, b_ref, r_ref, g_ref, t_ref, loss_ref, dx_ref, dxb_ref, dg_ref):
        xv = r_ref[...] + _dg(a_ref[...], b_ref[...], NN)
        gv = g_ref[...]
        r = lax.rsqrt(jnp.mean(xv * xv, axis=-1, keepdims=True) + EPS)
        xh = xv * r
        e = xh * gv - t_ref[...]
        part = jnp.sum(jnp.sum(e * e, axis=1, keepdims=True), axis=0, keepdims=True) * (0.5 / d)
        dy = e * (1.0 / d)
        dxh = dy * gv
        dx = r * (dxh - xh * jnp.mean(dxh * xh, axis=-1, keepdims=True))
        dx_ref[...] = dx
        dxb_ref[...] = dx.astype(BF16)

        @pl.when(pl.program_id(0) == 0)
        def _():
            dg_ref[...] = jnp.zeros_like(dg_ref)
            loss_ref[...] = jnp.zeros_like(loss_ref)

        dg_ref[...] += jnp.sum(dy * xh, axis=0, keepdims=True)
        loss_ref[...] += part

    row = lambda width: pl.BlockSpec((TM_FUSED, width), lambda i: (i, 0))
    vec = pl.BlockSpec((1, d), lambda i: (0, 0))
    return pl.pallas_call(
        body, grid=(l // TM_FUSED,),
        in_specs=[row(k), pl.BlockSpec((k, d), lambda i: (0, 0)), row(d), vec, row(d)],
        out_specs=[pl.BlockSpec((1, 1), lambda i: (0, 0)), row(d), row(d), vec],
        out_shape=[jax.ShapeDtypeStruct((1, 1), F32), jax.ShapeDtypeStruct((l, d), F32),
                   jax.ShapeDtypeStruct((l, d), BF16), jax.ShapeDtypeStruct((1, d), F32)],
        name="mm_down_loss", compiler_params=_params(("arbitrary",)),
    )(a, b, res, g, target)


def _mm_rms_bwd(a, b, a_spec, b_spec, matmul, x, g, res, name):
    l, d = x.shape

    def body(a_ref, b_ref, x_ref, g_ref, res_ref, dx_ref, dxb_ref, dg_ref):
        dx, dgr = _rms_bwd_vals(x_ref[...], g_ref[...], matmul(a_ref, b_ref))
        dx = dx + res_ref[...]
        dx_ref[...] = dx
        dxb_ref[...] = dx.astype(BF16)

        @pl.when(pl.program_id(0) == 0)
        def _():
            dg_ref[...] = jnp.zeros_like(dg_ref)

        dg_ref[...] += jnp.sum(dgr, axis=0, keepdims=True)

    row = pl.BlockSpec((TM_FUSED, d), lambda i: (i, 0))
    vec = pl.BlockSpec((1, d), lambda i: (0, 0))
    return pl.pallas_call(
        body, grid=(l // TM_FUSED,), in_specs=[a_spec, b_spec, row, vec, row], out_specs=[row, row, vec],
        out_shape=[jax.ShapeDtypeStruct((l, d), F32), jax.ShapeDtypeStruct((l, d), BF16),
                   jax.ShapeDtypeStruct((1, d), F32)],
        name=name, compiler_params=_params(("arbitrary",)),
    )(a, b, x, g, res)


def _mm_nn_rms_bwd(a, b, x, g, res, name):
    return _mm_rms_bwd(a, b, pl.BlockSpec((TM_FUSED, a.shape[1]), lambda i: (i, 0)),
                       pl.BlockSpec(b.shape, lambda i: (0, 0)),
                       lambda a_ref, b_ref: _dg(a_ref[...], b_ref[...], NN), x, g, res, name)


def _mm_cols_rms_bwd(a2, b4, x, g, res, name):
    h, _, wide = a2.shape
    s, _, n = b4.shape
    per = s // h

    def matmul(a_ref, b_ref):
        acc = None
        for j in range(s):
            part = _dg(a_ref[j // per, :, (j % per) * n:(j % per + 1) * n], b_ref[j], NT)
            acc = part if acc is None else acc + part
        return acc

    return _mm_rms_bwd(a2, b4, pl.BlockSpec((h, TM_FUSED, wide), lambda i: (0, i, 0)),
                       pl.BlockSpec(b4.shape, lambda i: (0, 0, 0)), matmul, x, g, res, name)


def _mix_fwd(attn, ys, g_attn, g_ssm):
    l, w = attn.shape

    def body(a_ref, y_ref, ga_ref, gs_ref, o_ref):
        for src, gr, off in ((a_ref, ga_ref, 0), (y_ref, gs_ref, w)):
            xv = src[...]
            r = lax.rsqrt(jnp.mean(xv * xv, axis=-1, keepdims=True) + EPS)
            o_ref[:, off:off + w] = (xv * r * gr[...]).astype(BF16)

    row = pl.BlockSpec((TM_EW, w), lambda i: (i, 0))
    vec = pl.BlockSpec((1, w), lambda i: (0, 0))
    return pl.pallas_call(
        body, grid=(l // TM_EW,), in_specs=[row, row, vec, vec],
        out_specs=pl.BlockSpec((TM_EW, 2 * w), lambda i: (i, 0)),
        out_shape=jax.ShapeDtypeStruct((l, 2 * w), BF16), name="mix_fwd",
        compiler_params=_params(("parallel",)),
    )(attn, ys, g_attn, g_ssm)


def _mix_bwd(attn, ys, g_attn, g_ssm, dmixed):
    l, w = attn.shape

    def body(a_ref, y_ref, ga_ref, gs_ref, dm_ref, da_ref, dy_ref, dga_ref, dgs_ref):
        @pl.when(pl.program_id(0) == 0)
        def _():
            dga_ref[...] = jnp.zeros_like(dga_ref)
            dgs_ref[...] = jnp.zeros_like(dgs_ref)

        for src, gr, off, dst, dgr in ((a_ref, ga_ref, 0, da_ref, dga_ref), (y_ref, gs_ref, w, dy_ref, dgs_ref)):
            dx, dg_rows = _rms_bwd_vals(src[...], gr[...], dm_ref[:, off:off + w])
            dst[...] = dx
            dgr[...] += jnp.sum(dg_rows, axis=0, keepdims=True)

    row = pl.BlockSpec((TM_EW, w), lambda i: (i, 0))
    vec = pl.BlockSpec((1, w), lambda i: (0, 0))
    return pl.pallas_call(
        body, grid=(l // TM_EW,),
        in_specs=[row, row, vec, vec, pl.BlockSpec((TM_EW, 2 * w), lambda i: (i, 0))],
        out_specs=[row, row, vec, vec],
        out_shape=[jax.ShapeDtypeStruct((l, w), F32), jax.ShapeDtypeStruct((l, w), F32),
                   jax.ShapeDtypeStruct((1, w), F32), jax.ShapeDtypeStruct((1, w), F32)],
        name="mix_bwd", compiler_params=_params(("arbitrary",)),
    )(attn, ys, g_attn, g_ssm, dmixed)


def _rope_tables(l):
    half = ROPE_DIM // 2
    inv_freq = jnp.power(ROPE_THETA, -jnp.arange(half, dtype=F32) / half)
    ang = jnp.arange(l, dtype=F32)[:, None] * inv_freq[None, :]
    cos, sin = jnp.cos(ang), jnp.sin(ang)
    ones = jnp.ones((l, HEAD_DIM - ROPE_DIM), F32)
    zeros = jnp.zeros((l, HEAD_DIM - ROPE_DIM), F32)
    zh = jnp.zeros((l, half), F32)
    c = jnp.concatenate([cos, cos, ones], axis=1)
    s_lo = jnp.concatenate([-sin, zh, zeros], axis=1)
    s_hi = jnp.concatenate([zh, sin, zeros], axis=1)
    return tuple(jnp.tile(t, (1, LANES // HEAD_DIM)) for t in (c, s_lo, s_hi))


def _rope_fwd(proj, tabs):
    l = proj.shape[0]
    nq = ATTN_WIDTH // LANES

    def body(p_ref, c_ref, lo_ref, hi_ref, o_ref):
        c, lo, hi = c_ref[...], lo_ref[...], hi_ref[...]
        for blk in range(nq + 1):
            t = p_ref[:, blk * LANES:(blk + 1) * LANES]
            rot = t * c + pltpu.roll(t, LANES - 8, 1) * lo + pltpu.roll(t, 8, 1) * hi
            o_ref[:, blk * LANES:(blk + 1) * LANES] = rot.astype(BF16)
        o_ref[:, (nq + 1) * LANES:] = p_ref[:, (nq + 1) * LANES:].astype(BF16)

    tab = pl.BlockSpec((TM_EW, LANES), lambda i: (i, 0))
    return pl.pallas_call(
        body, grid=(l // TM_EW,),
        in_specs=[pl.BlockSpec((TM_EW, QKV_WIDTH), lambda i: (i, 0)), tab, tab, tab],
        out_specs=pl.BlockSpec((TM_EW, QKV_WIDTH), lambda i: (i, 0)),
        out_shape=jax.ShapeDtypeStruct((l, QKV_WIDTH), BF16), name="rope_fwd",
        compiler_params=_params(("parallel",)),
    )(proj, *tabs)


def _rope_bwd(dq, dkv, du_ssm, dpre, d_skip, tabs):
    l = dq.shape[0]
    nq = ATTN_WIDTH // LANES

    def body(dq_ref, dkv_ref, du_ref, dpre_ref, ds_ref, c_ref, lo_ref, hi_ref, o_ref):
        c, lo, hi = c_ref[...], lo_ref[...], hi_ref[...]
        for blk in range(nq + 1):
            t = dq_ref[:, blk * LANES:(blk + 1) * LANES] if blk < nq else dkv_ref[:, :KV_WIDTH]
            g = t * c + pltpu.roll(t * lo, 8, 1) + pltpu.roll(t * hi, LANES - 8, 1)
            o_ref[:, blk * LANES:(blk + 1) * LANES] = g.astype(BF16)
        o_ref[:, (nq + 1) * LANES:QKV_WIDTH] = dkv_ref[:, KV_WIDTH:].astype(BF16)
        o_ref[:, QKV_WIDTH:] = (du_ref[...] + dpre_ref[...] * ds_ref[...]).astype(BF16)

    tab = pl.BlockSpec((TM_EW, LANES), lambda i: (i, 0))
    wide = pl.BlockSpec((TM_EW, SSM_WIDTH), lambda i: (i, 0))
    return pl.pallas_call(
        body, grid=(l // TM_EW,),
        in_specs=[wide, pl.BlockSpec((TM_EW, 2 * KV_WIDTH), lambda i: (i, 0)), wide, wide,
                  pl.BlockSpec((1, SSM_WIDTH), lambda i: (0, 0)), tab, tab, tab],
        out_specs=pl.BlockSpec((TM_EW, IN_WIDTH), lambda i: (i, 0)),
        out_shape=jax.ShapeDtypeStruct((l, IN_WIDTH), BF16), name="rope_bwd",
        compiler_params=_params(("parallel",)),
    )(dq, dkv, du_ssm, dpre, d_skip, *tabs)


_Q_COLS = ATTN_WIDTH // LANES
_SCALE = HEAD_DIM ** -0.5
_NEG = -1e30


def _window_specs(nb, width, col):
    return [
        pl.BlockSpec((BLOCK, width), lambda n: (jnp.maximum(n - 1, 0), col)),
        pl.BlockSpec((BLOCK, width), lambda n: (n, col)),
        pl.BlockSpec((BLOCK, width), lambda n: (jnp.minimum(n + 1, nb - 1), col)),
    ]


def _stacked_sink(sink_ref, heads):
    rid = lax.broadcasted_iota(jnp.int32, (len(heads) * BLOCK, 1), 0)
    sk = jnp.full(rid.shape, sink_ref[0, heads[-1]], F32)
    for g in range(len(heads) - 2, -1, -1):
        sk = jnp.where(rid < (g + 1) * BLOCK, sink_ref[0, heads[g]], sk)
    return sk


def _attn_fwd(qkv, sink):
    l = qkv.shape[0]
    nb = l // BLOCK
    grp = N_Q_HEADS // N_KV_HEADS

    def body(sink_ref, q_ref, k0, k1, k2, v0, v1, v2, o_ref, lse_ref):
        n = pl.program_id(0)
        q = q_ref[...]
        kw = jnp.concatenate([k0[...], k1[...], k2[...]], axis=0)
        vw = jnp.concatenate([v0[...], v1[...], v2[...]], axis=0)
        row = lax.broadcasted_iota(jnp.int32, (grp * BLOCK, 3 * BLOCK), 0)
        col = lax.broadcasted_iota(jnp.int32, (grp * BLOCK, 3 * BLOCK), 1)
        valid = jnp.abs(col - BLOCK - (row & (BLOCK - 1))) <= WINDOW
        valid &= jnp.logical_not((n == 0) & (col < BLOCK))
        valid &= jnp.logical_not((n == nb - 1) & (col >= 2 * BLOCK))
        for hk in range(N_KV_HEADS):
            heads = range(hk * grp, (hk + 1) * grp)
            qs = jnp.concatenate([q[:, h * HEAD_DIM:(h + 1) * HEAD_DIM] for h in heads], axis=0)
            kh = kw[:, hk * HEAD_DIM:(hk + 1) * HEAD_DIM]
            vh = vw[:, hk * HEAD_DIM:(hk + 1) * HEAD_DIM]
            s = jnp.where(valid, _dg(qs, kh, NT) * _SCALE, _NEG)
            sk = _stacked_sink(sink_ref, heads)
            m = jnp.maximum(jnp.max(s, axis=1, keepdims=True), sk)
            p = jnp.exp(s - m)
            denom = jnp.sum(p, axis=1, keepdims=True) + jnp.exp(sk - m)
            o = _dg((p / denom).astype(BF16), vh, NN)
            lse = m + jnp.log(denom)
            for g, h in enumerate(heads):
                o_ref[:, h * HEAD_DIM:(h + 1) * HEAD_DIM] = o[g * BLOCK:(g + 1) * BLOCK]
                lse_ref[:, h:h + 1] = lse[g * BLOCK:(g + 1) * BLOCK]

    return pl.pallas_call(
        body, grid=(nb,),
        in_specs=[pl.BlockSpec(memory_space=pltpu.SMEM),
                  pl.BlockSpec((BLOCK, ATTN_WIDTH), lambda n: (n, 0))]
        + _window_specs(nb, KV_WIDTH, _Q_COLS) + _window_specs(nb, KV_WIDTH, _Q_COLS + 1),
        out_specs=[pl.BlockSpec((BLOCK, ATTN_WIDTH), lambda n: (n, 0)),
                   pl.BlockSpec((BLOCK, N_Q_HEADS), lambda n: (n, 0))],
        out_shape=[jax.ShapeDtypeStruct((l, ATTN_WIDTH), F32), jax.ShapeDtypeStruct((l, N_Q_HEADS), F32)],
        name="attn_fwd", compiler_params=_params(("parallel",)),
    )(sink, qkv, qkv, qkv, qkv, qkv, qkv, qkv)


def _attn_bwd(qkv, attn, dattn, lse, sink):
    l = qkv.shape[0]
    nb = l // BLOCK
    grp = N_Q_HEADS // N_KV_HEADS
    win = 3 * BLOCK

    def body(sink_ref, q_ref, k0, k1, k2, v0, v1, v2, o_ref, d_ref, l_ref, dq_ref, dkv_ref, dsink_ref, ring_ref):
        n = pl.program_id(0)

        @pl.when(n == 0)
        def _():
            dsink_ref[...] = jnp.zeros_like(dsink_ref)
            ring_ref[...] = jnp.zeros_like(ring_ref)

        @pl.when(n < nb)
        def _():
            first, last = n == 0, n == nb - 1
            cat = lambda a, b, c: jnp.concatenate([a[...], b[...], c[...]], axis=0)
            q, kw, vw = q_ref[...], cat(k0, k1, k2), cat(v0, v1, v2)
            dov = d_ref[...]
            prod = o_ref[...] * dov
            dob = dov.astype(BF16)
            lse = l_ref[...]
            row = lax.broadcasted_iota(jnp.int32, (grp * BLOCK, win), 0)
            col = lax.broadcasted_iota(jnp.int32, (grp * BLOCK, win), 1)
            valid = jnp.abs(col - BLOCK - (row & (BLOCK - 1))) <= WINDOW
            valid &= jnp.logical_not(first & (col < BLOCK))
            valid &= jnp.logical_not(last & (col >= 2 * BLOCK))

            dsink_parts, dks, dvs = [], [], []
            for hk in range(N_KV_HEADS):
                heads = range(hk * grp, (hk + 1) * grp)
                ksl = slice(hk * HEAD_DIM, (hk + 1) * HEAD_DIM)
                hsl = [slice(h * HEAD_DIM, (h + 1) * HEAD_DIM) for h in heads]
                stack = lambda parts: jnp.concatenate(parts, axis=0)
                qs = stack([q[:, s_] for s_ in hsl])
                dos = stack([dob[:, s_] for s_ in hsl])
                deltas = stack([jnp.sum(prod[:, s_], axis=1, keepdims=True) for s_ in hsl])
                lses = stack([lse[:, h:h + 1] for h in heads])
                kh, vh = kw[:, ksl], vw[:, ksl]
                s = jnp.where(valid, _dg(qs, kh, NT) * _SCALE, _NEG)
                p = jnp.exp(s - lses)
                dp = _dg(dos, vh, NT)
                ds = (p * (dp - deltas) * _SCALE).astype(BF16)
                dq = _dg(ds, kh, NN)
                sink_rows = jnp.exp(_stacked_sink(sink_ref, heads) - lses) * deltas
                for g in range(grp):
                    dq_ref[:, hsl[g]] = dq[g * BLOCK:(g + 1) * BLOCK]
                    dsink_parts.append(jnp.sum(sink_rows[g * BLOCK:(g + 1) * BLOCK], axis=0, keepdims=True))
                dks.append(_dg(ds, qs, TN))
                dvs.append(_dg(p.astype(BF16), dos, TN))
            dsink_ref[...] -= jnp.concatenate(dsink_parts, axis=1)
            part = jnp.concatenate(dks + dvs, axis=1)
            ring_ref[(n + 2) % 3] += part[0:BLOCK]
            ring_ref[n % 3] += part[BLOCK:2 * BLOCK]
            ring_ref[(n + 1) % 3] = part[2 * BLOCK:]

        @pl.when(n >= 1)
        def _():
            dkv_ref[...] = ring_ref[(n + 2) % 3]

    centre = lambda n: jnp.minimum(n, nb - 1)
    window = lambda width, col: [
        pl.BlockSpec((BLOCK, width), lambda n: (jnp.maximum(centre(n) - 1, 0), col)),
        pl.BlockSpec((BLOCK, width), lambda n: (centre(n), col)),
        pl.BlockSpec((BLOCK, width), lambda n: (jnp.minimum(centre(n) + 1, nb - 1), col))]
    own = lambda width: pl.BlockSpec((BLOCK, width), lambda n: (centre(n), 0))
    return pl.pallas_call(
        body, grid=(nb + 1,),
        in_specs=[pl.BlockSpec(memory_space=pltpu.SMEM), own(ATTN_WIDTH)]
        + window(KV_WIDTH, _Q_COLS) + window(KV_WIDTH, _Q_COLS + 1)
        + [own(ATTN_WIDTH), own(ATTN_WIDTH), own(N_Q_HEADS)],
        out_specs=[own(ATTN_WIDTH), pl.BlockSpec((BLOCK, 2 * KV_WIDTH), lambda n: (jnp.maximum(n - 1, 0), 0)),
                   pl.BlockSpec((1, N_Q_HEADS), lambda n: (0, 0))],
        out_shape=[jax.ShapeDtypeStruct((l, ATTN_WIDTH), F32), jax.ShapeDtypeStruct((l, 2 * KV_WIDTH), F32),
                   jax.ShapeDtypeStruct((1, N_Q_HEADS), F32)],
        scratch_shapes=[pltpu.VMEM((3, BLOCK, 2 * KV_WIDTH), F32)],
        name="attn_bwd", compiler_params=_params(("arbitrary",)),
    )(sink, qkv, qkv, qkv, qkv, qkv, qkv, qkv, attn, dattn, lse)


def _ssm_disc(a_re, a_im, log_step, b_re, b_im):
    step = jnp.exp(log_step)[..., None]
    mag = jnp.exp(a_re * step)
    lb_re, lb_im = mag * jnp.cos(a_im * step), mag * jnp.sin(a_im * step)
    nr, ni = lb_re - 1.0, lb_im
    den = a_re * a_re + a_im * a_im
    f_re = ((nr * a_re + ni * a_im) / den)[..., None]
    f_im = ((ni * a_re - nr * a_im) / den)[..., None]
    return lb_re, lb_im, f_re * b_re - f_im * b_im, f_re * b_im + f_im * b_re


def _ssm_pack(lb_re, lb_im, bb_re, bb_im, c_re, c_im):
    eye = jnp.eye(SSM_CH // SSM_GROUP, dtype=F32)
    ng = SSM_CH // SSM_GROUP

    def diag_b(bb):
        t = bb.reshape(2, SSM_CB, ng, SSM_STATE, SSM_GROUP)
        return jnp.einsum('dkgpc,gh->dkgchp', t, eye).reshape(2, SSM_CB, SSM_CH, SSM_ST)

    def diag_c(cc):
        t = cc.reshape(2, SSM_CB, ng, SSM_GROUP, SSM_STATE)
        return jnp.einsum('dkgcp,gh->dkhpgc', t, eye).reshape(2, SSM_CB, SSM_ST, SSM_CH)

    bcat = jnp.concatenate([diag_b(bb_re), diag_b(bb_im)], axis=-1)
    ccat = jnp.concatenate([diag_c(c_re), -diag_c(c_im)], axis=-2)
    lam_re = lb_re.reshape(2, SSM_CB, 1, SSM_ST)
    lam_im = lb_im.reshape(2, SSM_CB, 1, SSM_ST)
    return bcat, ccat, lam_re, lam_im


def _ssm_unpack(dbcat, dccat, dlam_re, dlam_im):
    ng = SSM_CH // SSM_GROUP
    eye = jnp.eye(ng, dtype=F32)

    def undiag_b(t):
        t = t.reshape(2, SSM_CB, ng, SSM_GROUP, ng, SSM_STATE)
        return jnp.einsum('dkgchp,gh->dkgpc', t, eye).reshape(2, N_SSM_GROUPS, SSM_STATE, SSM_GROUP)

    def undiag_c(t):
        t = t.reshape(2, SSM_CB, ng, SSM_STATE, ng, SSM_GROUP)
        return jnp.einsum('dkhpgc,gh->dkgcp', t, eye).reshape(2, N_SSM_GROUPS, SSM_GROUP, SSM_STATE)

    dbb_re, dbb_im = undiag_b(dbcat[..., :SSM_ST]), undiag_b(dbcat[..., SSM_ST:])
    dc_re, dc_im = undiag_c(dccat[:, :, :SSM_ST]), -undiag_c(dccat[:, :, SSM_ST:])
    shape = (2, N_SSM_GROUPS, SSM_STATE)
    return dlam_re.reshape(shape), dlam_im.reshape(shape), dbb_re, dbb_im, dc_re, dc_im


def _to_segments(t):
    l, w = t.shape
    return t.reshape(N_SEG, l // N_SEG, w).transpose(1, 0, 2).reshape(l, w)


def _from_segments(t):
    l, w = t.shape
    return t.reshape(l // N_SEG, N_SEG, w).transpose(1, 0, 2).reshape(l, w)


SSM_RC = 256
SSM_JC = SSM_RC // N_SEG
_RE, _IM = pl.ds(0, SSM_ST), pl.ds(SSM_ST, SSM_ST)


def _cfma(ar, ai, xr, xi, br, bi):
    return ar * xr - ai * xi + br, ar * xi + ai * xr + bi


def _chunk_rows(ci, rev, nc):
    start = jnp.where(rev, (nc - 1 - ci) * SSM_RC, ci * SSM_RC)
    return pl.ds(pl.multiple_of(start, SSM_RC), SSM_RC)


def _scan_chunk(src, dst, ar, ai, rev, nj, ci, carry, prev_ref=None):
    def rows_of(staged, j, k):
        at = jnp.where(rev, SSM_JC - 1 - k, k) if staged else j
        return pl.ds(pl.multiple_of(at * N_SEG, N_SEG), N_SEG)

    for k in range(SSM_JC):
        jj = ci * SSM_JC + k
        j = jnp.where(rev, nj - 1 - jj, jj)
        rows = rows_of(src[1], j, k)
        nr, ni = _cfma(ar, ai, carry[0], carry[1], src[0][rows, _RE], src[0][rows, _IM])
        if dst is not None:
            rows = rows_of(dst[1], j, k)
            dst[0][rows, _RE] = nr
            dst[0][rows, _IM] = ni
        if prev_ref is None:
            carry = (nr, ni)
            continue
        jp = jnp.where(rev, j - 1, j + 1)
        if k == SSM_JC - 1:
            inside = jnp.where((jp >= 0) & (jp < nj), 1.0, 0.0)
            jp = jnp.clip(jp, 0, nj - 1)
        prow = pl.ds(pl.multiple_of(jp * N_SEG, N_SEG), N_SEG)
        xr, xi = prev_ref[prow, _RE], prev_ref[prow, _IM]
        sr, si = nr * xr + ni * xi, ni * xr - nr * xi
        if k == SSM_JC - 1:
            sr, si = inside * sr, inside * si
        carry = (nr, ni, carry[2] + sr, carry[3] + si)
    return carry


def _segment_inits(ar, ai, end_r, end_i, rev, nj):
    pr, pi = ar, ai
    for _ in range(int(math.log2(nj))):
        pr, pi = pr * pr - pi * pi, 2.0 * pr * pi
    seg = lax.broadcasted_iota(jnp.int32, end_r.shape, 0)
    zero = jnp.zeros_like(end_r)

    def chain(shift, keep):
        ir, ii = zero, zero
        for _ in range(N_SEG - 1):
            tr, ti = _cfma(pr, pi, ir, ii, end_r, end_i)
            ir = jnp.where(keep, pltpu.roll(tr, shift, 0), 0.0)
            ii = jnp.where(keep, pltpu.roll(ti, shift, 0), 0.0)
        return ir, ii

    up_r, up_i = chain(1, seg >= 1)
    dn_r, dn_i = chain(N_SEG - 1, seg <= N_SEG - 2)
    return jnp.where(rev, dn_r, up_r), jnp.where(rev, dn_i, up_i)


def _ssm_specs(l):
    act = pl.BlockSpec((l, SSM_CH), lambda k, d: (0, k))
    bmat = pl.BlockSpec((None, None, SSM_CH, 2 * SSM_ST), lambda k, d: (d, k, 0, 0))
    cmat = pl.BlockSpec((None, None, 2 * SSM_ST, SSM_CH), lambda k, d: (d, k, 0, 0))
    lam = pl.BlockSpec((None, None, 1, SSM_ST), lambda k, d: (d, k, 0, 0))
    return act, bmat, cmat, lam


def _ssm_fwd(u_seg, bcat, ccat, lam_re, lam_im):
    l = u_seg.shape[0]
    nj = l // N_SEG
    nc = l // SSM_RC

    def body(u_ref, b_ref, c_ref, lr_ref, li_ref, y_ref, keep_ref, xs_ref, stage0, stage1, keep_sem):
        k, d = pl.program_id(0), pl.program_id(1)
        rev = d == 1
        shape = (N_SEG, SSM_ST)
        ar, ai = jnp.broadcast_to(lr_ref[...], shape), jnp.broadcast_to(li_ref[...], shape)
        zero = jnp.zeros(shape, F32)

        def inputs(ci, stage):
            rows = _chunk_rows(ci, rev, nc)
            bu = _dg(u_ref[rows, :], b_ref[...], NN)
            stage[...] = bu
            xs_ref[rows, :] = bu

        def first(stage, ci, carry):
            return _scan_chunk((stage, True), None, ar, ai, rev, nj, ci, carry)

        def first_pass(t, carry):
            inputs(2 * t + 1, stage1)
            carry = first(stage0, 2 * t, carry)
            inputs(2 * t + 2, stage0)
            return first(stage1, 2 * t + 1, carry)

        inputs(0, stage0)
        carry = lax.fori_loop(0, nc // 2 - 1, first_pass, (zero, zero))
        inputs(nc - 1, stage1)
        carry = first(stage0, nc - 2, carry)
        end_r, end_i = first(stage1, nc - 1, carry)
        init = _segment_inits(ar, ai, end_r, end_i, rev, nj)

        @pl.when(d == 0)
        def _():
            y_ref[...] = jnp.zeros_like(y_ref)

        def outputs(ci):
            rows = _chunk_rows(ci, rev, nc)
            y_ref[rows, :] += _dg(xs_ref[rows, :].astype(BF16), c_ref[...], NN)
            pltpu.make_async_copy(xs_ref.at[rows], keep_ref.at[d, k, rows], keep_sem).start()

        def second(ci, carry):
            return _scan_chunk((xs_ref, False), (xs_ref, False), ar, ai, rev, nj, ci, carry)

        def second_pass(ci, carry):
            outputs(ci - 1)
            return second(ci, carry)

        lax.fori_loop(1, nc, second_pass, second(0, init))
        outputs(nc - 1)
        pltpu.make_async_copy(xs_ref, keep_ref.at[d, k], keep_sem).wait()

    act, bmat, cmat, lam = _ssm_specs(l)
    return pl.pallas_call(
        body, grid=(SSM_CB, 2), in_specs=[act, bmat, cmat, lam, lam], out_specs=[act, ANY],
        out_shape=[jax.ShapeDtypeStruct((l, SSM_WIDTH), F32),
                   jax.ShapeDtypeStruct((2, SSM_CB, l, 2 * SSM_ST), F32)],
        scratch_shapes=[pltpu.VMEM((l, 2 * SSM_ST), F32), pltpu.VMEM((SSM_RC, 2 * SSM_ST), F32),
                        pltpu.VMEM((SSM_RC, 2 * SSM_ST), F32), pltpu.SemaphoreType.DMA],
        name="ssm_fwd", compiler_params=_params(("parallel", "arbitrary"), vmem_mb=56),
    )(u_seg, bcat.astype(BF16), ccat.astype(BF16), lam_re, lam_im)


def _ssm_bwd(u_seg, dy_seg, states, bcat, ccat, lam_re, lam_im):
    l = u_seg.shape[0]
    nj = l // N_SEG
    nc = l // SSM_RC

    def body(u_ref, dy_ref, keep_ref, b_ref, c_ref, lr_ref, li_ref,
             du_ref, db_ref, dc_ref, dlr_ref, dli_ref, xs_ref, gs_ref, stage0, stage1, keep_sem):
        k, d = pl.program_id(0), pl.program_id(1)
        rev = d == 1
        back = jnp.logical_not(rev)
        shape = (N_SEG, SSM_ST)
        ar, ai = jnp.broadcast_to(lr_ref[...], shape), -jnp.broadcast_to(li_ref[...], shape)
        zero = jnp.zeros(shape, F32)
        fetch = pltpu.make_async_copy(keep_ref.at[d, k], xs_ref, keep_sem)
        fetch.start()

        def inputs(ci, stage):
            rows = _chunk_rows(ci, back, nc)
            dx = _dg(dy_ref[rows, :], c_ref[...], NT)
            stage[...] = dx
            gs_ref[rows, :] = dx

        def first(stage, ci, carry):
            return _scan_chunk((stage, True), None, ar, ai, back, nj, ci, carry)

        def first_pass(t, carry):
            inputs(2 * t + 1, stage1)
            carry = first(stage0, 2 * t, carry)
            inputs(2 * t + 2, stage0)
            return first(stage1, 2 * t + 1, carry)

        inputs(0, stage0)
        carry = lax.fori_loop(0, nc // 2 - 1, first_pass, (zero, zero))
        inputs(nc - 1, stage1)
        carry = first(stage0, nc - 2, carry)
        end_r, end_i = first(stage1, nc - 1, carry)
        init = _segment_inits(ar, ai, end_r, end_i, back, nj)
        fetch.wait()
        db_ref[...] = jnp.zeros_like(db_ref)
        dc_ref[...] = jnp.zeros_like(dc_ref)

        @pl.when(d == 0)
        def _():
            du_ref[...] = jnp.zeros_like(du_ref)

        def outputs(ci, stage):
            rows = _chunk_rows(ci, back, nc)
            g = stage[...].astype(BF16)
            dc_ref[...] += _dg(xs_ref[rows, :].astype(BF16), dy_ref[rows, :], TN)
            db_ref[...] += _dg(u_ref[rows, :], g, TN)
            du_ref[rows, :] += _dg(g, b_ref[...], NT)

        def second(ci, stage, carry):
            return _scan_chunk((gs_ref, False), (stage, True), ar, ai, back, nj, ci, carry, prev_ref=xs_ref)

        def second_pass(t, carry):
            outputs(2 * t, stage0)
            carry = second(2 * t + 1, stage1, carry)
            outputs(2 * t + 1, stage1)
            return second(2 * t + 2, stage0, carry)

        carry = lax.fori_loop(0, nc // 2 - 1, second_pass, second(0, stage0, init + (zero, zero)))
        outputs(nc - 2, stage0)
        gr, gi, acc_r, acc_i = second(nc - 1, stage1, carry)
        outputs(nc - 1, stage1)

        seg = lax.broadcasted_iota(jnp.int32, shape, 0)
        jb = jnp.where(rev, nj - 1, 0)
        erow = pl.ds(pl.multiple_of((nj - 1 - jb) * N_SEG, N_SEG), N_SEG)

        def before(t):
            up = jnp.where(seg >= 1, pltpu.roll(t, 1, 0), 0.0)
            down = jnp.where(seg <= N_SEG - 2, pltpu.roll(t, N_SEG - 1, 0), 0.0)
            return jnp.where(rev, down, up)

        init_r, init_i = before(xs_ref[erow, _RE]), before(xs_ref[erow, _IM])
        acc_r = acc_r + gr * init_r + gi * init_i
        acc_i = acc_i + gi * init_r - gr * init_i
        dlr_ref[...] = jnp.sum(acc_r, axis=0, keepdims=True)
        dli_ref[...] = jnp.sum(acc_i, axis=0, keepdims=True)

    act, bmat, cmat, lam = _ssm_specs(l)
    return pl.pallas_call(
        body, grid=(SSM_CB, 2), in_specs=[act, act, ANY, bmat, cmat, lam, lam],
        out_specs=[act, bmat, cmat, lam, lam],
        out_shape=[jax.ShapeDtypeStruct((l, SSM_WIDTH), F32),
                   jax.ShapeDtypeStruct(bcat.shape, F32), jax.ShapeDtypeStruct(ccat.shape, F32),
                   jax.ShapeDtypeStruct(lam_re.shape, F32), jax.ShapeDtypeStruct(lam_im.shape, F32)],
        scratch_shapes=[pltpu.VMEM((l, 2 * SSM_ST), F32), pltpu.VMEM((l, 2 * SSM_ST), F32),
                        pltpu.VMEM((SSM_RC, 2 * SSM_ST), F32), pltpu.VMEM((SSM_RC, 2 * SSM_ST), F32),
                        pltpu.SemaphoreType.DMA],
        name="ssm_bwd", compiler_params=_params(("parallel", "arbitrary"), vmem_mb=58),
    )(u_seg, dy_seg, states, bcat.astype(BF16), ccat.astype(BF16), lam_re, lam_im)


def _glu_fwd(y_ssm, u, d_skip, w_glu):
    l, w = u.shape

    def body(y_ref, u_ref, d_ref, w_ref, pre_ref, s_ref, ys_ref):
        pre = y_ref[...] + d_ref[...] * u_ref[...]
        z = _gelu(pre)
        s = _dg(z.astype(BF16), w_ref[...], NN)
        pre_ref[...] = pre
        s_ref[...] = s
        ys_ref[...] = z * _sigmoid(s)

    row = pl.BlockSpec((TM_EW, w), lambda i: (i, 0))
    out = jax.ShapeDtypeStruct((l, w), F32)
    return pl.pallas_call(
        body, grid=(l // TM_EW,),
        in_specs=[row, row, pl.BlockSpec((1, w), lambda i: (0, 0)), pl.BlockSpec((w, w), lambda i: (0, 0))],
        out_specs=[row, row, row], out_shape=[out, out, out], name="glu_fwd",
        compiler_params=_params(("parallel",)),
    )(y_ssm, u, d_skip, w_glu)


def _glu_bwd(pre, s, dys, u, d_skip, w_glu):
    l, w = u.shape

    def body(pre_ref, s_ref, dys_ref, u_ref, d_ref, w_ref, dpre_ref, z_ref, ds_ref, dd_ref):
        pre, dys = pre_ref[...], dys_ref[...]
        z = _gelu(pre)
        sig = _sigmoid(s_ref[...])
        ds = (dys * z * sig * (1.0 - sig)).astype(BF16)
        dz = dys * sig + _dg(ds, w_ref[...], NT)
        dpre = dz * _gelu_grad(pre)
        dpre_ref[...] = dpre
        z_ref[...] = z.astype(BF16)
        ds_ref[...] = ds

        @pl.when(pl.program_id(0) == 0)
        def _():
            dd_ref[...] = jnp.zeros_like(dd_ref)

        dd_ref[...] += jnp.sum(dpre * u_ref[...], axis=0, keepdims=True)

    row = pl.BlockSpec((TM_EW, w), lambda i: (i, 0))
    vec = pl.BlockSpec((1, w), lambda i: (0, 0))
    return pl.pallas_call(
        body, grid=(l // TM_EW,),
        in_specs=[row, row, row, row, vec, pl.BlockSpec((w, w), lambda i: (0, 0))],
        out_specs=[row, row, row, vec],
        out_shape=[jax.ShapeDtypeStruct((l, w), F32), jax.ShapeDtypeStruct((l, w), BF16),
                   jax.ShapeDtypeStruct((l, w), BF16), jax.ShapeDtypeStruct((1, w), F32)],
        name="glu_bwd", compiler_params=_params(("arbitrary",)),
    )(pre, s, dys, u, d_skip, w_glu)


TM_CV = 512
TC_CV = 256
TM_CF = 256
TC_CF = D_FF // 2
HALO = SUBLANES


def _conv_specs(l, col0, tm=TM_CV, tc=TC_CV):
    per = tm // HALO
    nh = l // HALO
    off = col0 // tc
    return [
        pl.BlockSpec((HALO, tc), lambda j, i: (jnp.maximum(i * per - 1, 0), j + off)),
        pl.BlockSpec((tm, tc), lambda j, i: (i, j + off)),
        pl.BlockSpec((HALO, tc), lambda j, i: (jnp.minimum((i + 1) * per, nh - 1), j + off)),
    ]


def _ext(prev_ref, mid_ref, next_ref, first, last):
    p = jnp.where(first, 0.0, prev_ref[...])
    n = jnp.where(last, 0.0, next_ref[...])
    return jnp.concatenate([p, mid_ref[...], n], axis=0)


def _shift_dn(t):
    return pltpu.roll(t, 1, 0)


def _shift_up(t):
    return pltpu.roll(t, t.shape[0] - 1, 0)


def _conv3(e, w_ref, b_ref):
    return w_ref[0:1, :] * _shift_dn(e) + w_ref[1:2, :] * e + w_ref[2:3, :] * _shift_up(e) + b_ref[...]


def _convffn_fwd(up_pre, conv_w, conv_b):
    l = up_pre.shape[0]
    tm, tc = TM_CF, TC_CF
    ni = l // tm
    wspec = lambda off: pl.BlockSpec((3, tc), lambda j, i: (0, j + off))
    bspec = lambda off: pl.BlockSpec((1, tc), lambda j, i: (0, j + off))
    voff = D_FF // tc

    def body(gp, gm, gn, vp, vm, vn, wg, bg, wv, bv, o_ref):
        i = pl.program_id(1)
        first, last = i == 0, i == ni - 1
        gate = _conv3(_ext(gp, gm, gn, first, last), wg, bg)[HALO:HALO + tm]
        val = _conv3(_ext(vp, vm, vn, first, last), wv, bv)[HALO:HALO + tm]
        o_ref[...] = (gate * _sigmoid(gate) * val).astype(BF16)

    return pl.pallas_call(
        body, grid=(D_FF // tc, ni),
        in_specs=_conv_specs(l, 0, tm, tc) + _conv_specs(l, D_FF, tm, tc)
        + [wspec(0), bspec(0), wspec(voff), bspec(voff)],
        out_specs=pl.BlockSpec((tm, tc), lambda j, i: (i, j)),
        out_shape=jax.ShapeDtypeStruct((l, D_FF), BF16), name="convffn_fwd",
        compiler_params=_params(("parallel", "parallel")),
    )(up_pre, up_pre, up_pre, up_pre, up_pre, up_pre, conv_w, conv_b, conv_w, conv_b)


HALO_B = 2 * SUBLANES


def _convffn_bwd(up_pre, dx2b, w_down, conv_w, conv_b):
    l = up_pre.shape[0]
    ni = l // TM_CV
    d = dx2b.shape[1]
    wspec = lambda off: pl.BlockSpec((3, TC_CV), lambda i, j: (0, j + off))
    bspec = lambda off: pl.BlockSpec((1, TC_CV), lambda i, j: (0, j + off))
    voff = D_FF // TC_CV
    swap = lambda spec: pl.BlockSpec(spec.block_shape, lambda i, j, f=spec.index_map: f(j, i))
    per, nh = TM_CV // HALO_B, l // HALO_B
    dx_specs = [pl.BlockSpec((HALO_B, d), lambda i, j: (jnp.maximum(i * per - 1, 0), 0)),
                pl.BlockSpec((TM_CV, d), lambda i, j: (i, 0)),
                pl.BlockSpec((HALO_B, d), lambda i, j: (jnp.minimum((i + 1) * per, nh - 1), 0))]

    def body(gp, gm, gn, vp, vm, vn, xp, xm, xn, wd, wg, bg, wv, bv, dup_ref, pg_ref, pv_ref):
        i = pl.program_id(0)
        first, last = i == 0, i == ni - 1
        ge, ve = _ext(gp, gm, gn, first, last), _ext(vp, vm, vn, first, last)
        zero = jnp.zeros((HALO_B, d), BF16)
        dx = jnp.concatenate([jnp.where(first, zero, xp[...]), xm[...], jnp.where(last, zero, xn[...])], axis=0)
        de = _dg(dx, wd[...], NT)[HALO_B - HALO:HALO_B + TM_CV + HALO]
        gate, val = _conv3(ge, wg, bg), _conv3(ve, wv, bv)
        sig = _sigmoid(gate)
        silu = gate * sig
        dgate = de * val * (sig + silu * (1.0 - sig))
        dval = de * silu
        mid = slice(HALO, HALO + TM_CV)
        rid = lax.broadcasted_iota(jnp.int32, (SUBLANES, TC_CV), 0)
        for half, (dup, e, w_ref, p_ref) in enumerate(((dgate, ge, wg, pg_ref), (dval, ve, wv, pv_ref))):
            dpre = w_ref[0:1, :] * _shift_up(dup) + w_ref[1:2, :] * dup + w_ref[2:3, :] * _shift_dn(dup)
            dup_ref[half] = dpre[mid].astype(BF16)
            dm_ = dup[mid]
            sums = [jnp.sum(dm_ * _shift_dn(e)[mid], axis=0, keepdims=True),
                    jnp.sum(dm_ * e[mid], axis=0, keepdims=True),
                    jnp.sum(dm_ * _shift_up(e)[mid], axis=0, keepdims=True),
                    jnp.sum(dm_, axis=0, keepdims=True)]
            acc = jnp.zeros((SUBLANES, TC_CV), F32)
            for k, sk in enumerate(sums):
                acc = jnp.where(rid == k, sk, acc)
            p_ref[...] = acc

    par = pl.BlockSpec((None, SUBLANES, TC_CV), lambda i, j: (i, 0, j))
    dup, pg, pv = pl.pallas_call(
        body, grid=(ni, D_FF // TC_CV),
        in_specs=[swap(s) for s in _conv_specs(l, 0) + _conv_specs(l, D_FF)] + dx_specs
        + [pl.BlockSpec((TC_CV, d), lambda i, j: (j, 0)), wspec(0), bspec(0), wspec(voff), bspec(voff)],
        out_specs=[pl.BlockSpec((2, TM_CV, TC_CV), lambda i, j: (0, i, j)), par, par],
        out_shape=[jax.ShapeDtypeStruct((2, l, D_FF), BF16),
                   jax.ShapeDtypeStruct((ni, SUBLANES, D_FF), F32), jax.ShapeDtypeStruct((ni, SUBLANES, D_FF), F32)],
        name="convffn_bwd", compiler_params=_params(("parallel", "parallel")),
    )(up_pre, up_pre, up_pre, up_pre, up_pre, up_pre, dx2b, dx2b, dx2b, w_down, conv_w, conv_b, conv_w, conv_b)
    return dup, jnp.concatenate([jnp.sum(pg, axis=0), jnp.sum(pv, axis=0)], axis=1)


def _local_step(x, target, wb, sp, late_weights=None, ffn_grads_ready=None, ffn_grads_next=None):
    l = x.shape[0]
    tabs = _rope_tables(l)
    disc = _ssm_disc(sp["a_re"], sp["a_im"], sp["log_step"], sp["b_re"], sp["b_im"])
    bcat, ccat, lam_re, lam_im = _ssm_pack(*disc, sp["c_re"], sp["c_im"])
    d_skip = sp["d_skip"].reshape(1, SSM_WIDTH)

    big = min(l, 1024)
    h, proj, u = _rms_mm_split(x, sp["norm_mix_g"], wb["w_in"], QKV_WIDTH, "mm_in")
    qkv = _rope_fwd(proj, tabs)
    attn, lse = _attn_fwd(qkv, sp["sink"])
    u_seg = _to_segments(u).astype(BF16)
    y_seg, states = _ssm_fwd(u_seg, bcat, ccat, lam_re, lam_im)
    y_ssm = _from_segments(y_seg)
    pre, s_glu, ys = _glu_fwd(y_ssm, u, d_skip, wb["w_glu"])
    mixed = _mix_fwd(attn, ys, sp["norm_attn_g"], sp["norm_ssm_g"])
    x1, h2 = _mm_res_rms(mixed, wb["w_out"], x, sp["norm_ffn_g"], "mm_out")
    if late_weights is not None:
        wb = dict(wb, **late_weights(h2))
    up_pre = _mm_nn_cols(h2, wb["w_up"], big, "mm_up")
    act = _convffn_fwd(up_pre, sp["conv_w"], sp["conv_b"])
    loss, dx2, dx2b, d_final_g = _mm_res_loss(act, wb["w_down"], x1, sp["norm_final_g"].reshape(1, D_MODEL), target)

    g = {"norm_final_g": d_final_g.reshape(D_MODEL)}
    g["w_down"] = _mm_tn(act, dx2b, D_FF // 2, 512, "mm_down_dw")
    dup_pre, conv_par = _convffn_bwd(up_pre, dx2b, wb["w_down"], sp["conv_w"], sp["conv_b"])
    g["conv_w"], g["conv_b"] = conv_par[0:3], conv_par[3:4]
    g["w_up"] = _mm_tn_cols(h2, dup_pre, wb["w_up"].shape[0], 512, "mm_up_dw")
    zero = ffn_grads_ready(g["w_up"], g["w_down"]) if ffn_grads_ready is not None else 0.0
    dx1, dx1b, g["norm_ffn_g"] = _mm_cols_rms_bwd(dup_pre, wb["w_up"], x1, sp["norm_ffn_g"] + zero, dx2, "mm_up_dx")
    dmixed = _mm_nt(dx1b, wb["w_out"], big, 1024, F32, "mm_out_dx")
    g["w_out"] = _mm_tn(mixed, dx1b, 1024, 1024, "mm_out_dw")
    zero = ffn_grads_next(dmixed) if ffn_grads_next is not None else 0.0
    dattn, dys, g["norm_attn_g"], g["norm_ssm_g"] = _mix_bwd(attn, ys, sp["norm_attn_g"] + zero, sp["norm_ssm_g"],
                                                            dmixed)
    dpre, zb, dsb, dd = _glu_bwd(pre, s_glu, dys, u, d_skip, wb["w_glu"])
    g["d_skip"] = dd.reshape(N_SSM_GROUPS, SSM_GROUP)
    g["w_glu"] = _mm_tn(zb, dsb, 512, 512, "mm_glu_dw")
    du_seg, dbcat, dccat, dlam_re, dlam_im = _ssm_bwd(u_seg, _to_segments(dpre).astype(BF16), states, bcat, ccat,
                                                      lam_re, lam_im)
    dlb_re, dlb_im, dbb_re, dbb_im, g["c_re"], g["c_im"] = _ssm_unpack(dbcat, dccat, dlam_re, dlam_im)
    _, disc_vjp = jax.vjp(_ssm_disc, sp["a_re"], sp["a_im"], sp["log_step"], sp["b_re"], sp["b_im"])
    g["a_re"], g["a_im"], g["log_step"], g["b_re"], g["b_im"] = disc_vjp((dlb_re, dlb_im, dbb_re, dbb_im))
    dq, dkv, g["sink"] = _attn_bwd(qkv, attn, dattn, lse, sp["sink"])
    dproj = _rope_bwd(dq, dkv, _from_segments(du_seg), dpre, d_skip, tabs)
    g["w_in"] = _mm_tn(dproj, h, IN_WIDTH // 5, D_MODEL, "mm_in_dw")
    grad_x, _, g["norm_mix_g"] = _mm_nn_rms_bwd(dproj, wb["w_in"], x, sp["norm_mix_g"], dx1, "mm_in_dx")
    return loss, grad_x, g


MESH = pl.DeviceIdType.MESH
ANY = pl.BlockSpec(memory_space=pl.ANY)


def _place():
    x, y, c = lax.axis_index("x"), lax.axis_index("y"), lax.axis_index("c")
    chips = [(1 - x, y), (x, 1 - y), (1 - x, 1 - y)]
    return x, y, c, chips


def _chip_index(px, py):
    return 2 * px + py


CHUNK_BYTES = 256 * 1024
MAX_CHUNKS = 16


def _row_chunks(rows, row_bytes, align):
    n = max(1, min(MAX_CHUNKS, (rows * row_bytes) // CHUNK_BYTES))
    per = -(-rows // n)
    per = -(-per // align) * align
    return [(r0, min(per, rows - r0)) for r0 in range(0, rows, per)]


def _align_of(dtype):
    return SUBLANES * 4 // jnp.dtype(dtype).itemsize


def _remote(src, dst, send_sem, recv_sem, to):
    return pltpu.make_async_remote_copy(src_ref=src, dst_ref=dst, send_sem=send_sem, recv_sem=recv_sem,
                                        device_id=to, device_id_type=MESH)


CAST_ROWS = 64


def _gather_weights(shards, dtypes):
    nw = len(shards)

    def body(*refs):
        w_refs, o_refs = refs[:nw], refs[nw:2 * nw]
        send_sems, recv_sems, in_sems, out_sems = refs[2 * nw:2 * nw + 4]
        raw, cast = refs[2 * nw + 4:3 * nw + 4], refs[3 * nw + 4:]
        x, y, c, chips = _place()
        mine = _chip_index(x, y)
        sibling = (x, y, 1 - c)

        def rows_of(ref, chip, r0, nr):
            return ref.at[chip, pl.ds(r0, nr), :]

        def copy(wi, k, src, dst, to):
            return _remote(src, dst, send_sems.at[wi, k], recv_sems.at[wi, k], to)

        geo = []
        for wi in range(nw):
            rows, cols = w_refs[wi].shape
            row_bytes = cols * jnp.dtype(dtypes[wi]).itemsize
            geo.append((rows // 2, _row_chunks(rows // 2, row_bytes, _align_of(dtypes[wi]))))

        stage_in = [pltpu.make_async_copy(w_refs[wi], raw[wi], in_sems.at[wi]) for wi in range(nw)]
        for cp in stage_in:
            cp.start()
        staged = [raw[wi] if dtypes[wi] == w_refs[wi].dtype else cast[wi] for wi in range(nw)]
        stage_out = []
        for wi in range(nw):
            stage_in[wi].wait()
            if staged[wi] is not raw[wi]:
                def cast_rows(i, _, wi=wi):
                    rows = pl.ds(pl.multiple_of(i * CAST_ROWS, CAST_ROWS), CAST_ROWS)
                    cast[wi][rows, :] = raw[wi][rows, :].astype(dtypes[wi])
                    return 0

                lax.fori_loop(0, w_refs[wi].shape[0] // CAST_ROWS, cast_rows, 0)
            cp = pltpu.make_async_copy(staged[wi], o_refs[wi].at[mine], out_sems.at[wi])
            cp.start()
            stage_out.append(cp)

        for wi in range(nw):
            hr, half_chunks = geo[wi]
            for j, chip in enumerate(chips):
                for r0, nr in half_chunks:
                    copy(wi, j, staged[wi].at[pl.ds(c * hr + r0, nr), :],
                         rows_of(o_refs[wi], mine, c * hr + r0, nr), (*chip, c)).start()
        for wi in range(nw):
            hr, half_chunks = geo[wi]
            for j, chip in enumerate(chips):
                got = rows_of(o_refs[wi], _chip_index(*chip), c * hr, hr)
                copy(wi, j, got, got, (*chip, c)).wait_recv()
                for r0, nr in half_chunks:
                    piece = rows_of(o_refs[wi], _chip_index(*chip), c * hr + r0, nr)
                    copy(wi, 3 + j, piece, piece, sibling).start()
        for wi in range(nw):
            hr = geo[wi][0]
            for j, chip in enumerate(chips):
                got = rows_of(o_refs[wi], _chip_index(*chip), (1 - c) * hr, hr)
                copy(wi, 3 + j, got, got, sibling).wait_recv()
        for wi in range(nw):
            hr = geo[wi][0]
            sent = rows_of(o_refs[wi], mine, c * hr, hr)
            for k in range(6):
                copy(wi, k, sent, sent, sibling).wait_send()
            stage_out[wi].wait()

    return pl.pallas_call(
        body, in_specs=[ANY] * nw, out_specs=[ANY] * nw,
        out_shape=[jax.ShapeDtypeStruct((4, *s.shape), t) for s, t in zip(shards, dtypes)],
        scratch_shapes=[pltpu.SemaphoreType.DMA((nw, 6)), pltpu.SemaphoreType.DMA((nw, 6)),
                        pltpu.SemaphoreType.DMA((nw,)), pltpu.SemaphoreType.DMA((nw,))]
        + [pltpu.VMEM(s.shape, s.dtype) for s in shards] + [pltpu.VMEM(s.shape, t) for s, t in zip(shards, dtypes)],
        name="gather_weights", compiler_params=_params(vmem_mb=40),
    )(*shards)


HBM = pl.BlockSpec(memory_space=pltpu.HBM)
SEM = pl.BlockSpec(memory_space=pltpu.SEMAPHORE)
EFFECT = pltpu.SideEffectType.DATAFLOW_SIDE_EFFECTING


def _cast_place(w, place, dtype, after, name):
    rows, cols = w.shape
    tr = _row_tile(rows, cols, _align_of(dtype))

    def body(p_ref, w_ref, after_ref, o_ref):
        del p_ref, after_ref
        o_ref[...] = w_ref[...].astype(dtype)

    grid_spec = pltpu.PrefetchScalarGridSpec(
        num_scalar_prefetch=1, grid=(rows // tr,),
        in_specs=[pl.BlockSpec((tr, cols), lambda i, p: (i, 0)), ANY],
        out_specs=pl.BlockSpec((None, tr, cols), lambda i, p: (p[1], i, 0)))
    return pl.pallas_call(body, grid_spec=grid_spec, out_shape=jax.ShapeDtypeStruct((4, rows, cols), dtype),
                          name=name, compiler_params=_params(("parallel",)))(place, w, after)


def _split_start(name, arrays, n_pairs, issue):
    n = len(arrays)

    def body(*refs):
        issue(refs[:n], refs[n:n + n_pairs], refs[n + n_pairs:n + 2 * n_pairs])
        token = refs[2 * n + 2 * n_pairs]
        token[...] = jnp.zeros_like(token)

    dma = pltpu.SemaphoreType.DMA(())
    outs = pl.pallas_call(
        body, name=name,
        out_shape=[dma] * (2 * n_pairs) + [pltpu.HBM(t.shape, t.dtype) for t in arrays]
        + [jax.ShapeDtypeStruct((SUBLANES, LANES), F32)],
        in_specs=[HBM] * n, out_specs=[SEM] * (2 * n_pairs) + [HBM] * n + [pl.BlockSpec(memory_space=pltpu.VMEM)],
        input_output_aliases={a: 2 * n_pairs + a for a in range(n)},
        compiler_params=pltpu.CompilerParams(has_side_effects=EFFECT),
    )(*[pltpu.with_memory_space_constraint(t, pltpu.HBM) for t in arrays])
    return outs[:n_pairs], outs[n_pairs:2 * n_pairs], outs[2 * n_pairs:2 * n_pairs + n], outs[-1]


def _split_wait(name, send_sems, recv_sems, flying, sizes, after):
    n, n_pairs = len(flying), len(send_sems)

    def body(*refs):
        x, y, c, _ = _place()
        for k, ref in enumerate(sizes(refs[:n])):
            cp = _remote(ref, ref, refs[n + k], refs[n + n_pairs + k], (x, y, 1 - c))
            cp.wait_send()
            cp.wait_recv()

    return pl.pallas_call(
        body, name=name, out_shape=[pltpu.HBM(t.shape, t.dtype) for t in flying],
        in_specs=[HBM] * n + [SEM] * (2 * n_pairs) + [ANY], out_specs=[HBM] * n,
        input_output_aliases={a: a for a in range(n)},
        compiler_params=pltpu.CompilerParams(has_side_effects=EFFECT),
    )(*flying, *send_sems, *recv_sems, after)


def _spread_start(lands):
    def issue(land_refs, send_sems, recv_sems):
        x, y, c, chips = _place()
        mine = _chip_index(x, y)
        for a, land in enumerate(land_refs):
            _, rows, cols = land.shape
            hr = rows // 2
            row_bytes = cols * jnp.dtype(land.dtype).itemsize
            for r0, nr in _row_chunks(hr, row_bytes, _align_of(land.dtype)):
                piece = land.at[mine, pl.ds(c * hr + r0, nr), :]
                for chip in chips:
                    for core in (0, 1):
                        _remote(piece, piece, send_sems[a], recv_sems[a], (*chip, core)).start()

    return _split_start("spread_start", lands, len(lands), issue)


def _spread_wait(send_sems, recv_sems, flying, after):
    return _split_wait("spread_wait", send_sems, recv_sems, flying,
                       lambda refs: [r.at[pl.ds(0, 3)] for r in refs], after)


def _pair_start(grads):
    n = len(grads)
    zones = [lax.empty((4, g.shape[1] // 2, g.shape[2]), F32) for g in grads]

    def issue(refs, send_sems, recv_sems):
        x, y, c, _ = _place()
        for a in range(n):
            g_ref, z_ref = refs[a], refs[n + a]
            _, rows, cols = g_ref.shape
            hr = rows // 2
            for k in range(4):
                for r0, nr in _row_chunks(hr, cols * 4, SUBLANES):
                    _remote(g_ref.at[k, pl.ds((1 - c) * hr + r0, nr), :], z_ref.at[k, pl.ds(r0, nr), :],
                            send_sems[a], recv_sems[a], (x, y, 1 - c)).start()

    return _split_start("pair_start", list(grads) + zones, n, issue)


def _pair_wait(send_sems, recv_sems, flying, after):
    n = len(flying) // 2
    out = _split_wait("pair_wait", send_sems, recv_sems, flying, lambda refs: list(refs[n:]), after)
    return out[:n], out[n:]


def _chip_start(sums):
    n = len(sums)
    zones = [lax.empty((3, *s.shape[1:]), s.dtype) for s in sums]

    def issue(refs, send_sems, recv_sems):
        x, y, c, chips = _place()
        for a in range(n):
            s_ref, z_ref = refs[a], refs[n + a]
            _, rows, cols = s_ref.shape
            row_bytes = cols * jnp.dtype(s_ref.dtype).itemsize
            for r0, nr in _row_chunks(rows, row_bytes, _align_of(s_ref.dtype)):
                for j, chip in enumerate(chips):
                    _remote(s_ref.at[_chip_index(*chip), pl.ds(r0, nr), :], z_ref.at[j, pl.ds(r0, nr), :],
                            send_sems[a], recv_sems[a], (*chip, c)).start()

    return _split_start("chip_start", list(sums) + zones, n, issue)


def _chip_wait(send_sems, recv_sems, flying, after):
    n = len(flying) // 2
    return _split_wait("chip_wait", send_sems, recv_sems, flying, lambda refs: list(refs[n:]), after)[n:]


def _pair_exchange(grads):
    na = len(grads)

    def body(*refs):
        g_refs, o_refs = refs[:na], refs[na:2 * na]
        send_sems, recv_sems = refs[2 * na:]
        x, y, c, _ = _place()
        sibling = (x, y, 1 - c)
        for ai in range(na):
            _, rows, cols = g_refs[ai].shape
            hr = rows // 2
            for k in range(4):
                for r0, nr in _row_chunks(hr, cols * 4, SUBLANES):
                    _remote(g_refs[ai].at[k, pl.ds((1 - c) * hr + r0, nr), :], o_refs[ai].at[k, pl.ds(r0, nr), :],
                            send_sems.at[ai], recv_sems.at[ai], sibling).start()
        for ai in range(na):
            _remote(o_refs[ai], o_refs[ai], send_sems.at[ai], recv_sems.at[ai], sibling).wait()

    return pl.pallas_call(
        body, in_specs=[ANY] * na, out_specs=[ANY] * na,
        out_shape=[jax.ShapeDtypeStruct((4, g.shape[1] // 2, g.shape[2]), F32) for g in grads],
        scratch_shapes=[pltpu.SemaphoreType.DMA((na,)), pltpu.SemaphoreType.DMA((na,))],
        name="pair_exchange",
    )(*grads)


def _row_tile(rows, cols, align):
    best = align
    for cand in range(align, rows + 1, align):
        if rows % cand == 0 and cand * cols <= 256 * 1024:
            best = cand
    return best


def _pair_sum(g, got, place, transit, name):
    _, rows, cols = g.shape
    hr = rows // 2
    tr = _row_tile(hr, cols, _align_of(transit))
    nt = hr // tr

    def body(p_ref, g_ref, r_ref, s_ref, own_ref):
        total = g_ref[...] + r_ref[...]
        s_ref[...] = total.astype(transit)

        @pl.when(pl.program_id(1) == p_ref[1])
        def _():
            own_ref[...] = total

    grid_spec = pltpu.PrefetchScalarGridSpec(
        num_scalar_prefetch=1, grid=(nt, 4),
        in_specs=[pl.BlockSpec((None, tr, cols), lambda i, k, p: (k, p[0] * nt + i, 0)),
                  pl.BlockSpec((None, tr, cols), lambda i, k, p: (k, i, 0))],
        out_specs=[pl.BlockSpec((None, tr, cols), lambda i, k, p: (k, i, 0)),
                   pl.BlockSpec((tr, cols), lambda i, k, p: (i, 0))])
    return pl.pallas_call(
        body, grid_spec=grid_spec,
        out_shape=[jax.ShapeDtypeStruct((4, hr, cols), transit), jax.ShapeDtypeStruct((hr, cols), F32)],
        name=name, compiler_params=_params(("parallel", "arbitrary")),
    )(place, g, got)


def _chip_exchange(sums):
    na = len(sums)

    def body(*refs):
        s_refs, o_refs = refs[:na], refs[na:2 * na]
        send_sems, recv_sems = refs[2 * na:]
        x, y, c, chips = _place()
        for ai in range(na):
            _, rows, cols = s_refs[ai].shape
            row_bytes = cols * jnp.dtype(s_refs[ai].dtype).itemsize
            for r0, nr in _row_chunks(rows, row_bytes, _align_of(s_refs[ai].dtype)):
                for j, chip in enumerate(chips):
                    _remote(s_refs[ai].at[_chip_index(*chip), pl.ds(r0, nr), :], o_refs[ai].at[j, pl.ds(r0, nr), :],
                            send_sems.at[ai, j], recv_sems.at[ai, j], (*chip, c)).start()
        for ai in range(na):
            for j, chip in enumerate(chips):
                _remote(o_refs[ai].at[j], o_refs[ai].at[j], send_sems.at[ai, j], recv_sems.at[ai, j],
                        (*chip, c)).wait()

    return pl.pallas_call(
        body, in_specs=[ANY] * na, out_specs=[ANY] * na,
        out_shape=[jax.ShapeDtypeStruct((3, *s.shape[1:]), s.dtype) for s in sums],
        scratch_shapes=[pltpu.SemaphoreType.DMA((na, 3)), pltpu.SemaphoreType.DMA((na, 3))],
        name="chip_exchange",
    )(*sums)


def _chip_sum(own, landed, name):
    hr, cols = own.shape
    tr = _row_tile(hr, cols, _align_of(landed.dtype))

    def body(o_ref, l_ref, f_ref):
        acc = o_ref[...]
        for j in range(3):
            acc = acc + l_ref[j].astype(F32)
        f_ref[...] = acc

    return pl.pallas_call(
        body, grid=(hr // tr,),
        in_specs=[pl.BlockSpec((tr, cols), lambda i: (i, 0)), pl.BlockSpec((3, tr, cols), lambda i: (0, i, 0))],
        out_specs=pl.BlockSpec((tr, cols), lambda i: (i, 0)),
        out_shape=jax.ShapeDtypeStruct((hr, cols), F32), name=name,
        compiler_params=_params(("parallel",)),
    )(own, landed)


def _final_exchange(halves, small):
    nh = len(halves)

    def body(*refs):
        h_refs, s_ref = refs[:nh], refs[nh]
        o_refs, so_ref = refs[nh + 1:2 * nh + 1], refs[2 * nh + 1]
        send_sems, recv_sems, local_sem, ssend_sems, srecv_sems = refs[2 * nh + 2:]
        x, y, c, _ = _place()
        me = 4 * x + 2 * y + c
        sibling = (x, y, 1 - c)
        for hi in range(nh):
            hr, cols = h_refs[hi].shape
            for r0, nr in _row_chunks(hr, cols * 4, SUBLANES):
                _remote(h_refs[hi].at[pl.ds(r0, nr), :], o_refs[hi].at[pl.ds(r0, nr), :],
                        send_sems.at[hi], recv_sems.at[hi], sibling).start()
        small_cps = [pltpu.make_async_copy(s_ref, so_ref.at[me], local_sem)]
        for r in range(1, 8):
            fx, fy, fc = (r >> 2) & 1, (r >> 1) & 1, r & 1
            peer = (1 - x if fx else x, 1 - y if fy else y, 1 - c if fc else c)
            small_cps.append(_remote(s_ref, so_ref.at[me], ssend_sems.at[r - 1], srecv_sems.at[r - 1], peer))
        for cp in small_cps:
            cp.start()
        for hi in range(nh):
            _remote(h_refs[hi], o_refs[hi], send_sems.at[hi], recv_sems.at[hi], sibling).wait()
        for cp in small_cps:
            cp.wait()

    return pl.pallas_call(
        body, in_specs=[ANY] * (nh + 1), out_specs=[ANY] * (nh + 1),
        out_shape=[jax.ShapeDtypeStruct(h.shape, F32) for h in halves]
        + [jax.ShapeDtypeStruct((8, *small.shape), F32)],
        scratch_shapes=[pltpu.SemaphoreType.DMA((nh,)), pltpu.SemaphoreType.DMA((nh,)),
                        pltpu.SemaphoreType.DMA, pltpu.SemaphoreType.DMA((7,)), pltpu.SemaphoreType.DMA((7,))],
        name="final_exchange",
    )(*halves, small)


def _adamw(w, g, m, v, name):
    shape = w.shape
    n = w.size
    if w.ndim >= 2 and shape[-1] >= LANES:
        two_d = (n // shape[-1], shape[-1])
    elif n % LANES == 0:
        two_d = (n // LANES, LANES)
    else:
        two_d = (1, n)
    r, c = two_d
    tr = r
    for cand in (512, 256, 176, 128, 64):
        if r > cand and r % cand == 0 and cand * c <= 256 * 1024:
            tr = cand
            break
    c1 = 1.0 - ADAM_B1 ** ADAM_STEP
    c2 = 1.0 - ADAM_B2 ** ADAM_STEP

    def body(w_ref, g_ref, m_ref, v_ref, d_ref, nm_ref, nv_ref):
        gv = g_ref[...]
        nm = ADAM_B1 * m_ref[...] + (1.0 - ADAM_B1) * gv
        nv = ADAM_B2 * v_ref[...] + (1.0 - ADAM_B2) * (gv * gv)
        d_ref[...] = -ADAM_LR * ((nm / c1) / (jnp.sqrt(nv / c2) + ADAM_EPS) + ADAM_WD * w_ref[...])
        nm_ref[...] = nm
        nv_ref[...] = nv

    spec = pl.BlockSpec((tr, c), lambda i: (i, 0))
    out = jax.ShapeDtypeStruct((r, c), F32)
    d, nm, nv = pl.pallas_call(
        body, grid=(r // tr,), in_specs=[spec] * 4, out_specs=[spec] * 3, out_shape=[out] * 3, name=name,
        compiler_params=_params(("parallel",)),
    )(w.reshape(two_d), g.reshape(two_d), m.reshape(two_d), v.reshape(two_d))
    return d.reshape(shape), nm.reshape(shape), nv.reshape(shape)


def _adamw_many(ws, gs, ms, vs, name):
    n = len(ws)
    c1 = 1.0 - ADAM_B1 ** ADAM_STEP
    c2 = 1.0 - ADAM_B2 ** ADAM_STEP

    def body(*refs):
        w_refs, g_refs, m_refs, v_refs = (refs[k * n:(k + 1) * n] for k in range(4))
        d_refs, nm_refs, nv_refs = (refs[(4 + k) * n:(5 + k) * n] for k in range(3))
        for i in range(n):
            gv = g_refs[i][...]
            nm = ADAM_B1 * m_refs[i][...] + (1.0 - ADAM_B1) * gv
            nv = ADAM_B2 * v_refs[i][...] + (1.0 - ADAM_B2) * (gv * gv)
            d_refs[i][...] = -ADAM_LR * ((nm / c1) / (jnp.sqrt(nv / c2) + ADAM_EPS) + ADAM_WD * w_refs[i][...])
            nm_refs[i][...] = nm
            nv_refs[i][...] = nv

    vmem = pl.BlockSpec(memory_space=pltpu.VMEM)
    shapes = [jax.ShapeDtypeStruct(t.shape, F32) for t in ws]
    outs = pl.pallas_call(body, in_specs=[vmem] * (4 * n), out_specs=[vmem] * (3 * n), out_shape=shapes * 3,
                          name=name, compiler_params=_params(vmem_mb=56))(*ws, *gs, *ms, *vs)
    return outs[:n], outs[n:2 * n], outs[2 * n:]


BIG = ("w_in", "w_glu", "w_out", "w_up", "w_down")
WEIGHTS = ("norm_mix_g", "w_in", "a_re", "a_im", "log_step", "b_re", "b_im", "c_re", "c_im", "d_skip", "w_glu",
           "sink", "norm_attn_g", "norm_ssm_g", "w_out", "norm_ffn_g", "w_up", "conv_w", "conv_b", "w_down",
           "norm_final_g")
SMALL = ("norm_mix_g", "a_re", "a_im", "log_step", "b_re", "b_im", "c_re", "c_im", "d_skip", "sink",
         "norm_attn_g", "norm_ssm_g", "norm_ffn_g", "conv_w", "conv_b", "norm_final_g")
SMALL_ROWS = 40
N_DEV = 8


def _by_owner(name, g):
    if name == "w_up":
        return g
    return g.reshape(4, g.shape[0] // 4, g.shape[1])


def _view(name, t):
    if name == "w_in":
        return jnp.swapaxes(t[0], 0, 1)
    if name in ("b_re", "b_im"):
        return jnp.swapaxes(t, -1, -2)
    return t


def _unview(name, t):
    if name == "w_in":
        return jnp.swapaxes(t, 0, 1)[None]
    if name in ("b_re", "b_im"):
        return jnp.swapaxes(t, -1, -2)
    return t


def kernel(x, norm_mix_g, w_in, a_re, a_im, log_step, b_re, b_im, c_re, c_im, d_skip, w_glu, sink, norm_attn_g, norm_ssm_g, w_out, norm_ffn_g, w_up, conv_w, conv_b, w_down, norm_final_g, loss_target, m_norm_mix_g, m_w_in, m_a_re, m_a_im, m_log_step, m_b_re, m_b_im, m_c_re, m_c_im, m_d_skip, m_w_glu, m_sink, m_norm_attn_g, m_norm_ssm_g, m_w_out, m_norm_ffn_g, m_w_up, m_conv_w, m_conv_b, m_w_down, m_norm_final_g, v_norm_mix_g, v_w_in, v_a_re, v_a_im, v_log_step, v_b_re, v_b_im, v_c_re, v_c_im, v_d_skip, v_w_glu, v_sink, v_norm_attn_g, v_norm_ssm_g, v_w_out, v_norm_ffn_g, v_w_up, v_conv_w, v_conv_b, v_w_down, v_norm_final_g):
    given = dict(locals())
    w = {n: given[n] for n in WEIGHTS}
    m = {n: given["m_" + n] for n in WEIGHTS}
    v = {n: given["v_" + n] for n in WEIGHTS}
    xy = 2 * lax.axis_index("x") + lax.axis_index("y")

    core = lax.axis_index("c")
    place = jnp.stack([core, xy]).astype(jnp.int32)

    conv_rows = jnp.pad(w["conv_w"][0], ((0, 2 * SUBLANES - 3), (0, 0)))
    early = ("w_in", "w_glu", "w_out")
    shard = lambda n: _view(n, w[n]) if n == "w_in" else w[n][0]
    *gathered, conv_all = _gather_weights([shard(n) for n in early] + [conv_rows], [BF16] * len(early) + [F32])
    late = ("w_up", "w_down")
    send_sems, recv_sems, flying, token = _spread_start(
        [_cast_place(w[n][0], place, BF16, conv_all, "cast_" + n) for n in late])
    rows = lambda t: t.reshape(4 * t.shape[1], t.shape[2])
    wb = {n: rows(t) for n, t in zip(early, gathered)}

    def late_weights(after):
        w_up4, w_down4 = _spread_wait(send_sems, recv_sems, flying, after)
        return {"w_up": w_up4, "w_down": rows(w_down4)}

    sp = {n: w[n][0] for n in ("a_re", "a_im", "log_step", "b_re", "b_im", "c_re", "c_im", "d_skip",
                               "norm_mix_g", "norm_attn_g", "norm_ssm_g", "norm_ffn_g", "sink", "conv_b")}
    for n in ("norm_mix_g", "norm_attn_g", "norm_ssm_g", "norm_ffn_g", "sink", "conv_b"):
        sp[n] = sp[n].reshape(1, -1)
    sp["norm_mix_g"] = sp["norm_mix_g"] + token[:1, :1]
    sp["conv_w"] = conv_all[:, :3].transpose(1, 0, 2).reshape(3, 2 * D_FF)
    sp["norm_final_g"] = w["norm_final_g"]
    flight = {}

    def ffn_grads_ready(dw_up, dw_down):
        *flight["pair"], token = _pair_start([dw_up, _by_owner("w_down", dw_down)])
        return token[:1, :1]

    def ffn_grads_next(after):
        mine, got = _pair_wait(*flight["pair"], after)
        sums, flight["own"] = zip(*[_pair_sum(a, b, place, BF16, "pair_sum_" + n) for n, a, b in zip(late, mine, got)])
        *flight["chip"], token = _chip_start(list(sums))
        return token[:1, :1]

    loss, grad_x, g = _local_step(x[0], loss_target[0], wb, sp, late_weights, ffn_grads_ready, ffn_grads_next)

    flat = jnp.concatenate([g[n].reshape(-1) for n in SMALL] + [loss.reshape(-1)])
    pad = N_DEV * SMALL_ROWS * D_MODEL - flat.shape[0]
    small = jnp.concatenate([flat, jnp.zeros((pad,), F32)]).reshape(4, 2 * SMALL_ROWS, D_MODEL)
    by_owner = [_by_owner(n, g[n]) for n in early] + [small]
    got = _pair_exchange(by_owner)
    transit = [BF16] * len(early) + [F32]
    chip_sums, own_sums = zip(*[_pair_sum(a, b, place, t, "pair_sum_" + n)
                                for n, a, b, t in zip(early + ("small",), by_owner, got, transit)])
    landed = _chip_exchange(list(chip_sums))
    halves = {n: _chip_sum(o, t, "chip_sum_" + n) for n, o, t in zip(early + ("small",), own_sums, landed)}
    late_landed = _chip_wait(*flight["chip"], grad_x)
    for n, o, t in zip(late, flight["own"], late_landed):
        halves[n] = _chip_sum(o, t, "chip_sum_" + n)
    *others, small_all = _final_exchange([halves[n] for n in BIG], halves["small"])
    grads = {n: jnp.concatenate([jnp.where(core == 0, halves[n], o), jnp.where(core == 0, o, halves[n])], axis=0)
             for n, o in zip(BIG, others)}
    flat = small_all.reshape(-1)
    off = 0
    for n in SMALL:
        shape = (3, 4 * w[n].shape[-1]) if n == "conv_w" else w[n].shape[1:] if n != "norm_final_g" else w[n].shape
        size = math.prod(shape)
        grads[n] = flat[off:off + size].reshape(shape)
        off += size
    loss = flat[off]
    cw = w["conv_w"].shape[-1]
    grads["conv_w"] = lax.dynamic_slice_in_dim(grads["conv_w"], xy * cw, cw, axis=1)
    grads = {n: grads[n] if n == "w_in" else _view(n, grads[n].reshape(w[n].shape)) for n in WEIGHTS}
    wv, mv, vv = ({n: _view(n, t[n]) for n in WEIGHTS} for t in (w, m, v))

    delta, new_m, new_v = {}, {}, {}
    for n in BIG:
        delta[n], new_m[n], new_v[n] = _adamw(wv[n], grads[n].reshape(wv[n].shape), mv[n], vv[n], "adamw_" + n)
    for group, name in ((("b_re", "b_im"), "adamw_b"), (tuple(n for n in SMALL if n not in ("b_re", "b_im")), "adamw_small")):
        row = lambda t: t.reshape(1, -1) if t.ndim == 1 else t
        d_, m_, v_ = _adamw_many(*[[row(t[n]) for n in group] for t in (wv, grads, mv, vv)], name)
        for n, dn, mn, vn in zip(group, d_, m_, v_):
            delta[n], new_m[n], new_v[n] = (t.reshape(wv[n].shape) for t in (dn, mn, vn))
    natural = lambda t: [_unview(n, t[n].reshape(wv[n].shape)) for n in WEIGHTS]
    return (loss, grad_x[None], *natural(grads), *natural(delta), *natural(new_m), *natural(new_v))
```

```python
import functools
import math

import jax
import jax.numpy as jnp
from jax import lax
from jax.experimental import pallas as pl
from jax.experimental.pallas import tpu as pltpu

F32 = jnp.float32
BF16 = jnp.bfloat16

D_MODEL = 1024
N_Q_HEADS = 8
N_KV_HEADS = 2
HEAD_DIM = 64
ATTN_WIDTH = 512
KV_WIDTH = 128
QKV_WIDTH = ATTN_WIDTH + 2 * KV_WIDTH
WINDOW = 128
BLOCK = 128
ROPE_DIM = 16
ROPE_THETA = 500000.0
SSM_WIDTH = 512
SSM_GROUP = 16
N_SSM_GROUPS = 32
SSM_STATE = 64
IN_WIDTH = 1280
D_FF = 2816
EPS = 1e-6
ADAM_LR = 0.001
ADAM_B1 = 0.9
ADAM_B2 = 0.999
ADAM_EPS = 1e-08
ADAM_WD = 0.01
ADAM_STEP = 10

VMEM_BYTES_V7X = 64 * 1024 * 1024
SUBLANES = 8
LANES = 128
SSM_CB = 4
SSM_CH = 128
SSM_ST = 512
N_SEG = SUBLANES

NN = (((1,), (0,)), ((), ()))
NT = (((1,), (1,)), ((), ()))
TN = (((0,), (0,)), ((), ()))


def _params(sem=None, vmem_mb=48):
    limit = vmem_mb * 1024 * 1024
    assert limit < VMEM_BYTES_V7X
    return pltpu.CompilerParams(dimension_semantics=sem, vmem_limit_bytes=limit)


def _dg(a, b, dims):
    return lax.dot_general(a, b, dims, preferred_element_type=F32)


def _sigmoid(x):
    return 1.0 / (1.0 + jnp.exp(-x))


_SQRT_HALF = 0.7071067811865476
_INV_SQRT_2PI = 0.3989422804014327


def _gelu(x):
    return 0.5 * x * (1.0 + lax.erf(x * _SQRT_HALF))


def _gelu_grad(x):
    return 0.5 * (1.0 + lax.erf(x * _SQRT_HALF)) + x * (_INV_SQRT_2PI * jnp.exp(-0.5 * x * x))


def _mm_nt(a, b, tm, tn, out_dtype, name):
    m, k = a.shape
    n = b.shape[0]

    def body(a_ref, b_ref, o_ref):
        o_ref[...] = _dg(a_ref[...], b_ref[...], NT).astype(out_dtype)

    return pl.pallas_call(
        body, grid=(m // tm, n // tn),
        in_specs=[pl.BlockSpec((tm, k), lambda i, j: (i, 0)), pl.BlockSpec((tn, k), lambda i, j: (j, 0))],
        out_specs=pl.BlockSpec((tm, tn), lambda i, j: (i, j)),
        out_shape=jax.ShapeDtypeStruct((m, n), out_dtype), name=name,
        compiler_params=_params(("parallel", "parallel")),
    )(a, b)


def _mm_tn(a, b, tm, tn, name):
    k, m = a.shape
    n = b.shape[1]

    def body(a_ref, b_ref, o_ref):
        o_ref[...] = _dg(a_ref[...], b_ref[...], TN)

    return pl.pallas_call(
        body, grid=(m // tm, n // tn),
        in_specs=[pl.BlockSpec((k, tm), lambda i, j: (0, i)), pl.BlockSpec((k, tn), lambda i, j: (0, j))],
        out_specs=pl.BlockSpec((tm, tn), lambda i, j: (i, j)),
        out_shape=jax.ShapeDtypeStruct((m, n), F32), name=name,
        compiler_params=_params(("parallel", "parallel")),
    )(a, b)


def _mm_nn_cols(a, b4, tm, name):
    m, k = a.shape
    s, _, n = b4.shape

    def body(a_ref, b_ref, o_ref):
        o_ref[...] = _dg(a_ref[...], b_ref[...], NN)

    return pl.pallas_call(
        body, grid=(m // tm, s),
        in_specs=[pl.BlockSpec((tm, k), lambda i, j: (i, 0)), pl.BlockSpec((None, k, n), lambda i, j: (j, 0, 0))],
        out_specs=pl.BlockSpec((tm, n), lambda i, j: (i, j)),
        out_shape=jax.ShapeDtypeStruct((m, s * n), F32), name=name,
        compiler_params=_params(("parallel", "parallel")),
    )(a, b4)


def _mm_tn_cols(a, b2, s, tm, name):
    k, m = a.shape
    h, _, wide = b2.shape
    per = s // h
    n = wide // per

    def body(a_ref, b_ref, o_ref):
        o_ref[...] = _dg(a_ref[...], b_ref[...], TN)

    return pl.pallas_call(
        body, grid=(s, m // tm),
        in_specs=[pl.BlockSpec((k, tm), lambda j, i: (0, i)),
                  pl.BlockSpec((None, k, n), lambda j, i: (j // per, 0, j % per))],
        out_specs=pl.BlockSpec((None, tm, n), lambda j, i: (j, i, 0)),
        out_shape=jax.ShapeDtypeStruct((s, m, n), F32), name=name,
        compiler_params=_params(("parallel", "parallel")),
    )(a, b2)


TM_EW = 256


def _rms_bwd_vals(xv, gv, dy):
    r = lax.rsqrt(jnp.mean(xv * xv, axis=-1, keepdims=True) + EPS)
    xh = xv * r
    dxh = dy * gv
    dx = r * (dxh - xh * jnp.mean(dxh * xh, axis=-1, keepdims=True))
    return dx, dy * xh


TM_FUSED = 256


def _rms_vals(xv, gv):
    return xv * lax.rsqrt(jnp.mean(xv * xv, axis=-1, keepdims=True) + EPS) * gv


def _rms_mm_split(x, g, wt, split, name):
    l, d = x.shape
    n = wt.shape[0]

    def body(x_ref, g_ref, w_ref, h_ref, lo_ref, hi_ref):
        h = _rms_vals(x_ref[...], g_ref[...]).astype(BF16)
        h_ref[...] = h
        out = _dg(h, w_ref[...], NT)
        lo_ref[...] = out[:, :split]
        hi_ref[...] = out[:, split:]

    row = lambda width: pl.BlockSpec((TM_FUSED, width), lambda i: (i, 0))
    return pl.pallas_call(
        body, grid=(l // TM_FUSED,),
        in_specs=[row(d), pl.BlockSpec((1, d), lambda i: (0, 0)), pl.BlockSpec((n, d), lambda i: (0, 0))],
        out_specs=[row(d), row(split), row(n - split)],
        out_shape=[jax.ShapeDtypeStruct((l, d), BF16), jax.ShapeDtypeStruct((l, split), F32),
                   jax.ShapeDtypeStruct((l, n - split), F32)],
        name=name, compiler_params=_params(("parallel",)),
    )(x, g, wt)


def _mm_res_rms(a, b, res, g, name):
    l, k = a.shape
    d = b.shape[1]

    def body(a_ref, b_ref, r_ref, g_ref, x_ref, h_ref):
        xv = r_ref[...] + _dg(a_ref[...], b_ref[...], NN)
        x_ref[...] = xv
        h_ref[...] = _rms_vals(xv, g_ref[...]).astype(BF16)

    row = lambda width: pl.BlockSpec((TM_FUSED, width), lambda i: (i, 0))
    return pl.pallas_call(
        body, grid=(l // TM_FUSED,),
        in_specs=[row(k), pl.BlockSpec((k, d), lambda i: (0, 0)), row(d), pl.BlockSpec((1, d), lambda i: (0, 0))],
        out_specs=[row(d), row(d)],
        out_shape=[jax.ShapeDtypeStruct((l, d), F32), jax.ShapeDtypeStruct((l, d), BF16)],
        name=name, compiler_params=_params(("parallel",)),
    )(a, b, res, g)


def _mm_res_loss(a, b, res, g, target):
    l, k = a.shape
    d = b.shape[1]

    def body(a_ref, b_ref, r_ref, g_ref, t_ref, loss_ref, dx_ref, dxb_ref, dg_ref):
        xv = r_ref[...] + _dg(a_ref[...], b_ref[...], NN)
        gv = g_ref[...]
        r = lax.rsqrt(jnp.mean(xv * xv, axis=-1, keepdims=True) + EPS)
        xh = xv * r
        e = xh * gv - t_ref[...]
        part = jnp.sum(jnp.sum(e * e, axis=1, keepdims=True), axis=0, keepdims=True) * (0.5 / d)
        dy = e * (1.0 / d)
        dxh = dy * gv
        dx = r * (dxh - xh * jnp.mean(dxh * xh, axis=-1, keepdims=True))
        dx_ref[...] = dx
        dxb_ref[...] = dx.astype(BF16)

        @pl.when(pl.program_id(0) == 0)
        def _():
            dg_ref[...] = jnp.zeros_like(dg_ref)
            loss_ref[...] = jnp.zeros_like(loss_ref)

        dg_ref[...] += jnp.sum(dy * xh, axis=0, keepdims=True)
        loss_ref[...] += part

    row = lambda width: pl.BlockSpec((TM_FUSED, width), lambda i: (i, 0))
    vec = pl.BlockSpec((1, d), lambda i: (0, 0))
    return pl.pallas_call(
        body, grid=(l // TM_FUSED,),
        in_specs=[row(k), pl.BlockSpec((k, d), lambda i: (0, 0)), row(d), vec, row(d)],
        out_specs=[pl.BlockSpec((1, 1), lambda i: (0, 0)), row(d), row(d), vec],
        out_shape=[jax.ShapeDtypeStruct((1, 1), F32), jax.ShapeDtypeStruct((l, d), F32),
                   jax.ShapeDtypeStruct((l, d), BF16), jax.ShapeDtypeStruct((1, d), F32)],
        name="mm_down_loss", compiler_params=_params(("arbitrary",)),
    )(a, b, res, g, target)


def _mm_rms_bwd(a, b, a_spec, b_spec, matmul, x, g, res, name):
    l, d = x.shape

    def body(a_ref, b_ref, x_ref, g_ref, res_ref, dx_ref, dxb_ref, dg_ref):
        dx, dgr = _rms_bwd_vals(x_ref[...], g_ref[...], matmul(a_ref, b_ref))
        dx = dx + res_ref[...]
        dx_ref[...] = dx
        dxb_ref[...] = dx.astype(BF16)

        @pl.when(pl.program_id(0) == 0)
        def _():
            dg_ref[...] = jnp.zeros_like(dg_ref)

        dg_ref[...] += jnp.sum(dgr, axis=0, keepdims=True)

    row = pl.BlockSpec((TM_FUSED, d), lambda i: (i, 0))
    vec = pl.BlockSpec((1, d), lambda i: (0, 0))
    return pl.pallas_call(
        body, grid=(l // TM_FUSED,), in_specs=[a_spec, b_spec, row, vec, row], out_specs=[row, row, vec],
        out_shape=[jax.ShapeDtypeStruct((l, d), F32), jax.ShapeDtypeStruct((l, d), BF16),
                   jax.ShapeDtypeStruct((1, d), F32)],
        name=name, compiler_params=_params(("arbitrary",)),
    )(a, b, x, g, res)


def _mm_nn_rms_bwd(a, b, x, g, res, name):
    return _mm_rms_bwd(a, b, pl.BlockSpec((TM_FUSED, a.shape[1]), lambda i: (i, 0)),
                       pl.BlockSpec(b.shape, lambda i: (0, 0)),
                       lambda a_ref, b_ref: _dg(a_ref[...], b_ref[...], NN), x, g, res, name)


def _mm_cols_rms_bwd(a2, b4, x, g, res, name):
    h, _, wide = a2.shape
    s, _, n = b4.shape
    per = s // h

    def matmul(a_ref, b_ref):
        acc = None
        for j in range(s):
            part = _dg(a_ref[j // per, :, (j % per) * n:(j % per + 1) * n], b_ref[j], NT)
            acc = part if acc is None else acc + part
        return acc

    return _mm_rms_bwd(a2, b4, pl.BlockSpec((h, TM_FUSED, wide), lambda i: (0, i, 0)),
                       pl.BlockSpec(b4.shape, lambda i: (0, 0, 0)), matmul, x, g, res, name)


def _mix_fwd(attn, ys, g_attn, g_ssm):
    l, w = attn.shape

    def body(a_ref, y_ref, ga_ref, gs_ref, o_ref):
        for src, gr, off in ((a_ref, ga_ref, 0), (y_ref, gs_ref, w)):
            xv = src[...]
            r = lax.rsqrt(jnp.mean(xv * xv, axis=-1, keepdims=True) + EPS)
            o_ref[:, off:off + w] = (xv * r * gr[...]).astype(BF16)

    row = pl.BlockSpec((TM_EW, w), lambda i: (i, 0))
    vec = pl.BlockSpec((1, w), lambda i: (0, 0))
    return pl.pallas_call(
        body, grid=(l // TM_EW,), in_specs=[row, row, vec, vec],
        out_specs=pl.BlockSpec((TM_EW, 2 * w), lambda i: (i, 0)),
        out_shape=jax.ShapeDtypeStruct((l, 2 * w), BF16), name="mix_fwd",
        compiler_params=_params(("parallel",)),
    )(attn, ys, g_attn, g_ssm)


def _mix_bwd(attn, ys, g_attn, g_ssm, dmixed):
    l, w = attn.shape

    def body(a_ref, y_ref, ga_ref, gs_ref, dm_ref, da_ref, dy_ref, dga_ref, dgs_ref):
        @pl.when(pl.program_id(0) == 0)
        def _():
            dga_ref[...] = jnp.zeros_like(dga_ref)
            dgs_ref[...] = jnp.zeros_like(dgs_ref)

        for src, gr, off, dst, dgr in ((a_ref, ga_ref, 0, da_ref, dga_ref), (y_ref, gs_ref, w, dy_ref, dgs_ref)):
            dx, dg_rows = _rms_bwd_vals(src[...], gr[...], dm_ref[:, off:off + w])
            dst[...] = dx
            dgr[...] += jnp.sum(dg_rows, axis=0, keepdims=True)

    row = pl.BlockSpec((TM_EW, w), lambda i: (i, 0))
    vec = pl.BlockSpec((1, w), lambda i: (0, 0))
    return pl.pallas_call(
        body, grid=(l // TM_EW,),
        in_specs=[row, row, vec, vec, pl.BlockSpec((TM_EW, 2 * w), lambda i: (i, 0))],
        out_specs=[row, row, vec, vec],
        out_shape=[jax.ShapeDtypeStruct((l, w), F32), jax.ShapeDtypeStruct((l, w), F32),
                   jax.ShapeDtypeStruct((1, w), F32), jax.ShapeDtypeStruct((1, w), F32)],
        name="mix_bwd", compiler_params=_params(("arbitrary",)),
    )(attn, ys, g_attn, g_ssm, dmixed)


def _rope_tables(l):
    half = ROPE_DIM // 2
    inv_freq = jnp.power(ROPE_THETA, -jnp.arange(half, dtype=F32) / half)
    ang = jnp.arange(l, dtype=F32)[:, None] * inv_freq[None, :]
    cos, sin = jnp.cos(ang), jnp.sin(ang)
    ones = jnp.ones((l, HEAD_DIM - ROPE_DIM), F32)
    zeros = jnp.zeros((l, HEAD_DIM - ROPE_DIM), F32)
    zh = jnp.zeros((l, half), F32)
    c = jnp.concatenate([cos, cos, ones], axis=1)
    s_lo = jnp.concatenate([-sin, zh, zeros], axis=1)
    s_hi = jnp.concatenate([zh, sin, zeros], axis=1)
    return tuple(jnp.tile(t, (1, LANES // HEAD_DIM)) for t in (c, s_lo, s_hi))


def _rope_fwd(proj, tabs):
    l = proj.shape[0]
    nq = ATTN_WIDTH // LANES

    def body(p_ref, c_ref, lo_ref, hi_ref, o_ref):
        c, lo, hi = c_ref[...], lo_ref[...], hi_ref[...]
        for blk in range(nq + 1):
            t = p_ref[:, blk * LANES:(blk + 1) * LANES]
            rot = t * c + pltpu.roll(t, LANES - 8, 1) * lo + pltpu.roll(t, 8, 1) * hi
            o_ref[:, blk * LANES:(blk + 1) * LANES] = rot.astype(BF16)
        o_ref[:, (nq + 1) * LANES:] = p_ref[:, (nq + 1) * LANES:].astype(BF16)

    tab = pl.BlockSpec((TM_EW, LANES), lambda i: (i, 0))
    return pl.pallas_call(
        body, grid=(l // TM_EW,),
        in_specs=[pl.BlockSpec((TM_EW, QKV_WIDTH), lambda i: (i, 0)), tab, tab, tab],
        out_specs=pl.BlockSpec((TM_EW, QKV_WIDTH), lambda i: (i, 0)),
        out_shape=jax.ShapeDtypeStruct((l, QKV_WIDTH), BF16), name="rope_fwd",
        compiler_params=_params(("parallel",)),
    )(proj, *tabs)


def _rope_bwd(dq, dkv, du_ssm, dpre, d_skip, tabs):
    l = dq.shape[0]
    nq = ATTN_WIDTH // LANES

    def body(dq_ref, dkv_ref, du_ref, dpre_ref, ds_ref, c_ref, lo_ref, hi_ref, o_ref):
        c, lo, hi = c_ref[...], lo_ref[...], hi_ref[...]
        for blk in range(nq + 1):
            t = dq_ref[:, blk * LANES:(blk + 1) * LANES] if blk < nq else dkv_ref[:, :KV_WIDTH]
            g = t * c + pltpu.roll(t * lo, 8, 1) + pltpu.roll(t * hi, LANES - 8, 1)
            o_ref[:, blk * LANES:(blk + 1) * LANES] = g.astype(BF16)
        o_ref[:, (nq + 1) * LANES:QKV_WIDTH] = dkv_ref[:, KV_WIDTH:].astype(BF16)
        o_ref[:, QKV_WIDTH:] = (du_ref[...] + dpre_ref[...] * ds_ref[...]).astype(BF16)

    tab = pl.BlockSpec((TM_EW, LANES), lambda i: (i, 0))
    wide = pl.BlockSpec((TM_EW, SSM_WIDTH), lambda i: (i, 0))
    return pl.pallas_call(
        body, grid=(l // TM_EW,),
        in_specs=[wide, pl.BlockSpec((TM_EW, 2 * KV_WIDTH), lambda i: (i, 0)), wide, wide,
                  pl.BlockSpec((1, SSM_WIDTH), lambda i: (0, 0)), tab, tab, tab],
        out_specs=pl.BlockSpec((TM_EW, IN_WIDTH), lambda i: (i, 0)),
        out_shape=jax.ShapeDtypeStruct((l, IN_WIDTH), BF16), name="rope_bwd",
        compiler_params=_params(("parallel",)),
    )(dq, dkv, du_ssm, dpre, d_skip, *tabs)


_Q_COLS = ATTN_WIDTH // LANES
_SCALE = HEAD_DIM ** -0.5
_NEG = -1e30


def _window_specs(nb, width, col):
    return [
        pl.BlockSpec((BLOCK, width), lambda n: (jnp.maximum(n - 1, 0), col)),
        pl.BlockSpec((BLOCK, width), lambda n: (n, col)),
        pl.BlockSpec((BLOCK, width), lambda n: (jnp.minimum(n + 1, nb - 1), col)),
    ]


def _stacked_sink(sink_ref, heads):
    rid = lax.broadcasted_iota(jnp.int32, (len(heads) * BLOCK, 1), 0)
    sk = jnp.full(rid.shape, sink_ref[0, heads[-1]], F32)
    for g in range(len(heads) - 2, -1, -1):
        sk = jnp.where(rid < (g + 1) * BLOCK, sink_ref[0, heads[g]], sk)
    return sk


def _attn_fwd(qkv, sink):
    l = qkv.shape[0]
    nb = l // BLOCK
    grp = N_Q_HEADS // N_KV_HEADS

    def body(sink_ref, q_ref, k0, k1, k2, v0, v1, v2, o_ref, lse_ref):
        n = pl.program_id(0)
        q = q_ref[...]
        kw = jnp.concatenate([k0[...], k1[...], k2[...]], axis=0)
        vw = jnp.concatenate([v0[...], v1[...], v2[...]], axis=0)
        row = lax.broadcasted_iota(jnp.int32, (grp * BLOCK, 3 * BLOCK), 0)
        col = lax.broadcasted_iota(jnp.int32, (grp * BLOCK, 3 * BLOCK), 1)
        valid = jnp.abs(col - BLOCK - (row & (BLOCK - 1))) <= WINDOW
        valid &= jnp.logical_not((n == 0) & (col < BLOCK))
        valid &= jnp.logical_not((n == nb - 1) & (col >= 2 * BLOCK))
        for hk in range(N_KV_HEADS):
            heads = range(hk * grp, (hk + 1) * grp)
            qs = jnp.concatenate([q[:, h * HEAD_DIM:(h + 1) * HEAD_DIM] for h in heads], axis=0)
            kh = kw[:, hk * HEAD_DIM:(hk + 1) * HEAD_DIM]
            vh = vw[:, hk * HEAD_DIM:(hk + 1) * HEAD_DIM]
            s = jnp.where(valid, _dg(qs, kh, NT) * _SCALE, _NEG)
            sk = _stacked_sink(sink_ref, heads)
            m = jnp.maximum(jnp.max(s, axis=1, keepdims=True), sk)
            p = jnp.exp(s - m)
            denom = jnp.sum(p, axis=1, keepdims=True) + jnp.exp(sk - m)
            o = _dg((p / denom).astype(BF16), vh, NN)
            lse = m + jnp.log(denom)
            for g, h in enumerate(heads):
                o_ref[:, h * HEAD_DIM:(h + 1) * HEAD_DIM] = o[g * BLOCK:(g + 1) * BLOCK]
                lse_ref[:, h:h + 1] = lse[g * BLOCK:(g + 1) * BLOCK]

    return pl.pallas_call(
        body, grid=(nb,),
        in_specs=[pl.BlockSpec(memory_space=pltpu.SMEM),
                  pl.BlockSpec((BLOCK, ATTN_WIDTH), lambda n: (n, 0))]
        + _window_specs(nb, KV_WIDTH, _Q_COLS) + _window_specs(nb, KV_WIDTH, _Q_COLS + 1),
        out_specs=[pl.BlockSpec((BLOCK, ATTN_WIDTH), lambda n: (n, 0)),
                   pl.BlockSpec((BLOCK, N_Q_HEADS), lambda n: (n, 0))],
        out_shape=[jax.ShapeDtypeStruct((l, ATTN_WIDTH), F32), jax.ShapeDtypeStruct((l, N_Q_HEADS), F32)],
        name="attn_fwd", compiler_params=_params(("parallel",)),
    )(sink, qkv, qkv, qkv, qkv, qkv, qkv, qkv)


def _attn_bwd(qkv, attn, dattn, lse, sink):
    l = qkv.shape[0]
    nb = l // BLOCK
    grp = N_Q_HEADS // N_KV_HEADS
    win = 3 * BLOCK

    def body(sink_ref, q_ref, k0, k1, k2, v0, v1, v2, o_ref, d_ref, l_ref, dq_ref, dkv_ref, dsink_ref, ring_ref):
        n = pl.program_id(0)

        @pl.when(n == 0)
        def _():
            dsink_ref[...] = jnp.zeros_like(dsink_ref)
            ring_ref[...] = jnp.zeros_like(ring_ref)

        @pl.when(n < nb)
        def _():
            first, last = n == 0, n == nb - 1
            cat = lambda a, b, c: jnp.concatenate([a[...], b[...], c[...]], axis=0)
            q, kw, vw = q_ref[...], cat(k0, k1, k2), cat(v0, v1, v2)
            dov = d_ref[...]
            prod = o_ref[...] * dov
            dob = dov.astype(BF16)
            lse = l_ref[...]
            row = lax.broadcasted_iota(jnp.int32, (grp * BLOCK, win), 0)
            col = lax.broadcasted_iota(jnp.int32, (grp * BLOCK, win), 1)
            valid = jnp.abs(col - BLOCK - (row & (BLOCK - 1))) <= WINDOW
            valid &= jnp.logical_not(first & (col < BLOCK))
            valid &= jnp.logical_not(last & (col >= 2 * BLOCK))

            dsink_parts, dks, dvs = [], [], []
            for hk in range(N_KV_HEADS):
                heads = range(hk * grp, (hk + 1) * grp)
                ksl = slice(hk * HEAD_DIM, (hk + 1) * HEAD_DIM)
                hsl = [slice(h * HEAD_DIM, (h + 1) * HEAD_DIM) for h in heads]
                stack = lambda parts: jnp.concatenate(parts, axis=0)
                qs = stack([q[:, s_] for s_ in hsl])
                dos = stack([dob[:, s_] for s_ in hsl])
                deltas = stack([jnp.sum(prod[:, s_], axis=1, keepdims=True) for s_ in hsl])
                lses = stack([lse[:, h:h + 1] for h in heads])
                kh, vh = kw[:, ksl], vw[:, ksl]
                s = jnp.where(valid, _dg(qs, kh, NT) * _SCALE, _NEG)
                p = jnp.exp(s - lses)
                dp = _dg(dos, vh, NT)
                ds = (p * (dp - deltas) * _SCALE).astype(BF16)
                dq = _dg(ds, kh, NN)
                sink_rows = jnp.exp(_stacked_sink(sink_ref, heads) - lses) * deltas
                for g in range(grp):
                    dq_ref[:, hsl[g]] = dq[g * BLOCK:(g + 1) * BLOCK]
                    dsink_parts.append(jnp.sum(sink_rows[g * BLOCK:(g + 1) * BLOCK], axis=0, keepdims=True))
                dks.append(_dg(ds, qs, TN))
                dvs.append(_dg(p.astype(BF16), dos, TN))
            dsink_ref[...] -= jnp.concatenate(dsink_parts, axis=1)
            part = jnp.concatenate(dks + dvs, axis=1)
            ring_ref[(n + 2) % 3] += part[0:BLOCK]
            ring_ref[n % 3] += part[BLOCK:2 * BLOCK]
            ring_ref[(n + 1) % 3] = part[2 * BLOCK:]

        @pl.when(n >= 1)
        def _():
            dkv_ref[...] = ring_ref[(n + 2) % 3]

    centre = lambda n: jnp.minimum(n, nb - 1)
    window = lambda width, col: [
        pl.BlockSpec((BLOCK, width), lambda n: (jnp.maximum(centre(n) - 1, 0), col)),
        pl.BlockSpec((BLOCK, width), lambda n: (centre(n), col)),
        pl.BlockSpec((BLOCK, width), lambda n: (jnp.minimum(centre(n) + 1, nb - 1), col))]
    own = lambda width: pl.BlockSpec((BLOCK, width), lambda n: (centre(n), 0))
    return pl.pallas_call(
        body, grid=(nb + 1,),
        in_specs=[pl.BlockSpec(memory_space=pltpu.SMEM), own(ATTN_WIDTH)]
        + window(KV_WIDTH, _Q_COLS) + window(KV_WIDTH, _Q_COLS + 1)
        + [own(ATTN_WIDTH), own(ATTN_WIDTH), own(N_Q_HEADS)],
        out_specs=[own(ATTN_WIDTH), pl.BlockSpec((BLOCK, 2 * KV_WIDTH), lambda n: (jnp.maximum(n - 1, 0), 0)),
                   pl.BlockSpec((1, N_Q_HEADS), lambda n: (0, 0))],
        out_shape=[jax.ShapeDtypeStruct((l, ATTN_WIDTH), F32), jax.ShapeDtypeStruct((l, 2 * KV_WIDTH), F32),
                   jax.ShapeDtypeStruct((1, N_Q_HEADS), F32)],
        scratch_shapes=[pltpu.VMEM((3, BLOCK, 2 * KV_WIDTH), F32)],
        name="attn_bwd", compiler_params=_params(("arbitrary",)),
    )(sink, qkv, qkv, qkv, qkv, qkv, qkv, qkv, attn, dattn, lse)


def _ssm_disc(a_re, a_im, log_step, b_re, b_im):
    step = jnp.exp(log_step)[..., None]
    mag = jnp.exp(a_re * step)
    lb_re, lb_im = mag * jnp.cos(a_im * step), mag * jnp.sin(a_im * step)
    nr, ni = lb_re - 1.0, lb_im
    den = a_re * a_re + a_im * a_im
    f_re = ((nr * a_re + ni * a_im) / den)[..., None]
    f_im = ((ni * a_re - nr * a_im) / den)[..., None]
    return lb_re, lb_im, f_re * b_re - f_im * b_im, f_re * b_im + f_im * b_re


def _ssm_pack(lb_re, lb_im, bb_re, bb_im, c_re, c_im):
    eye = jnp.eye(SSM_CH // SSM_GROUP, dtype=F32)
    ng = SSM_CH // SSM_GROUP

    def diag_b(bb):
        t = bb.reshape(2, SSM_CB, ng, SSM_STATE, SSM_GROUP)
        return jnp.einsum('dkgpc,gh->dkgchp', t, eye).reshape(2, SSM_CB, SSM_CH, SSM_ST)

    def diag_c(cc):
        t = cc.reshape(2, SSM_CB, ng, SSM_GROUP, SSM_STATE)
        return jnp.einsum('dkgcp,gh->dkhpgc', t, eye).reshape(2, SSM_CB, SSM_ST, SSM_CH)

    bcat = jnp.concatenate([diag_b(bb_re), diag_b(bb_im)], axis=-1)
    ccat = jnp.concatenate([diag_c(c_re), -diag_c(c_im)], axis=-2)
    lam_re = lb_re.reshape(2, SSM_CB, 1, SSM_ST)
    lam_im = lb_im.reshape(2, SSM_CB, 1, SSM_ST)
    return bcat, ccat, lam_re, lam_im


def _ssm_unpack(dbcat, dccat, dlam_re, dlam_im):
    ng = SSM_CH // SSM_GROUP
    eye = jnp.eye(ng, dtype=F32)

    def undiag_b(t):
        t = t.reshape(2, SSM_CB, ng, SSM_GROUP, ng, SSM_STATE)
        return jnp.einsum('dkgchp,gh->dkgpc', t, eye).reshape(2, N_SSM_GROUPS, SSM_STATE, SSM_GROUP)

    def undiag_c(t):
        t = t.reshape(2, SSM_CB, ng, SSM_STATE, ng, SSM_GROUP)
        return jnp.einsum('dkhpgc,gh->dkgcp', t, eye).reshape(2, N_SSM_GROUPS, SSM_GROUP, SSM_STATE)

    dbb_re, dbb_im = undiag_b(dbcat[..., :SSM_ST]), undiag_b(dbcat[..., SSM_ST:])
    dc_re, dc_im = undiag_c(dccat[:, :, :SSM_ST]), -undiag_c(dccat[:, :, SSM_ST:])
    shape = (2, N_SSM_GROUPS, SSM_STATE)
    return dlam_re.reshape(shape), dlam_im.reshape(shape), dbb_re, dbb_im, dc_re, dc_im


def _to_segments(t):
    l, w = t.shape
    return t.reshape(N_SEG, l // N_SEG, w).transpose(1, 0, 2).reshape(l, w)


def _from_segments(t):
    l, w = t.shape
    return t.reshape(l // N_SEG, N_SEG, w).transpose(1, 0, 2).reshape(l, w)


SSM_RC = 256
SSM_JC = SSM_RC // N_SEG
_RE, _IM = pl.ds(0, SSM_ST), pl.ds(SSM_ST, SSM_ST)


def _cfma(ar, ai, xr, xi, br, bi):
    return ar * xr - ai * xi + br, ar * xi + ai * xr + bi


def _chunk_rows(ci, rev, nc):
    start = jnp.where(rev, (nc - 1 - ci) * SSM_RC, ci * SSM_RC)
    return pl.ds(pl.multiple_of(start, SSM_RC), SSM_RC)


def _scan_chunk(src, dst, ar, ai, rev, nj, ci, carry, prev_ref=None):
    def rows_of(staged, j, k):
        at = jnp.where(rev, SSM_JC - 1 - k, k) if staged else j
        return pl.ds(pl.multiple_of(at * N_SEG, N_SEG), N_SEG)

    for k in range(SSM_JC):
        jj = ci * SSM_JC + k
        j = jnp.where(rev, nj - 1 - jj, jj)
        rows = rows_of(src[1], j, k)
        nr, ni = _cfma(ar, ai, carry[0], carry[1], src[0][rows, _RE], src[0][rows, _IM])
        if dst is not None:
            rows = rows_of(dst[1], j, k)
            dst[0][rows, _RE] = nr
            dst[0][rows, _IM] = ni
        if prev_ref is None:
            carry = (nr, ni)
            continue
        jp = jnp.where(rev, j - 1, j + 1)
        if k == SSM_JC - 1:
            inside = jnp.where((jp >= 0) & (jp < nj), 1.0, 0.0)
            jp = jnp.clip(jp, 0, nj - 1)
        prow = pl.ds(pl.multiple_of(jp * N_SEG, N_SEG), N_SEG)
        xr, xi = prev_ref[prow, _RE], prev_ref[prow, _IM]
        sr, si = nr * xr + ni * xi, ni * xr - nr * xi
        if k == SSM_JC - 1:
            sr, si = inside * sr, inside * si
        carry = (nr, ni, carry[2] + sr, carry[3] + si)
    return carry


def _segment_inits(ar, ai, end_r, end_i, rev, nj):
    pr, pi = ar, ai
    for _ in range(int(math.log2(nj))):
        pr, pi = pr * pr - pi * pi, 2.0 * pr * pi
    seg = lax.broadcasted_iota(jnp.int32, end_r.shape, 0)
    zero = jnp.zeros_like(end_r)

    def chain(shift, keep):
        ir, ii = zero, zero
        for _ in range(N_SEG - 1):
            tr, ti = _cfma(pr, pi, ir, ii, end_r, end_i)
            ir = jnp.where(keep, pltpu.roll(tr, shift, 0), 0.0)
            ii = jnp.where(keep, pltpu.roll(ti, shift, 0), 0.0)
        return ir, ii

    up_r, up_i = chain(1, seg >= 1)
    dn_r, dn_i = chain(N_SEG - 1, seg <= N_SEG - 2)
    return jnp.where(rev, dn_r, up_r), jnp.where(rev, dn_i, up_i)


def _ssm_specs(l):
    act = pl.BlockSpec((l, SSM_CH), lambda k, d: (0, k))
    bmat = pl.BlockSpec((None, None, SSM_CH, 2 * SSM_ST), lambda k, d: (d, k, 0, 0))
    cmat = pl.BlockSpec((None, None, 2 * SSM_ST, SSM_CH), lambda k, d: (d, k, 0, 0))
    lam = pl.BlockSpec((None, None, 1, SSM_ST), lambda k, d: (d, k, 0, 0))
    return act, bmat, cmat, lam


def _ssm_fwd(u_seg, bcat, ccat, lam_re, lam_im):
    l = u_seg.shape[0]
    nj = l // N_SEG
    nc = l // SSM_RC

    def body(u_ref, b_ref, c_ref, lr_ref, li_ref, y_ref, keep_ref, xs_ref, stage0, stage1, keep_sem):
        k, d = pl.program_id(0), pl.program_id(1)
        rev = d == 1
        shape = (N_SEG, SSM_ST)
        ar, ai = jnp.broadcast_to(lr_ref[...], shape), jnp.broadcast_to(li_ref[...], shape)
        zero = jnp.zeros(shape, F32)

        def inputs(ci, stage):
            rows = _chunk_rows(ci, rev, nc)
            bu = _dg(u_ref[rows, :], b_ref[...], NN)
            stage[...] = bu
            xs_ref[rows, :] = bu

        def first(stage, ci, carry):
            return _scan_chunk((stage, True), None, ar, ai, rev, nj, ci, carry)

        def first_pass(t, carry):
            inputs(2 * t + 1, stage1)
            carry = first(stage0, 2 * t, carry)
            inputs(2 * t + 2, stage0)
            return first(stage1, 2 * t + 1, carry)

        inputs(0, stage0)
        carry = lax.fori_loop(0, nc // 2 - 1, first_pass, (zero, zero))
        inputs(nc - 1, stage1)
        carry = first(stage0, nc - 2, carry)
        end_r, end_i = first(stage1, nc - 1, carry)
        init = _segment_inits(ar, ai, end_r, end_i, rev, nj)

        @pl.when(d == 0)
        def _():
            y_ref[...] = jnp.zeros_like(y_ref)

        def outputs(ci):
            rows = _chunk_rows(ci, rev, nc)
            y_ref[rows, :] += _dg(xs_ref[rows, :].astype(BF16), c_ref[...], NN)
            pltpu.make_async_copy(xs_ref.at[rows], keep_ref.at[d, k, rows], keep_sem).start()

        def second(ci, carry):
            return _scan_chunk((xs_ref, False), (xs_ref, False), ar, ai, rev, nj, ci, carry)

        def second_pass(ci, carry):
            outputs(ci - 1)
            return second(ci, carry)

        lax.fori_loop(1, nc, second_pass, second(0, init))
        outputs(nc - 1)
        pltpu.make_async_copy(xs_ref, keep_ref.at[d, k], keep_sem).wait()

    act, bmat, cmat, lam = _ssm_specs(l)
    return pl.pallas_call(
        body, grid=(SSM_CB, 2), in_specs=[act, bmat, cmat, lam, lam], out_specs=[act, ANY],
        out_shape=[jax.ShapeDtypeStruct((l, SSM_WIDTH), F32),
                   jax.ShapeDtypeStruct((2, SSM_CB, l, 2 * SSM_ST), F32)],
        scratch_shapes=[pltpu.VMEM((l, 2 * SSM_ST), F32), pltpu.VMEM((SSM_RC, 2 * SSM_ST), F32),
                        pltpu.VMEM((SSM_RC, 2 * SSM_ST), F32), pltpu.SemaphoreType.DMA],
        name="ssm_fwd", compiler_params=_params(("parallel", "arbitrary"), vmem_mb=56),
    )(u_seg, bcat.astype(BF16), ccat.astype(BF16), lam_re, lam_im)


def _ssm_bwd(u_seg, dy_seg, states, bcat, ccat, lam_re, lam_im):
    l = u_seg.shape[0]
    nj = l // N_SEG
    nc = l // SSM_RC

    def body(u_ref, dy_ref, keep_ref, b_ref, c_ref, lr_ref, li_ref,
             du_ref, db_ref, dc_ref, dlr_ref, dli_ref, xs_ref, gs_ref, stage0, stage1, keep_sem):
        k, d = pl.program_id(0), pl.program_id(1)
        rev = d == 1
        back = jnp.logical_not(rev)
        shape = (N_SEG, SSM_ST)
        ar, ai = jnp.broadcast_to(lr_ref[...], shape), -jnp.broadcast_to(li_ref[...], shape)
        zero = jnp.zeros(shape, F32)
        fetch = pltpu.make_async_copy(keep_ref.at[d, k], xs_ref, keep_sem)
        fetch.start()

        def inputs(ci, stage):
            rows = _chunk_rows(ci, back, nc)
            dx = _dg(dy_ref[rows, :], c_ref[...], NT)
            stage[...] = dx
            gs_ref[rows, :] = dx

        def first(stage, ci, carry):
            return _scan_chunk((stage, True), None, ar, ai, back, nj, ci, carry)

        def first_pass(t, carry):
            inputs(2 * t + 1, stage1)
            carry = first(stage0, 2 * t, carry)
            inputs(2 * t + 2, stage0)
            return first(stage1, 2 * t + 1, carry)

        inputs(0, stage0)
        carry = lax.fori_loop(0, nc // 2 - 1, first_pass, (zero, zero))
        inputs(nc - 1, stage1)
        carry = first(stage0, nc - 2, carry)
        end_r, end_i = first(stage1, nc - 1, carry)
        init = _segment_inits(ar, ai, end_r, end_i, back, nj)
        fetch.wait()
        db_ref[...] = jnp.zeros_like(db_ref)
        dc_ref[...] = jnp.zeros_like(dc_ref)

        @pl.when(d == 0)
        def _():
            du_ref[...] = jnp.zeros_like(du_ref)

        def outputs(ci, stage):
            rows = _chunk_rows(ci, back, nc)
            g = stage[...].astype(BF16)
            dc_ref[...] += _dg(xs_ref[rows, :].astype(BF16), dy_ref[rows, :], TN)
            db_ref[...] += _dg(u_ref[rows, :], g, TN)
            du_ref[rows, :] += _dg(g, b_ref[...], NT)

        def second(ci, stage, carry):
            return _scan_chunk((gs_ref, False), (stage, True), ar, ai, back, nj, ci, carry, prev_ref=xs_ref)

        def second_pass(t, carry):
            outputs(2 * t, stage0)
            carry = second(2 * t + 1, stage1, carry)
            outputs(2 * t + 1, stage1)
            return second(2 * t + 2, stage0, carry)

        carry = lax.fori_loop(0, nc // 2 - 1, second_pass, second(0, stage0, init + (zero, zero)))
        outputs(nc - 2, stage0)
        gr, gi, acc_r, acc_i = second(nc - 1, stage1, carry)
        outputs(nc - 1, stage1)

        seg = lax.broadcasted_iota(jnp.int32, shape, 0)
        jb = jnp.where(rev, nj - 1, 0)
        erow = pl.ds(pl.multiple_of((nj - 1 - jb) * N_SEG, N_SEG), N_SEG)

        def before(t):
            up = jnp.where(seg >= 1, pltpu.roll(t, 1, 0), 0.0)
            down = jnp.where(seg <= N_SEG - 2, pltpu.roll(t, N_SEG - 1, 0), 0.0)
            return jnp.where(rev, down, up)

        init_r, init_i = before(xs_ref[erow, _RE]), before(xs_ref[erow, _IM])
        acc_r = acc_r + gr * init_r + gi * init_i
        acc_i = acc_i + gi * init_r - gr * init_i
        dlr_ref[...] = jnp.sum(acc_r, axis=0, keepdims=True)
        dli_ref[...] = jnp.sum(acc_i, axis=0, keepdims=True)

    act, bmat, cmat, lam = _ssm_specs(l)
    return pl.pallas_call(
        body, grid=(SSM_CB, 2), in_specs=[act, act, ANY, bmat, cmat, lam, lam],
        out_specs=[act, bmat, cmat, lam, lam],
        out_shape=[jax.ShapeDtypeStruct((l, SSM_WIDTH), F32),
                   jax.ShapeDtypeStruct(bcat.shape, F32), jax.ShapeDtypeStruct(ccat.shape, F32),
                   jax.ShapeDtypeStruct(lam_re.shape, F32), jax.ShapeDtypeStruct(lam_im.shape, F32)],
        scratch_shapes=[pltpu.VMEM((l, 2 * SSM_ST), F32), pltpu.VMEM((l, 2 * SSM_ST), F32),
                        pltpu.VMEM((SSM_RC, 2 * SSM_ST), F32), pltpu.VMEM((SSM_RC, 2 * SSM_ST), F32),
                        pltpu.SemaphoreType.DMA],
        name="ssm_bwd", compiler_params=_params(("parallel", "arbitrary"), vmem_mb=58),
    )(u_seg, dy_seg, states, bcat.astype(BF16), ccat.astype(BF16), lam_re, lam_im)


def _glu_fwd(y_ssm, u, d_skip, w_glu):
    l, w = u.shape

    def body(y_ref, u_ref, d_ref, w_ref, pre_ref, s_ref, ys_ref):
        pre = y_ref[...] + d_ref[...] * u_ref[...]
        z = _gelu(pre)
        s = _dg(z.astype(BF16), w_ref[...], NN)
        pre_ref[...] = pre
        s_ref[...] = s
        ys_ref[...] = z * _sigmoid(s)

    row = pl.BlockSpec((TM_EW, w), lambda i: (i, 0))
    out = jax.ShapeDtypeStruct((l, w), F32)
    return pl.pallas_call(
        body, grid=(l // TM_EW,),
        in_specs=[row, row, pl.BlockSpec((1, w), lambda i: (0, 0)), pl.BlockSpec((w, w), lambda i: (0, 0))],
        out_specs=[row, row, row], out_shape=[out, out, out], name="glu_fwd",
        compiler_params=_params(("parallel",)),
    )(y_ssm, u, d_skip, w_glu)


def _glu_bwd(pre, s, dys, u, d_skip, w_glu):
    l, w = u.shape

    def body(pre_ref, s_ref, dys_ref, u_ref, d_ref, w_ref, dpre_ref, z_ref, ds_ref, dd_ref):
        pre, dys = pre_ref[...], dys_ref[...]
        z = _gelu(pre)
        sig = _sigmoid(s_ref[...])
        ds = (dys * z * sig * (1.0 - sig)).astype(BF16)
        dz = dys * sig + _dg(ds, w_ref[...], NT)
        dpre = dz * _gelu_grad(pre)
        dpre_ref[...] = dpre
        z_ref[...] = z.astype(BF16)
        ds_ref[...] = ds

        @pl.when(pl.program_id(0) == 0)
        def _():
            dd_ref[...] = jnp.zeros_like(dd_ref)

        dd_ref[...] += jnp.sum(dpre * u_ref[...], axis=0, keepdims=True)

    row = pl.BlockSpec((TM_EW, w), lambda i: (i, 0))
    vec = pl.BlockSpec((1, w), lambda i: (0, 0))
    return pl.pallas_call(
        body, grid=(l // TM_EW,),
        in_specs=[row, row, row, row, vec, pl.BlockSpec((w, w), lambda i: (0, 0))],
        out_specs=[row, row, row, vec],
        out_shape=[jax.ShapeDtypeStruct((l, w), F32), jax.ShapeDtypeStruct((l, w), BF16),
                   jax.ShapeDtypeStruct((l, w), BF16), jax.ShapeDtypeStruct((1, w), F32)],
        name="glu_bwd", compiler_params=_params(("arbitrary",)),
    )(pre, s, dys, u, d_skip, w_glu)


TM_CV = 512
TC_CV = 256
TM_CF = 256
TC_CF = D_FF // 2
HALO = SUBLANES


def _conv_specs(l, col0, tm=TM_CV, tc=TC_CV):
    per = tm // HALO
    nh = l // HALO
    off = col0 // tc
    return [
        pl.BlockSpec((HALO, tc), lambda j, i: (jnp.maximum(i * per - 1, 0), j + off)),
        pl.BlockSpec((tm, tc), lambda j, i: (i, j + off)),
        pl.BlockSpec((HALO, tc), lambda j, i: (jnp.minimum((i + 1) * per, nh - 1), j + off)),
    ]


def _ext(prev_ref, mid_ref, next_ref, first, last):
    p = jnp.where(first, 0.0, prev_ref[...])
    n = jnp.where(last, 0.0, next_ref[...])
    return jnp.concatenate([p, mid_ref[...], n], axis=0)


def _shift_dn(t):
    return pltpu.roll(t, 1, 0)


def _shift_up(t):
    return pltpu.roll(t, t.shape[0] - 1, 0)


def _conv3(e, w_ref, b_ref):
    return w_ref[0:1, :] * _shift_dn(e) + w_ref[1:2, :] * e + w_ref[2:3, :] * _shift_up(e) + b_ref[...]


def _convffn_fwd(up_pre, conv_w, conv_b):
    l = up_pre.shape[0]
    tm, tc = TM_CF, TC_CF
    ni = l // tm
    wspec = lambda off: pl.BlockSpec((3, tc), lambda j, i: (0, j + off))
    bspec = lambda off: pl.BlockSpec((1, tc), lambda j, i: (0, j + off))
    voff = D_FF // tc

    def body(gp, gm, gn, vp, vm, vn, wg, bg, wv, bv, o_ref):
        i = pl.program_id(1)
        first, last = i == 0, i == ni - 1
        gate = _conv3(_ext(gp, gm, gn, first, last), wg, bg)[HALO:HALO + tm]
        val = _conv3(_ext(vp, vm, vn, first, last), wv, bv)[HALO:HALO + tm]
        o_ref[...] = (gate * _sigmoid(gate) * val).astype(BF16)

    return pl.pallas_call(
        body, grid=(D_FF // tc, ni),
        in_specs=_conv_specs(l, 0, tm, tc) + _conv_specs(l, D_FF, tm, tc)
        + [wspec(0), bspec(0), wspec(voff), bspec(voff)],
        out_specs=pl.BlockSpec((tm, tc), lambda j, i: (i, j)),
        out_shape=jax.ShapeDtypeStruct((l, D_FF), BF16), name="convffn_fwd",
        compiler_params=_params(("parallel", "parallel")),
    )(up_pre, up_pre, up_pre, up_pre, up_pre, up_pre, conv_w, conv_b, conv_w, conv_b)


HALO_B = 2 * SUBLANES


def _convffn_bwd(up_pre, dx2b, w_down, conv_w, conv_b):
    l = up_pre.shape[0]
    ni = l // TM_CV
    d = dx2b.shape[1]
    wspec = lambda off: pl.BlockSpec((3, TC_CV), lambda i, j: (0, j + off))
    bspec = lambda off: pl.BlockSpec((1, TC_CV), lambda i, j: (0, j + off))
    voff = D_FF // TC_CV
    swap = lambda spec: pl.BlockSpec(spec.block_shape, lambda i, j, f=spec.index_map: f(j, i))
    per, nh = TM_CV // HALO_B, l // HALO_B
    dx_specs = [pl.BlockSpec((HALO_B, d), lambda i, j: (jnp.maximum(i * per - 1, 0), 0)),
                pl.BlockSpec((TM_CV, d), lambda i, j: (i, 0)),
                pl.BlockSpec((HALO_B, d), lambda i, j: (jnp.minimum((i + 1) * per, nh - 1), 0))]

    def body(gp, gm, gn, vp, vm, vn, xp, xm, xn, wd, wg, bg, wv, bv, dup_ref, pg_ref, pv_ref):
        i = pl.program_id(0)
        first, last = i == 0, i == ni - 1
        ge, ve = _ext(gp, gm, gn, first, last), _ext(vp, vm, vn, first, last)
        zero = jnp.zeros((HALO_B, d), BF16)
        dx = jnp.concatenate([jnp.where(first, zero, xp[...]), xm[...], jnp.where(last, zero, xn[...])], axis=0)
        de = _dg(dx, wd[...], NT)[HALO_B - HALO:HALO_B + TM_CV + HALO]
        taps = [(_shift_dn(e), e, _shift_up(e)) for e in (ge, ve)]
        conv = lambda t, w_ref, b_ref: w_ref[0:1, :] * t[0] + w_ref[1:2, :] * t[1] + w_ref[2:3, :] * t[2] + b_ref[...]
        gate, val = conv(taps[0], wg, bg), conv(taps[1], wv, bv)
        sig = _sigmoid(gate)
        silu = gate * sig
        dgate = de * val * (sig + silu * (1.0 - sig))
        dval = de * silu
        mid = slice(HALO, HALO + TM_CV)
        rid = lax.broadcasted_iota(jnp.int32, (SUBLANES, TC_CV), 0)
        for half, (dup, tap, w_ref, p_ref) in enumerate(((dgate, taps[0], wg, pg_ref), (dval, taps[1], wv, pv_ref))):
            dpre = w_ref[0:1, :] * _shift_up(dup) + w_ref[1:2, :] * dup + w_ref[2:3, :] * _shift_dn(dup)
            dup_ref[half] = dpre[mid].astype(BF16)
            dm_ = dup[mid]
            sums = [jnp.sum(dm_ * t[mid], axis=0, keepdims=True) for t in tap]
            sums.append(jnp.sum(dm_, axis=0, keepdims=True))
            acc = jnp.zeros((SUBLANES, TC_CV), F32)
            for k, sk in enumerate(sums):
                acc = jnp.where(rid == k, sk, acc)
            p_ref[...] = acc

    par = pl.BlockSpec((None, SUBLANES, TC_CV), lambda i, j: (i, 0, j))
    dup, pg, pv = pl.pallas_call(
        body, grid=(ni, D_FF // TC_CV),
        in_specs=[swap(s) for s in _conv_specs(l, 0) + _conv_specs(l, D_FF)] + dx_specs
        + [pl.BlockSpec((TC_CV, d), lambda i, j: (j, 0)), wspec(0), bspec(0), wspec(voff), bspec(voff)],
        out_specs=[pl.BlockSpec((2, TM_CV, TC_CV), lambda i, j: (0, i, j)), par, par],
        out_shape=[jax.ShapeDtypeStruct((2, l, D_FF), BF16),
                   jax.ShapeDtypeStruct((ni, SUBLANES, D_FF), F32), jax.ShapeDtypeStruct((ni, SUBLANES, D_FF), F32)],
        name="convffn_bwd", compiler_params=_params(("parallel", "parallel")),
    )(up_pre, up_pre, up_pre, up_pre, up_pre, up_pre, dx2b, dx2b, dx2b, w_down, conv_w, conv_b, conv_w, conv_b)
    return dup, jnp.concatenate([jnp.sum(pg, axis=0), jnp.sum(pv, axis=0)], axis=1)


def _local_step(x, target, wb, sp, mixer_weights=None, late_weights=None, ffn_grads_ready=None,
                ffn_grads_next=None):
    l = x.shape[0]
    tabs = _rope_tables(l)
    disc = _ssm_disc(sp["a_re"], sp["a_im"], sp["log_step"], sp["b_re"], sp["b_im"])
    bcat, ccat, lam_re, lam_im = _ssm_pack(*disc, sp["c_re"], sp["c_im"])
    d_skip = sp["d_skip"].reshape(1, SSM_WIDTH)

    big = min(l, 1024)
    h, proj, u = _rms_mm_split(x, sp["norm_mix_g"], wb["w_in"], QKV_WIDTH, "mm_in")
    qkv = _rope_fwd(proj, tabs)
    attn, lse = _attn_fwd(qkv, sp["sink"])
    u_seg = _to_segments(u).astype(BF16)
    y_seg, states = _ssm_fwd(u_seg, bcat, ccat, lam_re, lam_im)
    y_ssm = _from_segments(y_seg)
    if mixer_weights is not None:
        wb = dict(wb, **mixer_weights(attn))
    pre, s_glu, ys = _glu_fwd(y_ssm, u, d_skip, wb["w_glu"])
    mixed = _mix_fwd(attn, ys, sp["norm_attn_g"], sp["norm_ssm_g"])
    x1, h2 = _mm_res_rms(mixed, wb["w_out"], x, sp["norm_ffn_g"], "mm_out")
    if late_weights is not None:
        wb = dict(wb, **late_weights(h2))
    up_pre = _mm_nn_cols(h2, wb["w_up"], big, "mm_up")
    conv_w = wb["conv_w"]
    act = _convffn_fwd(up_pre, conv_w, sp["conv_b"])
    loss, dx2, dx2b, d_final_g = _mm_res_loss(act, wb["w_down"], x1, sp["norm_final_g"].reshape(1, D_MODEL), target)

    g = {"norm_final_g": d_final_g.reshape(D_MODEL)}
    g["w_down"] = _mm_tn(act, dx2b, D_FF // 2, 512, "mm_down_dw")
    dup_pre, conv_par = _convffn_bwd(up_pre, dx2b, wb["w_down"], conv_w, sp["conv_b"])
    g["conv_w"], g["conv_b"] = conv_par[0:3], conv_par[3:4]
    g["w_up"] = _mm_tn_cols(h2, dup_pre, wb["w_up"].shape[0], 512, "mm_up_dw")
    zero = ffn_grads_ready(g["w_up"], g["w_down"]) if ffn_grads_ready is not None else 0.0
    dx1, dx1b, g["norm_ffn_g"] = _mm_cols_rms_bwd(dup_pre, wb["w_up"], x1, sp["norm_ffn_g"] + zero, dx2, "mm_up_dx")
    dmixed = _mm_nt(dx1b, wb["w_out"], big, 1024, F32, "mm_out_dx")
    g["w_out"] = _mm_tn(mixed, dx1b, 1024, 1024, "mm_out_dw")
    zero = ffn_grads_next(dmixed) if ffn_grads_next is not None else 0.0
    dattn, dys, g["norm_attn_g"], g["norm_ssm_g"] = _mix_bwd(attn, ys, sp["norm_attn_g"] + zero, sp["norm_ssm_g"],
                                                            dmixed)
    dpre, zb, dsb, dd = _glu_bwd(pre, s_glu, dys, u, d_skip, wb["w_glu"])
    g["d_skip"] = dd.reshape(N_SSM_GROUPS, SSM_GROUP)
    g["w_glu"] = _mm_tn(zb, dsb, 512, 512, "mm_glu_dw")
    du_seg, dbcat, dccat, dlam_re, dlam_im = _ssm_bwd(u_seg, _to_segments(dpre).astype(BF16), states, bcat, ccat,
                                                      lam_re, lam_im)
    dlb_re, dlb_im, dbb_re, dbb_im, g["c_re"], g["c_im"] = _ssm_unpack(dbcat, dccat, dlam_re, dlam_im)
    _, disc_vjp = jax.vjp(_ssm_disc, sp["a_re"], sp["a_im"], sp["log_step"], sp["b_re"], sp["b_im"])
    g["a_re"], g["a_im"], g["log_step"], g["b_re"], g["b_im"] = disc_vjp((dlb_re, dlb_im, dbb_re, dbb_im))
    dq, dkv, g["sink"] = _attn_bwd(qkv, attn, dattn, lse, sp["sink"])
    dproj = _rope_bwd(dq, dkv, _from_segments(du_seg), dpre, d_skip, tabs)
    g["w_in"] = _mm_tn(dproj, h, IN_WIDTH // 5, D_MODEL, "mm_in_dw")
    grad_x, _, g["norm_mix_g"] = _mm_nn_rms_bwd(dproj, wb["w_in"], x, sp["norm_mix_g"], dx1, "mm_in_dx")
    return loss, grad_x, g


MESH = pl.DeviceIdType.MESH
ANY = pl.BlockSpec(memory_space=pl.ANY)


def _place():
    x, y, c = lax.axis_index("x"), lax.axis_index("y"), lax.axis_index("c")
    chips = [(1 - x, y), (x, 1 - y), (1 - x, 1 - y)]
    return x, y, c, chips


def _chip_index(px, py):
    return 2 * px + py


CHUNK_BYTES = 256 * 1024
MAX_CHUNKS = 16


def _row_chunks(rows, row_bytes, align):
    n = max(1, min(MAX_CHUNKS, (rows * row_bytes) // CHUNK_BYTES))
    per = -(-rows // n)
    per = -(-per // align) * align
    return [(r0, min(per, rows - r0)) for r0 in range(0, rows, per)]


def _align_of(dtype):
    return SUBLANES * 4 // jnp.dtype(dtype).itemsize


def _remote(src, dst, send_sem, recv_sem, to):
    return pltpu.make_async_remote_copy(src_ref=src, dst_ref=dst, send_sem=send_sem, recv_sem=recv_sem,
                                        device_id=to, device_id_type=MESH)


CAST_ROWS = 64


def _gather_weights(shards, dtypes):
    nw = len(shards)

    def body(*refs):
        w_refs, o_refs = refs[:nw], refs[nw:2 * nw]
        send_sems, recv_sems, in_sems, out_sems = refs[2 * nw:2 * nw + 4]
        raw, cast = refs[2 * nw + 4:3 * nw + 4], refs[3 * nw + 4:]
        x, y, c, chips = _place()
        mine = _chip_index(x, y)
        sibling = (x, y, 1 - c)

        def rows_of(ref, chip, r0, nr):
            return ref.at[chip, pl.ds(r0, nr), :]

        def copy(wi, k, src, dst, to):
            return _remote(src, dst, send_sems.at[wi, k], recv_sems.at[wi, k], to)

        geo = []
        for wi in range(nw):
            rows, cols = w_refs[wi].shape
            row_bytes = cols * jnp.dtype(dtypes[wi]).itemsize
            geo.append((rows // 2, _row_chunks(rows // 2, row_bytes, _align_of(dtypes[wi]))))

        stage_in = [pltpu.make_async_copy(w_refs[wi], raw[wi], in_sems.at[wi]) for wi in range(nw)]
        for cp in stage_in:
            cp.start()
        staged = [raw[wi] if dtypes[wi] == w_refs[wi].dtype else cast[wi] for wi in range(nw)]
        stage_out = []
        for wi in range(nw):
            stage_in[wi].wait()
            if staged[wi] is not raw[wi]:
                def cast_rows(i, _, wi=wi):
                    rows = pl.ds(pl.multiple_of(i * CAST_ROWS, CAST_ROWS), CAST_ROWS)
                    cast[wi][rows, :] = raw[wi][rows, :].astype(dtypes[wi])
                    return 0

                lax.fori_loop(0, w_refs[wi].shape[0] // CAST_ROWS, cast_rows, 0)
            cp = pltpu.make_async_copy(staged[wi], o_refs[wi].at[mine], out_sems.at[wi])
            cp.start()
            stage_out.append(cp)

        for wi in range(nw):
            hr, half_chunks = geo[wi]
            for j, chip in enumerate(chips):
                for r0, nr in half_chunks:
                    copy(wi, j, staged[wi].at[pl.ds(c * hr + r0, nr), :],
                         rows_of(o_refs[wi], mine, c * hr + r0, nr), (*chip, c)).start()
        for wi in range(nw):
            hr, half_chunks = geo[wi]
            for j, chip in enumerate(chips):
                got = rows_of(o_refs[wi], _chip_index(*chip), c * hr, hr)
                copy(wi, j, got, got, (*chip, c)).wait_recv()
                for r0, nr in half_chunks:
                    piece = rows_of(o_refs[wi], _chip_index(*chip), c * hr + r0, nr)
                    copy(wi, 3 + j, piece, piece, sibling).start()
        for wi in range(nw):
            hr = geo[wi][0]
            for j, chip in enumerate(chips):
                got = rows_of(o_refs[wi], _chip_index(*chip), (1 - c) * hr, hr)
                copy(wi, 3 + j, got, got, sibling).wait_recv()
        for wi in range(nw):
            hr = geo[wi][0]
            sent = rows_of(o_refs[wi], mine, c * hr, hr)
            for k in range(6):
                copy(wi, k, sent, sent, sibling).wait_send()
            stage_out[wi].wait()

    return pl.pallas_call(
        body, in_specs=[ANY] * nw, out_specs=[ANY] * nw,
        out_shape=[jax.ShapeDtypeStruct((4, *s.shape), t) for s, t in zip(shards, dtypes)],
        scratch_shapes=[pltpu.SemaphoreType.DMA((nw, 6)), pltpu.SemaphoreType.DMA((nw, 6)),
                        pltpu.SemaphoreType.DMA((nw,)), pltpu.SemaphoreType.DMA((nw,))]
        + [pltpu.VMEM(s.shape, s.dtype) for s in shards] + [pltpu.VMEM(s.shape, t) for s, t in zip(shards, dtypes)],
        name="gather_weights", compiler_params=_params(vmem_mb=40),
    )(*shards)


HBM = pl.BlockSpec(memory_space=pltpu.HBM)
SEM = pl.BlockSpec(memory_space=pltpu.SEMAPHORE)
EFFECT = pltpu.SideEffectType.DATAFLOW_SIDE_EFFECTING


def _cast_place(w, place, dtype, after, name):
    rows, cols = w.shape
    tr = _row_tile(rows, cols, _align_of(dtype))

    def body(p_ref, w_ref, after_ref, o_ref):
        del p_ref, after_ref
        o_ref[...] = w_ref[...].astype(dtype)

    grid_spec = pltpu.PrefetchScalarGridSpec(
        num_scalar_prefetch=1, grid=(rows // tr,),
        in_specs=[pl.BlockSpec((tr, cols), lambda i, p: (i, 0)), ANY],
        out_specs=pl.BlockSpec((None, tr, cols), lambda i, p: (p[1], i, 0)))
    return pl.pallas_call(body, grid_spec=grid_spec, out_shape=jax.ShapeDtypeStruct((4, rows, cols), dtype),
                          name=name, compiler_params=_params(("parallel",)))(place, w, after)


def _split_start(name, arrays, n_pairs, issue):
    n = len(arrays)

    def body(*refs):
        issue(refs[:n], refs[n:n + n_pairs], refs[n + n_pairs:n + 2 * n_pairs])
        token = refs[2 * n + 2 * n_pairs]
        token[...] = jnp.zeros_like(token)

    dma = pltpu.SemaphoreType.DMA(())
    outs = pl.pallas_call(
        body, name=name,
        out_shape=[dma] * (2 * n_pairs) + [pltpu.HBM(t.shape, t.dtype) for t in arrays]
        + [jax.ShapeDtypeStruct((SUBLANES, LANES), F32)],
        in_specs=[HBM] * n, out_specs=[SEM] * (2 * n_pairs) + [HBM] * n + [pl.BlockSpec(memory_space=pltpu.VMEM)],
        input_output_aliases={a: 2 * n_pairs + a for a in range(n)},
        compiler_params=pltpu.CompilerParams(has_side_effects=EFFECT),
    )(*[pltpu.with_memory_space_constraint(t, pltpu.HBM) for t in arrays])
    return outs[:n_pairs], outs[n_pairs:2 * n_pairs], outs[2 * n_pairs:2 * n_pairs + n], outs[-1]


def _split_wait(name, send_sems, recv_sems, flying, sizes, after):
    n, n_pairs = len(flying), len(send_sems)

    def body(*refs):
        x, y, c, _ = _place()
        for k, ref in enumerate(sizes(refs[:n])):
            cp = _remote(ref, ref, refs[n + k], refs[n + n_pairs + k], (x, y, 1 - c))
            cp.wait_send()
            cp.wait_recv()

    return pl.pallas_call(
        body, name=name, out_shape=[pltpu.HBM(t.shape, t.dtype) for t in flying],
        in_specs=[HBM] * n + [SEM] * (2 * n_pairs) + [ANY], out_specs=[HBM] * n,
        input_output_aliases={a: a for a in range(n)},
        compiler_params=pltpu.CompilerParams(has_side_effects=EFFECT),
    )(*flying, *send_sems, *recv_sems, after)


def _spread_start(lands, name):
    def issue(land_refs, send_sems, recv_sems):
        x, y, c, chips = _place()
        mine = _chip_index(x, y)
        for a, land in enumerate(land_refs):
            _, rows, cols = land.shape
            hr = rows // 2
            row_bytes = cols * jnp.dtype(land.dtype).itemsize
            for r0, nr in _row_chunks(hr, row_bytes, _align_of(land.dtype)):
                piece = land.at[mine, pl.ds(c * hr + r0, nr), :]
                for chip in chips:
                    for core in (0, 1):
                        _remote(piece, piece, send_sems[a], recv_sems[a], (*chip, core)).start()

    return _split_start(name, lands, len(lands), issue)


def _spread_wait(send_sems, recv_sems, flying, after, name):
    return _split_wait(name, send_sems, recv_sems, flying, lambda refs: [r.at[pl.ds(0, 3)] for r in refs], after)


def _pair_start(grads):
    n = len(grads)
    zones = [lax.empty((4, g.shape[1] // 2, g.shape[2]), F32) for g in grads]

    def issue(refs, send_sems, recv_sems):
        x, y, c, _ = _place()
        for a in range(n):
            g_ref, z_ref = refs[a], refs[n + a]
            _, rows, cols = g_ref.shape
            hr = rows // 2
            for k in range(4):
                for r0, nr in _row_chunks(hr, cols * 4, SUBLANES):
                    _remote(g_ref.at[k, pl.ds((1 - c) * hr + r0, nr), :], z_ref.at[k, pl.ds(r0, nr), :],
                            send_sems[a], recv_sems[a], (x, y, 1 - c)).start()

    return _split_start("pair_start", list(grads) + zones, n, issue)


def _pair_wait(send_sems, recv_sems, flying, after):
    n = len(flying) // 2
    out = _split_wait("pair_wait", send_sems, recv_sems, flying, lambda refs: list(refs[n:]), after)
    return out[:n], out[n:]


def _chip_start(sums):
    n = len(sums)
    zones = [lax.empty((3, *s.shape[1:]), s.dtype) for s in sums]

    def issue(refs, send_sems, recv_sems):
        x, y, c, chips = _place()
        for a in range(n):
            s_ref, z_ref = refs[a], refs[n + a]
            _, rows, cols = s_ref.shape
            row_bytes = cols * jnp.dtype(s_ref.dtype).itemsize
            for r0, nr in _row_chunks(rows, row_bytes, _align_of(s_ref.dtype)):
                for j, chip in enumerate(chips):
                    _remote(s_ref.at[_chip_index(*chip), pl.ds(r0, nr), :], z_ref.at[j, pl.ds(r0, nr), :],
                            send_sems[a], recv_sems[a], (*chip, c)).start()

    return _split_start("chip_start", list(sums) + zones, n, issue)


def _chip_wait(send_sems, recv_sems, flying, after):
    n = len(flying) // 2
    return _split_wait("chip_wait", send_sems, recv_sems, flying, lambda refs: list(refs[n:]), after)[n:]


def _pair_exchange(grads):
    na = len(grads)

    def body(*refs):
        g_refs, o_refs = refs[:na], refs[na:2 * na]
        send_sems, recv_sems = refs[2 * na:]
        x, y, c, _ = _place()
        sibling = (x, y, 1 - c)
        for ai in range(na):
            _, rows, cols = g_refs[ai].shape
            hr = rows // 2
            for k in range(4):
                for r0, nr in _row_chunks(hr, cols * 4, SUBLANES):
                    _remote(g_refs[ai].at[k, pl.ds((1 - c) * hr + r0, nr), :], o_refs[ai].at[k, pl.ds(r0, nr), :],
                            send_sems.at[ai], recv_sems.at[ai], sibling).start()
        for ai in range(na):
            _remote(o_refs[ai], o_refs[ai], send_sems.at[ai], recv_sems.at[ai], sibling).wait()

    return pl.pallas_call(
        body, in_specs=[ANY] * na, out_specs=[ANY] * na,
        out_shape=[jax.ShapeDtypeStruct((4, g.shape[1] // 2, g.shape[2]), F32) for g in grads],
        scratch_shapes=[pltpu.SemaphoreType.DMA((na,)), pltpu.SemaphoreType.DMA((na,))],
        name="pair_exchange",
    )(*grads)


def _row_tile(rows, cols, align):
    best = align
    for cand in range(align, rows + 1, align):
        if rows % cand == 0 and cand * cols <= 256 * 1024:
            best = cand
    return best


def _pair_sum(g, got, place, transit, name):
    _, rows, cols = g.shape
    hr = rows // 2
    tr = _row_tile(hr, cols, _align_of(transit))
    nt = hr // tr

    def body(p_ref, g_ref, r_ref, s_ref, own_ref):
        total = g_ref[...] + r_ref[...]
        s_ref[...] = total.astype(transit)

        @pl.when(pl.program_id(1) == p_ref[1])
        def _():
            own_ref[...] = total

    grid_spec = pltpu.PrefetchScalarGridSpec(
        num_scalar_prefetch=1, grid=(nt, 4),
        in_specs=[pl.BlockSpec((None, tr, cols), lambda i, k, p: (k, p[0] * nt + i, 0)),
                  pl.BlockSpec((None, tr, cols), lambda i, k, p: (k, i, 0))],
        out_specs=[pl.BlockSpec((None, tr, cols), lambda i, k, p: (k, i, 0)),
                   pl.BlockSpec((tr, cols), lambda i, k, p: (i, 0))])
    return pl.pallas_call(
        body, grid_spec=grid_spec,
        out_shape=[jax.ShapeDtypeStruct((4, hr, cols), transit), jax.ShapeDtypeStruct((hr, cols), F32)],
        name=name, compiler_params=_params(("parallel", "arbitrary")),
    )(place, g, got)


def _chip_exchange(sums):
    na = len(sums)

    def body(*refs):
        s_refs, o_refs = refs[:na], refs[na:2 * na]
        send_sems, recv_sems = refs[2 * na:]
        x, y, c, chips = _place()
        for ai in range(na):
            _, rows, cols = s_refs[ai].shape
            row_bytes = cols * jnp.dtype(s_refs[ai].dtype).itemsize
            for r0, nr in _row_chunks(rows, row_bytes, _align_of(s_refs[ai].dtype)):
                for j, chip in enumerate(chips):
                    _remote(s_refs[ai].at[_chip_index(*chip), pl.ds(r0, nr), :], o_refs[ai].at[j, pl.ds(r0, nr), :],
                            send_sems.at[ai, j], recv_sems.at[ai, j], (*chip, c)).start()
        for ai in range(na):
            for j, chip in enumerate(chips):
                _remote(o_refs[ai].at[j], o_refs[ai].at[j], send_sems.at[ai, j], recv_sems.at[ai, j],
                        (*chip, c)).wait()

    return pl.pallas_call(
        body, in_specs=[ANY] * na, out_specs=[ANY] * na,
        out_shape=[jax.ShapeDtypeStruct((3, *s.shape[1:]), s.dtype) for s in sums],
        scratch_shapes=[pltpu.SemaphoreType.DMA((na, 3)), pltpu.SemaphoreType.DMA((na, 3))],
        name="chip_exchange",
    )(*sums)


def _chip_sum(own, landed, name):
    hr, cols = own.shape
    tr = _row_tile(hr, cols, _align_of(landed.dtype))

    def body(o_ref, l_ref, f_ref):
        acc = o_ref[...]
        for j in range(3):
            acc = acc + l_ref[j].astype(F32)
        f_ref[...] = acc

    return pl.pallas_call(
        body, grid=(hr // tr,),
        in_specs=[pl.BlockSpec((tr, cols), lambda i: (i, 0)), pl.BlockSpec((3, tr, cols), lambda i: (0, i, 0))],
        out_specs=pl.BlockSpec((tr, cols), lambda i: (i, 0)),
        out_shape=jax.ShapeDtypeStruct((hr, cols), F32), name=name,
        compiler_params=_params(("parallel",)),
    )(own, landed)


def _final_exchange(halves, small):
    nh = len(halves)

    def body(*refs):
        h_refs, s_ref = refs[:nh], refs[nh]
        o_refs, so_ref = refs[nh + 1:2 * nh + 1], refs[2 * nh + 1]
        send_sems, recv_sems, local_sem, ssend_sems, srecv_sems = refs[2 * nh + 2:]
        x, y, c, _ = _place()
        me = 4 * x + 2 * y + c
        sibling = (x, y, 1 - c)
        for hi in range(nh):
            hr, cols = h_refs[hi].shape
            for r0, nr in _row_chunks(hr, cols * 4, SUBLANES):
                _remote(h_refs[hi].at[pl.ds(r0, nr), :], o_refs[hi].at[pl.ds(r0, nr), :],
                        send_sems.at[hi], recv_sems.at[hi], sibling).start()
        small_cps = [pltpu.make_async_copy(s_ref, so_ref.at[me], local_sem)]
        for r in range(1, 8):
            fx, fy, fc = (r >> 2) & 1, (r >> 1) & 1, r & 1
            peer = (1 - x if fx else x, 1 - y if fy else y, 1 - c if fc else c)
            small_cps.append(_remote(s_ref, so_ref.at[me], ssend_sems.at[r - 1], srecv_sems.at[r - 1], peer))
        for cp in small_cps:
            cp.start()
        for hi in range(nh):
            _remote(h_refs[hi], o_refs[hi], send_sems.at[hi], recv_sems.at[hi], sibling).wait()
        for cp in small_cps:
            cp.wait()

    return pl.pallas_call(
        body, in_specs=[ANY] * (nh + 1), out_specs=[ANY] * (nh + 1),
        out_shape=[jax.ShapeDtypeStruct(h.shape, F32) for h in halves]
        + [jax.ShapeDtypeStruct((8, *small.shape), F32)],
        scratch_shapes=[pltpu.SemaphoreType.DMA((nh,)), pltpu.SemaphoreType.DMA((nh,)),
                        pltpu.SemaphoreType.DMA, pltpu.SemaphoreType.DMA((7,)), pltpu.SemaphoreType.DMA((7,))],
        name="final_exchange",
    )(*halves, small)


def _adamw(w, g, m, v, name):
    shape = w.shape
    n = w.size
    if w.ndim >= 2 and shape[-1] >= LANES:
        two_d = (n // shape[-1], shape[-1])
    elif n % LANES == 0:
        two_d = (n // LANES, LANES)
    else:
        two_d = (1, n)
    r, c = two_d
    tr = r
    for cand in (512, 256, 176, 128, 64):
        if r > cand and r % cand == 0 and cand * c <= 256 * 1024:
            tr = cand
            break
    c1 = 1.0 - ADAM_B1 ** ADAM_STEP
    c2 = 1.0 - ADAM_B2 ** ADAM_STEP

    def body(w_ref, g_ref, m_ref, v_ref, d_ref, nm_ref, nv_ref):
        gv = g_ref[...]
        nm = ADAM_B1 * m_ref[...] + (1.0 - ADAM_B1) * gv
        nv = ADAM_B2 * v_ref[...] + (1.0 - ADAM_B2) * (gv * gv)
        d_ref[...] = -ADAM_LR * ((nm / c1) / (jnp.sqrt(nv / c2) + ADAM_EPS) + ADAM_WD * w_ref[...])
        nm_ref[...] = nm
        nv_ref[...] = nv

    spec = pl.BlockSpec((tr, c), lambda i: (i, 0))
    out = jax.ShapeDtypeStruct((r, c), F32)
    d, nm, nv = pl.pallas_call(
        body, grid=(r // tr,), in_specs=[spec] * 4, out_specs=[spec] * 3, out_shape=[out] * 3, name=name,
        compiler_params=_params(("parallel",)),
    )(w.reshape(two_d), g.reshape(two_d), m.reshape(two_d), v.reshape(two_d))
    return d.reshape(shape), nm.reshape(shape), nv.reshape(shape)


def _adamw_many(ws, gs, ms, vs, name):
    n = len(ws)
    c1 = 1.0 - ADAM_B1 ** ADAM_STEP
    c2 = 1.0 - ADAM_B2 ** ADAM_STEP

    def body(*refs):
        w_refs, g_refs, m_refs, v_refs = (refs[k * n:(k + 1) * n] for k in range(4))
        d_refs, nm_refs, nv_refs = (refs[(4 + k) * n:(5 + k) * n] for k in range(3))
        for i in range(n):
            gv = g_refs[i][...]
            nm = ADAM_B1 * m_refs[i][...] + (1.0 - ADAM_B1) * gv
            nv = ADAM_B2 * v_refs[i][...] + (1.0 - ADAM_B2) * (gv * gv)
            d_refs[i][...] = -ADAM_LR * ((nm / c1) / (jnp.sqrt(nv / c2) + ADAM_EPS) + ADAM_WD * w_refs[i][...])
            nm_refs[i][...] = nm
            nv_refs[i][...] = nv

    vmem = pl.BlockSpec(memory_space=pltpu.VMEM)
    shapes = [jax.ShapeDtypeStruct(t.shape, F32) for t in ws]
    outs = pl.pallas_call(body, in_specs=[vmem] * (4 * n), out_specs=[vmem] * (3 * n), out_shape=shapes * 3,
                          name=name, compiler_params=_params(vmem_mb=56))(*ws, *gs, *ms, *vs)
    return outs[:n], outs[n:2 * n], outs[2 * n:]


BIG = ("w_in", "w_glu", "w_out", "w_up", "w_down")
WEIGHTS = ("norm_mix_g", "w_in", "a_re", "a_im", "log_step", "b_re", "b_im", "c_re", "c_im", "d_skip", "w_glu",
           "sink", "norm_attn_g", "norm_ssm_g", "w_out", "norm_ffn_g", "w_up", "conv_w", "conv_b", "w_down",
           "norm_final_g")
SMALL = ("norm_mix_g", "a_re", "a_im", "log_step", "b_re", "b_im", "c_re", "c_im", "d_skip", "sink",
         "norm_attn_g", "norm_ssm_g", "norm_ffn_g", "conv_w", "conv_b", "norm_final_g")
SMALL_ROWS = 40
N_DEV = 8


def _by_owner(name, g):
    if name == "w_up":
        return g
    return g.reshape(4, g.shape[0] // 4, g.shape[1])


def _view(name, t):
    if name == "w_in":
        return jnp.swapaxes(t[0], 0, 1)
    if name in ("b_re", "b_im"):
        return jnp.swapaxes(t, -1, -2)
    return t


def _unview(name, t):
    if name == "w_in":
        return jnp.swapaxes(t, 0, 1)[None]
    if name in ("b_re", "b_im"):
        return jnp.swapaxes(t, -1, -2)
    return t


def kernel(x, norm_mix_g, w_in, a_re, a_im, log_step, b_re, b_im, c_re, c_im, d_skip, w_glu, sink, norm_attn_g, norm_ssm_g, w_out, norm_ffn_g, w_up, conv_w, conv_b, w_down, norm_final_g, loss_target, m_norm_mix_g, m_w_in, m_a_re, m_a_im, m_log_step, m_b_re, m_b_im, m_c_re, m_c_im, m_d_skip, m_w_glu, m_sink, m_norm_attn_g, m_norm_ssm_g, m_w_out, m_norm_ffn_g, m_w_up, m_conv_w, m_conv_b, m_w_down, m_norm_final_g, v_norm_mix_g, v_w_in, v_a_re, v_a_im, v_log_step, v_b_re, v_b_im, v_c_re, v_c_im, v_d_skip, v_w_glu, v_sink, v_norm_attn_g, v_norm_ssm_g, v_w_out, v_norm_ffn_g, v_w_up, v_conv_w, v_conv_b, v_w_down, v_norm_final_g):
    given = dict(locals())
    w = {n: given[n] for n in WEIGHTS}
    m = {n: given["m_" + n] for n in WEIGHTS}
    v = {n: given["v_" + n] for n in WEIGHTS}
    xy = 2 * lax.axis_index("x") + lax.axis_index("y")

    core = lax.axis_index("c")
    place = jnp.stack([core, xy]).astype(jnp.int32)

    conv_rows = jnp.pad(w["conv_w"][0], ((0, 2 * SUBLANES - 3), (0, 0)))
    rows = lambda t: t.reshape(4 * t.shape[1], t.shape[2])
    (w_in_all,) = _gather_weights([_view("w_in", w["w_in"])], [BF16])
    wb = {"w_in": rows(w_in_all)}
    early = ("w_in", "w_glu", "w_out")
    mixer = [_cast_place(w[n][0], place, BF16, w_in_all, "cast_" + n) for n in ("w_glu", "w_out")]
    mixer.append(_cast_place(conv_rows, place, F32, w_in_all, "cast_conv_w"))
    *mixer_flight, mixer_token = _spread_start(mixer, "spread_mixer_start")
    late = ("w_up", "w_down")
    *late_flight, token = _spread_start(
        [_cast_place(w[n][0], place, BF16, mixer_token, "cast_" + n) for n in late], "spread_ffn_start")

    def mixer_weights(after):
        w_glu4, w_out4, conv4 = _spread_wait(*mixer_flight, after, "spread_mixer_wait")
        return {"w_glu": rows(w_glu4), "w_out": rows(w_out4),
                "conv_w": conv4[:, :3].transpose(1, 0, 2).reshape(3, 2 * D_FF)}

    def late_weights(after):
        w_up4, w_down4 = _spread_wait(*late_flight, after, "spread_ffn_wait")
        return {"w_up": w_up4, "w_down": rows(w_down4)}

    sp = {n: w[n][0] for n in ("a_re", "a_im", "log_step", "b_re", "b_im", "c_re", "c_im", "d_skip",
                               "norm_mix_g", "norm_attn_g", "norm_ssm_g", "norm_ffn_g", "sink", "conv_b")}
    for n in ("norm_mix_g", "norm_attn_g", "norm_ssm_g", "norm_ffn_g", "sink", "conv_b"):
        sp[n] = sp[n].reshape(1, -1)
    sp["norm_mix_g"] = sp["norm_mix_g"] + token[:1, :1]
    sp["norm_final_g"] = w["norm_final_g"]
    flight = {}

    def ffn_grads_ready(dw_up, dw_down):
        *flight["pair"], token = _pair_start([dw_up, _by_owner("w_down", dw_down)])
        return token[:1, :1]

    def ffn_grads_next(after):
        mine, got = _pair_wait(*flight["pair"], after)
        sums, flight["own"] = zip(*[_pair_sum(a, b, place, BF16, "pair_sum_" + n) for n, a, b in zip(late, mine, got)])
        *flight["chip"], token = _chip_start(list(sums))
        return token[:1, :1]

    loss, grad_x, g = _local_step(x[0], loss_target[0], wb, sp, mixer_weights, late_weights, ffn_grads_ready,
                                  ffn_grads_next)

    flat = jnp.concatenate([g[n].reshape(-1) for n in SMALL] + [loss.reshape(-1)])
    pad = N_DEV * SMALL_ROWS * D_MODEL - flat.shape[0]
    small = jnp.concatenate([flat, jnp.zeros((pad,), F32)]).reshape(4, 2 * SMALL_ROWS, D_MODEL)
    by_owner = [_by_owner(n, g[n]) for n in early] + [small]
    got = _pair_exchange(by_owner)
    transit = [BF16] * len(early) + [F32]
    chip_sums, own_sums = zip(*[_pair_sum(a, b, place, t, "pair_sum_" + n)
                                for n, a, b, t in zip(early + ("small",), by_owner, got, transit)])
    landed = _chip_exchange(list(chip_sums))
    halves = {n: _chip_sum(o, t, "chip_sum_" + n) for n, o, t in zip(early + ("small",), own_sums, landed)}
    late_landed = _chip_wait(*flight["chip"], grad_x)
    for n, o, t in zip(late, flight["own"], late_landed):
        halves[n] = _chip_sum(o, t, "chip_sum_" + n)
    *others, small_all = _final_exchange([halves[n] for n in BIG], halves["small"])
    grads = {n: jnp.concatenate([jnp.where(core == 0, halves[n], o), jnp.where(core == 0, o, halves[n])], axis=0)
             for n, o in zip(BIG, others)}
    flat = small_all.reshape(-1)
    off = 0
    for n in SMALL:
        shape = (3, 4 * w[n].shape[-1]) if n == "conv_w" else w[n].shape[1:] if n != "norm_final_g" else w[n].shape
        size = math.prod(shape)
        grads[n] = flat[off:off + size].reshape(shape)
        off += size
    loss = flat[off]
    cw = w["conv_w"].shape[-1]
    grads["conv_w"] = lax.dynamic_slice_in_dim(grads["conv_w"], xy * cw, cw, axis=1)
    grads = {n: grads[n] if n == "w_in" else _view(n, grads[n].reshape(w[n].shape)) for n in WEIGHTS}
    wv, mv, vv = ({n: _view(n, t[n]) for n in WEIGHTS} for t in (w, m, v))

    delta, new_m, new_v = {}, {}, {}
    for n in BIG:
        delta[n], new_m[n], new_v[n] = _adamw(wv[n], grads[n].reshape(wv[n].shape), mv[n], vv[n], "adamw_" + n)
    for group, name in ((("b_re", "b_im"), "adamw_b"), (tuple(n for n in SMALL if n not in ("b_re", "b_im")), "adamw_small")):
        row = lambda t: t.reshape(1, -1) if t.ndim == 1 else t
        d_, m_, v_ = _adamw_many(*[[row(t[n]) for n in group] for t in (wv, grads, mv, vv)], name)
        for n, dn, mn, vn in zip(group, d_, m_, v_):
            delta[n], new_m[n], new_v[n] = (t.reshape(wv[n].shape) for t in (dn, mn, vn))
    natural = lambda t: [_unview(n, t[n].reshape(wv[n].shape)) for n in WEIGHTS]
    return (loss, grad_x[None], *natural(grads), *natural(delta), *natural(new_m), *natural(new_v))
```

```python
import functools
import math

import jax
import jax.numpy as jnp
from jax import lax
from jax.experimental import pallas as pl
from jax.experimental.pallas import tpu as pltpu

F32 = jnp.float32
BF16 = jnp.bfloat16

D_MODEL = 1024
N_Q_HEADS = 8
N_KV_HEADS = 2
HEAD_DIM = 64
ATTN_WIDTH = 512
KV_WIDTH = 128
QKV_WIDTH = ATTN_WIDTH + 2 * KV_WIDTH
WINDOW = 128
BLOCK = 128
ROPE_DIM = 16
ROPE_THETA = 500000.0
SSM_WIDTH = 512
SSM_GROUP = 16
N_SSM_GROUPS = 32
SSM_STATE = 64
IN_WIDTH = 1280
D_FF = 2816
EPS = 1e-6
ADAM_LR = 0.001
ADAM_B1 = 0.9
ADAM_B2 = 0.999
ADAM_EPS = 1e-08
ADAM_WD = 0.01
ADAM_STEP = 10

VMEM_BYTES_V7X = 64 * 1024 * 1024
SUBLANES = 8
LANES = 128
SSM_CB = 4
SSM_CH = 128
SSM_ST = 512
N_SEG = SUBLANES

NN = (((1,), (0,)), ((), ()))
NT = (((1,), (1,)), ((), ()))
TN = (((0,), (0,)), ((), ()))


def _params(sem=None, vmem_mb=48):
    limit = vmem_mb * 1024 * 1024
    assert limit < VMEM_BYTES_V7X
    return pltpu.CompilerParams(dimension_semantics=sem, vmem_limit_bytes=limit)


def _dg(a, b, dims):
    return lax.dot_general(a, b, dims, preferred_element_type=F32)


def _sigmoid(x):
    return 1.0 / (1.0 + jnp.exp(-x))


_SQRT_HALF = 0.7071067811865476
_INV_SQRT_2PI = 0.3989422804014327


def _gelu(x):
    return 0.5 * x * (1.0 + lax.erf(x * _SQRT_HALF))


def _gelu_grad(x):
    return 0.5 * (1.0 + lax.erf(x * _SQRT_HALF)) + x * (_INV_SQRT_2PI * jnp.exp(-0.5 * x * x))


def _mm_tn(a, b, tm, tn, name):
    k, m = a.shape
    n = b.shape[1]

    def body(a_ref, b_ref, o_ref):
        o_ref[...] = _dg(a_ref[...], b_ref[...], TN)

    return pl.pallas_call(
        body, grid=(m // tm, n // tn),
        in_specs=[pl.BlockSpec((k, tm), lambda i, j: (0, i)), pl.BlockSpec((k, tn), lambda i, j: (0, j))],
        out_specs=pl.BlockSpec((tm, tn), lambda i, j: (i, j)),
        out_shape=jax.ShapeDtypeStruct((m, n), F32), name=name,
        compiler_params=_params(("parallel", "parallel")),
    )(a, b)


def _mm_nn_cols(a, b4, tm, name):
    m, k = a.shape
    s, _, n = b4.shape

    def body(a_ref, b_ref, o_ref):
        o_ref[...] = _dg(a_ref[...], b_ref[...], NN)

    return pl.pallas_call(
        body, grid=(m // tm, s),
        in_specs=[pl.BlockSpec((tm, k), lambda i, j: (i, 0)), pl.BlockSpec((None, k, n), lambda i, j: (j, 0, 0))],
        out_specs=pl.BlockSpec((tm, n), lambda i, j: (i, j)),
        out_shape=jax.ShapeDtypeStruct((m, s * n), F32), name=name,
        compiler_params=_params(("parallel", "parallel")),
    )(a, b4)


def _mm_tn_cols(a, b2, s, tm, name):
    k, m = a.shape
    h, _, wide = b2.shape
    per = s // h
    n = wide // per

    def body(a_ref, b_ref, o_ref):
        o_ref[...] = _dg(a_ref[...], b_ref[...], TN)

    return pl.pallas_call(
        body, grid=(s, m // tm),
        in_specs=[pl.BlockSpec((k, tm), lambda j, i: (0, i)),
                  pl.BlockSpec((None, k, n), lambda j, i: (j // per, 0, j % per))],
        out_specs=pl.BlockSpec((None, tm, n), lambda j, i: (j, i, 0)),
        out_shape=jax.ShapeDtypeStruct((s, m, n), F32), name=name,
        compiler_params=_params(("parallel", "parallel")),
    )(a, b2)


TM_EW = 256


def _rms_bwd_vals(xv, gv, dy):
    r = lax.rsqrt(jnp.mean(xv * xv, axis=-1, keepdims=True) + EPS)
    xh = xv * r
    dxh = dy * gv
    dx = r * (dxh - xh * jnp.mean(dxh * xh, axis=-1, keepdims=True))
    return dx, dy * xh


TM_FUSED = 256


def _rms_vals(xv, gv):
    return xv * lax.rsqrt(jnp.mean(xv * xv, axis=-1, keepdims=True) + EPS) * gv


def _rope_blocks(src, dst, c, lo, hi):
    nq = ATTN_WIDTH // LANES
    for blk in range(nq + 1):
        t = src[:, blk * LANES:(blk + 1) * LANES]
        dst[:, blk * LANES:(blk + 1) * LANES] = (
            t * c + pltpu.roll(t, LANES - 8, 1) * lo + pltpu.roll(t, 8, 1) * hi).astype(BF16)
    dst[:, (nq + 1) * LANES:] = src[:, (nq + 1) * LANES:].astype(BF16)


def _rms_mm_rope(x, g, wt, tabs, name):
    l, d = x.shape
    n = wt.shape[0]

    def body(x_ref, g_ref, w_ref, c_ref, lo_ref, hi_ref, h_ref, qkv_ref, u_ref):
        h = _rms_vals(x_ref[...], g_ref[...]).astype(BF16)
        h_ref[...] = h
        out = _dg(h, w_ref[...], NT)
        _rope_blocks(out[:, :QKV_WIDTH], qkv_ref, c_ref[...], lo_ref[...], hi_ref[...])
        u_ref[...] = out[:, QKV_WIDTH:]

    row = lambda width: pl.BlockSpec((TM_FUSED, width), lambda i: (i, 0))
    return pl.pallas_call(
        body, grid=(l // TM_FUSED,),
        in_specs=[row(d), pl.BlockSpec((1, d), lambda i: (0, 0)), pl.BlockSpec((n, d), lambda i: (0, 0)),
                  row(LANES), row(LANES), row(LANES)],
        out_specs=[row(d), row(QKV_WIDTH), row(n - QKV_WIDTH)],
        out_shape=[jax.ShapeDtypeStruct((l, d), BF16), jax.ShapeDtypeStruct((l, QKV_WIDTH), BF16),
                   jax.ShapeDtypeStruct((l, n - QKV_WIDTH), F32)],
        name=name, compiler_params=_params(("parallel",)),
    )(x, g, wt, *tabs)


def _mix_mm_res_rms(attn, ys, g_attn, g_ssm, b, res, g, name):
    l, w = attn.shape
    d = b.shape[1]

    def body(a_ref, y_ref, ga_ref, gs_ref, b_ref, r_ref, g_ref, m_ref, x_ref, h_ref):
        m_ref[:, :w] = _rms_vals(a_ref[...], ga_ref[...]).astype(BF16)
        m_ref[:, w:] = _rms_vals(y_ref[...], gs_ref[...]).astype(BF16)
        xv = r_ref[...] + _dg(m_ref[...], b_ref[...], NN)
        x_ref[...] = xv
        h_ref[...] = _rms_vals(xv, g_ref[...]).astype(BF16)

    row = lambda width: pl.BlockSpec((TM_FUSED, width), lambda i: (i, 0))
    vec = lambda width: pl.BlockSpec((1, width), lambda i: (0, 0))
    return pl.pallas_call(
        body, grid=(l // TM_FUSED,),
        in_specs=[row(w), row(w), vec(w), vec(w), pl.BlockSpec((2 * w, d), lambda i: (0, 0)), row(d), vec(d)],
        out_specs=[row(2 * w), row(d), row(d)],
        out_shape=[jax.ShapeDtypeStruct((l, 2 * w), BF16), jax.ShapeDtypeStruct((l, d), F32),
                   jax.ShapeDtypeStruct((l, d), BF16)],
        name=name, compiler_params=_params(("parallel",)),
    )(attn, ys, g_attn, g_ssm, b, res, g)


def _mm_res_loss(a, b, res, g, target):
    l, k = a.shape
    d = b.shape[1]

    def body(a_ref, b_ref, r_ref, g_ref, t_ref, loss_ref, dx_ref, dxb_ref, dg_ref):
        xv = r_ref[...] + _dg(a_ref[...], b_ref[...], NN)
        gv = g_ref[...]
        r = lax.rsqrt(jnp.mean(xv * xv, axis=-1, keepdims=True) + EPS)
        xh = xv * r
        e = xh * gv - t_ref[...]
        part = jnp.sum(jnp.sum(e * e, axis=1, keepdims=True), axis=0, keepdims=True) * (0.5 / d)
        dy = e * (1.0 / d)
        dxh = dy * gv
        dx = r * (dxh - xh * jnp.mean(dxh * xh, axis=-1, keepdims=True))
        dx_ref[...] = dx
        dxb_ref[...] = dx.astype(BF16)

        @pl.when(pl.program_id(0) == 0)
        def _():
            dg_ref[...] = jnp.zeros_like(dg_ref)
            loss_ref[...] = jnp.zeros_like(loss_ref)

        dg_ref[...] += jnp.sum(dy * xh, axis=0, keepdims=True)
        loss_ref[...] += part

    row = lambda width: pl.BlockSpec((TM_FUSED, width), lambda i: (i, 0))
    vec = pl.BlockSpec((1, d), lambda i: (0, 0))
    return pl.pallas_call(
        body, grid=(l // TM_FUSED,),
        in_specs=[row(k), pl.BlockSpec((k, d), lambda i: (0, 0)), row(d), vec, row(d)],
        out_specs=[pl.BlockSpec((1, 1), lambda i: (0, 0)), row(d), row(d), vec],
        out_shape=[jax.ShapeDtypeStruct((1, 1), F32), jax.ShapeDtypeStruct((l, d), F32),
                   jax.ShapeDtypeStruct((l, d), BF16), jax.ShapeDtypeStruct((1, d), F32)],
        name="mm_down_loss", compiler_params=_params(("arbitrary",)),
    )(a, b, res, g, target)


def _mm_rms_bwd(a, b, a_spec, b_spec, matmul, x, g, res, name):
    l, d = x.shape

    def body(a_ref, b_ref, x_ref, g_ref, res_ref, dx_ref, dxb_ref, dg_ref):
        dx, dgr = _rms_bwd_vals(x_ref[...], g_ref[...], matmul(a_ref, b_ref))
        dx = dx + res_ref[...]
        dx_ref[...] = dx
        dxb_ref[...] = dx.astype(BF16)

        @pl.when(pl.program_id(0) == 0)
        def _():
            dg_ref[...] = jnp.zeros_like(dg_ref)

        dg_ref[...] += jnp.sum(dgr, axis=0, keepdims=True)

    row = pl.BlockSpec((TM_FUSED, d), lambda i: (i, 0))
    vec = pl.BlockSpec((1, d), lambda i: (0, 0))
    return pl.pallas_call(
        body, grid=(l // TM_FUSED,), in_specs=[a_spec, b_spec, row, vec, row], out_specs=[row, row, vec],
        out_shape=[jax.ShapeDtypeStruct((l, d), F32), jax.ShapeDtypeStruct((l, d), BF16),
                   jax.ShapeDtypeStruct((1, d), F32)],
        name=name, compiler_params=_params(("arbitrary",)),
    )(a, b, x, g, res)


def _mm_nn_rms_bwd(a, b, x, g, res, name):
    return _mm_rms_bwd(a, b, pl.BlockSpec((TM_FUSED, a.shape[1]), lambda i: (i, 0)),
                       pl.BlockSpec(b.shape, lambda i: (0, 0)),
                       lambda a_ref, b_ref: _dg(a_ref[...], b_ref[...], NN), x, g, res, name)


def _mm_cols_rms_bwd(a2, b4, x, g, res, name):
    h, _, wide = a2.shape
    s, _, n = b4.shape
    per = s // h

    def matmul(a_ref, b_ref):
        acc = None
        for j in range(s):
            part = _dg(a_ref[j // per, :, (j % per) * n:(j % per + 1) * n], b_ref[j], NT)
            acc = part if acc is None else acc + part
        return acc

    return _mm_rms_bwd(a2, b4, pl.BlockSpec((h, TM_FUSED, wide), lambda i: (0, i, 0)),
                       pl.BlockSpec(b4.shape, lambda i: (0, 0, 0)), matmul, x, g, res, name)


def _mm_mix_bwd(dx, b, attn, ys, g_attn, g_ssm, name):
    l, w = attn.shape
    d = dx.shape[1]

    def body(dx_ref, b_ref, a_ref, y_ref, ga_ref, gs_ref, da_ref, dy_ref, dga_ref, dgs_ref):
        @pl.when(pl.program_id(0) == 0)
        def _():
            dga_ref[...] = jnp.zeros_like(dga_ref)
            dgs_ref[...] = jnp.zeros_like(dgs_ref)

        dm = _dg(dx_ref[...], b_ref[...], NT)
        for src, gr, off, dst, dgr in ((a_ref, ga_ref, 0, da_ref, dga_ref), (y_ref, gs_ref, w, dy_ref, dgs_ref)):
            dxv, dg_rows = _rms_bwd_vals(src[...], gr[...], dm[:, off:off + w])
            dst[...] = dxv
            dgr[...] += jnp.sum(dg_rows, axis=0, keepdims=True)

    row = lambda width: pl.BlockSpec((TM_FUSED, width), lambda i: (i, 0))
    vec = pl.BlockSpec((1, w), lambda i: (0, 0))
    return pl.pallas_call(
        body, grid=(l // TM_FUSED,),
        in_specs=[row(d), pl.BlockSpec((2 * w, d), lambda i: (0, 0)), row(w), row(w), vec, vec],
        out_specs=[row(w), row(w), vec, vec],
        out_shape=[jax.ShapeDtypeStruct((l, w), F32), jax.ShapeDtypeStruct((l, w), F32),
                   jax.ShapeDtypeStruct((1, w), F32), jax.ShapeDtypeStruct((1, w), F32)],
        name=name, compiler_params=_params(("arbitrary",)),
    )(dx, b, attn, ys, g_attn, g_ssm)


def _rope_tables(l):
    half = ROPE_DIM // 2
    inv_freq = jnp.power(ROPE_THETA, -jnp.arange(half, dtype=F32) / half)
    ang = jnp.arange(l, dtype=F32)[:, None] * inv_freq[None, :]
    cos, sin = jnp.cos(ang), jnp.sin(ang)
    ones = jnp.ones((l, HEAD_DIM - ROPE_DIM), F32)
    zeros = jnp.zeros((l, HEAD_DIM - ROPE_DIM), F32)
    zh = jnp.zeros((l, half), F32)
    c = jnp.concatenate([cos, cos, ones], axis=1)
    s_lo = jnp.concatenate([-sin, zh, zeros], axis=1)
    s_hi = jnp.concatenate([zh, sin, zeros], axis=1)
    return tuple(jnp.tile(t, (1, LANES // HEAD_DIM)) for t in (c, s_lo, s_hi))


def _rope_bwd(dq, dkv, du_ssm, dpre, d_skip, tabs):
    l = dq.shape[0]
    nq = ATTN_WIDTH // LANES

    def body(dq_ref, dkv_ref, du_ref, dpre_ref, ds_ref, c_ref, lo_ref, hi_ref, o_ref):
        c, lo, hi = c_ref[...], lo_ref[...], hi_ref[...]
        for blk in range(nq + 1):
            t = dq_ref[:, blk * LANES:(blk + 1) * LANES] if blk < nq else dkv_ref[:, :KV_WIDTH]
            g = t * c + pltpu.roll(t * lo, 8, 1) + pltpu.roll(t * hi, LANES - 8, 1)
            o_ref[:, blk * LANES:(blk + 1) * LANES] = g.astype(BF16)
        o_ref[:, (nq + 1) * LANES:QKV_WIDTH] = dkv_ref[:, KV_WIDTH:].astype(BF16)
        o_ref[:, QKV_WIDTH:] = (du_ref[...] + dpre_ref[...] * ds_ref[...]).astype(BF16)

    tab = pl.BlockSpec((TM_EW, LANES), lambda i: (i, 0))
    wide = pl.BlockSpec((TM_EW, SSM_WIDTH), lambda i: (i, 0))
    return pl.pallas_call(
        body, grid=(l // TM_EW,),
        in_specs=[wide, pl.BlockSpec((TM_EW, 2 * KV_WIDTH), lambda i: (i, 0)), wide, wide,
                  pl.BlockSpec((1, SSM_WIDTH), lambda i: (0, 0)), tab, tab, tab],
        out_specs=pl.BlockSpec((TM_EW, IN_WIDTH), lambda i: (i, 0)),
        out_shape=jax.ShapeDtypeStruct((l, IN_WIDTH), BF16), name="rope_bwd",
        compiler_params=_params(("parallel",)),
    )(dq, dkv, du_ssm, dpre, d_skip, *tabs)


_Q_COLS = ATTN_WIDTH // LANES
_SCALE = HEAD_DIM ** -0.5
_NEG = -1e30


def _window_specs(nb, width, col):
    return [
        pl.BlockSpec((BLOCK, width), lambda n: (jnp.maximum(n - 1, 0), col)),
        pl.BlockSpec((BLOCK, width), lambda n: (n, col)),
        pl.BlockSpec((BLOCK, width), lambda n: (jnp.minimum(n + 1, nb - 1), col)),
    ]


def _stacked_sink(sink_ref, heads):
    rid = lax.broadcasted_iota(jnp.int32, (len(heads) * BLOCK, 1), 0)
    sk = jnp.full(rid.shape, sink_ref[0, heads[-1]], F32)
    for g in range(len(heads) - 2, -1, -1):
        sk = jnp.where(rid < (g + 1) * BLOCK, sink_ref[0, heads[g]], sk)
    return sk


def _attn_fwd(qkv, sink):
    l = qkv.shape[0]
    nb = l // BLOCK
    grp = N_Q_HEADS // N_KV_HEADS

    def body(sink_ref, q_ref, k0, k1, k2, v0, v1, v2, o_ref, lse_ref):
        n = pl.program_id(0)
        q = q_ref[...]
        kw = jnp.concatenate([k0[...], k1[...], k2[...]], axis=0)
        vw = jnp.concatenate([v0[...], v1[...], v2[...]], axis=0)
        row = lax.broadcasted_iota(jnp.int32, (grp * BLOCK, 3 * BLOCK), 0)
        col = lax.broadcasted_iota(jnp.int32, (grp * BLOCK, 3 * BLOCK), 1)
        valid = jnp.abs(col - BLOCK - (row & (BLOCK - 1))) <= WINDOW
        valid &= jnp.logical_not((n == 0) & (col < BLOCK))
        valid &= jnp.logical_not((n == nb - 1) & (col >= 2 * BLOCK))
        for hk in range(N_KV_HEADS):
            heads = range(hk * grp, (hk + 1) * grp)
            qs = jnp.concatenate([q[:, h * HEAD_DIM:(h + 1) * HEAD_DIM] for h in heads], axis=0)
            kh = kw[:, hk * HEAD_DIM:(hk + 1) * HEAD_DIM]
            vh = vw[:, hk * HEAD_DIM:(hk + 1) * HEAD_DIM]
            s = jnp.where(valid, _dg(qs, kh, NT) * _SCALE, _NEG)
            sk = _stacked_sink(sink_ref, heads)
            m = jnp.maximum(jnp.max(s, axis=1, keepdims=True), sk)
            p = jnp.exp(s - m)
            denom = jnp.sum(p, axis=1, keepdims=True) + jnp.exp(sk - m)
            o = _dg((p / denom).astype(BF16), vh, NN)
            lse = m + jnp.log(denom)
            for g, h in enumerate(heads):
                o_ref[:, h * HEAD_DIM:(h + 1) * HEAD_DIM] = o[g * BLOCK:(g + 1) * BLOCK]
                lse_ref[:, h:h + 1] = lse[g * BLOCK:(g + 1) * BLOCK]

    return pl.pallas_call(
        body, grid=(nb,),
        in_specs=[pl.BlockSpec(memory_space=pltpu.SMEM),
                  pl.BlockSpec((BLOCK, ATTN_WIDTH), lambda n: (n, 0))]
        + _window_specs(nb, KV_WIDTH, _Q_COLS) + _window_specs(nb, KV_WIDTH, _Q_COLS + 1),
        out_specs=[pl.BlockSpec((BLOCK, ATTN_WIDTH), lambda n: (n, 0)),
                   pl.BlockSpec((BLOCK, N_Q_HEADS), lambda n: (n, 0))],
        out_shape=[jax.ShapeDtypeStruct((l, ATTN_WIDTH), F32), jax.ShapeDtypeStruct((l, N_Q_HEADS), F32)],
        name="attn_fwd", compiler_params=_params(("parallel",)),
    )(sink, qkv, qkv, qkv, qkv, qkv, qkv, qkv)


def _attn_bwd(qkv, attn, dattn, lse, sink):
    l = qkv.shape[0]
    nb = l // BLOCK
    grp = N_Q_HEADS // N_KV_HEADS
    win = 3 * BLOCK

    def body(sink_ref, q_ref, k0, k1, k2, v0, v1, v2, o_ref, d_ref, l_ref, dq_ref, dkv_ref, dsink_ref, ring_ref):
        n = pl.program_id(0)

        @pl.when(n == 0)
        def _():
            dsink_ref[...] = jnp.zeros_like(dsink_ref)
            ring_ref[...] = jnp.zeros_like(ring_ref)

        @pl.when(n < nb)
        def _():
            first, last = n == 0, n == nb - 1
            cat = lambda a, b, c: jnp.concatenate([a[...], b[...], c[...]], axis=0)
            q, kw, vw = q_ref[...], cat(k0, k1, k2), cat(v0, v1, v2)
            dov = d_ref[...]
            prod = o_ref[...] * dov
            dob = dov.astype(BF16)
            lse = l_ref[...]
            row = lax.broadcasted_iota(jnp.int32, (grp * BLOCK, win), 0)
            col = lax.broadcasted_iota(jnp.int32, (grp * BLOCK, win), 1)
            valid = jnp.abs(col - BLOCK - (row & (BLOCK - 1))) <= WINDOW
            valid &= jnp.logical_not(first & (col < BLOCK))
            valid &= jnp.logical_not(last & (col >= 2 * BLOCK))

            dsink_parts, dks, dvs = [], [], []
            for hk in range(N_KV_HEADS):
                heads = range(hk * grp, (hk + 1) * grp)
                ksl = slice(hk * HEAD_DIM, (hk + 1) * HEAD_DIM)
                hsl = [slice(h * HEAD_DIM, (h + 1) * HEAD_DIM) for h in heads]
                stack = lambda parts: jnp.concatenate(parts, axis=0)
                qs = stack([q[:, s_] for s_ in hsl])
                dos = stack([dob[:, s_] for s_ in hsl])
                deltas = stack([jnp.sum(prod[:, s_], axis=1, keepdims=True) for s_ in hsl])
                lses = stack([lse[:, h:h + 1] for h in heads])
                kh, vh = kw[:, ksl], vw[:, ksl]
                s = jnp.where(valid, _dg(qs, kh, NT) * _SCALE, _NEG)
                p = jnp.exp(s - lses)
                dp = _dg(dos, vh, NT)
                ds = (p * (dp - deltas) * _SCALE).astype(BF16)
                dq = _dg(ds, kh, NN)
                sink_rows = jnp.exp(_stacked_sink(sink_ref, heads) - lses) * deltas
                for g in range(grp):
                    dq_ref[:, hsl[g]] = dq[g * BLOCK:(g + 1) * BLOCK]
                    dsink_parts.append(jnp.sum(sink_rows[g * BLOCK:(g + 1) * BLOCK], axis=0, keepdims=True))
                dks.append(_dg(ds, qs, TN))
                dvs.append(_dg(p.astype(BF16), dos, TN))
            dsink_ref[...] -= jnp.concatenate(dsink_parts, axis=1)
            part = jnp.concatenate(dks + dvs, axis=1)
            ring_ref[(n + 2) % 3] += part[0:BLOCK]
            ring_ref[n % 3] += part[BLOCK:2 * BLOCK]
            ring_ref[(n + 1) % 3] = part[2 * BLOCK:]

        @pl.when(n >= 1)
        def _():
            dkv_ref[...] = ring_ref[(n + 2) % 3]

    centre = lambda n: jnp.minimum(n, nb - 1)
    window = lambda width, col: [
        pl.BlockSpec((BLOCK, width), lambda n: (jnp.maximum(centre(n) - 1, 0), col)),
        pl.BlockSpec((BLOCK, width), lambda n: (centre(n), col)),
        pl.BlockSpec((BLOCK, width), lambda n: (jnp.minimum(centre(n) + 1, nb - 1), col))]
    own = lambda width: pl.BlockSpec((BLOCK, width), lambda n: (centre(n), 0))
    return pl.pallas_call(
        body, grid=(nb + 1,),
        in_specs=[pl.BlockSpec(memory_space=pltpu.SMEM), own(ATTN_WIDTH)]
        + window(KV_WIDTH, _Q_COLS) + window(KV_WIDTH, _Q_COLS + 1)
        + [own(ATTN_WIDTH), own(ATTN_WIDTH), own(N_Q_HEADS)],
        out_specs=[own(ATTN_WIDTH), pl.BlockSpec((BLOCK, 2 * KV_WIDTH), lambda n: (jnp.maximum(n - 1, 0), 0)),
                   pl.BlockSpec((1, N_Q_HEADS), lambda n: (0, 0))],
        out_shape=[jax.ShapeDtypeStruct((l, ATTN_WIDTH), F32), jax.ShapeDtypeStruct((l, 2 * KV_WIDTH), F32),
                   jax.ShapeDtypeStruct((1, N_Q_HEADS), F32)],
        scratch_shapes=[pltpu.VMEM((3, BLOCK, 2 * KV_WIDTH), F32)],
        name="attn_bwd", compiler_params=_params(("arbitrary",)),
    )(sink, qkv, qkv, qkv, qkv, qkv, qkv, qkv, attn, dattn, lse)


def _ssm_disc(a_re, a_im, log_step, b_re, b_im):
    step = jnp.exp(log_step)[..., None]
    mag = jnp.exp(a_re * step)
    lb_re, lb_im = mag * jnp.cos(a_im * step), mag * jnp.sin(a_im * step)
    nr, ni = lb_re - 1.0, lb_im
    den = a_re * a_re + a_im * a_im
    f_re = ((nr * a_re + ni * a_im) / den)[..., None]
    f_im = ((ni * a_re - nr * a_im) / den)[..., None]
    return lb_re, lb_im, f_re * b_re - f_im * b_im, f_re * b_im + f_im * b_re


def _ssm_pack(lb_re, lb_im, bb_re, bb_im, c_re, c_im):
    eye = jnp.eye(SSM_CH // SSM_GROUP, dtype=F32)
    ng = SSM_CH // SSM_GROUP

    def diag_b(bb):
        t = bb.reshape(2, SSM_CB, ng, SSM_STATE, SSM_GROUP)
        return jnp.einsum('dkgpc,gh->dkgchp', t, eye).reshape(2, SSM_CB, SSM_CH, SSM_ST)

    def diag_c(cc):
        t = cc.reshape(2, SSM_CB, ng, SSM_GROUP, SSM_STATE)
        return jnp.einsum('dkgcp,gh->dkhpgc', t, eye).reshape(2, SSM_CB, SSM_ST, SSM_CH)

    bcat = jnp.concatenate([diag_b(bb_re), diag_b(bb_im)], axis=-1)
    ccat = jnp.concatenate([diag_c(c_re), -diag_c(c_im)], axis=-2)
    lam_re = lb_re.reshape(2, SSM_CB, 1, SSM_ST)
    lam_im = lb_im.reshape(2, SSM_CB, 1, SSM_ST)
    return bcat, ccat, lam_re, lam_im


def _ssm_unpack(dbcat, dccat, dlam_re, dlam_im):
    ng = SSM_CH // SSM_GROUP
    eye = jnp.eye(ng, dtype=F32)

    def undiag_b(t):
        t = t.reshape(2, SSM_CB, ng, SSM_GROUP, ng, SSM_STATE)
        return jnp.einsum('dkgchp,gh->dkgpc', t, eye).reshape(2, N_SSM_GROUPS, SSM_STATE, SSM_GROUP)

    def undiag_c(t):
        t = t.reshape(2, SSM_CB, ng, SSM_STATE, ng, SSM_GROUP)
        return jnp.einsum('dkhpgc,gh->dkgcp', t, eye).reshape(2, N_SSM_GROUPS, SSM_GROUP, SSM_STATE)

    dbb_re, dbb_im = undiag_b(dbcat[..., :SSM_ST]), undiag_b(dbcat[..., SSM_ST:])
    dc_re, dc_im = undiag_c(dccat[:, :, :SSM_ST]), -undiag_c(dccat[:, :, SSM_ST:])
    shape = (2, N_SSM_GROUPS, SSM_STATE)
    return dlam_re.reshape(shape), dlam_im.reshape(shape), dbb_re, dbb_im, dc_re, dc_im


def _to_segments(t):
    l, w = t.shape
    return t.reshape(N_SEG, l // N_SEG, w).transpose(1, 0, 2).reshape(l, w)


def _from_segments(t):
    l, w = t.shape
    return t.reshape(l // N_SEG, N_SEG, w).transpose(1, 0, 2).reshape(l, w)


SSM_RC = 256
SSM_JC = SSM_RC // N_SEG
_RE, _IM = pl.ds(0, SSM_ST), pl.ds(SSM_ST, SSM_ST)


def _cfma(ar, ai, xr, xi, br, bi):
    return ar * xr - ai * xi + br, ar * xi + ai * xr + bi


def _chunk_rows(ci, rev, nc):
    start = jnp.where(rev, (nc - 1 - ci) * SSM_RC, ci * SSM_RC)
    return pl.ds(pl.multiple_of(start, SSM_RC), SSM_RC)


def _scan_chunk(src, dst, ar, ai, rev, nj, ci, carry, prev_ref=None):
    def rows_of(staged, j, k):
        at = jnp.where(rev, SSM_JC - 1 - k, k) if staged else j
        return pl.ds(pl.multiple_of(at * N_SEG, N_SEG), N_SEG)

    for k in range(SSM_JC):
        jj = ci * SSM_JC + k
        j = jnp.where(rev, nj - 1 - jj, jj)
        rows = rows_of(src[1], j, k)
        nr, ni = _cfma(ar, ai, carry[0], carry[1], src[0][rows, _RE], src[0][rows, _IM])
        if dst is not None:
            rows = rows_of(dst[1], j, k)
            dst[0][rows, _RE] = nr
            dst[0][rows, _IM] = ni
        if prev_ref is None:
            carry = (nr, ni)
            continue
        jp = jnp.where(rev, j - 1, j + 1)
        if k == SSM_JC - 1:
            inside = jnp.where((jp >= 0) & (jp < nj), 1.0, 0.0)
            jp = jnp.clip(jp, 0, nj - 1)
        prow = pl.ds(pl.multiple_of(jp * N_SEG, N_SEG), N_SEG)
        xr, xi = prev_ref[prow, _RE], prev_ref[prow, _IM]
        sr, si = nr * xr + ni * xi, ni * xr - nr * xi
        if k == SSM_JC - 1:
            sr, si = inside * sr, inside * si
        carry = (nr, ni, carry[2] + sr, carry[3] + si)
    return carry


def _segment_inits(ar, ai, end_r, end_i, rev, nj):
    pr, pi = ar, ai
    for _ in range(int(math.log2(nj))):
        pr, pi = pr * pr - pi * pi, 2.0 * pr * pi
    seg = lax.broadcasted_iota(jnp.int32, end_r.shape, 0)
    zero = jnp.zeros_like(end_r)

    def chain(shift, keep):
        ir, ii = zero, zero
        for _ in range(N_SEG - 1):
            tr, ti = _cfma(pr, pi, ir, ii, end_r, end_i)
            ir = jnp.where(keep, pltpu.roll(tr, shift, 0), 0.0)
            ii = jnp.where(keep, pltpu.roll(ti, shift, 0), 0.0)
        return ir, ii

    up_r, up_i = chain(1, seg >= 1)
    dn_r, dn_i = chain(N_SEG - 1, seg <= N_SEG - 2)
    return jnp.where(rev, dn_r, up_r), jnp.where(rev, dn_i, up_i)


def _ssm_specs(l):
    act = pl.BlockSpec((l, SSM_CH), lambda k, d: (0, k))
    bmat = pl.BlockSpec((None, None, SSM_CH, 2 * SSM_ST), lambda k, d: (d, k, 0, 0))
    cmat = pl.BlockSpec((None, None, 2 * SSM_ST, SSM_CH), lambda k, d: (d, k, 0, 0))
    lam = pl.BlockSpec((None, None, 1, SSM_ST), lambda k, d: (d, k, 0, 0))
    return act, bmat, cmat, lam


def _ssm_fwd(u_seg, bcat, ccat, lam_re, lam_im):
    l = u_seg.shape[0]
    nj = l // N_SEG
    nc = l // SSM_RC

    def body(u_ref, b_ref, c_ref, lr_ref, li_ref, y_ref, keep_ref, xs_ref, stage0, stage1, keep_sem):
        k, d = pl.program_id(0), pl.program_id(1)
        rev = d == 1
        shape = (N_SEG, SSM_ST)
        ar, ai = jnp.broadcast_to(lr_ref[...], shape), jnp.broadcast_to(li_ref[...], shape)
        zero = jnp.zeros(shape, F32)

        def inputs(ci, stage):
            rows = _chunk_rows(ci, rev, nc)
            bu = _dg(u_ref[rows, :], b_ref[...], NN)
            stage[...] = bu
            xs_ref[rows, :] = bu

        def first(stage, ci, carry):
            return _scan_chunk((stage, True), None, ar, ai, rev, nj, ci, carry)

        def first_pass(t, carry):
            inputs(2 * t + 1, stage1)
            carry = first(stage0, 2 * t, carry)
            inputs(2 * t + 2, stage0)
            return first(stage1, 2 * t + 1, carry)

        inputs(0, stage0)
        carry = lax.fori_loop(0, nc // 2 - 1, first_pass, (zero, zero))
        inputs(nc - 1, stage1)
        carry = first(stage0, nc - 2, carry)
        end_r, end_i = first(stage1, nc - 1, carry)
        init = _segment_inits(ar, ai, end_r, end_i, rev, nj)

        @pl.when(d == 0)
        def _():
            y_ref[...] = jnp.zeros_like(y_ref)

        def outputs(ci):
            rows = _chunk_rows(ci, rev, nc)
            y_ref[rows, :] += _dg(xs_ref[rows, :].astype(BF16), c_ref[...], NN)
            pltpu.make_async_copy(xs_ref.at[rows], keep_ref.at[d, k, rows], keep_sem).start()

        def second(ci, carry):
            return _scan_chunk((xs_ref, False), (xs_ref, False), ar, ai, rev, nj, ci, carry)

        def second_pass(ci, carry):
            outputs(ci - 1)
            return second(ci, carry)

        lax.fori_loop(1, nc, second_pass, second(0, init))
        outputs(nc - 1)
        pltpu.make_async_copy(xs_ref, keep_ref.at[d, k], keep_sem).wait()

    act, bmat, cmat, lam = _ssm_specs(l)
    return pl.pallas_call(
        body, grid=(SSM_CB, 2), in_specs=[act, bmat, cmat, lam, lam], out_specs=[act, ANY],
        out_shape=[jax.ShapeDtypeStruct((l, SSM_WIDTH), F32),
                   jax.ShapeDtypeStruct((2, SSM_CB, l, 2 * SSM_ST), F32)],
        scratch_shapes=[pltpu.VMEM((l, 2 * SSM_ST), F32), pltpu.VMEM((SSM_RC, 2 * SSM_ST), F32),
                        pltpu.VMEM((SSM_RC, 2 * SSM_ST), F32), pltpu.SemaphoreType.DMA],
        name="ssm_fwd", compiler_params=_params(("parallel", "arbitrary"), vmem_mb=56),
    )(u_seg, bcat.astype(BF16), ccat.astype(BF16), lam_re, lam_im)


def _ssm_bwd(u_seg, dy_seg, states, bcat, ccat, lam_re, lam_im):
    l = u_seg.shape[0]
    nj = l // N_SEG
    nc = l // SSM_RC

    def body(u_ref, dy_ref, keep_ref, b_ref, c_ref, lr_ref, li_ref,
             du_ref, db_ref, dc_ref, dlr_ref, dli_ref, xs_ref, gs_ref, stage0, stage1, keep_sem):
        k, d = pl.program_id(0), pl.program_id(1)
        rev = d == 1
        back = jnp.logical_not(rev)
        shape = (N_SEG, SSM_ST)
        ar, ai = jnp.broadcast_to(lr_ref[...], shape), -jnp.broadcast_to(li_ref[...], shape)
        zero = jnp.zeros(shape, F32)
        fetch = pltpu.make_async_copy(keep_ref.at[d, k], xs_ref, keep_sem)
        fetch.start()

        def inputs(ci, stage):
            rows = _chunk_rows(ci, back, nc)
            dx = _dg(dy_ref[rows, :], c_ref[...], NT)
            stage[...] = dx
            gs_ref[rows, :] = dx

        def first(stage, ci, carry):
            return _scan_chunk((stage, True), None, ar, ai, back, nj, ci, carry)

        def first_pass(t, carry):
            inputs(2 * t + 1, stage1)
            carry = first(stage0, 2 * t, carry)
            inputs(2 * t + 2, stage0)
            return first(stage1, 2 * t + 1, carry)

        inputs(0, stage0)
        carry = lax.fori_loop(0, nc // 2 - 1, first_pass, (zero, zero))
        inputs(nc - 1, stage1)
        carry = first(stage0, nc - 2, carry)
        end_r, end_i = first(stage1, nc - 1, carry)
        init = _segment_inits(ar, ai, end_r, end_i, back, nj)
        fetch.wait()
        db_ref[...] = jnp.zeros_like(db_ref)
        dc_ref[...] = jnp.zeros_like(dc_ref)

        @pl.when(d == 0)
        def _():
            du_ref[...] = jnp.zeros_like(du_ref)

        def outputs(ci, stage):
            rows = _chunk_rows(ci, back, nc)
            g = stage[...].astype(BF16)
            dc_ref[...] += _dg(xs_ref[rows, :].astype(BF16), dy_ref[rows, :], TN)
            db_ref[...] += _dg(u_ref[rows, :], g, TN)
            du_ref[rows, :] += _dg(g, b_ref[...], NT)

        def second(ci, stage, carry):
            return _scan_chunk((gs_ref, False), (stage, True), ar, ai, back, nj, ci, carry, prev_ref=xs_ref)

        def second_pass(t, carry):
            outputs(2 * t, stage0)
            carry = second(2 * t + 1, stage1, carry)
            outputs(2 * t + 1, stage1)
            return second(2 * t + 2, stage0, carry)

        carry = lax.fori_loop(0, nc // 2 - 1, second_pass, second(0, stage0, init + (zero, zero)))
        outputs(nc - 2, stage0)
        gr, gi, acc_r, acc_i = second(nc - 1, stage1, carry)
        outputs(nc - 1, stage1)

        seg = lax.broadcasted_iota(jnp.int32, shape, 0)
        jb = jnp.where(rev, nj - 1, 0)
        erow = pl.ds(pl.multiple_of((nj - 1 - jb) * N_SEG, N_SEG), N_SEG)

        def before(t):
            up = jnp.where(seg >= 1, pltpu.roll(t, 1, 0), 0.0)
            down = jnp.where(seg <= N_SEG - 2, pltpu.roll(t, N_SEG - 1, 0), 0.0)
            return jnp.where(rev, down, up)

        init_r, init_i = before(xs_ref[erow, _RE]), before(xs_ref[erow, _IM])
        acc_r = acc_r + gr * init_r + gi * init_i
        acc_i = acc_i + gi * init_r - gr * init_i
        dlr_ref[...] = jnp.sum(acc_r, axis=0, keepdims=True)
        dli_ref[...] = jnp.sum(acc_i, axis=0, keepdims=True)

    act, bmat, cmat, lam = _ssm_specs(l)
    return pl.pallas_call(
        body, grid=(SSM_CB, 2), in_specs=[act, act, ANY, bmat, cmat, lam, lam],
        out_specs=[act, bmat, cmat, lam, lam],
        out_shape=[jax.ShapeDtypeStruct((l, SSM_WIDTH), F32),
                   jax.ShapeDtypeStruct(bcat.shape, F32), jax.ShapeDtypeStruct(ccat.shape, F32),
                   jax.ShapeDtypeStruct(lam_re.shape, F32), jax.ShapeDtypeStruct(lam_im.shape, F32)],
        scratch_shapes=[pltpu.VMEM((l, 2 * SSM_ST), F32), pltpu.VMEM((l, 2 * SSM_ST), F32),
                        pltpu.VMEM((SSM_RC, 2 * SSM_ST), F32), pltpu.VMEM((SSM_RC, 2 * SSM_ST), F32),
                        pltpu.SemaphoreType.DMA],
        name="ssm_bwd", compiler_params=_params(("parallel", "arbitrary"), vmem_mb=58),
    )(u_seg, dy_seg, states, bcat.astype(BF16), ccat.astype(BF16), lam_re, lam_im)


def _glu_fwd(y_ssm, u, d_skip, w_glu):
    l, w = u.shape

    def body(y_ref, u_ref, d_ref, w_ref, pre_ref, s_ref, ys_ref):
        pre = y_ref[...] + d_ref[...] * u_ref[...]
        z = _gelu(pre)
        s = _dg(z.astype(BF16), w_ref[...], NN)
        pre_ref[...] = pre
        s_ref[...] = s
        ys_ref[...] = z * _sigmoid(s)

    row = pl.BlockSpec((TM_EW, w), lambda i: (i, 0))
    out = jax.ShapeDtypeStruct((l, w), F32)
    return pl.pallas_call(
        body, grid=(l // TM_EW,),
        in_specs=[row, row, pl.BlockSpec((1, w), lambda i: (0, 0)), pl.BlockSpec((w, w), lambda i: (0, 0))],
        out_specs=[row, row, row], out_shape=[out, out, out], name="glu_fwd",
        compiler_params=_params(("parallel",)),
    )(y_ssm, u, d_skip, w_glu)


def _glu_bwd(pre, s, dys, u, d_skip, w_glu):
    l, w = u.shape

    def body(pre_ref, s_ref, dys_ref, u_ref, d_ref, w_ref, dpre_ref, z_ref, ds_ref, dd_ref):
        pre, dys = pre_ref[...], dys_ref[...]
        z = _gelu(pre)
        sig = _sigmoid(s_ref[...])
        ds = (dys * z * sig * (1.0 - sig)).astype(BF16)
        dz = dys * sig + _dg(ds, w_ref[...], NT)
        dpre = dz * _gelu_grad(pre)
        dpre_ref[...] = dpre
        z_ref[...] = z.astype(BF16)
        ds_ref[...] = ds

        @pl.when(pl.program_id(0) == 0)
        def _():
            dd_ref[...] = jnp.zeros_like(dd_ref)

        dd_ref[...] += jnp.sum(dpre * u_ref[...], axis=0, keepdims=True)

    row = pl.BlockSpec((TM_EW, w), lambda i: (i, 0))
    vec = pl.BlockSpec((1, w), lambda i: (0, 0))
    return pl.pallas_call(
        body, grid=(l // TM_EW,),
        in_specs=[row, row, row, row, vec, pl.BlockSpec((w, w), lambda i: (0, 0))],
        out_specs=[row, row, row, vec],
        out_shape=[jax.ShapeDtypeStruct((l, w), F32), jax.ShapeDtypeStruct((l, w), BF16),
                   jax.ShapeDtypeStruct((l, w), BF16), jax.ShapeDtypeStruct((1, w), F32)],
        name="glu_bwd", compiler_params=_params(("arbitrary",)),
    )(pre, s, dys, u, d_skip, w_glu)


TM_CV = 512
TC_CV = 256
TM_CF = 256
TC_CF = D_FF // 2
HALO = SUBLANES


def _conv_specs(l, col0, tm=TM_CV, tc=TC_CV):
    per = tm // HALO
    nh = l // HALO
    off = col0 // tc
    return [
        pl.BlockSpec((HALO, tc), lambda j, i: (jnp.maximum(i * per - 1, 0), j + off)),
        pl.BlockSpec((tm, tc), lambda j, i: (i, j + off)),
        pl.BlockSpec((HALO, tc), lambda j, i: (jnp.minimum((i + 1) * per, nh - 1), j + off)),
    ]


def _ext(prev_ref, mid_ref, next_ref, first, last):
    p = jnp.where(first, 0.0, prev_ref[...])
    n = jnp.where(last, 0.0, next_ref[...])
    return jnp.concatenate([p, mid_ref[...], n], axis=0)


def _shift_dn(t):
    return pltpu.roll(t, 1, 0)


def _shift_up(t):
    return pltpu.roll(t, t.shape[0] - 1, 0)


def _conv3(e, w_ref, b_ref):
    return w_ref[0:1, :] * _shift_dn(e) + w_ref[1:2, :] * e + w_ref[2:3, :] * _shift_up(e) + b_ref[...]


def _convffn_fwd(up_pre, conv_w, conv_b):
    l = up_pre.shape[0]
    tm, tc = TM_CF, TC_CF
    ni = l // tm
    wspec = lambda off: pl.BlockSpec((3, tc), lambda j, i: (0, j + off))
    bspec = lambda off: pl.BlockSpec((1, tc), lambda j, i: (0, j + off))
    voff = D_FF // tc

    def body(gp, gm, gn, vp, vm, vn, wg, bg, wv, bv, o_ref):
        i = pl.program_id(1)
        first, last = i == 0, i == ni - 1
        gate = _conv3(_ext(gp, gm, gn, first, last), wg, bg)[HALO:HALO + tm]
        val = _conv3(_ext(vp, vm, vn, first, last), wv, bv)[HALO:HALO + tm]
        o_ref[...] = (gate * _sigmoid(gate) * val).astype(BF16)

    return pl.pallas_call(
        body, grid=(D_FF // tc, ni),
        in_specs=_conv_specs(l, 0, tm, tc) + _conv_specs(l, D_FF, tm, tc)
        + [wspec(0), bspec(0), wspec(voff), bspec(voff)],
        out_specs=pl.BlockSpec((tm, tc), lambda j, i: (i, j)),
        out_shape=jax.ShapeDtypeStruct((l, D_FF), BF16), name="convffn_fwd",
        compiler_params=_params(("parallel", "parallel")),
    )(up_pre, up_pre, up_pre, up_pre, up_pre, up_pre, conv_w, conv_b, conv_w, conv_b)


HALO_B = 2 * SUBLANES


def _convffn_bwd(up_pre, dx2b, w_down, conv_w, conv_b):
    l = up_pre.shape[0]
    ni = l // TM_CV
    d = dx2b.shape[1]
    wspec = lambda off: pl.BlockSpec((3, TC_CV), lambda i, j: (0, j + off))
    bspec = lambda off: pl.BlockSpec((1, TC_CV), lambda i, j: (0, j + off))
    voff = D_FF // TC_CV
    swap = lambda spec: pl.BlockSpec(spec.block_shape, lambda i, j, f=spec.index_map: f(j, i))
    per, nh = TM_CV // HALO_B, l // HALO_B
    dx_specs = [pl.BlockSpec((HALO_B, d), lambda i, j: (jnp.maximum(i * per - 1, 0), 0)),
                pl.BlockSpec((TM_CV, d), lambda i, j: (i, 0)),
                pl.BlockSpec((HALO_B, d), lambda i, j: (jnp.minimum((i + 1) * per, nh - 1), 0))]

    def body(gp, gm, gn, vp, vm, vn, xp, xm, xn, wd, wg, bg, wv, bv, dup_ref, pg_ref, pv_ref):
        i = pl.program_id(0)
        first, last = i == 0, i == ni - 1
        ge, ve = _ext(gp, gm, gn, first, last), _ext(vp, vm, vn, first, last)
        zero = jnp.zeros((HALO_B, d), BF16)
        dx = jnp.concatenate([jnp.where(first, zero, xp[...]), xm[...], jnp.where(last, zero, xn[...])], axis=0)
        de = _dg(dx, wd[...], NT)[HALO_B - HALO:HALO_B + TM_CV + HALO]
        taps = [(_shift_dn(e), e, _shift_up(e)) for e in (ge, ve)]
        conv = lambda t, w_ref, b_ref: w_ref[0:1, :] * t[0] + w_ref[1:2, :] * t[1] + w_ref[2:3, :] * t[2] + b_ref[...]
        gate, val = conv(taps[0], wg, bg), conv(taps[1], wv, bv)
        sig = _sigmoid(gate)
        silu = gate * sig
        dgate = de * val * (sig + silu * (1.0 - sig))
        dval = de * silu
        mid = slice(HALO, HALO + TM_CV)
        rid = lax.broadcasted_iota(jnp.int32, (SUBLANES, TC_CV), 0)
        for half, (dup, tap, w_ref, p_ref) in enumerate(((dgate, taps[0], wg, pg_ref), (dval, taps[1], wv, pv_ref))):
            dpre = w_ref[0:1, :] * _shift_up(dup) + w_ref[1:2, :] * dup + w_ref[2:3, :] * _shift_dn(dup)
            dup_ref[half] = dpre[mid].astype(BF16)
            dm_ = dup[mid]
            sums = [jnp.sum(dm_ * t[mid], axis=0, keepdims=True) for t in tap]
            sums.append(jnp.sum(dm_, axis=0, keepdims=True))
            acc = jnp.zeros((SUBLANES, TC_CV), F32)
            for k, sk in enumerate(sums):
                acc = jnp.where(rid == k, sk, acc)
            p_ref[...] = acc

    par = pl.BlockSpec((None, SUBLANES, TC_CV), lambda i, j: (i, 0, j))
    dup, pg, pv = pl.pallas_call(
        body, grid=(ni, D_FF // TC_CV),
        in_specs=[swap(s) for s in _conv_specs(l, 0) + _conv_specs(l, D_FF)] + dx_specs
        + [pl.BlockSpec((TC_CV, d), lambda i, j: (j, 0)), wspec(0), bspec(0), wspec(voff), bspec(voff)],
        out_specs=[pl.BlockSpec((2, TM_CV, TC_CV), lambda i, j: (0, i, j)), par, par],
        out_shape=[jax.ShapeDtypeStruct((2, l, D_FF), BF16),
                   jax.ShapeDtypeStruct((ni, SUBLANES, D_FF), F32), jax.ShapeDtypeStruct((ni, SUBLANES, D_FF), F32)],
        name="convffn_bwd", compiler_params=_params(("parallel", "parallel")),
    )(up_pre, up_pre, up_pre, up_pre, up_pre, up_pre, dx2b, dx2b, dx2b, w_down, conv_w, conv_b, conv_w, conv_b)
    return dup, jnp.concatenate([jnp.sum(pg, axis=0), jnp.sum(pv, axis=0)], axis=1)


def _local_step(x, target, wb, sp, mixer_weights=None, late_weights=None, ffn_grads_ready=None,
                ffn_grads_next=None):
    l = x.shape[0]
    tabs = _rope_tables(l)
    disc = _ssm_disc(sp["a_re"], sp["a_im"], sp["log_step"], sp["b_re"], sp["b_im"])
    bcat, ccat, lam_re, lam_im = _ssm_pack(*disc, sp["c_re"], sp["c_im"])
    d_skip = sp["d_skip"].reshape(1, SSM_WIDTH)

    big = min(l, 1024)
    h, qkv, u = _rms_mm_rope(x, sp["norm_mix_g"], wb["w_in"], tabs, "mm_in")
    attn, lse = _attn_fwd(qkv, sp["sink"])
    u_seg = _to_segments(u).astype(BF16)
    y_seg, states = _ssm_fwd(u_seg, bcat, ccat, lam_re, lam_im)
    y_ssm = _from_segments(y_seg)
    if mixer_weights is not None:
        wb = dict(wb, **mixer_weights(attn))
    pre, s_glu, ys = _glu_fwd(y_ssm, u, d_skip, wb["w_glu"])
    mixed, x1, h2 = _mix_mm_res_rms(attn, ys, sp["norm_attn_g"], sp["norm_ssm_g"], wb["w_out"], x,
                                    sp["norm_ffn_g"], "mm_out")
    if late_weights is not None:
        wb = dict(wb, **late_weights(h2))
    up_pre = _mm_nn_cols(h2, wb["w_up"], big, "mm_up")
    conv_w = wb["conv_w"]
    act = _convffn_fwd(up_pre, conv_w, sp["conv_b"])
    loss, dx2, dx2b, d_final_g = _mm_res_loss(act, wb["w_down"], x1, sp["norm_final_g"].reshape(1, D_MODEL), target)

    g = {"norm_final_g": d_final_g.reshape(D_MODEL)}
    g["w_down"] = _mm_tn(act, dx2b, D_FF // 2, 512, "mm_down_dw")
    dup_pre, conv_par = _convffn_bwd(up_pre, dx2b, wb["w_down"], conv_w, sp["conv_b"])
    g["conv_w"], g["conv_b"] = conv_par[0:3], conv_par[3:4]
    g["w_up"] = _mm_tn_cols(h2, dup_pre, wb["w_up"].shape[0], 512, "mm_up_dw")
    zero = ffn_grads_ready(g["w_up"], g["w_down"]) if ffn_grads_ready is not None else 0.0
    dx1, dx1b, g["norm_ffn_g"] = _mm_cols_rms_bwd(dup_pre, wb["w_up"], x1, sp["norm_ffn_g"] + zero, dx2, "mm_up_dx")
    g["w_out"] = _mm_tn(mixed, dx1b, 1024, 1024, "mm_out_dw")
    zero = ffn_grads_next(g["w_out"]) if ffn_grads_next is not None else 0.0
    dattn, dys, g["norm_attn_g"], g["norm_ssm_g"] = _mm_mix_bwd(
        dx1b, wb["w_out"], attn, ys, sp["norm_attn_g"] + zero, sp["norm_ssm_g"], "mm_out_dx")
    dpre, zb, dsb, dd = _glu_bwd(pre, s_glu, dys, u, d_skip, wb["w_glu"])
    g["d_skip"] = dd.reshape(N_SSM_GROUPS, SSM_GROUP)
    g["w_glu"] = _mm_tn(zb, dsb, 512, 512, "mm_glu_dw")
    du_seg, dbcat, dccat, dlam_re, dlam_im = _ssm_bwd(u_seg, _to_segments(dpre).astype(BF16), states, bcat, ccat,
                                                      lam_re, lam_im)
    dlb_re, dlb_im, dbb_re, dbb_im, g["c_re"], g["c_im"] = _ssm_unpack(dbcat, dccat, dlam_re, dlam_im)
    _, disc_vjp = jax.vjp(_ssm_disc, sp["a_re"], sp["a_im"], sp["log_step"], sp["b_re"], sp["b_im"])
    g["a_re"], g["a_im"], g["log_step"], g["b_re"], g["b_im"] = disc_vjp((dlb_re, dlb_im, dbb_re, dbb_im))
    dq, dkv, g["sink"] = _attn_bwd(qkv, attn, dattn, lse, sp["sink"])
    dproj = _rope_bwd(dq, dkv, _from_segments(du_seg), dpre, d_skip, tabs)
    g["w_in"] = _mm_tn(dproj, h, IN_WIDTH // 5, D_MODEL, "mm_in_dw")
    grad_x, _, g["norm_mix_g"] = _mm_nn_rms_bwd(dproj, wb["w_in"], x, sp["norm_mix_g"], dx1, "mm_in_dx")
    return loss, grad_x, g


MESH = pl.DeviceIdType.MESH
ANY = pl.BlockSpec(memory_space=pl.ANY)


def _place():
    x, y, c = lax.axis_index("x"), lax.axis_index("y"), lax.axis_index("c")
    chips = [(1 - x, y), (x, 1 - y), (1 - x, 1 - y)]
    return x, y, c, chips


def _chip_index(px, py):
    return 2 * px + py


CHUNK_BYTES = 256 * 1024
MAX_CHUNKS = 16


def _row_chunks(rows, row_bytes, align):
    n = max(1, min(MAX_CHUNKS, (rows * row_bytes) // CHUNK_BYTES))
    per = -(-rows // n)
    per = -(-per // align) * align
    return [(r0, min(per, rows - r0)) for r0 in range(0, rows, per)]


def _align_of(dtype):
    return SUBLANES * 4 // jnp.dtype(dtype).itemsize


def _remote(src, dst, send_sem, recv_sem, to):
    return pltpu.make_async_remote_copy(src_ref=src, dst_ref=dst, send_sem=send_sem, recv_sem=recv_sem,
                                        device_id=to, device_id_type=MESH)


CAST_ROWS = 64


def _gather_weights(shards, dtypes):
    nw = len(shards)

    def body(*refs):
        w_refs, o_refs = refs[:nw], refs[nw:2 * nw]
        send_sems, recv_sems, in_sems, out_sems = refs[2 * nw:2 * nw + 4]
        raw, cast = refs[2 * nw + 4:3 * nw + 4], refs[3 * nw + 4:]
        x, y, c, chips = _place()
        mine = _chip_index(x, y)
        sibling = (x, y, 1 - c)

        def rows_of(ref, chip, r0, nr):
            return ref.at[chip, pl.ds(r0, nr), :]

        def copy(wi, k, src, dst, to):
            return _remote(src, dst, send_sems.at[wi, k], recv_sems.at[wi, k], to)

        geo = []
        for wi in range(nw):
            rows, cols = w_refs[wi].shape
            row_bytes = cols * jnp.dtype(dtypes[wi]).itemsize
            geo.append((rows // 2, _row_chunks(rows // 2, row_bytes, _align_of(dtypes[wi]))))

        stage_in = [pltpu.make_async_copy(w_refs[wi], raw[wi], in_sems.at[wi]) for wi in range(nw)]
        for cp in stage_in:
            cp.start()
        staged = [raw[wi] if dtypes[wi] == w_refs[wi].dtype else cast[wi] for wi in range(nw)]
        stage_out = []
        for wi in range(nw):
            stage_in[wi].wait()
            if staged[wi] is not raw[wi]:
                def cast_rows(i, _, wi=wi):
                    rows = pl.ds(pl.multiple_of(i * CAST_ROWS, CAST_ROWS), CAST_ROWS)
                    cast[wi][rows, :] = raw[wi][rows, :].astype(dtypes[wi])
                    return 0

                lax.fori_loop(0, w_refs[wi].shape[0] // CAST_ROWS, cast_rows, 0)
            cp = pltpu.make_async_copy(staged[wi], o_refs[wi].at[mine], out_sems.at[wi])
            cp.start()
            stage_out.append(cp)

        for wi in range(nw):
            hr, half_chunks = geo[wi]
            for j, chip in enumerate(chips):
                for r0, nr in half_chunks:
                    copy(wi, j, staged[wi].at[pl.ds(c * hr + r0, nr), :],
                         rows_of(o_refs[wi], mine, c * hr + r0, nr), (*chip, c)).start()
        for wi in range(nw):
            hr, half_chunks = geo[wi]
            for j, chip in enumerate(chips):
                got = rows_of(o_refs[wi], _chip_index(*chip), c * hr, hr)
                copy(wi, j, got, got, (*chip, c)).wait_recv()
                for r0, nr in half_chunks:
                    piece = rows_of(o_refs[wi], _chip_index(*chip), c * hr + r0, nr)
                    copy(wi, 3 + j, piece, piece, sibling).start()
        for wi in range(nw):
            hr = geo[wi][0]
            for j, chip in enumerate(chips):
                got = rows_of(o_refs[wi], _chip_index(*chip), (1 - c) * hr, hr)
                copy(wi, 3 + j, got, got, sibling).wait_recv()
        for wi in range(nw):
            hr = geo[wi][0]
            sent = rows_of(o_refs[wi], mine, c * hr, hr)
            for k in range(6):
                copy(wi, k, sent, sent, sibling).wait_send()
            stage_out[wi].wait()

    return pl.pallas_call(
        body, in_specs=[ANY] * nw, out_specs=[ANY] * nw,
        out_shape=[jax.ShapeDtypeStruct((4, *s.shape), t) for s, t in zip(shards, dtypes)],
        scratch_shapes=[pltpu.SemaphoreType.DMA((nw, 6)), pltpu.SemaphoreType.DMA((nw, 6)),
                        pltpu.SemaphoreType.DMA((nw,)), pltpu.SemaphoreType.DMA((nw,))]
        + [pltpu.VMEM(s.shape, s.dtype) for s in shards] + [pltpu.VMEM(s.shape, t) for s, t in zip(shards, dtypes)],
        name="gather_weights", compiler_params=_params(vmem_mb=40),
    )(*shards)


HBM = pl.BlockSpec(memory_space=pltpu.HBM)
SEM = pl.BlockSpec(memory_space=pltpu.SEMAPHORE)
EFFECT = pltpu.SideEffectType.DATAFLOW_SIDE_EFFECTING


def _cast_place(w, place, dtype, after, name):
    rows, cols = w.shape
    tr = _row_tile(rows, cols, _align_of(dtype))

    def body(p_ref, w_ref, after_ref, o_ref):
        del p_ref, after_ref
        o_ref[...] = w_ref[...].astype(dtype)

    grid_spec = pltpu.PrefetchScalarGridSpec(
        num_scalar_prefetch=1, grid=(rows // tr,),
        in_specs=[pl.BlockSpec((tr, cols), lambda i, p: (i, 0)), ANY],
        out_specs=pl.BlockSpec((None, tr, cols), lambda i, p: (p[1], i, 0)))
    return pl.pallas_call(body, grid_spec=grid_spec, out_shape=jax.ShapeDtypeStruct((4, rows, cols), dtype),
                          name=name, compiler_params=_params(("parallel",)))(place, w, after)


def _split_start(name, arrays, n_pairs, issue):
    n = len(arrays)

    def body(*refs):
        issue(refs[:n], refs[n:n + n_pairs], refs[n + n_pairs:n + 2 * n_pairs])
        token = refs[2 * n + 2 * n_pairs]
        token[...] = jnp.zeros_like(token)

    dma = pltpu.SemaphoreType.DMA(())
    outs = pl.pallas_call(
        body, name=name,
        out_shape=[dma] * (2 * n_pairs) + [pltpu.HBM(t.shape, t.dtype) for t in arrays]
        + [jax.ShapeDtypeStruct((SUBLANES, LANES), F32)],
        in_specs=[HBM] * n, out_specs=[SEM] * (2 * n_pairs) + [HBM] * n + [pl.BlockSpec(memory_space=pltpu.VMEM)],
        input_output_aliases={a: 2 * n_pairs + a for a in range(n)},
        compiler_params=pltpu.CompilerParams(has_side_effects=EFFECT),
    )(*[pltpu.with_memory_space_constraint(t, pltpu.HBM) for t in arrays])
    return outs[:n_pairs], outs[n_pairs:2 * n_pairs], outs[2 * n_pairs:2 * n_pairs + n], outs[-1]


def _split_wait(name, send_sems, recv_sems, flying, sizes, after):
    n, n_pairs = len(flying), len(send_sems)

    def body(*refs):
        x, y, c, _ = _place()
        for k, ref in enumerate(sizes(refs[:n])):
            cp = _remote(ref, ref, refs[n + k], refs[n + n_pairs + k], (x, y, 1 - c))
            cp.wait_send()
            cp.wait_recv()

    return pl.pallas_call(
        body, name=name, out_shape=[pltpu.HBM(t.shape, t.dtype) for t in flying],
        in_specs=[HBM] * n + [SEM] * (2 * n_pairs) + [ANY], out_specs=[HBM] * n,
        input_output_aliases={a: a for a in range(n)},
        compiler_params=pltpu.CompilerParams(has_side_effects=EFFECT),
    )(*flying, *send_sems, *recv_sems, after)


def _spread_start(lands, name):
    def issue(land_refs, send_sems, recv_sems):
        x, y, c, chips = _place()
        mine = _chip_index(x, y)
        for a, land in enumerate(land_refs):
            _, rows, cols = land.shape
            hr = rows // 2
            row_bytes = cols * jnp.dtype(land.dtype).itemsize
            for r0, nr in _row_chunks(hr, row_bytes, _align_of(land.dtype)):
                piece = land.at[mine, pl.ds(c * hr + r0, nr), :]
                for chip in chips:
                    for core in (0, 1):
                        _remote(piece, piece, send_sems[a], recv_sems[a], (*chip, core)).start()

    return _split_start(name, lands, len(lands), issue)


def _spread_wait(send_sems, recv_sems, flying, after, name):
    return _split_wait(name, send_sems, recv_sems, flying, lambda refs: [r.at[pl.ds(0, 3)] for r in refs], after)


def _pair_start(grads):
    n = len(grads)
    zones = [lax.empty((4, g.shape[1] // 2, g.shape[2]), F32) for g in grads]

    def issue(refs, send_sems, recv_sems):
        x, y, c, _ = _place()
        for a in range(n):
            g_ref, z_ref = refs[a], refs[n + a]
            _, rows, cols = g_ref.shape
            hr = rows // 2
            for k in range(4):
                for r0, nr in _row_chunks(hr, cols * 4, SUBLANES):
                    _remote(g_ref.at[k, pl.ds((1 - c) * hr + r0, nr), :], z_ref.at[k, pl.ds(r0, nr), :],
                            send_sems[a], recv_sems[a], (x, y, 1 - c)).start()

    return _split_start("pair_start", list(grads) + zones, n, issue)


def _pair_wait(send_sems, recv_sems, flying, after):
    n = len(flying) // 2
    out = _split_wait("pair_wait", send_sems, recv_sems, flying, lambda refs: list(refs[n:]), after)
    return out[:n], out[n:]


def _chip_start(sums):
    n = len(sums)
    zones = [lax.empty((3, *s.shape[1:]), s.dtype) for s in sums]

    def issue(refs, send_sems, recv_sems):
        x, y, c, chips = _place()
        for a in range(n):
            s_ref, z_ref = refs[a], refs[n + a]
            _, rows, cols = s_ref.shape
            row_bytes = cols * jnp.dtype(s_ref.dtype).itemsize
            for r0, nr in _row_chunks(rows, row_bytes, _align_of(s_ref.dtype)):
                for j, chip in enumerate(chips):
                    _remote(s_ref.at[_chip_index(*chip), pl.ds(r0, nr), :], z_ref.at[j, pl.ds(r0, nr), :],
                            send_sems[a], recv_sems[a], (*chip, c)).start()

    return _split_start("chip_start", list(sums) + zones, n, issue)


def _chip_wait(send_sems, recv_sems, flying, after):
    n = len(flying) // 2
    return _split_wait("chip_wait", send_sems, recv_sems, flying, lambda refs: list(refs[n:]), after)[n:]


def _pair_exchange(grads):
    na = len(grads)

    def body(*refs):
        g_refs, o_refs = refs[:na], refs[na:2 * na]
        send_sems, recv_sems = refs[2 * na:]
        x, y, c, _ = _place()
        sibling = (x, y, 1 - c)
        for ai in range(na):
            _, rows, cols = g_refs[ai].shape
            hr = rows // 2
            for k in range(4):
                for r0, nr in _row_chunks(hr, cols * 4, SUBLANES):
                    _remote(g_refs[ai].at[k, pl.ds((1 - c) * hr + r0, nr), :], o_refs[ai].at[k, pl.ds(r0, nr), :],
                            send_sems.at[ai], recv_sems.at[ai], sibling).start()
        for ai in range(na):
            _remote(o_refs[ai], o_refs[ai], send_sems.at[ai], recv_sems.at[ai], sibling).wait()

    return pl.pallas_call(
        body, in_specs=[ANY] * na, out_specs=[ANY] * na,
        out_shape=[jax.ShapeDtypeStruct((4, g.shape[1] // 2, g.shape[2]), F32) for g in grads],
        scratch_shapes=[pltpu.SemaphoreType.DMA((na,)), pltpu.SemaphoreType.DMA((na,))],
        name="pair_exchange",
    )(*grads)


def _row_tile(rows, cols, align):
    best = align
    for cand in range(align, rows + 1, align):
        if rows % cand == 0 and cand * cols <= 256 * 1024:
            best = cand
    return best


def _pair_sum(g, got, place, transit, name):
    _, rows, cols = g.shape
    hr = rows // 2
    tr = _row_tile(hr, cols, _align_of(transit))
    nt = hr // tr

    def body(p_ref, g_ref, r_ref, s_ref, own_ref):
        total = g_ref[...] + r_ref[...]
        s_ref[...] = total.astype(transit)

        @pl.when(pl.program_id(1) == p_ref[1])
        def _():
            own_ref[...] = total

    grid_spec = pltpu.PrefetchScalarGridSpec(
        num_scalar_prefetch=1, grid=(nt, 4),
        in_specs=[pl.BlockSpec((None, tr, cols), lambda i, k, p: (k, p[0] * nt + i, 0)),
                  pl.BlockSpec((None, tr, cols), lambda i, k, p: (k, i, 0))],
        out_specs=[pl.BlockSpec((None, tr, cols), lambda i, k, p: (k, i, 0)),
                   pl.BlockSpec((tr, cols), lambda i, k, p: (i, 0))])
    return pl.pallas_call(
        body, grid_spec=grid_spec,
        out_shape=[jax.ShapeDtypeStruct((4, hr, cols), transit), jax.ShapeDtypeStruct((hr, cols), F32)],
        name=name, compiler_params=_params(("parallel", "arbitrary")),
    )(place, g, got)


def _chip_exchange(sums):
    na = len(sums)

    def body(*refs):
        s_refs, o_refs = refs[:na], refs[na:2 * na]
        send_sems, recv_sems = refs[2 * na:]
        x, y, c, chips = _place()
        for ai in range(na):
            _, rows, cols = s_refs[ai].shape
            row_bytes = cols * jnp.dtype(s_refs[ai].dtype).itemsize
            for r0, nr in _row_chunks(rows, row_bytes, _align_of(s_refs[ai].dtype)):
                for j, chip in enumerate(chips):
                    _remote(s_refs[ai].at[_chip_index(*chip), pl.ds(r0, nr), :], o_refs[ai].at[j, pl.ds(r0, nr), :],
                            send_sems.at[ai, j], recv_sems.at[ai, j], (*chip, c)).start()
        for ai in range(na):
            for j, chip in enumerate(chips):
                _remote(o_refs[ai].at[j], o_refs[ai].at[j], send_sems.at[ai, j], recv_sems.at[ai, j],
                        (*chip, c)).wait()

    return pl.pallas_call(
        body, in_specs=[ANY] * na, out_specs=[ANY] * na,
        out_shape=[jax.ShapeDtypeStruct((3, *s.shape[1:]), s.dtype) for s in sums],
        scratch_shapes=[pltpu.SemaphoreType.DMA((na, 3)), pltpu.SemaphoreType.DMA((na, 3))],
        name="chip_exchange",
    )(*sums)


def _chip_sum(own, landed, name):
    hr, cols = own.shape
    tr = _row_tile(hr, cols, _align_of(landed.dtype))

    def body(o_ref, l_ref, f_ref):
        acc = o_ref[...]
        for j in range(3):
            acc = acc + l_ref[j].astype(F32)
        f_ref[...] = acc

    return pl.pallas_call(
        body, grid=(hr // tr,),
        in_specs=[pl.BlockSpec((tr, cols), lambda i: (i, 0)), pl.BlockSpec((3, tr, cols), lambda i: (0, i, 0))],
        out_specs=pl.BlockSpec((tr, cols), lambda i: (i, 0)),
        out_shape=jax.ShapeDtypeStruct((hr, cols), F32), name=name,
        compiler_params=_params(("parallel",)),
    )(own, landed)


def _final_exchange(halves, small):
    nh = len(halves)

    def body(*refs):
        h_refs, s_ref = refs[:nh], refs[nh]
        o_refs, so_ref = refs[nh + 1:2 * nh + 1], refs[2 * nh + 1]
        send_sems, recv_sems, local_sem, ssend_sems, srecv_sems = refs[2 * nh + 2:]
        x, y, c, _ = _place()
        me = 4 * x + 2 * y + c
        sibling = (x, y, 1 - c)
        for hi in range(nh):
            hr, cols = h_refs[hi].shape
            for r0, nr in _row_chunks(hr, cols * 4, SUBLANES):
                _remote(h_refs[hi].at[pl.ds(r0, nr), :], o_refs[hi].at[pl.ds(r0, nr), :],
                        send_sems.at[hi], recv_sems.at[hi], sibling).start()
        small_cps = [pltpu.make_async_copy(s_ref, so_ref.at[me], local_sem)]
        for r in range(1, 8):
            fx, fy, fc = (r >> 2) & 1, (r >> 1) & 1, r & 1
            peer = (1 - x if fx else x, 1 - y if fy else y, 1 - c if fc else c)
            small_cps.append(_remote(s_ref, so_ref.at[me], ssend_sems.at[r - 1], srecv_sems.at[r - 1], peer))
        for cp in small_cps:
            cp.start()
        for hi in range(nh):
            _remote(h_refs[hi], o_refs[hi], send_sems.at[hi], recv_sems.at[hi], sibling).wait()
        for cp in small_cps:
            cp.wait()

    return pl.pallas_call(
        body, in_specs=[ANY] * (nh + 1), out_specs=[ANY] * (nh + 1),
        out_shape=[jax.ShapeDtypeStruct(h.shape, F32) for h in halves]
        + [jax.ShapeDtypeStruct((8, *small.shape), F32)],
        scratch_shapes=[pltpu.SemaphoreType.DMA((nh,)), pltpu.SemaphoreType.DMA((nh,)),
                        pltpu.SemaphoreType.DMA, pltpu.SemaphoreType.DMA((7,)), pltpu.SemaphoreType.DMA((7,))],
        name="final_exchange",
    )(*halves, small)


def _adamw_halves(w, own, other, m, v, place, name):
    r, c = w.shape
    hr = r // 2
    tr = _row_tile(hr, c, SUBLANES)
    nt = hr // tr
    c1 = 1.0 - ADAM_B1 ** ADAM_STEP
    c2 = 1.0 - ADAM_B2 ** ADAM_STEP

    def body(p_ref, w_ref, own_ref, other_ref, m_ref, v_ref, g_ref, d_ref, nm_ref, nv_ref):
        mine = pl.program_id(0) // nt == p_ref[0]
        gv = jnp.where(mine, own_ref[...], other_ref[...])
        nm = ADAM_B1 * m_ref[...] + (1.0 - ADAM_B1) * gv
        nv = ADAM_B2 * v_ref[...] + (1.0 - ADAM_B2) * (gv * gv)
        g_ref[...] = gv
        d_ref[...] = -ADAM_LR * ((nm / c1) / (jnp.sqrt(nv / c2) + ADAM_EPS) + ADAM_WD * w_ref[...])
        nm_ref[...] = nm
        nv_ref[...] = nv

    full = pl.BlockSpec((tr, c), lambda i, p: (i, 0))
    half = pl.BlockSpec((tr, c), lambda i, p: (i % nt, 0))
    out = jax.ShapeDtypeStruct((r, c), F32)
    grid_spec = pltpu.PrefetchScalarGridSpec(num_scalar_prefetch=1, grid=(2 * nt,),
                                             in_specs=[full, half, half, full, full], out_specs=[full] * 4)
    return pl.pallas_call(body, grid_spec=grid_spec, out_shape=[out] * 4, name=name,
                          compiler_params=_params(("parallel",)))(place, w, own, other, m, v)


def _adamw_many(ws, gs, ms, vs, name):
    n = len(ws)
    c1 = 1.0 - ADAM_B1 ** ADAM_STEP
    c2 = 1.0 - ADAM_B2 ** ADAM_STEP

    def body(*refs):
        w_refs, g_refs, m_refs, v_refs = (refs[k * n:(k + 1) * n] for k in range(4))
        d_refs, nm_refs, nv_refs = (refs[(4 + k) * n:(5 + k) * n] for k in range(3))
        for i in range(n):
            gv = g_refs[i][...]
            nm = ADAM_B1 * m_refs[i][...] + (1.0 - ADAM_B1) * gv
            nv = ADAM_B2 * v_refs[i][...] + (1.0 - ADAM_B2) * (gv * gv)
            d_refs[i][...] = -ADAM_LR * ((nm / c1) / (jnp.sqrt(nv / c2) + ADAM_EPS) + ADAM_WD * w_refs[i][...])
            nm_refs[i][...] = nm
            nv_refs[i][...] = nv

    vmem = pl.BlockSpec(memory_space=pltpu.VMEM)
    shapes = [jax.ShapeDtypeStruct(t.shape, F32) for t in ws]
    outs = pl.pallas_call(body, in_specs=[vmem] * (4 * n), out_specs=[vmem] * (3 * n), out_shape=shapes * 3,
                          name=name, compiler_params=_params(vmem_mb=56))(*ws, *gs, *ms, *vs)
    return outs[:n], outs[n:2 * n], outs[2 * n:]


BIG = ("w_in", "w_glu", "w_out", "w_up", "w_down")
WEIGHTS = ("norm_mix_g", "w_in", "a_re", "a_im", "log_step", "b_re", "b_im", "c_re", "c_im", "d_skip", "w_glu",
           "sink", "norm_attn_g", "norm_ssm_g", "w_out", "norm_ffn_g", "w_up", "conv_w", "conv_b", "w_down",
           "norm_final_g")
SMALL = ("norm_mix_g", "a_re", "a_im", "log_step", "b_re", "b_im", "c_re", "c_im", "d_skip", "sink",
         "norm_attn_g", "norm_ssm_g", "norm_ffn_g", "conv_w", "conv_b", "norm_final_g")
SMALL_ROWS = 40
N_DEV = 8


def _by_owner(name, g):
    if name == "w_up":
        return g
    return g.reshape(4, g.shape[0] // 4, g.shape[1])


def _view(name, t):
    if name == "w_in":
        return jnp.swapaxes(t[0], 0, 1)
    if name in ("b_re", "b_im"):
        return jnp.swapaxes(t, -1, -2)
    return t


def _unview(name, t):
    if name == "w_in":
        return jnp.swapaxes(t, 0, 1)[None]
    if name in ("b_re", "b_im"):
        return jnp.swapaxes(t, -1, -2)
    return t


def kernel(x, norm_mix_g, w_in, a_re, a_im, log_step, b_re, b_im, c_re, c_im, d_skip, w_glu, sink, norm_attn_g, norm_ssm_g, w_out, norm_ffn_g, w_up, conv_w, conv_b, w_down, norm_final_g, loss_target, m_norm_mix_g, m_w_in, m_a_re, m_a_im, m_log_step, m_b_re, m_b_im, m_c_re, m_c_im, m_d_skip, m_w_glu, m_sink, m_norm_attn_g, m_norm_ssm_g, m_w_out, m_norm_ffn_g, m_w_up, m_conv_w, m_conv_b, m_w_down, m_norm_final_g, v_norm_mix_g, v_w_in, v_a_re, v_a_im, v_log_step, v_b_re, v_b_im, v_c_re, v_c_im, v_d_skip, v_w_glu, v_sink, v_norm_attn_g, v_norm_ssm_g, v_w_out, v_norm_ffn_g, v_w_up, v_conv_w, v_conv_b, v_w_down, v_norm_final_g):
    given = dict(locals())
    w = {n: given[n] for n in WEIGHTS}
    m = {n: given["m_" + n] for n in WEIGHTS}
    v = {n: given["v_" + n] for n in WEIGHTS}
    xy = 2 * lax.axis_index("x") + lax.axis_index("y")

    core = lax.axis_index("c")
    place = jnp.stack([core, xy]).astype(jnp.int32)

    conv_rows = jnp.pad(w["conv_w"][0], ((0, 2 * SUBLANES - 3), (0, 0)))
    rows = lambda t: t.reshape(4 * t.shape[1], t.shape[2])
    (w_in_all,) = _gather_weights([_view("w_in", w["w_in"])], [BF16])
    wb = {"w_in": rows(w_in_all)}
    early = ("w_in", "w_glu", "w_out")
    mixer = [_cast_place(w[n][0], place, BF16, w_in_all, "cast_" + n) for n in ("w_glu", "w_out")]
    mixer.append(_cast_place(conv_rows, place, F32, w_in_all, "cast_conv_w"))
    *mixer_flight, mixer_token = _spread_start(mixer, "spread_mixer_start")
    late = ("w_up", "w_down")
    *late_flight, token = _spread_start(
        [_cast_place(w[n][0], place, BF16, mixer_token, "cast_" + n) for n in late], "spread_ffn_start")

    def mixer_weights(after):
        w_glu4, w_out4, conv4 = _spread_wait(*mixer_flight, after, "spread_mixer_wait")
        return {"w_glu": rows(w_glu4), "w_out": rows(w_out4),
                "conv_w": conv4[:, :3].transpose(1, 0, 2).reshape(3, 2 * D_FF)}

    def late_weights(after):
        w_up4, w_down4 = _spread_wait(*late_flight, after, "spread_ffn_wait")
        return {"w_up": w_up4, "w_down": rows(w_down4)}

    sp = {n: w[n][0] for n in ("a_re", "a_im", "log_step", "b_re", "b_im", "c_re", "c_im", "d_skip",
                               "norm_mix_g", "norm_attn_g", "norm_ssm_g", "norm_ffn_g", "sink", "conv_b")}
    for n in ("norm_mix_g", "norm_attn_g", "norm_ssm_g", "norm_ffn_g", "sink", "conv_b"):
        sp[n] = sp[n].reshape(1, -1)
    sp["norm_mix_g"] = sp["norm_mix_g"] + token[:1, :1]
    sp["norm_final_g"] = w["norm_final_g"]
    flight = {}

    def ffn_grads_ready(dw_up, dw_down):
        *flight["pair"], token = _pair_start([dw_up, _by_owner("w_down", dw_down)])
        return token[:1, :1]

    def ffn_grads_next(after):
        mine, got = _pair_wait(*flight["pair"], after)
        sums, flight["own"] = zip(*[_pair_sum(a, b, place, BF16, "pair_sum_" + n) for n, a, b in zip(late, mine, got)])
        *flight["chip"], token = _chip_start(list(sums))
        return token[:1, :1]

    loss, grad_x, g = _local_step(x[0], loss_target[0], wb, sp, mixer_weights, late_weights, ffn_grads_ready,
                                  ffn_grads_next)

    flat = jnp.concatenate([g[n].reshape(-1) for n in SMALL] + [loss.reshape(-1)])
    pad = N_DEV * SMALL_ROWS * D_MODEL - flat.shape[0]
    small = jnp.concatenate([flat, jnp.zeros((pad,), F32)]).reshape(4, 2 * SMALL_ROWS, D_MODEL)
    by_owner = [_by_owner(n, g[n]) for n in early] + [small]
    got = _pair_exchange(by_owner)
    transit = [BF16] * len(early) + [F32]
    chip_sums, own_sums = zip(*[_pair_sum(a, b, place, t, "pair_sum_" + n)
                                for n, a, b, t in zip(early + ("small",), by_owner, got, transit)])
    landed = _chip_exchange(list(chip_sums))
    halves = {n: _chip_sum(o, t, "chip_sum_" + n) for n, o, t in zip(early + ("small",), own_sums, landed)}
    late_landed = _chip_wait(*flight["chip"], grad_x)
    for n, o, t in zip(late, flight["own"], late_landed):
        halves[n] = _chip_sum(o, t, "chip_sum_" + n)
    *others, small_all = _final_exchange([halves[n] for n in BIG], halves["small"])
    flat = small_all.reshape(-1)
    grads, off = {}, 0
    for n in SMALL:
        shape = (3, 4 * w[n].shape[-1]) if n == "conv_w" else w[n].shape[1:] if n != "norm_final_g" else w[n].shape
        size = math.prod(shape)
        grads[n] = flat[off:off + size].reshape(shape)
        off += size
    loss = flat[off]
    cw = w["conv_w"].shape[-1]
    grads["conv_w"] = lax.dynamic_slice_in_dim(grads["conv_w"], xy * cw, cw, axis=1)
    grads = {n: _view(n, grads[n].reshape(w[n].shape)) for n in SMALL}
    wv, mv, vv = ({n: _view(n, t[n]) for n in WEIGHTS} for t in (w, m, v))

    delta, new_m, new_v = {}, {}, {}
    for n, other in zip(BIG, others):
        two_d = lambda t: t.reshape(t.shape[-2:])
        grads[n], delta[n], new_m[n], new_v[n] = _adamw_halves(
            two_d(wv[n]), halves[n], other, two_d(mv[n]), two_d(vv[n]), place, "adamw_" + n)
    for group, name in ((("b_re", "b_im"), "adamw_b"), (tuple(n for n in SMALL if n not in ("b_re", "b_im")), "adamw_small")):
        row = lambda t: t.reshape(1, -1) if t.ndim == 1 else t
        d_, m_, v_ = _adamw_many(*[[row(t[n]) for n in group] for t in (wv, grads, mv, vv)], name)
        for n, dn, mn, vn in zip(group, d_, m_, v_):
            delta[n], new_m[n], new_v[n] = (t.reshape(wv[n].shape) for t in (dn, mn, vn))
    natural = lambda t: [_unview(n, t[n].reshape(wv[n].shape)) for n in WEIGHTS]
    return (loss, grad_x[None], *natural(grads), *natural(delta), *natural(new_m), *natural(new_v))
```

```python
import functools
import math

import jax
import jax.numpy as jnp
import numpy as np
from jax import lax
from jax.experimental import pallas as pl
from jax.experimental.pallas import tpu as pltpu

F32 = jnp.float32
BF16 = jnp.bfloat16

D_MODEL = 1024
N_Q_HEADS = 8
N_KV_HEADS = 2
HEAD_DIM = 64
ATTN_WIDTH = 512
KV_WIDTH = 128
QKV_WIDTH = ATTN_WIDTH + 2 * KV_WIDTH
WINDOW = 128
BLOCK = 128
ROPE_DIM = 16
ROPE_THETA = 500000.0
SSM_WIDTH = 512
SSM_GROUP = 16
N_SSM_GROUPS = 32
SSM_STATE = 64
IN_WIDTH = 1280
D_FF = 2816
EPS = 1e-6
ADAM_LR = 0.001
ADAM_B1 = 0.9
ADAM_B2 = 0.999
ADAM_EPS = 1e-08
ADAM_WD = 0.01
ADAM_STEP = 10

VMEM_BYTES_V7X = 64 * 1024 * 1024
SUBLANES = 8
LANES = 128
SSM_CB = 4
SSM_CH = 128
SSM_ST = 512
N_SEG = SUBLANES

NN = (((1,), (0,)), ((), ()))
NT = (((1,), (1,)), ((), ()))
TN = (((0,), (0,)), ((), ()))


def _params(sem=None, vmem_mb=48):
    limit = vmem_mb * 1024 * 1024
    assert limit < VMEM_BYTES_V7X
    return pltpu.CompilerParams(dimension_semantics=sem, vmem_limit_bytes=limit)


def _dg(a, b, dims):
    return lax.dot_general(a, b, dims, preferred_element_type=F32)


def _sigmoid(x):
    return 1.0 / (1.0 + jnp.exp(-x))


_SQRT_HALF = 0.7071067811865476
_INV_SQRT_2PI = 0.3989422804014327


def _gelu(x):
    return 0.5 * x * (1.0 + lax.erf(x * _SQRT_HALF))


def _gelu_grad(x):
    return 0.5 * (1.0 + lax.erf(x * _SQRT_HALF)) + x * (_INV_SQRT_2PI * jnp.exp(-0.5 * x * x))


def _mm_tn(a, b, tm, tn, name):
    k, m = a.shape
    n = b.shape[1]

    def body(a_ref, b_ref, o_ref):
        o_ref[...] = _dg(a_ref[...], b_ref[...], TN)

    return pl.pallas_call(
        body, grid=(m // tm, n // tn),
        in_specs=[pl.BlockSpec((k, tm), lambda i, j: (0, i)), pl.BlockSpec((k, tn), lambda i, j: (0, j))],
        out_specs=pl.BlockSpec((tm, tn), lambda i, j: (i, j)),
        out_shape=jax.ShapeDtypeStruct((m, n), F32), name=name,
        compiler_params=_params(("parallel", "parallel")),
    )(a, b)


def _mm_nn_cols(a, b4, tm, name):
    m, k = a.shape
    s, _, n = b4.shape

    def body(a_ref, b_ref, o_ref):
        o_ref[...] = _dg(a_ref[...], b_ref[...], NN)

    return pl.pallas_call(
        body, grid=(m // tm, s),
        in_specs=[pl.BlockSpec((tm, k), lambda i, j: (i, 0)), pl.BlockSpec((None, k, n), lambda i, j: (j, 0, 0))],
        out_specs=pl.BlockSpec((tm, n), lambda i, j: (i, j)),
        out_shape=jax.ShapeDtypeStruct((m, s * n), F32), name=name,
        compiler_params=_params(("parallel", "parallel")),
    )(a, b4)


def _mm_tn_cols(a, b2, s, tm, name):
    k, m = a.shape
    h, _, wide = b2.shape
    per = s // h
    n = wide // per

    def body(a_ref, b_ref, o_ref):
        o_ref[...] = _dg(a_ref[...], b_ref[...], TN)

    return pl.pallas_call(
        body, grid=(s, m // tm),
        in_specs=[pl.BlockSpec((k, tm), lambda j, i: (0, i)),
                  pl.BlockSpec((None, k, n), lambda j, i: (j // per, 0, j % per))],
        out_specs=pl.BlockSpec((None, tm, n), lambda j, i: (j, i, 0)),
        out_shape=jax.ShapeDtypeStruct((s, m, n), F32), name=name,
        compiler_params=_params(("parallel", "parallel")),
    )(a, b2)


TM_EW = 256


def _rms_bwd_vals(xv, gv, dy):
    r = lax.rsqrt(jnp.mean(xv * xv, axis=-1, keepdims=True) + EPS)
    xh = xv * r
    dxh = dy * gv
    dx = r * (dxh - xh * jnp.mean(dxh * xh, axis=-1, keepdims=True))
    return dx, dy * xh


TM_FUSED = 256


def _rms_vals(xv, gv):
    return xv * lax.rsqrt(jnp.mean(xv * xv, axis=-1, keepdims=True) + EPS) * gv


def _rope_blocks(src, dst, c, lo, hi):
    nq = ATTN_WIDTH // LANES
    for blk in range(nq + 1):
        t = src[:, blk * LANES:(blk + 1) * LANES]
        dst[:, blk * LANES:(blk + 1) * LANES] = (
            t * c + pltpu.roll(t, LANES - 8, 1) * lo + pltpu.roll(t, 8, 1) * hi).astype(BF16)
    dst[:, (nq + 1) * LANES:] = src[:, (nq + 1) * LANES:].astype(BF16)


def _rms_mm_rope(x, g, wt, tabs, name):
    l, d = x.shape
    n = wt.shape[0]

    def body(x_ref, g_ref, w_ref, c_ref, lo_ref, hi_ref, h_ref, qkv_ref, u_ref):
        h = _rms_vals(x_ref[...], g_ref[...]).astype(BF16)
        h_ref[...] = h
        out = _dg(h, w_ref[...], NT)
        _rope_blocks(out[:, :QKV_WIDTH], qkv_ref, c_ref[...], lo_ref[...], hi_ref[...])
        u_ref[...] = out[:, QKV_WIDTH:]

    row = lambda width: pl.BlockSpec((TM_FUSED, width), lambda i: (i, 0))
    return pl.pallas_call(
        body, grid=(l // TM_FUSED,),
        in_specs=[row(d), pl.BlockSpec((1, d), lambda i: (0, 0)), pl.BlockSpec((n, d), lambda i: (0, 0)),
                  row(LANES), row(LANES), row(LANES)],
        out_specs=[row(d), row(QKV_WIDTH), row(n - QKV_WIDTH)],
        out_shape=[jax.ShapeDtypeStruct((l, d), BF16), jax.ShapeDtypeStruct((l, QKV_WIDTH), BF16),
                   jax.ShapeDtypeStruct((l, n - QKV_WIDTH), F32)],
        name=name, compiler_params=_params(("parallel",)),
    )(x, g, wt, *tabs)


def _mix_mm_res_rms(attn, ys, g_attn, g_ssm, b, res, g, name):
    l, w = attn.shape
    d = b.shape[1]

    def body(a_ref, y_ref, ga_ref, gs_ref, b_ref, r_ref, g_ref, m_ref, x_ref, h_ref):
        m_ref[:, :w] = _rms_vals(a_ref[...], ga_ref[...]).astype(BF16)
        m_ref[:, w:] = _rms_vals(y_ref[...], gs_ref[...]).astype(BF16)
        xv = r_ref[...] + _dg(m_ref[...], b_ref[...], NN)
        x_ref[...] = xv
        h_ref[...] = _rms_vals(xv, g_ref[...]).astype(BF16)

    row = lambda width: pl.BlockSpec((TM_FUSED, width), lambda i: (i, 0))
    vec = lambda width: pl.BlockSpec((1, width), lambda i: (0, 0))
    return pl.pallas_call(
        body, grid=(l // TM_FUSED,),
        in_specs=[row(w), row(w), vec(w), vec(w), pl.BlockSpec((2 * w, d), lambda i: (0, 0)), row(d), vec(d)],
        out_specs=[row(2 * w), row(d), row(d)],
        out_shape=[jax.ShapeDtypeStruct((l, 2 * w), BF16), jax.ShapeDtypeStruct((l, d), F32),
                   jax.ShapeDtypeStruct((l, d), BF16)],
        name=name, compiler_params=_params(("parallel",)),
    )(attn, ys, g_attn, g_ssm, b, res, g)


def _mm_res_loss(a, b, res, g, target):
    l, k = a.shape
    d = b.shape[1]

    def body(a_ref, b_ref, r_ref, g_ref, t_ref, loss_ref, dx_ref, dxb_ref, dg_ref):
        xv = r_ref[...] + _dg(a_ref[...], b_ref[...], NN)
        gv = g_ref[...]
        r = lax.rsqrt(jnp.mean(xv * xv, axis=-1, keepdims=True) + EPS)
        xh = xv * r
        e = xh * gv - t_ref[...]
        part = jnp.sum(jnp.sum(e * e, axis=1, keepdims=True), axis=0, keepdims=True) * (0.5 / d)
        dy = e * (1.0 / d)
        dxh = dy * gv
        dx = r * (dxh - xh * jnp.mean(dxh * xh, axis=-1, keepdims=True))
        dx_ref[...] = dx
        dxb_ref[...] = dx.astype(BF16)

        @pl.when(pl.program_id(0) == 0)
        def _():
            dg_ref[...] = jnp.zeros_like(dg_ref)
            loss_ref[...] = jnp.zeros_like(loss_ref)

        dg_ref[...] += jnp.sum(dy * xh, axis=0, keepdims=True)
        loss_ref[...] += part

    row = lambda width: pl.BlockSpec((TM_FUSED, width), lambda i: (i, 0))
    vec = pl.BlockSpec((1, d), lambda i: (0, 0))
    return pl.pallas_call(
        body, grid=(l // TM_FUSED,),
        in_specs=[row(k), pl.BlockSpec((k, d), lambda i: (0, 0)), row(d), vec, row(d)],
        out_specs=[pl.BlockSpec((1, 1), lambda i: (0, 0)), row(d), row(d), vec],
        out_shape=[jax.ShapeDtypeStruct((1, 1), F32), jax.ShapeDtypeStruct((l, d), F32),
                   jax.ShapeDtypeStruct((l, d), BF16), jax.ShapeDtypeStruct((1, d), F32)],
        name="mm_down_loss", compiler_params=_params(("arbitrary",)),
    )(a, b, res, g, target)


def _mm_rms_bwd(a, b, a_spec, b_spec, matmul, x, g, res, name):
    l, d = x.shape

    def body(a_ref, b_ref, x_ref, g_ref, res_ref, dx_ref, dxb_ref, dg_ref):
        dx, dgr = _rms_bwd_vals(x_ref[...], g_ref[...], matmul(a_ref, b_ref))
        dx = dx + res_ref[...]
        dx_ref[...] = dx
        dxb_ref[...] = dx.astype(BF16)

        @pl.when(pl.program_id(0) == 0)
        def _():
            dg_ref[...] = jnp.zeros_like(dg_ref)

        dg_ref[...] += jnp.sum(dgr, axis=0, keepdims=True)

    row = pl.BlockSpec((TM_FUSED, d), lambda i: (i, 0))
    vec = pl.BlockSpec((1, d), lambda i: (0, 0))
    return pl.pallas_call(
        body, grid=(l // TM_FUSED,), in_specs=[a_spec, b_spec, row, vec, row], out_specs=[row, row, vec],
        out_shape=[jax.ShapeDtypeStruct((l, d), F32), jax.ShapeDtypeStruct((l, d), BF16),
                   jax.ShapeDtypeStruct((1, d), F32)],
        name=name, compiler_params=_params(("arbitrary",)),
    )(a, b, x, g, res)


def _mm_nn_rms_bwd(a, b, x, g, res, name):
    return _mm_rms_bwd(a, b, pl.BlockSpec((TM_FUSED, a.shape[1]), lambda i: (i, 0)),
                       pl.BlockSpec(b.shape, lambda i: (0, 0)),
                       lambda a_ref, b_ref: _dg(a_ref[...], b_ref[...], NN), x, g, res, name)


def _mm_cols_rms_bwd(a2, b4, x, g, res, name):
    h, _, wide = a2.shape
    s, _, n = b4.shape
    per = s // h

    def matmul(a_ref, b_ref):
        acc = None
        for j in range(s):
            part = _dg(a_ref[j // per, :, (j % per) * n:(j % per + 1) * n], b_ref[j], NT)
            acc = part if acc is None else acc + part
        return acc

    return _mm_rms_bwd(a2, b4, pl.BlockSpec((h, TM_FUSED, wide), lambda i: (0, i, 0)),
                       pl.BlockSpec(b4.shape, lambda i: (0, 0, 0)), matmul, x, g, res, name)


def _mm_mix_bwd(dx, b, attn, ys, g_attn, g_ssm, name):
    l, w = attn.shape
    d = dx.shape[1]

    def body(dx_ref, b_ref, a_ref, y_ref, ga_ref, gs_ref, da_ref, dy_ref, dga_ref, dgs_ref):
        @pl.when(pl.program_id(0) == 0)
        def _():
            dga_ref[...] = jnp.zeros_like(dga_ref)
            dgs_ref[...] = jnp.zeros_like(dgs_ref)

        dm = _dg(dx_ref[...], b_ref[...], NT)
        for src, gr, off, dst, dgr in ((a_ref, ga_ref, 0, da_ref, dga_ref), (y_ref, gs_ref, w, dy_ref, dgs_ref)):
            dxv, dg_rows = _rms_bwd_vals(src[...], gr[...], dm[:, off:off + w])
            dst[...] = dxv
            dgr[...] += jnp.sum(dg_rows, axis=0, keepdims=True)

    row = lambda width: pl.BlockSpec((TM_FUSED, width), lambda i: (i, 0))
    vec = pl.BlockSpec((1, w), lambda i: (0, 0))
    return pl.pallas_call(
        body, grid=(l // TM_FUSED,),
        in_specs=[row(d), pl.BlockSpec((2 * w, d), lambda i: (0, 0)), row(w), row(w), vec, vec],
        out_specs=[row(w), row(w), vec, vec],
        out_shape=[jax.ShapeDtypeStruct((l, w), F32), jax.ShapeDtypeStruct((l, w), F32),
                   jax.ShapeDtypeStruct((1, w), F32), jax.ShapeDtypeStruct((1, w), F32)],
        name=name, compiler_params=_params(("arbitrary",)),
    )(dx, b, attn, ys, g_attn, g_ssm)


def _rope_tables(l):
    half = ROPE_DIM // 2
    f32 = np.float32
    inv_freq = np.power(f32(ROPE_THETA), -np.arange(half, dtype=f32) / f32(half))
    ang = np.arange(l, dtype=f32)[:, None] * inv_freq[None, :]
    cos, sin = np.cos(ang), np.sin(ang)
    ones = np.ones((l, HEAD_DIM - ROPE_DIM), f32)
    zeros = np.zeros((l, HEAD_DIM - ROPE_DIM), f32)
    zh = np.zeros((l, half), f32)
    c = np.concatenate([cos, cos, ones], axis=1)
    s_lo = np.concatenate([-sin, zh, zeros], axis=1)
    s_hi = np.concatenate([zh, sin, zeros], axis=1)
    return tuple(jnp.asarray(np.tile(t, (1, LANES // HEAD_DIM)), F32) for t in (c, s_lo, s_hi))


def _rope_bwd(dq, dkv, du_ssm, dpre, d_skip, tabs):
    l = dq.shape[0]
    nq = ATTN_WIDTH // LANES

    def body(dq_ref, dkv_ref, du_ref, dpre_ref, ds_ref, c_ref, lo_ref, hi_ref, o_ref):
        c, lo, hi = c_ref[...], lo_ref[...], hi_ref[...]
        for blk in range(nq + 1):
            t = dq_ref[:, blk * LANES:(blk + 1) * LANES] if blk < nq else dkv_ref[:, :KV_WIDTH]
            g = t * c + pltpu.roll(t * lo, 8, 1) + pltpu.roll(t * hi, LANES - 8, 1)
            o_ref[:, blk * LANES:(blk + 1) * LANES] = g.astype(BF16)
        o_ref[:, (nq + 1) * LANES:QKV_WIDTH] = dkv_ref[:, KV_WIDTH:].astype(BF16)
        o_ref[:, QKV_WIDTH:] = (du_ref[...] + dpre_ref[...] * ds_ref[...]).astype(BF16)

    tab = pl.BlockSpec((TM_EW, LANES), lambda i: (i, 0))
    wide = pl.BlockSpec((TM_EW, SSM_WIDTH), lambda i: (i, 0))
    return pl.pallas_call(
        body, grid=(l // TM_EW,),
        in_specs=[wide, pl.BlockSpec((TM_EW, 2 * KV_WIDTH), lambda i: (i, 0)), wide, wide,
                  pl.BlockSpec((1, SSM_WIDTH), lambda i: (0, 0)), tab, tab, tab],
        out_specs=pl.BlockSpec((TM_EW, IN_WIDTH), lambda i: (i, 0)),
        out_shape=jax.ShapeDtypeStruct((l, IN_WIDTH), BF16), name="rope_bwd",
        compiler_params=_params(("parallel",)),
    )(dq, dkv, du_ssm, dpre, d_skip, *tabs)


_Q_COLS = ATTN_WIDTH // LANES
_SCALE = HEAD_DIM ** -0.5
_NEG = -1e30


def _window_specs(nb, width, col):
    return [
        pl.BlockSpec((BLOCK, width), lambda n: (jnp.maximum(n - 1, 0), col)),
        pl.BlockSpec((BLOCK, width), lambda n: (n, col)),
        pl.BlockSpec((BLOCK, width), lambda n: (jnp.minimum(n + 1, nb - 1), col)),
    ]


def _stacked_sink(sink_ref, heads):
    rid = lax.broadcasted_iota(jnp.int32, (len(heads) * BLOCK, 1), 0)
    sk = jnp.full(rid.shape, sink_ref[0, heads[-1]], F32)
    for g in range(len(heads) - 2, -1, -1):
        sk = jnp.where(rid < (g + 1) * BLOCK, sink_ref[0, heads[g]], sk)
    return sk


def _attn_fwd(qkv, sink):
    l = qkv.shape[0]
    nb = l // BLOCK
    grp = N_Q_HEADS // N_KV_HEADS

    def body(sink_ref, q_ref, k0, k1, k2, v0, v1, v2, o_ref, lse_ref):
        n = pl.program_id(0)
        q = q_ref[...]
        kw = jnp.concatenate([k0[...], k1[...], k2[...]], axis=0)
        vw = jnp.concatenate([v0[...], v1[...], v2[...]], axis=0)
        row = lax.broadcasted_iota(jnp.int32, (grp * BLOCK, 3 * BLOCK), 0)
        col = lax.broadcasted_iota(jnp.int32, (grp * BLOCK, 3 * BLOCK), 1)
        valid = jnp.abs(col - BLOCK - (row & (BLOCK - 1))) <= WINDOW
        valid &= jnp.logical_not((n == 0) & (col < BLOCK))
        valid &= jnp.logical_not((n == nb - 1) & (col >= 2 * BLOCK))
        for hk in range(N_KV_HEADS):
            heads = range(hk * grp, (hk + 1) * grp)
            qs = jnp.concatenate([q[:, h * HEAD_DIM:(h + 1) * HEAD_DIM] for h in heads], axis=0)
            kh = kw[:, hk * HEAD_DIM:(hk + 1) * HEAD_DIM]
            vh = vw[:, hk * HEAD_DIM:(hk + 1) * HEAD_DIM]
            s = jnp.where(valid, _dg(qs, kh, NT) * _SCALE, _NEG)
            sk = _stacked_sink(sink_ref, heads)
            m = jnp.maximum(jnp.max(s, axis=1, keepdims=True), sk)
            p = jnp.exp(s - m)
            denom = jnp.sum(p, axis=1, keepdims=True) + jnp.exp(sk - m)
            o = _dg((p / denom).astype(BF16), vh, NN)
            lse = m + jnp.log(denom)
            for g, h in enumerate(heads):
                o_ref[:, h * HEAD_DIM:(h + 1) * HEAD_DIM] = o[g * BLOCK:(g + 1) * BLOCK]
                lse_ref[:, h:h + 1] = lse[g * BLOCK:(g + 1) * BLOCK]

    return pl.pallas_call(
        body, grid=(nb,),
        in_specs=[pl.BlockSpec(memory_space=pltpu.SMEM),
                  pl.BlockSpec((BLOCK, ATTN_WIDTH), lambda n: (n, 0))]
        + _window_specs(nb, KV_WIDTH, _Q_COLS) + _window_specs(nb, KV_WIDTH, _Q_COLS + 1),
        out_specs=[pl.BlockSpec((BLOCK, ATTN_WIDTH), lambda n: (n, 0)),
                   pl.BlockSpec((BLOCK, N_Q_HEADS), lambda n: (n, 0))],
        out_shape=[jax.ShapeDtypeStruct((l, ATTN_WIDTH), F32), jax.ShapeDtypeStruct((l, N_Q_HEADS), F32)],
        name="attn_fwd", compiler_params=_params(("parallel",)),
    )(sink, qkv, qkv, qkv, qkv, qkv, qkv, qkv)


def _attn_bwd(qkv, attn, dattn, lse, sink):
    l = qkv.shape[0]
    nb = l // BLOCK
    grp = N_Q_HEADS // N_KV_HEADS
    win = 3 * BLOCK

    def body(sink_ref, q_ref, k0, k1, k2, v0, v1, v2, o_ref, d_ref, l_ref, dq_ref, dkv_ref, dsink_ref, ring_ref):
        n = pl.program_id(0)

        @pl.when(n == 0)
        def _():
            dsink_ref[...] = jnp.zeros_like(dsink_ref)
            ring_ref[...] = jnp.zeros_like(ring_ref)

        @pl.when(n < nb)
        def _():
            first, last = n == 0, n == nb - 1
            cat = lambda a, b, c: jnp.concatenate([a[...], b[...], c[...]], axis=0)
            q, kw, vw = q_ref[...], cat(k0, k1, k2), cat(v0, v1, v2)
            dov = d_ref[...]
            prod = o_ref[...] * dov
            dob = dov.astype(BF16)
            lse = l_ref[...]
            row = lax.broadcasted_iota(jnp.int32, (grp * BLOCK, win), 0)
            col = lax.broadcasted_iota(jnp.int32, (grp * BLOCK, win), 1)
            valid = jnp.abs(col - BLOCK - (row & (BLOCK - 1))) <= WINDOW
            valid &= jnp.logical_not(first & (col < BLOCK))
            valid &= jnp.logical_not(last & (col >= 2 * BLOCK))

            dsink_parts, dks, dvs = [], [], []
            for hk in range(N_KV_HEADS):
                heads = range(hk * grp, (hk + 1) * grp)
                ksl = slice(hk * HEAD_DIM, (hk + 1) * HEAD_DIM)
                hsl = [slice(h * HEAD_DIM, (h + 1) * HEAD_DIM) for h in heads]
                stack = lambda parts: jnp.concatenate(parts, axis=0)
                qs = stack([q[:, s_] for s_ in hsl])
                dos = stack([dob[:, s_] for s_ in hsl])
                deltas = stack([jnp.sum(prod[:, s_], axis=1, keepdims=True) for s_ in hsl])
                lses = stack([lse[:, h:h + 1] for h in heads])
                kh, vh = kw[:, ksl], vw[:, ksl]
                s = jnp.where(valid, _dg(qs, kh, NT) * _SCALE, _NEG)
                p = jnp.exp(s - lses)
                dp = _dg(dos, vh, NT)
                ds = (p * (dp - deltas) * _SCALE).astype(BF16)
                dq = _dg(ds, kh, NN)
                sink_rows = jnp.exp(_stacked_sink(sink_ref, heads) - lses) * deltas
                for g in range(grp):
                    dq_ref[:, hsl[g]] = dq[g * BLOCK:(g + 1) * BLOCK]
                    dsink_parts.append(jnp.sum(sink_rows[g * BLOCK:(g + 1) * BLOCK], axis=0, keepdims=True))
                dks.append(_dg(ds, qs, TN))
                dvs.append(_dg(p.astype(BF16), dos, TN))
            dsink_ref[...] -= jnp.concatenate(dsink_parts, axis=1)
            part = jnp.concatenate(dks + dvs, axis=1)
            ring_ref[(n + 2) % 3] += part[0:BLOCK]
            ring_ref[n % 3] += part[BLOCK:2 * BLOCK]
            ring_ref[(n + 1) % 3] = part[2 * BLOCK:]

        @pl.when(n >= 1)
        def _():
            dkv_ref[...] = ring_ref[(n + 2) % 3]

    centre = lambda n: jnp.minimum(n, nb - 1)
    window = lambda width, col: [
        pl.BlockSpec((BLOCK, width), lambda n: (jnp.maximum(centre(n) - 1, 0), col)),
        pl.BlockSpec((BLOCK, width), lambda n: (centre(n), col)),
        pl.BlockSpec((BLOCK, width), lambda n: (jnp.minimum(centre(n) + 1, nb - 1), col))]
    own = lambda width: pl.BlockSpec((BLOCK, width), lambda n: (centre(n), 0))
    return pl.pallas_call(
        body, grid=(nb + 1,),
        in_specs=[pl.BlockSpec(memory_space=pltpu.SMEM), own(ATTN_WIDTH)]
        + window(KV_WIDTH, _Q_COLS) + window(KV_WIDTH, _Q_COLS + 1)
        + [own(ATTN_WIDTH), own(ATTN_WIDTH), own(N_Q_HEADS)],
        out_specs=[own(ATTN_WIDTH), pl.BlockSpec((BLOCK, 2 * KV_WIDTH), lambda n: (jnp.maximum(n - 1, 0), 0)),
                   pl.BlockSpec((1, N_Q_HEADS), lambda n: (0, 0))],
        out_shape=[jax.ShapeDtypeStruct((l, ATTN_WIDTH), F32), jax.ShapeDtypeStruct((l, 2 * KV_WIDTH), F32),
                   jax.ShapeDtypeStruct((1, N_Q_HEADS), F32)],
        scratch_shapes=[pltpu.VMEM((3, BLOCK, 2 * KV_WIDTH), F32)],
        name="attn_bwd", compiler_params=_params(("arbitrary",)),
    )(sink, qkv, qkv, qkv, qkv, qkv, qkv, qkv, attn, dattn, lse)


def _ssm_disc(a_re, a_im, log_step, b_re, b_im):
    step = jnp.exp(log_step)[..., None]
    mag = jnp.exp(a_re * step)
    lb_re, lb_im = mag * jnp.cos(a_im * step), mag * jnp.sin(a_im * step)
    nr, ni = lb_re - 1.0, lb_im
    den = a_re * a_re + a_im * a_im
    f_re = ((nr * a_re + ni * a_im) / den)[..., None]
    f_im = ((ni * a_re - nr * a_im) / den)[..., None]
    return lb_re, lb_im, f_re * b_re - f_im * b_im, f_re * b_im + f_im * b_re


def _ssm_pack(lb_re, lb_im, bb_re, bb_im, c_re, c_im):
    eye = jnp.eye(SSM_CH // SSM_GROUP, dtype=F32)
    ng = SSM_CH // SSM_GROUP

    def diag_b(bb):
        t = bb.reshape(2, SSM_CB, ng, SSM_STATE, SSM_GROUP)
        return jnp.einsum('dkgpc,gh->dkgchp', t, eye).reshape(2, SSM_CB, SSM_CH, SSM_ST)

    def diag_c(cc):
        t = cc.reshape(2, SSM_CB, ng, SSM_GROUP, SSM_STATE)
        return jnp.einsum('dkgcp,gh->dkhpgc', t, eye).reshape(2, SSM_CB, SSM_ST, SSM_CH)

    bcat = jnp.concatenate([diag_b(bb_re), diag_b(bb_im)], axis=-1)
    ccat = jnp.concatenate([diag_c(c_re), -diag_c(c_im)], axis=-2)
    lam_re = lb_re.reshape(2, SSM_CB, 1, SSM_ST)
    lam_im = lb_im.reshape(2, SSM_CB, 1, SSM_ST)
    return bcat, ccat, lam_re, lam_im


def _ssm_unpack(dbcat, dccat, dlam_re, dlam_im):
    ng = SSM_CH // SSM_GROUP
    eye = jnp.eye(ng, dtype=F32)

    def undiag_b(t):
        t = t.reshape(2, SSM_CB, ng, SSM_GROUP, ng, SSM_STATE)
        return jnp.einsum('dkgchp,gh->dkgpc', t, eye).reshape(2, N_SSM_GROUPS, SSM_STATE, SSM_GROUP)

    def undiag_c(t):
        t = t.reshape(2, SSM_CB, ng, SSM_STATE, ng, SSM_GROUP)
        return jnp.einsum('dkhpgc,gh->dkgcp', t, eye).reshape(2, N_SSM_GROUPS, SSM_GROUP, SSM_STATE)

    dbb_re, dbb_im = undiag_b(dbcat[..., :SSM_ST]), undiag_b(dbcat[..., SSM_ST:])
    dc_re, dc_im = undiag_c(dccat[:, :, :SSM_ST]), -undiag_c(dccat[:, :, SSM_ST:])
    shape = (2, N_SSM_GROUPS, SSM_STATE)
    return dlam_re.reshape(shape), dlam_im.reshape(shape), dbb_re, dbb_im, dc_re, dc_im


def _to_segments(t):
    l, w = t.shape
    return t.reshape(N_SEG, l // N_SEG, w).transpose(1, 0, 2).reshape(l, w)


def _from_segments(t):
    l, w = t.shape
    return t.reshape(l // N_SEG, N_SEG, w).transpose(1, 0, 2).reshape(l, w)


SSM_RC = 256
SSM_JC = SSM_RC // N_SEG
_RE, _IM = pl.ds(0, SSM_ST), pl.ds(SSM_ST, SSM_ST)


def _cfma(ar, ai, xr, xi, br, bi):
    return ar * xr - ai * xi + br, ar * xi + ai * xr + bi


def _chunk_rows(ci, rev, nc):
    start = jnp.where(rev, (nc - 1 - ci) * SSM_RC, ci * SSM_RC)
    return pl.ds(pl.multiple_of(start, SSM_RC), SSM_RC)


def _scan_chunk(src, dst, ar, ai, rev, nj, ci, carry, prev_ref=None):
    def rows_of(staged, j, k):
        at = jnp.where(rev, SSM_JC - 1 - k, k) if staged else j
        return pl.ds(pl.multiple_of(at * N_SEG, N_SEG), N_SEG)

    for k in range(SSM_JC):
        jj = ci * SSM_JC + k
        j = jnp.where(rev, nj - 1 - jj, jj)
        rows = rows_of(src[1], j, k)
        nr, ni = _cfma(ar, ai, carry[0], carry[1], src[0][rows, _RE], src[0][rows, _IM])
        if dst is not None:
            rows = rows_of(dst[1], j, k)
            dst[0][rows, _RE] = nr
            dst[0][rows, _IM] = ni
        if prev_ref is None:
            carry = (nr, ni)
            continue
        jp = jnp.where(rev, j - 1, j + 1)
        if k == SSM_JC - 1:
            inside = jnp.where((jp >= 0) & (jp < nj), 1.0, 0.0)
            jp = jnp.clip(jp, 0, nj - 1)
        prow = pl.ds(pl.multiple_of(jp * N_SEG, N_SEG), N_SEG)
        xr, xi = prev_ref[prow, _RE], prev_ref[prow, _IM]
        sr, si = nr * xr + ni * xi, ni * xr - nr * xi
        if k == SSM_JC - 1:
            sr, si = inside * sr, inside * si
        carry = (nr, ni, carry[2] + sr, carry[3] + si)
    return carry


def _segment_inits(ar, ai, end_r, end_i, rev, nj):
    pr, pi = ar, ai
    for _ in range(int(math.log2(nj))):
        pr, pi = pr * pr - pi * pi, 2.0 * pr * pi
    seg = lax.broadcasted_iota(jnp.int32, end_r.shape, 0)
    zero = jnp.zeros_like(end_r)

    def chain(shift, keep):
        ir, ii = zero, zero
        for _ in range(N_SEG - 1):
            tr, ti = _cfma(pr, pi, ir, ii, end_r, end_i)
            ir = jnp.where(keep, pltpu.roll(tr, shift, 0), 0.0)
            ii = jnp.where(keep, pltpu.roll(ti, shift, 0), 0.0)
        return ir, ii

    up_r, up_i = chain(1, seg >= 1)
    dn_r, dn_i = chain(N_SEG - 1, seg <= N_SEG - 2)
    return jnp.where(rev, dn_r, up_r), jnp.where(rev, dn_i, up_i)


def _ssm_specs(l):
    act = pl.BlockSpec((l, SSM_CH), lambda k, d: (0, k))
    bmat = pl.BlockSpec((None, None, SSM_CH, 2 * SSM_ST), lambda k, d: (d, k, 0, 0))
    cmat = pl.BlockSpec((None, None, 2 * SSM_ST, SSM_CH), lambda k, d: (d, k, 0, 0))
    lam = pl.BlockSpec((None, None, 1, SSM_ST), lambda k, d: (d, k, 0, 0))
    return act, bmat, cmat, lam


def _ssm_fwd(u_seg, bcat, ccat, lam_re, lam_im):
    l = u_seg.shape[0]
    nj = l // N_SEG
    nc = l // SSM_RC

    def body(u_ref, b_ref, c_ref, lr_ref, li_ref, y_ref, keep_ref, xs_ref, stage0, stage1, keep_sem):
        k, d = pl.program_id(0), pl.program_id(1)
        rev = d == 1
        shape = (N_SEG, SSM_ST)
        ar, ai = jnp.broadcast_to(lr_ref[...], shape), jnp.broadcast_to(li_ref[...], shape)
        zero = jnp.zeros(shape, F32)

        def inputs(ci, stage):
            rows = _chunk_rows(ci, rev, nc)
            bu = _dg(u_ref[rows, :], b_ref[...], NN)
            stage[...] = bu
            xs_ref[rows, :] = bu

        def first(stage, ci, carry):
            return _scan_chunk((stage, True), None, ar, ai, rev, nj, ci, carry)

        def first_pass(t, carry):
            inputs(2 * t + 1, stage1)
            carry = first(stage0, 2 * t, carry)
            inputs(2 * t + 2, stage0)
            return first(stage1, 2 * t + 1, carry)

        inputs(0, stage0)
        carry = lax.fori_loop(0, nc // 2 - 1, first_pass, (zero, zero))
        inputs(nc - 1, stage1)
        carry = first(stage0, nc - 2, carry)
        end_r, end_i = first(stage1, nc - 1, carry)
        init = _segment_inits(ar, ai, end_r, end_i, rev, nj)

        @pl.when(d == 0)
        def _():
            y_ref[...] = jnp.zeros_like(y_ref)

        def outputs(ci):
            rows = _chunk_rows(ci, rev, nc)
            y_ref[rows, :] += _dg(xs_ref[rows, :].astype(BF16), c_ref[...], NN)
            pltpu.make_async_copy(xs_ref.at[rows], keep_ref.at[d, k, rows], keep_sem).start()

        def second(ci, carry):
            return _scan_chunk((xs_ref, False), (xs_ref, False), ar, ai, rev, nj, ci, carry)

        def second_pass(ci, carry):
            outputs(ci - 1)
            return second(ci, carry)

        lax.fori_loop(1, nc, second_pass, second(0, init))
        outputs(nc - 1)
        pltpu.make_async_copy(xs_ref, keep_ref.at[d, k], keep_sem).wait()

    act, bmat, cmat, lam = _ssm_specs(l)
    return pl.pallas_call(
        body, grid=(SSM_CB, 2), in_specs=[act, bmat, cmat, lam, lam], out_specs=[act, ANY],
        out_shape=[jax.ShapeDtypeStruct((l, SSM_WIDTH), F32),
                   jax.ShapeDtypeStruct((2, SSM_CB, l, 2 * SSM_ST), F32)],
        scratch_shapes=[pltpu.VMEM((l, 2 * SSM_ST), F32), pltpu.VMEM((SSM_RC, 2 * SSM_ST), F32),
                        pltpu.VMEM((SSM_RC, 2 * SSM_ST), F32), pltpu.SemaphoreType.DMA],
        name="ssm_fwd", compiler_params=_params(("parallel", "arbitrary"), vmem_mb=56),
    )(u_seg, bcat.astype(BF16), ccat.astype(BF16), lam_re, lam_im)


def _ssm_bwd(u_seg, dy_seg, states, bcat, ccat, lam_re, lam_im):
    l = u_seg.shape[0]
    nj = l // N_SEG
    nc = l // SSM_RC

    def body(u_ref, dy_ref, keep_ref, b_ref, c_ref, lr_ref, li_ref,
             du_ref, db_ref, dc_ref, dlr_ref, dli_ref, xs_ref, gs_ref, stage0, stage1, keep_sems):
        k, d = pl.program_id(0), pl.program_id(1)
        rev = d == 1
        back = jnp.logical_not(rev)
        shape = (N_SEG, SSM_ST)
        ar, ai = jnp.broadcast_to(lr_ref[...], shape), -jnp.broadcast_to(li_ref[...], shape)
        zero = jnp.zeros(shape, F32)
        def fetch(ci):
            rows = _chunk_rows(ci, back, nc)
            return pltpu.make_async_copy(keep_ref.at[d, k, rows], xs_ref.at[rows], keep_sems.at[ci])

        for ci in range(nc):
            fetch(ci).start()

        def inputs(ci, stage):
            rows = _chunk_rows(ci, back, nc)
            dx = _dg(dy_ref[rows, :], c_ref[...], NT)
            stage[...] = dx
            gs_ref[rows, :] = dx

        def first(stage, ci, carry):
            return _scan_chunk((stage, True), None, ar, ai, back, nj, ci, carry)

        def first_pass(t, carry):
            inputs(2 * t + 1, stage1)
            carry = first(stage0, 2 * t, carry)
            inputs(2 * t + 2, stage0)
            return first(stage1, 2 * t + 1, carry)

        inputs(0, stage0)
        carry = lax.fori_loop(0, nc // 2 - 1, first_pass, (zero, zero))
        inputs(nc - 1, stage1)
        carry = first(stage0, nc - 2, carry)
        end_r, end_i = first(stage1, nc - 1, carry)
        init = _segment_inits(ar, ai, end_r, end_i, back, nj)
        db_ref[...] = jnp.zeros_like(db_ref)
        dc_ref[...] = jnp.zeros_like(dc_ref)

        @pl.when(d == 0)
        def _():
            du_ref[...] = jnp.zeros_like(du_ref)

        def outputs(ci, stage):
            rows = _chunk_rows(ci, back, nc)
            g = stage[...].astype(BF16)
            dc_ref[...] += _dg(xs_ref[rows, :].astype(BF16), dy_ref[rows, :], TN)
            db_ref[...] += _dg(u_ref[rows, :], g, TN)
            du_ref[rows, :] += _dg(g, b_ref[...], NT)

        def second(ci, stage, carry):
            return _scan_chunk((gs_ref, False), (stage, True), ar, ai, back, nj, ci, carry, prev_ref=xs_ref)

        def second_pass(t, carry):
            fetch(2 * t + 2).wait()
            outputs(2 * t, stage0)
            carry = second(2 * t + 1, stage1, carry)
            fetch(2 * t + 3).wait()
            outputs(2 * t + 1, stage1)
            return second(2 * t + 2, stage0, carry)

        fetch(0).wait()
        fetch(1).wait()
        carry = lax.fori_loop(0, nc // 2 - 1, second_pass, second(0, stage0, init + (zero, zero)))
        outputs(nc - 2, stage0)
        gr, gi, acc_r, acc_i = second(nc - 1, stage1, carry)
        outputs(nc - 1, stage1)

        seg = lax.broadcasted_iota(jnp.int32, shape, 0)
        jb = jnp.where(rev, nj - 1, 0)
        erow = pl.ds(pl.multiple_of((nj - 1 - jb) * N_SEG, N_SEG), N_SEG)

        def before(t):
            up = jnp.where(seg >= 1, pltpu.roll(t, 1, 0), 0.0)
            down = jnp.where(seg <= N_SEG - 2, pltpu.roll(t, N_SEG - 1, 0), 0.0)
            return jnp.where(rev, down, up)

        init_r, init_i = before(xs_ref[erow, _RE]), before(xs_ref[erow, _IM])
        acc_r = acc_r + gr * init_r + gi * init_i
        acc_i = acc_i + gi * init_r - gr * init_i
        dlr_ref[...] = jnp.sum(acc_r, axis=0, keepdims=True)
        dli_ref[...] = jnp.sum(acc_i, axis=0, keepdims=True)

    act, bmat, cmat, lam = _ssm_specs(l)
    return pl.pallas_call(
        body, grid=(SSM_CB, 2), in_specs=[act, act, ANY, bmat, cmat, lam, lam],
        out_specs=[act, bmat, cmat, lam, lam],
        out_shape=[jax.ShapeDtypeStruct((l, SSM_WIDTH), F32),
                   jax.ShapeDtypeStruct(bcat.shape, F32), jax.ShapeDtypeStruct(ccat.shape, F32),
                   jax.ShapeDtypeStruct(lam_re.shape, F32), jax.ShapeDtypeStruct(lam_im.shape, F32)],
        scratch_shapes=[pltpu.VMEM((l, 2 * SSM_ST), F32), pltpu.VMEM((l, 2 * SSM_ST), F32),
                        pltpu.VMEM((SSM_RC, 2 * SSM_ST), F32), pltpu.VMEM((SSM_RC, 2 * SSM_ST), F32),
                        pltpu.SemaphoreType.DMA((nc,))],
        name="ssm_bwd", compiler_params=_params(("parallel", "arbitrary"), vmem_mb=58),
    )(u_seg, dy_seg, states, bcat.astype(BF16), ccat.astype(BF16), lam_re, lam_im)


def _glu_fwd(y_ssm, u, d_skip, w_glu):
    l, w = u.shape

    def body(y_ref, u_ref, d_ref, w_ref, pre_ref, s_ref, ys_ref):
        pre = y_ref[...] + d_ref[...] * u_ref[...]
        z = _gelu(pre)
        s = _dg(z.astype(BF16), w_ref[...], NN)
        pre_ref[...] = pre
        s_ref[...] = s
        ys_ref[...] = z * _sigmoid(s)

    row = pl.BlockSpec((TM_EW, w), lambda i: (i, 0))
    out = jax.ShapeDtypeStruct((l, w), F32)
    return pl.pallas_call(
        body, grid=(l // TM_EW,),
        in_specs=[row, row, pl.BlockSpec((1, w), lambda i: (0, 0)), pl.BlockSpec((w, w), lambda i: (0, 0))],
        out_specs=[row, row, row], out_shape=[out, out, out], name="glu_fwd",
        compiler_params=_params(("parallel",)),
    )(y_ssm, u, d_skip, w_glu)


def _glu_bwd(pre, s, dys, u, d_skip, w_glu):
    l, w = u.shape

    def body(pre_ref, s_ref, dys_ref, u_ref, d_ref, w_ref, dpre_ref, z_ref, ds_ref, dd_ref):
        pre, dys = pre_ref[...], dys_ref[...]
        z = _gelu(pre)
        sig = _sigmoid(s_ref[...])
        ds = (dys * z * sig * (1.0 - sig)).astype(BF16)
        dz = dys * sig + _dg(ds, w_ref[...], NT)
        dpre = dz * _gelu_grad(pre)
        dpre_ref[...] = dpre
        z_ref[...] = z.astype(BF16)
        ds_ref[...] = ds

        @pl.when(pl.program_id(0) == 0)
        def _():
            dd_ref[...] = jnp.zeros_like(dd_ref)

        dd_ref[...] += jnp.sum(dpre * u_ref[...], axis=0, keepdims=True)

    row = pl.BlockSpec((TM_EW, w), lambda i: (i, 0))
    vec = pl.BlockSpec((1, w), lambda i: (0, 0))
    return pl.pallas_call(
        body, grid=(l // TM_EW,),
        in_specs=[row, row, row, row, vec, pl.BlockSpec((w, w), lambda i: (0, 0))],
        out_specs=[row, row, row, vec],
        out_shape=[jax.ShapeDtypeStruct((l, w), F32), jax.ShapeDtypeStruct((l, w), BF16),
                   jax.ShapeDtypeStruct((l, w), BF16), jax.ShapeDtypeStruct((1, w), F32)],
        name="glu_bwd", compiler_params=_params(("arbitrary",)),
    )(pre, s, dys, u, d_skip, w_glu)


TM_CV = 512
TC_CV = 256
TM_CF = 256
TC_CF = D_FF // 2
HALO = SUBLANES


def _conv_specs(l, col0, tm=TM_CV, tc=TC_CV):
    per = tm // HALO
    nh = l // HALO
    off = col0 // tc
    return [
        pl.BlockSpec((HALO, tc), lambda j, i: (jnp.maximum(i * per - 1, 0), j + off)),
        pl.BlockSpec((tm, tc), lambda j, i: (i, j + off)),
        pl.BlockSpec((HALO, tc), lambda j, i: (jnp.minimum((i + 1) * per, nh - 1), j + off)),
    ]


def _ext(prev_ref, mid_ref, next_ref, first, last):
    p = jnp.where(first, 0.0, prev_ref[...])
    n = jnp.where(last, 0.0, next_ref[...])
    return jnp.concatenate([p, mid_ref[...], n], axis=0)


def _shift_dn(t):
    return pltpu.roll(t, 1, 0)


def _shift_up(t):
    return pltpu.roll(t, t.shape[0] - 1, 0)


def _conv3(e, w_ref, b_ref):
    return w_ref[0:1, :] * _shift_dn(e) + w_ref[1:2, :] * e + w_ref[2:3, :] * _shift_up(e) + b_ref[...]


def _convffn_fwd(up_pre, conv_w, conv_b):
    l = up_pre.shape[0]
    tm, tc = TM_CF, TC_CF
    ni = l // tm
    wspec = lambda off: pl.BlockSpec((3, tc), lambda j, i: (0, j + off))
    bspec = lambda off: pl.BlockSpec((1, tc), lambda j, i: (0, j + off))
    voff = D_FF // tc

    def body(gp, gm, gn, vp, vm, vn, wg, bg, wv, bv, o_ref):
        i = pl.program_id(1)
        first, last = i == 0, i == ni - 1
        gate = _conv3(_ext(gp, gm, gn, first, last), wg, bg)[HALO:HALO + tm]
        val = _conv3(_ext(vp, vm, vn, first, last), wv, bv)[HALO:HALO + tm]
        o_ref[...] = (gate * _sigmoid(gate) * val).astype(BF16)

    return pl.pallas_call(
        body, grid=(D_FF // tc, ni),
        in_specs=_conv_specs(l, 0, tm, tc) + _conv_specs(l, D_FF, tm, tc)
        + [wspec(0), bspec(0), wspec(voff), bspec(voff)],
        out_specs=pl.BlockSpec((tm, tc), lambda j, i: (i, j)),
        out_shape=jax.ShapeDtypeStruct((l, D_FF), BF16), name="convffn_fwd",
        compiler_params=_params(("parallel", "parallel")),
    )(up_pre, up_pre, up_pre, up_pre, up_pre, up_pre, conv_w, conv_b, conv_w, conv_b)


HALO_B = 2 * SUBLANES


def _convffn_bwd(up_pre, dx2b, w_down, conv_w, conv_b):
    l = up_pre.shape[0]
    ni = l // TM_CV
    d = dx2b.shape[1]
    wspec = lambda off: pl.BlockSpec((3, TC_CV), lambda i, j: (0, j + off))
    bspec = lambda off: pl.BlockSpec((1, TC_CV), lambda i, j: (0, j + off))
    voff = D_FF // TC_CV
    swap = lambda spec: pl.BlockSpec(spec.block_shape, lambda i, j, f=spec.index_map: f(j, i))
    per, nh = TM_CV // HALO_B, l // HALO_B
    dx_specs = [pl.BlockSpec((HALO_B, d), lambda i, j: (jnp.maximum(i * per - 1, 0), 0)),
                pl.BlockSpec((TM_CV, d), lambda i, j: (i, 0)),
                pl.BlockSpec((HALO_B, d), lambda i, j: (jnp.minimum((i + 1) * per, nh - 1), 0))]

    def body(gp, gm, gn, vp, vm, vn, xp, xm, xn, wd, wg, bg, wv, bv, dup_ref, pg_ref, pv_ref):
        i = pl.program_id(0)
        first, last = i == 0, i == ni - 1
        ge, ve = _ext(gp, gm, gn, first, last), _ext(vp, vm, vn, first, last)
        zero = jnp.zeros((HALO_B, d), BF16)
        dx = jnp.concatenate([jnp.where(first, zero, xp[...]), xm[...], jnp.where(last, zero, xn[...])], axis=0)
        de = _dg(dx, wd[...], NT)[HALO_B - HALO:HALO_B + TM_CV + HALO]
        taps = [(_shift_dn(e), e, _shift_up(e)) for e in (ge, ve)]
        conv = lambda t, w_ref, b_ref: w_ref[0:1, :] * t[0] + w_ref[1:2, :] * t[1] + w_ref[2:3, :] * t[2] + b_ref[...]
        gate, val = conv(taps[0], wg, bg), conv(taps[1], wv, bv)
        sig = _sigmoid(gate)
        silu = gate * sig
        dgate = de * val * (sig + silu * (1.0 - sig))
        dval = de * silu
        mid = slice(HALO, HALO + TM_CV)
        rid = lax.broadcasted_iota(jnp.int32, (SUBLANES, TC_CV), 0)
        for half, (dup, tap, w_ref, p_ref) in enumerate(((dgate, taps[0], wg, pg_ref), (dval, taps[1], wv, pv_ref))):
            dpre = w_ref[0:1, :] * _shift_up(dup) + w_ref[1:2, :] * dup + w_ref[2:3, :] * _shift_dn(dup)
            dup_ref[half] = dpre[mid].astype(BF16)
            dm_ = dup[mid]
            sums = [jnp.sum(dm_ * t[mid], axis=0, keepdims=True) for t in tap]
            sums.append(jnp.sum(dm_, axis=0, keepdims=True))
            acc = jnp.zeros((SUBLANES, TC_CV), F32)
            for k, sk in enumerate(sums):
                acc = jnp.where(rid == k, sk, acc)
            p_ref[...] = acc

    par = pl.BlockSpec((None, SUBLANES, TC_CV), lambda i, j: (i, 0, j))
    dup, pg, pv = pl.pallas_call(
        body, grid=(ni, D_FF // TC_CV),
        in_specs=[swap(s) for s in _conv_specs(l, 0) + _conv_specs(l, D_FF)] + dx_specs
        + [pl.BlockSpec((TC_CV, d), lambda i, j: (j, 0)), wspec(0), bspec(0), wspec(voff), bspec(voff)],
        out_specs=[pl.BlockSpec((2, TM_CV, TC_CV), lambda i, j: (0, i, j)), par, par],
        out_shape=[jax.ShapeDtypeStruct((2, l, D_FF), BF16),
                   jax.ShapeDtypeStruct((ni, SUBLANES, D_FF), F32), jax.ShapeDtypeStruct((ni, SUBLANES, D_FF), F32)],
        name="convffn_bwd", compiler_params=_params(("parallel", "parallel")),
    )(up_pre, up_pre, up_pre, up_pre, up_pre, up_pre, dx2b, dx2b, dx2b, w_down, conv_w, conv_b, conv_w, conv_b)
    return dup, jnp.concatenate([jnp.sum(pg, axis=0), jnp.sum(pv, axis=0)], axis=1)


def _local_step(x, target, wb, sp, mixer_weights=None, late_weights=None, ffn_grads_ready=None,
                ffn_grads_next=None):
    l = x.shape[0]
    tabs = _rope_tables(l)
    disc = _ssm_disc(sp["a_re"], sp["a_im"], sp["log_step"], sp["b_re"], sp["b_im"])
    bcat, ccat, lam_re, lam_im = _ssm_pack(*disc, sp["c_re"], sp["c_im"])
    d_skip = sp["d_skip"].reshape(1, SSM_WIDTH)

    big = min(l, 1024)
    h, qkv, u = _rms_mm_rope(x, sp["norm_mix_g"], wb["w_in"], tabs, "mm_in")
    attn, lse = _attn_fwd(qkv, sp["sink"])
    u_seg = _to_segments(u).astype(BF16)
    y_seg, states = _ssm_fwd(u_seg, bcat, ccat, lam_re, lam_im)
    y_ssm = _from_segments(y_seg)
    if mixer_weights is not None:
        wb = dict(wb, **mixer_weights(attn))
    pre, s_glu, ys = _glu_fwd(y_ssm, u, d_skip, wb["w_glu"])
    mixed, x1, h2 = _mix_mm_res_rms(attn, ys, sp["norm_attn_g"], sp["norm_ssm_g"], wb["w_out"], x,
                                    sp["norm_ffn_g"], "mm_out")
    if late_weights is not None:
        wb = dict(wb, **late_weights(h2))
    up_pre = _mm_nn_cols(h2, wb["w_up"], big, "mm_up")
    conv_w = wb["conv_w"]
    act = _convffn_fwd(up_pre, conv_w, sp["conv_b"])
    loss, dx2, dx2b, d_final_g = _mm_res_loss(act, wb["w_down"], x1, sp["norm_final_g"].reshape(1, D_MODEL), target)

    g = {"norm_final_g": d_final_g.reshape(D_MODEL)}
    g["w_down"] = _mm_tn(act, dx2b, D_FF // 2, 512, "mm_down_dw")
    dup_pre, conv_par = _convffn_bwd(up_pre, dx2b, wb["w_down"], conv_w, sp["conv_b"])
    g["conv_w"], g["conv_b"] = conv_par[0:3], conv_par[3:4]
    g["w_up"] = _mm_tn_cols(h2, dup_pre, wb["w_up"].shape[0], 512, "mm_up_dw")
    zero = ffn_grads_ready(g["w_up"], g["w_down"]) if ffn_grads_ready is not None else 0.0
    dx1, dx1b, g["norm_ffn_g"] = _mm_cols_rms_bwd(dup_pre, wb["w_up"], x1, sp["norm_ffn_g"] + zero, dx2, "mm_up_dx")
    g["w_out"] = _mm_tn(mixed, dx1b, 1024, 1024, "mm_out_dw")
    zero = ffn_grads_next(g["w_out"]) if ffn_grads_next is not None else 0.0
    dattn, dys, g["norm_attn_g"], g["norm_ssm_g"] = _mm_mix_bwd(
        dx1b, wb["w_out"], attn, ys, sp["norm_attn_g"] + zero, sp["norm_ssm_g"], "mm_out_dx")
    dpre, zb, dsb, dd = _glu_bwd(pre, s_glu, dys, u, d_skip, wb["w_glu"])
    g["d_skip"] = dd.reshape(N_SSM_GROUPS, SSM_GROUP)
    g["w_glu"] = _mm_tn(zb, dsb, 512, 512, "mm_glu_dw")
    du_seg, dbcat, dccat, dlam_re, dlam_im = _ssm_bwd(u_seg, _to_segments(dpre).astype(BF16), states, bcat, ccat,
                                                      lam_re, lam_im)
    dlb_re, dlb_im, dbb_re, dbb_im, g["c_re"], g["c_im"] = _ssm_unpack(dbcat, dccat, dlam_re, dlam_im)
    _, disc_vjp = jax.vjp(_ssm_disc, sp["a_re"], sp["a_im"], sp["log_step"], sp["b_re"], sp["b_im"])
    g["a_re"], g["a_im"], g["log_step"], g["b_re"], g["b_im"] = disc_vjp((dlb_re, dlb_im, dbb_re, dbb_im))
    dq, dkv, g["sink"] = _attn_bwd(qkv, attn, dattn, lse, sp["sink"])
    dproj = _rope_bwd(dq, dkv, _from_segments(du_seg), dpre, d_skip, tabs)
    g["w_in"] = _mm_tn(dproj, h, IN_WIDTH // 5, D_MODEL, "mm_in_dw")
    grad_x, _, g["norm_mix_g"] = _mm_nn_rms_bwd(dproj, wb["w_in"], x, sp["norm_mix_g"], dx1, "mm_in_dx")
    return loss, grad_x, g


MESH = pl.DeviceIdType.MESH
ANY = pl.BlockSpec(memory_space=pl.ANY)


def _place():
    x, y, c = lax.axis_index("x"), lax.axis_index("y"), lax.axis_index("c")
    chips = [(1 - x, y), (x, 1 - y), (1 - x, 1 - y)]
    return x, y, c, chips


def _chip_index(px, py):
    return 2 * px + py


CHUNK_BYTES = 256 * 1024
MAX_CHUNKS = 16


def _row_chunks(rows, row_bytes, align):
    n = max(1, min(MAX_CHUNKS, (rows * row_bytes) // CHUNK_BYTES))
    per = -(-rows // n)
    per = -(-per // align) * align
    return [(r0, min(per, rows - r0)) for r0 in range(0, rows, per)]


def _align_of(dtype):
    return SUBLANES * 4 // jnp.dtype(dtype).itemsize


def _remote(src, dst, send_sem, recv_sem, to):
    return pltpu.make_async_remote_copy(src_ref=src, dst_ref=dst, send_sem=send_sem, recv_sem=recv_sem,
                                        device_id=to, device_id_type=MESH)


CAST_ROWS = 64


def _gather_weights(shards, dtypes):
    nw = len(shards)

    def body(*refs):
        w_refs, o_refs = refs[:nw], refs[nw:2 * nw]
        send_sems, recv_sems, in_sems, out_sems = refs[2 * nw:2 * nw + 4]
        raw, cast = refs[2 * nw + 4:3 * nw + 4], refs[3 * nw + 4:]
        x, y, c, chips = _place()
        mine = _chip_index(x, y)
        sibling = (x, y, 1 - c)

        def rows_of(ref, chip, r0, nr):
            return ref.at[chip, pl.ds(r0, nr), :]

        def copy(wi, k, src, dst, to):
            return _remote(src, dst, send_sems.at[wi, k], recv_sems.at[wi, k], to)

        geo = []
        for wi in range(nw):
            rows, cols = w_refs[wi].shape
            row_bytes = cols * jnp.dtype(dtypes[wi]).itemsize
            geo.append((rows // 2, _row_chunks(rows // 2, row_bytes, _align_of(dtypes[wi]))))

        stage_in = [pltpu.make_async_copy(w_refs[wi], raw[wi], in_sems.at[wi]) for wi in range(nw)]
        for cp in stage_in:
            cp.start()
        staged = [raw[wi] if dtypes[wi] == w_refs[wi].dtype else cast[wi] for wi in range(nw)]
        stage_out = []
        for wi in range(nw):
            stage_in[wi].wait()
            if staged[wi] is not raw[wi]:
                def cast_rows(i, _, wi=wi):
                    rows = pl.ds(pl.multiple_of(i * CAST_ROWS, CAST_ROWS), CAST_ROWS)
                    cast[wi][rows, :] = raw[wi][rows, :].astype(dtypes[wi])
                    return 0

                lax.fori_loop(0, w_refs[wi].shape[0] // CAST_ROWS, cast_rows, 0)
            cp = pltpu.make_async_copy(staged[wi], o_refs[wi].at[mine], out_sems.at[wi])
            cp.start()
            stage_out.append(cp)

        for wi in range(nw):
            hr, half_chunks = geo[wi]
            for j, chip in enumerate(chips):
                for r0, nr in half_chunks:
                    copy(wi, j, staged[wi].at[pl.ds(c * hr + r0, nr), :],
                         rows_of(o_refs[wi], mine, c * hr + r0, nr), (*chip, c)).start()
        for wi in range(nw):
            hr, half_chunks = geo[wi]
            for j, chip in enumerate(chips):
                got = rows_of(o_refs[wi], _chip_index(*chip), c * hr, hr)
                copy(wi, j, got, got, (*chip, c)).wait_recv()
                for r0, nr in half_chunks:
                    piece = rows_of(o_refs[wi], _chip_index(*chip), c * hr + r0, nr)
                    copy(wi, 3 + j, piece, piece, sibling).start()
        for wi in range(nw):
            hr = geo[wi][0]
            for j, chip in enumerate(chips):
                got = rows_of(o_refs[wi], _chip_index(*chip), (1 - c) * hr, hr)
                copy(wi, 3 + j, got, got, sibling).wait_recv()
        for wi in range(nw):
            hr = geo[wi][0]
            sent = rows_of(o_refs[wi], mine, c * hr, hr)
            for k in range(6):
                copy(wi, k, sent, sent, sibling).wait_send()
            stage_out[wi].wait()

    return pl.pallas_call(
        body, in_specs=[ANY] * nw, out_specs=[ANY] * nw,
        out_shape=[jax.ShapeDtypeStruct((4, *s.shape), t) for s, t in zip(shards, dtypes)],
        scratch_shapes=[pltpu.SemaphoreType.DMA((nw, 6)), pltpu.SemaphoreType.DMA((nw, 6)),
                        pltpu.SemaphoreType.DMA((nw,)), pltpu.SemaphoreType.DMA((nw,))]
        + [pltpu.VMEM(s.shape, s.dtype) for s in shards] + [pltpu.VMEM(s.shape, t) for s, t in zip(shards, dtypes)],
        name="gather_weights", compiler_params=_params(vmem_mb=40),
    )(*shards)


HBM = pl.BlockSpec(memory_space=pltpu.HBM)
SEM = pl.BlockSpec(memory_space=pltpu.SEMAPHORE)
EFFECT = pltpu.SideEffectType.DATAFLOW_SIDE_EFFECTING


def _cast_place(w, place, dtype, after, name):
    rows, cols = w.shape
    tr = _row_tile(rows, cols, _align_of(dtype))

    def body(p_ref, w_ref, after_ref, o_ref):
        del p_ref, after_ref
        o_ref[...] = w_ref[...].astype(dtype)

    grid_spec = pltpu.PrefetchScalarGridSpec(
        num_scalar_prefetch=1, grid=(rows // tr,),
        in_specs=[pl.BlockSpec((tr, cols), lambda i, p: (i, 0)), ANY],
        out_specs=pl.BlockSpec((None, tr, cols), lambda i, p: (p[1], i, 0)))
    return pl.pallas_call(body, grid_spec=grid_spec, out_shape=jax.ShapeDtypeStruct((4, rows, cols), dtype),
                          name=name, compiler_params=_params(("parallel",)))(place, w, after)


def _split_start(name, arrays, n_pairs, issue):
    n = len(arrays)

    def body(*refs):
        issue(refs[:n], refs[n:n + n_pairs], refs[n + n_pairs:n + 2 * n_pairs])
        token = refs[2 * n + 2 * n_pairs]
        token[...] = jnp.zeros_like(token)

    dma = pltpu.SemaphoreType.DMA(())
    outs = pl.pallas_call(
        body, name=name,
        out_shape=[dma] * (2 * n_pairs) + [pltpu.HBM(t.shape, t.dtype) for t in arrays]
        + [jax.ShapeDtypeStruct((SUBLANES, LANES), F32)],
        in_specs=[HBM] * n, out_specs=[SEM] * (2 * n_pairs) + [HBM] * n + [pl.BlockSpec(memory_space=pltpu.VMEM)],
        input_output_aliases={a: 2 * n_pairs + a for a in range(n)},
        compiler_params=pltpu.CompilerParams(has_side_effects=EFFECT),
    )(*[pltpu.with_memory_space_constraint(t, pltpu.HBM) for t in arrays])
    return outs[:n_pairs], outs[n_pairs:2 * n_pairs], outs[2 * n_pairs:2 * n_pairs + n], outs[-1]


def _split_wait(name, send_sems, recv_sems, flying, sizes, after):
    n, n_pairs = len(flying), len(send_sems)

    def body(*refs):
        x, y, c, _ = _place()
        for k, ref in enumerate(sizes(refs[:n])):
            cp = _remote(ref, ref, refs[n + k], refs[n + n_pairs + k], (x, y, 1 - c))
            cp.wait_send()
            cp.wait_recv()

    return pl.pallas_call(
        body, name=name, out_shape=[pltpu.HBM(t.shape, t.dtype) for t in flying],
        in_specs=[HBM] * n + [SEM] * (2 * n_pairs) + [ANY], out_specs=[HBM] * n,
        input_output_aliases={a: a for a in range(n)},
        compiler_params=pltpu.CompilerParams(has_side_effects=EFFECT),
    )(*flying, *send_sems, *recv_sems, after)


def _spread_start(lands, name):
    def issue(land_refs, send_sems, recv_sems):
        x, y, c, chips = _place()
        mine = _chip_index(x, y)
        for a, land in enumerate(land_refs):
            _, rows, cols = land.shape
            hr = rows // 2
            row_bytes = cols * jnp.dtype(land.dtype).itemsize
            for r0, nr in _row_chunks(hr, row_bytes, _align_of(land.dtype)):
                piece = land.at[mine, pl.ds(c * hr + r0, nr), :]
                for chip in chips:
                    for core in (0, 1):
                        _remote(piece, piece, send_sems[a], recv_sems[a], (*chip, core)).start()

    return _split_start(name, lands, len(lands), issue)


def _spread_wait(send_sems, recv_sems, flying, after, name):
    return _split_wait(name, send_sems, recv_sems, flying, lambda refs: [r.at[pl.ds(0, 3)] for r in refs], after)


def _pair_start(grads):
    n = len(grads)
    zones = [lax.empty((4, g.shape[1] // 2, g.shape[2]), F32) for g in grads]

    def issue(refs, send_sems, recv_sems):
        x, y, c, _ = _place()
        for a in range(n):
            g_ref, z_ref = refs[a], refs[n + a]
            _, rows, cols = g_ref.shape
            hr = rows // 2
            for k in range(4):
                for r0, nr in _row_chunks(hr, cols * 4, SUBLANES):
                    _remote(g_ref.at[k, pl.ds((1 - c) * hr + r0, nr), :], z_ref.at[k, pl.ds(r0, nr), :],
                            send_sems[a], recv_sems[a], (x, y, 1 - c)).start()

    return _split_start("pair_start", list(grads) + zones, n, issue)


def _pair_wait(send_sems, recv_sems, flying, after):
    n = len(flying) // 2
    out = _split_wait("pair_wait", send_sems, recv_sems, flying, lambda refs: list(refs[n:]), after)
    return out[:n], out[n:]


def _chip_start(sums):
    n = len(sums)
    zones = [lax.empty((3, *s.shape[1:]), s.dtype) for s in sums]

    def issue(refs, send_sems, recv_sems):
        x, y, c, chips = _place()
        for a in range(n):
            s_ref, z_ref = refs[a], refs[n + a]
            _, rows, cols = s_ref.shape
            row_bytes = cols * jnp.dtype(s_ref.dtype).itemsize
            for r0, nr in _row_chunks(rows, row_bytes, _align_of(s_ref.dtype)):
                for j, chip in enumerate(chips):
                    _remote(s_ref.at[_chip_index(*chip), pl.ds(r0, nr), :], z_ref.at[j, pl.ds(r0, nr), :],
                            send_sems[a], recv_sems[a], (*chip, c)).start()

    return _split_start("chip_start", list(sums) + zones, n, issue)


def _chip_wait(send_sems, recv_sems, flying, after):
    n = len(flying) // 2
    return _split_wait("chip_wait", send_sems, recv_sems, flying, lambda refs: list(refs[n:]), after)[n:]


def _pair_exchange(grads):
    na = len(grads)

    def body(*refs):
        g_refs, o_refs = refs[:na], refs[na:2 * na]
        send_sems, recv_sems = refs[2 * na:]
        x, y, c, _ = _place()
        sibling = (x, y, 1 - c)
        for ai in range(na):
            _, rows, cols = g_refs[ai].shape
            hr = rows // 2
            for k in range(4):
                for r0, nr in _row_chunks(hr, cols * 4, SUBLANES):
                    _remote(g_refs[ai].at[k, pl.ds((1 - c) * hr + r0, nr), :], o_refs[ai].at[k, pl.ds(r0, nr), :],
                            send_sems.at[ai], recv_sems.at[ai], sibling).start()
        for ai in range(na):
            _remote(o_refs[ai], o_refs[ai], send_sems.at[ai], recv_sems.at[ai], sibling).wait()

    return pl.pallas_call(
        body, in_specs=[ANY] * na, out_specs=[ANY] * na,
        out_shape=[jax.ShapeDtypeStruct((4, g.shape[1] // 2, g.shape[2]), F32) for g in grads],
        scratch_shapes=[pltpu.SemaphoreType.DMA((na,)), pltpu.SemaphoreType.DMA((na,))],
        name="pair_exchange",
    )(*grads)


def _row_tile(rows, cols, align):
    best = align
    for cand in range(align, rows + 1, align):
        if rows % cand == 0 and cand * cols <= 256 * 1024:
            best = cand
    return best


def _pair_sum(g, got, place, transit, name):
    _, rows, cols = g.shape
    hr = rows // 2
    tr = _row_tile(hr, cols, _align_of(transit))
    nt = hr // tr

    def body(p_ref, g_ref, r_ref, s_ref, own_ref):
        total = g_ref[...] + r_ref[...]
        s_ref[...] = total.astype(transit)

        @pl.when(pl.program_id(1) == p_ref[1])
        def _():
            own_ref[...] = total

    grid_spec = pltpu.PrefetchScalarGridSpec(
        num_scalar_prefetch=1, grid=(nt, 4),
        in_specs=[pl.BlockSpec((None, tr, cols), lambda i, k, p: (k, p[0] * nt + i, 0)),
                  pl.BlockSpec((None, tr, cols), lambda i, k, p: (k, i, 0))],
        out_specs=[pl.BlockSpec((None, tr, cols), lambda i, k, p: (k, i, 0)),
                   pl.BlockSpec((tr, cols), lambda i, k, p: (i, 0))])
    return pl.pallas_call(
        body, grid_spec=grid_spec,
        out_shape=[jax.ShapeDtypeStruct((4, hr, cols), transit), jax.ShapeDtypeStruct((hr, cols), F32)],
        name=name, compiler_params=_params(("parallel", "arbitrary")),
    )(place, g, got)


def _chip_exchange(sums):
    na = len(sums)

    def body(*refs):
        s_refs, o_refs = refs[:na], refs[na:2 * na]
        send_sems, recv_sems = refs[2 * na:]
        x, y, c, chips = _place()
        for ai in range(na):
            _, rows, cols = s_refs[ai].shape
            row_bytes = cols * jnp.dtype(s_refs[ai].dtype).itemsize
            for r0, nr in _row_chunks(rows, row_bytes, _align_of(s_refs[ai].dtype)):
                for j, chip in enumerate(chips):
                    _remote(s_refs[ai].at[_chip_index(*chip), pl.ds(r0, nr), :], o_refs[ai].at[j, pl.ds(r0, nr), :],
                            send_sems.at[ai, j], recv_sems.at[ai, j], (*chip, c)).start()
        for ai in range(na):
            for j, chip in enumerate(chips):
                _remote(o_refs[ai].at[j], o_refs[ai].at[j], send_sems.at[ai, j], recv_sems.at[ai, j],
                        (*chip, c)).wait()

    return pl.pallas_call(
        body, in_specs=[ANY] * na, out_specs=[ANY] * na,
        out_shape=[jax.ShapeDtypeStruct((3, *s.shape[1:]), s.dtype) for s in sums],
        scratch_shapes=[pltpu.SemaphoreType.DMA((na, 3)), pltpu.SemaphoreType.DMA((na, 3))],
        name="chip_exchange",
    )(*sums)


def _chip_sum(own, landed, name):
    hr, cols = own.shape
    tr = _row_tile(hr, cols, _align_of(landed.dtype))

    def body(o_ref, l_ref, f_ref):
        acc = o_ref[...]
        for j in range(3):
            acc = acc + l_ref[j].astype(F32)
        f_ref[...] = acc

    return pl.pallas_call(
        body, grid=(hr // tr,),
        in_specs=[pl.BlockSpec((tr, cols), lambda i: (i, 0)), pl.BlockSpec((3, tr, cols), lambda i: (0, i, 0))],
        out_specs=pl.BlockSpec((tr, cols), lambda i: (i, 0)),
        out_shape=jax.ShapeDtypeStruct((hr, cols), F32), name=name,
        compiler_params=_params(("parallel",)),
    )(own, landed)


def _final_exchange(halves, small):
    nh = len(halves)

    def body(*refs):
        h_refs, s_ref = refs[:nh], refs[nh]
        o_refs, so_ref = refs[nh + 1:2 * nh + 1], refs[2 * nh + 1]
        send_sems, recv_sems, local_sem, ssend_sems, srecv_sems = refs[2 * nh + 2:]
        x, y, c, _ = _place()
        me = 4 * x + 2 * y + c
        sibling = (x, y, 1 - c)
        for hi in range(nh):
            hr, cols = h_refs[hi].shape
            for r0, nr in _row_chunks(hr, cols * 4, SUBLANES):
                _remote(h_refs[hi].at[pl.ds(r0, nr), :], o_refs[hi].at[pl.ds(r0, nr), :],
                        send_sems.at[hi], recv_sems.at[hi], sibling).start()
        small_cps = [pltpu.make_async_copy(s_ref, so_ref.at[me], local_sem)]
        for r in range(1, 8):
            fx, fy, fc = (r >> 2) & 1, (r >> 1) & 1, r & 1
            peer = (1 - x if fx else x, 1 - y if fy else y, 1 - c if fc else c)
            small_cps.append(_remote(s_ref, so_ref.at[me], ssend_sems.at[r - 1], srecv_sems.at[r - 1], peer))
        for cp in small_cps:
            cp.start()
        for hi in range(nh):
            _remote(h_refs[hi], o_refs[hi], send_sems.at[hi], recv_sems.at[hi], sibling).wait()
        for cp in small_cps:
            cp.wait()

    return pl.pallas_call(
        body, in_specs=[ANY] * (nh + 1), out_specs=[ANY] * (nh + 1),
        out_shape=[jax.ShapeDtypeStruct(h.shape, F32) for h in halves]
        + [jax.ShapeDtypeStruct((8, *small.shape), F32)],
        scratch_shapes=[pltpu.SemaphoreType.DMA((nh,)), pltpu.SemaphoreType.DMA((nh,)),
                        pltpu.SemaphoreType.DMA, pltpu.SemaphoreType.DMA((7,)), pltpu.SemaphoreType.DMA((7,))],
        name="final_exchange",
    )(*halves, small)


def _adamw_halves(w, own, other, m, v, place, name):
    r, c = w.shape
    hr = r // 2
    tr = _row_tile(hr, c, SUBLANES)
    nt = hr // tr
    c1 = 1.0 - ADAM_B1 ** ADAM_STEP
    c2 = 1.0 - ADAM_B2 ** ADAM_STEP

    def body(p_ref, w_ref, own_ref, other_ref, m_ref, v_ref, g_ref, d_ref, nm_ref, nv_ref):
        mine = pl.program_id(0) // nt == p_ref[0]
        gv = jnp.where(mine, own_ref[...], other_ref[...])
        nm = ADAM_B1 * m_ref[...] + (1.0 - ADAM_B1) * gv
        nv = ADAM_B2 * v_ref[...] + (1.0 - ADAM_B2) * (gv * gv)
        g_ref[...] = gv
        d_ref[...] = -ADAM_LR * ((nm / c1) / (jnp.sqrt(nv / c2) + ADAM_EPS) + ADAM_WD * w_ref[...])
        nm_ref[...] = nm
        nv_ref[...] = nv

    full = pl.BlockSpec((tr, c), lambda i, p: (i, 0))
    half = pl.BlockSpec((tr, c), lambda i, p: (i % nt, 0))
    out = jax.ShapeDtypeStruct((r, c), F32)
    grid_spec = pltpu.PrefetchScalarGridSpec(num_scalar_prefetch=1, grid=(2 * nt,),
                                             in_specs=[full, half, half, full, full], out_specs=[full] * 4)
    return pl.pallas_call(body, grid_spec=grid_spec, out_shape=[out] * 4, name=name,
                          compiler_params=_params(("parallel",)))(place, w, own, other, m, v)


def _adamw_many(ws, gs, ms, vs, name):
    n = len(ws)
    c1 = 1.0 - ADAM_B1 ** ADAM_STEP
    c2 = 1.0 - ADAM_B2 ** ADAM_STEP

    def body(*refs):
        w_refs, g_refs, m_refs, v_refs = (refs[k * n:(k + 1) * n] for k in range(4))
        d_refs, nm_refs, nv_refs = (refs[(4 + k) * n:(5 + k) * n] for k in range(3))
        for i in range(n):
            gv = g_refs[i][...]
            nm = ADAM_B1 * m_refs[i][...] + (1.0 - ADAM_B1) * gv
            nv = ADAM_B2 * v_refs[i][...] + (1.0 - ADAM_B2) * (gv * gv)
            d_refs[i][...] = -ADAM_LR * ((nm / c1) / (jnp.sqrt(nv / c2) + ADAM_EPS) + ADAM_WD * w_refs[i][...])
            nm_refs[i][...] = nm
            nv_refs[i][...] = nv

    vmem = pl.BlockSpec(memory_space=pltpu.VMEM)
    shapes = [jax.ShapeDtypeStruct(t.shape, F32) for t in ws]
    outs = pl.pallas_call(body, in_specs=[vmem] * (4 * n), out_specs=[vmem] * (3 * n), out_shape=shapes * 3,
                          name=name, compiler_params=_params(vmem_mb=56))(*ws, *gs, *ms, *vs)
    return outs[:n], outs[n:2 * n], outs[2 * n:]


BIG = ("w_in", "w_glu", "w_out", "w_up", "w_down")
WEIGHTS = ("norm_mix_g", "w_in", "a_re", "a_im", "log_step", "b_re", "b_im", "c_re", "c_im", "d_skip", "w_glu",
           "sink", "norm_attn_g", "norm_ssm_g", "w_out", "norm_ffn_g", "w_up", "conv_w", "conv_b", "w_down",
           "norm_final_g")
SMALL = ("norm_mix_g", "a_re", "a_im", "log_step", "b_re", "b_im", "c_re", "c_im", "d_skip", "sink",
         "norm_attn_g", "norm_ssm_g", "norm_ffn_g", "conv_w", "conv_b", "norm_final_g")
SMALL_ROWS = 40
N_DEV = 8


def _by_owner(name, g):
    if name == "w_up":
        return g
    return g.reshape(4, g.shape[0] // 4, g.shape[1])


def _view(name, t):
    if name == "w_in":
        return jnp.swapaxes(t[0], 0, 1)
    if name in ("b_re", "b_im"):
        return jnp.swapaxes(t, -1, -2)
    return t


def _unview(name, t):
    if name == "w_in":
        return jnp.swapaxes(t, 0, 1)[None]
    if name in ("b_re", "b_im"):
        return jnp.swapaxes(t, -1, -2)
    return t


def kernel(x, norm_mix_g, w_in, a_re, a_im, log_step, b_re, b_im, c_re, c_im, d_skip, w_glu, sink, norm_attn_g, norm_ssm_g, w_out, norm_ffn_g, w_up, conv_w, conv_b, w_down, norm_final_g, loss_target, m_norm_mix_g, m_w_in, m_a_re, m_a_im, m_log_step, m_b_re, m_b_im, m_c_re, m_c_im, m_d_skip, m_w_glu, m_sink, m_norm_attn_g, m_norm_ssm_g, m_w_out, m_norm_ffn_g, m_w_up, m_conv_w, m_conv_b, m_w_down, m_norm_final_g, v_norm_mix_g, v_w_in, v_a_re, v_a_im, v_log_step, v_b_re, v_b_im, v_c_re, v_c_im, v_d_skip, v_w_glu, v_sink, v_norm_attn_g, v_norm_ssm_g, v_w_out, v_norm_ffn_g, v_w_up, v_conv_w, v_conv_b, v_w_down, v_norm_final_g):
    given = dict(locals())
    w = {n: given[n] for n in WEIGHTS}
    m = {n: given["m_" + n] for n in WEIGHTS}
    v = {n: given["v_" + n] for n in WEIGHTS}
    xy = 2 * lax.axis_index("x") + lax.axis_index("y")

    core = lax.axis_index("c")
    place = jnp.stack([core, xy]).astype(jnp.int32)

    conv_rows = jnp.pad(w["conv_w"][0], ((0, 2 * SUBLANES - 3), (0, 0)))
    rows = lambda t: t.reshape(4 * t.shape[1], t.shape[2])
    (w_in_all,) = _gather_weights([_view("w_in", w["w_in"])], [BF16])
    wb = {"w_in": rows(w_in_all)}
    early = ("w_in", "w_glu", "w_out")
    mixer = [_cast_place(w[n][0], place, BF16, w_in_all, "cast_" + n) for n in ("w_glu", "w_out")]
    mixer.append(_cast_place(conv_rows, place, F32, w_in_all, "cast_conv_w"))
    *mixer_flight, mixer_token = _spread_start(mixer, "spread_mixer_start")
    late = ("w_up", "w_down")
    *late_flight, token = _spread_start(
        [_cast_place(w[n][0], place, BF16, mixer_token, "cast_" + n) for n in late], "spread_ffn_start")

    def mixer_weights(after):
        w_glu4, w_out4, conv4 = _spread_wait(*mixer_flight, after, "spread_mixer_wait")
        return {"w_glu": rows(w_glu4), "w_out": rows(w_out4),
                "conv_w": conv4[:, :3].transpose(1, 0, 2).reshape(3, 2 * D_FF)}

    def late_weights(after):
        w_up4, w_down4 = _spread_wait(*late_flight, after, "spread_ffn_wait")
        return {"w_up": w_up4, "w_down": rows(w_down4)}

    sp = {n: w[n][0] for n in ("a_re", "a_im", "log_step", "b_re", "b_im", "c_re", "c_im", "d_skip",
                               "norm_mix_g", "norm_attn_g", "norm_ssm_g", "norm_ffn_g", "sink", "conv_b")}
    for n in ("norm_mix_g", "norm_attn_g", "norm_ssm_g", "norm_ffn_g", "sink", "conv_b"):
        sp[n] = sp[n].reshape(1, -1)
    sp["norm_mix_g"] = sp["norm_mix_g"] + token[:1, :1]
    sp["norm_final_g"] = w["norm_final_g"]
    flight = {}

    def ffn_grads_ready(dw_up, dw_down):
        *flight["pair"], token = _pair_start([dw_up, _by_owner("w_down", dw_down)])
        return token[:1, :1]

    def ffn_grads_next(after):
        mine, got = _pair_wait(*flight["pair"], after)
        sums, flight["own"] = zip(*[_pair_sum(a, b, place, BF16, "pair_sum_" + n) for n, a, b in zip(late, mine, got)])
        *flight["chip"], token = _chip_start(list(sums))
        return token[:1, :1]

    loss, grad_x, g = _local_step(x[0], loss_target[0], wb, sp, mixer_weights, late_weights, ffn_grads_ready,
                                  ffn_grads_next)

    flat = jnp.concatenate([g[n].reshape(-1) for n in SMALL] + [loss.reshape(-1)])
    pad = N_DEV * SMALL_ROWS * D_MODEL - flat.shape[0]
    small = jnp.concatenate([flat, jnp.zeros((pad,), F32)]).reshape(4, 2 * SMALL_ROWS, D_MODEL)
    by_owner = [_by_owner(n, g[n]) for n in early] + [small]
    got = _pair_exchange(by_owner)
    transit = [BF16] * len(early) + [F32]
    chip_sums, own_sums = zip(*[_pair_sum(a, b, place, t, "pair_sum_" + n)
                                for n, a, b, t in zip(early + ("small",), by_owner, got, transit)])
    landed = _chip_exchange(list(chip_sums))
    halves = {n: _chip_sum(o, t, "chip_sum_" + n) for n, o, t in zip(early + ("small",), own_sums, landed)}
    late_landed = _chip_wait(*flight["chip"], grad_x)
    for n, o, t in zip(late, flight["own"], late_landed):
        halves[n] = _chip_sum(o, t, "chip_sum_" + n)
    *others, small_all = _final_exchange([halves[n] for n in BIG], halves["small"])
    flat = small_all.reshape(-1)
    grads, off = {}, 0
    for n in SMALL:
        shape = (3, 4 * w[n].shape[-1]) if n == "conv_w" else w[n].shape[1:] if n != "norm_final_g" else w[n].shape
        size = math.prod(shape)
        grads[n] = flat[off:off + size].reshape(shape)
        off += size
    loss = flat[off]
    cw = w["conv_w"].shape[-1]
    grads["conv_w"] = lax.dynamic_slice_in_dim(grads["conv_w"], xy * cw, cw, axis=1)
    grads = {n: _view(n, grads[n].reshape(w[n].shape)) for n in SMALL}
    wv, mv, vv = ({n: _view(n, t[n]) for n in WEIGHTS} for t in (w, m, v))

    delta, new_m, new_v = {}, {}, {}
    for n, other in zip(BIG, others):
        two_d = lambda t: t.reshape(t.shape[-2:])
        grads[n], delta[n], new_m[n], new_v[n] = _adamw_halves(
            two_d(wv[n]), halves[n], other, two_d(mv[n]), two_d(vv[n]), place, "adamw_" + n)
    for group, name in ((("b_re", "b_im"), "adamw_b"), (tuple(n for n in SMALL if n not in ("b_re", "b_im")), "adamw_small")):
        row = lambda t: t.reshape(1, -1) if t.ndim == 1 else t
        d_, m_, v_ = _adamw_many(*[[row(t[n]) for n in group] for t in (wv, grads, mv, vv)], name)
        for n, dn, mn, vn in zip(group, d_, m_, v_):
            delta[n], new_m[n], new_v[n] = (t.reshape(wv[n].shape) for t in (dn, mn, vn))
    natural = lambda t: [_unview(n, t[n].reshape(wv[n].shape)) for n in WEIGHTS]
    return (loss, grad_x[None], *natural(grads), *natural(delta), *natural(new_m), *natural(new_v))
```

```python
import functools
import math

import jax
import jax.numpy as jnp
import numpy as np
from jax import lax
from jax.experimental import pallas as pl
from jax.experimental.pallas import tpu as pltpu

F32 = jnp.float32
BF16 = jnp.bfloat16

D_MODEL = 1024
N_Q_HEADS = 8
N_KV_HEADS = 2
HEAD_DIM = 64
ATTN_WIDTH = 512
KV_WIDTH = 128
QKV_WIDTH = ATTN_WIDTH + 2 * KV_WIDTH
WINDOW = 128
BLOCK = 128
ROPE_DIM = 16
ROPE_THETA = 500000.0
SSM_WIDTH = 512
SSM_GROUP = 16
N_SSM_GROUPS = 32
SSM_STATE = 64
IN_WIDTH = 1280
D_FF = 2816
EPS = 1e-6
ADAM_LR = 0.001
ADAM_B1 = 0.9
ADAM_B2 = 0.999
ADAM_EPS = 1e-08
ADAM_WD = 0.01
ADAM_STEP = 10

VMEM_BYTES_V7X = 64 * 1024 * 1024
SUBLANES = 8
LANES = 128
SSM_CB = 4
SSM_CH = 128
SSM_ST = 512
N_SEG = SUBLANES

NN = (((1,), (0,)), ((), ()))
NT = (((1,), (1,)), ((), ()))
TN = (((0,), (0,)), ((), ()))


def _params(sem=None, vmem_mb=48):
    limit = vmem_mb * 1024 * 1024
    assert limit < VMEM_BYTES_V7X
    return pltpu.CompilerParams(dimension_semantics=sem, vmem_limit_bytes=limit)


def _dg(a, b, dims):
    return lax.dot_general(a, b, dims, preferred_element_type=F32)


def _sigmoid(x):
    return 1.0 / (1.0 + jnp.exp(-x))


_SQRT_HALF = 0.7071067811865476
_INV_SQRT_2PI = 0.3989422804014327


def _gelu(x):
    return 0.5 * x * (1.0 + lax.erf(x * _SQRT_HALF))


def _gelu_grad(x):
    return 0.5 * (1.0 + lax.erf(x * _SQRT_HALF)) + x * (_INV_SQRT_2PI * jnp.exp(-0.5 * x * x))


def _mm_tn(a, b, tm, tn, name):
    k, m = a.shape
    n = b.shape[1]

    def body(a_ref, b_ref, o_ref):
        o_ref[...] = _dg(a_ref[...], b_ref[...], TN)

    return pl.pallas_call(
        body, grid=(m // tm, n // tn),
        in_specs=[pl.BlockSpec((k, tm), lambda i, j: (0, i)), pl.BlockSpec((k, tn), lambda i, j: (0, j))],
        out_specs=pl.BlockSpec((tm, tn), lambda i, j: (i, j)),
        out_shape=jax.ShapeDtypeStruct((m, n), F32), name=name,
        compiler_params=_params(("parallel", "parallel")),
    )(a, b)


def _mm_nn_cols(a, b4, tm, name):
    m, k = a.shape
    s, _, n = b4.shape

    def body(a_ref, b_ref, o_ref):
        o_ref[...] = _dg(a_ref[...], b_ref[...], NN)

    return pl.pallas_call(
        body, grid=(m // tm, s),
        in_specs=[pl.BlockSpec((tm, k), lambda i, j: (i, 0)), pl.BlockSpec((None, k, n), lambda i, j: (j, 0, 0))],
        out_specs=pl.BlockSpec((tm, n), lambda i, j: (i, j)),
        out_shape=jax.ShapeDtypeStruct((m, s * n), F32), name=name,
        compiler_params=_params(("parallel", "parallel")),
    )(a, b4)


def _mm_tn_cols(a, b2, s, tm, name):
    k, m = a.shape
    h, _, wide = b2.shape
    per = s // h
    n = wide // per

    def body(a_ref, b_ref, o_ref):
        o_ref[...] = _dg(a_ref[...], b_ref[...], TN)

    return pl.pallas_call(
        body, grid=(s, m // tm),
        in_specs=[pl.BlockSpec((k, tm), lambda j, i: (0, i)),
                  pl.BlockSpec((None, k, n), lambda j, i: (j // per, 0, j % per))],
        out_specs=pl.BlockSpec((None, tm, n), lambda j, i: (j, i, 0)),
        out_shape=jax.ShapeDtypeStruct((s, m, n), F32), name=name,
        compiler_params=_params(("parallel", "parallel")),
    )(a, b2)


TM_EW = 256


def _rms_bwd_vals(xv, gv, dy):
    r = lax.rsqrt(jnp.mean(xv * xv, axis=-1, keepdims=True) + EPS)
    xh = xv * r
    dxh = dy * gv
    dx = r * (dxh - xh * jnp.mean(dxh * xh, axis=-1, keepdims=True))
    return dx, dy * xh


TM_FUSED = 256


def _rms_vals(xv, gv):
    return xv * lax.rsqrt(jnp.mean(xv * xv, axis=-1, keepdims=True) + EPS) * gv


def _rope_blocks(src, dst, c, lo, hi):
    nq = ATTN_WIDTH // LANES
    for blk in range(nq + 1):
        t = src[:, blk * LANES:(blk + 1) * LANES]
        dst[:, blk * LANES:(blk + 1) * LANES] = (
            t * c + pltpu.roll(t, LANES - 8, 1) * lo + pltpu.roll(t, 8, 1) * hi).astype(BF16)
    dst[:, (nq + 1) * LANES:] = src[:, (nq + 1) * LANES:].astype(BF16)


def _rms_mm_rope(x, g, wt, tabs, name):
    l, d = x.shape
    n = wt.shape[0]

    def body(x_ref, g_ref, w_ref, c_ref, lo_ref, hi_ref, h_ref, qkv_ref, u_ref):
        h = _rms_vals(x_ref[...], g_ref[...]).astype(BF16)
        h_ref[...] = h
        out = _dg(h, w_ref[...], NT)
        _rope_blocks(out[:, :QKV_WIDTH], qkv_ref, c_ref[...], lo_ref[...], hi_ref[...])
        u_ref[...] = out[:, QKV_WIDTH:]

    row = lambda width: pl.BlockSpec((TM_FUSED, width), lambda i: (i, 0))
    return pl.pallas_call(
        body, grid=(l // TM_FUSED,),
        in_specs=[row(d), pl.BlockSpec((1, d), lambda i: (0, 0)), pl.BlockSpec((n, d), lambda i: (0, 0)),
                  row(LANES), row(LANES), row(LANES)],
        out_specs=[row(d), row(QKV_WIDTH), row(n - QKV_WIDTH)],
        out_shape=[jax.ShapeDtypeStruct((l, d), BF16), jax.ShapeDtypeStruct((l, QKV_WIDTH), BF16),
                   jax.ShapeDtypeStruct((l, n - QKV_WIDTH), F32)],
        name=name, compiler_params=_params(("parallel",)),
    )(x, g, wt, *tabs)


def _mix_mm_res_rms(attn, ys, g_attn, g_ssm, b, res, g, name):
    l, w = attn.shape
    d = b.shape[1]

    def body(a_ref, y_ref, ga_ref, gs_ref, b_ref, r_ref, g_ref, m_ref, x_ref, h_ref):
        m_ref[:, :w] = _rms_vals(a_ref[...], ga_ref[...]).astype(BF16)
        m_ref[:, w:] = _rms_vals(y_ref[...], gs_ref[...]).astype(BF16)
        xv = r_ref[...] + _dg(m_ref[...], b_ref[...], NN)
        x_ref[...] = xv
        h_ref[...] = _rms_vals(xv, g_ref[...]).astype(BF16)

    row = lambda width: pl.BlockSpec((TM_FUSED, width), lambda i: (i, 0))
    vec = lambda width: pl.BlockSpec((1, width), lambda i: (0, 0))
    return pl.pallas_call(
        body, grid=(l // TM_FUSED,),
        in_specs=[row(w), row(w), vec(w), vec(w), pl.BlockSpec((2 * w, d), lambda i: (0, 0)), row(d), vec(d)],
        out_specs=[row(2 * w), row(d), row(d)],
        out_shape=[jax.ShapeDtypeStruct((l, 2 * w), BF16), jax.ShapeDtypeStruct((l, d), F32),
                   jax.ShapeDtypeStruct((l, d), BF16)],
        name=name, compiler_params=_params(("parallel",)),
    )(attn, ys, g_attn, g_ssm, b, res, g)


def _mm_res_loss(a, b, res, g, target):
    l, k = a.shape
    d = b.shape[1]

    def body(a_ref, b_ref, r_ref, g_ref, t_ref, loss_ref, dx_ref, dxb_ref, dg_ref):
        xv = r_ref[...] + _dg(a_ref[...], b_ref[...], NN)
        gv = g_ref[...]
        r = lax.rsqrt(jnp.mean(xv * xv, axis=-1, keepdims=True) + EPS)
        xh = xv * r
        e = xh * gv - t_ref[...]
        part = jnp.sum(jnp.sum(e * e, axis=1, keepdims=True), axis=0, keepdims=True) * (0.5 / d)
        dy = e * (1.0 / d)
        dxh = dy * gv
        dx = r * (dxh - xh * jnp.mean(dxh * xh, axis=-1, keepdims=True))
        dx_ref[...] = dx
        dxb_ref[...] = dx.astype(BF16)

        @pl.when(pl.program_id(0) == 0)
        def _():
            dg_ref[...] = jnp.zeros_like(dg_ref)
            loss_ref[...] = jnp.zeros_like(loss_ref)

        dg_ref[...] += jnp.sum(dy * xh, axis=0, keepdims=True)
        loss_ref[...] += part

    row = lambda width: pl.BlockSpec((TM_FUSED, width), lambda i: (i, 0))
    vec = pl.BlockSpec((1, d), lambda i: (0, 0))
    return pl.pallas_call(
        body, grid=(l // TM_FUSED,),
        in_specs=[row(k), pl.BlockSpec((k, d), lambda i: (0, 0)), row(d), vec, row(d)],
        out_specs=[pl.BlockSpec((1, 1), lambda i: (0, 0)), row(d), row(d), vec],
        out_shape=[jax.ShapeDtypeStruct((1, 1), F32), jax.ShapeDtypeStruct((l, d), F32),
                   jax.ShapeDtypeStruct((l, d), BF16), jax.ShapeDtypeStruct((1, d), F32)],
        name="mm_down_loss", compiler_params=_params(("arbitrary",)),
    )(a, b, res, g, target)


def _mm_rms_bwd(a, b, a_spec, b_spec, matmul, x, g, res, name):
    l, d = x.shape

    def body(a_ref, b_ref, x_ref, g_ref, res_ref, dx_ref, dxb_ref, dg_ref):
        dx, dgr = _rms_bwd_vals(x_ref[...], g_ref[...], matmul(a_ref, b_ref))
        dx = dx + res_ref[...]
        dx_ref[...] = dx
        dxb_ref[...] = dx.astype(BF16)

        @pl.when(pl.program_id(0) == 0)
        def _():
            dg_ref[...] = jnp.zeros_like(dg_ref)

        dg_ref[...] += jnp.sum(dgr, axis=0, keepdims=True)

    row = pl.BlockSpec((TM_FUSED, d), lambda i: (i, 0))
    vec = pl.BlockSpec((1, d), lambda i: (0, 0))
    return pl.pallas_call(
        body, grid=(l // TM_FUSED,), in_specs=[a_spec, b_spec, row, vec, row], out_specs=[row, row, vec],
        out_shape=[jax.ShapeDtypeStruct((l, d), F32), jax.ShapeDtypeStruct((l, d), BF16),
                   jax.ShapeDtypeStruct((1, d), F32)],
        name=name, compiler_params=_params(("arbitrary",)),
    )(a, b, x, g, res)


def _mm_nn_rms_bwd(a, b, x, g, res, name):
    return _mm_rms_bwd(a, b, pl.BlockSpec((TM_FUSED, a.shape[1]), lambda i: (i, 0)),
                       pl.BlockSpec(b.shape, lambda i: (0, 0)),
                       lambda a_ref, b_ref: _dg(a_ref[...], b_ref[...], NN), x, g, res, name)


def _mm_cols_rms_bwd(a2, b4, x, g, res, name):
    h, _, wide = a2.shape
    s, _, n = b4.shape
    per = s // h

    def matmul(a_ref, b_ref):
        acc = None
        for j in range(s):
            part = _dg(a_ref[j // per, :, (j % per) * n:(j % per + 1) * n], b_ref[j], NT)
            acc = part if acc is None else acc + part
        return acc

    return _mm_rms_bwd(a2, b4, pl.BlockSpec((h, TM_FUSED, wide), lambda i: (0, i, 0)),
                       pl.BlockSpec(b4.shape, lambda i: (0, 0, 0)), matmul, x, g, res, name)


def _mm_mix_bwd(dx, b, attn, ys, g_attn, g_ssm, name):
    l, w = attn.shape
    d = dx.shape[1]

    def body(dx_ref, b_ref, a_ref, y_ref, ga_ref, gs_ref, da_ref, dy_ref, dga_ref, dgs_ref):
        @pl.when(pl.program_id(0) == 0)
        def _():
            dga_ref[...] = jnp.zeros_like(dga_ref)
            dgs_ref[...] = jnp.zeros_like(dgs_ref)

        dm = _dg(dx_ref[...], b_ref[...], NT)
        for src, gr, off, dst, dgr in ((a_ref, ga_ref, 0, da_ref, dga_ref), (y_ref, gs_ref, w, dy_ref, dgs_ref)):
            dxv, dg_rows = _rms_bwd_vals(src[...], gr[...], dm[:, off:off + w])
            dst[...] = dxv
            dgr[...] += jnp.sum(dg_rows, axis=0, keepdims=True)

    row = lambda width: pl.BlockSpec((TM_FUSED, width), lambda i: (i, 0))
    vec = pl.BlockSpec((1, w), lambda i: (0, 0))
    return pl.pallas_call(
        body, grid=(l // TM_FUSED,),
        in_specs=[row(d), pl.BlockSpec((2 * w, d), lambda i: (0, 0)), row(w), row(w), vec, vec],
        out_specs=[row(w), row(w), vec, vec],
        out_shape=[jax.ShapeDtypeStruct((l, w), F32), jax.ShapeDtypeStruct((l, w), F32),
                   jax.ShapeDtypeStruct((1, w), F32), jax.ShapeDtypeStruct((1, w), F32)],
        name=name, compiler_params=_params(("arbitrary",)),
    )(dx, b, attn, ys, g_attn, g_ssm)


def _rope_tables(l):
    half = ROPE_DIM // 2
    f32 = np.float32
    inv_freq = np.power(f32(ROPE_THETA), -np.arange(half, dtype=f32) / f32(half))
    ang = np.arange(l, dtype=f32)[:, None] * inv_freq[None, :]
    cos, sin = np.cos(ang), np.sin(ang)
    ones = np.ones((l, HEAD_DIM - ROPE_DIM), f32)
    zeros = np.zeros((l, HEAD_DIM - ROPE_DIM), f32)
    zh = np.zeros((l, half), f32)
    c = np.concatenate([cos, cos, ones], axis=1)
    s_lo = np.concatenate([-sin, zh, zeros], axis=1)
    s_hi = np.concatenate([zh, sin, zeros], axis=1)
    return tuple(jnp.asarray(np.tile(t, (1, LANES // HEAD_DIM)), F32) for t in (c, s_lo, s_hi))


def _rope_bwd(dq, dkv, du_ssm, dpre, d_skip, tabs):
    l = dq.shape[0]
    nq = ATTN_WIDTH // LANES

    def body(dq_ref, dkv_ref, du_ref, dpre_ref, ds_ref, c_ref, lo_ref, hi_ref, o_ref):
        c, lo, hi = c_ref[...], lo_ref[...], hi_ref[...]
        for blk in range(nq + 1):
            t = dq_ref[:, blk * LANES:(blk + 1) * LANES] if blk < nq else dkv_ref[:, :KV_WIDTH]
            g = t * c + pltpu.roll(t * lo, 8, 1) + pltpu.roll(t * hi, LANES - 8, 1)
            o_ref[:, blk * LANES:(blk + 1) * LANES] = g.astype(BF16)
        o_ref[:, (nq + 1) * LANES:QKV_WIDTH] = dkv_ref[:, KV_WIDTH:].astype(BF16)
        o_ref[:, QKV_WIDTH:] = (du_ref[...] + dpre_ref[...] * ds_ref[...]).astype(BF16)

    tab = pl.BlockSpec((TM_EW, LANES), lambda i: (i, 0))
    wide = pl.BlockSpec((TM_EW, SSM_WIDTH), lambda i: (i, 0))
    return pl.pallas_call(
        body, grid=(l // TM_EW,),
        in_specs=[wide, pl.BlockSpec((TM_EW, 2 * KV_WIDTH), lambda i: (i, 0)), wide, wide,
                  pl.BlockSpec((1, SSM_WIDTH), lambda i: (0, 0)), tab, tab, tab],
        out_specs=pl.BlockSpec((TM_EW, IN_WIDTH), lambda i: (i, 0)),
        out_shape=jax.ShapeDtypeStruct((l, IN_WIDTH), BF16), name="rope_bwd",
        compiler_params=_params(("parallel",)),
    )(dq, dkv, du_ssm, dpre, d_skip, *tabs)


_Q_COLS = ATTN_WIDTH // LANES
_SCALE = HEAD_DIM ** -0.5
_NEG = -1e30


def _window_specs(nb, width, col):
    return [
        pl.BlockSpec((BLOCK, width), lambda n: (jnp.maximum(n - 1, 0), col)),
        pl.BlockSpec((BLOCK, width), lambda n: (n, col)),
        pl.BlockSpec((BLOCK, width), lambda n: (jnp.minimum(n + 1, nb - 1), col)),
    ]


def _stacked_sink(sink_ref, heads):
    rid = lax.broadcasted_iota(jnp.int32, (len(heads) * BLOCK, 1), 0)
    sk = jnp.full(rid.shape, sink_ref[0, heads[-1]], F32)
    for g in range(len(heads) - 2, -1, -1):
        sk = jnp.where(rid < (g + 1) * BLOCK, sink_ref[0, heads[g]], sk)
    return sk


def _attn_fwd(qkv, sink):
    l = qkv.shape[0]
    nb = l // BLOCK
    grp = N_Q_HEADS // N_KV_HEADS

    def body(sink_ref, q_ref, k0, k1, k2, v0, v1, v2, o_ref, lse_ref):
        n = pl.program_id(0)
        q = q_ref[...]
        kw = jnp.concatenate([k0[...], k1[...], k2[...]], axis=0)
        vw = jnp.concatenate([v0[...], v1[...], v2[...]], axis=0)
        row = lax.broadcasted_iota(jnp.int32, (grp * BLOCK, 3 * BLOCK), 0)
        col = lax.broadcasted_iota(jnp.int32, (grp * BLOCK, 3 * BLOCK), 1)
        valid = jnp.abs(col - BLOCK - (row & (BLOCK - 1))) <= WINDOW
        valid &= jnp.logical_not((n == 0) & (col < BLOCK))
        valid &= jnp.logical_not((n == nb - 1) & (col >= 2 * BLOCK))
        for hk in range(N_KV_HEADS):
            heads = range(hk * grp, (hk + 1) * grp)
            qs = jnp.concatenate([q[:, h * HEAD_DIM:(h + 1) * HEAD_DIM] for h in heads], axis=0)
            kh = kw[:, hk * HEAD_DIM:(hk + 1) * HEAD_DIM]
            vh = vw[:, hk * HEAD_DIM:(hk + 1) * HEAD_DIM]
            s = jnp.where(valid, _dg(qs, kh, NT) * _SCALE, _NEG)
            sk = _stacked_sink(sink_ref, heads)
            m = jnp.maximum(jnp.max(s, axis=1, keepdims=True), sk)
            p = jnp.exp(s - m)
            denom = jnp.sum(p, axis=1, keepdims=True) + jnp.exp(sk - m)
            o = _dg((p / denom).astype(BF16), vh, NN)
            lse = m + jnp.log(denom)
            for g, h in enumerate(heads):
                o_ref[:, h * HEAD_DIM:(h + 1) * HEAD_DIM] = o[g * BLOCK:(g + 1) * BLOCK]
                lse_ref[:, h:h + 1] = lse[g * BLOCK:(g + 1) * BLOCK]

    return pl.pallas_call(
        body, grid=(nb,),
        in_specs=[pl.BlockSpec(memory_space=pltpu.SMEM),
                  pl.BlockSpec((BLOCK, ATTN_WIDTH), lambda n: (n, 0))]
        + _window_specs(nb, KV_WIDTH, _Q_COLS) + _window_specs(nb, KV_WIDTH, _Q_COLS + 1),
        out_specs=[pl.BlockSpec((BLOCK, ATTN_WIDTH), lambda n: (n, 0)),
                   pl.BlockSpec((BLOCK, N_Q_HEADS), lambda n: (n, 0))],
        out_shape=[jax.ShapeDtypeStruct((l, ATTN_WIDTH), F32), jax.ShapeDtypeStruct((l, N_Q_HEADS), F32)],
        name="attn_fwd", compiler_params=_params(("parallel",)),
    )(sink, qkv, qkv, qkv, qkv, qkv, qkv, qkv)


def _attn_bwd(qkv, attn, dattn, lse, sink):
    l = qkv.shape[0]
    nb = l // BLOCK
    grp = N_Q_HEADS // N_KV_HEADS
    win = 3 * BLOCK

    def body(sink_ref, q_ref, k0, k1, k2, v0, v1, v2, o_ref, d_ref, l_ref, dq_ref, dkv_ref, dsink_ref, ring_ref):
        n = pl.program_id(0)

        @pl.when(n == 0)
        def _():
            dsink_ref[...] = jnp.zeros_like(dsink_ref)
            ring_ref[...] = jnp.zeros_like(ring_ref)

        @pl.when(n < nb)
        def _():
            first, last = n == 0, n == nb - 1
            cat = lambda a, b, c: jnp.concatenate([a[...], b[...], c[...]], axis=0)
            q, kw, vw = q_ref[...], cat(k0, k1, k2), cat(v0, v1, v2)
            dov = d_ref[...]
            prod = o_ref[...] * dov
            dob = dov.astype(BF16)
            lse = l_ref[...]
            row = lax.broadcasted_iota(jnp.int32, (grp * BLOCK, win), 0)
            col = lax.broadcasted_iota(jnp.int32, (grp * BLOCK, win), 1)
            valid = jnp.abs(col - BLOCK - (row & (BLOCK - 1))) <= WINDOW
            valid &= jnp.logical_not(first & (col < BLOCK))
            valid &= jnp.logical_not(last & (col >= 2 * BLOCK))

            dsink_parts, dks, dvs = [], [], []
            for hk in range(N_KV_HEADS):
                heads = range(hk * grp, (hk + 1) * grp)
                ksl = slice(hk * HEAD_DIM, (hk + 1) * HEAD_DIM)
                hsl = [slice(h * HEAD_DIM, (h + 1) * HEAD_DIM) for h in heads]
                stack = lambda parts: jnp.concatenate(parts, axis=0)
                qs = stack([q[:, s_] for s_ in hsl])
                dos = stack([dob[:, s_] for s_ in hsl])
                deltas = stack([jnp.sum(prod[:, s_], axis=1, keepdims=True) for s_ in hsl])
                lses = stack([lse[:, h:h + 1] for h in heads])
                kh, vh = kw[:, ksl], vw[:, ksl]
                s = jnp.where(valid, _dg(qs, kh, NT) * _SCALE, _NEG)
                p = jnp.exp(s - lses)
                dp = _dg(dos, vh, NT)
                ds = (p * (dp - deltas) * _SCALE).astype(BF16)
                dq = _dg(ds, kh, NN)
                sink_rows = jnp.exp(_stacked_sink(sink_ref, heads) - lses) * deltas
                for g in range(grp):
                    dq_ref[:, hsl[g]] = dq[g * BLOCK:(g + 1) * BLOCK]
                    dsink_parts.append(jnp.sum(sink_rows[g * BLOCK:(g + 1) * BLOCK], axis=0, keepdims=True))
                dks.append(_dg(ds, qs, TN))
                dvs.append(_dg(p.astype(BF16), dos, TN))
            dsink_ref[...] -= jnp.concatenate(dsink_parts, axis=1)
            part = jnp.concatenate(dks + dvs, axis=1)
            ring_ref[(n + 2) % 3] += part[0:BLOCK]
            ring_ref[n % 3] += part[BLOCK:2 * BLOCK]
            ring_ref[(n + 1) % 3] = part[2 * BLOCK:]

        @pl.when(n >= 1)
        def _():
            dkv_ref[...] = ring_ref[(n + 2) % 3]

    centre = lambda n: jnp.minimum(n, nb - 1)
    window = lambda width, col: [
        pl.BlockSpec((BLOCK, width), lambda n: (jnp.maximum(centre(n) - 1, 0), col)),
        pl.BlockSpec((BLOCK, width), lambda n: (centre(n), col)),
        pl.BlockSpec((BLOCK, width), lambda n: (jnp.minimum(centre(n) + 1, nb - 1), col))]
    own = lambda width: pl.BlockSpec((BLOCK, width), lambda n: (centre(n), 0))
    return pl.pallas_call(
        body, grid=(nb + 1,),
        in_specs=[pl.BlockSpec(memory_space=pltpu.SMEM), own(ATTN_WIDTH)]
        + window(KV_WIDTH, _Q_COLS) + window(KV_WIDTH, _Q_COLS + 1)
        + [own(ATTN_WIDTH), own(ATTN_WIDTH), own(N_Q_HEADS)],
        out_specs=[own(ATTN_WIDTH), pl.BlockSpec((BLOCK, 2 * KV_WIDTH), lambda n: (jnp.maximum(n - 1, 0), 0)),
                   pl.BlockSpec((1, N_Q_HEADS), lambda n: (0, 0))],
        out_shape=[jax.ShapeDtypeStruct((l, ATTN_WIDTH), F32), jax.ShapeDtypeStruct((l, 2 * KV_WIDTH), F32),
                   jax.ShapeDtypeStruct((1, N_Q_HEADS), F32)],
        scratch_shapes=[pltpu.VMEM((3, BLOCK, 2 * KV_WIDTH), F32)],
        name="attn_bwd", compiler_params=_params(("arbitrary",)),
    )(sink, qkv, qkv, qkv, qkv, qkv, qkv, qkv, attn, dattn, lse)


def _ssm_disc(a_re, a_im, log_step, b_re, b_im):
    step = jnp.exp(log_step)[..., None]
    mag = jnp.exp(a_re * step)
    lb_re, lb_im = mag * jnp.cos(a_im * step), mag * jnp.sin(a_im * step)
    nr, ni = lb_re - 1.0, lb_im
    den = a_re * a_re + a_im * a_im
    f_re = ((nr * a_re + ni * a_im) / den)[..., None]
    f_im = ((ni * a_re - nr * a_im) / den)[..., None]
    return lb_re, lb_im, f_re * b_re - f_im * b_im, f_re * b_im + f_im * b_re


def _ssm_pack(lb_re, lb_im, bb_re, bb_im, c_re, c_im):
    eye = jnp.eye(SSM_CH // SSM_GROUP, dtype=F32)
    ng = SSM_CH // SSM_GROUP

    def diag_b(bb):
        t = bb.reshape(2, SSM_CB, ng, SSM_STATE, SSM_GROUP)
        return jnp.einsum('dkgpc,gh->dkgchp', t, eye).reshape(2, SSM_CB, SSM_CH, SSM_ST)

    def diag_c(cc):
        t = cc.reshape(2, SSM_CB, ng, SSM_GROUP, SSM_STATE)
        return jnp.einsum('dkgcp,gh->dkhpgc', t, eye).reshape(2, SSM_CB, SSM_ST, SSM_CH)

    bcat = jnp.concatenate([diag_b(bb_re), diag_b(bb_im)], axis=-1)
    ccat = jnp.concatenate([diag_c(c_re), -diag_c(c_im)], axis=-2)
    lam_re = lb_re.reshape(2, SSM_CB, 1, SSM_ST)
    lam_im = lb_im.reshape(2, SSM_CB, 1, SSM_ST)
    return bcat, ccat, lam_re, lam_im


def _ssm_unpack(dbcat, dccat, dlam_re, dlam_im):
    ng = SSM_CH // SSM_GROUP
    eye = jnp.eye(ng, dtype=F32)

    def undiag_b(t):
        t = t.reshape(2, SSM_CB, ng, SSM_GROUP, ng, SSM_STATE)
        return jnp.einsum('dkgchp,gh->dkgpc', t, eye).reshape(2, N_SSM_GROUPS, SSM_STATE, SSM_GROUP)

    def undiag_c(t):
        t = t.reshape(2, SSM_CB, ng, SSM_STATE, ng, SSM_GROUP)
        return jnp.einsum('dkhpgc,gh->dkgcp', t, eye).reshape(2, N_SSM_GROUPS, SSM_GROUP, SSM_STATE)

    dbb_re, dbb_im = undiag_b(dbcat[..., :SSM_ST]), undiag_b(dbcat[..., SSM_ST:])
    dc_re, dc_im = undiag_c(dccat[:, :, :SSM_ST]), -undiag_c(dccat[:, :, SSM_ST:])
    shape = (2, N_SSM_GROUPS, SSM_STATE)
    return dlam_re.reshape(shape), dlam_im.reshape(shape), dbb_re, dbb_im, dc_re, dc_im


def _to_segments(t):
    l, w = t.shape
    return t.reshape(N_SEG, l // N_SEG, w).transpose(1, 0, 2).reshape(l, w)


def _from_segments(t):
    l, w = t.shape
    return t.reshape(l // N_SEG, N_SEG, w).transpose(1, 0, 2).reshape(l, w)


SSM_RC = 256
SSM_JC = SSM_RC // N_SEG
_RE, _IM = pl.ds(0, SSM_ST), pl.ds(SSM_ST, SSM_ST)


def _cfma(ar, ai, xr, xi, br, bi):
    return ar * xr - ai * xi + br, ar * xi + ai * xr + bi


def _chunk_rows(ci, rev, nc):
    start = jnp.where(rev, (nc - 1 - ci) * SSM_RC, ci * SSM_RC)
    return pl.ds(pl.multiple_of(start, SSM_RC), SSM_RC)


def _scan_chunk(src, dst, ar, ai, rev, nj, ci, carry, prev_ref=None):
    def rows_of(staged, j, k):
        at = jnp.where(rev, SSM_JC - 1 - k, k) if staged else j
        return pl.ds(pl.multiple_of(at * N_SEG, N_SEG), N_SEG)

    for k in range(SSM_JC):
        jj = ci * SSM_JC + k
        j = jnp.where(rev, nj - 1 - jj, jj)
        rows = rows_of(src[1], j, k)
        nr, ni = _cfma(ar, ai, carry[0], carry[1], src[0][rows, _RE], src[0][rows, _IM])
        if dst is not None:
            rows = rows_of(dst[1], j, k)
            dst[0][rows, _RE] = nr
            dst[0][rows, _IM] = ni
        if prev_ref is None:
            carry = (nr, ni)
            continue
        jp = jnp.where(rev, j - 1, j + 1)
        if k == SSM_JC - 1:
            inside = jnp.where((jp >= 0) & (jp < nj), 1.0, 0.0)
            jp = jnp.clip(jp, 0, nj - 1)
        prow = pl.ds(pl.multiple_of(jp * N_SEG, N_SEG), N_SEG)
        xr, xi = prev_ref[prow, _RE], prev_ref[prow, _IM]
        sr, si = nr * xr + ni * xi, ni * xr - nr * xi
        if k == SSM_JC - 1:
            sr, si = inside * sr, inside * si
        carry = (nr, ni, carry[2] + sr, carry[3] + si)
    return carry


def _segment_inits(ar, ai, end_r, end_i, rev, nj):
    pr, pi = ar, ai
    for _ in range(int(math.log2(nj))):
        pr, pi = pr * pr - pi * pi, 2.0 * pr * pi
    seg = lax.broadcasted_iota(jnp.int32, end_r.shape, 0)
    zero = jnp.zeros_like(end_r)

    def chain(shift, keep):
        ir, ii = zero, zero
        for _ in range(N_SEG - 1):
            tr, ti = _cfma(pr, pi, ir, ii, end_r, end_i)
            ir = jnp.where(keep, pltpu.roll(tr, shift, 0), 0.0)
            ii = jnp.where(keep, pltpu.roll(ti, shift, 0), 0.0)
        return ir, ii

    up_r, up_i = chain(1, seg >= 1)
    dn_r, dn_i = chain(N_SEG - 1, seg <= N_SEG - 2)
    return jnp.where(rev, dn_r, up_r), jnp.where(rev, dn_i, up_i)


def _ssm_specs(l):
    act = pl.BlockSpec((l, SSM_CH), lambda k, d: (0, k))
    bmat = pl.BlockSpec((None, None, SSM_CH, 2 * SSM_ST), lambda k, d: (d, k, 0, 0))
    cmat = pl.BlockSpec((None, None, 2 * SSM_ST, SSM_CH), lambda k, d: (d, k, 0, 0))
    lam = pl.BlockSpec((None, None, 1, SSM_ST), lambda k, d: (d, k, 0, 0))
    return act, bmat, cmat, lam


def _ssm_fwd(u_seg, bcat, ccat, lam_re, lam_im):
    l = u_seg.shape[0]
    nj = l // N_SEG
    nc = l // SSM_RC

    def body(u_ref, b_ref, c_ref, lr_ref, li_ref, y_ref, keep_ref, xs_ref, stage0, stage1, keep_sem):
        k, d = pl.program_id(0), pl.program_id(1)
        rev = d == 1
        shape = (N_SEG, SSM_ST)
        ar, ai = jnp.broadcast_to(lr_ref[...], shape), jnp.broadcast_to(li_ref[...], shape)
        zero = jnp.zeros(shape, F32)

        def inputs(ci, stage):
            rows = _chunk_rows(ci, rev, nc)
            bu = _dg(u_ref[rows, :], b_ref[...], NN)
            stage[...] = bu
            xs_ref[rows, :] = bu

        def first(stage, ci, carry):
            return _scan_chunk((stage, True), None, ar, ai, rev, nj, ci, carry)

        def first_pass(t, carry):
            inputs(2 * t + 1, stage1)
            carry = first(stage0, 2 * t, carry)
            inputs(2 * t + 2, stage0)
            return first(stage1, 2 * t + 1, carry)

        inputs(0, stage0)
        carry = lax.fori_loop(0, nc // 2 - 1, first_pass, (zero, zero))
        inputs(nc - 1, stage1)
        carry = first(stage0, nc - 2, carry)
        end_r, end_i = first(stage1, nc - 1, carry)
        init = _segment_inits(ar, ai, end_r, end_i, rev, nj)

        @pl.when(d == 0)
        def _():
            y_ref[...] = jnp.zeros_like(y_ref)

        def outputs(ci):
            rows = _chunk_rows(ci, rev, nc)
            y_ref[rows, :] += _dg(xs_ref[rows, :].astype(BF16), c_ref[...], NN)
            pltpu.make_async_copy(xs_ref.at[rows], keep_ref.at[d, k, rows], keep_sem).start()

        def second(ci, carry):
            return _scan_chunk((xs_ref, False), (xs_ref, False), ar, ai, rev, nj, ci, carry)

        def second_pass(ci, carry):
            outputs(ci - 1)
            return second(ci, carry)

        lax.fori_loop(1, nc, second_pass, second(0, init))
        outputs(nc - 1)
        pltpu.make_async_copy(xs_ref, keep_ref.at[d, k], keep_sem).wait()

    act, bmat, cmat, lam = _ssm_specs(l)
    return pl.pallas_call(
        body, grid=(SSM_CB, 2), in_specs=[act, bmat, cmat, lam, lam], out_specs=[act, ANY],
        out_shape=[jax.ShapeDtypeStruct((l, SSM_WIDTH), F32),
                   jax.ShapeDtypeStruct((2, SSM_CB, l, 2 * SSM_ST), F32)],
        scratch_shapes=[pltpu.VMEM((l, 2 * SSM_ST), F32), pltpu.VMEM((SSM_RC, 2 * SSM_ST), F32),
                        pltpu.VMEM((SSM_RC, 2 * SSM_ST), F32), pltpu.SemaphoreType.DMA],
        name="ssm_fwd", compiler_params=_params(("parallel", "arbitrary"), vmem_mb=56),
    )(u_seg, bcat.astype(BF16), ccat.astype(BF16), lam_re, lam_im)


def _ssm_bwd(u_seg, dy_seg, states, bcat, ccat, lam_re, lam_im):
    l = u_seg.shape[0]
    nj = l // N_SEG
    nc = l // SSM_RC

    def body(u_ref, dy_ref, keep_ref, b_ref, c_ref, lr_ref, li_ref,
             du_ref, db_ref, dc_ref, dlr_ref, dli_ref, xs_ref, gs_ref, stage0, stage1, keep_sem):
        k, d = pl.program_id(0), pl.program_id(1)
        rev = d == 1
        back = jnp.logical_not(rev)
        shape = (N_SEG, SSM_ST)
        ar, ai = jnp.broadcast_to(lr_ref[...], shape), -jnp.broadcast_to(li_ref[...], shape)
        zero = jnp.zeros(shape, F32)
        fetch = pltpu.make_async_copy(keep_ref.at[d, k], xs_ref, keep_sem)
        fetch.start()

        def inputs(ci, stage):
            rows = _chunk_rows(ci, back, nc)
            dx = _dg(dy_ref[rows, :], c_ref[...], NT)
            stage[...] = dx
            gs_ref[rows, :] = dx

        def first(stage, ci, carry):
            return _scan_chunk((stage, True), None, ar, ai, back, nj, ci, carry)

        def first_pass(t, carry):
            inputs(2 * t + 1, stage1)
            carry = first(stage0, 2 * t, carry)
            inputs(2 * t + 2, stage0)
            return first(stage1, 2 * t + 1, carry)

        inputs(0, stage0)
        carry = lax.fori_loop(0, nc // 2 - 1, first_pass, (zero, zero))
        inputs(nc - 1, stage1)
        carry = first(stage0, nc - 2, carry)
        end_r, end_i = first(stage1, nc - 1, carry)
        init = _segment_inits(ar, ai, end_r, end_i, back, nj)
        fetch.wait()
        db_ref[...] = jnp.zeros_like(db_ref)
        dc_ref[...] = jnp.zeros_like(dc_ref)

        @pl.when(d == 0)
        def _():
            du_ref[...] = jnp.zeros_like(du_ref)

        def outputs(ci, stage):
            rows = _chunk_rows(ci, back, nc)
            g = stage[...].astype(BF16)
            dc_ref[...] += _dg(xs_ref[rows, :].astype(BF16), dy_ref[rows, :], TN)
            db_ref[...] += _dg(u_ref[rows, :], g, TN)
            du_ref[rows, :] += _dg(g, b_ref[...], NT)

        def second(ci, stage, carry):
            return _scan_chunk((gs_ref, False), (stage, True), ar, ai, back, nj, ci, carry, prev_ref=xs_ref)

        def second_pass(t, carry):
            outputs(2 * t, stage0)
            carry = second(2 * t + 1, stage1, carry)
            outputs(2 * t + 1, stage1)
            return second(2 * t + 2, stage0, carry)

        carry = lax.fori_loop(0, nc // 2 - 1, second_pass, second(0, stage0, init + (zero, zero)))
        outputs(nc - 2, stage0)
        gr, gi, acc_r, acc_i = second(nc - 1, stage1, carry)
        outputs(nc - 1, stage1)

        seg = lax.broadcasted_iota(jnp.int32, shape, 0)
        jb = jnp.where(rev, nj - 1, 0)
        erow = pl.ds(pl.multiple_of((nj - 1 - jb) * N_SEG, N_SEG), N_SEG)

        def before(t):
            up = jnp.where(seg >= 1, pltpu.roll(t, 1, 0), 0.0)
            down = jnp.where(seg <= N_SEG - 2, pltpu.roll(t, N_SEG - 1, 0), 0.0)
            return jnp.where(rev, down, up)

        init_r, init_i = before(xs_ref[erow, _RE]), before(xs_ref[erow, _IM])
        acc_r = acc_r + gr * init_r + gi * init_i
        acc_i = acc_i + gi * init_r - gr * init_i
        dlr_ref[...] = jnp.sum(acc_r, axis=0, keepdims=True)
        dli_ref[...] = jnp.sum(acc_i, axis=0, keepdims=True)

    act, bmat, cmat, lam = _ssm_specs(l)
    return pl.pallas_call(
        body, grid=(SSM_CB, 2), in_specs=[act, act, ANY, bmat, cmat, lam, lam],
        out_specs=[act, bmat, cmat, lam, lam],
        out_shape=[jax.ShapeDtypeStruct((l, SSM_WIDTH), F32),
                   jax.ShapeDtypeStruct(bcat.shape, F32), jax.ShapeDtypeStruct(ccat.shape, F32),
                   jax.ShapeDtypeStruct(lam_re.shape, F32), jax.ShapeDtypeStruct(lam_im.shape, F32)],
        scratch_shapes=[pltpu.VMEM((l, 2 * SSM_ST), F32), pltpu.VMEM((l, 2 * SSM_ST), F32),
                        pltpu.VMEM((SSM_RC, 2 * SSM_ST), F32), pltpu.VMEM((SSM_RC, 2 * SSM_ST), F32),
                        pltpu.SemaphoreType.DMA],
        name="ssm_bwd", compiler_params=_params(("parallel", "arbitrary"), vmem_mb=58),
    )(u_seg, dy_seg, states, bcat.astype(BF16), ccat.astype(BF16), lam_re, lam_im)


def _glu_fwd(y_ssm, u, d_skip, w_glu):
    l, w = u.shape

    def body(y_ref, u_ref, d_ref, w_ref, pre_ref, s_ref, ys_ref):
        pre = y_ref[...] + d_ref[...] * u_ref[...]
        z = _gelu(pre)
        s = _dg(z.astype(BF16), w_ref[...], NN)
        pre_ref[...] = pre
        s_ref[...] = s
        ys_ref[...] = z * _sigmoid(s)

    row = pl.BlockSpec((TM_EW, w), lambda i: (i, 0))
    out = jax.ShapeDtypeStruct((l, w), F32)
    return pl.pallas_call(
        body, grid=(l // TM_EW,),
        in_specs=[row, row, pl.BlockSpec((1, w), lambda i: (0, 0)), pl.BlockSpec((w, w), lambda i: (0, 0))],
        out_specs=[row, row, row], out_shape=[out, out, out], name="glu_fwd",
        compiler_params=_params(("parallel",)),
    )(y_ssm, u, d_skip, w_glu)


def _glu_bwd(pre, s, dys, u, d_skip, w_glu):
    l, w = u.shape

    def body(pre_ref, s_ref, dys_ref, u_ref, d_ref, w_ref, dpre_ref, z_ref, ds_ref, dd_ref):
        pre, dys = pre_ref[...], dys_ref[...]
        z = _gelu(pre)
        sig = _sigmoid(s_ref[...])
        ds = (dys * z * sig * (1.0 - sig)).astype(BF16)
        dz = dys * sig + _dg(ds, w_ref[...], NT)
        dpre = dz * _gelu_grad(pre)
        dpre_ref[...] = dpre
        z_ref[...] = z.astype(BF16)
        ds_ref[...] = ds

        @pl.when(pl.program_id(0) == 0)
        def _():
            dd_ref[...] = jnp.zeros_like(dd_ref)

        dd_ref[...] += jnp.sum(dpre * u_ref[...], axis=0, keepdims=True)

    row = pl.BlockSpec((TM_EW, w), lambda i: (i, 0))
    vec = pl.BlockSpec((1, w), lambda i: (0, 0))
    return pl.pallas_call(
        body, grid=(l // TM_EW,),
        in_specs=[row, row, row, row, vec, pl.BlockSpec((w, w), lambda i: (0, 0))],
        out_specs=[row, row, row, vec],
        out_shape=[jax.ShapeDtypeStruct((l, w), F32), jax.ShapeDtypeStruct((l, w), BF16),
                   jax.ShapeDtypeStruct((l, w), BF16), jax.ShapeDtypeStruct((1, w), F32)],
        name="glu_bwd", compiler_params=_params(("arbitrary",)),
    )(pre, s, dys, u, d_skip, w_glu)


TM_CV = 512
TC_CV = 256
TM_CF = 256
TC_CF = D_FF // 2
HALO = SUBLANES


def _conv_specs(l, col0, tm=TM_CV, tc=TC_CV):
    per = tm // HALO
    nh = l // HALO
    off = col0 // tc
    return [
        pl.BlockSpec((HALO, tc), lambda j, i: (jnp.maximum(i * per - 1, 0), j + off)),
        pl.BlockSpec((tm, tc), lambda j, i: (i, j + off)),
        pl.BlockSpec((HALO, tc), lambda j, i: (jnp.minimum((i + 1) * per, nh - 1), j + off)),
    ]


def _ext(prev_ref, mid_ref, next_ref, first, last):
    p = jnp.where(first, 0.0, prev_ref[...])
    n = jnp.where(last, 0.0, next_ref[...])
    return jnp.concatenate([p, mid_ref[...], n], axis=0)


def _shift_dn(t):
    return pltpu.roll(t, 1, 0)


def _shift_up(t):
    return pltpu.roll(t, t.shape[0] - 1, 0)


def _conv3(e, w_ref, b_ref):
    return w_ref[0:1, :] * _shift_dn(e) + w_ref[1:2, :] * e + w_ref[2:3, :] * _shift_up(e) + b_ref[...]


def _convffn_fwd(up_pre, conv_w, conv_b):
    l = up_pre.shape[0]
    tm, tc = TM_CF, TC_CF
    ni = l // tm
    wspec = lambda off: pl.BlockSpec((3, tc), lambda j, i: (0, j + off))
    bspec = lambda off: pl.BlockSpec((1, tc), lambda j, i: (0, j + off))
    voff = D_FF // tc

    def body(gp, gm, gn, vp, vm, vn, wg, bg, wv, bv, o_ref):
        i = pl.program_id(1)
        first, last = i == 0, i == ni - 1
        gate = _conv3(_ext(gp, gm, gn, first, last), wg, bg)[HALO:HALO + tm]
        val = _conv3(_ext(vp, vm, vn, first, last), wv, bv)[HALO:HALO + tm]
        o_ref[...] = (gate * _sigmoid(gate) * val).astype(BF16)

    return pl.pallas_call(
        body, grid=(D_FF // tc, ni),
        in_specs=_conv_specs(l, 0, tm, tc) + _conv_specs(l, D_FF, tm, tc)
        + [wspec(0), bspec(0), wspec(voff), bspec(voff)],
        out_specs=pl.BlockSpec((tm, tc), lambda j, i: (i, j)),
        out_shape=jax.ShapeDtypeStruct((l, D_FF), BF16), name="convffn_fwd",
        compiler_params=_params(("parallel", "parallel")),
    )(up_pre, up_pre, up_pre, up_pre, up_pre, up_pre, conv_w, conv_b, conv_w, conv_b)


HALO_B = 2 * SUBLANES


def _convffn_bwd(up_pre, dx2b, w_down, conv_w, conv_b):
    l = up_pre.shape[0]
    ni = l // TM_CV
    d = dx2b.shape[1]
    wspec = lambda off: pl.BlockSpec((3, TC_CV), lambda i, j: (0, j + off))
    bspec = lambda off: pl.BlockSpec((1, TC_CV), lambda i, j: (0, j + off))
    voff = D_FF // TC_CV
    swap = lambda spec: pl.BlockSpec(spec.block_shape, lambda i, j, f=spec.index_map: f(j, i))
    per, nh = TM_CV // HALO_B, l // HALO_B
    dx_specs = [pl.BlockSpec((HALO_B, d), lambda i, j: (jnp.maximum(i * per - 1, 0), 0)),
                pl.BlockSpec((TM_CV, d), lambda i, j: (i, 0)),
                pl.BlockSpec((HALO_B, d), lambda i, j: (jnp.minimum((i + 1) * per, nh - 1), 0))]

    def body(gp, gm, gn, vp, vm, vn, xp, xm, xn, wd, wg, bg, wv, bv, dup_ref, pg_ref, pv_ref):
        i = pl.program_id(0)
        first, last = i == 0, i == ni - 1
        ge, ve = _ext(gp, gm, gn, first, last), _ext(vp, vm, vn, first, last)
        zero = jnp.zeros((HALO_B, d), BF16)
        dx = jnp.concatenate([jnp.where(first, zero, xp[...]), xm[...], jnp.where(last, zero, xn[...])], axis=0)
        de = _dg(dx, wd[...], NT)[HALO_B - HALO:HALO_B + TM_CV + HALO]
        taps = [(_shift_dn(e), e, _shift_up(e)) for e in (ge, ve)]
        conv = lambda t, w_ref, b_ref: w_ref[0:1, :] * t[0] + w_ref[1:2, :] * t[1] + w_ref[2:3, :] * t[2] + b_ref[...]
        gate, val = conv(taps[0], wg, bg), conv(taps[1], wv, bv)
        sig = _sigmoid(gate)
        silu = gate * sig
        dgate = de * val * (sig + silu * (1.0 - sig))
        dval = de * silu
        mid = slice(HALO, HALO + TM_CV)
        rid = lax.broadcasted_iota(jnp.int32, (SUBLANES, TC_CV), 0)
        for half, (dup, tap, w_ref, p_ref) in enumerate(((dgate, taps[0], wg, pg_ref), (dval, taps[1], wv, pv_ref))):
            dpre = w_ref[0:1, :] * _shift_up(dup) + w_ref[1:2, :] * dup + w_ref[2:3, :] * _shift_dn(dup)
            dup_ref[half] = dpre[mid].astype(BF16)
            dm_ = dup[mid]
            sums = [jnp.sum(dm_ * t[mid], axis=0, keepdims=True) for t in tap]
            sums.append(jnp.sum(dm_, axis=0, keepdims=True))
            acc = jnp.zeros((SUBLANES, TC_CV), F32)
            for k, sk in enumerate(sums):
                acc = jnp.where(rid == k, sk, acc)
            p_ref[...] = acc

    par = pl.BlockSpec((None, SUBLANES, TC_CV), lambda i, j: (i, 0, j))
    dup, pg, pv = pl.pallas_call(
        body, grid=(ni, D_FF // TC_CV),
        in_specs=[swap(s) for s in _conv_specs(l, 0) + _conv_specs(l, D_FF)] + dx_specs
        + [pl.BlockSpec((TC_CV, d), lambda i, j: (j, 0)), wspec(0), bspec(0), wspec(voff), bspec(voff)],
        out_specs=[pl.BlockSpec((2, TM_CV, TC_CV), lambda i, j: (0, i, j)), par, par],
        out_shape=[jax.ShapeDtypeStruct((2, l, D_FF), BF16),
                   jax.ShapeDtypeStruct((ni, SUBLANES, D_FF), F32), jax.ShapeDtypeStruct((ni, SUBLANES, D_FF), F32)],
        name="convffn_bwd", compiler_params=_params(("parallel", "parallel")),
    )(up_pre, up_pre, up_pre, up_pre, up_pre, up_pre, dx2b, dx2b, dx2b, w_down, conv_w, conv_b, conv_w, conv_b)
    return dup, jnp.concatenate([jnp.sum(pg, axis=0), jnp.sum(pv, axis=0)], axis=1)


def _local_step(x, target, wb, sp, mixer_weights=None, late_weights=None, ffn_grads_ready=None,
                ffn_grads_next=None, ffn_grads_last=None):
    l = x.shape[0]
    tabs = _rope_tables(l)
    disc = _ssm_disc(sp["a_re"], sp["a_im"], sp["log_step"], sp["b_re"], sp["b_im"])
    bcat, ccat, lam_re, lam_im = _ssm_pack(*disc, sp["c_re"], sp["c_im"])
    d_skip = sp["d_skip"].reshape(1, SSM_WIDTH)

    big = min(l, 1024)
    h, qkv, u = _rms_mm_rope(x, sp["norm_mix_g"], wb["w_in"], tabs, "mm_in")
    attn, lse = _attn_fwd(qkv, sp["sink"])
    u_seg = _to_segments(u).astype(BF16)
    y_seg, states = _ssm_fwd(u_seg, bcat, ccat, lam_re, lam_im)
    y_ssm = _from_segments(y_seg)
    if mixer_weights is not None:
        wb = dict(wb, **mixer_weights(attn))
    pre, s_glu, ys = _glu_fwd(y_ssm, u, d_skip, wb["w_glu"])
    mixed, x1, h2 = _mix_mm_res_rms(attn, ys, sp["norm_attn_g"], sp["norm_ssm_g"], wb["w_out"], x,
                                    sp["norm_ffn_g"], "mm_out")
    if late_weights is not None:
        wb = dict(wb, **late_weights(h2))
    up_pre = _mm_nn_cols(h2, wb["w_up"], big, "mm_up")
    conv_w = wb["conv_w"]
    act = _convffn_fwd(up_pre, conv_w, sp["conv_b"])
    loss, dx2, dx2b, d_final_g = _mm_res_loss(act, wb["w_down"], x1, sp["norm_final_g"].reshape(1, D_MODEL), target)

    g = {"norm_final_g": d_final_g.reshape(D_MODEL)}
    g["w_down"] = _mm_tn(act, dx2b, D_FF // 2, 512, "mm_down_dw")
    dup_pre, conv_par = _convffn_bwd(up_pre, dx2b, wb["w_down"], conv_w, sp["conv_b"])
    g["conv_w"], g["conv_b"] = conv_par[0:3], conv_par[3:4]
    g["w_up"] = _mm_tn_cols(h2, dup_pre, wb["w_up"].shape[0], 512, "mm_up_dw")
    zero = ffn_grads_ready(g["w_up"], g["w_down"]) if ffn_grads_ready is not None else 0.0
    dx1, dx1b, g["norm_ffn_g"] = _mm_cols_rms_bwd(dup_pre, wb["w_up"], x1, sp["norm_ffn_g"] + zero, dx2, "mm_up_dx")
    g["w_out"] = _mm_tn(mixed, dx1b, 1024, 1024, "mm_out_dw")
    zero = ffn_grads_next(g["w_out"]) if ffn_grads_next is not None else 0.0
    dattn, dys, g["norm_attn_g"], g["norm_ssm_g"] = _mm_mix_bwd(
        dx1b, wb["w_out"], attn, ys, sp["norm_attn_g"] + zero, sp["norm_ssm_g"], "mm_out_dx")
    dpre, zb, dsb, dd = _glu_bwd(pre, s_glu, dys, u, d_skip, wb["w_glu"])
    g["d_skip"] = dd.reshape(N_SSM_GROUPS, SSM_GROUP)
    g["w_glu"] = _mm_tn(zb, dsb, 512, 512, "mm_glu_dw")
    du_seg, dbcat, dccat, dlam_re, dlam_im = _ssm_bwd(u_seg, _to_segments(dpre).astype(BF16), states, bcat, ccat,
                                                      lam_re, lam_im)
    dlb_re, dlb_im, dbb_re, dbb_im, g["c_re"], g["c_im"] = _ssm_unpack(dbcat, dccat, dlam_re, dlam_im)
    _, disc_vjp = jax.vjp(_ssm_disc, sp["a_re"], sp["a_im"], sp["log_step"], sp["b_re"], sp["b_im"])
    g["a_re"], g["a_im"], g["log_step"], g["b_re"], g["b_im"] = disc_vjp((dlb_re, dlb_im, dbb_re, dbb_im))
    dq, dkv, g["sink"] = _attn_bwd(qkv, attn, dattn, lse, sp["sink"])
    zero = ffn_grads_last(dq) if ffn_grads_last is not None else 0.0
    dproj = _rope_bwd(dq, dkv, _from_segments(du_seg), dpre, d_skip + zero, tabs)
    g["w_in"] = _mm_tn(dproj, h, IN_WIDTH // 5, D_MODEL, "mm_in_dw")
    grad_x, _, g["norm_mix_g"] = _mm_nn_rms_bwd(dproj, wb["w_in"], x, sp["norm_mix_g"], dx1, "mm_in_dx")
    return loss, grad_x, g


MESH = pl.DeviceIdType.MESH
ANY = pl.BlockSpec(memory_space=pl.ANY)


def _place():
    x, y, c = lax.axis_index("x"), lax.axis_index("y"), lax.axis_index("c")
    chips = [(1 - x, y), (x, 1 - y), (1 - x, 1 - y)]
    return x, y, c, chips


def _chip_index(px, py):
    return 2 * px + py


CHUNK_BYTES = 256 * 1024
MAX_CHUNKS = 16


def _row_chunks(rows, row_bytes, align):
    n = max(1, min(MAX_CHUNKS, (rows * row_bytes) // CHUNK_BYTES))
    per = -(-rows // n)
    per = -(-per // align) * align
    return [(r0, min(per, rows - r0)) for r0 in range(0, rows, per)]


def _align_of(dtype):
    return SUBLANES * 4 // jnp.dtype(dtype).itemsize


def _remote(src, dst, send_sem, recv_sem, to):
    return pltpu.make_async_remote_copy(src_ref=src, dst_ref=dst, send_sem=send_sem, recv_sem=recv_sem,
                                        device_id=to, device_id_type=MESH)


CAST_ROWS = 64


def _gather_weights(shards, dtypes):
    nw = len(shards)

    def body(*refs):
        w_refs, o_refs = refs[:nw], refs[nw:2 * nw]
        send_sems, recv_sems, in_sems, out_sems = refs[2 * nw:2 * nw + 4]
        raw, cast = refs[2 * nw + 4:3 * nw + 4], refs[3 * nw + 4:]
        x, y, c, chips = _place()
        mine = _chip_index(x, y)
        sibling = (x, y, 1 - c)

        def rows_of(ref, chip, r0, nr):
            return ref.at[chip, pl.ds(r0, nr), :]

        def copy(wi, k, src, dst, to):
            return _remote(src, dst, send_sems.at[wi, k], recv_sems.at[wi, k], to)

        geo = []
        for wi in range(nw):
            rows, cols = w_refs[wi].shape
            row_bytes = cols * jnp.dtype(dtypes[wi]).itemsize
            geo.append((rows // 2, _row_chunks(rows // 2, row_bytes, _align_of(dtypes[wi]))))

        stage_in = [pltpu.make_async_copy(w_refs[wi], raw[wi], in_sems.at[wi]) for wi in range(nw)]
        for cp in stage_in:
            cp.start()
        staged = [raw[wi] if dtypes[wi] == w_refs[wi].dtype else cast[wi] for wi in range(nw)]
        stage_out = []
        for wi in range(nw):
            stage_in[wi].wait()
            if staged[wi] is not raw[wi]:
                def cast_rows(i, _, wi=wi):
                    rows = pl.ds(pl.multiple_of(i * CAST_ROWS, CAST_ROWS), CAST_ROWS)
                    cast[wi][rows, :] = raw[wi][rows, :].astype(dtypes[wi])
                    return 0

                lax.fori_loop(0, w_refs[wi].shape[0] // CAST_ROWS, cast_rows, 0)
            cp = pltpu.make_async_copy(staged[wi], o_refs[wi].at[mine], out_sems.at[wi])
            cp.start()
            stage_out.append(cp)

        for wi in range(nw):
            hr, half_chunks = geo[wi]
            for j, chip in enumerate(chips):
                for r0, nr in half_chunks:
                    copy(wi, j, staged[wi].at[pl.ds(c * hr + r0, nr), :],
                         rows_of(o_refs[wi], mine, c * hr + r0, nr), (*chip, c)).start()
        for wi in range(nw):
            hr, half_chunks = geo[wi]
            for j, chip in enumerate(chips):
                got = rows_of(o_refs[wi], _chip_index(*chip), c * hr, hr)
                copy(wi, j, got, got, (*chip, c)).wait_recv()
                for r0, nr in half_chunks:
                    piece = rows_of(o_refs[wi], _chip_index(*chip), c * hr + r0, nr)
                    copy(wi, 3 + j, piece, piece, sibling).start()
        for wi in range(nw):
            hr = geo[wi][0]
            for j, chip in enumerate(chips):
                got = rows_of(o_refs[wi], _chip_index(*chip), (1 - c) * hr, hr)
                copy(wi, 3 + j, got, got, sibling).wait_recv()
        for wi in range(nw):
            hr = geo[wi][0]
            sent = rows_of(o_refs[wi], mine, c * hr, hr)
            for k in range(6):
                copy(wi, k, sent, sent, sibling).wait_send()
            stage_out[wi].wait()

    return pl.pallas_call(
        body, in_specs=[ANY] * nw, out_specs=[ANY] * nw,
        out_shape=[jax.ShapeDtypeStruct((4, *s.shape), t) for s, t in zip(shards, dtypes)],
        scratch_shapes=[pltpu.SemaphoreType.DMA((nw, 6)), pltpu.SemaphoreType.DMA((nw, 6)),
                        pltpu.SemaphoreType.DMA((nw,)), pltpu.SemaphoreType.DMA((nw,))]
        + [pltpu.VMEM(s.shape, s.dtype) for s in shards] + [pltpu.VMEM(s.shape, t) for s, t in zip(shards, dtypes)],
        name="gather_weights", compiler_params=_params(vmem_mb=40),
    )(*shards)


HBM = pl.BlockSpec(memory_space=pltpu.HBM)
SEM = pl.BlockSpec(memory_space=pltpu.SEMAPHORE)
EFFECT = pltpu.SideEffectType.DATAFLOW_SIDE_EFFECTING


def _cast_place(w, place, dtype, after, name):
    rows, cols = w.shape
    tr = _row_tile(rows, cols, _align_of(dtype))

    def body(p_ref, w_ref, after_ref, o_ref):
        del p_ref, after_ref
        o_ref[...] = w_ref[...].astype(dtype)

    grid_spec = pltpu.PrefetchScalarGridSpec(
        num_scalar_prefetch=1, grid=(rows // tr,),
        in_specs=[pl.BlockSpec((tr, cols), lambda i, p: (i, 0)), ANY],
        out_specs=pl.BlockSpec((None, tr, cols), lambda i, p: (p[1], i, 0)))
    return pl.pallas_call(body, grid_spec=grid_spec, out_shape=jax.ShapeDtypeStruct((4, rows, cols), dtype),
                          name=name, compiler_params=_params(("parallel",)))(place, w, after)


def _split_start(name, arrays, n_pairs, issue):
    n = len(arrays)

    def body(*refs):
        issue(refs[:n], refs[n:n + n_pairs], refs[n + n_pairs:n + 2 * n_pairs])
        token = refs[2 * n + 2 * n_pairs]
        token[...] = jnp.zeros_like(token)

    dma = pltpu.SemaphoreType.DMA(())
    outs = pl.pallas_call(
        body, name=name,
        out_shape=[dma] * (2 * n_pairs) + [pltpu.HBM(t.shape, t.dtype) for t in arrays]
        + [jax.ShapeDtypeStruct((SUBLANES, LANES), F32)],
        in_specs=[HBM] * n, out_specs=[SEM] * (2 * n_pairs) + [HBM] * n + [pl.BlockSpec(memory_space=pltpu.VMEM)],
        input_output_aliases={a: 2 * n_pairs + a for a in range(n)},
        compiler_params=pltpu.CompilerParams(has_side_effects=EFFECT),
    )(*[pltpu.with_memory_space_constraint(t, pltpu.HBM) for t in arrays])
    return outs[:n_pairs], outs[n_pairs:2 * n_pairs], outs[2 * n_pairs:2 * n_pairs + n], outs[-1]


def _split_wait(name, send_sems, recv_sems, flying, sizes, after):
    n, n_pairs = len(flying), len(send_sems)

    def body(*refs):
        x, y, c, _ = _place()
        for k, ref in enumerate(sizes(refs[:n])):
            cp = _remote(ref, ref, refs[n + k], refs[n + n_pairs + k], (x, y, 1 - c))
            cp.wait_send()
            cp.wait_recv()

    return pl.pallas_call(
        body, name=name, out_shape=[pltpu.HBM(t.shape, t.dtype) for t in flying],
        in_specs=[HBM] * n + [SEM] * (2 * n_pairs) + [ANY], out_specs=[HBM] * n,
        input_output_aliases={a: a for a in range(n)},
        compiler_params=pltpu.CompilerParams(has_side_effects=EFFECT),
    )(*flying, *send_sems, *recv_sems, after)


def _spread_start(lands, name):
    def issue(land_refs, send_sems, recv_sems):
        x, y, c, chips = _place()
        mine = _chip_index(x, y)
        for a, land in enumerate(land_refs):
            _, rows, cols = land.shape
            hr = rows // 2
            row_bytes = cols * jnp.dtype(land.dtype).itemsize
            for r0, nr in _row_chunks(hr, row_bytes, _align_of(land.dtype)):
                piece = land.at[mine, pl.ds(c * hr + r0, nr), :]
                for chip in chips:
                    for core in (0, 1):
                        _remote(piece, piece, send_sems[a], recv_sems[a], (*chip, core)).start()

    return _split_start(name, lands, len(lands), issue)


def _spread_wait(send_sems, recv_sems, flying, after, name):
    return _split_wait(name, send_sems, recv_sems, flying, lambda refs: [r.at[pl.ds(0, 3)] for r in refs], after)


def _pair_start(grads):
    n = len(grads)
    zones = [lax.empty((4, g.shape[1] // 2, g.shape[2]), F32) for g in grads]

    def issue(refs, send_sems, recv_sems):
        x, y, c, _ = _place()
        for a in range(n):
            g_ref, z_ref = refs[a], refs[n + a]
            _, rows, cols = g_ref.shape
            hr = rows // 2
            for k in range(4):
                for r0, nr in _row_chunks(hr, cols * 4, SUBLANES):
                    _remote(g_ref.at[k, pl.ds((1 - c) * hr + r0, nr), :], z_ref.at[k, pl.ds(r0, nr), :],
                            send_sems[a], recv_sems[a], (x, y, 1 - c)).start()

    return _split_start("pair_start", list(grads) + zones, n, issue)


def _pair_wait(send_sems, recv_sems, flying, after):
    n = len(flying) // 2
    out = _split_wait("pair_wait", send_sems, recv_sems, flying, lambda refs: list(refs[n:]), after)
    return out[:n], out[n:]


def _chip_start(sums):
    n = len(sums)
    zones = [lax.empty((3, *s.shape[1:]), s.dtype) for s in sums]

    def issue(refs, send_sems, recv_sems):
        x, y, c, chips = _place()
        for a in range(n):
            s_ref, z_ref = refs[a], refs[n + a]
            _, rows, cols = s_ref.shape
            row_bytes = cols * jnp.dtype(s_ref.dtype).itemsize
            for r0, nr in _row_chunks(rows, row_bytes, _align_of(s_ref.dtype)):
                for j, chip in enumerate(chips):
                    _remote(s_ref.at[_chip_index(*chip), pl.ds(r0, nr), :], z_ref.at[j, pl.ds(r0, nr), :],
                            send_sems[a], recv_sems[a], (*chip, c)).start()

    return _split_start("chip_start", list(sums) + zones, n, issue)


def _chip_wait(send_sems, recv_sems, flying, after):
    n = len(flying) // 2
    return _split_wait("chip_wait", send_sems, recv_sems, flying, lambda refs: list(refs[n:]), after)[n:]


def _swap_start(halves):
    n = len(halves)
    zones = [lax.empty(h.shape, F32) for h in halves]

    def issue(refs, send_sems, recv_sems):
        x, y, c, _ = _place()
        for a in range(n):
            rows, cols = refs[a].shape
            for r0, nr in _row_chunks(rows, cols * 4, SUBLANES):
                _remote(refs[a].at[pl.ds(r0, nr), :], refs[n + a].at[pl.ds(r0, nr), :], send_sems[a], recv_sems[a],
                        (x, y, 1 - c)).start()

    return _split_start("swap_start", list(halves) + zones, n, issue)


def _swap_wait(send_sems, recv_sems, flying, after):
    n = len(flying) // 2
    return _split_wait("swap_wait", send_sems, recv_sems, flying, lambda refs: list(refs[n:]), after)[n:]


def _pair_exchange(grads):
    na = len(grads)

    def body(*refs):
        g_refs, o_refs = refs[:na], refs[na:2 * na]
        send_sems, recv_sems = refs[2 * na:]
        x, y, c, _ = _place()
        sibling = (x, y, 1 - c)
        for ai in range(na):
            _, rows, cols = g_refs[ai].shape
            hr = rows // 2
            for k in range(4):
                for r0, nr in _row_chunks(hr, cols * 4, SUBLANES):
                    _remote(g_refs[ai].at[k, pl.ds((1 - c) * hr + r0, nr), :], o_refs[ai].at[k, pl.ds(r0, nr), :],
                            send_sems.at[ai], recv_sems.at[ai], sibling).start()
        for ai in range(na):
            _remote(o_refs[ai], o_refs[ai], send_sems.at[ai], recv_sems.at[ai], sibling).wait()

    return pl.pallas_call(
        body, in_specs=[ANY] * na, out_specs=[ANY] * na,
        out_shape=[jax.ShapeDtypeStruct((4, g.shape[1] // 2, g.shape[2]), F32) for g in grads],
        scratch_shapes=[pltpu.SemaphoreType.DMA((na,)), pltpu.SemaphoreType.DMA((na,))],
        name="pair_exchange",
    )(*grads)


def _row_tile(rows, cols, align):
    best = align
    for cand in range(align, rows + 1, align):
        if rows % cand == 0 and cand * cols <= 256 * 1024:
            best = cand
    return best


def _pair_sum(g, got, place, transit, name):
    _, rows, cols = g.shape
    hr = rows // 2
    tr = _row_tile(hr, cols, _align_of(transit))
    nt = hr // tr

    def body(p_ref, g_ref, r_ref, s_ref, own_ref):
        total = g_ref[...] + r_ref[...]
        s_ref[...] = total.astype(transit)

        @pl.when(pl.program_id(1) == p_ref[1])
        def _():
            own_ref[...] = total

    grid_spec = pltpu.PrefetchScalarGridSpec(
        num_scalar_prefetch=1, grid=(nt, 4),
        in_specs=[pl.BlockSpec((None, tr, cols), lambda i, k, p: (k, p[0] * nt + i, 0)),
                  pl.BlockSpec((None, tr, cols), lambda i, k, p: (k, i, 0))],
        out_specs=[pl.BlockSpec((None, tr, cols), lambda i, k, p: (k, i, 0)),
                   pl.BlockSpec((tr, cols), lambda i, k, p: (i, 0))])
    return pl.pallas_call(
        body, grid_spec=grid_spec,
        out_shape=[jax.ShapeDtypeStruct((4, hr, cols), transit), jax.ShapeDtypeStruct((hr, cols), F32)],
        name=name, compiler_params=_params(("parallel", "arbitrary")),
    )(place, g, got)


def _chip_exchange(sums):
    na = len(sums)

    def body(*refs):
        s_refs, o_refs = refs[:na], refs[na:2 * na]
        send_sems, recv_sems = refs[2 * na:]
        x, y, c, chips = _place()
        for ai in range(na):
            _, rows, cols = s_refs[ai].shape
            row_bytes = cols * jnp.dtype(s_refs[ai].dtype).itemsize
            for r0, nr in _row_chunks(rows, row_bytes, _align_of(s_refs[ai].dtype)):
                for j, chip in enumerate(chips):
                    _remote(s_refs[ai].at[_chip_index(*chip), pl.ds(r0, nr), :], o_refs[ai].at[j, pl.ds(r0, nr), :],
                            send_sems.at[ai, j], recv_sems.at[ai, j], (*chip, c)).start()
        for ai in range(na):
            for j, chip in enumerate(chips):
                _remote(o_refs[ai].at[j], o_refs[ai].at[j], send_sems.at[ai, j], recv_sems.at[ai, j],
                        (*chip, c)).wait()

    return pl.pallas_call(
        body, in_specs=[ANY] * na, out_specs=[ANY] * na,
        out_shape=[jax.ShapeDtypeStruct((3, *s.shape[1:]), s.dtype) for s in sums],
        scratch_shapes=[pltpu.SemaphoreType.DMA((na, 3)), pltpu.SemaphoreType.DMA((na, 3))],
        name="chip_exchange",
    )(*sums)


def _chip_sum(own, landed, name):
    hr, cols = own.shape
    tr = _row_tile(hr, cols, _align_of(landed.dtype))

    def body(o_ref, l_ref, f_ref):
        acc = o_ref[...]
        for j in range(3):
            acc = acc + l_ref[j].astype(F32)
        f_ref[...] = acc

    return pl.pallas_call(
        body, grid=(hr // tr,),
        in_specs=[pl.BlockSpec((tr, cols), lambda i: (i, 0)), pl.BlockSpec((3, tr, cols), lambda i: (0, i, 0))],
        out_specs=pl.BlockSpec((tr, cols), lambda i: (i, 0)),
        out_shape=jax.ShapeDtypeStruct((hr, cols), F32), name=name,
        compiler_params=_params(("parallel",)),
    )(own, landed)


def _final_exchange(halves, small):
    nh = len(halves)

    def body(*refs):
        h_refs, s_ref = refs[:nh], refs[nh]
        o_refs, so_ref = refs[nh + 1:2 * nh + 1], refs[2 * nh + 1]
        send_sems, recv_sems, local_sem, ssend_sems, srecv_sems = refs[2 * nh + 2:]
        x, y, c, _ = _place()
        me = 4 * x + 2 * y + c
        sibling = (x, y, 1 - c)
        for hi in range(nh):
            hr, cols = h_refs[hi].shape
            for r0, nr in _row_chunks(hr, cols * 4, SUBLANES):
                _remote(h_refs[hi].at[pl.ds(r0, nr), :], o_refs[hi].at[pl.ds(r0, nr), :],
                        send_sems.at[hi], recv_sems.at[hi], sibling).start()
        small_cps = [pltpu.make_async_copy(s_ref, so_ref.at[me], local_sem)]
        for r in range(1, 8):
            fx, fy, fc = (r >> 2) & 1, (r >> 1) & 1, r & 1
            peer = (1 - x if fx else x, 1 - y if fy else y, 1 - c if fc else c)
            small_cps.append(_remote(s_ref, so_ref.at[me], ssend_sems.at[r - 1], srecv_sems.at[r - 1], peer))
        for cp in small_cps:
            cp.start()
        for hi in range(nh):
            _remote(h_refs[hi], o_refs[hi], send_sems.at[hi], recv_sems.at[hi], sibling).wait()
        for cp in small_cps:
            cp.wait()

    return pl.pallas_call(
        body, in_specs=[ANY] * (nh + 1), out_specs=[ANY] * (nh + 1),
        out_shape=[jax.ShapeDtypeStruct(h.shape, F32) for h in halves]
        + [jax.ShapeDtypeStruct((8, *small.shape), F32)],
        scratch_shapes=[pltpu.SemaphoreType.DMA((nh,)), pltpu.SemaphoreType.DMA((nh,)),
                        pltpu.SemaphoreType.DMA, pltpu.SemaphoreType.DMA((7,)), pltpu.SemaphoreType.DMA((7,))],
        name="final_exchange",
    )(*halves, small)


def _adamw_halves(w, own, other, m, v, place, name):
    r, c = w.shape
    hr = r // 2
    tr = _row_tile(hr, c, SUBLANES)
    nt = hr // tr
    c1 = 1.0 - ADAM_B1 ** ADAM_STEP
    c2 = 1.0 - ADAM_B2 ** ADAM_STEP

    def body(p_ref, w_ref, own_ref, other_ref, m_ref, v_ref, g_ref, d_ref, nm_ref, nv_ref):
        mine = pl.program_id(0) // nt == p_ref[0]
        gv = jnp.where(mine, own_ref[...], other_ref[...])
        nm = ADAM_B1 * m_ref[...] + (1.0 - ADAM_B1) * gv
        nv = ADAM_B2 * v_ref[...] + (1.0 - ADAM_B2) * (gv * gv)
        g_ref[...] = gv
        d_ref[...] = -ADAM_LR * ((nm / c1) / (jnp.sqrt(nv / c2) + ADAM_EPS) + ADAM_WD * w_ref[...])
        nm_ref[...] = nm
        nv_ref[...] = nv

    full = pl.BlockSpec((tr, c), lambda i, p: (i, 0))
    half = pl.BlockSpec((tr, c), lambda i, p: (i % nt, 0))
    out = jax.ShapeDtypeStruct((r, c), F32)
    grid_spec = pltpu.PrefetchScalarGridSpec(num_scalar_prefetch=1, grid=(2 * nt,),
                                             in_specs=[full, half, half, full, full], out_specs=[full] * 4)
    return pl.pallas_call(body, grid_spec=grid_spec, out_shape=[out] * 4, name=name,
                          compiler_params=_params(("parallel",)))(place, w, own, other, m, v)


def _adamw_many(ws, gs, ms, vs, name):
    n = len(ws)
    c1 = 1.0 - ADAM_B1 ** ADAM_STEP
    c2 = 1.0 - ADAM_B2 ** ADAM_STEP

    def body(*refs):
        w_refs, g_refs, m_refs, v_refs = (refs[k * n:(k + 1) * n] for k in range(4))
        d_refs, nm_refs, nv_refs = (refs[(4 + k) * n:(5 + k) * n] for k in range(3))
        for i in range(n):
            gv = g_refs[i][...]
            nm = ADAM_B1 * m_refs[i][...] + (1.0 - ADAM_B1) * gv
            nv = ADAM_B2 * v_refs[i][...] + (1.0 - ADAM_B2) * (gv * gv)
            d_refs[i][...] = -ADAM_LR * ((nm / c1) / (jnp.sqrt(nv / c2) + ADAM_EPS) + ADAM_WD * w_refs[i][...])
            nm_refs[i][...] = nm
            nv_refs[i][...] = nv

    vmem = pl.BlockSpec(memory_space=pltpu.VMEM)
    shapes = [jax.ShapeDtypeStruct(t.shape, F32) for t in ws]
    outs = pl.pallas_call(body, in_specs=[vmem] * (4 * n), out_specs=[vmem] * (3 * n), out_shape=shapes * 3,
                          name=name, compiler_params=_params(vmem_mb=56))(*ws, *gs, *ms, *vs)
    return outs[:n], outs[n:2 * n], outs[2 * n:]


BIG = ("w_in", "w_glu", "w_out", "w_up", "w_down")
WEIGHTS = ("norm_mix_g", "w_in", "a_re", "a_im", "log_step", "b_re", "b_im", "c_re", "c_im", "d_skip", "w_glu",
           "sink", "norm_attn_g", "norm_ssm_g", "w_out", "norm_ffn_g", "w_up", "conv_w", "conv_b", "w_down",
           "norm_final_g")
SMALL = ("norm_mix_g", "a_re", "a_im", "log_step", "b_re", "b_im", "c_re", "c_im", "d_skip", "sink",
         "norm_attn_g", "norm_ssm_g", "norm_ffn_g", "conv_w", "conv_b", "norm_final_g")
SMALL_ROWS = 40
N_DEV = 8


def _by_owner(name, g):
    if name == "w_up":
        return g
    return g.reshape(4, g.shape[0] // 4, g.shape[1])


def _view(name, t):
    if name == "w_in":
        return jnp.swapaxes(t[0], 0, 1)
    if name in ("b_re", "b_im"):
        return jnp.swapaxes(t, -1, -2)
    return t


def _unview(name, t):
    if name == "w_in":
        return jnp.swapaxes(t, 0, 1)[None]
    if name in ("b_re", "b_im"):
        return jnp.swapaxes(t, -1, -2)
    return t


def kernel(x, norm_mix_g, w_in, a_re, a_im, log_step, b_re, b_im, c_re, c_im, d_skip, w_glu, sink, norm_attn_g, norm_ssm_g, w_out, norm_ffn_g, w_up, conv_w, conv_b, w_down, norm_final_g, loss_target, m_norm_mix_g, m_w_in, m_a_re, m_a_im, m_log_step, m_b_re, m_b_im, m_c_re, m_c_im, m_d_skip, m_w_glu, m_sink, m_norm_attn_g, m_norm_ssm_g, m_w_out, m_norm_ffn_g, m_w_up, m_conv_w, m_conv_b, m_w_down, m_norm_final_g, v_norm_mix_g, v_w_in, v_a_re, v_a_im, v_log_step, v_b_re, v_b_im, v_c_re, v_c_im, v_d_skip, v_w_glu, v_sink, v_norm_attn_g, v_norm_ssm_g, v_w_out, v_norm_ffn_g, v_w_up, v_conv_w, v_conv_b, v_w_down, v_norm_final_g):
    given = dict(locals())
    w = {n: given[n] for n in WEIGHTS}
    m = {n: given["m_" + n] for n in WEIGHTS}
    v = {n: given["v_" + n] for n in WEIGHTS}
    xy = 2 * lax.axis_index("x") + lax.axis_index("y")

    core = lax.axis_index("c")
    place = jnp.stack([core, xy]).astype(jnp.int32)

    conv_rows = jnp.pad(w["conv_w"][0], ((0, 2 * SUBLANES - 3), (0, 0)))
    rows = lambda t: t.reshape(4 * t.shape[1], t.shape[2])
    (w_in_all,) = _gather_weights([_view("w_in", w["w_in"])], [BF16])
    wb = {"w_in": rows(w_in_all)}
    early = ("w_in", "w_glu", "w_out")
    mixer = [_cast_place(w[n][0], place, BF16, w_in_all, "cast_" + n) for n in ("w_glu", "w_out")]
    mixer.append(_cast_place(conv_rows, place, F32, w_in_all, "cast_conv_w"))
    *mixer_flight, mixer_token = _spread_start(mixer, "spread_mixer_start")
    late = ("w_up", "w_down")
    *late_flight, token = _spread_start(
        [_cast_place(w[n][0], place, BF16, mixer_token, "cast_" + n) for n in late], "spread_ffn_start")

    def mixer_weights(after):
        w_glu4, w_out4, conv4 = _spread_wait(*mixer_flight, after, "spread_mixer_wait")
        return {"w_glu": rows(w_glu4), "w_out": rows(w_out4),
                "conv_w": conv4[:, :3].transpose(1, 0, 2).reshape(3, 2 * D_FF)}

    def late_weights(after):
        w_up4, w_down4 = _spread_wait(*late_flight, after, "spread_ffn_wait")
        return {"w_up": w_up4, "w_down": rows(w_down4)}

    sp = {n: w[n][0] for n in ("a_re", "a_im", "log_step", "b_re", "b_im", "c_re", "c_im", "d_skip",
                               "norm_mix_g", "norm_attn_g", "norm_ssm_g", "norm_ffn_g", "sink", "conv_b")}
    for n in ("norm_mix_g", "norm_attn_g", "norm_ssm_g", "norm_ffn_g", "sink", "conv_b"):
        sp[n] = sp[n].reshape(1, -1)
    sp["norm_mix_g"] = sp["norm_mix_g"] + token[:1, :1]
    sp["norm_final_g"] = w["norm_final_g"]
    flight = {}

    def ffn_grads_ready(dw_up, dw_down):
        *flight["pair"], token = _pair_start([dw_up, _by_owner("w_down", dw_down)])
        return token[:1, :1]

    def ffn_grads_next(after):
        mine, got = _pair_wait(*flight["pair"], after)
        sums, flight["own"] = zip(*[_pair_sum(a, b, place, BF16, "pair_sum_" + n) for n, a, b in zip(late, mine, got)])
        *flight["chip"], token = _chip_start(list(sums))
        return token[:1, :1]

    halves = {}

    def ffn_grads_last(after):
        for n, o, t in zip(late, flight["own"], _chip_wait(*flight["chip"], after)):
            halves[n] = _chip_sum(o, t, "chip_sum_" + n)
        *flight["swap"], token = _swap_start([halves[n] for n in late])
        return token[:1, :1]

    loss, grad_x, g = _local_step(x[0], loss_target[0], wb, sp, mixer_weights, late_weights, ffn_grads_ready,
                                  ffn_grads_next, ffn_grads_last)

    flat = jnp.concatenate([g[n].reshape(-1) for n in SMALL] + [loss.reshape(-1)])
    pad = N_DEV * SMALL_ROWS * D_MODEL - flat.shape[0]
    small = jnp.concatenate([flat, jnp.zeros((pad,), F32)]).reshape(4, 2 * SMALL_ROWS, D_MODEL)
    by_owner = [_by_owner(n, g[n]) for n in early] + [small]
    got = _pair_exchange(by_owner)
    transit = [BF16] * len(early) + [F32]
    chip_sums, own_sums = zip(*[_pair_sum(a, b, place, t, "pair_sum_" + n)
                                for n, a, b, t in zip(early + ("small",), by_owner, got, transit)])
    landed = _chip_exchange(list(chip_sums))
    for n, o, t in zip(early + ("small",), own_sums, landed):
        halves[n] = _chip_sum(o, t, "chip_sum_" + n)
    *others, small_all = _final_exchange([halves[n] for n in early], halves["small"])
    others += _swap_wait(*flight["swap"], grad_x)
    flat = small_all.reshape(-1)
    grads, off = {}, 0
    for n in SMALL:
        shape = (3, 4 * w[n].shape[-1]) if n == "conv_w" else w[n].shape[1:] if n != "norm_final_g" else w[n].shape
        size = math.prod(shape)
        grads[n] = flat[off:off + size].reshape(shape)
        off += size
    loss = flat[off]
    cw = w["conv_w"].shape[-1]
    grads["conv_w"] = lax.dynamic_slice_in_dim(grads["conv_w"], xy * cw, cw, axis=1)
    grads = {n: _view(n, grads[n].reshape(w[n].shape)) for n in SMALL}
    wv, mv, vv = ({n: _view(n, t[n]) for n in WEIGHTS} for t in (w, m, v))

    delta, new_m, new_v = {}, {}, {}
    for n, other in zip(BIG, others):
        two_d = lambda t: t.reshape(t.shape[-2:])
        grads[n], delta[n], new_m[n], new_v[n] = _adamw_halves(
            two_d(wv[n]), halves[n], other, two_d(mv[n]), two_d(vv[n]), place, "adamw_" + n)
    for group, name in ((("b_re", "b_im"), "adamw_b"), (tuple(n for n in SMALL if n not in ("b_re", "b_im")), "adamw_small")):
        row = lambda t: t.reshape(1, -1) if t.ndim == 1 else t
        d_, m_, v_ = _adamw_many(*[[row(t[n]) for n in group] for t in (wv, grads, mv, vv)], name)
        for n, dn, mn, vn in zip(group, d_, m_, v_):
            delta[n], new_m[n], new_v[n] = (t.reshape(wv[n].shape) for t in (dn, mn, vn))
    natural = lambda t: [_unview(n, t[n].reshape(wv[n].shape)) for n in WEIGHTS]
    return (loss, grad_x[None], *natural(grads), *natural(delta), *natural(new_m), *natural(new_v))
```

```python
import functools
import math

import jax
import jax.numpy as jnp
import numpy as np
from jax import lax
from jax.experimental import pallas as pl
from jax.experimental.pallas import tpu as pltpu

F32 = jnp.float32
BF16 = jnp.bfloat16

D_MODEL = 1024
N_Q_HEADS = 8
N_KV_HEADS = 2
HEAD_DIM = 64
ATTN_WIDTH = 512
KV_WIDTH = 128
QKV_WIDTH = ATTN_WIDTH + 2 * KV_WIDTH
WINDOW = 128
BLOCK = 128
ROPE_DIM = 16
ROPE_THETA = 500000.0
SSM_WIDTH = 512
SSM_GROUP = 16
N_SSM_GROUPS = 32
SSM_STATE = 64
IN_WIDTH = 1280
D_FF = 2816
EPS = 1e-6
ADAM_LR = 0.001
ADAM_B1 = 0.9
ADAM_B2 = 0.999
ADAM_EPS = 1e-08
ADAM_WD = 0.01
ADAM_STEP = 10

VMEM_BYTES_V7X = 64 * 1024 * 1024
SUBLANES = 8
LANES = 128
SSM_CB = 4
SSM_CH = 128
SSM_ST = 512
N_SEG = SUBLANES

NN = (((1,), (0,)), ((), ()))
NT = (((1,), (1,)), ((), ()))
TN = (((0,), (0,)), ((), ()))


def _params(sem=None, vmem_mb=48):
    limit = vmem_mb * 1024 * 1024
    assert limit < VMEM_BYTES_V7X
    return pltpu.CompilerParams(dimension_semantics=sem, vmem_limit_bytes=limit)


def _dg(a, b, dims):
    return lax.dot_general(a, b, dims, preferred_element_type=F32)


def _sigmoid(x):
    return 1.0 / (1.0 + jnp.exp(-x))


_SQRT_HALF = 0.7071067811865476
_INV_SQRT_2PI = 0.3989422804014327


def _gelu(x):
    return 0.5 * x * (1.0 + lax.erf(x * _SQRT_HALF))


def _gelu_grad(x):
    return 0.5 * (1.0 + lax.erf(x * _SQRT_HALF)) + x * (_INV_SQRT_2PI * jnp.exp(-0.5 * x * x))


def _mm_tn(a, b, tm, tn, name):
    k, m = a.shape
    n = b.shape[1]

    def body(a_ref, b_ref, o_ref):
        o_ref[...] = _dg(a_ref[...], b_ref[...], TN)

    return pl.pallas_call(
        body, grid=(m // tm, n // tn),
        in_specs=[pl.BlockSpec((k, tm), lambda i, j: (0, i)), pl.BlockSpec((k, tn), lambda i, j: (0, j))],
        out_specs=pl.BlockSpec((tm, tn), lambda i, j: (i, j)),
        out_shape=jax.ShapeDtypeStruct((m, n), F32), name=name,
        compiler_params=_params(("parallel", "parallel")),
    )(a, b)


def _mm_nn_cols(a, b4, tm, name):
    m, k = a.shape
    s, _, n = b4.shape

    def body(a_ref, b_ref, o_ref):
        o_ref[...] = _dg(a_ref[...], b_ref[...], NN)

    return pl.pallas_call(
        body, grid=(m // tm, s),
        in_specs=[pl.BlockSpec((tm, k), lambda i, j: (i, 0)), pl.BlockSpec((None, k, n), lambda i, j: (j, 0, 0))],
        out_specs=pl.BlockSpec((tm, n), lambda i, j: (i, j)),
        out_shape=jax.ShapeDtypeStruct((m, s * n), F32), name=name,
        compiler_params=_params(("parallel", "parallel")),
    )(a, b4)


def _mm_tn_cols(a, b2, s, tm, name):
    k, m = a.shape
    h, _, wide = b2.shape
    per = s // h
    n = wide // per

    def body(a_ref, b_ref, o_ref):
        o_ref[...] = _dg(a_ref[...], b_ref[...], TN)

    return pl.pallas_call(
        body, grid=(s, m // tm),
        in_specs=[pl.BlockSpec((k, tm), lambda j, i: (0, i)),
                  pl.BlockSpec((None, k, n), lambda j, i: (j // per, 0, j % per))],
        out_specs=pl.BlockSpec((None, tm, n), lambda j, i: (j, i, 0)),
        out_shape=jax.ShapeDtypeStruct((s, m, n), F32), name=name,
        compiler_params=_params(("parallel", "parallel")),
    )(a, b2)


TM_EW = 256


def _rms_bwd_vals(xv, gv, dy):
    r = lax.rsqrt(jnp.mean(xv * xv, axis=-1, keepdims=True) + EPS)
    xh = xv * r
    dxh = dy * gv
    dx = r * (dxh - xh * jnp.mean(dxh * xh, axis=-1, keepdims=True))
    return dx, dy * xh


TM_FUSED = 256


def _rms_vals(xv, gv):
    return xv * lax.rsqrt(jnp.mean(xv * xv, axis=-1, keepdims=True) + EPS) * gv


def _rope_blocks(src, dst, c, lo, hi):
    nq = ATTN_WIDTH // LANES
    for blk in range(nq + 1):
        t = src[:, blk * LANES:(blk + 1) * LANES]
        dst[:, blk * LANES:(blk + 1) * LANES] = (
            t * c + pltpu.roll(t, LANES - 8, 1) * lo + pltpu.roll(t, 8, 1) * hi).astype(BF16)
    dst[:, (nq + 1) * LANES:] = src[:, (nq + 1) * LANES:].astype(BF16)


def _rms_mm_rope(x, g, wt, tabs, name):
    l, d = x.shape
    n = wt.shape[0]

    def body(x_ref, g_ref, w_ref, c_ref, lo_ref, hi_ref, h_ref, qkv_ref, u_ref):
        h = _rms_vals(x_ref[...], g_ref[...]).astype(BF16)
        h_ref[...] = h
        out = _dg(h, w_ref[...], NT)
        _rope_blocks(out[:, :QKV_WIDTH], qkv_ref, c_ref[...], lo_ref[...], hi_ref[...])
        u_ref[...] = out[:, QKV_WIDTH:]

    row = lambda width: pl.BlockSpec((TM_FUSED, width), lambda i: (i, 0))
    return pl.pallas_call(
        body, grid=(l // TM_FUSED,),
        in_specs=[row(d), pl.BlockSpec((1, d), lambda i: (0, 0)), pl.BlockSpec((n, d), lambda i: (0, 0)),
                  row(LANES), row(LANES), row(LANES)],
        out_specs=[row(d), row(QKV_WIDTH), row(n - QKV_WIDTH)],
        out_shape=[jax.ShapeDtypeStruct((l, d), BF16), jax.ShapeDtypeStruct((l, QKV_WIDTH), BF16),
                   jax.ShapeDtypeStruct((l, n - QKV_WIDTH), F32)],
        name=name, compiler_params=_params(("parallel",)),
    )(x, g, wt, *tabs)


def _mix_mm_res_rms(attn, ys, g_attn, g_ssm, b, res, g, name):
    l, w = attn.shape
    d = b.shape[1]

    def body(a_ref, y_ref, ga_ref, gs_ref, b_ref, r_ref, g_ref, m_ref, x_ref, h_ref):
        m_ref[:, :w] = _rms_vals(a_ref[...], ga_ref[...]).astype(BF16)
        m_ref[:, w:] = _rms_vals(y_ref[...], gs_ref[...]).astype(BF16)
        xv = r_ref[...] + _dg(m_ref[...], b_ref[...], NN)
        x_ref[...] = xv
        h_ref[...] = _rms_vals(xv, g_ref[...]).astype(BF16)

    row = lambda width: pl.BlockSpec((TM_FUSED, width), lambda i: (i, 0))
    vec = lambda width: pl.BlockSpec((1, width), lambda i: (0, 0))
    return pl.pallas_call(
        body, grid=(l // TM_FUSED,),
        in_specs=[row(w), row(w), vec(w), vec(w), pl.BlockSpec((2 * w, d), lambda i: (0, 0)), row(d), vec(d)],
        out_specs=[row(2 * w), row(d), row(d)],
        out_shape=[jax.ShapeDtypeStruct((l, 2 * w), BF16), jax.ShapeDtypeStruct((l, d), F32),
                   jax.ShapeDtypeStruct((l, d), BF16)],
        name=name, compiler_params=_params(("parallel",)),
    )(attn, ys, g_attn, g_ssm, b, res, g)


def _mm_res_loss(a, b, res, g, target):
    l, k = a.shape
    d = b.shape[1]

    def body(a_ref, b_ref, r_ref, g_ref, t_ref, loss_ref, dx_ref, dxb_ref, dg_ref):
        xv = r_ref[...] + _dg(a_ref[...], b_ref[...], NN)
        gv = g_ref[...]
        r = lax.rsqrt(jnp.mean(xv * xv, axis=-1, keepdims=True) + EPS)
        xh = xv * r
        e = xh * gv - t_ref[...]
        part = jnp.sum(jnp.sum(e * e, axis=1, keepdims=True), axis=0, keepdims=True) * (0.5 / d)
        dy = e * (1.0 / d)
        dxh = dy * gv
        dx = r * (dxh - xh * jnp.mean(dxh * xh, axis=-1, keepdims=True))
        dx_ref[...] = dx
        dxb_ref[...] = dx.astype(BF16)

        @pl.when(pl.program_id(0) == 0)
        def _():
            dg_ref[...] = jnp.zeros_like(dg_ref)
            loss_ref[...] = jnp.zeros_like(loss_ref)

        dg_ref[...] += jnp.sum(dy * xh, axis=0, keepdims=True)
        loss_ref[...] += part

    row = lambda width: pl.BlockSpec((TM_FUSED, width), lambda i: (i, 0))
    vec = pl.BlockSpec((1, d), lambda i: (0, 0))
    return pl.pallas_call(
        body, grid=(l // TM_FUSED,),
        in_specs=[row(k), pl.BlockSpec((k, d), lambda i: (0, 0)), row(d), vec, row(d)],
        out_specs=[pl.BlockSpec((1, 1), lambda i: (0, 0)), row(d), row(d), vec],
        out_shape=[jax.ShapeDtypeStruct((1, 1), F32), jax.ShapeDtypeStruct((l, d), F32),
                   jax.ShapeDtypeStruct((l, d), BF16), jax.ShapeDtypeStruct((1, d), F32)],
        name="mm_down_loss", compiler_params=_params(("arbitrary",)),
    )(a, b, res, g, target)


def _mm_rms_bwd(a, b, a_spec, b_spec, matmul, x, g, res, name):
    l, d = x.shape

    def body(a_ref, b_ref, x_ref, g_ref, res_ref, dx_ref, dxb_ref, dg_ref):
        dx, dgr = _rms_bwd_vals(x_ref[...], g_ref[...], matmul(a_ref, b_ref))
        dx = dx + res_ref[...]
        dx_ref[...] = dx
        dxb_ref[...] = dx.astype(BF16)

        @pl.when(pl.program_id(0) == 0)
        def _():
            dg_ref[...] = jnp.zeros_like(dg_ref)

        dg_ref[...] += jnp.sum(dgr, axis=0, keepdims=True)

    row = pl.BlockSpec((TM_FUSED, d), lambda i: (i, 0))
    vec = pl.BlockSpec((1, d), lambda i: (0, 0))
    return pl.pallas_call(
        body, grid=(l // TM_FUSED,), in_specs=[a_spec, b_spec, row, vec, row], out_specs=[row, row, vec],
        out_shape=[jax.ShapeDtypeStruct((l, d), F32), jax.ShapeDtypeStruct((l, d), BF16),
                   jax.ShapeDtypeStruct((1, d), F32)],
        name=name, compiler_params=_params(("arbitrary",)),
    )(a, b, x, g, res)


def _mm_nn_rms_bwd(a, b, x, g, res, name):
    return _mm_rms_bwd(a, b, pl.BlockSpec((TM_FUSED, a.shape[1]), lambda i: (i, 0)),
                       pl.BlockSpec(b.shape, lambda i: (0, 0)),
                       lambda a_ref, b_ref: _dg(a_ref[...], b_ref[...], NN), x, g, res, name)


def _mm_cols_rms_bwd(a2, b4, x, g, res, name):
    h, _, wide = a2.shape
    s, _, n = b4.shape
    per = s // h

    def matmul(a_ref, b_ref):
        acc = None
        for j in range(s):
            part = _dg(a_ref[j // per, :, (j % per) * n:(j % per + 1) * n], b_ref[j], NT)
            acc = part if acc is None else acc + part
        return acc

    return _mm_rms_bwd(a2, b4, pl.BlockSpec((h, TM_FUSED, wide), lambda i: (0, i, 0)),
                       pl.BlockSpec(b4.shape, lambda i: (0, 0, 0)), matmul, x, g, res, name)


def _mm_mix_bwd(dx, b, attn, ys, g_attn, g_ssm, name):
    l, w = attn.shape
    d = dx.shape[1]

    def body(dx_ref, b_ref, a_ref, y_ref, ga_ref, gs_ref, da_ref, dy_ref, dga_ref, dgs_ref):
        @pl.when(pl.program_id(0) == 0)
        def _():
            dga_ref[...] = jnp.zeros_like(dga_ref)
            dgs_ref[...] = jnp.zeros_like(dgs_ref)

        dm = _dg(dx_ref[...], b_ref[...], NT)
        for src, gr, off, dst, dgr in ((a_ref, ga_ref, 0, da_ref, dga_ref), (y_ref, gs_ref, w, dy_ref, dgs_ref)):
            dxv, dg_rows = _rms_bwd_vals(src[...], gr[...], dm[:, off:off + w])
            dst[...] = dxv
            dgr[...] += jnp.sum(dg_rows, axis=0, keepdims=True)

    row = lambda width: pl.BlockSpec((TM_FUSED, width), lambda i: (i, 0))
    vec = pl.BlockSpec((1, w), lambda i: (0, 0))
    return pl.pallas_call(
        body, grid=(l // TM_FUSED,),
        in_specs=[row(d), pl.BlockSpec((2 * w, d), lambda i: (0, 0)), row(w), row(w), vec, vec],
        out_specs=[row(w), row(w), vec, vec],
        out_shape=[jax.ShapeDtypeStruct((l, w), F32), jax.ShapeDtypeStruct((l, w), F32),
                   jax.ShapeDtypeStruct((1, w), F32), jax.ShapeDtypeStruct((1, w), F32)],
        name=name, compiler_params=_params(("arbitrary",)),
    )(dx, b, attn, ys, g_attn, g_ssm)


def _rope_tables(l):
    half = ROPE_DIM // 2
    f32 = np.float32
    inv_freq = np.power(f32(ROPE_THETA), -np.arange(half, dtype=f32) / f32(half))
    ang = np.arange(l, dtype=f32)[:, None] * inv_freq[None, :]
    cos, sin = np.cos(ang), np.sin(ang)
    ones = np.ones((l, HEAD_DIM - ROPE_DIM), f32)
    zeros = np.zeros((l, HEAD_DIM - ROPE_DIM), f32)
    zh = np.zeros((l, half), f32)
    c = np.concatenate([cos, cos, ones], axis=1)
    s_lo = np.concatenate([-sin, zh, zeros], axis=1)
    s_hi = np.concatenate([zh, sin, zeros], axis=1)
    return tuple(jnp.asarray(np.tile(t, (1, LANES // HEAD_DIM)), F32) for t in (c, s_lo, s_hi))


def _rope_bwd(dq, dkv, du_ssm, dpre, d_skip, tabs):
    l = dq.shape[0]
    nq = ATTN_WIDTH // LANES

    def body(dq_ref, dkv_ref, du_ref, dpre_ref, ds_ref, c_ref, lo_ref, hi_ref, o_ref):
        c, lo, hi = c_ref[...], lo_ref[...], hi_ref[...]
        for blk in range(nq + 1):
            t = dq_ref[:, blk * LANES:(blk + 1) * LANES] if blk < nq else dkv_ref[:, :KV_WIDTH]
            g = t * c + pltpu.roll(t * lo, 8, 1) + pltpu.roll(t * hi, LANES - 8, 1)
            o_ref[:, blk * LANES:(blk + 1) * LANES] = g.astype(BF16)
        o_ref[:, (nq + 1) * LANES:QKV_WIDTH] = dkv_ref[:, KV_WIDTH:].astype(BF16)
        o_ref[:, QKV_WIDTH:] = (du_ref[...] + dpre_ref[...] * ds_ref[...]).astype(BF16)

    tab = pl.BlockSpec((TM_EW, LANES), lambda i: (i, 0))
    wide = pl.BlockSpec((TM_EW, SSM_WIDTH), lambda i: (i, 0))
    return pl.pallas_call(
        body, grid=(l // TM_EW,),
        in_specs=[wide, pl.BlockSpec((TM_EW, 2 * KV_WIDTH), lambda i: (i, 0)), wide, wide,
                  pl.BlockSpec((1, SSM_WIDTH), lambda i: (0, 0)), tab, tab, tab],
        out_specs=pl.BlockSpec((TM_EW, IN_WIDTH), lambda i: (i, 0)),
        out_shape=jax.ShapeDtypeStruct((l, IN_WIDTH), BF16), name="rope_bwd",
        compiler_params=_params(("parallel",)),
    )(dq, dkv, du_ssm, dpre, d_skip, *tabs)


_Q_COLS = ATTN_WIDTH // LANES
_SCALE = HEAD_DIM ** -0.5
_NEG = -1e30


def _window_specs(nb, width, col):
    return [
        pl.BlockSpec((BLOCK, width), lambda n: (jnp.maximum(n - 1, 0), col)),
        pl.BlockSpec((BLOCK, width), lambda n: (n, col)),
        pl.BlockSpec((BLOCK, width), lambda n: (jnp.minimum(n + 1, nb - 1), col)),
    ]


def _stacked_sink(sink_ref, heads):
    rid = lax.broadcasted_iota(jnp.int32, (len(heads) * BLOCK, 1), 0)
    sk = jnp.full(rid.shape, sink_ref[0, heads[-1]], F32)
    for g in range(len(heads) - 2, -1, -1):
        sk = jnp.where(rid < (g + 1) * BLOCK, sink_ref[0, heads[g]], sk)
    return sk


def _attn_fwd(qkv, sink):
    l = qkv.shape[0]
    nb = l // BLOCK
    grp = N_Q_HEADS // N_KV_HEADS

    def body(sink_ref, q_ref, k0, k1, k2, v0, v1, v2, o_ref, lse_ref):
        n = pl.program_id(0)
        q = q_ref[...]
        kw = jnp.concatenate([k0[...], k1[...], k2[...]], axis=0)
        vw = jnp.concatenate([v0[...], v1[...], v2[...]], axis=0)
        row = lax.broadcasted_iota(jnp.int32, (grp * BLOCK, 3 * BLOCK), 0)
        col = lax.broadcasted_iota(jnp.int32, (grp * BLOCK, 3 * BLOCK), 1)
        valid = jnp.abs(col - BLOCK - (row & (BLOCK - 1))) <= WINDOW
        valid &= jnp.logical_not((n == 0) & (col < BLOCK))
        valid &= jnp.logical_not((n == nb - 1) & (col >= 2 * BLOCK))
        for hk in range(N_KV_HEADS):
            heads = range(hk * grp, (hk + 1) * grp)
            qs = jnp.concatenate([q[:, h * HEAD_DIM:(h + 1) * HEAD_DIM] for h in heads], axis=0)
            kh = kw[:, hk * HEAD_DIM:(hk + 1) * HEAD_DIM]
            vh = vw[:, hk * HEAD_DIM:(hk + 1) * HEAD_DIM]
            s = jnp.where(valid, _dg(qs, kh, NT) * _SCALE, _NEG)
            sk = _stacked_sink(sink_ref, heads)
            m = jnp.maximum(jnp.max(s, axis=1, keepdims=True), sk)
            p = jnp.exp(s - m)
            denom = jnp.sum(p, axis=1, keepdims=True) + jnp.exp(sk - m)
            o = _dg((p / denom).astype(BF16), vh, NN)
            lse = m + jnp.log(denom)
            for g, h in enumerate(heads):
                o_ref[:, h * HEAD_DIM:(h + 1) * HEAD_DIM] = o[g * BLOCK:(g + 1) * BLOCK]
                lse_ref[:, h:h + 1] = lse[g * BLOCK:(g + 1) * BLOCK]

    return pl.pallas_call(
        body, grid=(nb,),
        in_specs=[pl.BlockSpec(memory_space=pltpu.SMEM),
                  pl.BlockSpec((BLOCK, ATTN_WIDTH), lambda n: (n, 0))]
        + _window_specs(nb, KV_WIDTH, _Q_COLS) + _window_specs(nb, KV_WIDTH, _Q_COLS + 1),
        out_specs=[pl.BlockSpec((BLOCK, ATTN_WIDTH), lambda n: (n, 0)),
                   pl.BlockSpec((BLOCK, N_Q_HEADS), lambda n: (n, 0))],
        out_shape=[jax.ShapeDtypeStruct((l, ATTN_WIDTH), F32), jax.ShapeDtypeStruct((l, N_Q_HEADS), F32)],
        name="attn_fwd", compiler_params=_params(("parallel",)),
    )(sink, qkv, qkv, qkv, qkv, qkv, qkv, qkv)


def _attn_bwd(qkv, attn, dattn, lse, sink):
    l = qkv.shape[0]
    nb = l // BLOCK
    grp = N_Q_HEADS // N_KV_HEADS
    win = 3 * BLOCK

    def body(sink_ref, q_ref, k0, k1, k2, v0, v1, v2, o_ref, d_ref, l_ref, dq_ref, dkv_ref, dsink_ref, ring_ref):
        n = pl.program_id(0)

        @pl.when(n == 0)
        def _():
            dsink_ref[...] = jnp.zeros_like(dsink_ref)
            ring_ref[...] = jnp.zeros_like(ring_ref)

        @pl.when(n < nb)
        def _():
            first, last = n == 0, n == nb - 1
            cat = lambda a, b, c: jnp.concatenate([a[...], b[...], c[...]], axis=0)
            q, kw, vw = q_ref[...], cat(k0, k1, k2), cat(v0, v1, v2)
            dov = d_ref[...]
            prod = o_ref[...] * dov
            dob = dov.astype(BF16)
            lse = l_ref[...]
            row = lax.broadcasted_iota(jnp.int32, (grp * BLOCK, win), 0)
            col = lax.broadcasted_iota(jnp.int32, (grp * BLOCK, win), 1)
            valid = jnp.abs(col - BLOCK - (row & (BLOCK - 1))) <= WINDOW
            valid &= jnp.logical_not(first & (col < BLOCK))
            valid &= jnp.logical_not(last & (col >= 2 * BLOCK))

            dsink_parts, dks, dvs = [], [], []
            for hk in range(N_KV_HEADS):
                heads = range(hk * grp, (hk + 1) * grp)
                ksl = slice(hk * HEAD_DIM, (hk + 1) * HEAD_DIM)
                hsl = [slice(h * HEAD_DIM, (h + 1) * HEAD_DIM) for h in heads]
                stack = lambda parts: jnp.concatenate(parts, axis=0)
                qs = stack([q[:, s_] for s_ in hsl])
                dos = stack([dob[:, s_] for s_ in hsl])
                deltas = stack([jnp.sum(prod[:, s_], axis=1, keepdims=True) for s_ in hsl])
                lses = stack([lse[:, h:h + 1] for h in heads])
                kh, vh = kw[:, ksl], vw[:, ksl]
                s = jnp.where(valid, _dg(qs, kh, NT) * _SCALE, _NEG)
                p = jnp.exp(s - lses)
                dp = _dg(dos, vh, NT)
                ds = (p * (dp - deltas) * _SCALE).astype(BF16)
                dq = _dg(ds, kh, NN)
                sink_rows = jnp.exp(_stacked_sink(sink_ref, heads) - lses) * deltas
                for g in range(grp):
                    dq_ref[:, hsl[g]] = dq[g * BLOCK:(g + 1) * BLOCK]
                    dsink_parts.append(jnp.sum(sink_rows[g * BLOCK:(g + 1) * BLOCK], axis=0, keepdims=True))
                dks.append(_dg(ds, qs, TN))
                dvs.append(_dg(p.astype(BF16), dos, TN))
            dsink_ref[...] -= jnp.concatenate(dsink_parts, axis=1)
            part = jnp.concatenate(dks + dvs, axis=1)
            ring_ref[(n + 2) % 3] += part[0:BLOCK]
            ring_ref[n % 3] += part[BLOCK:2 * BLOCK]
            ring_ref[(n + 1) % 3] = part[2 * BLOCK:]

        @pl.when(n >= 1)
        def _():
            dkv_ref[...] = ring_ref[(n + 2) % 3]

    centre = lambda n: jnp.minimum(n, nb - 1)
    window = lambda width, col: [
        pl.BlockSpec((BLOCK, width), lambda n: (jnp.maximum(centre(n) - 1, 0), col)),
        pl.BlockSpec((BLOCK, width), lambda n: (centre(n), col)),
        pl.BlockSpec((BLOCK, width), lambda n: (jnp.minimum(centre(n) + 1, nb - 1), col))]
    own = lambda width: pl.BlockSpec((BLOCK, width), lambda n: (centre(n), 0))
    return pl.pallas_call(
        body, grid=(nb + 1,),
        in_specs=[pl.BlockSpec(memory_space=pltpu.SMEM), own(ATTN_WIDTH)]
        + window(KV_WIDTH, _Q_COLS) + window(KV_WIDTH, _Q_COLS + 1)
        + [own(ATTN_WIDTH), own(ATTN_WIDTH), own(N_Q_HEADS)],
        out_specs=[own(ATTN_WIDTH), pl.BlockSpec((BLOCK, 2 * KV_WIDTH), lambda n: (jnp.maximum(n - 1, 0), 0)),
                   pl.BlockSpec((1, N_Q_HEADS), lambda n: (0, 0))],
        out_shape=[jax.ShapeDtypeStruct((l, ATTN_WIDTH), F32), jax.ShapeDtypeStruct((l, 2 * KV_WIDTH), F32),
                   jax.ShapeDtypeStruct((1, N_Q_HEADS), F32)],
        scratch_shapes=[pltpu.VMEM((3, BLOCK, 2 * KV_WIDTH), F32)],
        name="attn_bwd", compiler_params=_params(("arbitrary",)),
    )(sink, qkv, qkv, qkv, qkv, qkv, qkv, qkv, attn, dattn, lse)


def _ssm_disc(a_re, a_im, log_step, b_re, b_im):
    step = jnp.exp(log_step)[..., None]
    mag = jnp.exp(a_re * step)
    lb_re, lb_im = mag * jnp.cos(a_im * step), mag * jnp.sin(a_im * step)
    nr, ni = lb_re - 1.0, lb_im
    den = a_re * a_re + a_im * a_im
    f_re = ((nr * a_re + ni * a_im) / den)[..., None]
    f_im = ((ni * a_re - nr * a_im) / den)[..., None]
    return lb_re, lb_im, f_re * b_re - f_im * b_im, f_re * b_im + f_im * b_re


def _ssm_pack(lb_re, lb_im, bb_re, bb_im, c_re, c_im):
    eye = jnp.eye(SSM_CH // SSM_GROUP, dtype=F32)
    ng = SSM_CH // SSM_GROUP

    def diag_b(bb):
        t = bb.reshape(2, SSM_CB, ng, SSM_STATE, SSM_GROUP)
        return jnp.einsum('dkgpc,gh->dkgchp', t, eye).reshape(2, SSM_CB, SSM_CH, SSM_ST)

    def diag_c(cc):
        t = cc.reshape(2, SSM_CB, ng, SSM_GROUP, SSM_STATE)
        return jnp.einsum('dkgcp,gh->dkhpgc', t, eye).reshape(2, SSM_CB, SSM_ST, SSM_CH)

    bcat = jnp.concatenate([diag_b(bb_re), diag_b(bb_im)], axis=-1)
    ccat = jnp.concatenate([diag_c(c_re), -diag_c(c_im)], axis=-2)
    lam_re = lb_re.reshape(2, SSM_CB, 1, SSM_ST)
    lam_im = lb_im.reshape(2, SSM_CB, 1, SSM_ST)
    return bcat, ccat, lam_re, lam_im


def _ssm_unpack(dbcat, dccat, dlam_re, dlam_im):
    ng = SSM_CH // SSM_GROUP
    eye = jnp.eye(ng, dtype=F32)

    def undiag_b(t):
        t = t.reshape(2, SSM_CB, ng, SSM_GROUP, ng, SSM_STATE)
        return jnp.einsum('dkgchp,gh->dkgpc', t, eye).reshape(2, N_SSM_GROUPS, SSM_STATE, SSM_GROUP)

    def undiag_c(t):
        t = t.reshape(2, SSM_CB, ng, SSM_STATE, ng, SSM_GROUP)
        return jnp.einsum('dkhpgc,gh->dkgcp', t, eye).reshape(2, N_SSM_GROUPS, SSM_GROUP, SSM_STATE)

    dbb_re, dbb_im = undiag_b(dbcat[..., :SSM_ST]), undiag_b(dbcat[..., SSM_ST:])
    dc_re, dc_im = undiag_c(dccat[:, :, :SSM_ST]), -undiag_c(dccat[:, :, SSM_ST:])
    shape = (2, N_SSM_GROUPS, SSM_STATE)
    return dlam_re.reshape(shape), dlam_im.reshape(shape), dbb_re, dbb_im, dc_re, dc_im


def _to_segments(t):
    l, w = t.shape
    return t.reshape(N_SEG, l // N_SEG, w).transpose(1, 0, 2).reshape(l, w)


def _from_segments(t):
    l, w = t.shape
    return t.reshape(l // N_SEG, N_SEG, w).transpose(1, 0, 2).reshape(l, w)


SSM_RC = 256
SSM_JC = SSM_RC // N_SEG
_RE, _IM = pl.ds(0, SSM_ST), pl.ds(SSM_ST, SSM_ST)


def _cfma(ar, ai, xr, xi, br, bi):
    return ar * xr - ai * xi + br, ar * xi + ai * xr + bi


def _chunk_rows(ci, rev, nc):
    start = jnp.where(rev, (nc - 1 - ci) * SSM_RC, ci * SSM_RC)
    return pl.ds(pl.multiple_of(start, SSM_RC), SSM_RC)


def _scan_chunk(src, dst, ar, ai, rev, nj, ci, carry, prev_ref=None):
    def rows_of(staged, j, k):
        at = jnp.where(rev, SSM_JC - 1 - k, k) if staged else j
        return pl.ds(pl.multiple_of(at * N_SEG, N_SEG), N_SEG)

    for k in range(SSM_JC):
        jj = ci * SSM_JC + k
        j = jnp.where(rev, nj - 1 - jj, jj)
        rows = rows_of(src[1], j, k)
        nr, ni = _cfma(ar, ai, carry[0], carry[1], src[0][rows, _RE], src[0][rows, _IM])
        if dst is not None:
            rows = rows_of(dst[1], j, k)
            dst[0][rows, _RE] = nr
            dst[0][rows, _IM] = ni
        if prev_ref is None:
            carry = (nr, ni)
            continue
        jp = jnp.where(rev, j - 1, j + 1)
        if k == SSM_JC - 1:
            inside = jnp.where((jp >= 0) & (jp < nj), 1.0, 0.0)
            jp = jnp.clip(jp, 0, nj - 1)
        prow = pl.ds(pl.multiple_of(jp * N_SEG, N_SEG), N_SEG)
        xr, xi = prev_ref[prow, _RE], prev_ref[prow, _IM]
        sr, si = nr * xr + ni * xi, ni * xr - nr * xi
        if k == SSM_JC - 1:
            sr, si = inside * sr, inside * si
        carry = (nr, ni, carry[2] + sr, carry[3] + si)
    return carry


def _segment_inits(ar, ai, end_r, end_i, rev, nj):
    pr, pi = ar, ai
    for _ in range(int(math.log2(nj))):
        pr, pi = pr * pr - pi * pi, 2.0 * pr * pi
    seg = lax.broadcasted_iota(jnp.int32, end_r.shape, 0)
    zero = jnp.zeros_like(end_r)

    def chain(shift, keep):
        ir, ii = zero, zero
        for _ in range(N_SEG - 1):
            tr, ti = _cfma(pr, pi, ir, ii, end_r, end_i)
            ir = jnp.where(keep, pltpu.roll(tr, shift, 0), 0.0)
            ii = jnp.where(keep, pltpu.roll(ti, shift, 0), 0.0)
        return ir, ii

    up_r, up_i = chain(1, seg >= 1)
    dn_r, dn_i = chain(N_SEG - 1, seg <= N_SEG - 2)
    return jnp.where(rev, dn_r, up_r), jnp.where(rev, dn_i, up_i)


def _ssm_specs(l):
    act = pl.BlockSpec((l, SSM_CH), lambda k, d: (0, k))
    bmat = pl.BlockSpec((None, None, SSM_CH, 2 * SSM_ST), lambda k, d: (d, k, 0, 0))
    cmat = pl.BlockSpec((None, None, 2 * SSM_ST, SSM_CH), lambda k, d: (d, k, 0, 0))
    lam = pl.BlockSpec((None, None, 1, SSM_ST), lambda k, d: (d, k, 0, 0))
    return act, bmat, cmat, lam


def _ssm_fwd(u_seg, bcat, ccat, lam_re, lam_im):
    l = u_seg.shape[0]
    nj = l // N_SEG
    nc = l // SSM_RC

    def body(u_ref, b_ref, c_ref, lr_ref, li_ref, y_ref, keep_ref, xs_ref, stage0, stage1, keep_sem):
        k, d = pl.program_id(0), pl.program_id(1)
        rev = d == 1
        shape = (N_SEG, SSM_ST)
        ar, ai = jnp.broadcast_to(lr_ref[...], shape), jnp.broadcast_to(li_ref[...], shape)
        zero = jnp.zeros(shape, F32)

        def inputs(ci, stage):
            rows = _chunk_rows(ci, rev, nc)
            bu = _dg(u_ref[rows, :], b_ref[...], NN)
            stage[...] = bu
            xs_ref[rows, :] = bu

        def first(stage, ci, carry):
            return _scan_chunk((stage, True), None, ar, ai, rev, nj, ci, carry)

        def first_pass(t, carry):
            inputs(2 * t + 1, stage1)
            carry = first(stage0, 2 * t, carry)
            inputs(2 * t + 2, stage0)
            return first(stage1, 2 * t + 1, carry)

        inputs(0, stage0)
        carry = lax.fori_loop(0, nc // 2 - 1, first_pass, (zero, zero))
        inputs(nc - 1, stage1)
        carry = first(stage0, nc - 2, carry)
        end_r, end_i = first(stage1, nc - 1, carry)
        init = _segment_inits(ar, ai, end_r, end_i, rev, nj)

        @pl.when(d == 0)
        def _():
            y_ref[...] = jnp.zeros_like(y_ref)

        def outputs(ci):
            rows = _chunk_rows(ci, rev, nc)
            y_ref[rows, :] += _dg(xs_ref[rows, :].astype(BF16), c_ref[...], NN)
            pltpu.make_async_copy(xs_ref.at[rows], keep_ref.at[d, k, rows], keep_sem).start()

        def second(ci, carry):
            return _scan_chunk((xs_ref, False), (xs_ref, False), ar, ai, rev, nj, ci, carry)

        def second_pass(ci, carry):
            outputs(ci - 1)
            return second(ci, carry)

        lax.fori_loop(1, nc, second_pass, second(0, init))
        outputs(nc - 1)
        pltpu.make_async_copy(xs_ref, keep_ref.at[d, k], keep_sem).wait()

    act, bmat, cmat, lam = _ssm_specs(l)
    return pl.pallas_call(
        body, grid=(SSM_CB, 2), in_specs=[act, bmat, cmat, lam, lam], out_specs=[act, ANY],
        out_shape=[jax.ShapeDtypeStruct((l, SSM_WIDTH), F32),
                   jax.ShapeDtypeStruct((2, SSM_CB, l, 2 * SSM_ST), F32)],
        scratch_shapes=[pltpu.VMEM((l, 2 * SSM_ST), F32), pltpu.VMEM((SSM_RC, 2 * SSM_ST), F32),
                        pltpu.VMEM((SSM_RC, 2 * SSM_ST), F32), pltpu.SemaphoreType.DMA],
        name="ssm_fwd", compiler_params=_params(("parallel", "arbitrary"), vmem_mb=56),
    )(u_seg, bcat.astype(BF16), ccat.astype(BF16), lam_re, lam_im)


def _ssm_bwd(u_seg, dy_seg, states, bcat, ccat, lam_re, lam_im):
    l = u_seg.shape[0]
    nj = l // N_SEG
    nc = l // SSM_RC

    def body(u_ref, dy_ref, keep_ref, b_ref, c_ref, lr_ref, li_ref,
             du_ref, db_ref, dc_ref, dlr_ref, dli_ref, xs_ref, gs_ref, stage0, stage1, keep_sem):
        k, d = pl.program_id(0), pl.program_id(1)
        rev = d == 1
        back = jnp.logical_not(rev)
        shape = (N_SEG, SSM_ST)
        ar, ai = jnp.broadcast_to(lr_ref[...], shape), -jnp.broadcast_to(li_ref[...], shape)
        zero = jnp.zeros(shape, F32)
        fetch = pltpu.make_async_copy(keep_ref.at[d, k], xs_ref, keep_sem)
        fetch.start()

        def inputs(ci, stage):
            rows = _chunk_rows(ci, back, nc)
            dx = _dg(dy_ref[rows, :], c_ref[...], NT)
            stage[...] = dx
            gs_ref[rows, :] = dx

        def first(stage, ci, carry):
            return _scan_chunk((stage, True), None, ar, ai, back, nj, ci, carry)

        def first_pass(t, carry):
            inputs(2 * t + 1, stage1)
            carry = first(stage0, 2 * t, carry)
            inputs(2 * t + 2, stage0)
            return first(stage1, 2 * t + 1, carry)

        inputs(0, stage0)
        carry = lax.fori_loop(0, nc // 2 - 1, first_pass, (zero, zero))
        inputs(nc - 1, stage1)
        carry = first(stage0, nc - 2, carry)
        end_r, end_i = first(stage1, nc - 1, carry)
        init = _segment_inits(ar, ai, end_r, end_i, back, nj)
        fetch.wait()
        db_ref[...] = jnp.zeros_like(db_ref)
        dc_ref[...] = jnp.zeros_like(dc_ref)

        @pl.when(d == 0)
        def _():
            du_ref[...] = jnp.zeros_like(du_ref)

        def outputs(ci, stage):
            rows = _chunk_rows(ci, back, nc)
            g = stage[...].astype(BF16)
            dc_ref[...] += _dg(xs_ref[rows, :].astype(BF16), dy_ref[rows, :], TN)
            db_ref[...] += _dg(u_ref[rows, :], g, TN)
            du_ref[rows, :] += _dg(g, b_ref[...], NT)

        def second(ci, stage, carry):
            return _scan_chunk((gs_ref, False), (stage, True), ar, ai, back, nj, ci, carry, prev_ref=xs_ref)

        def second_pass(t, carry):
            outputs(2 * t, stage0)
            carry = second(2 * t + 1, stage1, carry)
            outputs(2 * t + 1, stage1)
            return second(2 * t + 2, stage0, carry)

        carry = lax.fori_loop(0, nc // 2 - 1, second_pass, second(0, stage0, init + (zero, zero)))
        outputs(nc - 2, stage0)
        gr, gi, acc_r, acc_i = second(nc - 1, stage1, carry)
        outputs(nc - 1, stage1)

        seg = lax.broadcasted_iota(jnp.int32, shape, 0)
        jb = jnp.where(rev, nj - 1, 0)
        erow = pl.ds(pl.multiple_of((nj - 1 - jb) * N_SEG, N_SEG), N_SEG)

        def before(t):
            up = jnp.where(seg >= 1, pltpu.roll(t, 1, 0), 0.0)
            down = jnp.where(seg <= N_SEG - 2, pltpu.roll(t, N_SEG - 1, 0), 0.0)
            return jnp.where(rev, down, up)

        init_r, init_i = before(xs_ref[erow, _RE]), before(xs_ref[erow, _IM])
        acc_r = acc_r + gr * init_r + gi * init_i
        acc_i = acc_i + gi * init_r - gr * init_i
        dlr_ref[...] = jnp.sum(acc_r, axis=0, keepdims=True)
        dli_ref[...] = jnp.sum(acc_i, axis=0, keepdims=True)

    act, bmat, cmat, lam = _ssm_specs(l)
    return pl.pallas_call(
        body, grid=(SSM_CB, 2), in_specs=[act, act, ANY, bmat, cmat, lam, lam],
        out_specs=[act, bmat, cmat, lam, lam],
        out_shape=[jax.ShapeDtypeStruct((l, SSM_WIDTH), F32),
                   jax.ShapeDtypeStruct(bcat.shape, F32), jax.ShapeDtypeStruct(ccat.shape, F32),
                   jax.ShapeDtypeStruct(lam_re.shape, F32), jax.ShapeDtypeStruct(lam_im.shape, F32)],
        scratch_shapes=[pltpu.VMEM((l, 2 * SSM_ST), F32), pltpu.VMEM((l, 2 * SSM_ST), F32),
                        pltpu.VMEM((SSM_RC, 2 * SSM_ST), F32), pltpu.VMEM((SSM_RC, 2 * SSM_ST), F32),
                        pltpu.SemaphoreType.DMA],
        name="ssm_bwd", compiler_params=_params(("parallel", "arbitrary"), vmem_mb=58),
    )(u_seg, dy_seg, states, bcat.astype(BF16), ccat.astype(BF16), lam_re, lam_im)


def _glu_fwd(y_ssm, u, d_skip, w_glu):
    l, w = u.shape

    def body(y_ref, u_ref, d_ref, w_ref, pre_ref, s_ref, ys_ref):
        pre = y_ref[...] + d_ref[...] * u_ref[...]
        z = _gelu(pre)
        s = _dg(z.astype(BF16), w_ref[...], NN)
        pre_ref[...] = pre
        s_ref[...] = s
        ys_ref[...] = z * _sigmoid(s)

    row = pl.BlockSpec((TM_EW, w), lambda i: (i, 0))
    out = jax.ShapeDtypeStruct((l, w), F32)
    return pl.pallas_call(
        body, grid=(l // TM_EW,),
        in_specs=[row, row, pl.BlockSpec((1, w), lambda i: (0, 0)), pl.BlockSpec((w, w), lambda i: (0, 0))],
        out_specs=[row, row, row], out_shape=[out, out, out], name="glu_fwd",
        compiler_params=_params(("parallel",)),
    )(y_ssm, u, d_skip, w_glu)


def _glu_bwd(pre, s, dys, u, d_skip, w_glu):
    l, w = u.shape

    def body(pre_ref, s_ref, dys_ref, u_ref, d_ref, w_ref, dpre_ref, z_ref, ds_ref, dd_ref):
        pre, dys = pre_ref[...], dys_ref[...]
        z = _gelu(pre)
        sig = _sigmoid(s_ref[...])
        ds = (dys * z * sig * (1.0 - sig)).astype(BF16)
        dz = dys * sig + _dg(ds, w_ref[...], NT)
        dpre = dz * _gelu_grad(pre)
        dpre_ref[...] = dpre
        z_ref[...] = z.astype(BF16)
        ds_ref[...] = ds

        @pl.when(pl.program_id(0) == 0)
        def _():
            dd_ref[...] = jnp.zeros_like(dd_ref)

        dd_ref[...] += jnp.sum(dpre * u_ref[...], axis=0, keepdims=True)

    row = pl.BlockSpec((TM_EW, w), lambda i: (i, 0))
    vec = pl.BlockSpec((1, w), lambda i: (0, 0))
    return pl.pallas_call(
        body, grid=(l // TM_EW,),
        in_specs=[row, row, row, row, vec, pl.BlockSpec((w, w), lambda i: (0, 0))],
        out_specs=[row, row, row, vec],
        out_shape=[jax.ShapeDtypeStruct((l, w), F32), jax.ShapeDtypeStruct((l, w), BF16),
                   jax.ShapeDtypeStruct((l, w), BF16), jax.ShapeDtypeStruct((1, w), F32)],
        name="glu_bwd", compiler_params=_params(("arbitrary",)),
    )(pre, s, dys, u, d_skip, w_glu)


TM_CV = 512
TC_CV = 256
TM_CF = 256
TC_CF = D_FF // 2
HALO = SUBLANES


def _conv_specs(l, col0, tm=TM_CV, tc=TC_CV):
    per = tm // HALO
    nh = l // HALO
    off = col0 // tc
    return [
        pl.BlockSpec((HALO, tc), lambda j, i: (jnp.maximum(i * per - 1, 0), j + off)),
        pl.BlockSpec((tm, tc), lambda j, i: (i, j + off)),
        pl.BlockSpec((HALO, tc), lambda j, i: (jnp.minimum((i + 1) * per, nh - 1), j + off)),
    ]


def _ext(prev_ref, mid_ref, next_ref, first, last):
    p = jnp.where(first, 0.0, prev_ref[...])
    n = jnp.where(last, 0.0, next_ref[...])
    return jnp.concatenate([p, mid_ref[...], n], axis=0)


def _shift_dn(t):
    return pltpu.roll(t, 1, 0)


def _shift_up(t):
    return pltpu.roll(t, t.shape[0] - 1, 0)


def _conv3(e, w_ref, b_ref):
    return w_ref[0:1, :] * _shift_dn(e) + w_ref[1:2, :] * e + w_ref[2:3, :] * _shift_up(e) + b_ref[...]


def _convffn_fwd(up_pre, conv_w, conv_b):
    l = up_pre.shape[0]
    tm, tc = TM_CF, TC_CF
    ni = l // tm
    wspec = lambda off: pl.BlockSpec((3, tc), lambda j, i: (0, j + off))
    bspec = lambda off: pl.BlockSpec((1, tc), lambda j, i: (0, j + off))
    voff = D_FF // tc

    def body(gp, gm, gn, vp, vm, vn, wg, bg, wv, bv, o_ref):
        i = pl.program_id(1)
        first, last = i == 0, i == ni - 1
        gate = _conv3(_ext(gp, gm, gn, first, last), wg, bg)[HALO:HALO + tm]
        val = _conv3(_ext(vp, vm, vn, first, last), wv, bv)[HALO:HALO + tm]
        o_ref[...] = (gate * _sigmoid(gate) * val).astype(BF16)

    return pl.pallas_call(
        body, grid=(D_FF // tc, ni),
        in_specs=_conv_specs(l, 0, tm, tc) + _conv_specs(l, D_FF, tm, tc)
        + [wspec(0), bspec(0), wspec(voff), bspec(voff)],
        out_specs=pl.BlockSpec((tm, tc), lambda j, i: (i, j)),
        out_shape=jax.ShapeDtypeStruct((l, D_FF), BF16), name="convffn_fwd",
        compiler_params=_params(("parallel", "parallel")),
    )(up_pre, up_pre, up_pre, up_pre, up_pre, up_pre, conv_w, conv_b, conv_w, conv_b)


HALO_B = 2 * SUBLANES


def _convffn_bwd(up_pre, dx2b, w_down, conv_w, conv_b):
    l = up_pre.shape[0]
    ni = l // TM_CV
    d = dx2b.shape[1]
    wspec = lambda off: pl.BlockSpec((3, TC_CV), lambda i, j: (0, j + off))
    bspec = lambda off: pl.BlockSpec((1, TC_CV), lambda i, j: (0, j + off))
    voff = D_FF // TC_CV
    swap = lambda spec: pl.BlockSpec(spec.block_shape, lambda i, j, f=spec.index_map: f(j, i))
    per, nh = TM_CV // HALO_B, l // HALO_B
    dx_specs = [pl.BlockSpec((HALO_B, d), lambda i, j: (jnp.maximum(i * per - 1, 0), 0)),
                pl.BlockSpec((TM_CV, d), lambda i, j: (i, 0)),
                pl.BlockSpec((HALO_B, d), lambda i, j: (jnp.minimum((i + 1) * per, nh - 1), 0))]

    def body(gp, gm, gn, vp, vm, vn, xp, xm, xn, wd, wg, bg, wv, bv, dup_ref, pg_ref, pv_ref):
        i = pl.program_id(0)
        first, last = i == 0, i == ni - 1
        ge, ve = _ext(gp, gm, gn, first, last), _ext(vp, vm, vn, first, last)
        zero = jnp.zeros((HALO_B, d), BF16)
        dx = jnp.concatenate([jnp.where(first, zero, xp[...]), xm[...], jnp.where(last, zero, xn[...])], axis=0)
        de = _dg(dx, wd[...], NT)[HALO_B - HALO:HALO_B + TM_CV + HALO]
        taps = [(_shift_dn(e), e, _shift_up(e)) for e in (ge, ve)]
        conv = lambda t, w_ref, b_ref: w_ref[0:1, :] * t[0] + w_ref[1:2, :] * t[1] + w_ref[2:3, :] * t[2] + b_ref[...]
        gate, val = conv(taps[0], wg, bg), conv(taps[1], wv, bv)
        sig = _sigmoid(gate)
        silu = gate * sig
        dgate = de * val * (sig + silu * (1.0 - sig))
        dval = de * silu
        mid = slice(HALO, HALO + TM_CV)
        rid = lax.broadcasted_iota(jnp.int32, (SUBLANES, TC_CV), 0)
        for half, (dup, tap, w_ref, p_ref) in enumerate(((dgate, taps[0], wg, pg_ref), (dval, taps[1], wv, pv_ref))):
            dpre = w_ref[0:1, :] * _shift_up(dup) + w_ref[1:2, :] * dup + w_ref[2:3, :] * _shift_dn(dup)
            dup_ref[half] = dpre[mid].astype(BF16)
            dm_ = dup[mid]
            sums = [jnp.sum(dm_ * t[mid], axis=0, keepdims=True) for t in tap]
            sums.append(jnp.sum(dm_, axis=0, keepdims=True))
            acc = jnp.zeros((SUBLANES, TC_CV), F32)
            for k, sk in enumerate(sums):
                acc = jnp.where(rid == k, sk, acc)
            p_ref[...] = acc

    par = pl.BlockSpec((None, SUBLANES, TC_CV), lambda i, j: (i, 0, j))
    dup, pg, pv = pl.pallas_call(
        body, grid=(ni, D_FF // TC_CV),
        in_specs=[swap(s) for s in _conv_specs(l, 0) + _conv_specs(l, D_FF)] + dx_specs
        + [pl.BlockSpec((TC_CV, d), lambda i, j: (j, 0)), wspec(0), bspec(0), wspec(voff), bspec(voff)],
        out_specs=[pl.BlockSpec((2, TM_CV, TC_CV), lambda i, j: (0, i, j)), par, par],
        out_shape=[jax.ShapeDtypeStruct((2, l, D_FF), BF16),
                   jax.ShapeDtypeStruct((ni, SUBLANES, D_FF), F32), jax.ShapeDtypeStruct((ni, SUBLANES, D_FF), F32)],
        name="convffn_bwd", compiler_params=_params(("parallel", "parallel")),
    )(up_pre, up_pre, up_pre, up_pre, up_pre, up_pre, dx2b, dx2b, dx2b, w_down, conv_w, conv_b, conv_w, conv_b)
    return dup, jnp.concatenate([jnp.sum(pg, axis=0), jnp.sum(pv, axis=0)], axis=1)


def _local_step(x, target, wb, sp, mixer_weights=None, late_weights=None, ffn_grads_ready=None,
                ffn_grads_next=None):
    l = x.shape[0]
    tabs = _rope_tables(l)
    disc = _ssm_disc(sp["a_re"], sp["a_im"], sp["log_step"], sp["b_re"], sp["b_im"])
    bcat, ccat, lam_re, lam_im = _ssm_pack(*disc, sp["c_re"], sp["c_im"])
    d_skip = sp["d_skip"].reshape(1, SSM_WIDTH)

    big = min(l, 1024)
    h, qkv, u = _rms_mm_rope(x, sp["norm_mix_g"], wb["w_in"], tabs, "mm_in")
    attn, lse = _attn_fwd(qkv, sp["sink"])
    u_seg = _to_segments(u).astype(BF16)
    y_seg, states = _ssm_fwd(u_seg, bcat, ccat, lam_re, lam_im)
    y_ssm = _from_segments(y_seg)
    if mixer_weights is not None:
        wb = dict(wb, **mixer_weights(attn))
    pre, s_glu, ys = _glu_fwd(y_ssm, u, d_skip, wb["w_glu"])
    mixed, x1, h2 = _mix_mm_res_rms(attn, ys, sp["norm_attn_g"], sp["norm_ssm_g"], wb["w_out"], x,
                                    sp["norm_ffn_g"], "mm_out")
    if late_weights is not None:
        wb = dict(wb, **late_weights(h2))
    up_pre = _mm_nn_cols(h2, wb["w_up"], big, "mm_up")
    conv_w = wb["conv_w"]
    act = _convffn_fwd(up_pre, conv_w, sp["conv_b"])
    loss, dx2, dx2b, d_final_g = _mm_res_loss(act, wb["w_down"], x1, sp["norm_final_g"].reshape(1, D_MODEL), target)

    g = {"norm_final_g": d_final_g.reshape(D_MODEL)}
    g["w_down"] = _mm_tn(act, dx2b, D_FF // 2, 512, "mm_down_dw")
    dup_pre, conv_par = _convffn_bwd(up_pre, dx2b, wb["w_down"], conv_w, sp["conv_b"])
    g["conv_w"], g["conv_b"] = conv_par[0:3], conv_par[3:4]
    g["w_up"] = _mm_tn_cols(h2, dup_pre, wb["w_up"].shape[0], 512, "mm_up_dw")
    zero = ffn_grads_ready(g["w_up"], g["w_down"]) if ffn_grads_ready is not None else 0.0
    dx1, dx1b, g["norm_ffn_g"] = _mm_cols_rms_bwd(dup_pre, wb["w_up"], x1, sp["norm_ffn_g"] + zero, dx2, "mm_up_dx")
    g["w_out"] = _mm_tn(mixed, dx1b, 1024, 1024, "mm_out_dw")
    zero = ffn_grads_next(g["w_out"]) if ffn_grads_next is not None else 0.0
    dattn, dys, g["norm_attn_g"], g["norm_ssm_g"] = _mm_mix_bwd(
        dx1b, wb["w_out"], attn, ys, sp["norm_attn_g"] + zero, sp["norm_ssm_g"], "mm_out_dx")
    dpre, zb, dsb, dd = _glu_bwd(pre, s_glu, dys, u, d_skip, wb["w_glu"])
    g["d_skip"] = dd.reshape(N_SSM_GROUPS, SSM_GROUP)
    g["w_glu"] = _mm_tn(zb, dsb, 512, 512, "mm_glu_dw")
    du_seg, dbcat, dccat, dlam_re, dlam_im = _ssm_bwd(u_seg, _to_segments(dpre).astype(BF16), states, bcat, ccat,
                                                      lam_re, lam_im)
    dlb_re, dlb_im, dbb_re, dbb_im, g["c_re"], g["c_im"] = _ssm_unpack(dbcat, dccat, dlam_re, dlam_im)
    _, disc_vjp = jax.vjp(_ssm_disc, sp["a_re"], sp["a_im"], sp["log_step"], sp["b_re"], sp["b_im"])
    g["a_re"], g["a_im"], g["log_step"], g["b_re"], g["b_im"] = disc_vjp((dlb_re, dlb_im, dbb_re, dbb_im))
    dq, dkv, g["sink"] = _attn_bwd(qkv, attn, dattn, lse, sp["sink"])
    dproj = _rope_bwd(dq, dkv, _from_segments(du_seg), dpre, d_skip, tabs)
    g["w_in"] = _mm_tn(dproj, h, IN_WIDTH // 5, D_MODEL, "mm_in_dw")
    grad_x, _, g["norm_mix_g"] = _mm_nn_rms_bwd(dproj, wb["w_in"], x, sp["norm_mix_g"], dx1, "mm_in_dx")
    return loss, grad_x, g


MESH = pl.DeviceIdType.MESH
ANY = pl.BlockSpec(memory_space=pl.ANY)


def _place():
    x, y, c = lax.axis_index("x"), lax.axis_index("y"), lax.axis_index("c")
    chips = [(1 - x, y), (x, 1 - y), (1 - x, 1 - y)]
    return x, y, c, chips


def _chip_index(px, py):
    return 2 * px + py


CHUNK_BYTES = 256 * 1024
MAX_CHUNKS = 16


def _row_chunks(rows, row_bytes, align):
    n = max(1, min(MAX_CHUNKS, (rows * row_bytes) // CHUNK_BYTES))
    per = -(-rows // n)
    per = -(-per // align) * align
    return [(r0, min(per, rows - r0)) for r0 in range(0, rows, per)]


def _align_of(dtype):
    return SUBLANES * 4 // jnp.dtype(dtype).itemsize


def _remote(src, dst, send_sem, recv_sem, to):
    return pltpu.make_async_remote_copy(src_ref=src, dst_ref=dst, send_sem=send_sem, recv_sem=recv_sem,
                                        device_id=to, device_id_type=MESH)


CAST_ROWS = 64


def _gather_weights(shards, dtypes):
    nw = len(shards)

    def body(*refs):
        w_refs, o_refs = refs[:nw], refs[nw:2 * nw]
        send_sems, recv_sems, in_sems, out_sems = refs[2 * nw:2 * nw + 4]
        raw, cast = refs[2 * nw + 4:3 * nw + 4], refs[3 * nw + 4:]
        x, y, c, chips = _place()
        mine = _chip_index(x, y)
        sibling = (x, y, 1 - c)

        def rows_of(ref, chip, r0, nr):
            return ref.at[chip, pl.ds(r0, nr), :]

        def copy(wi, k, src, dst, to):
            return _remote(src, dst, send_sems.at[wi, k], recv_sems.at[wi, k], to)

        geo = []
        for wi in range(nw):
            rows, cols = w_refs[wi].shape
            row_bytes = cols * jnp.dtype(dtypes[wi]).itemsize
            geo.append((rows // 2, _row_chunks(rows // 2, row_bytes, _align_of(dtypes[wi]))))

        stage_in = [pltpu.make_async_copy(w_refs[wi], raw[wi], in_sems.at[wi]) for wi in range(nw)]
        for cp in stage_in:
            cp.start()
        staged = [raw[wi] if dtypes[wi] == w_refs[wi].dtype else cast[wi] for wi in range(nw)]
        stage_out = []
        for wi in range(nw):
            stage_in[wi].wait()
            if staged[wi] is not raw[wi]:
                def cast_rows(i, _, wi=wi):
                    rows = pl.ds(pl.multiple_of(i * CAST_ROWS, CAST_ROWS), CAST_ROWS)
                    cast[wi][rows, :] = raw[wi][rows, :].astype(dtypes[wi])
                    return 0

                lax.fori_loop(0, w_refs[wi].shape[0] // CAST_ROWS, cast_rows, 0)
            cp = pltpu.make_async_copy(staged[wi], o_refs[wi].at[mine], out_sems.at[wi])
            cp.start()
            stage_out.append(cp)

        for wi in range(nw):
            hr, half_chunks = geo[wi]
            for j, chip in enumerate(chips):
                for r0, nr in half_chunks:
                    copy(wi, j, staged[wi].at[pl.ds(c * hr + r0, nr), :],
                         rows_of(o_refs[wi], mine, c * hr + r0, nr), (*chip, c)).start()
        for wi in range(nw):
            hr, half_chunks = geo[wi]
            for j, chip in enumerate(chips):
                got = rows_of(o_refs[wi], _chip_index(*chip), c * hr, hr)
                copy(wi, j, got, got, (*chip, c)).wait_recv()
                for r0, nr in half_chunks:
                    piece = rows_of(o_refs[wi], _chip_index(*chip), c * hr + r0, nr)
                    copy(wi, 3 + j, piece, piece, sibling).start()
        for wi in range(nw):
            hr = geo[wi][0]
            for j, chip in enumerate(chips):
                got = rows_of(o_refs[wi], _chip_index(*chip), (1 - c) * hr, hr)
                copy(wi, 3 + j, got, got, sibling).wait_recv()
        for wi in range(nw):
            hr = geo[wi][0]
            sent = rows_of(o_refs[wi], mine, c * hr, hr)
            for k in range(6):
                copy(wi, k, sent, sent, sibling).wait_send()
            stage_out[wi].wait()

    return pl.pallas_call(
        body, in_specs=[ANY] * nw, out_specs=[ANY] * nw,
        out_shape=[jax.ShapeDtypeStruct((4, *s.shape), t) for s, t in zip(shards, dtypes)],
        scratch_shapes=[pltpu.SemaphoreType.DMA((nw, 6)), pltpu.SemaphoreType.DMA((nw, 6)),
                        pltpu.SemaphoreType.DMA((nw,)), pltpu.SemaphoreType.DMA((nw,))]
        + [pltpu.VMEM(s.shape, s.dtype) for s in shards] + [pltpu.VMEM(s.shape, t) for s, t in zip(shards, dtypes)],
        name="gather_weights", compiler_params=_params(vmem_mb=40),
    )(*shards)


HBM = pl.BlockSpec(memory_space=pltpu.HBM)
SEM = pl.BlockSpec(memory_space=pltpu.SEMAPHORE)
EFFECT = pltpu.SideEffectType.DATAFLOW_SIDE_EFFECTING


def _cast_place(w, place, dtype, after, name):
    rows, cols = w.shape
    tr = _row_tile(rows, cols, _align_of(dtype))

    def body(p_ref, w_ref, after_ref, o_ref):
        del p_ref, after_ref
        o_ref[...] = w_ref[...].astype(dtype)

    grid_spec = pltpu.PrefetchScalarGridSpec(
        num_scalar_prefetch=1, grid=(rows // tr,),
        in_specs=[pl.BlockSpec((tr, cols), lambda i, p: (i, 0)), ANY],
        out_specs=pl.BlockSpec((None, tr, cols), lambda i, p: (p[1], i, 0)))
    return pl.pallas_call(body, grid_spec=grid_spec, out_shape=jax.ShapeDtypeStruct((4, rows, cols), dtype),
                          name=name, compiler_params=_params(("parallel",)))(place, w, after)


def _split_start(name, arrays, n_pairs, issue):
    n = len(arrays)

    def body(*refs):
        issue(refs[:n], refs[n:n + n_pairs], refs[n + n_pairs:n + 2 * n_pairs])
        token = refs[2 * n + 2 * n_pairs]
        token[...] = jnp.zeros_like(token)

    dma = pltpu.SemaphoreType.DMA(())
    outs = pl.pallas_call(
        body, name=name,
        out_shape=[dma] * (2 * n_pairs) + [pltpu.HBM(t.shape, t.dtype) for t in arrays]
        + [jax.ShapeDtypeStruct((SUBLANES, LANES), F32)],
        in_specs=[HBM] * n, out_specs=[SEM] * (2 * n_pairs) + [HBM] * n + [pl.BlockSpec(memory_space=pltpu.VMEM)],
        input_output_aliases={a: 2 * n_pairs + a for a in range(n)},
        compiler_params=pltpu.CompilerParams(has_side_effects=EFFECT),
    )(*[pltpu.with_memory_space_constraint(t, pltpu.HBM) for t in arrays])
    return outs[:n_pairs], outs[n_pairs:2 * n_pairs], outs[2 * n_pairs:2 * n_pairs + n], outs[-1]


def _split_wait(name, send_sems, recv_sems, flying, sizes, after):
    n, n_pairs = len(flying), len(send_sems)

    def body(*refs):
        x, y, c, _ = _place()
        for k, ref in enumerate(sizes(refs[:n])):
            cp = _remote(ref, ref, refs[n + k], refs[n + n_pairs + k], (x, y, 1 - c))
            cp.wait_send()
            cp.wait_recv()

    return pl.pallas_call(
        body, name=name, out_shape=[pltpu.HBM(t.shape, t.dtype) for t in flying],
        in_specs=[HBM] * n + [SEM] * (2 * n_pairs) + [ANY], out_specs=[HBM] * n,
        input_output_aliases={a: a for a in range(n)},
        compiler_params=pltpu.CompilerParams(has_side_effects=EFFECT),
    )(*flying, *send_sems, *recv_sems, after)


def _spread_start(lands, name):
    def issue(land_refs, send_sems, recv_sems):
        x, y, c, chips = _place()
        mine = _chip_index(x, y)
        for a, land in enumerate(land_refs):
            _, rows, cols = land.shape
            hr = rows // 2
            row_bytes = cols * jnp.dtype(land.dtype).itemsize
            for r0, nr in _row_chunks(hr, row_bytes, _align_of(land.dtype)):
                piece = land.at[mine, pl.ds(c * hr + r0, nr), :]
                for chip in chips:
                    for core in (0, 1):
                        _remote(piece, piece, send_sems[a], recv_sems[a], (*chip, core)).start()

    return _split_start(name, lands, len(lands), issue)


def _spread_wait(send_sems, recv_sems, flying, after, name):
    return _split_wait(name, send_sems, recv_sems, flying, lambda refs: [r.at[pl.ds(0, 3)] for r in refs], after)


def _pair_start(grads):
    n = len(grads)
    zones = [lax.empty((4, g.shape[1] // 2, g.shape[2]), F32) for g in grads]

    def issue(refs, send_sems, recv_sems):
        x, y, c, _ = _place()
        for a in range(n):
            g_ref, z_ref = refs[a], refs[n + a]
            _, rows, cols = g_ref.shape
            hr = rows // 2
            for k in range(4):
                for r0, nr in _row_chunks(hr, cols * 4, SUBLANES):
                    _remote(g_ref.at[k, pl.ds((1 - c) * hr + r0, nr), :], z_ref.at[k, pl.ds(r0, nr), :],
                            send_sems[a], recv_sems[a], (x, y, 1 - c)).start()

    return _split_start("pair_start", list(grads) + zones, n, issue)


def _pair_wait(send_sems, recv_sems, flying, after):
    n = len(flying) // 2
    out = _split_wait("pair_wait", send_sems, recv_sems, flying, lambda refs: list(refs[n:]), after)
    return out[:n], out[n:]


def _chip_start(sums):
    n = len(sums)
    zones = [lax.empty((3, *s.shape[1:]), s.dtype) for s in sums]

    def issue(refs, send_sems, recv_sems):
        x, y, c, chips = _place()
        for a in range(n):
            s_ref, z_ref = refs[a], refs[n + a]
            _, rows, cols = s_ref.shape
            row_bytes = cols * jnp.dtype(s_ref.dtype).itemsize
            for r0, nr in _row_chunks(rows, row_bytes, _align_of(s_ref.dtype)):
                for j, chip in enumerate(chips):
                    _remote(s_ref.at[_chip_index(*chip), pl.ds(r0, nr), :], z_ref.at[j, pl.ds(r0, nr), :],
                            send_sems[a], recv_sems[a], (*chip, c)).start()

    return _split_start("chip_start", list(sums) + zones, n, issue)


def _chip_wait(send_sems, recv_sems, flying, after):
    n = len(flying) // 2
    return _split_wait("chip_wait", send_sems, recv_sems, flying, lambda refs: list(refs[n:]), after)[n:]


def _pair_exchange(grads):
    na = len(grads)

    def body(*refs):
        g_refs, o_refs = refs[:na], refs[na:2 * na]
        send_sems, recv_sems = refs[2 * na:]
        x, y, c, _ = _place()
        sibling = (x, y, 1 - c)
        for ai in range(na):
            _, rows, cols = g_refs[ai].shape
            hr = rows // 2
            for k in range(4):
                for r0, nr in _row_chunks(hr, cols * 4, SUBLANES):
                    _remote(g_refs[ai].at[k, pl.ds((1 - c) * hr + r0, nr), :], o_refs[ai].at[k, pl.ds(r0, nr), :],
                            send_sems.at[ai], recv_sems.at[ai], sibling).start()
        for ai in range(na):
            _remote(o_refs[ai], o_refs[ai], send_sems.at[ai], recv_sems.at[ai], sibling).wait()

    return pl.pallas_call(
        body, in_specs=[ANY] * na, out_specs=[ANY] * na,
        out_shape=[jax.ShapeDtypeStruct((4, g.shape[1] // 2, g.shape[2]), F32) for g in grads],
        scratch_shapes=[pltpu.SemaphoreType.DMA((na,)), pltpu.SemaphoreType.DMA((na,))],
        name="pair_exchange",
    )(*grads)


def _row_tile(rows, cols, align):
    best = align
    for cand in range(align, rows + 1, align):
        if rows % cand == 0 and cand * cols <= 256 * 1024:
            best = cand
    return best


def _pair_sum(g, got, place, transit, name):
    _, rows, cols = g.shape
    hr = rows // 2
    tr = _row_tile(hr, cols, _align_of(transit))
    nt = hr // tr

    def body(p_ref, g_ref, r_ref, s_ref, own_ref):
        total = g_ref[...] + r_ref[...]
        s_ref[...] = total.astype(transit)

        @pl.when(pl.program_id(1) == p_ref[1])
        def _():
            own_ref[...] = total

    grid_spec = pltpu.PrefetchScalarGridSpec(
        num_scalar_prefetch=1, grid=(nt, 4),
        in_specs=[pl.BlockSpec((None, tr, cols), lambda i, k, p: (k, p[0] * nt + i, 0)),
                  pl.BlockSpec((None, tr, cols), lambda i, k, p: (k, i, 0))],
        out_specs=[pl.BlockSpec((None, tr, cols), lambda i, k, p: (k, i, 0)),
                   pl.BlockSpec((tr, cols), lambda i, k, p: (i, 0))])
    return pl.pallas_call(
        body, grid_spec=grid_spec,
        out_shape=[jax.ShapeDtypeStruct((4, hr, cols), transit), jax.ShapeDtypeStruct((hr, cols), F32)],
        name=name, compiler_params=_params(("parallel", "arbitrary")),
    )(place, g, got)


def _chip_exchange(sums):
    na = len(sums)

    def body(*refs):
        s_refs, o_refs = refs[:na], refs[na:2 * na]
        send_sems, recv_sems = refs[2 * na:]
        x, y, c, chips = _place()
        for ai in range(na):
            _, rows, cols = s_refs[ai].shape
            row_bytes = cols * jnp.dtype(s_refs[ai].dtype).itemsize
            for r0, nr in _row_chunks(rows, row_bytes, _align_of(s_refs[ai].dtype)):
                for j, chip in enumerate(chips):
                    _remote(s_refs[ai].at[_chip_index(*chip), pl.ds(r0, nr), :], o_refs[ai].at[j, pl.ds(r0, nr), :],
                            send_sems.at[ai, j], recv_sems.at[ai, j], (*chip, c)).start()
        for ai in range(na):
            for j, chip in enumerate(chips):
                _remote(o_refs[ai].at[j], o_refs[ai].at[j], send_sems.at[ai, j], recv_sems.at[ai, j],
                        (*chip, c)).wait()

    return pl.pallas_call(
        body, in_specs=[ANY] * na, out_specs=[ANY] * na,
        out_shape=[jax.ShapeDtypeStruct((3, *s.shape[1:]), s.dtype) for s in sums],
        scratch_shapes=[pltpu.SemaphoreType.DMA((na, 3)), pltpu.SemaphoreType.DMA((na, 3))],
        name="chip_exchange",
    )(*sums)


def _chip_sum(own, landed, name):
    hr, cols = own.shape
    tr = _row_tile(hr, cols, _align_of(landed.dtype))

    def body(o_ref, l_ref, f_ref):
        acc = o_ref[...]
        for j in range(3):
            acc = acc + l_ref[j].astype(F32)
        f_ref[...] = acc

    return pl.pallas_call(
        body, grid=(hr // tr,),
        in_specs=[pl.BlockSpec((tr, cols), lambda i: (i, 0)), pl.BlockSpec((3, tr, cols), lambda i: (0, i, 0))],
        out_specs=pl.BlockSpec((tr, cols), lambda i: (i, 0)),
        out_shape=jax.ShapeDtypeStruct((hr, cols), F32), name=name,
        compiler_params=_params(("parallel",)),
    )(own, landed)


def _final_exchange(halves, small):
    nh = len(halves)

    def body(*refs):
        h_refs, s_ref = refs[:nh], refs[nh]
        o_refs, so_ref = refs[nh + 1:2 * nh + 1], refs[2 * nh + 1]
        send_sems, recv_sems, local_sem, ssend_sems, srecv_sems = refs[2 * nh + 2:]
        x, y, c, _ = _place()
        me = 4 * x + 2 * y + c
        sibling = (x, y, 1 - c)
        for hi in range(nh):
            hr, cols = h_refs[hi].shape
            for r0, nr in _row_chunks(hr, cols * 4, SUBLANES):
                _remote(h_refs[hi].at[pl.ds(r0, nr), :], o_refs[hi].at[pl.ds(r0, nr), :],
                        send_sems.at[hi], recv_sems.at[hi], sibling).start()
        small_cps = [pltpu.make_async_copy(s_ref, so_ref.at[me], local_sem)]
        for r in range(1, 8):
            fx, fy, fc = (r >> 2) & 1, (r >> 1) & 1, r & 1
            peer = (1 - x if fx else x, 1 - y if fy else y, 1 - c if fc else c)
            small_cps.append(_remote(s_ref, so_ref.at[me], ssend_sems.at[r - 1], srecv_sems.at[r - 1], peer))
        for cp in small_cps:
            cp.start()
        for hi in range(nh):
            _remote(h_refs[hi], o_refs[hi], send_sems.at[hi], recv_sems.at[hi], sibling).wait()
        for cp in small_cps:
            cp.wait()

    return pl.pallas_call(
        body, in_specs=[ANY] * (nh + 1), out_specs=[ANY] * (nh + 1),
        out_shape=[jax.ShapeDtypeStruct(h.shape, F32) for h in halves]
        + [jax.ShapeDtypeStruct((8, *small.shape), F32)],
        scratch_shapes=[pltpu.SemaphoreType.DMA((nh,)), pltpu.SemaphoreType.DMA((nh,)),
                        pltpu.SemaphoreType.DMA, pltpu.SemaphoreType.DMA((7,)), pltpu.SemaphoreType.DMA((7,))],
        name="final_exchange",
    )(*halves, small)


def _adamw_halves(w, own, other, m, v, place, name):
    r, c = w.shape
    hr = r // 2
    tr = _row_tile(hr, c, SUBLANES)
    nt = hr // tr
    c1 = 1.0 - ADAM_B1 ** ADAM_STEP
    c2 = 1.0 - ADAM_B2 ** ADAM_STEP

    def body(p_ref, w_ref, own_ref, other_ref, m_ref, v_ref, g_ref, d_ref, nm_ref, nv_ref):
        mine = pl.program_id(0) // nt == p_ref[0]
        gv = jnp.where(mine, own_ref[...], other_ref[...])
        nm = ADAM_B1 * m_ref[...] + (1.0 - ADAM_B1) * gv
        nv = ADAM_B2 * v_ref[...] + (1.0 - ADAM_B2) * (gv * gv)
        g_ref[...] = gv
        d_ref[...] = -ADAM_LR * ((nm / c1) / (jnp.sqrt(nv / c2) + ADAM_EPS) + ADAM_WD * w_ref[...])
        nm_ref[...] = nm
        nv_ref[...] = nv

    full = pl.BlockSpec((tr, c), lambda i, p: (i, 0))
    half = pl.BlockSpec((tr, c), lambda i, p: (i % nt, 0))
    out = jax.ShapeDtypeStruct((r, c), F32)
    grid_spec = pltpu.PrefetchScalarGridSpec(num_scalar_prefetch=1, grid=(2 * nt,),
                                             in_specs=[full, half, half, full, full], out_specs=[full] * 4)
    return pl.pallas_call(body, grid_spec=grid_spec, out_shape=[out] * 4, name=name,
                          compiler_params=_params(("parallel",)))(place, w, own, other, m, v)


def _adamw_many(ws, gs, ms, vs, name):
    n = len(ws)
    c1 = 1.0 - ADAM_B1 ** ADAM_STEP
    c2 = 1.0 - ADAM_B2 ** ADAM_STEP

    def body(*refs):
        w_refs, g_refs, m_refs, v_refs = (refs[k * n:(k + 1) * n] for k in range(4))
        d_refs, nm_refs, nv_refs = (refs[(4 + k) * n:(5 + k) * n] for k in range(3))
        for i in range(n):
            gv = g_refs[i][...]
            nm = ADAM_B1 * m_refs[i][...] + (1.0 - ADAM_B1) * gv
            nv = ADAM_B2 * v_refs[i][...] + (1.0 - ADAM_B2) * (gv * gv)
            d_refs[i][...] = -ADAM_LR * ((nm / c1) / (jnp.sqrt(nv / c2) + ADAM_EPS) + ADAM_WD * w_refs[i][...])
            nm_refs[i][...] = nm
            nv_refs[i][...] = nv

    vmem = pl.BlockSpec(memory_space=pltpu.VMEM)
    shapes = [jax.ShapeDtypeStruct(t.shape, F32) for t in ws]
    outs = pl.pallas_call(body, in_specs=[vmem] * (4 * n), out_specs=[vmem] * (3 * n), out_shape=shapes * 3,
                          name=name, compiler_params=_params(vmem_mb=56))(*ws, *gs, *ms, *vs)
    return outs[:n], outs[n:2 * n], outs[2 * n:]


BIG = ("w_in", "w_glu", "w_out", "w_up", "w_down")
WEIGHTS = ("norm_mix_g", "w_in", "a_re", "a_im", "log_step", "b_re", "b_im", "c_re", "c_im", "d_skip", "w_glu",
           "sink", "norm_attn_g", "norm_ssm_g", "w_out", "norm_ffn_g", "w_up", "conv_w", "conv_b", "w_down",
           "norm_final_g")
SMALL = ("norm_mix_g", "a_re", "a_im", "log_step", "b_re", "b_im", "c_re", "c_im", "d_skip", "sink",
         "norm_attn_g", "norm_ssm_g", "norm_ffn_g", "conv_w", "conv_b", "norm_final_g")
SMALL_ROWS = 48
N_DEV = 8


def _tile_rows(size):
    return -(-size // (SUBLANES * D_MODEL)) * SUBLANES


def _by_owner(name, g):
    if name == "w_up":
        return g
    return g.reshape(4, g.shape[0] // 4, g.shape[1])


def _view(name, t):
    if name == "w_in":
        return jnp.swapaxes(t[0], 0, 1)
    if name in ("b_re", "b_im"):
        return jnp.swapaxes(t, -1, -2)
    return t


def _unview(name, t):
    if name == "w_in":
        return jnp.swapaxes(t, 0, 1)[None]
    if name in ("b_re", "b_im"):
        return jnp.swapaxes(t, -1, -2)
    return t


def kernel(x, norm_mix_g, w_in, a_re, a_im, log_step, b_re, b_im, c_re, c_im, d_skip, w_glu, sink, norm_attn_g, norm_ssm_g, w_out, norm_ffn_g, w_up, conv_w, conv_b, w_down, norm_final_g, loss_target, m_norm_mix_g, m_w_in, m_a_re, m_a_im, m_log_step, m_b_re, m_b_im, m_c_re, m_c_im, m_d_skip, m_w_glu, m_sink, m_norm_attn_g, m_norm_ssm_g, m_w_out, m_norm_ffn_g, m_w_up, m_conv_w, m_conv_b, m_w_down, m_norm_final_g, v_norm_mix_g, v_w_in, v_a_re, v_a_im, v_log_step, v_b_re, v_b_im, v_c_re, v_c_im, v_d_skip, v_w_glu, v_sink, v_norm_attn_g, v_norm_ssm_g, v_w_out, v_norm_ffn_g, v_w_up, v_conv_w, v_conv_b, v_w_down, v_norm_final_g):
    given = dict(locals())
    w = {n: given[n] for n in WEIGHTS}
    m = {n: given["m_" + n] for n in WEIGHTS}
    v = {n: given["v_" + n] for n in WEIGHTS}
    xy = 2 * lax.axis_index("x") + lax.axis_index("y")

    core = lax.axis_index("c")
    place = jnp.stack([core, xy]).astype(jnp.int32)

    conv_rows = jnp.pad(w["conv_w"][0], ((0, 2 * SUBLANES - 3), (0, 0)))
    rows = lambda t: t.reshape(4 * t.shape[1], t.shape[2])
    (w_in_all,) = _gather_weights([_view("w_in", w["w_in"])], [BF16])
    wb = {"w_in": rows(w_in_all)}
    early = ("w_in", "w_glu", "w_out")
    mixer = [_cast_place(w[n][0], place, BF16, w_in_all, "cast_" + n) for n in ("w_glu", "w_out")]
    mixer.append(_cast_place(conv_rows, place, F32, w_in_all, "cast_conv_w"))
    *mixer_flight, mixer_token = _spread_start(mixer, "spread_mixer_start")
    late = ("w_up", "w_down")
    *late_flight, token = _spread_start(
        [_cast_place(w[n][0], place, BF16, mixer_token, "cast_" + n) for n in late], "spread_ffn_start")

    def mixer_weights(after):
        w_glu4, w_out4, conv4 = _spread_wait(*mixer_flight, after, "spread_mixer_wait")
        return {"w_glu": rows(w_glu4), "w_out": rows(w_out4),
                "conv_w": conv4[:, :3].transpose(1, 0, 2).reshape(3, 2 * D_FF)}

    def late_weights(after):
        w_up4, w_down4 = _spread_wait(*late_flight, after, "spread_ffn_wait")
        return {"w_up": w_up4, "w_down": rows(w_down4)}

    sp = {n: w[n][0] for n in ("a_re", "a_im", "log_step", "b_re", "b_im", "c_re", "c_im", "d_skip",
                               "norm_mix_g", "norm_attn_g", "norm_ssm_g", "norm_ffn_g", "sink", "conv_b")}
    for n in ("norm_mix_g", "norm_attn_g", "norm_ssm_g", "norm_ffn_g", "sink", "conv_b"):
        sp[n] = sp[n].reshape(1, -1)
    sp["norm_mix_g"] = sp["norm_mix_g"] + token[:1, :1]
    sp["norm_final_g"] = w["norm_final_g"]
    flight = {}

    def ffn_grads_ready(dw_up, dw_down):
        *flight["pair"], token = _pair_start([dw_up, _by_owner("w_down", dw_down)])
        return token[:1, :1]

    def ffn_grads_next(after):
        mine, got = _pair_wait(*flight["pair"], after)
        sums, flight["own"] = zip(*[_pair_sum(a, b, place, BF16, "pair_sum_" + n) for n, a, b in zip(late, mine, got)])
        *flight["chip"], token = _chip_start(list(sums))
        return token[:1, :1]

    loss, grad_x, g = _local_step(x[0], loss_target[0], wb, sp, mixer_weights, late_weights, ffn_grads_ready,
                                  ffn_grads_next)

    def as_rows(t):
        rows = _tile_rows(t.size)
        return jnp.pad(t.reshape(-1), (0, rows * D_MODEL - t.size)).reshape(rows, D_MODEL)

    pieces = [as_rows(g[n]) for n in SMALL] + [as_rows(loss)]
    spare = N_DEV * SMALL_ROWS - sum(p.shape[0] for p in pieces)
    small = jnp.concatenate(pieces + [jnp.zeros((spare, D_MODEL), F32)]).reshape(4, 2 * SMALL_ROWS, D_MODEL)
    by_owner = [_by_owner(n, g[n]) for n in early] + [small]
    got = _pair_exchange(by_owner)
    transit = [BF16] * len(early) + [F32]
    chip_sums, own_sums = zip(*[_pair_sum(a, b, place, t, "pair_sum_" + n)
                                for n, a, b, t in zip(early + ("small",), by_owner, got, transit)])
    landed = _chip_exchange(list(chip_sums))
    halves = {n: _chip_sum(o, t, "chip_sum_" + n) for n, o, t in zip(early + ("small",), own_sums, landed)}
    late_landed = _chip_wait(*flight["chip"], grad_x)
    for n, o, t in zip(late, flight["own"], late_landed):
        halves[n] = _chip_sum(o, t, "chip_sum_" + n)
    *others, small_all = _final_exchange([halves[n] for n in BIG], halves["small"])
    small_all = small_all.reshape(N_DEV * SMALL_ROWS, D_MODEL)
    grads, row = {}, 0
    for n in SMALL:
        shape = (3, 4 * w[n].shape[-1]) if n == "conv_w" else w[n].shape[1:] if n != "norm_final_g" else w[n].shape
        size = math.prod(shape)
        grads[n] = small_all[row:row + _tile_rows(size)].reshape(-1)[:size].reshape(shape)
        row += _tile_rows(size)
    loss = small_all[row, 0]
    cw = w["conv_w"].shape[-1]
    grads["conv_w"] = lax.dynamic_slice_in_dim(grads["conv_w"], xy * cw, cw, axis=1)
    grads = {n: _view(n, grads[n].reshape(w[n].shape)) for n in SMALL}
    wv, mv, vv = ({n: _view(n, t[n]) for n in WEIGHTS} for t in (w, m, v))

    delta, new_m, new_v = {}, {}, {}
    for n, other in zip(BIG, others):
        two_d = lambda t: t.reshape(t.shape[-2:])
        grads[n], delta[n], new_m[n], new_v[n] = _adamw_halves(
            two_d(wv[n]), halves[n], other, two_d(mv[n]), two_d(vv[n]), place, "adamw_" + n)
    for group, name in ((("b_re", "b_im"), "adamw_b"), (tuple(n for n in SMALL if n not in ("b_re", "b_im")), "adamw_small")):
        row = lambda t: t.reshape(1, -1) if t.ndim == 1 else t
        d_, m_, v_ = _adamw_many(*[[row(t[n]) for n in group] for t in (wv, grads, mv, vv)], name)
        for n, dn, mn, vn in zip(group, d_, m_, v_):
            delta[n], new_m[n], new_v[n] = (t.reshape(wv[n].shape) for t in (dn, mn, vn))
    natural = lambda t: [_unview(n, t[n].reshape(wv[n].shape)) for n in WEIGHTS]
    return (loss, grad_x[None], *natural(grads), *natural(delta), *natural(new_m), *natural(new_v))
```

```python
import functools
import math

import jax
import jax.numpy as jnp
import numpy as np
from jax import lax
from jax.experimental import pallas as pl
from jax.experimental.pallas import tpu as pltpu

F32 = jnp.float32
BF16 = jnp.bfloat16

D_MODEL = 1024
N_Q_HEADS = 8
N_KV_HEADS = 2
HEAD_DIM = 64
ATTN_WIDTH = 512
KV_WIDTH = 128
QKV_WIDTH = ATTN_WIDTH + 2 * KV_WIDTH
WINDOW = 128
BLOCK = 128
ROPE_DIM = 16
ROPE_THETA = 500000.0
SSM_WIDTH = 512
SSM_GROUP = 16
N_SSM_GROUPS = 32
SSM_STATE = 64
IN_WIDTH = 1280
D_FF = 2816
EPS = 1e-6
ADAM_LR = 0.001
ADAM_B1 = 0.9
ADAM_B2 = 0.999
ADAM_EPS = 1e-08
ADAM_WD = 0.01
ADAM_STEP = 10

VMEM_BYTES_V7X = 64 * 1024 * 1024
SUBLANES = 8
LANES = 128
SSM_CB = 4
SSM_CH = 128
SSM_ST = 512
N_SEG = SUBLANES

NN = (((1,), (0,)), ((), ()))
NT = (((1,), (1,)), ((), ()))
TN = (((0,), (0,)), ((), ()))


def _params(sem=None, vmem_mb=48):
    limit = vmem_mb * 1024 * 1024
    assert limit < VMEM_BYTES_V7X
    return pltpu.CompilerParams(dimension_semantics=sem, vmem_limit_bytes=limit)


def _dg(a, b, dims):
    return lax.dot_general(a, b, dims, preferred_element_type=F32)


def _sigmoid(x):
    return 1.0 / (1.0 + jnp.exp(-x))


_SQRT_HALF = 0.7071067811865476
_INV_SQRT_2PI = 0.3989422804014327


def _gelu(x):
    return 0.5 * x * (1.0 + lax.erf(x * _SQRT_HALF))


def _gelu_grad(x):
    return 0.5 * (1.0 + lax.erf(x * _SQRT_HALF)) + x * (_INV_SQRT_2PI * jnp.exp(-0.5 * x * x))


def _mm_tn(a, b, tm, tn, name):
    k, m = a.shape
    n = b.shape[1]

    def body(a_ref, b_ref, o_ref):
        o_ref[...] = _dg(a_ref[...], b_ref[...], TN)

    return pl.pallas_call(
        body, grid=(m // tm, n // tn),
        in_specs=[pl.BlockSpec((k, tm), lambda i, j: (0, i)), pl.BlockSpec((k, tn), lambda i, j: (0, j))],
        out_specs=pl.BlockSpec((tm, tn), lambda i, j: (i, j)),
        out_shape=jax.ShapeDtypeStruct((m, n), F32), name=name,
        compiler_params=_params(("parallel", "parallel")),
    )(a, b)


def _mm_nn_cols(a, b4, tm, name):
    m, k = a.shape
    s, _, n = b4.shape

    def body(a_ref, b_ref, o_ref):
        o_ref[...] = _dg(a_ref[...], b_ref[...], NN)

    return pl.pallas_call(
        body, grid=(m // tm, s),
        in_specs=[pl.BlockSpec((tm, k), lambda i, j: (i, 0)), pl.BlockSpec((None, k, n), lambda i, j: (j, 0, 0))],
        out_specs=pl.BlockSpec((tm, n), lambda i, j: (i, j)),
        out_shape=jax.ShapeDtypeStruct((m, s * n), F32), name=name,
        compiler_params=_params(("parallel", "parallel")),
    )(a, b4)


def _mm_tn_cols(a, b2, s, tm, name):
    k, m = a.shape
    h, _, wide = b2.shape
    per = s // h
    n = wide // per

    def body(a_ref, b_ref, o_ref):
        o_ref[...] = _dg(a_ref[...], b_ref[...], TN)

    return pl.pallas_call(
        body, grid=(s, m // tm),
        in_specs=[pl.BlockSpec((k, tm), lambda j, i: (0, i)),
                  pl.BlockSpec((None, k, n), lambda j, i: (j // per, 0, j % per))],
        out_specs=pl.BlockSpec((None, tm, n), lambda j, i: (j, i, 0)),
        out_shape=jax.ShapeDtypeStruct((s, m, n), F32), name=name,
        compiler_params=_params(("parallel", "parallel")),
    )(a, b2)


TM_EW = 256


def _rms_bwd_vals(xv, gv, dy):
    r = lax.rsqrt(jnp.mean(xv * xv, axis=-1, keepdims=True) + EPS)
    xh = xv * r
    dxh = dy * gv
    dx = r * (dxh - xh * jnp.mean(dxh * xh, axis=-1, keepdims=True))
    return dx, dy * xh


TM_FUSED = 512


def _rms_vals(xv, gv):
    return xv * lax.rsqrt(jnp.mean(xv * xv, axis=-1, keepdims=True) + EPS) * gv


def _rope_blocks(src, dst, c, lo, hi):
    nq = ATTN_WIDTH // LANES
    for blk in range(nq + 1):
        t = src[:, blk * LANES:(blk + 1) * LANES]
        dst[:, blk * LANES:(blk + 1) * LANES] = (
            t * c + pltpu.roll(t, LANES - 8, 1) * lo + pltpu.roll(t, 8, 1) * hi).astype(BF16)
    dst[:, (nq + 1) * LANES:] = src[:, (nq + 1) * LANES:].astype(BF16)


def _rms_mm_rope(x, g, wt, tabs, name):
    l, d = x.shape
    n = wt.shape[0]

    def body(x_ref, g_ref, w_ref, c_ref, lo_ref, hi_ref, h_ref, qkv_ref, u_ref):
        h = _rms_vals(x_ref[...], g_ref[...]).astype(BF16)
        h_ref[...] = h
        out = _dg(h, w_ref[...], NT)
        _rope_blocks(out[:, :QKV_WIDTH], qkv_ref, c_ref[...], lo_ref[...], hi_ref[...])
        u_ref[...] = out[:, QKV_WIDTH:]

    row = lambda width: pl.BlockSpec((TM_FUSED, width), lambda i: (i, 0))
    return pl.pallas_call(
        body, grid=(l // TM_FUSED,),
        in_specs=[row(d), pl.BlockSpec((1, d), lambda i: (0, 0)), pl.BlockSpec((n, d), lambda i: (0, 0)),
                  row(LANES), row(LANES), row(LANES)],
        out_specs=[row(d), row(QKV_WIDTH), row(n - QKV_WIDTH)],
        out_shape=[jax.ShapeDtypeStruct((l, d), BF16), jax.ShapeDtypeStruct((l, QKV_WIDTH), BF16),
                   jax.ShapeDtypeStruct((l, n - QKV_WIDTH), F32)],
        name=name, compiler_params=_params(("parallel",)),
    )(x, g, wt, *tabs)


def _mix_mm_res_rms(attn, ys, g_attn, g_ssm, b, res, g, name):
    l, w = attn.shape
    d = b.shape[1]

    def body(a_ref, y_ref, ga_ref, gs_ref, b_ref, r_ref, g_ref, m_ref, x_ref, h_ref):
        m_ref[:, :w] = _rms_vals(a_ref[...], ga_ref[...]).astype(BF16)
        m_ref[:, w:] = _rms_vals(y_ref[...], gs_ref[...]).astype(BF16)
        xv = r_ref[...] + _dg(m_ref[...], b_ref[...], NN)
        x_ref[...] = xv
        h_ref[...] = _rms_vals(xv, g_ref[...]).astype(BF16)

    row = lambda width: pl.BlockSpec((TM_FUSED, width), lambda i: (i, 0))
    vec = lambda width: pl.BlockSpec((1, width), lambda i: (0, 0))
    return pl.pallas_call(
        body, grid=(l // TM_FUSED,),
        in_specs=[row(w), row(w), vec(w), vec(w), pl.BlockSpec((2 * w, d), lambda i: (0, 0)), row(d), vec(d)],
        out_specs=[row(2 * w), row(d), row(d)],
        out_shape=[jax.ShapeDtypeStruct((l, 2 * w), BF16), jax.ShapeDtypeStruct((l, d), F32),
                   jax.ShapeDtypeStruct((l, d), BF16)],
        name=name, compiler_params=_params(("parallel",)),
    )(attn, ys, g_attn, g_ssm, b, res, g)


def _mm_res_loss(a, b, res, g, target):
    l, k = a.shape
    d = b.shape[1]

    def body(a_ref, b_ref, r_ref, g_ref, t_ref, loss_ref, dx_ref, dxb_ref, dg_ref):
        xv = r_ref[...] + _dg(a_ref[...], b_ref[...], NN)
        gv = g_ref[...]
        r = lax.rsqrt(jnp.mean(xv * xv, axis=-1, keepdims=True) + EPS)
        xh = xv * r
        e = xh * gv - t_ref[...]
        part = jnp.sum(jnp.sum(e * e, axis=1, keepdims=True), axis=0, keepdims=True) * (0.5 / d)
        dy = e * (1.0 / d)
        dxh = dy * gv
        dx = r * (dxh - xh * jnp.mean(dxh * xh, axis=-1, keepdims=True))
        dx_ref[...] = dx
        dxb_ref[...] = dx.astype(BF16)

        @pl.when(pl.program_id(0) == 0)
        def _():
            dg_ref[...] = jnp.zeros_like(dg_ref)
            loss_ref[...] = jnp.zeros_like(loss_ref)

        dg_ref[...] += jnp.sum(dy * xh, axis=0, keepdims=True)
        loss_ref[...] += part

    row = lambda width: pl.BlockSpec((TM_FUSED, width), lambda i: (i, 0))
    vec = pl.BlockSpec((1, d), lambda i: (0, 0))
    return pl.pallas_call(
        body, grid=(l // TM_FUSED,),
        in_specs=[row(k), pl.BlockSpec((k, d), lambda i: (0, 0)), row(d), vec, row(d)],
        out_specs=[pl.BlockSpec((1, 1), lambda i: (0, 0)), row(d), row(d), vec],
        out_shape=[jax.ShapeDtypeStruct((1, 1), F32), jax.ShapeDtypeStruct((l, d), F32),
                   jax.ShapeDtypeStruct((l, d), BF16), jax.ShapeDtypeStruct((1, d), F32)],
        name="mm_down_loss", compiler_params=_params(("arbitrary",)),
    )(a, b, res, g, target)


def _mm_rms_bwd(a, b, a_spec, b_spec, matmul, x, g, res, name):
    l, d = x.shape

    def body(a_ref, b_ref, x_ref, g_ref, res_ref, dx_ref, dxb_ref, dg_ref):
        dx, dgr = _rms_bwd_vals(x_ref[...], g_ref[...], matmul(a_ref, b_ref))
        dx = dx + res_ref[...]
        dx_ref[...] = dx
        dxb_ref[...] = dx.astype(BF16)

        @pl.when(pl.program_id(0) == 0)
        def _():
            dg_ref[...] = jnp.zeros_like(dg_ref)

        dg_ref[...] += jnp.sum(dgr, axis=0, keepdims=True)

    row = pl.BlockSpec((TM_FUSED, d), lambda i: (i, 0))
    vec = pl.BlockSpec((1, d), lambda i: (0, 0))
    return pl.pallas_call(
        body, grid=(l // TM_FUSED,), in_specs=[a_spec, b_spec, row, vec, row], out_specs=[row, row, vec],
        out_shape=[jax.ShapeDtypeStruct((l, d), F32), jax.ShapeDtypeStruct((l, d), BF16),
                   jax.ShapeDtypeStruct((1, d), F32)],
        name=name, compiler_params=_params(("arbitrary",)),
    )(a, b, x, g, res)


def _mm_nn_rms_bwd(a, b, x, g, res, name):
    return _mm_rms_bwd(a, b, pl.BlockSpec((TM_FUSED, a.shape[1]), lambda i: (i, 0)),
                       pl.BlockSpec(b.shape, lambda i: (0, 0)),
                       lambda a_ref, b_ref: _dg(a_ref[...], b_ref[...], NN), x, g, res, name)


def _mm_cols_rms_bwd(a2, b4, x, g, res, name):
    h, _, wide = a2.shape
    s, _, n = b4.shape
    per = s // h

    def matmul(a_ref, b_ref):
        acc = None
        for j in range(s):
            part = _dg(a_ref[j // per, :, (j % per) * n:(j % per + 1) * n], b_ref[j], NT)
            acc = part if acc is None else acc + part
        return acc

    return _mm_rms_bwd(a2, b4, pl.BlockSpec((h, TM_FUSED, wide), lambda i: (0, i, 0)),
                       pl.BlockSpec(b4.shape, lambda i: (0, 0, 0), pipeline_mode=pl.Buffered(1)),
                       matmul, x, g, res, name)


def _mm_mix_bwd(dx, b, attn, ys, g_attn, g_ssm, name):
    l, w = attn.shape
    d = dx.shape[1]

    def body(dx_ref, b_ref, a_ref, y_ref, ga_ref, gs_ref, da_ref, dy_ref, dga_ref, dgs_ref):
        @pl.when(pl.program_id(0) == 0)
        def _():
            dga_ref[...] = jnp.zeros_like(dga_ref)
            dgs_ref[...] = jnp.zeros_like(dgs_ref)

        dm = _dg(dx_ref[...], b_ref[...], NT)
        for src, gr, off, dst, dgr in ((a_ref, ga_ref, 0, da_ref, dga_ref), (y_ref, gs_ref, w, dy_ref, dgs_ref)):
            dxv, dg_rows = _rms_bwd_vals(src[...], gr[...], dm[:, off:off + w])
            dst[...] = dxv
            dgr[...] += jnp.sum(dg_rows, axis=0, keepdims=True)

    row = lambda width: pl.BlockSpec((TM_FUSED, width), lambda i: (i, 0))
    vec = pl.BlockSpec((1, w), lambda i: (0, 0))
    return pl.pallas_call(
        body, grid=(l // TM_FUSED,),
        in_specs=[row(d), pl.BlockSpec((2 * w, d), lambda i: (0, 0)), row(w), row(w), vec, vec],
        out_specs=[row(w), row(w), vec, vec],
        out_shape=[jax.ShapeDtypeStruct((l, w), F32), jax.ShapeDtypeStruct((l, w), F32),
                   jax.ShapeDtypeStruct((1, w), F32), jax.ShapeDtypeStruct((1, w), F32)],
        name=name, compiler_params=_params(("arbitrary",)),
    )(dx, b, attn, ys, g_attn, g_ssm)


def _rope_tables(l):
    half = ROPE_DIM // 2
    f32 = np.float32
    inv_freq = np.power(f32(ROPE_THETA), -np.arange(half, dtype=f32) / f32(half))
    ang = np.arange(l, dtype=f32)[:, None] * inv_freq[None, :]
    cos, sin = np.cos(ang), np.sin(ang)
    ones = np.ones((l, HEAD_DIM - ROPE_DIM), f32)
    zeros = np.zeros((l, HEAD_DIM - ROPE_DIM), f32)
    zh = np.zeros((l, half), f32)
    c = np.concatenate([cos, cos, ones], axis=1)
    s_lo = np.concatenate([-sin, zh, zeros], axis=1)
    s_hi = np.concatenate([zh, sin, zeros], axis=1)
    return tuple(jnp.asarray(np.tile(t, (1, LANES // HEAD_DIM)), F32) for t in (c, s_lo, s_hi))


def _rope_bwd(dq, dkv, du_ssm, dpre, d_skip, tabs):
    l = dq.shape[0]
    nq = ATTN_WIDTH // LANES

    def body(dq_ref, dkv_ref, du_ref, dpre_ref, ds_ref, c_ref, lo_ref, hi_ref, o_ref):
        c, lo, hi = c_ref[...], lo_ref[...], hi_ref[...]
        for blk in range(nq + 1):
            t = dq_ref[:, blk * LANES:(blk + 1) * LANES] if blk < nq else dkv_ref[:, :KV_WIDTH]
            g = t * c + pltpu.roll(t * lo, 8, 1) + pltpu.roll(t * hi, LANES - 8, 1)
            o_ref[:, blk * LANES:(blk + 1) * LANES] = g.astype(BF16)
        o_ref[:, (nq + 1) * LANES:QKV_WIDTH] = dkv_ref[:, KV_WIDTH:].astype(BF16)
        o_ref[:, QKV_WIDTH:] = (du_ref[...] + dpre_ref[...] * ds_ref[...]).astype(BF16)

    tab = pl.BlockSpec((TM_EW, LANES), lambda i: (i, 0))
    wide = pl.BlockSpec((TM_EW, SSM_WIDTH), lambda i: (i, 0))
    return pl.pallas_call(
        body, grid=(l // TM_EW,),
        in_specs=[wide, pl.BlockSpec((TM_EW, 2 * KV_WIDTH), lambda i: (i, 0)), wide, wide,
                  pl.BlockSpec((1, SSM_WIDTH), lambda i: (0, 0)), tab, tab, tab],
        out_specs=pl.BlockSpec((TM_EW, IN_WIDTH), lambda i: (i, 0)),
        out_shape=jax.ShapeDtypeStruct((l, IN_WIDTH), BF16), name="rope_bwd",
        compiler_params=_params(("parallel",)),
    )(dq, dkv, du_ssm, dpre, d_skip, *tabs)


_Q_COLS = ATTN_WIDTH // LANES
_SCALE = HEAD_DIM ** -0.5
_NEG = -1e30


def _window_specs(nb, width, col):
    return [
        pl.BlockSpec((BLOCK, width), lambda n: (jnp.maximum(n - 1, 0), col)),
        pl.BlockSpec((BLOCK, width), lambda n: (n, col)),
        pl.BlockSpec((BLOCK, width), lambda n: (jnp.minimum(n + 1, nb - 1), col)),
    ]


def _stacked_sink(sink_ref, heads):
    rid = lax.broadcasted_iota(jnp.int32, (len(heads) * BLOCK, 1), 0)
    sk = jnp.full(rid.shape, sink_ref[0, heads[-1]], F32)
    for g in range(len(heads) - 2, -1, -1):
        sk = jnp.where(rid < (g + 1) * BLOCK, sink_ref[0, heads[g]], sk)
    return sk


def _attn_fwd(qkv, sink):
    l = qkv.shape[0]
    nb = l // BLOCK
    grp = N_Q_HEADS // N_KV_HEADS

    def body(sink_ref, q_ref, k0, k1, k2, v0, v1, v2, o_ref, lse_ref):
        n = pl.program_id(0)
        q = q_ref[...]
        kw = jnp.concatenate([k0[...], k1[...], k2[...]], axis=0)
        vw = jnp.concatenate([v0[...], v1[...], v2[...]], axis=0)
        row = lax.broadcasted_iota(jnp.int32, (grp * BLOCK, 3 * BLOCK), 0)
        col = lax.broadcasted_iota(jnp.int32, (grp * BLOCK, 3 * BLOCK), 1)
        valid = jnp.abs(col - BLOCK - (row & (BLOCK - 1))) <= WINDOW
        valid &= jnp.logical_not((n == 0) & (col < BLOCK))
        valid &= jnp.logical_not((n == nb - 1) & (col >= 2 * BLOCK))
        for hk in range(N_KV_HEADS):
            heads = range(hk * grp, (hk + 1) * grp)
            qs = jnp.concatenate([q[:, h * HEAD_DIM:(h + 1) * HEAD_DIM] for h in heads], axis=0)
            kh = kw[:, hk * HEAD_DIM:(hk + 1) * HEAD_DIM]
            vh = vw[:, hk * HEAD_DIM:(hk + 1) * HEAD_DIM]
            s = jnp.where(valid, _dg(qs, kh, NT) * _SCALE, _NEG)
            sk = _stacked_sink(sink_ref, heads)
            m = jnp.maximum(jnp.max(s, axis=1, keepdims=True), sk)
            p = jnp.exp(s - m)
            denom = jnp.sum(p, axis=1, keepdims=True) + jnp.exp(sk - m)
            o = _dg((p / denom).astype(BF16), vh, NN)
            lse = m + jnp.log(denom)
            for g, h in enumerate(heads):
                o_ref[:, h * HEAD_DIM:(h + 1) * HEAD_DIM] = o[g * BLOCK:(g + 1) * BLOCK]
                lse_ref[:, h:h + 1] = lse[g * BLOCK:(g + 1) * BLOCK]

    return pl.pallas_call(
        body, grid=(nb,),
        in_specs=[pl.BlockSpec(memory_space=pltpu.SMEM),
                  pl.BlockSpec((BLOCK, ATTN_WIDTH), lambda n: (n, 0))]
        + _window_specs(nb, KV_WIDTH, _Q_COLS) + _window_specs(nb, KV_WIDTH, _Q_COLS + 1),
        out_specs=[pl.BlockSpec((BLOCK, ATTN_WIDTH), lambda n: (n, 0)),
                   pl.BlockSpec((BLOCK, N_Q_HEADS), lambda n: (n, 0))],
        out_shape=[jax.ShapeDtypeStruct((l, ATTN_WIDTH), F32), jax.ShapeDtypeStruct((l, N_Q_HEADS), F32)],
        name="attn_fwd", compiler_params=_params(("parallel",)),
    )(sink, qkv, qkv, qkv, qkv, qkv, qkv, qkv)


def _attn_bwd(qkv, attn, dattn, lse, sink):
    l = qkv.shape[0]
    nb = l // BLOCK
    grp = N_Q_HEADS // N_KV_HEADS
    win = 3 * BLOCK

    def body(sink_ref, q_ref, k0, k1, k2, v0, v1, v2, o_ref, d_ref, l_ref, dq_ref, dkv_ref, dsink_ref, ring_ref):
        n = pl.program_id(0)

        @pl.when(n == 0)
        def _():
            dsink_ref[...] = jnp.zeros_like(dsink_ref)
            ring_ref[...] = jnp.zeros_like(ring_ref)

        @pl.when(n < nb)
        def _():
            first, last = n == 0, n == nb - 1
            cat = lambda a, b, c: jnp.concatenate([a[...], b[...], c[...]], axis=0)
            q, kw, vw = q_ref[...], cat(k0, k1, k2), cat(v0, v1, v2)
            dov = d_ref[...]
            prod = o_ref[...] * dov
            dob = dov.astype(BF16)
            lse = l_ref[...]
            row = lax.broadcasted_iota(jnp.int32, (grp * BLOCK, win), 0)
            col = lax.broadcasted_iota(jnp.int32, (grp * BLOCK, win), 1)
            valid = jnp.abs(col - BLOCK - (row & (BLOCK - 1))) <= WINDOW
            valid &= jnp.logical_not(first & (col < BLOCK))
            valid &= jnp.logical_not(last & (col >= 2 * BLOCK))

            dsink_parts, dks, dvs = [], [], []
            for hk in range(N_KV_HEADS):
                heads = range(hk * grp, (hk + 1) * grp)
                ksl = slice(hk * HEAD_DIM, (hk + 1) * HEAD_DIM)
                hsl = [slice(h * HEAD_DIM, (h + 1) * HEAD_DIM) for h in heads]
                stack = lambda parts: jnp.concatenate(parts, axis=0)
                qs = stack([q[:, s_] for s_ in hsl])
                dos = stack([dob[:, s_] for s_ in hsl])
                deltas = stack([jnp.sum(prod[:, s_], axis=1, keepdims=True) for s_ in hsl])
                lses = stack([lse[:, h:h + 1] for h in heads])
                kh, vh = kw[:, ksl], vw[:, ksl]
                s = jnp.where(valid, _dg(qs, kh, NT) * _SCALE, _NEG)
                p = jnp.exp(s - lses)
                dp = _dg(dos, vh, NT)
                ds = (p * (dp - deltas) * _SCALE).astype(BF16)
                dq = _dg(ds, kh, NN)
                sink_rows = jnp.exp(_stacked_sink(sink_ref, heads) - lses) * deltas
                for g in range(grp):
                    dq_ref[:, hsl[g]] = dq[g * BLOCK:(g + 1) * BLOCK]
                    dsink_parts.append(jnp.sum(sink_rows[g * BLOCK:(g + 1) * BLOCK], axis=0, keepdims=True))
                dks.append(_dg(ds, qs, TN))
                dvs.append(_dg(p.astype(BF16), dos, TN))
            dsink_ref[...] -= jnp.concatenate(dsink_parts, axis=1)
            part = jnp.concatenate(dks + dvs, axis=1)
            ring_ref[(n + 2) % 3] += part[0:BLOCK]
            ring_ref[n % 3] += part[BLOCK:2 * BLOCK]
            ring_ref[(n + 1) % 3] = part[2 * BLOCK:]

        @pl.when(n >= 1)
        def _():
            dkv_ref[...] = ring_ref[(n + 2) % 3]

    centre = lambda n: jnp.minimum(n, nb - 1)
    window = lambda width, col: [
        pl.BlockSpec((BLOCK, width), lambda n: (jnp.maximum(centre(n) - 1, 0), col)),
        pl.BlockSpec((BLOCK, width), lambda n: (centre(n), col)),
        pl.BlockSpec((BLOCK, width), lambda n: (jnp.minimum(centre(n) + 1, nb - 1), col))]
    own = lambda width: pl.BlockSpec((BLOCK, width), lambda n: (centre(n), 0))
    return pl.pallas_call(
        body, grid=(nb + 1,),
        in_specs=[pl.BlockSpec(memory_space=pltpu.SMEM), own(ATTN_WIDTH)]
        + window(KV_WIDTH, _Q_COLS) + window(KV_WIDTH, _Q_COLS + 1)
        + [own(ATTN_WIDTH), own(ATTN_WIDTH), own(N_Q_HEADS)],
        out_specs=[own(ATTN_WIDTH), pl.BlockSpec((BLOCK, 2 * KV_WIDTH), lambda n: (jnp.maximum(n - 1, 0), 0)),
                   pl.BlockSpec((1, N_Q_HEADS), lambda n: (0, 0))],
        out_shape=[jax.ShapeDtypeStruct((l, ATTN_WIDTH), F32), jax.ShapeDtypeStruct((l, 2 * KV_WIDTH), F32),
                   jax.ShapeDtypeStruct((1, N_Q_HEADS), F32)],
        scratch_shapes=[pltpu.VMEM((3, BLOCK, 2 * KV_WIDTH), F32)],
        name="attn_bwd", compiler_params=_params(("arbitrary",)),
    )(sink, qkv, qkv, qkv, qkv, qkv, qkv, qkv, attn, dattn, lse)


def _ssm_disc(a_re, a_im, log_step, b_re, b_im):
    step = jnp.exp(log_step)[..., None]
    mag = jnp.exp(a_re * step)
    lb_re, lb_im = mag * jnp.cos(a_im * step), mag * jnp.sin(a_im * step)
    nr, ni = lb_re - 1.0, lb_im
    den = a_re * a_re + a_im * a_im
    f_re = ((nr * a_re + ni * a_im) / den)[..., None]
    f_im = ((ni * a_re - nr * a_im) / den)[..., None]
    return lb_re, lb_im, f_re * b_re - f_im * b_im, f_re * b_im + f_im * b_re


def _ssm_pack(lb_re, lb_im, bb_re, bb_im, c_re, c_im):
    eye = jnp.eye(SSM_CH // SSM_GROUP, dtype=F32)
    ng = SSM_CH // SSM_GROUP

    def diag_b(bb):
        t = bb.reshape(2, SSM_CB, ng, SSM_STATE, SSM_GROUP)
        return jnp.einsum('dkgpc,gh->dkgchp', t, eye).reshape(2, SSM_CB, SSM_CH, SSM_ST)

    def diag_c(cc):
        t = cc.reshape(2, SSM_CB, ng, SSM_GROUP, SSM_STATE)
        return jnp.einsum('dkgcp,gh->dkhpgc', t, eye).reshape(2, SSM_CB, SSM_ST, SSM_CH)

    bcat = jnp.concatenate([diag_b(bb_re), diag_b(bb_im)], axis=-1)
    ccat = jnp.concatenate([diag_c(c_re), -diag_c(c_im)], axis=-2)
    lam_re = lb_re.reshape(2, SSM_CB, 1, SSM_ST)
    lam_im = lb_im.reshape(2, SSM_CB, 1, SSM_ST)
    return bcat, ccat, lam_re, lam_im


def _ssm_unpack(dbcat, dccat, dlam_re, dlam_im):
    ng = SSM_CH // SSM_GROUP
    eye = jnp.eye(ng, dtype=F32)

    def undiag_b(t):
        t = t.reshape(2, SSM_CB, ng, SSM_GROUP, ng, SSM_STATE)
        return jnp.einsum('dkgchp,gh->dkgpc', t, eye).reshape(2, N_SSM_GROUPS, SSM_STATE, SSM_GROUP)

    def undiag_c(t):
        t = t.reshape(2, SSM_CB, ng, SSM_STATE, ng, SSM_GROUP)
        return jnp.einsum('dkhpgc,gh->dkgcp', t, eye).reshape(2, N_SSM_GROUPS, SSM_GROUP, SSM_STATE)

    dbb_re, dbb_im = undiag_b(dbcat[..., :SSM_ST]), undiag_b(dbcat[..., SSM_ST:])
    dc_re, dc_im = undiag_c(dccat[:, :, :SSM_ST]), -undiag_c(dccat[:, :, SSM_ST:])
    shape = (2, N_SSM_GROUPS, SSM_STATE)
    return dlam_re.reshape(shape), dlam_im.reshape(shape), dbb_re, dbb_im, dc_re, dc_im


def _to_segments(t):
    l, w = t.shape
    return t.reshape(N_SEG, l // N_SEG, w).transpose(1, 0, 2).reshape(l, w)


def _from_segments(t):
    l, w = t.shape
    return t.reshape(l // N_SEG, N_SEG, w).transpose(1, 0, 2).reshape(l, w)


SSM_RC = 256
SSM_JC = SSM_RC // N_SEG
_RE, _IM = pl.ds(0, SSM_ST), pl.ds(SSM_ST, SSM_ST)


def _cfma(ar, ai, xr, xi, br, bi):
    return ar * xr - ai * xi + br, ar * xi + ai * xr + bi


def _chunk_rows(ci, rev, nc):
    start = jnp.where(rev, (nc - 1 - ci) * SSM_RC, ci * SSM_RC)
    return pl.ds(pl.multiple_of(start, SSM_RC), SSM_RC)


def _scan_chunk(src, dst, ar, ai, rev, nj, ci, carry, prev_ref=None):
    def rows_of(staged, j, k):
        at = jnp.where(rev, SSM_JC - 1 - k, k) if staged else j
        return pl.ds(pl.multiple_of(at * N_SEG, N_SEG), N_SEG)

    for k in range(SSM_JC):
        jj = ci * SSM_JC + k
        j = jnp.where(rev, nj - 1 - jj, jj)
        rows = rows_of(src[1], j, k)
        nr, ni = _cfma(ar, ai, carry[0], carry[1], src[0][rows, _RE], src[0][rows, _IM])
        if dst is not None:
            rows = rows_of(dst[1], j, k)
            dst[0][rows, _RE] = nr
            dst[0][rows, _IM] = ni
        if prev_ref is None:
            carry = (nr, ni)
            continue
        jp = jnp.where(rev, j - 1, j + 1)
        if k == SSM_JC - 1:
            inside = jnp.where((jp >= 0) & (jp < nj), 1.0, 0.0)
            jp = jnp.clip(jp, 0, nj - 1)
        prow = pl.ds(pl.multiple_of(jp * N_SEG, N_SEG), N_SEG)
        xr, xi = prev_ref[prow, _RE], prev_ref[prow, _IM]
        sr, si = nr * xr + ni * xi, ni * xr - nr * xi
        if k == SSM_JC - 1:
            sr, si = inside * sr, inside * si
        carry = (nr, ni, carry[2] + sr, carry[3] + si)
    return carry


def _segment_inits(ar, ai, end_r, end_i, rev, nj):
    pr, pi = ar, ai
    for _ in range(int(math.log2(nj))):
        pr, pi = pr * pr - pi * pi, 2.0 * pr * pi
    seg = lax.broadcasted_iota(jnp.int32, end_r.shape, 0)
    zero = jnp.zeros_like(end_r)

    def chain(shift, keep):
        ir, ii = zero, zero
        for _ in range(N_SEG - 1):
            tr, ti = _cfma(pr, pi, ir, ii, end_r, end_i)
            ir = jnp.where(keep, pltpu.roll(tr, shift, 0), 0.0)
            ii = jnp.where(keep, pltpu.roll(ti, shift, 0), 0.0)
        return ir, ii

    up_r, up_i = chain(1, seg >= 1)
    dn_r, dn_i = chain(N_SEG - 1, seg <= N_SEG - 2)
    return jnp.where(rev, dn_r, up_r), jnp.where(rev, dn_i, up_i)


def _ssm_specs(l):
    act = pl.BlockSpec((l, SSM_CH), lambda k, d: (0, k))
    bmat = pl.BlockSpec((None, None, SSM_CH, 2 * SSM_ST), lambda k, d: (d, k, 0, 0))
    cmat = pl.BlockSpec((None, None, 2 * SSM_ST, SSM_CH), lambda k, d: (d, k, 0, 0))
    lam = pl.BlockSpec((None, None, 1, SSM_ST), lambda k, d: (d, k, 0, 0))
    return act, bmat, cmat, lam


def _ssm_fwd(u_seg, bcat, ccat, lam_re, lam_im):
    l = u_seg.shape[0]
    nj = l // N_SEG
    nc = l // SSM_RC

    def body(u_ref, b_ref, c_ref, lr_ref, li_ref, y_ref, keep_ref, xs_ref, stage0, stage1, keep_sem):
        k, d = pl.program_id(0), pl.program_id(1)
        rev = d == 1
        shape = (N_SEG, SSM_ST)
        ar, ai = jnp.broadcast_to(lr_ref[...], shape), jnp.broadcast_to(li_ref[...], shape)
        zero = jnp.zeros(shape, F32)

        def inputs(ci, stage):
            rows = _chunk_rows(ci, rev, nc)
            bu = _dg(u_ref[rows, :], b_ref[...], NN)
            stage[...] = bu
            xs_ref[rows, :] = bu

        def first(stage, ci, carry):
            return _scan_chunk((stage, True), None, ar, ai, rev, nj, ci, carry)

        def first_pass(t, carry):
            inputs(2 * t + 1, stage1)
            carry = first(stage0, 2 * t, carry)
            inputs(2 * t + 2, stage0)
            return first(stage1, 2 * t + 1, carry)

        inputs(0, stage0)
        carry = lax.fori_loop(0, nc // 2 - 1, first_pass, (zero, zero))
        inputs(nc - 1, stage1)
        carry = first(stage0, nc - 2, carry)
        end_r, end_i = first(stage1, nc - 1, carry)
        init = _segment_inits(ar, ai, end_r, end_i, rev, nj)

        @pl.when(d == 0)
        def _():
            y_ref[...] = jnp.zeros_like(y_ref)

        def outputs(ci):
            rows = _chunk_rows(ci, rev, nc)
            y_ref[rows, :] += _dg(xs_ref[rows, :].astype(BF16), c_ref[...], NN)
            pltpu.make_async_copy(xs_ref.at[rows], keep_ref.at[d, k, rows], keep_sem).start()

        def second(ci, carry):
            return _scan_chunk((xs_ref, False), (xs_ref, False), ar, ai, rev, nj, ci, carry)

        def second_pass(ci, carry):
            outputs(ci - 1)
            return second(ci, carry)

        lax.fori_loop(1, nc, second_pass, second(0, init))
        outputs(nc - 1)
        pltpu.make_async_copy(xs_ref, keep_ref.at[d, k], keep_sem).wait()

    act, bmat, cmat, lam = _ssm_specs(l)
    return pl.pallas_call(
        body, grid=(SSM_CB, 2), in_specs=[act, bmat, cmat, lam, lam], out_specs=[act, ANY],
        out_shape=[jax.ShapeDtypeStruct((l, SSM_WIDTH), F32),
                   jax.ShapeDtypeStruct((2, SSM_CB, l, 2 * SSM_ST), F32)],
        scratch_shapes=[pltpu.VMEM((l, 2 * SSM_ST), F32), pltpu.VMEM((SSM_RC, 2 * SSM_ST), F32),
                        pltpu.VMEM((SSM_RC, 2 * SSM_ST), F32), pltpu.SemaphoreType.DMA],
        name="ssm_fwd", compiler_params=_params(("parallel", "arbitrary"), vmem_mb=56),
    )(u_seg, bcat.astype(BF16), ccat.astype(BF16), lam_re, lam_im)


def _ssm_bwd(u_seg, dy_seg, states, bcat, ccat, lam_re, lam_im):
    l = u_seg.shape[0]
    nj = l // N_SEG
    nc = l // SSM_RC

    def body(u_ref, dy_ref, keep_ref, b_ref, c_ref, lr_ref, li_ref,
             du_ref, db_ref, dc_ref, dlr_ref, dli_ref, xs_ref, gs_ref, stage0, stage1, keep_sem):
        k, d = pl.program_id(0), pl.program_id(1)
        rev = d == 1
        back = jnp.logical_not(rev)
        shape = (N_SEG, SSM_ST)
        ar, ai = jnp.broadcast_to(lr_ref[...], shape), -jnp.broadcast_to(li_ref[...], shape)
        zero = jnp.zeros(shape, F32)
        fetch = pltpu.make_async_copy(keep_ref.at[d, k], xs_ref, keep_sem)
        fetch.start()

        def inputs(ci, stage):
            rows = _chunk_rows(ci, back, nc)
            dx = _dg(dy_ref[rows, :], c_ref[...], NT)
            stage[...] = dx
            gs_ref[rows, :] = dx

        def first(stage, ci, carry):
            return _scan_chunk((stage, True), None, ar, ai, back, nj, ci, carry)

        def first_pass(t, carry):
            inputs(2 * t + 1, stage1)
            carry = first(stage0, 2 * t, carry)
            inputs(2 * t + 2, stage0)
            return first(stage1, 2 * t + 1, carry)

        inputs(0, stage0)
        carry = lax.fori_loop(0, nc // 2 - 1, first_pass, (zero, zero))
        inputs(nc - 1, stage1)
        carry = first(stage0, nc - 2, carry)
        end_r, end_i = first(stage1, nc - 1, carry)
        init = _segment_inits(ar, ai, end_r, end_i, back, nj)
        fetch.wait()
        db_ref[...] = jnp.zeros_like(db_ref)
        dc_ref[...] = jnp.zeros_like(dc_ref)

        @pl.when(d == 0)
        def _():
            du_ref[...] = jnp.zeros_like(du_ref)

        def outputs(ci, stage):
            rows = _chunk_rows(ci, back, nc)
            g = stage[...].astype(BF16)
            dc_ref[...] += _dg(xs_ref[rows, :].astype(BF16), dy_ref[rows, :], TN)
            db_ref[...] += _dg(u_ref[rows, :], g, TN)
            du_ref[rows, :] += _dg(g, b_ref[...], NT)

        def second(ci, stage, carry):
            return _scan_chunk((gs_ref, False), (stage, True), ar, ai, back, nj, ci, carry, prev_ref=xs_ref)

        def second_pass(t, carry):
            outputs(2 * t, stage0)
            carry = second(2 * t + 1, stage1, carry)
            outputs(2 * t + 1, stage1)
            return second(2 * t + 2, stage0, carry)

        carry = lax.fori_loop(0, nc // 2 - 1, second_pass, second(0, stage0, init + (zero, zero)))
        outputs(nc - 2, stage0)
        gr, gi, acc_r, acc_i = second(nc - 1, stage1, carry)
        outputs(nc - 1, stage1)

        seg = lax.broadcasted_iota(jnp.int32, shape, 0)
        jb = jnp.where(rev, nj - 1, 0)
        erow = pl.ds(pl.multiple_of((nj - 1 - jb) * N_SEG, N_SEG), N_SEG)

        def before(t):
            up = jnp.where(seg >= 1, pltpu.roll(t, 1, 0), 0.0)
            down = jnp.where(seg <= N_SEG - 2, pltpu.roll(t, N_SEG - 1, 0), 0.0)
            return jnp.where(rev, down, up)

        init_r, init_i = before(xs_ref[erow, _RE]), before(xs_ref[erow, _IM])
        acc_r = acc_r + gr * init_r + gi * init_i
        acc_i = acc_i + gi * init_r - gr * init_i
        dlr_ref[...] = jnp.sum(acc_r, axis=0, keepdims=True)
        dli_ref[...] = jnp.sum(acc_i, axis=0, keepdims=True)

    act, bmat, cmat, lam = _ssm_specs(l)
    return pl.pallas_call(
        body, grid=(SSM_CB, 2), in_specs=[act, act, ANY, bmat, cmat, lam, lam],
        out_specs=[act, bmat, cmat, lam, lam],
        out_shape=[jax.ShapeDtypeStruct((l, SSM_WIDTH), F32),
                   jax.ShapeDtypeStruct(bcat.shape, F32), jax.ShapeDtypeStruct(ccat.shape, F32),
                   jax.ShapeDtypeStruct(lam_re.shape, F32), jax.ShapeDtypeStruct(lam_im.shape, F32)],
        scratch_shapes=[pltpu.VMEM((l, 2 * SSM_ST), F32), pltpu.VMEM((l, 2 * SSM_ST), F32),
                        pltpu.VMEM((SSM_RC, 2 * SSM_ST), F32), pltpu.VMEM((SSM_RC, 2 * SSM_ST), F32),
                        pltpu.SemaphoreType.DMA],
        name="ssm_bwd", compiler_params=_params(("parallel", "arbitrary"), vmem_mb=58),
    )(u_seg, dy_seg, states, bcat.astype(BF16), ccat.astype(BF16), lam_re, lam_im)


def _glu_fwd(y_ssm, u, d_skip, w_glu):
    l, w = u.shape

    def body(y_ref, u_ref, d_ref, w_ref, pre_ref, s_ref, ys_ref):
        pre = y_ref[...] + d_ref[...] * u_ref[...]
        z = _gelu(pre)
        s = _dg(z.astype(BF16), w_ref[...], NN)
        pre_ref[...] = pre
        s_ref[...] = s
        ys_ref[...] = z * _sigmoid(s)

    row = pl.BlockSpec((TM_EW, w), lambda i: (i, 0))
    out = jax.ShapeDtypeStruct((l, w), F32)
    return pl.pallas_call(
        body, grid=(l // TM_EW,),
        in_specs=[row, row, pl.BlockSpec((1, w), lambda i: (0, 0)), pl.BlockSpec((w, w), lambda i: (0, 0))],
        out_specs=[row, row, row], out_shape=[out, out, out], name="glu_fwd",
        compiler_params=_params(("parallel",)),
    )(y_ssm, u, d_skip, w_glu)


def _glu_bwd(pre, s, dys, u, d_skip, w_glu):
    l, w = u.shape

    def body(pre_ref, s_ref, dys_ref, u_ref, d_ref, w_ref, dpre_ref, z_ref, ds_ref, dd_ref):
        pre, dys = pre_ref[...], dys_ref[...]
        z = _gelu(pre)
        sig = _sigmoid(s_ref[...])
        ds = (dys * z * sig * (1.0 - sig)).astype(BF16)
        dz = dys * sig + _dg(ds, w_ref[...], NT)
        dpre = dz * _gelu_grad(pre)
        dpre_ref[...] = dpre
        z_ref[...] = z.astype(BF16)
        ds_ref[...] = ds

        @pl.when(pl.program_id(0) == 0)
        def _():
            dd_ref[...] = jnp.zeros_like(dd_ref)

        dd_ref[...] += jnp.sum(dpre * u_ref[...], axis=0, keepdims=True)

    row = pl.BlockSpec((TM_EW, w), lambda i: (i, 0))
    vec = pl.BlockSpec((1, w), lambda i: (0, 0))
    return pl.pallas_call(
        body, grid=(l // TM_EW,),
        in_specs=[row, row, row, row, vec, pl.BlockSpec((w, w), lambda i: (0, 0))],
        out_specs=[row, row, row, vec],
        out_shape=[jax.ShapeDtypeStruct((l, w), F32), jax.ShapeDtypeStruct((l, w), BF16),
                   jax.ShapeDtypeStruct((l, w), BF16), jax.ShapeDtypeStruct((1, w), F32)],
        name="glu_bwd", compiler_params=_params(("arbitrary",)),
    )(pre, s, dys, u, d_skip, w_glu)


TM_CV = 512
TC_CV = 256
TM_CF = 256
TC_CF = D_FF // 2
HALO = SUBLANES


def _conv_specs(l, col0, tm=TM_CV, tc=TC_CV):
    per = tm // HALO
    nh = l // HALO
    off = col0 // tc
    return [
        pl.BlockSpec((HALO, tc), lambda j, i: (jnp.maximum(i * per - 1, 0), j + off)),
        pl.BlockSpec((tm, tc), lambda j, i: (i, j + off)),
        pl.BlockSpec((HALO, tc), lambda j, i: (jnp.minimum((i + 1) * per, nh - 1), j + off)),
    ]


def _ext(prev_ref, mid_ref, next_ref, first, last):
    p = jnp.where(first, 0.0, prev_ref[...])
    n = jnp.where(last, 0.0, next_ref[...])
    return jnp.concatenate([p, mid_ref[...], n], axis=0)


def _shift_dn(t):
    return pltpu.roll(t, 1, 0)


def _shift_up(t):
    return pltpu.roll(t, t.shape[0] - 1, 0)


def _conv3(e, w_ref, b_ref):
    return w_ref[0:1, :] * _shift_dn(e) + w_ref[1:2, :] * e + w_ref[2:3, :] * _shift_up(e) + b_ref[...]


def _convffn_fwd(up_pre, conv_w, conv_b):
    l = up_pre.shape[0]
    tm, tc = TM_CF, TC_CF
    ni = l // tm
    wspec = lambda off: pl.BlockSpec((3, tc), lambda j, i: (0, j + off))
    bspec = lambda off: pl.BlockSpec((1, tc), lambda j, i: (0, j + off))
    voff = D_FF // tc

    def body(gp, gm, gn, vp, vm, vn, wg, bg, wv, bv, o_ref):
        i = pl.program_id(1)
        first, last = i == 0, i == ni - 1
        gate = _conv3(_ext(gp, gm, gn, first, last), wg, bg)[HALO:HALO + tm]
        val = _conv3(_ext(vp, vm, vn, first, last), wv, bv)[HALO:HALO + tm]
        o_ref[...] = (gate * _sigmoid(gate) * val).astype(BF16)

    return pl.pallas_call(
        body, grid=(D_FF // tc, ni),
        in_specs=_conv_specs(l, 0, tm, tc) + _conv_specs(l, D_FF, tm, tc)
        + [wspec(0), bspec(0), wspec(voff), bspec(voff)],
        out_specs=pl.BlockSpec((tm, tc), lambda j, i: (i, j)),
        out_shape=jax.ShapeDtypeStruct((l, D_FF), BF16), name="convffn_fwd",
        compiler_params=_params(("parallel", "parallel")),
    )(up_pre, up_pre, up_pre, up_pre, up_pre, up_pre, conv_w, conv_b, conv_w, conv_b)


HALO_B = 2 * SUBLANES


def _convffn_bwd(up_pre, dx2b, w_down, conv_w, conv_b):
    l = up_pre.shape[0]
    ni = l // TM_CV
    d = dx2b.shape[1]
    wspec = lambda off: pl.BlockSpec((3, TC_CV), lambda i, j: (0, j + off))
    bspec = lambda off: pl.BlockSpec((1, TC_CV), lambda i, j: (0, j + off))
    voff = D_FF // TC_CV
    swap = lambda spec: pl.BlockSpec(spec.block_shape, lambda i, j, f=spec.index_map: f(j, i))
    per, nh = TM_CV // HALO_B, l // HALO_B
    dx_specs = [pl.BlockSpec((HALO_B, d), lambda i, j: (jnp.maximum(i * per - 1, 0), 0)),
                pl.BlockSpec((TM_CV, d), lambda i, j: (i, 0)),
                pl.BlockSpec((HALO_B, d), lambda i, j: (jnp.minimum((i + 1) * per, nh - 1), 0))]

    def body(gp, gm, gn, vp, vm, vn, xp, xm, xn, wd, wg, bg, wv, bv, dup_ref, pg_ref, pv_ref):
        i = pl.program_id(0)
        first, last = i == 0, i == ni - 1
        ge, ve = _ext(gp, gm, gn, first, last), _ext(vp, vm, vn, first, last)
        zero = jnp.zeros((HALO_B, d), BF16)
        dx = jnp.concatenate([jnp.where(first, zero, xp[...]), xm[...], jnp.where(last, zero, xn[...])], axis=0)
        de = _dg(dx, wd[...], NT)[HALO_B - HALO:HALO_B + TM_CV + HALO]
        taps = [(_shift_dn(e), e, _shift_up(e)) for e in (ge, ve)]
        conv = lambda t, w_ref, b_ref: w_ref[0:1, :] * t[0] + w_ref[1:2, :] * t[1] + w_ref[2:3, :] * t[2] + b_ref[...]
        gate, val = conv(taps[0], wg, bg), conv(taps[1], wv, bv)
        sig = _sigmoid(gate)
        silu = gate * sig
        dgate = de * val * (sig + silu * (1.0 - sig))
        dval = de * silu
        mid = slice(HALO, HALO + TM_CV)
        rid = lax.broadcasted_iota(jnp.int32, (SUBLANES, TC_CV), 0)
        for half, (dup, tap, w_ref, p_ref) in enumerate(((dgate, taps[0], wg, pg_ref), (dval, taps[1], wv, pv_ref))):
            dpre = w_ref[0:1, :] * _shift_up(dup) + w_ref[1:2, :] * dup + w_ref[2:3, :] * _shift_dn(dup)
            dup_ref[half] = dpre[mid].astype(BF16)
            dm_ = dup[mid]
            sums = [jnp.sum(dm_ * t[mid], axis=0, keepdims=True) for t in tap]
            sums.append(jnp.sum(dm_, axis=0, keepdims=True))
            acc = jnp.zeros((SUBLANES, TC_CV), F32)
            for k, sk in enumerate(sums):
                acc = jnp.where(rid == k, sk, acc)
            p_ref[...] = acc

    par = pl.BlockSpec((None, SUBLANES, TC_CV), lambda i, j: (i, 0, j))
    dup, pg, pv = pl.pallas_call(
        body, grid=(ni, D_FF // TC_CV),
        in_specs=[swap(s) for s in _conv_specs(l, 0) + _conv_specs(l, D_FF)] + dx_specs
        + [pl.BlockSpec((TC_CV, d), lambda i, j: (j, 0)), wspec(0), bspec(0), wspec(voff), bspec(voff)],
        out_specs=[pl.BlockSpec((2, TM_CV, TC_CV), lambda i, j: (0, i, j)), par, par],
        out_shape=[jax.ShapeDtypeStruct((2, l, D_FF), BF16),
                   jax.ShapeDtypeStruct((ni, SUBLANES, D_FF), F32), jax.ShapeDtypeStruct((ni, SUBLANES, D_FF), F32)],
        name="convffn_bwd", compiler_params=_params(("parallel", "parallel")),
    )(up_pre, up_pre, up_pre, up_pre, up_pre, up_pre, dx2b, dx2b, dx2b, w_down, conv_w, conv_b, conv_w, conv_b)
    return dup, jnp.concatenate([jnp.sum(pg, axis=0), jnp.sum(pv, axis=0)], axis=1)


def _local_step(x, target, wb, sp, mixer_weights=None, late_weights=None, ffn_grads_ready=None,
                ffn_grads_next=None):
    l = x.shape[0]
    tabs = _rope_tables(l)
    disc = _ssm_disc(sp["a_re"], sp["a_im"], sp["log_step"], sp["b_re"], sp["b_im"])
    bcat, ccat, lam_re, lam_im = _ssm_pack(*disc, sp["c_re"], sp["c_im"])
    d_skip = sp["d_skip"].reshape(1, SSM_WIDTH)

    big = min(l, 1024)
    h, qkv, u = _rms_mm_rope(x, sp["norm_mix_g"], wb["w_in"], tabs, "mm_in")
    attn, lse = _attn_fwd(qkv, sp["sink"])
    u_seg = _to_segments(u).astype(BF16)
    y_seg, states = _ssm_fwd(u_seg, bcat, ccat, lam_re, lam_im)
    y_ssm = _from_segments(y_seg)
    if mixer_weights is not None:
        wb = dict(wb, **mixer_weights(attn))
    pre, s_glu, ys = _glu_fwd(y_ssm, u, d_skip, wb["w_glu"])
    mixed, x1, h2 = _mix_mm_res_rms(attn, ys, sp["norm_attn_g"], sp["norm_ssm_g"], wb["w_out"], x,
                                    sp["norm_ffn_g"], "mm_out")
    if late_weights is not None:
        wb = dict(wb, **late_weights(h2))
    up_pre = _mm_nn_cols(h2, wb["w_up"], big, "mm_up")
    conv_w = wb["conv_w"]
    act = _convffn_fwd(up_pre, conv_w, sp["conv_b"])
    loss, dx2, dx2b, d_final_g = _mm_res_loss(act, wb["w_down"], x1, sp["norm_final_g"].reshape(1, D_MODEL), target)

    g = {"norm_final_g": d_final_g.reshape(D_MODEL)}
    g["w_down"] = _mm_tn(act, dx2b, D_FF // 2, 512, "mm_down_dw")
    dup_pre, conv_par = _convffn_bwd(up_pre, dx2b, wb["w_down"], conv_w, sp["conv_b"])
    g["conv_w"], g["conv_b"] = conv_par[0:3], conv_par[3:4]
    g["w_up"] = _mm_tn_cols(h2, dup_pre, wb["w_up"].shape[0], 512, "mm_up_dw")
    zero = ffn_grads_ready(g["w_up"], g["w_down"]) if ffn_grads_ready is not None else 0.0
    dx1, dx1b, g["norm_ffn_g"] = _mm_cols_rms_bwd(dup_pre, wb["w_up"], x1, sp["norm_ffn_g"] + zero, dx2, "mm_up_dx")
    g["w_out"] = _mm_tn(mixed, dx1b, 1024, 1024, "mm_out_dw")
    zero = ffn_grads_next(g["w_out"]) if ffn_grads_next is not None else 0.0
    dattn, dys, g["norm_attn_g"], g["norm_ssm_g"] = _mm_mix_bwd(
        dx1b, wb["w_out"], attn, ys, sp["norm_attn_g"] + zero, sp["norm_ssm_g"], "mm_out_dx")
    dpre, zb, dsb, dd = _glu_bwd(pre, s_glu, dys, u, d_skip, wb["w_glu"])
    g["d_skip"] = dd.reshape(N_SSM_GROUPS, SSM_GROUP)
    g["w_glu"] = _mm_tn(zb, dsb, 512, 512, "mm_glu_dw")
    du_seg, dbcat, dccat, dlam_re, dlam_im = _ssm_bwd(u_seg, _to_segments(dpre).astype(BF16), states, bcat, ccat,
                                                      lam_re, lam_im)
    dlb_re, dlb_im, dbb_re, dbb_im, g["c_re"], g["c_im"] = _ssm_unpack(dbcat, dccat, dlam_re, dlam_im)
    _, disc_vjp = jax.vjp(_ssm_disc, sp["a_re"], sp["a_im"], sp["log_step"], sp["b_re"], sp["b_im"])
    g["a_re"], g["a_im"], g["log_step"], g["b_re"], g["b_im"] = disc_vjp((dlb_re, dlb_im, dbb_re, dbb_im))
    dq, dkv, g["sink"] = _attn_bwd(qkv, attn, dattn, lse, sp["sink"])
    dproj = _rope_bwd(dq, dkv, _from_segments(du_seg), dpre, d_skip, tabs)
    g["w_in"] = _mm_tn(dproj, h, IN_WIDTH // 5, D_MODEL, "mm_in_dw")
    grad_x, _, g["norm_mix_g"] = _mm_nn_rms_bwd(dproj, wb["w_in"], x, sp["norm_mix_g"], dx1, "mm_in_dx")
    return loss, grad_x, g


MESH = pl.DeviceIdType.MESH
ANY = pl.BlockSpec(memory_space=pl.ANY)


def _place():
    x, y, c = lax.axis_index("x"), lax.axis_index("y"), lax.axis_index("c")
    chips = [(1 - x, y), (x, 1 - y), (1 - x, 1 - y)]
    return x, y, c, chips


def _chip_index(px, py):
    return 2 * px + py


CHUNK_BYTES = 256 * 1024
MAX_CHUNKS = 16


def _row_chunks(rows, row_bytes, align):
    n = max(1, min(MAX_CHUNKS, (rows * row_bytes) // CHUNK_BYTES))
    per = -(-rows // n)
    per = -(-per // align) * align
    return [(r0, min(per, rows - r0)) for r0 in range(0, rows, per)]


def _align_of(dtype):
    return SUBLANES * 4 // jnp.dtype(dtype).itemsize


def _remote(src, dst, send_sem, recv_sem, to):
    return pltpu.make_async_remote_copy(src_ref=src, dst_ref=dst, send_sem=send_sem, recv_sem=recv_sem,
                                        device_id=to, device_id_type=MESH)


CAST_ROWS = 64


def _gather_weights(shards, dtypes):
    nw = len(shards)

    def body(*refs):
        w_refs, o_refs = refs[:nw], refs[nw:2 * nw]
        send_sems, recv_sems, in_sems, out_sems = refs[2 * nw:2 * nw + 4]
        raw, cast = refs[2 * nw + 4:3 * nw + 4], refs[3 * nw + 4:]
        x, y, c, chips = _place()
        mine = _chip_index(x, y)
        sibling = (x, y, 1 - c)

        def rows_of(ref, chip, r0, nr):
            return ref.at[chip, pl.ds(r0, nr), :]

        def copy(wi, k, src, dst, to):
            return _remote(src, dst, send_sems.at[wi, k], recv_sems.at[wi, k], to)

        geo = []
        for wi in range(nw):
            rows, cols = w_refs[wi].shape
            row_bytes = cols * jnp.dtype(dtypes[wi]).itemsize
            geo.append((rows // 2, _row_chunks(rows // 2, row_bytes, _align_of(dtypes[wi]))))

        stage_in = [pltpu.make_async_copy(w_refs[wi], raw[wi], in_sems.at[wi]) for wi in range(nw)]
        for cp in stage_in:
            cp.start()
        staged = [raw[wi] if dtypes[wi] == w_refs[wi].dtype else cast[wi] for wi in range(nw)]
        stage_out = []
        for wi in range(nw):
            stage_in[wi].wait()
            if staged[wi] is not raw[wi]:
                def cast_rows(i, _, wi=wi):
                    rows = pl.ds(pl.multiple_of(i * CAST_ROWS, CAST_ROWS), CAST_ROWS)
                    cast[wi][rows, :] = raw[wi][rows, :].astype(dtypes[wi])
                    return 0

                lax.fori_loop(0, w_refs[wi].shape[0] // CAST_ROWS, cast_rows, 0)
            cp = pltpu.make_async_copy(staged[wi], o_refs[wi].at[mine], out_sems.at[wi])
            cp.start()
            stage_out.append(cp)

        for wi in range(nw):
            hr, half_chunks = geo[wi]
            for j, chip in enumerate(chips):
                for r0, nr in half_chunks:
                    copy(wi, j, staged[wi].at[pl.ds(c * hr + r0, nr), :],
                         rows_of(o_refs[wi], mine, c * hr + r0, nr), (*chip, c)).start()
        for wi in range(nw):
            hr, half_chunks = geo[wi]
            for j, chip in enumerate(chips):
                got = rows_of(o_refs[wi], _chip_index(*chip), c * hr, hr)
                copy(wi, j, got, got, (*chip, c)).wait_recv()
                for r0, nr in half_chunks:
                    piece = rows_of(o_refs[wi], _chip_index(*chip), c * hr + r0, nr)
                    copy(wi, 3 + j, piece, piece, sibling).start()
        for wi in range(nw):
            hr = geo[wi][0]
            for j, chip in enumerate(chips):
                got = rows_of(o_refs[wi], _chip_index(*chip), (1 - c) * hr, hr)
                copy(wi, 3 + j, got, got, sibling).wait_recv()
        for wi in range(nw):
            hr = geo[wi][0]
            sent = rows_of(o_refs[wi], mine, c * hr, hr)
            for k in range(6):
                copy(wi, k, sent, sent, sibling).wait_send()
            stage_out[wi].wait()

    return pl.pallas_call(
        body, in_specs=[ANY] * nw, out_specs=[ANY] * nw,
        out_shape=[jax.ShapeDtypeStruct((4, *s.shape), t) for s, t in zip(shards, dtypes)],
        scratch_shapes=[pltpu.SemaphoreType.DMA((nw, 6)), pltpu.SemaphoreType.DMA((nw, 6)),
                        pltpu.SemaphoreType.DMA((nw,)), pltpu.SemaphoreType.DMA((nw,))]
        + [pltpu.VMEM(s.shape, s.dtype) for s in shards] + [pltpu.VMEM(s.shape, t) for s, t in zip(shards, dtypes)],
        name="gather_weights", compiler_params=_params(vmem_mb=40),
    )(*shards)


HBM = pl.BlockSpec(memory_space=pltpu.HBM)
SEM = pl.BlockSpec(memory_space=pltpu.SEMAPHORE)
EFFECT = pltpu.SideEffectType.DATAFLOW_SIDE_EFFECTING


def _cast_place(w, place, dtype, after, name):
    rows, cols = w.shape
    tr = _row_tile(rows, cols, _align_of(dtype))

    def body(p_ref, w_ref, after_ref, o_ref):
        del p_ref, after_ref
        o_ref[...] = w_ref[...].astype(dtype)

    grid_spec = pltpu.PrefetchScalarGridSpec(
        num_scalar_prefetch=1, grid=(rows // tr,),
        in_specs=[pl.BlockSpec((tr, cols), lambda i, p: (i, 0)), ANY],
        out_specs=pl.BlockSpec((None, tr, cols), lambda i, p: (p[1], i, 0)))
    return pl.pallas_call(body, grid_spec=grid_spec, out_shape=jax.ShapeDtypeStruct((4, rows, cols), dtype),
                          name=name, compiler_params=_params(("parallel",)))(place, w, after)


def _split_start(name, arrays, n_pairs, issue):
    n = len(arrays)

    def body(*refs):
        issue(refs[:n], refs[n:n + n_pairs], refs[n + n_pairs:n + 2 * n_pairs])
        token = refs[2 * n + 2 * n_pairs]
        token[...] = jnp.zeros_like(token)

    dma = pltpu.SemaphoreType.DMA(())
    outs = pl.pallas_call(
        body, name=name,
        out_shape=[dma] * (2 * n_pairs) + [pltpu.HBM(t.shape, t.dtype) for t in arrays]
        + [jax.ShapeDtypeStruct((SUBLANES, LANES), F32)],
        in_specs=[HBM] * n, out_specs=[SEM] * (2 * n_pairs) + [HBM] * n + [pl.BlockSpec(memory_space=pltpu.VMEM)],
        input_output_aliases={a: 2 * n_pairs + a for a in range(n)},
        compiler_params=pltpu.CompilerParams(has_side_effects=EFFECT),
    )(*[pltpu.with_memory_space_constraint(t, pltpu.HBM) for t in arrays])
    return outs[:n_pairs], outs[n_pairs:2 * n_pairs], outs[2 * n_pairs:2 * n_pairs + n], outs[-1]


def _split_wait(name, send_sems, recv_sems, flying, sizes, after):
    n, n_pairs = len(flying), len(send_sems)

    def body(*refs):
        x, y, c, _ = _place()
        for k, ref in enumerate(sizes(refs[:n])):
            cp = _remote(ref, ref, refs[n + k], refs[n + n_pairs + k], (x, y, 1 - c))
            cp.wait_send()
            cp.wait_recv()

    return pl.pallas_call(
        body, name=name, out_shape=[pltpu.HBM(t.shape, t.dtype) for t in flying],
        in_specs=[HBM] * n + [SEM] * (2 * n_pairs) + [ANY], out_specs=[HBM] * n,
        input_output_aliases={a: a for a in range(n)},
        compiler_params=pltpu.CompilerParams(has_side_effects=EFFECT),
    )(*flying, *send_sems, *recv_sems, after)


def _spread_start(lands, name):
    def issue(land_refs, send_sems, recv_sems):
        x, y, c, chips = _place()
        mine = _chip_index(x, y)
        for a, land in enumerate(land_refs):
            _, rows, cols = land.shape
            hr = rows // 2
            row_bytes = cols * jnp.dtype(land.dtype).itemsize
            for r0, nr in _row_chunks(hr, row_bytes, _align_of(land.dtype)):
                piece = land.at[mine, pl.ds(c * hr + r0, nr), :]
                for chip in chips:
                    for core in (0, 1):
                        _remote(piece, piece, send_sems[a], recv_sems[a], (*chip, core)).start()

    return _split_start(name, lands, len(lands), issue)


def _spread_wait(send_sems, recv_sems, flying, after, name):
    return _split_wait(name, send_sems, recv_sems, flying, lambda refs: [r.at[pl.ds(0, 3)] for r in refs], after)


def _pair_start(grads):
    n = len(grads)
    zones = [lax.empty((4, g.shape[1] // 2, g.shape[2]), F32) for g in grads]

    def issue(refs, send_sems, recv_sems):
        x, y, c, _ = _place()
        for a in range(n):
            g_ref, z_ref = refs[a], refs[n + a]
            _, rows, cols = g_ref.shape
            hr = rows // 2
            for k in range(4):
                for r0, nr in _row_chunks(hr, cols * 4, SUBLANES):
                    _remote(g_ref.at[k, pl.ds((1 - c) * hr + r0, nr), :], z_ref.at[k, pl.ds(r0, nr), :],
                            send_sems[a], recv_sems[a], (x, y, 1 - c)).start()

    return _split_start("pair_start", list(grads) + zones, n, issue)


def _pair_wait(send_sems, recv_sems, flying, after):
    n = len(flying) // 2
    out = _split_wait("pair_wait", send_sems, recv_sems, flying, lambda refs: list(refs[n:]), after)
    return out[:n], out[n:]


def _chip_start(sums):
    n = len(sums)
    zones = [lax.empty((3, *s.shape[1:]), s.dtype) for s in sums]

    def issue(refs, send_sems, recv_sems):
        x, y, c, chips = _place()
        for a in range(n):
            s_ref, z_ref = refs[a], refs[n + a]
            _, rows, cols = s_ref.shape
            row_bytes = cols * jnp.dtype(s_ref.dtype).itemsize
            for r0, nr in _row_chunks(rows, row_bytes, _align_of(s_ref.dtype)):
                for j, chip in enumerate(chips):
                    _remote(s_ref.at[_chip_index(*chip), pl.ds(r0, nr), :], z_ref.at[j, pl.ds(r0, nr), :],
                            send_sems[a], recv_sems[a], (*chip, c)).start()

    return _split_start("chip_start", list(sums) + zones, n, issue)


def _chip_wait(send_sems, recv_sems, flying, after):
    n = len(flying) // 2
    return _split_wait("chip_wait", send_sems, recv_sems, flying, lambda refs: list(refs[n:]), after)[n:]


def _pair_exchange(grads):
    na = len(grads)

    def body(*refs):
        g_refs, o_refs = refs[:na], refs[na:2 * na]
        send_sems, recv_sems = refs[2 * na:]
        x, y, c, _ = _place()
        sibling = (x, y, 1 - c)
        for ai in range(na):
            _, rows, cols = g_refs[ai].shape
            hr = rows // 2
            for k in range(4):
                for r0, nr in _row_chunks(hr, cols * 4, SUBLANES):
                    _remote(g_refs[ai].at[k, pl.ds((1 - c) * hr + r0, nr), :], o_refs[ai].at[k, pl.ds(r0, nr), :],
                            send_sems.at[ai], recv_sems.at[ai], sibling).start()
        for ai in range(na):
            _remote(o_refs[ai], o_refs[ai], send_sems.at[ai], recv_sems.at[ai], sibling).wait()

    return pl.pallas_call(
        body, in_specs=[ANY] * na, out_specs=[ANY] * na,
        out_shape=[jax.ShapeDtypeStruct((4, g.shape[1] // 2, g.shape[2]), F32) for g in grads],
        scratch_shapes=[pltpu.SemaphoreType.DMA((na,)), pltpu.SemaphoreType.DMA((na,))],
        name="pair_exchange",
    )(*grads)


def _row_tile(rows, cols, align):
    best = align
    for cand in range(align, rows + 1, align):
        if rows % cand == 0 and cand * cols <= 256 * 1024:
            best = cand
    return best


def _pair_sum(g, got, place, transit, name):
    _, rows, cols = g.shape
    hr = rows // 2
    tr = _row_tile(hr, cols, _align_of(transit))
    nt = hr // tr

    def body(p_ref, g_ref, r_ref, s_ref, own_ref):
        total = g_ref[...] + r_ref[...]
        s_ref[...] = total.astype(transit)

        @pl.when(pl.program_id(1) == p_ref[1])
        def _():
            own_ref[...] = total

    grid_spec = pltpu.PrefetchScalarGridSpec(
        num_scalar_prefetch=1, grid=(nt, 4),
        in_specs=[pl.BlockSpec((None, tr, cols), lambda i, k, p: (k, p[0] * nt + i, 0)),
                  pl.BlockSpec((None, tr, cols), lambda i, k, p: (k, i, 0))],
        out_specs=[pl.BlockSpec((None, tr, cols), lambda i, k, p: (k, i, 0)),
                   pl.BlockSpec((tr, cols), lambda i, k, p: (i, 0))])
    return pl.pallas_call(
        body, grid_spec=grid_spec,
        out_shape=[jax.ShapeDtypeStruct((4, hr, cols), transit), jax.ShapeDtypeStruct((hr, cols), F32)],
        name=name, compiler_params=_params(("parallel", "arbitrary")),
    )(place, g, got)


def _chip_exchange(sums):
    na = len(sums)

    def body(*refs):
        s_refs, o_refs = refs[:na], refs[na:2 * na]
        send_sems, recv_sems = refs[2 * na:]
        x, y, c, chips = _place()
        for ai in range(na):
            _, rows, cols = s_refs[ai].shape
            row_bytes = cols * jnp.dtype(s_refs[ai].dtype).itemsize
            for r0, nr in _row_chunks(rows, row_bytes, _align_of(s_refs[ai].dtype)):
                for j, chip in enumerate(chips):
                    _remote(s_refs[ai].at[_chip_index(*chip), pl.ds(r0, nr), :], o_refs[ai].at[j, pl.ds(r0, nr), :],
                            send_sems.at[ai, j], recv_sems.at[ai, j], (*chip, c)).start()
        for ai in range(na):
            for j, chip in enumerate(chips):
                _remote(o_refs[ai].at[j], o_refs[ai].at[j], send_sems.at[ai, j], recv_sems.at[ai, j],
                        (*chip, c)).wait()

    return pl.pallas_call(
        body, in_specs=[ANY] * na, out_specs=[ANY] * na,
        out_shape=[jax.ShapeDtypeStruct((3, *s.shape[1:]), s.dtype) for s in sums],
        scratch_shapes=[pltpu.SemaphoreType.DMA((na, 3)), pltpu.SemaphoreType.DMA((na, 3))],
        name="chip_exchange",
    )(*sums)


def _chip_sum(own, landed, name):
    hr, cols = own.shape
    tr = _row_tile(hr, cols, _align_of(landed.dtype))

    def body(o_ref, l_ref, f_ref):
        acc = o_ref[...]
        for j in range(3):
            acc = acc + l_ref[j].astype(F32)
        f_ref[...] = acc

    return pl.pallas_call(
        body, grid=(hr // tr,),
        in_specs=[pl.BlockSpec((tr, cols), lambda i: (i, 0)), pl.BlockSpec((3, tr, cols), lambda i: (0, i, 0))],
        out_specs=pl.BlockSpec((tr, cols), lambda i: (i, 0)),
        out_shape=jax.ShapeDtypeStruct((hr, cols), F32), name=name,
        compiler_params=_params(("parallel",)),
    )(own, landed)


def _final_exchange(halves, small):
    nh = len(halves)

    def body(*refs):
        h_refs, s_ref = refs[:nh], refs[nh]
        o_refs, so_ref = refs[nh + 1:2 * nh + 1], refs[2 * nh + 1]
        send_sems, recv_sems, local_sem, ssend_sems, srecv_sems = refs[2 * nh + 2:]
        x, y, c, _ = _place()
        me = 4 * x + 2 * y + c
        sibling = (x, y, 1 - c)
        for hi in range(nh):
            hr, cols = h_refs[hi].shape
            for r0, nr in _row_chunks(hr, cols * 4, SUBLANES):
                _remote(h_refs[hi].at[pl.ds(r0, nr), :], o_refs[hi].at[pl.ds(r0, nr), :],
                        send_sems.at[hi], recv_sems.at[hi], sibling).start()
        small_cps = [pltpu.make_async_copy(s_ref, so_ref.at[me], local_sem)]
        for r in range(1, 8):
            fx, fy, fc = (r >> 2) & 1, (r >> 1) & 1, r & 1
            peer = (1 - x if fx else x, 1 - y if fy else y, 1 - c if fc else c)
            small_cps.append(_remote(s_ref, so_ref.at[me], ssend_sems.at[r - 1], srecv_sems.at[r - 1], peer))
        for cp in small_cps:
            cp.start()
        for hi in range(nh):
            _remote(h_refs[hi], o_refs[hi], send_sems.at[hi], recv_sems.at[hi], sibling).wait()
        for cp in small_cps:
            cp.wait()

    return pl.pallas_call(
        body, in_specs=[ANY] * (nh + 1), out_specs=[ANY] * (nh + 1),
        out_shape=[jax.ShapeDtypeStruct(h.shape, F32) for h in halves]
        + [jax.ShapeDtypeStruct((8, *small.shape), F32)],
        scratch_shapes=[pltpu.SemaphoreType.DMA((nh,)), pltpu.SemaphoreType.DMA((nh,)),
                        pltpu.SemaphoreType.DMA, pltpu.SemaphoreType.DMA((7,)), pltpu.SemaphoreType.DMA((7,))],
        name="final_exchange",
    )(*halves, small)


def _adamw_halves(w, own, other, m, v, place, name):
    r, c = w.shape
    hr = r // 2
    tr = _row_tile(hr, c, SUBLANES)
    nt = hr // tr
    c1 = 1.0 - ADAM_B1 ** ADAM_STEP
    c2 = 1.0 - ADAM_B2 ** ADAM_STEP

    def body(p_ref, w_ref, own_ref, other_ref, m_ref, v_ref, g_ref, d_ref, nm_ref, nv_ref):
        mine = pl.program_id(0) // nt == p_ref[0]
        gv = jnp.where(mine, own_ref[...], other_ref[...])
        nm = ADAM_B1 * m_ref[...] + (1.0 - ADAM_B1) * gv
        nv = ADAM_B2 * v_ref[...] + (1.0 - ADAM_B2) * (gv * gv)
        g_ref[...] = gv
        d_ref[...] = -ADAM_LR * ((nm / c1) / (jnp.sqrt(nv / c2) + ADAM_EPS) + ADAM_WD * w_ref[...])
        nm_ref[...] = nm
        nv_ref[...] = nv

    full = pl.BlockSpec((tr, c), lambda i, p: (i, 0))
    half = pl.BlockSpec((tr, c), lambda i, p: (i % nt, 0))
    out = jax.ShapeDtypeStruct((r, c), F32)
    grid_spec = pltpu.PrefetchScalarGridSpec(num_scalar_prefetch=1, grid=(2 * nt,),
                                             in_specs=[full, half, half, full, full], out_specs=[full] * 4)
    return pl.pallas_call(body, grid_spec=grid_spec, out_shape=[out] * 4, name=name,
                          compiler_params=_params(("parallel",)))(place, w, own, other, m, v)


def _adamw_many(ws, gs, ms, vs, name):
    n = len(ws)
    c1 = 1.0 - ADAM_B1 ** ADAM_STEP
    c2 = 1.0 - ADAM_B2 ** ADAM_STEP

    def body(*refs):
        w_refs, g_refs, m_refs, v_refs = (refs[k * n:(k + 1) * n] for k in range(4))
        d_refs, nm_refs, nv_refs = (refs[(4 + k) * n:(5 + k) * n] for k in range(3))
        for i in range(n):
            gv = g_refs[i][...]
            nm = ADAM_B1 * m_refs[i][...] + (1.0 - ADAM_B1) * gv
            nv = ADAM_B2 * v_refs[i][...] + (1.0 - ADAM_B2) * (gv * gv)
            d_refs[i][...] = -ADAM_LR * ((nm / c1) / (jnp.sqrt(nv / c2) + ADAM_EPS) + ADAM_WD * w_refs[i][...])
            nm_refs[i][...] = nm
            nv_refs[i][...] = nv

    vmem = pl.BlockSpec(memory_space=pltpu.VMEM)
    shapes = [jax.ShapeDtypeStruct(t.shape, F32) for t in ws]
    outs = pl.pallas_call(body, in_specs=[vmem] * (4 * n), out_specs=[vmem] * (3 * n), out_shape=shapes * 3,
                          name=name, compiler_params=_params(vmem_mb=56))(*ws, *gs, *ms, *vs)
    return outs[:n], outs[n:2 * n], outs[2 * n:]


BIG = ("w_in", "w_glu", "w_out", "w_up", "w_down")
WEIGHTS = ("norm_mix_g", "w_in", "a_re", "a_im", "log_step", "b_re", "b_im", "c_re", "c_im", "d_skip", "w_glu",
           "sink", "norm_attn_g", "norm_ssm_g", "w_out", "norm_ffn_g", "w_up", "conv_w", "conv_b", "w_down",
           "norm_final_g")
SMALL = ("norm_mix_g", "a_re", "a_im", "log_step", "b_re", "b_im", "c_re", "c_im", "d_skip", "sink",
         "norm_attn_g", "norm_ssm_g", "norm_ffn_g", "conv_w", "conv_b", "norm_final_g")
SMALL_ROWS = 48
N_DEV = 8


def _tile_rows(size):
    return -(-size // (SUBLANES * D_MODEL)) * SUBLANES


def _by_owner(name, g):
    if name == "w_up":
        return g
    return g.reshape(4, g.shape[0] // 4, g.shape[1])


def _view(name, t):
    if name == "w_in":
        return jnp.swapaxes(t[0], 0, 1)
    if name in ("b_re", "b_im"):
        return jnp.swapaxes(t, -1, -2)
    return t


def _unview(name, t):
    if name == "w_in":
        return jnp.swapaxes(t, 0, 1)[None]
    if name in ("b_re", "b_im"):
        return jnp.swapaxes(t, -1, -2)
    return t


def kernel(x, norm_mix_g, w_in, a_re, a_im, log_step, b_re, b_im, c_re, c_im, d_skip, w_glu, sink, norm_attn_g, norm_ssm_g, w_out, norm_ffn_g, w_up, conv_w, conv_b, w_down, norm_final_g, loss_target, m_norm_mix_g, m_w_in, m_a_re, m_a_im, m_log_step, m_b_re, m_b_im, m_c_re, m_c_im, m_d_skip, m_w_glu, m_sink, m_norm_attn_g, m_norm_ssm_g, m_w_out, m_norm_ffn_g, m_w_up, m_conv_w, m_conv_b, m_w_down, m_norm_final_g, v_norm_mix_g, v_w_in, v_a_re, v_a_im, v_log_step, v_b_re, v_b_im, v_c_re, v_c_im, v_d_skip, v_w_glu, v_sink, v_norm_attn_g, v_norm_ssm_g, v_w_out, v_norm_ffn_g, v_w_up, v_conv_w, v_conv_b, v_w_down, v_norm_final_g):
    given = dict(locals())
    w = {n: given[n] for n in WEIGHTS}
    m = {n: given["m_" + n] for n in WEIGHTS}
    v = {n: given["v_" + n] for n in WEIGHTS}
    xy = 2 * lax.axis_index("x") + lax.axis_index("y")

    core = lax.axis_index("c")
    place = jnp.stack([core, xy]).astype(jnp.int32)

    conv_rows = jnp.pad(w["conv_w"][0], ((0, 2 * SUBLANES - 3), (0, 0)))
    rows = lambda t: t.reshape(4 * t.shape[1], t.shape[2])
    (w_in_all,) = _gather_weights([_view("w_in", w["w_in"])], [BF16])
    wb = {"w_in": rows(w_in_all)}
    early = ("w_in", "w_glu", "w_out")
    mixer = [_cast_place(w[n][0], place, BF16, w_in_all, "cast_" + n) for n in ("w_glu", "w_out")]
    mixer.append(_cast_place(conv_rows, place, F32, w_in_all, "cast_conv_w"))
    *mixer_flight, mixer_token = _spread_start(mixer, "spread_mixer_start")
    late = ("w_up", "w_down")
    *late_flight, token = _spread_start(
        [_cast_place(w[n][0], place, BF16, mixer_token, "cast_" + n) for n in late], "spread_ffn_start")

    def mixer_weights(after):
        w_glu4, w_out4, conv4 = _spread_wait(*mixer_flight, after, "spread_mixer_wait")
        return {"w_glu": rows(w_glu4), "w_out": rows(w_out4),
                "conv_w": conv4[:, :3].transpose(1, 0, 2).reshape(3, 2 * D_FF)}

    def late_weights(after):
        w_up4, w_down4 = _spread_wait(*late_flight, after, "spread_ffn_wait")
        return {"w_up": w_up4, "w_down": rows(w_down4)}

    sp = {n: w[n][0] for n in ("a_re", "a_im", "log_step", "b_re", "b_im", "c_re", "c_im", "d_skip",
                               "norm_mix_g", "norm_attn_g", "norm_ssm_g", "norm_ffn_g", "sink", "conv_b")}
    for n in ("norm_mix_g", "norm_attn_g", "norm_ssm_g", "norm_ffn_g", "sink", "conv_b"):
        sp[n] = sp[n].reshape(1, -1)
    sp["norm_mix_g"] = sp["norm_mix_g"] + token[:1, :1]
    sp["norm_final_g"] = w["norm_final_g"]
    flight = {}

    def ffn_grads_ready(dw_up, dw_down):
        *flight["pair"], token = _pair_start([dw_up, _by_owner("w_down", dw_down)])
        return token[:1, :1]

    def ffn_grads_next(after):
        mine, got = _pair_wait(*flight["pair"], after)
        sums, flight["own"] = zip(*[_pair_sum(a, b, place, BF16, "pair_sum_" + n) for n, a, b in zip(late, mine, got)])
        *flight["chip"], token = _chip_start(list(sums))
        return token[:1, :1]

    loss, grad_x, g = _local_step(x[0], loss_target[0], wb, sp, mixer_weights, late_weights, ffn_grads_ready,
                                  ffn_grads_next)

    def as_rows(t):
        rows = _tile_rows(t.size)
        return jnp.pad(t.reshape(-1), (0, rows * D_MODEL - t.size)).reshape(rows, D_MODEL)

    pieces = [as_rows(g[n]) for n in SMALL] + [as_rows(loss)]
    spare = N_DEV * SMALL_ROWS - sum(p.shape[0] for p in pieces)
    small = jnp.concatenate(pieces + [jnp.zeros((spare, D_MODEL), F32)]).reshape(4, 2 * SMALL_ROWS, D_MODEL)
    by_owner = [_by_owner(n, g[n]) for n in early] + [small]
    got = _pair_exchange(by_owner)
    transit = [BF16] * len(early) + [F32]
    chip_sums, own_sums = zip(*[_pair_sum(a, b, place, t, "pair_sum_" + n)
                                for n, a, b, t in zip(early + ("small",), by_owner, got, transit)])
    landed = _chip_exchange(list(chip_sums))
    halves = {n: _chip_sum(o, t, "chip_sum_" + n) for n, o, t in zip(early + ("small",), own_sums, landed)}
    late_landed = _chip_wait(*flight["chip"], grad_x)
    for n, o, t in zip(late, flight["own"], late_landed):
        halves[n] = _chip_sum(o, t, "chip_sum_" + n)
    *others, small_all = _final_exchange([halves[n] for n in BIG], halves["small"])
    small_all = small_all.reshape(N_DEV * SMALL_ROWS, D_MODEL)
    grads, row = {}, 0
    for n in SMALL:
        shape = (3, 4 * w[n].shape[-1]) if n == "conv_w" else w[n].shape[1:] if n != "norm_final_g" else w[n].shape
        size = math.prod(shape)
        grads[n] = small_all[row:row + _tile_rows(size)].reshape(-1)[:size].reshape(shape)
        row += _tile_rows(size)
    loss = small_all[row, 0]
    cw = w["conv_w"].shape[-1]
    grads["conv_w"] = lax.dynamic_slice_in_dim(grads["conv_w"], xy * cw, cw, axis=1)
    grads = {n: _view(n, grads[n].reshape(w[n].shape)) for n in SMALL}
    wv, mv, vv = ({n: _view(n, t[n]) for n in WEIGHTS} for t in (w, m, v))

    delta, new_m, new_v = {}, {}, {}
    for n, other in zip(BIG, others):
        two_d = lambda t: t.reshape(t.shape[-2:])
        grads[n], delta[n], new_m[n], new_v[n] = _adamw_halves(
            two_d(wv[n]), halves[n], other, two_d(mv[n]), two_d(vv[n]), place, "adamw_" + n)
    for group, name in ((("b_re", "b_im"), "adamw_b"), (tuple(n for n in SMALL if n not in ("b_re", "b_im")), "adamw_small")):
        row = lambda t: t.reshape(1, -1) if t.ndim == 1 else t
        d_, m_, v_ = _adamw_many(*[[row(t[n]) for n in group] for t in (wv, grads, mv, vv)], name)
        for n, dn, mn, vn in zip(group, d_, m_, v_):
            delta[n], new_m[n], new_v[n] = (t.reshape(wv[n].shape) for t in (dn, mn, vn))
    natural = lambda t: [_unview(n, t[n].reshape(wv[n].shape)) for n in WEIGHTS]
    return (loss, grad_x[None], *natural(grads), *natural(delta), *natural(new_m), *natural(new_v))
```

```python
import functools
import math

import jax
import jax.numpy as jnp
import numpy as np
from jax import lax
from jax.experimental import pallas as pl
from jax.experimental.pallas import tpu as pltpu

F32 = jnp.float32
BF16 = jnp.bfloat16

D_MODEL = 1024
N_Q_HEADS = 8
N_KV_HEADS = 2
HEAD_DIM = 64
ATTN_WIDTH = 512
KV_WIDTH = 128
QKV_WIDTH = ATTN_WIDTH + 2 * KV_WIDTH
WINDOW = 128
BLOCK = 128
ROPE_DIM = 16
ROPE_THETA = 500000.0
SSM_WIDTH = 512
SSM_GROUP = 16
N_SSM_GROUPS = 32
SSM_STATE = 64
IN_WIDTH = 1280
D_FF = 2816
EPS = 1e-6
ADAM_LR = 0.001
ADAM_B1 = 0.9
ADAM_B2 = 0.999
ADAM_EPS = 1e-08
ADAM_WD = 0.01
ADAM_STEP = 10

VMEM_BYTES_V7X = 64 * 1024 * 1024
SUBLANES = 8
LANES = 128
SSM_CB = 4
SSM_CH = 128
SSM_ST = 512
N_SEG = SUBLANES

NN = (((1,), (0,)), ((), ()))
NT = (((1,), (1,)), ((), ()))
TN = (((0,), (0,)), ((), ()))


def _params(sem=None, vmem_mb=48):
    limit = vmem_mb * 1024 * 1024
    assert limit < VMEM_BYTES_V7X
    return pltpu.CompilerParams(dimension_semantics=sem, vmem_limit_bytes=limit)


def _dg(a, b, dims):
    return lax.dot_general(a, b, dims, preferred_element_type=F32)


def _sigmoid(x):
    return 1.0 / (1.0 + jnp.exp(-x))


_SQRT_HALF = 0.7071067811865476
_INV_SQRT_2PI = 0.3989422804014327


def _gelu(x):
    return 0.5 * x * (1.0 + lax.erf(x * _SQRT_HALF))


def _gelu_grad(x):
    return 0.5 * (1.0 + lax.erf(x * _SQRT_HALF)) + x * (_INV_SQRT_2PI * jnp.exp(-0.5 * x * x))


def _mm_tn(a, b, tm, tn, name):
    k, m = a.shape
    n = b.shape[1]

    def body(a_ref, b_ref, o_ref):
        o_ref[...] = _dg(a_ref[...], b_ref[...], TN)

    return pl.pallas_call(
        body, grid=(m // tm, n // tn),
        in_specs=[pl.BlockSpec((k, tm), lambda i, j: (0, i)), pl.BlockSpec((k, tn), lambda i, j: (0, j))],
        out_specs=pl.BlockSpec((tm, tn), lambda i, j: (i, j)),
        out_shape=jax.ShapeDtypeStruct((m, n), F32), name=name,
        compiler_params=_params(("parallel", "parallel")),
    )(a, b)


def _mm_nn_cols(a, b4, tm, name):
    m, k = a.shape
    s, _, n = b4.shape

    def body(a_ref, b_ref, o_ref):
        o_ref[...] = _dg(a_ref[...], b_ref[...], NN)

    return pl.pallas_call(
        body, grid=(m // tm, s),
        in_specs=[pl.BlockSpec((tm, k), lambda i, j: (i, 0)), pl.BlockSpec((None, k, n), lambda i, j: (j, 0, 0))],
        out_specs=pl.BlockSpec((tm, n), lambda i, j: (i, j)),
        out_shape=jax.ShapeDtypeStruct((m, s * n), F32), name=name,
        compiler_params=_params(("parallel", "parallel")),
    )(a, b4)


def _mm_tn_cols(a, b2, s, tm, name):
    k, m = a.shape
    h, _, wide = b2.shape
    per = s // h
    n = wide // per

    def body(a_ref, b_ref, o_ref):
        o_ref[...] = _dg(a_ref[...], b_ref[...], TN)

    return pl.pallas_call(
        body, grid=(s, m // tm),
        in_specs=[pl.BlockSpec((k, tm), lambda j, i: (0, i)),
                  pl.BlockSpec((None, k, n), lambda j, i: (j // per, 0, j % per))],
        out_specs=pl.BlockSpec((None, tm, n), lambda j, i: (j, i, 0)),
        out_shape=jax.ShapeDtypeStruct((s, m, n), F32), name=name,
        compiler_params=_params(("parallel", "parallel")),
    )(a, b2)


TM_EW = 256


def _rms_bwd_vals(xv, gv, dy):
    r = lax.rsqrt(jnp.mean(xv * xv, axis=-1, keepdims=True) + EPS)
    xh = xv * r
    dxh = dy * gv
    dx = r * (dxh - xh * jnp.mean(dxh * xh, axis=-1, keepdims=True))
    return dx, dy * xh


TM_FUSED = 512
TM_LOSS = 256


def _rms_vals(xv, gv):
    return xv * lax.rsqrt(jnp.mean(xv * xv, axis=-1, keepdims=True) + EPS) * gv


def _rope_blocks(src, dst, c, lo, hi):
    nq = ATTN_WIDTH // LANES
    for blk in range(nq + 1):
        t = src[:, blk * LANES:(blk + 1) * LANES]
        dst[:, blk * LANES:(blk + 1) * LANES] = (
            t * c + pltpu.roll(t, LANES - 8, 1) * lo + pltpu.roll(t, 8, 1) * hi).astype(BF16)
    dst[:, (nq + 1) * LANES:] = src[:, (nq + 1) * LANES:].astype(BF16)


def _rms_mm_rope(x, g, wt, tabs, name):
    l, d = x.shape
    n = wt.shape[0]

    def body(x_ref, g_ref, w_ref, c_ref, lo_ref, hi_ref, h_ref, qkv_ref, u_ref):
        h = _rms_vals(x_ref[...], g_ref[...]).astype(BF16)
        h_ref[...] = h
        out = _dg(h, w_ref[...], NT)
        _rope_blocks(out[:, :QKV_WIDTH], qkv_ref, c_ref[...], lo_ref[...], hi_ref[...])
        u_ref[...] = out[:, QKV_WIDTH:]

    row = lambda width: pl.BlockSpec((TM_FUSED, width), lambda i: (i, 0))
    return pl.pallas_call(
        body, grid=(l // TM_FUSED,),
        in_specs=[row(d), pl.BlockSpec((1, d), lambda i: (0, 0)), pl.BlockSpec((n, d), lambda i: (0, 0)),
                  row(LANES), row(LANES), row(LANES)],
        out_specs=[row(d), row(QKV_WIDTH), row(n - QKV_WIDTH)],
        out_shape=[jax.ShapeDtypeStruct((l, d), BF16), jax.ShapeDtypeStruct((l, QKV_WIDTH), BF16),
                   jax.ShapeDtypeStruct((l, n - QKV_WIDTH), F32)],
        name=name, compiler_params=_params(("parallel",)),
    )(x, g, wt, *tabs)


def _mix_mm_res_rms(attn, ys, g_attn, g_ssm, b, res, g, name):
    l, w = attn.shape
    d = b.shape[1]

    def body(a_ref, y_ref, ga_ref, gs_ref, b_ref, r_ref, g_ref, m_ref, x_ref, h_ref):
        m_ref[:, :w] = _rms_vals(a_ref[...], ga_ref[...]).astype(BF16)
        m_ref[:, w:] = _rms_vals(y_ref[...], gs_ref[...]).astype(BF16)
        xv = r_ref[...] + _dg(m_ref[...], b_ref[...], NN)
        x_ref[...] = xv
        h_ref[...] = _rms_vals(xv, g_ref[...]).astype(BF16)

    row = lambda width: pl.BlockSpec((TM_FUSED, width), lambda i: (i, 0))
    vec = lambda width: pl.BlockSpec((1, width), lambda i: (0, 0))
    return pl.pallas_call(
        body, grid=(l // TM_FUSED,),
        in_specs=[row(w), row(w), vec(w), vec(w), pl.BlockSpec((2 * w, d), lambda i: (0, 0)), row(d), vec(d)],
        out_specs=[row(2 * w), row(d), row(d)],
        out_shape=[jax.ShapeDtypeStruct((l, 2 * w), BF16), jax.ShapeDtypeStruct((l, d), F32),
                   jax.ShapeDtypeStruct((l, d), BF16)],
        name=name, compiler_params=_params(("parallel",)),
    )(attn, ys, g_attn, g_ssm, b, res, g)


def _mm_res_loss(a, b, res, g, target):
    l, k = a.shape
    d = b.shape[1]

    def body(a_ref, b_ref, r_ref, g_ref, t_ref, loss_ref, dx_ref, dxb_ref, dg_ref):
        xv = r_ref[...] + _dg(a_ref[...], b_ref[...], NN)
        gv = g_ref[...]
        r = lax.rsqrt(jnp.mean(xv * xv, axis=-1, keepdims=True) + EPS)
        xh = xv * r
        e = xh * gv - t_ref[...]
        part = jnp.sum(jnp.sum(e * e, axis=1, keepdims=True), axis=0, keepdims=True) * (0.5 / d)
        dy = e * (1.0 / d)
        dxh = dy * gv
        dx = r * (dxh - xh * jnp.mean(dxh * xh, axis=-1, keepdims=True))
        dx_ref[...] = dx
        dxb_ref[...] = dx.astype(BF16)

        @pl.when(pl.program_id(0) == 0)
        def _():
            dg_ref[...] = jnp.zeros_like(dg_ref)
            loss_ref[...] = jnp.zeros_like(loss_ref)

        dg_ref[...] += jnp.sum(dy * xh, axis=0, keepdims=True)
        loss_ref[...] += part

    row = lambda width: pl.BlockSpec((TM_LOSS, width), lambda i: (i, 0))
    vec = pl.BlockSpec((1, d), lambda i: (0, 0))
    return pl.pallas_call(
        body, grid=(l // TM_LOSS,),
        in_specs=[row(k), pl.BlockSpec((k, d), lambda i: (0, 0)), row(d), vec, row(d)],
        out_specs=[pl.BlockSpec((1, 1), lambda i: (0, 0)), row(d), row(d), vec],
        out_shape=[jax.ShapeDtypeStruct((1, 1), F32), jax.ShapeDtypeStruct((l, d), F32),
                   jax.ShapeDtypeStruct((l, d), BF16), jax.ShapeDtypeStruct((1, d), F32)],
        name="mm_down_loss", compiler_params=_params(("arbitrary",)),
    )(a, b, res, g, target)


def _mm_rms_bwd(a, b, a_spec, b_spec, matmul, x, g, res, name):
    l, d = x.shape

    def body(a_ref, b_ref, x_ref, g_ref, res_ref, dx_ref, dxb_ref, dg_ref):
        dx, dgr = _rms_bwd_vals(x_ref[...], g_ref[...], matmul(a_ref, b_ref))
        dx = dx + res_ref[...]
        dx_ref[...] = dx
        dxb_ref[...] = dx.astype(BF16)

        @pl.when(pl.program_id(0) == 0)
        def _():
            dg_ref[...] = jnp.zeros_like(dg_ref)

        dg_ref[...] += jnp.sum(dgr, axis=0, keepdims=True)

    row = pl.BlockSpec((TM_FUSED, d), lambda i: (i, 0))
    vec = pl.BlockSpec((1, d), lambda i: (0, 0))
    return pl.pallas_call(
        body, grid=(l // TM_FUSED,), in_specs=[a_spec, b_spec, row, vec, row], out_specs=[row, row, vec],
        out_shape=[jax.ShapeDtypeStruct((l, d), F32), jax.ShapeDtypeStruct((l, d), BF16),
                   jax.ShapeDtypeStruct((1, d), F32)],
        name=name, compiler_params=_params(("arbitrary",)),
    )(a, b, x, g, res)


def _mm_nn_rms_bwd(a, b, x, g, res, name):
    return _mm_rms_bwd(a, b, pl.BlockSpec((TM_FUSED, a.shape[1]), lambda i: (i, 0)),
                       pl.BlockSpec(b.shape, lambda i: (0, 0)),
                       lambda a_ref, b_ref: _dg(a_ref[...], b_ref[...], NN), x, g, res, name)


def _mm_cols_rms_bwd(a2, b4, x, g, res, name):
    h, _, wide = a2.shape
    s, _, n = b4.shape
    per = s // h

    def matmul(a_ref, b_ref):
        acc = None
        for j in range(s):
            part = _dg(a_ref[j // per, :, (j % per) * n:(j % per + 1) * n], b_ref[j], NT)
            acc = part if acc is None else acc + part
        return acc

    return _mm_rms_bwd(a2, b4, pl.BlockSpec((h, TM_FUSED, wide), lambda i: (0, i, 0)),
                       pl.BlockSpec(b4.shape, lambda i: (0, 0, 0), pipeline_mode=pl.Buffered(1)),
                       matmul, x, g, res, name)


def _mm_mix_bwd(dx, b, attn, ys, g_attn, g_ssm, name):
    l, w = attn.shape
    d = dx.shape[1]

    def body(dx_ref, b_ref, a_ref, y_ref, ga_ref, gs_ref, da_ref, dy_ref, dga_ref, dgs_ref):
        @pl.when(pl.program_id(0) == 0)
        def _():
            dga_ref[...] = jnp.zeros_like(dga_ref)
            dgs_ref[...] = jnp.zeros_like(dgs_ref)

        dm = _dg(dx_ref[...], b_ref[...], NT)
        for src, gr, off, dst, dgr in ((a_ref, ga_ref, 0, da_ref, dga_ref), (y_ref, gs_ref, w, dy_ref, dgs_ref)):
            dxv, dg_rows = _rms_bwd_vals(src[...], gr[...], dm[:, off:off + w])
            dst[...] = dxv
            dgr[...] += jnp.sum(dg_rows, axis=0, keepdims=True)

    row = lambda width: pl.BlockSpec((TM_FUSED, width), lambda i: (i, 0))
    vec = pl.BlockSpec((1, w), lambda i: (0, 0))
    return pl.pallas_call(
        body, grid=(l // TM_FUSED,),
        in_specs=[row(d), pl.BlockSpec((2 * w, d), lambda i: (0, 0)), row(w), row(w), vec, vec],
        out_specs=[row(w), row(w), vec, vec],
        out_shape=[jax.ShapeDtypeStruct((l, w), F32), jax.ShapeDtypeStruct((l, w), F32),
                   jax.ShapeDtypeStruct((1, w), F32), jax.ShapeDtypeStruct((1, w), F32)],
        name=name, compiler_params=_params(("arbitrary",)),
    )(dx, b, attn, ys, g_attn, g_ssm)


def _rope_tables(l):
    half = ROPE_DIM // 2
    f32 = np.float32
    inv_freq = np.power(f32(ROPE_THETA), -np.arange(half, dtype=f32) / f32(half))
    ang = np.arange(l, dtype=f32)[:, None] * inv_freq[None, :]
    cos, sin = np.cos(ang), np.sin(ang)
    ones = np.ones((l, HEAD_DIM - ROPE_DIM), f32)
    zeros = np.zeros((l, HEAD_DIM - ROPE_DIM), f32)
    zh = np.zeros((l, half), f32)
    c = np.concatenate([cos, cos, ones], axis=1)
    s_lo = np.concatenate([-sin, zh, zeros], axis=1)
    s_hi = np.concatenate([zh, sin, zeros], axis=1)
    return tuple(jnp.asarray(np.tile(t, (1, LANES // HEAD_DIM)), F32) for t in (c, s_lo, s_hi))


def _rope_bwd(dq, dkv, du_ssm, dpre, d_skip, tabs):
    l = dq.shape[0]
    nq = ATTN_WIDTH // LANES

    def body(dq_ref, dkv_ref, du_ref, dpre_ref, ds_ref, c_ref, lo_ref, hi_ref, o_ref):
        c, lo, hi = c_ref[...], lo_ref[...], hi_ref[...]
        for blk in range(nq + 1):
            t = dq_ref[:, blk * LANES:(blk + 1) * LANES] if blk < nq else dkv_ref[:, :KV_WIDTH]
            g = t * c + pltpu.roll(t * lo, 8, 1) + pltpu.roll(t * hi, LANES - 8, 1)
            o_ref[:, blk * LANES:(blk + 1) * LANES] = g.astype(BF16)
        o_ref[:, (nq + 1) * LANES:QKV_WIDTH] = dkv_ref[:, KV_WIDTH:].astype(BF16)
        o_ref[:, QKV_WIDTH:] = (du_ref[...] + dpre_ref[...] * ds_ref[...]).astype(BF16)

    tab = pl.BlockSpec((TM_EW, LANES), lambda i: (i, 0))
    wide = pl.BlockSpec((TM_EW, SSM_WIDTH), lambda i: (i, 0))
    return pl.pallas_call(
        body, grid=(l // TM_EW,),
        in_specs=[wide, pl.BlockSpec((TM_EW, 2 * KV_WIDTH), lambda i: (i, 0)), wide, wide,
                  pl.BlockSpec((1, SSM_WIDTH), lambda i: (0, 0)), tab, tab, tab],
        out_specs=pl.BlockSpec((TM_EW, IN_WIDTH), lambda i: (i, 0)),
        out_shape=jax.ShapeDtypeStruct((l, IN_WIDTH), BF16), name="rope_bwd",
        compiler_params=_params(("parallel",)),
    )(dq, dkv, du_ssm, dpre, d_skip, *tabs)


_Q_COLS = ATTN_WIDTH // LANES
_SCALE = HEAD_DIM ** -0.5
_NEG = -1e30


def _window_specs(nb, width, col):
    return [
        pl.BlockSpec((BLOCK, width), lambda n: (jnp.maximum(n - 1, 0), col)),
        pl.BlockSpec((BLOCK, width), lambda n: (n, col)),
        pl.BlockSpec((BLOCK, width), lambda n: (jnp.minimum(n + 1, nb - 1), col)),
    ]


def _stacked_sink(sink_ref, heads):
    rid = lax.broadcasted_iota(jnp.int32, (len(heads) * BLOCK, 1), 0)
    sk = jnp.full(rid.shape, sink_ref[0, heads[-1]], F32)
    for g in range(len(heads) - 2, -1, -1):
        sk = jnp.where(rid < (g + 1) * BLOCK, sink_ref[0, heads[g]], sk)
    return sk


def _attn_fwd(qkv, sink):
    l = qkv.shape[0]
    nb = l // BLOCK
    grp = N_Q_HEADS // N_KV_HEADS

    def body(sink_ref, q_ref, k0, k1, k2, v0, v1, v2, o_ref, lse_ref):
        n = pl.program_id(0)
        q = q_ref[...]
        kw = jnp.concatenate([k0[...], k1[...], k2[...]], axis=0)
        vw = jnp.concatenate([v0[...], v1[...], v2[...]], axis=0)
        row = lax.broadcasted_iota(jnp.int32, (grp * BLOCK, 3 * BLOCK), 0)
        col = lax.broadcasted_iota(jnp.int32, (grp * BLOCK, 3 * BLOCK), 1)
        valid = jnp.abs(col - BLOCK - (row & (BLOCK - 1))) <= WINDOW
        valid &= jnp.logical_not((n == 0) & (col < BLOCK))
        valid &= jnp.logical_not((n == nb - 1) & (col >= 2 * BLOCK))
        for hk in range(N_KV_HEADS):
            heads = range(hk * grp, (hk + 1) * grp)
            qs = jnp.concatenate([q[:, h * HEAD_DIM:(h + 1) * HEAD_DIM] for h in heads], axis=0)
            kh = kw[:, hk * HEAD_DIM:(hk + 1) * HEAD_DIM]
            vh = vw[:, hk * HEAD_DIM:(hk + 1) * HEAD_DIM]
            s = jnp.where(valid, _dg(qs, kh, NT) * _SCALE, _NEG)
            sk = _stacked_sink(sink_ref, heads)
            m = jnp.maximum(jnp.max(s, axis=1, keepdims=True), sk)
            p = jnp.exp(s - m)
            denom = jnp.sum(p, axis=1, keepdims=True) + jnp.exp(sk - m)
            o = _dg((p / denom).astype(BF16), vh, NN)
            lse = m + jnp.log(denom)
            for g, h in enumerate(heads):
                o_ref[:, h * HEAD_DIM:(h + 1) * HEAD_DIM] = o[g * BLOCK:(g + 1) * BLOCK]
                lse_ref[:, h:h + 1] = lse[g * BLOCK:(g + 1) * BLOCK]

    return pl.pallas_call(
        body, grid=(nb,),
        in_specs=[pl.BlockSpec(memory_space=pltpu.SMEM),
                  pl.BlockSpec((BLOCK, ATTN_WIDTH), lambda n: (n, 0))]
        + _window_specs(nb, KV_WIDTH, _Q_COLS) + _window_specs(nb, KV_WIDTH, _Q_COLS + 1),
        out_specs=[pl.BlockSpec((BLOCK, ATTN_WIDTH), lambda n: (n, 0)),
                   pl.BlockSpec((BLOCK, N_Q_HEADS), lambda n: (n, 0))],
        out_shape=[jax.ShapeDtypeStruct((l, ATTN_WIDTH), F32), jax.ShapeDtypeStruct((l, N_Q_HEADS), F32)],
        name="attn_fwd", compiler_params=_params(("parallel",)),
    )(sink, qkv, qkv, qkv, qkv, qkv, qkv, qkv)


def _attn_bwd(qkv, attn, dattn, lse, sink):
    l = qkv.shape[0]
    nb = l // BLOCK
    grp = N_Q_HEADS // N_KV_HEADS
    win = 3 * BLOCK

    def body(sink_ref, q_ref, k0, k1, k2, v0, v1, v2, o_ref, d_ref, l_ref, dq_ref, dkv_ref, dsink_ref, ring_ref):
        n = pl.program_id(0)

        @pl.when(n == 0)
        def _():
            dsink_ref[...] = jnp.zeros_like(dsink_ref)
            ring_ref[...] = jnp.zeros_like(ring_ref)

        @pl.when(n < nb)
        def _():
            first, last = n == 0, n == nb - 1
            cat = lambda a, b, c: jnp.concatenate([a[...], b[...], c[...]], axis=0)
            q, kw, vw = q_ref[...], cat(k0, k1, k2), cat(v0, v1, v2)
            dov = d_ref[...]
            prod = o_ref[...] * dov
            dob = dov.astype(BF16)
            lse = l_ref[...]
            row = lax.broadcasted_iota(jnp.int32, (grp * BLOCK, win), 0)
            col = lax.broadcasted_iota(jnp.int32, (grp * BLOCK, win), 1)
            valid = jnp.abs(col - BLOCK - (row & (BLOCK - 1))) <= WINDOW
            valid &= jnp.logical_not(first & (col < BLOCK))
            valid &= jnp.logical_not(last & (col >= 2 * BLOCK))

            dsink_parts, dks, dvs = [], [], []
            for hk in range(N_KV_HEADS):
                heads = range(hk * grp, (hk + 1) * grp)
                ksl = slice(hk * HEAD_DIM, (hk + 1) * HEAD_DIM)
                hsl = [slice(h * HEAD_DIM, (h + 1) * HEAD_DIM) for h in heads]
                stack = lambda parts: jnp.concatenate(parts, axis=0)
                qs = stack([q[:, s_] for s_ in hsl])
                dos = stack([dob[:, s_] for s_ in hsl])
                deltas = stack([jnp.sum(prod[:, s_], axis=1, keepdims=True) for s_ in hsl])
                lses = stack([lse[:, h:h + 1] for h in heads])
                kh, vh = kw[:, ksl], vw[:, ksl]
                s = jnp.where(valid, _dg(qs, kh, NT) * _SCALE, _NEG)
                p = jnp.exp(s - lses)
                dp = _dg(dos, vh, NT)
                ds = (p * (dp - deltas) * _SCALE).astype(BF16)
                dq = _dg(ds, kh, NN)
                sink_rows = jnp.exp(_stacked_sink(sink_ref, heads) - lses) * deltas
                for g in range(grp):
                    dq_ref[:, hsl[g]] = dq[g * BLOCK:(g + 1) * BLOCK]
                    dsink_parts.append(jnp.sum(sink_rows[g * BLOCK:(g + 1) * BLOCK], axis=0, keepdims=True))
                dks.append(_dg(ds, qs, TN))
                dvs.append(_dg(p.astype(BF16), dos, TN))
            dsink_ref[...] -= jnp.concatenate(dsink_parts, axis=1)
            part = jnp.concatenate(dks + dvs, axis=1)
            ring_ref[(n + 2) % 3] += part[0:BLOCK]
            ring_ref[n % 3] += part[BLOCK:2 * BLOCK]
            ring_ref[(n + 1) % 3] = part[2 * BLOCK:]

        @pl.when(n >= 1)
        def _():
            dkv_ref[...] = ring_ref[(n + 2) % 3]

    centre = lambda n: jnp.minimum(n, nb - 1)
    window = lambda width, col: [
        pl.BlockSpec((BLOCK, width), lambda n: (jnp.maximum(centre(n) - 1, 0), col)),
        pl.BlockSpec((BLOCK, width), lambda n: (centre(n), col)),
        pl.BlockSpec((BLOCK, width), lambda n: (jnp.minimum(centre(n) + 1, nb - 1), col))]
    own = lambda width: pl.BlockSpec((BLOCK, width), lambda n: (centre(n), 0))
    return pl.pallas_call(
        body, grid=(nb + 1,),
        in_specs=[pl.BlockSpec(memory_space=pltpu.SMEM), own(ATTN_WIDTH)]
        + window(KV_WIDTH, _Q_COLS) + window(KV_WIDTH, _Q_COLS + 1)
        + [own(ATTN_WIDTH), own(ATTN_WIDTH), own(N_Q_HEADS)],
        out_specs=[own(ATTN_WIDTH), pl.BlockSpec((BLOCK, 2 * KV_WIDTH), lambda n: (jnp.maximum(n - 1, 0), 0)),
                   pl.BlockSpec((1, N_Q_HEADS), lambda n: (0, 0))],
        out_shape=[jax.ShapeDtypeStruct((l, ATTN_WIDTH), F32), jax.ShapeDtypeStruct((l, 2 * KV_WIDTH), F32),
                   jax.ShapeDtypeStruct((1, N_Q_HEADS), F32)],
        scratch_shapes=[pltpu.VMEM((3, BLOCK, 2 * KV_WIDTH), F32)],
        name="attn_bwd", compiler_params=_params(("arbitrary",)),
    )(sink, qkv, qkv, qkv, qkv, qkv, qkv, qkv, attn, dattn, lse)


def _ssm_disc(a_re, a_im, log_step, b_re, b_im):
    step = jnp.exp(log_step)[..., None]
    mag = jnp.exp(a_re * step)
    lb_re, lb_im = mag * jnp.cos(a_im * step), mag * jnp.sin(a_im * step)
    nr, ni = lb_re - 1.0, lb_im
    den = a_re * a_re + a_im * a_im
    f_re = ((nr * a_re + ni * a_im) / den)[..., None]
    f_im = ((ni * a_re - nr * a_im) / den)[..., None]
    return lb_re, lb_im, f_re * b_re - f_im * b_im, f_re * b_im + f_im * b_re


def _ssm_pack(lb_re, lb_im, bb_re, bb_im, c_re, c_im):
    eye = jnp.eye(SSM_CH // SSM_GROUP, dtype=F32)
    ng = SSM_CH // SSM_GROUP

    def diag_b(bb):
        t = bb.reshape(2, SSM_CB, ng, SSM_STATE, SSM_GROUP)
        return jnp.einsum('dkgpc,gh->dkgchp', t, eye).reshape(2, SSM_CB, SSM_CH, SSM_ST)

    def diag_c(cc):
        t = cc.reshape(2, SSM_CB, ng, SSM_GROUP, SSM_STATE)
        return jnp.einsum('dkgcp,gh->dkhpgc', t, eye).reshape(2, SSM_CB, SSM_ST, SSM_CH)

    bcat = jnp.concatenate([diag_b(bb_re), diag_b(bb_im)], axis=-1)
    ccat = jnp.concatenate([diag_c(c_re), -diag_c(c_im)], axis=-2)
    lam_re = lb_re.reshape(2, SSM_CB, 1, SSM_ST)
    lam_im = lb_im.reshape(2, SSM_CB, 1, SSM_ST)
    return bcat, ccat, lam_re, lam_im


def _ssm_unpack(dbcat, dccat, dlam_re, dlam_im):
    ng = SSM_CH // SSM_GROUP
    eye = jnp.eye(ng, dtype=F32)

    def undiag_b(t):
        t = t.reshape(2, SSM_CB, ng, SSM_GROUP, ng, SSM_STATE)
        return jnp.einsum('dkgchp,gh->dkgpc', t, eye).reshape(2, N_SSM_GROUPS, SSM_STATE, SSM_GROUP)

    def undiag_c(t):
        t = t.reshape(2, SSM_CB, ng, SSM_STATE, ng, SSM_GROUP)
        return jnp.einsum('dkhpgc,gh->dkgcp', t, eye).reshape(2, N_SSM_GROUPS, SSM_GROUP, SSM_STATE)

    dbb_re, dbb_im = undiag_b(dbcat[..., :SSM_ST]), undiag_b(dbcat[..., SSM_ST:])
    dc_re, dc_im = undiag_c(dccat[:, :, :SSM_ST]), -undiag_c(dccat[:, :, SSM_ST:])
    shape = (2, N_SSM_GROUPS, SSM_STATE)
    return dlam_re.reshape(shape), dlam_im.reshape(shape), dbb_re, dbb_im, dc_re, dc_im


def _to_segments(t):
    l, w = t.shape
    return t.reshape(N_SEG, l // N_SEG, w).transpose(1, 0, 2).reshape(l, w)


def _from_segments(t):
    l, w = t.shape
    return t.reshape(l // N_SEG, N_SEG, w).transpose(1, 0, 2).reshape(l, w)


SSM_RC = 256
SSM_JC = SSM_RC // N_SEG
_RE, _IM = pl.ds(0, SSM_ST), pl.ds(SSM_ST, SSM_ST)


def _cfma(ar, ai, xr, xi, br, bi):
    return ar * xr - ai * xi + br, ar * xi + ai * xr + bi


def _chunk_rows(ci, rev, nc):
    start = jnp.where(rev, (nc - 1 - ci) * SSM_RC, ci * SSM_RC)
    return pl.ds(pl.multiple_of(start, SSM_RC), SSM_RC)


def _scan_chunk(src, dst, ar, ai, rev, nj, ci, carry, prev_ref=None):
    def rows_of(staged, j, k):
        at = jnp.where(rev, SSM_JC - 1 - k, k) if staged else j
        return pl.ds(pl.multiple_of(at * N_SEG, N_SEG), N_SEG)

    for k in range(SSM_JC):
        jj = ci * SSM_JC + k
        j = jnp.where(rev, nj - 1 - jj, jj)
        rows = rows_of(src[1], j, k)
        nr, ni = _cfma(ar, ai, carry[0], carry[1], src[0][rows, _RE], src[0][rows, _IM])
        if dst is not None:
            rows = rows_of(dst[1], j, k)
            dst[0][rows, _RE] = nr
            dst[0][rows, _IM] = ni
        if prev_ref is None:
            carry = (nr, ni)
            continue
        jp = jnp.where(rev, j - 1, j + 1)
        if k == SSM_JC - 1:
            inside = jnp.where((jp >= 0) & (jp < nj), 1.0, 0.0)
            jp = jnp.clip(jp, 0, nj - 1)
        prow = pl.ds(pl.multiple_of(jp * N_SEG, N_SEG), N_SEG)
        xr, xi = prev_ref[prow, _RE], prev_ref[prow, _IM]
        sr, si = nr * xr + ni * xi, ni * xr - nr * xi
        if k == SSM_JC - 1:
            sr, si = inside * sr, inside * si
        carry = (nr, ni, carry[2] + sr, carry[3] + si)
    return carry


def _segment_inits(ar, ai, end_r, end_i, rev, nj):
    pr, pi = ar, ai
    for _ in range(int(math.log2(nj))):
        pr, pi = pr * pr - pi * pi, 2.0 * pr * pi
    seg = lax.broadcasted_iota(jnp.int32, end_r.shape, 0)
    zero = jnp.zeros_like(end_r)

    def chain(shift, keep):
        ir, ii = zero, zero
        for _ in range(N_SEG - 1):
            tr, ti = _cfma(pr, pi, ir, ii, end_r, end_i)
            ir = jnp.where(keep, pltpu.roll(tr, shift, 0), 0.0)
            ii = jnp.where(keep, pltpu.roll(ti, shift, 0), 0.0)
        return ir, ii

    up_r, up_i = chain(1, seg >= 1)
    dn_r, dn_i = chain(N_SEG - 1, seg <= N_SEG - 2)
    return jnp.where(rev, dn_r, up_r), jnp.where(rev, dn_i, up_i)


def _ssm_specs(l):
    act = pl.BlockSpec((l, SSM_CH), lambda k, d: (0, k))
    bmat = pl.BlockSpec((None, None, SSM_CH, 2 * SSM_ST), lambda k, d: (d, k, 0, 0))
    cmat = pl.BlockSpec((None, None, 2 * SSM_ST, SSM_CH), lambda k, d: (d, k, 0, 0))
    lam = pl.BlockSpec((None, None, 1, SSM_ST), lambda k, d: (d, k, 0, 0))
    return act, bmat, cmat, lam


def _ssm_fwd(u_seg, bcat, ccat, lam_re, lam_im):
    l = u_seg.shape[0]
    nj = l // N_SEG
    nc = l // SSM_RC

    def body(u_ref, b_ref, c_ref, lr_ref, li_ref, y_ref, keep_ref, xs_ref, stage0, stage1, keep_sem):
        k, d = pl.program_id(0), pl.program_id(1)
        rev = d == 1
        shape = (N_SEG, SSM_ST)
        ar, ai = jnp.broadcast_to(lr_ref[...], shape), jnp.broadcast_to(li_ref[...], shape)
        zero = jnp.zeros(shape, F32)

        def inputs(ci, stage):
            rows = _chunk_rows(ci, rev, nc)
            bu = _dg(u_ref[rows, :], b_ref[...], NN)
            stage[...] = bu
            xs_ref[rows, :] = bu

        def first(stage, ci, carry):
            return _scan_chunk((stage, True), None, ar, ai, rev, nj, ci, carry)

        def first_pass(t, carry):
            inputs(2 * t + 1, stage1)
            carry = first(stage0, 2 * t, carry)
            inputs(2 * t + 2, stage0)
            return first(stage1, 2 * t + 1, carry)

        inputs(0, stage0)
        carry = lax.fori_loop(0, nc // 2 - 1, first_pass, (zero, zero))
        inputs(nc - 1, stage1)
        carry = first(stage0, nc - 2, carry)
        end_r, end_i = first(stage1, nc - 1, carry)
        init = _segment_inits(ar, ai, end_r, end_i, rev, nj)

        @pl.when(d == 0)
        def _():
            y_ref[...] = jnp.zeros_like(y_ref)

        def outputs(ci):
            rows = _chunk_rows(ci, rev, nc)
            y_ref[rows, :] += _dg(xs_ref[rows, :].astype(BF16), c_ref[...], NN)
            pltpu.make_async_copy(xs_ref.at[rows], keep_ref.at[d, k, rows], keep_sem).start()

        def second(ci, carry):
            return _scan_chunk((xs_ref, False), (xs_ref, False), ar, ai, rev, nj, ci, carry)

        def second_pass(ci, carry):
            outputs(ci - 1)
            return second(ci, carry)

        lax.fori_loop(1, nc, second_pass, second(0, init))
        outputs(nc - 1)
        pltpu.make_async_copy(xs_ref, keep_ref.at[d, k], keep_sem).wait()

    act, bmat, cmat, lam = _ssm_specs(l)
    return pl.pallas_call(
        body, grid=(SSM_CB, 2), in_specs=[act, bmat, cmat, lam, lam], out_specs=[act, ANY],
        out_shape=[jax.ShapeDtypeStruct((l, SSM_WIDTH), F32),
                   jax.ShapeDtypeStruct((2, SSM_CB, l, 2 * SSM_ST), F32)],
        scratch_shapes=[pltpu.VMEM((l, 2 * SSM_ST), F32), pltpu.VMEM((SSM_RC, 2 * SSM_ST), F32),
                        pltpu.VMEM((SSM_RC, 2 * SSM_ST), F32), pltpu.SemaphoreType.DMA],
        name="ssm_fwd", compiler_params=_params(("parallel", "arbitrary"), vmem_mb=56),
    )(u_seg, bcat.astype(BF16), ccat.astype(BF16), lam_re, lam_im)


def _ssm_bwd(u_seg, dy_seg, states, bcat, ccat, lam_re, lam_im):
    l = u_seg.shape[0]
    nj = l // N_SEG
    nc = l // SSM_RC

    def body(u_ref, dy_ref, keep_ref, b_ref, c_ref, lr_ref, li_ref,
             du_ref, db_ref, dc_ref, dlr_ref, dli_ref, xs_ref, gs_ref, stage0, stage1, keep_sem):
        k, d = pl.program_id(0), pl.program_id(1)
        rev = d == 1
        back = jnp.logical_not(rev)
        shape = (N_SEG, SSM_ST)
        ar, ai = jnp.broadcast_to(lr_ref[...], shape), -jnp.broadcast_to(li_ref[...], shape)
        zero = jnp.zeros(shape, F32)
        fetch = pltpu.make_async_copy(keep_ref.at[d, k], xs_ref, keep_sem)
        fetch.start()

        def inputs(ci, stage):
            rows = _chunk_rows(ci, back, nc)
            dx = _dg(dy_ref[rows, :], c_ref[...], NT)
            stage[...] = dx
            gs_ref[rows, :] = dx

        def first(stage, ci, carry):
            return _scan_chunk((stage, True), None, ar, ai, back, nj, ci, carry)

        def first_pass(t, carry):
            inputs(2 * t + 1, stage1)
            carry = first(stage0, 2 * t, carry)
            inputs(2 * t + 2, stage0)
            return first(stage1, 2 * t + 1, carry)

        inputs(0, stage0)
        carry = lax.fori_loop(0, nc // 2 - 1, first_pass, (zero, zero))
        inputs(nc - 1, stage1)
        carry = first(stage0, nc - 2, carry)
        end_r, end_i = first(stage1, nc - 1, carry)
        init = _segment_inits(ar, ai, end_r, end_i, back, nj)
        fetch.wait()
        db_ref[...] = jnp.zeros_like(db_ref)
        dc_ref[...] = jnp.zeros_like(dc_ref)

        @pl.when(d == 0)
        def _():
            du_ref[...] = jnp.zeros_like(du_ref)

        def outputs(ci, stage):
            rows = _chunk_rows(ci, back, nc)
            g = stage[...].astype(BF16)
            dc_ref[...] += _dg(xs_ref[rows, :].astype(BF16), dy_ref[rows, :], TN)
            db_ref[...] += _dg(u_ref[rows, :], g, TN)
            du_ref[rows, :] += _dg(g, b_ref[...], NT)

        def second(ci, stage, carry):
            return _scan_chunk((gs_ref, False), (stage, True), ar, ai, back, nj, ci, carry, prev_ref=xs_ref)

        def second_pass(t, carry):
            outputs(2 * t, stage0)
            carry = second(2 * t + 1, stage1, carry)
            outputs(2 * t + 1, stage1)
            return second(2 * t + 2, stage0, carry)

        carry = lax.fori_loop(0, nc // 2 - 1, second_pass, second(0, stage0, init + (zero, zero)))
        outputs(nc - 2, stage0)
        gr, gi, acc_r, acc_i = second(nc - 1, stage1, carry)
        outputs(nc - 1, stage1)

        seg = lax.broadcasted_iota(jnp.int32, shape, 0)
        jb = jnp.where(rev, nj - 1, 0)
        erow = pl.ds(pl.multiple_of((nj - 1 - jb) * N_SEG, N_SEG), N_SEG)

        def before(t):
            up = jnp.where(seg >= 1, pltpu.roll(t, 1, 0), 0.0)
            down = jnp.where(seg <= N_SEG - 2, pltpu.roll(t, N_SEG - 1, 0), 0.0)
            return jnp.where(rev, down, up)

        init_r, init_i = before(xs_ref[erow, _RE]), before(xs_ref[erow, _IM])
        acc_r = acc_r + gr * init_r + gi * init_i
        acc_i = acc_i + gi * init_r - gr * init_i
        dlr_ref[...] = jnp.sum(acc_r, axis=0, keepdims=True)
        dli_ref[...] = jnp.sum(acc_i, axis=0, keepdims=True)

    act, bmat, cmat, lam = _ssm_specs(l)
    return pl.pallas_call(
        body, grid=(SSM_CB, 2), in_specs=[act, act, ANY, bmat, cmat, lam, lam],
        out_specs=[act, bmat, cmat, lam, lam],
        out_shape=[jax.ShapeDtypeStruct((l, SSM_WIDTH), F32),
                   jax.ShapeDtypeStruct(bcat.shape, F32), jax.ShapeDtypeStruct(ccat.shape, F32),
                   jax.ShapeDtypeStruct(lam_re.shape, F32), jax.ShapeDtypeStruct(lam_im.shape, F32)],
        scratch_shapes=[pltpu.VMEM((l, 2 * SSM_ST), F32), pltpu.VMEM((l, 2 * SSM_ST), F32),
                        pltpu.VMEM((SSM_RC, 2 * SSM_ST), F32), pltpu.VMEM((SSM_RC, 2 * SSM_ST), F32),
                        pltpu.SemaphoreType.DMA],
        name="ssm_bwd", compiler_params=_params(("parallel", "arbitrary"), vmem_mb=58),
    )(u_seg, dy_seg, states, bcat.astype(BF16), ccat.astype(BF16), lam_re, lam_im)


def _glu_fwd(y_ssm, u, d_skip, w_glu):
    l, w = u.shape

    def body(y_ref, u_ref, d_ref, w_ref, pre_ref, s_ref, ys_ref):
        pre = y_ref[...] + d_ref[...] * u_ref[...]
        z = _gelu(pre)
        s = _dg(z.astype(BF16), w_ref[...], NN)
        pre_ref[...] = pre
        s_ref[...] = s
        ys_ref[...] = z * _sigmoid(s)

    row = pl.BlockSpec((TM_EW, w), lambda i: (i, 0))
    out = jax.ShapeDtypeStruct((l, w), F32)
    return pl.pallas_call(
        body, grid=(l // TM_EW,),
        in_specs=[row, row, pl.BlockSpec((1, w), lambda i: (0, 0)), pl.BlockSpec((w, w), lambda i: (0, 0))],
        out_specs=[row, row, row], out_shape=[out, out, out], name="glu_fwd",
        compiler_params=_params(("parallel",)),
    )(y_ssm, u, d_skip, w_glu)


def _glu_bwd(pre, s, dys, u, d_skip, w_glu):
    l, w = u.shape

    def body(pre_ref, s_ref, dys_ref, u_ref, d_ref, w_ref, dpre_ref, z_ref, ds_ref, dd_ref):
        pre, dys = pre_ref[...], dys_ref[...]
        z = _gelu(pre)
        sig = _sigmoid(s_ref[...])
        ds = (dys * z * sig * (1.0 - sig)).astype(BF16)
        dz = dys * sig + _dg(ds, w_ref[...], NT)
        dpre = dz * _gelu_grad(pre)
        dpre_ref[...] = dpre
        z_ref[...] = z.astype(BF16)
        ds_ref[...] = ds

        @pl.when(pl.program_id(0) == 0)
        def _():
            dd_ref[...] = jnp.zeros_like(dd_ref)

        dd_ref[...] += jnp.sum(dpre * u_ref[...], axis=0, keepdims=True)

    row = pl.BlockSpec((TM_EW, w), lambda i: (i, 0))
    vec = pl.BlockSpec((1, w), lambda i: (0, 0))
    return pl.pallas_call(
        body, grid=(l // TM_EW,),
        in_specs=[row, row, row, row, vec, pl.BlockSpec((w, w), lambda i: (0, 0))],
        out_specs=[row, row, row, vec],
        out_shape=[jax.ShapeDtypeStruct((l, w), F32), jax.ShapeDtypeStruct((l, w), BF16),
                   jax.ShapeDtypeStruct((l, w), BF16), jax.ShapeDtypeStruct((1, w), F32)],
        name="glu_bwd", compiler_params=_params(("arbitrary",)),
    )(pre, s, dys, u, d_skip, w_glu)


TM_CV = 512
TC_CV = 256
TM_CF = 256
TC_CF = D_FF // 2
HALO = SUBLANES


def _conv_specs(l, col0, tm=TM_CV, tc=TC_CV):
    per = tm // HALO
    nh = l // HALO
    off = col0 // tc
    return [
        pl.BlockSpec((HALO, tc), lambda j, i: (jnp.maximum(i * per - 1, 0), j + off)),
        pl.BlockSpec((tm, tc), lambda j, i: (i, j + off)),
        pl.BlockSpec((HALO, tc), lambda j, i: (jnp.minimum((i + 1) * per, nh - 1), j + off)),
    ]


def _ext(prev_ref, mid_ref, next_ref, first, last):
    p = jnp.where(first, 0.0, prev_ref[...])
    n = jnp.where(last, 0.0, next_ref[...])
    return jnp.concatenate([p, mid_ref[...], n], axis=0)


def _shift_dn(t):
    return pltpu.roll(t, 1, 0)


def _shift_up(t):
    return pltpu.roll(t, t.shape[0] - 1, 0)


def _conv3(e, w_ref, b_ref):
    return w_ref[0:1, :] * _shift_dn(e) + w_ref[1:2, :] * e + w_ref[2:3, :] * _shift_up(e) + b_ref[...]


def _convffn_fwd(up_pre, conv_w, conv_b):
    l = up_pre.shape[0]
    tm, tc = TM_CF, TC_CF
    ni = l // tm
    wspec = lambda off: pl.BlockSpec((3, tc), lambda j, i: (0, j + off))
    bspec = lambda off: pl.BlockSpec((1, tc), lambda j, i: (0, j + off))
    voff = D_FF // tc

    def body(gp, gm, gn, vp, vm, vn, wg, bg, wv, bv, o_ref):
        i = pl.program_id(1)
        first, last = i == 0, i == ni - 1
        gate = _conv3(_ext(gp, gm, gn, first, last), wg, bg)[HALO:HALO + tm]
        val = _conv3(_ext(vp, vm, vn, first, last), wv, bv)[HALO:HALO + tm]
        o_ref[...] = (gate * _sigmoid(gate) * val).astype(BF16)

    return pl.pallas_call(
        body, grid=(D_FF // tc, ni),
        in_specs=_conv_specs(l, 0, tm, tc) + _conv_specs(l, D_FF, tm, tc)
        + [wspec(0), bspec(0), wspec(voff), bspec(voff)],
        out_specs=pl.BlockSpec((tm, tc), lambda j, i: (i, j)),
        out_shape=jax.ShapeDtypeStruct((l, D_FF), BF16), name="convffn_fwd",
        compiler_params=_params(("parallel", "parallel")),
    )(up_pre, up_pre, up_pre, up_pre, up_pre, up_pre, conv_w, conv_b, conv_w, conv_b)


HALO_B = 2 * SUBLANES


def _convffn_bwd(up_pre, dx2b, w_down, conv_w, conv_b):
    l = up_pre.shape[0]
    ni = l // TM_CV
    d = dx2b.shape[1]
    wspec = lambda off: pl.BlockSpec((3, TC_CV), lambda i, j: (0, j + off))
    bspec = lambda off: pl.BlockSpec((1, TC_CV), lambda i, j: (0, j + off))
    voff = D_FF // TC_CV
    swap = lambda spec: pl.BlockSpec(spec.block_shape, lambda i, j, f=spec.index_map: f(j, i))
    per, nh = TM_CV // HALO_B, l // HALO_B
    dx_specs = [pl.BlockSpec((HALO_B, d), lambda i, j: (jnp.maximum(i * per - 1, 0), 0)),
                pl.BlockSpec((TM_CV, d), lambda i, j: (i, 0)),
                pl.BlockSpec((HALO_B, d), lambda i, j: (jnp.minimum((i + 1) * per, nh - 1), 0))]

    def body(gp, gm, gn, vp, vm, vn, xp, xm, xn, wd, wg, bg, wv, bv, dup_ref, pg_ref, pv_ref):
        i = pl.program_id(0)
        first, last = i == 0, i == ni - 1
        ge, ve = _ext(gp, gm, gn, first, last), _ext(vp, vm, vn, first, last)
        zero = jnp.zeros((HALO_B, d), BF16)
        dx = jnp.concatenate([jnp.where(first, zero, xp[...]), xm[...], jnp.where(last, zero, xn[...])], axis=0)
        de = _dg(dx, wd[...], NT)[HALO_B - HALO:HALO_B + TM_CV + HALO]
        taps = [(_shift_dn(e), e, _shift_up(e)) for e in (ge, ve)]
        conv = lambda t, w_ref, b_ref: w_ref[0:1, :] * t[0] + w_ref[1:2, :] * t[1] + w_ref[2:3, :] * t[2] + b_ref[...]
        gate, val = conv(taps[0], wg, bg), conv(taps[1], wv, bv)
        sig = _sigmoid(gate)
        silu = gate * sig
        dgate = de * val * (sig + silu * (1.0 - sig))
        dval = de * silu
        mid = slice(HALO, HALO + TM_CV)
        rid = lax.broadcasted_iota(jnp.int32, (SUBLANES, TC_CV), 0)
        for half, (dup, tap, w_ref, p_ref) in enumerate(((dgate, taps[0], wg, pg_ref), (dval, taps[1], wv, pv_ref))):
            dpre = w_ref[0:1, :] * _shift_up(dup) + w_ref[1:2, :] * dup + w_ref[2:3, :] * _shift_dn(dup)
            dup_ref[half] = dpre[mid].astype(BF16)
            dm_ = dup[mid]
            sums = [jnp.sum(dm_ * t[mid], axis=0, keepdims=True) for t in tap]
            sums.append(jnp.sum(dm_, axis=0, keepdims=True))
            acc = jnp.zeros((SUBLANES, TC_CV), F32)
            for k, sk in enumerate(sums):
                acc = jnp.where(rid == k, sk, acc)
            p_ref[...] = acc

    par = pl.BlockSpec((None, SUBLANES, TC_CV), lambda i, j: (i, 0, j))
    dup, pg, pv = pl.pallas_call(
        body, grid=(ni, D_FF // TC_CV),
        in_specs=[swap(s) for s in _conv_specs(l, 0) + _conv_specs(l, D_FF)] + dx_specs
        + [pl.BlockSpec((TC_CV, d), lambda i, j: (j, 0)), wspec(0), bspec(0), wspec(voff), bspec(voff)],
        out_specs=[pl.BlockSpec((2, TM_CV, TC_CV), lambda i, j: (0, i, j)), par, par],
        out_shape=[jax.ShapeDtypeStruct((2, l, D_FF), BF16),
                   jax.ShapeDtypeStruct((ni, SUBLANES, D_FF), F32), jax.ShapeDtypeStruct((ni, SUBLANES, D_FF), F32)],
        name="convffn_bwd", compiler_params=_params(("parallel", "parallel")),
    )(up_pre, up_pre, up_pre, up_pre, up_pre, up_pre, dx2b, dx2b, dx2b, w_down, conv_w, conv_b, conv_w, conv_b)
    return dup, jnp.concatenate([jnp.sum(pg, axis=0), jnp.sum(pv, axis=0)], axis=1)


def _local_step(x, target, wb, sp, mixer_weights=None, late_weights=None, ffn_grads_ready=None,
                ffn_grads_next=None):
    l = x.shape[0]
    tabs = _rope_tables(l)
    disc = _ssm_disc(sp["a_re"], sp["a_im"], sp["log_step"], sp["b_re"], sp["b_im"])
    bcat, ccat, lam_re, lam_im = _ssm_pack(*disc, sp["c_re"], sp["c_im"])
    d_skip = sp["d_skip"].reshape(1, SSM_WIDTH)

    h, qkv, u = _rms_mm_rope(x, sp["norm_mix_g"], wb["w_in"], tabs, "mm_in")
    attn, lse = _attn_fwd(qkv, sp["sink"])
    u_seg = _to_segments(u).astype(BF16)
    y_seg, states = _ssm_fwd(u_seg, bcat, ccat, lam_re, lam_im)
    y_ssm = _from_segments(y_seg)
    if mixer_weights is not None:
        wb = dict(wb, **mixer_weights(attn))
    pre, s_glu, ys = _glu_fwd(y_ssm, u, d_skip, wb["w_glu"])
    mixed, x1, h2 = _mix_mm_res_rms(attn, ys, sp["norm_attn_g"], sp["norm_ssm_g"], wb["w_out"], x,
                                    sp["norm_ffn_g"], "mm_out")
    if late_weights is not None:
        wb = dict(wb, **late_weights(h2))
    up_pre = _mm_nn_cols(h2, wb["w_up"], min(l, 2048), "mm_up")
    conv_w = wb["conv_w"]
    act = _convffn_fwd(up_pre, conv_w, sp["conv_b"])
    loss, dx2, dx2b, d_final_g = _mm_res_loss(act, wb["w_down"], x1, sp["norm_final_g"].reshape(1, D_MODEL), target)

    g = {"norm_final_g": d_final_g.reshape(D_MODEL)}
    g["w_down"] = _mm_tn(act, dx2b, D_FF // 2, 512, "mm_down_dw")
    dup_pre, conv_par = _convffn_bwd(up_pre, dx2b, wb["w_down"], conv_w, sp["conv_b"])
    g["conv_w"], g["conv_b"] = conv_par[0:3], conv_par[3:4]
    g["w_up"] = _mm_tn_cols(h2, dup_pre, wb["w_up"].shape[0], 512, "mm_up_dw")
    zero = ffn_grads_ready(g["w_up"], g["w_down"]) if ffn_grads_ready is not None else 0.0
    dx1, dx1b, g["norm_ffn_g"] = _mm_cols_rms_bwd(dup_pre, wb["w_up"], x1, sp["norm_ffn_g"] + zero, dx2, "mm_up_dx")
    g["w_out"] = _mm_tn(mixed, dx1b, 1024, 1024, "mm_out_dw")
    zero = ffn_grads_next(g["w_out"]) if ffn_grads_next is not None else 0.0
    dattn, dys, g["norm_attn_g"], g["norm_ssm_g"] = _mm_mix_bwd(
        dx1b, wb["w_out"], attn, ys, sp["norm_attn_g"] + zero, sp["norm_ssm_g"], "mm_out_dx")
    dpre, zb, dsb, dd = _glu_bwd(pre, s_glu, dys, u, d_skip, wb["w_glu"])
    g["d_skip"] = dd.reshape(N_SSM_GROUPS, SSM_GROUP)
    g["w_glu"] = _mm_tn(zb, dsb, 512, 512, "mm_glu_dw")
    du_seg, dbcat, dccat, dlam_re, dlam_im = _ssm_bwd(u_seg, _to_segments(dpre).astype(BF16), states, bcat, ccat,
                                                      lam_re, lam_im)
    dlb_re, dlb_im, dbb_re, dbb_im, g["c_re"], g["c_im"] = _ssm_unpack(dbcat, dccat, dlam_re, dlam_im)
    _, disc_vjp = jax.vjp(_ssm_disc, sp["a_re"], sp["a_im"], sp["log_step"], sp["b_re"], sp["b_im"])
    g["a_re"], g["a_im"], g["log_step"], g["b_re"], g["b_im"] = disc_vjp((dlb_re, dlb_im, dbb_re, dbb_im))
    dq, dkv, g["sink"] = _attn_bwd(qkv, attn, dattn, lse, sp["sink"])
    dproj = _rope_bwd(dq, dkv, _from_segments(du_seg), dpre, d_skip, tabs)
    g["w_in"] = _mm_tn(dproj, h, IN_WIDTH // 5, D_MODEL, "mm_in_dw")
    grad_x, _, g["norm_mix_g"] = _mm_nn_rms_bwd(dproj, wb["w_in"], x, sp["norm_mix_g"], dx1, "mm_in_dx")
    return loss, grad_x, g


MESH = pl.DeviceIdType.MESH
ANY = pl.BlockSpec(memory_space=pl.ANY)


def _place():
    x, y, c = lax.axis_index("x"), lax.axis_index("y"), lax.axis_index("c")
    chips = [(1 - x, y), (x, 1 - y), (1 - x, 1 - y)]
    return x, y, c, chips


def _chip_index(px, py):
    return 2 * px + py


CHUNK_BYTES = 256 * 1024
MAX_CHUNKS = 16


def _row_chunks(rows, row_bytes, align):
    n = max(1, min(MAX_CHUNKS, (rows * row_bytes) // CHUNK_BYTES))
    per = -(-rows // n)
    per = -(-per // align) * align
    return [(r0, min(per, rows - r0)) for r0 in range(0, rows, per)]


def _align_of(dtype):
    return SUBLANES * 4 // jnp.dtype(dtype).itemsize


def _remote(src, dst, send_sem, recv_sem, to):
    return pltpu.make_async_remote_copy(src_ref=src, dst_ref=dst, send_sem=send_sem, recv_sem=recv_sem,
                                        device_id=to, device_id_type=MESH)


CAST_ROWS = 64


def _gather_weights(shards, dtypes):
    nw = len(shards)

    def body(*refs):
        w_refs, o_refs = refs[:nw], refs[nw:2 * nw]
        send_sems, recv_sems, in_sems, out_sems = refs[2 * nw:2 * nw + 4]
        raw, cast = refs[2 * nw + 4:3 * nw + 4], refs[3 * nw + 4:]
        x, y, c, chips = _place()
        mine = _chip_index(x, y)
        sibling = (x, y, 1 - c)

        def rows_of(ref, chip, r0, nr):
            return ref.at[chip, pl.ds(r0, nr), :]

        def copy(wi, k, src, dst, to):
            return _remote(src, dst, send_sems.at[wi, k], recv_sems.at[wi, k], to)

        geo = []
        for wi in range(nw):
            rows, cols = w_refs[wi].shape
            row_bytes = cols * jnp.dtype(dtypes[wi]).itemsize
            geo.append((rows // 2, _row_chunks(rows // 2, row_bytes, _align_of(dtypes[wi]))))

        stage_in = [pltpu.make_async_copy(w_refs[wi], raw[wi], in_sems.at[wi]) for wi in range(nw)]
        for cp in stage_in:
            cp.start()
        staged = [raw[wi] if dtypes[wi] == w_refs[wi].dtype else cast[wi] for wi in range(nw)]
        stage_out = []
        for wi in range(nw):
            stage_in[wi].wait()
            if staged[wi] is not raw[wi]:
                def cast_rows(i, _, wi=wi):
                    rows = pl.ds(pl.multiple_of(i * CAST_ROWS, CAST_ROWS), CAST_ROWS)
                    cast[wi][rows, :] = raw[wi][rows, :].astype(dtypes[wi])
                    return 0

                lax.fori_loop(0, w_refs[wi].shape[0] // CAST_ROWS, cast_rows, 0)
            cp = pltpu.make_async_copy(staged[wi], o_refs[wi].at[mine], out_sems.at[wi])
            cp.start()
            stage_out.append(cp)

        for wi in range(nw):
            hr, half_chunks = geo[wi]
            for j, chip in enumerate(chips):
                for r0, nr in half_chunks:
                    copy(wi, j, staged[wi].at[pl.ds(c * hr + r0, nr), :],
                         rows_of(o_refs[wi], mine, c * hr + r0, nr), (*chip, c)).start()
        for wi in range(nw):
            hr, half_chunks = geo[wi]
            for j, chip in enumerate(chips):
                got = rows_of(o_refs[wi], _chip_index(*chip), c * hr, hr)
                copy(wi, j, got, got, (*chip, c)).wait_recv()
                for r0, nr in half_chunks:
                    piece = rows_of(o_refs[wi], _chip_index(*chip), c * hr + r0, nr)
                    copy(wi, 3 + j, piece, piece, sibling).start()
        for wi in range(nw):
            hr = geo[wi][0]
            for j, chip in enumerate(chips):
                got = rows_of(o_refs[wi], _chip_index(*chip), (1 - c) * hr, hr)
                copy(wi, 3 + j, got, got, sibling).wait_recv()
        for wi in range(nw):
            hr = geo[wi][0]
            sent = rows_of(o_refs[wi], mine, c * hr, hr)
            for k in range(6):
                copy(wi, k, sent, sent, sibling).wait_send()
            stage_out[wi].wait()

    return pl.pallas_call(
        body, in_specs=[ANY] * nw, out_specs=[ANY] * nw,
        out_shape=[jax.ShapeDtypeStruct((4, *s.shape), t) for s, t in zip(shards, dtypes)],
        scratch_shapes=[pltpu.SemaphoreType.DMA((nw, 6)), pltpu.SemaphoreType.DMA((nw, 6)),
                        pltpu.SemaphoreType.DMA((nw,)), pltpu.SemaphoreType.DMA((nw,))]
        + [pltpu.VMEM(s.shape, s.dtype) for s in shards] + [pltpu.VMEM(s.shape, t) for s, t in zip(shards, dtypes)],
        name="gather_weights", compiler_params=_params(vmem_mb=40),
    )(*shards)


HBM = pl.BlockSpec(memory_space=pltpu.HBM)
SEM = pl.BlockSpec(memory_space=pltpu.SEMAPHORE)
EFFECT = pltpu.SideEffectType.DATAFLOW_SIDE_EFFECTING


def _cast_place(w, place, dtype, after, name):
    rows, cols = w.shape
    tr = _row_tile(rows, cols, _align_of(dtype))

    def body(p_ref, w_ref, after_ref, o_ref):
        del p_ref, after_ref
        o_ref[...] = w_ref[...].astype(dtype)

    grid_spec = pltpu.PrefetchScalarGridSpec(
        num_scalar_prefetch=1, grid=(rows // tr,),
        in_specs=[pl.BlockSpec((tr, cols), lambda i, p: (i, 0)), ANY],
        out_specs=pl.BlockSpec((None, tr, cols), lambda i, p: (p[1], i, 0)))
    return pl.pallas_call(body, grid_spec=grid_spec, out_shape=jax.ShapeDtypeStruct((4, rows, cols), dtype),
                          name=name, compiler_params=_params(("parallel",)))(place, w, after)


def _split_start(name, arrays, n_pairs, issue):
    n = len(arrays)

    def body(*refs):
        issue(refs[:n], refs[n:n + n_pairs], refs[n + n_pairs:n + 2 * n_pairs])
        token = refs[2 * n + 2 * n_pairs]
        token[...] = jnp.zeros_like(token)

    dma = pltpu.SemaphoreType.DMA(())
    outs = pl.pallas_call(
        body, name=name,
        out_shape=[dma] * (2 * n_pairs) + [pltpu.HBM(t.shape, t.dtype) for t in arrays]
        + [jax.ShapeDtypeStruct((SUBLANES, LANES), F32)],
        in_specs=[HBM] * n, out_specs=[SEM] * (2 * n_pairs) + [HBM] * n + [pl.BlockSpec(memory_space=pltpu.VMEM)],
        input_output_aliases={a: 2 * n_pairs + a for a in range(n)},
        compiler_params=pltpu.CompilerParams(has_side_effects=EFFECT),
    )(*[pltpu.with_memory_space_constraint(t, pltpu.HBM) for t in arrays])
    return outs[:n_pairs], outs[n_pairs:2 * n_pairs], outs[2 * n_pairs:2 * n_pairs + n], outs[-1]


def _split_wait(name, send_sems, recv_sems, flying, sizes, after):
    n, n_pairs = len(flying), len(send_sems)

    def body(*refs):
        x, y, c, _ = _place()
        for k, ref in enumerate(sizes(refs[:n])):
            cp = _remote(ref, ref, refs[n + k], refs[n + n_pairs + k], (x, y, 1 - c))
            cp.wait_send()
            cp.wait_recv()

    return pl.pallas_call(
        body, name=name, out_shape=[pltpu.HBM(t.shape, t.dtype) for t in flying],
        in_specs=[HBM] * n + [SEM] * (2 * n_pairs) + [ANY], out_specs=[HBM] * n,
        input_output_aliases={a: a for a in range(n)},
        compiler_params=pltpu.CompilerParams(has_side_effects=EFFECT),
    )(*flying, *send_sems, *recv_sems, after)


def _spread_start(lands, name):
    def issue(land_refs, send_sems, recv_sems):
        x, y, c, chips = _place()
        mine = _chip_index(x, y)
        for a, land in enumerate(land_refs):
            _, rows, cols = land.shape
            hr = rows // 2
            row_bytes = cols * jnp.dtype(land.dtype).itemsize
            for r0, nr in _row_chunks(hr, row_bytes, _align_of(land.dtype)):
                piece = land.at[mine, pl.ds(c * hr + r0, nr), :]
                for chip in chips:
                    for core in (0, 1):
                        _remote(piece, piece, send_sems[a], recv_sems[a], (*chip, core)).start()

    return _split_start(name, lands, len(lands), issue)


def _spread_wait(send_sems, recv_sems, flying, after, name):
    return _split_wait(name, send_sems, recv_sems, flying, lambda refs: [r.at[pl.ds(0, 3)] for r in refs], after)


def _pair_start(grads):
    n = len(grads)
    zones = [lax.empty((4, g.shape[1] // 2, g.shape[2]), F32) for g in grads]

    def issue(refs, send_sems, recv_sems):
        x, y, c, _ = _place()
        for a in range(n):
            g_ref, z_ref = refs[a], refs[n + a]
            _, rows, cols = g_ref.shape
            hr = rows // 2
            for k in range(4):
                for r0, nr in _row_chunks(hr, cols * 4, SUBLANES):
                    _remote(g_ref.at[k, pl.ds((1 - c) * hr + r0, nr), :], z_ref.at[k, pl.ds(r0, nr), :],
                            send_sems[a], recv_sems[a], (x, y, 1 - c)).start()

    return _split_start("pair_start", list(grads) + zones, n, issue)


def _pair_wait(send_sems, recv_sems, flying, after):
    n = len(flying) // 2
    out = _split_wait("pair_wait", send_sems, recv_sems, flying, lambda refs: list(refs[n:]), after)
    return out[:n], out[n:]


def _chip_start(sums):
    n = len(sums)
    zones = [lax.empty((3, *s.shape[1:]), s.dtype) for s in sums]

    def issue(refs, send_sems, recv_sems):
        x, y, c, chips = _place()
        for a in range(n):
            s_ref, z_ref = refs[a], refs[n + a]
            _, rows, cols = s_ref.shape
            row_bytes = cols * jnp.dtype(s_ref.dtype).itemsize
            for r0, nr in _row_chunks(rows, row_bytes, _align_of(s_ref.dtype)):
                for j, chip in enumerate(chips):
                    _remote(s_ref.at[_chip_index(*chip), pl.ds(r0, nr), :], z_ref.at[j, pl.ds(r0, nr), :],
                            send_sems[a], recv_sems[a], (*chip, c)).start()

    return _split_start("chip_start", list(sums) + zones, n, issue)


def _chip_wait(send_sems, recv_sems, flying, after):
    n = len(flying) // 2
    return _split_wait("chip_wait", send_sems, recv_sems, flying, lambda refs: list(refs[n:]), after)[n:]


def _pair_exchange(grads):
    na = len(grads)

    def body(*refs):
        g_refs, o_refs = refs[:na], refs[na:2 * na]
        send_sems, recv_sems = refs[2 * na:]
        x, y, c, _ = _place()
        sibling = (x, y, 1 - c)
        for ai in range(na):
            _, rows, cols = g_refs[ai].shape
            hr = rows // 2
            for k in range(4):
                for r0, nr in _row_chunks(hr, cols * 4, SUBLANES):
                    _remote(g_refs[ai].at[k, pl.ds((1 - c) * hr + r0, nr), :], o_refs[ai].at[k, pl.ds(r0, nr), :],
                            send_sems.at[ai], recv_sems.at[ai], sibling).start()
        for ai in range(na):
            _remote(o_refs[ai], o_refs[ai], send_sems.at[ai], recv_sems.at[ai], sibling).wait()

    return pl.pallas_call(
        body, in_specs=[ANY] * na, out_specs=[ANY] * na,
        out_shape=[jax.ShapeDtypeStruct((4, g.shape[1] // 2, g.shape[2]), F32) for g in grads],
        scratch_shapes=[pltpu.SemaphoreType.DMA((na,)), pltpu.SemaphoreType.DMA((na,))],
        name="pair_exchange",
    )(*grads)


def _row_tile(rows, cols, align):
    best = align
    for cand in range(align, rows + 1, align):
        if rows % cand == 0 and cand * cols <= 256 * 1024:
            best = cand
    return best


def _pair_sum(g, got, place, transit, name):
    _, rows, cols = g.shape
    hr = rows // 2
    tr = _row_tile(hr, cols, _align_of(transit))
    nt = hr // tr

    def body(p_ref, g_ref, r_ref, s_ref, own_ref):
        total = g_ref[...] + r_ref[...]
        s_ref[...] = total.astype(transit)

        @pl.when(pl.program_id(1) == p_ref[1])
        def _():
            own_ref[...] = total

    grid_spec = pltpu.PrefetchScalarGridSpec(
        num_scalar_prefetch=1, grid=(nt, 4),
        in_specs=[pl.BlockSpec((None, tr, cols), lambda i, k, p: (k, p[0] * nt + i, 0)),
                  pl.BlockSpec((None, tr, cols), lambda i, k, p: (k, i, 0))],
        out_specs=[pl.BlockSpec((None, tr, cols), lambda i, k, p: (k, i, 0)),
                   pl.BlockSpec((tr, cols), lambda i, k, p: (i, 0))])
    return pl.pallas_call(
        body, grid_spec=grid_spec,
        out_shape=[jax.ShapeDtypeStruct((4, hr, cols), transit), jax.ShapeDtypeStruct((hr, cols), F32)],
        name=name, compiler_params=_params(("parallel", "arbitrary")),
    )(place, g, got)


def _chip_exchange(sums):
    na = len(sums)

    def body(*refs):
        s_refs, o_refs = refs[:na], refs[na:2 * na]
        send_sems, recv_sems = refs[2 * na:]
        x, y, c, chips = _place()
        for ai in range(na):
            _, rows, cols = s_refs[ai].shape
            row_bytes = cols * jnp.dtype(s_refs[ai].dtype).itemsize
            for r0, nr in _row_chunks(rows, row_bytes, _align_of(s_refs[ai].dtype)):
                for j, chip in enumerate(chips):
                    _remote(s_refs[ai].at[_chip_index(*chip), pl.ds(r0, nr), :], o_refs[ai].at[j, pl.ds(r0, nr), :],
                            send_sems.at[ai, j], recv_sems.at[ai, j], (*chip, c)).start()
        for ai in range(na):
            for j, chip in enumerate(chips):
                _remote(o_refs[ai].at[j], o_refs[ai].at[j], send_sems.at[ai, j], recv_sems.at[ai, j],
                        (*chip, c)).wait()

    return pl.pallas_call(
        body, in_specs=[ANY] * na, out_specs=[ANY] * na,
        out_shape=[jax.ShapeDtypeStruct((3, *s.shape[1:]), s.dtype) for s in sums],
        scratch_shapes=[pltpu.SemaphoreType.DMA((na, 3)), pltpu.SemaphoreType.DMA((na, 3))],
        name="chip_exchange",
    )(*sums)


def _chip_sum(own, landed, name):
    hr, cols = own.shape
    tr = _row_tile(hr, cols, _align_of(landed.dtype))

    def body(o_ref, l_ref, f_ref):
        acc = o_ref[...]
        for j in range(3):
            acc = acc + l_ref[j].astype(F32)
        f_ref[...] = acc

    return pl.pallas_call(
        body, grid=(hr // tr,),
        in_specs=[pl.BlockSpec((tr, cols), lambda i: (i, 0)), pl.BlockSpec((3, tr, cols), lambda i: (0, i, 0))],
        out_specs=pl.BlockSpec((tr, cols), lambda i: (i, 0)),
        out_shape=jax.ShapeDtypeStruct((hr, cols), F32), name=name,
        compiler_params=_params(("parallel",)),
    )(own, landed)


def _final_exchange(halves, small):
    nh = len(halves)

    def body(*refs):
        h_refs, s_ref = refs[:nh], refs[nh]
        o_refs, so_ref = refs[nh + 1:2 * nh + 1], refs[2 * nh + 1]
        send_sems, recv_sems, local_sem, ssend_sems, srecv_sems = refs[2 * nh + 2:]
        x, y, c, _ = _place()
        me = 4 * x + 2 * y + c
        sibling = (x, y, 1 - c)
        for hi in range(nh):
            hr, cols = h_refs[hi].shape
            for r0, nr in _row_chunks(hr, cols * 4, SUBLANES):
                _remote(h_refs[hi].at[pl.ds(r0, nr), :], o_refs[hi].at[pl.ds(r0, nr), :],
                        send_sems.at[hi], recv_sems.at[hi], sibling).start()
        small_cps = [pltpu.make_async_copy(s_ref, so_ref.at[me], local_sem)]
        for r in range(1, 8):
            fx, fy, fc = (r >> 2) & 1, (r >> 1) & 1, r & 1
            peer = (1 - x if fx else x, 1 - y if fy else y, 1 - c if fc else c)
            small_cps.append(_remote(s_ref, so_ref.at[me], ssend_sems.at[r - 1], srecv_sems.at[r - 1], peer))
        for cp in small_cps:
            cp.start()
        for hi in range(nh):
            _remote(h_refs[hi], o_refs[hi], send_sems.at[hi], recv_sems.at[hi], sibling).wait()
        for cp in small_cps:
            cp.wait()

    return pl.pallas_call(
        body, in_specs=[ANY] * (nh + 1), out_specs=[ANY] * (nh + 1),
        out_shape=[jax.ShapeDtypeStruct(h.shape, F32) for h in halves]
        + [jax.ShapeDtypeStruct((8, *small.shape), F32)],
        scratch_shapes=[pltpu.SemaphoreType.DMA((nh,)), pltpu.SemaphoreType.DMA((nh,)),
                        pltpu.SemaphoreType.DMA, pltpu.SemaphoreType.DMA((7,)), pltpu.SemaphoreType.DMA((7,))],
        name="final_exchange",
    )(*halves, small)


def _adamw_halves(w, own, other, m, v, place, name):
    r, c = w.shape
    hr = r // 2
    tr = _row_tile(hr, c, SUBLANES)
    nt = hr // tr
    c1 = 1.0 - ADAM_B1 ** ADAM_STEP
    c2 = 1.0 - ADAM_B2 ** ADAM_STEP

    def body(p_ref, w_ref, own_ref, other_ref, m_ref, v_ref, g_ref, d_ref, nm_ref, nv_ref):
        mine = pl.program_id(0) // nt == p_ref[0]
        gv = jnp.where(mine, own_ref[...], other_ref[...])
        nm = ADAM_B1 * m_ref[...] + (1.0 - ADAM_B1) * gv
        nv = ADAM_B2 * v_ref[...] + (1.0 - ADAM_B2) * (gv * gv)
        g_ref[...] = gv
        d_ref[...] = -ADAM_LR * ((nm / c1) / (jnp.sqrt(nv / c2) + ADAM_EPS) + ADAM_WD * w_ref[...])
        nm_ref[...] = nm
        nv_ref[...] = nv

    full = pl.BlockSpec((tr, c), lambda i, p: (i, 0))
    half = pl.BlockSpec((tr, c), lambda i, p: (i % nt, 0))
    out = jax.ShapeDtypeStruct((r, c), F32)
    grid_spec = pltpu.PrefetchScalarGridSpec(num_scalar_prefetch=1, grid=(2 * nt,),
                                             in_specs=[full, half, half, full, full], out_specs=[full] * 4)
    return pl.pallas_call(body, grid_spec=grid_spec, out_shape=[out] * 4, name=name,
                          compiler_params=_params(("parallel",)))(place, w, own, other, m, v)


def _adamw_many(ws, gs, ms, vs, name):
    n = len(ws)
    c1 = 1.0 - ADAM_B1 ** ADAM_STEP
    c2 = 1.0 - ADAM_B2 ** ADAM_STEP

    def body(*refs):
        w_refs, g_refs, m_refs, v_refs = (refs[k * n:(k + 1) * n] for k in range(4))
        d_refs, nm_refs, nv_refs = (refs[(4 + k) * n:(5 + k) * n] for k in range(3))
        for i in range(n):
            gv = g_refs[i][...]
            nm = ADAM_B1 * m_refs[i][...] + (1.0 - ADAM_B1) * gv
            nv = ADAM_B2 * v_refs[i][...] + (1.0 - ADAM_B2) * (gv * gv)
            d_refs[i][...] = -ADAM_LR * ((nm / c1) / (jnp.sqrt(nv / c2) + ADAM_EPS) + ADAM_WD * w_refs[i][...])
            nm_refs[i][...] = nm
            nv_refs[i][...] = nv

    vmem = pl.BlockSpec(memory_space=pltpu.VMEM)
    shapes = [jax.ShapeDtypeStruct(t.shape, F32) for t in ws]
    outs = pl.pallas_call(body, in_specs=[vmem] * (4 * n), out_specs=[vmem] * (3 * n), out_shape=shapes * 3,
                          name=name, compiler_params=_params(vmem_mb=56))(*ws, *gs, *ms, *vs)
    return outs[:n], outs[n:2 * n], outs[2 * n:]


BIG = ("w_in", "w_glu", "w_out", "w_up", "w_down")
WEIGHTS = ("norm_mix_g", "w_in", "a_re", "a_im", "log_step", "b_re", "b_im", "c_re", "c_im", "d_skip", "w_glu",
           "sink", "norm_attn_g", "norm_ssm_g", "w_out", "norm_ffn_g", "w_up", "conv_w", "conv_b", "w_down",
           "norm_final_g")
SMALL = ("norm_mix_g", "a_re", "a_im", "log_step", "b_re", "b_im", "c_re", "c_im", "d_skip", "sink",
         "norm_attn_g", "norm_ssm_g", "norm_ffn_g", "conv_w", "conv_b", "norm_final_g")
SMALL_ROWS = 48
N_DEV = 8


def _tile_rows(size):
    return -(-size // (SUBLANES * D_MODEL)) * SUBLANES


def _by_owner(name, g):
    if name == "w_up":
        return g
    return g.reshape(4, g.shape[0] // 4, g.shape[1])


def _view(name, t):
    if name == "w_in":
        return jnp.swapaxes(t[0], 0, 1)
    if name in ("b_re", "b_im"):
        return jnp.swapaxes(t, -1, -2)
    return t


def _unview(name, t):
    if name == "w_in":
        return jnp.swapaxes(t, 0, 1)[None]
    if name in ("b_re", "b_im"):
        return jnp.swapaxes(t, -1, -2)
    return t


def kernel(x, norm_mix_g, w_in, a_re, a_im, log_step, b_re, b_im, c_re, c_im, d_skip, w_glu, sink, norm_attn_g, norm_ssm_g, w_out, norm_ffn_g, w_up, conv_w, conv_b, w_down, norm_final_g, loss_target, m_norm_mix_g, m_w_in, m_a_re, m_a_im, m_log_step, m_b_re, m_b_im, m_c_re, m_c_im, m_d_skip, m_w_glu, m_sink, m_norm_attn_g, m_norm_ssm_g, m_w_out, m_norm_ffn_g, m_w_up, m_conv_w, m_conv_b, m_w_down, m_norm_final_g, v_norm_mix_g, v_w_in, v_a_re, v_a_im, v_log_step, v_b_re, v_b_im, v_c_re, v_c_im, v_d_skip, v_w_glu, v_sink, v_norm_attn_g, v_norm_ssm_g, v_w_out, v_norm_ffn_g, v_w_up, v_conv_w, v_conv_b, v_w_down, v_norm_final_g):
    given = dict(locals())
    w = {n: given[n] for n in WEIGHTS}
    m = {n: given["m_" + n] for n in WEIGHTS}
    v = {n: given["v_" + n] for n in WEIGHTS}
    xy = 2 * lax.axis_index("x") + lax.axis_index("y")

    core = lax.axis_index("c")
    place = jnp.stack([core, xy]).astype(jnp.int32)

    conv_rows = jnp.pad(w["conv_w"][0], ((0, 2 * SUBLANES - 3), (0, 0)))
    rows = lambda t: t.reshape(4 * t.shape[1], t.shape[2])
    (w_in_all,) = _gather_weights([_view("w_in", w["w_in"])], [BF16])
    wb = {"w_in": rows(w_in_all)}
    early = ("w_in", "w_glu", "w_out")
    mixer = [_cast_place(w[n][0], place, BF16, w_in_all, "cast_" + n) for n in ("w_glu", "w_out")]
    mixer.append(_cast_place(conv_rows, place, F32, w_in_all, "cast_conv_w"))
    *mixer_flight, mixer_token = _spread_start(mixer, "spread_mixer_start")
    late = ("w_up", "w_down")
    *late_flight, token = _spread_start(
        [_cast_place(w[n][0], place, BF16, mixer_token, "cast_" + n) for n in late], "spread_ffn_start")

    def mixer_weights(after):
        w_glu4, w_out4, conv4 = _spread_wait(*mixer_flight, after, "spread_mixer_wait")
        return {"w_glu": rows(w_glu4), "w_out": rows(w_out4),
                "conv_w": conv4[:, :3].transpose(1, 0, 2).reshape(3, 2 * D_FF)}

    def late_weights(after):
        w_up4, w_down4 = _spread_wait(*late_flight, after, "spread_ffn_wait")
        return {"w_up": w_up4, "w_down": rows(w_down4)}

    sp = {n: w[n][0] for n in ("a_re", "a_im", "log_step", "b_re", "b_im", "c_re", "c_im", "d_skip",
                               "norm_mix_g", "norm_attn_g", "norm_ssm_g", "norm_ffn_g", "sink", "conv_b")}
    for n in ("norm_mix_g", "norm_attn_g", "norm_ssm_g", "norm_ffn_g", "sink", "conv_b"):
        sp[n] = sp[n].reshape(1, -1)
    sp["norm_mix_g"] = sp["norm_mix_g"] + token[:1, :1]
    sp["norm_final_g"] = w["norm_final_g"]
    flight = {}

    def ffn_grads_ready(dw_up, dw_down):
        *flight["pair"], token = _pair_start([dw_up, _by_owner("w_down", dw_down)])
        return token[:1, :1]

    def ffn_grads_next(after):
        mine, got = _pair_wait(*flight["pair"], after)
        sums, flight["own"] = zip(*[_pair_sum(a, b, place, BF16, "pair_sum_" + n) for n, a, b in zip(late, mine, got)])
        *flight["chip"], token = _chip_start(list(sums))
        return token[:1, :1]

    loss, grad_x, g = _local_step(x[0], loss_target[0], wb, sp, mixer_weights, late_weights, ffn_grads_ready,
                                  ffn_grads_next)

    def as_rows(t):
        rows = _tile_rows(t.size)
        return jnp.pad(t.reshape(-1), (0, rows * D_MODEL - t.size)).reshape(rows, D_MODEL)

    pieces = [as_rows(g[n]) for n in SMALL] + [as_rows(loss)]
    spare = N_DEV * SMALL_ROWS - sum(p.shape[0] for p in pieces)
    small = jnp.concatenate(pieces + [jnp.zeros((spare, D_MODEL), F32)]).reshape(4, 2 * SMALL_ROWS, D_MODEL)
    by_owner = [_by_owner(n, g[n]) for n in early] + [small]
    got = _pair_exchange(by_owner)
    transit = [BF16] * len(early) + [F32]
    chip_sums, own_sums = zip(*[_pair_sum(a, b, place, t, "pair_sum_" + n)
                                for n, a, b, t in zip(early + ("small",), by_owner, got, transit)])
    landed = _chip_exchange(list(chip_sums))
    halves = {n: _chip_sum(o, t, "chip_sum_" + n) for n, o, t in zip(early + ("small",), own_sums, landed)}
    late_landed = _chip_wait(*flight["chip"], grad_x)
    for n, o, t in zip(late, flight["own"], late_landed):
        halves[n] = _chip_sum(o, t, "chip_sum_" + n)
    *others, small_all = _final_exchange([halves[n] for n in BIG], halves["small"])
    small_all = small_all.reshape(N_DEV * SMALL_ROWS, D_MODEL)
    grads, row = {}, 0
    for n in SMALL:
        shape = (3, 4 * w[n].shape[-1]) if n == "conv_w" else w[n].shape[1:] if n != "norm_final_g" else w[n].shape
        size = math.prod(shape)
        grads[n] = small_all[row:row + _tile_rows(size)].reshape(-1)[:size].reshape(shape)
        row += _tile_rows(size)
    loss = small_all[row, 0]
    cw = w["conv_w"].shape[-1]
    grads["conv_w"] = lax.dynamic_slice_in_dim(grads["conv_w"], xy * cw, cw, axis=1)
    grads = {n: _view(n, grads[n].reshape(w[n].shape)) for n in SMALL}
    wv, mv, vv = ({n: _view(n, t[n]) for n in WEIGHTS} for t in (w, m, v))

    delta, new_m, new_v = {}, {}, {}
    for n, other in zip(BIG, others):
        two_d = lambda t: t.reshape(t.shape[-2:])
        grads[n], delta[n], new_m[n], new_v[n] = _adamw_halves(
            two_d(wv[n]), halves[n], other, two_d(mv[n]), two_d(vv[n]), place, "adamw_" + n)
    for group, name in ((("b_re", "b_im"), "adamw_b"), (tuple(n for n in SMALL if n not in ("b_re", "b_im")), "adamw_small")):
        row = lambda t: t.reshape(1, -1) if t.ndim == 1 else t
        d_, m_, v_ = _adamw_many(*[[row(t[n]) for n in group] for t in (wv, grads, mv, vv)], name)
        for n, dn, mn, vn in zip(group, d_, m_, v_):
            delta[n], new_m[n], new_v[n] = (t.reshape(wv[n].shape) for t in (dn, mn, vn))
    natural = lambda t: [_unview(n, t[n].reshape(wv[n].shape)) for n in WEIGHTS]
    return (loss, grad_x[None], *natural(grads), *natural(delta), *natural(new_m), *natural(new_v))
```

```python
import functools
import math

import jax
import jax.numpy as jnp
import numpy as np
from jax import lax
from jax.experimental import pallas as pl
from jax.experimental.pallas import tpu as pltpu

F32 = jnp.float32
BF16 = jnp.bfloat16

D_MODEL = 1024
N_Q_HEADS = 8
N_KV_HEADS = 2
HEAD_DIM = 64
ATTN_WIDTH = 512
KV_WIDTH = 128
QKV_WIDTH = ATTN_WIDTH + 2 * KV_WIDTH
WINDOW = 128
BLOCK = 128
ROPE_DIM = 16
ROPE_THETA = 500000.0
SSM_WIDTH = 512
SSM_GROUP = 16
N_SSM_GROUPS = 32
SSM_STATE = 64
IN_WIDTH = 1280
D_FF = 2816
EPS = 1e-6
ADAM_LR = 0.001
ADAM_B1 = 0.9
ADAM_B2 = 0.999
ADAM_EPS = 1e-08
ADAM_WD = 0.01
ADAM_STEP = 10

VMEM_BYTES_V7X = 64 * 1024 * 1024
SUBLANES = 8
LANES = 128
SSM_CB = 4
SSM_CH = 128
SSM_ST = 512
N_SEG = SUBLANES

NN = (((1,), (0,)), ((), ()))
NT = (((1,), (1,)), ((), ()))
TN = (((0,), (0,)), ((), ()))


def _params(sem=None, vmem_mb=48):
    limit = vmem_mb * 1024 * 1024
    assert limit < VMEM_BYTES_V7X
    return pltpu.CompilerParams(dimension_semantics=sem, vmem_limit_bytes=limit)


def _dg(a, b, dims):
    return lax.dot_general(a, b, dims, preferred_element_type=F32)


def _sigmoid(x):
    return 1.0 / (1.0 + jnp.exp(-x))


_SQRT_HALF = 0.7071067811865476
_INV_SQRT_2PI = 0.3989422804014327


def _gelu(x):
    return 0.5 * x * (1.0 + lax.erf(x * _SQRT_HALF))


def _gelu_grad(x):
    return 0.5 * (1.0 + lax.erf(x * _SQRT_HALF)) + x * (_INV_SQRT_2PI * jnp.exp(-0.5 * x * x))


def _mm_tn(a, b, tm, tn, name):
    k, m = a.shape
    n = b.shape[1]

    def body(a_ref, b_ref, o_ref):
        o_ref[...] = _dg(a_ref[...], b_ref[...], TN)

    return pl.pallas_call(
        body, grid=(m // tm, n // tn),
        in_specs=[pl.BlockSpec((k, tm), lambda i, j: (0, i)), pl.BlockSpec((k, tn), lambda i, j: (0, j))],
        out_specs=pl.BlockSpec((tm, tn), lambda i, j: (i, j)),
        out_shape=jax.ShapeDtypeStruct((m, n), F32), name=name,
        compiler_params=_params(("parallel", "parallel")),
    )(a, b)


def _mm_nn_cols(a, b4, tm, name):
    m, k = a.shape
    s, _, n = b4.shape

    def body(a_ref, b_ref, o_ref):
        o_ref[...] = _dg(a_ref[...], b_ref[...], NN)

    return pl.pallas_call(
        body, grid=(m // tm, s),
        in_specs=[pl.BlockSpec((tm, k), lambda i, j: (i, 0)), pl.BlockSpec((None, k, n), lambda i, j: (j, 0, 0))],
        out_specs=pl.BlockSpec((tm, n), lambda i, j: (i, j)),
        out_shape=jax.ShapeDtypeStruct((m, s * n), F32), name=name,
        compiler_params=_params(("parallel", "parallel")),
    )(a, b4)


def _mm_tn_cols(a, b2, s, tm, name):
    k, m = a.shape
    h, _, wide = b2.shape
    per = s // h
    n = wide // per

    def body(a_ref, b_ref, o_ref):
        o_ref[...] = _dg(a_ref[...], b_ref[...], TN)

    return pl.pallas_call(
        body, grid=(s, m // tm),
        in_specs=[pl.BlockSpec((k, tm), lambda j, i: (0, i)),
                  pl.BlockSpec((None, k, n), lambda j, i: (j // per, 0, j % per))],
        out_specs=pl.BlockSpec((None, tm, n), lambda j, i: (j, i, 0)),
        out_shape=jax.ShapeDtypeStruct((s, m, n), F32), name=name,
        compiler_params=_params(("parallel", "parallel")),
    )(a, b2)


TM_EW = 256


def _rms_bwd_vals(xv, gv, dy):
    r = lax.rsqrt(jnp.mean(xv * xv, axis=-1, keepdims=True) + EPS)
    xh = xv * r
    dxh = dy * gv
    dx = r * (dxh - xh * jnp.mean(dxh * xh, axis=-1, keepdims=True))
    return dx, dy * xh


TM_FUSED = 512
TM_LOSS = 256


def _rms_vals(xv, gv):
    return xv * lax.rsqrt(jnp.mean(xv * xv, axis=-1, keepdims=True) + EPS) * gv


def _rope_blocks(src, dst, c, lo, hi):
    nq = ATTN_WIDTH // LANES
    for blk in range(nq + 1):
        t = src[:, blk * LANES:(blk + 1) * LANES]
        dst[:, blk * LANES:(blk + 1) * LANES] = (
            t * c + pltpu.roll(t, LANES - 8, 1) * lo + pltpu.roll(t, 8, 1) * hi).astype(BF16)
    dst[:, (nq + 1) * LANES:] = src[:, (nq + 1) * LANES:].astype(BF16)


def _rms_mm_rope(x, g, wt, tabs, name):
    l, d = x.shape
    n = wt.shape[0]

    def body(x_ref, g_ref, w_ref, c_ref, lo_ref, hi_ref, h_ref, qkv_ref, u_ref):
        h = _rms_vals(x_ref[...], g_ref[...]).astype(BF16)
        h_ref[...] = h
        out = _dg(h, w_ref[...], NT)
        _rope_blocks(out[:, :QKV_WIDTH], qkv_ref, c_ref[...], lo_ref[...], hi_ref[...])
        u_ref[...] = out[:, QKV_WIDTH:]

    row = lambda width: pl.BlockSpec((TM_FUSED, width), lambda i: (i, 0))
    return pl.pallas_call(
        body, grid=(l // TM_FUSED,),
        in_specs=[row(d), pl.BlockSpec((1, d), lambda i: (0, 0)), pl.BlockSpec((n, d), lambda i: (0, 0)),
                  row(LANES), row(LANES), row(LANES)],
        out_specs=[row(d), row(QKV_WIDTH), row(n - QKV_WIDTH)],
        out_shape=[jax.ShapeDtypeStruct((l, d), BF16), jax.ShapeDtypeStruct((l, QKV_WIDTH), BF16),
                   jax.ShapeDtypeStruct((l, n - QKV_WIDTH), F32)],
        name=name, compiler_params=_params(("parallel",)),
    )(x, g, wt, *tabs)


def _mix_mm_res_rms(attn, ys, g_attn, g_ssm, b, res, g, name):
    l, w = attn.shape
    d = b.shape[1]

    def body(a_ref, y_ref, ga_ref, gs_ref, b_ref, r_ref, g_ref, m_ref, x_ref, h_ref):
        m_ref[:, :w] = _rms_vals(a_ref[...], ga_ref[...]).astype(BF16)
        m_ref[:, w:] = _rms_vals(y_ref[...], gs_ref[...]).astype(BF16)
        xv = r_ref[...] + _dg(m_ref[...], b_ref[...], NN)
        x_ref[...] = xv
        h_ref[...] = _rms_vals(xv, g_ref[...]).astype(BF16)

    row = lambda width: pl.BlockSpec((TM_FUSED, width), lambda i: (i, 0))
    vec = lambda width: pl.BlockSpec((1, width), lambda i: (0, 0))
    return pl.pallas_call(
        body, grid=(l // TM_FUSED,),
        in_specs=[row(w), row(w), vec(w), vec(w), pl.BlockSpec((2 * w, d), lambda i: (0, 0)), row(d), vec(d)],
        out_specs=[row(2 * w), row(d), row(d)],
        out_shape=[jax.ShapeDtypeStruct((l, 2 * w), BF16), jax.ShapeDtypeStruct((l, d), F32),
                   jax.ShapeDtypeStruct((l, d), BF16)],
        name=name, compiler_params=_params(("parallel",)),
    )(attn, ys, g_attn, g_ssm, b, res, g)


def _mm_res_loss(a, b, res, g, target):
    l, k = a.shape
    d = b.shape[1]

    def body(a_ref, b_ref, r_ref, g_ref, t_ref, loss_ref, dx_ref, dxb_ref, dg_ref):
        xv = r_ref[...] + _dg(a_ref[...], b_ref[...], NN)
        gv = g_ref[...]
        r = lax.rsqrt(jnp.mean(xv * xv, axis=-1, keepdims=True) + EPS)
        xh = xv * r
        e = xh * gv - t_ref[...]
        part = jnp.sum(jnp.sum(e * e, axis=1, keepdims=True), axis=0, keepdims=True) * (0.5 / d)
        dy = e * (1.0 / d)
        dxh = dy * gv
        dx = r * (dxh - xh * jnp.mean(dxh * xh, axis=-1, keepdims=True))
        dx_ref[...] = dx
        dxb_ref[...] = dx.astype(BF16)

        @pl.when(pl.program_id(0) == 0)
        def _():
            dg_ref[...] = jnp.zeros_like(dg_ref)
            loss_ref[...] = jnp.zeros_like(loss_ref)

        dg_ref[...] += jnp.sum(dy * xh, axis=0, keepdims=True)
        loss_ref[...] += part

    row = lambda width: pl.BlockSpec((TM_LOSS, width), lambda i: (i, 0))
    vec = pl.BlockSpec((1, d), lambda i: (0, 0))
    return pl.pallas_call(
        body, grid=(l // TM_LOSS,),
        in_specs=[row(k), pl.BlockSpec((k, d), lambda i: (0, 0)), row(d), vec, row(d)],
        out_specs=[pl.BlockSpec((1, 1), lambda i: (0, 0)), row(d), row(d), vec],
        out_shape=[jax.ShapeDtypeStruct((1, 1), F32), jax.ShapeDtypeStruct((l, d), F32),
                   jax.ShapeDtypeStruct((l, d), BF16), jax.ShapeDtypeStruct((1, d), F32)],
        name="mm_down_loss", compiler_params=_params(("arbitrary",)),
    )(a, b, res, g, target)


def _mm_rms_bwd(a, b, a_spec, b_spec, matmul, x, g, res, name):
    l, d = x.shape

    def body(a_ref, b_ref, x_ref, g_ref, res_ref, dx_ref, dxb_ref, dg_ref):
        dx, dgr = _rms_bwd_vals(x_ref[...], g_ref[...], matmul(a_ref, b_ref))
        dx = dx + res_ref[...]
        dx_ref[...] = dx
        dxb_ref[...] = dx.astype(BF16)

        @pl.when(pl.program_id(0) == 0)
        def _():
            dg_ref[...] = jnp.zeros_like(dg_ref)

        dg_ref[...] += jnp.sum(dgr, axis=0, keepdims=True)

    row = pl.BlockSpec((TM_FUSED, d), lambda i: (i, 0))
    vec = pl.BlockSpec((1, d), lambda i: (0, 0))
    return pl.pallas_call(
        body, grid=(l // TM_FUSED,), in_specs=[a_spec, b_spec, row, vec, row], out_specs=[row, row, vec],
        out_shape=[jax.ShapeDtypeStruct((l, d), F32), jax.ShapeDtypeStruct((l, d), BF16),
                   jax.ShapeDtypeStruct((1, d), F32)],
        name=name, compiler_params=_params(("arbitrary",)),
    )(a, b, x, g, res)


def _mm_nn_rms_bwd(a, b, x, g, res, name):
    return _mm_rms_bwd(a, b, pl.BlockSpec((TM_FUSED, a.shape[1]), lambda i: (i, 0)),
                       pl.BlockSpec(b.shape, lambda i: (0, 0)),
                       lambda a_ref, b_ref: _dg(a_ref[...], b_ref[...], NN), x, g, res, name)


def _mm_cols_rms_bwd(a2, b4, x, g, res, name):
    h, _, wide = a2.shape
    s, _, n = b4.shape
    per = s // h

    def matmul(a_ref, b_ref):
        acc = None
        for j in range(s):
            part = _dg(a_ref[j // per, :, (j % per) * n:(j % per + 1) * n], b_ref[j], NT)
            acc = part if acc is None else acc + part
        return acc

    return _mm_rms_bwd(a2, b4, pl.BlockSpec((h, TM_FUSED, wide), lambda i: (0, i, 0)),
                       pl.BlockSpec(b4.shape, lambda i: (0, 0, 0), pipeline_mode=pl.Buffered(1)),
                       matmul, x, g, res, name)


def _mm_mix_bwd(dx, b, attn, ys, g_attn, g_ssm, name):
    l, w = attn.shape
    d = dx.shape[1]

    def body(dx_ref, b_ref, a_ref, y_ref, ga_ref, gs_ref, da_ref, dy_ref, dga_ref, dgs_ref):
        @pl.when(pl.program_id(0) == 0)
        def _():
            dga_ref[...] = jnp.zeros_like(dga_ref)
            dgs_ref[...] = jnp.zeros_like(dgs_ref)

        dm = _dg(dx_ref[...], b_ref[...], NT)
        for src, gr, off, dst, dgr in ((a_ref, ga_ref, 0, da_ref, dga_ref), (y_ref, gs_ref, w, dy_ref, dgs_ref)):
            dxv, dg_rows = _rms_bwd_vals(src[...], gr[...], dm[:, off:off + w])
            dst[...] = dxv
            dgr[...] += jnp.sum(dg_rows, axis=0, keepdims=True)

    row = lambda width: pl.BlockSpec((TM_FUSED, width), lambda i: (i, 0))
    vec = pl.BlockSpec((1, w), lambda i: (0, 0))
    return pl.pallas_call(
        body, grid=(l // TM_FUSED,),
        in_specs=[row(d), pl.BlockSpec((2 * w, d), lambda i: (0, 0)), row(w), row(w), vec, vec],
        out_specs=[row(w), row(w), vec, vec],
        out_shape=[jax.ShapeDtypeStruct((l, w), F32), jax.ShapeDtypeStruct((l, w), F32),
                   jax.ShapeDtypeStruct((1, w), F32), jax.ShapeDtypeStruct((1, w), F32)],
        name=name, compiler_params=_params(("arbitrary",)),
    )(dx, b, attn, ys, g_attn, g_ssm)


def _rope_tables(l):
    half = ROPE_DIM // 2
    f32 = np.float32
    inv_freq = np.power(f32(ROPE_THETA), -np.arange(half, dtype=f32) / f32(half))
    ang = np.arange(l, dtype=f32)[:, None] * inv_freq[None, :]
    cos, sin = np.cos(ang), np.sin(ang)
    ones = np.ones((l, HEAD_DIM - ROPE_DIM), f32)
    zeros = np.zeros((l, HEAD_DIM - ROPE_DIM), f32)
    zh = np.zeros((l, half), f32)
    c = np.concatenate([cos, cos, ones], axis=1)
    s_lo = np.concatenate([-sin, zh, zeros], axis=1)
    s_hi = np.concatenate([zh, sin, zeros], axis=1)
    return tuple(jnp.asarray(np.tile(t, (1, LANES // HEAD_DIM)), F32) for t in (c, s_lo, s_hi))


def _rope_bwd(dq, dkv, du_ssm, dpre, d_skip, tabs):
    l = dq.shape[0]
    nq = ATTN_WIDTH // LANES

    def body(dq_ref, dkv_ref, du_ref, dpre_ref, ds_ref, c_ref, lo_ref, hi_ref, o_ref):
        c, lo, hi = c_ref[...], lo_ref[...], hi_ref[...]
        for blk in range(nq + 1):
            t = dq_ref[:, blk * LANES:(blk + 1) * LANES] if blk < nq else dkv_ref[:, :KV_WIDTH]
            g = t * c + pltpu.roll(t * lo, 8, 1) + pltpu.roll(t * hi, LANES - 8, 1)
            o_ref[:, blk * LANES:(blk + 1) * LANES] = g.astype(BF16)
        o_ref[:, (nq + 1) * LANES:QKV_WIDTH] = dkv_ref[:, KV_WIDTH:].astype(BF16)
        o_ref[:, QKV_WIDTH:] = (du_ref[...] + dpre_ref[...] * ds_ref[...]).astype(BF16)

    tab = pl.BlockSpec((TM_EW, LANES), lambda i: (i, 0))
    wide = pl.BlockSpec((TM_EW, SSM_WIDTH), lambda i: (i, 0))
    return pl.pallas_call(
        body, grid=(l // TM_EW,),
        in_specs=[wide, pl.BlockSpec((TM_EW, 2 * KV_WIDTH), lambda i: (i, 0)), wide, wide,
                  pl.BlockSpec((1, SSM_WIDTH), lambda i: (0, 0)), tab, tab, tab],
        out_specs=pl.BlockSpec((TM_EW, IN_WIDTH), lambda i: (i, 0)),
        out_shape=jax.ShapeDtypeStruct((l, IN_WIDTH), BF16), name="rope_bwd",
        compiler_params=_params(("parallel",)),
    )(dq, dkv, du_ssm, dpre, d_skip, *tabs)


_Q_COLS = ATTN_WIDTH // LANES
_SCALE = HEAD_DIM ** -0.5
_NEG = -1e30


def _window_specs(nb, width, col):
    return [
        pl.BlockSpec((BLOCK, width), lambda n: (jnp.maximum(n - 1, 0), col)),
        pl.BlockSpec((BLOCK, width), lambda n: (n, col)),
        pl.BlockSpec((BLOCK, width), lambda n: (jnp.minimum(n + 1, nb - 1), col)),
    ]


def _stacked_sink(sink_ref, heads):
    rid = lax.broadcasted_iota(jnp.int32, (len(heads) * BLOCK, 1), 0)
    sk = jnp.full(rid.shape, sink_ref[0, heads[-1]], F32)
    for g in range(len(heads) - 2, -1, -1):
        sk = jnp.where(rid < (g + 1) * BLOCK, sink_ref[0, heads[g]], sk)
    return sk


def _attn_fwd(qkv, sink):
    l = qkv.shape[0]
    nb = l // BLOCK
    grp = N_Q_HEADS // N_KV_HEADS

    def body(sink_ref, q_ref, k0, k1, k2, v0, v1, v2, o_ref, lse_ref):
        n = pl.program_id(0)
        q = q_ref[...]
        kw = jnp.concatenate([k0[...], k1[...], k2[...]], axis=0)
        vw = jnp.concatenate([v0[...], v1[...], v2[...]], axis=0)
        row = lax.broadcasted_iota(jnp.int32, (grp * BLOCK, 3 * BLOCK), 0)
        col = lax.broadcasted_iota(jnp.int32, (grp * BLOCK, 3 * BLOCK), 1)
        valid = jnp.abs(col - BLOCK - (row & (BLOCK - 1))) <= WINDOW
        valid &= jnp.logical_not((n == 0) & (col < BLOCK))
        valid &= jnp.logical_not((n == nb - 1) & (col >= 2 * BLOCK))
        for hk in range(N_KV_HEADS):
            heads = range(hk * grp, (hk + 1) * grp)
            qs = jnp.concatenate([q[:, h * HEAD_DIM:(h + 1) * HEAD_DIM] for h in heads], axis=0)
            kh = kw[:, hk * HEAD_DIM:(hk + 1) * HEAD_DIM]
            vh = vw[:, hk * HEAD_DIM:(hk + 1) * HEAD_DIM]
            s = jnp.where(valid, _dg(qs, kh, NT) * _SCALE, _NEG)
            sk = _stacked_sink(sink_ref, heads)
            m = jnp.maximum(jnp.max(s, axis=1, keepdims=True), sk)
            p = jnp.exp(s - m)
            denom = jnp.sum(p, axis=1, keepdims=True) + jnp.exp(sk - m)
            o = _dg((p / denom).astype(BF16), vh, NN)
            lse = m + jnp.log(denom)
            for g, h in enumerate(heads):
                o_ref[:, h * HEAD_DIM:(h + 1) * HEAD_DIM] = o[g * BLOCK:(g + 1) * BLOCK]
                lse_ref[:, h:h + 1] = lse[g * BLOCK:(g + 1) * BLOCK]

    return pl.pallas_call(
        body, grid=(nb,),
        in_specs=[pl.BlockSpec(memory_space=pltpu.SMEM),
                  pl.BlockSpec((BLOCK, ATTN_WIDTH), lambda n: (n, 0))]
        + _window_specs(nb, KV_WIDTH, _Q_COLS) + _window_specs(nb, KV_WIDTH, _Q_COLS + 1),
        out_specs=[pl.BlockSpec((BLOCK, ATTN_WIDTH), lambda n: (n, 0)),
                   pl.BlockSpec((BLOCK, N_Q_HEADS), lambda n: (n, 0))],
        out_shape=[jax.ShapeDtypeStruct((l, ATTN_WIDTH), F32), jax.ShapeDtypeStruct((l, N_Q_HEADS), F32)],
        name="attn_fwd", compiler_params=_params(("parallel",)),
    )(sink, qkv, qkv, qkv, qkv, qkv, qkv, qkv)


def _attn_bwd(qkv, attn, dattn, lse, sink):
    l = qkv.shape[0]
    nb = l // BLOCK
    grp = N_Q_HEADS // N_KV_HEADS
    win = 3 * BLOCK

    def body(sink_ref, q_ref, k0, k1, k2, v0, v1, v2, o_ref, d_ref, l_ref, dq_ref, dkv_ref, dsink_ref, ring_ref):
        n = pl.program_id(0)

        @pl.when(n == 0)
        def _():
            dsink_ref[...] = jnp.zeros_like(dsink_ref)
            ring_ref[...] = jnp.zeros_like(ring_ref)

        @pl.when(n < nb)
        def _():
            first, last = n == 0, n == nb - 1
            cat = lambda a, b, c: jnp.concatenate([a[...], b[...], c[...]], axis=0)
            q, kw, vw = q_ref[...], cat(k0, k1, k2), cat(v0, v1, v2)
            dov = d_ref[...]
            prod = o_ref[...] * dov
            dob = dov.astype(BF16)
            lse = l_ref[...]
            row = lax.broadcasted_iota(jnp.int32, (grp * BLOCK, win), 0)
            col = lax.broadcasted_iota(jnp.int32, (grp * BLOCK, win), 1)
            valid = jnp.abs(col - BLOCK - (row & (BLOCK - 1))) <= WINDOW
            valid &= jnp.logical_not(first & (col < BLOCK))
            valid &= jnp.logical_not(last & (col >= 2 * BLOCK))

            dsink_parts, dks, dvs = [], [], []
            for hk in range(N_KV_HEADS):
                heads = range(hk * grp, (hk + 1) * grp)
                ksl = slice(hk * HEAD_DIM, (hk + 1) * HEAD_DIM)
                hsl = [slice(h * HEAD_DIM, (h + 1) * HEAD_DIM) for h in heads]
                stack = lambda parts: jnp.concatenate(parts, axis=0)
                qs = stack([q[:, s_] for s_ in hsl])
                dos = stack([dob[:, s_] for s_ in hsl])
                deltas = stack([jnp.sum(prod[:, s_], axis=1, keepdims=True) for s_ in hsl])
                lses = stack([lse[:, h:h + 1] for h in heads])
                kh, vh = kw[:, ksl], vw[:, ksl]
                s = jnp.where(valid, _dg(qs, kh, NT) * _SCALE, _NEG)
                p = jnp.exp(s - lses)
                dp = _dg(dos, vh, NT)
                ds = (p * (dp - deltas) * _SCALE).astype(BF16)
                dq = _dg(ds, kh, NN)
                sink_rows = jnp.exp(_stacked_sink(sink_ref, heads) - lses) * deltas
                for g in range(grp):
                    dq_ref[:, hsl[g]] = dq[g * BLOCK:(g + 1) * BLOCK]
                    dsink_parts.append(jnp.sum(sink_rows[g * BLOCK:(g + 1) * BLOCK], axis=0, keepdims=True))
                dks.append(_dg(ds, qs, TN))
                dvs.append(_dg(p.astype(BF16), dos, TN))
            dsink_ref[...] -= jnp.concatenate(dsink_parts, axis=1)
            part = jnp.concatenate(dks + dvs, axis=1)
            ring_ref[(n + 2) % 3] += part[0:BLOCK]
            ring_ref[n % 3] += part[BLOCK:2 * BLOCK]
            ring_ref[(n + 1) % 3] = part[2 * BLOCK:]

        @pl.when(n >= 1)
        def _():
            dkv_ref[...] = ring_ref[(n + 2) % 3]

    centre = lambda n: jnp.minimum(n, nb - 1)
    window = lambda width, col: [
        pl.BlockSpec((BLOCK, width), lambda n: (jnp.maximum(centre(n) - 1, 0), col)),
        pl.BlockSpec((BLOCK, width), lambda n: (centre(n), col)),
        pl.BlockSpec((BLOCK, width), lambda n: (jnp.minimum(centre(n) + 1, nb - 1), col))]
    own = lambda width: pl.BlockSpec((BLOCK, width), lambda n: (centre(n), 0))
    return pl.pallas_call(
        body, grid=(nb + 1,),
        in_specs=[pl.BlockSpec(memory_space=pltpu.SMEM), own(ATTN_WIDTH)]
        + window(KV_WIDTH, _Q_COLS) + window(KV_WIDTH, _Q_COLS + 1)
        + [own(ATTN_WIDTH), own(ATTN_WIDTH), own(N_Q_HEADS)],
        out_specs=[own(ATTN_WIDTH), pl.BlockSpec((BLOCK, 2 * KV_WIDTH), lambda n: (jnp.maximum(n - 1, 0), 0)),
                   pl.BlockSpec((1, N_Q_HEADS), lambda n: (0, 0))],
        out_shape=[jax.ShapeDtypeStruct((l, ATTN_WIDTH), F32), jax.ShapeDtypeStruct((l, 2 * KV_WIDTH), F32),
                   jax.ShapeDtypeStruct((1, N_Q_HEADS), F32)],
        scratch_shapes=[pltpu.VMEM((3, BLOCK, 2 * KV_WIDTH), F32)],
        name="attn_bwd", compiler_params=_params(("arbitrary",)),
    )(sink, qkv, qkv, qkv, qkv, qkv, qkv, qkv, attn, dattn, lse)


def _ssm_disc(a_re, a_im, log_step, b_re, b_im):
    step = jnp.exp(log_step)[..., None]
    mag = jnp.exp(a_re * step)
    lb_re, lb_im = mag * jnp.cos(a_im * step), mag * jnp.sin(a_im * step)
    nr, ni = lb_re - 1.0, lb_im
    den = a_re * a_re + a_im * a_im
    f_re = ((nr * a_re + ni * a_im) / den)[..., None]
    f_im = ((ni * a_re - nr * a_im) / den)[..., None]
    return lb_re, lb_im, f_re * b_re - f_im * b_im, f_re * b_im + f_im * b_re


def _ssm_pack(lb_re, lb_im, bb_re, bb_im, c_re, c_im):
    eye = jnp.eye(SSM_CH // SSM_GROUP, dtype=F32)
    ng = SSM_CH // SSM_GROUP

    def diag_b(bb):
        t = bb.reshape(2, SSM_CB, ng, SSM_STATE, SSM_GROUP)
        return jnp.einsum('dkgpc,gh->dkgchp', t, eye).reshape(2, SSM_CB, SSM_CH, SSM_ST)

    def diag_c(cc):
        t = cc.reshape(2, SSM_CB, ng, SSM_GROUP, SSM_STATE)
        return jnp.einsum('dkgcp,gh->dkhpgc', t, eye).reshape(2, SSM_CB, SSM_ST, SSM_CH)

    bcat = jnp.concatenate([diag_b(bb_re), diag_b(bb_im)], axis=-1)
    ccat = jnp.concatenate([diag_c(c_re), -diag_c(c_im)], axis=-2)
    lam_re = lb_re.reshape(2, SSM_CB, 1, SSM_ST)
    lam_im = lb_im.reshape(2, SSM_CB, 1, SSM_ST)
    return bcat, ccat, lam_re, lam_im


def _ssm_unpack(dbcat, dccat, dlam_re, dlam_im):
    ng = SSM_CH // SSM_GROUP
    eye = jnp.eye(ng, dtype=F32)

    def undiag_b(t):
        t = t.reshape(2, SSM_CB, ng, SSM_GROUP, ng, SSM_STATE)
        return jnp.einsum('dkgchp,gh->dkgpc', t, eye).reshape(2, N_SSM_GROUPS, SSM_STATE, SSM_GROUP)

    def undiag_c(t):
        t = t.reshape(2, SSM_CB, ng, SSM_STATE, ng, SSM_GROUP)
        return jnp.einsum('dkhpgc,gh->dkgcp', t, eye).reshape(2, N_SSM_GROUPS, SSM_GROUP, SSM_STATE)

    dbb_re, dbb_im = undiag_b(dbcat[..., :SSM_ST]), undiag_b(dbcat[..., SSM_ST:])
    dc_re, dc_im = undiag_c(dccat[:, :, :SSM_ST]), -undiag_c(dccat[:, :, SSM_ST:])
    shape = (2, N_SSM_GROUPS, SSM_STATE)
    return dlam_re.reshape(shape), dlam_im.reshape(shape), dbb_re, dbb_im, dc_re, dc_im


def _to_segments(t):
    l, w = t.shape
    return t.reshape(N_SEG, l // N_SEG, w).transpose(1, 0, 2).reshape(l, w)


def _from_segments(t):
    l, w = t.shape
    return t.reshape(l // N_SEG, N_SEG, w).transpose(1, 0, 2).reshape(l, w)


SSM_RC = 256
SSM_JC = SSM_RC // N_SEG
_RE, _IM = pl.ds(0, SSM_ST), pl.ds(SSM_ST, SSM_ST)


def _cfma(ar, ai, xr, xi, br, bi):
    return ar * xr - ai * xi + br, ar * xi + ai * xr + bi


def _chunk_rows(ci, rev, nc):
    start = jnp.where(rev, (nc - 1 - ci) * SSM_RC, ci * SSM_RC)
    return pl.ds(pl.multiple_of(start, SSM_RC), SSM_RC)


def _scan_chunk(src, dst, ar, ai, rev, nj, ci, carry, prev_ref=None):
    def rows_of(staged, j, k):
        at = jnp.where(rev, SSM_JC - 1 - k, k) if staged else j
        return pl.ds(pl.multiple_of(at * N_SEG, N_SEG), N_SEG)

    for k in range(SSM_JC):
        jj = ci * SSM_JC + k
        j = jnp.where(rev, nj - 1 - jj, jj)
        rows = rows_of(src[1], j, k)
        nr, ni = _cfma(ar, ai, carry[0], carry[1], src[0][rows, _RE], src[0][rows, _IM])
        if dst is not None:
            rows = rows_of(dst[1], j, k)
            dst[0][rows, _RE] = nr
            dst[0][rows, _IM] = ni
        if prev_ref is None:
            carry = (nr, ni)
            continue
        jp = jnp.where(rev, j - 1, j + 1)
        if k == SSM_JC - 1:
            inside = jnp.where((jp >= 0) & (jp < nj), 1.0, 0.0)
            jp = jnp.clip(jp, 0, nj - 1)
        prow = pl.ds(pl.multiple_of(jp * N_SEG, N_SEG), N_SEG)
        xr, xi = prev_ref[prow, _RE], prev_ref[prow, _IM]
        sr, si = nr * xr + ni * xi, ni * xr - nr * xi
        if k == SSM_JC - 1:
            sr, si = inside * sr, inside * si
        carry = (nr, ni, carry[2] + sr, carry[3] + si)
    return carry


def _segment_inits(ar, ai, end_r, end_i, rev, nj):
    pr, pi = ar, ai
    for _ in range(int(math.log2(nj))):
        pr, pi = pr * pr - pi * pi, 2.0 * pr * pi
    seg = lax.broadcasted_iota(jnp.int32, end_r.shape, 0)
    zero = jnp.zeros_like(end_r)

    def chain(shift, keep):
        ir, ii = zero, zero
        for _ in range(N_SEG - 1):
            tr, ti = _cfma(pr, pi, ir, ii, end_r, end_i)
            ir = jnp.where(keep, pltpu.roll(tr, shift, 0), 0.0)
            ii = jnp.where(keep, pltpu.roll(ti, shift, 0), 0.0)
        return ir, ii

    up_r, up_i = chain(1, seg >= 1)
    dn_r, dn_i = chain(N_SEG - 1, seg <= N_SEG - 2)
    return jnp.where(rev, dn_r, up_r), jnp.where(rev, dn_i, up_i)


def _ssm_specs(l):
    act = pl.BlockSpec((l, SSM_CH), lambda k, d: (0, k))
    bmat = pl.BlockSpec((None, None, SSM_CH, 2 * SSM_ST), lambda k, d: (d, k, 0, 0))
    cmat = pl.BlockSpec((None, None, 2 * SSM_ST, SSM_CH), lambda k, d: (d, k, 0, 0))
    lam = pl.BlockSpec((None, None, 1, SSM_ST), lambda k, d: (d, k, 0, 0))
    return act, bmat, cmat, lam


def _ssm_fwd(u_seg, bcat, ccat, lam_re, lam_im):
    l = u_seg.shape[0]
    nj = l // N_SEG
    nc = l // SSM_RC

    def body(u_ref, b_ref, c_ref, lr_ref, li_ref, y_ref, keep_ref, xs_ref, stage0, stage1, keep_sem):
        k, d = pl.program_id(0), pl.program_id(1)
        rev = d == 1
        shape = (N_SEG, SSM_ST)
        ar, ai = jnp.broadcast_to(lr_ref[...], shape), jnp.broadcast_to(li_ref[...], shape)
        zero = jnp.zeros(shape, F32)

        def inputs(ci, stage):
            rows = _chunk_rows(ci, rev, nc)
            bu = _dg(u_ref[rows, :], b_ref[...], NN)
            stage[...] = bu
            xs_ref[rows, :] = bu

        def first(stage, ci, carry):
            return _scan_chunk((stage, True), None, ar, ai, rev, nj, ci, carry)

        def first_pass(t, carry):
            inputs(2 * t + 1, stage1)
            carry = first(stage0, 2 * t, carry)
            inputs(2 * t + 2, stage0)
            return first(stage1, 2 * t + 1, carry)

        inputs(0, stage0)
        carry = lax.fori_loop(0, nc // 2 - 1, first_pass, (zero, zero))
        inputs(nc - 1, stage1)
        carry = first(stage0, nc - 2, carry)
        end_r, end_i = first(stage1, nc - 1, carry)
        init = _segment_inits(ar, ai, end_r, end_i, rev, nj)

        @pl.when(d == 0)
        def _():
            y_ref[...] = jnp.zeros_like(y_ref)

        def outputs(ci):
            rows = _chunk_rows(ci, rev, nc)
            y_ref[rows, :] += _dg(xs_ref[rows, :].astype(BF16), c_ref[...], NN)
            pltpu.make_async_copy(xs_ref.at[rows], keep_ref.at[d, k, rows], keep_sem).start()

        def second(ci, carry):
            return _scan_chunk((xs_ref, False), (xs_ref, False), ar, ai, rev, nj, ci, carry)

        def second_pass(ci, carry):
            outputs(ci - 1)
            return second(ci, carry)

        lax.fori_loop(1, nc, second_pass, second(0, init))
        outputs(nc - 1)
        pltpu.make_async_copy(xs_ref, keep_ref.at[d, k], keep_sem).wait()

    act, bmat, cmat, lam = _ssm_specs(l)
    return pl.pallas_call(
        body, grid=(SSM_CB, 2), in_specs=[act, bmat, cmat, lam, lam], out_specs=[act, ANY],
        out_shape=[jax.ShapeDtypeStruct((l, SSM_WIDTH), F32),
                   jax.ShapeDtypeStruct((2, SSM_CB, l, 2 * SSM_ST), F32)],
        scratch_shapes=[pltpu.VMEM((l, 2 * SSM_ST), F32), pltpu.VMEM((SSM_RC, 2 * SSM_ST), F32),
                        pltpu.VMEM((SSM_RC, 2 * SSM_ST), F32), pltpu.SemaphoreType.DMA],
        name="ssm_fwd", compiler_params=_params(("parallel", "arbitrary"), vmem_mb=56),
    )(u_seg, bcat.astype(BF16), ccat.astype(BF16), lam_re, lam_im)


def _ssm_bwd(u_seg, dy_seg, states, bcat, ccat, lam_re, lam_im):
    l = u_seg.shape[0]
    nj = l // N_SEG
    nc = l // SSM_RC

    def body(u_ref, dy_ref, keep_ref, b_ref, c_ref, lr_ref, li_ref,
             du_ref, db_ref, dc_ref, dlr_ref, dli_ref, xs_ref, gs_ref, stage0, stage1, keep_sem):
        k, d = pl.program_id(0), pl.program_id(1)
        rev = d == 1
        back = jnp.logical_not(rev)
        shape = (N_SEG, SSM_ST)
        ar, ai = jnp.broadcast_to(lr_ref[...], shape), -jnp.broadcast_to(li_ref[...], shape)
        zero = jnp.zeros(shape, F32)
        fetch = pltpu.make_async_copy(keep_ref.at[d, k], xs_ref, keep_sem)
        fetch.start()

        def inputs(ci, stage):
            rows = _chunk_rows(ci, back, nc)
            dx = _dg(dy_ref[rows, :], c_ref[...], NT)
            stage[...] = dx
            gs_ref[rows, :] = dx

        def first(stage, ci, carry):
            return _scan_chunk((stage, True), None, ar, ai, back, nj, ci, carry)

        def first_pass(t, carry):
            inputs(2 * t + 1, stage1)
            carry = first(stage0, 2 * t, carry)
            inputs(2 * t + 2, stage0)
            return first(stage1, 2 * t + 1, carry)

        inputs(0, stage0)
        carry = lax.fori_loop(0, nc // 2 - 1, first_pass, (zero, zero))
        inputs(nc - 1, stage1)
        carry = first(stage0, nc - 2, carry)
        end_r, end_i = first(stage1, nc - 1, carry)
        init = _segment_inits(ar, ai, end_r, end_i, back, nj)
        fetch.wait()
        db_ref[...] = jnp.zeros_like(db_ref)
        dc_ref[...] = jnp.zeros_like(dc_ref)

        @pl.when(d == 0)
        def _():
            du_ref[...] = jnp.zeros_like(du_ref)

        def outputs(ci, stage):
            rows = _chunk_rows(ci, back, nc)
            g = stage[...].astype(BF16)
            dc_ref[...] += _dg(xs_ref[rows, :].astype(BF16), dy_ref[rows, :], TN)
            db_ref[...] += _dg(u_ref[rows, :], g, TN)
            du_ref[rows, :] += _dg(g, b_ref[...], NT)

        def second(ci, stage, carry):
            return _scan_chunk((gs_ref, False), (stage, True), ar, ai, back, nj, ci, carry, prev_ref=xs_ref)

        def second_pass(t, carry):
            outputs(2 * t, stage0)
            carry = second(2 * t + 1, stage1, carry)
            outputs(2 * t + 1, stage1)
            return second(2 * t + 2, stage0, carry)

        carry = lax.fori_loop(0, nc // 2 - 1, second_pass, second(0, stage0, init + (zero, zero)))
        outputs(nc - 2, stage0)
        gr, gi, acc_r, acc_i = second(nc - 1, stage1, carry)
        outputs(nc - 1, stage1)

        seg = lax.broadcasted_iota(jnp.int32, shape, 0)
        jb = jnp.where(rev, nj - 1, 0)
        erow = pl.ds(pl.multiple_of((nj - 1 - jb) * N_SEG, N_SEG), N_SEG)

        def before(t):
            up = jnp.where(seg >= 1, pltpu.roll(t, 1, 0), 0.0)
            down = jnp.where(seg <= N_SEG - 2, pltpu.roll(t, N_SEG - 1, 0), 0.0)
            return jnp.where(rev, down, up)

        init_r, init_i = before(xs_ref[erow, _RE]), before(xs_ref[erow, _IM])
        acc_r = acc_r + gr * init_r + gi * init_i
        acc_i = acc_i + gi * init_r - gr * init_i
        dlr_ref[...] = jnp.sum(acc_r, axis=0, keepdims=True)
        dli_ref[...] = jnp.sum(acc_i, axis=0, keepdims=True)

    act, bmat, cmat, lam = _ssm_specs(l)
    return pl.pallas_call(
        body, grid=(SSM_CB, 2), in_specs=[act, act, ANY, bmat, cmat, lam, lam],
        out_specs=[act, bmat, cmat, lam, lam],
        out_shape=[jax.ShapeDtypeStruct((l, SSM_WIDTH), F32),
                   jax.ShapeDtypeStruct(bcat.shape, F32), jax.ShapeDtypeStruct(ccat.shape, F32),
                   jax.ShapeDtypeStruct(lam_re.shape, F32), jax.ShapeDtypeStruct(lam_im.shape, F32)],
        scratch_shapes=[pltpu.VMEM((l, 2 * SSM_ST), F32), pltpu.VMEM((l, 2 * SSM_ST), F32),
                        pltpu.VMEM((SSM_RC, 2 * SSM_ST), F32), pltpu.VMEM((SSM_RC, 2 * SSM_ST), F32),
                        pltpu.SemaphoreType.DMA],
        name="ssm_bwd", compiler_params=_params(("parallel", "arbitrary"), vmem_mb=58),
    )(u_seg, dy_seg, states, bcat.astype(BF16), ccat.astype(BF16), lam_re, lam_im)


def _glu_fwd(y_ssm, u, d_skip, w_glu):
    l, w = u.shape

    def body(y_ref, u_ref, d_ref, w_ref, pre_ref, s_ref, ys_ref):
        pre = y_ref[...] + d_ref[...] * u_ref[...]
        z = _gelu(pre)
        s = _dg(z.astype(BF16), w_ref[...], NN)
        pre_ref[...] = pre
        s_ref[...] = s
        ys_ref[...] = z * _sigmoid(s)

    row = pl.BlockSpec((TM_EW, w), lambda i: (i, 0))
    out = jax.ShapeDtypeStruct((l, w), F32)
    return pl.pallas_call(
        body, grid=(l // TM_EW,),
        in_specs=[row, row, pl.BlockSpec((1, w), lambda i: (0, 0)), pl.BlockSpec((w, w), lambda i: (0, 0))],
        out_specs=[row, row, row], out_shape=[out, out, out], name="glu_fwd",
        compiler_params=_params(("parallel",)),
    )(y_ssm, u, d_skip, w_glu)


def _glu_bwd(pre, s, dys, u, d_skip, w_glu):
    l, w = u.shape

    def body(pre_ref, s_ref, dys_ref, u_ref, d_ref, w_ref, dpre_ref, z_ref, ds_ref, dd_ref):
        pre, dys = pre_ref[...], dys_ref[...]
        z = _gelu(pre)
        sig = _sigmoid(s_ref[...])
        ds = (dys * z * sig * (1.0 - sig)).astype(BF16)
        dz = dys * sig + _dg(ds, w_ref[...], NT)
        dpre = dz * _gelu_grad(pre)
        dpre_ref[...] = dpre
        z_ref[...] = z.astype(BF16)
        ds_ref[...] = ds

        @pl.when(pl.program_id(0) == 0)
        def _():
            dd_ref[...] = jnp.zeros_like(dd_ref)

        dd_ref[...] += jnp.sum(dpre * u_ref[...], axis=0, keepdims=True)

    row = pl.BlockSpec((TM_EW, w), lambda i: (i, 0))
    vec = pl.BlockSpec((1, w), lambda i: (0, 0))
    return pl.pallas_call(
        body, grid=(l // TM_EW,),
        in_specs=[row, row, row, row, vec, pl.BlockSpec((w, w), lambda i: (0, 0))],
        out_specs=[row, row, row, vec],
        out_shape=[jax.ShapeDtypeStruct((l, w), F32), jax.ShapeDtypeStruct((l, w), BF16),
                   jax.ShapeDtypeStruct((l, w), BF16), jax.ShapeDtypeStruct((1, w), F32)],
        name="glu_bwd", compiler_params=_params(("arbitrary",)),
    )(pre, s, dys, u, d_skip, w_glu)


TM_CV = 512
TC_CV = 256
TM_CF = 256
TC_CF = D_FF // 2
HALO = SUBLANES


def _conv_specs(l, col0, tm=TM_CV, tc=TC_CV):
    per = tm // HALO
    nh = l // HALO
    off = col0 // tc
    return [
        pl.BlockSpec((HALO, tc), lambda j, i: (jnp.maximum(i * per - 1, 0), j + off)),
        pl.BlockSpec((tm, tc), lambda j, i: (i, j + off)),
        pl.BlockSpec((HALO, tc), lambda j, i: (jnp.minimum((i + 1) * per, nh - 1), j + off)),
    ]


def _ext(prev_ref, mid_ref, next_ref, first, last):
    p = jnp.where(first, 0.0, prev_ref[...])
    n = jnp.where(last, 0.0, next_ref[...])
    return jnp.concatenate([p, mid_ref[...], n], axis=0)


def _shift_dn(t):
    return pltpu.roll(t, 1, 0)


def _shift_up(t):
    return pltpu.roll(t, t.shape[0] - 1, 0)


def _conv3(e, w_ref, b_ref):
    return w_ref[0:1, :] * _shift_dn(e) + w_ref[1:2, :] * e + w_ref[2:3, :] * _shift_up(e) + b_ref[...]


def _convffn_fwd(up_pre, conv_w, conv_b):
    l = up_pre.shape[0]
    tm, tc = TM_CF, TC_CF
    ni = l // tm
    wspec = lambda off: pl.BlockSpec((3, tc), lambda j, i: (0, j + off))
    bspec = lambda off: pl.BlockSpec((1, tc), lambda j, i: (0, j + off))
    voff = D_FF // tc

    def body(gp, gm, gn, vp, vm, vn, wg, bg, wv, bv, o_ref):
        i = pl.program_id(1)
        first, last = i == 0, i == ni - 1
        gate = _conv3(_ext(gp, gm, gn, first, last), wg, bg)[HALO:HALO + tm]
        val = _conv3(_ext(vp, vm, vn, first, last), wv, bv)[HALO:HALO + tm]
        o_ref[...] = (gate * _sigmoid(gate) * val).astype(BF16)

    return pl.pallas_call(
        body, grid=(D_FF // tc, ni),
        in_specs=_conv_specs(l, 0, tm, tc) + _conv_specs(l, D_FF, tm, tc)
        + [wspec(0), bspec(0), wspec(voff), bspec(voff)],
        out_specs=pl.BlockSpec((tm, tc), lambda j, i: (i, j)),
        out_shape=jax.ShapeDtypeStruct((l, D_FF), BF16), name="convffn_fwd",
        compiler_params=_params(("parallel", "parallel")),
    )(up_pre, up_pre, up_pre, up_pre, up_pre, up_pre, conv_w, conv_b, conv_w, conv_b)


HALO_B = 2 * SUBLANES


def _convffn_bwd(up_pre, dx2b, w_down, conv_w, conv_b):
    l = up_pre.shape[0]
    ni = l // TM_CV
    d = dx2b.shape[1]
    wspec = lambda off: pl.BlockSpec((3, TC_CV), lambda i, j: (0, j + off))
    bspec = lambda off: pl.BlockSpec((1, TC_CV), lambda i, j: (0, j + off))
    voff = D_FF // TC_CV
    swap = lambda spec: pl.BlockSpec(spec.block_shape, lambda i, j, f=spec.index_map: f(j, i))
    per, nh = TM_CV // HALO_B, l // HALO_B
    dx_specs = [pl.BlockSpec((HALO_B, d), lambda i, j: (jnp.maximum(i * per - 1, 0), 0)),
                pl.BlockSpec((TM_CV, d), lambda i, j: (i, 0)),
                pl.BlockSpec((HALO_B, d), lambda i, j: (jnp.minimum((i + 1) * per, nh - 1), 0))]

    def body(gp, gm, gn, vp, vm, vn, xp, xm, xn, wd, wg, bg, wv, bv, dup_ref, pg_ref, pv_ref):
        i = pl.program_id(0)
        first, last = i == 0, i == ni - 1
        ge, ve = _ext(gp, gm, gn, first, last), _ext(vp, vm, vn, first, last)
        zero = jnp.zeros((HALO_B, d), BF16)
        dx = jnp.concatenate([jnp.where(first, zero, xp[...]), xm[...], jnp.where(last, zero, xn[...])], axis=0)
        de = _dg(dx, wd[...], NT)[HALO_B - HALO:HALO_B + TM_CV + HALO]
        taps = [(_shift_dn(e), e, _shift_up(e)) for e in (ge, ve)]
        conv = lambda t, w_ref, b_ref: w_ref[0:1, :] * t[0] + w_ref[1:2, :] * t[1] + w_ref[2:3, :] * t[2] + b_ref[...]
        gate, val = conv(taps[0], wg, bg), conv(taps[1], wv, bv)
        sig = _sigmoid(gate)
        silu = gate * sig
        dgate = de * val * (sig + silu * (1.0 - sig))
        dval = de * silu
        mid = slice(HALO, HALO + TM_CV)
        rid = lax.broadcasted_iota(jnp.int32, (SUBLANES, TC_CV), 0)
        for half, (dup, tap, w_ref, p_ref) in enumerate(((dgate, taps[0], wg, pg_ref), (dval, taps[1], wv, pv_ref))):
            dpre = w_ref[0:1, :] * _shift_up(dup) + w_ref[1:2, :] * dup + w_ref[2:3, :] * _shift_dn(dup)
            dup_ref[half] = dpre[mid].astype(BF16)
            dm_ = dup[mid]
            sums = [jnp.sum(dm_ * t[mid], axis=0, keepdims=True) for t in tap]
            sums.append(jnp.sum(dm_, axis=0, keepdims=True))
            acc = jnp.zeros((SUBLANES, TC_CV), F32)
            for k, sk in enumerate(sums):
                acc = jnp.where(rid == k, sk, acc)
            p_ref[...] = acc

    par = pl.BlockSpec((None, SUBLANES, TC_CV), lambda i, j: (i, 0, j))
    dup, pg, pv = pl.pallas_call(
        body, grid=(ni, D_FF // TC_CV),
        in_specs=[swap(s) for s in _conv_specs(l, 0) + _conv_specs(l, D_FF)] + dx_specs
        + [pl.BlockSpec((TC_CV, d), lambda i, j: (j, 0)), wspec(0), bspec(0), wspec(voff), bspec(voff)],
        out_specs=[pl.BlockSpec((2, TM_CV, TC_CV), lambda i, j: (0, i, j)), par, par],
        out_shape=[jax.ShapeDtypeStruct((2, l, D_FF), BF16),
                   jax.ShapeDtypeStruct((ni, SUBLANES, D_FF), F32), jax.ShapeDtypeStruct((ni, SUBLANES, D_FF), F32)],
        name="convffn_bwd", compiler_params=_params(("parallel", "parallel")),
    )(up_pre, up_pre, up_pre, up_pre, up_pre, up_pre, dx2b, dx2b, dx2b, w_down, conv_w, conv_b, conv_w, conv_b)
    return dup, jnp.concatenate([jnp.sum(pg, axis=0), jnp.sum(pv, axis=0)], axis=1)


def _local_step(x, target, wb, sp, mixer_weights=None, late_weights=None, ffn_grads_ready=None,
                ffn_grads_next=None):
    l = x.shape[0]
    tabs = _rope_tables(l)
    disc = _ssm_disc(sp["a_re"], sp["a_im"], sp["log_step"], sp["b_re"], sp["b_im"])
    bcat, ccat, lam_re, lam_im = _ssm_pack(*disc, sp["c_re"], sp["c_im"])
    d_skip = sp["d_skip"].reshape(1, SSM_WIDTH)

    h, qkv, u = _rms_mm_rope(x, sp["norm_mix_g"], wb["w_in"], tabs, "mm_in")
    attn, lse = _attn_fwd(qkv, sp["sink"])
    u_seg = _to_segments(u).astype(BF16)
    y_seg, states = _ssm_fwd(u_seg, bcat, ccat, lam_re, lam_im)
    y_ssm = _from_segments(y_seg)
    if mixer_weights is not None:
        wb = dict(wb, **mixer_weights(attn))
    pre, s_glu, ys = _glu_fwd(y_ssm, u, d_skip, wb["w_glu"])
    mixed, x1, h2 = _mix_mm_res_rms(attn, ys, sp["norm_attn_g"], sp["norm_ssm_g"], wb["w_out"], x,
                                    sp["norm_ffn_g"], "mm_out")
    if late_weights is not None:
        wb = dict(wb, **late_weights(h2))
    up_pre = _mm_nn_cols(h2, wb["w_up"], min(l, 1024), "mm_up")
    conv_w = wb["conv_w"]
    act = _convffn_fwd(up_pre, conv_w, sp["conv_b"])
    loss, dx2, dx2b, d_final_g = _mm_res_loss(act, wb["w_down"], x1, sp["norm_final_g"].reshape(1, D_MODEL), target)

    g = {"norm_final_g": d_final_g.reshape(D_MODEL)}
    g["w_down"] = _mm_tn(act, dx2b, D_FF // 2, 512, "mm_down_dw")
    dup_pre, conv_par = _convffn_bwd(up_pre, dx2b, wb["w_down"], conv_w, sp["conv_b"])
    g["conv_w"], g["conv_b"] = conv_par[0:3], conv_par[3:4]
    g["w_up"] = _mm_tn_cols(h2, dup_pre, wb["w_up"].shape[0], 512, "mm_up_dw")
    zero = ffn_grads_ready(g["w_up"], g["w_down"]) if ffn_grads_ready is not None else 0.0
    dx1, dx1b, g["norm_ffn_g"] = _mm_cols_rms_bwd(dup_pre, wb["w_up"], x1, sp["norm_ffn_g"] + zero, dx2, "mm_up_dx")
    g["w_out"] = _mm_tn(mixed, dx1b, 1024, 1024, "mm_out_dw")
    zero = ffn_grads_next(g["w_out"]) if ffn_grads_next is not None else 0.0
    dattn, dys, g["norm_attn_g"], g["norm_ssm_g"] = _mm_mix_bwd(
        dx1b, wb["w_out"], attn, ys, sp["norm_attn_g"] + zero, sp["norm_ssm_g"], "mm_out_dx")
    dpre, zb, dsb, dd = _glu_bwd(pre, s_glu, dys, u, d_skip, wb["w_glu"])
    g["d_skip"] = dd.reshape(N_SSM_GROUPS, SSM_GROUP)
    g["w_glu"] = _mm_tn(zb, dsb, 512, 512, "mm_glu_dw")
    du_seg, dbcat, dccat, dlam_re, dlam_im = _ssm_bwd(u_seg, _to_segments(dpre).astype(BF16), states, bcat, ccat,
                                                      lam_re, lam_im)
    dlb_re, dlb_im, dbb_re, dbb_im, g["c_re"], g["c_im"] = _ssm_unpack(dbcat, dccat, dlam_re, dlam_im)
    _, disc_vjp = jax.vjp(_ssm_disc, sp["a_re"], sp["a_im"], sp["log_step"], sp["b_re"], sp["b_im"])
    g["a_re"], g["a_im"], g["log_step"], g["b_re"], g["b_im"] = disc_vjp((dlb_re, dlb_im, dbb_re, dbb_im))
    dq, dkv, g["sink"] = _attn_bwd(qkv, attn, dattn, lse, sp["sink"])
    dproj = _rope_bwd(dq, dkv, _from_segments(du_seg), dpre, d_skip, tabs)
    g["w_in"] = _mm_tn(dproj, h, IN_WIDTH // 5, D_MODEL, "mm_in_dw")
    grad_x, _, g["norm_mix_g"] = _mm_nn_rms_bwd(dproj, wb["w_in"], x, sp["norm_mix_g"], dx1, "mm_in_dx")
    return loss, grad_x, g


MESH = pl.DeviceIdType.MESH
ANY = pl.BlockSpec(memory_space=pl.ANY)


def _place():
    x, y, c = lax.axis_index("x"), lax.axis_index("y"), lax.axis_index("c")
    chips = [(1 - x, y), (x, 1 - y), (1 - x, 1 - y)]
    return x, y, c, chips


def _chip_index(px, py):
    return 2 * px + py


CHUNK_BYTES = 256 * 1024
MAX_CHUNKS = 16


def _row_chunks(rows, row_bytes, align):
    n = max(1, min(MAX_CHUNKS, (rows * row_bytes) // CHUNK_BYTES))
    per = -(-rows // n)
    per = -(-per // align) * align
    return [(r0, min(per, rows - r0)) for r0 in range(0, rows, per)]


def _align_of(dtype):
    return SUBLANES * 4 // jnp.dtype(dtype).itemsize


def _remote(src, dst, send_sem, recv_sem, to):
    return pltpu.make_async_remote_copy(src_ref=src, dst_ref=dst, send_sem=send_sem, recv_sem=recv_sem,
                                        device_id=to, device_id_type=MESH)


CAST_ROWS = 64


def _gather_weights(shards, dtypes):
    nw = len(shards)

    def body(*refs):
        w_refs, o_refs = refs[:nw], refs[nw:2 * nw]
        send_sems, recv_sems, in_sems, out_sems = refs[2 * nw:2 * nw + 4]
        raw, cast = refs[2 * nw + 4:3 * nw + 4], refs[3 * nw + 4:]
        x, y, c, chips = _place()
        mine = _chip_index(x, y)
        sibling = (x, y, 1 - c)

        def rows_of(ref, chip, r0, nr):
            return ref.at[chip, pl.ds(r0, nr), :]

        def copy(wi, k, src, dst, to):
            return _remote(src, dst, send_sems.at[wi, k], recv_sems.at[wi, k], to)

        geo = []
        for wi in range(nw):
            rows, cols = w_refs[wi].shape
            row_bytes = cols * jnp.dtype(dtypes[wi]).itemsize
            geo.append((rows // 2, _row_chunks(rows // 2, row_bytes, _align_of(dtypes[wi]))))

        stage_in = [pltpu.make_async_copy(w_refs[wi], raw[wi], in_sems.at[wi]) for wi in range(nw)]
        for cp in stage_in:
            cp.start()
        staged = [raw[wi] if dtypes[wi] == w_refs[wi].dtype else cast[wi] for wi in range(nw)]
        stage_out = []
        for wi in range(nw):
            stage_in[wi].wait()
            if staged[wi] is not raw[wi]:
                def cast_rows(i, _, wi=wi):
                    rows = pl.ds(pl.multiple_of(i * CAST_ROWS, CAST_ROWS), CAST_ROWS)
                    cast[wi][rows, :] = raw[wi][rows, :].astype(dtypes[wi])
                    return 0

                lax.fori_loop(0, w_refs[wi].shape[0] // CAST_ROWS, cast_rows, 0)
            cp = pltpu.make_async_copy(staged[wi], o_refs[wi].at[mine], out_sems.at[wi])
            cp.start()
            stage_out.append(cp)

        for wi in range(nw):
            hr, half_chunks = geo[wi]
            for j, chip in enumerate(chips):
                for r0, nr in half_chunks:
                    copy(wi, j, staged[wi].at[pl.ds(c * hr + r0, nr), :],
                         rows_of(o_refs[wi], mine, c * hr + r0, nr), (*chip, c)).start()
        for wi in range(nw):
            hr, half_chunks = geo[wi]
            for j, chip in enumerate(chips):
                got = rows_of(o_refs[wi], _chip_index(*chip), c * hr, hr)
                copy(wi, j, got, got, (*chip, c)).wait_recv()
                for r0, nr in half_chunks:
                    piece = rows_of(o_refs[wi], _chip_index(*chip), c * hr + r0, nr)
                    copy(wi, 3 + j, piece, piece, sibling).start()
        for wi in range(nw):
            hr = geo[wi][0]
            for j, chip in enumerate(chips):
                got = rows_of(o_refs[wi], _chip_index(*chip), (1 - c) * hr, hr)
                copy(wi, 3 + j, got, got, sibling).wait_recv()
        for wi in range(nw):
            hr = geo[wi][0]
            sent = rows_of(o_refs[wi], mine, c * hr, hr)
            for k in range(6):
                copy(wi, k, sent, sent, sibling).wait_send()
            stage_out[wi].wait()

    return pl.pallas_call(
        body, in_specs=[ANY] * nw, out_specs=[ANY] * nw,
        out_shape=[jax.ShapeDtypeStruct((4, *s.shape), t) for s, t in zip(shards, dtypes)],
        scratch_shapes=[pltpu.SemaphoreType.DMA((nw, 6)), pltpu.SemaphoreType.DMA((nw, 6)),
                        pltpu.SemaphoreType.DMA((nw,)), pltpu.SemaphoreType.DMA((nw,))]
        + [pltpu.VMEM(s.shape, s.dtype) for s in shards] + [pltpu.VMEM(s.shape, t) for s, t in zip(shards, dtypes)],
        name="gather_weights", compiler_params=_params(vmem_mb=40),
    )(*shards)


HBM = pl.BlockSpec(memory_space=pltpu.HBM)
SEM = pl.BlockSpec(memory_space=pltpu.SEMAPHORE)
EFFECT = pltpu.SideEffectType.DATAFLOW_SIDE_EFFECTING


def _cast_place(w, place, dtype, after, name):
    rows, cols = w.shape
    tr = _row_tile(rows, cols, _align_of(dtype))

    def body(p_ref, w_ref, after_ref, o_ref):
        del p_ref, after_ref
        o_ref[...] = w_ref[...].astype(dtype)

    grid_spec = pltpu.PrefetchScalarGridSpec(
        num_scalar_prefetch=1, grid=(rows // tr,),
        in_specs=[pl.BlockSpec((tr, cols), lambda i, p: (i, 0)), ANY],
        out_specs=pl.BlockSpec((None, tr, cols), lambda i, p: (p[1], i, 0)))
    return pl.pallas_call(body, grid_spec=grid_spec, out_shape=jax.ShapeDtypeStruct((4, rows, cols), dtype),
                          name=name, compiler_params=_params(("parallel",)))(place, w, after)


def _split_start(name, arrays, n_pairs, issue):
    n = len(arrays)

    def body(*refs):
        issue(refs[:n], refs[n:n + n_pairs], refs[n + n_pairs:n + 2 * n_pairs])
        token = refs[2 * n + 2 * n_pairs]
        token[...] = jnp.zeros_like(token)

    dma = pltpu.SemaphoreType.DMA(())
    outs = pl.pallas_call(
        body, name=name,
        out_shape=[dma] * (2 * n_pairs) + [pltpu.HBM(t.shape, t.dtype) for t in arrays]
        + [jax.ShapeDtypeStruct((SUBLANES, LANES), F32)],
        in_specs=[HBM] * n, out_specs=[SEM] * (2 * n_pairs) + [HBM] * n + [pl.BlockSpec(memory_space=pltpu.VMEM)],
        input_output_aliases={a: 2 * n_pairs + a for a in range(n)},
        compiler_params=pltpu.CompilerParams(has_side_effects=EFFECT),
    )(*[pltpu.with_memory_space_constraint(t, pltpu.HBM) for t in arrays])
    return outs[:n_pairs], outs[n_pairs:2 * n_pairs], outs[2 * n_pairs:2 * n_pairs + n], outs[-1]


def _split_wait(name, send_sems, recv_sems, flying, sizes, after):
    n, n_pairs = len(flying), len(send_sems)

    def body(*refs):
        x, y, c, _ = _place()
        for k, ref in enumerate(sizes(refs[:n])):
            cp = _remote(ref, ref, refs[n + k], refs[n + n_pairs + k], (x, y, 1 - c))
            cp.wait_send()
            cp.wait_recv()

    return pl.pallas_call(
        body, name=name, out_shape=[pltpu.HBM(t.shape, t.dtype) for t in flying],
        in_specs=[HBM] * n + [SEM] * (2 * n_pairs) + [ANY], out_specs=[HBM] * n,
        input_output_aliases={a: a for a in range(n)},
        compiler_params=pltpu.CompilerParams(has_side_effects=EFFECT),
    )(*flying, *send_sems, *recv_sems, after)


def _spread_start(lands, name):
    def issue(land_refs, send_sems, recv_sems):
        x, y, c, chips = _place()
        mine = _chip_index(x, y)
        for a, land in enumerate(land_refs):
            _, rows, cols = land.shape
            hr = rows // 2
            row_bytes = cols * jnp.dtype(land.dtype).itemsize
            for r0, nr in _row_chunks(hr, row_bytes, _align_of(land.dtype)):
                piece = land.at[mine, pl.ds(c * hr + r0, nr), :]
                for chip in chips:
                    for core in (0, 1):
                        _remote(piece, piece, send_sems[a], recv_sems[a], (*chip, core)).start()

    return _split_start(name, lands, len(lands), issue)


def _spread_wait(send_sems, recv_sems, flying, after, name):
    return _split_wait(name, send_sems, recv_sems, flying, lambda refs: [r.at[pl.ds(0, 3)] for r in refs], after)


def _pair_start(grads):
    n = len(grads)
    zones = [lax.empty((4, g.shape[1] // 2, g.shape[2]), F32) for g in grads]

    def issue(refs, send_sems, recv_sems):
        x, y, c, _ = _place()
        for a in range(n):
            g_ref, z_ref = refs[a], refs[n + a]
            _, rows, cols = g_ref.shape
            hr = rows // 2
            for k in range(4):
                for r0, nr in _row_chunks(hr, cols * 4, SUBLANES):
                    _remote(g_ref.at[k, pl.ds((1 - c) * hr + r0, nr), :], z_ref.at[k, pl.ds(r0, nr), :],
                            send_sems[a], recv_sems[a], (x, y, 1 - c)).start()

    return _split_start("pair_start", list(grads) + zones, n, issue)


def _pair_wait(send_sems, recv_sems, flying, after):
    n = len(flying) // 2
    out = _split_wait("pair_wait", send_sems, recv_sems, flying, lambda refs: list(refs[n:]), after)
    return out[:n], out[n:]


def _chip_start(sums):
    n = len(sums)
    zones = [lax.empty((3, *s.shape[1:]), s.dtype) for s in sums]

    def issue(refs, send_sems, recv_sems):
        x, y, c, chips = _place()
        for a in range(n):
            s_ref, z_ref = refs[a], refs[n + a]
            _, rows, cols = s_ref.shape
            row_bytes = cols * jnp.dtype(s_ref.dtype).itemsize
            for r0, nr in _row_chunks(rows, row_bytes, _align_of(s_ref.dtype)):
                for j, chip in enumerate(chips):
                    _remote(s_ref.at[_chip_index(*chip), pl.ds(r0, nr), :], z_ref.at[j, pl.ds(r0, nr), :],
                            send_sems[a], recv_sems[a], (*chip, c)).start()

    return _split_start("chip_start", list(sums) + zones, n, issue)


def _chip_wait(send_sems, recv_sems, flying, after):
    n = len(flying) // 2
    return _split_wait("chip_wait", send_sems, recv_sems, flying, lambda refs: list(refs[n:]), after)[n:]


def _pair_exchange(grads):
    na = len(grads)

    def body(*refs):
        g_refs, o_refs = refs[:na], refs[na:2 * na]
        send_sems, recv_sems = refs[2 * na:]
        x, y, c, _ = _place()
        sibling = (x, y, 1 - c)
        for ai in range(na):
            _, rows, cols = g_refs[ai].shape
            hr = rows // 2
            for k in range(4):
                for r0, nr in _row_chunks(hr, cols * 4, SUBLANES):
                    _remote(g_refs[ai].at[k, pl.ds((1 - c) * hr + r0, nr), :], o_refs[ai].at[k, pl.ds(r0, nr), :],
                            send_sems.at[ai], recv_sems.at[ai], sibling).start()
        for ai in range(na):
            _remote(o_refs[ai], o_refs[ai], send_sems.at[ai], recv_sems.at[ai], sibling).wait()

    return pl.pallas_call(
        body, in_specs=[ANY] * na, out_specs=[ANY] * na,
        out_shape=[jax.ShapeDtypeStruct((4, g.shape[1] // 2, g.shape[2]), F32) for g in grads],
        scratch_shapes=[pltpu.SemaphoreType.DMA((na,)), pltpu.SemaphoreType.DMA((na,))],
        name="pair_exchange",
    )(*grads)


def _row_tile(rows, cols, align):
    best = align
    for cand in range(align, rows + 1, align):
        if rows % cand == 0 and cand * cols <= 256 * 1024:
            best = cand
    return best


def _pair_sum(g, got, place, transit, name):
    _, rows, cols = g.shape
    hr = rows // 2
    tr = _row_tile(hr, cols, _align_of(transit))
    nt = hr // tr

    def body(p_ref, g_ref, r_ref, s_ref, own_ref):
        total = g_ref[...] + r_ref[...]
        s_ref[...] = total.astype(transit)

        @pl.when(pl.program_id(1) == p_ref[1])
        def _():
            own_ref[...] = total

    grid_spec = pltpu.PrefetchScalarGridSpec(
        num_scalar_prefetch=1, grid=(nt, 4),
        in_specs=[pl.BlockSpec((None, tr, cols), lambda i, k, p: (k, p[0] * nt + i, 0)),
                  pl.BlockSpec((None, tr, cols), lambda i, k, p: (k, i, 0))],
        out_specs=[pl.BlockSpec((None, tr, cols), lambda i, k, p: (k, i, 0)),
                   pl.BlockSpec((tr, cols), lambda i, k, p: (i, 0))])
    return pl.pallas_call(
        body, grid_spec=grid_spec,
        out_shape=[jax.ShapeDtypeStruct((4, hr, cols), transit), jax.ShapeDtypeStruct((hr, cols), F32)],
        name=name, compiler_params=_params(("parallel", "arbitrary")),
    )(place, g, got)


def _chip_exchange(sums):
    na = len(sums)

    def body(*refs):
        s_refs, o_refs = refs[:na], refs[na:2 * na]
        send_sems, recv_sems = refs[2 * na:]
        x, y, c, chips = _place()
        for ai in range(na):
            _, rows, cols = s_refs[ai].shape
            row_bytes = cols * jnp.dtype(s_refs[ai].dtype).itemsize
            for r0, nr in _row_chunks(rows, row_bytes, _align_of(s_refs[ai].dtype)):
                for j, chip in enumerate(chips):
                    _remote(s_refs[ai].at[_chip_index(*chip), pl.ds(r0, nr), :], o_refs[ai].at[j, pl.ds(r0, nr), :],
                            send_sems.at[ai, j], recv_sems.at[ai, j], (*chip, c)).start()
        for ai in range(na):
            for j, chip in enumerate(chips):
                _remote(o_refs[ai].at[j], o_refs[ai].at[j], send_sems.at[ai, j], recv_sems.at[ai, j],
                        (*chip, c)).wait()

    return pl.pallas_call(
        body, in_specs=[ANY] * na, out_specs=[ANY] * na,
        out_shape=[jax.ShapeDtypeStruct((3, *s.shape[1:]), s.dtype) for s in sums],
        scratch_shapes=[pltpu.SemaphoreType.DMA((na, 3)), pltpu.SemaphoreType.DMA((na, 3))],
        name="chip_exchange",
    )(*sums)


def _chip_sum(own, landed, name):
    hr, cols = own.shape
    tr = _row_tile(hr, cols, _align_of(landed.dtype))

    def body(o_ref, l_ref, f_ref):
        acc = o_ref[...]
        for j in range(3):
            acc = acc + l_ref[j].astype(F32)
        f_ref[...] = acc

    return pl.pallas_call(
        body, grid=(hr // tr,),
        in_specs=[pl.BlockSpec((tr, cols), lambda i: (i, 0)), pl.BlockSpec((3, tr, cols), lambda i: (0, i, 0))],
        out_specs=pl.BlockSpec((tr, cols), lambda i: (i, 0)),
        out_shape=jax.ShapeDtypeStruct((hr, cols), F32), name=name,
        compiler_params=_params(("parallel",)),
    )(own, landed)


def _final_exchange(halves, small):
    nh = len(halves)

    def body(*refs):
        h_refs, s_ref = refs[:nh], refs[nh]
        o_refs, so_ref = refs[nh + 1:2 * nh + 1], refs[2 * nh + 1]
        send_sems, recv_sems, local_sem, ssend_sems, srecv_sems = refs[2 * nh + 2:]
        x, y, c, _ = _place()
        me = 4 * x + 2 * y + c
        sibling = (x, y, 1 - c)
        for hi in range(nh):
            hr, cols = h_refs[hi].shape
            for r0, nr in _row_chunks(hr, cols * 4, SUBLANES):
                _remote(h_refs[hi].at[pl.ds(r0, nr), :], o_refs[hi].at[pl.ds(r0, nr), :],
                        send_sems.at[hi], recv_sems.at[hi], sibling).start()
        small_cps = [pltpu.make_async_copy(s_ref, so_ref.at[me], local_sem)]
        for r in range(1, 8):
            fx, fy, fc = (r >> 2) & 1, (r >> 1) & 1, r & 1
            peer = (1 - x if fx else x, 1 - y if fy else y, 1 - c if fc else c)
            small_cps.append(_remote(s_ref, so_ref.at[me], ssend_sems.at[r - 1], srecv_sems.at[r - 1], peer))
        for cp in small_cps:
            cp.start()
        for hi in range(nh):
            _remote(h_refs[hi], o_refs[hi], send_sems.at[hi], recv_sems.at[hi], sibling).wait()
        for cp in small_cps:
            cp.wait()

    return pl.pallas_call(
        body, in_specs=[ANY] * (nh + 1), out_specs=[ANY] * (nh + 1),
        out_shape=[jax.ShapeDtypeStruct(h.shape, F32) for h in halves]
        + [jax.ShapeDtypeStruct((8, *small.shape), F32)],
        scratch_shapes=[pltpu.SemaphoreType.DMA((nh,)), pltpu.SemaphoreType.DMA((nh,)),
                        pltpu.SemaphoreType.DMA, pltpu.SemaphoreType.DMA((7,)), pltpu.SemaphoreType.DMA((7,))],
        name="final_exchange",
    )(*halves, small)


def _adamw_halves(w, own, other, m, v, place, name):
    r, c = w.shape
    hr = r // 2
    tr = _row_tile(hr, c, SUBLANES)
    nt = hr // tr
    c1 = 1.0 - ADAM_B1 ** ADAM_STEP
    c2 = 1.0 - ADAM_B2 ** ADAM_STEP

    def body(p_ref, w_ref, own_ref, other_ref, m_ref, v_ref, g_ref, d_ref, nm_ref, nv_ref):
        mine = pl.program_id(0) // nt == p_ref[0]
        gv = jnp.where(mine, own_ref[...], other_ref[...])
        nm = ADAM_B1 * m_ref[...] + (1.0 - ADAM_B1) * gv
        nv = ADAM_B2 * v_ref[...] + (1.0 - ADAM_B2) * (gv * gv)
        g_ref[...] = gv
        d_ref[...] = -ADAM_LR * ((nm / c1) / (jnp.sqrt(nv / c2) + ADAM_EPS) + ADAM_WD * w_ref[...])
        nm_ref[...] = nm
        nv_ref[...] = nv

    full = pl.BlockSpec((tr, c), lambda i, p: (i, 0))
    half = pl.BlockSpec((tr, c), lambda i, p: (i % nt, 0))
    out = jax.ShapeDtypeStruct((r, c), F32)
    grid_spec = pltpu.PrefetchScalarGridSpec(num_scalar_prefetch=1, grid=(2 * nt,),
                                             in_specs=[full, half, half, full, full], out_specs=[full] * 4)
    return pl.pallas_call(body, grid_spec=grid_spec, out_shape=[out] * 4, name=name,
                          compiler_params=_params(("parallel",)))(place, w, own, other, m, v)


def _adamw_many(ws, gs, ms, vs, name):
    n = len(ws)
    c1 = 1.0 - ADAM_B1 ** ADAM_STEP
    c2 = 1.0 - ADAM_B2 ** ADAM_STEP

    def body(*refs):
        w_refs, g_refs, m_refs, v_refs = (refs[k * n:(k + 1) * n] for k in range(4))
        d_refs, nm_refs, nv_refs = (refs[(4 + k) * n:(5 + k) * n] for k in range(3))
        for i in range(n):
            gv = g_refs[i][...]
            nm = ADAM_B1 * m_refs[i][...] + (1.0 - ADAM_B1) * gv
            nv = ADAM_B2 * v_refs[i][...] + (1.0 - ADAM_B2) * (gv * gv)
            d_refs[i][...] = -ADAM_LR * ((nm / c1) / (jnp.sqrt(nv / c2) + ADAM_EPS) + ADAM_WD * w_refs[i][...])
            nm_refs[i][...] = nm
            nv_refs[i][...] = nv

    vmem = pl.BlockSpec(memory_space=pltpu.VMEM)
    shapes = [jax.ShapeDtypeStruct(t.shape, F32) for t in ws]
    outs = pl.pallas_call(body, in_specs=[vmem] * (4 * n), out_specs=[vmem] * (3 * n), out_shape=shapes * 3,
                          name=name, compiler_params=_params(vmem_mb=56))(*ws, *gs, *ms, *vs)
    return outs[:n], outs[n:2 * n], outs[2 * n:]


BIG = ("w_in", "w_glu", "w_out", "w_up", "w_down")
WEIGHTS = ("norm_mix_g", "w_in", "a_re", "a_im", "log_step", "b_re", "b_im", "c_re", "c_im", "d_skip", "w_glu",
           "sink", "norm_attn_g", "norm_ssm_g", "w_out", "norm_ffn_g", "w_up", "conv_w", "conv_b", "w_down",
           "norm_final_g")
SMALL = ("norm_mix_g", "a_re", "a_im", "log_step", "b_re", "b_im", "c_re", "c_im", "d_skip", "sink",
         "norm_attn_g", "norm_ssm_g", "norm_ffn_g", "conv_w", "conv_b", "norm_final_g")
SMALL_ROWS = 48
N_DEV = 8


def _tile_rows(size):
    return -(-size // (SUBLANES * D_MODEL)) * SUBLANES


def _by_owner(name, g):
    if name == "w_up":
        return g
    return g.reshape(4, g.shape[0] // 4, g.shape[1])


def _view(name, t):
    if name == "w_in":
        return jnp.swapaxes(t[0], 0, 1)
    if name in ("b_re", "b_im"):
        return jnp.swapaxes(t, -1, -2)
    return t


def _unview(name, t):
    if name == "w_in":
        return jnp.swapaxes(t, 0, 1)[None]
    if name in ("b_re", "b_im"):
        return jnp.swapaxes(t, -1, -2)
    return t


def kernel(x, norm_mix_g, w_in, a_re, a_im, log_step, b_re, b_im, c_re, c_im, d_skip, w_glu, sink, norm_attn_g, norm_ssm_g, w_out, norm_ffn_g, w_up, conv_w, conv_b, w_down, norm_final_g, loss_target, m_norm_mix_g, m_w_in, m_a_re, m_a_im, m_log_step, m_b_re, m_b_im, m_c_re, m_c_im, m_d_skip, m_w_glu, m_sink, m_norm_attn_g, m_norm_ssm_g, m_w_out, m_norm_ffn_g, m_w_up, m_conv_w, m_conv_b, m_w_down, m_norm_final_g, v_norm_mix_g, v_w_in, v_a_re, v_a_im, v_log_step, v_b_re, v_b_im, v_c_re, v_c_im, v_d_skip, v_w_glu, v_sink, v_norm_attn_g, v_norm_ssm_g, v_w_out, v_norm_ffn_g, v_w_up, v_conv_w, v_conv_b, v_w_down, v_norm_final_g):
    given = dict(locals())
    w = {n: given[n] for n in WEIGHTS}
    m = {n: given["m_" + n] for n in WEIGHTS}
    v = {n: given["v_" + n] for n in WEIGHTS}
    xy = 2 * lax.axis_index("x") + lax.axis_index("y")

    core = lax.axis_index("c")
    place = jnp.stack([core, xy]).astype(jnp.int32)

    conv_rows = jnp.pad(w["conv_w"][0], ((0, 2 * SUBLANES - 3), (0, 0)))
    rows = lambda t: t.reshape(4 * t.shape[1], t.shape[2])
    (w_in_all,) = _gather_weights([_view("w_in", w["w_in"])], [BF16])
    wb = {"w_in": rows(w_in_all)}
    early = ("w_in", "w_glu", "w_out")
    mixer = [_cast_place(w[n][0], place, BF16, w_in_all, "cast_" + n) for n in ("w_glu", "w_out")]
    mixer.append(_cast_place(conv_rows, place, F32, w_in_all, "cast_conv_w"))
    *mixer_flight, mixer_token = _spread_start(mixer, "spread_mixer_start")
    late = ("w_up", "w_down")
    *late_flight, token = _spread_start(
        [_cast_place(w[n][0], place, BF16, mixer_token, "cast_" + n) for n in late], "spread_ffn_start")

    def mixer_weights(after):
        w_glu4, w_out4, conv4 = _spread_wait(*mixer_flight, after, "spread_mixer_wait")
        return {"w_glu": rows(w_glu4), "w_out": rows(w_out4),
                "conv_w": conv4[:, :3].transpose(1, 0, 2).reshape(3, 2 * D_FF)}

    def late_weights(after):
        w_up4, w_down4 = _spread_wait(*late_flight, after, "spread_ffn_wait")
        return {"w_up": w_up4, "w_down": rows(w_down4)}

    sp = {n: w[n][0] for n in ("a_re", "a_im", "log_step", "b_re", "b_im", "c_re", "c_im", "d_skip",
                               "norm_mix_g", "norm_attn_g", "norm_ssm_g", "norm_ffn_g", "sink", "conv_b")}
    for n in ("norm_mix_g", "norm_attn_g", "norm_ssm_g", "norm_ffn_g", "sink", "conv_b"):
        sp[n] = sp[n].reshape(1, -1)
    sp["norm_mix_g"] = sp["norm_mix_g"] + token[:1, :1]
    sp["norm_final_g"] = w["norm_final_g"]
    flight = {}

    def ffn_grads_ready(dw_up, dw_down):
        *flight["pair"], token = _pair_start([dw_up, _by_owner("w_down", dw_down)])
        return token[:1, :1]

    def ffn_grads_next(after):
        mine, got = _pair_wait(*flight["pair"], after)
        sums, flight["own"] = zip(*[_pair_sum(a, b, place, BF16, "pair_sum_" + n) for n, a, b in zip(late, mine, got)])
        *flight["chip"], token = _chip_start(list(sums))
        return token[:1, :1]

    loss, grad_x, g = _local_step(x[0], loss_target[0], wb, sp, mixer_weights, late_weights, ffn_grads_ready,
                                  ffn_grads_next)

    def as_rows(t):
        rows = _tile_rows(t.size)
        return jnp.pad(t.reshape(-1), (0, rows * D_MODEL - t.size)).reshape(rows, D_MODEL)

    pieces = [as_rows(g[n]) for n in SMALL] + [as_rows(loss)]
    spare = N_DEV * SMALL_ROWS - sum(p.shape[0] for p in pieces)
    small = jnp.concatenate(pieces + [jnp.zeros((spare, D_MODEL), F32)]).reshape(4, 2 * SMALL_ROWS, D_MODEL)
    by_owner = [_by_owner(n, g[n]) for n in early] + [small]
    got = _pair_exchange(by_owner)
    transit = [BF16] * len(early) + [F32]
    chip_sums, own_sums = zip(*[_pair_sum(a, b, place, t, "pair_sum_" + n)
                                for n, a, b, t in zip(early + ("small",), by_owner, got, transit)])
    landed = _chip_exchange(list(chip_sums))
    halves = {n: _chip_sum(o, t, "chip_sum_" + n) for n, o, t in zip(early + ("small",), own_sums, landed)}
    late_landed = _chip_wait(*flight["chip"], grad_x)
    for n, o, t in zip(late, flight["own"], late_landed):
        halves[n] = _chip_sum(o, t, "chip_sum_" + n)
    *others, small_all = _final_exchange([halves[n] for n in BIG], halves["small"])
    small_all = small_all.reshape(N_DEV * SMALL_ROWS, D_MODEL)
    grads, row = {}, 0
    for n in SMALL:
        shape = (3, 4 * w[n].shape[-1]) if n == "conv_w" else w[n].shape[1:] if n != "norm_final_g" else w[n].shape
        size = math.prod(shape)
        grads[n] = small_all[row:row + _tile_rows(size)].reshape(-1)[:size].reshape(shape)
        row += _tile_rows(size)
    loss = small_all[row, 0]
    cw = w["conv_w"].shape[-1]
    grads["conv_w"] = lax.dynamic_slice_in_dim(grads["conv_w"], xy * cw, cw, axis=1)
    grads = {n: _view(n, grads[n].reshape(w[n].shape)) for n in SMALL}
    wv, mv, vv = ({n: _view(n, t[n]) for n in WEIGHTS} for t in (w, m, v))

    delta, new_m, new_v = {}, {}, {}
    for n, other in zip(BIG, others):
        two_d = lambda t: t.reshape(t.shape[-2:])
        grads[n], delta[n], new_m[n], new_v[n] = _adamw_halves(
            two_d(wv[n]), halves[n], other, two_d(mv[n]), two_d(vv[n]), place, "adamw_" + n)
    for group, name in ((("b_re", "b_im"), "adamw_b"), (tuple(n for n in SMALL if n not in ("b_re", "b_im")), "adamw_small")):
        row = lambda t: t.reshape(1, -1) if t.ndim == 1 else t
        d_, m_, v_ = _adamw_many(*[[row(t[n]) for n in group] for t in (wv, grads, mv, vv)], name)
        for n, dn, mn, vn in zip(group, d_, m_, v_):
            delta[n], new_m[n], new_v[n] = (t.reshape(wv[n].shape) for t in (dn, mn, vn))
    natural = lambda t: [_unview(n, t[n].reshape(wv[n].shape)) for n in WEIGHTS]
    return (loss, grad_x[None], *natural(grads), *natural(delta), *natural(new_m), *natural(new_v))
```

```python
import functools
import math

import jax
import jax.numpy as jnp
import numpy as np
from jax import lax
from jax.experimental import pallas as pl
from jax.experimental.pallas import tpu as pltpu

F32 = jnp.float32
BF16 = jnp.bfloat16

D_MODEL = 1024
N_Q_HEADS = 8
N_KV_HEADS = 2
HEAD_DIM = 64
ATTN_WIDTH = 512
KV_WIDTH = 128
QKV_WIDTH = ATTN_WIDTH + 2 * KV_WIDTH
WINDOW = 128
BLOCK = 128
ROPE_DIM = 16
ROPE_THETA = 500000.0
SCORE_SCALE = HEAD_DIM ** -0.5
SSM_WIDTH = 512
SSM_GROUP = 16
N_SSM_GROUPS = 32
SSM_STATE = 64
IN_WIDTH = 1280
D_FF = 2816
EPS = 1e-6
ADAM_LR = 0.001
ADAM_B1 = 0.9
ADAM_B2 = 0.999
ADAM_EPS = 1e-08
ADAM_WD = 0.01
ADAM_STEP = 10

VMEM_BYTES_V7X = 64 * 1024 * 1024
SUBLANES = 8
LANES = 128
SSM_CB = 4
SSM_CH = 128
SSM_ST = 512
N_SEG = SUBLANES

NN = (((1,), (0,)), ((), ()))
NT = (((1,), (1,)), ((), ()))
TN = (((0,), (0,)), ((), ()))


def _params(sem=None, vmem_mb=48):
    limit = vmem_mb * 1024 * 1024
    assert limit < VMEM_BYTES_V7X
    return pltpu.CompilerParams(dimension_semantics=sem, vmem_limit_bytes=limit)


def _dg(a, b, dims):
    return lax.dot_general(a, b, dims, preferred_element_type=F32)


def _sigmoid(x):
    return 1.0 / (1.0 + jnp.exp(-x))


_SQRT_HALF = 0.7071067811865476
_INV_SQRT_2PI = 0.3989422804014327


def _gelu(x):
    return 0.5 * x * (1.0 + lax.erf(x * _SQRT_HALF))


def _gelu_grad(x):
    return 0.5 * (1.0 + lax.erf(x * _SQRT_HALF)) + x * (_INV_SQRT_2PI * jnp.exp(-0.5 * x * x))


def _mm_tn(a, b, tm, tn, name):
    k, m = a.shape
    n = b.shape[1]

    def body(a_ref, b_ref, o_ref):
        o_ref[...] = _dg(a_ref[...], b_ref[...], TN)

    return pl.pallas_call(
        body, grid=(m // tm, n // tn),
        in_specs=[pl.BlockSpec((k, tm), lambda i, j: (0, i)), pl.BlockSpec((k, tn), lambda i, j: (0, j))],
        out_specs=pl.BlockSpec((tm, tn), lambda i, j: (i, j)),
        out_shape=jax.ShapeDtypeStruct((m, n), F32), name=name,
        compiler_params=_params(("parallel", "parallel")),
    )(a, b)


def _mm_nn_cols(a, b4, tm, name):
    m, k = a.shape
    s, _, n = b4.shape

    def body(a_ref, b_ref, o_ref):
        o_ref[...] = _dg(a_ref[...], b_ref[...], NN)

    return pl.pallas_call(
        body, grid=(m // tm, s),
        in_specs=[pl.BlockSpec((tm, k), lambda i, j: (i, 0)), pl.BlockSpec((None, k, n), lambda i, j: (j, 0, 0))],
        out_specs=pl.BlockSpec((tm, n), lambda i, j: (i, j)),
        out_shape=jax.ShapeDtypeStruct((m, s * n), F32), name=name,
        compiler_params=_params(("parallel", "parallel")),
    )(a, b4)


def _mm_tn_cols(a, b2, s, tm, name):
    k, m = a.shape
    h, _, wide = b2.shape
    per = s // h
    n = wide // per

    def body(a_ref, b_ref, o_ref):
        o_ref[...] = _dg(a_ref[...], b_ref[...], TN)

    return pl.pallas_call(
        body, grid=(s, m // tm),
        in_specs=[pl.BlockSpec((k, tm), lambda j, i: (0, i)),
                  pl.BlockSpec((None, k, n), lambda j, i: (j // per, 0, j % per))],
        out_specs=pl.BlockSpec((None, tm, n), lambda j, i: (j, i, 0)),
        out_shape=jax.ShapeDtypeStruct((s, m, n), F32), name=name,
        compiler_params=_params(("parallel", "parallel")),
    )(a, b2)


TM_EW = 256


def _rms_bwd_vals(xv, gv, dy):
    r = lax.rsqrt(jnp.mean(xv * xv, axis=-1, keepdims=True) + EPS)
    xh = xv * r
    dxh = dy * gv
    dx = r * (dxh - xh * jnp.mean(dxh * xh, axis=-1, keepdims=True))
    return dx, dy * xh


TM_FUSED = 512
TM_LOSS = 256


def _rms_vals(xv, gv):
    return xv * lax.rsqrt(jnp.mean(xv * xv, axis=-1, keepdims=True) + EPS) * gv


def _rope_blocks(src, dst, c, lo, hi):
    nq = ATTN_WIDTH // LANES
    for blk in range(nq + 1):
        t = src[:, blk * LANES:(blk + 1) * LANES]
        rot = t * c + pltpu.roll(t, LANES - 8, 1) * lo + pltpu.roll(t, 8, 1) * hi
        dst[:, blk * LANES:(blk + 1) * LANES] = (rot * SCORE_SCALE if blk < nq else rot).astype(BF16)
    dst[:, (nq + 1) * LANES:] = src[:, (nq + 1) * LANES:].astype(BF16)


def _rms_mm_rope(x, g, wt, tabs, name):
    l, d = x.shape
    n = wt.shape[0]

    def body(x_ref, g_ref, w_ref, c_ref, lo_ref, hi_ref, h_ref, qkv_ref, u_ref):
        h = _rms_vals(x_ref[...], g_ref[...]).astype(BF16)
        h_ref[...] = h
        out = _dg(h, w_ref[...], NT)
        _rope_blocks(out[:, :QKV_WIDTH], qkv_ref, c_ref[...], lo_ref[...], hi_ref[...])
        u_ref[...] = out[:, QKV_WIDTH:]

    row = lambda width: pl.BlockSpec((TM_FUSED, width), lambda i: (i, 0))
    return pl.pallas_call(
        body, grid=(l // TM_FUSED,),
        in_specs=[row(d), pl.BlockSpec((1, d), lambda i: (0, 0)), pl.BlockSpec((n, d), lambda i: (0, 0)),
                  row(LANES), row(LANES), row(LANES)],
        out_specs=[row(d), row(QKV_WIDTH), row(n - QKV_WIDTH)],
        out_shape=[jax.ShapeDtypeStruct((l, d), BF16), jax.ShapeDtypeStruct((l, QKV_WIDTH), BF16),
                   jax.ShapeDtypeStruct((l, n - QKV_WIDTH), F32)],
        name=name, compiler_params=_params(("parallel",)),
    )(x, g, wt, *tabs)


def _mix_mm_res_rms(attn, ys, g_attn, g_ssm, b, res, g, name):
    l, w = attn.shape
    d = b.shape[1]

    def body(a_ref, y_ref, ga_ref, gs_ref, b_ref, r_ref, g_ref, m_ref, x_ref, h_ref):
        m_ref[:, :w] = _rms_vals(a_ref[...], ga_ref[...]).astype(BF16)
        m_ref[:, w:] = _rms_vals(y_ref[...], gs_ref[...]).astype(BF16)
        xv = r_ref[...] + _dg(m_ref[...], b_ref[...], NN)
        x_ref[...] = xv
        h_ref[...] = _rms_vals(xv, g_ref[...]).astype(BF16)

    row = lambda width: pl.BlockSpec((TM_FUSED, width), lambda i: (i, 0))
    vec = lambda width: pl.BlockSpec((1, width), lambda i: (0, 0))
    return pl.pallas_call(
        body, grid=(l // TM_FUSED,),
        in_specs=[row(w), row(w), vec(w), vec(w), pl.BlockSpec((2 * w, d), lambda i: (0, 0)), row(d), vec(d)],
        out_specs=[row(2 * w), row(d), row(d)],
        out_shape=[jax.ShapeDtypeStruct((l, 2 * w), BF16), jax.ShapeDtypeStruct((l, d), F32),
                   jax.ShapeDtypeStruct((l, d), BF16)],
        name=name, compiler_params=_params(("parallel",)),
    )(attn, ys, g_attn, g_ssm, b, res, g)


def _mm_res_loss(a, b, res, g, target):
    l, k = a.shape
    d = b.shape[1]

    def body(a_ref, b_ref, r_ref, g_ref, t_ref, loss_ref, dx_ref, dxb_ref, dg_ref):
        xv = r_ref[...] + _dg(a_ref[...], b_ref[...], NN)
        gv = g_ref[...]
        r = lax.rsqrt(jnp.mean(xv * xv, axis=-1, keepdims=True) + EPS)
        xh = xv * r
        e = xh * gv - t_ref[...]
        part = jnp.sum(jnp.sum(e * e, axis=1, keepdims=True), axis=0, keepdims=True) * (0.5 / d)
        dy = e * (1.0 / d)
        dxh = dy * gv
        dx = r * (dxh - xh * jnp.mean(dxh * xh, axis=-1, keepdims=True))
        dx_ref[...] = dx
        dxb_ref[...] = dx.astype(BF16)

        @pl.when(pl.program_id(0) == 0)
        def _():
            dg_ref[...] = jnp.zeros_like(dg_ref)
            loss_ref[...] = jnp.zeros_like(loss_ref)

        dg_ref[...] += jnp.sum(dy * xh, axis=0, keepdims=True)
        loss_ref[...] += part

    row = lambda width: pl.BlockSpec((TM_LOSS, width), lambda i: (i, 0))
    vec = pl.BlockSpec((1, d), lambda i: (0, 0))
    return pl.pallas_call(
        body, grid=(l // TM_LOSS,),
        in_specs=[row(k), pl.BlockSpec((k, d), lambda i: (0, 0)), row(d), vec, row(d)],
        out_specs=[pl.BlockSpec((1, 1), lambda i: (0, 0)), row(d), row(d), vec],
        out_shape=[jax.ShapeDtypeStruct((1, 1), F32), jax.ShapeDtypeStruct((l, d), F32),
                   jax.ShapeDtypeStruct((l, d), BF16), jax.ShapeDtypeStruct((1, d), F32)],
        name="mm_down_loss", compiler_params=_params(("arbitrary",)),
    )(a, b, res, g, target)


def _mm_rms_bwd(a, b, a_spec, b_spec, matmul, x, g, res, name):
    l, d = x.shape

    def body(a_ref, b_ref, x_ref, g_ref, res_ref, dx_ref, dxb_ref, dg_ref):
        dx, dgr = _rms_bwd_vals(x_ref[...], g_ref[...], matmul(a_ref, b_ref))
        dx = dx + res_ref[...]
        dx_ref[...] = dx
        dxb_ref[...] = dx.astype(BF16)

        @pl.when(pl.program_id(0) == 0)
        def _():
            dg_ref[...] = jnp.zeros_like(dg_ref)

        dg_ref[...] += jnp.sum(dgr, axis=0, keepdims=True)

    row = pl.BlockSpec((TM_FUSED, d), lambda i: (i, 0))
    vec = pl.BlockSpec((1, d), lambda i: (0, 0))
    return pl.pallas_call(
        body, grid=(l // TM_FUSED,), in_specs=[a_spec, b_spec, row, vec, row], out_specs=[row, row, vec],
        out_shape=[jax.ShapeDtypeStruct((l, d), F32), jax.ShapeDtypeStruct((l, d), BF16),
                   jax.ShapeDtypeStruct((1, d), F32)],
        name=name, compiler_params=_params(("arbitrary",)),
    )(a, b, x, g, res)


def _mm_nn_rms_bwd(a, b, x, g, res, name):
    return _mm_rms_bwd(a, b, pl.BlockSpec((TM_FUSED, a.shape[1]), lambda i: (i, 0)),
                       pl.BlockSpec(b.shape, lambda i: (0, 0)),
                       lambda a_ref, b_ref: _dg(a_ref[...], b_ref[...], NN), x, g, res, name)


def _mm_cols_rms_bwd(a2, b4, x, g, res, name):
    h, _, wide = a2.shape
    s, _, n = b4.shape
    per = s // h

    def matmul(a_ref, b_ref):
        acc = None
        for j in range(s):
            part = _dg(a_ref[j // per, :, (j % per) * n:(j % per + 1) * n], b_ref[j], NT)
            acc = part if acc is None else acc + part
        return acc

    return _mm_rms_bwd(a2, b4, pl.BlockSpec((h, TM_FUSED, wide), lambda i: (0, i, 0)),
                       pl.BlockSpec(b4.shape, lambda i: (0, 0, 0), pipeline_mode=pl.Buffered(1)),
                       matmul, x, g, res, name)


def _mm_mix_bwd(dx, b, attn, ys, g_attn, g_ssm, name):
    l, w = attn.shape
    d = dx.shape[1]

    def body(dx_ref, b_ref, a_ref, y_ref, ga_ref, gs_ref, da_ref, dy_ref, dga_ref, dgs_ref):
        @pl.when(pl.program_id(0) == 0)
        def _():
            dga_ref[...] = jnp.zeros_like(dga_ref)
            dgs_ref[...] = jnp.zeros_like(dgs_ref)

        dm = _dg(dx_ref[...], b_ref[...], NT)
        for src, gr, off, dst, dgr in ((a_ref, ga_ref, 0, da_ref, dga_ref), (y_ref, gs_ref, w, dy_ref, dgs_ref)):
            dxv, dg_rows = _rms_bwd_vals(src[...], gr[...], dm[:, off:off + w])
            dst[...] = dxv
            dgr[...] += jnp.sum(dg_rows, axis=0, keepdims=True)

    row = lambda width: pl.BlockSpec((TM_FUSED, width), lambda i: (i, 0))
    vec = pl.BlockSpec((1, w), lambda i: (0, 0))
    return pl.pallas_call(
        body, grid=(l // TM_FUSED,),
        in_specs=[row(d), pl.BlockSpec((2 * w, d), lambda i: (0, 0)), row(w), row(w), vec, vec],
        out_specs=[row(w), row(w), vec, vec],
        out_shape=[jax.ShapeDtypeStruct((l, w), F32), jax.ShapeDtypeStruct((l, w), F32),
                   jax.ShapeDtypeStruct((1, w), F32), jax.ShapeDtypeStruct((1, w), F32)],
        name=name, compiler_params=_params(("arbitrary",)),
    )(dx, b, attn, ys, g_attn, g_ssm)


def _rope_tables(l):
    half = ROPE_DIM // 2
    f32 = np.float32
    inv_freq = np.power(f32(ROPE_THETA), -np.arange(half, dtype=f32) / f32(half))
    ang = np.arange(l, dtype=f32)[:, None] * inv_freq[None, :]
    cos, sin = np.cos(ang), np.sin(ang)
    ones = np.ones((l, HEAD_DIM - ROPE_DIM), f32)
    zeros = np.zeros((l, HEAD_DIM - ROPE_DIM), f32)
    zh = np.zeros((l, half), f32)
    c = np.concatenate([cos, cos, ones], axis=1)
    s_lo = np.concatenate([-sin, zh, zeros], axis=1)
    s_hi = np.concatenate([zh, sin, zeros], axis=1)
    return tuple(jnp.asarray(np.tile(t, (1, LANES // HEAD_DIM)), F32) for t in (c, s_lo, s_hi))


def _rope_bwd(dq, dkv, du_ssm, dpre, d_skip, tabs):
    l = dq.shape[0]
    nq = ATTN_WIDTH // LANES

    def body(dq_ref, dkv_ref, du_ref, dpre_ref, ds_ref, c_ref, lo_ref, hi_ref, o_ref):
        c, lo, hi = c_ref[...], lo_ref[...], hi_ref[...]
        for blk in range(nq + 1):
            t = dq_ref[:, blk * LANES:(blk + 1) * LANES] if blk < nq else dkv_ref[:, :KV_WIDTH]
            g = t * c + pltpu.roll(t * lo, 8, 1) + pltpu.roll(t * hi, LANES - 8, 1)
            o_ref[:, blk * LANES:(blk + 1) * LANES] = g.astype(BF16)
        o_ref[:, (nq + 1) * LANES:QKV_WIDTH] = dkv_ref[:, KV_WIDTH:].astype(BF16)
        o_ref[:, QKV_WIDTH:] = (du_ref[...] + dpre_ref[...] * ds_ref[...]).astype(BF16)

    tab = pl.BlockSpec((TM_EW, LANES), lambda i: (i, 0))
    wide = pl.BlockSpec((TM_EW, SSM_WIDTH), lambda i: (i, 0))
    return pl.pallas_call(
        body, grid=(l // TM_EW,),
        in_specs=[wide, pl.BlockSpec((TM_EW, 2 * KV_WIDTH), lambda i: (i, 0)), wide, wide,
                  pl.BlockSpec((1, SSM_WIDTH), lambda i: (0, 0)), tab, tab, tab],
        out_specs=pl.BlockSpec((TM_EW, IN_WIDTH), lambda i: (i, 0)),
        out_shape=jax.ShapeDtypeStruct((l, IN_WIDTH), BF16), name="rope_bwd",
        compiler_params=_params(("parallel",)),
    )(dq, dkv, du_ssm, dpre, d_skip, *tabs)


_Q_COLS = ATTN_WIDTH // LANES
_NEG = -1e30


def _window_specs(nb, width, col):
    return [
        pl.BlockSpec((BLOCK, width), lambda n: (jnp.maximum(n - 1, 0), col)),
        pl.BlockSpec((BLOCK, width), lambda n: (n, col)),
        pl.BlockSpec((BLOCK, width), lambda n: (jnp.minimum(n + 1, nb - 1), col)),
    ]


def _stacked_sink(sink_ref, heads):
    rid = lax.broadcasted_iota(jnp.int32, (len(heads) * BLOCK, 1), 0)
    sk = jnp.full(rid.shape, sink_ref[0, heads[-1]], F32)
    for g in range(len(heads) - 2, -1, -1):
        sk = jnp.where(rid < (g + 1) * BLOCK, sink_ref[0, heads[g]], sk)
    return sk


def _attn_fwd(qkv, sink):
    l = qkv.shape[0]
    nb = l // BLOCK
    grp = N_Q_HEADS // N_KV_HEADS

    def body(sink_ref, q_ref, k0, k1, k2, v0, v1, v2, o_ref, lse_ref):
        n = pl.program_id(0)
        q = q_ref[...]
        kw = jnp.concatenate([k0[...], k1[...], k2[...]], axis=0)
        vw = jnp.concatenate([v0[...], v1[...], v2[...]], axis=0)
        row = lax.broadcasted_iota(jnp.int32, (grp * BLOCK, 3 * BLOCK), 0)
        col = lax.broadcasted_iota(jnp.int32, (grp * BLOCK, 3 * BLOCK), 1)
        valid = jnp.abs(col - BLOCK - (row & (BLOCK - 1))) <= WINDOW
        valid &= jnp.logical_not((n == 0) & (col < BLOCK))
        valid &= jnp.logical_not((n == nb - 1) & (col >= 2 * BLOCK))
        for hk in range(N_KV_HEADS):
            heads = range(hk * grp, (hk + 1) * grp)
            qs = jnp.concatenate([q[:, h * HEAD_DIM:(h + 1) * HEAD_DIM] for h in heads], axis=0)
            kh = kw[:, hk * HEAD_DIM:(hk + 1) * HEAD_DIM]
            vh = vw[:, hk * HEAD_DIM:(hk + 1) * HEAD_DIM]
            s = jnp.where(valid, _dg(qs, kh, NT), _NEG)
            sk = _stacked_sink(sink_ref, heads)
            m = jnp.maximum(jnp.max(s, axis=1, keepdims=True), sk)
            p = jnp.exp(s - m)
            denom = jnp.sum(p, axis=1, keepdims=True) + jnp.exp(sk - m)
            o = _dg((p / denom).astype(BF16), vh, NN)
            lse = m + jnp.log(denom)
            for g, h in enumerate(heads):
                o_ref[:, h * HEAD_DIM:(h + 1) * HEAD_DIM] = o[g * BLOCK:(g + 1) * BLOCK]
                lse_ref[:, h:h + 1] = lse[g * BLOCK:(g + 1) * BLOCK]

    return pl.pallas_call(
        body, grid=(nb,),
        in_specs=[pl.BlockSpec(memory_space=pltpu.SMEM),
                  pl.BlockSpec((BLOCK, ATTN_WIDTH), lambda n: (n, 0))]
        + _window_specs(nb, KV_WIDTH, _Q_COLS) + _window_specs(nb, KV_WIDTH, _Q_COLS + 1),
        out_specs=[pl.BlockSpec((BLOCK, ATTN_WIDTH), lambda n: (n, 0)),
                   pl.BlockSpec((BLOCK, N_Q_HEADS), lambda n: (n, 0))],
        out_shape=[jax.ShapeDtypeStruct((l, ATTN_WIDTH), F32), jax.ShapeDtypeStruct((l, N_Q_HEADS), F32)],
        name="attn_fwd", compiler_params=_params(("parallel",)),
    )(sink, qkv, qkv, qkv, qkv, qkv, qkv, qkv)


def _attn_bwd(qkv, attn, dattn, lse, sink):
    l = qkv.shape[0]
    nb = l // BLOCK
    grp = N_Q_HEADS // N_KV_HEADS
    win = 3 * BLOCK

    def body(sink_ref, q_ref, k0, k1, k2, v0, v1, v2, o_ref, d_ref, l_ref, dq_ref, dkv_ref, dsink_ref, ring_ref):
        n = pl.program_id(0)

        @pl.when(n == 0)
        def _():
            dsink_ref[...] = jnp.zeros_like(dsink_ref)
            ring_ref[...] = jnp.zeros_like(ring_ref)

        @pl.when(n < nb)
        def _():
            first, last = n == 0, n == nb - 1
            cat = lambda a, b, c: jnp.concatenate([a[...], b[...], c[...]], axis=0)
            q, kw, vw = q_ref[...], cat(k0, k1, k2), cat(v0, v1, v2)
            dov = d_ref[...]
            prod = o_ref[...] * dov
            dob = dov.astype(BF16)
            lse = l_ref[...]
            row = lax.broadcasted_iota(jnp.int32, (grp * BLOCK, win), 0)
            col = lax.broadcasted_iota(jnp.int32, (grp * BLOCK, win), 1)
            valid = jnp.abs(col - BLOCK - (row & (BLOCK - 1))) <= WINDOW
            valid &= jnp.logical_not(first & (col < BLOCK))
            valid &= jnp.logical_not(last & (col >= 2 * BLOCK))

            dsink_parts, dks, dvs = [], [], []
            for hk in range(N_KV_HEADS):
                heads = range(hk * grp, (hk + 1) * grp)
                ksl = slice(hk * HEAD_DIM, (hk + 1) * HEAD_DIM)
                hsl = [slice(h * HEAD_DIM, (h + 1) * HEAD_DIM) for h in heads]
                stack = lambda parts: jnp.concatenate(parts, axis=0)
                qs = stack([q[:, s_] for s_ in hsl])
                dos = stack([dob[:, s_] for s_ in hsl])
                deltas = stack([jnp.sum(prod[:, s_], axis=1, keepdims=True) for s_ in hsl])
                lses = stack([lse[:, h:h + 1] for h in heads])
                kh, vh = kw[:, ksl], vw[:, ksl]
                s = jnp.where(valid, _dg(qs, kh, NT), _NEG)
                p = jnp.exp(s - lses)
                dp = _dg(dos, vh, NT)
                ds = (p * (dp - deltas)).astype(BF16)
                dq = _dg(ds, kh, NN) * SCORE_SCALE
                sink_rows = jnp.exp(_stacked_sink(sink_ref, heads) - lses) * deltas
                for g in range(grp):
                    dq_ref[:, hsl[g]] = dq[g * BLOCK:(g + 1) * BLOCK]
                    dsink_parts.append(jnp.sum(sink_rows[g * BLOCK:(g + 1) * BLOCK], axis=0, keepdims=True))
                dks.append(_dg(ds, qs, TN))
                dvs.append(_dg(p.astype(BF16), dos, TN))
            dsink_ref[...] -= jnp.concatenate(dsink_parts, axis=1)
            part = jnp.concatenate(dks + dvs, axis=1)
            ring_ref[(n + 2) % 3] += part[0:BLOCK]
            ring_ref[n % 3] += part[BLOCK:2 * BLOCK]
            ring_ref[(n + 1) % 3] = part[2 * BLOCK:]

        @pl.when(n >= 1)
        def _():
            dkv_ref[...] = ring_ref[(n + 2) % 3]

    centre = lambda n: jnp.minimum(n, nb - 1)
    window = lambda width, col: [
        pl.BlockSpec((BLOCK, width), lambda n: (jnp.maximum(centre(n) - 1, 0), col)),
        pl.BlockSpec((BLOCK, width), lambda n: (centre(n), col)),
        pl.BlockSpec((BLOCK, width), lambda n: (jnp.minimum(centre(n) + 1, nb - 1), col))]
    own = lambda width: pl.BlockSpec((BLOCK, width), lambda n: (centre(n), 0))
    return pl.pallas_call(
        body, grid=(nb + 1,),
        in_specs=[pl.BlockSpec(memory_space=pltpu.SMEM), own(ATTN_WIDTH)]
        + window(KV_WIDTH, _Q_COLS) + window(KV_WIDTH, _Q_COLS + 1)
        + [own(ATTN_WIDTH), own(ATTN_WIDTH), own(N_Q_HEADS)],
        out_specs=[own(ATTN_WIDTH), pl.BlockSpec((BLOCK, 2 * KV_WIDTH), lambda n: (jnp.maximum(n - 1, 0), 0)),
                   pl.BlockSpec((1, N_Q_HEADS), lambda n: (0, 0))],
        out_shape=[jax.ShapeDtypeStruct((l, ATTN_WIDTH), F32), jax.ShapeDtypeStruct((l, 2 * KV_WIDTH), F32),
                   jax.ShapeDtypeStruct((1, N_Q_HEADS), F32)],
        scratch_shapes=[pltpu.VMEM((3, BLOCK, 2 * KV_WIDTH), F32)],
        name="attn_bwd", compiler_params=_params(("arbitrary",)),
    )(sink, qkv, qkv, qkv, qkv, qkv, qkv, qkv, attn, dattn, lse)


def _ssm_disc(a_re, a_im, log_step, b_re, b_im):
    step = jnp.exp(log_step)[..., None]
    mag = jnp.exp(a_re * step)
    lb_re, lb_im = mag * jnp.cos(a_im * step), mag * jnp.sin(a_im * step)
    nr, ni = lb_re - 1.0, lb_im
    den = a_re * a_re + a_im * a_im
    f_re = ((nr * a_re + ni * a_im) / den)[..., None]
    f_im = ((ni * a_re - nr * a_im) / den)[..., None]
    return lb_re, lb_im, f_re * b_re - f_im * b_im, f_re * b_im + f_im * b_re


def _ssm_pack(lb_re, lb_im, bb_re, bb_im, c_re, c_im):
    eye = jnp.eye(SSM_CH // SSM_GROUP, dtype=F32)
    ng = SSM_CH // SSM_GROUP

    def diag_b(bb):
        t = bb.reshape(2, SSM_CB, ng, SSM_STATE, SSM_GROUP)
        return jnp.einsum('dkgpc,gh->dkgchp', t, eye).reshape(2, SSM_CB, SSM_CH, SSM_ST)

    def diag_c(cc):
        t = cc.reshape(2, SSM_CB, ng, SSM_GROUP, SSM_STATE)
        return jnp.einsum('dkgcp,gh->dkhpgc', t, eye).reshape(2, SSM_CB, SSM_ST, SSM_CH)

    bcat = jnp.concatenate([diag_b(bb_re), diag_b(bb_im)], axis=-1)
    ccat = jnp.concatenate([diag_c(c_re), -diag_c(c_im)], axis=-2)
    lam_re = lb_re.reshape(2, SSM_CB, 1, SSM_ST)
    lam_im = lb_im.reshape(2, SSM_CB, 1, SSM_ST)
    return bcat, ccat, lam_re, lam_im


def _ssm_unpack(dbcat, dccat_t, dlam_re, dlam_im):
    ng = SSM_CH // SSM_GROUP
    eye = jnp.eye(ng, dtype=F32)

    def undiag(t, order):
        t = t.reshape(2, SSM_CB, ng, SSM_GROUP, ng, SSM_STATE)
        return jnp.einsum('dkgchp,gh->' + order, t, eye)

    b_shape = (2, N_SSM_GROUPS, SSM_STATE, SSM_GROUP)
    c_shape = (2, N_SSM_GROUPS, SSM_GROUP, SSM_STATE)
    dbb_re = undiag(dbcat[..., :SSM_ST], 'dkgpc').reshape(b_shape)
    dbb_im = undiag(dbcat[..., SSM_ST:], 'dkgpc').reshape(b_shape)
    dc_re = undiag(dccat_t[..., :SSM_ST], 'dkgcp').reshape(c_shape)
    dc_im = -undiag(dccat_t[..., SSM_ST:], 'dkgcp').reshape(c_shape)
    shape = (2, N_SSM_GROUPS, SSM_STATE)
    return dlam_re.reshape(shape), dlam_im.reshape(shape), dbb_re, dbb_im, dc_re, dc_im


def _to_segments(t):
    l, w = t.shape
    return t.reshape(N_SEG, l // N_SEG, w).transpose(1, 0, 2).reshape(l, w)


def _from_segments(t):
    l, w = t.shape
    return t.reshape(l // N_SEG, N_SEG, w).transpose(1, 0, 2).reshape(l, w)


SSM_RC = 256
SSM_JC = SSM_RC // N_SEG
_RE, _IM = pl.ds(0, SSM_ST), pl.ds(SSM_ST, SSM_ST)


def _cfma(ar, ai, xr, xi, br, bi):
    return ar * xr - ai * xi + br, ar * xi + ai * xr + bi


def _chunk_rows(ci, rev, nc):
    start = jnp.where(rev, (nc - 1 - ci) * SSM_RC, ci * SSM_RC)
    return pl.ds(pl.multiple_of(start, SSM_RC), SSM_RC)


def _scan_chunk(src, dst, ar, ai, rev, nj, ci, carry, prev_ref=None):
    def rows_of(staged, j, k):
        at = jnp.where(rev, SSM_JC - 1 - k, k) if staged else j
        return pl.ds(pl.multiple_of(at * N_SEG, N_SEG), N_SEG)

    for k in range(SSM_JC):
        jj = ci * SSM_JC + k
        j = jnp.where(rev, nj - 1 - jj, jj)
        rows = rows_of(src[1], j, k)
        nr, ni = _cfma(ar, ai, carry[0], carry[1], src[0][rows, _RE], src[0][rows, _IM])
        if dst is not None:
            rows = rows_of(dst[1], j, k)
            dst[0][rows, _RE] = nr
            dst[0][rows, _IM] = ni
        if prev_ref is None:
            carry = (nr, ni)
            continue
        jp = jnp.where(rev, j - 1, j + 1)
        if k == SSM_JC - 1:
            inside = jnp.where((jp >= 0) & (jp < nj), 1.0, 0.0)
            jp = jnp.clip(jp, 0, nj - 1)
        prow = pl.ds(pl.multiple_of(jp * N_SEG, N_SEG), N_SEG)
        xr, xi = prev_ref[prow, _RE], prev_ref[prow, _IM]
        sr, si = nr * xr + ni * xi, ni * xr - nr * xi
        if k == SSM_JC - 1:
            sr, si = inside * sr, inside * si
        carry = (nr, ni, carry[2] + sr, carry[3] + si)
    return carry


def _segment_inits(ar, ai, end_r, end_i, rev, nj):
    pr, pi = ar, ai
    for _ in range(int(math.log2(nj))):
        pr, pi = pr * pr - pi * pi, 2.0 * pr * pi
    seg = lax.broadcasted_iota(jnp.int32, end_r.shape, 0)
    zero = jnp.zeros_like(end_r)

    def chain(shift, keep):
        ir, ii = zero, zero
        for _ in range(N_SEG - 1):
            tr, ti = _cfma(pr, pi, ir, ii, end_r, end_i)
            ir = jnp.where(keep, pltpu.roll(tr, shift, 0), 0.0)
            ii = jnp.where(keep, pltpu.roll(ti, shift, 0), 0.0)
        return ir, ii

    up_r, up_i = chain(1, seg >= 1)
    dn_r, dn_i = chain(N_SEG - 1, seg <= N_SEG - 2)
    return jnp.where(rev, dn_r, up_r), jnp.where(rev, dn_i, up_i)


def _ssm_specs(l):
    act = pl.BlockSpec((l, SSM_CH), lambda k, d: (0, k))
    bmat = pl.BlockSpec((None, None, SSM_CH, 2 * SSM_ST), lambda k, d: (d, k, 0, 0))
    cmat = pl.BlockSpec((None, None, 2 * SSM_ST, SSM_CH), lambda k, d: (d, k, 0, 0))
    lam = pl.BlockSpec((None, None, 1, SSM_ST), lambda k, d: (d, k, 0, 0))
    return act, bmat, cmat, lam


def _ssm_fwd(u_seg, bcat, ccat, lam_re, lam_im):
    l = u_seg.shape[0]
    nj = l // N_SEG
    nc = l // SSM_RC

    def body(u_ref, b_ref, c_ref, lr_ref, li_ref, y_ref, keep_ref, xs_ref, stage0, stage1, keep_sem):
        k, d = pl.program_id(0), pl.program_id(1)
        rev = d == 1
        shape = (N_SEG, SSM_ST)
        ar, ai = jnp.broadcast_to(lr_ref[...], shape), jnp.broadcast_to(li_ref[...], shape)
        zero = jnp.zeros(shape, F32)

        def inputs(ci, stage):
            rows = _chunk_rows(ci, rev, nc)
            bu = _dg(u_ref[rows, :], b_ref[...], NN)
            stage[...] = bu
            xs_ref[rows, :] = bu

        def first(stage, ci, carry):
            return _scan_chunk((stage, True), None, ar, ai, rev, nj, ci, carry)

        def first_pass(t, carry):
            inputs(2 * t + 1, stage1)
            carry = first(stage0, 2 * t, carry)
            inputs(2 * t + 2, stage0)
            return first(stage1, 2 * t + 1, carry)

        inputs(0, stage0)
        carry = lax.fori_loop(0, nc // 2 - 1, first_pass, (zero, zero))
        inputs(nc - 1, stage1)
        carry = first(stage0, nc - 2, carry)
        end_r, end_i = first(stage1, nc - 1, carry)
        init = _segment_inits(ar, ai, end_r, end_i, rev, nj)

        @pl.when(d == 0)
        def _():
            y_ref[...] = jnp.zeros_like(y_ref)

        def outputs(ci):
            rows = _chunk_rows(ci, rev, nc)
            y_ref[rows, :] += _dg(xs_ref[rows, :].astype(BF16), c_ref[...], NN)
            pltpu.make_async_copy(xs_ref.at[rows], keep_ref.at[d, k, rows], keep_sem).start()

        def second(ci, carry):
            return _scan_chunk((xs_ref, False), (xs_ref, False), ar, ai, rev, nj, ci, carry)

        def second_pass(ci, carry):
            outputs(ci - 1)
            return second(ci, carry)

        lax.fori_loop(1, nc, second_pass, second(0, init))
        outputs(nc - 1)
        pltpu.make_async_copy(xs_ref, keep_ref.at[d, k], keep_sem).wait()

    act, bmat, cmat, lam = _ssm_specs(l)
    return pl.pallas_call(
        body, grid=(SSM_CB, 2), in_specs=[act, bmat, cmat, lam, lam], out_specs=[act, ANY],
        out_shape=[jax.ShapeDtypeStruct((l, SSM_WIDTH), F32),
                   jax.ShapeDtypeStruct((2, SSM_CB, l, 2 * SSM_ST), F32)],
        scratch_shapes=[pltpu.VMEM((l, 2 * SSM_ST), F32), pltpu.VMEM((SSM_RC, 2 * SSM_ST), F32),
                        pltpu.VMEM((SSM_RC, 2 * SSM_ST), F32), pltpu.SemaphoreType.DMA],
        name="ssm_fwd", compiler_params=_params(("parallel", "arbitrary"), vmem_mb=56),
    )(u_seg, bcat.astype(BF16), ccat.astype(BF16), lam_re, lam_im)


def _ssm_bwd(u_seg, dy_seg, states, bcat, ccat, lam_re, lam_im):
    l = u_seg.shape[0]
    nj = l // N_SEG
    nc = l // SSM_RC

    def body(u_ref, dy_ref, keep_ref, b_ref, c_ref, lr_ref, li_ref,
             du_ref, db_ref, dc_ref, dlr_ref, dli_ref, xs_ref, gs_ref, stage0, stage1, keep_sem):
        k, d = pl.program_id(0), pl.program_id(1)
        rev = d == 1
        back = jnp.logical_not(rev)
        shape = (N_SEG, SSM_ST)
        ar, ai = jnp.broadcast_to(lr_ref[...], shape), -jnp.broadcast_to(li_ref[...], shape)
        zero = jnp.zeros(shape, F32)
        fetch = pltpu.make_async_copy(keep_ref.at[d, k], xs_ref, keep_sem)
        fetch.start()

        def inputs(ci, stage):
            rows = _chunk_rows(ci, back, nc)
            dx = _dg(dy_ref[rows, :], c_ref[...], NT)
            stage[...] = dx
            gs_ref[rows, :] = dx

        def first(stage, ci, carry):
            return _scan_chunk((stage, True), None, ar, ai, back, nj, ci, carry)

        def first_pass(t, carry):
            inputs(2 * t + 1, stage1)
            carry = first(stage0, 2 * t, carry)
            inputs(2 * t + 2, stage0)
            return first(stage1, 2 * t + 1, carry)

        inputs(0, stage0)
        carry = lax.fori_loop(0, nc // 2 - 1, first_pass, (zero, zero))
        inputs(nc - 1, stage1)
        carry = first(stage0, nc - 2, carry)
        end_r, end_i = first(stage1, nc - 1, carry)
        init = _segment_inits(ar, ai, end_r, end_i, back, nj)
        fetch.wait()
        db_ref[...] = jnp.zeros_like(db_ref)
        dc_ref[...] = jnp.zeros_like(dc_ref)

        @pl.when(d == 0)
        def _():
            du_ref[...] = jnp.zeros_like(du_ref)

        def outputs(ci, stage):
            rows = _chunk_rows(ci, back, nc)
            g = stage[...].astype(BF16)
            dc_ref[...] += _dg(dy_ref[rows, :], xs_ref[rows, :].astype(BF16), TN)
            db_ref[...] += _dg(u_ref[rows, :], g, TN)
            du_ref[rows, :] += _dg(g, b_ref[...], NT)

        def second(ci, stage, carry):
            return _scan_chunk((gs_ref, False), (stage, True), ar, ai, back, nj, ci, carry, prev_ref=xs_ref)

        def second_pass(t, carry):
            outputs(2 * t, stage0)
            carry = second(2 * t + 1, stage1, carry)
            outputs(2 * t + 1, stage1)
            return second(2 * t + 2, stage0, carry)

        carry = lax.fori_loop(0, nc // 2 - 1, second_pass, second(0, stage0, init + (zero, zero)))
        outputs(nc - 2, stage0)
        gr, gi, acc_r, acc_i = second(nc - 1, stage1, carry)
        outputs(nc - 1, stage1)

        seg = lax.broadcasted_iota(jnp.int32, shape, 0)
        jb = jnp.where(rev, nj - 1, 0)
        erow = pl.ds(pl.multiple_of((nj - 1 - jb) * N_SEG, N_SEG), N_SEG)

        def before(t):
            up = jnp.where(seg >= 1, pltpu.roll(t, 1, 0), 0.0)
            down = jnp.where(seg <= N_SEG - 2, pltpu.roll(t, N_SEG - 1, 0), 0.0)
            return jnp.where(rev, down, up)

        init_r, init_i = before(xs_ref[erow, _RE]), before(xs_ref[erow, _IM])
        acc_r = acc_r + gr * init_r + gi * init_i
        acc_i = acc_i + gi * init_r - gr * init_i
        dlr_ref[...] = jnp.sum(acc_r, axis=0, keepdims=True)
        dli_ref[...] = jnp.sum(acc_i, axis=0, keepdims=True)

    act, bmat, cmat, lam = _ssm_specs(l)
    return pl.pallas_call(
        body, grid=(SSM_CB, 2), in_specs=[act, act, ANY, bmat, cmat, lam, lam],
        out_specs=[act, bmat, bmat, lam, lam],
        out_shape=[jax.ShapeDtypeStruct((l, SSM_WIDTH), F32),
                   jax.ShapeDtypeStruct(bcat.shape, F32), jax.ShapeDtypeStruct(bcat.shape, F32),
                   jax.ShapeDtypeStruct(lam_re.shape, F32), jax.ShapeDtypeStruct(lam_im.shape, F32)],
        scratch_shapes=[pltpu.VMEM((l, 2 * SSM_ST), F32), pltpu.VMEM((l, 2 * SSM_ST), F32),
                        pltpu.VMEM((SSM_RC, 2 * SSM_ST), F32), pltpu.VMEM((SSM_RC, 2 * SSM_ST), F32),
                        pltpu.SemaphoreType.DMA],
        name="ssm_bwd", compiler_params=_params(("parallel", "arbitrary"), vmem_mb=58),
    )(u_seg, dy_seg, states, bcat.astype(BF16), ccat.astype(BF16), lam_re, lam_im)


def _glu_fwd(y_ssm, u, d_skip, w_glu):
    l, w = u.shape

    def body(y_ref, u_ref, d_ref, w_ref, pre_ref, s_ref, ys_ref):
        pre = y_ref[...] + d_ref[...] * u_ref[...]
        z = _gelu(pre)
        s = _dg(z.astype(BF16), w_ref[...], NN)
        pre_ref[...] = pre
        s_ref[...] = s
        ys_ref[...] = z * _sigmoid(s)

    row = pl.BlockSpec((TM_EW, w), lambda i: (i, 0))
    out = jax.ShapeDtypeStruct((l, w), F32)
    return pl.pallas_call(
        body, grid=(l // TM_EW,),
        in_specs=[row, row, pl.BlockSpec((1, w), lambda i: (0, 0)), pl.BlockSpec((w, w), lambda i: (0, 0))],
        out_specs=[row, row, row], out_shape=[out, out, out], name="glu_fwd",
        compiler_params=_params(("parallel",)),
    )(y_ssm, u, d_skip, w_glu)


def _glu_bwd(pre, s, dys, u, d_skip, w_glu):
    l, w = u.shape

    def body(pre_ref, s_ref, dys_ref, u_ref, d_ref, w_ref, dpre_ref, z_ref, ds_ref, dd_ref):
        pre, dys = pre_ref[...], dys_ref[...]
        z = _gelu(pre)
        sig = _sigmoid(s_ref[...])
        ds = (dys * z * sig * (1.0 - sig)).astype(BF16)
        dz = dys * sig + _dg(ds, w_ref[...], NT)
        dpre = dz * _gelu_grad(pre)
        dpre_ref[...] = dpre
        z_ref[...] = z.astype(BF16)
        ds_ref[...] = ds

        @pl.when(pl.program_id(0) == 0)
        def _():
            dd_ref[...] = jnp.zeros_like(dd_ref)

        dd_ref[...] += jnp.sum(dpre * u_ref[...], axis=0, keepdims=True)

    row = pl.BlockSpec((TM_EW, w), lambda i: (i, 0))
    vec = pl.BlockSpec((1, w), lambda i: (0, 0))
    return pl.pallas_call(
        body, grid=(l // TM_EW,),
        in_specs=[row, row, row, row, vec, pl.BlockSpec((w, w), lambda i: (0, 0))],
        out_specs=[row, row, row, vec],
        out_shape=[jax.ShapeDtypeStruct((l, w), F32), jax.ShapeDtypeStruct((l, w), BF16),
                   jax.ShapeDtypeStruct((l, w), BF16), jax.ShapeDtypeStruct((1, w), F32)],
        name="glu_bwd", compiler_params=_params(("arbitrary",)),
    )(pre, s, dys, u, d_skip, w_glu)


TM_CV = 512
TC_CV = 256
TM_CF = 256
TC_CF = D_FF // 2
HALO = SUBLANES


def _conv_specs(l, col0, tm=TM_CV, tc=TC_CV):
    per = tm // HALO
    nh = l // HALO
    off = col0 // tc
    return [
        pl.BlockSpec((HALO, tc), lambda j, i: (jnp.maximum(i * per - 1, 0), j + off)),
        pl.BlockSpec((tm, tc), lambda j, i: (i, j + off)),
        pl.BlockSpec((HALO, tc), lambda j, i: (jnp.minimum((i + 1) * per, nh - 1), j + off)),
    ]


def _ext(prev_ref, mid_ref, next_ref, first, last):
    p = jnp.where(first, 0.0, prev_ref[...])
    n = jnp.where(last, 0.0, next_ref[...])
    return jnp.concatenate([p, mid_ref[...], n], axis=0)


def _shift_dn(t):
    return pltpu.roll(t, 1, 0)


def _shift_up(t):
    return pltpu.roll(t, t.shape[0] - 1, 0)


def _conv3(e, w_ref, b_ref):
    return w_ref[0:1, :] * _shift_dn(e) + w_ref[1:2, :] * e + w_ref[2:3, :] * _shift_up(e) + b_ref[...]


def _convffn_fwd(up_pre, conv_w, conv_b):
    l = up_pre.shape[0]
    tm, tc = TM_CF, TC_CF
    ni = l // tm
    wspec = lambda off: pl.BlockSpec((3, tc), lambda j, i: (0, j + off))
    bspec = lambda off: pl.BlockSpec((1, tc), lambda j, i: (0, j + off))
    voff = D_FF // tc

    def body(gp, gm, gn, vp, vm, vn, wg, bg, wv, bv, o_ref):
        i = pl.program_id(1)
        first, last = i == 0, i == ni - 1
        gate = _conv3(_ext(gp, gm, gn, first, last), wg, bg)[HALO:HALO + tm]
        val = _conv3(_ext(vp, vm, vn, first, last), wv, bv)[HALO:HALO + tm]
        o_ref[...] = (gate * _sigmoid(gate) * val).astype(BF16)

    return pl.pallas_call(
        body, grid=(D_FF // tc, ni),
        in_specs=_conv_specs(l, 0, tm, tc) + _conv_specs(l, D_FF, tm, tc)
        + [wspec(0), bspec(0), wspec(voff), bspec(voff)],
        out_specs=pl.BlockSpec((tm, tc), lambda j, i: (i, j)),
        out_shape=jax.ShapeDtypeStruct((l, D_FF), BF16), name="convffn_fwd",
        compiler_params=_params(("parallel", "parallel")),
    )(up_pre, up_pre, up_pre, up_pre, up_pre, up_pre, conv_w, conv_b, conv_w, conv_b)


HALO_B = 2 * SUBLANES


def _convffn_bwd(up_pre, dx2b, w_down, conv_w, conv_b):
    l = up_pre.shape[0]
    ni = l // TM_CV
    d = dx2b.shape[1]
    wspec = lambda off: pl.BlockSpec((3, TC_CV), lambda i, j: (0, j + off))
    bspec = lambda off: pl.BlockSpec((1, TC_CV), lambda i, j: (0, j + off))
    voff = D_FF // TC_CV
    swap = lambda spec: pl.BlockSpec(spec.block_shape, lambda i, j, f=spec.index_map: f(j, i))
    per, nh = TM_CV // HALO_B, l // HALO_B
    dx_specs = [pl.BlockSpec((HALO_B, d), lambda i, j: (jnp.maximum(i * per - 1, 0), 0)),
                pl.BlockSpec((TM_CV, d), lambda i, j: (i, 0)),
                pl.BlockSpec((HALO_B, d), lambda i, j: (jnp.minimum((i + 1) * per, nh - 1), 0))]

    def body(gp, gm, gn, vp, vm, vn, xp, xm, xn, wd, wg, bg, wv, bv, dup_ref, pg_ref, pv_ref):
        i = pl.program_id(0)
        first, last = i == 0, i == ni - 1
        ge, ve = _ext(gp, gm, gn, first, last), _ext(vp, vm, vn, first, last)
        zero = jnp.zeros((HALO_B, d), BF16)
        dx = jnp.concatenate([jnp.where(first, zero, xp[...]), xm[...], jnp.where(last, zero, xn[...])], axis=0)
        de = _dg(dx, wd[...], NT)[HALO_B - HALO:HALO_B + TM_CV + HALO]
        taps = [(_shift_dn(e), e, _shift_up(e)) for e in (ge, ve)]
        conv = lambda t, w_ref, b_ref: w_ref[0:1, :] * t[0] + w_ref[1:2, :] * t[1] + w_ref[2:3, :] * t[2] + b_ref[...]
        gate, val = conv(taps[0], wg, bg), conv(taps[1], wv, bv)
        sig = _sigmoid(gate)
        silu = gate * sig
        dgate = de * val * (sig + silu * (1.0 - sig))
        dval = de * silu
        mid = slice(HALO, HALO + TM_CV)
        rid = lax.broadcasted_iota(jnp.int32, (SUBLANES, TC_CV), 0)
        for half, (dup, tap, w_ref, p_ref) in enumerate(((dgate, taps[0], wg, pg_ref), (dval, taps[1], wv, pv_ref))):
            dpre = w_ref[0:1, :] * _shift_up(dup) + w_ref[1:2, :] * dup + w_ref[2:3, :] * _shift_dn(dup)
            dup_ref[half] = dpre[mid].astype(BF16)
            dm_ = dup[mid]
            sums = [jnp.sum(dm_ * t[mid], axis=0, keepdims=True) for t in tap]
            sums.append(jnp.sum(dm_, axis=0, keepdims=True))
            acc = jnp.zeros((SUBLANES, TC_CV), F32)
            for k, sk in enumerate(sums):
                acc = jnp.where(rid == k, sk, acc)
            p_ref[...] = acc

    par = pl.BlockSpec((None, SUBLANES, TC_CV), lambda i, j: (i, 0, j))
    dup, pg, pv = pl.pallas_call(
        body, grid=(ni, D_FF // TC_CV),
        in_specs=[swap(s) for s in _conv_specs(l, 0) + _conv_specs(l, D_FF)] + dx_specs
        + [pl.BlockSpec((TC_CV, d), lambda i, j: (j, 0)), wspec(0), bspec(0), wspec(voff), bspec(voff)],
        out_specs=[pl.BlockSpec((2, TM_CV, TC_CV), lambda i, j: (0, i, j)), par, par],
        out_shape=[jax.ShapeDtypeStruct((2, l, D_FF), BF16),
                   jax.ShapeDtypeStruct((ni, SUBLANES, D_FF), F32), jax.ShapeDtypeStruct((ni, SUBLANES, D_FF), F32)],
        name="convffn_bwd", compiler_params=_params(("parallel", "parallel")),
    )(up_pre, up_pre, up_pre, up_pre, up_pre, up_pre, dx2b, dx2b, dx2b, w_down, conv_w, conv_b, conv_w, conv_b)
    return dup, jnp.concatenate([jnp.sum(pg, axis=0), jnp.sum(pv, axis=0)], axis=1)


def _local_step(x, target, wb, sp, mixer_weights=None, late_weights=None, ffn_grads_ready=None,
                ffn_grads_next=None):
    l = x.shape[0]
    tabs = _rope_tables(l)
    disc = _ssm_disc(sp["a_re"], sp["a_im"], sp["log_step"], sp["b_re"], sp["b_im"])
    bcat, ccat, lam_re, lam_im = _ssm_pack(*disc, sp["c_re"], sp["c_im"])
    d_skip = sp["d_skip"].reshape(1, SSM_WIDTH)

    h, qkv, u = _rms_mm_rope(x, sp["norm_mix_g"], wb["w_in"], tabs, "mm_in")
    attn, lse = _attn_fwd(qkv, sp["sink"])
    u_seg = _to_segments(u).astype(BF16)
    y_seg, states = _ssm_fwd(u_seg, bcat, ccat, lam_re, lam_im)
    y_ssm = _from_segments(y_seg)
    if mixer_weights is not None:
        wb = dict(wb, **mixer_weights(attn))
    pre, s_glu, ys = _glu_fwd(y_ssm, u, d_skip, wb["w_glu"])
    mixed, x1, h2 = _mix_mm_res_rms(attn, ys, sp["norm_attn_g"], sp["norm_ssm_g"], wb["w_out"], x,
                                    sp["norm_ffn_g"], "mm_out")
    if late_weights is not None:
        wb = dict(wb, **late_weights(h2))
    up_pre = _mm_nn_cols(h2, wb["w_up"], min(l, 1024), "mm_up")
    conv_w = wb["conv_w"]
    act = _convffn_fwd(up_pre, conv_w, sp["conv_b"])
    loss, dx2, dx2b, d_final_g = _mm_res_loss(act, wb["w_down"], x1, sp["norm_final_g"].reshape(1, D_MODEL), target)

    g = {"norm_final_g": d_final_g.reshape(D_MODEL)}
    g["w_down"] = _mm_tn(act, dx2b, D_FF // 2, 512, "mm_down_dw")
    dup_pre, conv_par = _convffn_bwd(up_pre, dx2b, wb["w_down"], conv_w, sp["conv_b"])
    g["conv_w"], g["conv_b"] = conv_par[0:3], conv_par[3:4]
    g["w_up"] = _mm_tn_cols(h2, dup_pre, wb["w_up"].shape[0], 512, "mm_up_dw")
    zero = ffn_grads_ready(g["w_up"], g["w_down"]) if ffn_grads_ready is not None else 0.0
    dx1, dx1b, g["norm_ffn_g"] = _mm_cols_rms_bwd(dup_pre, wb["w_up"], x1, sp["norm_ffn_g"] + zero, dx2, "mm_up_dx")
    g["w_out"] = _mm_tn(mixed, dx1b, 1024, 1024, "mm_out_dw")
    zero = ffn_grads_next(g["w_out"]) if ffn_grads_next is not None else 0.0
    dattn, dys, g["norm_attn_g"], g["norm_ssm_g"] = _mm_mix_bwd(
        dx1b, wb["w_out"], attn, ys, sp["norm_attn_g"] + zero, sp["norm_ssm_g"], "mm_out_dx")
    dpre, zb, dsb, dd = _glu_bwd(pre, s_glu, dys, u, d_skip, wb["w_glu"])
    g["d_skip"] = dd.reshape(N_SSM_GROUPS, SSM_GROUP)
    g["w_glu"] = _mm_tn(zb, dsb, 512, 512, "mm_glu_dw")
    du_seg, dbcat, dccat, dlam_re, dlam_im = _ssm_bwd(u_seg, _to_segments(dpre).astype(BF16), states, bcat, ccat,
                                                      lam_re, lam_im)
    dlb_re, dlb_im, dbb_re, dbb_im, g["c_re"], g["c_im"] = _ssm_unpack(dbcat, dccat, dlam_re, dlam_im)
    _, disc_vjp = jax.vjp(_ssm_disc, sp["a_re"], sp["a_im"], sp["log_step"], sp["b_re"], sp["b_im"])
    g["a_re"], g["a_im"], g["log_step"], g["b_re"], g["b_im"] = disc_vjp((dlb_re, dlb_im, dbb_re, dbb_im))
    dq, dkv, g["sink"] = _attn_bwd(qkv, attn, dattn, lse, sp["sink"])
    dproj = _rope_bwd(dq, dkv, _from_segments(du_seg), dpre, d_skip, tabs)
    g["w_in"] = _mm_tn(dproj, h, IN_WIDTH // 5, D_MODEL, "mm_in_dw")
    grad_x, _, g["norm_mix_g"] = _mm_nn_rms_bwd(dproj, wb["w_in"], x, sp["norm_mix_g"], dx1, "mm_in_dx")
    return loss, grad_x, g


MESH = pl.DeviceIdType.MESH
ANY = pl.BlockSpec(memory_space=pl.ANY)


def _place():
    x, y, c = lax.axis_index("x"), lax.axis_index("y"), lax.axis_index("c")
    chips = [(1 - x, y), (x, 1 - y), (1 - x, 1 - y)]
    return x, y, c, chips


def _chip_index(px, py):
    return 2 * px + py


CHUNK_BYTES = 256 * 1024
MAX_CHUNKS = 16


def _row_chunks(rows, row_bytes, align):
    n = max(1, min(MAX_CHUNKS, (rows * row_bytes) // CHUNK_BYTES))
    per = -(-rows // n)
    per = -(-per // align) * align
    return [(r0, min(per, rows - r0)) for r0 in range(0, rows, per)]


def _align_of(dtype):
    return SUBLANES * 4 // jnp.dtype(dtype).itemsize


def _remote(src, dst, send_sem, recv_sem, to):
    return pltpu.make_async_remote_copy(src_ref=src, dst_ref=dst, send_sem=send_sem, recv_sem=recv_sem,
                                        device_id=to, device_id_type=MESH)


CAST_ROWS = 64


def _gather_weights(shards, dtypes):
    nw = len(shards)

    def body(*refs):
        w_refs, o_refs = refs[:nw], refs[nw:2 * nw]
        send_sems, recv_sems, in_sems, out_sems = refs[2 * nw:2 * nw + 4]
        raw, cast = refs[2 * nw + 4:3 * nw + 4], refs[3 * nw + 4:]
        x, y, c, chips = _place()
        mine = _chip_index(x, y)
        sibling = (x, y, 1 - c)

        def rows_of(ref, chip, r0, nr):
            return ref.at[chip, pl.ds(r0, nr), :]

        def copy(wi, k, src, dst, to):
            return _remote(src, dst, send_sems.at[wi, k], recv_sems.at[wi, k], to)

        geo = []
        for wi in range(nw):
            rows, cols = w_refs[wi].shape
            row_bytes = cols * jnp.dtype(dtypes[wi]).itemsize
            geo.append((rows // 2, _row_chunks(rows // 2, row_bytes, _align_of(dtypes[wi]))))

        stage_in = [pltpu.make_async_copy(w_refs[wi], raw[wi], in_sems.at[wi]) for wi in range(nw)]
        for cp in stage_in:
            cp.start()
        staged = [raw[wi] if dtypes[wi] == w_refs[wi].dtype else cast[wi] for wi in range(nw)]
        stage_out = []
        for wi in range(nw):
            stage_in[wi].wait()
            if staged[wi] is not raw[wi]:
                def cast_rows(i, _, wi=wi):
                    rows = pl.ds(pl.multiple_of(i * CAST_ROWS, CAST_ROWS), CAST_ROWS)
                    cast[wi][rows, :] = raw[wi][rows, :].astype(dtypes[wi])
                    return 0

                lax.fori_loop(0, w_refs[wi].shape[0] // CAST_ROWS, cast_rows, 0)
            cp = pltpu.make_async_copy(staged[wi], o_refs[wi].at[mine], out_sems.at[wi])
            cp.start()
            stage_out.append(cp)

        for wi in range(nw):
            hr, half_chunks = geo[wi]
            for j, chip in enumerate(chips):
                for r0, nr in half_chunks:
                    copy(wi, j, staged[wi].at[pl.ds(c * hr + r0, nr), :],
                         rows_of(o_refs[wi], mine, c * hr + r0, nr), (*chip, c)).start()
        for wi in range(nw):
            hr, half_chunks = geo[wi]
            for j, chip in enumerate(chips):
                got = rows_of(o_refs[wi], _chip_index(*chip), c * hr, hr)
                copy(wi, j, got, got, (*chip, c)).wait_recv()
                for r0, nr in half_chunks:
                    piece = rows_of(o_refs[wi], _chip_index(*chip), c * hr + r0, nr)
                    copy(wi, 3 + j, piece, piece, sibling).start()
        for wi in range(nw):
            hr = geo[wi][0]
            for j, chip in enumerate(chips):
                got = rows_of(o_refs[wi], _chip_index(*chip), (1 - c) * hr, hr)
                copy(wi, 3 + j, got, got, sibling).wait_recv()
        for wi in range(nw):
            hr = geo[wi][0]
            sent = rows_of(o_refs[wi], mine, c * hr, hr)
            for k in range(6):
                copy(wi, k, sent, sent, sibling).wait_send()
            stage_out[wi].wait()

    return pl.pallas_call(
        body, in_specs=[ANY] * nw, out_specs=[ANY] * nw,
        out_shape=[jax.ShapeDtypeStruct((4, *s.shape), t) for s, t in zip(shards, dtypes)],
        scratch_shapes=[pltpu.SemaphoreType.DMA((nw, 6)), pltpu.SemaphoreType.DMA((nw, 6)),
                        pltpu.SemaphoreType.DMA((nw,)), pltpu.SemaphoreType.DMA((nw,))]
        + [pltpu.VMEM(s.shape, s.dtype) for s in shards] + [pltpu.VMEM(s.shape, t) for s, t in zip(shards, dtypes)],
        name="gather_weights", compiler_params=_params(vmem_mb=40),
    )(*shards)


HBM = pl.BlockSpec(memory_space=pltpu.HBM)
SEM = pl.BlockSpec(memory_space=pltpu.SEMAPHORE)
EFFECT = pltpu.SideEffectType.DATAFLOW_SIDE_EFFECTING


def _cast_place(w, place, dtype, after, name):
    rows, cols = w.shape
    tr = _row_tile(rows, cols, _align_of(dtype))

    def body(p_ref, w_ref, after_ref, o_ref):
        del p_ref, after_ref
        o_ref[...] = w_ref[...].astype(dtype)

    grid_spec = pltpu.PrefetchScalarGridSpec(
        num_scalar_prefetch=1, grid=(rows // tr,),
        in_specs=[pl.BlockSpec((tr, cols), lambda i, p: (i, 0)), ANY],
        out_specs=pl.BlockSpec((None, tr, cols), lambda i, p: (p[1], i, 0)))
    return pl.pallas_call(body, grid_spec=grid_spec, out_shape=jax.ShapeDtypeStruct((4, rows, cols), dtype),
                          name=name, compiler_params=_params(("parallel",)))(place, w, after)


def _split_start(name, arrays, n_pairs, issue):
    n = len(arrays)

    def body(*refs):
        issue(refs[:n], refs[n:n + n_pairs], refs[n + n_pairs:n + 2 * n_pairs])
        token = refs[2 * n + 2 * n_pairs]
        token[...] = jnp.zeros_like(token)

    dma = pltpu.SemaphoreType.DMA(())
    outs = pl.pallas_call(
        body, name=name,
        out_shape=[dma] * (2 * n_pairs) + [pltpu.HBM(t.shape, t.dtype) for t in arrays]
        + [jax.ShapeDtypeStruct((SUBLANES, LANES), F32)],
        in_specs=[HBM] * n, out_specs=[SEM] * (2 * n_pairs) + [HBM] * n + [pl.BlockSpec(memory_space=pltpu.VMEM)],
        input_output_aliases={a: 2 * n_pairs + a for a in range(n)},
        compiler_params=pltpu.CompilerParams(has_side_effects=EFFECT),
    )(*[pltpu.with_memory_space_constraint(t, pltpu.HBM) for t in arrays])
    return outs[:n_pairs], outs[n_pairs:2 * n_pairs], outs[2 * n_pairs:2 * n_pairs + n], outs[-1]


def _split_wait(name, send_sems, recv_sems, flying, sizes, after):
    n, n_pairs = len(flying), len(send_sems)

    def body(*refs):
        x, y, c, _ = _place()
        for k, ref in enumerate(sizes(refs[:n])):
            cp = _remote(ref, ref, refs[n + k], refs[n + n_pairs + k], (x, y, 1 - c))
            cp.wait_send()
            cp.wait_recv()

    return pl.pallas_call(
        body, name=name, out_shape=[pltpu.HBM(t.shape, t.dtype) for t in flying],
        in_specs=[HBM] * n + [SEM] * (2 * n_pairs) + [ANY], out_specs=[HBM] * n,
        input_output_aliases={a: a for a in range(n)},
        compiler_params=pltpu.CompilerParams(has_side_effects=EFFECT),
    )(*flying, *send_sems, *recv_sems, after)


def _spread_start(lands, name):
    def issue(land_refs, send_sems, recv_sems):
        x, y, c, chips = _place()
        mine = _chip_index(x, y)
        for a, land in enumerate(land_refs):
            _, rows, cols = land.shape
            hr = rows // 2
            row_bytes = cols * jnp.dtype(land.dtype).itemsize
            for r0, nr in _row_chunks(hr, row_bytes, _align_of(land.dtype)):
                piece = land.at[mine, pl.ds(c * hr + r0, nr), :]
                for chip in chips:
                    for core in (0, 1):
                        _remote(piece, piece, send_sems[a], recv_sems[a], (*chip, core)).start()

    return _split_start(name, lands, len(lands), issue)


def _spread_wait(send_sems, recv_sems, flying, after, name):
    return _split_wait(name, send_sems, recv_sems, flying, lambda refs: [r.at[pl.ds(0, 3)] for r in refs], after)


def _pair_start(grads):
    n = len(grads)
    zones = [lax.empty((4, g.shape[1] // 2, g.shape[2]), F32) for g in grads]

    def issue(refs, send_sems, recv_sems):
        x, y, c, _ = _place()
        for a in range(n):
            g_ref, z_ref = refs[a], refs[n + a]
            _, rows, cols = g_ref.shape
            hr = rows // 2
            for k in range(4):
                for r0, nr in _row_chunks(hr, cols * 4, SUBLANES):
                    _remote(g_ref.at[k, pl.ds((1 - c) * hr + r0, nr), :], z_ref.at[k, pl.ds(r0, nr), :],
                            send_sems[a], recv_sems[a], (x, y, 1 - c)).start()

    return _split_start("pair_start", list(grads) + zones, n, issue)


def _pair_wait(send_sems, recv_sems, flying, after):
    n = len(flying) // 2
    out = _split_wait("pair_wait", send_sems, recv_sems, flying, lambda refs: list(refs[n:]), after)
    return out[:n], out[n:]


def _chip_start(sums):
    n = len(sums)
    zones = [lax.empty((3, *s.shape[1:]), s.dtype) for s in sums]

    def issue(refs, send_sems, recv_sems):
        x, y, c, chips = _place()
        for a in range(n):
            s_ref, z_ref = refs[a], refs[n + a]
            _, rows, cols = s_ref.shape
            row_bytes = cols * jnp.dtype(s_ref.dtype).itemsize
            for r0, nr in _row_chunks(rows, row_bytes, _align_of(s_ref.dtype)):
                for j, chip in enumerate(chips):
                    _remote(s_ref.at[_chip_index(*chip), pl.ds(r0, nr), :], z_ref.at[j, pl.ds(r0, nr), :],
                            send_sems[a], recv_sems[a], (*chip, c)).start()

    return _split_start("chip_start", list(sums) + zones, n, issue)


def _chip_wait(send_sems, recv_sems, flying, after):
    n = len(flying) // 2
    return _split_wait("chip_wait", send_sems, recv_sems, flying, lambda refs: list(refs[n:]), after)[n:]


def _pair_exchange(grads):
    na = len(grads)

    def body(*refs):
        g_refs, o_refs = refs[:na], refs[na:2 * na]
        send_sems, recv_sems = refs[2 * na:]
        x, y, c, _ = _place()
        sibling = (x, y, 1 - c)
        for ai in range(na):
            _, rows, cols = g_refs[ai].shape
            hr = rows // 2
            for k in range(4):
                for r0, nr in _row_chunks(hr, cols * 4, SUBLANES):
                    _remote(g_refs[ai].at[k, pl.ds((1 - c) * hr + r0, nr), :], o_refs[ai].at[k, pl.ds(r0, nr), :],
                            send_sems.at[ai], recv_sems.at[ai], sibling).start()
        for ai in range(na):
            _remote(o_refs[ai], o_refs[ai], send_sems.at[ai], recv_sems.at[ai], sibling).wait()

    return pl.pallas_call(
        body, in_specs=[ANY] * na, out_specs=[ANY] * na,
        out_shape=[jax.ShapeDtypeStruct((4, g.shape[1] // 2, g.shape[2]), F32) for g in grads],
        scratch_shapes=[pltpu.SemaphoreType.DMA((na,)), pltpu.SemaphoreType.DMA((na,))],
        name="pair_exchange",
    )(*grads)


def _row_tile(rows, cols, align):
    best = align
    for cand in range(align, rows + 1, align):
        if rows % cand == 0 and cand * cols <= 256 * 1024:
            best = cand
    return best


def _pair_sum(g, got, place, transit, name):
    _, rows, cols = g.shape
    hr = rows // 2
    tr = _row_tile(hr, cols, _align_of(transit))
    nt = hr // tr

    def body(p_ref, g_ref, r_ref, s_ref, own_ref):
        total = g_ref[...] + r_ref[...]
        s_ref[...] = total.astype(transit)

        @pl.when(pl.program_id(1) == p_ref[1])
        def _():
            own_ref[...] = total

    grid_spec = pltpu.PrefetchScalarGridSpec(
        num_scalar_prefetch=1, grid=(nt, 4),
        in_specs=[pl.BlockSpec((None, tr, cols), lambda i, k, p: (k, p[0] * nt + i, 0)),
                  pl.BlockSpec((None, tr, cols), lambda i, k, p: (k, i, 0))],
        out_specs=[pl.BlockSpec((None, tr, cols), lambda i, k, p: (k, i, 0)),
                   pl.BlockSpec((tr, cols), lambda i, k, p: (i, 0))])
    return pl.pallas_call(
        body, grid_spec=grid_spec,
        out_shape=[jax.ShapeDtypeStruct((4, hr, cols), transit), jax.ShapeDtypeStruct((hr, cols), F32)],
        name=name, compiler_params=_params(("parallel", "arbitrary")),
    )(place, g, got)


def _chip_exchange(sums):
    na = len(sums)

    def body(*refs):
        s_refs, o_refs = refs[:na], refs[na:2 * na]
        send_sems, recv_sems = refs[2 * na:]
        x, y, c, chips = _place()
        for ai in range(na):
            _, rows, cols = s_refs[ai].shape
            row_bytes = cols * jnp.dtype(s_refs[ai].dtype).itemsize
            for r0, nr in _row_chunks(rows, row_bytes, _align_of(s_refs[ai].dtype)):
                for j, chip in enumerate(chips):
                    _remote(s_refs[ai].at[_chip_index(*chip), pl.ds(r0, nr), :], o_refs[ai].at[j, pl.ds(r0, nr), :],
                            send_sems.at[ai, j], recv_sems.at[ai, j], (*chip, c)).start()
        for ai in range(na):
            for j, chip in enumerate(chips):
                _remote(o_refs[ai].at[j], o_refs[ai].at[j], send_sems.at[ai, j], recv_sems.at[ai, j],
                        (*chip, c)).wait()

    return pl.pallas_call(
        body, in_specs=[ANY] * na, out_specs=[ANY] * na,
        out_shape=[jax.ShapeDtypeStruct((3, *s.shape[1:]), s.dtype) for s in sums],
        scratch_shapes=[pltpu.SemaphoreType.DMA((na, 3)), pltpu.SemaphoreType.DMA((na, 3))],
        name="chip_exchange",
    )(*sums)


def _chip_sum(own, landed, name):
    hr, cols = own.shape
    tr = _row_tile(hr, cols, _align_of(landed.dtype))

    def body(o_ref, l_ref, f_ref):
        acc = o_ref[...]
        for j in range(3):
            acc = acc + l_ref[j].astype(F32)
        f_ref[...] = acc

    return pl.pallas_call(
        body, grid=(hr // tr,),
        in_specs=[pl.BlockSpec((tr, cols), lambda i: (i, 0)), pl.BlockSpec((3, tr, cols), lambda i: (0, i, 0))],
        out_specs=pl.BlockSpec((tr, cols), lambda i: (i, 0)),
        out_shape=jax.ShapeDtypeStruct((hr, cols), F32), name=name,
        compiler_params=_params(("parallel",)),
    )(own, landed)


def _final_exchange(halves, small):
    nh = len(halves)

    def body(*refs):
        h_refs, s_ref = refs[:nh], refs[nh]
        o_refs, so_ref = refs[nh + 1:2 * nh + 1], refs[2 * nh + 1]
        send_sems, recv_sems, local_sem, ssend_sems, srecv_sems = refs[2 * nh + 2:]
        x, y, c, _ = _place()
        me = 4 * x + 2 * y + c
        sibling = (x, y, 1 - c)
        for hi in range(nh):
            hr, cols = h_refs[hi].shape
            for r0, nr in _row_chunks(hr, cols * 4, SUBLANES):
                _remote(h_refs[hi].at[pl.ds(r0, nr), :], o_refs[hi].at[pl.ds(r0, nr), :],
                        send_sems.at[hi], recv_sems.at[hi], sibling).start()
        small_cps = [pltpu.make_async_copy(s_ref, so_ref.at[me], local_sem)]
        for r in range(1, 8):
            fx, fy, fc = (r >> 2) & 1, (r >> 1) & 1, r & 1
            peer = (1 - x if fx else x, 1 - y if fy else y, 1 - c if fc else c)
            small_cps.append(_remote(s_ref, so_ref.at[me], ssend_sems.at[r - 1], srecv_sems.at[r - 1], peer))
        for cp in small_cps:
            cp.start()
        for hi in range(nh):
            _remote(h_refs[hi], o_refs[hi], send_sems.at[hi], recv_sems.at[hi], sibling).wait()
        for cp in small_cps:
            cp.wait()

    return pl.pallas_call(
        body, in_specs=[ANY] * (nh + 1), out_specs=[ANY] * (nh + 1),
        out_shape=[jax.ShapeDtypeStruct(h.shape, F32) for h in halves]
        + [jax.ShapeDtypeStruct((8, *small.shape), F32)],
        scratch_shapes=[pltpu.SemaphoreType.DMA((nh,)), pltpu.SemaphoreType.DMA((nh,)),
                        pltpu.SemaphoreType.DMA, pltpu.SemaphoreType.DMA((7,)), pltpu.SemaphoreType.DMA((7,))],
        name="final_exchange",
    )(*halves, small)


def _adamw_halves(w, own, other, m, v, place, name):
    r, c = w.shape
    hr = r // 2
    tr = _row_tile(hr, c, SUBLANES)
    nt = hr // tr
    c1 = 1.0 - ADAM_B1 ** ADAM_STEP
    c2 = 1.0 - ADAM_B2 ** ADAM_STEP

    def body(p_ref, w_ref, own_ref, other_ref, m_ref, v_ref, g_ref, d_ref, nm_ref, nv_ref):
        mine = pl.program_id(0) // nt == p_ref[0]
        gv = jnp.where(mine, own_ref[...], other_ref[...])
        nm = ADAM_B1 * m_ref[...] + (1.0 - ADAM_B1) * gv
        nv = ADAM_B2 * v_ref[...] + (1.0 - ADAM_B2) * (gv * gv)
        g_ref[...] = gv
        d_ref[...] = -ADAM_LR * ((nm / c1) / (jnp.sqrt(nv / c2) + ADAM_EPS) + ADAM_WD * w_ref[...])
        nm_ref[...] = nm
        nv_ref[...] = nv

    full = pl.BlockSpec((tr, c), lambda i, p: (i, 0))
    half = pl.BlockSpec((tr, c), lambda i, p: (i % nt, 0))
    out = jax.ShapeDtypeStruct((r, c), F32)
    grid_spec = pltpu.PrefetchScalarGridSpec(num_scalar_prefetch=1, grid=(2 * nt,),
                                             in_specs=[full, half, half, full, full], out_specs=[full] * 4)
    return pl.pallas_call(body, grid_spec=grid_spec, out_shape=[out] * 4, name=name,
                          compiler_params=_params(("parallel",)))(place, w, own, other, m, v)


def _adamw_many(ws, gs, ms, vs, name):
    n = len(ws)
    c1 = 1.0 - ADAM_B1 ** ADAM_STEP
    c2 = 1.0 - ADAM_B2 ** ADAM_STEP

    def body(*refs):
        w_refs, g_refs, m_refs, v_refs = (refs[k * n:(k + 1) * n] for k in range(4))
        d_refs, nm_refs, nv_refs = (refs[(4 + k) * n:(5 + k) * n] for k in range(3))
        for i in range(n):
            gv = g_refs[i][...]
            nm = ADAM_B1 * m_refs[i][...] + (1.0 - ADAM_B1) * gv
            nv = ADAM_B2 * v_refs[i][...] + (1.0 - ADAM_B2) * (gv * gv)
            d_refs[i][...] = -ADAM_LR * ((nm / c1) / (jnp.sqrt(nv / c2) + ADAM_EPS) + ADAM_WD * w_refs[i][...])
            nm_refs[i][...] = nm
            nv_refs[i][...] = nv

    vmem = pl.BlockSpec(memory_space=pltpu.VMEM)
    shapes = [jax.ShapeDtypeStruct(t.shape, F32) for t in ws]
    outs = pl.pallas_call(body, in_specs=[vmem] * (4 * n), out_specs=[vmem] * (3 * n), out_shape=shapes * 3,
                          name=name, compiler_params=_params(vmem_mb=56))(*ws, *gs, *ms, *vs)
    return outs[:n], outs[n:2 * n], outs[2 * n:]


BIG = ("w_in", "w_glu", "w_out", "w_up", "w_down")
WEIGHTS = ("norm_mix_g", "w_in", "a_re", "a_im", "log_step", "b_re", "b_im", "c_re", "c_im", "d_skip", "w_glu",
           "sink", "norm_attn_g", "norm_ssm_g", "w_out", "norm_ffn_g", "w_up", "conv_w", "conv_b", "w_down",
           "norm_final_g")
SMALL = ("norm_mix_g", "a_re", "a_im", "log_step", "b_re", "b_im", "c_re", "c_im", "d_skip", "sink",
         "norm_attn_g", "norm_ssm_g", "norm_ffn_g", "conv_w", "conv_b", "norm_final_g")
SMALL_ROWS = 48
N_DEV = 8


def _tile_rows(size):
    return -(-size // (SUBLANES * D_MODEL)) * SUBLANES


def _by_owner(name, g):
    if name == "w_up":
        return g
    return g.reshape(4, g.shape[0] // 4, g.shape[1])


def _view(name, t):
    if name == "w_in":
        return jnp.swapaxes(t[0], 0, 1)
    if name in ("b_re", "b_im"):
        return jnp.swapaxes(t, -1, -2)
    return t


def _unview(name, t):
    if name == "w_in":
        return jnp.swapaxes(t, 0, 1)[None]
    if name in ("b_re", "b_im"):
        return jnp.swapaxes(t, -1, -2)
    return t


def kernel(x, norm_mix_g, w_in, a_re, a_im, log_step, b_re, b_im, c_re, c_im, d_skip, w_glu, sink, norm_attn_g, norm_ssm_g, w_out, norm_ffn_g, w_up, conv_w, conv_b, w_down, norm_final_g, loss_target, m_norm_mix_g, m_w_in, m_a_re, m_a_im, m_log_step, m_b_re, m_b_im, m_c_re, m_c_im, m_d_skip, m_w_glu, m_sink, m_norm_attn_g, m_norm_ssm_g, m_w_out, m_norm_ffn_g, m_w_up, m_conv_w, m_conv_b, m_w_down, m_norm_final_g, v_norm_mix_g, v_w_in, v_a_re, v_a_im, v_log_step, v_b_re, v_b_im, v_c_re, v_c_im, v_d_skip, v_w_glu, v_sink, v_norm_attn_g, v_norm_ssm_g, v_w_out, v_norm_ffn_g, v_w_up, v_conv_w, v_conv_b, v_w_down, v_norm_final_g):
    given = dict(locals())
    w = {n: given[n] for n in WEIGHTS}
    m = {n: given["m_" + n] for n in WEIGHTS}
    v = {n: given["v_" + n] for n in WEIGHTS}
    xy = 2 * lax.axis_index("x") + lax.axis_index("y")

    core = lax.axis_index("c")
    place = jnp.stack([core, xy]).astype(jnp.int32)

    conv_rows = jnp.pad(w["conv_w"][0], ((0, 2 * SUBLANES - 3), (0, 0)))
    rows = lambda t: t.reshape(4 * t.shape[1], t.shape[2])
    (w_in_all,) = _gather_weights([_view("w_in", w["w_in"])], [BF16])
    wb = {"w_in": rows(w_in_all)}
    early = ("w_in", "w_glu", "w_out")
    mixer = [_cast_place(w[n][0], place, BF16, w_in_all, "cast_" + n) for n in ("w_glu", "w_out")]
    mixer.append(_cast_place(conv_rows, place, F32, w_in_all, "cast_conv_w"))
    *mixer_flight, mixer_token = _spread_start(mixer, "spread_mixer_start")
    late = ("w_up", "w_down")
    *late_flight, token = _spread_start(
        [_cast_place(w[n][0], place, BF16, mixer_token, "cast_" + n) for n in late], "spread_ffn_start")

    def mixer_weights(after):
        w_glu4, w_out4, conv4 = _spread_wait(*mixer_flight, after, "spread_mixer_wait")
        return {"w_glu": rows(w_glu4), "w_out": rows(w_out4),
                "conv_w": conv4[:, :3].transpose(1, 0, 2).reshape(3, 2 * D_FF)}

    def late_weights(after):
        w_up4, w_down4 = _spread_wait(*late_flight, after, "spread_ffn_wait")
        return {"w_up": w_up4, "w_down": rows(w_down4)}

    sp = {n: w[n][0] for n in ("a_re", "a_im", "log_step", "b_re", "b_im", "c_re", "c_im", "d_skip",
                               "norm_mix_g", "norm_attn_g", "norm_ssm_g", "norm_ffn_g", "sink", "conv_b")}
    for n in ("norm_mix_g", "norm_attn_g", "norm_ssm_g", "norm_ffn_g", "sink", "conv_b"):
        sp[n] = sp[n].reshape(1, -1)
    sp["norm_mix_g"] = sp["norm_mix_g"] + token[:1, :1]
    sp["norm_final_g"] = w["norm_final_g"]
    flight = {}

    def ffn_grads_ready(dw_up, dw_down):
        *flight["pair"], token = _pair_start([dw_up, _by_owner("w_down", dw_down)])
        return token[:1, :1]

    def ffn_grads_next(after):
        mine, got = _pair_wait(*flight["pair"], after)
        sums, flight["own"] = zip(*[_pair_sum(a, b, place, BF16, "pair_sum_" + n) for n, a, b in zip(late, mine, got)])
        *flight["chip"], token = _chip_start(list(sums))
        return token[:1, :1]

    loss, grad_x, g = _local_step(x[0], loss_target[0], wb, sp, mixer_weights, late_weights, ffn_grads_ready,
                                  ffn_grads_next)

    def as_rows(t):
        rows = _tile_rows(t.size)
        return jnp.pad(t.reshape(-1), (0, rows * D_MODEL - t.size)).reshape(rows, D_MODEL)

    pieces = [as_rows(g[n]) for n in SMALL] + [as_rows(loss)]
    spare = N_DEV * SMALL_ROWS - sum(p.shape[0] for p in pieces)
    small = jnp.concatenate(pieces + [jnp.zeros((spare, D_MODEL), F32)]).reshape(4, 2 * SMALL_ROWS, D_MODEL)
    by_owner = [_by_owner(n, g[n]) for n in early] + [small]
    got = _pair_exchange(by_owner)
    transit = [BF16] * len(early) + [F32]
    chip_sums, own_sums = zip(*[_pair_sum(a, b, place, t, "pair_sum_" + n)
                                for n, a, b, t in zip(early + ("small",), by_owner, got, transit)])
    landed = _chip_exchange(list(chip_sums))
    halves = {n: _chip_sum(o, t, "chip_sum_" + n) for n, o, t in zip(early + ("small",), own_sums, landed)}
    late_landed = _chip_wait(*flight["chip"], grad_x)
    for n, o, t in zip(late, flight["own"], late_landed):
        halves[n] = _chip_sum(o, t, "chip_sum_" + n)
    *others, small_all = _final_exchange([halves[n] for n in BIG], halves["small"])
    small_all = small_all.reshape(N_DEV * SMALL_ROWS, D_MODEL)
    grads, row = {}, 0
    for n in SMALL:
        shape = (3, 4 * w[n].shape[-1]) if n == "conv_w" else w[n].shape[1:] if n != "norm_final_g" else w[n].shape
        size = math.prod(shape)
        grads[n] = small_all[row:row + _tile_rows(size)].reshape(-1)[:size].reshape(shape)
        row += _tile_rows(size)
    loss = small_all[row, 0]
    cw = w["conv_w"].shape[-1]
    grads["conv_w"] = lax.dynamic_slice_in_dim(grads["conv_w"], xy * cw, cw, axis=1)
    grads = {n: _view(n, grads[n].reshape(w[n].shape)) for n in SMALL}
    wv, mv, vv = ({n: _view(n, t[n]) for n in WEIGHTS} for t in (w, m, v))

    delta, new_m, new_v = {}, {}, {}
    for n, other in zip(BIG, others):
        two_d = lambda t: t.reshape(t.shape[-2:])
        grads[n], delta[n], new_m[n], new_v[n] = _adamw_halves(
            two_d(wv[n]), halves[n], other, two_d(mv[n]), two_d(vv[n]), place, "adamw_" + n)
    for group, name in ((("b_re", "b_im"), "adamw_b"), (tuple(n for n in SMALL if n not in ("b_re", "b_im")), "adamw_small")):
        row = lambda t: t.reshape(1, -1) if t.ndim == 1 else t
        d_, m_, v_ = _adamw_many(*[[row(t[n]) for n in group] for t in (wv, grads, mv, vv)], name)
        for n, dn, mn, vn in zip(group, d_, m_, v_):
            delta[n], new_m[n], new_v[n] = (t.reshape(wv[n].shape) for t in (dn, mn, vn))
    natural = lambda t: [_unview(n, t[n].reshape(wv[n].shape)) for n in WEIGHTS]
    return (loss, grad_x[None], *natural(grads), *natural(delta), *natural(new_m), *natural(new_v))
```

```python
import functools
import math

import jax
import jax.numpy as jnp
import numpy as np
from jax import lax
from jax.experimental import pallas as pl
from jax.experimental.pallas import tpu as pltpu

F32 = jnp.float32
BF16 = jnp.bfloat16

D_MODEL = 1024
N_Q_HEADS = 8
N_KV_HEADS = 2
HEAD_DIM = 64
ATTN_WIDTH = 512
KV_WIDTH = 128
QKV_WIDTH = ATTN_WIDTH + 2 * KV_WIDTH
WINDOW = 128
BLOCK = 128
ROPE_DIM = 16
ROPE_THETA = 500000.0
SCORE_SCALE = HEAD_DIM ** -0.5
SSM_WIDTH = 512
SSM_GROUP = 16
N_SSM_GROUPS = 32
SSM_STATE = 64
IN_WIDTH = 1280
D_FF = 2816
EPS = 1e-6
ADAM_LR = 0.001
ADAM_B1 = 0.9
ADAM_B2 = 0.999
ADAM_EPS = 1e-08
ADAM_WD = 0.01
ADAM_STEP = 10

VMEM_BYTES_V7X = 64 * 1024 * 1024
SUBLANES = 8
LANES = 128
SSM_CB = 4
SSM_CH = 128
SSM_ST = 512
N_SEG = SUBLANES

NN = (((1,), (0,)), ((), ()))
NT = (((1,), (1,)), ((), ()))
TN = (((0,), (0,)), ((), ()))


def _params(sem=None, vmem_mb=48):
    limit = vmem_mb * 1024 * 1024
    assert limit < VMEM_BYTES_V7X
    return pltpu.CompilerParams(dimension_semantics=sem, vmem_limit_bytes=limit)


def _dg(a, b, dims):
    return lax.dot_general(a, b, dims, preferred_element_type=F32)


def _sigmoid(x):
    return 1.0 / (1.0 + jnp.exp(-x))


_SQRT_HALF = 0.7071067811865476
_INV_SQRT_2PI = 0.3989422804014327


def _gelu(x):
    return 0.5 * x * (1.0 + lax.erf(x * _SQRT_HALF))


def _gelu_grad(x):
    return 0.5 * (1.0 + lax.erf(x * _SQRT_HALF)) + x * (_INV_SQRT_2PI * jnp.exp(-0.5 * x * x))


def _mm_tn(a, b, tm, tn, name):
    k, m = a.shape
    n = b.shape[1]

    def body(a_ref, b_ref, o_ref):
        o_ref[...] = _dg(a_ref[...], b_ref[...], TN)

    return pl.pallas_call(
        body, grid=(m // tm, n // tn),
        in_specs=[pl.BlockSpec((k, tm), lambda i, j: (0, i)), pl.BlockSpec((k, tn), lambda i, j: (0, j))],
        out_specs=pl.BlockSpec((tm, tn), lambda i, j: (i, j)),
        out_shape=jax.ShapeDtypeStruct((m, n), F32), name=name,
        compiler_params=_params(("parallel", "parallel")),
    )(a, b)


def _mm_nn_cols(a, b4, tm, name):
    m, k = a.shape
    s, _, n = b4.shape

    def body(a_ref, b_ref, o_ref):
        o_ref[...] = _dg(a_ref[...], b_ref[...], NN)

    return pl.pallas_call(
        body, grid=(m // tm, s),
        in_specs=[pl.BlockSpec((tm, k), lambda i, j: (i, 0)), pl.BlockSpec((None, k, n), lambda i, j: (j, 0, 0))],
        out_specs=pl.BlockSpec((tm, n), lambda i, j: (i, j)),
        out_shape=jax.ShapeDtypeStruct((m, s * n), F32), name=name,
        compiler_params=_params(("parallel", "parallel")),
    )(a, b4)


def _mm_tn_cols(a, b2, s, tm, name):
    k, m = a.shape
    h, _, wide = b2.shape
    per = s // h
    n = wide // per

    def body(a_ref, b_ref, o_ref):
        o_ref[...] = _dg(a_ref[...], b_ref[...], TN)

    return pl.pallas_call(
        body, grid=(s, m // tm),
        in_specs=[pl.BlockSpec((k, tm), lambda j, i: (0, i)),
                  pl.BlockSpec((None, k, n), lambda j, i: (j // per, 0, j % per))],
        out_specs=pl.BlockSpec((None, tm, n), lambda j, i: (j, i, 0)),
        out_shape=jax.ShapeDtypeStruct((s, m, n), F32), name=name,
        compiler_params=_params(("parallel", "parallel")),
    )(a, b2)


TM_EW = 256


def _rms_bwd_vals(xv, gv, dy):
    r = lax.rsqrt(jnp.mean(xv * xv, axis=-1, keepdims=True) + EPS)
    xh = xv * r
    dxh = dy * gv
    dx = r * (dxh - xh * jnp.mean(dxh * xh, axis=-1, keepdims=True))
    return dx, dy * xh


TM_FUSED = 512
TM_LOSS = 256


def _rms_vals(xv, gv):
    return xv * lax.rsqrt(jnp.mean(xv * xv, axis=-1, keepdims=True) + EPS) * gv


def _rope_blocks(src, dst, c, lo, hi):
    nq = ATTN_WIDTH // LANES
    for blk in range(nq + 1):
        t = src[:, blk * LANES:(blk + 1) * LANES]
        rot = t * c + pltpu.roll(t, LANES - 8, 1) * lo + pltpu.roll(t, 8, 1) * hi
        dst[:, blk * LANES:(blk + 1) * LANES] = (rot * SCORE_SCALE if blk < nq else rot).astype(BF16)
    dst[:, (nq + 1) * LANES:] = src[:, (nq + 1) * LANES:].astype(BF16)


def _rms_mm_rope(x, g, wt, tabs, name):
    l, d = x.shape
    n = wt.shape[0]

    def body(x_ref, g_ref, w_ref, c_ref, lo_ref, hi_ref, h_ref, qkv_ref, u_ref):
        h = _rms_vals(x_ref[...], g_ref[...]).astype(BF16)
        h_ref[...] = h
        out = _dg(h, w_ref[...], NT)
        _rope_blocks(out[:, :QKV_WIDTH], qkv_ref, c_ref[...], lo_ref[...], hi_ref[...])
        u_ref[...] = out[:, QKV_WIDTH:]

    row = lambda width: pl.BlockSpec((TM_FUSED, width), lambda i: (i, 0))
    return pl.pallas_call(
        body, grid=(l // TM_FUSED,),
        in_specs=[row(d), pl.BlockSpec((1, d), lambda i: (0, 0)), pl.BlockSpec((n, d), lambda i: (0, 0)),
                  row(LANES), row(LANES), row(LANES)],
        out_specs=[row(d), row(QKV_WIDTH), row(n - QKV_WIDTH)],
        out_shape=[jax.ShapeDtypeStruct((l, d), BF16), jax.ShapeDtypeStruct((l, QKV_WIDTH), BF16),
                   jax.ShapeDtypeStruct((l, n - QKV_WIDTH), F32)],
        name=name, compiler_params=_params(("parallel",)),
    )(x, g, wt, *tabs)


def _mix_mm_res_rms(attn, ys, g_attn, g_ssm, b, res, g, name):
    l, w = attn.shape
    d = b.shape[1]

    def body(a_ref, y_ref, ga_ref, gs_ref, b_ref, r_ref, g_ref, m_ref, x_ref, h_ref):
        m_ref[:, :w] = _rms_vals(a_ref[...], ga_ref[...]).astype(BF16)
        m_ref[:, w:] = _rms_vals(y_ref[...], gs_ref[...]).astype(BF16)
        xv = r_ref[...] + _dg(m_ref[...], b_ref[...], NN)
        x_ref[...] = xv
        h_ref[...] = _rms_vals(xv, g_ref[...]).astype(BF16)

    row = lambda width: pl.BlockSpec((TM_FUSED, width), lambda i: (i, 0))
    vec = lambda width: pl.BlockSpec((1, width), lambda i: (0, 0))
    return pl.pallas_call(
        body, grid=(l // TM_FUSED,),
        in_specs=[row(w), row(w), vec(w), vec(w), pl.BlockSpec((2 * w, d), lambda i: (0, 0)), row(d), vec(d)],
        out_specs=[row(2 * w), row(d), row(d)],
        out_shape=[jax.ShapeDtypeStruct((l, 2 * w), BF16), jax.ShapeDtypeStruct((l, d), F32),
                   jax.ShapeDtypeStruct((l, d), BF16)],
        name=name, compiler_params=_params(("parallel",)),
    )(attn, ys, g_attn, g_ssm, b, res, g)


def _mm_res_loss(a, b, res, g, target):
    l, k = a.shape
    d = b.shape[1]

    def body(a_ref, b_ref, r_ref, g_ref, t_ref, loss_ref, dx_ref, dxb_ref, dg_ref):
        xv = r_ref[...] + _dg(a_ref[...], b_ref[...], NN)
        gv = g_ref[...]
        r = lax.rsqrt(jnp.mean(xv * xv, axis=-1, keepdims=True) + EPS)
        xh = xv * r
        e = xh * gv - t_ref[...]
        part = jnp.sum(jnp.sum(e * e, axis=1, keepdims=True), axis=0, keepdims=True) * (0.5 / d)
        dy = e * (1.0 / d)
        dxh = dy * gv
        dx = r * (dxh - xh * jnp.mean(dxh * xh, axis=-1, keepdims=True))
        dx_ref[...] = dx
        dxb_ref[...] = dx.astype(BF16)

        @pl.when(pl.program_id(0) == 0)
        def _():
            dg_ref[...] = jnp.zeros_like(dg_ref)
            loss_ref[...] = jnp.zeros_like(loss_ref)

        dg_ref[...] += jnp.sum(dy * xh, axis=0, keepdims=True)
        loss_ref[...] += part

    row = lambda width: pl.BlockSpec((TM_LOSS, width), lambda i: (i, 0))
    vec = pl.BlockSpec((1, d), lambda i: (0, 0))
    return pl.pallas_call(
        body, grid=(l // TM_LOSS,),
        in_specs=[row(k), pl.BlockSpec((k, d), lambda i: (0, 0)), row(d), vec, row(d)],
        out_specs=[pl.BlockSpec((1, 1), lambda i: (0, 0)), row(d), row(d), vec],
        out_shape=[jax.ShapeDtypeStruct((1, 1), F32), jax.ShapeDtypeStruct((l, d), F32),
                   jax.ShapeDtypeStruct((l, d), BF16), jax.ShapeDtypeStruct((1, d), F32)],
        name="mm_down_loss", compiler_params=_params(("arbitrary",)),
    )(a, b, res, g, target)


def _mm_rms_bwd(a, b, a_spec, b_spec, matmul, x, g, res, name):
    l, d = x.shape

    def body(a_ref, b_ref, x_ref, g_ref, res_ref, dx_ref, dxb_ref, dg_ref):
        dx, dgr = _rms_bwd_vals(x_ref[...], g_ref[...], matmul(a_ref, b_ref))
        dx = dx + res_ref[...]
        dx_ref[...] = dx
        dxb_ref[...] = dx.astype(BF16)

        @pl.when(pl.program_id(0) == 0)
        def _():
            dg_ref[...] = jnp.zeros_like(dg_ref)

        dg_ref[...] += jnp.sum(dgr, axis=0, keepdims=True)

    row = pl.BlockSpec((TM_FUSED, d), lambda i: (i, 0))
    vec = pl.BlockSpec((1, d), lambda i: (0, 0))
    return pl.pallas_call(
        body, grid=(l // TM_FUSED,), in_specs=[a_spec, b_spec, row, vec, row], out_specs=[row, row, vec],
        out_shape=[jax.ShapeDtypeStruct((l, d), F32), jax.ShapeDtypeStruct((l, d), BF16),
                   jax.ShapeDtypeStruct((1, d), F32)],
        name=name, compiler_params=_params(("arbitrary",)),
    )(a, b, x, g, res)


def _mm_nn_rms_bwd(a, b, x, g, res, name):
    return _mm_rms_bwd(a, b, pl.BlockSpec((TM_FUSED, a.shape[1]), lambda i: (i, 0)),
                       pl.BlockSpec(b.shape, lambda i: (0, 0)),
                       lambda a_ref, b_ref: _dg(a_ref[...], b_ref[...], NN), x, g, res, name)


def _mm_cols_rms_bwd(a2, b4, x, g, res, name):
    h, _, wide = a2.shape
    s, _, n = b4.shape
    per = s // h

    def matmul(a_ref, b_ref):
        acc = None
        for j in range(s):
            part = _dg(a_ref[j // per, :, (j % per) * n:(j % per + 1) * n], b_ref[j], NT)
            acc = part if acc is None else acc + part
        return acc

    return _mm_rms_bwd(a2, b4, pl.BlockSpec((h, TM_FUSED, wide), lambda i: (0, i, 0)),
                       pl.BlockSpec(b4.shape, lambda i: (0, 0, 0), pipeline_mode=pl.Buffered(1)),
                       matmul, x, g, res, name)


def _mm_mix_bwd(dx, b, attn, ys, g_attn, g_ssm, name):
    l, w = attn.shape
    d = dx.shape[1]

    def body(dx_ref, b_ref, a_ref, y_ref, ga_ref, gs_ref, da_ref, dy_ref, dga_ref, dgs_ref):
        @pl.when(pl.program_id(0) == 0)
        def _():
            dga_ref[...] = jnp.zeros_like(dga_ref)
            dgs_ref[...] = jnp.zeros_like(dgs_ref)

        dm = _dg(dx_ref[...], b_ref[...], NT)
        for src, gr, off, dst, dgr in ((a_ref, ga_ref, 0, da_ref, dga_ref), (y_ref, gs_ref, w, dy_ref, dgs_ref)):
            dxv, dg_rows = _rms_bwd_vals(src[...], gr[...], dm[:, off:off + w])
            dst[...] = dxv
            dgr[...] += jnp.sum(dg_rows, axis=0, keepdims=True)

    row = lambda width: pl.BlockSpec((TM_FUSED, width), lambda i: (i, 0))
    vec = pl.BlockSpec((1, w), lambda i: (0, 0))
    return pl.pallas_call(
        body, grid=(l // TM_FUSED,),
        in_specs=[row(d), pl.BlockSpec((2 * w, d), lambda i: (0, 0)), row(w), row(w), vec, vec],
        out_specs=[row(w), row(w), vec, vec],
        out_shape=[jax.ShapeDtypeStruct((l, w), F32), jax.ShapeDtypeStruct((l, w), F32),
                   jax.ShapeDtypeStruct((1, w), F32), jax.ShapeDtypeStruct((1, w), F32)],
        name=name, compiler_params=_params(("arbitrary",)),
    )(dx, b, attn, ys, g_attn, g_ssm)


def _rope_tables(l):
    half = ROPE_DIM // 2
    f32 = np.float32
    inv_freq = np.power(f32(ROPE_THETA), -np.arange(half, dtype=f32) / f32(half))
    ang = np.arange(l, dtype=f32)[:, None] * inv_freq[None, :]
    cos, sin = np.cos(ang), np.sin(ang)
    ones = np.ones((l, HEAD_DIM - ROPE_DIM), f32)
    zeros = np.zeros((l, HEAD_DIM - ROPE_DIM), f32)
    zh = np.zeros((l, half), f32)
    c = np.concatenate([cos, cos, ones], axis=1)
    s_lo = np.concatenate([-sin, zh, zeros], axis=1)
    s_hi = np.concatenate([zh, sin, zeros], axis=1)
    return tuple(jnp.asarray(np.tile(t, (1, LANES // HEAD_DIM)), F32) for t in (c, s_lo, s_hi))


def _rope_bwd(dq, dkv, du_ssm, dpre, d_skip, tabs):
    l = dq.shape[0]
    nq = ATTN_WIDTH // LANES

    def body(dq_ref, dkv_ref, du_ref, dpre_ref, ds_ref, c_ref, lo_ref, hi_ref, o_ref):
        c, lo, hi = c_ref[...], lo_ref[...], hi_ref[...]
        for blk in range(nq + 1):
            t = dq_ref[:, blk * LANES:(blk + 1) * LANES] if blk < nq else dkv_ref[:, :KV_WIDTH]
            g = t * c + pltpu.roll(t * lo, 8, 1) + pltpu.roll(t * hi, LANES - 8, 1)
            o_ref[:, blk * LANES:(blk + 1) * LANES] = g.astype(BF16)
        o_ref[:, (nq + 1) * LANES:QKV_WIDTH] = dkv_ref[:, KV_WIDTH:].astype(BF16)
        o_ref[:, QKV_WIDTH:] = (du_ref[...] + dpre_ref[...] * ds_ref[...]).astype(BF16)

    tab = pl.BlockSpec((TM_EW, LANES), lambda i: (i, 0))
    wide = pl.BlockSpec((TM_EW, SSM_WIDTH), lambda i: (i, 0))
    return pl.pallas_call(
        body, grid=(l // TM_EW,),
        in_specs=[wide, pl.BlockSpec((TM_EW, 2 * KV_WIDTH), lambda i: (i, 0)), wide, wide,
                  pl.BlockSpec((1, SSM_WIDTH), lambda i: (0, 0)), tab, tab, tab],
        out_specs=pl.BlockSpec((TM_EW, IN_WIDTH), lambda i: (i, 0)),
        out_shape=jax.ShapeDtypeStruct((l, IN_WIDTH), BF16), name="rope_bwd",
        compiler_params=_params(("parallel",)),
    )(dq, dkv, du_ssm, dpre, d_skip, *tabs)


_Q_COLS = ATTN_WIDTH // LANES
_NEG = -1e30


def _window_specs(nb, width, col):
    return [
        pl.BlockSpec((BLOCK, width), lambda n: (jnp.maximum(n - 1, 0), col)),
        pl.BlockSpec((BLOCK, width), lambda n: (n, col)),
        pl.BlockSpec((BLOCK, width), lambda n: (jnp.minimum(n + 1, nb - 1), col)),
    ]


def _stacked_sink(sink_ref, heads):
    rid = lax.broadcasted_iota(jnp.int32, (len(heads) * BLOCK, 1), 0)
    sk = jnp.full(rid.shape, sink_ref[0, heads[-1]], F32)
    for g in range(len(heads) - 2, -1, -1):
        sk = jnp.where(rid < (g + 1) * BLOCK, sink_ref[0, heads[g]], sk)
    return sk


def _attn_fwd(qkv, sink):
    l = qkv.shape[0]
    nb = l // BLOCK
    grp = N_Q_HEADS // N_KV_HEADS

    def body(sink_ref, q_ref, k0, k1, k2, v0, v1, v2, o_ref, lse_ref):
        n = pl.program_id(0)
        q = q_ref[...]
        kw = jnp.concatenate([k0[...], k1[...], k2[...]], axis=0)
        vw = jnp.concatenate([v0[...], v1[...], v2[...]], axis=0)
        row = lax.broadcasted_iota(jnp.int32, (grp * BLOCK, 3 * BLOCK), 0)
        col = lax.broadcasted_iota(jnp.int32, (grp * BLOCK, 3 * BLOCK), 1)
        valid = jnp.abs(col - BLOCK - (row & (BLOCK - 1))) <= WINDOW
        valid &= jnp.logical_not((n == 0) & (col < BLOCK))
        valid &= jnp.logical_not((n == nb - 1) & (col >= 2 * BLOCK))
        for hk in range(N_KV_HEADS):
            heads = range(hk * grp, (hk + 1) * grp)
            qs = jnp.concatenate([q[:, h * HEAD_DIM:(h + 1) * HEAD_DIM] for h in heads], axis=0)
            kh = kw[:, hk * HEAD_DIM:(hk + 1) * HEAD_DIM]
            vh = vw[:, hk * HEAD_DIM:(hk + 1) * HEAD_DIM]
            s = jnp.where(valid, _dg(qs, kh, NT), _NEG)
            sk = _stacked_sink(sink_ref, heads)
            m = jnp.maximum(jnp.max(s, axis=1, keepdims=True), sk)
            p = jnp.exp(s - m)
            denom = jnp.sum(p, axis=1, keepdims=True) + jnp.exp(sk - m)
            o = _dg((p / denom).astype(BF16), vh, NN)
            lse = m + jnp.log(denom)
            for g, h in enumerate(heads):
                o_ref[:, h * HEAD_DIM:(h + 1) * HEAD_DIM] = o[g * BLOCK:(g + 1) * BLOCK]
                lse_ref[:, h:h + 1] = lse[g * BLOCK:(g + 1) * BLOCK]

    return pl.pallas_call(
        body, grid=(nb,),
        in_specs=[pl.BlockSpec(memory_space=pltpu.SMEM),
                  pl.BlockSpec((BLOCK, ATTN_WIDTH), lambda n: (n, 0))]
        + _window_specs(nb, KV_WIDTH, _Q_COLS) + _window_specs(nb, KV_WIDTH, _Q_COLS + 1),
        out_specs=[pl.BlockSpec((BLOCK, ATTN_WIDTH), lambda n: (n, 0)),
                   pl.BlockSpec((BLOCK, N_Q_HEADS), lambda n: (n, 0))],
        out_shape=[jax.ShapeDtypeStruct((l, ATTN_WIDTH), F32), jax.ShapeDtypeStruct((l, N_Q_HEADS), F32)],
        name="attn_fwd", compiler_params=_params(("parallel",)),
    )(sink, qkv, qkv, qkv, qkv, qkv, qkv, qkv)


def _attn_bwd(qkv, attn, dattn, lse, sink):
    l = qkv.shape[0]
    nb = l // BLOCK
    grp = N_Q_HEADS // N_KV_HEADS
    win = 3 * BLOCK

    def body(sink_ref, q_ref, k0, k1, k2, v0, v1, v2, o_ref, d_ref, l_ref, dq_ref, dkv_ref, dsink_ref, ring_ref):
        n = pl.program_id(0)

        @pl.when(n == 0)
        def _():
            dsink_ref[...] = jnp.zeros_like(dsink_ref)
            ring_ref[...] = jnp.zeros_like(ring_ref)

        @pl.when(n < nb)
        def _():
            first, last = n == 0, n == nb - 1
            cat = lambda a, b, c: jnp.concatenate([a[...], b[...], c[...]], axis=0)
            q, kw, vw = q_ref[...], cat(k0, k1, k2), cat(v0, v1, v2)
            dov = d_ref[...]
            prod = o_ref[...] * dov
            dob = dov.astype(BF16)
            lse = l_ref[...]
            row = lax.broadcasted_iota(jnp.int32, (grp * BLOCK, win), 0)
            col = lax.broadcasted_iota(jnp.int32, (grp * BLOCK, win), 1)
            valid = jnp.abs(col - BLOCK - (row & (BLOCK - 1))) <= WINDOW
            valid &= jnp.logical_not(first & (col < BLOCK))
            valid &= jnp.logical_not(last & (col >= 2 * BLOCK))

            dsink_parts, dks, dvs = [], [], []
            for hk in range(N_KV_HEADS):
                heads = range(hk * grp, (hk + 1) * grp)
                ksl = slice(hk * HEAD_DIM, (hk + 1) * HEAD_DIM)
                hsl = [slice(h * HEAD_DIM, (h + 1) * HEAD_DIM) for h in heads]
                stack = lambda parts: jnp.concatenate(parts, axis=0)
                qs = stack([q[:, s_] for s_ in hsl])
                dos = stack([dob[:, s_] for s_ in hsl])
                deltas = stack([jnp.sum(prod[:, s_], axis=1, keepdims=True) for s_ in hsl])
                lses = stack([lse[:, h:h + 1] for h in heads])
                kh, vh = kw[:, ksl], vw[:, ksl]
                s = jnp.where(valid, _dg(qs, kh, NT), _NEG)
                p = jnp.exp(s - lses)
                dp = _dg(dos, vh, NT)
                ds = (p * (dp - deltas)).astype(BF16)
                dq = _dg(ds, kh, NN) * SCORE_SCALE
                sink_rows = jnp.exp(_stacked_sink(sink_ref, heads) - lses) * deltas
                for g in range(grp):
                    dq_ref[:, hsl[g]] = dq[g * BLOCK:(g + 1) * BLOCK]
                    dsink_parts.append(jnp.sum(sink_rows[g * BLOCK:(g + 1) * BLOCK], axis=0, keepdims=True))
                dks.append(_dg(ds, qs, TN))
                dvs.append(_dg(p.astype(BF16), dos, TN))
            dsink_ref[...] -= jnp.concatenate(dsink_parts, axis=1)
            part = jnp.concatenate(dks + dvs, axis=1)
            ring_ref[(n + 2) % 3] += part[0:BLOCK]
            ring_ref[n % 3] += part[BLOCK:2 * BLOCK]
            ring_ref[(n + 1) % 3] = part[2 * BLOCK:]

        @pl.when(n >= 1)
        def _():
            dkv_ref[...] = ring_ref[(n + 2) % 3]

    centre = lambda n: jnp.minimum(n, nb - 1)
    window = lambda width, col: [
        pl.BlockSpec((BLOCK, width), lambda n: (jnp.maximum(centre(n) - 1, 0), col)),
        pl.BlockSpec((BLOCK, width), lambda n: (centre(n), col)),
        pl.BlockSpec((BLOCK, width), lambda n: (jnp.minimum(centre(n) + 1, nb - 1), col))]
    own = lambda width: pl.BlockSpec((BLOCK, width), lambda n: (centre(n), 0))
    return pl.pallas_call(
        body, grid=(nb + 1,),
        in_specs=[pl.BlockSpec(memory_space=pltpu.SMEM), own(ATTN_WIDTH)]
        + window(KV_WIDTH, _Q_COLS) + window(KV_WIDTH, _Q_COLS + 1)
        + [own(ATTN_WIDTH), own(ATTN_WIDTH), own(N_Q_HEADS)],
        out_specs=[own(ATTN_WIDTH), pl.BlockSpec((BLOCK, 2 * KV_WIDTH), lambda n: (jnp.maximum(n - 1, 0), 0)),
                   pl.BlockSpec((1, N_Q_HEADS), lambda n: (0, 0))],
        out_shape=[jax.ShapeDtypeStruct((l, ATTN_WIDTH), F32), jax.ShapeDtypeStruct((l, 2 * KV_WIDTH), F32),
                   jax.ShapeDtypeStruct((1, N_Q_HEADS), F32)],
        scratch_shapes=[pltpu.VMEM((3, BLOCK, 2 * KV_WIDTH), F32)],
        name="attn_bwd", compiler_params=_params(("arbitrary",)),
    )(sink, qkv, qkv, qkv, qkv, qkv, qkv, qkv, attn, dattn, lse)


def _ssm_disc(a_re, a_im, log_step, b_re, b_im):
    step = jnp.exp(log_step)[..., None]
    mag = jnp.exp(a_re * step)
    lb_re, lb_im = mag * jnp.cos(a_im * step), mag * jnp.sin(a_im * step)
    nr, ni = lb_re - 1.0, lb_im
    den = a_re * a_re + a_im * a_im
    f_re = ((nr * a_re + ni * a_im) / den)[..., None]
    f_im = ((ni * a_re - nr * a_im) / den)[..., None]
    return lb_re, lb_im, f_re * b_re - f_im * b_im, f_re * b_im + f_im * b_re


def _ssm_pack(lb_re, lb_im, bb_re, bb_im, c_re, c_im):
    eye = jnp.eye(SSM_CH // SSM_GROUP, dtype=F32)
    ng = SSM_CH // SSM_GROUP

    def diag_b(bb):
        t = bb.reshape(2, SSM_CB, ng, SSM_STATE, SSM_GROUP)
        return jnp.einsum('dkgpc,gh->dkgchp', t, eye).reshape(2, SSM_CB, SSM_CH, SSM_ST)

    def diag_c(cc):
        t = cc.reshape(2, SSM_CB, ng, SSM_GROUP, SSM_STATE)
        return jnp.einsum('dkgcp,gh->dkhpgc', t, eye).reshape(2, SSM_CB, SSM_ST, SSM_CH)

    bcat = jnp.concatenate([diag_b(bb_re), diag_b(bb_im)], axis=-1)
    ccat = jnp.concatenate([diag_c(c_re), -diag_c(c_im)], axis=-2)
    lam_re = lb_re.reshape(2, SSM_CB, 1, SSM_ST)
    lam_im = lb_im.reshape(2, SSM_CB, 1, SSM_ST)
    return bcat, ccat, lam_re, lam_im


def _ssm_unpack(dbcat, dccat, dlam_re, dlam_im):
    ng = SSM_CH // SSM_GROUP
    eye = jnp.eye(ng, dtype=F32)

    def undiag_b(t):
        t = t.reshape(2, SSM_CB, ng, SSM_GROUP, ng, SSM_STATE)
        return jnp.einsum('dkgchp,gh->dkgpc', t, eye).reshape(2, N_SSM_GROUPS, SSM_STATE, SSM_GROUP)

    def undiag_c(t):
        t = t.reshape(2, SSM_CB, ng, SSM_STATE, ng, SSM_GROUP)
        return jnp.einsum('dkhpgc,gh->dkgcp', t, eye).reshape(2, N_SSM_GROUPS, SSM_GROUP, SSM_STATE)

    dbb_re, dbb_im = undiag_b(dbcat[..., :SSM_ST]), undiag_b(dbcat[..., SSM_ST:])
    dc_re, dc_im = undiag_c(dccat[:, :, :SSM_ST]), -undiag_c(dccat[:, :, SSM_ST:])
    shape = (2, N_SSM_GROUPS, SSM_STATE)
    return dlam_re.reshape(shape), dlam_im.reshape(shape), dbb_re, dbb_im, dc_re, dc_im


def _to_segments(t):
    l, w = t.shape
    return t.reshape(N_SEG, l // N_SEG, w).transpose(1, 0, 2).reshape(l, w)


def _from_segments(t):
    l, w = t.shape
    return t.reshape(l // N_SEG, N_SEG, w).transpose(1, 0, 2).reshape(l, w)


SSM_RC = 256
SSM_JC = SSM_RC // N_SEG
_RE, _IM = pl.ds(0, SSM_ST), pl.ds(SSM_ST, SSM_ST)


def _cfma(ar, ai, xr, xi, br, bi):
    return ar * xr - ai * xi + br, ar * xi + ai * xr + bi


def _chunk_rows(ci, rev, nc):
    start = jnp.where(rev, (nc - 1 - ci) * SSM_RC, ci * SSM_RC)
    return pl.ds(pl.multiple_of(start, SSM_RC), SSM_RC)


def _scan_chunk(src, dst, ar, ai, rev, nj, ci, carry, prev_ref=None):
    def rows_of(staged, j, k):
        at = jnp.where(rev, SSM_JC - 1 - k, k) if staged else j
        return pl.ds(pl.multiple_of(at * N_SEG, N_SEG), N_SEG)

    for k in range(SSM_JC):
        jj = ci * SSM_JC + k
        j = jnp.where(rev, nj - 1 - jj, jj)
        rows = rows_of(src[1], j, k)
        nr, ni = _cfma(ar, ai, carry[0], carry[1], src[0][rows, _RE], src[0][rows, _IM])
        if dst is not None:
            rows = rows_of(dst[1], j, k)
            dst[0][rows, _RE] = nr
            dst[0][rows, _IM] = ni
        if prev_ref is None:
            carry = (nr, ni)
            continue
        jp = jnp.where(rev, j - 1, j + 1)
        if k == SSM_JC - 1:
            inside = jnp.where((jp >= 0) & (jp < nj), 1.0, 0.0)
            jp = jnp.clip(jp, 0, nj - 1)
        prow = pl.ds(pl.multiple_of(jp * N_SEG, N_SEG), N_SEG)
        xr, xi = prev_ref[prow, _RE], prev_ref[prow, _IM]
        sr, si = nr * xr + ni * xi, ni * xr - nr * xi
        if k == SSM_JC - 1:
            sr, si = inside * sr, inside * si
        carry = (nr, ni, carry[2] + sr, carry[3] + si)
    return carry


def _segment_inits(ar, ai, end_r, end_i, rev, nj):
    pr, pi = ar, ai
    for _ in range(int(math.log2(nj))):
        pr, pi = pr * pr - pi * pi, 2.0 * pr * pi
    seg = lax.broadcasted_iota(jnp.int32, end_r.shape, 0)
    zero = jnp.zeros_like(end_r)

    def chain(shift, keep):
        ir, ii = zero, zero
        for _ in range(N_SEG - 1):
            tr, ti = _cfma(pr, pi, ir, ii, end_r, end_i)
            ir = jnp.where(keep, pltpu.roll(tr, shift, 0), 0.0)
            ii = jnp.where(keep, pltpu.roll(ti, shift, 0), 0.0)
        return ir, ii

    up_r, up_i = chain(1, seg >= 1)
    dn_r, dn_i = chain(N_SEG - 1, seg <= N_SEG - 2)
    return jnp.where(rev, dn_r, up_r), jnp.where(rev, dn_i, up_i)


def _ssm_specs(l):
    act = pl.BlockSpec((l, SSM_CH), lambda k, d: (0, k))
    bmat = pl.BlockSpec((None, None, SSM_CH, 2 * SSM_ST), lambda k, d: (d, k, 0, 0))
    cmat = pl.BlockSpec((None, None, 2 * SSM_ST, SSM_CH), lambda k, d: (d, k, 0, 0))
    lam = pl.BlockSpec((None, None, 1, SSM_ST), lambda k, d: (d, k, 0, 0))
    return act, bmat, cmat, lam


def _ssm_fwd(u_seg, bcat, ccat, lam_re, lam_im):
    l = u_seg.shape[0]
    nj = l // N_SEG
    nc = l // SSM_RC

    def body(u_ref, b_ref, c_ref, lr_ref, li_ref, y_ref, keep_ref, xs_ref, stage0, stage1, keep_sem):
        k, d = pl.program_id(0), pl.program_id(1)
        rev = d == 1
        shape = (N_SEG, SSM_ST)
        ar, ai = jnp.broadcast_to(lr_ref[...], shape), jnp.broadcast_to(li_ref[...], shape)
        zero = jnp.zeros(shape, F32)

        def inputs(ci, stage):
            rows = _chunk_rows(ci, rev, nc)
            bu = _dg(u_ref[rows, :], b_ref[...], NN)
            stage[...] = bu
            xs_ref[rows, :] = bu

        def first(stage, ci, carry):
            return _scan_chunk((stage, True), None, ar, ai, rev, nj, ci, carry)

        def first_pass(t, carry):
            inputs(2 * t + 1, stage1)
            carry = first(stage0, 2 * t, carry)
            inputs(2 * t + 2, stage0)
            return first(stage1, 2 * t + 1, carry)

        inputs(0, stage0)
        carry = lax.fori_loop(0, nc // 2 - 1, first_pass, (zero, zero))
        inputs(nc - 1, stage1)
        carry = first(stage0, nc - 2, carry)
        end_r, end_i = first(stage1, nc - 1, carry)
        init = _segment_inits(ar, ai, end_r, end_i, rev, nj)

        @pl.when(d == 0)
        def _():
            y_ref[...] = jnp.zeros_like(y_ref)

        def outputs(ci):
            rows = _chunk_rows(ci, rev, nc)
            y_ref[rows, :] += _dg(xs_ref[rows, :].astype(BF16), c_ref[...], NN)
            pltpu.make_async_copy(xs_ref.at[rows], keep_ref.at[d, k, rows], keep_sem).start()

        def second(ci, carry):
            return _scan_chunk((xs_ref, False), (xs_ref, False), ar, ai, rev, nj, ci, carry)

        def second_pass(ci, carry):
            outputs(ci - 1)
            return second(ci, carry)

        lax.fori_loop(1, nc, second_pass, second(0, init))
        outputs(nc - 1)
        pltpu.make_async_copy(xs_ref, keep_ref.at[d, k], keep_sem).wait()

    act, bmat, cmat, lam = _ssm_specs(l)
    return pl.pallas_call(
        body, grid=(SSM_CB, 2), in_specs=[act, bmat, cmat, lam, lam], out_specs=[act, ANY],
        out_shape=[jax.ShapeDtypeStruct((l, SSM_WIDTH), F32),
                   jax.ShapeDtypeStruct((2, SSM_CB, l, 2 * SSM_ST), F32)],
        scratch_shapes=[pltpu.VMEM((l, 2 * SSM_ST), F32), pltpu.VMEM((SSM_RC, 2 * SSM_ST), F32),
                        pltpu.VMEM((SSM_RC, 2 * SSM_ST), F32), pltpu.SemaphoreType.DMA],
        name="ssm_fwd", compiler_params=_params(("parallel", "arbitrary"), vmem_mb=56),
    )(u_seg, bcat.astype(BF16), ccat.astype(BF16), lam_re, lam_im)


def _ssm_bwd(u_seg, dy_seg, states, bcat, ccat, lam_re, lam_im):
    l = u_seg.shape[0]
    nj = l // N_SEG
    nc = l // SSM_RC

    def body(u_ref, dy_ref, keep_ref, b_ref, c_ref, lr_ref, li_ref,
             du_ref, db_ref, dc_ref, dlr_ref, dli_ref, xs_ref, gs_ref, stage0, stage1, keep_sem):
        k, d = pl.program_id(0), pl.program_id(1)
        rev = d == 1
        back = jnp.logical_not(rev)
        shape = (N_SEG, SSM_ST)
        ar, ai = jnp.broadcast_to(lr_ref[...], shape), -jnp.broadcast_to(li_ref[...], shape)
        zero = jnp.zeros(shape, F32)
        fetch = pltpu.make_async_copy(keep_ref.at[d, k], xs_ref, keep_sem)
        fetch.start()

        def inputs(ci, stage):
            rows = _chunk_rows(ci, back, nc)
            dx = _dg(dy_ref[rows, :], c_ref[...], NT)
            stage[...] = dx
            gs_ref[rows, :] = dx

        def first(stage, ci, carry):
            return _scan_chunk((stage, True), None, ar, ai, back, nj, ci, carry)

        def first_pass(t, carry):
            inputs(2 * t + 1, stage1)
            carry = first(stage0, 2 * t, carry)
            inputs(2 * t + 2, stage0)
            return first(stage1, 2 * t + 1, carry)

        inputs(0, stage0)
        carry = lax.fori_loop(0, nc // 2 - 1, first_pass, (zero, zero))
        inputs(nc - 1, stage1)
        carry = first(stage0, nc - 2, carry)
        end_r, end_i = first(stage1, nc - 1, carry)
        init = _segment_inits(ar, ai, end_r, end_i, back, nj)
        fetch.wait()
        db_ref[...] = jnp.zeros_like(db_ref)
        dc_ref[...] = jnp.zeros_like(dc_ref)

        @pl.when(d == 0)
        def _():
            du_ref[...] = jnp.zeros_like(du_ref)

        def outputs(ci, stage):
            rows = _chunk_rows(ci, back, nc)
            g = stage[...].astype(BF16)
            dc_ref[...] += _dg(xs_ref[rows, :].astype(BF16), dy_ref[rows, :], TN)
            db_ref[...] += _dg(u_ref[rows, :], g, TN)
            du_ref[rows, :] += _dg(g, b_ref[...], NT)

        def second(ci, stage, carry):
            return _scan_chunk((gs_ref, False), (stage, True), ar, ai, back, nj, ci, carry, prev_ref=xs_ref)

        def second_pass(t, carry):
            outputs(2 * t, stage0)
            carry = second(2 * t + 1, stage1, carry)
            outputs(2 * t + 1, stage1)
            return second(2 * t + 2, stage0, carry)

        carry = lax.fori_loop(0, nc // 2 - 1, second_pass, second(0, stage0, init + (zero, zero)))
        outputs(nc - 2, stage0)
        gr, gi, acc_r, acc_i = second(nc - 1, stage1, carry)
        outputs(nc - 1, stage1)

        seg = lax.broadcasted_iota(jnp.int32, shape, 0)
        jb = jnp.where(rev, nj - 1, 0)
        erow = pl.ds(pl.multiple_of((nj - 1 - jb) * N_SEG, N_SEG), N_SEG)

        def before(t):
            up = jnp.where(seg >= 1, pltpu.roll(t, 1, 0), 0.0)
            down = jnp.where(seg <= N_SEG - 2, pltpu.roll(t, N_SEG - 1, 0), 0.0)
            return jnp.where(rev, down, up)

        init_r, init_i = before(xs_ref[erow, _RE]), before(xs_ref[erow, _IM])
        acc_r = acc_r + gr * init_r + gi * init_i
        acc_i = acc_i + gi * init_r - gr * init_i
        dlr_ref[...] = jnp.sum(acc_r, axis=0, keepdims=True)
        dli_ref[...] = jnp.sum(acc_i, axis=0, keepdims=True)

    act, bmat, cmat, lam = _ssm_specs(l)
    return pl.pallas_call(
        body, grid=(SSM_CB, 2), in_specs=[act, act, ANY, bmat, cmat, lam, lam],
        out_specs=[act, bmat, cmat, lam, lam],
        out_shape=[jax.ShapeDtypeStruct((l, SSM_WIDTH), F32),
                   jax.ShapeDtypeStruct(bcat.shape, F32), jax.ShapeDtypeStruct(ccat.shape, F32),
                   jax.ShapeDtypeStruct(lam_re.shape, F32), jax.ShapeDtypeStruct(lam_im.shape, F32)],
        scratch_shapes=[pltpu.VMEM((l, 2 * SSM_ST), F32), pltpu.VMEM((l, 2 * SSM_ST), F32),
                        pltpu.VMEM((SSM_RC, 2 * SSM_ST), F32), pltpu.VMEM((SSM_RC, 2 * SSM_ST), F32),
                        pltpu.SemaphoreType.DMA],
        name="ssm_bwd", compiler_params=_params(("parallel", "arbitrary"), vmem_mb=58),
    )(u_seg, dy_seg, states, bcat.astype(BF16), ccat.astype(BF16), lam_re, lam_im)


def _glu_fwd(y_ssm, u, d_skip, w_glu):
    l, w = u.shape

    def body(y_ref, u_ref, d_ref, w_ref, pre_ref, s_ref, ys_ref):
        pre = y_ref[...] + d_ref[...] * u_ref[...]
        z = _gelu(pre)
        s = _dg(z.astype(BF16), w_ref[...], NN)
        pre_ref[...] = pre
        s_ref[...] = s
        ys_ref[...] = z * _sigmoid(s)

    row = pl.BlockSpec((TM_EW, w), lambda i: (i, 0))
    out = jax.ShapeDtypeStruct((l, w), F32)
    return pl.pallas_call(
        body, grid=(l // TM_EW,),
        in_specs=[row, row, pl.BlockSpec((1, w), lambda i: (0, 0)), pl.BlockSpec((w, w), lambda i: (0, 0))],
        out_specs=[row, row, row], out_shape=[out, out, out], name="glu_fwd",
        compiler_params=_params(("parallel",)),
    )(y_ssm, u, d_skip, w_glu)


def _glu_bwd(pre, s, dys, u, d_skip, w_glu):
    l, w = u.shape

    def body(pre_ref, s_ref, dys_ref, u_ref, d_ref, w_ref, dpre_ref, z_ref, ds_ref, dd_ref):
        pre, dys = pre_ref[...], dys_ref[...]
        z = _gelu(pre)
        sig = _sigmoid(s_ref[...])
        ds = (dys * z * sig * (1.0 - sig)).astype(BF16)
        dz = dys * sig + _dg(ds, w_ref[...], NT)
        dpre = dz * _gelu_grad(pre)
        dpre_ref[...] = dpre
        z_ref[...] = z.astype(BF16)
        ds_ref[...] = ds

        @pl.when(pl.program_id(0) == 0)
        def _():
            dd_ref[...] = jnp.zeros_like(dd_ref)

        dd_ref[...] += jnp.sum(dpre * u_ref[...], axis=0, keepdims=True)

    row = pl.BlockSpec((TM_EW, w), lambda i: (i, 0))
    vec = pl.BlockSpec((1, w), lambda i: (0, 0))
    return pl.pallas_call(
        body, grid=(l // TM_EW,),
        in_specs=[row, row, row, row, vec, pl.BlockSpec((w, w), lambda i: (0, 0))],
        out_specs=[row, row, row, vec],
        out_shape=[jax.ShapeDtypeStruct((l, w), F32), jax.ShapeDtypeStruct((l, w), BF16),
                   jax.ShapeDtypeStruct((l, w), BF16), jax.ShapeDtypeStruct((1, w), F32)],
        name="glu_bwd", compiler_params=_params(("arbitrary",)),
    )(pre, s, dys, u, d_skip, w_glu)


TM_CV = 512
TC_CV = 256
TM_CF = 256
TC_CF = D_FF // 2
HALO = SUBLANES


def _conv_specs(l, col0, tm=TM_CV, tc=TC_CV):
    per = tm // HALO
    nh = l // HALO
    off = col0 // tc
    return [
        pl.BlockSpec((HALO, tc), lambda j, i: (jnp.maximum(i * per - 1, 0), j + off)),
        pl.BlockSpec((tm, tc), lambda j, i: (i, j + off)),
        pl.BlockSpec((HALO, tc), lambda j, i: (jnp.minimum((i + 1) * per, nh - 1), j + off)),
    ]


def _ext(prev_ref, mid_ref, next_ref, first, last):
    p = jnp.where(first, 0.0, prev_ref[...])
    n = jnp.where(last, 0.0, next_ref[...])
    return jnp.concatenate([p, mid_ref[...], n], axis=0)


def _shift_dn(t):
    return pltpu.roll(t, 1, 0)


def _shift_up(t):
    return pltpu.roll(t, t.shape[0] - 1, 0)


def _conv3(e, w_ref, b_ref):
    return w_ref[0:1, :] * _shift_dn(e) + w_ref[1:2, :] * e + w_ref[2:3, :] * _shift_up(e) + b_ref[...]


def _convffn_fwd(up_pre, conv_w, conv_b):
    l = up_pre.shape[0]
    tm, tc = TM_CF, TC_CF
    ni = l // tm
    wspec = lambda off: pl.BlockSpec((3, tc), lambda j, i: (0, j + off))
    bspec = lambda off: pl.BlockSpec((1, tc), lambda j, i: (0, j + off))
    voff = D_FF // tc

    def body(gp, gm, gn, vp, vm, vn, wg, bg, wv, bv, o_ref):
        i = pl.program_id(1)
        first, last = i == 0, i == ni - 1
        gate = _conv3(_ext(gp, gm, gn, first, last), wg, bg)[HALO:HALO + tm]
        val = _conv3(_ext(vp, vm, vn, first, last), wv, bv)[HALO:HALO + tm]
        o_ref[...] = (gate * _sigmoid(gate) * val).astype(BF16)

    return pl.pallas_call(
        body, grid=(D_FF // tc, ni),
        in_specs=_conv_specs(l, 0, tm, tc) + _conv_specs(l, D_FF, tm, tc)
        + [wspec(0), bspec(0), wspec(voff), bspec(voff)],
        out_specs=pl.BlockSpec((tm, tc), lambda j, i: (i, j)),
        out_shape=jax.ShapeDtypeStruct((l, D_FF), BF16), name="convffn_fwd",
        compiler_params=_params(("parallel", "parallel")),
    )(up_pre, up_pre, up_pre, up_pre, up_pre, up_pre, conv_w, conv_b, conv_w, conv_b)


HALO_B = 2 * SUBLANES


def _convffn_bwd(up_pre, dx2b, w_down, conv_w, conv_b):
    l = up_pre.shape[0]
    ni = l // TM_CV
    d = dx2b.shape[1]
    wspec = lambda off: pl.BlockSpec((3, TC_CV), lambda i, j: (0, j + off))
    bspec = lambda off: pl.BlockSpec((1, TC_CV), lambda i, j: (0, j + off))
    voff = D_FF // TC_CV
    swap = lambda spec: pl.BlockSpec(spec.block_shape, lambda i, j, f=spec.index_map: f(j, i))
    per, nh = TM_CV // HALO_B, l // HALO_B
    dx_specs = [pl.BlockSpec((HALO_B, d), lambda i, j: (jnp.maximum(i * per - 1, 0), 0)),
                pl.BlockSpec((TM_CV, d), lambda i, j: (i, 0)),
                pl.BlockSpec((HALO_B, d), lambda i, j: (jnp.minimum((i + 1) * per, nh - 1), 0))]

    def body(gp, gm, gn, vp, vm, vn, xp, xm, xn, wd, wg, bg, wv, bv, dup_ref, pg_ref, pv_ref):
        i = pl.program_id(0)
        first, last = i == 0, i == ni - 1
        ge, ve = _ext(gp, gm, gn, first, last), _ext(vp, vm, vn, first, last)
        zero = jnp.zeros((HALO_B, d), BF16)
        dx = jnp.concatenate([jnp.where(first, zero, xp[...]), xm[...], jnp.where(last, zero, xn[...])], axis=0)
        de = _dg(dx, wd[...], NT)[HALO_B - HALO:HALO_B + TM_CV + HALO]
        taps = [(_shift_dn(e), e, _shift_up(e)) for e in (ge, ve)]
        conv = lambda t, w_ref, b_ref: w_ref[0:1, :] * t[0] + w_ref[1:2, :] * t[1] + w_ref[2:3, :] * t[2] + b_ref[...]
        gate, val = conv(taps[0], wg, bg), conv(taps[1], wv, bv)
        sig = _sigmoid(gate)
        silu = gate * sig
        dgate = de * val * (sig + silu * (1.0 - sig))
        dval = de * silu
        mid = slice(HALO, HALO + TM_CV)
        rid = lax.broadcasted_iota(jnp.int32, (SUBLANES, TC_CV), 0)
        for half, (dup, tap, w_ref, p_ref) in enumerate(((dgate, taps[0], wg, pg_ref), (dval, taps[1], wv, pv_ref))):
            dpre = w_ref[0:1, :] * _shift_up(dup) + w_ref[1:2, :] * dup + w_ref[2:3, :] * _shift_dn(dup)
            dup_ref[half] = dpre[mid].astype(BF16)
            dm_ = dup[mid]
            sums = [jnp.sum(dm_ * t[mid], axis=0, keepdims=True) for t in tap]
            sums.append(jnp.sum(dm_, axis=0, keepdims=True))
            acc = jnp.zeros((SUBLANES, TC_CV), F32)
            for k, sk in enumerate(sums):
                acc = jnp.where(rid == k, sk, acc)
            p_ref[...] = acc

    par = pl.BlockSpec((None, SUBLANES, TC_CV), lambda i, j: (i, 0, j))
    dup, pg, pv = pl.pallas_call(
        body, grid=(ni, D_FF // TC_CV),
        in_specs=[swap(s) for s in _conv_specs(l, 0) + _conv_specs(l, D_FF)] + dx_specs
        + [pl.BlockSpec((TC_CV, d), lambda i, j: (j, 0)), wspec(0), bspec(0), wspec(voff), bspec(voff)],
        out_specs=[pl.BlockSpec((2, TM_CV, TC_CV), lambda i, j: (0, i, j)), par, par],
        out_shape=[jax.ShapeDtypeStruct((2, l, D_FF), BF16),
                   jax.ShapeDtypeStruct((ni, SUBLANES, D_FF), F32), jax.ShapeDtypeStruct((ni, SUBLANES, D_FF), F32)],
        name="convffn_bwd", compiler_params=_params(("parallel", "parallel")),
    )(up_pre, up_pre, up_pre, up_pre, up_pre, up_pre, dx2b, dx2b, dx2b, w_down, conv_w, conv_b, conv_w, conv_b)
    return dup, jnp.concatenate([jnp.sum(pg, axis=0), jnp.sum(pv, axis=0)], axis=1)


def _local_step(x, target, wb, sp, mixer_weights=None, late_weights=None, grads_ready=None,
                grads_next=None):
    l = x.shape[0]
    tabs = _rope_tables(l)
    disc = _ssm_disc(sp["a_re"], sp["a_im"], sp["log_step"], sp["b_re"], sp["b_im"])
    bcat, ccat, lam_re, lam_im = _ssm_pack(*disc, sp["c_re"], sp["c_im"])
    d_skip = sp["d_skip"].reshape(1, SSM_WIDTH)

    h, qkv, u = _rms_mm_rope(x, sp["norm_mix_g"], wb["w_in"], tabs, "mm_in")
    attn, lse = _attn_fwd(qkv, sp["sink"])
    u_seg = _to_segments(u).astype(BF16)
    y_seg, states = _ssm_fwd(u_seg, bcat, ccat, lam_re, lam_im)
    y_ssm = _from_segments(y_seg)
    if mixer_weights is not None:
        wb = dict(wb, **mixer_weights(attn))
    pre, s_glu, ys = _glu_fwd(y_ssm, u, d_skip, wb["w_glu"])
    mixed, x1, h2 = _mix_mm_res_rms(attn, ys, sp["norm_attn_g"], sp["norm_ssm_g"], wb["w_out"], x,
                                    sp["norm_ffn_g"], "mm_out")
    if late_weights is not None:
        wb = dict(wb, **late_weights(h2))
    up_pre = _mm_nn_cols(h2, wb["w_up"], min(l, 1024), "mm_up")
    conv_w = wb["conv_w"]
    act = _convffn_fwd(up_pre, conv_w, sp["conv_b"])
    loss, dx2, dx2b, d_final_g = _mm_res_loss(act, wb["w_down"], x1, sp["norm_final_g"].reshape(1, D_MODEL), target)

    g = {"norm_final_g": d_final_g.reshape(D_MODEL)}
    g["w_down"] = _mm_tn(act, dx2b, D_FF // 2, 512, "mm_down_dw")
    dup_pre, conv_par = _convffn_bwd(up_pre, dx2b, wb["w_down"], conv_w, sp["conv_b"])
    g["conv_w"], g["conv_b"] = conv_par[0:3], conv_par[3:4]
    g["w_up"] = _mm_tn_cols(h2, dup_pre, wb["w_up"].shape[0], 512, "mm_up_dw")
    dx1, dx1b, g["norm_ffn_g"] = _mm_cols_rms_bwd(dup_pre, wb["w_up"], x1, sp["norm_ffn_g"], dx2, "mm_up_dx")
    g["w_out"] = _mm_tn(mixed, dx1b, 1024, 1024, "mm_out_dw")
    zero = grads_ready(g["w_up"], g["w_down"], g["w_out"]) if grads_ready is not None else 0.0
    dattn, dys, g["norm_attn_g"], g["norm_ssm_g"] = _mm_mix_bwd(
        dx1b, wb["w_out"], attn, ys, sp["norm_attn_g"] + zero, sp["norm_ssm_g"], "mm_out_dx")
    dpre, zb, dsb, dd = _glu_bwd(pre, s_glu, dys, u, d_skip, wb["w_glu"])
    g["d_skip"] = dd.reshape(N_SSM_GROUPS, SSM_GROUP)
    g["w_glu"] = _mm_tn(zb, dsb, 512, 512, "mm_glu_dw")
    zero = grads_next(g["w_glu"]) if grads_next is not None else 0.0
    du_seg, dbcat, dccat, dlam_re, dlam_im = _ssm_bwd(u_seg, _to_segments(dpre).astype(BF16), states, bcat, ccat,
                                                      lam_re + zero, lam_im)
    dlb_re, dlb_im, dbb_re, dbb_im, g["c_re"], g["c_im"] = _ssm_unpack(dbcat, dccat, dlam_re, dlam_im)
    _, disc_vjp = jax.vjp(_ssm_disc, sp["a_re"], sp["a_im"], sp["log_step"], sp["b_re"], sp["b_im"])
    g["a_re"], g["a_im"], g["log_step"], g["b_re"], g["b_im"] = disc_vjp((dlb_re, dlb_im, dbb_re, dbb_im))
    dq, dkv, g["sink"] = _attn_bwd(qkv, attn, dattn, lse, sp["sink"])
    dproj = _rope_bwd(dq, dkv, _from_segments(du_seg), dpre, d_skip, tabs)
    g["w_in"] = _mm_tn(dproj, h, IN_WIDTH // 5, D_MODEL, "mm_in_dw")
    grad_x, _, g["norm_mix_g"] = _mm_nn_rms_bwd(dproj, wb["w_in"], x, sp["norm_mix_g"], dx1, "mm_in_dx")
    return loss, grad_x, g


MESH = pl.DeviceIdType.MESH
ANY = pl.BlockSpec(memory_space=pl.ANY)


def _place():
    x, y, c = lax.axis_index("x"), lax.axis_index("y"), lax.axis_index("c")
    chips = [(1 - x, y), (x, 1 - y), (1 - x, 1 - y)]
    return x, y, c, chips


def _chip_index(px, py):
    return 2 * px + py


CHUNK_BYTES = 256 * 1024
MAX_CHUNKS = 16


def _row_chunks(rows, row_bytes, align):
    n = max(1, min(MAX_CHUNKS, (rows * row_bytes) // CHUNK_BYTES))
    per = -(-rows // n)
    per = -(-per // align) * align
    return [(r0, min(per, rows - r0)) for r0 in range(0, rows, per)]


def _align_of(dtype):
    return SUBLANES * 4 // jnp.dtype(dtype).itemsize


def _remote(src, dst, send_sem, recv_sem, to):
    return pltpu.make_async_remote_copy(src_ref=src, dst_ref=dst, send_sem=send_sem, recv_sem=recv_sem,
                                        device_id=to, device_id_type=MESH)


CAST_ROWS = 64


def _gather_weights(shards, dtypes):
    nw = len(shards)

    def body(*refs):
        w_refs, o_refs = refs[:nw], refs[nw:2 * nw]
        send_sems, recv_sems, in_sems, out_sems = refs[2 * nw:2 * nw + 4]
        raw, cast = refs[2 * nw + 4:3 * nw + 4], refs[3 * nw + 4:]
        x, y, c, chips = _place()
        mine = _chip_index(x, y)
        sibling = (x, y, 1 - c)

        def rows_of(ref, chip, r0, nr):
            return ref.at[chip, pl.ds(r0, nr), :]

        def copy(wi, k, src, dst, to):
            return _remote(src, dst, send_sems.at[wi, k], recv_sems.at[wi, k], to)

        geo = []
        for wi in range(nw):
            rows, cols = w_refs[wi].shape
            row_bytes = cols * jnp.dtype(dtypes[wi]).itemsize
            geo.append((rows // 2, _row_chunks(rows // 2, row_bytes, _align_of(dtypes[wi]))))

        stage_in = [pltpu.make_async_copy(w_refs[wi], raw[wi], in_sems.at[wi]) for wi in range(nw)]
        for cp in stage_in:
            cp.start()
        staged = [raw[wi] if dtypes[wi] == w_refs[wi].dtype else cast[wi] for wi in range(nw)]
        stage_out = []
        for wi in range(nw):
            stage_in[wi].wait()
            if staged[wi] is not raw[wi]:
                def cast_rows(i, _, wi=wi):
                    rows = pl.ds(pl.multiple_of(i * CAST_ROWS, CAST_ROWS), CAST_ROWS)
                    cast[wi][rows, :] = raw[wi][rows, :].astype(dtypes[wi])
                    return 0

                lax.fori_loop(0, w_refs[wi].shape[0] // CAST_ROWS, cast_rows, 0)
            cp = pltpu.make_async_copy(staged[wi], o_refs[wi].at[mine], out_sems.at[wi])
            cp.start()
            stage_out.append(cp)

        for wi in range(nw):
            hr, half_chunks = geo[wi]
            for j, chip in enumerate(chips):
                for r0, nr in half_chunks:
                    copy(wi, j, staged[wi].at[pl.ds(c * hr + r0, nr), :],
                         rows_of(o_refs[wi], mine, c * hr + r0, nr), (*chip, c)).start()
        for wi in range(nw):
            hr, half_chunks = geo[wi]
            for j, chip in enumerate(chips):
                got = rows_of(o_refs[wi], _chip_index(*chip), c * hr, hr)
                copy(wi, j, got, got, (*chip, c)).wait_recv()
                for r0, nr in half_chunks:
                    piece = rows_of(o_refs[wi], _chip_index(*chip), c * hr + r0, nr)
                    copy(wi, 3 + j, piece, piece, sibling).start()
        for wi in range(nw):
            hr = geo[wi][0]
            for j, chip in enumerate(chips):
                got = rows_of(o_refs[wi], _chip_index(*chip), (1 - c) * hr, hr)
                copy(wi, 3 + j, got, got, sibling).wait_recv()
        for wi in range(nw):
            hr = geo[wi][0]
            sent = rows_of(o_refs[wi], mine, c * hr, hr)
            for k in range(6):
                copy(wi, k, sent, sent, sibling).wait_send()
            stage_out[wi].wait()

    return pl.pallas_call(
        body, in_specs=[ANY] * nw, out_specs=[ANY] * nw,
        out_shape=[jax.ShapeDtypeStruct((4, *s.shape), t) for s, t in zip(shards, dtypes)],
        scratch_shapes=[pltpu.SemaphoreType.DMA((nw, 6)), pltpu.SemaphoreType.DMA((nw, 6)),
                        pltpu.SemaphoreType.DMA((nw,)), pltpu.SemaphoreType.DMA((nw,))]
        + [pltpu.VMEM(s.shape, s.dtype) for s in shards] + [pltpu.VMEM(s.shape, t) for s, t in zip(shards, dtypes)],
        name="gather_weights", compiler_params=_params(vmem_mb=40),
    )(*shards)


HBM = pl.BlockSpec(memory_space=pltpu.HBM)
SEM = pl.BlockSpec(memory_space=pltpu.SEMAPHORE)
EFFECT = pltpu.SideEffectType.DATAFLOW_SIDE_EFFECTING


def _cast_place(w, place, dtype, after, name):
    rows, cols = w.shape
    tr = _row_tile(rows, cols, _align_of(dtype))

    def body(p_ref, w_ref, after_ref, o_ref):
        del p_ref, after_ref
        o_ref[...] = w_ref[...].astype(dtype)

    grid_spec = pltpu.PrefetchScalarGridSpec(
        num_scalar_prefetch=1, grid=(rows // tr,),
        in_specs=[pl.BlockSpec((tr, cols), lambda i, p: (i, 0)), ANY],
        out_specs=pl.BlockSpec((None, tr, cols), lambda i, p: (p[1], i, 0)))
    return pl.pallas_call(body, grid_spec=grid_spec, out_shape=jax.ShapeDtypeStruct((4, rows, cols), dtype),
                          name=name, compiler_params=_params(("parallel",)))(place, w, after)


def _split_start(name, arrays, n_pairs, issue):
    n = len(arrays)

    def body(*refs):
        issue(refs[:n], refs[n:n + n_pairs], refs[n + n_pairs:n + 2 * n_pairs])
        token = refs[2 * n + 2 * n_pairs]
        token[...] = jnp.zeros_like(token)

    dma = pltpu.SemaphoreType.DMA(())
    outs = pl.pallas_call(
        body, name=name,
        out_shape=[dma] * (2 * n_pairs) + [pltpu.HBM(t.shape, t.dtype) for t in arrays]
        + [jax.ShapeDtypeStruct((SUBLANES, LANES), F32)],
        in_specs=[HBM] * n, out_specs=[SEM] * (2 * n_pairs) + [HBM] * n + [pl.BlockSpec(memory_space=pltpu.VMEM)],
        input_output_aliases={a: 2 * n_pairs + a for a in range(n)},
        compiler_params=pltpu.CompilerParams(has_side_effects=EFFECT),
    )(*[pltpu.with_memory_space_constraint(t, pltpu.HBM) for t in arrays])
    return outs[:n_pairs], outs[n_pairs:2 * n_pairs], outs[2 * n_pairs:2 * n_pairs + n], outs[-1]


def _split_wait(name, send_sems, recv_sems, flying, sizes, after):
    n, n_pairs = len(flying), len(send_sems)

    def body(*refs):
        x, y, c, _ = _place()
        for k, ref in enumerate(sizes(refs[:n])):
            cp = _remote(ref, ref, refs[n + k], refs[n + n_pairs + k], (x, y, 1 - c))
            cp.wait_send()
            cp.wait_recv()

    return pl.pallas_call(
        body, name=name, out_shape=[pltpu.HBM(t.shape, t.dtype) for t in flying],
        in_specs=[HBM] * n + [SEM] * (2 * n_pairs) + [ANY], out_specs=[HBM] * n,
        input_output_aliases={a: a for a in range(n)},
        compiler_params=pltpu.CompilerParams(has_side_effects=EFFECT),
    )(*flying, *send_sems, *recv_sems, after)


def _spread_start(lands, name):
    def issue(land_refs, send_sems, recv_sems):
        x, y, c, chips = _place()
        mine = _chip_index(x, y)
        for a, land in enumerate(land_refs):
            _, rows, cols = land.shape
            hr = rows // 2
            row_bytes = cols * jnp.dtype(land.dtype).itemsize
            for r0, nr in _row_chunks(hr, row_bytes, _align_of(land.dtype)):
                piece = land.at[mine, pl.ds(c * hr + r0, nr), :]
                for chip in chips:
                    for core in (0, 1):
                        _remote(piece, piece, send_sems[a], recv_sems[a], (*chip, core)).start()

    return _split_start(name, lands, len(lands), issue)


def _spread_wait(send_sems, recv_sems, flying, after, name):
    return _split_wait(name, send_sems, recv_sems, flying, lambda refs: [r.at[pl.ds(0, 3)] for r in refs], after)


def _pair_start(grads):
    n = len(grads)
    zones = [lax.empty((4, g.shape[1] // 2, g.shape[2]), F32) for g in grads]

    def issue(refs, send_sems, recv_sems):
        x, y, c, _ = _place()
        for a in range(n):
            g_ref, z_ref = refs[a], refs[n + a]
            _, rows, cols = g_ref.shape
            hr = rows // 2
            for k in range(4):
                for r0, nr in _row_chunks(hr, cols * 4, SUBLANES):
                    _remote(g_ref.at[k, pl.ds((1 - c) * hr + r0, nr), :], z_ref.at[k, pl.ds(r0, nr), :],
                            send_sems[a], recv_sems[a], (x, y, 1 - c)).start()

    return _split_start("pair_start", list(grads) + zones, n, issue)


def _pair_wait(send_sems, recv_sems, flying, after):
    n = len(flying) // 2
    out = _split_wait("pair_wait", send_sems, recv_sems, flying, lambda refs: list(refs[n:]), after)
    return out[:n], out[n:]


def _chip_start(sums):
    n = len(sums)
    zones = [lax.empty((3, *s.shape[1:]), s.dtype) for s in sums]

    def issue(refs, send_sems, recv_sems):
        x, y, c, chips = _place()
        for a in range(n):
            s_ref, z_ref = refs[a], refs[n + a]
            _, rows, cols = s_ref.shape
            row_bytes = cols * jnp.dtype(s_ref.dtype).itemsize
            for r0, nr in _row_chunks(rows, row_bytes, _align_of(s_ref.dtype)):
                for j, chip in enumerate(chips):
                    _remote(s_ref.at[_chip_index(*chip), pl.ds(r0, nr), :], z_ref.at[j, pl.ds(r0, nr), :],
                            send_sems[a], recv_sems[a], (*chip, c)).start()

    return _split_start("chip_start", list(sums) + zones, n, issue)


def _chip_wait(send_sems, recv_sems, flying, after):
    n = len(flying) // 2
    return _split_wait("chip_wait", send_sems, recv_sems, flying, lambda refs: list(refs[n:]), after)[n:]


def _pair_exchange(grads):
    na = len(grads)

    def body(*refs):
        g_refs, o_refs = refs[:na], refs[na:2 * na]
        send_sems, recv_sems = refs[2 * na:]
        x, y, c, _ = _place()
        sibling = (x, y, 1 - c)
        for ai in range(na):
            _, rows, cols = g_refs[ai].shape
            hr = rows // 2
            for k in range(4):
                for r0, nr in _row_chunks(hr, cols * 4, SUBLANES):
                    _remote(g_refs[ai].at[k, pl.ds((1 - c) * hr + r0, nr), :], o_refs[ai].at[k, pl.ds(r0, nr), :],
                            send_sems.at[ai], recv_sems.at[ai], sibling).start()
        for ai in range(na):
            _remote(o_refs[ai], o_refs[ai], send_sems.at[ai], recv_sems.at[ai], sibling).wait()

    return pl.pallas_call(
        body, in_specs=[ANY] * na, out_specs=[ANY] * na,
        out_shape=[jax.ShapeDtypeStruct((4, g.shape[1] // 2, g.shape[2]), F32) for g in grads],
        scratch_shapes=[pltpu.SemaphoreType.DMA((na,)), pltpu.SemaphoreType.DMA((na,))],
        name="pair_exchange",
    )(*grads)


def _row_tile(rows, cols, align):
    best = align
    for cand in range(align, rows + 1, align):
        if rows % cand == 0 and cand * cols <= 256 * 1024:
            best = cand
    return best


def _pair_sum(g, got, place, transit, name):
    _, rows, cols = g.shape
    hr = rows // 2
    tr = _row_tile(hr, cols, _align_of(transit))
    nt = hr // tr

    def body(p_ref, g_ref, r_ref, s_ref, own_ref):
        total = g_ref[...] + r_ref[...]
        s_ref[...] = total.astype(transit)

        @pl.when(pl.program_id(1) == p_ref[1])
        def _():
            own_ref[...] = total

    grid_spec = pltpu.PrefetchScalarGridSpec(
        num_scalar_prefetch=1, grid=(nt, 4),
        in_specs=[pl.BlockSpec((None, tr, cols), lambda i, k, p: (k, p[0] * nt + i, 0)),
                  pl.BlockSpec((None, tr, cols), lambda i, k, p: (k, i, 0))],
        out_specs=[pl.BlockSpec((None, tr, cols), lambda i, k, p: (k, i, 0)),
                   pl.BlockSpec((tr, cols), lambda i, k, p: (i, 0))])
    return pl.pallas_call(
        body, grid_spec=grid_spec,
        out_shape=[jax.ShapeDtypeStruct((4, hr, cols), transit), jax.ShapeDtypeStruct((hr, cols), F32)],
        name=name, compiler_params=_params(("parallel", "arbitrary")),
    )(place, g, got)


def _chip_exchange(sums):
    na = len(sums)

    def body(*refs):
        s_refs, o_refs = refs[:na], refs[na:2 * na]
        send_sems, recv_sems = refs[2 * na:]
        x, y, c, chips = _place()
        for ai in range(na):
            _, rows, cols = s_refs[ai].shape
            row_bytes = cols * jnp.dtype(s_refs[ai].dtype).itemsize
            for r0, nr in _row_chunks(rows, row_bytes, _align_of(s_refs[ai].dtype)):
                for j, chip in enumerate(chips):
                    _remote(s_refs[ai].at[_chip_index(*chip), pl.ds(r0, nr), :], o_refs[ai].at[j, pl.ds(r0, nr), :],
                            send_sems.at[ai, j], recv_sems.at[ai, j], (*chip, c)).start()
        for ai in range(na):
            for j, chip in enumerate(chips):
                _remote(o_refs[ai].at[j], o_refs[ai].at[j], send_sems.at[ai, j], recv_sems.at[ai, j],
                        (*chip, c)).wait()

    return pl.pallas_call(
        body, in_specs=[ANY] * na, out_specs=[ANY] * na,
        out_shape=[jax.ShapeDtypeStruct((3, *s.shape[1:]), s.dtype) for s in sums],
        scratch_shapes=[pltpu.SemaphoreType.DMA((na, 3)), pltpu.SemaphoreType.DMA((na, 3))],
        name="chip_exchange",
    )(*sums)


def _chip_sum(own, landed, name):
    hr, cols = own.shape
    tr = _row_tile(hr, cols, _align_of(landed.dtype))

    def body(o_ref, l_ref, f_ref):
        acc = o_ref[...]
        for j in range(3):
            acc = acc + l_ref[j].astype(F32)
        f_ref[...] = acc

    return pl.pallas_call(
        body, grid=(hr // tr,),
        in_specs=[pl.BlockSpec((tr, cols), lambda i: (i, 0)), pl.BlockSpec((3, tr, cols), lambda i: (0, i, 0))],
        out_specs=pl.BlockSpec((tr, cols), lambda i: (i, 0)),
        out_shape=jax.ShapeDtypeStruct((hr, cols), F32), name=name,
        compiler_params=_params(("parallel",)),
    )(own, landed)


def _final_exchange(halves, small):
    nh = len(halves)

    def body(*refs):
        h_refs, s_ref = refs[:nh], refs[nh]
        o_refs, so_ref = refs[nh + 1:2 * nh + 1], refs[2 * nh + 1]
        send_sems, recv_sems, local_sem, ssend_sems, srecv_sems = refs[2 * nh + 2:]
        x, y, c, _ = _place()
        me = 4 * x + 2 * y + c
        sibling = (x, y, 1 - c)
        for hi in range(nh):
            hr, cols = h_refs[hi].shape
            for r0, nr in _row_chunks(hr, cols * 4, SUBLANES):
                _remote(h_refs[hi].at[pl.ds(r0, nr), :], o_refs[hi].at[pl.ds(r0, nr), :],
                        send_sems.at[hi], recv_sems.at[hi], sibling).start()
        small_cps = [pltpu.make_async_copy(s_ref, so_ref.at[me], local_sem)]
        for r in range(1, 8):
            fx, fy, fc = (r >> 2) & 1, (r >> 1) & 1, r & 1
            peer = (1 - x if fx else x, 1 - y if fy else y, 1 - c if fc else c)
            small_cps.append(_remote(s_ref, so_ref.at[me], ssend_sems.at[r - 1], srecv_sems.at[r - 1], peer))
        for cp in small_cps:
            cp.start()
        for hi in range(nh):
            _remote(h_refs[hi], o_refs[hi], send_sems.at[hi], recv_sems.at[hi], sibling).wait()
        for cp in small_cps:
            cp.wait()

    return pl.pallas_call(
        body, in_specs=[ANY] * (nh + 1), out_specs=[ANY] * (nh + 1),
        out_shape=[jax.ShapeDtypeStruct(h.shape, F32) for h in halves]
        + [jax.ShapeDtypeStruct((8, *small.shape), F32)],
        scratch_shapes=[pltpu.SemaphoreType.DMA((nh,)), pltpu.SemaphoreType.DMA((nh,)),
                        pltpu.SemaphoreType.DMA, pltpu.SemaphoreType.DMA((7,)), pltpu.SemaphoreType.DMA((7,))],
        name="final_exchange",
    )(*halves, small)


def _adamw_halves(w, own, other, m, v, place, name):
    r, c = w.shape
    hr = r // 2
    tr = _row_tile(hr, c, SUBLANES)
    nt = hr // tr
    c1 = 1.0 - ADAM_B1 ** ADAM_STEP
    c2 = 1.0 - ADAM_B2 ** ADAM_STEP

    def body(p_ref, w_ref, own_ref, other_ref, m_ref, v_ref, g_ref, d_ref, nm_ref, nv_ref):
        mine = pl.program_id(0) // nt == p_ref[0]
        gv = jnp.where(mine, own_ref[...], other_ref[...])
        nm = ADAM_B1 * m_ref[...] + (1.0 - ADAM_B1) * gv
        nv = ADAM_B2 * v_ref[...] + (1.0 - ADAM_B2) * (gv * gv)
        g_ref[...] = gv
        d_ref[...] = -ADAM_LR * ((nm / c1) / (jnp.sqrt(nv / c2) + ADAM_EPS) + ADAM_WD * w_ref[...])
        nm_ref[...] = nm
        nv_ref[...] = nv

    full = pl.BlockSpec((tr, c), lambda i, p: (i, 0))
    half = pl.BlockSpec((tr, c), lambda i, p: (i % nt, 0))
    out = jax.ShapeDtypeStruct((r, c), F32)
    grid_spec = pltpu.PrefetchScalarGridSpec(num_scalar_prefetch=1, grid=(2 * nt,),
                                             in_specs=[full, half, half, full, full], out_specs=[full] * 4)
    return pl.pallas_call(body, grid_spec=grid_spec, out_shape=[out] * 4, name=name,
                          compiler_params=_params(("parallel",)))(place, w, own, other, m, v)


def _adamw_many(ws, gs, ms, vs, name):
    n = len(ws)
    c1 = 1.0 - ADAM_B1 ** ADAM_STEP
    c2 = 1.0 - ADAM_B2 ** ADAM_STEP

    def body(*refs):
        w_refs, g_refs, m_refs, v_refs = (refs[k * n:(k + 1) * n] for k in range(4))
        d_refs, nm_refs, nv_refs = (refs[(4 + k) * n:(5 + k) * n] for k in range(3))
        for i in range(n):
            gv = g_refs[i][...]
            nm = ADAM_B1 * m_refs[i][...] + (1.0 - ADAM_B1) * gv
            nv = ADAM_B2 * v_refs[i][...] + (1.0 - ADAM_B2) * (gv * gv)
            d_refs[i][...] = -ADAM_LR * ((nm / c1) / (jnp.sqrt(nv / c2) + ADAM_EPS) + ADAM_WD * w_refs[i][...])
            nm_refs[i][...] = nm
            nv_refs[i][...] = nv

    vmem = pl.BlockSpec(memory_space=pltpu.VMEM)
    shapes = [jax.ShapeDtypeStruct(t.shape, F32) for t in ws]
    outs = pl.pallas_call(body, in_specs=[vmem] * (4 * n), out_specs=[vmem] * (3 * n), out_shape=shapes * 3,
                          name=name, compiler_params=_params(vmem_mb=56))(*ws, *gs, *ms, *vs)
    return outs[:n], outs[n:2 * n], outs[2 * n:]


BIG = ("w_in", "w_glu", "w_out", "w_up", "w_down")
WEIGHTS = ("norm_mix_g", "w_in", "a_re", "a_im", "log_step", "b_re", "b_im", "c_re", "c_im", "d_skip", "w_glu",
           "sink", "norm_attn_g", "norm_ssm_g", "w_out", "norm_ffn_g", "w_up", "conv_w", "conv_b", "w_down",
           "norm_final_g")
SMALL = ("norm_mix_g", "a_re", "a_im", "log_step", "b_re", "b_im", "c_re", "c_im", "d_skip", "sink",
         "norm_attn_g", "norm_ssm_g", "norm_ffn_g", "conv_w", "conv_b", "norm_final_g")
SMALL_ROWS = 48
N_DEV = 8


def _tile_rows(size):
    return -(-size // (SUBLANES * D_MODEL)) * SUBLANES


def _by_owner(name, g):
    if name == "w_up":
        return g
    return g.reshape(4, g.shape[0] // 4, g.shape[1])


def _view(name, t):
    if name == "w_in":
        return jnp.swapaxes(t[0], 0, 1)
    if name in ("b_re", "b_im"):
        return jnp.swapaxes(t, -1, -2)
    return t


def _unview(name, t):
    if name == "w_in":
        return jnp.swapaxes(t, 0, 1)[None]
    if name in ("b_re", "b_im"):
        return jnp.swapaxes(t, -1, -2)
    return t


def kernel(x, norm_mix_g, w_in, a_re, a_im, log_step, b_re, b_im, c_re, c_im, d_skip, w_glu, sink, norm_attn_g, norm_ssm_g, w_out, norm_ffn_g, w_up, conv_w, conv_b, w_down, norm_final_g, loss_target, m_norm_mix_g, m_w_in, m_a_re, m_a_im, m_log_step, m_b_re, m_b_im, m_c_re, m_c_im, m_d_skip, m_w_glu, m_sink, m_norm_attn_g, m_norm_ssm_g, m_w_out, m_norm_ffn_g, m_w_up, m_conv_w, m_conv_b, m_w_down, m_norm_final_g, v_norm_mix_g, v_w_in, v_a_re, v_a_im, v_log_step, v_b_re, v_b_im, v_c_re, v_c_im, v_d_skip, v_w_glu, v_sink, v_norm_attn_g, v_norm_ssm_g, v_w_out, v_norm_ffn_g, v_w_up, v_conv_w, v_conv_b, v_w_down, v_norm_final_g):
    given = dict(locals())
    w = {n: given[n] for n in WEIGHTS}
    m = {n: given["m_" + n] for n in WEIGHTS}
    v = {n: given["v_" + n] for n in WEIGHTS}
    xy = 2 * lax.axis_index("x") + lax.axis_index("y")

    core = lax.axis_index("c")
    place = jnp.stack([core, xy]).astype(jnp.int32)

    conv_rows = jnp.pad(w["conv_w"][0], ((0, 2 * SUBLANES - 3), (0, 0)))
    rows = lambda t: t.reshape(4 * t.shape[1], t.shape[2])
    (w_in_all,) = _gather_weights([_view("w_in", w["w_in"])], [BF16])
    wb = {"w_in": rows(w_in_all)}
    mixer =[_cast_place(w[n][0], place, BF16, w_in_all, "cast_" + n) for n in ("w_glu", "w_out")]
    mixer.append(_cast_place(conv_rows, place, F32, w_in_all, "cast_conv_w"))
    *mixer_flight, mixer_token = _spread_start(mixer, "spread_mixer_start")
    late = ("w_up", "w_down")
    *late_flight, token = _spread_start(
        [_cast_place(w[n][0], place, BF16, mixer_token, "cast_" + n) for n in late], "spread_ffn_start")

    def mixer_weights(after):
        w_glu4, w_out4, conv4 = _spread_wait(*mixer_flight, after, "spread_mixer_wait")
        return {"w_glu": rows(w_glu4), "w_out": rows(w_out4),
                "conv_w": conv4[:, :3].transpose(1, 0, 2).reshape(3, 2 * D_FF)}

    def late_weights(after):
        w_up4, w_down4 = _spread_wait(*late_flight, after, "spread_ffn_wait")
        return {"w_up": w_up4, "w_down": rows(w_down4)}

    sp = {n: w[n][0] for n in ("a_re", "a_im", "log_step", "b_re", "b_im", "c_re", "c_im", "d_skip",
                               "norm_mix_g", "norm_attn_g", "norm_ssm_g", "norm_ffn_g", "sink", "conv_b")}
    for n in ("norm_mix_g", "norm_attn_g", "norm_ssm_g", "norm_ffn_g", "sink", "conv_b"):
        sp[n] = sp[n].reshape(1, -1)
    sp["norm_mix_g"] = sp["norm_mix_g"] + token[:1, :1]
    sp["norm_final_g"] = w["norm_final_g"]
    early, tail = late + ("w_out",), ("w_in", "w_glu")
    flight = {}

    def grads_ready(dw_up, dw_down, dw_out):
        *flight["pair"], token = _pair_start([dw_up, _by_owner("w_down", dw_down), _by_owner("w_out", dw_out)])
        return token[:1, :1]

    def grads_next(after):
        mine, got = _pair_wait(*flight["pair"], after)
        sums, flight["own"] = zip(*[_pair_sum(a, b, place, BF16, "pair_sum_" + n) for n, a, b in zip(early, mine, got)])
        *flight["chip"], token = _chip_start(list(sums))
        return token[:1, :1]

    loss, grad_x, g = _local_step(x[0], loss_target[0], wb, sp, mixer_weights, late_weights, grads_ready,
                                  grads_next)

    def as_rows(t):
        rows = _tile_rows(t.size)
        return jnp.pad(t.reshape(-1), (0, rows * D_MODEL - t.size)).reshape(rows, D_MODEL)

    pieces = [as_rows(g[n]) for n in SMALL] + [as_rows(loss)]
    spare = N_DEV * SMALL_ROWS - sum(p.shape[0] for p in pieces)
    small = jnp.concatenate(pieces + [jnp.zeros((spare, D_MODEL), F32)]).reshape(4, 2 * SMALL_ROWS, D_MODEL)
    by_owner = [_by_owner(n, g[n]) for n in tail] + [small]
    got = _pair_exchange(by_owner)
    transit = [BF16] * len(tail) + [F32]
    chip_sums, own_sums = zip(*[_pair_sum(a, b, place, t, "pair_sum_" + n)
                                for n, a, b, t in zip(tail + ("small",), by_owner, got, transit)])
    landed = _chip_exchange(list(chip_sums))
    halves = {n: _chip_sum(o, t, "chip_sum_" + n) for n, o, t in zip(tail + ("small",), own_sums, landed)}
    early_landed = _chip_wait(*flight["chip"], grad_x)
    for n, o, t in zip(early, flight["own"], early_landed):
        halves[n] = _chip_sum(o, t, "chip_sum_" + n)
    *others, small_all = _final_exchange([halves[n] for n in BIG], halves["small"])
    small_all = small_all.reshape(N_DEV * SMALL_ROWS, D_MODEL)
    grads, row = {}, 0
    for n in SMALL:
        shape = (3, 4 * w[n].shape[-1]) if n == "conv_w" else w[n].shape[1:] if n != "norm_final_g" else w[n].shape
        size = math.prod(shape)
        grads[n] = small_all[row:row + _tile_rows(size)].reshape(-1)[:size].reshape(shape)
        row += _tile_rows(size)
    loss = small_all[row, 0]
    cw = w["conv_w"].shape[-1]
    grads["conv_w"] = lax.dynamic_slice_in_dim(grads["conv_w"], xy * cw, cw, axis=1)
    grads = {n: _view(n, grads[n].reshape(w[n].shape)) for n in SMALL}
    wv, mv, vv = ({n: _view(n, t[n]) for n in WEIGHTS} for t in (w, m, v))

    delta, new_m, new_v = {}, {}, {}
    for n, other in zip(BIG, others):
        two_d = lambda t: t.reshape(t.shape[-2:])
        grads[n], delta[n], new_m[n], new_v[n] = _adamw_halves(
            two_d(wv[n]), halves[n], other, two_d(mv[n]), two_d(vv[n]), place, "adamw_" + n)
    for group, name in ((("b_re", "b_im"), "adamw_b"), (tuple(n for n in SMALL if n not in ("b_re", "b_im")), "adamw_small")):
        row = lambda t: t.reshape(1, -1) if t.ndim == 1 else t
        d_, m_, v_ = _adamw_many(*[[row(t[n]) for n in group] for t in (wv, grads, mv, vv)], name)
        for n, dn, mn, vn in zip(group, d_, m_, v_):
            delta[n], new_m[n], new_v[n] = (t.reshape(wv[n].shape) for t in (dn, mn, vn))
    natural = lambda t: [_unview(n, t[n].reshape(wv[n].shape)) for n in WEIGHTS]
    return (loss, grad_x[None], *natural(grads), *natural(delta), *natural(new_m), *natural(new_v))
```

```python
import functools
import math

import jax
import jax.numpy as jnp
import numpy as np
from jax import lax
from jax.experimental import pallas as pl
from jax.experimental.pallas import tpu as pltpu

F32 = jnp.float32
BF16 = jnp.bfloat16

D_MODEL = 1024
N_Q_HEADS = 8
N_KV_HEADS = 2
HEAD_DIM = 64
ATTN_WIDTH = 512
KV_WIDTH = 128
QKV_WIDTH = ATTN_WIDTH + 2 * KV_WIDTH
WINDOW = 128
BLOCK = 128
ROPE_DIM = 16
ROPE_THETA = 500000.0
SCORE_SCALE = HEAD_DIM ** -0.5
SSM_WIDTH = 512
SSM_GROUP = 16
N_SSM_GROUPS = 32
SSM_STATE = 64
IN_WIDTH = 1280
D_FF = 2816
EPS = 1e-6
ADAM_LR = 0.001
ADAM_B1 = 0.9
ADAM_B2 = 0.999
ADAM_EPS = 1e-08
ADAM_WD = 0.01
ADAM_STEP = 10

VMEM_BYTES_V7X = 64 * 1024 * 1024
SUBLANES = 8
LANES = 128
SSM_CB = 4
SSM_CH = 128
SSM_ST = 512
N_SEG = SUBLANES

NN = (((1,), (0,)), ((), ()))
NT = (((1,), (1,)), ((), ()))
TN = (((0,), (0,)), ((), ()))


def _params(sem=None, vmem_mb=48):
    limit = vmem_mb * 1024 * 1024
    assert limit < VMEM_BYTES_V7X
    return pltpu.CompilerParams(dimension_semantics=sem, vmem_limit_bytes=limit)


def _dg(a, b, dims):
    return lax.dot_general(a, b, dims, preferred_element_type=F32)


def _sigmoid(x):
    return 1.0 / (1.0 + jnp.exp(-x))


_SQRT_HALF = 0.7071067811865476
_INV_SQRT_2PI = 0.3989422804014327


def _gelu(x):
    return 0.5 * x * (1.0 + lax.erf(x * _SQRT_HALF))


def _gelu_grad(x):
    return 0.5 * (1.0 + lax.erf(x * _SQRT_HALF)) + x * (_INV_SQRT_2PI * jnp.exp(-0.5 * x * x))


def _mm_tn(a, b, tm, tn, name):
    k, m = a.shape
    n = b.shape[1]

    def body(a_ref, b_ref, o_ref):
        o_ref[...] = _dg(a_ref[...], b_ref[...], TN)

    return pl.pallas_call(
        body, grid=(m // tm, n // tn),
        in_specs=[pl.BlockSpec((k, tm), lambda i, j: (0, i)), pl.BlockSpec((k, tn), lambda i, j: (0, j))],
        out_specs=pl.BlockSpec((tm, tn), lambda i, j: (i, j)),
        out_shape=jax.ShapeDtypeStruct((m, n), F32), name=name,
        compiler_params=_params(("parallel", "parallel")),
    )(a, b)


def _mm_nn_cols(a, b4, tm, name):
    m, k = a.shape
    s, _, n = b4.shape

    def body(a_ref, b_ref, o_ref):
        o_ref[...] = _dg(a_ref[...], b_ref[...], NN)

    return pl.pallas_call(
        body, grid=(m // tm, s),
        in_specs=[pl.BlockSpec((tm, k), lambda i, j: (i, 0)), pl.BlockSpec((None, k, n), lambda i, j: (j, 0, 0))],
        out_specs=pl.BlockSpec((tm, n), lambda i, j: (i, j)),
        out_shape=jax.ShapeDtypeStruct((m, s * n), F32), name=name,
        compiler_params=_params(("parallel", "parallel")),
    )(a, b4)


def _mm_tn_cols(a, b2, s, tm, name):
    k, m = a.shape
    h, _, wide = b2.shape
    per = s // h
    n = wide // per

    def body(a_ref, b_ref, o_ref):
        o_ref[...] = _dg(a_ref[...], b_ref[...], TN)

    return pl.pallas_call(
        body, grid=(s, m // tm),
        in_specs=[pl.BlockSpec((k, tm), lambda j, i: (0, i)),
                  pl.BlockSpec((None, k, n), lambda j, i: (j // per, 0, j % per))],
        out_specs=pl.BlockSpec((None, tm, n), lambda j, i: (j, i, 0)),
        out_shape=jax.ShapeDtypeStruct((s, m, n), F32), name=name,
        compiler_params=_params(("parallel", "parallel")),
    )(a, b2)


TM_EW = 256


def _rms_bwd_vals(xv, gv, dy):
    r = lax.rsqrt(jnp.mean(xv * xv, axis=-1, keepdims=True) + EPS)
    xh = xv * r
    dxh = dy * gv
    dx = r * (dxh - xh * jnp.mean(dxh * xh, axis=-1, keepdims=True))
    return dx, dy * xh


TM_FUSED = 512
TM_LOSS = 256


def _rms_vals(xv, gv):
    return xv * lax.rsqrt(jnp.mean(xv * xv, axis=-1, keepdims=True) + EPS) * gv


def _rope_blocks(src, dst, c, lo, hi):
    nq = ATTN_WIDTH // LANES
    for blk in range(nq + 1):
        t = src[:, blk * LANES:(blk + 1) * LANES]
        rot = t * c + pltpu.roll(t, LANES - 8, 1) * lo + pltpu.roll(t, 8, 1) * hi
        dst[:, blk * LANES:(blk + 1) * LANES] = (rot * SCORE_SCALE if blk < nq else rot).astype(BF16)
    dst[:, (nq + 1) * LANES:] = src[:, (nq + 1) * LANES:].astype(BF16)


def _rms_mm_rope(x, g, wt, tabs, name):
    l, d = x.shape
    n = wt.shape[0]

    def body(x_ref, g_ref, w_ref, c_ref, lo_ref, hi_ref, h_ref, qkv_ref, u_ref):
        h = _rms_vals(x_ref[...], g_ref[...]).astype(BF16)
        h_ref[...] = h
        out = _dg(h, w_ref[...], NT)
        _rope_blocks(out[:, :QKV_WIDTH], qkv_ref, c_ref[...], lo_ref[...], hi_ref[...])
        u_ref[...] = out[:, QKV_WIDTH:]

    row = lambda width: pl.BlockSpec((TM_FUSED, width), lambda i: (i, 0))
    return pl.pallas_call(
        body, grid=(l // TM_FUSED,),
        in_specs=[row(d), pl.BlockSpec((1, d), lambda i: (0, 0)), pl.BlockSpec((n, d), lambda i: (0, 0)),
                  row(LANES), row(LANES), row(LANES)],
        out_specs=[row(d), row(QKV_WIDTH), row(n - QKV_WIDTH)],
        out_shape=[jax.ShapeDtypeStruct((l, d), BF16), jax.ShapeDtypeStruct((l, QKV_WIDTH), BF16),
                   jax.ShapeDtypeStruct((l, n - QKV_WIDTH), F32)],
        name=name, compiler_params=_params(("parallel",)),
    )(x, g, wt, *tabs)


def _mix_mm_res_rms(attn, ys, g_attn, g_ssm, b, res, g, name):
    l, w = attn.shape
    d = b.shape[1]

    def body(a_ref, y_ref, ga_ref, gs_ref, b_ref, r_ref, g_ref, m_ref, x_ref, h_ref):
        m_ref[:, :w] = _rms_vals(a_ref[...], ga_ref[...]).astype(BF16)
        m_ref[:, w:] = _rms_vals(y_ref[...], gs_ref[...]).astype(BF16)
        xv = r_ref[...] + _dg(m_ref[...], b_ref[...], NN)
        x_ref[...] = xv
        h_ref[...] = _rms_vals(xv, g_ref[...]).astype(BF16)

    row = lambda width: pl.BlockSpec((TM_FUSED, width), lambda i: (i, 0))
    vec = lambda width: pl.BlockSpec((1, width), lambda i: (0, 0))
    return pl.pallas_call(
        body, grid=(l // TM_FUSED,),
        in_specs=[row(w), row(w), vec(w), vec(w), pl.BlockSpec((2 * w, d), lambda i: (0, 0)), row(d), vec(d)],
        out_specs=[row(2 * w), row(d), row(d)],
        out_shape=[jax.ShapeDtypeStruct((l, 2 * w), BF16), jax.ShapeDtypeStruct((l, d), F32),
                   jax.ShapeDtypeStruct((l, d), BF16)],
        name=name, compiler_params=_params(("parallel",)),
    )(attn, ys, g_attn, g_ssm, b, res, g)


def _mm_res_loss(a, b, res, g, target):
    l, k = a.shape
    d = b.shape[1]

    def body(a_ref, b_ref, r_ref, g_ref, t_ref, loss_ref, dx_ref, dxb_ref, dg_ref):
        xv = r_ref[...] + _dg(a_ref[...], b_ref[...], NN)
        gv = g_ref[...]
        r = lax.rsqrt(jnp.mean(xv * xv, axis=-1, keepdims=True) + EPS)
        xh = xv * r
        e = xh * gv - t_ref[...]
        part = jnp.sum(jnp.sum(e * e, axis=1, keepdims=True), axis=0, keepdims=True) * (0.5 / d)
        dy = e * (1.0 / d)
        dxh = dy * gv
        dx = r * (dxh - xh * jnp.mean(dxh * xh, axis=-1, keepdims=True))
        dx_ref[...] = dx
        dxb_ref[...] = dx.astype(BF16)

        @pl.when(pl.program_id(0) == 0)
        def _():
            dg_ref[...] = jnp.zeros_like(dg_ref)
            loss_ref[...] = jnp.zeros_like(loss_ref)

        dg_ref[...] += jnp.sum(dy * xh, axis=0, keepdims=True)
        loss_ref[...] += part

    row = lambda width: pl.BlockSpec((TM_LOSS, width), lambda i: (i, 0))
    vec = pl.BlockSpec((1, d), lambda i: (0, 0))
    return pl.pallas_call(
        body, grid=(l // TM_LOSS,),
        in_specs=[row(k), pl.BlockSpec((k, d), lambda i: (0, 0)), row(d), vec, row(d)],
        out_specs=[pl.BlockSpec((1, 1), lambda i: (0, 0)), row(d), row(d), vec],
        out_shape=[jax.ShapeDtypeStruct((1, 1), F32), jax.ShapeDtypeStruct((l, d), F32),
                   jax.ShapeDtypeStruct((l, d), BF16), jax.ShapeDtypeStruct((1, d), F32)],
        name="mm_down_loss", compiler_params=_params(("arbitrary",)),
    )(a, b, res, g, target)


def _mm_rms_bwd(a, b, a_spec, b_spec, matmul, x, g, res, name):
    l, d = x.shape

    def body(a_ref, b_ref, x_ref, g_ref, res_ref, dx_ref, dxb_ref, dg_ref):
        dx, dgr = _rms_bwd_vals(x_ref[...], g_ref[...], matmul(a_ref, b_ref))
        dx = dx + res_ref[...]
        dx_ref[...] = dx
        dxb_ref[...] = dx.astype(BF16)

        @pl.when(pl.program_id(0) == 0)
        def _():
            dg_ref[...] = jnp.zeros_like(dg_ref)

        dg_ref[...] += jnp.sum(dgr, axis=0, keepdims=True)

    row = pl.BlockSpec((TM_FUSED, d), lambda i: (i, 0))
    vec = pl.BlockSpec((1, d), lambda i: (0, 0))
    return pl.pallas_call(
        body, grid=(l // TM_FUSED,), in_specs=[a_spec, b_spec, row, vec, row], out_specs=[row, row, vec],
        out_shape=[jax.ShapeDtypeStruct((l, d), F32), jax.ShapeDtypeStruct((l, d), BF16),
                   jax.ShapeDtypeStruct((1, d), F32)],
        name=name, compiler_params=_params(("arbitrary",)),
    )(a, b, x, g, res)


def _mm_nn_rms_bwd(a, b, x, g, res, name):
    return _mm_rms_bwd(a, b, pl.BlockSpec((TM_FUSED, a.shape[1]), lambda i: (i, 0)),
                       pl.BlockSpec(b.shape, lambda i: (0, 0)),
                       lambda a_ref, b_ref: _dg(a_ref[...], b_ref[...], NN), x, g, res, name)


def _mm_cols_rms_bwd(a2, b4, x, g, res, name):
    h, _, wide = a2.shape
    s, _, n = b4.shape
    per = s // h

    def matmul(a_ref, b_ref):
        acc = None
        for j in range(s):
            part = _dg(a_ref[j // per, :, (j % per) * n:(j % per + 1) * n], b_ref[j], NT)
            acc = part if acc is None else acc + part
        return acc

    return _mm_rms_bwd(a2, b4, pl.BlockSpec((h, TM_FUSED, wide), lambda i: (0, i, 0)),
                       pl.BlockSpec(b4.shape, lambda i: (0, 0, 0), pipeline_mode=pl.Buffered(1)),
                       matmul, x, g, res, name)


def _mm_mix_bwd(dx, b, attn, ys, g_attn, g_ssm, name):
    l, w = attn.shape
    d = dx.shape[1]

    def body(dx_ref, b_ref, a_ref, y_ref, ga_ref, gs_ref, da_ref, dy_ref, dga_ref, dgs_ref):
        @pl.when(pl.program_id(0) == 0)
        def _():
            dga_ref[...] = jnp.zeros_like(dga_ref)
            dgs_ref[...] = jnp.zeros_like(dgs_ref)

        dm = _dg(dx_ref[...], b_ref[...], NT)
        for src, gr, off, dst, dgr in ((a_ref, ga_ref, 0, da_ref, dga_ref), (y_ref, gs_ref, w, dy_ref, dgs_ref)):
            dxv, dg_rows = _rms_bwd_vals(src[...], gr[...], dm[:, off:off + w])
            dst[...] = dxv
            dgr[...] += jnp.sum(dg_rows, axis=0, keepdims=True)

    row = lambda width: pl.BlockSpec((TM_FUSED, width), lambda i: (i, 0))
    vec = pl.BlockSpec((1, w), lambda i: (0, 0))
    return pl.pallas_call(
        body, grid=(l // TM_FUSED,),
        in_specs=[row(d), pl.BlockSpec((2 * w, d), lambda i: (0, 0)), row(w), row(w), vec, vec],
        out_specs=[row(w), row(w), vec, vec],
        out_shape=[jax.ShapeDtypeStruct((l, w), F32), jax.ShapeDtypeStruct((l, w), F32),
                   jax.ShapeDtypeStruct((1, w), F32), jax.ShapeDtypeStruct((1, w), F32)],
        name=name, compiler_params=_params(("arbitrary",)),
    )(dx, b, attn, ys, g_attn, g_ssm)


def _rope_tables(l):
    half = ROPE_DIM // 2
    f32 = np.float32
    inv_freq = np.power(f32(ROPE_THETA), -np.arange(half, dtype=f32) / f32(half))
    ang = np.arange(l, dtype=f32)[:, None] * inv_freq[None, :]
    cos, sin = np.cos(ang), np.sin(ang)
    ones = np.ones((l, HEAD_DIM - ROPE_DIM), f32)
    zeros = np.zeros((l, HEAD_DIM - ROPE_DIM), f32)
    zh = np.zeros((l, half), f32)
    c = np.concatenate([cos, cos, ones], axis=1)
    s_lo = np.concatenate([-sin, zh, zeros], axis=1)
    s_hi = np.concatenate([zh, sin, zeros], axis=1)
    return tuple(jnp.asarray(np.tile(t, (1, LANES // HEAD_DIM)), F32) for t in (c, s_lo, s_hi))


def _rope_bwd(dq, dkv, du_ssm, dpre, d_skip, tabs):
    l = dq.shape[0]
    nq = ATTN_WIDTH // LANES

    def body(dq_ref, dkv_ref, du_ref, dpre_ref, ds_ref, c_ref, lo_ref, hi_ref, o_ref):
        c, lo, hi = c_ref[...], lo_ref[...], hi_ref[...]
        for blk in range(nq + 1):
            t = dq_ref[:, blk * LANES:(blk + 1) * LANES] if blk < nq else dkv_ref[:, :KV_WIDTH]
            g = t * c + pltpu.roll(t * lo, 8, 1) + pltpu.roll(t * hi, LANES - 8, 1)
            o_ref[:, blk * LANES:(blk + 1) * LANES] = g.astype(BF16)
        o_ref[:, (nq + 1) * LANES:QKV_WIDTH] = dkv_ref[:, KV_WIDTH:].astype(BF16)
        o_ref[:, QKV_WIDTH:] = (du_ref[...] + dpre_ref[...] * ds_ref[...]).astype(BF16)

    tab = pl.BlockSpec((TM_EW, LANES), lambda i: (i, 0))
    wide = pl.BlockSpec((TM_EW, SSM_WIDTH), lambda i: (i, 0))
    return pl.pallas_call(
        body, grid=(l // TM_EW,),
        in_specs=[wide, pl.BlockSpec((TM_EW, 2 * KV_WIDTH), lambda i: (i, 0)), wide, wide,
                  pl.BlockSpec((1, SSM_WIDTH), lambda i: (0, 0)), tab, tab, tab],
        out_specs=pl.BlockSpec((TM_EW, IN_WIDTH), lambda i: (i, 0)),
        out_shape=jax.ShapeDtypeStruct((l, IN_WIDTH), BF16), name="rope_bwd",
        compiler_params=_params(("parallel",)),
    )(dq, dkv, du_ssm, dpre, d_skip, *tabs)


_Q_COLS = ATTN_WIDTH // LANES
_NEG = -1e30


def _window_specs(nb, width, col):
    return [
        pl.BlockSpec((BLOCK, width), lambda n: (jnp.maximum(n - 1, 0), col)),
        pl.BlockSpec((BLOCK, width), lambda n: (n, col)),
        pl.BlockSpec((BLOCK, width), lambda n: (jnp.minimum(n + 1, nb - 1), col)),
    ]


def _stacked_sink(sink_ref, heads):
    rid = lax.broadcasted_iota(jnp.int32, (len(heads) * BLOCK, 1), 0)
    sk = jnp.full(rid.shape, sink_ref[0, heads[-1]], F32)
    for g in range(len(heads) - 2, -1, -1):
        sk = jnp.where(rid < (g + 1) * BLOCK, sink_ref[0, heads[g]], sk)
    return sk


def _attn_fwd(qkv, sink):
    l = qkv.shape[0]
    nb = l // BLOCK
    grp = N_Q_HEADS // N_KV_HEADS

    def body(sink_ref, q_ref, k0, k1, k2, v0, v1, v2, o_ref, lse_ref):
        n = pl.program_id(0)
        q = q_ref[...]
        kw = jnp.concatenate([k0[...], k1[...], k2[...]], axis=0)
        vw = jnp.concatenate([v0[...], v1[...], v2[...]], axis=0)
        row = lax.broadcasted_iota(jnp.int32, (grp * BLOCK, 3 * BLOCK), 0)
        col = lax.broadcasted_iota(jnp.int32, (grp * BLOCK, 3 * BLOCK), 1)
        valid = jnp.abs(col - BLOCK - (row & (BLOCK - 1))) <= WINDOW
        valid &= jnp.logical_not((n == 0) & (col < BLOCK))
        valid &= jnp.logical_not((n == nb - 1) & (col >= 2 * BLOCK))
        for hk in range(N_KV_HEADS):
            heads = range(hk * grp, (hk + 1) * grp)
            qs = jnp.concatenate([q[:, h * HEAD_DIM:(h + 1) * HEAD_DIM] for h in heads], axis=0)
            kh = kw[:, hk * HEAD_DIM:(hk + 1) * HEAD_DIM]
            vh = vw[:, hk * HEAD_DIM:(hk + 1) * HEAD_DIM]
            s = jnp.where(valid, _dg(qs, kh, NT), _NEG)
            sk = _stacked_sink(sink_ref, heads)
            m = jnp.maximum(jnp.max(s, axis=1, keepdims=True), sk)
            p = jnp.exp(s - m)
            denom = jnp.sum(p, axis=1, keepdims=True) + jnp.exp(sk - m)
            o = _dg((p / denom).astype(BF16), vh, NN)
            lse = m + jnp.log(denom)
            for g, h in enumerate(heads):
                o_ref[:, h * HEAD_DIM:(h + 1) * HEAD_DIM] = o[g * BLOCK:(g + 1) * BLOCK]
                lse_ref[:, h:h + 1] = lse[g * BLOCK:(g + 1) * BLOCK]

    return pl.pallas_call(
        body, grid=(nb,),
        in_specs=[pl.BlockSpec(memory_space=pltpu.SMEM),
                  pl.BlockSpec((BLOCK, ATTN_WIDTH), lambda n: (n, 0))]
        + _window_specs(nb, KV_WIDTH, _Q_COLS) + _window_specs(nb, KV_WIDTH, _Q_COLS + 1),
        out_specs=[pl.BlockSpec((BLOCK, ATTN_WIDTH), lambda n: (n, 0)),
                   pl.BlockSpec((BLOCK, N_Q_HEADS), lambda n: (n, 0))],
        out_shape=[jax.ShapeDtypeStruct((l, ATTN_WIDTH), F32), jax.ShapeDtypeStruct((l, N_Q_HEADS), F32)],
        name="attn_fwd", compiler_params=_params(("parallel",)),
    )(sink, qkv, qkv, qkv, qkv, qkv, qkv, qkv)


def _attn_bwd(qkv, attn, dattn, lse, sink):
    l = qkv.shape[0]
    nb = l // BLOCK
    grp = N_Q_HEADS // N_KV_HEADS
    win = 3 * BLOCK

    def body(sink_ref, q_ref, k0, k1, k2, v0, v1, v2, o_ref, d_ref, l_ref, dq_ref, dkv_ref, dsink_ref, ring_ref):
        n = pl.program_id(0)

        @pl.when(n == 0)
        def _():
            dsink_ref[...] = jnp.zeros_like(dsink_ref)
            ring_ref[...] = jnp.zeros_like(ring_ref)

        @pl.when(n < nb)
        def _():
            first, last = n == 0, n == nb - 1
            cat = lambda a, b, c: jnp.concatenate([a[...], b[...], c[...]], axis=0)
            q, kw, vw = q_ref[...], cat(k0, k1, k2), cat(v0, v1, v2)
            dov = d_ref[...]
            prod = o_ref[...] * dov
            dob = dov.astype(BF16)
            lse = l_ref[...]
            row = lax.broadcasted_iota(jnp.int32, (grp * BLOCK, win), 0)
            col = lax.broadcasted_iota(jnp.int32, (grp * BLOCK, win), 1)
            valid = jnp.abs(col - BLOCK - (row & (BLOCK - 1))) <= WINDOW
            valid &= jnp.logical_not(first & (col < BLOCK))
            valid &= jnp.logical_not(last & (col >= 2 * BLOCK))

            dsink_parts, dks, dvs = [], [], []
            for hk in range(N_KV_HEADS):
                heads = range(hk * grp, (hk + 1) * grp)
                ksl = slice(hk * HEAD_DIM, (hk + 1) * HEAD_DIM)
                hsl = [slice(h * HEAD_DIM, (h + 1) * HEAD_DIM) for h in heads]
                stack = lambda parts: jnp.concatenate(parts, axis=0)
                qs = stack([q[:, s_] for s_ in hsl])
                dos = stack([dob[:, s_] for s_ in hsl])
                deltas = stack([jnp.sum(prod[:, s_], axis=1, keepdims=True) for s_ in hsl])
                lses = stack([lse[:, h:h + 1] for h in heads])
                kh, vh = kw[:, ksl], vw[:, ksl]
                s = jnp.where(valid, _dg(qs, kh, NT), _NEG)
                p = jnp.exp(s - lses)
                dp = _dg(dos, vh, NT)
                ds = (p * (dp - deltas)).astype(BF16)
                dq = _dg(ds, kh, NN) * SCORE_SCALE
                sink_rows = jnp.exp(_stacked_sink(sink_ref, heads) - lses) * deltas
                for g in range(grp):
                    dq_ref[:, hsl[g]] = dq[g * BLOCK:(g + 1) * BLOCK]
                    dsink_parts.append(jnp.sum(sink_rows[g * BLOCK:(g + 1) * BLOCK], axis=0, keepdims=True))
                dks.append(_dg(ds, qs, TN))
                dvs.append(_dg(p.astype(BF16), dos, TN))
            dsink_ref[...] -= jnp.concatenate(dsink_parts, axis=1)
            part = jnp.concatenate(dks + dvs, axis=1)
            ring_ref[(n + 2) % 3] += part[0:BLOCK]
            ring_ref[n % 3] += part[BLOCK:2 * BLOCK]
            ring_ref[(n + 1) % 3] = part[2 * BLOCK:]

        @pl.when(n >= 1)
        def _():
            dkv_ref[...] = ring_ref[(n + 2) % 3]

    centre = lambda n: jnp.minimum(n, nb - 1)
    window = lambda width, col: [
        pl.BlockSpec((BLOCK, width), lambda n: (jnp.maximum(centre(n) - 1, 0), col)),
        pl.BlockSpec((BLOCK, width), lambda n: (centre(n), col)),
        pl.BlockSpec((BLOCK, width), lambda n: (jnp.minimum(centre(n) + 1, nb - 1), col))]
    own = lambda width: pl.BlockSpec((BLOCK, width), lambda n: (centre(n), 0))
    return pl.pallas_call(
        body, grid=(nb + 1,),
        in_specs=[pl.BlockSpec(memory_space=pltpu.SMEM), own(ATTN_WIDTH)]
        + window(KV_WIDTH, _Q_COLS) + window(KV_WIDTH, _Q_COLS + 1)
        + [own(ATTN_WIDTH), own(ATTN_WIDTH), own(N_Q_HEADS)],
        out_specs=[own(ATTN_WIDTH), pl.BlockSpec((BLOCK, 2 * KV_WIDTH), lambda n: (jnp.maximum(n - 1, 0), 0)),
                   pl.BlockSpec((1, N_Q_HEADS), lambda n: (0, 0))],
        out_shape=[jax.ShapeDtypeStruct((l, ATTN_WIDTH), F32), jax.ShapeDtypeStruct((l, 2 * KV_WIDTH), F32),
                   jax.ShapeDtypeStruct((1, N_Q_HEADS), F32)],
        scratch_shapes=[pltpu.VMEM((3, BLOCK, 2 * KV_WIDTH), F32)],
        name="attn_bwd", compiler_params=_params(("arbitrary",)),
    )(sink, qkv, qkv, qkv, qkv, qkv, qkv, qkv, attn, dattn, lse)


def _ssm_disc(a_re, a_im, log_step, b_re, b_im):
    step = jnp.exp(log_step)[..., None]
    mag = jnp.exp(a_re * step)
    lb_re, lb_im = mag * jnp.cos(a_im * step), mag * jnp.sin(a_im * step)
    nr, ni = lb_re - 1.0, lb_im
    den = a_re * a_re + a_im * a_im
    f_re = ((nr * a_re + ni * a_im) / den)[..., None]
    f_im = ((ni * a_re - nr * a_im) / den)[..., None]
    return lb_re, lb_im, f_re * b_re - f_im * b_im, f_re * b_im + f_im * b_re


def _ssm_pack(lb_re, lb_im, bb_re, bb_im, c_re, c_im):
    eye = jnp.eye(SSM_CH // SSM_GROUP, dtype=F32)
    ng = SSM_CH // SSM_GROUP

    def diag_b(bb):
        t = bb.reshape(2, SSM_CB, ng, SSM_STATE, SSM_GROUP)
        return jnp.einsum('dkgpc,gh->dkgchp', t, eye).reshape(2, SSM_CB, SSM_CH, SSM_ST)

    def diag_c(cc):
        t = cc.reshape(2, SSM_CB, ng, SSM_GROUP, SSM_STATE)
        return jnp.einsum('dkgcp,gh->dkhpgc', t, eye).reshape(2, SSM_CB, SSM_ST, SSM_CH)

    bcat = jnp.concatenate([diag_b(bb_re), diag_b(bb_im)], axis=-1)
    ccat = jnp.concatenate([diag_c(c_re), -diag_c(c_im)], axis=-2)
    lam_re = lb_re.reshape(2, SSM_CB, 1, SSM_ST)
    lam_im = lb_im.reshape(2, SSM_CB, 1, SSM_ST)
    return bcat, ccat, lam_re, lam_im


def _ssm_unpack(dbcat, dccat, dlam_re, dlam_im):
    ng = SSM_CH // SSM_GROUP
    eye = jnp.eye(ng, dtype=F32)

    def undiag_b(t):
        t = t.reshape(2, SSM_CB, ng, SSM_GROUP, ng, SSM_STATE)
        return jnp.einsum('dkgchp,gh->dkgpc', t, eye).reshape(2, N_SSM_GROUPS, SSM_STATE, SSM_GROUP)

    def undiag_c(t):
        t = t.reshape(2, SSM_CB, ng, SSM_STATE, ng, SSM_GROUP)
        return jnp.einsum('dkhpgc,gh->dkgcp', t, eye).reshape(2, N_SSM_GROUPS, SSM_GROUP, SSM_STATE)

    dbb_re, dbb_im = undiag_b(dbcat[..., :SSM_ST]), undiag_b(dbcat[..., SSM_ST:])
    dc_re, dc_im = undiag_c(dccat[:, :, :SSM_ST]), -undiag_c(dccat[:, :, SSM_ST:])
    shape = (2, N_SSM_GROUPS, SSM_STATE)
    return dlam_re.reshape(shape), dlam_im.reshape(shape), dbb_re, dbb_im, dc_re, dc_im


def _to_segments(t):
    l, w = t.shape
    return t.reshape(N_SEG, l // N_SEG, w).transpose(1, 0, 2).reshape(l, w)


def _from_segments(t):
    l, w = t.shape
    return t.reshape(l // N_SEG, N_SEG, w).transpose(1, 0, 2).reshape(l, w)


SSM_RC = 256
SSM_JC = SSM_RC // N_SEG
_RE, _IM = pl.ds(0, SSM_ST), pl.ds(SSM_ST, SSM_ST)


def _cfma(ar, ai, xr, xi, br, bi):
    return ar * xr - ai * xi + br, ar * xi + ai * xr + bi


def _chunk_rows(ci, rev, nc):
    start = jnp.where(rev, (nc - 1 - ci) * SSM_RC, ci * SSM_RC)
    return pl.ds(pl.multiple_of(start, SSM_RC), SSM_RC)


def _scan_chunk(src, dst, ar, ai, rev, nj, ci, carry, prev_ref=None):
    def rows_of(staged, j, k):
        at = jnp.where(rev, SSM_JC - 1 - k, k) if staged else j
        return pl.ds(pl.multiple_of(at * N_SEG, N_SEG), N_SEG)

    for k in range(SSM_JC):
        jj = ci * SSM_JC + k
        j = jnp.where(rev, nj - 1 - jj, jj)
        rows = rows_of(src[1], j, k)
        nr, ni = _cfma(ar, ai, carry[0], carry[1], src[0][rows, _RE], src[0][rows, _IM])
        if dst is not None:
            rows = rows_of(dst[1], j, k)
            dst[0][rows, _RE] = nr
            dst[0][rows, _IM] = ni
        if prev_ref is None:
            carry = (nr, ni)
            continue
        jp = jnp.where(rev, j - 1, j + 1)
        if k == SSM_JC - 1:
            inside = jnp.where((jp >= 0) & (jp < nj), 1.0, 0.0)
            jp = jnp.clip(jp, 0, nj - 1)
        prow = pl.ds(pl.multiple_of(jp * N_SEG, N_SEG), N_SEG)
        xr, xi = prev_ref[prow, _RE], prev_ref[prow, _IM]
        sr, si = nr * xr + ni * xi, ni * xr - nr * xi
        if k == SSM_JC - 1:
            sr, si = inside * sr, inside * si
        carry = (nr, ni, carry[2] + sr, carry[3] + si)
    return carry


def _segment_inits(ar, ai, end_r, end_i, rev, nj):
    pr, pi = ar, ai
    for _ in range(int(math.log2(nj))):
        pr, pi = pr * pr - pi * pi, 2.0 * pr * pi
    seg = lax.broadcasted_iota(jnp.int32, end_r.shape, 0)
    zero = jnp.zeros_like(end_r)

    def chain(shift, keep):
        ir, ii = zero, zero
        for _ in range(N_SEG - 1):
            tr, ti = _cfma(pr, pi, ir, ii, end_r, end_i)
            ir = jnp.where(keep, pltpu.roll(tr, shift, 0), 0.0)
            ii = jnp.where(keep, pltpu.roll(ti, shift, 0), 0.0)
        return ir, ii

    up_r, up_i = chain(1, seg >= 1)
    dn_r, dn_i = chain(N_SEG - 1, seg <= N_SEG - 2)
    return jnp.where(rev, dn_r, up_r), jnp.where(rev, dn_i, up_i)


def _ssm_specs(l):
    act = pl.BlockSpec((l, SSM_CH), lambda k, d: (0, k))
    bmat = pl.BlockSpec((None, None, SSM_CH, 2 * SSM_ST), lambda k, d: (d, k, 0, 0))
    cmat = pl.BlockSpec((None, None, 2 * SSM_ST, SSM_CH), lambda k, d: (d, k, 0, 0))
    lam = pl.BlockSpec((None, None, 1, SSM_ST), lambda k, d: (d, k, 0, 0))
    return act, bmat, cmat, lam


def _ssm_fwd(u_seg, bcat, ccat, lam_re, lam_im):
    l = u_seg.shape[0]
    nj = l // N_SEG
    nc = l // SSM_RC

    def body(u_ref, b_ref, c_ref, lr_ref, li_ref, y_ref, keep_ref, xs_ref, stage0, stage1, keep_sem):
        k, d = pl.program_id(0), pl.program_id(1)
        rev = d == 1
        shape = (N_SEG, SSM_ST)
        ar, ai = jnp.broadcast_to(lr_ref[...], shape), jnp.broadcast_to(li_ref[...], shape)
        zero = jnp.zeros(shape, F32)

        def inputs(ci, stage):
            rows = _chunk_rows(ci, rev, nc)
            bu = _dg(u_ref[rows, :].astype(BF16), b_ref[...], NN)
            stage[...] = bu
            xs_ref[rows, :] = bu

        def first(stage, ci, carry):
            return _scan_chunk((stage, True), None, ar, ai, rev, nj, ci, carry)

        def first_pass(t, carry):
            inputs(2 * t + 1, stage1)
            carry = first(stage0, 2 * t, carry)
            inputs(2 * t + 2, stage0)
            return first(stage1, 2 * t + 1, carry)

        inputs(0, stage0)
        carry = lax.fori_loop(0, nc // 2 - 1, first_pass, (zero, zero))
        inputs(nc - 1, stage1)
        carry = first(stage0, nc - 2, carry)
        end_r, end_i = first(stage1, nc - 1, carry)
        init = _segment_inits(ar, ai, end_r, end_i, rev, nj)

        @pl.when(d == 0)
        def _():
            y_ref[...] = jnp.zeros_like(y_ref)

        def outputs(ci):
            rows = _chunk_rows(ci, rev, nc)
            y_ref[rows, :] += _dg(xs_ref[rows, :].astype(BF16), c_ref[...], NN)
            pltpu.make_async_copy(xs_ref.at[rows], keep_ref.at[d, k, rows], keep_sem).start()

        def second(ci, carry):
            return _scan_chunk((xs_ref, False), (xs_ref, False), ar, ai, rev, nj, ci, carry)

        def second_pass(ci, carry):
            outputs(ci - 1)
            return second(ci, carry)

        lax.fori_loop(1, nc, second_pass, second(0, init))
        outputs(nc - 1)
        pltpu.make_async_copy(xs_ref, keep_ref.at[d, k], keep_sem).wait()

    act, bmat, cmat, lam = _ssm_specs(l)
    return pl.pallas_call(
        body, grid=(SSM_CB, 2), in_specs=[act, bmat, cmat, lam, lam], out_specs=[act, ANY],
        out_shape=[jax.ShapeDtypeStruct((l, SSM_WIDTH), F32),
                   jax.ShapeDtypeStruct((2, SSM_CB, l, 2 * SSM_ST), F32)],
        scratch_shapes=[pltpu.VMEM((l, 2 * SSM_ST), F32), pltpu.VMEM((SSM_RC, 2 * SSM_ST), F32),
                        pltpu.VMEM((SSM_RC, 2 * SSM_ST), F32), pltpu.SemaphoreType.DMA],
        name="ssm_fwd", compiler_params=_params(("parallel", "arbitrary"), vmem_mb=56),
    )(u_seg, bcat.astype(BF16), ccat.astype(BF16), lam_re, lam_im)


def _ssm_bwd(u_seg, dy_seg, states, bcat, ccat, lam_re, lam_im):
    l = u_seg.shape[0]
    nj = l // N_SEG
    nc = l // SSM_RC

    def body(u_ref, dy_ref, keep_ref, b_ref, c_ref, lr_ref, li_ref,
             du_ref, db_ref, dc_ref, dlr_ref, dli_ref, xs_ref, gs_ref, stage0, stage1, keep_sem):
        k, d = pl.program_id(0), pl.program_id(1)
        rev = d == 1
        back = jnp.logical_not(rev)
        shape = (N_SEG, SSM_ST)
        ar, ai = jnp.broadcast_to(lr_ref[...], shape), -jnp.broadcast_to(li_ref[...], shape)
        zero = jnp.zeros(shape, F32)
        fetch = pltpu.make_async_copy(keep_ref.at[d, k], xs_ref, keep_sem)
        fetch.start()

        def inputs(ci, stage):
            rows = _chunk_rows(ci, back, nc)
            dx = _dg(dy_ref[rows, :].astype(BF16), c_ref[...], NT)
            stage[...] = dx
            gs_ref[rows, :] = dx

        def first(stage, ci, carry):
            return _scan_chunk((stage, True), None, ar, ai, back, nj, ci, carry)

        def first_pass(t, carry):
            inputs(2 * t + 1, stage1)
            carry = first(stage0, 2 * t, carry)
            inputs(2 * t + 2, stage0)
            return first(stage1, 2 * t + 1, carry)

        inputs(0, stage0)
        carry = lax.fori_loop(0, nc // 2 - 1, first_pass, (zero, zero))
        inputs(nc - 1, stage1)
        carry = first(stage0, nc - 2, carry)
        end_r, end_i = first(stage1, nc - 1, carry)
        init = _segment_inits(ar, ai, end_r, end_i, back, nj)
        fetch.wait()
        db_ref[...] = jnp.zeros_like(db_ref)
        dc_ref[...] = jnp.zeros_like(dc_ref)

        @pl.when(d == 0)
        def _():
            du_ref[...] = jnp.zeros_like(du_ref)

        def outputs(ci, stage):
            rows = _chunk_rows(ci, back, nc)
            g = stage[...].astype(BF16)
            dc_ref[...] += _dg(xs_ref[rows, :].astype(BF16), dy_ref[rows, :].astype(BF16), TN)
            db_ref[...] += _dg(u_ref[rows, :].astype(BF16), g, TN)
            du_ref[rows, :] += _dg(g, b_ref[...], NT)

        def second(ci, stage, carry):
            return _scan_chunk((gs_ref, False), (stage, True), ar, ai, back, nj, ci, carry, prev_ref=xs_ref)

        def second_pass(t, carry):
            outputs(2 * t, stage0)
            carry = second(2 * t + 1, stage1, carry)
            outputs(2 * t + 1, stage1)
            return second(2 * t + 2, stage0, carry)

        carry = lax.fori_loop(0, nc // 2 - 1, second_pass, second(0, stage0, init + (zero, zero)))
        outputs(nc - 2, stage0)
        gr, gi, acc_r, acc_i = second(nc - 1, stage1, carry)
        outputs(nc - 1, stage1)

        seg = lax.broadcasted_iota(jnp.int32, shape, 0)
        jb = jnp.where(rev, nj - 1, 0)
        erow = pl.ds(pl.multiple_of((nj - 1 - jb) * N_SEG, N_SEG), N_SEG)

        def before(t):
            up = jnp.where(seg >= 1, pltpu.roll(t, 1, 0), 0.0)
            down = jnp.where(seg <= N_SEG - 2, pltpu.roll(t, N_SEG - 1, 0), 0.0)
            return jnp.where(rev, down, up)

        init_r, init_i = before(xs_ref[erow, _RE]), before(xs_ref[erow, _IM])
        acc_r = acc_r + gr * init_r + gi * init_i
        acc_i = acc_i + gi * init_r - gr * init_i
        dlr_ref[...] = jnp.sum(acc_r, axis=0, keepdims=True)
        dli_ref[...] = jnp.sum(acc_i, axis=0, keepdims=True)

    act, bmat, cmat, lam = _ssm_specs(l)
    return pl.pallas_call(
        body, grid=(SSM_CB, 2), in_specs=[act, act, ANY, bmat, cmat, lam, lam],
        out_specs=[act, bmat, cmat, lam, lam],
        out_shape=[jax.ShapeDtypeStruct((l, SSM_WIDTH), F32),
                   jax.ShapeDtypeStruct(bcat.shape, F32), jax.ShapeDtypeStruct(ccat.shape, F32),
                   jax.ShapeDtypeStruct(lam_re.shape, F32), jax.ShapeDtypeStruct(lam_im.shape, F32)],
        scratch_shapes=[pltpu.VMEM((l, 2 * SSM_ST), F32), pltpu.VMEM((l, 2 * SSM_ST), F32),
                        pltpu.VMEM((SSM_RC, 2 * SSM_ST), F32), pltpu.VMEM((SSM_RC, 2 * SSM_ST), F32),
                        pltpu.SemaphoreType.DMA],
        name="ssm_bwd", compiler_params=_params(("parallel", "arbitrary"), vmem_mb=58),
    )(u_seg, dy_seg, states, bcat.astype(BF16), ccat.astype(BF16), lam_re, lam_im)


def _glu_fwd(y_ssm, u, d_skip, w_glu):
    l, w = u.shape

    def body(y_ref, u_ref, d_ref, w_ref, pre_ref, s_ref, ys_ref):
        pre = y_ref[...] + d_ref[...] * u_ref[...]
        z = _gelu(pre)
        s = _dg(z.astype(BF16), w_ref[...], NN)
        pre_ref[...] = pre
        s_ref[...] = s
        ys_ref[...] = z * _sigmoid(s)

    row = pl.BlockSpec((TM_EW, w), lambda i: (i, 0))
    out = jax.ShapeDtypeStruct((l, w), F32)
    return pl.pallas_call(
        body, grid=(l // TM_EW,),
        in_specs=[row, row, pl.BlockSpec((1, w), lambda i: (0, 0)), pl.BlockSpec((w, w), lambda i: (0, 0))],
        out_specs=[row, row, row], out_shape=[out, out, out], name="glu_fwd",
        compiler_params=_params(("parallel",)),
    )(y_ssm, u, d_skip, w_glu)


def _glu_bwd(pre, s, dys, u, d_skip, w_glu):
    l, w = u.shape

    def body(pre_ref, s_ref, dys_ref, u_ref, d_ref, w_ref, dpre_ref, z_ref, ds_ref, dd_ref):
        pre, dys = pre_ref[...], dys_ref[...]
        z = _gelu(pre)
        sig = _sigmoid(s_ref[...])
        ds = (dys * z * sig * (1.0 - sig)).astype(BF16)
        dz = dys * sig + _dg(ds, w_ref[...], NT)
        dpre = dz * _gelu_grad(pre)
        dpre_ref[...] = dpre
        z_ref[...] = z.astype(BF16)
        ds_ref[...] = ds

        @pl.when(pl.program_id(0) == 0)
        def _():
            dd_ref[...] = jnp.zeros_like(dd_ref)

        dd_ref[...] += jnp.sum(dpre * u_ref[...], axis=0, keepdims=True)

    row = pl.BlockSpec((TM_EW, w), lambda i: (i, 0))
    vec = pl.BlockSpec((1, w), lambda i: (0, 0))
    return pl.pallas_call(
        body, grid=(l // TM_EW,),
        in_specs=[row, row, row, row, vec, pl.BlockSpec((w, w), lambda i: (0, 0))],
        out_specs=[row, row, row, vec],
        out_shape=[jax.ShapeDtypeStruct((l, w), F32), jax.ShapeDtypeStruct((l, w), BF16),
                   jax.ShapeDtypeStruct((l, w), BF16), jax.ShapeDtypeStruct((1, w), F32)],
        name="glu_bwd", compiler_params=_params(("arbitrary",)),
    )(pre, s, dys, u, d_skip, w_glu)


TM_CV = 512
TC_CV = 256
TM_CF = 256
TC_CF = D_FF // 2
HALO = SUBLANES


def _conv_specs(l, col0, tm=TM_CV, tc=TC_CV):
    per = tm // HALO
    nh = l // HALO
    off = col0 // tc
    return [
        pl.BlockSpec((HALO, tc), lambda j, i: (jnp.maximum(i * per - 1, 0), j + off)),
        pl.BlockSpec((tm, tc), lambda j, i: (i, j + off)),
        pl.BlockSpec((HALO, tc), lambda j, i: (jnp.minimum((i + 1) * per, nh - 1), j + off)),
    ]


def _ext(prev_ref, mid_ref, next_ref, first, last):
    p = jnp.where(first, 0.0, prev_ref[...])
    n = jnp.where(last, 0.0, next_ref[...])
    return jnp.concatenate([p, mid_ref[...], n], axis=0)


def _shift_dn(t):
    return pltpu.roll(t, 1, 0)


def _shift_up(t):
    return pltpu.roll(t, t.shape[0] - 1, 0)


def _conv3(e, w_ref, b_ref):
    return w_ref[0:1, :] * _shift_dn(e) + w_ref[1:2, :] * e + w_ref[2:3, :] * _shift_up(e) + b_ref[...]


def _convffn_fwd(up_pre, conv_w, conv_b):
    l = up_pre.shape[0]
    tm, tc = TM_CF, TC_CF
    ni = l // tm
    wspec = lambda off: pl.BlockSpec((3, tc), lambda j, i: (0, j + off))
    bspec = lambda off: pl.BlockSpec((1, tc), lambda j, i: (0, j + off))
    voff = D_FF // tc

    def body(gp, gm, gn, vp, vm, vn, wg, bg, wv, bv, o_ref):
        i = pl.program_id(1)
        first, last = i == 0, i == ni - 1
        gate = _conv3(_ext(gp, gm, gn, first, last), wg, bg)[HALO:HALO + tm]
        val = _conv3(_ext(vp, vm, vn, first, last), wv, bv)[HALO:HALO + tm]
        o_ref[...] = (gate * _sigmoid(gate) * val).astype(BF16)

    return pl.pallas_call(
        body, grid=(D_FF // tc, ni),
        in_specs=_conv_specs(l, 0, tm, tc) + _conv_specs(l, D_FF, tm, tc)
        + [wspec(0), bspec(0), wspec(voff), bspec(voff)],
        out_specs=pl.BlockSpec((tm, tc), lambda j, i: (i, j)),
        out_shape=jax.ShapeDtypeStruct((l, D_FF), BF16), name="convffn_fwd",
        compiler_params=_params(("parallel", "parallel")),
    )(up_pre, up_pre, up_pre, up_pre, up_pre, up_pre, conv_w, conv_b, conv_w, conv_b)


HALO_B = 2 * SUBLANES


def _convffn_bwd(up_pre, dx2b, w_down, conv_w, conv_b):
    l = up_pre.shape[0]
    ni = l // TM_CV
    d = dx2b.shape[1]
    wspec = lambda off: pl.BlockSpec((3, TC_CV), lambda i, j: (0, j + off))
    bspec = lambda off: pl.BlockSpec((1, TC_CV), lambda i, j: (0, j + off))
    voff = D_FF // TC_CV
    swap = lambda spec: pl.BlockSpec(spec.block_shape, lambda i, j, f=spec.index_map: f(j, i))
    per, nh = TM_CV // HALO_B, l // HALO_B
    dx_specs = [pl.BlockSpec((HALO_B, d), lambda i, j: (jnp.maximum(i * per - 1, 0), 0)),
                pl.BlockSpec((TM_CV, d), lambda i, j: (i, 0)),
                pl.BlockSpec((HALO_B, d), lambda i, j: (jnp.minimum((i + 1) * per, nh - 1), 0))]

    def body(gp, gm, gn, vp, vm, vn, xp, xm, xn, wd, wg, bg, wv, bv, dup_ref, pg_ref, pv_ref):
        i = pl.program_id(0)
        first, last = i == 0, i == ni - 1
        ge, ve = _ext(gp, gm, gn, first, last), _ext(vp, vm, vn, first, last)
        zero = jnp.zeros((HALO_B, d), BF16)
        dx = jnp.concatenate([jnp.where(first, zero, xp[...]), xm[...], jnp.where(last, zero, xn[...])], axis=0)
        de = _dg(dx, wd[...], NT)[HALO_B - HALO:HALO_B + TM_CV + HALO]
        taps = [(_shift_dn(e), e, _shift_up(e)) for e in (ge, ve)]
        conv = lambda t, w_ref, b_ref: w_ref[0:1, :] * t[0] + w_ref[1:2, :] * t[1] + w_ref[2:3, :] * t[2] + b_ref[...]
        gate, val = conv(taps[0], wg, bg), conv(taps[1], wv, bv)
        sig = _sigmoid(gate)
        silu = gate * sig
        dgate = de * val * (sig + silu * (1.0 - sig))
        dval = de * silu
        mid = slice(HALO, HALO + TM_CV)
        rid = lax.broadcasted_iota(jnp.int32, (SUBLANES, TC_CV), 0)
        for half, (dup, tap, w_ref, p_ref) in enumerate(((dgate, taps[0], wg, pg_ref), (dval, taps[1], wv, pv_ref))):
            dpre = w_ref[0:1, :] * _shift_up(dup) + w_ref[1:2, :] * dup + w_ref[2:3, :] * _shift_dn(dup)
            dup_ref[half] = dpre[mid].astype(BF16)
            dm_ = dup[mid]
            sums = [jnp.sum(dm_ * t[mid], axis=0, keepdims=True) for t in tap]
            sums.append(jnp.sum(dm_, axis=0, keepdims=True))
            acc = jnp.zeros((SUBLANES, TC_CV), F32)
            for k, sk in enumerate(sums):
                acc = jnp.where(rid == k, sk, acc)
            p_ref[...] = acc

    par = pl.BlockSpec((None, SUBLANES, TC_CV), lambda i, j: (i, 0, j))
    dup, pg, pv = pl.pallas_call(
        body, grid=(ni, D_FF // TC_CV),
        in_specs=[swap(s) for s in _conv_specs(l, 0) + _conv_specs(l, D_FF)] + dx_specs
        + [pl.BlockSpec((TC_CV, d), lambda i, j: (j, 0)), wspec(0), bspec(0), wspec(voff), bspec(voff)],
        out_specs=[pl.BlockSpec((2, TM_CV, TC_CV), lambda i, j: (0, i, j)), par, par],
        out_shape=[jax.ShapeDtypeStruct((2, l, D_FF), BF16),
                   jax.ShapeDtypeStruct((ni, SUBLANES, D_FF), F32), jax.ShapeDtypeStruct((ni, SUBLANES, D_FF), F32)],
        name="convffn_bwd", compiler_params=_params(("parallel", "parallel")),
    )(up_pre, up_pre, up_pre, up_pre, up_pre, up_pre, dx2b, dx2b, dx2b, w_down, conv_w, conv_b, conv_w, conv_b)
    return dup, jnp.concatenate([jnp.sum(pg, axis=0), jnp.sum(pv, axis=0)], axis=1)


def _local_step(x, target, wb, sp, mixer_weights=None, late_weights=None, grads_ready=None,
                grads_next=None):
    l = x.shape[0]
    tabs = _rope_tables(l)
    disc = _ssm_disc(sp["a_re"], sp["a_im"], sp["log_step"], sp["b_re"], sp["b_im"])
    bcat, ccat, lam_re, lam_im = _ssm_pack(*disc, sp["c_re"], sp["c_im"])
    d_skip = sp["d_skip"].reshape(1, SSM_WIDTH)

    h, qkv, u = _rms_mm_rope(x, sp["norm_mix_g"], wb["w_in"], tabs, "mm_in")
    attn, lse = _attn_fwd(qkv, sp["sink"])
    u_seg = _to_segments(u)
    y_seg, states = _ssm_fwd(u_seg, bcat, ccat, lam_re, lam_im)
    y_ssm = _from_segments(y_seg)
    if mixer_weights is not None:
        wb = dict(wb, **mixer_weights(attn))
    pre, s_glu, ys = _glu_fwd(y_ssm, u, d_skip, wb["w_glu"])
    mixed, x1, h2 = _mix_mm_res_rms(attn, ys, sp["norm_attn_g"], sp["norm_ssm_g"], wb["w_out"], x,
                                    sp["norm_ffn_g"], "mm_out")
    if late_weights is not None:
        wb = dict(wb, **late_weights(h2))
    up_pre = _mm_nn_cols(h2, wb["w_up"], min(l, 1024), "mm_up")
    conv_w = wb["conv_w"]
    act = _convffn_fwd(up_pre, conv_w, sp["conv_b"])
    loss, dx2, dx2b, d_final_g = _mm_res_loss(act, wb["w_down"], x1, sp["norm_final_g"].reshape(1, D_MODEL), target)

    g = {"norm_final_g": d_final_g.reshape(D_MODEL)}
    g["w_down"] = _mm_tn(act, dx2b, D_FF // 2, 512, "mm_down_dw")
    dup_pre, conv_par = _convffn_bwd(up_pre, dx2b, wb["w_down"], conv_w, sp["conv_b"])
    g["conv_w"], g["conv_b"] = conv_par[0:3], conv_par[3:4]
    g["w_up"] = _mm_tn_cols(h2, dup_pre, wb["w_up"].shape[0], 512, "mm_up_dw")
    dx1, dx1b, g["norm_ffn_g"] = _mm_cols_rms_bwd(dup_pre, wb["w_up"], x1, sp["norm_ffn_g"], dx2, "mm_up_dx")
    g["w_out"] = _mm_tn(mixed, dx1b, 1024, 1024, "mm_out_dw")
    zero = grads_ready(g["w_up"], g["w_down"], g["w_out"]) if grads_ready is not None else 0.0
    dattn, dys, g["norm_attn_g"], g["norm_ssm_g"] = _mm_mix_bwd(
        dx1b, wb["w_out"], attn, ys, sp["norm_attn_g"] + zero, sp["norm_ssm_g"], "mm_out_dx")
    dpre, zb, dsb, dd = _glu_bwd(pre, s_glu, dys, u, d_skip, wb["w_glu"])
    g["d_skip"] = dd.reshape(N_SSM_GROUPS, SSM_GROUP)
    g["w_glu"] = _mm_tn(zb, dsb, 512, 512, "mm_glu_dw")
    zero = grads_next(g["w_glu"]) if grads_next is not None else 0.0
    du_seg, dbcat, dccat, dlam_re, dlam_im = _ssm_bwd(u_seg, _to_segments(dpre), states, bcat, ccat,
                                                      lam_re + zero, lam_im)
    dlb_re, dlb_im, dbb_re, dbb_im, g["c_re"], g["c_im"] = _ssm_unpack(dbcat, dccat, dlam_re, dlam_im)
    _, disc_vjp = jax.vjp(_ssm_disc, sp["a_re"], sp["a_im"], sp["log_step"], sp["b_re"], sp["b_im"])
    g["a_re"], g["a_im"], g["log_step"], g["b_re"], g["b_im"] = disc_vjp((dlb_re, dlb_im, dbb_re, dbb_im))
    dq, dkv, g["sink"] = _attn_bwd(qkv, attn, dattn, lse, sp["sink"])
    dproj = _rope_bwd(dq, dkv, _from_segments(du_seg), dpre, d_skip, tabs)
    g["w_in"] = _mm_tn(dproj, h, IN_WIDTH // 5, D_MODEL, "mm_in_dw")
    grad_x, _, g["norm_mix_g"] = _mm_nn_rms_bwd(dproj, wb["w_in"], x, sp["norm_mix_g"], dx1, "mm_in_dx")
    return loss, grad_x, g


MESH = pl.DeviceIdType.MESH
ANY = pl.BlockSpec(memory_space=pl.ANY)


def _place():
    x, y, c = lax.axis_index("x"), lax.axis_index("y"), lax.axis_index("c")
    chips = [(1 - x, y), (x, 1 - y), (1 - x, 1 - y)]
    return x, y, c, chips


def _chip_index(px, py):
    return 2 * px + py


CHUNK_BYTES = 256 * 1024
MAX_CHUNKS = 16


def _row_chunks(rows, row_bytes, align):
    n = max(1, min(MAX_CHUNKS, (rows * row_bytes) // CHUNK_BYTES))
    per = -(-rows // n)
    per = -(-per // align) * align
    return [(r0, min(per, rows - r0)) for r0 in range(0, rows, per)]


def _align_of(dtype):
    return SUBLANES * 4 // jnp.dtype(dtype).itemsize


def _remote(src, dst, send_sem, recv_sem, to):
    return pltpu.make_async_remote_copy(src_ref=src, dst_ref=dst, send_sem=send_sem, recv_sem=recv_sem,
                                        device_id=to, device_id_type=MESH)


CAST_ROWS = 64


def _gather_weights(shards, dtypes):
    nw = len(shards)

    def body(*refs):
        w_refs, o_refs = refs[:nw], refs[nw:2 * nw]
        send_sems, recv_sems, in_sems, out_sems = refs[2 * nw:2 * nw + 4]
        raw, cast = refs[2 * nw + 4:3 * nw + 4], refs[3 * nw + 4:]
        x, y, c, chips = _place()
        mine = _chip_index(x, y)
        sibling = (x, y, 1 - c)

        def rows_of(ref, chip, r0, nr):
            return ref.at[chip, pl.ds(r0, nr), :]

        def copy(wi, k, src, dst, to):
            return _remote(src, dst, send_sems.at[wi, k], recv_sems.at[wi, k], to)

        geo = []
        for wi in range(nw):
            rows, cols = w_refs[wi].shape
            row_bytes = cols * jnp.dtype(dtypes[wi]).itemsize
            geo.append((rows // 2, _row_chunks(rows // 2, row_bytes, _align_of(dtypes[wi]))))

        stage_in = [pltpu.make_async_copy(w_refs[wi], raw[wi], in_sems.at[wi]) for wi in range(nw)]
        for cp in stage_in:
            cp.start()
        staged = [raw[wi] if dtypes[wi] == w_refs[wi].dtype else cast[wi] for wi in range(nw)]
        stage_out = []
        for wi in range(nw):
            stage_in[wi].wait()
            if staged[wi] is not raw[wi]:
                def cast_rows(i, _, wi=wi):
                    rows = pl.ds(pl.multiple_of(i * CAST_ROWS, CAST_ROWS), CAST_ROWS)
                    cast[wi][rows, :] = raw[wi][rows, :].astype(dtypes[wi])
                    return 0

                lax.fori_loop(0, w_refs[wi].shape[0] // CAST_ROWS, cast_rows, 0)
            cp = pltpu.make_async_copy(staged[wi], o_refs[wi].at[mine], out_sems.at[wi])
            cp.start()
            stage_out.append(cp)

        for wi in range(nw):
            hr, half_chunks = geo[wi]
            for j, chip in enumerate(chips):
                for r0, nr in half_chunks:
                    copy(wi, j, staged[wi].at[pl.ds(c * hr + r0, nr), :],
                         rows_of(o_refs[wi], mine, c * hr + r0, nr), (*chip, c)).start()
        for wi in range(nw):
            hr, half_chunks = geo[wi]
            for j, chip in enumerate(chips):
                got = rows_of(o_refs[wi], _chip_index(*chip), c * hr, hr)
                copy(wi, j, got, got, (*chip, c)).wait_recv()
                for r0, nr in half_chunks:
                    piece = rows_of(o_refs[wi], _chip_index(*chip), c * hr + r0, nr)
                    copy(wi, 3 + j, piece, piece, sibling).start()
        for wi in range(nw):
            hr = geo[wi][0]
            for j, chip in enumerate(chips):
                got = rows_of(o_refs[wi], _chip_index(*chip), (1 - c) * hr, hr)
                copy(wi, 3 + j, got, got, sibling).wait_recv()
        for wi in range(nw):
            hr = geo[wi][0]
            sent = rows_of(o_refs[wi], mine, c * hr, hr)
            for k in range(6):
                copy(wi, k, sent, sent, sibling).wait_send()
            stage_out[wi].wait()

    return pl.pallas_call(
        body, in_specs=[ANY] * nw, out_specs=[ANY] * nw,
        out_shape=[jax.ShapeDtypeStruct((4, *s.shape), t) for s, t in zip(shards, dtypes)],
        scratch_shapes=[pltpu.SemaphoreType.DMA((nw, 6)), pltpu.SemaphoreType.DMA((nw, 6)),
                        pltpu.SemaphoreType.DMA((nw,)), pltpu.SemaphoreType.DMA((nw,))]
        + [pltpu.VMEM(s.shape, s.dtype) for s in shards] + [pltpu.VMEM(s.shape, t) for s, t in zip(shards, dtypes)],
        name="gather_weights", compiler_params=_params(vmem_mb=40),
    )(*shards)


HBM = pl.BlockSpec(memory_space=pltpu.HBM)
SEM = pl.BlockSpec(memory_space=pltpu.SEMAPHORE)
EFFECT = pltpu.SideEffectType.DATAFLOW_SIDE_EFFECTING


def _cast_place(w, place, dtype, after, name):
    rows, cols = w.shape
    tr = _row_tile(rows, cols, _align_of(dtype))

    def body(p_ref, w_ref, after_ref, o_ref):
        del p_ref, after_ref
        o_ref[...] = w_ref[...].astype(dtype)

    grid_spec = pltpu.PrefetchScalarGridSpec(
        num_scalar_prefetch=1, grid=(rows // tr,),
        in_specs=[pl.BlockSpec((tr, cols), lambda i, p: (i, 0)), ANY],
        out_specs=pl.BlockSpec((None, tr, cols), lambda i, p: (p[1], i, 0)))
    return pl.pallas_call(body, grid_spec=grid_spec, out_shape=jax.ShapeDtypeStruct((4, rows, cols), dtype),
                          name=name, compiler_params=_params(("parallel",)))(place, w, after)


def _split_start(name, arrays, n_pairs, issue):
    n = len(arrays)

    def body(*refs):
        issue(refs[:n], refs[n:n + n_pairs], refs[n + n_pairs:n + 2 * n_pairs])
        token = refs[2 * n + 2 * n_pairs]
        token[...] = jnp.zeros_like(token)

    dma = pltpu.SemaphoreType.DMA(())
    outs = pl.pallas_call(
        body, name=name,
        out_shape=[dma] * (2 * n_pairs) + [pltpu.HBM(t.shape, t.dtype) for t in arrays]
        + [jax.ShapeDtypeStruct((SUBLANES, LANES), F32)],
        in_specs=[HBM] * n, out_specs=[SEM] * (2 * n_pairs) + [HBM] * n + [pl.BlockSpec(memory_space=pltpu.VMEM)],
        input_output_aliases={a: 2 * n_pairs + a for a in range(n)},
        compiler_params=pltpu.CompilerParams(has_side_effects=EFFECT),
    )(*[pltpu.with_memory_space_constraint(t, pltpu.HBM) for t in arrays])
    return outs[:n_pairs], outs[n_pairs:2 * n_pairs], outs[2 * n_pairs:2 * n_pairs + n], outs[-1]


def _split_wait(name, send_sems, recv_sems, flying, sizes, after):
    n, n_pairs = len(flying), len(send_sems)

    def body(*refs):
        x, y, c, _ = _place()
        for k, ref in enumerate(sizes(refs[:n])):
            cp = _remote(ref, ref, refs[n + k], refs[n + n_pairs + k], (x, y, 1 - c))
            cp.wait_send()
            cp.wait_recv()

    return pl.pallas_call(
        body, name=name, out_shape=[pltpu.HBM(t.shape, t.dtype) for t in flying],
        in_specs=[HBM] * n + [SEM] * (2 * n_pairs) + [ANY], out_specs=[HBM] * n,
        input_output_aliases={a: a for a in range(n)},
        compiler_params=pltpu.CompilerParams(has_side_effects=EFFECT),
    )(*flying, *send_sems, *recv_sems, after)


def _spread_start(lands, name):
    def issue(land_refs, send_sems, recv_sems):
        x, y, c, chips = _place()
        mine = _chip_index(x, y)
        for a, land in enumerate(land_refs):
            _, rows, cols = land.shape
            hr = rows // 2
            row_bytes = cols * jnp.dtype(land.dtype).itemsize
            for r0, nr in _row_chunks(hr, row_bytes, _align_of(land.dtype)):
                piece = land.at[mine, pl.ds(c * hr + r0, nr), :]
                for chip in chips:
                    for core in (0, 1):
                        _remote(piece, piece, send_sems[a], recv_sems[a], (*chip, core)).start()

    return _split_start(name, lands, len(lands), issue)


def _spread_wait(send_sems, recv_sems, flying, after, name):
    return _split_wait(name, send_sems, recv_sems, flying, lambda refs: [r.at[pl.ds(0, 3)] for r in refs], after)


def _pair_start(grads):
    n = len(grads)
    zones = [lax.empty((4, g.shape[1] // 2, g.shape[2]), F32) for g in grads]

    def issue(refs, send_sems, recv_sems):
        x, y, c, _ = _place()
        for a in range(n):
            g_ref, z_ref = refs[a], refs[n + a]
            _, rows, cols = g_ref.shape
            hr = rows // 2
            for k in range(4):
                for r0, nr in _row_chunks(hr, cols * 4, SUBLANES):
                    _remote(g_ref.at[k, pl.ds((1 - c) * hr + r0, nr), :], z_ref.at[k, pl.ds(r0, nr), :],
                            send_sems[a], recv_sems[a], (x, y, 1 - c)).start()

    return _split_start("pair_start", list(grads) + zones, n, issue)


def _pair_wait(send_sems, recv_sems, flying, after):
    n = len(flying) // 2
    out = _split_wait("pair_wait", send_sems, recv_sems, flying, lambda refs: list(refs[n:]), after)
    return out[:n], out[n:]


def _chip_start(sums):
    n = len(sums)
    zones = [lax.empty((3, *s.shape[1:]), s.dtype) for s in sums]

    def issue(refs, send_sems, recv_sems):
        x, y, c, chips = _place()
        for a in range(n):
            s_ref, z_ref = refs[a], refs[n + a]
            _, rows, cols = s_ref.shape
            row_bytes = cols * jnp.dtype(s_ref.dtype).itemsize
            for r0, nr in _row_chunks(rows, row_bytes, _align_of(s_ref.dtype)):
                for j, chip in enumerate(chips):
                    _remote(s_ref.at[_chip_index(*chip), pl.ds(r0, nr), :], z_ref.at[j, pl.ds(r0, nr), :],
                            send_sems[a], recv_sems[a], (*chip, c)).start()

    return _split_start("chip_start", list(sums) + zones, n, issue)


def _chip_wait(send_sems, recv_sems, flying, after):
    n = len(flying) // 2
    return _split_wait("chip_wait", send_sems, recv_sems, flying, lambda refs: list(refs[n:]), after)[n:]


def _pair_exchange(grads):
    na = len(grads)

    def body(*refs):
        g_refs, o_refs = refs[:na], refs[na:2 * na]
        send_sems, recv_sems = refs[2 * na:]
        x, y, c, _ = _place()
        sibling = (x, y, 1 - c)
        for ai in range(na):
            _, rows, cols = g_refs[ai].shape
            hr = rows // 2
            for k in range(4):
                for r0, nr in _row_chunks(hr, cols * 4, SUBLANES):
                    _remote(g_refs[ai].at[k, pl.ds((1 - c) * hr + r0, nr), :], o_refs[ai].at[k, pl.ds(r0, nr), :],
                            send_sems.at[ai], recv_sems.at[ai], sibling).start()
        for ai in range(na):
            _remote(o_refs[ai], o_refs[ai], send_sems.at[ai], recv_sems.at[ai], sibling).wait()

    return pl.pallas_call(
        body, in_specs=[ANY] * na, out_specs=[ANY] * na,
        out_shape=[jax.ShapeDtypeStruct((4, g.shape[1] // 2, g.shape[2]), F32) for g in grads],
        scratch_shapes=[pltpu.SemaphoreType.DMA((na,)), pltpu.SemaphoreType.DMA((na,))],
        name="pair_exchange",
    )(*grads)


def _row_tile(rows, cols, align):
    best = align
    for cand in range(align, rows + 1, align):
        if rows % cand == 0 and cand * cols <= 256 * 1024:
            best = cand
    return best


def _pair_sum(g, got, place, transit, name):
    _, rows, cols = g.shape
    hr = rows // 2
    tr = _row_tile(hr, cols, _align_of(transit))
    nt = hr // tr

    def body(p_ref, g_ref, r_ref, s_ref, own_ref):
        total = g_ref[...] + r_ref[...]
        s_ref[...] = total.astype(transit)

        @pl.when(pl.program_id(1) == p_ref[1])
        def _():
            own_ref[...] = total

    grid_spec = pltpu.PrefetchScalarGridSpec(
        num_scalar_prefetch=1, grid=(nt, 4),
        in_specs=[pl.BlockSpec((None, tr, cols), lambda i, k, p: (k, p[0] * nt + i, 0)),
                  pl.BlockSpec((None, tr, cols), lambda i, k, p: (k, i, 0))],
        out_specs=[pl.BlockSpec((None, tr, cols), lambda i, k, p: (k, i, 0)),
                   pl.BlockSpec((tr, cols), lambda i, k, p: (i, 0))])
    return pl.pallas_call(
        body, grid_spec=grid_spec,
        out_shape=[jax.ShapeDtypeStruct((4, hr, cols), transit), jax.ShapeDtypeStruct((hr, cols), F32)],
        name=name, compiler_params=_params(("parallel", "arbitrary")),
    )(place, g, got)


def _chip_exchange(sums):
    na = len(sums)

    def body(*refs):
        s_refs, o_refs = refs[:na], refs[na:2 * na]
        send_sems, recv_sems = refs[2 * na:]
        x, y, c, chips = _place()
        for ai in range(na):
            _, rows, cols = s_refs[ai].shape
            row_bytes = cols * jnp.dtype(s_refs[ai].dtype).itemsize
            for r0, nr in _row_chunks(rows, row_bytes, _align_of(s_refs[ai].dtype)):
                for j, chip in enumerate(chips):
                    _remote(s_refs[ai].at[_chip_index(*chip), pl.ds(r0, nr), :], o_refs[ai].at[j, pl.ds(r0, nr), :],
                            send_sems.at[ai, j], recv_sems.at[ai, j], (*chip, c)).start()
        for ai in range(na):
            for j, chip in enumerate(chips):
                _remote(o_refs[ai].at[j], o_refs[ai].at[j], send_sems.at[ai, j], recv_sems.at[ai, j],
                        (*chip, c)).wait()

    return pl.pallas_call(
        body, in_specs=[ANY] * na, out_specs=[ANY] * na,
        out_shape=[jax.ShapeDtypeStruct((3, *s.shape[1:]), s.dtype) for s in sums],
        scratch_shapes=[pltpu.SemaphoreType.DMA((na, 3)), pltpu.SemaphoreType.DMA((na, 3))],
        name="chip_exchange",
    )(*sums)


def _chip_sum(own, landed, name):
    hr, cols = own.shape
    tr = _row_tile(hr, cols, _align_of(landed.dtype))

    def body(o_ref, l_ref, f_ref):
        acc = o_ref[...]
        for j in range(3):
            acc = acc + l_ref[j].astype(F32)
        f_ref[...] = acc

    return pl.pallas_call(
        body, grid=(hr // tr,),
        in_specs=[pl.BlockSpec((tr, cols), lambda i: (i, 0)), pl.BlockSpec((3, tr, cols), lambda i: (0, i, 0))],
        out_specs=pl.BlockSpec((tr, cols), lambda i: (i, 0)),
        out_shape=jax.ShapeDtypeStruct((hr, cols), F32), name=name,
        compiler_params=_params(("parallel",)),
    )(own, landed)


def _final_exchange(halves, small):
    nh = len(halves)

    def body(*refs):
        h_refs, s_ref = refs[:nh], refs[nh]
        o_refs, so_ref = refs[nh + 1:2 * nh + 1], refs[2 * nh + 1]
        send_sems, recv_sems, local_sem, ssend_sems, srecv_sems = refs[2 * nh + 2:]
        x, y, c, _ = _place()
        me = 4 * x + 2 * y + c
        sibling = (x, y, 1 - c)
        for hi in range(nh):
            hr, cols = h_refs[hi].shape
            for r0, nr in _row_chunks(hr, cols * 4, SUBLANES):
                _remote(h_refs[hi].at[pl.ds(r0, nr), :], o_refs[hi].at[pl.ds(r0, nr), :],
                        send_sems.at[hi], recv_sems.at[hi], sibling).start()
        small_cps = [pltpu.make_async_copy(s_ref, so_ref.at[me], local_sem)]
        for r in range(1, 8):
            fx, fy, fc = (r >> 2) & 1, (r >> 1) & 1, r & 1
            peer = (1 - x if fx else x, 1 - y if fy else y, 1 - c if fc else c)
            small_cps.append(_remote(s_ref, so_ref.at[me], ssend_sems.at[r - 1], srecv_sems.at[r - 1], peer))
        for cp in small_cps:
            cp.start()
        for hi in range(nh):
            _remote(h_refs[hi], o_refs[hi], send_sems.at[hi], recv_sems.at[hi], sibling).wait()
        for cp in small_cps:
            cp.wait()

    return pl.pallas_call(
        body, in_specs=[ANY] * (nh + 1), out_specs=[ANY] * (nh + 1),
        out_shape=[jax.ShapeDtypeStruct(h.shape, F32) for h in halves]
        + [jax.ShapeDtypeStruct((8, *small.shape), F32)],
        scratch_shapes=[pltpu.SemaphoreType.DMA((nh,)), pltpu.SemaphoreType.DMA((nh,)),
                        pltpu.SemaphoreType.DMA, pltpu.SemaphoreType.DMA((7,)), pltpu.SemaphoreType.DMA((7,))],
        name="final_exchange",
    )(*halves, small)


def _adamw_halves(w, own, other, m, v, place, name):
    r, c = w.shape
    hr = r // 2
    tr = _row_tile(hr, c, SUBLANES)
    nt = hr // tr
    c1 = 1.0 - ADAM_B1 ** ADAM_STEP
    c2 = 1.0 - ADAM_B2 ** ADAM_STEP

    def body(p_ref, w_ref, own_ref, other_ref, m_ref, v_ref, g_ref, d_ref, nm_ref, nv_ref):
        mine = pl.program_id(0) // nt == p_ref[0]
        gv = jnp.where(mine, own_ref[...], other_ref[...])
        nm = ADAM_B1 * m_ref[...] + (1.0 - ADAM_B1) * gv
        nv = ADAM_B2 * v_ref[...] + (1.0 - ADAM_B2) * (gv * gv)
        g_ref[...] = gv
        d_ref[...] = -ADAM_LR * ((nm / c1) / (jnp.sqrt(nv / c2) + ADAM_EPS) + ADAM_WD * w_ref[...])
        nm_ref[...] = nm
        nv_ref[...] = nv

    full = pl.BlockSpec((tr, c), lambda i, p: (i, 0))
    half = pl.BlockSpec((tr, c), lambda i, p: (i % nt, 0))
    out = jax.ShapeDtypeStruct((r, c), F32)
    grid_spec = pltpu.PrefetchScalarGridSpec(num_scalar_prefetch=1, grid=(2 * nt,),
                                             in_specs=[full, half, half, full, full], out_specs=[full] * 4)
    return pl.pallas_call(body, grid_spec=grid_spec, out_shape=[out] * 4, name=name,
                          compiler_params=_params(("parallel",)))(place, w, own, other, m, v)


def _adamw_many(ws, gs, ms, vs, name):
    n = len(ws)
    c1 = 1.0 - ADAM_B1 ** ADAM_STEP
    c2 = 1.0 - ADAM_B2 ** ADAM_STEP

    def body(*refs):
        w_refs, g_refs, m_refs, v_refs = (refs[k * n:(k + 1) * n] for k in range(4))
        d_refs, nm_refs, nv_refs = (refs[(4 + k) * n:(5 + k) * n] for k in range(3))
        for i in range(n):
            gv = g_refs[i][...]
            nm = ADAM_B1 * m_refs[i][...] + (1.0 - ADAM_B1) * gv
            nv = ADAM_B2 * v_refs[i][...] + (1.0 - ADAM_B2) * (gv * gv)
            d_refs[i][...] = -ADAM_LR * ((nm / c1) / (jnp.sqrt(nv / c2) + ADAM_EPS) + ADAM_WD * w_refs[i][...])
            nm_refs[i][...] = nm
            nv_refs[i][...] = nv

    vmem = pl.BlockSpec(memory_space=pltpu.VMEM)
    shapes = [jax.ShapeDtypeStruct(t.shape, F32) for t in ws]
    outs = pl.pallas_call(body, in_specs=[vmem] * (4 * n), out_specs=[vmem] * (3 * n), out_shape=shapes * 3,
                          name=name, compiler_params=_params(vmem_mb=56))(*ws, *gs, *ms, *vs)
    return outs[:n], outs[n:2 * n], outs[2 * n:]


BIG = ("w_in", "w_glu", "w_out", "w_up", "w_down")
WEIGHTS = ("norm_mix_g", "w_in", "a_re", "a_im", "log_step", "b_re", "b_im", "c_re", "c_im", "d_skip", "w_glu",
           "sink", "norm_attn_g", "norm_ssm_g", "w_out", "norm_ffn_g", "w_up", "conv_w", "conv_b", "w_down",
           "norm_final_g")
SMALL = ("norm_mix_g", "a_re", "a_im", "log_step", "b_re", "b_im", "c_re", "c_im", "d_skip", "sink",
         "norm_attn_g", "norm_ssm_g", "norm_ffn_g", "conv_w", "conv_b", "norm_final_g")
SMALL_ROWS = 48
N_DEV = 8


def _tile_rows(size):
    return -(-size // (SUBLANES * D_MODEL)) * SUBLANES


def _by_owner(name, g):
    if name == "w_up":
        return g
    return g.reshape(4, g.shape[0] // 4, g.shape[1])


def _view(name, t):
    if name == "w_in":
        return jnp.swapaxes(t[0], 0, 1)
    if name in ("b_re", "b_im"):
        return jnp.swapaxes(t, -1, -2)
    return t


def _unview(name, t):
    if name == "w_in":
        return jnp.swapaxes(t, 0, 1)[None]
    if name in ("b_re", "b_im"):
        return jnp.swapaxes(t, -1, -2)
    return t


def kernel(x, norm_mix_g, w_in, a_re, a_im, log_step, b_re, b_im, c_re, c_im, d_skip, w_glu, sink, norm_attn_g, norm_ssm_g, w_out, norm_ffn_g, w_up, conv_w, conv_b, w_down, norm_final_g, loss_target, m_norm_mix_g, m_w_in, m_a_re, m_a_im, m_log_step, m_b_re, m_b_im, m_c_re, m_c_im, m_d_skip, m_w_glu, m_sink, m_norm_attn_g, m_norm_ssm_g, m_w_out, m_norm_ffn_g, m_w_up, m_conv_w, m_conv_b, m_w_down, m_norm_final_g, v_norm_mix_g, v_w_in, v_a_re, v_a_im, v_log_step, v_b_re, v_b_im, v_c_re, v_c_im, v_d_skip, v_w_glu, v_sink, v_norm_attn_g, v_norm_ssm_g, v_w_out, v_norm_ffn_g, v_w_up, v_conv_w, v_conv_b, v_w_down, v_norm_final_g):
    given = dict(locals())
    w = {n: given[n] for n in WEIGHTS}
    m = {n: given["m_" + n] for n in WEIGHTS}
    v = {n: given["v_" + n] for n in WEIGHTS}
    xy = 2 * lax.axis_index("x") + lax.axis_index("y")

    core = lax.axis_index("c")
    place = jnp.stack([core, xy]).astype(jnp.int32)

    conv_rows = jnp.pad(w["conv_w"][0], ((0, 2 * SUBLANES - 3), (0, 0)))
    rows = lambda t: t.reshape(4 * t.shape[1], t.shape[2])
    (w_in_all,) = _gather_weights([_view("w_in", w["w_in"])], [BF16])
    wb = {"w_in": rows(w_in_all)}
    mixer = [_cast_place(w[n][0], place, BF16, w_in_all, "cast_" + n) for n in ("w_glu", "w_out")]
    mixer.append(_cast_place(conv_rows, place, F32, w_in_all, "cast_conv_w"))
    *mixer_flight, mixer_token = _spread_start(mixer, "spread_mixer_start")
    late = ("w_up", "w_down")
    *late_flight, token = _spread_start(
        [_cast_place(w[n][0], place, BF16, mixer_token, "cast_" + n) for n in late], "spread_ffn_start")

    def mixer_weights(after):
        w_glu4, w_out4, conv4 = _spread_wait(*mixer_flight, after, "spread_mixer_wait")
        return {"w_glu": rows(w_glu4), "w_out": rows(w_out4),
                "conv_w": conv4[:, :3].transpose(1, 0, 2).reshape(3, 2 * D_FF)}

    def late_weights(after):
        w_up4, w_down4 = _spread_wait(*late_flight, after, "spread_ffn_wait")
        return {"w_up": w_up4, "w_down": rows(w_down4)}

    sp = {n: w[n][0] for n in ("a_re", "a_im", "log_step", "b_re", "b_im", "c_re", "c_im", "d_skip",
                               "norm_mix_g", "norm_attn_g", "norm_ssm_g", "norm_ffn_g", "sink", "conv_b")}
    for n in ("norm_mix_g", "norm_attn_g", "norm_ssm_g", "norm_ffn_g", "sink", "conv_b"):
        sp[n] = sp[n].reshape(1, -1)
    sp["norm_mix_g"] = sp["norm_mix_g"] + token[:1, :1]
    sp["norm_final_g"] = w["norm_final_g"]
    early, tail = late + ("w_out",), ("w_in", "w_glu")
    flight = {}

    def grads_ready(dw_up, dw_down, dw_out):
        *flight["pair"], token = _pair_start([dw_up, _by_owner("w_down", dw_down), _by_owner("w_out", dw_out)])
        return token[:1, :1]

    def grads_next(after):
        mine, got = _pair_wait(*flight["pair"], after)
        sums, flight["own"] = zip(*[_pair_sum(a, b, place, BF16, "pair_sum_" + n) for n, a, b in zip(early, mine, got)])
        *flight["chip"], token = _chip_start(list(sums))
        return token[:1, :1]

    loss, grad_x, g = _local_step(x[0], loss_target[0], wb, sp, mixer_weights, late_weights, grads_ready,
                                  grads_next)

    def as_rows(t):
        rows = _tile_rows(t.size)
        return jnp.pad(t.reshape(-1), (0, rows * D_MODEL - t.size)).reshape(rows, D_MODEL)

    pieces = [as_rows(g[n]) for n in SMALL] + [as_rows(loss)]
    spare = N_DEV * SMALL_ROWS - sum(p.shape[0] for p in pieces)
    small = jnp.concatenate(pieces + [jnp.zeros((spare, D_MODEL), F32)]).reshape(4, 2 * SMALL_ROWS, D_MODEL)
    by_owner = [_by_owner(n, g[n]) for n in tail] + [small]
    got = _pair_exchange(by_owner)
    transit = [BF16] * len(tail) + [F32]
    chip_sums, own_sums = zip(*[_pair_sum(a, b, place, t, "pair_sum_" + n)
                                for n, a, b, t in zip(tail + ("small",), by_owner, got, transit)])
    landed = _chip_exchange(list(chip_sums))
    halves = {n: _chip_sum(o, t, "chip_sum_" + n) for n, o, t in zip(tail + ("small",), own_sums, landed)}
    early_landed = _chip_wait(*flight["chip"], grad_x)
    for n, o, t in zip(early, flight["own"], early_landed):
        halves[n] = _chip_sum(o, t, "chip_sum_" + n)
    *others, small_all = _final_exchange([halves[n] for n in BIG], halves["small"])
    small_all = small_all.reshape(N_DEV * SMALL_ROWS, D_MODEL)
    grads, row = {}, 0
    for n in SMALL:
        shape = (3, 4 * w[n].shape[-1]) if n == "conv_w" else w[n].shape[1:] if n != "norm_final_g" else w[n].shape
        size = math.prod(shape)
        grads[n] = small_all[row:row + _tile_rows(size)].reshape(-1)[:size].reshape(shape)
        row += _tile_rows(size)
    loss = small_all[row, 0]
    cw = w["conv_w"].shape[-1]
    grads["conv_w"] = lax.dynamic_slice_in_dim(grads["conv_w"], xy * cw, cw, axis=1)
    grads = {n: _view(n, grads[n].reshape(w[n].shape)) for n in SMALL}
    wv, mv, vv = ({n: _view(n, t[n]) for n in WEIGHTS} for t in (w, m, v))

    delta, new_m, new_v = {}, {}, {}
    for n, other in zip(BIG, others):
        two_d = lambda t: t.reshape(t.shape[-2:])
        grads[n], delta[n], new_m[n], new_v[n] = _adamw_halves(
            two_d(wv[n]), halves[n], other, two_d(mv[n]), two_d(vv[n]), place, "adamw_" + n)
    for group, name in ((("b_re", "b_im"), "adamw_b"), (tuple(n for n in SMALL if n not in ("b_re", "b_im")), "adamw_small")):
        row = lambda t: t.reshape(1, -1) if t.ndim == 1 else t
        d_, m_, v_ = _adamw_many(*[[row(t[n]) for n in group] for t in (wv, grads, mv, vv)], name)
        for n, dn, mn, vn in zip(group, d_, m_, v_):
            delta[n], new_m[n], new_v[n] = (t.reshape(wv[n].shape) for t in (dn, mn, vn))
    natural = lambda t: [_unview(n, t[n].reshape(wv[n].shape)) for n in WEIGHTS]
    return (loss, grad_x[None], *natural(grads), *natural(delta), *natural(new_m), *natural(new_v))
```

```python
import functools
import math

import jax
import jax.numpy as jnp
import numpy as np
from jax import lax
from jax.experimental import pallas as pl
from jax.experimental.pallas import tpu as pltpu

F32 = jnp.float32
BF16 = jnp.bfloat16

D_MODEL = 1024
N_Q_HEADS = 8
N_KV_HEADS = 2
HEAD_DIM = 64
ATTN_WIDTH = 512
KV_WIDTH = 128
QKV_WIDTH = ATTN_WIDTH + 2 * KV_WIDTH
WINDOW = 128
BLOCK = 128
ROPE_DIM = 16
ROPE_THETA = 500000.0
SCORE_SCALE = HEAD_DIM ** -0.5
SSM_WIDTH = 512
SSM_GROUP = 16
N_SSM_GROUPS = 32
SSM_STATE = 64
IN_WIDTH = 1280
D_FF = 2816
EPS = 1e-6
ADAM_LR = 0.001
ADAM_B1 = 0.9
ADAM_B2 = 0.999
ADAM_EPS = 1e-08
ADAM_WD = 0.01
ADAM_STEP = 10

VMEM_BYTES_V7X = 64 * 1024 * 1024
SUBLANES = 8
LANES = 128
SSM_CB = 4
SSM_CH = 128
SSM_ST = 512
N_SEG = SUBLANES

NN = (((1,), (0,)), ((), ()))
NT = (((1,), (1,)), ((), ()))
TN = (((0,), (0,)), ((), ()))


def _params(sem=None, vmem_mb=48):
    limit = vmem_mb * 1024 * 1024
    assert limit < VMEM_BYTES_V7X
    return pltpu.CompilerParams(dimension_semantics=sem, vmem_limit_bytes=limit)


def _dg(a, b, dims):
    return lax.dot_general(a, b, dims, preferred_element_type=F32)


def _sigmoid(x):
    return 1.0 / (1.0 + jnp.exp(-x))


_SQRT_HALF = 0.7071067811865476
_INV_SQRT_2PI = 0.3989422804014327


def _gelu(x):
    return 0.5 * x * (1.0 + lax.erf(x * _SQRT_HALF))


def _gelu_grad(x):
    return 0.5 * (1.0 + lax.erf(x * _SQRT_HALF)) + x * (_INV_SQRT_2PI * jnp.exp(-0.5 * x * x))


def _mm_tn(a, b, tm, tn, name):
    k, m = a.shape
    n = b.shape[1]

    def body(a_ref, b_ref, o_ref):
        o_ref[...] = _dg(a_ref[...], b_ref[...], TN)

    return pl.pallas_call(
        body, grid=(m // tm, n // tn),
        in_specs=[pl.BlockSpec((k, tm), lambda i, j: (0, i)), pl.BlockSpec((k, tn), lambda i, j: (0, j))],
        out_specs=pl.BlockSpec((tm, tn), lambda i, j: (i, j)),
        out_shape=jax.ShapeDtypeStruct((m, n), F32), name=name,
        compiler_params=_params(("parallel", "parallel")),
    )(a, b)


def _mm_nn_cols(a, b4, tm, name):
    m, k = a.shape
    s, _, n = b4.shape

    def body(a_ref, b_ref, o_ref):
        o_ref[...] = _dg(a_ref[...], b_ref[...], NN)

    return pl.pallas_call(
        body, grid=(m // tm, s),
        in_specs=[pl.BlockSpec((tm, k), lambda i, j: (i, 0)), pl.BlockSpec((None, k, n), lambda i, j: (j, 0, 0))],
        out_specs=pl.BlockSpec((tm, n), lambda i, j: (i, j)),
        out_shape=jax.ShapeDtypeStruct((m, s * n), F32), name=name,
        compiler_params=_params(("parallel", "parallel")),
    )(a, b4)


def _mm_tn_cols(a, b2, s, tm, name):
    k, m = a.shape
    h, _, wide = b2.shape
    per = s // h
    n = wide // per

    def body(a_ref, b_ref, o_ref):
        o_ref[...] = _dg(a_ref[...], b_ref[...], TN)

    return pl.pallas_call(
        body, grid=(s, m // tm),
        in_specs=[pl.BlockSpec((k, tm), lambda j, i: (0, i)),
                  pl.BlockSpec((None, k, n), lambda j, i: (j // per, 0, j % per))],
        out_specs=pl.BlockSpec((None, tm, n), lambda j, i: (j, i, 0)),
        out_shape=jax.ShapeDtypeStruct((s, m, n), F32), name=name,
        compiler_params=_params(("parallel", "parallel")),
    )(a, b2)


TM_EW = 256


def _rms_bwd_vals(xv, gv, dy):
    r = lax.rsqrt(jnp.mean(xv * xv, axis=-1, keepdims=True) + EPS)
    xh = xv * r
    dxh = dy * gv
    dx = r * (dxh - xh * jnp.mean(dxh * xh, axis=-1, keepdims=True))
    return dx, dy * xh


TM_FUSED = 512
TM_LOSS = 256


def _rms_vals(xv, gv):
    return xv * lax.rsqrt(jnp.mean(xv * xv, axis=-1, keepdims=True) + EPS) * gv


def _rope_blocks(src, dst, c, lo, hi):
    nq = ATTN_WIDTH // LANES
    for blk in range(nq + 1):
        t = src[:, blk * LANES:(blk + 1) * LANES]
        rot = t * c + pltpu.roll(t, LANES - 8, 1) * lo + pltpu.roll(t, 8, 1) * hi
        dst[:, blk * LANES:(blk + 1) * LANES] = (rot * SCORE_SCALE if blk < nq else rot).astype(BF16)
    dst[:, (nq + 1) * LANES:] = src[:, (nq + 1) * LANES:].astype(BF16)


def _rms_mm_rope(x, g, wt, tabs, name):
    l, d = x.shape
    n = wt.shape[0]

    def body(x_ref, g_ref, w_ref, c_ref, lo_ref, hi_ref, h_ref, qkv_ref, u_ref):
        h = _rms_vals(x_ref[...], g_ref[...]).astype(BF16)
        h_ref[...] = h
        out = _dg(h, w_ref[...], NT)
        _rope_blocks(out[:, :QKV_WIDTH], qkv_ref, c_ref[...], lo_ref[...], hi_ref[...])
        u_ref[...] = out[:, QKV_WIDTH:]

    row = lambda width: pl.BlockSpec((TM_FUSED, width), lambda i: (i, 0))
    return pl.pallas_call(
        body, grid=(l // TM_FUSED,),
        in_specs=[row(d), pl.BlockSpec((1, d), lambda i: (0, 0)), pl.BlockSpec((n, d), lambda i: (0, 0)),
                  row(LANES), row(LANES), row(LANES)],
        out_specs=[row(d), row(QKV_WIDTH), row(n - QKV_WIDTH)],
        out_shape=[jax.ShapeDtypeStruct((l, d), BF16), jax.ShapeDtypeStruct((l, QKV_WIDTH), BF16),
                   jax.ShapeDtypeStruct((l, n - QKV_WIDTH), F32)],
        name=name, compiler_params=_params(("parallel",)),
    )(x, g, wt, *tabs)


def _mix_mm_res_rms(attn, ys, g_attn, g_ssm, b, res, g, name):
    l, w = attn.shape
    d = b.shape[1]

    def body(a_ref, y_ref, ga_ref, gs_ref, b_ref, r_ref, g_ref, m_ref, x_ref, h_ref):
        m_ref[:, :w] = _rms_vals(a_ref[...], ga_ref[...]).astype(BF16)
        m_ref[:, w:] = _rms_vals(y_ref[...], gs_ref[...]).astype(BF16)
        xv = r_ref[...] + _dg(m_ref[...], b_ref[...], NN)
        x_ref[...] = xv
        h_ref[...] = _rms_vals(xv, g_ref[...]).astype(BF16)

    row = lambda width: pl.BlockSpec((TM_FUSED, width), lambda i: (i, 0))
    vec = lambda width: pl.BlockSpec((1, width), lambda i: (0, 0))
    return pl.pallas_call(
        body, grid=(l // TM_FUSED,),
        in_specs=[row(w), row(w), vec(w), vec(w), pl.BlockSpec((2 * w, d), lambda i: (0, 0)), row(d), vec(d)],
        out_specs=[row(2 * w), row(d), row(d)],
        out_shape=[jax.ShapeDtypeStruct((l, 2 * w), BF16), jax.ShapeDtypeStruct((l, d), F32),
                   jax.ShapeDtypeStruct((l, d), BF16)],
        name=name, compiler_params=_params(("parallel",)),
    )(attn, ys, g_attn, g_ssm, b, res, g)


def _mm_res_loss(a, b, res, g, target):
    l, k = a.shape
    d = b.shape[1]

    def body(a_ref, b_ref, r_ref, g_ref, t_ref, loss_ref, dx_ref, dxb_ref, dg_ref):
        xv = r_ref[...] + _dg(a_ref[...], b_ref[...], NN)
        gv = g_ref[...]
        r = lax.rsqrt(jnp.mean(xv * xv, axis=-1, keepdims=True) + EPS)
        xh = xv * r
        e = xh * gv - t_ref[...]
        part = jnp.sum(jnp.sum(e * e, axis=1, keepdims=True), axis=0, keepdims=True) * (0.5 / d)
        dy = e * (1.0 / d)
        dxh = dy * gv
        dx = r * (dxh - xh * jnp.mean(dxh * xh, axis=-1, keepdims=True))
        dx_ref[...] = dx
        dxb_ref[...] = dx.astype(BF16)

        @pl.when(pl.program_id(0) == 0)
        def _():
            dg_ref[...] = jnp.zeros_like(dg_ref)
            loss_ref[...] = jnp.zeros_like(loss_ref)

        dg_ref[...] += jnp.sum(dy * xh, axis=0, keepdims=True)
        loss_ref[...] += part

    row = lambda width: pl.BlockSpec((TM_LOSS, width), lambda i: (i, 0))
    vec = pl.BlockSpec((1, d), lambda i: (0, 0))
    return pl.pallas_call(
        body, grid=(l // TM_LOSS,),
        in_specs=[row(k), pl.BlockSpec((k, d), lambda i: (0, 0)), row(d), vec, row(d)],
        out_specs=[pl.BlockSpec((1, 1), lambda i: (0, 0)), row(d), row(d), vec],
        out_shape=[jax.ShapeDtypeStruct((1, 1), F32), jax.ShapeDtypeStruct((l, d), F32),
                   jax.ShapeDtypeStruct((l, d), BF16), jax.ShapeDtypeStruct((1, d), F32)],
        name="mm_down_loss", compiler_params=_params(("arbitrary",)),
    )(a, b, res, g, target)


def _mm_rms_bwd(a, b, a_spec, b_spec, matmul, x, g, res, name):
    l, d = x.shape

    def body(a_ref, b_ref, x_ref, g_ref, res_ref, dx_ref, dxb_ref, dg_ref):
        dx, dgr = _rms_bwd_vals(x_ref[...], g_ref[...], matmul(a_ref, b_ref))
        dx = dx + res_ref[...]
        dx_ref[...] = dx
        dxb_ref[...] = dx.astype(BF16)

        @pl.when(pl.program_id(0) == 0)
        def _():
            dg_ref[...] = jnp.zeros_like(dg_ref)

        dg_ref[...] += jnp.sum(dgr, axis=0, keepdims=True)

    row = pl.BlockSpec((TM_FUSED, d), lambda i: (i, 0))
    vec = pl.BlockSpec((1, d), lambda i: (0, 0))
    return pl.pallas_call(
        body, grid=(l // TM_FUSED,), in_specs=[a_spec, b_spec, row, vec, row], out_specs=[row, row, vec],
        out_shape=[jax.ShapeDtypeStruct((l, d), F32), jax.ShapeDtypeStruct((l, d), BF16),
                   jax.ShapeDtypeStruct((1, d), F32)],
        name=name, compiler_params=_params(("arbitrary",)),
    )(a, b, x, g, res)


def _mm_nn_rms_bwd(a, b, x, g, res, name):
    return _mm_rms_bwd(a, b, pl.BlockSpec((TM_FUSED, a.shape[1]), lambda i: (i, 0)),
                       pl.BlockSpec(b.shape, lambda i: (0, 0)),
                       lambda a_ref, b_ref: _dg(a_ref[...], b_ref[...], NN), x, g, res, name)


def _mm_cols_rms_bwd(a2, b4, x, g, res, name):
    h, _, wide = a2.shape
    s, _, n = b4.shape
    per = s // h

    def matmul(a_ref, b_ref):
        acc = None
        for j in range(s):
            part = _dg(a_ref[j // per, :, (j % per) * n:(j % per + 1) * n], b_ref[j], NT)
            acc = part if acc is None else acc + part
        return acc

    return _mm_rms_bwd(a2, b4, pl.BlockSpec((h, TM_FUSED, wide), lambda i: (0, i, 0)),
                       pl.BlockSpec(b4.shape, lambda i: (0, 0, 0), pipeline_mode=pl.Buffered(1)),
                       matmul, x, g, res, name)


def _mm_mix_bwd(dx, b, attn, ys, g_attn, g_ssm, name):
    l, w = attn.shape
    d = dx.shape[1]

    def body(dx_ref, b_ref, a_ref, y_ref, ga_ref, gs_ref, da_ref, dy_ref, dga_ref, dgs_ref):
        @pl.when(pl.program_id(0) == 0)
        def _():
            dga_ref[...] = jnp.zeros_like(dga_ref)
            dgs_ref[...] = jnp.zeros_like(dgs_ref)

        dm = _dg(dx_ref[...], b_ref[...], NT)
        for src, gr, off, dst, dgr in ((a_ref, ga_ref, 0, da_ref, dga_ref), (y_ref, gs_ref, w, dy_ref, dgs_ref)):
            dxv, dg_rows = _rms_bwd_vals(src[...], gr[...], dm[:, off:off + w])
            dst[...] = dxv
            dgr[...] += jnp.sum(dg_rows, axis=0, keepdims=True)

    row = lambda width: pl.BlockSpec((TM_FUSED, width), lambda i: (i, 0))
    vec = pl.BlockSpec((1, w), lambda i: (0, 0))
    return pl.pallas_call(
        body, grid=(l // TM_FUSED,),
        in_specs=[row(d), pl.BlockSpec((2 * w, d), lambda i: (0, 0)), row(w), row(w), vec, vec],
        out_specs=[row(w), row(w), vec, vec],
        out_shape=[jax.ShapeDtypeStruct((l, w), F32), jax.ShapeDtypeStruct((l, w), F32),
                   jax.ShapeDtypeStruct((1, w), F32), jax.ShapeDtypeStruct((1, w), F32)],
        name=name, compiler_params=_params(("arbitrary",)),
    )(dx, b, attn, ys, g_attn, g_ssm)


def _rope_tables(l):
    half = ROPE_DIM // 2
    f32 = np.float32
    inv_freq = np.power(f32(ROPE_THETA), -np.arange(half, dtype=f32) / f32(half))
    ang = np.arange(l, dtype=f32)[:, None] * inv_freq[None, :]
    cos, sin = np.cos(ang), np.sin(ang)
    ones = np.ones((l, HEAD_DIM - ROPE_DIM), f32)
    zeros = np.zeros((l, HEAD_DIM - ROPE_DIM), f32)
    zh = np.zeros((l, half), f32)
    c = np.concatenate([cos, cos, ones], axis=1)
    s_lo = np.concatenate([-sin, zh, zeros], axis=1)
    s_hi = np.concatenate([zh, sin, zeros], axis=1)
    return tuple(jnp.asarray(np.tile(t, (1, LANES // HEAD_DIM)), F32) for t in (c, s_lo, s_hi))


def _rope_bwd(dq, dkv, du_ssm, dpre, d_skip, tabs):
    l = dq.shape[0]
    nq = ATTN_WIDTH // LANES

    def body(dq_ref, dkv_ref, du_ref, dpre_ref, ds_ref, c_ref, lo_ref, hi_ref, o_ref):
        c, lo, hi = c_ref[...], lo_ref[...], hi_ref[...]
        for blk in range(nq + 1):
            t = dq_ref[:, blk * LANES:(blk + 1) * LANES] if blk < nq else dkv_ref[:, :KV_WIDTH]
            g = t * c + pltpu.roll(t * lo, 8, 1) + pltpu.roll(t * hi, LANES - 8, 1)
            o_ref[:, blk * LANES:(blk + 1) * LANES] = g.astype(BF16)
        o_ref[:, (nq + 1) * LANES:QKV_WIDTH] = dkv_ref[:, KV_WIDTH:].astype(BF16)
        o_ref[:, QKV_WIDTH:] = (du_ref[...] + dpre_ref[...] * ds_ref[...]).astype(BF16)

    tab = pl.BlockSpec((TM_EW, LANES), lambda i: (i, 0))
    wide = pl.BlockSpec((TM_EW, SSM_WIDTH), lambda i: (i, 0))
    return pl.pallas_call(
        body, grid=(l // TM_EW,),
        in_specs=[wide, pl.BlockSpec((TM_EW, 2 * KV_WIDTH), lambda i: (i, 0)), wide, wide,
                  pl.BlockSpec((1, SSM_WIDTH), lambda i: (0, 0)), tab, tab, tab],
        out_specs=pl.BlockSpec((TM_EW, IN_WIDTH), lambda i: (i, 0)),
        out_shape=jax.ShapeDtypeStruct((l, IN_WIDTH), BF16), name="rope_bwd",
        compiler_params=_params(("parallel",)),
    )(dq, dkv, du_ssm, dpre, d_skip, *tabs)


_Q_COLS = ATTN_WIDTH // LANES
_NEG = -1e30


def _window_specs(nb, width, col):
    return [
        pl.BlockSpec((BLOCK, width), lambda n: (jnp.maximum(n - 1, 0), col)),
        pl.BlockSpec((BLOCK, width), lambda n: (n, col)),
        pl.BlockSpec((BLOCK, width), lambda n: (jnp.minimum(n + 1, nb - 1), col)),
    ]


def _stacked_sink(sink_ref, heads):
    rid = lax.broadcasted_iota(jnp.int32, (len(heads) * BLOCK, 1), 0)
    sk = jnp.full(rid.shape, sink_ref[0, heads[-1]], F32)
    for g in range(len(heads) - 2, -1, -1):
        sk = jnp.where(rid < (g + 1) * BLOCK, sink_ref[0, heads[g]], sk)
    return sk


def _attn_fwd(qkv, sink):
    l = qkv.shape[0]
    nb = l // BLOCK
    grp = N_Q_HEADS // N_KV_HEADS

    def body(sink_ref, q_ref, k0, k1, k2, v0, v1, v2, o_ref, lse_ref):
        n = pl.program_id(0)
        q = q_ref[...]
        kw = jnp.concatenate([k0[...], k1[...], k2[...]], axis=0)
        vw = jnp.concatenate([v0[...], v1[...], v2[...]], axis=0)
        row = lax.broadcasted_iota(jnp.int32, (grp * BLOCK, 3 * BLOCK), 0)
        col = lax.broadcasted_iota(jnp.int32, (grp * BLOCK, 3 * BLOCK), 1)
        valid = jnp.abs(col - BLOCK - (row & (BLOCK - 1))) <= WINDOW
        valid &= jnp.logical_not((n == 0) & (col < BLOCK))
        valid &= jnp.logical_not((n == nb - 1) & (col >= 2 * BLOCK))
        for hk in range(N_KV_HEADS):
            heads = range(hk * grp, (hk + 1) * grp)
            qs = jnp.concatenate([q[:, h * HEAD_DIM:(h + 1) * HEAD_DIM] for h in heads], axis=0)
            kh = kw[:, hk * HEAD_DIM:(hk + 1) * HEAD_DIM]
            vh = vw[:, hk * HEAD_DIM:(hk + 1) * HEAD_DIM]
            s = jnp.where(valid, _dg(qs, kh, NT), _NEG)
            sk = _stacked_sink(sink_ref, heads)
            m = jnp.maximum(jnp.max(s, axis=1, keepdims=True), sk)
            p = jnp.exp(s - m)
            denom = jnp.sum(p, axis=1, keepdims=True) + jnp.exp(sk - m)
            o = _dg((p / denom).astype(BF16), vh, NN)
            lse = m + jnp.log(denom)
            for g, h in enumerate(heads):
                o_ref[:, h * HEAD_DIM:(h + 1) * HEAD_DIM] = o[g * BLOCK:(g + 1) * BLOCK]
                lse_ref[:, h:h + 1] = lse[g * BLOCK:(g + 1) * BLOCK]

    return pl.pallas_call(
        body, grid=(nb,),
        in_specs=[pl.BlockSpec(memory_space=pltpu.SMEM),
                  pl.BlockSpec((BLOCK, ATTN_WIDTH), lambda n: (n, 0))]
        + _window_specs(nb, KV_WIDTH, _Q_COLS) + _window_specs(nb, KV_WIDTH, _Q_COLS + 1),
        out_specs=[pl.BlockSpec((BLOCK, ATTN_WIDTH), lambda n: (n, 0)),
                   pl.BlockSpec((BLOCK, N_Q_HEADS), lambda n: (n, 0))],
        out_shape=[jax.ShapeDtypeStruct((l, ATTN_WIDTH), F32), jax.ShapeDtypeStruct((l, N_Q_HEADS), F32)],
        name="attn_fwd", compiler_params=_params(("parallel",)),
    )(sink, qkv, qkv, qkv, qkv, qkv, qkv, qkv)


def _attn_bwd(qkv, attn, dattn, lse, sink):
    l = qkv.shape[0]
    nb = l // BLOCK
    grp = N_Q_HEADS // N_KV_HEADS
    win = 3 * BLOCK

    def body(sink_ref, q_ref, k0, k1, k2, v0, v1, v2, o_ref, d_ref, l_ref, dq_ref, dkv_ref, dsink_ref, ring_ref):
        n = pl.program_id(0)

        @pl.when(n == 0)
        def _():
            dsink_ref[...] = jnp.zeros_like(dsink_ref)
            ring_ref[...] = jnp.zeros_like(ring_ref)

        @pl.when(n < nb)
        def _():
            first, last = n == 0, n == nb - 1
            cat = lambda a, b, c: jnp.concatenate([a[...], b[...], c[...]], axis=0)
            q, kw, vw = q_ref[...], cat(k0, k1, k2), cat(v0, v1, v2)
            dov = d_ref[...]
            prod = o_ref[...] * dov
            dob = dov.astype(BF16)
            lse = l_ref[...]
            row = lax.broadcasted_iota(jnp.int32, (grp * BLOCK, win), 0)
            col = lax.broadcasted_iota(jnp.int32, (grp * BLOCK, win), 1)
            valid = jnp.abs(col - BLOCK - (row & (BLOCK - 1))) <= WINDOW
            valid &= jnp.logical_not(first & (col < BLOCK))
            valid &= jnp.logical_not(last & (col >= 2 * BLOCK))

            dsink_parts, dks, dvs = [], [], []
            for hk in range(N_KV_HEADS):
                heads = range(hk * grp, (hk + 1) * grp)
                ksl = slice(hk * HEAD_DIM, (hk + 1) * HEAD_DIM)
                hsl = [slice(h * HEAD_DIM, (h + 1) * HEAD_DIM) for h in heads]
                stack = lambda parts: jnp.concatenate(parts, axis=0)
                qs = stack([q[:, s_] for s_ in hsl])
                dos = stack([dob[:, s_] for s_ in hsl])
                deltas = stack([jnp.sum(prod[:, s_], axis=1, keepdims=True) for s_ in hsl])
                lses = stack([lse[:, h:h + 1] for h in heads])
                kh, vh = kw[:, ksl], vw[:, ksl]
                s = jnp.where(valid, _dg(qs, kh, NT), _NEG)
                p = jnp.exp(s - lses)
                dp = _dg(dos, vh, NT)
                ds = (p * (dp - deltas)).astype(BF16)
                dq = _dg(ds, kh, NN) * SCORE_SCALE
                sink_rows = jnp.exp(_stacked_sink(sink_ref, heads) - lses) * deltas
                for g in range(grp):
                    dq_ref[:, hsl[g]] = dq[g * BLOCK:(g + 1) * BLOCK]
                    dsink_parts.append(jnp.sum(sink_rows[g * BLOCK:(g + 1) * BLOCK], axis=0, keepdims=True))
                dks.append(_dg(ds, qs, TN))
                dvs.append(_dg(p.astype(BF16), dos, TN))
            dsink_ref[...] -= jnp.concatenate(dsink_parts, axis=1)
            part = jnp.concatenate(dks + dvs, axis=1)
            ring_ref[(n + 2) % 3] += part[0:BLOCK]
            ring_ref[n % 3] += part[BLOCK:2 * BLOCK]
            ring_ref[(n + 1) % 3] = part[2 * BLOCK:]

        @pl.when(n >= 1)
        def _():
            dkv_ref[...] = ring_ref[(n + 2) % 3]

    centre = lambda n: jnp.minimum(n, nb - 1)
    window = lambda width, col: [
        pl.BlockSpec((BLOCK, width), lambda n: (jnp.maximum(centre(n) - 1, 0), col)),
        pl.BlockSpec((BLOCK, width), lambda n: (centre(n), col)),
        pl.BlockSpec((BLOCK, width), lambda n: (jnp.minimum(centre(n) + 1, nb - 1), col))]
    own = lambda width: pl.BlockSpec((BLOCK, width), lambda n: (centre(n), 0))
    return pl.pallas_call(
        body, grid=(nb + 1,),
        in_specs=[pl.BlockSpec(memory_space=pltpu.SMEM), own(ATTN_WIDTH)]
        + window(KV_WIDTH, _Q_COLS) + window(KV_WIDTH, _Q_COLS + 1)
        + [own(ATTN_WIDTH), own(ATTN_WIDTH), own(N_Q_HEADS)],
        out_specs=[own(ATTN_WIDTH), pl.BlockSpec((BLOCK, 2 * KV_WIDTH), lambda n: (jnp.maximum(n - 1, 0), 0)),
                   pl.BlockSpec((1, N_Q_HEADS), lambda n: (0, 0))],
        out_shape=[jax.ShapeDtypeStruct((l, ATTN_WIDTH), F32), jax.ShapeDtypeStruct((l, 2 * KV_WIDTH), F32),
                   jax.ShapeDtypeStruct((1, N_Q_HEADS), F32)],
        scratch_shapes=[pltpu.VMEM((3, BLOCK, 2 * KV_WIDTH), F32)],
        name="attn_bwd", compiler_params=_params(("arbitrary",)),
    )(sink, qkv, qkv, qkv, qkv, qkv, qkv, qkv, attn, dattn, lse)


def _ssm_disc(a_re, a_im, log_step, b_re, b_im):
    step = jnp.exp(log_step)[..., None]
    mag = jnp.exp(a_re * step)
    lb_re, lb_im = mag * jnp.cos(a_im * step), mag * jnp.sin(a_im * step)
    nr, ni = lb_re - 1.0, lb_im
    den = a_re * a_re + a_im * a_im
    f_re = ((nr * a_re + ni * a_im) / den)[..., None]
    f_im = ((ni * a_re - nr * a_im) / den)[..., None]
    return lb_re, lb_im, f_re * b_re - f_im * b_im, f_re * b_im + f_im * b_re


def _ssm_pack(lb_re, lb_im, bb_re, bb_im, c_re, c_im):
    eye = jnp.eye(SSM_CH // SSM_GROUP, dtype=F32)
    ng = SSM_CH // SSM_GROUP

    def diag_b(bb):
        t = bb.reshape(2, SSM_CB, ng, SSM_STATE, SSM_GROUP)
        return jnp.einsum('dkgpc,gh->dkgchp', t, eye).reshape(2, SSM_CB, SSM_CH, SSM_ST)

    def diag_c(cc):
        t = cc.reshape(2, SSM_CB, ng, SSM_GROUP, SSM_STATE)
        return jnp.einsum('dkgcp,gh->dkhpgc', t, eye).reshape(2, SSM_CB, SSM_ST, SSM_CH)

    bcat = jnp.concatenate([diag_b(bb_re), diag_b(bb_im)], axis=-1)
    ccat = jnp.concatenate([diag_c(c_re), -diag_c(c_im)], axis=-2)
    lam_re = lb_re.reshape(2, SSM_CB, 1, SSM_ST)
    lam_im = lb_im.reshape(2, SSM_CB, 1, SSM_ST)
    return bcat, ccat, lam_re, lam_im


def _ssm_unpack(dbcat, dccat, dlam_re, dlam_im):
    ng = SSM_CH // SSM_GROUP
    eye = jnp.eye(ng, dtype=F32)

    def undiag_b(t):
        t = t.reshape(2, SSM_CB, ng, SSM_GROUP, ng, SSM_STATE)
        return jnp.einsum('dkgchp,gh->dkgpc', t, eye).reshape(2, N_SSM_GROUPS, SSM_STATE, SSM_GROUP)

    def undiag_c(t):
        t = t.reshape(2, SSM_CB, ng, SSM_STATE, ng, SSM_GROUP)
        return jnp.einsum('dkhpgc,gh->dkgcp', t, eye).reshape(2, N_SSM_GROUPS, SSM_GROUP, SSM_STATE)

    dbb_re, dbb_im = undiag_b(dbcat[..., :SSM_ST]), undiag_b(dbcat[..., SSM_ST:])
    dc_re, dc_im = undiag_c(dccat[:, :, :SSM_ST]), -undiag_c(dccat[:, :, SSM_ST:])
    shape = (2, N_SSM_GROUPS, SSM_STATE)
    return dlam_re.reshape(shape), dlam_im.reshape(shape), dbb_re, dbb_im, dc_re, dc_im


def _to_segments(t):
    l, w = t.shape
    return t.reshape(N_SEG, l // N_SEG, w).transpose(1, 0, 2).reshape(l, w)


def _from_segments(t):
    l, w = t.shape
    return t.reshape(l // N_SEG, N_SEG, w).transpose(1, 0, 2).reshape(l, w)


SSM_RC = 256
SSM_JC = SSM_RC // N_SEG
_RE, _IM = pl.ds(0, SSM_ST), pl.ds(SSM_ST, SSM_ST)


def _cfma(ar, ai, xr, xi, br, bi):
    return ar * xr - ai * xi + br, ar * xi + ai * xr + bi


def _chunk_rows(ci, rev, nc):
    start = jnp.where(rev, (nc - 1 - ci) * SSM_RC, ci * SSM_RC)
    return pl.ds(pl.multiple_of(start, SSM_RC), SSM_RC)


def _scan_chunk(src, dst, ar, ai, rev, nj, ci, carry, prev_ref=None):
    def rows_of(staged, j, k):
        at = jnp.where(rev, SSM_JC - 1 - k, k) if staged else j
        return pl.ds(pl.multiple_of(at * N_SEG, N_SEG), N_SEG)

    for k in range(SSM_JC):
        jj = ci * SSM_JC + k
        j = jnp.where(rev, nj - 1 - jj, jj)
        rows = rows_of(src[1], j, k)
        nr, ni = _cfma(ar, ai, carry[0], carry[1], src[0][rows, _RE], src[0][rows, _IM])
        if dst is not None:
            rows = rows_of(dst[1], j, k)
            dst[0][rows, _RE] = nr
            dst[0][rows, _IM] = ni
        if prev_ref is None:
            carry = (nr, ni)
            continue
        jp = jnp.where(rev, j - 1, j + 1)
        if k == SSM_JC - 1:
            inside = jnp.where((jp >= 0) & (jp < nj), 1.0, 0.0)
            jp = jnp.clip(jp, 0, nj - 1)
        prow = pl.ds(pl.multiple_of(jp * N_SEG, N_SEG), N_SEG)
        xr, xi = prev_ref[prow, _RE], prev_ref[prow, _IM]
        sr, si = nr * xr + ni * xi, ni * xr - nr * xi
        if k == SSM_JC - 1:
            sr, si = inside * sr, inside * si
        carry = (nr, ni, carry[2] + sr, carry[3] + si)
    return carry


def _segment_inits(ar, ai, end_r, end_i, rev, nj):
    pr, pi = ar, ai
    for _ in range(int(math.log2(nj))):
        pr, pi = pr * pr - pi * pi, 2.0 * pr * pi
    seg = lax.broadcasted_iota(jnp.int32, end_r.shape, 0)
    zero = jnp.zeros_like(end_r)

    def chain(shift, keep):
        ir, ii = zero, zero
        for _ in range(N_SEG - 1):
            tr, ti = _cfma(pr, pi, ir, ii, end_r, end_i)
            ir = jnp.where(keep, pltpu.roll(tr, shift, 0), 0.0)
            ii = jnp.where(keep, pltpu.roll(ti, shift, 0), 0.0)
        return ir, ii

    up_r, up_i = chain(1, seg >= 1)
    dn_r, dn_i = chain(N_SEG - 1, seg <= N_SEG - 2)
    return jnp.where(rev, dn_r, up_r), jnp.where(rev, dn_i, up_i)


def _ssm_specs(l):
    act = pl.BlockSpec((l, SSM_CH), lambda k, d: (0, k))
    bmat = pl.BlockSpec((None, None, SSM_CH, 2 * SSM_ST), lambda k, d: (d, k, 0, 0))
    cmat = pl.BlockSpec((None, None, 2 * SSM_ST, SSM_CH), lambda k, d: (d, k, 0, 0))
    lam = pl.BlockSpec((None, None, 1, SSM_ST), lambda k, d: (d, k, 0, 0))
    return act, bmat, cmat, lam


def _ssm_fwd(u_seg, bcat, ccat, lam_re, lam_im):
    l = u_seg.shape[0]
    nj = l // N_SEG
    nc = l // SSM_RC

    def body(u_ref, b_ref, c_ref, lr_ref, li_ref, y_ref, ub_ref, keep_ref, xs_ref, stage0, stage1, keep_sem):
        k, d = pl.program_id(0), pl.program_id(1)
        rev = d == 1
        shape = (N_SEG, SSM_ST)
        ar, ai = jnp.broadcast_to(lr_ref[...], shape), jnp.broadcast_to(li_ref[...], shape)
        zero = jnp.zeros(shape, F32)

        def inputs(ci, stage):
            rows = _chunk_rows(ci, rev, nc)
            ub = u_ref[rows, :].astype(BF16)
            ub_ref[rows, :] = ub
            bu = _dg(ub, b_ref[...], NN)
            stage[...] = bu
            xs_ref[rows, :] = bu

        def first(stage, ci, carry):
            return _scan_chunk((stage, True), None, ar, ai, rev, nj, ci, carry)

        def first_pass(t, carry):
            inputs(2 * t + 1, stage1)
            carry = first(stage0, 2 * t, carry)
            inputs(2 * t + 2, stage0)
            return first(stage1, 2 * t + 1, carry)

        inputs(0, stage0)
        carry = lax.fori_loop(0, nc // 2 - 1, first_pass, (zero, zero))
        inputs(nc - 1, stage1)
        carry = first(stage0, nc - 2, carry)
        end_r, end_i = first(stage1, nc - 1, carry)
        init = _segment_inits(ar, ai, end_r, end_i, rev, nj)

        @pl.when(d == 0)
        def _():
            y_ref[...] = jnp.zeros_like(y_ref)

        def outputs(ci):
            rows = _chunk_rows(ci, rev, nc)
            y_ref[rows, :] += _dg(xs_ref[rows, :].astype(BF16), c_ref[...], NN)
            pltpu.make_async_copy(xs_ref.at[rows], keep_ref.at[d, k, rows], keep_sem).start()

        def second(ci, carry):
            return _scan_chunk((xs_ref, False), (xs_ref, False), ar, ai, rev, nj, ci, carry)

        def second_pass(ci, carry):
            outputs(ci - 1)
            return second(ci, carry)

        lax.fori_loop(1, nc, second_pass, second(0, init))
        outputs(nc - 1)
        pltpu.make_async_copy(xs_ref, keep_ref.at[d, k], keep_sem).wait()

    act, bmat, cmat, lam = _ssm_specs(l)
    return pl.pallas_call(
        body, grid=(SSM_CB, 2), in_specs=[act, bmat, cmat, lam, lam], out_specs=[act, act, ANY],
        out_shape=[jax.ShapeDtypeStruct((l, SSM_WIDTH), F32), jax.ShapeDtypeStruct((l, SSM_WIDTH), BF16),
                   jax.ShapeDtypeStruct((2, SSM_CB, l, 2 * SSM_ST), F32)],
        scratch_shapes=[pltpu.VMEM((l, 2 * SSM_ST), F32), pltpu.VMEM((SSM_RC, 2 * SSM_ST), F32),
                        pltpu.VMEM((SSM_RC, 2 * SSM_ST), F32), pltpu.SemaphoreType.DMA],
        name="ssm_fwd", compiler_params=_params(("parallel", "arbitrary"), vmem_mb=56),
    )(u_seg, bcat.astype(BF16), ccat.astype(BF16), lam_re, lam_im)


def _ssm_bwd(u_seg, dy_seg, states, bcat, ccat, lam_re, lam_im):
    l = u_seg.shape[0]
    nj = l // N_SEG
    nc = l // SSM_RC

    def body(u_ref, dy_ref, keep_ref, b_ref, c_ref, lr_ref, li_ref,
             du_ref, db_ref, dc_ref, dlr_ref, dli_ref, xs_ref, gs_ref, dyb_ref, stage0, stage1, keep_sem):
        k, d = pl.program_id(0), pl.program_id(1)
        rev = d == 1
        back = jnp.logical_not(rev)
        shape = (N_SEG, SSM_ST)
        ar, ai = jnp.broadcast_to(lr_ref[...], shape), -jnp.broadcast_to(li_ref[...], shape)
        zero = jnp.zeros(shape, F32)
        fetch = pltpu.make_async_copy(keep_ref.at[d, k], xs_ref, keep_sem)
        fetch.start()

        def inputs(ci, stage):
            rows = _chunk_rows(ci, back, nc)
            dyb = dy_ref[rows, :].astype(BF16)
            dyb_ref[rows, :] = dyb
            dx = _dg(dyb, c_ref[...], NT)
            stage[...] = dx
            gs_ref[rows, :] = dx

        def first(stage, ci, carry):
            return _scan_chunk((stage, True), None, ar, ai, back, nj, ci, carry)

        def first_pass(t, carry):
            inputs(2 * t + 1, stage1)
            carry = first(stage0, 2 * t, carry)
            inputs(2 * t + 2, stage0)
            return first(stage1, 2 * t + 1, carry)

        inputs(0, stage0)
        carry = lax.fori_loop(0, nc // 2 - 1, first_pass, (zero, zero))
        inputs(nc - 1, stage1)
        carry = first(stage0, nc - 2, carry)
        end_r, end_i = first(stage1, nc - 1, carry)
        init = _segment_inits(ar, ai, end_r, end_i, back, nj)
        fetch.wait()
        db_ref[...] = jnp.zeros_like(db_ref)
        dc_ref[...] = jnp.zeros_like(dc_ref)

        @pl.when(d == 0)
        def _():
            du_ref[...] = jnp.zeros_like(du_ref)

        def outputs(ci, stage):
            rows = _chunk_rows(ci, back, nc)
            g = stage[...].astype(BF16)
            dc_ref[...] += _dg(xs_ref[rows, :].astype(BF16), dyb_ref[rows, :], TN)
            db_ref[...] += _dg(u_ref[rows, :], g, TN)
            du_ref[rows, :] += _dg(g, b_ref[...], NT)

        def second(ci, stage, carry):
            return _scan_chunk((gs_ref, False), (stage, True), ar, ai, back, nj, ci, carry, prev_ref=xs_ref)

        def second_pass(t, carry):
            outputs(2 * t, stage0)
            carry = second(2 * t + 1, stage1, carry)
            outputs(2 * t + 1, stage1)
            return second(2 * t + 2, stage0, carry)

        carry = lax.fori_loop(0, nc // 2 - 1, second_pass, second(0, stage0, init + (zero, zero)))
        outputs(nc - 2, stage0)
        gr, gi, acc_r, acc_i = second(nc - 1, stage1, carry)
        outputs(nc - 1, stage1)

        seg = lax.broadcasted_iota(jnp.int32, shape, 0)
        jb = jnp.where(rev, nj - 1, 0)
        erow = pl.ds(pl.multiple_of((nj - 1 - jb) * N_SEG, N_SEG), N_SEG)

        def before(t):
            up = jnp.where(seg >= 1, pltpu.roll(t, 1, 0), 0.0)
            down = jnp.where(seg <= N_SEG - 2, pltpu.roll(t, N_SEG - 1, 0), 0.0)
            return jnp.where(rev, down, up)

        init_r, init_i = before(xs_ref[erow, _RE]), before(xs_ref[erow, _IM])
        acc_r = acc_r + gr * init_r + gi * init_i
        acc_i = acc_i + gi * init_r - gr * init_i
        dlr_ref[...] = jnp.sum(acc_r, axis=0, keepdims=True)
        dli_ref[...] = jnp.sum(acc_i, axis=0, keepdims=True)

    act, bmat, cmat, lam = _ssm_specs(l)
    return pl.pallas_call(
        body, grid=(SSM_CB, 2), in_specs=[act, act, ANY, bmat, cmat, lam, lam],
        out_specs=[act, bmat, cmat, lam, lam],
        out_shape=[jax.ShapeDtypeStruct((l, SSM_WIDTH), F32),
                   jax.ShapeDtypeStruct(bcat.shape, F32), jax.ShapeDtypeStruct(ccat.shape, F32),
                   jax.ShapeDtypeStruct(lam_re.shape, F32), jax.ShapeDtypeStruct(lam_im.shape, F32)],
        scratch_shapes=[pltpu.VMEM((l, 2 * SSM_ST), F32), pltpu.VMEM((l, 2 * SSM_ST), F32),
                        pltpu.VMEM((l, SSM_CH), BF16),
                        pltpu.VMEM((SSM_RC, 2 * SSM_ST), F32), pltpu.VMEM((SSM_RC, 2 * SSM_ST), F32),
                        pltpu.SemaphoreType.DMA],
        name="ssm_bwd", compiler_params=_params(("parallel", "arbitrary"), vmem_mb=58),
    )(u_seg, dy_seg, states, bcat.astype(BF16), ccat.astype(BF16), lam_re, lam_im)


def _glu_fwd(y_ssm, u, d_skip, w_glu):
    l, w = u.shape

    def body(y_ref, u_ref, d_ref, w_ref, pre_ref, s_ref, ys_ref):
        pre = y_ref[...] + d_ref[...] * u_ref[...]
        z = _gelu(pre)
        s = _dg(z.astype(BF16), w_ref[...], NN)
        pre_ref[...] = pre
        s_ref[...] = s
        ys_ref[...] = z * _sigmoid(s)

    row = pl.BlockSpec((TM_EW, w), lambda i: (i, 0))
    out = jax.ShapeDtypeStruct((l, w), F32)
    return pl.pallas_call(
        body, grid=(l // TM_EW,),
        in_specs=[row, row, pl.BlockSpec((1, w), lambda i: (0, 0)), pl.BlockSpec((w, w), lambda i: (0, 0))],
        out_specs=[row, row, row], out_shape=[out, out, out], name="glu_fwd",
        compiler_params=_params(("parallel",)),
    )(y_ssm, u, d_skip, w_glu)


def _glu_bwd(pre, s, dys, u, d_skip, w_glu):
    l, w = u.shape

    def body(pre_ref, s_ref, dys_ref, u_ref, d_ref, w_ref, dpre_ref, z_ref, ds_ref, dd_ref):
        pre, dys = pre_ref[...], dys_ref[...]
        z = _gelu(pre)
        sig = _sigmoid(s_ref[...])
        ds = (dys * z * sig * (1.0 - sig)).astype(BF16)
        dz = dys * sig + _dg(ds, w_ref[...], NT)
        dpre = dz * _gelu_grad(pre)
        dpre_ref[...] = dpre
        z_ref[...] = z.astype(BF16)
        ds_ref[...] = ds

        @pl.when(pl.program_id(0) == 0)
        def _():
            dd_ref[...] = jnp.zeros_like(dd_ref)

        dd_ref[...] += jnp.sum(dpre * u_ref[...], axis=0, keepdims=True)

    row = pl.BlockSpec((TM_EW, w), lambda i: (i, 0))
    vec = pl.BlockSpec((1, w), lambda i: (0, 0))
    return pl.pallas_call(
        body, grid=(l // TM_EW,),
        in_specs=[row, row, row, row, vec, pl.BlockSpec((w, w), lambda i: (0, 0))],
        out_specs=[row, row, row, vec],
        out_shape=[jax.ShapeDtypeStruct((l, w), F32), jax.ShapeDtypeStruct((l, w), BF16),
                   jax.ShapeDtypeStruct((l, w), BF16), jax.ShapeDtypeStruct((1, w), F32)],
        name="glu_bwd", compiler_params=_params(("arbitrary",)),
    )(pre, s, dys, u, d_skip, w_glu)


TM_CV = 512
TC_CV = 256
TM_CF = 256
TC_CF = D_FF // 2
HALO = SUBLANES


def _conv_specs(l, col0, tm=TM_CV, tc=TC_CV):
    per = tm // HALO
    nh = l // HALO
    off = col0 // tc
    return [
        pl.BlockSpec((HALO, tc), lambda j, i: (jnp.maximum(i * per - 1, 0), j + off)),
        pl.BlockSpec((tm, tc), lambda j, i: (i, j + off)),
        pl.BlockSpec((HALO, tc), lambda j, i: (jnp.minimum((i + 1) * per, nh - 1), j + off)),
    ]


def _ext(prev_ref, mid_ref, next_ref, first, last):
    p = jnp.where(first, 0.0, prev_ref[...])
    n = jnp.where(last, 0.0, next_ref[...])
    return jnp.concatenate([p, mid_ref[...], n], axis=0)


def _shift_dn(t):
    return pltpu.roll(t, 1, 0)


def _shift_up(t):
    return pltpu.roll(t, t.shape[0] - 1, 0)


def _conv3(e, w_ref, b_ref):
    return w_ref[0:1, :] * _shift_dn(e) + w_ref[1:2, :] * e + w_ref[2:3, :] * _shift_up(e) + b_ref[...]


def _convffn_fwd(up_pre, conv_w, conv_b):
    l = up_pre.shape[0]
    tm, tc = TM_CF, TC_CF
    ni = l // tm
    wspec = lambda off: pl.BlockSpec((3, tc), lambda j, i: (0, j + off))
    bspec = lambda off: pl.BlockSpec((1, tc), lambda j, i: (0, j + off))
    voff = D_FF // tc

    def body(gp, gm, gn, vp, vm, vn, wg, bg, wv, bv, o_ref):
        i = pl.program_id(1)
        first, last = i == 0, i == ni - 1
        gate = _conv3(_ext(gp, gm, gn, first, last), wg, bg)[HALO:HALO + tm]
        val = _conv3(_ext(vp, vm, vn, first, last), wv, bv)[HALO:HALO + tm]
        o_ref[...] = (gate * _sigmoid(gate) * val).astype(BF16)

    return pl.pallas_call(
        body, grid=(D_FF // tc, ni),
        in_specs=_conv_specs(l, 0, tm, tc) + _conv_specs(l, D_FF, tm, tc)
        + [wspec(0), bspec(0), wspec(voff), bspec(voff)],
        out_specs=pl.BlockSpec((tm, tc), lambda j, i: (i, j)),
        out_shape=jax.ShapeDtypeStruct((l, D_FF), BF16), name="convffn_fwd",
        compiler_params=_params(("parallel", "parallel")),
    )(up_pre, up_pre, up_pre, up_pre, up_pre, up_pre, conv_w, conv_b, conv_w, conv_b)


HALO_B = 2 * SUBLANES


def _convffn_bwd(up_pre, dx2b, w_down, conv_w, conv_b):
    l = up_pre.shape[0]
    ni = l // TM_CV
    d = dx2b.shape[1]
    wspec = lambda off: pl.BlockSpec((3, TC_CV), lambda i, j: (0, j + off))
    bspec = lambda off: pl.BlockSpec((1, TC_CV), lambda i, j: (0, j + off))
    voff = D_FF // TC_CV
    swap = lambda spec: pl.BlockSpec(spec.block_shape, lambda i, j, f=spec.index_map: f(j, i))
    per, nh = TM_CV // HALO_B, l // HALO_B
    dx_specs = [pl.BlockSpec((HALO_B, d), lambda i, j: (jnp.maximum(i * per - 1, 0), 0)),
                pl.BlockSpec((TM_CV, d), lambda i, j: (i, 0)),
                pl.BlockSpec((HALO_B, d), lambda i, j: (jnp.minimum((i + 1) * per, nh - 1), 0))]

    def body(gp, gm, gn, vp, vm, vn, xp, xm, xn, wd, wg, bg, wv, bv, dup_ref, pg_ref, pv_ref):
        i = pl.program_id(0)
        first, last = i == 0, i == ni - 1
        ge, ve = _ext(gp, gm, gn, first, last), _ext(vp, vm, vn, first, last)
        zero = jnp.zeros((HALO_B, d), BF16)
        dx = jnp.concatenate([jnp.where(first, zero, xp[...]), xm[...], jnp.where(last, zero, xn[...])], axis=0)
        de = _dg(dx, wd[...], NT)[HALO_B - HALO:HALO_B + TM_CV + HALO]
        taps = [(_shift_dn(e), e, _shift_up(e)) for e in (ge, ve)]
        conv = lambda t, w_ref, b_ref: w_ref[0:1, :] * t[0] + w_ref[1:2, :] * t[1] + w_ref[2:3, :] * t[2] + b_ref[...]
        gate, val = conv(taps[0], wg, bg), conv(taps[1], wv, bv)
        sig = _sigmoid(gate)
        silu = gate * sig
        dgate = de * val * (sig + silu * (1.0 - sig))
        dval = de * silu
        mid = slice(HALO, HALO + TM_CV)
        rid = lax.broadcasted_iota(jnp.int32, (SUBLANES, TC_CV), 0)
        for half, (dup, tap, w_ref, p_ref) in enumerate(((dgate, taps[0], wg, pg_ref), (dval, taps[1], wv, pv_ref))):
            dpre = w_ref[0:1, :] * _shift_up(dup) + w_ref[1:2, :] * dup + w_ref[2:3, :] * _shift_dn(dup)
            dup_ref[half] = dpre[mid].astype(BF16)
            dm_ = dup[mid]
            sums = [jnp.sum(dm_ * t[mid], axis=0, keepdims=True) for t in tap]
            sums.append(jnp.sum(dm_, axis=0, keepdims=True))
            acc = jnp.zeros((SUBLANES, TC_CV), F32)
            for k, sk in enumerate(sums):
                acc = jnp.where(rid == k, sk, acc)
            p_ref[...] = acc

    par = pl.BlockSpec((None, SUBLANES, TC_CV), lambda i, j: (i, 0, j))
    dup, pg, pv = pl.pallas_call(
        body, grid=(ni, D_FF // TC_CV),
        in_specs=[swap(s) for s in _conv_specs(l, 0) + _conv_specs(l, D_FF)] + dx_specs
        + [pl.BlockSpec((TC_CV, d), lambda i, j: (j, 0)), wspec(0), bspec(0), wspec(voff), bspec(voff)],
        out_specs=[pl.BlockSpec((2, TM_CV, TC_CV), lambda i, j: (0, i, j)), par, par],
        out_shape=[jax.ShapeDtypeStruct((2, l, D_FF), BF16),
                   jax.ShapeDtypeStruct((ni, SUBLANES, D_FF), F32), jax.ShapeDtypeStruct((ni, SUBLANES, D_FF), F32)],
        name="convffn_bwd", compiler_params=_params(("parallel", "parallel")),
    )(up_pre, up_pre, up_pre, up_pre, up_pre, up_pre, dx2b, dx2b, dx2b, w_down, conv_w, conv_b, conv_w, conv_b)
    return dup, jnp.concatenate([jnp.sum(pg, axis=0), jnp.sum(pv, axis=0)], axis=1)


def _local_step(x, target, wb, sp, mixer_weights=None, late_weights=None, grads_ready=None,
                grads_next=None):
    l = x.shape[0]
    tabs = _rope_tables(l)
    disc = _ssm_disc(sp["a_re"], sp["a_im"], sp["log_step"], sp["b_re"], sp["b_im"])
    bcat, ccat, lam_re, lam_im = _ssm_pack(*disc, sp["c_re"], sp["c_im"])
    d_skip = sp["d_skip"].reshape(1, SSM_WIDTH)

    h, qkv, u = _rms_mm_rope(x, sp["norm_mix_g"], wb["w_in"], tabs, "mm_in")
    attn, lse = _attn_fwd(qkv, sp["sink"])
    y_seg, u_seg, states = _ssm_fwd(_to_segments(u), bcat, ccat, lam_re, lam_im)
    y_ssm = _from_segments(y_seg)
    if mixer_weights is not None:
        wb = dict(wb, **mixer_weights(attn))
    pre, s_glu, ys = _glu_fwd(y_ssm, u, d_skip, wb["w_glu"])
    mixed, x1, h2 = _mix_mm_res_rms(attn, ys, sp["norm_attn_g"], sp["norm_ssm_g"], wb["w_out"], x,
                                    sp["norm_ffn_g"], "mm_out")
    if late_weights is not None:
        wb = dict(wb, **late_weights(h2))
    up_pre = _mm_nn_cols(h2, wb["w_up"], min(l, 1024), "mm_up")
    conv_w = wb["conv_w"]
    act = _convffn_fwd(up_pre, conv_w, sp["conv_b"])
    loss, dx2, dx2b, d_final_g = _mm_res_loss(act, wb["w_down"], x1, sp["norm_final_g"].reshape(1, D_MODEL), target)

    g = {"norm_final_g": d_final_g.reshape(D_MODEL)}
    g["w_down"] = _mm_tn(act, dx2b, D_FF // 2, 512, "mm_down_dw")
    dup_pre, conv_par = _convffn_bwd(up_pre, dx2b, wb["w_down"], conv_w, sp["conv_b"])
    g["conv_w"], g["conv_b"] = conv_par[0:3], conv_par[3:4]
    g["w_up"] = _mm_tn_cols(h2, dup_pre, wb["w_up"].shape[0], 512, "mm_up_dw")
    dx1, dx1b, g["norm_ffn_g"] = _mm_cols_rms_bwd(dup_pre, wb["w_up"], x1, sp["norm_ffn_g"], dx2, "mm_up_dx")
    g["w_out"] = _mm_tn(mixed, dx1b, 1024, 1024, "mm_out_dw")
    zero = grads_ready(g["w_up"], g["w_down"], g["w_out"]) if grads_ready is not None else 0.0
    dattn, dys, g["norm_attn_g"], g["norm_ssm_g"] = _mm_mix_bwd(
        dx1b, wb["w_out"], attn, ys, sp["norm_attn_g"] + zero, sp["norm_ssm_g"], "mm_out_dx")
    dpre, zb, dsb, dd = _glu_bwd(pre, s_glu, dys, u, d_skip, wb["w_glu"])
    g["d_skip"] = dd.reshape(N_SSM_GROUPS, SSM_GROUP)
    g["w_glu"] = _mm_tn(zb, dsb, 512, 512, "mm_glu_dw")
    zero = grads_next(g["w_glu"]) if grads_next is not None else 0.0
    du_seg, dbcat, dccat, dlam_re, dlam_im = _ssm_bwd(u_seg, _to_segments(dpre), states, bcat, ccat,
                                                      lam_re + zero, lam_im)
    dlb_re, dlb_im, dbb_re, dbb_im, g["c_re"], g["c_im"] = _ssm_unpack(dbcat, dccat, dlam_re, dlam_im)
    _, disc_vjp = jax.vjp(_ssm_disc, sp["a_re"], sp["a_im"], sp["log_step"], sp["b_re"], sp["b_im"])
    g["a_re"], g["a_im"], g["log_step"], g["b_re"], g["b_im"] = disc_vjp((dlb_re, dlb_im, dbb_re, dbb_im))
    dq, dkv, g["sink"] = _attn_bwd(qkv, attn, dattn, lse, sp["sink"])
    dproj = _rope_bwd(dq, dkv, _from_segments(du_seg), dpre, d_skip, tabs)
    g["w_in"] = _mm_tn(dproj, h, IN_WIDTH // 5, D_MODEL, "mm_in_dw")
    grad_x, _, g["norm_mix_g"] = _mm_nn_rms_bwd(dproj, wb["w_in"], x, sp["norm_mix_g"], dx1, "mm_in_dx")
    return loss, grad_x, g


MESH = pl.DeviceIdType.MESH
ANY = pl.BlockSpec(memory_space=pl.ANY)


def _place():
    x, y, c = lax.axis_index("x"), lax.axis_index("y"), lax.axis_index("c")
    chips = [(1 - x, y), (x, 1 - y), (1 - x, 1 - y)]
    return x, y, c, chips


def _chip_index(px, py):
    return 2 * px + py


CHUNK_BYTES = 256 * 1024
MAX_CHUNKS = 16


def _row_chunks(rows, row_bytes, align):
    n = max(1, min(MAX_CHUNKS, (rows * row_bytes) // CHUNK_BYTES))
    per = -(-rows // n)
    per = -(-per // align) * align
    return [(r0, min(per, rows - r0)) for r0 in range(0, rows, per)]


def _align_of(dtype):
    return SUBLANES * 4 // jnp.dtype(dtype).itemsize


def _remote(src, dst, send_sem, recv_sem, to):
    return pltpu.make_async_remote_copy(src_ref=src, dst_ref=dst, send_sem=send_sem, recv_sem=recv_sem,
                                        device_id=to, device_id_type=MESH)


CAST_ROWS = 64


def _gather_weights(shards, dtypes):
    nw = len(shards)

    def body(*refs):
        w_refs, o_refs = refs[:nw], refs[nw:2 * nw]
        send_sems, recv_sems, in_sems, out_sems = refs[2 * nw:2 * nw + 4]
        raw, cast = refs[2 * nw + 4:3 * nw + 4], refs[3 * nw + 4:]
        x, y, c, chips = _place()
        mine = _chip_index(x, y)
        sibling = (x, y, 1 - c)

        def rows_of(ref, chip, r0, nr):
            return ref.at[chip, pl.ds(r0, nr), :]

        def copy(wi, k, src, dst, to):
            return _remote(src, dst, send_sems.at[wi, k], recv_sems.at[wi, k], to)

        geo = []
        for wi in range(nw):
            rows, cols = w_refs[wi].shape
            row_bytes = cols * jnp.dtype(dtypes[wi]).itemsize
            geo.append((rows // 2, _row_chunks(rows // 2, row_bytes, _align_of(dtypes[wi]))))

        stage_in = [pltpu.make_async_copy(w_refs[wi], raw[wi], in_sems.at[wi]) for wi in range(nw)]
        for cp in stage_in:
            cp.start()
        staged = [raw[wi] if dtypes[wi] == w_refs[wi].dtype else cast[wi] for wi in range(nw)]
        stage_out = []
        for wi in range(nw):
            stage_in[wi].wait()
            if staged[wi] is not raw[wi]:
                def cast_rows(i, _, wi=wi):
                    rows = pl.ds(pl.multiple_of(i * CAST_ROWS, CAST_ROWS), CAST_ROWS)
                    cast[wi][rows, :] = raw[wi][rows, :].astype(dtypes[wi])
                    return 0

                lax.fori_loop(0, w_refs[wi].shape[0] // CAST_ROWS, cast_rows, 0)
            cp = pltpu.make_async_copy(staged[wi], o_refs[wi].at[mine], out_sems.at[wi])
            cp.start()
            stage_out.append(cp)

        for wi in range(nw):
            hr, half_chunks = geo[wi]
            for j, chip in enumerate(chips):
                for r0, nr in half_chunks:
                    copy(wi, j, staged[wi].at[pl.ds(c * hr + r0, nr), :],
                         rows_of(o_refs[wi], mine, c * hr + r0, nr), (*chip, c)).start()
        for wi in range(nw):
            hr, half_chunks = geo[wi]
            for j, chip in enumerate(chips):
                got = rows_of(o_refs[wi], _chip_index(*chip), c * hr, hr)
                copy(wi, j, got, got, (*chip, c)).wait_recv()
                for r0, nr in half_chunks:
                    piece = rows_of(o_refs[wi], _chip_index(*chip), c * hr + r0, nr)
                    copy(wi, 3 + j, piece, piece, sibling).start()
        for wi in range(nw):
            hr = geo[wi][0]
            for j, chip in enumerate(chips):
                got = rows_of(o_refs[wi], _chip_index(*chip), (1 - c) * hr, hr)
                copy(wi, 3 + j, got, got, sibling).wait_recv()
        for wi in range(nw):
            hr = geo[wi][0]
            sent = rows_of(o_refs[wi], mine, c * hr, hr)
            for k in range(6):
                copy(wi, k, sent, sent, sibling).wait_send()
            stage_out[wi].wait()

    return pl.pallas_call(
        body, in_specs=[ANY] * nw, out_specs=[ANY] * nw,
        out_shape=[jax.ShapeDtypeStruct((4, *s.shape), t) for s, t in zip(shards, dtypes)],
        scratch_shapes=[pltpu.SemaphoreType.DMA((nw, 6)), pltpu.SemaphoreType.DMA((nw, 6)),
                        pltpu.SemaphoreType.DMA((nw,)), pltpu.SemaphoreType.DMA((nw,))]
        + [pltpu.VMEM(s.shape, s.dtype) for s in shards] + [pltpu.VMEM(s.shape, t) for s, t in zip(shards, dtypes)],
        name="gather_weights", compiler_params=_params(vmem_mb=40),
    )(*shards)


HBM = pl.BlockSpec(memory_space=pltpu.HBM)
SEM = pl.BlockSpec(memory_space=pltpu.SEMAPHORE)
EFFECT = pltpu.SideEffectType.DATAFLOW_SIDE_EFFECTING


def _cast_place(w, place, dtype, after, name):
    rows, cols = w.shape
    tr = _row_tile(rows, cols, _align_of(dtype))

    def body(p_ref, w_ref, after_ref, o_ref):
        del p_ref, after_ref
        o_ref[...] = w_ref[...].astype(dtype)

    grid_spec = pltpu.PrefetchScalarGridSpec(
        num_scalar_prefetch=1, grid=(rows // tr,),
        in_specs=[pl.BlockSpec((tr, cols), lambda i, p: (i, 0)), ANY],
        out_specs=pl.BlockSpec((None, tr, cols), lambda i, p: (p[1], i, 0)))
    return pl.pallas_call(body, grid_spec=grid_spec, out_shape=jax.ShapeDtypeStruct((4, rows, cols), dtype),
                          name=name, compiler_params=_params(("parallel",)))(place, w, after)


def _split_start(name, arrays, n_pairs, issue):
    n = len(arrays)

    def body(*refs):
        issue(refs[:n], refs[n:n + n_pairs], refs[n + n_pairs:n + 2 * n_pairs])
        token = refs[2 * n + 2 * n_pairs]
        token[...] = jnp.zeros_like(token)

    dma = pltpu.SemaphoreType.DMA(())
    outs = pl.pallas_call(
        body, name=name,
        out_shape=[dma] * (2 * n_pairs) + [pltpu.HBM(t.shape, t.dtype) for t in arrays]
        + [jax.ShapeDtypeStruct((SUBLANES, LANES), F32)],
        in_specs=[HBM] * n, out_specs=[SEM] * (2 * n_pairs) + [HBM] * n + [pl.BlockSpec(memory_space=pltpu.VMEM)],
        input_output_aliases={a: 2 * n_pairs + a for a in range(n)},
        compiler_params=pltpu.CompilerParams(has_side_effects=EFFECT),
    )(*[pltpu.with_memory_space_constraint(t, pltpu.HBM) for t in arrays])
    return outs[:n_pairs], outs[n_pairs:2 * n_pairs], outs[2 * n_pairs:2 * n_pairs + n], outs[-1]


def _split_wait(name, send_sems, recv_sems, flying, sizes, after):
    n, n_pairs = len(flying), len(send_sems)

    def body(*refs):
        x, y, c, _ = _place()
        for k, ref in enumerate(sizes(refs[:n])):
            cp = _remote(ref, ref, refs[n + k], refs[n + n_pairs + k], (x, y, 1 - c))
            cp.wait_send()
            cp.wait_recv()

    return pl.pallas_call(
        body, name=name, out_shape=[pltpu.HBM(t.shape, t.dtype) for t in flying],
        in_specs=[HBM] * n + [SEM] * (2 * n_pairs) + [ANY], out_specs=[HBM] * n,
        input_output_aliases={a: a for a in range(n)},
        compiler_params=pltpu.CompilerParams(has_side_effects=EFFECT),
    )(*flying, *send_sems, *recv_sems, after)


def _spread_start(lands, name):
    def issue(land_refs, send_sems, recv_sems):
        x, y, c, chips = _place()
        mine = _chip_index(x, y)
        for a, land in enumerate(land_refs):
            _, rows, cols = land.shape
            hr = rows // 2
            row_bytes = cols * jnp.dtype(land.dtype).itemsize
            for r0, nr in _row_chunks(hr, row_bytes, _align_of(land.dtype)):
                piece = land.at[mine, pl.ds(c * hr + r0, nr), :]
                for chip in chips:
                    for core in (0, 1):
                        _remote(piece, piece, send_sems[a], recv_sems[a], (*chip, core)).start()

    return _split_start(name, lands, len(lands), issue)


def _spread_wait(send_sems, recv_sems, flying, after, name):
    return _split_wait(name, send_sems, recv_sems, flying, lambda refs: [r.at[pl.ds(0, 3)] for r in refs], after)


def _pair_start(grads):
    n = len(grads)
    zones = [lax.empty((4, g.shape[1] // 2, g.shape[2]), F32) for g in grads]

    def issue(refs, send_sems, recv_sems):
        x, y, c, _ = _place()
        for a in range(n):
            g_ref, z_ref = refs[a], refs[n + a]
            _, rows, cols = g_ref.shape
            hr = rows // 2
            for k in range(4):
                for r0, nr in _row_chunks(hr, cols * 4, SUBLANES):
                    _remote(g_ref.at[k, pl.ds((1 - c) * hr + r0, nr), :], z_ref.at[k, pl.ds(r0, nr), :],
                            send_sems[a], recv_sems[a], (x, y, 1 - c)).start()

    return _split_start("pair_start", list(grads) + zones, n, issue)


def _pair_wait(send_sems, recv_sems, flying, after):
    n = len(flying) // 2
    out = _split_wait("pair_wait", send_sems, recv_sems, flying, lambda refs: list(refs[n:]), after)
    return out[:n], out[n:]


def _chip_start(sums):
    n = len(sums)
    zones = [lax.empty((3, *s.shape[1:]), s.dtype) for s in sums]

    def issue(refs, send_sems, recv_sems):
        x, y, c, chips = _place()
        for a in range(n):
            s_ref, z_ref = refs[a], refs[n + a]
            _, rows, cols = s_ref.shape
            row_bytes = cols * jnp.dtype(s_ref.dtype).itemsize
            for r0, nr in _row_chunks(rows, row_bytes, _align_of(s_ref.dtype)):
                for j, chip in enumerate(chips):
                    _remote(s_ref.at[_chip_index(*chip), pl.ds(r0, nr), :], z_ref.at[j, pl.ds(r0, nr), :],
                            send_sems[a], recv_sems[a], (*chip, c)).start()

    return _split_start("chip_start", list(sums) + zones, n, issue)


def _chip_wait(send_sems, recv_sems, flying, after):
    n = len(flying) // 2
    return _split_wait("chip_wait", send_sems, recv_sems, flying, lambda refs: list(refs[n:]), after)[n:]


def _pair_exchange(grads):
    na = len(grads)

    def body(*refs):
        g_refs, o_refs = refs[:na], refs[na:2 * na]
        send_sems, recv_sems = refs[2 * na:]
        x, y, c, _ = _place()
        sibling = (x, y, 1 - c)
        for ai in range(na):
            _, rows, cols = g_refs[ai].shape
            hr = rows // 2
            for k in range(4):
                for r0, nr in _row_chunks(hr, cols * 4, SUBLANES):
                    _remote(g_refs[ai].at[k, pl.ds((1 - c) * hr + r0, nr), :], o_refs[ai].at[k, pl.ds(r0, nr), :],
                            send_sems.at[ai], recv_sems.at[ai], sibling).start()
        for ai in range(na):
            _remote(o_refs[ai], o_refs[ai], send_sems.at[ai], recv_sems.at[ai], sibling).wait()

    return pl.pallas_call(
        body, in_specs=[ANY] * na, out_specs=[ANY] * na,
        out_shape=[jax.ShapeDtypeStruct((4, g.shape[1] // 2, g.shape[2]), F32) for g in grads],
        scratch_shapes=[pltpu.SemaphoreType.DMA((na,)), pltpu.SemaphoreType.DMA((na,))],
        name="pair_exchange",
    )(*grads)


def _row_tile(rows, cols, align):
    best = align
    for cand in range(align, rows + 1, align):
        if rows % cand == 0 and cand * cols <= 256 * 1024:
            best = cand
    return best


def _pair_sum(g, got, place, transit, name):
    _, rows, cols = g.shape
    hr = rows // 2
    tr = _row_tile(hr, cols, _align_of(transit))
    nt = hr // tr

    def body(p_ref, g_ref, r_ref, s_ref, own_ref):
        total = g_ref[...] + r_ref[...]
        s_ref[...] = total.astype(transit)

        @pl.when(pl.program_id(1) == p_ref[1])
        def _():
            own_ref[...] = total

    grid_spec = pltpu.PrefetchScalarGridSpec(
        num_scalar_prefetch=1, grid=(nt, 4),
        in_specs=[pl.BlockSpec((None, tr, cols), lambda i, k, p: (k, p[0] * nt + i, 0)),
                  pl.BlockSpec((None, tr, cols), lambda i, k, p: (k, i, 0))],
        out_specs=[pl.BlockSpec((None, tr, cols), lambda i, k, p: (k, i, 0)),
                   pl.BlockSpec((tr, cols), lambda i, k, p: (i, 0))])
    return pl.pallas_call(
        body, grid_spec=grid_spec,
        out_shape=[jax.ShapeDtypeStruct((4, hr, cols), transit), jax.ShapeDtypeStruct((hr, cols), F32)],
        name=name, compiler_params=_params(("parallel", "arbitrary")),
    )(place, g, got)


def _chip_exchange(sums):
    na = len(sums)

    def body(*refs):
        s_refs, o_refs = refs[:na], refs[na:2 * na]
        send_sems, recv_sems = refs[2 * na:]
        x, y, c, chips = _place()
        for ai in range(na):
            _, rows, cols = s_refs[ai].shape
            row_bytes = cols * jnp.dtype(s_refs[ai].dtype).itemsize
            for r0, nr in _row_chunks(rows, row_bytes, _align_of(s_refs[ai].dtype)):
                for j, chip in enumerate(chips):
                    _remote(s_refs[ai].at[_chip_index(*chip), pl.ds(r0, nr), :], o_refs[ai].at[j, pl.ds(r0, nr), :],
                            send_sems.at[ai, j], recv_sems.at[ai, j], (*chip, c)).start()
        for ai in range(na):
            for j, chip in enumerate(chips):
                _remote(o_refs[ai].at[j], o_refs[ai].at[j], send_sems.at[ai, j], recv_sems.at[ai, j],
                        (*chip, c)).wait()

    return pl.pallas_call(
        body, in_specs=[ANY] * na, out_specs=[ANY] * na,
        out_shape=[jax.ShapeDtypeStruct((3, *s.shape[1:]), s.dtype) for s in sums],
        scratch_shapes=[pltpu.SemaphoreType.DMA((na, 3)), pltpu.SemaphoreType.DMA((na, 3))],
        name="chip_exchange",
    )(*sums)


def _chip_sum(own, landed, name):
    hr, cols = own.shape
    tr = _row_tile(hr, cols, _align_of(landed.dtype))

    def body(o_ref, l_ref, f_ref):
        acc = o_ref[...]
        for j in range(3):
            acc = acc + l_ref[j].astype(F32)
        f_ref[...] = acc

    return pl.pallas_call(
        body, grid=(hr // tr,),
        in_specs=[pl.BlockSpec((tr, cols), lambda i: (i, 0)), pl.BlockSpec((3, tr, cols), lambda i: (0, i, 0))],
        out_specs=pl.BlockSpec((tr, cols), lambda i: (i, 0)),
        out_shape=jax.ShapeDtypeStruct((hr, cols), F32), name=name,
        compiler_params=_params(("parallel",)),
    )(own, landed)


def _final_exchange(halves, small):
    nh = len(halves)

    def body(*refs):
        h_refs, s_ref = refs[:nh], refs[nh]
        o_refs, so_ref = refs[nh + 1:2 * nh + 1], refs[2 * nh + 1]
        send_sems, recv_sems, local_sem, ssend_sems, srecv_sems = refs[2 * nh + 2:]
        x, y, c, _ = _place()
        me = 4 * x + 2 * y + c
        sibling = (x, y, 1 - c)
        for hi in range(nh):
            hr, cols = h_refs[hi].shape
            for r0, nr in _row_chunks(hr, cols * 4, SUBLANES):
                _remote(h_refs[hi].at[pl.ds(r0, nr), :], o_refs[hi].at[pl.ds(r0, nr), :],
                        send_sems.at[hi], recv_sems.at[hi], sibling).start()
        small_cps = [pltpu.make_async_copy(s_ref, so_ref.at[me], local_sem)]
        for r in range(1, 8):
            fx, fy, fc = (r >> 2) & 1, (r >> 1) & 1, r & 1
            peer = (1 - x if fx else x, 1 - y if fy else y, 1 - c if fc else c)
            small_cps.append(_remote(s_ref, so_ref.at[me], ssend_sems.at[r - 1], srecv_sems.at[r - 1], peer))
        for cp in small_cps:
            cp.start()
        for hi in range(nh):
            _remote(h_refs[hi], o_refs[hi], send_sems.at[hi], recv_sems.at[hi], sibling).wait()
        for cp in small_cps:
            cp.wait()

    return pl.pallas_call(
        body, in_specs=[ANY] * (nh + 1), out_specs=[ANY] * (nh + 1),
        out_shape=[jax.ShapeDtypeStruct(h.shape, F32) for h in halves]
        + [jax.ShapeDtypeStruct((8, *small.shape), F32)],
        scratch_shapes=[pltpu.SemaphoreType.DMA((nh,)), pltpu.SemaphoreType.DMA((nh,)),
                        pltpu.SemaphoreType.DMA, pltpu.SemaphoreType.DMA((7,)), pltpu.SemaphoreType.DMA((7,))],
        name="final_exchange",
    )(*halves, small)


def _adamw_halves(w, own, other, m, v, place, name):
    r, c = w.shape
    hr = r // 2
    tr = _row_tile(hr, c, SUBLANES)
    nt = hr // tr
    c1 = 1.0 - ADAM_B1 ** ADAM_STEP
    c2 = 1.0 - ADAM_B2 ** ADAM_STEP

    def body(p_ref, w_ref, own_ref, other_ref, m_ref, v_ref, g_ref, d_ref, nm_ref, nv_ref):
        mine = pl.program_id(0) // nt == p_ref[0]
        gv = jnp.where(mine, own_ref[...], other_ref[...])
        nm = ADAM_B1 * m_ref[...] + (1.0 - ADAM_B1) * gv
        nv = ADAM_B2 * v_ref[...] + (1.0 - ADAM_B2) * (gv * gv)
        g_ref[...] = gv
        d_ref[...] = -ADAM_LR * ((nm / c1) / (jnp.sqrt(nv / c2) + ADAM_EPS) + ADAM_WD * w_ref[...])
        nm_ref[...] = nm
        nv_ref[...] = nv

    full = pl.BlockSpec((tr, c), lambda i, p: (i, 0))
    half = pl.BlockSpec((tr, c), lambda i, p: (i % nt, 0))
    out = jax.ShapeDtypeStruct((r, c), F32)
    grid_spec = pltpu.PrefetchScalarGridSpec(num_scalar_prefetch=1, grid=(2 * nt,),
                                             in_specs=[full, half, half, full, full], out_specs=[full] * 4)
    return pl.pallas_call(body, grid_spec=grid_spec, out_shape=[out] * 4, name=name,
                          compiler_params=_params(("parallel",)))(place, w, own, other, m, v)


def _adamw_many(ws, gs, ms, vs, name):
    n = len(ws)
    c1 = 1.0 - ADAM_B1 ** ADAM_STEP
    c2 = 1.0 - ADAM_B2 ** ADAM_STEP

    def body(*refs):
        w_refs, g_refs, m_refs, v_refs = (refs[k * n:(k + 1) * n] for k in range(4))
        d_refs, nm_refs, nv_refs = (refs[(4 + k) * n:(5 + k) * n] for k in range(3))
        for i in range(n):
            gv = g_refs[i][...]
            nm = ADAM_B1 * m_refs[i][...] + (1.0 - ADAM_B1) * gv
            nv = ADAM_B2 * v_refs[i][...] + (1.0 - ADAM_B2) * (gv * gv)
            d_refs[i][...] = -ADAM_LR * ((nm / c1) / (jnp.sqrt(nv / c2) + ADAM_EPS) + ADAM_WD * w_refs[i][...])
            nm_refs[i][...] = nm
            nv_refs[i][...] = nv

    vmem = pl.BlockSpec(memory_space=pltpu.VMEM)
    shapes = [jax.ShapeDtypeStruct(t.shape, F32) for t in ws]
    outs = pl.pallas_call(body, in_specs=[vmem] * (4 * n), out_specs=[vmem] * (3 * n), out_shape=shapes * 3,
                          name=name, compiler_params=_params(vmem_mb=56))(*ws, *gs, *ms, *vs)
    return outs[:n], outs[n:2 * n], outs[2 * n:]


BIG = ("w_in", "w_glu", "w_out", "w_up", "w_down")
WEIGHTS = ("norm_mix_g", "w_in", "a_re", "a_im", "log_step", "b_re", "b_im", "c_re", "c_im", "d_skip", "w_glu",
           "sink", "norm_attn_g", "norm_ssm_g", "w_out", "norm_ffn_g", "w_up", "conv_w", "conv_b", "w_down",
           "norm_final_g")
SMALL = ("norm_mix_g", "a_re", "a_im", "log_step", "b_re", "b_im", "c_re", "c_im", "d_skip", "sink",
         "norm_attn_g", "norm_ssm_g", "norm_ffn_g", "conv_w", "conv_b", "norm_final_g")
SMALL_ROWS = 48
N_DEV = 8


def _tile_rows(size):
    return -(-size // (SUBLANES * D_MODEL)) * SUBLANES


def _by_owner(name, g):
    if name == "w_up":
        return g
    return g.reshape(4, g.shape[0] // 4, g.shape[1])


def _view(name, t):
    if name == "w_in":
        return jnp.swapaxes(t[0], 0, 1)
    if name in ("b_re", "b_im"):
        return jnp.swapaxes(t, -1, -2)
    return t


def _unview(name, t):
    if name == "w_in":
        return jnp.swapaxes(t, 0, 1)[None]
    if name in ("b_re", "b_im"):
        return jnp.swapaxes(t, -1, -2)
    return t


def kernel(x, norm_mix_g, w_in, a_re, a_im, log_step, b_re, b_im, c_re, c_im, d_skip, w_glu, sink, norm_attn_g, norm_ssm_g, w_out, norm_ffn_g, w_up, conv_w, conv_b, w_down, norm_final_g, loss_target, m_norm_mix_g, m_w_in, m_a_re, m_a_im, m_log_step, m_b_re, m_b_im, m_c_re, m_c_im, m_d_skip, m_w_glu, m_sink, m_norm_attn_g, m_norm_ssm_g, m_w_out, m_norm_ffn_g, m_w_up, m_conv_w, m_conv_b, m_w_down, m_norm_final_g, v_norm_mix_g, v_w_in, v_a_re, v_a_im, v_log_step, v_b_re, v_b_im, v_c_re, v_c_im, v_d_skip, v_w_glu, v_sink, v_norm_attn_g, v_norm_ssm_g, v_w_out, v_norm_ffn_g, v_w_up, v_conv_w, v_conv_b, v_w_down, v_norm_final_g):
    given = dict(locals())
    w = {n: given[n] for n in WEIGHTS}
    m = {n: given["m_" + n] for n in WEIGHTS}
    v = {n: given["v_" + n] for n in WEIGHTS}
    xy = 2 * lax.axis_index("x") + lax.axis_index("y")

    core = lax.axis_index("c")
    place = jnp.stack([core, xy]).astype(jnp.int32)

    conv_rows = jnp.pad(w["conv_w"][0], ((0, 2 * SUBLANES - 3), (0, 0)))
    rows = lambda t: t.reshape(4 * t.shape[1], t.shape[2])
    (w_in_all,) = _gather_weights([_view("w_in", w["w_in"])], [BF16])
    wb = {"w_in": rows(w_in_all)}
    mixer = [_cast_place(w[n][0], place, BF16, w_in_all, "cast_" + n) for n in ("w_glu", "w_out")]
    mixer.append(_cast_place(conv_rows, place, F32, w_in_all, "cast_conv_w"))
    *mixer_flight, mixer_token = _spread_start(mixer, "spread_mixer_start")
    late = ("w_up", "w_down")
    *late_flight, token = _spread_start(
        [_cast_place(w[n][0], place, BF16, mixer_token, "cast_" + n) for n in late], "spread_ffn_start")

    def mixer_weights(after):
        w_glu4, w_out4, conv4 = _spread_wait(*mixer_flight, after, "spread_mixer_wait")
        return {"w_glu": rows(w_glu4), "w_out": rows(w_out4),
                "conv_w": conv4[:, :3].transpose(1, 0, 2).reshape(3, 2 * D_FF)}

    def late_weights(after):
        w_up4, w_down4 = _spread_wait(*late_flight, after, "spread_ffn_wait")
        return {"w_up": w_up4, "w_down": rows(w_down4)}

    sp = {n: w[n][0] for n in ("a_re", "a_im", "log_step", "b_re", "b_im", "c_re", "c_im", "d_skip",
                               "norm_mix_g", "norm_attn_g", "norm_ssm_g", "norm_ffn_g", "sink", "conv_b")}
    for n in ("norm_mix_g", "norm_attn_g", "norm_ssm_g", "norm_ffn_g", "sink", "conv_b"):
        sp[n] = sp[n].reshape(1, -1)
    sp["norm_mix_g"] = sp["norm_mix_g"] + token[:1, :1]
    sp["norm_final_g"] = w["norm_final_g"]
    early, tail = late + ("w_out",), ("w_in", "w_glu")
    flight = {}

    def grads_ready(dw_up, dw_down, dw_out):
        *flight["pair"], token = _pair_start([dw_up, _by_owner("w_down", dw_down), _by_owner("w_out", dw_out)])
        return token[:1, :1]

    def grads_next(after):
        mine, got = _pair_wait(*flight["pair"], after)
        sums, flight["own"] = zip(*[_pair_sum(a, b, place, BF16, "pair_sum_" + n) for n, a, b in zip(early, mine, got)])
        *flight["chip"], token = _chip_start(list(sums))
        return token[:1, :1]

    loss, grad_x, g = _local_step(x[0], loss_target[0], wb, sp, mixer_weights, late_weights, grads_ready,
                                  grads_next)

    def as_rows(t):
        rows = _tile_rows(t.size)
        return jnp.pad(t.reshape(-1), (0, rows * D_MODEL - t.size)).reshape(rows, D_MODEL)

    pieces = [as_rows(g[n]) for n in SMALL] + [as_rows(loss)]
    spare = N_DEV * SMALL_ROWS - sum(p.shape[0] for p in pieces)
    small = jnp.concatenate(pieces + [jnp.zeros((spare, D_MODEL), F32)]).reshape(4, 2 * SMALL_ROWS, D_MODEL)
    by_owner = [_by_owner(n, g[n]) for n in tail] + [small]
    got = _pair_exchange(by_owner)
    transit = [BF16] * len(tail) + [F32]
    chip_sums, own_sums = zip(*[_pair_sum(a, b, place, t, "pair_sum_" + n)
                                for n, a, b, t in zip(tail + ("small",), by_owner, got, transit)])
    landed = _chip_exchange(list(chip_sums))
    halves = {n: _chip_sum(o, t, "chip_sum_" + n) for n, o, t in zip(tail + ("small",), own_sums, landed)}
    early_landed = _chip_wait(*flight["chip"], grad_x)
    for n, o, t in zip(early, flight["own"], early_landed):
        halves[n] = _chip_sum(o, t, "chip_sum_" + n)
    *others, small_all = _final_exchange([halves[n] for n in BIG], halves["small"])
    small_all = small_all.reshape(N_DEV * SMALL_ROWS, D_MODEL)
    grads, row = {}, 0
    for n in SMALL:
        shape = (3, 4 * w[n].shape[-1]) if n == "conv_w" else w[n].shape[1:] if n != "norm_final_g" else w[n].shape
        size = math.prod(shape)
        grads[n] = small_all[row:row + _tile_rows(size)].reshape(-1)[:size].reshape(shape)
        row += _tile_rows(size)
    loss = small_all[row, 0]
    cw = w["conv_w"].shape[-1]
    grads["conv_w"] = lax.dynamic_slice_in_dim(grads["conv_w"], xy * cw, cw, axis=1)
    grads = {n: _view(n, grads[n].reshape(w[n].shape)) for n in SMALL}
    wv, mv, vv = ({n: _view(n, t[n]) for n in WEIGHTS} for t in (w, m, v))

    delta, new_m, new_v = {}, {}, {}
    for n, other in zip(BIG, others):
        two_d = lambda t: t.reshape(t.shape[-2:])
        grads[n], delta[n], new_m[n], new_v[n] = _adamw_halves(
            two_d(wv[n]), halves[n], other, two_d(mv[n]), two_d(vv[n]), place, "adamw_" + n)
    for group, name in ((("b_re", "b_im"), "adamw_b"), (tuple(n for n in SMALL if n not in ("b_re", "b_im")), "adamw_small")):
        row = lambda t: t.reshape(1, -1) if t.ndim == 1 else t
        d_, m_, v_ = _adamw_many(*[[row(t[n]) for n in group] for t in (wv, grads, mv, vv)], name)
        for n, dn, mn, vn in zip(group, d_, m_, v_):
            delta[n], new_m[n], new_v[n] = (t.reshape(wv[n].shape) for t in (dn, mn, vn))
    natural = lambda t: [_unview(n, t[n].reshape(wv[n].shape)) for n in WEIGHTS]
    return (loss, grad_x[None], *natural(grads), *natural(delta), *natural(new_m), *natural(new_v))
```

```python
import functools
import math

import jax
import jax.numpy as jnp
import numpy as np
from jax import lax
from jax.experimental import pallas as pl
from jax.experimental.pallas import tpu as pltpu

F32 = jnp.float32
BF16 = jnp.bfloat16

D_MODEL = 1024
N_Q_HEADS = 8
N_KV_HEADS = 2
HEAD_DIM = 64
ATTN_WIDTH = 512
KV_WIDTH = 128
QKV_WIDTH = ATTN_WIDTH + 2 * KV_WIDTH
WINDOW = 128
BLOCK = 128
ROPE_DIM = 16
ROPE_THETA = 500000.0
SCORE_SCALE = HEAD_DIM ** -0.5
SSM_WIDTH = 512
SSM_GROUP = 16
N_SSM_GROUPS = 32
SSM_STATE = 64
IN_WIDTH = 1280
D_FF = 2816
EPS = 1e-6
ADAM_LR = 0.001
ADAM_B1 = 0.9
ADAM_B2 = 0.999
ADAM_EPS = 1e-08
ADAM_WD = 0.01
ADAM_STEP = 10

VMEM_BYTES_V7X = 64 * 1024 * 1024
SUBLANES = 8
LANES = 128
SSM_CB = 4
SSM_CH = 128
SSM_ST = 512
N_SEG = SUBLANES

NN = (((1,), (0,)), ((), ()))
NT = (((1,), (1,)), ((), ()))
TN = (((0,), (0,)), ((), ()))


def _params(sem=None, vmem_mb=48):
    limit = vmem_mb * 1024 * 1024
    assert limit < VMEM_BYTES_V7X
    return pltpu.CompilerParams(dimension_semantics=sem, vmem_limit_bytes=limit)


def _dg(a, b, dims):
    return lax.dot_general(a, b, dims, preferred_element_type=F32)


def _sigmoid(x):
    return 1.0 / (1.0 + jnp.exp(-x))


_SQRT_HALF = 0.7071067811865476
_INV_SQRT_2PI = 0.3989422804014327


def _gelu(x):
    return 0.5 * x * (1.0 + lax.erf(x * _SQRT_HALF))


def _gelu_grad(x):
    return 0.5 * (1.0 + lax.erf(x * _SQRT_HALF)) + x * (_INV_SQRT_2PI * jnp.exp(-0.5 * x * x))


def _mm_tn(a, b, tm, tn, name):
    k, m = a.shape
    n = b.shape[1]

    def body(a_ref, b_ref, o_ref):
        o_ref[...] = _dg(a_ref[...], b_ref[...], TN)

    return pl.pallas_call(
        body, grid=(m // tm, n // tn),
        in_specs=[pl.BlockSpec((k, tm), lambda i, j: (0, i)), pl.BlockSpec((k, tn), lambda i, j: (0, j))],
        out_specs=pl.BlockSpec((tm, tn), lambda i, j: (i, j)),
        out_shape=jax.ShapeDtypeStruct((m, n), F32), name=name,
        compiler_params=_params(("parallel", "parallel")),
    )(a, b)


def _mm_nn_cols(a, b4, tm, name):
    m, k = a.shape
    s, _, n = b4.shape

    def body(a_ref, b_ref, o_ref):
        o_ref[...] = _dg(a_ref[...], b_ref[...], NN)

    return pl.pallas_call(
        body, grid=(m // tm, s),
        in_specs=[pl.BlockSpec((tm, k), lambda i, j: (i, 0)), pl.BlockSpec((None, k, n), lambda i, j: (j, 0, 0))],
        out_specs=pl.BlockSpec((tm, n), lambda i, j: (i, j)),
        out_shape=jax.ShapeDtypeStruct((m, s * n), F32), name=name,
        compiler_params=_params(("parallel", "parallel")),
    )(a, b4)


def _mm_tn_cols(a, b2, s, tm, name):
    k, m = a.shape
    h, _, wide = b2.shape
    per = s // h
    n = wide // per

    def body(a_ref, b_ref, o_ref):
        o_ref[...] = _dg(a_ref[...], b_ref[...], TN)

    return pl.pallas_call(
        body, grid=(s, m // tm),
        in_specs=[pl.BlockSpec((k, tm), lambda j, i: (0, i)),
                  pl.BlockSpec((None, k, n), lambda j, i: (j // per, 0, j % per))],
        out_specs=pl.BlockSpec((None, tm, n), lambda j, i: (j, i, 0)),
        out_shape=jax.ShapeDtypeStruct((s, m, n), F32), name=name,
        compiler_params=_params(("parallel", "parallel")),
    )(a, b2)


TM_EW = 256


def _rms_bwd_vals(xv, gv, dy):
    r = lax.rsqrt(jnp.mean(xv * xv, axis=-1, keepdims=True) + EPS)
    xh = xv * r
    dxh = dy * gv
    dx = r * (dxh - xh * jnp.mean(dxh * xh, axis=-1, keepdims=True))
    return dx, dy * xh


TM_FUSED = 512
TM_LOSS = 256


def _rms_vals(xv, gv):
    return xv * lax.rsqrt(jnp.mean(xv * xv, axis=-1, keepdims=True) + EPS) * gv


def _rope_blocks(src, dst, c, lo, hi):
    nq = ATTN_WIDTH // LANES
    for blk in range(nq + 1):
        t = src[:, blk * LANES:(blk + 1) * LANES]
        rot = t * c + pltpu.roll(t, LANES - 8, 1) * lo + pltpu.roll(t, 8, 1) * hi
        dst[:, blk * LANES:(blk + 1) * LANES] = (rot * SCORE_SCALE if blk < nq else rot).astype(BF16)
    dst[:, (nq + 1) * LANES:] = src[:, (nq + 1) * LANES:].astype(BF16)


def _rms_mm_rope(x, g, wt, tabs, name):
    l, d = x.shape
    n = wt.shape[0]

    def body(x_ref, g_ref, w_ref, c_ref, lo_ref, hi_ref, h_ref, qkv_ref, u_ref):
        h = _rms_vals(x_ref[...], g_ref[...]).astype(BF16)
        h_ref[...] = h
        out = _dg(h, w_ref[...], NT)
        _rope_blocks(out[:, :QKV_WIDTH], qkv_ref, c_ref[...], lo_ref[...], hi_ref[...])
        u_ref[...] = out[:, QKV_WIDTH:]

    row = lambda width: pl.BlockSpec((TM_FUSED, width), lambda i: (i, 0))
    return pl.pallas_call(
        body, grid=(l // TM_FUSED,),
        in_specs=[row(d), pl.BlockSpec((1, d), lambda i: (0, 0)), pl.BlockSpec((n, d), lambda i: (0, 0)),
                  row(LANES), row(LANES), row(LANES)],
        out_specs=[row(d), row(QKV_WIDTH), row(n - QKV_WIDTH)],
        out_shape=[jax.ShapeDtypeStruct((l, d), BF16), jax.ShapeDtypeStruct((l, QKV_WIDTH), BF16),
                   jax.ShapeDtypeStruct((l, n - QKV_WIDTH), F32)],
        name=name, compiler_params=_params(("parallel",)),
    )(x, g, wt, *tabs)


def _mix_mm_res_rms(attn, ys, g_attn, g_ssm, b, res, g, name):
    l, w = attn.shape
    d = b.shape[1]

    def body(a_ref, y_ref, ga_ref, gs_ref, b_ref, r_ref, g_ref, m_ref, x_ref, h_ref):
        m_ref[:, :w] = _rms_vals(a_ref[...], ga_ref[...]).astype(BF16)
        m_ref[:, w:] = _rms_vals(y_ref[...], gs_ref[...]).astype(BF16)
        xv = r_ref[...] + _dg(m_ref[...], b_ref[...], NN)
        x_ref[...] = xv
        h_ref[...] = _rms_vals(xv, g_ref[...]).astype(BF16)

    row = lambda width: pl.BlockSpec((TM_FUSED, width), lambda i: (i, 0))
    vec = lambda width: pl.BlockSpec((1, width), lambda i: (0, 0))
    return pl.pallas_call(
        body, grid=(l // TM_FUSED,),
        in_specs=[row(w), row(w), vec(w), vec(w), pl.BlockSpec((2 * w, d), lambda i: (0, 0)), row(d), vec(d)],
        out_specs=[row(2 * w), row(d), row(d)],
        out_shape=[jax.ShapeDtypeStruct((l, 2 * w), BF16), jax.ShapeDtypeStruct((l, d), F32),
                   jax.ShapeDtypeStruct((l, d), BF16)],
        name=name, compiler_params=_params(("parallel",)),
    )(attn, ys, g_attn, g_ssm, b, res, g)


def _mm_res_loss(a, b, res, g, target):
    l, k = a.shape
    d = b.shape[1]

    def body(a_ref, b_ref, r_ref, g_ref, t_ref, loss_ref, dx_ref, dxb_ref, dg_ref):
        xv = r_ref[...] + _dg(a_ref[...], b_ref[...], NN)
        gv = g_ref[...]
        r = lax.rsqrt(jnp.mean(xv * xv, axis=-1, keepdims=True) + EPS)
        xh = xv * r
        e = xh * gv - t_ref[...]
        part = jnp.sum(jnp.sum(e * e, axis=1, keepdims=True), axis=0, keepdims=True) * (0.5 / d)
        dy = e * (1.0 / d)
        dxh = dy * gv
        dx = r * (dxh - xh * jnp.mean(dxh * xh, axis=-1, keepdims=True))
        dx_ref[...] = dx
        dxb_ref[...] = dx.astype(BF16)

        @pl.when(pl.program_id(0) == 0)
        def _():
            dg_ref[...] = jnp.zeros_like(dg_ref)
            loss_ref[...] = jnp.zeros_like(loss_ref)

        dg_ref[...] += jnp.sum(dy * xh, axis=0, keepdims=True)
        loss_ref[...] += part

    row = lambda width: pl.BlockSpec((TM_LOSS, width), lambda i: (i, 0))
    vec = pl.BlockSpec((1, d), lambda i: (0, 0))
    return pl.pallas_call(
        body, grid=(l // TM_LOSS,),
        in_specs=[row(k), pl.BlockSpec((k, d), lambda i: (0, 0)), row(d), vec, row(d)],
        out_specs=[pl.BlockSpec((1, 1), lambda i: (0, 0)), row(d), row(d), vec],
        out_shape=[jax.ShapeDtypeStruct((1, 1), F32), jax.ShapeDtypeStruct((l, d), F32),
                   jax.ShapeDtypeStruct((l, d), BF16), jax.ShapeDtypeStruct((1, d), F32)],
        name="mm_down_loss", compiler_params=_params(("arbitrary",)),
    )(a, b, res, g, target)


def _mm_rms_bwd(a, b, a_spec, b_spec, matmul, x, g, res, name):
    l, d = x.shape

    def body(a_ref, b_ref, x_ref, g_ref, res_ref, dx_ref, dxb_ref, dg_ref):
        dx, dgr = _rms_bwd_vals(x_ref[...], g_ref[...], matmul(a_ref, b_ref))
        dx = dx + res_ref[...]
        dx_ref[...] = dx
        dxb_ref[...] = dx.astype(BF16)

        @pl.when(pl.program_id(0) == 0)
        def _():
            dg_ref[...] = jnp.zeros_like(dg_ref)

        dg_ref[...] += jnp.sum(dgr, axis=0, keepdims=True)

    row = pl.BlockSpec((TM_FUSED, d), lambda i: (i, 0))
    vec = pl.BlockSpec((1, d), lambda i: (0, 0))
    return pl.pallas_call(
        body, grid=(l // TM_FUSED,), in_specs=[a_spec, b_spec, row, vec, row], out_specs=[row, row, vec],
        out_shape=[jax.ShapeDtypeStruct((l, d), F32), jax.ShapeDtypeStruct((l, d), BF16),
                   jax.ShapeDtypeStruct((1, d), F32)],
        name=name, compiler_params=_params(("arbitrary",)),
    )(a, b, x, g, res)


def _mm_nn_rms_bwd(a, b, x, g, res, name):
    return _mm_rms_bwd(a, b, pl.BlockSpec((TM_FUSED, a.shape[1]), lambda i: (i, 0)),
                       pl.BlockSpec(b.shape, lambda i: (0, 0)),
                       lambda a_ref, b_ref: _dg(a_ref[...], b_ref[...], NN), x, g, res, name)


def _mm_cols_rms_bwd(a2, b4, x, g, res, name):
    h, _, wide = a2.shape
    s, _, n = b4.shape
    per = s // h

    def matmul(a_ref, b_ref):
        acc = None
        for j in range(s):
            part = _dg(a_ref[j // per, :, (j % per) * n:(j % per + 1) * n], b_ref[j], NT)
            acc = part if acc is None else acc + part
        return acc

    return _mm_rms_bwd(a2, b4, pl.BlockSpec((h, TM_FUSED, wide), lambda i: (0, i, 0)),
                       pl.BlockSpec(b4.shape, lambda i: (0, 0, 0), pipeline_mode=pl.Buffered(1)),
                       matmul, x, g, res, name)


def _mm_mix_bwd(dx, b, attn, ys, g_attn, g_ssm, name):
    l, w = attn.shape
    d = dx.shape[1]

    def body(dx_ref, b_ref, a_ref, y_ref, ga_ref, gs_ref, da_ref, dy_ref, dga_ref, dgs_ref):
        @pl.when(pl.program_id(0) == 0)
        def _():
            dga_ref[...] = jnp.zeros_like(dga_ref)
            dgs_ref[...] = jnp.zeros_like(dgs_ref)

        dm = _dg(dx_ref[...], b_ref[...], NT)
        for src, gr, off, dst, dgr in ((a_ref, ga_ref, 0, da_ref, dga_ref), (y_ref, gs_ref, w, dy_ref, dgs_ref)):
            dxv, dg_rows = _rms_bwd_vals(src[...], gr[...], dm[:, off:off + w])
            dst[...] = dxv
            dgr[...] += jnp.sum(dg_rows, axis=0, keepdims=True)

    row = lambda width: pl.BlockSpec((TM_FUSED, width), lambda i: (i, 0))
    vec = pl.BlockSpec((1, w), lambda i: (0, 0))
    return pl.pallas_call(
        body, grid=(l // TM_FUSED,),
        in_specs=[row(d), pl.BlockSpec((2 * w, d), lambda i: (0, 0)), row(w), row(w), vec, vec],
        out_specs=[row(w), row(w), vec, vec],
        out_shape=[jax.ShapeDtypeStruct((l, w), F32), jax.ShapeDtypeStruct((l, w), F32),
                   jax.ShapeDtypeStruct((1, w), F32), jax.ShapeDtypeStruct((1, w), F32)],
        name=name, compiler_params=_params(("arbitrary",)),
    )(dx, b, attn, ys, g_attn, g_ssm)


def _rope_tables(l):
    half = ROPE_DIM // 2
    f32 = np.float32
    inv_freq = np.power(f32(ROPE_THETA), -np.arange(half, dtype=f32) / f32(half))
    ang = np.arange(l, dtype=f32)[:, None] * inv_freq[None, :]
    cos, sin = np.cos(ang), np.sin(ang)
    ones = np.ones((l, HEAD_DIM - ROPE_DIM), f32)
    zeros = np.zeros((l, HEAD_DIM - ROPE_DIM), f32)
    zh = np.zeros((l, half), f32)
    c = np.concatenate([cos, cos, ones], axis=1)
    s_lo = np.concatenate([-sin, zh, zeros], axis=1)
    s_hi = np.concatenate([zh, sin, zeros], axis=1)
    return tuple(jnp.asarray(np.tile(t, (1, LANES // HEAD_DIM)), F32) for t in (c, s_lo, s_hi))


def _rope_bwd(dq, dkv, du_ssm, dpre, d_skip, tabs):
    l = dq.shape[0]
    nq = ATTN_WIDTH // LANES

    def body(dq_ref, dkv_ref, du_ref, dpre_ref, ds_ref, c_ref, lo_ref, hi_ref, o_ref):
        c, lo, hi = c_ref[...], lo_ref[...], hi_ref[...]
        for blk in range(nq + 1):
            t = dq_ref[:, blk * LANES:(blk + 1) * LANES] if blk < nq else dkv_ref[:, :KV_WIDTH]
            g = t * c + pltpu.roll(t * lo, 8, 1) + pltpu.roll(t * hi, LANES - 8, 1)
            o_ref[:, blk * LANES:(blk + 1) * LANES] = g.astype(BF16)
        o_ref[:, (nq + 1) * LANES:QKV_WIDTH] = dkv_ref[:, KV_WIDTH:].astype(BF16)
        o_ref[:, QKV_WIDTH:] = (du_ref[...] + dpre_ref[...] * ds_ref[...]).astype(BF16)

    tab = pl.BlockSpec((TM_EW, LANES), lambda i: (i, 0))
    wide = pl.BlockSpec((TM_EW, SSM_WIDTH), lambda i: (i, 0))
    return pl.pallas_call(
        body, grid=(l // TM_EW,),
        in_specs=[wide, pl.BlockSpec((TM_EW, 2 * KV_WIDTH), lambda i: (i, 0)), wide, wide,
                  pl.BlockSpec((1, SSM_WIDTH), lambda i: (0, 0)), tab, tab, tab],
        out_specs=pl.BlockSpec((TM_EW, IN_WIDTH), lambda i: (i, 0)),
        out_shape=jax.ShapeDtypeStruct((l, IN_WIDTH), BF16), name="rope_bwd",
        compiler_params=_params(("parallel",)),
    )(dq, dkv, du_ssm, dpre, d_skip, *tabs)


_Q_COLS = ATTN_WIDTH // LANES
_NEG = -1e30


def _window_specs(nb, width, col):
    return [
        pl.BlockSpec((BLOCK, width), lambda n: (jnp.maximum(n - 1, 0), col)),
        pl.BlockSpec((BLOCK, width), lambda n: (n, col)),
        pl.BlockSpec((BLOCK, width), lambda n: (jnp.minimum(n + 1, nb - 1), col)),
    ]


def _stacked_sink(sink_ref, heads):
    rid = lax.broadcasted_iota(jnp.int32, (len(heads) * BLOCK, 1), 0)
    sk = jnp.full(rid.shape, sink_ref[0, heads[-1]], F32)
    for g in range(len(heads) - 2, -1, -1):
        sk = jnp.where(rid < (g + 1) * BLOCK, sink_ref[0, heads[g]], sk)
    return sk


def _attn_fwd(qkv, sink):
    l = qkv.shape[0]
    nb = l // BLOCK
    grp = N_Q_HEADS // N_KV_HEADS

    def body(sink_ref, q_ref, k0, k1, k2, v0, v1, v2, o_ref, lse_ref):
        n = pl.program_id(0)
        q = q_ref[...]
        kw = jnp.concatenate([k0[...], k1[...], k2[...]], axis=0)
        vw = jnp.concatenate([v0[...], v1[...], v2[...]], axis=0)
        row = lax.broadcasted_iota(jnp.int32, (grp * BLOCK, 3 * BLOCK), 0)
        col = lax.broadcasted_iota(jnp.int32, (grp * BLOCK, 3 * BLOCK), 1)
        valid = jnp.abs(col - BLOCK - (row & (BLOCK - 1))) <= WINDOW
        valid &= jnp.logical_not((n == 0) & (col < BLOCK))
        valid &= jnp.logical_not((n == nb - 1) & (col >= 2 * BLOCK))
        for hk in range(N_KV_HEADS):
            heads = range(hk * grp, (hk + 1) * grp)
            qs = jnp.concatenate([q[:, h * HEAD_DIM:(h + 1) * HEAD_DIM] for h in heads], axis=0)
            kh = kw[:, hk * HEAD_DIM:(hk + 1) * HEAD_DIM]
            vh = vw[:, hk * HEAD_DIM:(hk + 1) * HEAD_DIM]
            s = jnp.where(valid, _dg(qs, kh, NT), _NEG)
            sk = _stacked_sink(sink_ref, heads)
            m = jnp.maximum(jnp.max(s, axis=1, keepdims=True), sk)
            p = jnp.exp(s - m)
            denom = jnp.sum(p, axis=1, keepdims=True) + jnp.exp(sk - m)
            o = _dg((p / denom).astype(BF16), vh, NN)
            lse = m + jnp.log(denom)
            for g, h in enumerate(heads):
                o_ref[:, h * HEAD_DIM:(h + 1) * HEAD_DIM] = o[g * BLOCK:(g + 1) * BLOCK]
                lse_ref[:, h:h + 1] = lse[g * BLOCK:(g + 1) * BLOCK]

    return pl.pallas_call(
        body, grid=(nb,),
        in_specs=[pl.BlockSpec(memory_space=pltpu.SMEM),
                  pl.BlockSpec((BLOCK, ATTN_WIDTH), lambda n: (n, 0))]
        + _window_specs(nb, KV_WIDTH, _Q_COLS) + _window_specs(nb, KV_WIDTH, _Q_COLS + 1),
        out_specs=[pl.BlockSpec((BLOCK, ATTN_WIDTH), lambda n: (n, 0)),
                   pl.BlockSpec((BLOCK, N_Q_HEADS), lambda n: (n, 0))],
        out_shape=[jax.ShapeDtypeStruct((l, ATTN_WIDTH), F32), jax.ShapeDtypeStruct((l, N_Q_HEADS), F32)],
        name="attn_fwd", compiler_params=_params(("parallel",)),
    )(sink, qkv, qkv, qkv, qkv, qkv, qkv, qkv)


def _attn_bwd(qkv, attn, dattn, lse, sink):
    l = qkv.shape[0]
    nb = l // BLOCK
    grp = N_Q_HEADS // N_KV_HEADS
    win = 3 * BLOCK

    def body(sink_ref, q_ref, k0, k1, k2, v0, v1, v2, o_ref, d_ref, l_ref, dq_ref, dkv_ref, dsink_ref, ring_ref):
        n = pl.program_id(0)

        @pl.when(n == 0)
        def _():
            dsink_ref[...] = jnp.zeros_like(dsink_ref)
            ring_ref[...] = jnp.zeros_like(ring_ref)

        @pl.when(n < nb)
        def _():
            first, last = n == 0, n == nb - 1
            cat = lambda a, b, c: jnp.concatenate([a[...], b[...], c[...]], axis=0)
            q, kw, vw = q_ref[...], cat(k0, k1, k2), cat(v0, v1, v2)
            dov = d_ref[...]
            prod = o_ref[...] * dov
            dob = dov.astype(BF16)
            lse = l_ref[...]
            row = lax.broadcasted_iota(jnp.int32, (grp * BLOCK, win), 0)
            col = lax.broadcasted_iota(jnp.int32, (grp * BLOCK, win), 1)
            valid = jnp.abs(col - BLOCK - (row & (BLOCK - 1))) <= WINDOW
            valid &= jnp.logical_not(first & (col < BLOCK))
            valid &= jnp.logical_not(last & (col >= 2 * BLOCK))

            dsink_parts, dks, dvs = [], [], []
            for hk in range(N_KV_HEADS):
                heads = range(hk * grp, (hk + 1) * grp)
                ksl = slice(hk * HEAD_DIM, (hk + 1) * HEAD_DIM)
                hsl = [slice(h * HEAD_DIM, (h + 1) * HEAD_DIM) for h in heads]
                stack = lambda parts: jnp.concatenate(parts, axis=0)
                qs = stack([q[:, s_] for s_ in hsl])
                dos = stack([dob[:, s_] for s_ in hsl])
                deltas = stack([jnp.sum(prod[:, s_], axis=1, keepdims=True) for s_ in hsl])
                lses = stack([lse[:, h:h + 1] for h in heads])
                kh, vh = kw[:, ksl], vw[:, ksl]
                s = jnp.where(valid, _dg(qs, kh, NT), _NEG)
                p = jnp.exp(s - lses)
                dp = _dg(dos, vh, NT)
                ds = (p * (dp - deltas)).astype(BF16)
                dq = _dg(ds, kh, NN) * SCORE_SCALE
                sink_rows = jnp.exp(_stacked_sink(sink_ref, heads) - lses) * deltas
                for g in range(grp):
                    dq_ref[:, hsl[g]] = dq[g * BLOCK:(g + 1) * BLOCK]
                    dsink_parts.append(jnp.sum(sink_rows[g * BLOCK:(g + 1) * BLOCK], axis=0, keepdims=True))
                dks.append(_dg(ds, qs, TN))
                dvs.append(_dg(p.astype(BF16), dos, TN))
            dsink_ref[...] -= jnp.concatenate(dsink_parts, axis=1)
            part = jnp.concatenate(dks + dvs, axis=1)
            ring_ref[(n + 2) % 3] += part[0:BLOCK]
            ring_ref[n % 3] += part[BLOCK:2 * BLOCK]
            ring_ref[(n + 1) % 3] = part[2 * BLOCK:]

        @pl.when(n >= 1)
        def _():
            dkv_ref[...] = ring_ref[(n + 2) % 3]

    centre = lambda n: jnp.minimum(n, nb - 1)
    window = lambda width, col: [
        pl.BlockSpec((BLOCK, width), lambda n: (jnp.maximum(centre(n) - 1, 0), col)),
        pl.BlockSpec((BLOCK, width), lambda n: (centre(n), col)),
        pl.BlockSpec((BLOCK, width), lambda n: (jnp.minimum(centre(n) + 1, nb - 1), col))]
    own = lambda width: pl.BlockSpec((BLOCK, width), lambda n: (centre(n), 0))
    return pl.pallas_call(
        body, grid=(nb + 1,),
        in_specs=[pl.BlockSpec(memory_space=pltpu.SMEM), own(ATTN_WIDTH)]
        + window(KV_WIDTH, _Q_COLS) + window(KV_WIDTH, _Q_COLS + 1)
        + [own(ATTN_WIDTH), own(ATTN_WIDTH), own(N_Q_HEADS)],
        out_specs=[own(ATTN_WIDTH), pl.BlockSpec((BLOCK, 2 * KV_WIDTH), lambda n: (jnp.maximum(n - 1, 0), 0)),
                   pl.BlockSpec((1, N_Q_HEADS), lambda n: (0, 0))],
        out_shape=[jax.ShapeDtypeStruct((l, ATTN_WIDTH), F32), jax.ShapeDtypeStruct((l, 2 * KV_WIDTH), F32),
                   jax.ShapeDtypeStruct((1, N_Q_HEADS), F32)],
        scratch_shapes=[pltpu.VMEM((3, BLOCK, 2 * KV_WIDTH), F32)],
        name="attn_bwd", compiler_params=_params(("arbitrary",)),
    )(sink, qkv, qkv, qkv, qkv, qkv, qkv, qkv, attn, dattn, lse)


def _ssm_disc(a_re, a_im, log_step, b_re, b_im):
    step = jnp.exp(log_step)[..., None]
    mag = jnp.exp(a_re * step)
    lb_re, lb_im = mag * jnp.cos(a_im * step), mag * jnp.sin(a_im * step)
    nr, ni = lb_re - 1.0, lb_im
    den = a_re * a_re + a_im * a_im
    f_re = ((nr * a_re + ni * a_im) / den)[..., None]
    f_im = ((ni * a_re - nr * a_im) / den)[..., None]
    return lb_re, lb_im, f_re * b_re - f_im * b_im, f_re * b_im + f_im * b_re


def _ssm_pack(lb_re, lb_im, bb_re, bb_im, c_re, c_im):
    eye = jnp.eye(SSM_CH // SSM_GROUP, dtype=F32)
    ng = SSM_CH // SSM_GROUP

    def diag_b(bb):
        t = bb.reshape(2, SSM_CB, ng, SSM_STATE, SSM_GROUP)
        return jnp.einsum('dkgpc,gh->dkgchp', t, eye).reshape(2, SSM_CB, SSM_CH, SSM_ST)

    def diag_c(cc):
        t = cc.reshape(2, SSM_CB, ng, SSM_GROUP, SSM_STATE)
        return jnp.einsum('dkgcp,gh->dkhpgc', t, eye).reshape(2, SSM_CB, SSM_ST, SSM_CH)

    bcat = jnp.concatenate([diag_b(bb_re), diag_b(bb_im)], axis=-1)
    ccat = jnp.concatenate([diag_c(c_re), -diag_c(c_im)], axis=-2)
    lam_re = lb_re.reshape(2, SSM_CB, 1, SSM_ST)
    lam_im = lb_im.reshape(2, SSM_CB, 1, SSM_ST)
    return bcat, ccat, lam_re, lam_im


def _ssm_unpack(dbcat, dccat, dlam_re, dlam_im):
    ng = SSM_CH // SSM_GROUP
    same = jnp.eye(ng, dtype=F32)[None, None, :, None, :, None]

    def undiag_b(t):
        t = t.reshape(2, SSM_CB, ng, SSM_GROUP, ng, SSM_STATE)
        return jnp.sum(t * same, axis=4).transpose(0, 1, 2, 4, 3).reshape(2, N_SSM_GROUPS, SSM_STATE, SSM_GROUP)

    def undiag_c(t):
        t = t.reshape(2, SSM_CB, ng, SSM_STATE, ng, SSM_GROUP)
        return jnp.sum(t * same, axis=2).transpose(0, 1, 3, 4, 2).reshape(2, N_SSM_GROUPS, SSM_GROUP, SSM_STATE)

    dbb_re, dbb_im = undiag_b(dbcat[..., :SSM_ST]), undiag_b(dbcat[..., SSM_ST:])
    dc_re, dc_im = undiag_c(dccat[:, :, :SSM_ST]), -undiag_c(dccat[:, :, SSM_ST:])
    shape = (2, N_SSM_GROUPS, SSM_STATE)
    return dlam_re.reshape(shape), dlam_im.reshape(shape), dbb_re, dbb_im, dc_re, dc_im


def _to_segments(t):
    l, w = t.shape
    return t.reshape(N_SEG, l // N_SEG, w).transpose(1, 0, 2).reshape(l, w)


def _from_segments(t):
    l, w = t.shape
    return t.reshape(l // N_SEG, N_SEG, w).transpose(1, 0, 2).reshape(l, w)


SSM_RC = 256
SSM_JC = SSM_RC // N_SEG
_RE, _IM = pl.ds(0, SSM_ST), pl.ds(SSM_ST, SSM_ST)


def _cfma(ar, ai, xr, xi, br, bi):
    return ar * xr - ai * xi + br, ar * xi + ai * xr + bi


def _chunk_rows(ci, rev, nc):
    start = jnp.where(rev, (nc - 1 - ci) * SSM_RC, ci * SSM_RC)
    return pl.ds(pl.multiple_of(start, SSM_RC), SSM_RC)


def _scan_chunk(src, dst, ar, ai, rev, nj, ci, carry, prev_ref=None):
    def rows_of(staged, j, k):
        at = jnp.where(rev, SSM_JC - 1 - k, k) if staged else j
        return pl.ds(pl.multiple_of(at * N_SEG, N_SEG), N_SEG)

    for k in range(SSM_JC):
        jj = ci * SSM_JC + k
        j = jnp.where(rev, nj - 1 - jj, jj)
        rows = rows_of(src[1], j, k)
        nr, ni = _cfma(ar, ai, carry[0], carry[1], src[0][rows, _RE], src[0][rows, _IM])
        if dst is not None:
            rows = rows_of(dst[1], j, k)
            dst[0][rows, _RE] = nr
            dst[0][rows, _IM] = ni
        if prev_ref is None:
            carry = (nr, ni)
            continue
        jp = jnp.where(rev, j - 1, j + 1)
        if k == SSM_JC - 1:
            inside = jnp.where((jp >= 0) & (jp < nj), 1.0, 0.0)
            jp = jnp.clip(jp, 0, nj - 1)
        prow = pl.ds(pl.multiple_of(jp * N_SEG, N_SEG), N_SEG)
        xr, xi = prev_ref[prow, _RE], prev_ref[prow, _IM]
        sr, si = nr * xr + ni * xi, ni * xr - nr * xi
        if k == SSM_JC - 1:
            sr, si = inside * sr, inside * si
        carry = (nr, ni, carry[2] + sr, carry[3] + si)
    return carry


def _segment_inits(ar, ai, end_r, end_i, rev, nj):
    pr, pi = ar, ai
    for _ in range(int(math.log2(nj))):
        pr, pi = pr * pr - pi * pi, 2.0 * pr * pi
    seg = lax.broadcasted_iota(jnp.int32, end_r.shape, 0)
    zero = jnp.zeros_like(end_r)

    def chain(shift, keep):
        ir, ii = zero, zero
        for _ in range(N_SEG - 1):
            tr, ti = _cfma(pr, pi, ir, ii, end_r, end_i)
            ir = jnp.where(keep, pltpu.roll(tr, shift, 0), 0.0)
            ii = jnp.where(keep, pltpu.roll(ti, shift, 0), 0.0)
        return ir, ii

    up_r, up_i = chain(1, seg >= 1)
    dn_r, dn_i = chain(N_SEG - 1, seg <= N_SEG - 2)
    return jnp.where(rev, dn_r, up_r), jnp.where(rev, dn_i, up_i)


def _ssm_specs(l):
    act = pl.BlockSpec((l, SSM_CH), lambda k, d: (0, k))
    bmat = pl.BlockSpec((None, None, SSM_CH, 2 * SSM_ST), lambda k, d: (d, k, 0, 0))
    cmat = pl.BlockSpec((None, None, 2 * SSM_ST, SSM_CH), lambda k, d: (d, k, 0, 0))
    lam = pl.BlockSpec((None, None, 1, SSM_ST), lambda k, d: (d, k, 0, 0))
    return act, bmat, cmat, lam


def _ssm_fwd(u_seg, bcat, ccat, lam_re, lam_im):
    l = u_seg.shape[0]
    nj = l // N_SEG
    nc = l // SSM_RC

    def body(u_ref, b_ref, c_ref, lr_ref, li_ref, y_ref, ub_ref, keep_ref, xs_ref, stage0, stage1, keep_sem):
        k, d = pl.program_id(0), pl.program_id(1)
        rev = d == 1
        shape = (N_SEG, SSM_ST)
        ar, ai = jnp.broadcast_to(lr_ref[...], shape), jnp.broadcast_to(li_ref[...], shape)
        zero = jnp.zeros(shape, F32)

        def inputs(ci, stage):
            rows = _chunk_rows(ci, rev, nc)
            ub = u_ref[rows, :].astype(BF16)
            ub_ref[rows, :] = ub
            bu = _dg(ub, b_ref[...], NN)
            stage[...] = bu
            xs_ref[rows, :] = bu

        def first(stage, ci, carry):
            return _scan_chunk((stage, True), None, ar, ai, rev, nj, ci, carry)

        def first_pass(t, carry):
            inputs(2 * t + 1, stage1)
            carry = first(stage0, 2 * t, carry)
            inputs(2 * t + 2, stage0)
            return first(stage1, 2 * t + 1, carry)

        inputs(0, stage0)
        carry = lax.fori_loop(0, nc // 2 - 1, first_pass, (zero, zero))
        inputs(nc - 1, stage1)
        carry = first(stage0, nc - 2, carry)
        end_r, end_i = first(stage1, nc - 1, carry)
        init = _segment_inits(ar, ai, end_r, end_i, rev, nj)

        @pl.when(d == 0)
        def _():
            y_ref[...] = jnp.zeros_like(y_ref)

        def outputs(ci):
            rows = _chunk_rows(ci, rev, nc)
            y_ref[rows, :] += _dg(xs_ref[rows, :].astype(BF16), c_ref[...], NN)
            pltpu.make_async_copy(xs_ref.at[rows], keep_ref.at[d, k, rows], keep_sem).start()

        def second(ci, carry):
            return _scan_chunk((xs_ref, False), (xs_ref, False), ar, ai, rev, nj, ci, carry)

        def second_pass(ci, carry):
            outputs(ci - 1)
            return second(ci, carry)

        lax.fori_loop(1, nc, second_pass, second(0, init))
        outputs(nc - 1)
        pltpu.make_async_copy(xs_ref, keep_ref.at[d, k], keep_sem).wait()

    act, bmat, cmat, lam = _ssm_specs(l)
    return pl.pallas_call(
        body, grid=(SSM_CB, 2), in_specs=[act, bmat, cmat, lam, lam], out_specs=[act, act, ANY],
        out_shape=[jax.ShapeDtypeStruct((l, SSM_WIDTH), F32), jax.ShapeDtypeStruct((l, SSM_WIDTH), BF16),
                   jax.ShapeDtypeStruct((2, SSM_CB, l, 2 * SSM_ST), F32)],
        scratch_shapes=[pltpu.VMEM((l, 2 * SSM_ST), F32), pltpu.VMEM((SSM_RC, 2 * SSM_ST), F32),
                        pltpu.VMEM((SSM_RC, 2 * SSM_ST), F32), pltpu.SemaphoreType.DMA],
        name="ssm_fwd", compiler_params=_params(("parallel", "arbitrary"), vmem_mb=56),
    )(u_seg, bcat.astype(BF16), ccat.astype(BF16), lam_re, lam_im)


def _ssm_bwd(u_seg, dy_seg, states, bcat, ccat, lam_re, lam_im):
    l = u_seg.shape[0]
    nj = l // N_SEG
    nc = l // SSM_RC

    def body(u_ref, dy_ref, keep_ref, b_ref, c_ref, lr_ref, li_ref,
             du_ref, db_ref, dc_ref, dlr_ref, dli_ref, xs_ref, gs_ref, dyb_ref, stage0, stage1, keep_sem):
        k, d = pl.program_id(0), pl.program_id(1)
        rev = d == 1
        back = jnp.logical_not(rev)
        shape = (N_SEG, SSM_ST)
        ar, ai = jnp.broadcast_to(lr_ref[...], shape), -jnp.broadcast_to(li_ref[...], shape)
        zero = jnp.zeros(shape, F32)
        fetch = pltpu.make_async_copy(keep_ref.at[d, k], xs_ref, keep_sem)
        fetch.start()

        def inputs(ci, stage):
            rows = _chunk_rows(ci, back, nc)
            dyb = dy_ref[rows, :].astype(BF16)
            dyb_ref[rows, :] = dyb
            dx = _dg(dyb, c_ref[...], NT)
            stage[...] = dx
            gs_ref[rows, :] = dx

        def first(stage, ci, carry):
            return _scan_chunk((stage, True), None, ar, ai, back, nj, ci, carry)

        def first_pass(t, carry):
            inputs(2 * t + 1, stage1)
            carry = first(stage0, 2 * t, carry)
            inputs(2 * t + 2, stage0)
            return first(stage1, 2 * t + 1, carry)

        inputs(0, stage0)
        carry = lax.fori_loop(0, nc // 2 - 1, first_pass, (zero, zero))
        inputs(nc - 1, stage1)
        carry = first(stage0, nc - 2, carry)
        end_r, end_i = first(stage1, nc - 1, carry)
        init = _segment_inits(ar, ai, end_r, end_i, back, nj)
        fetch.wait()
        db_ref[...] = jnp.zeros_like(db_ref)
        dc_ref[...] = jnp.zeros_like(dc_ref)

        @pl.when(d == 0)
        def _():
            du_ref[...] = jnp.zeros_like(du_ref)

        def outputs(ci, stage):
            rows = _chunk_rows(ci, back, nc)
            g = stage[...].astype(BF16)
            dc_ref[...] += _dg(xs_ref[rows, :].astype(BF16), dyb_ref[rows, :], TN)
            db_ref[...] += _dg(u_ref[rows, :], g, TN)
            du_ref[rows, :] += _dg(g, b_ref[...], NT)

        def second(ci, stage, carry):
            return _scan_chunk((gs_ref, False), (stage, True), ar, ai, back, nj, ci, carry, prev_ref=xs_ref)

        def second_pass(t, carry):
            outputs(2 * t, stage0)
            carry = second(2 * t + 1, stage1, carry)
            outputs(2 * t + 1, stage1)
            return second(2 * t + 2, stage0, carry)

        carry = lax.fori_loop(0, nc // 2 - 1, second_pass, second(0, stage0, init + (zero, zero)))
        outputs(nc - 2, stage0)
        gr, gi, acc_r, acc_i = second(nc - 1, stage1, carry)
        outputs(nc - 1, stage1)

        seg = lax.broadcasted_iota(jnp.int32, shape, 0)
        jb = jnp.where(rev, nj - 1, 0)
        erow = pl.ds(pl.multiple_of((nj - 1 - jb) * N_SEG, N_SEG), N_SEG)

        def before(t):
            up = jnp.where(seg >= 1, pltpu.roll(t, 1, 0), 0.0)
            down = jnp.where(seg <= N_SEG - 2, pltpu.roll(t, N_SEG - 1, 0), 0.0)
            return jnp.where(rev, down, up)

        init_r, init_i = before(xs_ref[erow, _RE]), before(xs_ref[erow, _IM])
        acc_r = acc_r + gr * init_r + gi * init_i
        acc_i = acc_i + gi * init_r - gr * init_i
        dlr_ref[...] = jnp.sum(acc_r, axis=0, keepdims=True)
        dli_ref[...] = jnp.sum(acc_i, axis=0, keepdims=True)

    act, bmat, cmat, lam = _ssm_specs(l)
    return pl.pallas_call(
        body, grid=(SSM_CB, 2), in_specs=[act, act, ANY, bmat, cmat, lam, lam],
        out_specs=[act, bmat, cmat, lam, lam],
        out_shape=[jax.ShapeDtypeStruct((l, SSM_WIDTH), F32),
                   jax.ShapeDtypeStruct(bcat.shape, F32), jax.ShapeDtypeStruct(ccat.shape, F32),
                   jax.ShapeDtypeStruct(lam_re.shape, F32), jax.ShapeDtypeStruct(lam_im.shape, F32)],
        scratch_shapes=[pltpu.VMEM((l, 2 * SSM_ST), F32), pltpu.VMEM((l, 2 * SSM_ST), F32),
                        pltpu.VMEM((l, SSM_CH), BF16),
                        pltpu.VMEM((SSM_RC, 2 * SSM_ST), F32), pltpu.VMEM((SSM_RC, 2 * SSM_ST), F32),
                        pltpu.SemaphoreType.DMA],
        name="ssm_bwd", compiler_params=_params(("parallel", "arbitrary"), vmem_mb=58),
    )(u_seg, dy_seg, states, bcat.astype(BF16), ccat.astype(BF16), lam_re, lam_im)


def _glu_fwd(y_ssm, u, d_skip, w_glu):
    l, w = u.shape

    def body(y_ref, u_ref, d_ref, w_ref, pre_ref, s_ref, ys_ref):
        pre = y_ref[...] + d_ref[...] * u_ref[...]
        z = _gelu(pre)
        s = _dg(z.astype(BF16), w_ref[...], NN)
        pre_ref[...] = pre
        s_ref[...] = s
        ys_ref[...] = z * _sigmoid(s)

    row = pl.BlockSpec((TM_EW, w), lambda i: (i, 0))
    out = jax.ShapeDtypeStruct((l, w), F32)
    return pl.pallas_call(
        body, grid=(l // TM_EW,),
        in_specs=[row, row, pl.BlockSpec((1, w), lambda i: (0, 0)), pl.BlockSpec((w, w), lambda i: (0, 0))],
        out_specs=[row, row, row], out_shape=[out, out, out], name="glu_fwd",
        compiler_params=_params(("parallel",)),
    )(y_ssm, u, d_skip, w_glu)


def _glu_bwd(pre, s, dys, u, d_skip, w_glu):
    l, w = u.shape

    def body(pre_ref, s_ref, dys_ref, u_ref, d_ref, w_ref, dpre_ref, z_ref, ds_ref, dd_ref):
        pre, dys = pre_ref[...], dys_ref[...]
        z = _gelu(pre)
        sig = _sigmoid(s_ref[...])
        ds = (dys * z * sig * (1.0 - sig)).astype(BF16)
        dz = dys * sig + _dg(ds, w_ref[...], NT)
        dpre = dz * _gelu_grad(pre)
        dpre_ref[...] = dpre
        z_ref[...] = z.astype(BF16)
        ds_ref[...] = ds

        @pl.when(pl.program_id(0) == 0)
        def _():
            dd_ref[...] = jnp.zeros_like(dd_ref)

        dd_ref[...] += jnp.sum(dpre * u_ref[...], axis=0, keepdims=True)

    row = pl.BlockSpec((TM_EW, w), lambda i: (i, 0))
    vec = pl.BlockSpec((1, w), lambda i: (0, 0))
    return pl.pallas_call(
        body, grid=(l // TM_EW,),
        in_specs=[row, row, row, row, vec, pl.BlockSpec((w, w), lambda i: (0, 0))],
        out_specs=[row, row, row, vec],
        out_shape=[jax.ShapeDtypeStruct((l, w), F32), jax.ShapeDtypeStruct((l, w), BF16),
                   jax.ShapeDtypeStruct((l, w), BF16), jax.ShapeDtypeStruct((1, w), F32)],
        name="glu_bwd", compiler_params=_params(("arbitrary",)),
    )(pre, s, dys, u, d_skip, w_glu)


TM_CV = 512
TC_CV = 256
TM_CF = 256
TC_CF = D_FF // 2
HALO = SUBLANES


def _conv_specs(l, col0, tm=TM_CV, tc=TC_CV):
    per = tm // HALO
    nh = l // HALO
    off = col0 // tc
    return [
        pl.BlockSpec((HALO, tc), lambda j, i: (jnp.maximum(i * per - 1, 0), j + off)),
        pl.BlockSpec((tm, tc), lambda j, i: (i, j + off)),
        pl.BlockSpec((HALO, tc), lambda j, i: (jnp.minimum((i + 1) * per, nh - 1), j + off)),
    ]


def _ext(prev_ref, mid_ref, next_ref, first, last):
    p = jnp.where(first, 0.0, prev_ref[...])
    n = jnp.where(last, 0.0, next_ref[...])
    return jnp.concatenate([p, mid_ref[...], n], axis=0)


def _shift_dn(t):
    return pltpu.roll(t, 1, 0)


def _shift_up(t):
    return pltpu.roll(t, t.shape[0] - 1, 0)


def _conv3(e, w_ref, b_ref):
    return w_ref[0:1, :] * _shift_dn(e) + w_ref[1:2, :] * e + w_ref[2:3, :] * _shift_up(e) + b_ref[...]


def _convffn_fwd(up_pre, conv_w, conv_b):
    l = up_pre.shape[0]
    tm, tc = TM_CF, TC_CF
    ni = l // tm
    wspec = lambda off: pl.BlockSpec((3, tc), lambda j, i: (0, j + off))
    bspec = lambda off: pl.BlockSpec((1, tc), lambda j, i: (0, j + off))
    voff = D_FF // tc

    def body(gp, gm, gn, vp, vm, vn, wg, bg, wv, bv, o_ref):
        i = pl.program_id(1)
        first, last = i == 0, i == ni - 1
        gate = _conv3(_ext(gp, gm, gn, first, last), wg, bg)[HALO:HALO + tm]
        val = _conv3(_ext(vp, vm, vn, first, last), wv, bv)[HALO:HALO + tm]
        o_ref[...] = (gate * _sigmoid(gate) * val).astype(BF16)

    return pl.pallas_call(
        body, grid=(D_FF // tc, ni),
        in_specs=_conv_specs(l, 0, tm, tc) + _conv_specs(l, D_FF, tm, tc)
        + [wspec(0), bspec(0), wspec(voff), bspec(voff)],
        out_specs=pl.BlockSpec((tm, tc), lambda j, i: (i, j)),
        out_shape=jax.ShapeDtypeStruct((l, D_FF), BF16), name="convffn_fwd",
        compiler_params=_params(("parallel", "parallel")),
    )(up_pre, up_pre, up_pre, up_pre, up_pre, up_pre, conv_w, conv_b, conv_w, conv_b)


HALO_B = 2 * SUBLANES


def _convffn_bwd(up_pre, dx2b, w_down, conv_w, conv_b):
    l = up_pre.shape[0]
    ni = l // TM_CV
    d = dx2b.shape[1]
    wspec = lambda off: pl.BlockSpec((3, TC_CV), lambda i, j: (0, j + off))
    bspec = lambda off: pl.BlockSpec((1, TC_CV), lambda i, j: (0, j + off))
    voff = D_FF // TC_CV
    swap = lambda spec: pl.BlockSpec(spec.block_shape, lambda i, j, f=spec.index_map: f(j, i))
    per, nh = TM_CV // HALO_B, l // HALO_B
    dx_specs = [pl.BlockSpec((HALO_B, d), lambda i, j: (jnp.maximum(i * per - 1, 0), 0)),
                pl.BlockSpec((TM_CV, d), lambda i, j: (i, 0)),
                pl.BlockSpec((HALO_B, d), lambda i, j: (jnp.minimum((i + 1) * per, nh - 1), 0))]

    def body(gp, gm, gn, vp, vm, vn, xp, xm, xn, wd, wg, bg, wv, bv, dup_ref, pg_ref, pv_ref):
        i = pl.program_id(0)
        first, last = i == 0, i == ni - 1
        ge, ve = _ext(gp, gm, gn, first, last), _ext(vp, vm, vn, first, last)
        zero = jnp.zeros((HALO_B, d), BF16)
        dx = jnp.concatenate([jnp.where(first, zero, xp[...]), xm[...], jnp.where(last, zero, xn[...])], axis=0)
        de = _dg(dx, wd[...], NT)[HALO_B - HALO:HALO_B + TM_CV + HALO]
        taps = [(_shift_dn(e), e, _shift_up(e)) for e in (ge, ve)]
        conv = lambda t, w_ref, b_ref: w_ref[0:1, :] * t[0] + w_ref[1:2, :] * t[1] + w_ref[2:3, :] * t[2] + b_ref[...]
        gate, val = conv(taps[0], wg, bg), conv(taps[1], wv, bv)
        sig = _sigmoid(gate)
        silu = gate * sig
        dgate = de * val * (sig + silu * (1.0 - sig))
        dval = de * silu
        mid = slice(HALO, HALO + TM_CV)
        rid = lax.broadcasted_iota(jnp.int32, (SUBLANES, TC_CV), 0)
        for half, (dup, tap, w_ref, p_ref) in enumerate(((dgate, taps[0], wg, pg_ref), (dval, taps[1], wv, pv_ref))):
            dpre = w_ref[0:1, :] * _shift_up(dup) + w_ref[1:2, :] * dup + w_ref[2:3, :] * _shift_dn(dup)
            dup_ref[half] = dpre[mid].astype(BF16)
            dm_ = dup[mid]
            sums = [jnp.sum(dm_ * t[mid], axis=0, keepdims=True) for t in tap]
            sums.append(jnp.sum(dm_, axis=0, keepdims=True))
            acc = jnp.zeros((SUBLANES, TC_CV), F32)
            for k, sk in enumerate(sums):
                acc = jnp.where(rid == k, sk, acc)
            p_ref[...] = acc

    par = pl.BlockSpec((None, SUBLANES, TC_CV), lambda i, j: (i, 0, j))
    dup, pg, pv = pl.pallas_call(
        body, grid=(ni, D_FF // TC_CV),
        in_specs=[swap(s) for s in _conv_specs(l, 0) + _conv_specs(l, D_FF)] + dx_specs
        + [pl.BlockSpec((TC_CV, d), lambda i, j: (j, 0)), wspec(0), bspec(0), wspec(voff), bspec(voff)],
        out_specs=[pl.BlockSpec((2, TM_CV, TC_CV), lambda i, j: (0, i, j)), par, par],
        out_shape=[jax.ShapeDtypeStruct((2, l, D_FF), BF16),
                   jax.ShapeDtypeStruct((ni, SUBLANES, D_FF), F32), jax.ShapeDtypeStruct((ni, SUBLANES, D_FF), F32)],
        name="convffn_bwd", compiler_params=_params(("parallel", "parallel")),
    )(up_pre, up_pre, up_pre, up_pre, up_pre, up_pre, dx2b, dx2b, dx2b, w_down, conv_w, conv_b, conv_w, conv_b)
    return dup, jnp.concatenate([jnp.sum(pg, axis=0), jnp.sum(pv, axis=0)], axis=1)


def _local_step(x, target, wb, sp, mixer_weights=None, late_weights=None, grads_ready=None,
                grads_next=None):
    l = x.shape[0]
    tabs = _rope_tables(l)
    disc = _ssm_disc(sp["a_re"], sp["a_im"], sp["log_step"], sp["b_re"], sp["b_im"])
    bcat, ccat, lam_re, lam_im = _ssm_pack(*disc, sp["c_re"], sp["c_im"])
    d_skip = sp["d_skip"].reshape(1, SSM_WIDTH)

    h, qkv, u = _rms_mm_rope(x, sp["norm_mix_g"], wb["w_in"], tabs, "mm_in")
    attn, lse = _attn_fwd(qkv, sp["sink"])
    y_seg, u_seg, states = _ssm_fwd(_to_segments(u), bcat, ccat, lam_re, lam_im)
    y_ssm = _from_segments(y_seg)
    if mixer_weights is not None:
        wb = dict(wb, **mixer_weights(attn))
    pre, s_glu, ys = _glu_fwd(y_ssm, u, d_skip, wb["w_glu"])
    mixed, x1, h2 = _mix_mm_res_rms(attn, ys, sp["norm_attn_g"], sp["norm_ssm_g"], wb["w_out"], x,
                                    sp["norm_ffn_g"], "mm_out")
    if late_weights is not None:
        wb = dict(wb, **late_weights(h2))
    up_pre = _mm_nn_cols(h2, wb["w_up"], min(l, 1024), "mm_up")
    conv_w = wb["conv_w"]
    act = _convffn_fwd(up_pre, conv_w, sp["conv_b"])
    loss, dx2, dx2b, d_final_g = _mm_res_loss(act, wb["w_down"], x1, sp["norm_final_g"].reshape(1, D_MODEL), target)

    g = {"norm_final_g": d_final_g.reshape(D_MODEL)}
    g["w_down"] = _mm_tn(act, dx2b, D_FF // 2, 512, "mm_down_dw")
    dup_pre, conv_par = _convffn_bwd(up_pre, dx2b, wb["w_down"], conv_w, sp["conv_b"])
    g["conv_w"], g["conv_b"] = conv_par[0:3], conv_par[3:4]
    g["w_up"] = _mm_tn_cols(h2, dup_pre, wb["w_up"].shape[0], 512, "mm_up_dw")
    dx1, dx1b, g["norm_ffn_g"] = _mm_cols_rms_bwd(dup_pre, wb["w_up"], x1, sp["norm_ffn_g"], dx2, "mm_up_dx")
    g["w_out"] = _mm_tn(mixed, dx1b, 1024, 1024, "mm_out_dw")
    zero = grads_ready(g["w_up"], g["w_down"], g["w_out"]) if grads_ready is not None else 0.0
    dattn, dys, g["norm_attn_g"], g["norm_ssm_g"] = _mm_mix_bwd(
        dx1b, wb["w_out"], attn, ys, sp["norm_attn_g"] + zero, sp["norm_ssm_g"], "mm_out_dx")
    dpre, zb, dsb, dd = _glu_bwd(pre, s_glu, dys, u, d_skip, wb["w_glu"])
    g["d_skip"] = dd.reshape(N_SSM_GROUPS, SSM_GROUP)
    g["w_glu"] = _mm_tn(zb, dsb, 512, 512, "mm_glu_dw")
    zero = grads_next(g["w_glu"]) if grads_next is not None else 0.0
    du_seg, dbcat, dccat, dlam_re, dlam_im = _ssm_bwd(u_seg, _to_segments(dpre), states, bcat, ccat,
                                                      lam_re + zero, lam_im)
    dlb_re, dlb_im, dbb_re, dbb_im, g["c_re"], g["c_im"] = _ssm_unpack(dbcat, dccat, dlam_re, dlam_im)
    _, disc_vjp = jax.vjp(_ssm_disc, sp["a_re"], sp["a_im"], sp["log_step"], sp["b_re"], sp["b_im"])
    g["a_re"], g["a_im"], g["log_step"], g["b_re"], g["b_im"] = disc_vjp((dlb_re, dlb_im, dbb_re, dbb_im))
    dq, dkv, g["sink"] = _attn_bwd(qkv, attn, dattn, lse, sp["sink"])
    dproj = _rope_bwd(dq, dkv, _from_segments(du_seg), dpre, d_skip, tabs)
    g["w_in"] = _mm_tn(dproj, h, IN_WIDTH // 5, D_MODEL, "mm_in_dw")
    grad_x, _, g["norm_mix_g"] = _mm_nn_rms_bwd(dproj, wb["w_in"], x, sp["norm_mix_g"], dx1, "mm_in_dx")
    return loss, grad_x, g


MESH = pl.DeviceIdType.MESH
ANY = pl.BlockSpec(memory_space=pl.ANY)


def _place():
    x, y, c = lax.axis_index("x"), lax.axis_index("y"), lax.axis_index("c")
    chips = [(1 - x, y), (x, 1 - y), (1 - x, 1 - y)]
    return x, y, c, chips


def _chip_index(px, py):
    return 2 * px + py


CHUNK_BYTES = 256 * 1024
MAX_CHUNKS = 16


def _row_chunks(rows, row_bytes, align):
    n = max(1, min(MAX_CHUNKS, (rows * row_bytes) // CHUNK_BYTES))
    per = -(-rows // n)
    per = -(-per // align) * align
    return [(r0, min(per, rows - r0)) for r0 in range(0, rows, per)]


def _align_of(dtype):
    return SUBLANES * 4 // jnp.dtype(dtype).itemsize


def _remote(src, dst, send_sem, recv_sem, to):
    return pltpu.make_async_remote_copy(src_ref=src, dst_ref=dst, send_sem=send_sem, recv_sem=recv_sem,
                                        device_id=to, device_id_type=MESH)


CAST_ROWS = 64


def _gather_weights(shards, dtypes):
    nw = len(shards)

    def body(*refs):
        w_refs, o_refs = refs[:nw], refs[nw:2 * nw]
        send_sems, recv_sems, in_sems, out_sems = refs[2 * nw:2 * nw + 4]
        raw, cast = refs[2 * nw + 4:3 * nw + 4], refs[3 * nw + 4:]
        x, y, c, chips = _place()
        mine = _chip_index(x, y)
        sibling = (x, y, 1 - c)

        def rows_of(ref, chip, r0, nr):
            return ref.at[chip, pl.ds(r0, nr), :]

        def copy(wi, k, src, dst, to):
            return _remote(src, dst, send_sems.at[wi, k], recv_sems.at[wi, k], to)

        geo = []
        for wi in range(nw):
            rows, cols = w_refs[wi].shape
            row_bytes = cols * jnp.dtype(dtypes[wi]).itemsize
            geo.append((rows // 2, _row_chunks(rows // 2, row_bytes, _align_of(dtypes[wi]))))

        stage_in = [pltpu.make_async_copy(w_refs[wi], raw[wi], in_sems.at[wi]) for wi in range(nw)]
        for cp in stage_in:
            cp.start()
        staged = [raw[wi] if dtypes[wi] == w_refs[wi].dtype else cast[wi] for wi in range(nw)]
        stage_out = []
        for wi in range(nw):
            stage_in[wi].wait()
            if staged[wi] is not raw[wi]:
                def cast_rows(i, _, wi=wi):
                    rows = pl.ds(pl.multiple_of(i * CAST_ROWS, CAST_ROWS), CAST_ROWS)
                    cast[wi][rows, :] = raw[wi][rows, :].astype(dtypes[wi])
                    return 0

                lax.fori_loop(0, w_refs[wi].shape[0] // CAST_ROWS, cast_rows, 0)
            cp = pltpu.make_async_copy(staged[wi], o_refs[wi].at[mine], out_sems.at[wi])
            cp.start()
            stage_out.append(cp)

        for wi in range(nw):
            hr, half_chunks = geo[wi]
            for j, chip in enumerate(chips):
                for r0, nr in half_chunks:
                    copy(wi, j, staged[wi].at[pl.ds(c * hr + r0, nr), :],
                         rows_of(o_refs[wi], mine, c * hr + r0, nr), (*chip, c)).start()
        for wi in range(nw):
            hr, half_chunks = geo[wi]
            for j, chip in enumerate(chips):
                got = rows_of(o_refs[wi], _chip_index(*chip), c * hr, hr)
                copy(wi, j, got, got, (*chip, c)).wait_recv()
                for r0, nr in half_chunks:
                    piece = rows_of(o_refs[wi], _chip_index(*chip), c * hr + r0, nr)
                    copy(wi, 3 + j, piece, piece, sibling).start()
        for wi in range(nw):
            hr = geo[wi][0]
            for j, chip in enumerate(chips):
                got = rows_of(o_refs[wi], _chip_index(*chip), (1 - c) * hr, hr)
                copy(wi, 3 + j, got, got, sibling).wait_recv()
        for wi in range(nw):
            hr = geo[wi][0]
            sent = rows_of(o_refs[wi], mine, c * hr, hr)
            for k in range(6):
                copy(wi, k, sent, sent, sibling).wait_send()
            stage_out[wi].wait()

    return pl.pallas_call(
        body, in_specs=[ANY] * nw, out_specs=[ANY] * nw,
        out_shape=[jax.ShapeDtypeStruct((4, *s.shape), t) for s, t in zip(shards, dtypes)],
        scratch_shapes=[pltpu.SemaphoreType.DMA((nw, 6)), pltpu.SemaphoreType.DMA((nw, 6)),
                        pltpu.SemaphoreType.DMA((nw,)), pltpu.SemaphoreType.DMA((nw,))]
        + [pltpu.VMEM(s.shape, s.dtype) for s in shards] + [pltpu.VMEM(s.shape, t) for s, t in zip(shards, dtypes)],
        name="gather_weights", compiler_params=_params(vmem_mb=40),
    )(*shards)


HBM = pl.BlockSpec(memory_space=pltpu.HBM)
SEM = pl.BlockSpec(memory_space=pltpu.SEMAPHORE)
EFFECT = pltpu.SideEffectType.DATAFLOW_SIDE_EFFECTING


def _cast_place(w, place, dtype, after, name):
    rows, cols = w.shape
    tr = _row_tile(rows, cols, _align_of(dtype))

    def body(p_ref, w_ref, after_ref, o_ref):
        del p_ref, after_ref
        o_ref[...] = w_ref[...].astype(dtype)

    grid_spec = pltpu.PrefetchScalarGridSpec(
        num_scalar_prefetch=1, grid=(rows // tr,),
        in_specs=[pl.BlockSpec((tr, cols), lambda i, p: (i, 0)), ANY],
        out_specs=pl.BlockSpec((None, tr, cols), lambda i, p: (p[1], i, 0)))
    return pl.pallas_call(body, grid_spec=grid_spec, out_shape=jax.ShapeDtypeStruct((4, rows, cols), dtype),
                          name=name, compiler_params=_params(("parallel",)))(place, w, after)


def _split_start(name, arrays, n_pairs, issue):
    n = len(arrays)

    def body(*refs):
        issue(refs[:n], refs[n:n + n_pairs], refs[n + n_pairs:n + 2 * n_pairs])
        token = refs[2 * n + 2 * n_pairs]
        token[...] = jnp.zeros_like(token)

    dma = pltpu.SemaphoreType.DMA(())
    outs = pl.pallas_call(
        body, name=name,
        out_shape=[dma] * (2 * n_pairs) + [pltpu.HBM(t.shape, t.dtype) for t in arrays]
        + [jax.ShapeDtypeStruct((SUBLANES, LANES), F32)],
        in_specs=[HBM] * n, out_specs=[SEM] * (2 * n_pairs) + [HBM] * n + [pl.BlockSpec(memory_space=pltpu.VMEM)],
        input_output_aliases={a: 2 * n_pairs + a for a in range(n)},
        compiler_params=pltpu.CompilerParams(has_side_effects=EFFECT),
    )(*[pltpu.with_memory_space_constraint(t, pltpu.HBM) for t in arrays])
    return outs[:n_pairs], outs[n_pairs:2 * n_pairs], outs[2 * n_pairs:2 * n_pairs + n], outs[-1]


def _split_wait(name, send_sems, recv_sems, flying, sizes, after):
    n, n_pairs = len(flying), len(send_sems)

    def body(*refs):
        x, y, c, _ = _place()
        for k, ref in enumerate(sizes(refs[:n])):
            cp = _remote(ref, ref, refs[n + k], refs[n + n_pairs + k], (x, y, 1 - c))
            cp.wait_send()
            cp.wait_recv()

    return pl.pallas_call(
        body, name=name, out_shape=[pltpu.HBM(t.shape, t.dtype) for t in flying],
        in_specs=[HBM] * n + [SEM] * (2 * n_pairs) + [ANY], out_specs=[HBM] * n,
        input_output_aliases={a: a for a in range(n)},
        compiler_params=pltpu.CompilerParams(has_side_effects=EFFECT),
    )(*flying, *send_sems, *recv_sems, after)


def _spread_start(lands, name):
    def issue(land_refs, send_sems, recv_sems):
        x, y, c, chips = _place()
        mine = _chip_index(x, y)
        for a, land in enumerate(land_refs):
            _, rows, cols = land.shape
            hr = rows // 2
            row_bytes = cols * jnp.dtype(land.dtype).itemsize
            for r0, nr in _row_chunks(hr, row_bytes, _align_of(land.dtype)):
                piece = land.at[mine, pl.ds(c * hr + r0, nr), :]
                for chip in chips:
                    for core in (0, 1):
                        _remote(piece, piece, send_sems[a], recv_sems[a], (*chip, core)).start()

    return _split_start(name, lands, len(lands), issue)


def _spread_wait(send_sems, recv_sems, flying, after, name):
    return _split_wait(name, send_sems, recv_sems, flying, lambda refs: [r.at[pl.ds(0, 3)] for r in refs], after)


def _pair_start(grads):
    n = len(grads)
    zones = [lax.empty((4, g.shape[1] // 2, g.shape[2]), F32) for g in grads]

    def issue(refs, send_sems, recv_sems):
        x, y, c, _ = _place()
        for a in range(n):
            g_ref, z_ref = refs[a], refs[n + a]
            _, rows, cols = g_ref.shape
            hr = rows // 2
            for k in range(4):
                for r0, nr in _row_chunks(hr, cols * 4, SUBLANES):
                    _remote(g_ref.at[k, pl.ds((1 - c) * hr + r0, nr), :], z_ref.at[k, pl.ds(r0, nr), :],
                            send_sems[a], recv_sems[a], (x, y, 1 - c)).start()

    return _split_start("pair_start", list(grads) + zones, n, issue)


def _pair_wait(send_sems, recv_sems, flying, after):
    n = len(flying) // 2
    out = _split_wait("pair_wait", send_sems, recv_sems, flying, lambda refs: list(refs[n:]), after)
    return out[:n], out[n:]


def _chip_start(sums):
    n = len(sums)
    zones = [lax.empty((3, *s.shape[1:]), s.dtype) for s in sums]

    def issue(refs, send_sems, recv_sems):
        x, y, c, chips = _place()
        for a in range(n):
            s_ref, z_ref = refs[a], refs[n + a]
            _, rows, cols = s_ref.shape
            row_bytes = cols * jnp.dtype(s_ref.dtype).itemsize
            for r0, nr in _row_chunks(rows, row_bytes, _align_of(s_ref.dtype)):
                for j, chip in enumerate(chips):
                    _remote(s_ref.at[_chip_index(*chip), pl.ds(r0, nr), :], z_ref.at[j, pl.ds(r0, nr), :],
                            send_sems[a], recv_sems[a], (*chip, c)).start()

    return _split_start("chip_start", list(sums) + zones, n, issue)


def _chip_wait(send_sems, recv_sems, flying, after):
    n = len(flying) // 2
    return _split_wait("chip_wait", send_sems, recv_sems, flying, lambda refs: list(refs[n:]), after)[n:]


def _pair_exchange(grads):
    na = len(grads)

    def body(*refs):
        g_refs, o_refs = refs[:na], refs[na:2 * na]
        send_sems, recv_sems = refs[2 * na:]
        x, y, c, _ = _place()
        sibling = (x, y, 1 - c)
        for ai in range(na):
            _, rows, cols = g_refs[ai].shape
            hr = rows // 2
            for k in range(4):
                for r0, nr in _row_chunks(hr, cols * 4, SUBLANES):
                    _remote(g_refs[ai].at[k, pl.ds((1 - c) * hr + r0, nr), :], o_refs[ai].at[k, pl.ds(r0, nr), :],
                            send_sems.at[ai], recv_sems.at[ai], sibling).start()
        for ai in range(na):
            _remote(o_refs[ai], o_refs[ai], send_sems.at[ai], recv_sems.at[ai], sibling).wait()

    return pl.pallas_call(
        body, in_specs=[ANY] * na, out_specs=[ANY] * na,
        out_shape=[jax.ShapeDtypeStruct((4, g.shape[1] // 2, g.shape[2]), F32) for g in grads],
        scratch_shapes=[pltpu.SemaphoreType.DMA((na,)), pltpu.SemaphoreType.DMA((na,))],
        name="pair_exchange",
    )(*grads)


def _row_tile(rows, cols, align, elems=256 * 1024):
    best = align
    for cand in range(align, rows + 1, align):
        if rows % cand == 0 and cand * cols <= elems:
            best = cand
    return best


def _pair_sum(g, got, place, transit, name):
    _, rows, cols = g.shape
    hr = rows // 2
    tr = _row_tile(hr, cols, _align_of(transit), 512 * 1024)
    nt = hr // tr

    def body(p_ref, g_ref, r_ref, s_ref, own_ref):
        total = g_ref[...] + r_ref[...]
        s_ref[...] = total.astype(transit)

        @pl.when(pl.program_id(1) == p_ref[1])
        def _():
            own_ref[...] = total

    grid_spec = pltpu.PrefetchScalarGridSpec(
        num_scalar_prefetch=1, grid=(nt, 4),
        in_specs=[pl.BlockSpec((None, tr, cols), lambda i, k, p: (k, p[0] * nt + i, 0)),
                  pl.BlockSpec((None, tr, cols), lambda i, k, p: (k, i, 0))],
        out_specs=[pl.BlockSpec((None, tr, cols), lambda i, k, p: (k, i, 0)),
                   pl.BlockSpec((tr, cols), lambda i, k, p: (i, 0))])
    return pl.pallas_call(
        body, grid_spec=grid_spec,
        out_shape=[jax.ShapeDtypeStruct((4, hr, cols), transit), jax.ShapeDtypeStruct((hr, cols), F32)],
        name=name, compiler_params=_params(("parallel", "arbitrary")),
    )(place, g, got)


def _chip_exchange(sums):
    na = len(sums)

    def body(*refs):
        s_refs, o_refs = refs[:na], refs[na:2 * na]
        send_sems, recv_sems = refs[2 * na:]
        x, y, c, chips = _place()
        for ai in range(na):
            _, rows, cols = s_refs[ai].shape
            row_bytes = cols * jnp.dtype(s_refs[ai].dtype).itemsize
            for r0, nr in _row_chunks(rows, row_bytes, _align_of(s_refs[ai].dtype)):
                for j, chip in enumerate(chips):
                    _remote(s_refs[ai].at[_chip_index(*chip), pl.ds(r0, nr), :], o_refs[ai].at[j, pl.ds(r0, nr), :],
                            send_sems.at[ai, j], recv_sems.at[ai, j], (*chip, c)).start()
        for ai in range(na):
            for j, chip in enumerate(chips):
                _remote(o_refs[ai].at[j], o_refs[ai].at[j], send_sems.at[ai, j], recv_sems.at[ai, j],
                        (*chip, c)).wait()

    return pl.pallas_call(
        body, in_specs=[ANY] * na, out_specs=[ANY] * na,
        out_shape=[jax.ShapeDtypeStruct((3, *s.shape[1:]), s.dtype) for s in sums],
        scratch_shapes=[pltpu.SemaphoreType.DMA((na, 3)), pltpu.SemaphoreType.DMA((na, 3))],
        name="chip_exchange",
    )(*sums)


def _chip_sum(own, landed, name):
    hr, cols = own.shape
    tr = _row_tile(hr, cols, _align_of(landed.dtype))

    def body(o_ref, l_ref, f_ref):
        acc = o_ref[...]
        for j in range(3):
            acc = acc + l_ref[j].astype(F32)
        f_ref[...] = acc

    return pl.pallas_call(
        body, grid=(hr // tr,),
        in_specs=[pl.BlockSpec((tr, cols), lambda i: (i, 0)), pl.BlockSpec((3, tr, cols), lambda i: (0, i, 0))],
        out_specs=pl.BlockSpec((tr, cols), lambda i: (i, 0)),
        out_shape=jax.ShapeDtypeStruct((hr, cols), F32), name=name,
        compiler_params=_params(("parallel",)),
    )(own, landed)


def _final_exchange(halves, small):
    nh = len(halves)

    def body(*refs):
        h_refs, s_ref = refs[:nh], refs[nh]
        o_refs, so_ref = refs[nh + 1:2 * nh + 1], refs[2 * nh + 1]
        send_sems, recv_sems, local_sem, ssend_sems, srecv_sems = refs[2 * nh + 2:]
        x, y, c, _ = _place()
        me = 4 * x + 2 * y + c
        sibling = (x, y, 1 - c)
        for hi in range(nh):
            hr, cols = h_refs[hi].shape
            for r0, nr in _row_chunks(hr, cols * 4, SUBLANES):
                _remote(h_refs[hi].at[pl.ds(r0, nr), :], o_refs[hi].at[pl.ds(r0, nr), :],
                        send_sems.at[hi], recv_sems.at[hi], sibling).start()
        small_cps = [pltpu.make_async_copy(s_ref, so_ref.at[me], local_sem)]
        for r in range(1, 8):
            fx, fy, fc = (r >> 2) & 1, (r >> 1) & 1, r & 1
            peer = (1 - x if fx else x, 1 - y if fy else y, 1 - c if fc else c)
            small_cps.append(_remote(s_ref, so_ref.at[me], ssend_sems.at[r - 1], srecv_sems.at[r - 1], peer))
        for cp in small_cps:
            cp.start()
        for hi in range(nh):
            _remote(h_refs[hi], o_refs[hi], send_sems.at[hi], recv_sems.at[hi], sibling).wait()
        for cp in small_cps:
            cp.wait()

    return pl.pallas_call(
        body, in_specs=[ANY] * (nh + 1), out_specs=[ANY] * (nh + 1),
        out_shape=[jax.ShapeDtypeStruct(h.shape, F32) for h in halves]
        + [jax.ShapeDtypeStruct((8, *small.shape), F32)],
        scratch_shapes=[pltpu.SemaphoreType.DMA((nh,)), pltpu.SemaphoreType.DMA((nh,)),
                        pltpu.SemaphoreType.DMA, pltpu.SemaphoreType.DMA((7,)), pltpu.SemaphoreType.DMA((7,))],
        name="final_exchange",
    )(*halves, small)


def _adamw_halves(w, own, other, m, v, place, name):
    r, c = w.shape
    hr = r // 2
    tr = _row_tile(hr, c, SUBLANES)
    nt = hr // tr
    c1 = 1.0 - ADAM_B1 ** ADAM_STEP
    c2 = 1.0 - ADAM_B2 ** ADAM_STEP

    def body(p_ref, w_ref, own_ref, other_ref, m_ref, v_ref, g_ref, d_ref, nm_ref, nv_ref):
        mine = pl.program_id(0) // nt == p_ref[0]
        gv = jnp.where(mine, own_ref[...], other_ref[...])
        nm = ADAM_B1 * m_ref[...] + (1.0 - ADAM_B1) * gv
        nv = ADAM_B2 * v_ref[...] + (1.0 - ADAM_B2) * (gv * gv)
        g_ref[...] = gv
        d_ref[...] = -ADAM_LR * ((nm / c1) / (jnp.sqrt(nv / c2) + ADAM_EPS) + ADAM_WD * w_ref[...])
        nm_ref[...] = nm
        nv_ref[...] = nv

    full = pl.BlockSpec((tr, c), lambda i, p: (i, 0))
    half = pl.BlockSpec((tr, c), lambda i, p: (i % nt, 0))
    out = jax.ShapeDtypeStruct((r, c), F32)
    grid_spec = pltpu.PrefetchScalarGridSpec(num_scalar_prefetch=1, grid=(2 * nt,),
                                             in_specs=[full, half, half, full, full], out_specs=[full] * 4)
    return pl.pallas_call(body, grid_spec=grid_spec, out_shape=[out] * 4, name=name,
                          compiler_params=_params(("parallel",)))(place, w, own, other, m, v)


def _adamw_many(ws, gs, ms, vs, name):
    n = len(ws)
    c1 = 1.0 - ADAM_B1 ** ADAM_STEP
    c2 = 1.0 - ADAM_B2 ** ADAM_STEP

    def body(*refs):
        w_refs, g_refs, m_refs, v_refs = (refs[k * n:(k + 1) * n] for k in range(4))
        d_refs, nm_refs, nv_refs = (refs[(4 + k) * n:(5 + k) * n] for k in range(3))
        for i in range(n):
            gv = g_refs[i][...]
            nm = ADAM_B1 * m_refs[i][...] + (1.0 - ADAM_B1) * gv
            nv = ADAM_B2 * v_refs[i][...] + (1.0 - ADAM_B2) * (gv * gv)
            d_refs[i][...] = -ADAM_LR * ((nm / c1) / (jnp.sqrt(nv / c2) + ADAM_EPS) + ADAM_WD * w_refs[i][...])
            nm_refs[i][...] = nm
            nv_refs[i][...] = nv

    vmem = pl.BlockSpec(memory_space=pltpu.VMEM)
    shapes = [jax.ShapeDtypeStruct(t.shape, F32) for t in ws]
    outs = pl.pallas_call(body, in_specs=[vmem] * (4 * n), out_specs=[vmem] * (3 * n), out_shape=shapes * 3,
                          name=name, compiler_params=_params(vmem_mb=56))(*ws, *gs, *ms, *vs)
    return outs[:n], outs[n:2 * n], outs[2 * n:]


BIG = ("w_in", "w_glu", "w_out", "w_up", "w_down")
WEIGHTS = ("norm_mix_g", "w_in", "a_re", "a_im", "log_step", "b_re", "b_im", "c_re", "c_im", "d_skip", "w_glu",
           "sink", "norm_attn_g", "norm_ssm_g", "w_out", "norm_ffn_g", "w_up", "conv_w", "conv_b", "w_down",
           "norm_final_g")
SMALL = ("norm_mix_g", "a_re", "a_im", "log_step", "b_re", "b_im", "c_re", "c_im", "d_skip", "sink",
         "norm_attn_g", "norm_ssm_g", "norm_ffn_g", "conv_w", "conv_b", "norm_final_g")
SMALL_ROWS = 48
N_DEV = 8


def _tile_rows(size):
    return -(-size // (SUBLANES * D_MODEL)) * SUBLANES


def _by_owner(name, g):
    if name == "w_up":
        return g
    return g.reshape(4, g.shape[0] // 4, g.shape[1])


def _view(name, t):
    if name == "w_in":
        return jnp.swapaxes(t[0], 0, 1)
    if name in ("b_re", "b_im"):
        return jnp.swapaxes(t, -1, -2)
    return t


def _unview(name, t):
    if name == "w_in":
        return jnp.swapaxes(t, 0, 1)[None]
    if name in ("b_re", "b_im"):
        return jnp.swapaxes(t, -1, -2)
    return t


def kernel(x, norm_mix_g, w_in, a_re, a_im, log_step, b_re, b_im, c_re, c_im, d_skip, w_glu, sink, norm_attn_g, norm_ssm_g, w_out, norm_ffn_g, w_up, conv_w, conv_b, w_down, norm_final_g, loss_target, m_norm_mix_g, m_w_in, m_a_re, m_a_im, m_log_step, m_b_re, m_b_im, m_c_re, m_c_im, m_d_skip, m_w_glu, m_sink, m_norm_attn_g, m_norm_ssm_g, m_w_out, m_norm_ffn_g, m_w_up, m_conv_w, m_conv_b, m_w_down, m_norm_final_g, v_norm_mix_g, v_w_in, v_a_re, v_a_im, v_log_step, v_b_re, v_b_im, v_c_re, v_c_im, v_d_skip, v_w_glu, v_sink, v_norm_attn_g, v_norm_ssm_g, v_w_out, v_norm_ffn_g, v_w_up, v_conv_w, v_conv_b, v_w_down, v_norm_final_g):
    given = dict(locals())
    w = {n: given[n] for n in WEIGHTS}
    m = {n: given["m_" + n] for n in WEIGHTS}
    v = {n: given["v_" + n] for n in WEIGHTS}
    xy = 2 * lax.axis_index("x") + lax.axis_index("y")

    core = lax.axis_index("c")
    place = jnp.stack([core, xy]).astype(jnp.int32)

    conv_rows = jnp.pad(w["conv_w"][0], ((0, 2 * SUBLANES - 3), (0, 0)))
    rows = lambda t: t.reshape(4 * t.shape[1], t.shape[2])
    (w_in_all,) = _gather_weights([_view("w_in", w["w_in"])], [BF16])
    wb = {"w_in": rows(w_in_all)}
    mixer = [_cast_place(w[n][0], place, BF16, w_in_all, "cast_" + n) for n in ("w_glu", "w_out")]
    mixer.append(_cast_place(conv_rows, place, F32, w_in_all, "cast_conv_w"))
    *mixer_flight, mixer_token = _spread_start(mixer, "spread_mixer_start")
    late = ("w_up", "w_down")
    *late_flight, token = _spread_start(
        [_cast_place(w[n][0], place, BF16, mixer_token, "cast_" + n) for n in late], "spread_ffn_start")

    def mixer_weights(after):
        w_glu4, w_out4, conv4 = _spread_wait(*mixer_flight, after, "spread_mixer_wait")
        return {"w_glu": rows(w_glu4), "w_out": rows(w_out4),
                "conv_w": conv4[:, :3].transpose(1, 0, 2).reshape(3, 2 * D_FF)}

    def late_weights(after):
        w_up4, w_down4 = _spread_wait(*late_flight, after, "spread_ffn_wait")
        return {"w_up": w_up4, "w_down": rows(w_down4)}

    sp = {n: w[n][0] for n in ("a_re", "a_im", "log_step", "b_re", "b_im", "c_re", "c_im", "d_skip",
                               "norm_mix_g", "norm_attn_g", "norm_ssm_g", "norm_ffn_g", "sink", "conv_b")}
    for n in ("norm_mix_g", "norm_attn_g", "norm_ssm_g", "norm_ffn_g", "sink", "conv_b"):
        sp[n] = sp[n].reshape(1, -1)
    sp["norm_mix_g"] = sp["norm_mix_g"] + token[:1, :1]
    sp["norm_final_g"] = w["norm_final_g"]
    early, tail = late + ("w_out",), ("w_in", "w_glu")
    flight = {}

    def grads_ready(dw_up, dw_down, dw_out):
        *flight["pair"], token = _pair_start([dw_up, _by_owner("w_down", dw_down), _by_owner("w_out", dw_out)])
        return token[:1, :1]

    def grads_next(after):
        mine, got = _pair_wait(*flight["pair"], after)
        sums, flight["own"] = zip(*[_pair_sum(a, b, place, BF16, "pair_sum_" + n) for n, a, b in zip(early, mine, got)])
        *flight["chip"], token = _chip_start(list(sums))
        return token[:1, :1]

    loss, grad_x, g = _local_step(x[0], loss_target[0], wb, sp, mixer_weights, late_weights, grads_ready,
                                  grads_next)

    def as_rows(t):
        rows = _tile_rows(t.size)
        return jnp.pad(t.reshape(-1), (0, rows * D_MODEL - t.size)).reshape(rows, D_MODEL)

    pieces = [as_rows(g[n]) for n in SMALL] + [as_rows(loss)]
    spare = N_DEV * SMALL_ROWS - sum(p.shape[0] for p in pieces)
    small = jnp.concatenate(pieces + [jnp.zeros((spare, D_MODEL), F32)]).reshape(4, 2 * SMALL_ROWS, D_MODEL)
    by_owner = [_by_owner(n, g[n]) for n in tail] + [small]
    got = _pair_exchange(by_owner)
    transit = [BF16] * len(tail) + [F32]
    chip_sums, own_sums = zip(*[_pair_sum(a, b, place, t, "pair_sum_" + n)
                                for n, a, b, t in zip(tail + ("small",), by_owner, got, transit)])
    landed = _chip_exchange(list(chip_sums))
    halves = {n: _chip_sum(o, t, "chip_sum_" + n) for n, o, t in zip(tail + ("small",), own_sums, landed)}
    early_landed = _chip_wait(*flight["chip"], grad_x)
    for n, o, t in zip(early, flight["own"], early_landed):
        halves[n] = _chip_sum(o, t, "chip_sum_" + n)
    *others, small_all = _final_exchange([halves[n] for n in BIG], halves["small"])
    small_all = small_all.reshape(N_DEV * SMALL_ROWS, D_MODEL)
    grads, row = {}, 0
    for n in SMALL:
        shape = (3, 4 * w[n].shape[-1]) if n == "conv_w" else w[n].shape[1:] if n != "norm_final_g" else w[n].shape
        size = math.prod(shape)
        grads[n] = small_all[row:row + _tile_rows(size)].reshape(-1)[:size].reshape(shape)
        row += _tile_rows(size)
    loss = small_all[row, 0]
    cw = w["conv_w"].shape[-1]
    grads["conv_w"] = lax.dynamic_slice_in_dim(grads["conv_w"], xy * cw, cw, axis=1)
    grads = {n: _view(n, grads[n].reshape(w[n].shape)) for n in SMALL}
    wv, mv, vv = ({n: _view(n, t[n]) for n in WEIGHTS} for t in (w, m, v))

    delta, new_m, new_v = {}, {}, {}
    for n, other in zip(BIG, others):
        two_d = lambda t: t.reshape(t.shape[-2:])
        grads[n], delta[n], new_m[n], new_v[n] = _adamw_halves(
            two_d(wv[n]), halves[n], other, two_d(mv[n]), two_d(vv[n]), place, "adamw_" + n)
    for group, name in ((("b_re", "b_im"), "adamw_b"), (tuple(n for n in SMALL if n not in ("b_re", "b_im")), "adamw_small")):
        row = lambda t: t.reshape(1, -1) if t.ndim == 1 else t
        d_, m_, v_ = _adamw_many(*[[row(t[n]) for n in group] for t in (wv, grads, mv, vv)], name)
        for n, dn, mn, vn in zip(group, d_, m_, v_):
            delta[n], new_m[n], new_v[n] = (t.reshape(wv[n].shape) for t in (dn, mn, vn))
    natural = lambda t: [_unview(n, t[n].reshape(wv[n].shape)) for n in WEIGHTS]
    return (loss, grad_x[None], *natural(grads), *natural(delta), *natural(new_m), *natural(new_v))
```

```python
import functools
import math

import jax
import jax.numpy as jnp
import numpy as np
from jax import lax
from jax.experimental import pallas as pl
from jax.experimental.pallas import tpu as pltpu

F32 = jnp.float32
BF16 = jnp.bfloat16

D_MODEL = 1024
N_Q_HEADS = 8
N_KV_HEADS = 2
HEAD_DIM = 64
ATTN_WIDTH = 512
KV_WIDTH = 128
QKV_WIDTH = ATTN_WIDTH + 2 * KV_WIDTH
WINDOW = 128
BLOCK = 128
ROPE_DIM = 16
ROPE_THETA = 500000.0
SCORE_SCALE = HEAD_DIM ** -0.5
SSM_WIDTH = 512
SSM_GROUP = 16
N_SSM_GROUPS = 32
SSM_STATE = 64
IN_WIDTH = 1280
D_FF = 2816
EPS = 1e-6
ADAM_LR = 0.001
ADAM_B1 = 0.9
ADAM_B2 = 0.999
ADAM_EPS = 1e-08
ADAM_WD = 0.01
ADAM_STEP = 10

VMEM_BYTES_V7X = 64 * 1024 * 1024
SUBLANES = 8
LANES = 128
SSM_CB = 4
SSM_CH = 128
SSM_ST = 512
N_SEG = SUBLANES

NN = (((1,), (0,)), ((), ()))
NT = (((1,), (1,)), ((), ()))
TN = (((0,), (0,)), ((), ()))


def _params(sem=None, vmem_mb=48):
    limit = vmem_mb * 1024 * 1024
    assert limit < VMEM_BYTES_V7X
    return pltpu.CompilerParams(dimension_semantics=sem, vmem_limit_bytes=limit)


def _dg(a, b, dims):
    return lax.dot_general(a, b, dims, preferred_element_type=F32)


def _sigmoid(x):
    return 1.0 / (1.0 + jnp.exp(-x))


_SQRT_HALF = 0.7071067811865476
_INV_SQRT_2PI = 0.3989422804014327


def _gelu(x):
    return 0.5 * x * (1.0 + lax.erf(x * _SQRT_HALF))


def _gelu_grad(x):
    return 0.5 * (1.0 + lax.erf(x * _SQRT_HALF)) + x * (_INV_SQRT_2PI * jnp.exp(-0.5 * x * x))


def _mm_tn(a, b, tm, tn, name):
    k, m = a.shape
    n = b.shape[1]

    def body(a_ref, b_ref, o_ref):
        o_ref[...] = _dg(a_ref[...], b_ref[...], TN)

    return pl.pallas_call(
        body, grid=(m // tm, n // tn),
        in_specs=[pl.BlockSpec((k, tm), lambda i, j: (0, i)), pl.BlockSpec((k, tn), lambda i, j: (0, j))],
        out_specs=pl.BlockSpec((tm, tn), lambda i, j: (i, j)),
        out_shape=jax.ShapeDtypeStruct((m, n), F32), name=name,
        compiler_params=_params(("parallel", "parallel")),
    )(a, b)


def _mm_nn_cols(a, b4, tm, name):
    m, k = a.shape
    s, _, n = b4.shape

    def body(a_ref, b_ref, o_ref):
        o_ref[...] = _dg(a_ref[...], b_ref[...], NN)

    return pl.pallas_call(
        body, grid=(m // tm, s),
        in_specs=[pl.BlockSpec((tm, k), lambda i, j: (i, 0)), pl.BlockSpec((None, k, n), lambda i, j: (j, 0, 0))],
        out_specs=pl.BlockSpec((tm, n), lambda i, j: (i, j)),
        out_shape=jax.ShapeDtypeStruct((m, s * n), F32), name=name,
        compiler_params=_params(("parallel", "parallel")),
    )(a, b4)


def _mm_tn_cols(a, b2, s, tm, name):
    k, m = a.shape
    h, _, wide = b2.shape
    per = s // h
    n = wide // per

    def body(a_ref, b_ref, o_ref):
        o_ref[...] = _dg(a_ref[...], b_ref[...], TN)

    return pl.pallas_call(
        body, grid=(s, m // tm),
        in_specs=[pl.BlockSpec((k, tm), lambda j, i: (0, i)),
                  pl.BlockSpec((None, k, n), lambda j, i: (j // per, 0, j % per))],
        out_specs=pl.BlockSpec((None, tm, n), lambda j, i: (j, i, 0)),
        out_shape=jax.ShapeDtypeStruct((s, m, n), F32), name=name,
        compiler_params=_params(("parallel", "parallel")),
    )(a, b2)


TM_EW = 256


def _rms_bwd_vals(xv, gv, dy):
    r = lax.rsqrt(jnp.mean(xv * xv, axis=-1, keepdims=True) + EPS)
    xh = xv * r
    dxh = dy * gv
    dx = r * (dxh - xh * jnp.mean(dxh * xh, axis=-1, keepdims=True))
    return dx, dy * xh


TM_FUSED = 512
TM_LOSS = 256


def _rms_vals(xv, gv):
    return xv * lax.rsqrt(jnp.mean(xv * xv, axis=-1, keepdims=True) + EPS) * gv


def _rope_blocks(src, dst, c, lo, hi):
    nq = ATTN_WIDTH // LANES
    for blk in range(nq + 1):
        t = src[:, blk * LANES:(blk + 1) * LANES]
        rot = t * c + pltpu.roll(t, LANES - 8, 1) * lo + pltpu.roll(t, 8, 1) * hi
        dst[:, blk * LANES:(blk + 1) * LANES] = (rot * SCORE_SCALE if blk < nq else rot).astype(BF16)
    dst[:, (nq + 1) * LANES:] = src[:, (nq + 1) * LANES:].astype(BF16)


def _rms_mm_rope(x, g, wt, tabs, name):
    l, d = x.shape
    n = wt.shape[0]

    def body(x_ref, g_ref, w_ref, c_ref, lo_ref, hi_ref, h_ref, qkv_ref, u_ref):
        h = _rms_vals(x_ref[...], g_ref[...]).astype(BF16)
        h_ref[...] = h
        out = _dg(h, w_ref[...], NT)
        _rope_blocks(out[:, :QKV_WIDTH], qkv_ref, c_ref[...], lo_ref[...], hi_ref[...])
        u_ref[...] = out[:, QKV_WIDTH:]

    row = lambda width: pl.BlockSpec((TM_FUSED, width), lambda i: (i, 0))
    return pl.pallas_call(
        body, grid=(l // TM_FUSED,),
        in_specs=[row(d), pl.BlockSpec((1, d), lambda i: (0, 0)), pl.BlockSpec((n, d), lambda i: (0, 0)),
                  row(LANES), row(LANES), row(LANES)],
        out_specs=[row(d), row(QKV_WIDTH), row(n - QKV_WIDTH)],
        out_shape=[jax.ShapeDtypeStruct((l, d), BF16), jax.ShapeDtypeStruct((l, QKV_WIDTH), BF16),
                   jax.ShapeDtypeStruct((l, n - QKV_WIDTH), F32)],
        name=name, compiler_params=_params(("parallel",)),
    )(x, g, wt, *tabs)


def _mix_mm_res_rms(attn, ys, g_attn, g_ssm, b, res, g, name):
    l, w = attn.shape
    d = b.shape[1]

    def body(a_ref, y_ref, ga_ref, gs_ref, b_ref, r_ref, g_ref, m_ref, x_ref, h_ref):
        m_ref[:, :w] = _rms_vals(a_ref[...], ga_ref[...]).astype(BF16)
        m_ref[:, w:] = _rms_vals(y_ref[...], gs_ref[...]).astype(BF16)
        xv = r_ref[...] + _dg(m_ref[...], b_ref[...], NN)
        x_ref[...] = xv
        h_ref[...] = _rms_vals(xv, g_ref[...]).astype(BF16)

    row = lambda width: pl.BlockSpec((TM_FUSED, width), lambda i: (i, 0))
    vec = lambda width: pl.BlockSpec((1, width), lambda i: (0, 0))
    return pl.pallas_call(
        body, grid=(l // TM_FUSED,),
        in_specs=[row(w), row(w), vec(w), vec(w), pl.BlockSpec((2 * w, d), lambda i: (0, 0)), row(d), vec(d)],
        out_specs=[row(2 * w), row(d), row(d)],
        out_shape=[jax.ShapeDtypeStruct((l, 2 * w), BF16), jax.ShapeDtypeStruct((l, d), F32),
                   jax.ShapeDtypeStruct((l, d), BF16)],
        name=name, compiler_params=_params(("parallel",)),
    )(attn, ys, g_attn, g_ssm, b, res, g)


def _mm_res_loss(a, b, res, g, target):
    l, k = a.shape
    d = b.shape[1]

    def body(a_ref, b_ref, r_ref, g_ref, t_ref, loss_ref, dx_ref, dxb_ref, dg_ref):
        xv = r_ref[...] + _dg(a_ref[...], b_ref[...], NN)
        gv = g_ref[...]
        r = lax.rsqrt(jnp.mean(xv * xv, axis=-1, keepdims=True) + EPS)
        xh = xv * r
        e = xh * gv - t_ref[...]
        part = jnp.sum(jnp.sum(e * e, axis=1, keepdims=True), axis=0, keepdims=True) * (0.5 / d)
        dy = e * (1.0 / d)
        dxh = dy * gv
        dx = r * (dxh - xh * jnp.mean(dxh * xh, axis=-1, keepdims=True))
        dx_ref[...] = dx
        dxb_ref[...] = dx.astype(BF16)

        @pl.when(pl.program_id(0) == 0)
        def _():
            dg_ref[...] = jnp.zeros_like(dg_ref)
            loss_ref[...] = jnp.zeros_like(loss_ref)

        dg_ref[...] += jnp.sum(dy * xh, axis=0, keepdims=True)
        loss_ref[...] += part

    row = lambda width: pl.BlockSpec((TM_LOSS, width), lambda i: (i, 0))
    vec = pl.BlockSpec((1, d), lambda i: (0, 0))
    return pl.pallas_call(
        body, grid=(l // TM_LOSS,),
        in_specs=[row(k), pl.BlockSpec((k, d), lambda i: (0, 0)), row(d), vec, row(d)],
        out_specs=[pl.BlockSpec((1, 1), lambda i: (0, 0)), row(d), row(d), vec],
        out_shape=[jax.ShapeDtypeStruct((1, 1), F32), jax.ShapeDtypeStruct((l, d), F32),
                   jax.ShapeDtypeStruct((l, d), BF16), jax.ShapeDtypeStruct((1, d), F32)],
        name="mm_down_loss", compiler_params=_params(("arbitrary",)),
    )(a, b, res, g, target)


def _mm_rms_bwd(a, b, a_spec, b_spec, matmul, x, g, res, name):
    l, d = x.shape

    def body(a_ref, b_ref, x_ref, g_ref, res_ref, dx_ref, dxb_ref, dg_ref):
        dx, dgr = _rms_bwd_vals(x_ref[...], g_ref[...], matmul(a_ref, b_ref))
        dx = dx + res_ref[...]
        dx_ref[...] = dx
        dxb_ref[...] = dx.astype(BF16)

        @pl.when(pl.program_id(0) == 0)
        def _():
            dg_ref[...] = jnp.zeros_like(dg_ref)

        dg_ref[...] += jnp.sum(dgr, axis=0, keepdims=True)

    row = pl.BlockSpec((TM_FUSED, d), lambda i: (i, 0))
    vec = pl.BlockSpec((1, d), lambda i: (0, 0))
    return pl.pallas_call(
        body, grid=(l // TM_FUSED,), in_specs=[a_spec, b_spec, row, vec, row], out_specs=[row, row, vec],
        out_shape=[jax.ShapeDtypeStruct((l, d), F32), jax.ShapeDtypeStruct((l, d), BF16),
                   jax.ShapeDtypeStruct((1, d), F32)],
        name=name, compiler_params=_params(("arbitrary",)),
    )(a, b, x, g, res)


def _mm_nn_rms_bwd(a, b, x, g, res, name):
    return _mm_rms_bwd(a, b, pl.BlockSpec((TM_FUSED, a.shape[1]), lambda i: (i, 0)),
                       pl.BlockSpec(b.shape, lambda i: (0, 0)),
                       lambda a_ref, b_ref: _dg(a_ref[...], b_ref[...], NN), x, g, res, name)


def _mm_cols_rms_bwd(a2, b4, x, g, res, name):
    h, _, wide = a2.shape
    s, _, n = b4.shape
    per = s // h

    def matmul(a_ref, b_ref):
        acc = None
        for j in range(s):
            part = _dg(a_ref[j // per, :, (j % per) * n:(j % per + 1) * n], b_ref[j], NT)
            acc = part if acc is None else acc + part
        return acc

    return _mm_rms_bwd(a2, b4, pl.BlockSpec((h, TM_FUSED, wide), lambda i: (0, i, 0)),
                       pl.BlockSpec(b4.shape, lambda i: (0, 0, 0), pipeline_mode=pl.Buffered(1)),
                       matmul, x, g, res, name)


def _mm_mix_bwd(dx, b, attn, ys, g_attn, g_ssm, name):
    l, w = attn.shape
    d = dx.shape[1]

    def body(dx_ref, b_ref, a_ref, y_ref, ga_ref, gs_ref, da_ref, dy_ref, dga_ref, dgs_ref):
        @pl.when(pl.program_id(0) == 0)
        def _():
            dga_ref[...] = jnp.zeros_like(dga_ref)
            dgs_ref[...] = jnp.zeros_like(dgs_ref)

        dm = _dg(dx_ref[...], b_ref[...], NT)
        for src, gr, off, dst, dgr in ((a_ref, ga_ref, 0, da_ref, dga_ref), (y_ref, gs_ref, w, dy_ref, dgs_ref)):
            dxv, dg_rows = _rms_bwd_vals(src[...], gr[...], dm[:, off:off + w])
            dst[...] = dxv
            dgr[...] += jnp.sum(dg_rows, axis=0, keepdims=True)

    row = lambda width: pl.BlockSpec((TM_FUSED, width), lambda i: (i, 0))
    vec = pl.BlockSpec((1, w), lambda i: (0, 0))
    return pl.pallas_call(
        body, grid=(l // TM_FUSED,),
        in_specs=[row(d), pl.BlockSpec((2 * w, d), lambda i: (0, 0)), row(w), row(w), vec, vec],
        out_specs=[row(w), row(w), vec, vec],
        out_shape=[jax.ShapeDtypeStruct((l, w), F32), jax.ShapeDtypeStruct((l, w), F32),
                   jax.ShapeDtypeStruct((1, w), F32), jax.ShapeDtypeStruct((1, w), F32)],
        name=name, compiler_params=_params(("arbitrary",)),
    )(dx, b, attn, ys, g_attn, g_ssm)


def _rope_tables(l):
    half = ROPE_DIM // 2
    f32 = np.float32
    inv_freq = np.power(f32(ROPE_THETA), -np.arange(half, dtype=f32) / f32(half))
    ang = np.arange(l, dtype=f32)[:, None] * inv_freq[None, :]
    cos, sin = np.cos(ang), np.sin(ang)
    ones = np.ones((l, HEAD_DIM - ROPE_DIM), f32)
    zeros = np.zeros((l, HEAD_DIM - ROPE_DIM), f32)
    zh = np.zeros((l, half), f32)
    c = np.concatenate([cos, cos, ones], axis=1)
    s_lo = np.concatenate([-sin, zh, zeros], axis=1)
    s_hi = np.concatenate([zh, sin, zeros], axis=1)
    return tuple(jnp.asarray(np.tile(t, (1, LANES // HEAD_DIM)), F32) for t in (c, s_lo, s_hi))


def _rope_bwd(dq, dkv, du_ssm, dpre, d_skip, tabs):
    l = dq.shape[0]
    nq = ATTN_WIDTH // LANES

    def body(dq_ref, dkv_ref, du_ref, dpre_ref, ds_ref, c_ref, lo_ref, hi_ref, o_ref):
        c, lo, hi = c_ref[...], lo_ref[...], hi_ref[...]
        for blk in range(nq + 1):
            t = dq_ref[:, blk * LANES:(blk + 1) * LANES] if blk < nq else dkv_ref[:, :KV_WIDTH]
            g = t * c + pltpu.roll(t * lo, 8, 1) + pltpu.roll(t * hi, LANES - 8, 1)
            o_ref[:, blk * LANES:(blk + 1) * LANES] = g.astype(BF16)
        o_ref[:, (nq + 1) * LANES:QKV_WIDTH] = dkv_ref[:, KV_WIDTH:].astype(BF16)
        o_ref[:, QKV_WIDTH:] = (du_ref[...] + dpre_ref[...] * ds_ref[...]).astype(BF16)

    tab = pl.BlockSpec((TM_EW, LANES), lambda i: (i, 0))
    wide = pl.BlockSpec((TM_EW, SSM_WIDTH), lambda i: (i, 0))
    return pl.pallas_call(
        body, grid=(l // TM_EW,),
        in_specs=[wide, pl.BlockSpec((TM_EW, 2 * KV_WIDTH), lambda i: (i, 0)), wide, wide,
                  pl.BlockSpec((1, SSM_WIDTH), lambda i: (0, 0)), tab, tab, tab],
        out_specs=pl.BlockSpec((TM_EW, IN_WIDTH), lambda i: (i, 0)),
        out_shape=jax.ShapeDtypeStruct((l, IN_WIDTH), BF16), name="rope_bwd",
        compiler_params=_params(("parallel",)),
    )(dq, dkv, du_ssm, dpre, d_skip, *tabs)


_Q_COLS = ATTN_WIDTH // LANES
_NEG = -1e30


def _window_specs(nb, width, col):
    return [
        pl.BlockSpec((BLOCK, width), lambda n: (jnp.maximum(n - 1, 0), col)),
        pl.BlockSpec((BLOCK, width), lambda n: (n, col)),
        pl.BlockSpec((BLOCK, width), lambda n: (jnp.minimum(n + 1, nb - 1), col)),
    ]


def _stacked_sink(sink_ref, heads):
    rid = lax.broadcasted_iota(jnp.int32, (len(heads) * BLOCK, 1), 0)
    sk = jnp.full(rid.shape, sink_ref[0, heads[-1]], F32)
    for g in range(len(heads) - 2, -1, -1):
        sk = jnp.where(rid < (g + 1) * BLOCK, sink_ref[0, heads[g]], sk)
    return sk


def _attn_fwd(qkv, sink):
    l = qkv.shape[0]
    nb = l // BLOCK
    grp = N_Q_HEADS // N_KV_HEADS

    def body(sink_ref, q_ref, k0, k1, k2, v0, v1, v2, o_ref, lse_ref):
        n = pl.program_id(0)
        q = q_ref[...]
        kw = jnp.concatenate([k0[...], k1[...], k2[...]], axis=0)
        vw = jnp.concatenate([v0[...], v1[...], v2[...]], axis=0)
        row = lax.broadcasted_iota(jnp.int32, (grp * BLOCK, 3 * BLOCK), 0)
        col = lax.broadcasted_iota(jnp.int32, (grp * BLOCK, 3 * BLOCK), 1)
        valid = jnp.abs(col - BLOCK - (row & (BLOCK - 1))) <= WINDOW
        valid &= jnp.logical_not((n == 0) & (col < BLOCK))
        valid &= jnp.logical_not((n == nb - 1) & (col >= 2 * BLOCK))
        for hk in range(N_KV_HEADS):
            heads = range(hk * grp, (hk + 1) * grp)
            qs = jnp.concatenate([q[:, h * HEAD_DIM:(h + 1) * HEAD_DIM] for h in heads], axis=0)
            kh = kw[:, hk * HEAD_DIM:(hk + 1) * HEAD_DIM]
            vh = vw[:, hk * HEAD_DIM:(hk + 1) * HEAD_DIM]
            s = jnp.where(valid, _dg(qs, kh, NT), _NEG)
            sk = _stacked_sink(sink_ref, heads)
            m = jnp.maximum(jnp.max(s, axis=1, keepdims=True), sk)
            p = jnp.exp(s - m)
            denom = jnp.sum(p, axis=1, keepdims=True) + jnp.exp(sk - m)
            o = _dg((p / denom).astype(BF16), vh, NN)
            lse = m + jnp.log(denom)
            for g, h in enumerate(heads):
                o_ref[:, h * HEAD_DIM:(h + 1) * HEAD_DIM] = o[g * BLOCK:(g + 1) * BLOCK]
                lse_ref[:, h:h + 1] = lse[g * BLOCK:(g + 1) * BLOCK]

    return pl.pallas_call(
        body, grid=(nb,),
        in_specs=[pl.BlockSpec(memory_space=pltpu.SMEM),
                  pl.BlockSpec((BLOCK, ATTN_WIDTH), lambda n: (n, 0))]
        + _window_specs(nb, KV_WIDTH, _Q_COLS) + _window_specs(nb, KV_WIDTH, _Q_COLS + 1),
        out_specs=[pl.BlockSpec((BLOCK, ATTN_WIDTH), lambda n: (n, 0)),
                   pl.BlockSpec((BLOCK, N_Q_HEADS), lambda n: (n, 0))],
        out_shape=[jax.ShapeDtypeStruct((l, ATTN_WIDTH), F32), jax.ShapeDtypeStruct((l, N_Q_HEADS), F32)],
        name="attn_fwd", compiler_params=_params(("parallel",)),
    )(sink, qkv, qkv, qkv, qkv, qkv, qkv, qkv)


def _attn_bwd(qkv, attn, dattn, lse, sink):
    l = qkv.shape[0]
    nb = l // BLOCK
    grp = N_Q_HEADS // N_KV_HEADS
    win = 3 * BLOCK

    def body(sink_ref, q_ref, k0, k1, k2, v0, v1, v2, o_ref, d_ref, l_ref, dq_ref, dkv_ref, dsink_ref, ring_ref):
        n = pl.program_id(0)

        @pl.when(n == 0)
        def _():
            dsink_ref[...] = jnp.zeros_like(dsink_ref)
            ring_ref[...] = jnp.zeros_like(ring_ref)

        @pl.when(n < nb)
        def _():
            first, last = n == 0, n == nb - 1
            cat = lambda a, b, c: jnp.concatenate([a[...], b[...], c[...]], axis=0)
            q, kw, vw = q_ref[...], cat(k0, k1, k2), cat(v0, v1, v2)
            dov = d_ref[...]
            prod = o_ref[...] * dov
            dob = dov.astype(BF16)
            lse = l_ref[...]
            row = lax.broadcasted_iota(jnp.int32, (grp * BLOCK, win), 0)
            col = lax.broadcasted_iota(jnp.int32, (grp * BLOCK, win), 1)
            valid = jnp.abs(col - BLOCK - (row & (BLOCK - 1))) <= WINDOW
            valid &= jnp.logical_not(first & (col < BLOCK))
            valid &= jnp.logical_not(last & (col >= 2 * BLOCK))

            dsink_parts, dks, dvs = [], [], []
            for hk in range(N_KV_HEADS):
                heads = range(hk * grp, (hk + 1) * grp)
                ksl = slice(hk * HEAD_DIM, (hk + 1) * HEAD_DIM)
                hsl = [slice(h * HEAD_DIM, (h + 1) * HEAD_DIM) for h in heads]
                stack = lambda parts: jnp.concatenate(parts, axis=0)
                qs = stack([q[:, s_] for s_ in hsl])
                dos = stack([dob[:, s_] for s_ in hsl])
                deltas = stack([jnp.sum(prod[:, s_], axis=1, keepdims=True) for s_ in hsl])
                lses = stack([lse[:, h:h + 1] for h in heads])
                kh, vh = kw[:, ksl], vw[:, ksl]
                s = jnp.where(valid, _dg(qs, kh, NT), _NEG)
                p = jnp.exp(s - lses)
                dp = _dg(dos, vh, NT)
                ds = (p * (dp - deltas)).astype(BF16)
                dq = _dg(ds, kh, NN) * SCORE_SCALE
                sink_rows = jnp.exp(_stacked_sink(sink_ref, heads) - lses) * deltas
                for g in range(grp):
                    dq_ref[:, hsl[g]] = dq[g * BLOCK:(g + 1) * BLOCK]
                    dsink_parts.append(jnp.sum(sink_rows[g * BLOCK:(g + 1) * BLOCK], axis=0, keepdims=True))
                dks.append(_dg(ds, qs, TN))
                dvs.append(_dg(p.astype(BF16), dos, TN))
            dsink_ref[...] -= jnp.concatenate(dsink_parts, axis=1)
            part = jnp.concatenate(dks + dvs, axis=1)
            ring_ref[(n + 2) % 3] += part[0:BLOCK]
            ring_ref[n % 3] += part[BLOCK:2 * BLOCK]
            ring_ref[(n + 1) % 3] = part[2 * BLOCK:]

        @pl.when(n >= 1)
        def _():
            dkv_ref[...] = ring_ref[(n + 2) % 3]

    centre = lambda n: jnp.minimum(n, nb - 1)
    window = lambda width, col: [
        pl.BlockSpec((BLOCK, width), lambda n: (jnp.maximum(centre(n) - 1, 0), col)),
        pl.BlockSpec((BLOCK, width), lambda n: (centre(n), col)),
        pl.BlockSpec((BLOCK, width), lambda n: (jnp.minimum(centre(n) + 1, nb - 1), col))]
    own = lambda width: pl.BlockSpec((BLOCK, width), lambda n: (centre(n), 0))
    return pl.pallas_call(
        body, grid=(nb + 1,),
        in_specs=[pl.BlockSpec(memory_space=pltpu.SMEM), own(ATTN_WIDTH)]
        + window(KV_WIDTH, _Q_COLS) + window(KV_WIDTH, _Q_COLS + 1)
        + [own(ATTN_WIDTH), own(ATTN_WIDTH), own(N_Q_HEADS)],
        out_specs=[own(ATTN_WIDTH), pl.BlockSpec((BLOCK, 2 * KV_WIDTH), lambda n: (jnp.maximum(n - 1, 0), 0)),
                   pl.BlockSpec((1, N_Q_HEADS), lambda n: (0, 0))],
        out_shape=[jax.ShapeDtypeStruct((l, ATTN_WIDTH), F32), jax.ShapeDtypeStruct((l, 2 * KV_WIDTH), F32),
                   jax.ShapeDtypeStruct((1, N_Q_HEADS), F32)],
        scratch_shapes=[pltpu.VMEM((3, BLOCK, 2 * KV_WIDTH), F32)],
        name="attn_bwd", compiler_params=_params(("arbitrary",)),
    )(sink, qkv, qkv, qkv, qkv, qkv, qkv, qkv, attn, dattn, lse)


def _ssm_disc(a_re, a_im, log_step, b_re, b_im):
    step = jnp.exp(log_step)[..., None]
    mag = jnp.exp(a_re * step)
    lb_re, lb_im = mag * jnp.cos(a_im * step), mag * jnp.sin(a_im * step)
    nr, ni = lb_re - 1.0, lb_im
    den = a_re * a_re + a_im * a_im
    f_re = ((nr * a_re + ni * a_im) / den)[..., None]
    f_im = ((ni * a_re - nr * a_im) / den)[..., None]
    return lb_re, lb_im, f_re * b_re - f_im * b_im, f_re * b_im + f_im * b_re


def _ssm_pack(lb_re, lb_im, bb_re, bb_im, c_re, c_im):
    eye = jnp.eye(SSM_CH // SSM_GROUP, dtype=F32)
    ng = SSM_CH // SSM_GROUP

    def diag_b(bb):
        t = bb.reshape(2, SSM_CB, ng, SSM_STATE, SSM_GROUP)
        return jnp.einsum('dkgpc,gh->dkgchp', t, eye).reshape(2, SSM_CB, SSM_CH, SSM_ST)

    def diag_c(cc):
        t = cc.reshape(2, SSM_CB, ng, SSM_GROUP, SSM_STATE)
        return jnp.einsum('dkgcp,gh->dkhpgc', t, eye).reshape(2, SSM_CB, SSM_ST, SSM_CH)

    bcat = jnp.concatenate([diag_b(bb_re), diag_b(bb_im)], axis=-1)
    ccat = jnp.concatenate([diag_c(c_re), -diag_c(c_im)], axis=-2)
    lam_re = lb_re.reshape(2, SSM_CB, 1, SSM_ST)
    lam_im = lb_im.reshape(2, SSM_CB, 1, SSM_ST)
    return bcat, ccat, lam_re, lam_im


def _ssm_unpack(dbcat, dccat, dlam_re, dlam_im):
    ng = SSM_CH // SSM_GROUP
    eye = jnp.eye(ng, dtype=F32)

    def undiag_b(t):
        t = t.reshape(2, SSM_CB, ng, SSM_GROUP, ng, SSM_STATE)
        return jnp.einsum('dkgchp,gh->dkgpc', t, eye).reshape(2, N_SSM_GROUPS, SSM_STATE, SSM_GROUP)

    def undiag_c(t):
        t = t.reshape(2, SSM_CB, ng, SSM_STATE, ng, SSM_GROUP)
        return jnp.einsum('dkhpgc,gh->dkgcp', t, eye).reshape(2, N_SSM_GROUPS, SSM_GROUP, SSM_STATE)

    dbb_re, dbb_im = undiag_b(dbcat[..., :SSM_ST]), undiag_b(dbcat[..., SSM_ST:])
    dc_re, dc_im = undiag_c(dccat[:, :, :SSM_ST]), -undiag_c(dccat[:, :, SSM_ST:])
    shape = (2, N_SSM_GROUPS, SSM_STATE)
    return dlam_re.reshape(shape), dlam_im.reshape(shape), dbb_re, dbb_im, dc_re, dc_im


def _to_segments(t):
    l, w = t.shape
    return t.reshape(N_SEG, l // N_SEG, w).transpose(1, 0, 2).reshape(l, w)


def _from_segments(t):
    l, w = t.shape
    return t.reshape(l // N_SEG, N_SEG, w).transpose(1, 0, 2).reshape(l, w)


SSM_RC = 256
SSM_JC = SSM_RC // N_SEG
_RE, _IM = pl.ds(0, SSM_ST), pl.ds(SSM_ST, SSM_ST)


def _cfma(ar, ai, xr, xi, br, bi):
    return ar * xr - ai * xi + br, ar * xi + ai * xr + bi


def _chunk_rows(ci, rev, nc):
    start = jnp.where(rev, (nc - 1 - ci) * SSM_RC, ci * SSM_RC)
    return pl.ds(pl.multiple_of(start, SSM_RC), SSM_RC)


def _scan_chunk(src, dst, ar, ai, rev, nj, ci, carry, prev_ref=None):
    def rows_of(staged, j, k):
        at = jnp.where(rev, SSM_JC - 1 - k, k) if staged else j
        return pl.ds(pl.multiple_of(at * N_SEG, N_SEG), N_SEG)

    for k in range(SSM_JC):
        jj = ci * SSM_JC + k
        j = jnp.where(rev, nj - 1 - jj, jj)
        rows = rows_of(src[1], j, k)
        nr, ni = _cfma(ar, ai, carry[0], carry[1], src[0][rows, _RE], src[0][rows, _IM])
        if dst is not None:
            rows = rows_of(dst[1], j, k)
            dst[0][rows, _RE] = nr
            dst[0][rows, _IM] = ni
        if prev_ref is None:
            carry = (nr, ni)
            continue
        jp = jnp.where(rev, j - 1, j + 1)
        if k == SSM_JC - 1:
            inside = jnp.where((jp >= 0) & (jp < nj), 1.0, 0.0)
            jp = jnp.clip(jp, 0, nj - 1)
        prow = pl.ds(pl.multiple_of(jp * N_SEG, N_SEG), N_SEG)
        xr, xi = prev_ref[prow, _RE], prev_ref[prow, _IM]
        sr, si = nr * xr + ni * xi, ni * xr - nr * xi
        if k == SSM_JC - 1:
            sr, si = inside * sr, inside * si
        carry = (nr, ni, carry[2] + sr, carry[3] + si)
    return carry


def _segment_inits(ar, ai, end_r, end_i, rev, nj):
    pr, pi = ar, ai
    for _ in range(int(math.log2(nj))):
        pr, pi = pr * pr - pi * pi, 2.0 * pr * pi
    seg = lax.broadcasted_iota(jnp.int32, end_r.shape, 0)
    zero = jnp.zeros_like(end_r)

    def chain(shift, keep):
        ir, ii = zero, zero
        for _ in range(N_SEG - 1):
            tr, ti = _cfma(pr, pi, ir, ii, end_r, end_i)
            ir = jnp.where(keep, pltpu.roll(tr, shift, 0), 0.0)
            ii = jnp.where(keep, pltpu.roll(ti, shift, 0), 0.0)
        return ir, ii

    up_r, up_i = chain(1, seg >= 1)
    dn_r, dn_i = chain(N_SEG - 1, seg <= N_SEG - 2)
    return jnp.where(rev, dn_r, up_r), jnp.where(rev, dn_i, up_i)


def _ssm_specs(l):
    act = pl.BlockSpec((l, SSM_CH), lambda k, d: (0, k))
    bmat = pl.BlockSpec((None, None, SSM_CH, 2 * SSM_ST), lambda k, d: (d, k, 0, 0))
    cmat = pl.BlockSpec((None, None, 2 * SSM_ST, SSM_CH), lambda k, d: (d, k, 0, 0))
    lam = pl.BlockSpec((None, None, 1, SSM_ST), lambda k, d: (d, k, 0, 0))
    return act, bmat, cmat, lam


def _ssm_fwd(u_seg, bcat, ccat, lam_re, lam_im):
    l = u_seg.shape[0]
    nj = l // N_SEG
    nc = l // SSM_RC

    def body(u_ref, b_ref, c_ref, lr_ref, li_ref, y_ref, ub_ref, keep_ref, xs_ref, stage0, stage1, keep_sem):
        k, d = pl.program_id(0), pl.program_id(1)
        rev = d == 1
        shape = (N_SEG, SSM_ST)
        ar, ai = jnp.broadcast_to(lr_ref[...], shape), jnp.broadcast_to(li_ref[...], shape)
        zero = jnp.zeros(shape, F32)

        def inputs(ci, stage):
            rows = _chunk_rows(ci, rev, nc)
            ub = u_ref[rows, :].astype(BF16)
            ub_ref[rows, :] = ub
            bu = _dg(ub, b_ref[...], NN)
            stage[...] = bu
            xs_ref[rows, :] = bu

        def first(stage, ci, carry):
            return _scan_chunk((stage, True), None, ar, ai, rev, nj, ci, carry)

        def first_pass(t, carry):
            inputs(2 * t + 1, stage1)
            carry = first(stage0, 2 * t, carry)
            inputs(2 * t + 2, stage0)
            return first(stage1, 2 * t + 1, carry)

        inputs(0, stage0)
        carry = lax.fori_loop(0, nc // 2 - 1, first_pass, (zero, zero))
        inputs(nc - 1, stage1)
        carry = first(stage0, nc - 2, carry)
        end_r, end_i = first(stage1, nc - 1, carry)
        init = _segment_inits(ar, ai, end_r, end_i, rev, nj)

        @pl.when(d == 0)
        def _():
            y_ref[...] = jnp.zeros_like(y_ref)

        def outputs(ci):
            rows = _chunk_rows(ci, rev, nc)
            y_ref[rows, :] += _dg(xs_ref[rows, :].astype(BF16), c_ref[...], NN)
            pltpu.make_async_copy(xs_ref.at[rows], keep_ref.at[d, k, rows], keep_sem).start()

        def second(ci, carry):
            return _scan_chunk((xs_ref, False), (xs_ref, False), ar, ai, rev, nj, ci, carry)

        def second_pass(ci, carry):
            outputs(ci - 1)
            return second(ci, carry)

        lax.fori_loop(1, nc, second_pass, second(0, init))
        outputs(nc - 1)
        pltpu.make_async_copy(xs_ref, keep_ref.at[d, k], keep_sem).wait()

    act, bmat, cmat, lam = _ssm_specs(l)
    return pl.pallas_call(
        body, grid=(SSM_CB, 2), in_specs=[act, bmat, cmat, lam, lam], out_specs=[act, act, ANY],
        out_shape=[jax.ShapeDtypeStruct((l, SSM_WIDTH), F32), jax.ShapeDtypeStruct((l, SSM_WIDTH), BF16),
                   jax.ShapeDtypeStruct((2, SSM_CB, l, 2 * SSM_ST), F32)],
        scratch_shapes=[pltpu.VMEM((l, 2 * SSM_ST), F32), pltpu.VMEM((SSM_RC, 2 * SSM_ST), F32),
                        pltpu.VMEM((SSM_RC, 2 * SSM_ST), F32), pltpu.SemaphoreType.DMA],
        name="ssm_fwd", compiler_params=_params(("parallel", "arbitrary"), vmem_mb=56),
    )(u_seg, bcat.astype(BF16), ccat.astype(BF16), lam_re, lam_im)


def _ssm_bwd(u_seg, dy_seg, states, bcat, ccat, lam_re, lam_im):
    l = u_seg.shape[0]
    nj = l // N_SEG
    nc = l // SSM_RC

    def body(u_ref, dy_ref, keep_ref, b_ref, c_ref, lr_ref, li_ref,
             du_ref, db_ref, dc_ref, dlr_ref, dli_ref, xs_ref, gs_ref, dyb_ref, stage0, stage1, keep_sem):
        k, d = pl.program_id(0), pl.program_id(1)
        rev = d == 1
        back = jnp.logical_not(rev)
        shape = (N_SEG, SSM_ST)
        ar, ai = jnp.broadcast_to(lr_ref[...], shape), -jnp.broadcast_to(li_ref[...], shape)
        zero = jnp.zeros(shape, F32)
        fetch = pltpu.make_async_copy(keep_ref.at[d, k], xs_ref, keep_sem)
        fetch.start()

        def inputs(ci, stage):
            rows = _chunk_rows(ci, back, nc)
            dyb = dy_ref[rows, :].astype(BF16)
            dyb_ref[rows, :] = dyb
            dx = _dg(dyb, c_ref[...], NT)
            stage[...] = dx
            gs_ref[rows, :] = dx

        def first(stage, ci, carry):
            return _scan_chunk((stage, True), None, ar, ai, back, nj, ci, carry)

        def first_pass(t, carry):
            inputs(2 * t + 1, stage1)
            carry = first(stage0, 2 * t, carry)
            inputs(2 * t + 2, stage0)
            return first(stage1, 2 * t + 1, carry)

        inputs(0, stage0)
        carry = lax.fori_loop(0, nc // 2 - 1, first_pass, (zero, zero))
        inputs(nc - 1, stage1)
        carry = first(stage0, nc - 2, carry)
        end_r, end_i = first(stage1, nc - 1, carry)
        init = _segment_inits(ar, ai, end_r, end_i, back, nj)
        fetch.wait()
        db_ref[...] = jnp.zeros_like(db_ref)
        dc_ref[...] = jnp.zeros_like(dc_ref)

        @pl.when(d == 0)
        def _():
            du_ref[...] = jnp.zeros_like(du_ref)

        def outputs(ci, stage):
            rows = _chunk_rows(ci, back, nc)
            g = stage[...].astype(BF16)
            dc_ref[...] += _dg(xs_ref[rows, :].astype(BF16), dyb_ref[rows, :], TN)
            db_ref[...] += _dg(u_ref[rows, :], g, TN)
            du_ref[rows, :] += _dg(g, b_ref[...], NT)

        def second(ci, stage, carry):
            return _scan_chunk((gs_ref, False), (stage, True), ar, ai, back, nj, ci, carry, prev_ref=xs_ref)

        def second_pass(t, carry):
            outputs(2 * t, stage0)
            carry = second(2 * t + 1, stage1, carry)
            outputs(2 * t + 1, stage1)
            return second(2 * t + 2, stage0, carry)

        carry = lax.fori_loop(0, nc // 2 - 1, second_pass, second(0, stage0, init + (zero, zero)))
        outputs(nc - 2, stage0)
        gr, gi, acc_r, acc_i = second(nc - 1, stage1, carry)
        outputs(nc - 1, stage1)

        seg = lax.broadcasted_iota(jnp.int32, shape, 0)
        jb = jnp.where(rev, nj - 1, 0)
        erow = pl.ds(pl.multiple_of((nj - 1 - jb) * N_SEG, N_SEG), N_SEG)

        def before(t):
            up = jnp.where(seg >= 1, pltpu.roll(t, 1, 0), 0.0)
            down = jnp.where(seg <= N_SEG - 2, pltpu.roll(t, N_SEG - 1, 0), 0.0)
            return jnp.where(rev, down, up)

        init_r, init_i = before(xs_ref[erow, _RE]), before(xs_ref[erow, _IM])
        acc_r = acc_r + gr * init_r + gi * init_i
        acc_i = acc_i + gi * init_r - gr * init_i
        dlr_ref[...] = jnp.sum(acc_r, axis=0, keepdims=True)
        dli_ref[...] = jnp.sum(acc_i, axis=0, keepdims=True)

    act, bmat, cmat, lam = _ssm_specs(l)
    return pl.pallas_call(
        body, grid=(SSM_CB, 2), in_specs=[act, act, ANY, bmat, cmat, lam, lam],
        out_specs=[act, bmat, cmat, lam, lam],
        out_shape=[jax.ShapeDtypeStruct((l, SSM_WIDTH), F32),
                   jax.ShapeDtypeStruct(bcat.shape, F32), jax.ShapeDtypeStruct(ccat.shape, F32),
                   jax.ShapeDtypeStruct(lam_re.shape, F32), jax.ShapeDtypeStruct(lam_im.shape, F32)],
        scratch_shapes=[pltpu.VMEM((l, 2 * SSM_ST), F32), pltpu.VMEM((l, 2 * SSM_ST), F32),
                        pltpu.VMEM((l, SSM_CH), BF16),
                        pltpu.VMEM((SSM_RC, 2 * SSM_ST), F32), pltpu.VMEM((SSM_RC, 2 * SSM_ST), F32),
                        pltpu.SemaphoreType.DMA],
        name="ssm_bwd", compiler_params=_params(("parallel", "arbitrary"), vmem_mb=58),
    )(u_seg, dy_seg, states, bcat.astype(BF16), ccat.astype(BF16), lam_re, lam_im)


def _glu_fwd(y_ssm, u, d_skip, w_glu):
    l, w = u.shape

    def body(y_ref, u_ref, d_ref, w_ref, pre_ref, s_ref, ys_ref):
        pre = y_ref[...] + d_ref[...] * u_ref[...]
        z = _gelu(pre)
        s = _dg(z.astype(BF16), w_ref[...], NN)
        pre_ref[...] = pre
        s_ref[...] = s
        ys_ref[...] = z * _sigmoid(s)

    row = pl.BlockSpec((TM_EW, w), lambda i: (i, 0))
    out = jax.ShapeDtypeStruct((l, w), F32)
    return pl.pallas_call(
        body, grid=(l // TM_EW,),
        in_specs=[row, row, pl.BlockSpec((1, w), lambda i: (0, 0)), pl.BlockSpec((w, w), lambda i: (0, 0))],
        out_specs=[row, row, row], out_shape=[out, out, out], name="glu_fwd",
        compiler_params=_params(("parallel",)),
    )(y_ssm, u, d_skip, w_glu)


def _glu_bwd(pre, s, dys, u, d_skip, w_glu):
    l, w = u.shape

    def body(pre_ref, s_ref, dys_ref, u_ref, d_ref, w_ref, dpre_ref, z_ref, ds_ref, dd_ref):
        pre, dys = pre_ref[...], dys_ref[...]
        z = _gelu(pre)
        sig = _sigmoid(s_ref[...])
        ds = (dys * z * sig * (1.0 - sig)).astype(BF16)
        dz = dys * sig + _dg(ds, w_ref[...], NT)
        dpre = dz * _gelu_grad(pre)
        dpre_ref[...] = dpre
        z_ref[...] = z.astype(BF16)
        ds_ref[...] = ds

        @pl.when(pl.program_id(0) == 0)
        def _():
            dd_ref[...] = jnp.zeros_like(dd_ref)

        dd_ref[...] += jnp.sum(dpre * u_ref[...], axis=0, keepdims=True)

    row = pl.BlockSpec((TM_EW, w), lambda i: (i, 0))
    vec = pl.BlockSpec((1, w), lambda i: (0, 0))
    return pl.pallas_call(
        body, grid=(l // TM_EW,),
        in_specs=[row, row, row, row, vec, pl.BlockSpec((w, w), lambda i: (0, 0))],
        out_specs=[row, row, row, vec],
        out_shape=[jax.ShapeDtypeStruct((l, w), F32), jax.ShapeDtypeStruct((l, w), BF16),
                   jax.ShapeDtypeStruct((l, w), BF16), jax.ShapeDtypeStruct((1, w), F32)],
        name="glu_bwd", compiler_params=_params(("arbitrary",)),
    )(pre, s, dys, u, d_skip, w_glu)


TM_CV = 512
TC_CV = 256
TM_CF = 256
TC_CF = D_FF // 2
HALO = SUBLANES


def _conv_specs(l, col0, tm=TM_CV, tc=TC_CV):
    per = tm // HALO
    nh = l // HALO
    off = col0 // tc
    return [
        pl.BlockSpec((HALO, tc), lambda j, i: (jnp.maximum(i * per - 1, 0), j + off)),
        pl.BlockSpec((tm, tc), lambda j, i: (i, j + off)),
        pl.BlockSpec((HALO, tc), lambda j, i: (jnp.minimum((i + 1) * per, nh - 1), j + off)),
    ]


def _ext(prev_ref, mid_ref, next_ref, first, last):
    p = jnp.where(first, 0.0, prev_ref[...])
    n = jnp.where(last, 0.0, next_ref[...])
    return jnp.concatenate([p, mid_ref[...], n], axis=0)


def _shift_dn(t):
    return pltpu.roll(t, 1, 0)


def _shift_up(t):
    return pltpu.roll(t, t.shape[0] - 1, 0)


def _conv3(e, w_ref, b_ref):
    return w_ref[0:1, :] * _shift_dn(e) + w_ref[1:2, :] * e + w_ref[2:3, :] * _shift_up(e) + b_ref[...]


def _convffn_fwd(up_pre, conv_w, conv_b):
    l = up_pre.shape[0]
    tm, tc = TM_CF, TC_CF
    ni = l // tm
    wspec = lambda off: pl.BlockSpec((3, tc), lambda j, i: (0, j + off))
    bspec = lambda off: pl.BlockSpec((1, tc), lambda j, i: (0, j + off))
    voff = D_FF // tc

    def body(gp, gm, gn, vp, vm, vn, wg, bg, wv, bv, o_ref):
        i = pl.program_id(1)
        first, last = i == 0, i == ni - 1
        gate = _conv3(_ext(gp, gm, gn, first, last), wg, bg)[HALO:HALO + tm]
        val = _conv3(_ext(vp, vm, vn, first, last), wv, bv)[HALO:HALO + tm]
        o_ref[...] = (gate * _sigmoid(gate) * val).astype(BF16)

    return pl.pallas_call(
        body, grid=(D_FF // tc, ni),
        in_specs=_conv_specs(l, 0, tm, tc) + _conv_specs(l, D_FF, tm, tc)
        + [wspec(0), bspec(0), wspec(voff), bspec(voff)],
        out_specs=pl.BlockSpec((tm, tc), lambda j, i: (i, j)),
        out_shape=jax.ShapeDtypeStruct((l, D_FF), BF16), name="convffn_fwd",
        compiler_params=_params(("parallel", "parallel")),
    )(up_pre, up_pre, up_pre, up_pre, up_pre, up_pre, conv_w, conv_b, conv_w, conv_b)


HALO_B = 2 * SUBLANES


def _convffn_bwd(up_pre, dx2b, w_down, conv_w, conv_b):
    l = up_pre.shape[0]
    ni = l // TM_CV
    d = dx2b.shape[1]
    wspec = lambda off: pl.BlockSpec((3, TC_CV), lambda i, j: (0, j + off))
    bspec = lambda off: pl.BlockSpec((1, TC_CV), lambda i, j: (0, j + off))
    voff = D_FF // TC_CV
    swap = lambda spec: pl.BlockSpec(spec.block_shape, lambda i, j, f=spec.index_map: f(j, i))
    per, nh = TM_CV // HALO_B, l // HALO_B
    dx_specs = [pl.BlockSpec((HALO_B, d), lambda i, j: (jnp.maximum(i * per - 1, 0), 0)),
                pl.BlockSpec((TM_CV, d), lambda i, j: (i, 0)),
                pl.BlockSpec((HALO_B, d), lambda i, j: (jnp.minimum((i + 1) * per, nh - 1), 0))]

    def body(gp, gm, gn, vp, vm, vn, xp, xm, xn, wd, wg, bg, wv, bv, dup_ref, pg_ref, pv_ref):
        i = pl.program_id(0)
        first, last = i == 0, i == ni - 1
        ge, ve = _ext(gp, gm, gn, first, last), _ext(vp, vm, vn, first, last)
        zero = jnp.zeros((HALO_B, d), BF16)
        dx = jnp.concatenate([jnp.where(first, zero, xp[...]), xm[...], jnp.where(last, zero, xn[...])], axis=0)
        de = _dg(dx, wd[...], NT)[HALO_B - HALO:HALO_B + TM_CV + HALO]
        taps = [(_shift_dn(e), e, _shift_up(e)) for e in (ge, ve)]
        conv = lambda t, w_ref, b_ref: w_ref[0:1, :] * t[0] + w_ref[1:2, :] * t[1] + w_ref[2:3, :] * t[2] + b_ref[...]
        gate, val = conv(taps[0], wg, bg), conv(taps[1], wv, bv)
        sig = _sigmoid(gate)
        silu = gate * sig
        dgate = de * val * (sig + silu * (1.0 - sig))
        dval = de * silu
        mid = slice(HALO, HALO + TM_CV)
        rid = lax.broadcasted_iota(jnp.int32, (SUBLANES, TC_CV), 0)
        for half, (dup, tap, w_ref, p_ref) in enumerate(((dgate, taps[0], wg, pg_ref), (dval, taps[1], wv, pv_ref))):
            dpre = w_ref[0:1, :] * _shift_up(dup) + w_ref[1:2, :] * dup + w_ref[2:3, :] * _shift_dn(dup)
            dup_ref[half] = dpre[mid].astype(BF16)
            dm_ = dup[mid]
            sums = [jnp.sum(dm_ * t[mid], axis=0, keepdims=True) for t in tap]
            sums.append(jnp.sum(dm_, axis=0, keepdims=True))
            acc = jnp.zeros((SUBLANES, TC_CV), F32)
            for k, sk in enumerate(sums):
                acc = jnp.where(rid == k, sk, acc)
            p_ref[...] = acc

    par = pl.BlockSpec((None, SUBLANES, TC_CV), lambda i, j: (i, 0, j))
    dup, pg, pv = pl.pallas_call(
        body, grid=(ni, D_FF // TC_CV),
        in_specs=[swap(s) for s in _conv_specs(l, 0) + _conv_specs(l, D_FF)] + dx_specs
        + [pl.BlockSpec((TC_CV, d), lambda i, j: (j, 0)), wspec(0), bspec(0), wspec(voff), bspec(voff)],
        out_specs=[pl.BlockSpec((2, TM_CV, TC_CV), lambda i, j: (0, i, j)), par, par],
        out_shape=[jax.ShapeDtypeStruct((2, l, D_FF), BF16),
                   jax.ShapeDtypeStruct((ni, SUBLANES, D_FF), F32), jax.ShapeDtypeStruct((ni, SUBLANES, D_FF), F32)],
        name="convffn_bwd", compiler_params=_params(("parallel", "parallel")),
    )(up_pre, up_pre, up_pre, up_pre, up_pre, up_pre, dx2b, dx2b, dx2b, w_down, conv_w, conv_b, conv_w, conv_b)
    return dup, jnp.concatenate([jnp.sum(pg, axis=0), jnp.sum(pv, axis=0)], axis=1)


def _local_step(x, target, wb, sp, mixer_weights=None, late_weights=None, grads_ready=None,
                grads_next=None):
    l = x.shape[0]
    tabs = _rope_tables(l)
    disc = _ssm_disc(sp["a_re"], sp["a_im"], sp["log_step"], sp["b_re"], sp["b_im"])
    bcat, ccat, lam_re, lam_im = _ssm_pack(*disc, sp["c_re"], sp["c_im"])
    d_skip = sp["d_skip"].reshape(1, SSM_WIDTH)

    h, qkv, u = _rms_mm_rope(x, sp["norm_mix_g"], wb["w_in"], tabs, "mm_in")
    attn, lse = _attn_fwd(qkv, sp["sink"])
    y_seg, u_seg, states = _ssm_fwd(_to_segments(u), bcat, ccat, lam_re, lam_im)
    y_ssm = _from_segments(y_seg)
    if mixer_weights is not None:
        wb = dict(wb, **mixer_weights(attn))
    pre, s_glu, ys = _glu_fwd(y_ssm, u, d_skip, wb["w_glu"])
    mixed, x1, h2 = _mix_mm_res_rms(attn, ys, sp["norm_attn_g"], sp["norm_ssm_g"], wb["w_out"], x,
                                    sp["norm_ffn_g"], "mm_out")
    if late_weights is not None:
        wb = dict(wb, **late_weights(h2))
    up_pre = _mm_nn_cols(h2, wb["w_up"], min(l, 1024), "mm_up")
    conv_w = wb["conv_w"]
    act = _convffn_fwd(up_pre, conv_w, sp["conv_b"])
    loss, dx2, dx2b, d_final_g = _mm_res_loss(act, wb["w_down"], x1, sp["norm_final_g"].reshape(1, D_MODEL), target)

    g = {"norm_final_g": d_final_g.reshape(D_MODEL)}
    g["w_down"] = _mm_tn(act, dx2b, D_FF // 2, 512, "mm_down_dw")
    dup_pre, conv_par = _convffn_bwd(up_pre, dx2b, wb["w_down"], conv_w, sp["conv_b"])
    g["conv_w"], g["conv_b"] = conv_par[0:3], conv_par[3:4]
    g["w_up"] = _mm_tn_cols(h2, dup_pre, wb["w_up"].shape[0], 512, "mm_up_dw")
    dx1, dx1b, g["norm_ffn_g"] = _mm_cols_rms_bwd(dup_pre, wb["w_up"], x1, sp["norm_ffn_g"], dx2, "mm_up_dx")
    g["w_out"] = _mm_tn(mixed, dx1b, 1024, 1024, "mm_out_dw")
    zero = grads_ready(g["w_up"], g["w_down"], g["w_out"]) if grads_ready is not None else 0.0
    dattn, dys, g["norm_attn_g"], g["norm_ssm_g"] = _mm_mix_bwd(
        dx1b, wb["w_out"], attn, ys, sp["norm_attn_g"] + zero, sp["norm_ssm_g"], "mm_out_dx")
    dpre, zb, dsb, dd = _glu_bwd(pre, s_glu, dys, u, d_skip, wb["w_glu"])
    g["d_skip"] = dd.reshape(N_SSM_GROUPS, SSM_GROUP)
    g["w_glu"] = _mm_tn(zb, dsb, 512, 512, "mm_glu_dw")
    zero = grads_next(g["w_glu"]) if grads_next is not None else 0.0
    du_seg, dbcat, dccat, dlam_re, dlam_im = _ssm_bwd(u_seg, _to_segments(dpre), states, bcat, ccat,
                                                      lam_re + zero, lam_im)
    dlb_re, dlb_im, dbb_re, dbb_im, g["c_re"], g["c_im"] = _ssm_unpack(dbcat, dccat, dlam_re, dlam_im)
    _, disc_vjp = jax.vjp(_ssm_disc, sp["a_re"], sp["a_im"], sp["log_step"], sp["b_re"], sp["b_im"])
    g["a_re"], g["a_im"], g["log_step"], g["b_re"], g["b_im"] = disc_vjp((dlb_re, dlb_im, dbb_re, dbb_im))
    dq, dkv, g["sink"] = _attn_bwd(qkv, attn, dattn, lse, sp["sink"])
    dproj = _rope_bwd(dq, dkv, _from_segments(du_seg), dpre, d_skip, tabs)
    g["w_in"] = _mm_tn(dproj, h, IN_WIDTH // 5, D_MODEL, "mm_in_dw")
    grad_x, _, g["norm_mix_g"] = _mm_nn_rms_bwd(dproj, wb["w_in"], x, sp["norm_mix_g"], dx1, "mm_in_dx")
    return loss, grad_x, g


MESH = pl.DeviceIdType.MESH
ANY = pl.BlockSpec(memory_space=pl.ANY)


def _place():
    x, y, c = lax.axis_index("x"), lax.axis_index("y"), lax.axis_index("c")
    chips = [(1 - x, y), (x, 1 - y), (1 - x, 1 - y)]
    return x, y, c, chips


def _chip_index(px, py):
    return 2 * px + py


CHUNK_BYTES = 256 * 1024
MAX_CHUNKS = 16


def _row_chunks(rows, row_bytes, align):
    n = max(1, min(MAX_CHUNKS, (rows * row_bytes) // CHUNK_BYTES))
    per = -(-rows // n)
    per = -(-per // align) * align
    return [(r0, min(per, rows - r0)) for r0 in range(0, rows, per)]


def _align_of(dtype):
    return SUBLANES * 4 // jnp.dtype(dtype).itemsize


def _remote(src, dst, send_sem, recv_sem, to):
    return pltpu.make_async_remote_copy(src_ref=src, dst_ref=dst, send_sem=send_sem, recv_sem=recv_sem,
                                        device_id=to, device_id_type=MESH)


CAST_ROWS = 64


def _gather_weights(shards, dtypes):
    nw = len(shards)

    def body(*refs):
        w_refs, o_refs = refs[:nw], refs[nw:2 * nw]
        send_sems, recv_sems, in_sems, out_sems = refs[2 * nw:2 * nw + 4]
        raw, cast = refs[2 * nw + 4:3 * nw + 4], refs[3 * nw + 4:]
        x, y, c, chips = _place()
        mine = _chip_index(x, y)
        sibling = (x, y, 1 - c)

        def rows_of(ref, chip, r0, nr):
            return ref.at[chip, pl.ds(r0, nr), :]

        def copy(wi, k, src, dst, to):
            return _remote(src, dst, send_sems.at[wi, k], recv_sems.at[wi, k], to)

        geo = []
        for wi in range(nw):
            rows, cols = w_refs[wi].shape
            row_bytes = cols * jnp.dtype(dtypes[wi]).itemsize
            geo.append((rows // 2, _row_chunks(rows // 2, row_bytes, _align_of(dtypes[wi]))))

        stage_in = [pltpu.make_async_copy(w_refs[wi], raw[wi], in_sems.at[wi]) for wi in range(nw)]
        for cp in stage_in:
            cp.start()
        staged = [raw[wi] if dtypes[wi] == w_refs[wi].dtype else cast[wi] for wi in range(nw)]
        stage_out = []
        for wi in range(nw):
            stage_in[wi].wait()
            if staged[wi] is not raw[wi]:
                def cast_rows(i, _, wi=wi):
                    rows = pl.ds(pl.multiple_of(i * CAST_ROWS, CAST_ROWS), CAST_ROWS)
                    cast[wi][rows, :] = raw[wi][rows, :].astype(dtypes[wi])
                    return 0

                lax.fori_loop(0, w_refs[wi].shape[0] // CAST_ROWS, cast_rows, 0)
            cp = pltpu.make_async_copy(staged[wi], o_refs[wi].at[mine], out_sems.at[wi])
            cp.start()
            stage_out.append(cp)

        for wi in range(nw):
            hr, half_chunks = geo[wi]
            for j, chip in enumerate(chips):
                for r0, nr in half_chunks:
                    copy(wi, j, staged[wi].at[pl.ds(c * hr + r0, nr), :],
                         rows_of(o_refs[wi], mine, c * hr + r0, nr), (*chip, c)).start()
        for wi in range(nw):
            hr, half_chunks = geo[wi]
            for j, chip in enumerate(chips):
                got = rows_of(o_refs[wi], _chip_index(*chip), c * hr, hr)
                copy(wi, j, got, got, (*chip, c)).wait_recv()
                for r0, nr in half_chunks:
                    piece = rows_of(o_refs[wi], _chip_index(*chip), c * hr + r0, nr)
                    copy(wi, 3 + j, piece, piece, sibling).start()
        for wi in range(nw):
            hr = geo[wi][0]
            for j, chip in enumerate(chips):
                got = rows_of(o_refs[wi], _chip_index(*chip), (1 - c) * hr, hr)
                copy(wi, 3 + j, got, got, sibling).wait_recv()
        for wi in range(nw):
            hr = geo[wi][0]
            sent = rows_of(o_refs[wi], mine, c * hr, hr)
            for k in range(6):
                copy(wi, k, sent, sent, sibling).wait_send()
            stage_out[wi].wait()

    return pl.pallas_call(
        body, in_specs=[ANY] * nw, out_specs=[ANY] * nw,
        out_shape=[jax.ShapeDtypeStruct((4, *s.shape), t) for s, t in zip(shards, dtypes)],
        scratch_shapes=[pltpu.SemaphoreType.DMA((nw, 6)), pltpu.SemaphoreType.DMA((nw, 6)),
                        pltpu.SemaphoreType.DMA((nw,)), pltpu.SemaphoreType.DMA((nw,))]
        + [pltpu.VMEM(s.shape, s.dtype) for s in shards] + [pltpu.VMEM(s.shape, t) for s, t in zip(shards, dtypes)],
        name="gather_weights", compiler_params=_params(vmem_mb=40),
    )(*shards)


HBM = pl.BlockSpec(memory_space=pltpu.HBM)
SEM = pl.BlockSpec(memory_space=pltpu.SEMAPHORE)
EFFECT = pltpu.SideEffectType.DATAFLOW_SIDE_EFFECTING


def _cast_place(w, place, dtype, after, name):
    rows, cols = w.shape
    tr = _row_tile(rows, cols, _align_of(dtype))

    def body(p_ref, w_ref, after_ref, o_ref):
        del p_ref, after_ref
        o_ref[...] = w_ref[...].astype(dtype)

    grid_spec = pltpu.PrefetchScalarGridSpec(
        num_scalar_prefetch=1, grid=(rows // tr,),
        in_specs=[pl.BlockSpec((tr, cols), lambda i, p: (i, 0)), ANY],
        out_specs=pl.BlockSpec((None, tr, cols), lambda i, p: (p[1], i, 0)))
    return pl.pallas_call(body, grid_spec=grid_spec, out_shape=jax.ShapeDtypeStruct((4, rows, cols), dtype),
                          name=name, compiler_params=_params(("parallel",)))(place, w, after)


def _split_start(name, arrays, n_pairs, issue):
    n = len(arrays)

    def body(*refs):
        issue(refs[:n], refs[n:n + n_pairs], refs[n + n_pairs:n + 2 * n_pairs])
        token = refs[2 * n + 2 * n_pairs]
        token[...] = jnp.zeros_like(token)

    dma = pltpu.SemaphoreType.DMA(())
    outs = pl.pallas_call(
        body, name=name,
        out_shape=[dma] * (2 * n_pairs) + [pltpu.HBM(t.shape, t.dtype) for t in arrays]
        + [jax.ShapeDtypeStruct((SUBLANES, LANES), F32)],
        in_specs=[HBM] * n, out_specs=[SEM] * (2 * n_pairs) + [HBM] * n + [pl.BlockSpec(memory_space=pltpu.VMEM)],
        input_output_aliases={a: 2 * n_pairs + a for a in range(n)},
        compiler_params=pltpu.CompilerParams(has_side_effects=EFFECT),
    )(*[pltpu.with_memory_space_constraint(t, pltpu.HBM) for t in arrays])
    return outs[:n_pairs], outs[n_pairs:2 * n_pairs], outs[2 * n_pairs:2 * n_pairs + n], outs[-1]


def _split_wait(name, send_sems, recv_sems, flying, sizes, after):
    n, n_pairs = len(flying), len(send_sems)

    def body(*refs):
        x, y, c, _ = _place()
        for k, ref in enumerate(sizes(refs[:n])):
            cp = _remote(ref, ref, refs[n + k], refs[n + n_pairs + k], (x, y, 1 - c))
            cp.wait_send()
            cp.wait_recv()

    return pl.pallas_call(
        body, name=name, out_shape=[pltpu.HBM(t.shape, t.dtype) for t in flying],
        in_specs=[HBM] * n + [SEM] * (2 * n_pairs) + [ANY], out_specs=[HBM] * n,
        input_output_aliases={a: a for a in range(n)},
        compiler_params=pltpu.CompilerParams(has_side_effects=EFFECT),
    )(*flying, *send_sems, *recv_sems, after)


def _spread_start(lands, name):
    def issue(land_refs, send_sems, recv_sems):
        x, y, c, chips = _place()
        mine = _chip_index(x, y)
        for a, land in enumerate(land_refs):
            _, rows, cols = land.shape
            hr = rows // 2
            row_bytes = cols * jnp.dtype(land.dtype).itemsize
            for r0, nr in _row_chunks(hr, row_bytes, _align_of(land.dtype)):
                piece = land.at[mine, pl.ds(c * hr + r0, nr), :]
                for chip in chips:
                    for core in (0, 1):
                        _remote(piece, piece, send_sems[a], recv_sems[a], (*chip, core)).start()

    return _split_start(name, lands, len(lands), issue)


def _spread_wait(send_sems, recv_sems, flying, after, name):
    return _split_wait(name, send_sems, recv_sems, flying, lambda refs: [r.at[pl.ds(0, 3)] for r in refs], after)


def _pair_start(grads):
    n = len(grads)
    zones = [lax.empty((4, g.shape[1] // 2, g.shape[2]), F32) for g in grads]

    def issue(refs, send_sems, recv_sems):
        x, y, c, _ = _place()
        for a in range(n):
            g_ref, z_ref = refs[a], refs[n + a]
            _, rows, cols = g_ref.shape
            hr = rows // 2
            for k in range(4):
                for r0, nr in _row_chunks(hr, cols * 4, SUBLANES):
                    _remote(g_ref.at[k, pl.ds((1 - c) * hr + r0, nr), :], z_ref.at[k, pl.ds(r0, nr), :],
                            send_sems[a], recv_sems[a], (x, y, 1 - c)).start()

    return _split_start("pair_start", list(grads) + zones, n, issue)


def _pair_wait(send_sems, recv_sems, flying, after):
    n = len(flying) // 2
    out = _split_wait("pair_wait", send_sems, recv_sems, flying, lambda refs: list(refs[n:]), after)
    return out[:n], out[n:]


def _chip_start(sums):
    n = len(sums)
    zones = [lax.empty((3, *s.shape[1:]), s.dtype) for s in sums]

    def issue(refs, send_sems, recv_sems):
        x, y, c, chips = _place()
        for a in range(n):
            s_ref, z_ref = refs[a], refs[n + a]
            _, rows, cols = s_ref.shape
            row_bytes = cols * jnp.dtype(s_ref.dtype).itemsize
            for r0, nr in _row_chunks(rows, row_bytes, _align_of(s_ref.dtype)):
                for j, chip in enumerate(chips):
                    _remote(s_ref.at[_chip_index(*chip), pl.ds(r0, nr), :], z_ref.at[j, pl.ds(r0, nr), :],
                            send_sems[a], recv_sems[a], (*chip, c)).start()

    return _split_start("chip_start", list(sums) + zones, n, issue)


def _chip_wait(send_sems, recv_sems, flying, after):
    n = len(flying) // 2
    return _split_wait("chip_wait", send_sems, recv_sems, flying, lambda refs: list(refs[n:]), after)[n:]


def _pair_exchange(grads):
    na = len(grads)

    def body(*refs):
        g_refs, o_refs = refs[:na], refs[na:2 * na]
        send_sems, recv_sems = refs[2 * na:]
        x, y, c, _ = _place()
        sibling = (x, y, 1 - c)
        for ai in range(na):
            _, rows, cols = g_refs[ai].shape
            hr = rows // 2
            for k in range(4):
                for r0, nr in _row_chunks(hr, cols * 4, SUBLANES):
                    _remote(g_refs[ai].at[k, pl.ds((1 - c) * hr + r0, nr), :], o_refs[ai].at[k, pl.ds(r0, nr), :],
                            send_sems.at[ai], recv_sems.at[ai], sibling).start()
        for ai in range(na):
            _remote(o_refs[ai], o_refs[ai], send_sems.at[ai], recv_sems.at[ai], sibling).wait()

    return pl.pallas_call(
        body, in_specs=[ANY] * na, out_specs=[ANY] * na,
        out_shape=[jax.ShapeDtypeStruct((4, g.shape[1] // 2, g.shape[2]), F32) for g in grads],
        scratch_shapes=[pltpu.SemaphoreType.DMA((na,)), pltpu.SemaphoreType.DMA((na,))],
        name="pair_exchange",
    )(*grads)


def _row_tile(rows, cols, align, elems=256 * 1024):
    best = align
    for cand in range(align, rows + 1, align):
        if rows % cand == 0 and cand * cols <= elems:
            best = cand
    return best


def _pair_sum(g, got, place, transit, name):
    _, rows, cols = g.shape
    hr = rows // 2
    tr = _row_tile(hr, cols, _align_of(transit), 512 * 1024)
    nt = hr // tr

    def body(p_ref, g_ref, r_ref, s_ref, own_ref):
        total = g_ref[...] + r_ref[...]
        s_ref[...] = total.astype(transit)

        @pl.when(pl.program_id(1) == p_ref[1])
        def _():
            own_ref[...] = total

    grid_spec = pltpu.PrefetchScalarGridSpec(
        num_scalar_prefetch=1, grid=(nt, 4),
        in_specs=[pl.BlockSpec((None, tr, cols), lambda i, k, p: (k, p[0] * nt + i, 0)),
                  pl.BlockSpec((None, tr, cols), lambda i, k, p: (k, i, 0))],
        out_specs=[pl.BlockSpec((None, tr, cols), lambda i, k, p: (k, i, 0)),
                   pl.BlockSpec((tr, cols), lambda i, k, p: (i, 0))])
    return pl.pallas_call(
        body, grid_spec=grid_spec,
        out_shape=[jax.ShapeDtypeStruct((4, hr, cols), transit), jax.ShapeDtypeStruct((hr, cols), F32)],
        name=name, compiler_params=_params(("parallel", "arbitrary")),
    )(place, g, got)


def _chip_exchange(sums):
    na = len(sums)

    def body(*refs):
        s_refs, o_refs = refs[:na], refs[na:2 * na]
        send_sems, recv_sems = refs[2 * na:]
        x, y, c, chips = _place()
        for ai in range(na):
            _, rows, cols = s_refs[ai].shape
            row_bytes = cols * jnp.dtype(s_refs[ai].dtype).itemsize
            for r0, nr in _row_chunks(rows, row_bytes, _align_of(s_refs[ai].dtype)):
                for j, chip in enumerate(chips):
                    _remote(s_refs[ai].at[_chip_index(*chip), pl.ds(r0, nr), :], o_refs[ai].at[j, pl.ds(r0, nr), :],
                            send_sems.at[ai, j], recv_sems.at[ai, j], (*chip, c)).start()
        for ai in range(na):
            for j, chip in enumerate(chips):
                _remote(o_refs[ai].at[j], o_refs[ai].at[j], send_sems.at[ai, j], recv_sems.at[ai, j],
                        (*chip, c)).wait()

    return pl.pallas_call(
        body, in_specs=[ANY] * na, out_specs=[ANY] * na,
        out_shape=[jax.ShapeDtypeStruct((3, *s.shape[1:]), s.dtype) for s in sums],
        scratch_shapes=[pltpu.SemaphoreType.DMA((na, 3)), pltpu.SemaphoreType.DMA((na, 3))],
        name="chip_exchange",
    )(*sums)


def _chip_sum(own, landed, name):
    hr, cols = own.shape
    tr = _row_tile(hr, cols, _align_of(landed.dtype))

    def body(o_ref, l_ref, f_ref):
        acc = o_ref[...]
        for j in range(3):
            acc = acc + l_ref[j].astype(F32)
        f_ref[...] = acc

    return pl.pallas_call(
        body, grid=(hr // tr,),
        in_specs=[pl.BlockSpec((tr, cols), lambda i: (i, 0)), pl.BlockSpec((3, tr, cols), lambda i: (0, i, 0))],
        out_specs=pl.BlockSpec((tr, cols), lambda i: (i, 0)),
        out_shape=jax.ShapeDtypeStruct((hr, cols), F32), name=name,
        compiler_params=_params(("parallel",)),
    )(own, landed)


def _final_exchange(halves, small):
    nh = len(halves)

    def body(*refs):
        h_refs, s_ref = refs[:nh], refs[nh]
        o_refs, so_ref = refs[nh + 1:2 * nh + 1], refs[2 * nh + 1]
        send_sems, recv_sems, local_sem, ssend_sems, srecv_sems = refs[2 * nh + 2:]
        x, y, c, _ = _place()
        me = 4 * x + 2 * y + c
        sibling = (x, y, 1 - c)
        for hi in range(nh):
            hr, cols = h_refs[hi].shape
            for r0, nr in _row_chunks(hr, cols * 4, SUBLANES):
                _remote(h_refs[hi].at[pl.ds(r0, nr), :], o_refs[hi].at[pl.ds(r0, nr), :],
                        send_sems.at[hi], recv_sems.at[hi], sibling).start()
        small_cps = [pltpu.make_async_copy(s_ref, so_ref.at[me], local_sem)]
        for r in range(1, 8):
            fx, fy, fc = (r >> 2) & 1, (r >> 1) & 1, r & 1
            peer = (1 - x if fx else x, 1 - y if fy else y, 1 - c if fc else c)
            small_cps.append(_remote(s_ref, so_ref.at[me], ssend_sems.at[r - 1], srecv_sems.at[r - 1], peer))
        for cp in small_cps:
            cp.start()
        for hi in range(nh):
            _remote(h_refs[hi], o_refs[hi], send_sems.at[hi], recv_sems.at[hi], sibling).wait()
        for cp in small_cps:
            cp.wait()

    return pl.pallas_call(
        body, in_specs=[ANY] * (nh + 1), out_specs=[ANY] * (nh + 1),
        out_shape=[jax.ShapeDtypeStruct(h.shape, F32) for h in halves]
        + [jax.ShapeDtypeStruct((8, *small.shape), F32)],
        scratch_shapes=[pltpu.SemaphoreType.DMA((nh,)), pltpu.SemaphoreType.DMA((nh,)),
                        pltpu.SemaphoreType.DMA, pltpu.SemaphoreType.DMA((7,)), pltpu.SemaphoreType.DMA((7,))],
        name="final_exchange",
    )(*halves, small)


def _adamw_halves(w, own, other, m, v, place, name):
    r, c = w.shape
    hr = r // 2
    tr = _row_tile(hr, c, SUBLANES)
    nt = hr // tr
    c1 = 1.0 - ADAM_B1 ** ADAM_STEP
    c2 = 1.0 - ADAM_B2 ** ADAM_STEP

    def body(p_ref, w_ref, own_ref, other_ref, m_ref, v_ref, g_ref, d_ref, nm_ref, nv_ref):
        mine = pl.program_id(0) // nt == p_ref[0]
        gv = jnp.where(mine, own_ref[...], other_ref[...])
        nm = ADAM_B1 * m_ref[...] + (1.0 - ADAM_B1) * gv
        nv = ADAM_B2 * v_ref[...] + (1.0 - ADAM_B2) * (gv * gv)
        g_ref[...] = gv
        d_ref[...] = -ADAM_LR * ((nm / c1) / (jnp.sqrt(nv / c2) + ADAM_EPS) + ADAM_WD * w_ref[...])
        nm_ref[...] = nm
        nv_ref[...] = nv

    full = pl.BlockSpec((tr, c), lambda i, p: (i, 0))
    half = pl.BlockSpec((tr, c), lambda i, p: (i % nt, 0))
    out = jax.ShapeDtypeStruct((r, c), F32)
    grid_spec = pltpu.PrefetchScalarGridSpec(num_scalar_prefetch=1, grid=(2 * nt,),
                                             in_specs=[full, half, half, full, full], out_specs=[full] * 4)
    return pl.pallas_call(body, grid_spec=grid_spec, out_shape=[out] * 4, name=name,
                          compiler_params=_params(("parallel",)))(place, w, own, other, m, v)


def _adamw_many(ws, gs, ms, vs, name):
    n = len(ws)
    c1 = 1.0 - ADAM_B1 ** ADAM_STEP
    c2 = 1.0 - ADAM_B2 ** ADAM_STEP

    def body(*refs):
        w_refs, g_refs, m_refs, v_refs = (refs[k * n:(k + 1) * n] for k in range(4))
        d_refs, nm_refs, nv_refs = (refs[(4 + k) * n:(5 + k) * n] for k in range(3))
        for i in range(n):
            gv = g_refs[i][...]
            nm = ADAM_B1 * m_refs[i][...] + (1.0 - ADAM_B1) * gv
            nv = ADAM_B2 * v_refs[i][...] + (1.0 - ADAM_B2) * (gv * gv)
            d_refs[i][...] = -ADAM_LR * ((nm / c1) / (jnp.sqrt(nv / c2) + ADAM_EPS) + ADAM_WD * w_refs[i][...])
            nm_refs[i][...] = nm
            nv_refs[i][...] = nv

    vmem = pl.BlockSpec(memory_space=pltpu.VMEM)
    shapes = [jax.ShapeDtypeStruct(t.shape, F32) for t in ws]
    outs = pl.pallas_call(body, in_specs=[vmem] * (4 * n), out_specs=[vmem] * (3 * n), out_shape=shapes * 3,
                          name=name, compiler_params=_params(vmem_mb=56))(*ws, *gs, *ms, *vs)
    return outs[:n], outs[n:2 * n], outs[2 * n:]


BIG = ("w_in", "w_glu", "w_out", "w_up", "w_down")
WEIGHTS = ("norm_mix_g", "w_in", "a_re", "a_im", "log_step", "b_re", "b_im", "c_re", "c_im", "d_skip", "w_glu",
           "sink", "norm_attn_g", "norm_ssm_g", "w_out", "norm_ffn_g", "w_up", "conv_w", "conv_b", "w_down",
           "norm_final_g")
SMALL = ("norm_mix_g", "a_re", "a_im", "log_step", "b_re", "b_im", "c_re", "c_im", "d_skip", "sink",
         "norm_attn_g", "norm_ssm_g", "norm_ffn_g", "conv_w", "conv_b", "norm_final_g")
SMALL_ROWS = 48
N_DEV = 8


def _tile_rows(size):
    return -(-size // (SUBLANES * D_MODEL)) * SUBLANES


def _by_owner(name, g):
    if name == "w_up":
        return g
    return g.reshape(4, g.shape[0] // 4, g.shape[1])


def _view(name, t):
    if name == "w_in":
        return jnp.swapaxes(t[0], 0, 1)
    if name in ("b_re", "b_im"):
        return jnp.swapaxes(t, -1, -2)
    return t


def _unview(name, t):
    if name == "w_in":
        return jnp.swapaxes(t, 0, 1)[None]
    if name in ("b_re", "b_im"):
        return jnp.swapaxes(t, -1, -2)
    return t


def kernel(x, norm_mix_g, w_in, a_re, a_im, log_step, b_re, b_im, c_re, c_im, d_skip, w_glu, sink, norm_attn_g, norm_ssm_g, w_out, norm_ffn_g, w_up, conv_w, conv_b, w_down, norm_final_g, loss_target, m_norm_mix_g, m_w_in, m_a_re, m_a_im, m_log_step, m_b_re, m_b_im, m_c_re, m_c_im, m_d_skip, m_w_glu, m_sink, m_norm_attn_g, m_norm_ssm_g, m_w_out, m_norm_ffn_g, m_w_up, m_conv_w, m_conv_b, m_w_down, m_norm_final_g, v_norm_mix_g, v_w_in, v_a_re, v_a_im, v_log_step, v_b_re, v_b_im, v_c_re, v_c_im, v_d_skip, v_w_glu, v_sink, v_norm_attn_g, v_norm_ssm_g, v_w_out, v_norm_ffn_g, v_w_up, v_conv_w, v_conv_b, v_w_down, v_norm_final_g):
    given = dict(locals())
    w = {n: given[n] for n in WEIGHTS}
    m = {n: given["m_" + n] for n in WEIGHTS}
    v = {n: given["v_" + n] for n in WEIGHTS}
    xy = 2 * lax.axis_index("x") + lax.axis_index("y")

    core = lax.axis_index("c")
    place = jnp.stack([core, xy]).astype(jnp.int32)

    conv_rows = jnp.pad(w["conv_w"][0], ((0, 2 * SUBLANES - 3), (0, 0)))
    rows = lambda t: t.reshape(4 * t.shape[1], t.shape[2])
    (w_in_all,) = _gather_weights([_view("w_in", w["w_in"])], [BF16])
    wb = {"w_in": rows(w_in_all)}
    mixer = [_cast_place(w[n][0], place, BF16, w_in_all, "cast_" + n) for n in ("w_glu", "w_out")]
    mixer.append(_cast_place(conv_rows, place, F32, w_in_all, "cast_conv_w"))
    *mixer_flight, mixer_token = _spread_start(mixer, "spread_mixer_start")
    late = ("w_up", "w_down")
    *late_flight, token = _spread_start(
        [_cast_place(w[n][0], place, BF16, mixer_token, "cast_" + n) for n in late], "spread_ffn_start")

    def mixer_weights(after):
        w_glu4, w_out4, conv4 = _spread_wait(*mixer_flight, after, "spread_mixer_wait")
        return {"w_glu": rows(w_glu4), "w_out": rows(w_out4),
                "conv_w": conv4[:, :3].transpose(1, 0, 2).reshape(3, 2 * D_FF)}

    def late_weights(after):
        w_up4, w_down4 = _spread_wait(*late_flight, after, "spread_ffn_wait")
        return {"w_up": w_up4, "w_down": rows(w_down4)}

    sp = {n: w[n][0] for n in ("a_re", "a_im", "log_step", "b_re", "b_im", "c_re", "c_im", "d_skip",
                               "norm_mix_g", "norm_attn_g", "norm_ssm_g", "norm_ffn_g", "sink", "conv_b")}
    for n in ("norm_mix_g", "norm_attn_g", "norm_ssm_g", "norm_ffn_g", "sink", "conv_b"):
        sp[n] = sp[n].reshape(1, -1)
    sp["norm_mix_g"] = sp["norm_mix_g"] + token[:1, :1]
    sp["norm_final_g"] = w["norm_final_g"]
    early, tail = late + ("w_out",), ("w_in", "w_glu")
    flight = {}

    def grads_ready(dw_up, dw_down, dw_out):
        *flight["pair"], token = _pair_start([dw_up, _by_owner("w_down", dw_down), _by_owner("w_out", dw_out)])
        return token[:1, :1]

    def grads_next(after):
        mine, got = _pair_wait(*flight["pair"], after)
        sums, flight["own"] = zip(*[_pair_sum(a, b, place, BF16, "pair_sum_" + n) for n, a, b in zip(early, mine, got)])
        *flight["chip"], token = _chip_start(list(sums))
        return token[:1, :1]

    loss, grad_x, g = _local_step(x[0], loss_target[0], wb, sp, mixer_weights, late_weights, grads_ready,
                                  grads_next)

    def as_rows(t):
        rows = _tile_rows(t.size)
        return jnp.pad(t.reshape(-1), (0, rows * D_MODEL - t.size)).reshape(rows, D_MODEL)

    pieces = [as_rows(g[n]) for n in SMALL] + [as_rows(loss)]
    spare = N_DEV * SMALL_ROWS - sum(p.shape[0] for p in pieces)
    small = jnp.concatenate(pieces + [jnp.zeros((spare, D_MODEL), F32)]).reshape(4, 2 * SMALL_ROWS, D_MODEL)
    by_owner = [_by_owner(n, g[n]) for n in tail] + [small]
    got = _pair_exchange(by_owner)
    transit = [BF16] * len(tail) + [F32]
    chip_sums, own_sums = zip(*[_pair_sum(a, b, place, t, "pair_sum_" + n)
                                for n, a, b, t in zip(tail + ("small",), by_owner, got, transit)])
    landed = _chip_exchange(list(chip_sums))
    halves = {n: _chip_sum(o, t, "chip_sum_" + n) for n, o, t in zip(tail + ("small",), own_sums, landed)}
    early_landed = _chip_wait(*flight["chip"], grad_x)
    for n, o, t in zip(early, flight["own"], early_landed):
        halves[n] = _chip_sum(o, t, "chip_sum_" + n)
    *others, small_all = _final_exchange([halves[n] for n in BIG], halves["small"])
    small_all = small_all.reshape(N_DEV * SMALL_ROWS, D_MODEL)
    grads, row = {}, 0
    for n in SMALL:
        shape = (3, 4 * w[n].shape[-1]) if n == "conv_w" else w[n].shape[1:] if n != "norm_final_g" else w[n].shape
        size = math.prod(shape)
        grads[n] = small_all[row:row + _tile_rows(size)].reshape(-1)[:size].reshape(shape)
        row += _tile_rows(size)
    loss = small_all[row, 0]
    cw = w["conv_w"].shape[-1]
    grads["conv_w"] = lax.dynamic_slice_in_dim(grads["conv_w"], xy * cw, cw, axis=1)
    grads = {n: _view(n, grads[n].reshape(w[n].shape)) for n in SMALL}
    wv, mv, vv = ({n: _view(n, t[n]) for n in WEIGHTS} for t in (w, m, v))

    delta, new_m, new_v = {}, {}, {}
    for n, other in zip(BIG, others):
        two_d = lambda t: t.reshape(t.shape[-2:])
        grads[n], delta[n], new_m[n], new_v[n] = _adamw_halves(
            two_d(wv[n]), halves[n], other, two_d(mv[n]), two_d(vv[n]), place, "adamw_" + n)
    for group, name in ((("b_re", "b_im"), "adamw_b"), (tuple(n for n in SMALL if n not in ("b_re", "b_im")), "adamw_small")):
        row = lambda t: t.reshape(1, -1) if t.ndim == 1 else t
        d_, m_, v_ = _adamw_many(*[[row(t[n]) for n in group] for t in (wv, grads, mv, vv)], name)
        for n, dn, mn, vn in zip(group, d_, m_, v_):
            delta[n], new_m[n], new_v[n] = (t.reshape(wv[n].shape) for t in (dn, mn, vn))
    natural = lambda t: [_unview(n, t[n].reshape(wv[n].shape)) for n in WEIGHTS]
    return (loss, grad_x[None], *natural(grads), *natural(delta), *natural(new_m), *natural(new_v))
```

```python
import functools
import math

import jax
import jax.numpy as jnp
import numpy as np
from jax import lax
from jax.experimental import pallas as pl
from jax.experimental.pallas import tpu as pltpu

F32 = jnp.float32
BF16 = jnp.bfloat16

D_MODEL = 1024
N_Q_HEADS = 8
N_KV_HEADS = 2
HEAD_DIM = 64
ATTN_WIDTH = 512
KV_WIDTH = 128
QKV_WIDTH = ATTN_WIDTH + 2 * KV_WIDTH
WINDOW = 128
BLOCK = 128
ROPE_DIM = 16
ROPE_THETA = 500000.0
SCORE_SCALE = HEAD_DIM ** -0.5
SSM_WIDTH = 512
SSM_GROUP = 16
N_SSM_GROUPS = 32
SSM_STATE = 64
IN_WIDTH = 1280
D_FF = 2816
EPS = 1e-6
ADAM_LR = 0.001
ADAM_B1 = 0.9
ADAM_B2 = 0.999
ADAM_EPS = 1e-08
ADAM_WD = 0.01
ADAM_STEP = 10

VMEM_BYTES_V7X = 64 * 1024 * 1024
SUBLANES = 8
LANES = 128
SSM_CB = 4
SSM_CH = 128
SSM_ST = 512
N_SEG = SUBLANES

NN = (((1,), (0,)), ((), ()))
NT = (((1,), (1,)), ((), ()))
TN = (((0,), (0,)), ((), ()))


def _params(sem=None, vmem_mb=48):
    limit = vmem_mb * 1024 * 1024
    assert limit < VMEM_BYTES_V7X
    return pltpu.CompilerParams(dimension_semantics=sem, vmem_limit_bytes=limit)


def _dg(a, b, dims):
    return lax.dot_general(a, b, dims, preferred_element_type=F32)


def _sigmoid(x):
    return 1.0 / (1.0 + jnp.exp(-x))


_SQRT_HALF = 0.7071067811865476
_INV_SQRT_2PI = 0.3989422804014327


def _gelu(x):
    return 0.5 * x * (1.0 + lax.erf(x * _SQRT_HALF))


def _gelu_grad(x):
    return 0.5 * (1.0 + lax.erf(x * _SQRT_HALF)) + x * (_INV_SQRT_2PI * jnp.exp(-0.5 * x * x))


def _mm_tn(a, b, tm, tn, name):
    k, m = a.shape
    n = b.shape[1]

    def body(a_ref, b_ref, o_ref):
        o_ref[...] = _dg(a_ref[...], b_ref[...], TN)

    return pl.pallas_call(
        body, grid=(m // tm, n // tn),
        in_specs=[pl.BlockSpec((k, tm), lambda i, j: (0, i)), pl.BlockSpec((k, tn), lambda i, j: (0, j))],
        out_specs=pl.BlockSpec((tm, tn), lambda i, j: (i, j)),
        out_shape=jax.ShapeDtypeStruct((m, n), F32), name=name,
        compiler_params=_params(("parallel", "parallel")),
    )(a, b)


def _mm_nn_cols(a, b4, tm, name):
    m, k = a.shape
    s, _, n = b4.shape

    def body(a_ref, b_ref, o_ref):
        o_ref[...] = _dg(a_ref[...], b_ref[...], NN)

    return pl.pallas_call(
        body, grid=(m // tm, s),
        in_specs=[pl.BlockSpec((tm, k), lambda i, j: (i, 0)), pl.BlockSpec((None, k, n), lambda i, j: (j, 0, 0))],
        out_specs=pl.BlockSpec((tm, n), lambda i, j: (i, j)),
        out_shape=jax.ShapeDtypeStruct((m, s * n), F32), name=name,
        compiler_params=_params(("parallel", "parallel")),
    )(a, b4)


def _mm_tn_cols(a, b2, s, tm, name):
    k, m = a.shape
    h, _, wide = b2.shape
    per = s // h
    n = wide // per

    def body(a_ref, b_ref, o_ref):
        o_ref[...] = _dg(a_ref[...], b_ref[...], TN)

    return pl.pallas_call(
        body, grid=(s, m // tm),
        in_specs=[pl.BlockSpec((k, tm), lambda j, i: (0, i)),
                  pl.BlockSpec((None, k, n), lambda j, i: (j // per, 0, j % per))],
        out_specs=pl.BlockSpec((None, tm, n), lambda j, i: (j, i, 0)),
        out_shape=jax.ShapeDtypeStruct((s, m, n), F32), name=name,
        compiler_params=_params(("parallel", "parallel")),
    )(a, b2)


TM_EW = 256


def _rms_bwd_vals(xv, gv, dy):
    r = lax.rsqrt(jnp.mean(xv * xv, axis=-1, keepdims=True) + EPS)
    xh = xv * r
    dxh = dy * gv
    dx = r * (dxh - xh * jnp.mean(dxh * xh, axis=-1, keepdims=True))
    return dx, dy * xh


TM_FUSED = 512
TM_LOSS = 256


def _rms_vals(xv, gv):
    return xv * lax.rsqrt(jnp.mean(xv * xv, axis=-1, keepdims=True) + EPS) * gv


def _rope_blocks(src, dst, c, lo, hi):
    nq = ATTN_WIDTH // LANES
    for blk in range(nq + 1):
        t = src[:, blk * LANES:(blk + 1) * LANES]
        rot = t * c + pltpu.roll(t, LANES - 8, 1) * lo + pltpu.roll(t, 8, 1) * hi
        dst[:, blk * LANES:(blk + 1) * LANES] = (rot * SCORE_SCALE if blk < nq else rot).astype(BF16)
    dst[:, (nq + 1) * LANES:] = src[:, (nq + 1) * LANES:].astype(BF16)


def _rms_mm_rope(x, g, wt, tabs, name):
    l, d = x.shape
    n = wt.shape[0]

    def body(x_ref, g_ref, w_ref, c_ref, lo_ref, hi_ref, h_ref, qkv_ref, u_ref):
        h = _rms_vals(x_ref[...], g_ref[...]).astype(BF16)
        h_ref[...] = h
        out = _dg(h, w_ref[...], NT)
        _rope_blocks(out[:, :QKV_WIDTH], qkv_ref, c_ref[...], lo_ref[...], hi_ref[...])
        u_ref[...] = out[:, QKV_WIDTH:]

    row = lambda width: pl.BlockSpec((TM_FUSED, width), lambda i: (i, 0))
    return pl.pallas_call(
        body, grid=(l // TM_FUSED,),
        in_specs=[row(d), pl.BlockSpec((1, d), lambda i: (0, 0)), pl.BlockSpec((n, d), lambda i: (0, 0)),
                  row(LANES), row(LANES), row(LANES)],
        out_specs=[row(d), row(QKV_WIDTH), row(n - QKV_WIDTH)],
        out_shape=[jax.ShapeDtypeStruct((l, d), BF16), jax.ShapeDtypeStruct((l, QKV_WIDTH), BF16),
                   jax.ShapeDtypeStruct((l, n - QKV_WIDTH), F32)],
        name=name, compiler_params=_params(("parallel",)),
    )(x, g, wt, *tabs)


def _mix_mm_res_rms(attn, ys, g_attn, g_ssm, b, res, g, name):
    l, w = attn.shape
    d = b.shape[1]

    def body(a_ref, y_ref, ga_ref, gs_ref, b_ref, r_ref, g_ref, m_ref, x_ref, h_ref):
        m_ref[:, :w] = _rms_vals(a_ref[...], ga_ref[...]).astype(BF16)
        m_ref[:, w:] = _rms_vals(y_ref[...], gs_ref[...]).astype(BF16)
        xv = r_ref[...] + _dg(m_ref[...], b_ref[...], NN)
        x_ref[...] = xv
        h_ref[...] = _rms_vals(xv, g_ref[...]).astype(BF16)

    row = lambda width: pl.BlockSpec((TM_FUSED, width), lambda i: (i, 0))
    vec = lambda width: pl.BlockSpec((1, width), lambda i: (0, 0))
    return pl.pallas_call(
        body, grid=(l // TM_FUSED,),
        in_specs=[row(w), row(w), vec(w), vec(w), pl.BlockSpec((2 * w, d), lambda i: (0, 0)), row(d), vec(d)],
        out_specs=[row(2 * w), row(d), row(d)],
        out_shape=[jax.ShapeDtypeStruct((l, 2 * w), BF16), jax.ShapeDtypeStruct((l, d), F32),
                   jax.ShapeDtypeStruct((l, d), BF16)],
        name=name, compiler_params=_params(("parallel",)),
    )(attn, ys, g_attn, g_ssm, b, res, g)


def _mm_res_loss(a, b, res, g, target):
    l, k = a.shape
    d = b.shape[1]

    def body(a_ref, b_ref, r_ref, g_ref, t_ref, loss_ref, dx_ref, dxb_ref, dg_ref):
        xv = r_ref[...] + _dg(a_ref[...], b_ref[...], NN)
        gv = g_ref[...]
        r = lax.rsqrt(jnp.mean(xv * xv, axis=-1, keepdims=True) + EPS)
        xh = xv * r
        e = xh * gv - t_ref[...]
        part = jnp.sum(jnp.sum(e * e, axis=1, keepdims=True), axis=0, keepdims=True) * (0.5 / d)
        dy = e * (1.0 / d)
        dxh = dy * gv
        dx = r * (dxh - xh * jnp.mean(dxh * xh, axis=-1, keepdims=True))
        dx_ref[...] = dx
        dxb_ref[...] = dx.astype(BF16)

        @pl.when(pl.program_id(0) == 0)
        def _():
            dg_ref[...] = jnp.zeros_like(dg_ref)
            loss_ref[...] = jnp.zeros_like(loss_ref)

        dg_ref[...] += jnp.sum(dy * xh, axis=0, keepdims=True)
        loss_ref[...] += part

    row = lambda width: pl.BlockSpec((TM_LOSS, width), lambda i: (i, 0))
    vec = pl.BlockSpec((1, d), lambda i: (0, 0))
    return pl.pallas_call(
        body, grid=(l // TM_LOSS,),
        in_specs=[row(k), pl.BlockSpec((k, d), lambda i: (0, 0)), row(d), vec, row(d)],
        out_specs=[pl.BlockSpec((1, 1), lambda i: (0, 0)), row(d), row(d), vec],
        out_shape=[jax.ShapeDtypeStruct((1, 1), F32), jax.ShapeDtypeStruct((l, d), F32),
                   jax.ShapeDtypeStruct((l, d), BF16), jax.ShapeDtypeStruct((1, d), F32)],
        name="mm_down_loss", compiler_params=_params(("arbitrary",)),
    )(a, b, res, g, target)


def _mm_rms_bwd(a, b, a_spec, b_spec, matmul, x, g, res, name):
    l, d = x.shape

    def body(a_ref, b_ref, x_ref, g_ref, res_ref, dx_ref, dxb_ref, dg_ref):
        dx, dgr = _rms_bwd_vals(x_ref[...], g_ref[...], matmul(a_ref, b_ref))
        dx = dx + res_ref[...]
        dx_ref[...] = dx
        dxb_ref[...] = dx.astype(BF16)

        @pl.when(pl.program_id(0) == 0)
        def _():
            dg_ref[...] = jnp.zeros_like(dg_ref)

        dg_ref[...] += jnp.sum(dgr, axis=0, keepdims=True)

    row = pl.BlockSpec((TM_FUSED, d), lambda i: (i, 0))
    vec = pl.BlockSpec((1, d), lambda i: (0, 0))
    return pl.pallas_call(
        body, grid=(l // TM_FUSED,), in_specs=[a_spec, b_spec, row, vec, row], out_specs=[row, row, vec],
        out_shape=[jax.ShapeDtypeStruct((l, d), F32), jax.ShapeDtypeStruct((l, d), BF16),
                   jax.ShapeDtypeStruct((1, d), F32)],
        name=name, compiler_params=_params(("arbitrary",)),
    )(a, b, x, g, res)


def _mm_nn_rms_bwd(a, b, x, g, res, name):
    return _mm_rms_bwd(a, b, pl.BlockSpec((TM_FUSED, a.shape[1]), lambda i: (i, 0)),
                       pl.BlockSpec(b.shape, lambda i: (0, 0)),
                       lambda a_ref, b_ref: _dg(a_ref[...], b_ref[...], NN), x, g, res, name)


def _mm_cols_rms_bwd(a2, b4, x, g, res, name):
    h, _, wide = a2.shape
    s, _, n = b4.shape
    per = s // h

    def matmul(a_ref, b_ref):
        acc = None
        for j in range(s):
            part = _dg(a_ref[j // per, :, (j % per) * n:(j % per + 1) * n], b_ref[j], NT)
            acc = part if acc is None else acc + part
        return acc

    return _mm_rms_bwd(a2, b4, pl.BlockSpec((h, TM_FUSED, wide), lambda i: (0, i, 0)),
                       pl.BlockSpec(b4.shape, lambda i: (0, 0, 0), pipeline_mode=pl.Buffered(1)),
                       matmul, x, g, res, name)


def _mm_mix_bwd(dx, b, attn, ys, g_attn, g_ssm, name):
    l, w = attn.shape
    d = dx.shape[1]

    def body(dx_ref, b_ref, a_ref, y_ref, ga_ref, gs_ref, da_ref, dy_ref, dga_ref, dgs_ref):
        @pl.when(pl.program_id(0) == 0)
        def _():
            dga_ref[...] = jnp.zeros_like(dga_ref)
            dgs_ref[...] = jnp.zeros_like(dgs_ref)

        dm = _dg(dx_ref[...], b_ref[...], NT)
        for src, gr, off, dst, dgr in ((a_ref, ga_ref, 0, da_ref, dga_ref), (y_ref, gs_ref, w, dy_ref, dgs_ref)):
            dxv, dg_rows = _rms_bwd_vals(src[...], gr[...], dm[:, off:off + w])
            dst[...] = dxv
            dgr[...] += jnp.sum(dg_rows, axis=0, keepdims=True)

    row = lambda width: pl.BlockSpec((TM_FUSED, width), lambda i: (i, 0))
    vec = pl.BlockSpec((1, w), lambda i: (0, 0))
    return pl.pallas_call(
        body, grid=(l // TM_FUSED,),
        in_specs=[row(d), pl.BlockSpec((2 * w, d), lambda i: (0, 0)), row(w), row(w), vec, vec],
        out_specs=[row(w), row(w), vec, vec],
        out_shape=[jax.ShapeDtypeStruct((l, w), F32), jax.ShapeDtypeStruct((l, w), F32),
                   jax.ShapeDtypeStruct((1, w), F32), jax.ShapeDtypeStruct((1, w), F32)],
        name=name, compiler_params=_params(("arbitrary",)),
    )(dx, b, attn, ys, g_attn, g_ssm)


def _rope_tables(l):
    half = ROPE_DIM // 2
    f32 = np.float32
    inv_freq = np.power(f32(ROPE_THETA), -np.arange(half, dtype=f32) / f32(half))
    ang = np.arange(l, dtype=f32)[:, None] * inv_freq[None, :]
    cos, sin = np.cos(ang), np.sin(ang)
    ones = np.ones((l, HEAD_DIM - ROPE_DIM), f32)
    zeros = np.zeros((l, HEAD_DIM - ROPE_DIM), f32)
    zh = np.zeros((l, half), f32)
    c = np.concatenate([cos, cos, ones], axis=1)
    s_lo = np.concatenate([-sin, zh, zeros], axis=1)
    s_hi = np.concatenate([zh, sin, zeros], axis=1)
    return tuple(jnp.asarray(np.tile(t, (1, LANES // HEAD_DIM)), F32) for t in (c, s_lo, s_hi))


def _rope_bwd(dq, dkv, du_ssm, dpre, d_skip, tabs):
    l = dq.shape[0]
    nq = ATTN_WIDTH // LANES

    def body(dq_ref, dkv_ref, du_ref, dpre_ref, ds_ref, c_ref, lo_ref, hi_ref, o_ref):
        c, lo, hi = c_ref[...], lo_ref[...], hi_ref[...]
        for blk in range(nq + 1):
            t = dq_ref[:, blk * LANES:(blk + 1) * LANES] if blk < nq else dkv_ref[:, :KV_WIDTH]
            g = t * c + pltpu.roll(t * lo, 8, 1) + pltpu.roll(t * hi, LANES - 8, 1)
            o_ref[:, blk * LANES:(blk + 1) * LANES] = g.astype(BF16)
        o_ref[:, (nq + 1) * LANES:QKV_WIDTH] = dkv_ref[:, KV_WIDTH:].astype(BF16)
        o_ref[:, QKV_WIDTH:] = (du_ref[...] + dpre_ref[...] * ds_ref[...]).astype(BF16)

    tab = pl.BlockSpec((TM_EW, LANES), lambda i: (i, 0))
    wide = pl.BlockSpec((TM_EW, SSM_WIDTH), lambda i: (i, 0))
    return pl.pallas_call(
        body, grid=(l // TM_EW,),
        in_specs=[wide, pl.BlockSpec((TM_EW, 2 * KV_WIDTH), lambda i: (i, 0)), wide, wide,
                  pl.BlockSpec((1, SSM_WIDTH), lambda i: (0, 0)), tab, tab, tab],
        out_specs=pl.BlockSpec((TM_EW, IN_WIDTH), lambda i: (i, 0)),
        out_shape=jax.ShapeDtypeStruct((l, IN_WIDTH), BF16), name="rope_bwd",
        compiler_params=_params(("parallel",)),
    )(dq, dkv, du_ssm, dpre, d_skip, *tabs)


_Q_COLS = ATTN_WIDTH // LANES
_NEG = -1e30


def _window_specs(nb, width, col):
    return [
        pl.BlockSpec((BLOCK, width), lambda n: (jnp.maximum(n - 1, 0), col)),
        pl.BlockSpec((BLOCK, width), lambda n: (n, col)),
        pl.BlockSpec((BLOCK, width), lambda n: (jnp.minimum(n + 1, nb - 1), col)),
    ]


def _stacked_sink(sink_ref, heads):
    rid = lax.broadcasted_iota(jnp.int32, (len(heads) * BLOCK, 1), 0)
    sk = jnp.full(rid.shape, sink_ref[0, heads[-1]], F32)
    for g in range(len(heads) - 2, -1, -1):
        sk = jnp.where(rid < (g + 1) * BLOCK, sink_ref[0, heads[g]], sk)
    return sk


def _attn_fwd(qkv, sink):
    l = qkv.shape[0]
    nb = l // BLOCK
    grp = N_Q_HEADS // N_KV_HEADS

    def body(sink_ref, q_ref, k0, k1, k2, v0, v1, v2, o_ref, lse_ref):
        n = pl.program_id(0)
        q = q_ref[...]
        kw = jnp.concatenate([k0[...], k1[...], k2[...]], axis=0)
        vw = jnp.concatenate([v0[...], v1[...], v2[...]], axis=0)
        row = lax.broadcasted_iota(jnp.int32, (grp * BLOCK, 3 * BLOCK), 0)
        col = lax.broadcasted_iota(jnp.int32, (grp * BLOCK, 3 * BLOCK), 1)
        valid = jnp.abs(col - BLOCK - (row & (BLOCK - 1))) <= WINDOW
        valid &= jnp.logical_not((n == 0) & (col < BLOCK))
        valid &= jnp.logical_not((n == nb - 1) & (col >= 2 * BLOCK))
        for hk in range(N_KV_HEADS):
            heads = range(hk * grp, (hk + 1) * grp)
            qs = jnp.concatenate([q[:, h * HEAD_DIM:(h + 1) * HEAD_DIM] for h in heads], axis=0)
            kh = kw[:, hk * HEAD_DIM:(hk + 1) * HEAD_DIM]
            vh = vw[:, hk * HEAD_DIM:(hk + 1) * HEAD_DIM]
            s = jnp.where(valid, _dg(qs, kh, NT), _NEG)
            sk = _stacked_sink(sink_ref, heads)
            m = jnp.maximum(jnp.max(s, axis=1, keepdims=True), sk)
            p = jnp.exp(s - m)
            denom = jnp.sum(p, axis=1, keepdims=True) + jnp.exp(sk - m)
            o = _dg((p / denom).astype(BF16), vh, NN)
            lse = m + jnp.log(denom)
            for g, h in enumerate(heads):
                o_ref[:, h * HEAD_DIM:(h + 1) * HEAD_DIM] = o[g * BLOCK:(g + 1) * BLOCK]
                lse_ref[:, h:h + 1] = lse[g * BLOCK:(g + 1) * BLOCK]

    return pl.pallas_call(
        body, grid=(nb,),
        in_specs=[pl.BlockSpec(memory_space=pltpu.SMEM),
                  pl.BlockSpec((BLOCK, ATTN_WIDTH), lambda n: (n, 0))]
        + _window_specs(nb, KV_WIDTH, _Q_COLS) + _window_specs(nb, KV_WIDTH, _Q_COLS + 1),
        out_specs=[pl.BlockSpec((BLOCK, ATTN_WIDTH), lambda n: (n, 0)),
                   pl.BlockSpec((BLOCK, N_Q_HEADS), lambda n: (n, 0))],
        out_shape=[jax.ShapeDtypeStruct((l, ATTN_WIDTH), F32), jax.ShapeDtypeStruct((l, N_Q_HEADS), F32)],
        name="attn_fwd", compiler_params=_params(("parallel",)),
    )(sink, qkv, qkv, qkv, qkv, qkv, qkv, qkv)


def _attn_bwd(qkv, attn, dattn, lse, sink):
    l = qkv.shape[0]
    nb = l // BLOCK
    grp = N_Q_HEADS // N_KV_HEADS
    win = 3 * BLOCK

    def body(sink_ref, q_ref, k0, k1, k2, v0, v1, v2, o_ref, d_ref, l_ref, dq_ref, dkv_ref, dsink_ref, ring_ref):
        n = pl.program_id(0)

        @pl.when(n == 0)
        def _():
            dsink_ref[...] = jnp.zeros_like(dsink_ref)
            ring_ref[...] = jnp.zeros_like(ring_ref)

        @pl.when(n < nb)
        def _():
            first, last = n == 0, n == nb - 1
            cat = lambda a, b, c: jnp.concatenate([a[...], b[...], c[...]], axis=0)
            q, kw, vw = q_ref[...], cat(k0, k1, k2), cat(v0, v1, v2)
            dov = d_ref[...]
            prod = o_ref[...] * dov
            dob = dov.astype(BF16)
            lse = l_ref[...]
            row = lax.broadcasted_iota(jnp.int32, (grp * BLOCK, win), 0)
            col = lax.broadcasted_iota(jnp.int32, (grp * BLOCK, win), 1)
            valid = jnp.abs(col - BLOCK - (row & (BLOCK - 1))) <= WINDOW
            valid &= jnp.logical_not(first & (col < BLOCK))
            valid &= jnp.logical_not(last & (col >= 2 * BLOCK))

            dsink_parts, dks, dvs = [], [], []
            for hk in range(N_KV_HEADS):
                heads = range(hk * grp, (hk + 1) * grp)
                ksl = slice(hk * HEAD_DIM, (hk + 1) * HEAD_DIM)
                hsl = [slice(h * HEAD_DIM, (h + 1) * HEAD_DIM) for h in heads]
                stack = lambda parts: jnp.concatenate(parts, axis=0)
                qs = stack([q[:, s_] for s_ in hsl])
                dos = stack([dob[:, s_] for s_ in hsl])
                deltas = stack([jnp.sum(prod[:, s_], axis=1, keepdims=True) for s_ in hsl])
                lses = stack([lse[:, h:h + 1] for h in heads])
                kh, vh = kw[:, ksl], vw[:, ksl]
                s = jnp.where(valid, _dg(qs, kh, NT), _NEG)
                p = jnp.exp(s - lses)
                dp = _dg(dos, vh, NT)
                ds = (p * (dp - deltas)).astype(BF16)
                dq = _dg(ds, kh, NN) * SCORE_SCALE
                sink_rows = jnp.exp(_stacked_sink(sink_ref, heads) - lses) * deltas
                for g in range(grp):
                    dq_ref[:, hsl[g]] = dq[g * BLOCK:(g + 1) * BLOCK]
                    dsink_parts.append(jnp.sum(sink_rows[g * BLOCK:(g + 1) * BLOCK], axis=0, keepdims=True))
                dks.append(_dg(ds, qs, TN))
                dvs.append(_dg(p.astype(BF16), dos, TN))
            dsink_ref[...] -= jnp.concatenate(dsink_parts, axis=1)
            part = jnp.concatenate(dks + dvs, axis=1)
            ring_ref[(n + 2) % 3] += part[0:BLOCK]
            ring_ref[n % 3] += part[BLOCK:2 * BLOCK]
            ring_ref[(n + 1) % 3] = part[2 * BLOCK:]

        @pl.when(n >= 1)
        def _():
            dkv_ref[...] = ring_ref[(n + 2) % 3]

    centre = lambda n: jnp.minimum(n, nb - 1)
    window = lambda width, col: [
        pl.BlockSpec((BLOCK, width), lambda n: (jnp.maximum(centre(n) - 1, 0), col)),
        pl.BlockSpec((BLOCK, width), lambda n: (centre(n), col)),
        pl.BlockSpec((BLOCK, width), lambda n: (jnp.minimum(centre(n) + 1, nb - 1), col))]
    own = lambda width: pl.BlockSpec((BLOCK, width), lambda n: (centre(n), 0))
    return pl.pallas_call(
        body, grid=(nb + 1,),
        in_specs=[pl.BlockSpec(memory_space=pltpu.SMEM), own(ATTN_WIDTH)]
        + window(KV_WIDTH, _Q_COLS) + window(KV_WIDTH, _Q_COLS + 1)
        + [own(ATTN_WIDTH), own(ATTN_WIDTH), own(N_Q_HEADS)],
        out_specs=[own(ATTN_WIDTH), pl.BlockSpec((BLOCK, 2 * KV_WIDTH), lambda n: (jnp.maximum(n - 1, 0), 0)),
                   pl.BlockSpec((1, N_Q_HEADS), lambda n: (0, 0))],
        out_shape=[jax.ShapeDtypeStruct((l, ATTN_WIDTH), F32), jax.ShapeDtypeStruct((l, 2 * KV_WIDTH), F32),
                   jax.ShapeDtypeStruct((1, N_Q_HEADS), F32)],
        scratch_shapes=[pltpu.VMEM((3, BLOCK, 2 * KV_WIDTH), F32)],
        name="attn_bwd", compiler_params=_params(("arbitrary",)),
    )(sink, qkv, qkv, qkv, qkv, qkv, qkv, qkv, attn, dattn, lse)


def _ssm_disc(a_re, a_im, log_step, b_re, b_im):
    step = jnp.exp(log_step)[..., None]
    mag = jnp.exp(a_re * step)
    lb_re, lb_im = mag * jnp.cos(a_im * step), mag * jnp.sin(a_im * step)
    nr, ni = lb_re - 1.0, lb_im
    den = a_re * a_re + a_im * a_im
    f_re = ((nr * a_re + ni * a_im) / den)[..., None]
    f_im = ((ni * a_re - nr * a_im) / den)[..., None]
    return lb_re, lb_im, f_re * b_re - f_im * b_im, f_re * b_im + f_im * b_re


def _ssm_pack(lb_re, lb_im, bb_re, bb_im, c_re, c_im):
    eye = jnp.eye(SSM_CH // SSM_GROUP, dtype=F32)
    ng = SSM_CH // SSM_GROUP

    def diag_b(bb):
        t = bb.reshape(2, SSM_CB, ng, SSM_STATE, SSM_GROUP)
        return jnp.einsum('dkgpc,gh->dkgchp', t, eye).reshape(2, SSM_CB, SSM_CH, SSM_ST)

    def diag_c(cc):
        t = cc.reshape(2, SSM_CB, ng, SSM_GROUP, SSM_STATE)
        return jnp.einsum('dkgcp,gh->dkhpgc', t, eye).reshape(2, SSM_CB, SSM_ST, SSM_CH)

    bcat = jnp.concatenate([diag_b(bb_re), diag_b(bb_im)], axis=-1)
    ccat = jnp.concatenate([diag_c(c_re), -diag_c(c_im)], axis=-2)
    lam_re = lb_re.reshape(2, SSM_CB, 1, SSM_ST)
    lam_im = lb_im.reshape(2, SSM_CB, 1, SSM_ST)
    return bcat, ccat, lam_re, lam_im


def _ssm_unpack(dbcat, dccat, dlam_re, dlam_im):
    ng = SSM_CH // SSM_GROUP
    eye = jnp.eye(ng, dtype=F32)

    def undiag_b(t):
        t = t.reshape(2, SSM_CB, ng, SSM_GROUP, ng, SSM_STATE)
        return jnp.einsum('dkgchp,gh->dkgpc', t, eye).reshape(2, N_SSM_GROUPS, SSM_STATE, SSM_GROUP)

    def undiag_c(t):
        t = t.reshape(2, SSM_CB, ng, SSM_STATE, ng, SSM_GROUP)
        return jnp.einsum('dkhpgc,gh->dkgcp', t, eye).reshape(2, N_SSM_GROUPS, SSM_GROUP, SSM_STATE)

    dbb_re, dbb_im = undiag_b(dbcat[..., :SSM_ST]), undiag_b(dbcat[..., SSM_ST:])
    dc_re, dc_im = undiag_c(dccat[:, :, :SSM_ST]), -undiag_c(dccat[:, :, SSM_ST:])
    shape = (2, N_SSM_GROUPS, SSM_STATE)
    return dlam_re.reshape(shape), dlam_im.reshape(shape), dbb_re, dbb_im, dc_re, dc_im


def _to_segments(t):
    l, w = t.shape
    return t.reshape(N_SEG, l // N_SEG, w).transpose(1, 0, 2).reshape(l, w)


def _from_segments(t):
    l, w = t.shape
    return t.reshape(l // N_SEG, N_SEG, w).transpose(1, 0, 2).reshape(l, w)


SSM_RC = 256
SSM_JC = SSM_RC // N_SEG
_RE, _IM = pl.ds(0, SSM_ST), pl.ds(SSM_ST, SSM_ST)


def _cfma(ar, ai, xr, xi, br, bi):
    return ar * xr - ai * xi + br, ar * xi + ai * xr + bi


def _chunk_rows(ci, rev, nc):
    start = jnp.where(rev, (nc - 1 - ci) * SSM_RC, ci * SSM_RC)
    return pl.ds(pl.multiple_of(start, SSM_RC), SSM_RC)


def _scan_chunk(src, dst, ar, ai, rev, nj, ci, carry, prev_ref=None):
    def rows_of(staged, j, k):
        at = jnp.where(rev, SSM_JC - 1 - k, k) if staged else j
        return pl.ds(pl.multiple_of(at * N_SEG, N_SEG), N_SEG)

    for k in range(SSM_JC):
        jj = ci * SSM_JC + k
        j = jnp.where(rev, nj - 1 - jj, jj)
        rows = rows_of(src[1], j, k)
        nr, ni = _cfma(ar, ai, carry[0], carry[1], src[0][rows, _RE], src[0][rows, _IM])
        if dst is not None:
            rows = rows_of(dst[1], j, k)
            dst[0][rows, _RE] = nr
            dst[0][rows, _IM] = ni
        if prev_ref is None:
            carry = (nr, ni)
            continue
        jp = jnp.where(rev, j - 1, j + 1)
        if k == SSM_JC - 1:
            inside = jnp.where((jp >= 0) & (jp < nj), 1.0, 0.0)
            jp = jnp.clip(jp, 0, nj - 1)
        prow = pl.ds(pl.multiple_of(jp * N_SEG, N_SEG), N_SEG)
        xr, xi = prev_ref[prow, _RE], prev_ref[prow, _IM]
        sr, si = nr * xr + ni * xi, ni * xr - nr * xi
        if k == SSM_JC - 1:
            sr, si = inside * sr, inside * si
        carry = (nr, ni, carry[2] + sr, carry[3] + si)
    return carry


def _segment_inits(ar, ai, end_r, end_i, rev, nj):
    pr, pi = ar, ai
    for _ in range(int(math.log2(nj))):
        pr, pi = pr * pr - pi * pi, 2.0 * pr * pi
    seg = lax.broadcasted_iota(jnp.int32, end_r.shape, 0)
    zero = jnp.zeros_like(end_r)

    def chain(shift, keep):
        ir, ii = zero, zero
        for _ in range(N_SEG - 1):
            tr, ti = _cfma(pr, pi, ir, ii, end_r, end_i)
            ir = jnp.where(keep, pltpu.roll(tr, shift, 0), 0.0)
            ii = jnp.where(keep, pltpu.roll(ti, shift, 0), 0.0)
        return ir, ii

    up_r, up_i = chain(1, seg >= 1)
    dn_r, dn_i = chain(N_SEG - 1, seg <= N_SEG - 2)
    return jnp.where(rev, dn_r, up_r), jnp.where(rev, dn_i, up_i)


def _ssm_specs(l):
    act = pl.BlockSpec((l, SSM_CH), lambda k, d: (0, k))
    bmat = pl.BlockSpec((None, None, SSM_CH, 2 * SSM_ST), lambda k, d: (d, k, 0, 0))
    cmat = pl.BlockSpec((None, None, 2 * SSM_ST, SSM_CH), lambda k, d: (d, k, 0, 0))
    lam = pl.BlockSpec((None, None, 1, SSM_ST), lambda k, d: (d, k, 0, 0))
    return act, bmat, cmat, lam


def _ssm_fwd(u_seg, bcat, ccat, lam_re, lam_im):
    l = u_seg.shape[0]
    nj = l // N_SEG
    nc = l // SSM_RC

    def body(u_ref, b_ref, c_ref, lr_ref, li_ref, y_ref, ub_ref, keep_ref, xs_ref, stage0, stage1, keep_sem):
        k, d = pl.program_id(0), pl.program_id(1)
        rev = d == 1
        shape = (N_SEG, SSM_ST)
        ar, ai = jnp.broadcast_to(lr_ref[...], shape), jnp.broadcast_to(li_ref[...], shape)
        zero = jnp.zeros(shape, F32)

        def inputs(ci, stage):
            rows = _chunk_rows(ci, rev, nc)
            ub = u_ref[rows, :].astype(BF16)
            ub_ref[rows, :] = ub
            bu = _dg(ub, b_ref[...], NN)
            stage[...] = bu
            xs_ref[rows, :] = bu

        def first(stage, ci, carry):
            return _scan_chunk((stage, True), None, ar, ai, rev, nj, ci, carry)

        def first_pass(t, carry):
            inputs(2 * t + 1, stage1)
            carry = first(stage0, 2 * t, carry)
            inputs(2 * t + 2, stage0)
            return first(stage1, 2 * t + 1, carry)

        inputs(0, stage0)
        carry = lax.fori_loop(0, nc // 2 - 1, first_pass, (zero, zero))
        inputs(nc - 1, stage1)
        carry = first(stage0, nc - 2, carry)
        end_r, end_i = first(stage1, nc - 1, carry)
        init = _segment_inits(ar, ai, end_r, end_i, rev, nj)

        @pl.when(d == 0)
        def _():
            y_ref[...] = jnp.zeros_like(y_ref)

        def outputs(ci):
            rows = _chunk_rows(ci, rev, nc)
            y_ref[rows, :] += _dg(xs_ref[rows, :].astype(BF16), c_ref[...], NN)
            pltpu.make_async_copy(xs_ref.at[rows], keep_ref.at[d, k, rows], keep_sem).start()

        def second(ci, carry):
            return _scan_chunk((xs_ref, False), (xs_ref, False), ar, ai, rev, nj, ci, carry)

        def second_pass(ci, carry):
            outputs(ci - 1)
            return second(ci, carry)

        lax.fori_loop(1, nc, second_pass, second(0, init))
        outputs(nc - 1)
        pltpu.make_async_copy(xs_ref, keep_ref.at[d, k], keep_sem).wait()

    act, bmat, cmat, lam = _ssm_specs(l)
    return pl.pallas_call(
        body, grid=(SSM_CB, 2), in_specs=[act, bmat, cmat, lam, lam], out_specs=[act, act, ANY],
        out_shape=[jax.ShapeDtypeStruct((l, SSM_WIDTH), F32), jax.ShapeDtypeStruct((l, SSM_WIDTH), BF16),
                   jax.ShapeDtypeStruct((2, SSM_CB, l, 2 * SSM_ST), F32)],
        scratch_shapes=[pltpu.VMEM((l, 2 * SSM_ST), F32), pltpu.VMEM((SSM_RC, 2 * SSM_ST), F32),
                        pltpu.VMEM((SSM_RC, 2 * SSM_ST), F32), pltpu.SemaphoreType.DMA],
        name="ssm_fwd", compiler_params=_params(("parallel", "arbitrary"), vmem_mb=56),
    )(u_seg, bcat.astype(BF16), ccat.astype(BF16), lam_re, lam_im)


def _ssm_bwd(u_seg, dy_seg, states, bcat, ccat, lam_re, lam_im):
    l = u_seg.shape[0]
    nj = l // N_SEG
    nc = l // SSM_RC

    def body(u_ref, dy_ref, keep_ref, b_ref, c_ref, lr_ref, li_ref,
             du_ref, db_ref, dc_ref, dlr_ref, dli_ref, xs_ref, gs_ref, dyb_ref, stage0, stage1, keep_sem):
        k, d = pl.program_id(0), pl.program_id(1)
        rev = d == 1
        back = jnp.logical_not(rev)
        shape = (N_SEG, SSM_ST)
        ar, ai = jnp.broadcast_to(lr_ref[...], shape), -jnp.broadcast_to(li_ref[...], shape)
        zero = jnp.zeros(shape, F32)
        fetch = pltpu.make_async_copy(keep_ref.at[d, k], xs_ref, keep_sem)
        fetch.start()

        def inputs(ci, stage):
            rows = _chunk_rows(ci, back, nc)
            dyb = dy_ref[rows, :].astype(BF16)
            dyb_ref[rows, :] = dyb
            dx = _dg(dyb, c_ref[...], NT)
            stage[...] = dx
            gs_ref[rows, :] = dx

        def first(stage, ci, carry):
            return _scan_chunk((stage, True), None, ar, ai, back, nj, ci, carry)

        def first_pass(t, carry):
            inputs(2 * t + 1, stage1)
            carry = first(stage0, 2 * t, carry)
            inputs(2 * t + 2, stage0)
            return first(stage1, 2 * t + 1, carry)

        inputs(0, stage0)
        carry = lax.fori_loop(0, nc // 2 - 1, first_pass, (zero, zero))
        inputs(nc - 1, stage1)
        carry = first(stage0, nc - 2, carry)
        end_r, end_i = first(stage1, nc - 1, carry)
        init = _segment_inits(ar, ai, end_r, end_i, back, nj)
        fetch.wait()
        db_ref[...] = jnp.zeros_like(db_ref)
        dc_ref[...] = jnp.zeros_like(dc_ref)

        @pl.when(d == 0)
        def _():
            du_ref[...] = jnp.zeros_like(du_ref)

        def outputs(ci, stage):
            rows = _chunk_rows(ci, back, nc)
            g = stage[...].astype(BF16)
            dc_ref[...] += _dg(xs_ref[rows, :].astype(BF16), dyb_ref[rows, :], TN)
            db_ref[...] += _dg(u_ref[rows, :], g, TN)
            du_ref[rows, :] += _dg(g, b_ref[...], NT)

        def second(ci, stage, carry):
            return _scan_chunk((gs_ref, False), (stage, True), ar, ai, back, nj, ci, carry, prev_ref=xs_ref)

        def second_pass(t, carry):
            outputs(2 * t, stage0)
            carry = second(2 * t + 1, stage1, carry)
            outputs(2 * t + 1, stage1)
            return second(2 * t + 2, stage0, carry)

        carry = lax.fori_loop(0, nc // 2 - 1, second_pass, second(0, stage0, init + (zero, zero)))
        outputs(nc - 2, stage0)
        gr, gi, acc_r, acc_i = second(nc - 1, stage1, carry)
        outputs(nc - 1, stage1)

        seg = lax.broadcasted_iota(jnp.int32, shape, 0)
        jb = jnp.where(rev, nj - 1, 0)
        erow = pl.ds(pl.multiple_of((nj - 1 - jb) * N_SEG, N_SEG), N_SEG)

        def before(t):
            up = jnp.where(seg >= 1, pltpu.roll(t, 1, 0), 0.0)
            down = jnp.where(seg <= N_SEG - 2, pltpu.roll(t, N_SEG - 1, 0), 0.0)
            return jnp.where(rev, down, up)

        init_r, init_i = before(xs_ref[erow, _RE]), before(xs_ref[erow, _IM])
        acc_r = acc_r + gr * init_r + gi * init_i
        acc_i = acc_i + gi * init_r - gr * init_i
        dlr_ref[...] = jnp.sum(acc_r, axis=0, keepdims=True)
        dli_ref[...] = jnp.sum(acc_i, axis=0, keepdims=True)

    act, bmat, cmat, lam = _ssm_specs(l)
    return pl.pallas_call(
        body, grid=(SSM_CB, 2), in_specs=[act, act, ANY, bmat, cmat, lam, lam],
        out_specs=[act, bmat, cmat, lam, lam],
        out_shape=[jax.ShapeDtypeStruct((l, SSM_WIDTH), F32),
                   jax.ShapeDtypeStruct(bcat.shape, F32), jax.ShapeDtypeStruct(ccat.shape, F32),
                   jax.ShapeDtypeStruct(lam_re.shape, F32), jax.ShapeDtypeStruct(lam_im.shape, F32)],
        scratch_shapes=[pltpu.VMEM((l, 2 * SSM_ST), F32), pltpu.VMEM((l, 2 * SSM_ST), F32),
                        pltpu.VMEM((l, SSM_CH), BF16),
                        pltpu.VMEM((SSM_RC, 2 * SSM_ST), F32), pltpu.VMEM((SSM_RC, 2 * SSM_ST), F32),
                        pltpu.SemaphoreType.DMA],
        name="ssm_bwd", compiler_params=_params(("parallel", "arbitrary"), vmem_mb=58),
    )(u_seg, dy_seg, states, bcat.astype(BF16), ccat.astype(BF16), lam_re, lam_im)


def _glu_fwd(y_ssm, u, d_skip, w_glu):
    l, w = u.shape

    def body(y_ref, u_ref, d_ref, w_ref, pre_ref, s_ref, ys_ref):
        pre = y_ref[...] + d_ref[...] * u_ref[...]
        z = _gelu(pre)
        s = _dg(z.astype(BF16), w_ref[...], NN)
        pre_ref[...] = pre
        s_ref[...] = s
        ys_ref[...] = z * _sigmoid(s)

    row = pl.BlockSpec((TM_EW, w), lambda i: (i, 0))
    out = jax.ShapeDtypeStruct((l, w), F32)
    return pl.pallas_call(
        body, grid=(l // TM_EW,),
        in_specs=[row, row, pl.BlockSpec((1, w), lambda i: (0, 0)), pl.BlockSpec((w, w), lambda i: (0, 0))],
        out_specs=[row, row, row], out_shape=[out, out, out], name="glu_fwd",
        compiler_params=_params(("parallel",)),
    )(y_ssm, u, d_skip, w_glu)


def _glu_bwd(pre, s, dys, u, d_skip, w_glu):
    l, w = u.shape

    def body(pre_ref, s_ref, dys_ref, u_ref, d_ref, w_ref, dpre_ref, z_ref, ds_ref, dd_ref):
        pre, dys = pre_ref[...], dys_ref[...]
        z = _gelu(pre)
        sig = _sigmoid(s_ref[...])
        ds = (dys * z * sig * (1.0 - sig)).astype(BF16)
        dz = dys * sig + _dg(ds, w_ref[...], NT)
        dpre = dz * _gelu_grad(pre)
        dpre_ref[...] = dpre
        z_ref[...] = z.astype(BF16)
        ds_ref[...] = ds

        @pl.when(pl.program_id(0) == 0)
        def _():
            dd_ref[...] = jnp.zeros_like(dd_ref)

        dd_ref[...] += jnp.sum(dpre * u_ref[...], axis=0, keepdims=True)

    row = pl.BlockSpec((TM_EW, w), lambda i: (i, 0))
    vec = pl.BlockSpec((1, w), lambda i: (0, 0))
    return pl.pallas_call(
        body, grid=(l // TM_EW,),
        in_specs=[row, row, row, row, vec, pl.BlockSpec((w, w), lambda i: (0, 0))],
        out_specs=[row, row, row, vec],
        out_shape=[jax.ShapeDtypeStruct((l, w), F32), jax.ShapeDtypeStruct((l, w), BF16),
                   jax.ShapeDtypeStruct((l, w), BF16), jax.ShapeDtypeStruct((1, w), F32)],
        name="glu_bwd", compiler_params=_params(("arbitrary",)),
    )(pre, s, dys, u, d_skip, w_glu)


TM_CV = 512
TC_CV = 256
TM_CF = 256
TC_CF = D_FF // 2
HALO = SUBLANES


def _conv_specs(l, col0, tm=TM_CV, tc=TC_CV):
    per = tm // HALO
    nh = l // HALO
    off = col0 // tc
    return [
        pl.BlockSpec((HALO, tc), lambda j, i: (jnp.maximum(i * per - 1, 0), j + off)),
        pl.BlockSpec((tm, tc), lambda j, i: (i, j + off)),
        pl.BlockSpec((HALO, tc), lambda j, i: (jnp.minimum((i + 1) * per, nh - 1), j + off)),
    ]


def _ext(prev_ref, mid_ref, next_ref, first, last):
    p = jnp.where(first, 0.0, prev_ref[...])
    n = jnp.where(last, 0.0, next_ref[...])
    return jnp.concatenate([p, mid_ref[...], n], axis=0)


def _shift_dn(t):
    return pltpu.roll(t, 1, 0)


def _shift_up(t):
    return pltpu.roll(t, t.shape[0] - 1, 0)


def _conv3(e, w_ref, b_ref):
    return w_ref[0:1, :] * _shift_dn(e) + w_ref[1:2, :] * e + w_ref[2:3, :] * _shift_up(e) + b_ref[...]


def _convffn_fwd(up_pre, conv_w, conv_b):
    l = up_pre.shape[0]
    tm, tc = TM_CF, TC_CF
    ni = l // tm
    wspec = lambda off: pl.BlockSpec((3, tc), lambda j, i: (0, j + off))
    bspec = lambda off: pl.BlockSpec((1, tc), lambda j, i: (0, j + off))
    voff = D_FF // tc

    def body(gp, gm, gn, vp, vm, vn, wg, bg, wv, bv, o_ref):
        i = pl.program_id(1)
        first, last = i == 0, i == ni - 1
        gate = _conv3(_ext(gp, gm, gn, first, last), wg, bg)[HALO:HALO + tm]
        val = _conv3(_ext(vp, vm, vn, first, last), wv, bv)[HALO:HALO + tm]
        o_ref[...] = (gate * _sigmoid(gate) * val).astype(BF16)

    return pl.pallas_call(
        body, grid=(D_FF // tc, ni),
        in_specs=_conv_specs(l, 0, tm, tc) + _conv_specs(l, D_FF, tm, tc)
        + [wspec(0), bspec(0), wspec(voff), bspec(voff)],
        out_specs=pl.BlockSpec((tm, tc), lambda j, i: (i, j)),
        out_shape=jax.ShapeDtypeStruct((l, D_FF), BF16), name="convffn_fwd",
        compiler_params=_params(("parallel", "parallel")),
    )(up_pre, up_pre, up_pre, up_pre, up_pre, up_pre, conv_w, conv_b, conv_w, conv_b)


HALO_B = 2 * SUBLANES


def _convffn_bwd(up_pre, dx2b, w_down, conv_w, conv_b):
    l = up_pre.shape[0]
    ni = l // TM_CV
    d = dx2b.shape[1]
    wspec = lambda off: pl.BlockSpec((3, TC_CV), lambda i, j: (0, j + off))
    bspec = lambda off: pl.BlockSpec((1, TC_CV), lambda i, j: (0, j + off))
    voff = D_FF // TC_CV
    swap = lambda spec: pl.BlockSpec(spec.block_shape, lambda i, j, f=spec.index_map: f(j, i))
    per, nh = TM_CV // HALO_B, l // HALO_B
    dx_specs = [pl.BlockSpec((HALO_B, d), lambda i, j: (jnp.maximum(i * per - 1, 0), 0)),
                pl.BlockSpec((TM_CV, d), lambda i, j: (i, 0)),
                pl.BlockSpec((HALO_B, d), lambda i, j: (jnp.minimum((i + 1) * per, nh - 1), 0))]

    def body(gp, gm, gn, vp, vm, vn, xp, xm, xn, wd, wg, bg, wv, bv, dup_ref, pg_ref, pv_ref):
        i = pl.program_id(0)
        first, last = i == 0, i == ni - 1
        ge, ve = _ext(gp, gm, gn, first, last), _ext(vp, vm, vn, first, last)
        zero = jnp.zeros((HALO_B, d), BF16)
        dx = jnp.concatenate([jnp.where(first, zero, xp[...]), xm[...], jnp.where(last, zero, xn[...])], axis=0)
        de = _dg(dx, wd[...], NT)[HALO_B - HALO:HALO_B + TM_CV + HALO]
        taps = [(_shift_dn(e), e, _shift_up(e)) for e in (ge, ve)]
        conv = lambda t, w_ref, b_ref: w_ref[0:1, :] * t[0] + w_ref[1:2, :] * t[1] + w_ref[2:3, :] * t[2] + b_ref[...]
        gate, val = conv(taps[0], wg, bg), conv(taps[1], wv, bv)
        sig = _sigmoid(gate)
        silu = gate * sig
        dgate = de * val * (sig + silu * (1.0 - sig))
        dval = de * silu
        mid = slice(HALO, HALO + TM_CV)
        rid = lax.broadcasted_iota(jnp.int32, (SUBLANES, TC_CV), 0)
        for half, (dup, tap, w_ref, p_ref) in enumerate(((dgate, taps[0], wg, pg_ref), (dval, taps[1], wv, pv_ref))):
            dpre = w_ref[0:1, :] * _shift_up(dup) + w_ref[1:2, :] * dup + w_ref[2:3, :] * _shift_dn(dup)
            dup_ref[half] = dpre[mid].astype(BF16)
            dm_ = dup[mid]
            sums = [jnp.sum(dm_ * t[mid], axis=0, keepdims=True) for t in tap]
            sums.append(jnp.sum(dm_, axis=0, keepdims=True))
            acc = jnp.zeros((SUBLANES, TC_CV), F32)
            for k, sk in enumerate(sums):
                acc = jnp.where(rid == k, sk, acc)
            p_ref[...] = acc

    par = pl.BlockSpec((None, SUBLANES, TC_CV), lambda i, j: (i, 0, j))
    dup, pg, pv = pl.pallas_call(
        body, grid=(ni, D_FF // TC_CV),
        in_specs=[swap(s) for s in _conv_specs(l, 0) + _conv_specs(l, D_FF)] + dx_specs
        + [pl.BlockSpec((TC_CV, d), lambda i, j: (j, 0)), wspec(0), bspec(0), wspec(voff), bspec(voff)],
        out_specs=[pl.BlockSpec((2, TM_CV, TC_CV), lambda i, j: (0, i, j)), par, par],
        out_shape=[jax.ShapeDtypeStruct((2, l, D_FF), BF16),
                   jax.ShapeDtypeStruct((ni, SUBLANES, D_FF), F32), jax.ShapeDtypeStruct((ni, SUBLANES, D_FF), F32)],
        name="convffn_bwd", compiler_params=_params(("parallel", "parallel")),
    )(up_pre, up_pre, up_pre, up_pre, up_pre, up_pre, dx2b, dx2b, dx2b, w_down, conv_w, conv_b, conv_w, conv_b)
    return dup, jnp.concatenate([jnp.sum(pg, axis=0), jnp.sum(pv, axis=0)], axis=1)


def _local_step(x, target, wb, sp, mixer_weights=None, late_weights=None, grads_ready=None,
                grads_next=None):
    l = x.shape[0]
    tabs = _rope_tables(l)
    disc = _ssm_disc(sp["a_re"], sp["a_im"], sp["log_step"], sp["b_re"], sp["b_im"])
    bcat, ccat, lam_re, lam_im = _ssm_pack(*disc, sp["c_re"], sp["c_im"])
    d_skip = sp["d_skip"].reshape(1, SSM_WIDTH)

    h, qkv, u = _rms_mm_rope(x, sp["norm_mix_g"], wb["w_in"], tabs, "mm_in")
    attn, lse = _attn_fwd(qkv, sp["sink"])
    y_seg, u_seg, states = _ssm_fwd(_to_segments(u), bcat, ccat, lam_re, lam_im)
    y_ssm = _from_segments(y_seg)
    if mixer_weights is not None:
        wb = dict(wb, **mixer_weights(attn))
    pre, s_glu, ys = _glu_fwd(y_ssm, u, d_skip, wb["w_glu"])
    mixed, x1, h2 = _mix_mm_res_rms(attn, ys, sp["norm_attn_g"], sp["norm_ssm_g"], wb["w_out"], x,
                                    sp["norm_ffn_g"], "mm_out")
    if late_weights is not None:
        wb = dict(wb, **late_weights(h2))
    up_pre = _mm_nn_cols(h2, wb["w_up"], min(l, 1024), "mm_up")
    conv_w = wb["conv_w"]
    act = _convffn_fwd(up_pre, conv_w, sp["conv_b"])
    loss, dx2, dx2b, d_final_g = _mm_res_loss(act, wb["w_down"], x1, sp["norm_final_g"].reshape(1, D_MODEL), target)

    g = {"norm_final_g": d_final_g.reshape(D_MODEL)}
    g["w_down"] = _mm_tn(act, dx2b, D_FF // 2, 512, "mm_down_dw")
    dup_pre, conv_par = _convffn_bwd(up_pre, dx2b, wb["w_down"], conv_w, sp["conv_b"])
    g["conv_w"], g["conv_b"] = conv_par[0:3], conv_par[3:4]
    g["w_up"] = _mm_tn_cols(h2, dup_pre, wb["w_up"].shape[0], 512, "mm_up_dw")
    dx1, dx1b, g["norm_ffn_g"] = _mm_cols_rms_bwd(dup_pre, wb["w_up"], x1, sp["norm_ffn_g"], dx2, "mm_up_dx")
    g["w_out"] = _mm_tn(mixed, dx1b, 1024, 1024, "mm_out_dw")
    zero = grads_ready(g["w_up"], g["w_down"], g["w_out"]) if grads_ready is not None else 0.0
    dattn, dys, g["norm_attn_g"], g["norm_ssm_g"] = _mm_mix_bwd(
        dx1b, wb["w_out"], attn, ys, sp["norm_attn_g"] + zero, sp["norm_ssm_g"], "mm_out_dx")
    dpre, zb, dsb, dd = _glu_bwd(pre, s_glu, dys, u, d_skip, wb["w_glu"])
    g["d_skip"] = dd.reshape(N_SSM_GROUPS, SSM_GROUP)
    g["w_glu"] = _mm_tn(zb, dsb, 512, 512, "mm_glu_dw")
    zero = grads_next(g["w_glu"]) if grads_next is not None else 0.0
    du_seg, dbcat, dccat, dlam_re, dlam_im = _ssm_bwd(u_seg, _to_segments(dpre), states, bcat, ccat,
                                                      lam_re + zero, lam_im)
    dlb_re, dlb_im, dbb_re, dbb_im, g["c_re"], g["c_im"] = _ssm_unpack(dbcat, dccat, dlam_re, dlam_im)
    _, disc_vjp = jax.vjp(_ssm_disc, sp["a_re"], sp["a_im"], sp["log_step"], sp["b_re"], sp["b_im"])
    g["a_re"], g["a_im"], g["log_step"], g["b_re"], g["b_im"] = disc_vjp((dlb_re, dlb_im, dbb_re, dbb_im))
    dq, dkv, g["sink"] = _attn_bwd(qkv, attn, dattn, lse, sp["sink"])
    dproj = _rope_bwd(dq, dkv, _from_segments(du_seg), dpre, d_skip, tabs)
    g["w_in"] = _mm_tn(dproj, h, IN_WIDTH // 5, D_MODEL, "mm_in_dw")
    grad_x, _, g["norm_mix_g"] = _mm_nn_rms_bwd(dproj, wb["w_in"], x, sp["norm_mix_g"], dx1, "mm_in_dx")
    return loss, grad_x, g


MESH = pl.DeviceIdType.MESH
ANY = pl.BlockSpec(memory_space=pl.ANY)


def _place():
    x, y, c = lax.axis_index("x"), lax.axis_index("y"), lax.axis_index("c")
    chips = [(1 - x, y), (x, 1 - y), (1 - x, 1 - y)]
    return x, y, c, chips


def _chip_index(px, py):
    return 2 * px + py


CHUNK_BYTES = 256 * 1024
MAX_CHUNKS = 16


def _row_chunks(rows, row_bytes, align):
    n = max(1, min(MAX_CHUNKS, (rows * row_bytes) // CHUNK_BYTES))
    per = -(-rows // n)
    per = -(-per // align) * align
    return [(r0, min(per, rows - r0)) for r0 in range(0, rows, per)]


def _align_of(dtype):
    return SUBLANES * 4 // jnp.dtype(dtype).itemsize


def _remote(src, dst, send_sem, recv_sem, to):
    return pltpu.make_async_remote_copy(src_ref=src, dst_ref=dst, send_sem=send_sem, recv_sem=recv_sem,
                                        device_id=to, device_id_type=MESH)


CAST_ROWS = 64


def _gather_weights(shards, dtypes):
    nw = len(shards)

    def body(*refs):
        w_refs, o_refs = refs[:nw], refs[nw:2 * nw]
        send_sems, recv_sems, in_sems, out_sems = refs[2 * nw:2 * nw + 4]
        raw, cast = refs[2 * nw + 4:3 * nw + 4], refs[3 * nw + 4:]
        x, y, c, chips = _place()
        mine = _chip_index(x, y)
        sibling = (x, y, 1 - c)

        def rows_of(ref, chip, r0, nr):
            return ref.at[chip, pl.ds(r0, nr), :]

        def copy(wi, k, src, dst, to):
            return _remote(src, dst, send_sems.at[wi, k], recv_sems.at[wi, k], to)

        geo = []
        for wi in range(nw):
            rows, cols = w_refs[wi].shape
            row_bytes = cols * jnp.dtype(dtypes[wi]).itemsize
            geo.append((rows // 2, _row_chunks(rows // 2, row_bytes, _align_of(dtypes[wi]))))

        stage_in = [pltpu.make_async_copy(w_refs[wi], raw[wi], in_sems.at[wi]) for wi in range(nw)]
        for cp in stage_in:
            cp.start()
        staged = [raw[wi] if dtypes[wi] == w_refs[wi].dtype else cast[wi] for wi in range(nw)]
        stage_out = []
        for wi in range(nw):
            stage_in[wi].wait()
            if staged[wi] is not raw[wi]:
                def cast_rows(i, _, wi=wi):
                    rows = pl.ds(pl.multiple_of(i * CAST_ROWS, CAST_ROWS), CAST_ROWS)
                    cast[wi][rows, :] = raw[wi][rows, :].astype(dtypes[wi])
                    return 0

                lax.fori_loop(0, w_refs[wi].shape[0] // CAST_ROWS, cast_rows, 0)
            cp = pltpu.make_async_copy(staged[wi], o_refs[wi].at[mine], out_sems.at[wi])
            cp.start()
            stage_out.append(cp)

        for wi in range(nw):
            hr, half_chunks = geo[wi]
            for j, chip in enumerate(chips):
                for r0, nr in half_chunks:
                    copy(wi, j, staged[wi].at[pl.ds(c * hr + r0, nr), :],
                         rows_of(o_refs[wi], mine, c * hr + r0, nr), (*chip, c)).start()
        for wi in range(nw):
            hr, half_chunks = geo[wi]
            for j, chip in enumerate(chips):
                got = rows_of(o_refs[wi], _chip_index(*chip), c * hr, hr)
                copy(wi, j, got, got, (*chip, c)).wait_recv()
                for r0, nr in half_chunks:
                    piece = rows_of(o_refs[wi], _chip_index(*chip), c * hr + r0, nr)
                    copy(wi, 3 + j, piece, piece, sibling).start()
        for wi in range(nw):
            hr = geo[wi][0]
            for j, chip in enumerate(chips):
                got = rows_of(o_refs[wi], _chip_index(*chip), (1 - c) * hr, hr)
                copy(wi, 3 + j, got, got, sibling).wait_recv()
        for wi in range(nw):
            hr = geo[wi][0]
            sent = rows_of(o_refs[wi], mine, c * hr, hr)
            for k in range(6):
                copy(wi, k, sent, sent, sibling).wait_send()
            stage_out[wi].wait()

    return pl.pallas_call(
        body, in_specs=[ANY] * nw, out_specs=[ANY] * nw,
        out_shape=[jax.ShapeDtypeStruct((4, *s.shape), t) for s, t in zip(shards, dtypes)],
        scratch_shapes=[pltpu.SemaphoreType.DMA((nw, 6)), pltpu.SemaphoreType.DMA((nw, 6)),
                        pltpu.SemaphoreType.DMA((nw,)), pltpu.SemaphoreType.DMA((nw,))]
        + [pltpu.VMEM(s.shape, s.dtype) for s in shards] + [pltpu.VMEM(s.shape, t) for s, t in zip(shards, dtypes)],
        name="gather_weights", compiler_params=_params(vmem_mb=40),
    )(*shards)


HBM = pl.BlockSpec(memory_space=pltpu.HBM)
SEM = pl.BlockSpec(memory_space=pltpu.SEMAPHORE)
EFFECT = pltpu.SideEffectType.DATAFLOW_SIDE_EFFECTING


def _cast_place(w, place, dtype, after, name):
    rows, cols = w.shape
    tr = _row_tile(rows, cols, _align_of(dtype))

    def body(p_ref, w_ref, after_ref, o_ref):
        del p_ref, after_ref
        o_ref[...] = w_ref[...].astype(dtype)

    grid_spec = pltpu.PrefetchScalarGridSpec(
        num_scalar_prefetch=1, grid=(rows // tr,),
        in_specs=[pl.BlockSpec((tr, cols), lambda i, p: (i, 0)), ANY],
        out_specs=pl.BlockSpec((None, tr, cols), lambda i, p: (p[1], i, 0)))
    return pl.pallas_call(body, grid_spec=grid_spec, out_shape=jax.ShapeDtypeStruct((4, rows, cols), dtype),
                          name=name, compiler_params=_params(("parallel",)))(place, w, after)


def _split_start(name, arrays, n_pairs, issue):
    n = len(arrays)

    def body(*refs):
        issue(refs[:n], refs[n:n + n_pairs], refs[n + n_pairs:n + 2 * n_pairs])
        token = refs[2 * n + 2 * n_pairs]
        token[...] = jnp.zeros_like(token)

    dma = pltpu.SemaphoreType.DMA(())
    outs = pl.pallas_call(
        body, name=name,
        out_shape=[dma] * (2 * n_pairs) + [pltpu.HBM(t.shape, t.dtype) for t in arrays]
        + [jax.ShapeDtypeStruct((SUBLANES, LANES), F32)],
        in_specs=[HBM] * n, out_specs=[SEM] * (2 * n_pairs) + [HBM] * n + [pl.BlockSpec(memory_space=pltpu.VMEM)],
        input_output_aliases={a: 2 * n_pairs + a for a in range(n)},
        compiler_params=pltpu.CompilerParams(has_side_effects=EFFECT),
    )(*[pltpu.with_memory_space_constraint(t, pltpu.HBM) for t in arrays])
    return outs[:n_pairs], outs[n_pairs:2 * n_pairs], outs[2 * n_pairs:2 * n_pairs + n], outs[-1]


def _split_wait(name, send_sems, recv_sems, flying, sizes, after):
    n, n_pairs = len(flying), len(send_sems)

    def body(*refs):
        x, y, c, _ = _place()
        for k, ref in enumerate(sizes(refs[:n])):
            cp = _remote(ref, ref, refs[n + k], refs[n + n_pairs + k], (x, y, 1 - c))
            cp.wait_send()
            cp.wait_recv()

    return pl.pallas_call(
        body, name=name, out_shape=[pltpu.HBM(t.shape, t.dtype) for t in flying],
        in_specs=[HBM] * n + [SEM] * (2 * n_pairs) + [ANY], out_specs=[HBM] * n,
        input_output_aliases={a: a for a in range(n)},
        compiler_params=pltpu.CompilerParams(has_side_effects=EFFECT),
    )(*flying, *send_sems, *recv_sems, after)


def _spread_start(lands, name):
    def issue(land_refs, send_sems, recv_sems):
        x, y, c, chips = _place()
        mine = _chip_index(x, y)
        for a, land in enumerate(land_refs):
            _, rows, cols = land.shape
            hr = rows // 2
            row_bytes = cols * jnp.dtype(land.dtype).itemsize
            for r0, nr in _row_chunks(hr, row_bytes, _align_of(land.dtype)):
                piece = land.at[mine, pl.ds(c * hr + r0, nr), :]
                for chip in chips:
                    for core in (0, 1):
                        _remote(piece, piece, send_sems[a], recv_sems[a], (*chip, core)).start()

    return _split_start(name, lands, len(lands), issue)


def _spread_wait(send_sems, recv_sems, flying, after, name):
    return _split_wait(name, send_sems, recv_sems, flying, lambda refs: [r.at[pl.ds(0, 3)] for r in refs], after)


def _pair_start(grads):
    n = len(grads)
    zones = [lax.empty((4, g.shape[1] // 2, g.shape[2]), F32) for g in grads]

    def issue(refs, send_sems, recv_sems):
        x, y, c, _ = _place()
        for a in range(n):
            g_ref, z_ref = refs[a], refs[n + a]
            _, rows, cols = g_ref.shape
            hr = rows // 2
            for k in range(4):
                for r0, nr in _row_chunks(hr, cols * 4, SUBLANES):
                    _remote(g_ref.at[k, pl.ds((1 - c) * hr + r0, nr), :], z_ref.at[k, pl.ds(r0, nr), :],
                            send_sems[a], recv_sems[a], (x, y, 1 - c)).start()

    return _split_start("pair_start", list(grads) + zones, n, issue)


def _pair_wait(send_sems, recv_sems, flying, after):
    n = len(flying) // 2
    out = _split_wait("pair_wait", send_sems, recv_sems, flying, lambda refs: list(refs[n:]), after)
    return out[:n], out[n:]


def _chip_start(sums):
    n = len(sums)
    zones = [lax.empty((3, *s.shape[1:]), s.dtype) for s in sums]

    def issue(refs, send_sems, recv_sems):
        x, y, c, chips = _place()
        for a in range(n):
            s_ref, z_ref = refs[a], refs[n + a]
            _, rows, cols = s_ref.shape
            row_bytes = cols * jnp.dtype(s_ref.dtype).itemsize
            for r0, nr in _row_chunks(rows, row_bytes, _align_of(s_ref.dtype)):
                for j, chip in enumerate(chips):
                    _remote(s_ref.at[_chip_index(*chip), pl.ds(r0, nr), :], z_ref.at[j, pl.ds(r0, nr), :],
                            send_sems[a], recv_sems[a], (*chip, c)).start()

    return _split_start("chip_start", list(sums) + zones, n, issue)


def _chip_wait(send_sems, recv_sems, flying, after):
    n = len(flying) // 2
    return _split_wait("chip_wait", send_sems, recv_sems, flying, lambda refs: list(refs[n:]), after)[n:]


def _pair_exchange(grads):
    na = len(grads)

    def body(*refs):
        g_refs, o_refs = refs[:na], refs[na:2 * na]
        send_sems, recv_sems = refs[2 * na:]
        x, y, c, _ = _place()
        sibling = (x, y, 1 - c)
        for ai in range(na):
            _, rows, cols = g_refs[ai].shape
            hr = rows // 2
            for k in range(4):
                for r0, nr in _row_chunks(hr, cols * 4, SUBLANES):
                    _remote(g_refs[ai].at[k, pl.ds((1 - c) * hr + r0, nr), :], o_refs[ai].at[k, pl.ds(r0, nr), :],
                            send_sems.at[ai], recv_sems.at[ai], sibling).start()
        for ai in range(na):
            _remote(o_refs[ai], o_refs[ai], send_sems.at[ai], recv_sems.at[ai], sibling).wait()

    return pl.pallas_call(
        body, in_specs=[ANY] * na, out_specs=[ANY] * na,
        out_shape=[jax.ShapeDtypeStruct((4, g.shape[1] // 2, g.shape[2]), F32) for g in grads],
        scratch_shapes=[pltpu.SemaphoreType.DMA((na,)), pltpu.SemaphoreType.DMA((na,))],
        name="pair_exchange",
    )(*grads)


def _row_tile(rows, cols, align, elems=256 * 1024):
    best = align
    for cand in range(align, rows + 1, align):
        if rows % cand == 0 and cand * cols <= elems:
            best = cand
    return best


def _pair_sum(g, got, place, transit, name):
    _, rows, cols = g.shape
    hr = rows // 2
    tr = _row_tile(hr, cols, _align_of(transit), 512 * 1024)
    nt = hr // tr

    def body(p_ref, g_ref, r_ref, s_ref, own_ref):
        total = g_ref[...] + r_ref[...]
        s_ref[...] = total.astype(transit)

        @pl.when(pl.program_id(1) == p_ref[1])
        def _():
            own_ref[...] = total

    grid_spec = pltpu.PrefetchScalarGridSpec(
        num_scalar_prefetch=1, grid=(nt, 4),
        in_specs=[pl.BlockSpec((None, tr, cols), lambda i, k, p: (k, p[0] * nt + i, 0)),
                  pl.BlockSpec((None, tr, cols), lambda i, k, p: (k, i, 0))],
        out_specs=[pl.BlockSpec((None, tr, cols), lambda i, k, p: (k, i, 0)),
                   pl.BlockSpec((tr, cols), lambda i, k, p: (i, 0))])
    return pl.pallas_call(
        body, grid_spec=grid_spec,
        out_shape=[jax.ShapeDtypeStruct((4, hr, cols), transit), jax.ShapeDtypeStruct((hr, cols), F32)],
        name=name, compiler_params=_params(("parallel", "arbitrary")),
    )(place, g, got)


def _chip_exchange(sums):
    na = len(sums)

    def body(*refs):
        s_refs, o_refs = refs[:na], refs[na:2 * na]
        send_sems, recv_sems = refs[2 * na:]
        x, y, c, chips = _place()
        for ai in range(na):
            _, rows, cols = s_refs[ai].shape
            row_bytes = cols * jnp.dtype(s_refs[ai].dtype).itemsize
            for r0, nr in _row_chunks(rows, row_bytes, _align_of(s_refs[ai].dtype)):
                for j, chip in enumerate(chips):
                    _remote(s_refs[ai].at[_chip_index(*chip), pl.ds(r0, nr), :], o_refs[ai].at[j, pl.ds(r0, nr), :],
                            send_sems.at[ai, j], recv_sems.at[ai, j], (*chip, c)).start()
        for ai in range(na):
            for j, chip in enumerate(chips):
                _remote(o_refs[ai].at[j], o_refs[ai].at[j], send_sems.at[ai, j], recv_sems.at[ai, j],
                        (*chip, c)).wait()

    return pl.pallas_call(
        body, in_specs=[ANY] * na, out_specs=[ANY] * na,
        out_shape=[jax.ShapeDtypeStruct((3, *s.shape[1:]), s.dtype) for s in sums],
        scratch_shapes=[pltpu.SemaphoreType.DMA((na, 3)), pltpu.SemaphoreType.DMA((na, 3))],
        name="chip_exchange",
    )(*sums)


def _chip_sum(own, landed, name):
    hr, cols = own.shape
    tr = _row_tile(hr, cols, _align_of(landed.dtype))

    def body(o_ref, l_ref, f_ref):
        acc = o_ref[...]
        for j in range(3):
            acc = acc + l_ref[j].astype(F32)
        f_ref[...] = acc

    return pl.pallas_call(
        body, grid=(hr // tr,),
        in_specs=[pl.BlockSpec((tr, cols), lambda i: (i, 0)), pl.BlockSpec((3, tr, cols), lambda i: (0, i, 0))],
        out_specs=pl.BlockSpec((tr, cols), lambda i: (i, 0)),
        out_shape=jax.ShapeDtypeStruct((hr, cols), F32), name=name,
        compiler_params=_params(("parallel",)),
    )(own, landed)


def _final_exchange(halves, small):
    nh = len(halves)

    def body(*refs):
        h_refs, s_ref = refs[:nh], refs[nh]
        o_refs, so_ref = refs[nh + 1:2 * nh + 1], refs[2 * nh + 1]
        send_sems, recv_sems, local_sem, ssend_sems, srecv_sems = refs[2 * nh + 2:]
        x, y, c, _ = _place()
        me = 4 * x + 2 * y + c
        sibling = (x, y, 1 - c)
        for hi in range(nh):
            hr, cols = h_refs[hi].shape
            for r0, nr in _row_chunks(hr, cols * 4, SUBLANES):
                _remote(h_refs[hi].at[pl.ds(r0, nr), :], o_refs[hi].at[pl.ds(r0, nr), :],
                        send_sems.at[hi], recv_sems.at[hi], sibling).start()
        small_cps = [pltpu.make_async_copy(s_ref, so_ref.at[me], local_sem)]
        for r in range(1, 8):
            fx, fy, fc = (r >> 2) & 1, (r >> 1) & 1, r & 1
            peer = (1 - x if fx else x, 1 - y if fy else y, 1 - c if fc else c)
            small_cps.append(_remote(s_ref, so_ref.at[me], ssend_sems.at[r - 1], srecv_sems.at[r - 1], peer))
        for cp in small_cps:
            cp.start()
        for hi in range(nh):
            _remote(h_refs[hi], o_refs[hi], send_sems.at[hi], recv_sems.at[hi], sibling).wait()
        for cp in small_cps:
            cp.wait()

    return pl.pallas_call(
        body, in_specs=[ANY] * (nh + 1), out_specs=[ANY] * (nh + 1),
        out_shape=[jax.ShapeDtypeStruct(h.shape, F32) for h in halves]
        + [jax.ShapeDtypeStruct((8, *small.shape), F32)],
        scratch_shapes=[pltpu.SemaphoreType.DMA((nh,)), pltpu.SemaphoreType.DMA((nh,)),
                        pltpu.SemaphoreType.DMA, pltpu.SemaphoreType.DMA((7,)), pltpu.SemaphoreType.DMA((7,))],
        name="final_exchange",
    )(*halves, small)


def _adamw_halves(w, own, other, m, v, place, name):
    r, c = w.shape
    hr = r // 2
    tr = _row_tile(hr, c, SUBLANES, 384 * 1024)
    nt = hr // tr
    c1 = 1.0 - ADAM_B1 ** ADAM_STEP
    c2 = 1.0 - ADAM_B2 ** ADAM_STEP

    def body(p_ref, w_ref, own_ref, other_ref, m_ref, v_ref, g_ref, d_ref, nm_ref, nv_ref):
        mine = pl.program_id(0) // nt == p_ref[0]
        gv = jnp.where(mine, own_ref[...], other_ref[...])
        nm = ADAM_B1 * m_ref[...] + (1.0 - ADAM_B1) * gv
        nv = ADAM_B2 * v_ref[...] + (1.0 - ADAM_B2) * (gv * gv)
        g_ref[...] = gv
        d_ref[...] = -ADAM_LR * ((nm / c1) / (jnp.sqrt(nv / c2) + ADAM_EPS) + ADAM_WD * w_ref[...])
        nm_ref[...] = nm
        nv_ref[...] = nv

    full = pl.BlockSpec((tr, c), lambda i, p: (i, 0))
    own_half = pl.BlockSpec((tr, c), lambda i, p: (jnp.where(i // nt == p[0], i % nt, 0), 0))
    other_half = pl.BlockSpec((tr, c), lambda i, p: (jnp.where(i // nt == p[0], 0, i % nt), 0))
    out = jax.ShapeDtypeStruct((r, c), F32)
    grid_spec = pltpu.PrefetchScalarGridSpec(num_scalar_prefetch=1, grid=(2 * nt,),
                                             in_specs=[full, own_half, other_half, full, full],
                                             out_specs=[full] * 4)
    return pl.pallas_call(body, grid_spec=grid_spec, out_shape=[out] * 4, name=name,
                          compiler_params=_params(("parallel",)))(place, w, own, other, m, v)


def _adamw_many(ws, gs, ms, vs, name):
    n = len(ws)
    c1 = 1.0 - ADAM_B1 ** ADAM_STEP
    c2 = 1.0 - ADAM_B2 ** ADAM_STEP

    def body(*refs):
        w_refs, g_refs, m_refs, v_refs = (refs[k * n:(k + 1) * n] for k in range(4))
        d_refs, nm_refs, nv_refs = (refs[(4 + k) * n:(5 + k) * n] for k in range(3))
        for i in range(n):
            gv = g_refs[i][...]
            nm = ADAM_B1 * m_refs[i][...] + (1.0 - ADAM_B1) * gv
            nv = ADAM_B2 * v_refs[i][...] + (1.0 - ADAM_B2) * (gv * gv)
            d_refs[i][...] = -ADAM_LR * ((nm / c1) / (jnp.sqrt(nv / c2) + ADAM_EPS) + ADAM_WD * w_refs[i][...])
            nm_refs[i][...] = nm
            nv_refs[i][...] = nv

    vmem = pl.BlockSpec(memory_space=pltpu.VMEM)
    shapes = [jax.ShapeDtypeStruct(t.shape, F32) for t in ws]
    outs = pl.pallas_call(body, in_specs=[vmem] * (4 * n), out_specs=[vmem] * (3 * n), out_shape=shapes * 3,
                          name=name, compiler_params=_params(vmem_mb=56))(*ws, *gs, *ms, *vs)
    return outs[:n], outs[n:2 * n], outs[2 * n:]


BIG = ("w_in", "w_glu", "w_out", "w_up", "w_down")
WEIGHTS = ("norm_mix_g", "w_in", "a_re", "a_im", "log_step", "b_re", "b_im", "c_re", "c_im", "d_skip", "w_glu",
           "sink", "norm_attn_g", "norm_ssm_g", "w_out", "norm_ffn_g", "w_up", "conv_w", "conv_b", "w_down",
           "norm_final_g")
SMALL = ("norm_mix_g", "a_re", "a_im", "log_step", "b_re", "b_im", "c_re", "c_im", "d_skip", "sink",
         "norm_attn_g", "norm_ssm_g", "norm_ffn_g", "conv_w", "conv_b", "norm_final_g")
SMALL_ROWS = 48
N_DEV = 8


def _tile_rows(size):
    return -(-size // (SUBLANES * D_MODEL)) * SUBLANES


def _by_owner(name, g):
    if name == "w_up":
        return g
    return g.reshape(4, g.shape[0] // 4, g.shape[1])


def _view(name, t):
    if name == "w_in":
        return jnp.swapaxes(t[0], 0, 1)
    if name in ("b_re", "b_im"):
        return jnp.swapaxes(t, -1, -2)
    return t


def _unview(name, t):
    if name == "w_in":
        return jnp.swapaxes(t, 0, 1)[None]
    if name in ("b_re", "b_im"):
        return jnp.swapaxes(t, -1, -2)
    return t


def kernel(x, norm_mix_g, w_in, a_re, a_im, log_step, b_re, b_im, c_re, c_im, d_skip, w_glu, sink, norm_attn_g, norm_ssm_g, w_out, norm_ffn_g, w_up, conv_w, conv_b, w_down, norm_final_g, loss_target, m_norm_mix_g, m_w_in, m_a_re, m_a_im, m_log_step, m_b_re, m_b_im, m_c_re, m_c_im, m_d_skip, m_w_glu, m_sink, m_norm_attn_g, m_norm_ssm_g, m_w_out, m_norm_ffn_g, m_w_up, m_conv_w, m_conv_b, m_w_down, m_norm_final_g, v_norm_mix_g, v_w_in, v_a_re, v_a_im, v_log_step, v_b_re, v_b_im, v_c_re, v_c_im, v_d_skip, v_w_glu, v_sink, v_norm_attn_g, v_norm_ssm_g, v_w_out, v_norm_ffn_g, v_w_up, v_conv_w, v_conv_b, v_w_down, v_norm_final_g):
    given = dict(locals())
    w = {n: given[n] for n in WEIGHTS}
    m = {n: given["m_" + n] for n in WEIGHTS}
    v = {n: given["v_" + n] for n in WEIGHTS}
    xy = 2 * lax.axis_index("x") + lax.axis_index("y")

    core = lax.axis_index("c")
    place = jnp.stack([core, xy]).astype(jnp.int32)

    conv_rows = jnp.pad(w["conv_w"][0], ((0, 2 * SUBLANES - 3), (0, 0)))
    rows = lambda t: t.reshape(4 * t.shape[1], t.shape[2])
    (w_in_all,) = _gather_weights([_view("w_in", w["w_in"])], [BF16])
    wb = {"w_in": rows(w_in_all)}
    mixer = [_cast_place(w[n][0], place, BF16, w_in_all, "cast_" + n) for n in ("w_glu", "w_out")]
    mixer.append(_cast_place(conv_rows, place, F32, w_in_all, "cast_conv_w"))
    *mixer_flight, mixer_token = _spread_start(mixer, "spread_mixer_start")
    late = ("w_up", "w_down")
    *late_flight, token = _spread_start(
        [_cast_place(w[n][0], place, BF16, mixer_token, "cast_" + n) for n in late], "spread_ffn_start")

    def mixer_weights(after):
        w_glu4, w_out4, conv4 = _spread_wait(*mixer_flight, after, "spread_mixer_wait")
        return {"w_glu": rows(w_glu4), "w_out": rows(w_out4),
                "conv_w": conv4[:, :3].transpose(1, 0, 2).reshape(3, 2 * D_FF)}

    def late_weights(after):
        w_up4, w_down4 = _spread_wait(*late_flight, after, "spread_ffn_wait")
        return {"w_up": w_up4, "w_down": rows(w_down4)}

    sp = {n: w[n][0] for n in ("a_re", "a_im", "log_step", "b_re", "b_im", "c_re", "c_im", "d_skip",
                               "norm_mix_g", "norm_attn_g", "norm_ssm_g", "norm_ffn_g", "sink", "conv_b")}
    for n in ("norm_mix_g", "norm_attn_g", "norm_ssm_g", "norm_ffn_g", "sink", "conv_b"):
        sp[n] = sp[n].reshape(1, -1)
    sp["norm_mix_g"] = sp["norm_mix_g"] + token[:1, :1]
    sp["norm_final_g"] = w["norm_final_g"]
    early, tail = late + ("w_out",), ("w_in", "w_glu")
    flight = {}

    def grads_ready(dw_up, dw_down, dw_out):
        *flight["pair"], token = _pair_start([dw_up, _by_owner("w_down", dw_down), _by_owner("w_out", dw_out)])
        return token[:1, :1]

    def grads_next(after):
        mine, got = _pair_wait(*flight["pair"], after)
        sums, flight["own"] = zip(*[_pair_sum(a, b, place, BF16, "pair_sum_" + n) for n, a, b in zip(early, mine, got)])
        *flight["chip"], token = _chip_start(list(sums))
        return token[:1, :1]

    loss, grad_x, g = _local_step(x[0], loss_target[0], wb, sp, mixer_weights, late_weights, grads_ready,
                                  grads_next)

    def as_rows(t):
        rows = _tile_rows(t.size)
        return jnp.pad(t.reshape(-1), (0, rows * D_MODEL - t.size)).reshape(rows, D_MODEL)

    pieces = [as_rows(g[n]) for n in SMALL] + [as_rows(loss)]
    spare = N_DEV * SMALL_ROWS - sum(p.shape[0] for p in pieces)
    small = jnp.concatenate(pieces + [jnp.zeros((spare, D_MODEL), F32)]).reshape(4, 2 * SMALL_ROWS, D_MODEL)
    by_owner = [_by_owner(n, g[n]) for n in tail] + [small]
    got = _pair_exchange(by_owner)
    transit = [BF16] * len(tail) + [F32]
    chip_sums, own_sums = zip(*[_pair_sum(a, b, place, t, "pair_sum_" + n)
                                for n, a, b, t in zip(tail + ("small",), by_owner, got, transit)])
    landed = _chip_exchange(list(chip_sums))
    halves = {n: _chip_sum(o, t, "chip_sum_" + n) for n, o, t in zip(tail + ("small",), own_sums, landed)}
    early_landed = _chip_wait(*flight["chip"], grad_x)
    for n, o, t in zip(early, flight["own"], early_landed):
        halves[n] = _chip_sum(o, t, "chip_sum_" + n)
    *others, small_all = _final_exchange([halves[n] for n in BIG], halves["small"])
    small_all = small_all.reshape(N_DEV * SMALL_ROWS, D_MODEL)
    grads, row = {}, 0
    for n in SMALL:
        shape = (3, 4 * w[n].shape[-1]) if n == "conv_w" else w[n].shape[1:] if n != "norm_final_g" else w[n].shape
        size = math.prod(shape)
        grads[n] = small_all[row:row + _tile_rows(size)].reshape(-1)[:size].reshape(shape)
        row += _tile_rows(size)
    loss = small_all[row, 0]
    cw = w["conv_w"].shape[-1]
    grads["conv_w"] = lax.dynamic_slice_in_dim(grads["conv_w"], xy * cw, cw, axis=1)
    grads = {n: _view(n, grads[n].reshape(w[n].shape)) for n in SMALL}
    wv, mv, vv = ({n: _view(n, t[n]) for n in WEIGHTS} for t in (w, m, v))

    delta, new_m, new_v = {}, {}, {}
    for n, other in zip(BIG, others):
        two_d = lambda t: t.reshape(t.shape[-2:])
        grads[n], delta[n], new_m[n], new_v[n] = _adamw_halves(
            two_d(wv[n]), halves[n], other, two_d(mv[n]), two_d(vv[n]), place, "adamw_" + n)
    for group, name in ((("b_re", "b_im"), "adamw_b"), (tuple(n for n in SMALL if n not in ("b_re", "b_im")), "adamw_small")):
        row = lambda t: t.reshape(1, -1) if t.ndim == 1 else t
        d_, m_, v_ = _adamw_many(*[[row(t[n]) for n in group] for t in (wv, grads, mv, vv)], name)
        for n, dn, mn, vn in zip(group, d_, m_, v_):
            delta[n], new_m[n], new_v[n] = (t.reshape(wv[n].shape) for t in (dn, mn, vn))
    natural = lambda t: [_unview(n, t[n].reshape(wv[n].shape)) for n in WEIGHTS]
    return (loss, grad_x[None], *natural(grads), *natural(delta), *natural(new_m), *natural(new_v))
```

```python
import functools
import math

import jax
import jax.numpy as jnp
import numpy as np
from jax import lax
from jax.experimental import pallas as pl
from jax.experimental.pallas import tpu as pltpu

F32 = jnp.float32
BF16 = jnp.bfloat16

D_MODEL = 1024
N_Q_HEADS = 8
N_KV_HEADS = 2
HEAD_DIM = 64
ATTN_WIDTH = 512
KV_WIDTH = 128
QKV_WIDTH = ATTN_WIDTH + 2 * KV_WIDTH
WINDOW = 128
BLOCK = 128
ROPE_DIM = 16
ROPE_THETA = 500000.0
SCORE_SCALE = HEAD_DIM ** -0.5
SSM_WIDTH = 512
SSM_GROUP = 16
N_SSM_GROUPS = 32
SSM_STATE = 64
IN_WIDTH = 1280
D_FF = 2816
EPS = 1e-6
ADAM_LR = 0.001
ADAM_B1 = 0.9
ADAM_B2 = 0.999
ADAM_EPS = 1e-08
ADAM_WD = 0.01
ADAM_STEP = 10

VMEM_BYTES_V7X = 64 * 1024 * 1024
SUBLANES = 8
LANES = 128
SSM_CB = 4
SSM_CH = 128
SSM_ST = 512
N_SEG = SUBLANES

NN = (((1,), (0,)), ((), ()))
NT = (((1,), (1,)), ((), ()))
TN = (((0,), (0,)), ((), ()))


def _params(sem=None, vmem_mb=48):
    limit = vmem_mb * 1024 * 1024
    assert limit < VMEM_BYTES_V7X
    return pltpu.CompilerParams(dimension_semantics=sem, vmem_limit_bytes=limit)


def _dg(a, b, dims):
    return lax.dot_general(a, b, dims, preferred_element_type=F32)


def _sigmoid(x):
    return 1.0 / (1.0 + jnp.exp(-x))


_SQRT_HALF = 0.7071067811865476
_INV_SQRT_2PI = 0.3989422804014327


def _gelu(x):
    return 0.5 * x * (1.0 + lax.erf(x * _SQRT_HALF))


def _gelu_grad(x):
    return 0.5 * (1.0 + lax.erf(x * _SQRT_HALF)) + x * (_INV_SQRT_2PI * jnp.exp(-0.5 * x * x))


def _mm_tn(a, b, tm, tn, name):
    k, m = a.shape
    n = b.shape[1]

    def body(a_ref, b_ref, o_ref):
        o_ref[...] = _dg(a_ref[...], b_ref[...], TN)

    return pl.pallas_call(
        body, grid=(m // tm, n // tn),
        in_specs=[pl.BlockSpec((k, tm), lambda i, j: (0, i)), pl.BlockSpec((k, tn), lambda i, j: (0, j))],
        out_specs=pl.BlockSpec((tm, tn), lambda i, j: (i, j)),
        out_shape=jax.ShapeDtypeStruct((m, n), F32), name=name,
        compiler_params=_params(("parallel", "parallel")),
    )(a, b)


def _mm_nn_cols(a, b4, tm, name):
    m, k = a.shape
    s, _, n = b4.shape

    def body(a_ref, b_ref, o_ref):
        o_ref[...] = _dg(a_ref[...], b_ref[...], NN)

    return pl.pallas_call(
        body, grid=(m // tm, s),
        in_specs=[pl.BlockSpec((tm, k), lambda i, j: (i, 0)), pl.BlockSpec((None, k, n), lambda i, j: (j, 0, 0))],
        out_specs=pl.BlockSpec((tm, n), lambda i, j: (i, j)),
        out_shape=jax.ShapeDtypeStruct((m, s * n), F32), name=name,
        compiler_params=_params(("parallel", "parallel")),
    )(a, b4)


def _mm_tn_cols(a, b2, s, tm, name):
    k, m = a.shape
    h, _, wide = b2.shape
    per = s // h
    n = wide // per

    def body(a_ref, b_ref, o_ref):
        o_ref[...] = _dg(a_ref[...], b_ref[...], TN)

    return pl.pallas_call(
        body, grid=(s, m // tm),
        in_specs=[pl.BlockSpec((k, tm), lambda j, i: (0, i)),
                  pl.BlockSpec((None, k, n), lambda j, i: (j // per, 0, j % per))],
        out_specs=pl.BlockSpec((None, tm, n), lambda j, i: (j, i, 0)),
        out_shape=jax.ShapeDtypeStruct((s, m, n), F32), name=name,
        compiler_params=_params(("parallel", "parallel")),
    )(a, b2)


TM_EW = 512


def _rms_bwd_vals(xv, gv, dy):
    r = lax.rsqrt(jnp.mean(xv * xv, axis=-1, keepdims=True) + EPS)
    xh = xv * r
    dxh = dy * gv
    dx = r * (dxh - xh * jnp.mean(dxh * xh, axis=-1, keepdims=True))
    return dx, dy * xh


TM_FUSED = 512
TM_LOSS = 256


def _rms_vals(xv, gv):
    return xv * lax.rsqrt(jnp.mean(xv * xv, axis=-1, keepdims=True) + EPS) * gv


def _rope_blocks(src, dst, c, lo, hi):
    nq = ATTN_WIDTH // LANES
    for blk in range(nq + 1):
        t = src[:, blk * LANES:(blk + 1) * LANES]
        rot = t * c + pltpu.roll(t, LANES - 8, 1) * lo + pltpu.roll(t, 8, 1) * hi
        dst[:, blk * LANES:(blk + 1) * LANES] = (rot * SCORE_SCALE if blk < nq else rot).astype(BF16)
    dst[:, (nq + 1) * LANES:] = src[:, (nq + 1) * LANES:].astype(BF16)


def _rms_mm_rope(x, g, wt, tabs, name):
    l, d = x.shape
    n = wt.shape[0]

    def body(x_ref, g_ref, w_ref, c_ref, lo_ref, hi_ref, h_ref, qkv_ref, u_ref):
        h = _rms_vals(x_ref[...], g_ref[...]).astype(BF16)
        h_ref[...] = h
        out = _dg(h, w_ref[...], NT)
        _rope_blocks(out[:, :QKV_WIDTH], qkv_ref, c_ref[...], lo_ref[...], hi_ref[...])
        u_ref[...] = out[:, QKV_WIDTH:]

    row = lambda width: pl.BlockSpec((TM_FUSED, width), lambda i: (i, 0))
    return pl.pallas_call(
        body, grid=(l // TM_FUSED,),
        in_specs=[row(d), pl.BlockSpec((1, d), lambda i: (0, 0)), pl.BlockSpec((n, d), lambda i: (0, 0)),
                  row(LANES), row(LANES), row(LANES)],
        out_specs=[row(d), row(QKV_WIDTH), row(n - QKV_WIDTH)],
        out_shape=[jax.ShapeDtypeStruct((l, d), BF16), jax.ShapeDtypeStruct((l, QKV_WIDTH), BF16),
                   jax.ShapeDtypeStruct((l, n - QKV_WIDTH), F32)],
        name=name, compiler_params=_params(("parallel",)),
    )(x, g, wt, *tabs)


def _mix_mm_res_rms(attn, ys, g_attn, g_ssm, b, res, g, name):
    l, w = attn.shape
    d = b.shape[1]

    def body(a_ref, y_ref, ga_ref, gs_ref, b_ref, r_ref, g_ref, m_ref, x_ref, h_ref):
        m_ref[:, :w] = _rms_vals(a_ref[...], ga_ref[...]).astype(BF16)
        m_ref[:, w:] = _rms_vals(y_ref[...], gs_ref[...]).astype(BF16)
        xv = r_ref[...] + _dg(m_ref[...], b_ref[...], NN)
        x_ref[...] = xv
        h_ref[...] = _rms_vals(xv, g_ref[...]).astype(BF16)

    row = lambda width: pl.BlockSpec((TM_FUSED, width), lambda i: (i, 0))
    vec = lambda width: pl.BlockSpec((1, width), lambda i: (0, 0))
    return pl.pallas_call(
        body, grid=(l // TM_FUSED,),
        in_specs=[row(w), row(w), vec(w), vec(w), pl.BlockSpec((2 * w, d), lambda i: (0, 0)), row(d), vec(d)],
        out_specs=[row(2 * w), row(d), row(d)],
        out_shape=[jax.ShapeDtypeStruct((l, 2 * w), BF16), jax.ShapeDtypeStruct((l, d), F32),
                   jax.ShapeDtypeStruct((l, d), BF16)],
        name=name, compiler_params=_params(("parallel",)),
    )(attn, ys, g_attn, g_ssm, b, res, g)


def _mm_res_loss(a, b, res, g, target):
    l, k = a.shape
    d = b.shape[1]

    def body(a_ref, b_ref, r_ref, g_ref, t_ref, loss_ref, dx_ref, dxb_ref, dg_ref):
        xv = r_ref[...] + _dg(a_ref[...], b_ref[...], NN)
        gv = g_ref[...]
        r = lax.rsqrt(jnp.mean(xv * xv, axis=-1, keepdims=True) + EPS)
        xh = xv * r
        e = xh * gv - t_ref[...]
        part = jnp.sum(jnp.sum(e * e, axis=1, keepdims=True), axis=0, keepdims=True) * (0.5 / d)
        dy = e * (1.0 / d)
        dxh = dy * gv
        dx = r * (dxh - xh * jnp.mean(dxh * xh, axis=-1, keepdims=True))
        dx_ref[...] = dx
        dxb_ref[...] = dx.astype(BF16)

        @pl.when(pl.program_id(0) == 0)
        def _():
            dg_ref[...] = jnp.zeros_like(dg_ref)
            loss_ref[...] = jnp.zeros_like(loss_ref)

        dg_ref[...] += jnp.sum(dy * xh, axis=0, keepdims=True)
        loss_ref[...] += part

    row = lambda width: pl.BlockSpec((TM_LOSS, width), lambda i: (i, 0))
    vec = pl.BlockSpec((1, d), lambda i: (0, 0))
    return pl.pallas_call(
        body, grid=(l // TM_LOSS,),
        in_specs=[row(k), pl.BlockSpec((k, d), lambda i: (0, 0)), row(d), vec, row(d)],
        out_specs=[pl.BlockSpec((1, 1), lambda i: (0, 0)), row(d), row(d), vec],
        out_shape=[jax.ShapeDtypeStruct((1, 1), F32), jax.ShapeDtypeStruct((l, d), F32),
                   jax.ShapeDtypeStruct((l, d), BF16), jax.ShapeDtypeStruct((1, d), F32)],
        name="mm_down_loss", compiler_params=_params(("arbitrary",)),
    )(a, b, res, g, target)


def _mm_rms_bwd(a, b, a_spec, b_spec, matmul, x, g, res, name):
    l, d = x.shape

    def body(a_ref, b_ref, x_ref, g_ref, res_ref, dx_ref, dxb_ref, dg_ref):
        dx, dgr = _rms_bwd_vals(x_ref[...], g_ref[...], matmul(a_ref, b_ref))
        dx = dx + res_ref[...]
        dx_ref[...] = dx
        dxb_ref[...] = dx.astype(BF16)

        @pl.when(pl.program_id(0) == 0)
        def _():
            dg_ref[...] = jnp.zeros_like(dg_ref)

        dg_ref[...] += jnp.sum(dgr, axis=0, keepdims=True)

    row = pl.BlockSpec((TM_FUSED, d), lambda i: (i, 0))
    vec = pl.BlockSpec((1, d), lambda i: (0, 0))
    return pl.pallas_call(
        body, grid=(l // TM_FUSED,), in_specs=[a_spec, b_spec, row, vec, row], out_specs=[row, row, vec],
        out_shape=[jax.ShapeDtypeStruct((l, d), F32), jax.ShapeDtypeStruct((l, d), BF16),
                   jax.ShapeDtypeStruct((1, d), F32)],
        name=name, compiler_params=_params(("arbitrary",)),
    )(a, b, x, g, res)


def _mm_nn_rms_bwd(a, b, x, g, res, name):
    return _mm_rms_bwd(a, b, pl.BlockSpec((TM_FUSED, a.shape[1]), lambda i: (i, 0)),
                       pl.BlockSpec(b.shape, lambda i: (0, 0)),
                       lambda a_ref, b_ref: _dg(a_ref[...], b_ref[...], NN), x, g, res, name)


def _mm_cols_rms_bwd(a2, b4, x, g, res, name):
    h, _, wide = a2.shape
    s, _, n = b4.shape
    per = s // h

    def matmul(a_ref, b_ref):
        acc = None
        for j in range(s):
            part = _dg(a_ref[j // per, :, (j % per) * n:(j % per + 1) * n], b_ref[j], NT)
            acc = part if acc is None else acc + part
        return acc

    return _mm_rms_bwd(a2, b4, pl.BlockSpec((h, TM_FUSED, wide), lambda i: (0, i, 0)),
                       pl.BlockSpec(b4.shape, lambda i: (0, 0, 0), pipeline_mode=pl.Buffered(1)),
                       matmul, x, g, res, name)


def _mm_mix_bwd(dx, b, attn, ys, g_attn, g_ssm, name):
    l, w = attn.shape
    d = dx.shape[1]

    def body(dx_ref, b_ref, a_ref, y_ref, ga_ref, gs_ref, da_ref, dy_ref, dga_ref, dgs_ref):
        @pl.when(pl.program_id(0) == 0)
        def _():
            dga_ref[...] = jnp.zeros_like(dga_ref)
            dgs_ref[...] = jnp.zeros_like(dgs_ref)

        dm = _dg(dx_ref[...], b_ref[...], NT)
        for src, gr, off, dst, dgr in ((a_ref, ga_ref, 0, da_ref, dga_ref), (y_ref, gs_ref, w, dy_ref, dgs_ref)):
            dxv, dg_rows = _rms_bwd_vals(src[...], gr[...], dm[:, off:off + w])
            dst[...] = dxv
            dgr[...] += jnp.sum(dg_rows, axis=0, keepdims=True)

    row = lambda width: pl.BlockSpec((TM_FUSED, width), lambda i: (i, 0))
    vec = pl.BlockSpec((1, w), lambda i: (0, 0))
    return pl.pallas_call(
        body, grid=(l // TM_FUSED,),
        in_specs=[row(d), pl.BlockSpec((2 * w, d), lambda i: (0, 0)), row(w), row(w), vec, vec],
        out_specs=[row(w), row(w), vec, vec],
        out_shape=[jax.ShapeDtypeStruct((l, w), F32), jax.ShapeDtypeStruct((l, w), F32),
                   jax.ShapeDtypeStruct((1, w), F32), jax.ShapeDtypeStruct((1, w), F32)],
        name=name, compiler_params=_params(("arbitrary",)),
    )(dx, b, attn, ys, g_attn, g_ssm)


def _rope_tables(l):
    half = ROPE_DIM // 2
    f32 = np.float32
    inv_freq = np.power(f32(ROPE_THETA), -np.arange(half, dtype=f32) / f32(half))
    ang = np.arange(l, dtype=f32)[:, None] * inv_freq[None, :]
    cos, sin = np.cos(ang), np.sin(ang)
    ones = np.ones((l, HEAD_DIM - ROPE_DIM), f32)
    zeros = np.zeros((l, HEAD_DIM - ROPE_DIM), f32)
    zh = np.zeros((l, half), f32)
    c = np.concatenate([cos, cos, ones], axis=1)
    s_lo = np.concatenate([-sin, zh, zeros], axis=1)
    s_hi = np.concatenate([zh, sin, zeros], axis=1)
    return tuple(jnp.asarray(np.tile(t, (1, LANES // HEAD_DIM)), F32) for t in (c, s_lo, s_hi))


def _rope_bwd(dq, dkv, du_ssm, dpre, d_skip, tabs):
    l = dq.shape[0]
    nq = ATTN_WIDTH // LANES

    def body(dq_ref, dkv_ref, du_ref, dpre_ref, ds_ref, c_ref, lo_ref, hi_ref, o_ref):
        c, lo, hi = c_ref[...], lo_ref[...], hi_ref[...]
        for blk in range(nq + 1):
            t = dq_ref[:, blk * LANES:(blk + 1) * LANES] if blk < nq else dkv_ref[:, :KV_WIDTH]
            g = t * c + pltpu.roll(t * lo, 8, 1) + pltpu.roll(t * hi, LANES - 8, 1)
            o_ref[:, blk * LANES:(blk + 1) * LANES] = g.astype(BF16)
        o_ref[:, (nq + 1) * LANES:QKV_WIDTH] = dkv_ref[:, KV_WIDTH:].astype(BF16)
        o_ref[:, QKV_WIDTH:] = (du_ref[...] + dpre_ref[...] * ds_ref[...]).astype(BF16)

    tab = pl.BlockSpec((TM_EW, LANES), lambda i: (i, 0))
    wide = pl.BlockSpec((TM_EW, SSM_WIDTH), lambda i: (i, 0))
    return pl.pallas_call(
        body, grid=(l // TM_EW,),
        in_specs=[wide, pl.BlockSpec((TM_EW, 2 * KV_WIDTH), lambda i: (i, 0)), wide, wide,
                  pl.BlockSpec((1, SSM_WIDTH), lambda i: (0, 0)), tab, tab, tab],
        out_specs=pl.BlockSpec((TM_EW, IN_WIDTH), lambda i: (i, 0)),
        out_shape=jax.ShapeDtypeStruct((l, IN_WIDTH), BF16), name="rope_bwd",
        compiler_params=_params(("parallel",)),
    )(dq, dkv, du_ssm, dpre, d_skip, *tabs)


_Q_COLS = ATTN_WIDTH // LANES
_NEG = -1e30


def _window_specs(nb, width, col):
    return [
        pl.BlockSpec((BLOCK, width), lambda n: (jnp.maximum(n - 1, 0), col)),
        pl.BlockSpec((BLOCK, width), lambda n: (n, col)),
        pl.BlockSpec((BLOCK, width), lambda n: (jnp.minimum(n + 1, nb - 1), col)),
    ]


def _stacked_sink(sink_ref, heads):
    rid = lax.broadcasted_iota(jnp.int32, (len(heads) * BLOCK, 1), 0)
    sk = jnp.full(rid.shape, sink_ref[0, heads[-1]], F32)
    for g in range(len(heads) - 2, -1, -1):
        sk = jnp.where(rid < (g + 1) * BLOCK, sink_ref[0, heads[g]], sk)
    return sk


def _attn_fwd(qkv, sink):
    l = qkv.shape[0]
    nb = l // BLOCK
    grp = N_Q_HEADS // N_KV_HEADS

    def body(sink_ref, q_ref, k0, k1, k2, v0, v1, v2, o_ref, lse_ref):
        n = pl.program_id(0)
        q = q_ref[...]
        kw = jnp.concatenate([k0[...], k1[...], k2[...]], axis=0)
        vw = jnp.concatenate([v0[...], v1[...], v2[...]], axis=0)
        row = lax.broadcasted_iota(jnp.int32, (grp * BLOCK, 3 * BLOCK), 0)
        col = lax.broadcasted_iota(jnp.int32, (grp * BLOCK, 3 * BLOCK), 1)
        valid = jnp.abs(col - BLOCK - (row & (BLOCK - 1))) <= WINDOW
        valid &= jnp.logical_not((n == 0) & (col < BLOCK))
        valid &= jnp.logical_not((n == nb - 1) & (col >= 2 * BLOCK))
        for hk in range(N_KV_HEADS):
            heads = range(hk * grp, (hk + 1) * grp)
            qs = jnp.concatenate([q[:, h * HEAD_DIM:(h + 1) * HEAD_DIM] for h in heads], axis=0)
            kh = kw[:, hk * HEAD_DIM:(hk + 1) * HEAD_DIM]
            vh = vw[:, hk * HEAD_DIM:(hk + 1) * HEAD_DIM]
            s = jnp.where(valid, _dg(qs, kh, NT), _NEG)
            sk = _stacked_sink(sink_ref, heads)
            m = jnp.maximum(jnp.max(s, axis=1, keepdims=True), sk)
            p = jnp.exp(s - m)
            denom = jnp.sum(p, axis=1, keepdims=True) + jnp.exp(sk - m)
            o = _dg((p / denom).astype(BF16), vh, NN)
            lse = m + jnp.log(denom)
            for g, h in enumerate(heads):
                o_ref[:, h * HEAD_DIM:(h + 1) * HEAD_DIM] = o[g * BLOCK:(g + 1) * BLOCK]
                lse_ref[:, h:h + 1] = lse[g * BLOCK:(g + 1) * BLOCK]

    return pl.pallas_call(
        body, grid=(nb,),
        in_specs=[pl.BlockSpec(memory_space=pltpu.SMEM),
                  pl.BlockSpec((BLOCK, ATTN_WIDTH), lambda n: (n, 0))]
        + _window_specs(nb, KV_WIDTH, _Q_COLS) + _window_specs(nb, KV_WIDTH, _Q_COLS + 1),
        out_specs=[pl.BlockSpec((BLOCK, ATTN_WIDTH), lambda n: (n, 0)),
                   pl.BlockSpec((BLOCK, N_Q_HEADS), lambda n: (n, 0))],
        out_shape=[jax.ShapeDtypeStruct((l, ATTN_WIDTH), F32), jax.ShapeDtypeStruct((l, N_Q_HEADS), F32)],
        name="attn_fwd", compiler_params=_params(("parallel",)),
    )(sink, qkv, qkv, qkv, qkv, qkv, qkv, qkv)


def _attn_bwd(qkv, attn, dattn, lse, sink):
    l = qkv.shape[0]
    nb = l // BLOCK
    grp = N_Q_HEADS // N_KV_HEADS
    win = 3 * BLOCK

    def body(sink_ref, q_ref, k0, k1, k2, v0, v1, v2, o_ref, d_ref, l_ref, dq_ref, dkv_ref, dsink_ref, ring_ref):
        n = pl.program_id(0)

        @pl.when(n == 0)
        def _():
            dsink_ref[...] = jnp.zeros_like(dsink_ref)
            ring_ref[...] = jnp.zeros_like(ring_ref)

        @pl.when(n < nb)
        def _():
            first, last = n == 0, n == nb - 1
            cat = lambda a, b, c: jnp.concatenate([a[...], b[...], c[...]], axis=0)
            q, kw, vw = q_ref[...], cat(k0, k1, k2), cat(v0, v1, v2)
            dov = d_ref[...]
            prod = o_ref[...] * dov
            dob = dov.astype(BF16)
            lse = l_ref[...]
            row = lax.broadcasted_iota(jnp.int32, (grp * BLOCK, win), 0)
            col = lax.broadcasted_iota(jnp.int32, (grp * BLOCK, win), 1)
            valid = jnp.abs(col - BLOCK - (row & (BLOCK - 1))) <= WINDOW
            valid &= jnp.logical_not(first & (col < BLOCK))
            valid &= jnp.logical_not(last & (col >= 2 * BLOCK))

            dsink_parts, dks, dvs = [], [], []
            for hk in range(N_KV_HEADS):
                heads = range(hk * grp, (hk + 1) * grp)
                ksl = slice(hk * HEAD_DIM, (hk + 1) * HEAD_DIM)
                hsl = [slice(h * HEAD_DIM, (h + 1) * HEAD_DIM) for h in heads]
                stack = lambda parts: jnp.concatenate(parts, axis=0)
                qs = stack([q[:, s_] for s_ in hsl])
                dos = stack([dob[:, s_] for s_ in hsl])
                deltas = stack([jnp.sum(prod[:, s_], axis=1, keepdims=True) for s_ in hsl])
                lses = stack([lse[:, h:h + 1] for h in heads])
                kh, vh = kw[:, ksl], vw[:, ksl]
                s = jnp.where(valid, _dg(qs, kh, NT), _NEG)
                p = jnp.exp(s - lses)
                dp = _dg(dos, vh, NT)
                ds = (p * (dp - deltas)).astype(BF16)
                dq = _dg(ds, kh, NN) * SCORE_SCALE
                sink_rows = jnp.exp(_stacked_sink(sink_ref, heads) - lses) * deltas
                for g in range(grp):
                    dq_ref[:, hsl[g]] = dq[g * BLOCK:(g + 1) * BLOCK]
                    dsink_parts.append(jnp.sum(sink_rows[g * BLOCK:(g + 1) * BLOCK], axis=0, keepdims=True))
                dks.append(_dg(ds, qs, TN))
                dvs.append(_dg(p.astype(BF16), dos, TN))
            dsink_ref[...] -= jnp.concatenate(dsink_parts, axis=1)
            part = jnp.concatenate(dks + dvs, axis=1)
            ring_ref[(n + 2) % 3] += part[0:BLOCK]
            ring_ref[n % 3] += part[BLOCK:2 * BLOCK]
            ring_ref[(n + 1) % 3] = part[2 * BLOCK:]

        @pl.when(n >= 1)
        def _():
            dkv_ref[...] = ring_ref[(n + 2) % 3]

    centre = lambda n: jnp.minimum(n, nb - 1)
    window = lambda width, col: [
        pl.BlockSpec((BLOCK, width), lambda n: (jnp.maximum(centre(n) - 1, 0), col)),
        pl.BlockSpec((BLOCK, width), lambda n: (centre(n), col)),
        pl.BlockSpec((BLOCK, width), lambda n: (jnp.minimum(centre(n) + 1, nb - 1), col))]
    own = lambda width: pl.BlockSpec((BLOCK, width), lambda n: (centre(n), 0))
    return pl.pallas_call(
        body, grid=(nb + 1,),
        in_specs=[pl.BlockSpec(memory_space=pltpu.SMEM), own(ATTN_WIDTH)]
        + window(KV_WIDTH, _Q_COLS) + window(KV_WIDTH, _Q_COLS + 1)
        + [own(ATTN_WIDTH), own(ATTN_WIDTH), own(N_Q_HEADS)],
        out_specs=[own(ATTN_WIDTH), pl.BlockSpec((BLOCK, 2 * KV_WIDTH), lambda n: (jnp.maximum(n - 1, 0), 0)),
                   pl.BlockSpec((1, N_Q_HEADS), lambda n: (0, 0))],
        out_shape=[jax.ShapeDtypeStruct((l, ATTN_WIDTH), F32), jax.ShapeDtypeStruct((l, 2 * KV_WIDTH), F32),
                   jax.ShapeDtypeStruct((1, N_Q_HEADS), F32)],
        scratch_shapes=[pltpu.VMEM((3, BLOCK, 2 * KV_WIDTH), F32)],
        name="attn_bwd", compiler_params=_params(("arbitrary",)),
    )(sink, qkv, qkv, qkv, qkv, qkv, qkv, qkv, attn, dattn, lse)


def _ssm_disc(a_re, a_im, log_step, b_re, b_im):
    step = jnp.exp(log_step)[..., None]
    mag = jnp.exp(a_re * step)
    lb_re, lb_im = mag * jnp.cos(a_im * step), mag * jnp.sin(a_im * step)
    nr, ni = lb_re - 1.0, lb_im
    den = a_re * a_re + a_im * a_im
    f_re = ((nr * a_re + ni * a_im) / den)[..., None]
    f_im = ((ni * a_re - nr * a_im) / den)[..., None]
    return lb_re, lb_im, f_re * b_re - f_im * b_im, f_re * b_im + f_im * b_re


def _ssm_pack(lb_re, lb_im, bb_re, bb_im, c_re, c_im):
    eye = jnp.eye(SSM_CH // SSM_GROUP, dtype=F32)
    ng = SSM_CH // SSM_GROUP

    def diag_b(bb):
        t = bb.reshape(2, SSM_CB, ng, SSM_STATE, SSM_GROUP)
        return jnp.einsum('dkgpc,gh->dkgchp', t, eye).reshape(2, SSM_CB, SSM_CH, SSM_ST)

    def diag_c(cc):
        t = cc.reshape(2, SSM_CB, ng, SSM_GROUP, SSM_STATE)
        return jnp.einsum('dkgcp,gh->dkhpgc', t, eye).reshape(2, SSM_CB, SSM_ST, SSM_CH)

    bcat = jnp.concatenate([diag_b(bb_re), diag_b(bb_im)], axis=-1)
    ccat = jnp.concatenate([diag_c(c_re), -diag_c(c_im)], axis=-2)
    lam_re = lb_re.reshape(2, SSM_CB, 1, SSM_ST)
    lam_im = lb_im.reshape(2, SSM_CB, 1, SSM_ST)
    return bcat, ccat, lam_re, lam_im


def _ssm_unpack(dbcat, dccat, dlam_re, dlam_im):
    ng = SSM_CH // SSM_GROUP
    eye = jnp.eye(ng, dtype=F32)

    def undiag_b(t):
        t = t.reshape(2, SSM_CB, ng, SSM_GROUP, ng, SSM_STATE)
        return jnp.einsum('dkgchp,gh->dkgpc', t, eye).reshape(2, N_SSM_GROUPS, SSM_STATE, SSM_GROUP)

    def undiag_c(t):
        t = t.reshape(2, SSM_CB, ng, SSM_STATE, ng, SSM_GROUP)
        return jnp.einsum('dkhpgc,gh->dkgcp', t, eye).reshape(2, N_SSM_GROUPS, SSM_GROUP, SSM_STATE)

    dbb_re, dbb_im = undiag_b(dbcat[..., :SSM_ST]), undiag_b(dbcat[..., SSM_ST:])
    dc_re, dc_im = undiag_c(dccat[:, :, :SSM_ST]), -undiag_c(dccat[:, :, SSM_ST:])
    shape = (2, N_SSM_GROUPS, SSM_STATE)
    return dlam_re.reshape(shape), dlam_im.reshape(shape), dbb_re, dbb_im, dc_re, dc_im


def _to_segments(t):
    l, w = t.shape
    return t.reshape(N_SEG, l // N_SEG, w).transpose(1, 0, 2).reshape(l, w)


def _from_segments(t):
    l, w = t.shape
    return t.reshape(l // N_SEG, N_SEG, w).transpose(1, 0, 2).reshape(l, w)


SSM_RC = 256
SSM_JC = SSM_RC // N_SEG
_RE, _IM = pl.ds(0, SSM_ST), pl.ds(SSM_ST, SSM_ST)


def _cfma(ar, ai, xr, xi, br, bi):
    return ar * xr - ai * xi + br, ar * xi + ai * xr + bi


def _chunk_rows(ci, rev, nc):
    start = jnp.where(rev, (nc - 1 - ci) * SSM_RC, ci * SSM_RC)
    return pl.ds(pl.multiple_of(start, SSM_RC), SSM_RC)


def _scan_chunk(src, dst, ar, ai, rev, nj, ci, carry, prev_ref=None):
    def rows_of(staged, j, k):
        at = jnp.where(rev, SSM_JC - 1 - k, k) if staged else j
        return pl.ds(pl.multiple_of(at * N_SEG, N_SEG), N_SEG)

    for k in range(SSM_JC):
        jj = ci * SSM_JC + k
        j = jnp.where(rev, nj - 1 - jj, jj)
        rows = rows_of(src[1], j, k)
        nr, ni = _cfma(ar, ai, carry[0], carry[1], src[0][rows, _RE], src[0][rows, _IM])
        if dst is not None:
            rows = rows_of(dst[1], j, k)
            dst[0][rows, _RE] = nr
            dst[0][rows, _IM] = ni
        if prev_ref is None:
            carry = (nr, ni)
            continue
        jp = jnp.where(rev, j - 1, j + 1)
        if k == SSM_JC - 1:
            inside = jnp.where((jp >= 0) & (jp < nj), 1.0, 0.0)
            jp = jnp.clip(jp, 0, nj - 1)
        prow = pl.ds(pl.multiple_of(jp * N_SEG, N_SEG), N_SEG)
        xr, xi = prev_ref[prow, _RE], prev_ref[prow, _IM]
        sr, si = nr * xr + ni * xi, ni * xr - nr * xi
        if k == SSM_JC - 1:
            sr, si = inside * sr, inside * si
        carry = (nr, ni, carry[2] + sr, carry[3] + si)
    return carry


def _segment_inits(ar, ai, end_r, end_i, rev, nj):
    pr, pi = ar, ai
    for _ in range(int(math.log2(nj))):
        pr, pi = pr * pr - pi * pi, 2.0 * pr * pi
    seg = lax.broadcasted_iota(jnp.int32, end_r.shape, 0)
    zero = jnp.zeros_like(end_r)

    def chain(shift, keep):
        ir, ii = zero, zero
        for _ in range(N_SEG - 1):
            tr, ti = _cfma(pr, pi, ir, ii, end_r, end_i)
            ir = jnp.where(keep, pltpu.roll(tr, shift, 0), 0.0)
            ii = jnp.where(keep, pltpu.roll(ti, shift, 0), 0.0)
        return ir, ii

    up_r, up_i = chain(1, seg >= 1)
    dn_r, dn_i = chain(N_SEG - 1, seg <= N_SEG - 2)
    return jnp.where(rev, dn_r, up_r), jnp.where(rev, dn_i, up_i)


def _ssm_specs(l):
    act = pl.BlockSpec((l, SSM_CH), lambda k, d: (0, k))
    bmat = pl.BlockSpec((None, None, SSM_CH, 2 * SSM_ST), lambda k, d: (d, k, 0, 0))
    cmat = pl.BlockSpec((None, None, 2 * SSM_ST, SSM_CH), lambda k, d: (d, k, 0, 0))
    lam = pl.BlockSpec((None, None, 1, SSM_ST), lambda k, d: (d, k, 0, 0))
    return act, bmat, cmat, lam


def _ssm_fwd(u_seg, bcat, ccat, lam_re, lam_im):
    l = u_seg.shape[0]
    nj = l // N_SEG
    nc = l // SSM_RC

    def body(u_ref, b_ref, c_ref, lr_ref, li_ref, y_ref, ub_ref, keep_ref, xs_ref, stage0, stage1, keep_sem):
        k, d = pl.program_id(0), pl.program_id(1)
        rev = d == 1
        shape = (N_SEG, SSM_ST)
        ar, ai = jnp.broadcast_to(lr_ref[...], shape), jnp.broadcast_to(li_ref[...], shape)
        zero = jnp.zeros(shape, F32)

        def inputs(ci, stage):
            rows = _chunk_rows(ci, rev, nc)
            ub = u_ref[rows, :].astype(BF16)
            ub_ref[rows, :] = ub
            bu = _dg(ub, b_ref[...], NN)
            stage[...] = bu
            xs_ref[rows, :] = bu

        def first(stage, ci, carry):
            return _scan_chunk((stage, True), None, ar, ai, rev, nj, ci, carry)

        def first_pass(t, carry):
            inputs(2 * t + 1, stage1)
            carry = first(stage0, 2 * t, carry)
            inputs(2 * t + 2, stage0)
            return first(stage1, 2 * t + 1, carry)

        inputs(0, stage0)
        carry = lax.fori_loop(0, nc // 2 - 1, first_pass, (zero, zero))
        inputs(nc - 1, stage1)
        carry = first(stage0, nc - 2, carry)
        end_r, end_i = first(stage1, nc - 1, carry)
        init = _segment_inits(ar, ai, end_r, end_i, rev, nj)

        @pl.when(d == 0)
        def _():
            y_ref[...] = jnp.zeros_like(y_ref)

        def outputs(ci):
            rows = _chunk_rows(ci, rev, nc)
            y_ref[rows, :] += _dg(xs_ref[rows, :].astype(BF16), c_ref[...], NN)
            pltpu.make_async_copy(xs_ref.at[rows], keep_ref.at[d, k, rows], keep_sem).start()

        def second(ci, carry):
            return _scan_chunk((xs_ref, False), (xs_ref, False), ar, ai, rev, nj, ci, carry)

        def second_pass(ci, carry):
            outputs(ci - 1)
            return second(ci, carry)

        lax.fori_loop(1, nc, second_pass, second(0, init))
        outputs(nc - 1)
        pltpu.make_async_copy(xs_ref, keep_ref.at[d, k], keep_sem).wait()

    act, bmat, cmat, lam = _ssm_specs(l)
    return pl.pallas_call(
        body, grid=(SSM_CB, 2), in_specs=[act, bmat, cmat, lam, lam], out_specs=[act, act, ANY],
        out_shape=[jax.ShapeDtypeStruct((l, SSM_WIDTH), F32), jax.ShapeDtypeStruct((l, SSM_WIDTH), BF16),
                   jax.ShapeDtypeStruct((2, SSM_CB, l, 2 * SSM_ST), F32)],
        scratch_shapes=[pltpu.VMEM((l, 2 * SSM_ST), F32), pltpu.VMEM((SSM_RC, 2 * SSM_ST), F32),
                        pltpu.VMEM((SSM_RC, 2 * SSM_ST), F32), pltpu.SemaphoreType.DMA],
        name="ssm_fwd", compiler_params=_params(("parallel", "arbitrary"), vmem_mb=56),
    )(u_seg, bcat.astype(BF16), ccat.astype(BF16), lam_re, lam_im)


def _ssm_bwd(u_seg, dy_seg, states, bcat, ccat, lam_re, lam_im):
    l = u_seg.shape[0]
    nj = l // N_SEG
    nc = l // SSM_RC

    def body(u_ref, dy_ref, keep_ref, b_ref, c_ref, lr_ref, li_ref,
             du_ref, db_ref, dc_ref, dlr_ref, dli_ref, xs_ref, gs_ref, dyb_ref, stage0, stage1, keep_sem):
        k, d = pl.program_id(0), pl.program_id(1)
        rev = d == 1
        back = jnp.logical_not(rev)
        shape = (N_SEG, SSM_ST)
        ar, ai = jnp.broadcast_to(lr_ref[...], shape), -jnp.broadcast_to(li_ref[...], shape)
        zero = jnp.zeros(shape, F32)
        fetch = pltpu.make_async_copy(keep_ref.at[d, k], xs_ref, keep_sem)
        fetch.start()

        def inputs(ci, stage):
            rows = _chunk_rows(ci, back, nc)
            dyb = dy_ref[rows, :].astype(BF16)
            dyb_ref[rows, :] = dyb
            dx = _dg(dyb, c_ref[...], NT)
            stage[...] = dx
            gs_ref[rows, :] = dx

        def first(stage, ci, carry):
            return _scan_chunk((stage, True), None, ar, ai, back, nj, ci, carry)

        def first_pass(t, carry):
            inputs(2 * t + 1, stage1)
            carry = first(stage0, 2 * t, carry)
            inputs(2 * t + 2, stage0)
            return first(stage1, 2 * t + 1, carry)

        inputs(0, stage0)
        carry = lax.fori_loop(0, nc // 2 - 1, first_pass, (zero, zero))
        inputs(nc - 1, stage1)
        carry = first(stage0, nc - 2, carry)
        end_r, end_i = first(stage1, nc - 1, carry)
        init = _segment_inits(ar, ai, end_r, end_i, back, nj)
        fetch.wait()
        db_ref[...] = jnp.zeros_like(db_ref)
        dc_ref[...] = jnp.zeros_like(dc_ref)

        @pl.when(d == 0)
        def _():
            du_ref[...] = jnp.zeros_like(du_ref)

        def outputs(ci, stage):
            rows = _chunk_rows(ci, back, nc)
            g = stage[...].astype(BF16)
            dc_ref[...] += _dg(xs_ref[rows, :].astype(BF16), dyb_ref[rows, :], TN)
            db_ref[...] += _dg(u_ref[rows, :], g, TN)
            du_ref[rows, :] += _dg(g, b_ref[...], NT)

        def second(ci, stage, carry):
            return _scan_chunk((gs_ref, False), (stage, True), ar, ai, back, nj, ci, carry, prev_ref=xs_ref)

        def second_pass(t, carry):
            outputs(2 * t, stage0)
            carry = second(2 * t + 1, stage1, carry)
            outputs(2 * t + 1, stage1)
            return second(2 * t + 2, stage0, carry)

        carry = lax.fori_loop(0, nc // 2 - 1, second_pass, second(0, stage0, init + (zero, zero)))
        outputs(nc - 2, stage0)
        gr, gi, acc_r, acc_i = second(nc - 1, stage1, carry)
        outputs(nc - 1, stage1)

        seg = lax.broadcasted_iota(jnp.int32, shape, 0)
        jb = jnp.where(rev, nj - 1, 0)
        erow = pl.ds(pl.multiple_of((nj - 1 - jb) * N_SEG, N_SEG), N_SEG)

        def before(t):
            up = jnp.where(seg >= 1, pltpu.roll(t, 1, 0), 0.0)
            down = jnp.where(seg <= N_SEG - 2, pltpu.roll(t, N_SEG - 1, 0), 0.0)
            return jnp.where(rev, down, up)

        init_r, init_i = before(xs_ref[erow, _RE]), before(xs_ref[erow, _IM])
        acc_r = acc_r + gr * init_r + gi * init_i
        acc_i = acc_i + gi * init_r - gr * init_i
        dlr_ref[...] = jnp.sum(acc_r, axis=0, keepdims=True)
        dli_ref[...] = jnp.sum(acc_i, axis=0, keepdims=True)

    act, bmat, cmat, lam = _ssm_specs(l)
    return pl.pallas_call(
        body, grid=(SSM_CB, 2), in_specs=[act, act, ANY, bmat, cmat, lam, lam],
        out_specs=[act, bmat, cmat, lam, lam],
        out_shape=[jax.ShapeDtypeStruct((l, SSM_WIDTH), F32),
                   jax.ShapeDtypeStruct(bcat.shape, F32), jax.ShapeDtypeStruct(ccat.shape, F32),
                   jax.ShapeDtypeStruct(lam_re.shape, F32), jax.ShapeDtypeStruct(lam_im.shape, F32)],
        scratch_shapes=[pltpu.VMEM((l, 2 * SSM_ST), F32), pltpu.VMEM((l, 2 * SSM_ST), F32),
                        pltpu.VMEM((l, SSM_CH), BF16),
                        pltpu.VMEM((SSM_RC, 2 * SSM_ST), F32), pltpu.VMEM((SSM_RC, 2 * SSM_ST), F32),
                        pltpu.SemaphoreType.DMA],
        name="ssm_bwd", compiler_params=_params(("parallel", "arbitrary"), vmem_mb=58),
    )(u_seg, dy_seg, states, bcat.astype(BF16), ccat.astype(BF16), lam_re, lam_im)


def _glu_fwd(y_ssm, u, d_skip, w_glu):
    l, w = u.shape

    def body(y_ref, u_ref, d_ref, w_ref, pre_ref, s_ref, ys_ref):
        pre = y_ref[...] + d_ref[...] * u_ref[...]
        z = _gelu(pre)
        s = _dg(z.astype(BF16), w_ref[...], NN)
        pre_ref[...] = pre
        s_ref[...] = s
        ys_ref[...] = z * _sigmoid(s)

    row = pl.BlockSpec((TM_EW, w), lambda i: (i, 0))
    out = jax.ShapeDtypeStruct((l, w), F32)
    return pl.pallas_call(
        body, grid=(l // TM_EW,),
        in_specs=[row, row, pl.BlockSpec((1, w), lambda i: (0, 0)), pl.BlockSpec((w, w), lambda i: (0, 0))],
        out_specs=[row, row, row], out_shape=[out, out, out], name="glu_fwd",
        compiler_params=_params(("parallel",)),
    )(y_ssm, u, d_skip, w_glu)


def _glu_bwd(pre, s, dys, u, d_skip, w_glu):
    l, w = u.shape

    def body(pre_ref, s_ref, dys_ref, u_ref, d_ref, w_ref, dpre_ref, z_ref, ds_ref, dd_ref):
        pre, dys = pre_ref[...], dys_ref[...]
        z = _gelu(pre)
        sig = _sigmoid(s_ref[...])
        ds = (dys * z * sig * (1.0 - sig)).astype(BF16)
        dz = dys * sig + _dg(ds, w_ref[...], NT)
        dpre = dz * _gelu_grad(pre)
        dpre_ref[...] = dpre
        z_ref[...] = z.astype(BF16)
        ds_ref[...] = ds

        @pl.when(pl.program_id(0) == 0)
        def _():
            dd_ref[...] = jnp.zeros_like(dd_ref)

        dd_ref[...] += jnp.sum(dpre * u_ref[...], axis=0, keepdims=True)

    row = pl.BlockSpec((TM_EW, w), lambda i: (i, 0))
    vec = pl.BlockSpec((1, w), lambda i: (0, 0))
    return pl.pallas_call(
        body, grid=(l // TM_EW,),
        in_specs=[row, row, row, row, vec, pl.BlockSpec((w, w), lambda i: (0, 0))],
        out_specs=[row, row, row, vec],
        out_shape=[jax.ShapeDtypeStruct((l, w), F32), jax.ShapeDtypeStruct((l, w), BF16),
                   jax.ShapeDtypeStruct((l, w), BF16), jax.ShapeDtypeStruct((1, w), F32)],
        name="glu_bwd", compiler_params=_params(("arbitrary",)),
    )(pre, s, dys, u, d_skip, w_glu)


TM_CV = 512
TC_CV = 256
TM_CF = 256
TC_CF = D_FF // 2
HALO = SUBLANES


def _conv_specs(l, col0, tm=TM_CV, tc=TC_CV):
    per = tm // HALO
    nh = l // HALO
    off = col0 // tc
    return [
        pl.BlockSpec((HALO, tc), lambda j, i: (jnp.maximum(i * per - 1, 0), j + off)),
        pl.BlockSpec((tm, tc), lambda j, i: (i, j + off)),
        pl.BlockSpec((HALO, tc), lambda j, i: (jnp.minimum((i + 1) * per, nh - 1), j + off)),
    ]


def _ext(prev_ref, mid_ref, next_ref, first, last):
    p = jnp.where(first, 0.0, prev_ref[...])
    n = jnp.where(last, 0.0, next_ref[...])
    return jnp.concatenate([p, mid_ref[...], n], axis=0)


def _shift_dn(t):
    return pltpu.roll(t, 1, 0)


def _shift_up(t):
    return pltpu.roll(t, t.shape[0] - 1, 0)


def _conv3(e, w_ref, b_ref):
    return w_ref[0:1, :] * _shift_dn(e) + w_ref[1:2, :] * e + w_ref[2:3, :] * _shift_up(e) + b_ref[...]


def _convffn_fwd(up_pre, conv_w, conv_b):
    l = up_pre.shape[0]
    tm, tc = TM_CF, TC_CF
    ni = l // tm
    wspec = lambda off: pl.BlockSpec((3, tc), lambda j, i: (0, j + off))
    bspec = lambda off: pl.BlockSpec((1, tc), lambda j, i: (0, j + off))
    voff = D_FF // tc

    def body(gp, gm, gn, vp, vm, vn, wg, bg, wv, bv, o_ref):
        i = pl.program_id(1)
        first, last = i == 0, i == ni - 1
        gate = _conv3(_ext(gp, gm, gn, first, last), wg, bg)[HALO:HALO + tm]
        val = _conv3(_ext(vp, vm, vn, first, last), wv, bv)[HALO:HALO + tm]
        o_ref[...] = (gate * _sigmoid(gate) * val).astype(BF16)

    return pl.pallas_call(
        body, grid=(D_FF // tc, ni),
        in_specs=_conv_specs(l, 0, tm, tc) + _conv_specs(l, D_FF, tm, tc)
        + [wspec(0), bspec(0), wspec(voff), bspec(voff)],
        out_specs=pl.BlockSpec((tm, tc), lambda j, i: (i, j)),
        out_shape=jax.ShapeDtypeStruct((l, D_FF), BF16), name="convffn_fwd",
        compiler_params=_params(("parallel", "parallel")),
    )(up_pre, up_pre, up_pre, up_pre, up_pre, up_pre, conv_w, conv_b, conv_w, conv_b)


HALO_B = 2 * SUBLANES


def _convffn_bwd(up_pre, dx2b, w_down, conv_w, conv_b):
    l = up_pre.shape[0]
    ni = l // TM_CV
    d = dx2b.shape[1]
    wspec = lambda off: pl.BlockSpec((3, TC_CV), lambda i, j: (0, j + off))
    bspec = lambda off: pl.BlockSpec((1, TC_CV), lambda i, j: (0, j + off))
    voff = D_FF // TC_CV
    swap = lambda spec: pl.BlockSpec(spec.block_shape, lambda i, j, f=spec.index_map: f(j, i))
    per, nh = TM_CV // HALO_B, l // HALO_B
    dx_specs = [pl.BlockSpec((HALO_B, d), lambda i, j: (jnp.maximum(i * per - 1, 0), 0)),
                pl.BlockSpec((TM_CV, d), lambda i, j: (i, 0)),
                pl.BlockSpec((HALO_B, d), lambda i, j: (jnp.minimum((i + 1) * per, nh - 1), 0))]

    def body(gp, gm, gn, vp, vm, vn, xp, xm, xn, wd, wg, bg, wv, bv, dup_ref, pg_ref, pv_ref):
        i = pl.program_id(0)
        first, last = i == 0, i == ni - 1
        ge, ve = _ext(gp, gm, gn, first, last), _ext(vp, vm, vn, first, last)
        zero = jnp.zeros((HALO_B, d), BF16)
        dx = jnp.concatenate([jnp.where(first, zero, xp[...]), xm[...], jnp.where(last, zero, xn[...])], axis=0)
        de = _dg(dx, wd[...], NT)[HALO_B - HALO:HALO_B + TM_CV + HALO]
        taps = [(_shift_dn(e), e, _shift_up(e)) for e in (ge, ve)]
        conv = lambda t, w_ref, b_ref: w_ref[0:1, :] * t[0] + w_ref[1:2, :] * t[1] + w_ref[2:3, :] * t[2] + b_ref[...]
        gate, val = conv(taps[0], wg, bg), conv(taps[1], wv, bv)
        sig = _sigmoid(gate)
        silu = gate * sig
        dgate = de * val * (sig + silu * (1.0 - sig))
        dval = de * silu
        mid = slice(HALO, HALO + TM_CV)
        rid = lax.broadcasted_iota(jnp.int32, (SUBLANES, TC_CV), 0)
        for half, (dup, tap, w_ref, p_ref) in enumerate(((dgate, taps[0], wg, pg_ref), (dval, taps[1], wv, pv_ref))):
            dpre = w_ref[0:1, :] * _shift_up(dup) + w_ref[1:2, :] * dup + w_ref[2:3, :] * _shift_dn(dup)
            dup_ref[half] = dpre[mid].astype(BF16)
            dm_ = dup[mid]
            sums = [jnp.sum(dm_ * t[mid], axis=0, keepdims=True) for t in tap]
            sums.append(jnp.sum(dm_, axis=0, keepdims=True))
            acc = jnp.zeros((SUBLANES, TC_CV), F32)
            for k, sk in enumerate(sums):
                acc = jnp.where(rid == k, sk, acc)
            p_ref[...] = acc

    par = pl.BlockSpec((None, SUBLANES, TC_CV), lambda i, j: (i, 0, j))
    dup, pg, pv = pl.pallas_call(
        body, grid=(ni, D_FF // TC_CV),
        in_specs=[swap(s) for s in _conv_specs(l, 0) + _conv_specs(l, D_FF)] + dx_specs
        + [pl.BlockSpec((TC_CV, d), lambda i, j: (j, 0)), wspec(0), bspec(0), wspec(voff), bspec(voff)],
        out_specs=[pl.BlockSpec((2, TM_CV, TC_CV), lambda i, j: (0, i, j)), par, par],
        out_shape=[jax.ShapeDtypeStruct((2, l, D_FF), BF16),
                   jax.ShapeDtypeStruct((ni, SUBLANES, D_FF), F32), jax.ShapeDtypeStruct((ni, SUBLANES, D_FF), F32)],
        name="convffn_bwd", compiler_params=_params(("parallel", "parallel")),
    )(up_pre, up_pre, up_pre, up_pre, up_pre, up_pre, dx2b, dx2b, dx2b, w_down, conv_w, conv_b, conv_w, conv_b)
    return dup, jnp.concatenate([jnp.sum(pg, axis=0), jnp.sum(pv, axis=0)], axis=1)


def _local_step(x, target, wb, sp, mixer_weights=None, late_weights=None, grads_ready=None,
                grads_next=None):
    l = x.shape[0]
    tabs = _rope_tables(l)
    disc = _ssm_disc(sp["a_re"], sp["a_im"], sp["log_step"], sp["b_re"], sp["b_im"])
    bcat, ccat, lam_re, lam_im = _ssm_pack(*disc, sp["c_re"], sp["c_im"])
    d_skip = sp["d_skip"].reshape(1, SSM_WIDTH)

    h, qkv, u = _rms_mm_rope(x, sp["norm_mix_g"], wb["w_in"], tabs, "mm_in")
    attn, lse = _attn_fwd(qkv, sp["sink"])
    y_seg, u_seg, states = _ssm_fwd(_to_segments(u), bcat, ccat, lam_re, lam_im)
    y_ssm = _from_segments(y_seg)
    if mixer_weights is not None:
        wb = dict(wb, **mixer_weights(attn))
    pre, s_glu, ys = _glu_fwd(y_ssm, u, d_skip, wb["w_glu"])
    mixed, x1, h2 = _mix_mm_res_rms(attn, ys, sp["norm_attn_g"], sp["norm_ssm_g"], wb["w_out"], x,
                                    sp["norm_ffn_g"], "mm_out")
    if late_weights is not None:
        wb = dict(wb, **late_weights(h2))
    up_pre = _mm_nn_cols(h2, wb["w_up"], min(l, 1024), "mm_up")
    conv_w = wb["conv_w"]
    act = _convffn_fwd(up_pre, conv_w, sp["conv_b"])
    loss, dx2, dx2b, d_final_g = _mm_res_loss(act, wb["w_down"], x1, sp["norm_final_g"].reshape(1, D_MODEL), target)

    g = {"norm_final_g": d_final_g.reshape(D_MODEL)}
    g["w_down"] = _mm_tn(act, dx2b, D_FF // 2, 512, "mm_down_dw")
    dup_pre, conv_par = _convffn_bwd(up_pre, dx2b, wb["w_down"], conv_w, sp["conv_b"])
    g["conv_w"], g["conv_b"] = conv_par[0:3], conv_par[3:4]
    g["w_up"] = _mm_tn_cols(h2, dup_pre, wb["w_up"].shape[0], 512, "mm_up_dw")
    dx1, dx1b, g["norm_ffn_g"] = _mm_cols_rms_bwd(dup_pre, wb["w_up"], x1, sp["norm_ffn_g"], dx2, "mm_up_dx")
    g["w_out"] = _mm_tn(mixed, dx1b, 1024, 1024, "mm_out_dw")
    zero = grads_ready(g["w_up"], g["w_down"], g["w_out"]) if grads_ready is not None else 0.0
    dattn, dys, g["norm_attn_g"], g["norm_ssm_g"] = _mm_mix_bwd(
        dx1b, wb["w_out"], attn, ys, sp["norm_attn_g"] + zero, sp["norm_ssm_g"], "mm_out_dx")
    dpre, zb, dsb, dd = _glu_bwd(pre, s_glu, dys, u, d_skip, wb["w_glu"])
    g["d_skip"] = dd.reshape(N_SSM_GROUPS, SSM_GROUP)
    g["w_glu"] = _mm_tn(zb, dsb, 512, 512, "mm_glu_dw")
    zero = grads_next(g["w_glu"]) if grads_next is not None else 0.0
    du_seg, dbcat, dccat, dlam_re, dlam_im = _ssm_bwd(u_seg, _to_segments(dpre), states, bcat, ccat,
                                                      lam_re + zero, lam_im)
    dlb_re, dlb_im, dbb_re, dbb_im, g["c_re"], g["c_im"] = _ssm_unpack(dbcat, dccat, dlam_re, dlam_im)
    _, disc_vjp = jax.vjp(_ssm_disc, sp["a_re"], sp["a_im"], sp["log_step"], sp["b_re"], sp["b_im"])
    g["a_re"], g["a_im"], g["log_step"], g["b_re"], g["b_im"] = disc_vjp((dlb_re, dlb_im, dbb_re, dbb_im))
    dq, dkv, g["sink"] = _attn_bwd(qkv, attn, dattn, lse, sp["sink"])
    dproj = _rope_bwd(dq, dkv, _from_segments(du_seg), dpre, d_skip, tabs)
    g["w_in"] = _mm_tn(dproj, h, IN_WIDTH // 5, D_MODEL, "mm_in_dw")
    grad_x, _, g["norm_mix_g"] = _mm_nn_rms_bwd(dproj, wb["w_in"], x, sp["norm_mix_g"], dx1, "mm_in_dx")
    return loss, grad_x, g


MESH = pl.DeviceIdType.MESH
ANY = pl.BlockSpec(memory_space=pl.ANY)


def _place():
    x, y, c = lax.axis_index("x"), lax.axis_index("y"), lax.axis_index("c")
    chips = [(1 - x, y), (x, 1 - y), (1 - x, 1 - y)]
    return x, y, c, chips


def _chip_index(px, py):
    return 2 * px + py


CHUNK_BYTES = 256 * 1024
MAX_CHUNKS = 16


def _row_chunks(rows, row_bytes, align):
    n = max(1, min(MAX_CHUNKS, (rows * row_bytes) // CHUNK_BYTES))
    per = -(-rows // n)
    per = -(-per // align) * align
    return [(r0, min(per, rows - r0)) for r0 in range(0, rows, per)]


def _align_of(dtype):
    return SUBLANES * 4 // jnp.dtype(dtype).itemsize


def _remote(src, dst, send_sem, recv_sem, to):
    return pltpu.make_async_remote_copy(src_ref=src, dst_ref=dst, send_sem=send_sem, recv_sem=recv_sem,
                                        device_id=to, device_id_type=MESH)


CAST_ROWS = 64


def _gather_weights(shards, dtypes):
    nw = len(shards)

    def body(*refs):
        w_refs, o_refs = refs[:nw], refs[nw:2 * nw]
        send_sems, recv_sems, in_sems, out_sems = refs[2 * nw:2 * nw + 4]
        raw, cast = refs[2 * nw + 4:3 * nw + 4], refs[3 * nw + 4:]
        x, y, c, chips = _place()
        mine = _chip_index(x, y)
        sibling = (x, y, 1 - c)

        def rows_of(ref, chip, r0, nr):
            return ref.at[chip, pl.ds(r0, nr), :]

        def copy(wi, k, src, dst, to):
            return _remote(src, dst, send_sems.at[wi, k], recv_sems.at[wi, k], to)

        geo = []
        for wi in range(nw):
            rows, cols = w_refs[wi].shape
            row_bytes = cols * jnp.dtype(dtypes[wi]).itemsize
            geo.append((rows // 2, _row_chunks(rows // 2, row_bytes, _align_of(dtypes[wi]))))

        stage_in = [pltpu.make_async_copy(w_refs[wi], raw[wi], in_sems.at[wi]) for wi in range(nw)]
        for cp in stage_in:
            cp.start()
        staged = [raw[wi] if dtypes[wi] == w_refs[wi].dtype else cast[wi] for wi in range(nw)]
        stage_out = []
        for wi in range(nw):
            stage_in[wi].wait()
            if staged[wi] is not raw[wi]:
                def cast_rows(i, _, wi=wi):
                    rows = pl.ds(pl.multiple_of(i * CAST_ROWS, CAST_ROWS), CAST_ROWS)
                    cast[wi][rows, :] = raw[wi][rows, :].astype(dtypes[wi])
                    return 0

                lax.fori_loop(0, w_refs[wi].shape[0] // CAST_ROWS, cast_rows, 0)
            cp = pltpu.make_async_copy(staged[wi], o_refs[wi].at[mine], out_sems.at[wi])
            cp.start()
            stage_out.append(cp)

        for wi in range(nw):
            hr, half_chunks = geo[wi]
            for j, chip in enumerate(chips):
                for r0, nr in half_chunks:
                    copy(wi, j, staged[wi].at[pl.ds(c * hr + r0, nr), :],
                         rows_of(o_refs[wi], mine, c * hr + r0, nr), (*chip, c)).start()
        for wi in range(nw):
            hr, half_chunks = geo[wi]
            for j, chip in enumerate(chips):
                got = rows_of(o_refs[wi], _chip_index(*chip), c * hr, hr)
                copy(wi, j, got, got, (*chip, c)).wait_recv()
                for r0, nr in half_chunks:
                    piece = rows_of(o_refs[wi], _chip_index(*chip), c * hr + r0, nr)
                    copy(wi, 3 + j, piece, piece, sibling).start()
        for wi in range(nw):
            hr = geo[wi][0]
            for j, chip in enumerate(chips):
                got = rows_of(o_refs[wi], _chip_index(*chip), (1 - c) * hr, hr)
                copy(wi, 3 + j, got, got, sibling).wait_recv()
        for wi in range(nw):
            hr = geo[wi][0]
            sent = rows_of(o_refs[wi], mine, c * hr, hr)
            for k in range(6):
                copy(wi, k, sent, sent, sibling).wait_send()
            stage_out[wi].wait()

    return pl.pallas_call(
        body, in_specs=[ANY] * nw, out_specs=[ANY] * nw,
        out_shape=[jax.ShapeDtypeStruct((4, *s.shape), t) for s, t in zip(shards, dtypes)],
        scratch_shapes=[pltpu.SemaphoreType.DMA((nw, 6)), pltpu.SemaphoreType.DMA((nw, 6)),
                        pltpu.SemaphoreType.DMA((nw,)), pltpu.SemaphoreType.DMA((nw,))]
        + [pltpu.VMEM(s.shape, s.dtype) for s in shards] + [pltpu.VMEM(s.shape, t) for s, t in zip(shards, dtypes)],
        name="gather_weights", compiler_params=_params(vmem_mb=40),
    )(*shards)


HBM = pl.BlockSpec(memory_space=pltpu.HBM)
SEM = pl.BlockSpec(memory_space=pltpu.SEMAPHORE)
EFFECT = pltpu.SideEffectType.DATAFLOW_SIDE_EFFECTING


def _cast_place(w, place, dtype, after, name):
    rows, cols = w.shape
    tr = _row_tile(rows, cols, _align_of(dtype))

    def body(p_ref, w_ref, after_ref, o_ref):
        del p_ref, after_ref
        o_ref[...] = w_ref[...].astype(dtype)

    grid_spec = pltpu.PrefetchScalarGridSpec(
        num_scalar_prefetch=1, grid=(rows // tr,),
        in_specs=[pl.BlockSpec((tr, cols), lambda i, p: (i, 0)), ANY],
        out_specs=pl.BlockSpec((None, tr, cols), lambda i, p: (p[1], i, 0)))
    return pl.pallas_call(body, grid_spec=grid_spec, out_shape=jax.ShapeDtypeStruct((4, rows, cols), dtype),
                          name=name, compiler_params=_params(("parallel",)))(place, w, after)


def _split_start(name, arrays, n_pairs, issue):
    n = len(arrays)

    def body(*refs):
        issue(refs[:n], refs[n:n + n_pairs], refs[n + n_pairs:n + 2 * n_pairs])
        token = refs[2 * n + 2 * n_pairs]
        token[...] = jnp.zeros_like(token)

    dma = pltpu.SemaphoreType.DMA(())
    outs = pl.pallas_call(
        body, name=name,
        out_shape=[dma] * (2 * n_pairs) + [pltpu.HBM(t.shape, t.dtype) for t in arrays]
        + [jax.ShapeDtypeStruct((SUBLANES, LANES), F32)],
        in_specs=[HBM] * n, out_specs=[SEM] * (2 * n_pairs) + [HBM] * n + [pl.BlockSpec(memory_space=pltpu.VMEM)],
        input_output_aliases={a: 2 * n_pairs + a for a in range(n)},
        compiler_params=pltpu.CompilerParams(has_side_effects=EFFECT),
    )(*[pltpu.with_memory_space_constraint(t, pltpu.HBM) for t in arrays])
    return outs[:n_pairs], outs[n_pairs:2 * n_pairs], outs[2 * n_pairs:2 * n_pairs + n], outs[-1]


def _split_wait(name, send_sems, recv_sems, flying, sizes, after):
    n, n_pairs = len(flying), len(send_sems)

    def body(*refs):
        x, y, c, _ = _place()
        for k, ref in enumerate(sizes(refs[:n])):
            cp = _remote(ref, ref, refs[n + k], refs[n + n_pairs + k], (x, y, 1 - c))
            cp.wait_send()
            cp.wait_recv()

    return pl.pallas_call(
        body, name=name, out_shape=[pltpu.HBM(t.shape, t.dtype) for t in flying],
        in_specs=[HBM] * n + [SEM] * (2 * n_pairs) + [ANY], out_specs=[HBM] * n,
        input_output_aliases={a: a for a in range(n)},
        compiler_params=pltpu.CompilerParams(has_side_effects=EFFECT),
    )(*flying, *send_sems, *recv_sems, after)


def _spread_start(lands, name):
    def issue(land_refs, send_sems, recv_sems):
        x, y, c, chips = _place()
        mine = _chip_index(x, y)
        for a, land in enumerate(land_refs):
            _, rows, cols = land.shape
            hr = rows // 2
            row_bytes = cols * jnp.dtype(land.dtype).itemsize
            for r0, nr in _row_chunks(hr, row_bytes, _align_of(land.dtype)):
                piece = land.at[mine, pl.ds(c * hr + r0, nr), :]
                for chip in chips:
                    for core in (0, 1):
                        _remote(piece, piece, send_sems[a], recv_sems[a], (*chip, core)).start()

    return _split_start(name, lands, len(lands), issue)


def _spread_wait(send_sems, recv_sems, flying, after, name):
    return _split_wait(name, send_sems, recv_sems, flying, lambda refs: [r.at[pl.ds(0, 3)] for r in refs], after)


def _pair_start(grads):
    n = len(grads)
    zones = [lax.empty((4, g.shape[1] // 2, g.shape[2]), F32) for g in grads]

    def issue(refs, send_sems, recv_sems):
        x, y, c, _ = _place()
        for a in range(n):
            g_ref, z_ref = refs[a], refs[n + a]
            _, rows, cols = g_ref.shape
            hr = rows // 2
            for k in range(4):
                for r0, nr in _row_chunks(hr, cols * 4, SUBLANES):
                    _remote(g_ref.at[k, pl.ds((1 - c) * hr + r0, nr), :], z_ref.at[k, pl.ds(r0, nr), :],
                            send_sems[a], recv_sems[a], (x, y, 1 - c)).start()

    return _split_start("pair_start", list(grads) + zones, n, issue)


def _pair_wait(send_sems, recv_sems, flying, after):
    n = len(flying) // 2
    out = _split_wait("pair_wait", send_sems, recv_sems, flying, lambda refs: list(refs[n:]), after)
    return out[:n], out[n:]


def _chip_start(sums):
    n = len(sums)
    zones = [lax.empty((3, *s.shape[1:]), s.dtype) for s in sums]

    def issue(refs, send_sems, recv_sems):
        x, y, c, chips = _place()
        for a in range(n):
            s_ref, z_ref = refs[a], refs[n + a]
            _, rows, cols = s_ref.shape
            row_bytes = cols * jnp.dtype(s_ref.dtype).itemsize
            for r0, nr in _row_chunks(rows, row_bytes, _align_of(s_ref.dtype)):
                for j, chip in enumerate(chips):
                    _remote(s_ref.at[_chip_index(*chip), pl.ds(r0, nr), :], z_ref.at[j, pl.ds(r0, nr), :],
                            send_sems[a], recv_sems[a], (*chip, c)).start()

    return _split_start("chip_start", list(sums) + zones, n, issue)


def _chip_wait(send_sems, recv_sems, flying, after):
    n = len(flying) // 2
    return _split_wait("chip_wait", send_sems, recv_sems, flying, lambda refs: list(refs[n:]), after)[n:]


def _pair_exchange(grads):
    na = len(grads)

    def body(*refs):
        g_refs, o_refs = refs[:na], refs[na:2 * na]
        send_sems, recv_sems = refs[2 * na:]
        x, y, c, _ = _place()
        sibling = (x, y, 1 - c)
        for ai in range(na):
            _, rows, cols = g_refs[ai].shape
            hr = rows // 2
            for k in range(4):
                for r0, nr in _row_chunks(hr, cols * 4, SUBLANES):
                    _remote(g_refs[ai].at[k, pl.ds((1 - c) * hr + r0, nr), :], o_refs[ai].at[k, pl.ds(r0, nr), :],
                            send_sems.at[ai], recv_sems.at[ai], sibling).start()
        for ai in range(na):
            _remote(o_refs[ai], o_refs[ai], send_sems.at[ai], recv_sems.at[ai], sibling).wait()

    return pl.pallas_call(
        body, in_specs=[ANY] * na, out_specs=[ANY] * na,
        out_shape=[jax.ShapeDtypeStruct((4, g.shape[1] // 2, g.shape[2]), F32) for g in grads],
        scratch_shapes=[pltpu.SemaphoreType.DMA((na,)), pltpu.SemaphoreType.DMA((na,))],
        name="pair_exchange",
    )(*grads)


def _row_tile(rows, cols, align, elems=256 * 1024):
    best = align
    for cand in range(align, rows + 1, align):
        if rows % cand == 0 and cand * cols <= elems:
            best = cand
    return best


def _pair_sum(g, got, place, transit, name):
    _, rows, cols = g.shape
    hr = rows // 2
    tr = _row_tile(hr, cols, _align_of(transit), 512 * 1024)
    nt = hr // tr

    def body(p_ref, g_ref, r_ref, s_ref, own_ref):
        total = g_ref[...] + r_ref[...]
        s_ref[...] = total.astype(transit)

        @pl.when(pl.program_id(1) == p_ref[1])
        def _():
            own_ref[...] = total

    grid_spec = pltpu.PrefetchScalarGridSpec(
        num_scalar_prefetch=1, grid=(nt, 4),
        in_specs=[pl.BlockSpec((None, tr, cols), lambda i, k, p: (k, p[0] * nt + i, 0)),
                  pl.BlockSpec((None, tr, cols), lambda i, k, p: (k, i, 0))],
        out_specs=[pl.BlockSpec((None, tr, cols), lambda i, k, p: (k, i, 0)),
                   pl.BlockSpec((tr, cols), lambda i, k, p: (i, 0))])
    return pl.pallas_call(
        body, grid_spec=grid_spec,
        out_shape=[jax.ShapeDtypeStruct((4, hr, cols), transit), jax.ShapeDtypeStruct((hr, cols), F32)],
        name=name, compiler_params=_params(("parallel", "arbitrary")),
    )(place, g, got)


def _chip_exchange(sums):
    na = len(sums)

    def body(*refs):
        s_refs, o_refs = refs[:na], refs[na:2 * na]
        send_sems, recv_sems = refs[2 * na:]
        x, y, c, chips = _place()
        for ai in range(na):
            _, rows, cols = s_refs[ai].shape
            row_bytes = cols * jnp.dtype(s_refs[ai].dtype).itemsize
            for r0, nr in _row_chunks(rows, row_bytes, _align_of(s_refs[ai].dtype)):
                for j, chip in enumerate(chips):
                    _remote(s_refs[ai].at[_chip_index(*chip), pl.ds(r0, nr), :], o_refs[ai].at[j, pl.ds(r0, nr), :],
                            send_sems.at[ai, j], recv_sems.at[ai, j], (*chip, c)).start()
        for ai in range(na):
            for j, chip in enumerate(chips):
                _remote(o_refs[ai].at[j], o_refs[ai].at[j], send_sems.at[ai, j], recv_sems.at[ai, j],
                        (*chip, c)).wait()

    return pl.pallas_call(
        body, in_specs=[ANY] * na, out_specs=[ANY] * na,
        out_shape=[jax.ShapeDtypeStruct((3, *s.shape[1:]), s.dtype) for s in sums],
        scratch_shapes=[pltpu.SemaphoreType.DMA((na, 3)), pltpu.SemaphoreType.DMA((na, 3))],
        name="chip_exchange",
    )(*sums)


def _chip_sum(own, landed, name):
    hr, cols = own.shape
    tr = _row_tile(hr, cols, _align_of(landed.dtype))

    def body(o_ref, l_ref, f_ref):
        acc = o_ref[...]
        for j in range(3):
            acc = acc + l_ref[j].astype(F32)
        f_ref[...] = acc

    return pl.pallas_call(
        body, grid=(hr // tr,),
        in_specs=[pl.BlockSpec((tr, cols), lambda i: (i, 0)), pl.BlockSpec((3, tr, cols), lambda i: (0, i, 0))],
        out_specs=pl.BlockSpec((tr, cols), lambda i: (i, 0)),
        out_shape=jax.ShapeDtypeStruct((hr, cols), F32), name=name,
        compiler_params=_params(("parallel",)),
    )(own, landed)


def _final_exchange(halves, small):
    nh = len(halves)

    def body(*refs):
        h_refs, s_ref = refs[:nh], refs[nh]
        o_refs, so_ref = refs[nh + 1:2 * nh + 1], refs[2 * nh + 1]
        send_sems, recv_sems, local_sem, ssend_sems, srecv_sems = refs[2 * nh + 2:]
        x, y, c, _ = _place()
        me = 4 * x + 2 * y + c
        sibling = (x, y, 1 - c)
        for hi in range(nh):
            hr, cols = h_refs[hi].shape
            for r0, nr in _row_chunks(hr, cols * 4, SUBLANES):
                _remote(h_refs[hi].at[pl.ds(r0, nr), :], o_refs[hi].at[pl.ds(r0, nr), :],
                        send_sems.at[hi], recv_sems.at[hi], sibling).start()
        small_cps = [pltpu.make_async_copy(s_ref, so_ref.at[me], local_sem)]
        for r in range(1, 8):
            fx, fy, fc = (r >> 2) & 1, (r >> 1) & 1, r & 1
            peer = (1 - x if fx else x, 1 - y if fy else y, 1 - c if fc else c)
            small_cps.append(_remote(s_ref, so_ref.at[me], ssend_sems.at[r - 1], srecv_sems.at[r - 1], peer))
        for cp in small_cps:
            cp.start()
        for hi in range(nh):
            _remote(h_refs[hi], o_refs[hi], send_sems.at[hi], recv_sems.at[hi], sibling).wait()
        for cp in small_cps:
            cp.wait()

    return pl.pallas_call(
        body, in_specs=[ANY] * (nh + 1), out_specs=[ANY] * (nh + 1),
        out_shape=[jax.ShapeDtypeStruct(h.shape, F32) for h in halves]
        + [jax.ShapeDtypeStruct((8, *small.shape), F32)],
        scratch_shapes=[pltpu.SemaphoreType.DMA((nh,)), pltpu.SemaphoreType.DMA((nh,)),
                        pltpu.SemaphoreType.DMA, pltpu.SemaphoreType.DMA((7,)), pltpu.SemaphoreType.DMA((7,))],
        name="final_exchange",
    )(*halves, small)


def _adamw_halves(w, own, other, m, v, place, name):
    r, c = w.shape
    hr = r // 2
    tr = _row_tile(hr, c, SUBLANES, 384 * 1024)
    nt = hr // tr
    c1 = 1.0 - ADAM_B1 ** ADAM_STEP
    c2 = 1.0 - ADAM_B2 ** ADAM_STEP

    def body(p_ref, w_ref, own_ref, other_ref, m_ref, v_ref, g_ref, d_ref, nm_ref, nv_ref):
        mine = pl.program_id(0) // nt == p_ref[0]
        gv = jnp.where(mine, own_ref[...], other_ref[...])
        nm = ADAM_B1 * m_ref[...] + (1.0 - ADAM_B1) * gv
        nv = ADAM_B2 * v_ref[...] + (1.0 - ADAM_B2) * (gv * gv)
        g_ref[...] = gv
        d_ref[...] = -ADAM_LR * ((nm / c1) / (jnp.sqrt(nv / c2) + ADAM_EPS) + ADAM_WD * w_ref[...])
        nm_ref[...] = nm
        nv_ref[...] = nv

    full = pl.BlockSpec((tr, c), lambda i, p: (i, 0))
    own_half = pl.BlockSpec((tr, c), lambda i, p: (jnp.where(i // nt == p[0], i % nt, 0), 0))
    other_half = pl.BlockSpec((tr, c), lambda i, p: (jnp.where(i // nt == p[0], 0, i % nt), 0))
    out = jax.ShapeDtypeStruct((r, c), F32)
    grid_spec = pltpu.PrefetchScalarGridSpec(num_scalar_prefetch=1, grid=(2 * nt,),
                                             in_specs=[full, own_half, other_half, full, full],
                                             out_specs=[full] * 4)
    return pl.pallas_call(body, grid_spec=grid_spec, out_shape=[out] * 4, name=name,
                          compiler_params=_params(("parallel",)))(place, w, own, other, m, v)


def _adamw_many(ws, gs, ms, vs, name):
    n = len(ws)
    c1 = 1.0 - ADAM_B1 ** ADAM_STEP
    c2 = 1.0 - ADAM_B2 ** ADAM_STEP

    def body(*refs):
        w_refs, g_refs, m_refs, v_refs = (refs[k * n:(k + 1) * n] for k in range(4))
        d_refs, nm_refs, nv_refs = (refs[(4 + k) * n:(5 + k) * n] for k in range(3))
        for i in range(n):
            gv = g_refs[i][...]
            nm = ADAM_B1 * m_refs[i][...] + (1.0 - ADAM_B1) * gv
            nv = ADAM_B2 * v_refs[i][...] + (1.0 - ADAM_B2) * (gv * gv)
            d_refs[i][...] = -ADAM_LR * ((nm / c1) / (jnp.sqrt(nv / c2) + ADAM_EPS) + ADAM_WD * w_refs[i][...])
            nm_refs[i][...] = nm
            nv_refs[i][...] = nv

    vmem = pl.BlockSpec(memory_space=pltpu.VMEM)
    shapes = [jax.ShapeDtypeStruct(t.shape, F32) for t in ws]
    outs = pl.pallas_call(body, in_specs=[vmem] * (4 * n), out_specs=[vmem] * (3 * n), out_shape=shapes * 3,
                          name=name, compiler_params=_params(vmem_mb=56))(*ws, *gs, *ms, *vs)
    return outs[:n], outs[n:2 * n], outs[2 * n:]


BIG = ("w_in", "w_glu", "w_out", "w_up", "w_down")
WEIGHTS = ("norm_mix_g", "w_in", "a_re", "a_im", "log_step", "b_re", "b_im", "c_re", "c_im", "d_skip", "w_glu",
           "sink", "norm_attn_g", "norm_ssm_g", "w_out", "norm_ffn_g", "w_up", "conv_w", "conv_b", "w_down",
           "norm_final_g")
SMALL = ("norm_mix_g", "a_re", "a_im", "log_step", "b_re", "b_im", "c_re", "c_im", "d_skip", "sink",
         "norm_attn_g", "norm_ssm_g", "norm_ffn_g", "conv_w", "conv_b", "norm_final_g")
SMALL_ROWS = 48
N_DEV = 8


def _tile_rows(size):
    return -(-size // (SUBLANES * D_MODEL)) * SUBLANES


def _by_owner(name, g):
    if name == "w_up":
        return g
    return g.reshape(4, g.shape[0] // 4, g.shape[1])


def _view(name, t):
    if name == "w_in":
        return jnp.swapaxes(t[0], 0, 1)
    if name in ("b_re", "b_im"):
        return jnp.swapaxes(t, -1, -2)
    return t


def _unview(name, t):
    if name == "w_in":
        return jnp.swapaxes(t, 0, 1)[None]
    if name in ("b_re", "b_im"):
        return jnp.swapaxes(t, -1, -2)
    return t


def kernel(x, norm_mix_g, w_in, a_re, a_im, log_step, b_re, b_im, c_re, c_im, d_skip, w_glu, sink, norm_attn_g, norm_ssm_g, w_out, norm_ffn_g, w_up, conv_w, conv_b, w_down, norm_final_g, loss_target, m_norm_mix_g, m_w_in, m_a_re, m_a_im, m_log_step, m_b_re, m_b_im, m_c_re, m_c_im, m_d_skip, m_w_glu, m_sink, m_norm_attn_g, m_norm_ssm_g, m_w_out, m_norm_ffn_g, m_w_up, m_conv_w, m_conv_b, m_w_down, m_norm_final_g, v_norm_mix_g, v_w_in, v_a_re, v_a_im, v_log_step, v_b_re, v_b_im, v_c_re, v_c_im, v_d_skip, v_w_glu, v_sink, v_norm_attn_g, v_norm_ssm_g, v_w_out, v_norm_ffn_g, v_w_up, v_conv_w, v_conv_b, v_w_down, v_norm_final_g):
    given = dict(locals())
    w = {n: given[n] for n in WEIGHTS}
    m = {n: given["m_" + n] for n in WEIGHTS}
    v = {n: given["v_" + n] for n in WEIGHTS}
    xy = 2 * lax.axis_index("x") + lax.axis_index("y")

    core = lax.axis_index("c")
    place = jnp.stack([core, xy]).astype(jnp.int32)

    conv_rows = jnp.pad(w["conv_w"][0], ((0, 2 * SUBLANES - 3), (0, 0)))
    rows = lambda t: t.reshape(4 * t.shape[1], t.shape[2])
    (w_in_all,) = _gather_weights([_view("w_in", w["w_in"])], [BF16])
    wb = {"w_in": rows(w_in_all)}
    mixer = [_cast_place(w[n][0], place, BF16, w_in_all, "cast_" + n) for n in ("w_glu", "w_out")]
    mixer.append(_cast_place(conv_rows, place, F32, w_in_all, "cast_conv_w"))
    *mixer_flight, mixer_token = _spread_start(mixer, "spread_mixer_start")
    late = ("w_up", "w_down")
    *late_flight, token = _spread_start(
        [_cast_place(w[n][0], place, BF16, mixer_token, "cast_" + n) for n in late], "spread_ffn_start")

    def mixer_weights(after):
        w_glu4, w_out4, conv4 = _spread_wait(*mixer_flight, after, "spread_mixer_wait")
        return {"w_glu": rows(w_glu4), "w_out": rows(w_out4),
                "conv_w": conv4[:, :3].transpose(1, 0, 2).reshape(3, 2 * D_FF)}

    def late_weights(after):
        w_up4, w_down4 = _spread_wait(*late_flight, after, "spread_ffn_wait")
        return {"w_up": w_up4, "w_down": rows(w_down4)}

    sp = {n: w[n][0] for n in ("a_re", "a_im", "log_step", "b_re", "b_im", "c_re", "c_im", "d_skip",
                               "norm_mix_g", "norm_attn_g", "norm_ssm_g", "norm_ffn_g", "sink", "conv_b")}
    for n in ("norm_mix_g", "norm_attn_g", "norm_ssm_g", "norm_ffn_g", "sink", "conv_b"):
        sp[n] = sp[n].reshape(1, -1)
    sp["norm_mix_g"] = sp["norm_mix_g"] + token[:1, :1]
    sp["norm_final_g"] = w["norm_final_g"]
    early, tail = late + ("w_out",), ("w_in", "w_glu")
    flight = {}

    def grads_ready(dw_up, dw_down, dw_out):
        *flight["pair"], token = _pair_start([dw_up, _by_owner("w_down", dw_down), _by_owner("w_out", dw_out)])
        return token[:1, :1]

    def grads_next(after):
        mine, got = _pair_wait(*flight["pair"], after)
        sums, flight["own"] = zip(*[_pair_sum(a, b, place, BF16, "pair_sum_" + n) for n, a, b in zip(early, mine, got)])
        *flight["chip"], token = _chip_start(list(sums))
        return token[:1, :1]

    loss, grad_x, g = _local_step(x[0], loss_target[0], wb, sp, mixer_weights, late_weights, grads_ready,
                                  grads_next)

    def as_rows(t):
        rows = _tile_rows(t.size)
        return jnp.pad(t.reshape(-1), (0, rows * D_MODEL - t.size)).reshape(rows, D_MODEL)

    pieces = [as_rows(g[n]) for n in SMALL] + [as_rows(loss)]
    spare = N_DEV * SMALL_ROWS - sum(p.shape[0] for p in pieces)
    small = jnp.concatenate(pieces + [jnp.zeros((spare, D_MODEL), F32)]).reshape(4, 2 * SMALL_ROWS, D_MODEL)
    by_owner = [_by_owner(n, g[n]) for n in tail] + [small]
    got = _pair_exchange(by_owner)
    transit = [BF16] * len(tail) + [F32]
    chip_sums, own_sums = zip(*[_pair_sum(a, b, place, t, "pair_sum_" + n)
                                for n, a, b, t in zip(tail + ("small",), by_owner, got, transit)])
    landed = _chip_exchange(list(chip_sums))
    halves = {n: _chip_sum(o, t, "chip_sum_" + n) for n, o, t in zip(tail + ("small",), own_sums, landed)}
    early_landed = _chip_wait(*flight["chip"], grad_x)
    for n, o, t in zip(early, flight["own"], early_landed):
        halves[n] = _chip_sum(o, t, "chip_sum_" + n)
    *others, small_all = _final_exchange([halves[n] for n in BIG], halves["small"])
    small_all = small_all.reshape(N_DEV * SMALL_ROWS, D_MODEL)
    grads, row = {}, 0
    for n in SMALL:
        shape = (3, 4 * w[n].shape[-1]) if n == "conv_w" else w[n].shape[1:] if n != "norm_final_g" else w[n].shape
        size = math.prod(shape)
        grads[n] = small_all[row:row + _tile_rows(size)].reshape(-1)[:size].reshape(shape)
        row += _tile_rows(size)
    loss = small_all[row, 0]
    cw = w["conv_w"].shape[-1]
    grads["conv_w"] = lax.dynamic_slice_in_dim(grads["conv_w"], xy * cw, cw, axis=1)
    grads = {n: _view(n, grads[n].reshape(w[n].shape)) for n in SMALL}
    wv, mv, vv = ({n: _view(n, t[n]) for n in WEIGHTS} for t in (w, m, v))

    delta, new_m, new_v = {}, {}, {}
    for n, other in zip(BIG, others):
        two_d = lambda t: t.reshape(t.shape[-2:])
        grads[n], delta[n], new_m[n], new_v[n] = _adamw_halves(
            two_d(wv[n]), halves[n], other, two_d(mv[n]), two_d(vv[n]), place, "adamw_" + n)
    for group, name in ((("b_re", "b_im"), "adamw_b"), (tuple(n for n in SMALL if n not in ("b_re", "b_im")), "adamw_small")):
        row = lambda t: t.reshape(1, -1) if t.ndim == 1 else t
        d_, m_, v_ = _adamw_many(*[[row(t[n]) for n in group] for t in (wv, grads, mv, vv)], name)
        for n, dn, mn, vn in zip(group, d_, m_, v_):
            delta[n], new_m[n], new_v[n] = (t.reshape(wv[n].shape) for t in (dn, mn, vn))
    natural = lambda t: [_unview(n, t[n].reshape(wv[n].shape)) for n in WEIGHTS]
    return (loss, grad_x[None], *natural(grads), *natural(delta), *natural(new_m), *natural(new_v))
```

```python
import functools
import math

import jax
import jax.numpy as jnp
import numpy as np
from jax import lax
from jax.experimental import pallas as pl
from jax.experimental.pallas import tpu as pltpu

F32 = jnp.float32
BF16 = jnp.bfloat16

D_MODEL = 1024
N_Q_HEADS = 8
N_KV_HEADS = 2
HEAD_DIM = 64
ATTN_WIDTH = 512
KV_WIDTH = 128
QKV_WIDTH = ATTN_WIDTH + 2 * KV_WIDTH
WINDOW = 128
BLOCK = 128
ROPE_DIM = 16
ROPE_THETA = 500000.0
SCORE_SCALE = HEAD_DIM ** -0.5
SSM_WIDTH = 512
SSM_GROUP = 16
N_SSM_GROUPS = 32
SSM_STATE = 64
IN_WIDTH = 1280
D_FF = 2816
EPS = 1e-6
ADAM_LR = 0.001
ADAM_B1 = 0.9
ADAM_B2 = 0.999
ADAM_EPS = 1e-08
ADAM_WD = 0.01
ADAM_STEP = 10

VMEM_BYTES_V7X = 64 * 1024 * 1024
SUBLANES = 8
LANES = 128
SSM_CB = 4
SSM_CH = 128
SSM_ST = 512
N_SEG = SUBLANES

NN = (((1,), (0,)), ((), ()))
NT = (((1,), (1,)), ((), ()))
TN = (((0,), (0,)), ((), ()))


def _params(sem=None, vmem_mb=48):
    limit = vmem_mb * 1024 * 1024
    assert limit < VMEM_BYTES_V7X
    return pltpu.CompilerParams(dimension_semantics=sem, vmem_limit_bytes=limit)


def _dg(a, b, dims):
    return lax.dot_general(a, b, dims, preferred_element_type=F32)


def _sigmoid(x):
    return 1.0 / (1.0 + jnp.exp(-x))


_SQRT_HALF = 0.7071067811865476
_INV_SQRT_2PI = 0.3989422804014327


def _gelu(x):
    return 0.5 * x * (1.0 + lax.erf(x * _SQRT_HALF))


def _gelu_grad(x):
    return 0.5 * (1.0 + lax.erf(x * _SQRT_HALF)) + x * (_INV_SQRT_2PI * jnp.exp(-0.5 * x * x))


def _mm_tn(a, b, tm, tn, name):
    k, m = a.shape
    n = b.shape[1]

    def body(a_ref, b_ref, o_ref):
        o_ref[...] = _dg(a_ref[...], b_ref[...], TN)

    return pl.pallas_call(
        body, grid=(m // tm, n // tn),
        in_specs=[pl.BlockSpec((k, tm), lambda i, j: (0, i)), pl.BlockSpec((k, tn), lambda i, j: (0, j))],
        out_specs=pl.BlockSpec((tm, tn), lambda i, j: (i, j)),
        out_shape=jax.ShapeDtypeStruct((m, n), F32), name=name,
        compiler_params=_params(("parallel", "parallel")),
    )(a, b)


def _mm_nn_cols(a, b4, tm, name):
    m, k = a.shape
    s, _, n = b4.shape

    def body(a_ref, b_ref, o_ref):
        o_ref[...] = _dg(a_ref[...], b_ref[...], NN)

    return pl.pallas_call(
        body, grid=(m // tm, s),
        in_specs=[pl.BlockSpec((tm, k), lambda i, j: (i, 0)), pl.BlockSpec((None, k, n), lambda i, j: (j, 0, 0))],
        out_specs=pl.BlockSpec((tm, n), lambda i, j: (i, j)),
        out_shape=jax.ShapeDtypeStruct((m, s * n), F32), name=name,
        compiler_params=_params(("parallel", "parallel")),
    )(a, b4)


def _mm_tn_cols(a, b2, s, tm, name):
    k, m = a.shape
    h, _, wide = b2.shape
    per = s // h
    n = wide // per

    def body(a_ref, b_ref, o_ref):
        o_ref[...] = _dg(a_ref[...], b_ref[...], TN)

    return pl.pallas_call(
        body, grid=(s, m // tm),
        in_specs=[pl.BlockSpec((k, tm), lambda j, i: (0, i)),
                  pl.BlockSpec((None, k, n), lambda j, i: (j // per, 0, j % per))],
        out_specs=pl.BlockSpec((None, tm, n), lambda j, i: (j, i, 0)),
        out_shape=jax.ShapeDtypeStruct((s, m, n), F32), name=name,
        compiler_params=_params(("parallel", "parallel")),
    )(a, b2)


TM_EW = 1024


def _rms_bwd_vals(xv, gv, dy):
    r = lax.rsqrt(jnp.mean(xv * xv, axis=-1, keepdims=True) + EPS)
    xh = xv * r
    dxh = dy * gv
    dx = r * (dxh - xh * jnp.mean(dxh * xh, axis=-1, keepdims=True))
    return dx, dy * xh


TM_FUSED = 512
TM_LOSS = 256


def _rms_vals(xv, gv):
    return xv * lax.rsqrt(jnp.mean(xv * xv, axis=-1, keepdims=True) + EPS) * gv


def _rope_blocks(src, dst, c, lo, hi):
    nq = ATTN_WIDTH // LANES
    for blk in range(nq + 1):
        t = src[:, blk * LANES:(blk + 1) * LANES]
        rot = t * c + pltpu.roll(t, LANES - 8, 1) * lo + pltpu.roll(t, 8, 1) * hi
        dst[:, blk * LANES:(blk + 1) * LANES] = (rot * SCORE_SCALE if blk < nq else rot).astype(BF16)
    dst[:, (nq + 1) * LANES:] = src[:, (nq + 1) * LANES:].astype(BF16)


def _rms_mm_rope(x, g, wt, tabs, name):
    l, d = x.shape
    n = wt.shape[0]

    def body(x_ref, g_ref, w_ref, c_ref, lo_ref, hi_ref, h_ref, qkv_ref, u_ref):
        h = _rms_vals(x_ref[...], g_ref[...]).astype(BF16)
        h_ref[...] = h
        out = _dg(h, w_ref[...], NT)
        _rope_blocks(out[:, :QKV_WIDTH], qkv_ref, c_ref[...], lo_ref[...], hi_ref[...])
        u_ref[...] = out[:, QKV_WIDTH:]

    row = lambda width: pl.BlockSpec((TM_FUSED, width), lambda i: (i, 0))
    return pl.pallas_call(
        body, grid=(l // TM_FUSED,),
        in_specs=[row(d), pl.BlockSpec((1, d), lambda i: (0, 0)), pl.BlockSpec((n, d), lambda i: (0, 0)),
                  row(LANES), row(LANES), row(LANES)],
        out_specs=[row(d), row(QKV_WIDTH), row(n - QKV_WIDTH)],
        out_shape=[jax.ShapeDtypeStruct((l, d), BF16), jax.ShapeDtypeStruct((l, QKV_WIDTH), BF16),
                   jax.ShapeDtypeStruct((l, n - QKV_WIDTH), F32)],
        name=name, compiler_params=_params(("parallel",)),
    )(x, g, wt, *tabs)


def _mix_mm_res_rms(attn, ys, g_attn, g_ssm, b, res, g, name):
    l, w = attn.shape
    d = b.shape[1]

    def body(a_ref, y_ref, ga_ref, gs_ref, b_ref, r_ref, g_ref, m_ref, x_ref, h_ref):
        m_ref[:, :w] = _rms_vals(a_ref[...], ga_ref[...]).astype(BF16)
        m_ref[:, w:] = _rms_vals(y_ref[...], gs_ref[...]).astype(BF16)
        xv = r_ref[...] + _dg(m_ref[...], b_ref[...], NN)
        x_ref[...] = xv
        h_ref[...] = _rms_vals(xv, g_ref[...]).astype(BF16)

    row = lambda width: pl.BlockSpec((TM_FUSED, width), lambda i: (i, 0))
    vec = lambda width: pl.BlockSpec((1, width), lambda i: (0, 0))
    return pl.pallas_call(
        body, grid=(l // TM_FUSED,),
        in_specs=[row(w), row(w), vec(w), vec(w), pl.BlockSpec((2 * w, d), lambda i: (0, 0)), row(d), vec(d)],
        out_specs=[row(2 * w), row(d), row(d)],
        out_shape=[jax.ShapeDtypeStruct((l, 2 * w), BF16), jax.ShapeDtypeStruct((l, d), F32),
                   jax.ShapeDtypeStruct((l, d), BF16)],
        name=name, compiler_params=_params(("parallel",)),
    )(attn, ys, g_attn, g_ssm, b, res, g)


def _mm_res_loss(a, b, res, g, target):
    l, k = a.shape
    d = b.shape[1]

    def body(a_ref, b_ref, r_ref, g_ref, t_ref, loss_ref, dx_ref, dxb_ref, dg_ref):
        xv = r_ref[...] + _dg(a_ref[...], b_ref[...], NN)
        gv = g_ref[...]
        r = lax.rsqrt(jnp.mean(xv * xv, axis=-1, keepdims=True) + EPS)
        xh = xv * r
        e = xh * gv - t_ref[...]
        part = jnp.sum(jnp.sum(e * e, axis=1, keepdims=True), axis=0, keepdims=True) * (0.5 / d)
        dy = e * (1.0 / d)
        dxh = dy * gv
        dx = r * (dxh - xh * jnp.mean(dxh * xh, axis=-1, keepdims=True))
        dx_ref[...] = dx
        dxb_ref[...] = dx.astype(BF16)

        @pl.when(pl.program_id(0) == 0)
        def _():
            dg_ref[...] = jnp.zeros_like(dg_ref)
            loss_ref[...] = jnp.zeros_like(loss_ref)

        dg_ref[...] += jnp.sum(dy * xh, axis=0, keepdims=True)
        loss_ref[...] += part

    row = lambda width: pl.BlockSpec((TM_LOSS, width), lambda i: (i, 0))
    vec = pl.BlockSpec((1, d), lambda i: (0, 0))
    return pl.pallas_call(
        body, grid=(l // TM_LOSS,),
        in_specs=[row(k), pl.BlockSpec((k, d), lambda i: (0, 0)), row(d), vec, row(d)],
        out_specs=[pl.BlockSpec((1, 1), lambda i: (0, 0)), row(d), row(d), vec],
        out_shape=[jax.ShapeDtypeStruct((1, 1), F32), jax.ShapeDtypeStruct((l, d), F32),
                   jax.ShapeDtypeStruct((l, d), BF16), jax.ShapeDtypeStruct((1, d), F32)],
        name="mm_down_loss", compiler_params=_params(("arbitrary",)),
    )(a, b, res, g, target)


def _mm_rms_bwd(a, b, a_spec, b_spec, matmul, x, g, res, name):
    l, d = x.shape

    def body(a_ref, b_ref, x_ref, g_ref, res_ref, dx_ref, dxb_ref, dg_ref):
        dx, dgr = _rms_bwd_vals(x_ref[...], g_ref[...], matmul(a_ref, b_ref))
        dx = dx + res_ref[...]
        dx_ref[...] = dx
        dxb_ref[...] = dx.astype(BF16)

        @pl.when(pl.program_id(0) == 0)
        def _():
            dg_ref[...] = jnp.zeros_like(dg_ref)

        dg_ref[...] += jnp.sum(dgr, axis=0, keepdims=True)

    row = pl.BlockSpec((TM_FUSED, d), lambda i: (i, 0))
    vec = pl.BlockSpec((1, d), lambda i: (0, 0))
    return pl.pallas_call(
        body, grid=(l // TM_FUSED,), in_specs=[a_spec, b_spec, row, vec, row], out_specs=[row, row, vec],
        out_shape=[jax.ShapeDtypeStruct((l, d), F32), jax.ShapeDtypeStruct((l, d), BF16),
                   jax.ShapeDtypeStruct((1, d), F32)],
        name=name, compiler_params=_params(("arbitrary",)),
    )(a, b, x, g, res)


def _mm_nn_rms_bwd(a, b, x, g, res, name):
    return _mm_rms_bwd(a, b, pl.BlockSpec((TM_FUSED, a.shape[1]), lambda i: (i, 0)),
                       pl.BlockSpec(b.shape, lambda i: (0, 0)),
                       lambda a_ref, b_ref: _dg(a_ref[...], b_ref[...], NN), x, g, res, name)


def _mm_cols_rms_bwd(a2, b4, x, g, res, name):
    h, _, wide = a2.shape
    s, _, n = b4.shape
    per = s // h

    def matmul(a_ref, b_ref):
        acc = None
        for j in range(s):
            part = _dg(a_ref[j // per, :, (j % per) * n:(j % per + 1) * n], b_ref[j], NT)
            acc = part if acc is None else acc + part
        return acc

    return _mm_rms_bwd(a2, b4, pl.BlockSpec((h, TM_FUSED, wide), lambda i: (0, i, 0)),
                       pl.BlockSpec(b4.shape, lambda i: (0, 0, 0), pipeline_mode=pl.Buffered(1)),
                       matmul, x, g, res, name)


def _mm_mix_bwd(dx, b, attn, ys, g_attn, g_ssm, name):
    l, w = attn.shape
    d = dx.shape[1]

    def body(dx_ref, b_ref, a_ref, y_ref, ga_ref, gs_ref, da_ref, dy_ref, dga_ref, dgs_ref):
        @pl.when(pl.program_id(0) == 0)
        def _():
            dga_ref[...] = jnp.zeros_like(dga_ref)
            dgs_ref[...] = jnp.zeros_like(dgs_ref)

        dm = _dg(dx_ref[...], b_ref[...], NT)
        for src, gr, off, dst, dgr in ((a_ref, ga_ref, 0, da_ref, dga_ref), (y_ref, gs_ref, w, dy_ref, dgs_ref)):
            dxv, dg_rows = _rms_bwd_vals(src[...], gr[...], dm[:, off:off + w])
            dst[...] = dxv
            dgr[...] += jnp.sum(dg_rows, axis=0, keepdims=True)

    row = lambda width: pl.BlockSpec((TM_FUSED, width), lambda i: (i, 0))
    vec = pl.BlockSpec((1, w), lambda i: (0, 0))
    return pl.pallas_call(
        body, grid=(l // TM_FUSED,),
        in_specs=[row(d), pl.BlockSpec((2 * w, d), lambda i: (0, 0)), row(w), row(w), vec, vec],
        out_specs=[row(w), row(w), vec, vec],
        out_shape=[jax.ShapeDtypeStruct((l, w), F32), jax.ShapeDtypeStruct((l, w), F32),
                   jax.ShapeDtypeStruct((1, w), F32), jax.ShapeDtypeStruct((1, w), F32)],
        name=name, compiler_params=_params(("arbitrary",)),
    )(dx, b, attn, ys, g_attn, g_ssm)


def _rope_tables(l):
    half = ROPE_DIM // 2
    f32 = np.float32
    inv_freq = np.power(f32(ROPE_THETA), -np.arange(half, dtype=f32) / f32(half))
    ang = np.arange(l, dtype=f32)[:, None] * inv_freq[None, :]
    cos, sin = np.cos(ang), np.sin(ang)
    ones = np.ones((l, HEAD_DIM - ROPE_DIM), f32)
    zeros = np.zeros((l, HEAD_DIM - ROPE_DIM), f32)
    zh = np.zeros((l, half), f32)
    c = np.concatenate([cos, cos, ones], axis=1)
    s_lo = np.concatenate([-sin, zh, zeros], axis=1)
    s_hi = np.concatenate([zh, sin, zeros], axis=1)
    return tuple(jnp.asarray(np.tile(t, (1, LANES // HEAD_DIM)), F32) for t in (c, s_lo, s_hi))


def _rope_bwd(dq, dkv, du_ssm, dpre, d_skip, tabs):
    l = dq.shape[0]
    nq = ATTN_WIDTH // LANES

    def body(dq_ref, dkv_ref, du_ref, dpre_ref, ds_ref, c_ref, lo_ref, hi_ref, o_ref):
        c, lo, hi = c_ref[...], lo_ref[...], hi_ref[...]
        for blk in range(nq + 1):
            t = dq_ref[:, blk * LANES:(blk + 1) * LANES] if blk < nq else dkv_ref[:, :KV_WIDTH]
            g = t * c + pltpu.roll(t * lo, 8, 1) + pltpu.roll(t * hi, LANES - 8, 1)
            o_ref[:, blk * LANES:(blk + 1) * LANES] = g.astype(BF16)
        o_ref[:, (nq + 1) * LANES:QKV_WIDTH] = dkv_ref[:, KV_WIDTH:].astype(BF16)
        o_ref[:, QKV_WIDTH:] = (du_ref[...] + dpre_ref[...] * ds_ref[...]).astype(BF16)

    tab = pl.BlockSpec((TM_EW, LANES), lambda i: (i, 0))
    wide = pl.BlockSpec((TM_EW, SSM_WIDTH), lambda i: (i, 0))
    return pl.pallas_call(
        body, grid=(l // TM_EW,),
        in_specs=[wide, pl.BlockSpec((TM_EW, 2 * KV_WIDTH), lambda i: (i, 0)), wide, wide,
                  pl.BlockSpec((1, SSM_WIDTH), lambda i: (0, 0)), tab, tab, tab],
        out_specs=pl.BlockSpec((TM_EW, IN_WIDTH), lambda i: (i, 0)),
        out_shape=jax.ShapeDtypeStruct((l, IN_WIDTH), BF16), name="rope_bwd",
        compiler_params=_params(("parallel",)),
    )(dq, dkv, du_ssm, dpre, d_skip, *tabs)


_Q_COLS = ATTN_WIDTH // LANES
_NEG = -1e30


def _window_specs(nb, width, col):
    return [
        pl.BlockSpec((BLOCK, width), lambda n: (jnp.maximum(n - 1, 0), col)),
        pl.BlockSpec((BLOCK, width), lambda n: (n, col)),
        pl.BlockSpec((BLOCK, width), lambda n: (jnp.minimum(n + 1, nb - 1), col)),
    ]


def _stacked_sink(sink_ref, heads):
    rid = lax.broadcasted_iota(jnp.int32, (len(heads) * BLOCK, 1), 0)
    sk = jnp.full(rid.shape, sink_ref[0, heads[-1]], F32)
    for g in range(len(heads) - 2, -1, -1):
        sk = jnp.where(rid < (g + 1) * BLOCK, sink_ref[0, heads[g]], sk)
    return sk


def _attn_fwd(qkv, sink):
    l = qkv.shape[0]
    nb = l // BLOCK
    grp = N_Q_HEADS // N_KV_HEADS

    def body(sink_ref, q_ref, k0, k1, k2, v0, v1, v2, o_ref, lse_ref):
        n = pl.program_id(0)
        q = q_ref[...]
        kw = jnp.concatenate([k0[...], k1[...], k2[...]], axis=0)
        vw = jnp.concatenate([v0[...], v1[...], v2[...]], axis=0)
        row = lax.broadcasted_iota(jnp.int32, (grp * BLOCK, 3 * BLOCK), 0)
        col = lax.broadcasted_iota(jnp.int32, (grp * BLOCK, 3 * BLOCK), 1)
        valid = jnp.abs(col - BLOCK - (row & (BLOCK - 1))) <= WINDOW
        valid &= jnp.logical_not((n == 0) & (col < BLOCK))
        valid &= jnp.logical_not((n == nb - 1) & (col >= 2 * BLOCK))
        for hk in range(N_KV_HEADS):
            heads = range(hk * grp, (hk + 1) * grp)
            qs = jnp.concatenate([q[:, h * HEAD_DIM:(h + 1) * HEAD_DIM] for h in heads], axis=0)
            kh = kw[:, hk * HEAD_DIM:(hk + 1) * HEAD_DIM]
            vh = vw[:, hk * HEAD_DIM:(hk + 1) * HEAD_DIM]
            s = jnp.where(valid, _dg(qs, kh, NT), _NEG)
            sk = _stacked_sink(sink_ref, heads)
            m = jnp.maximum(jnp.max(s, axis=1, keepdims=True), sk)
            p = jnp.exp(s - m)
            denom = jnp.sum(p, axis=1, keepdims=True) + jnp.exp(sk - m)
            o = _dg((p / denom).astype(BF16), vh, NN)
            lse = m + jnp.log(denom)
            for g, h in enumerate(heads):
                o_ref[:, h * HEAD_DIM:(h + 1) * HEAD_DIM] = o[g * BLOCK:(g + 1) * BLOCK]
                lse_ref[:, h:h + 1] = lse[g * BLOCK:(g + 1) * BLOCK]

    return pl.pallas_call(
        body, grid=(nb,),
        in_specs=[pl.BlockSpec(memory_space=pltpu.SMEM),
                  pl.BlockSpec((BLOCK, ATTN_WIDTH), lambda n: (n, 0))]
        + _window_specs(nb, KV_WIDTH, _Q_COLS) + _window_specs(nb, KV_WIDTH, _Q_COLS + 1),
        out_specs=[pl.BlockSpec((BLOCK, ATTN_WIDTH), lambda n: (n, 0)),
                   pl.BlockSpec((BLOCK, N_Q_HEADS), lambda n: (n, 0))],
        out_shape=[jax.ShapeDtypeStruct((l, ATTN_WIDTH), F32), jax.ShapeDtypeStruct((l, N_Q_HEADS), F32)],
        name="attn_fwd", compiler_params=_params(("parallel",)),
    )(sink, qkv, qkv, qkv, qkv, qkv, qkv, qkv)


def _attn_bwd(qkv, attn, dattn, lse, sink):
    l = qkv.shape[0]
    nb = l // BLOCK
    grp = N_Q_HEADS // N_KV_HEADS
    win = 3 * BLOCK

    def body(sink_ref, q_ref, k0, k1, k2, v0, v1, v2, o_ref, d_ref, l_ref, dq_ref, dkv_ref, dsink_ref, ring_ref):
        n = pl.program_id(0)

        @pl.when(n == 0)
        def _():
            dsink_ref[...] = jnp.zeros_like(dsink_ref)
            ring_ref[...] = jnp.zeros_like(ring_ref)

        @pl.when(n < nb)
        def _():
            first, last = n == 0, n == nb - 1
            cat = lambda a, b, c: jnp.concatenate([a[...], b[...], c[...]], axis=0)
            q, kw, vw = q_ref[...], cat(k0, k1, k2), cat(v0, v1, v2)
            dov = d_ref[...]
            prod = o_ref[...] * dov
            dob = dov.astype(BF16)
            lse = l_ref[...]
            row = lax.broadcasted_iota(jnp.int32, (grp * BLOCK, win), 0)
            col = lax.broadcasted_iota(jnp.int32, (grp * BLOCK, win), 1)
            valid = jnp.abs(col - BLOCK - (row & (BLOCK - 1))) <= WINDOW
            valid &= jnp.logical_not(first & (col < BLOCK))
            valid &= jnp.logical_not(last & (col >= 2 * BLOCK))

            dsink_parts, dks, dvs = [], [], []
            for hk in range(N_KV_HEADS):
                heads = range(hk * grp, (hk + 1) * grp)
                ksl = slice(hk * HEAD_DIM, (hk + 1) * HEAD_DIM)
                hsl = [slice(h * HEAD_DIM, (h + 1) * HEAD_DIM) for h in heads]
                stack = lambda parts: jnp.concatenate(parts, axis=0)
                qs = stack([q[:, s_] for s_ in hsl])
                dos = stack([dob[:, s_] for s_ in hsl])
                deltas = stack([jnp.sum(prod[:, s_], axis=1, keepdims=True) for s_ in hsl])
                lses = stack([lse[:, h:h + 1] for h in heads])
                kh, vh = kw[:, ksl], vw[:, ksl]
                s = jnp.where(valid, _dg(qs, kh, NT), _NEG)
                p = jnp.exp(s - lses)
                dp = _dg(dos, vh, NT)
                ds = (p * (dp - deltas)).astype(BF16)
                dq = _dg(ds, kh, NN) * SCORE_SCALE
                sink_rows = jnp.exp(_stacked_sink(sink_ref, heads) - lses) * deltas
                for g in range(grp):
                    dq_ref[:, hsl[g]] = dq[g * BLOCK:(g + 1) * BLOCK]
                    dsink_parts.append(jnp.sum(sink_rows[g * BLOCK:(g + 1) * BLOCK], axis=0, keepdims=True))
                dks.append(_dg(ds, qs, TN))
                dvs.append(_dg(p.astype(BF16), dos, TN))
            dsink_ref[...] -= jnp.concatenate(dsink_parts, axis=1)
            part = jnp.concatenate(dks + dvs, axis=1)
            ring_ref[(n + 2) % 3] += part[0:BLOCK]
            ring_ref[n % 3] += part[BLOCK:2 * BLOCK]
            ring_ref[(n + 1) % 3] = part[2 * BLOCK:]

        @pl.when(n >= 1)
        def _():
            dkv_ref[...] = ring_ref[(n + 2) % 3]

    centre = lambda n: jnp.minimum(n, nb - 1)
    window = lambda width, col: [
        pl.BlockSpec((BLOCK, width), lambda n: (jnp.maximum(centre(n) - 1, 0), col)),
        pl.BlockSpec((BLOCK, width), lambda n: (centre(n), col)),
        pl.BlockSpec((BLOCK, width), lambda n: (jnp.minimum(centre(n) + 1, nb - 1), col))]
    own = lambda width: pl.BlockSpec((BLOCK, width), lambda n: (centre(n), 0))
    return pl.pallas_call(
        body, grid=(nb + 1,),
        in_specs=[pl.BlockSpec(memory_space=pltpu.SMEM), own(ATTN_WIDTH)]
        + window(KV_WIDTH, _Q_COLS) + window(KV_WIDTH, _Q_COLS + 1)
        + [own(ATTN_WIDTH), own(ATTN_WIDTH), own(N_Q_HEADS)],
        out_specs=[own(ATTN_WIDTH), pl.BlockSpec((BLOCK, 2 * KV_WIDTH), lambda n: (jnp.maximum(n - 1, 0), 0)),
                   pl.BlockSpec((1, N_Q_HEADS), lambda n: (0, 0))],
        out_shape=[jax.ShapeDtypeStruct((l, ATTN_WIDTH), F32), jax.ShapeDtypeStruct((l, 2 * KV_WIDTH), F32),
                   jax.ShapeDtypeStruct((1, N_Q_HEADS), F32)],
        scratch_shapes=[pltpu.VMEM((3, BLOCK, 2 * KV_WIDTH), F32)],
        name="attn_bwd", compiler_params=_params(("arbitrary",)),
    )(sink, qkv, qkv, qkv, qkv, qkv, qkv, qkv, attn, dattn, lse)


def _ssm_disc(a_re, a_im, log_step, b_re, b_im):
    step = jnp.exp(log_step)[..., None]
    mag = jnp.exp(a_re * step)
    lb_re, lb_im = mag * jnp.cos(a_im * step), mag * jnp.sin(a_im * step)
    nr, ni = lb_re - 1.0, lb_im
    den = a_re * a_re + a_im * a_im
    f_re = ((nr * a_re + ni * a_im) / den)[..., None]
    f_im = ((ni * a_re - nr * a_im) / den)[..., None]
    return lb_re, lb_im, f_re * b_re - f_im * b_im, f_re * b_im + f_im * b_re


def _ssm_pack(lb_re, lb_im, bb_re, bb_im, c_re, c_im):
    eye = jnp.eye(SSM_CH // SSM_GROUP, dtype=F32)
    ng = SSM_CH // SSM_GROUP

    def diag_b(bb):
        t = bb.reshape(2, SSM_CB, ng, SSM_STATE, SSM_GROUP)
        return jnp.einsum('dkgpc,gh->dkgchp', t, eye).reshape(2, SSM_CB, SSM_CH, SSM_ST)

    def diag_c(cc):
        t = cc.reshape(2, SSM_CB, ng, SSM_GROUP, SSM_STATE)
        return jnp.einsum('dkgcp,gh->dkhpgc', t, eye).reshape(2, SSM_CB, SSM_ST, SSM_CH)

    bcat = jnp.concatenate([diag_b(bb_re), diag_b(bb_im)], axis=-1)
    ccat = jnp.concatenate([diag_c(c_re), -diag_c(c_im)], axis=-2)
    lam_re = lb_re.reshape(2, SSM_CB, 1, SSM_ST)
    lam_im = lb_im.reshape(2, SSM_CB, 1, SSM_ST)
    return bcat, ccat, lam_re, lam_im


def _ssm_unpack(dbcat, dccat, dlam_re, dlam_im):
    ng = SSM_CH // SSM_GROUP
    eye = jnp.eye(ng, dtype=F32)

    def undiag_b(t):
        t = t.reshape(2, SSM_CB, ng, SSM_GROUP, ng, SSM_STATE)
        return jnp.einsum('dkgchp,gh->dkgpc', t, eye).reshape(2, N_SSM_GROUPS, SSM_STATE, SSM_GROUP)

    def undiag_c(t):
        t = t.reshape(2, SSM_CB, ng, SSM_STATE, ng, SSM_GROUP)
        return jnp.einsum('dkhpgc,gh->dkgcp', t, eye).reshape(2, N_SSM_GROUPS, SSM_GROUP, SSM_STATE)

    dbb_re, dbb_im = undiag_b(dbcat[..., :SSM_ST]), undiag_b(dbcat[..., SSM_ST:])
    dc_re, dc_im = undiag_c(dccat[:, :, :SSM_ST]), -undiag_c(dccat[:, :, SSM_ST:])
    shape = (2, N_SSM_GROUPS, SSM_STATE)
    return dlam_re.reshape(shape), dlam_im.reshape(shape), dbb_re, dbb_im, dc_re, dc_im


def _to_segments(t):
    l, w = t.shape
    return t.reshape(N_SEG, l // N_SEG, w).transpose(1, 0, 2).reshape(l, w)


def _from_segments(t):
    l, w = t.shape
    return t.reshape(l // N_SEG, N_SEG, w).transpose(1, 0, 2).reshape(l, w)


SSM_RC = 256
SSM_JC = SSM_RC // N_SEG
_RE, _IM = pl.ds(0, SSM_ST), pl.ds(SSM_ST, SSM_ST)


def _cfma(ar, ai, xr, xi, br, bi):
    return ar * xr - ai * xi + br, ar * xi + ai * xr + bi


def _chunk_rows(ci, rev, nc):
    start = jnp.where(rev, (nc - 1 - ci) * SSM_RC, ci * SSM_RC)
    return pl.ds(pl.multiple_of(start, SSM_RC), SSM_RC)


def _scan_chunk(src, dst, ar, ai, rev, nj, ci, carry, prev_ref=None):
    def rows_of(staged, j, k):
        at = jnp.where(rev, SSM_JC - 1 - k, k) if staged else j
        return pl.ds(pl.multiple_of(at * N_SEG, N_SEG), N_SEG)

    for k in range(SSM_JC):
        jj = ci * SSM_JC + k
        j = jnp.where(rev, nj - 1 - jj, jj)
        rows = rows_of(src[1], j, k)
        nr, ni = _cfma(ar, ai, carry[0], carry[1], src[0][rows, _RE], src[0][rows, _IM])
        if dst is not None:
            rows = rows_of(dst[1], j, k)
            dst[0][rows, _RE] = nr
            dst[0][rows, _IM] = ni
        if prev_ref is None:
            carry = (nr, ni)
            continue
        jp = jnp.where(rev, j - 1, j + 1)
        if k == SSM_JC - 1:
            inside = jnp.where((jp >= 0) & (jp < nj), 1.0, 0.0)
            jp = jnp.clip(jp, 0, nj - 1)
        prow = pl.ds(pl.multiple_of(jp * N_SEG, N_SEG), N_SEG)
        xr, xi = prev_ref[prow, _RE], prev_ref[prow, _IM]
        sr, si = nr * xr + ni * xi, ni * xr - nr * xi
        if k == SSM_JC - 1:
            sr, si = inside * sr, inside * si
        carry = (nr, ni, carry[2] + sr, carry[3] + si)
    return carry


def _segment_inits(ar, ai, end_r, end_i, rev, nj):
    pr, pi = ar, ai
    for _ in range(int(math.log2(nj))):
        pr, pi = pr * pr - pi * pi, 2.0 * pr * pi
    seg = lax.broadcasted_iota(jnp.int32, end_r.shape, 0)
    zero = jnp.zeros_like(end_r)

    def chain(shift, keep):
        ir, ii = zero, zero
        for _ in range(N_SEG - 1):
            tr, ti = _cfma(pr, pi, ir, ii, end_r, end_i)
            ir = jnp.where(keep, pltpu.roll(tr, shift, 0), 0.0)
            ii = jnp.where(keep, pltpu.roll(ti, shift, 0), 0.0)
        return ir, ii

    up_r, up_i = chain(1, seg >= 1)
    dn_r, dn_i = chain(N_SEG - 1, seg <= N_SEG - 2)
    return jnp.where(rev, dn_r, up_r), jnp.where(rev, dn_i, up_i)


def _ssm_specs(l):
    act = pl.BlockSpec((l, SSM_CH), lambda k, d: (0, k))
    bmat = pl.BlockSpec((None, None, SSM_CH, 2 * SSM_ST), lambda k, d: (d, k, 0, 0))
    cmat = pl.BlockSpec((None, None, 2 * SSM_ST, SSM_CH), lambda k, d: (d, k, 0, 0))
    lam = pl.BlockSpec((None, None, 1, SSM_ST), lambda k, d: (d, k, 0, 0))
    return act, bmat, cmat, lam


def _ssm_fwd(u_seg, bcat, ccat, lam_re, lam_im):
    l = u_seg.shape[0]
    nj = l // N_SEG
    nc = l // SSM_RC

    def body(u_ref, b_ref, c_ref, lr_ref, li_ref, y_ref, ub_ref, keep_ref, xs_ref, stage0, stage1, keep_sem):
        k, d = pl.program_id(0), pl.program_id(1)
        rev = d == 1
        shape = (N_SEG, SSM_ST)
        ar, ai = jnp.broadcast_to(lr_ref[...], shape), jnp.broadcast_to(li_ref[...], shape)
        zero = jnp.zeros(shape, F32)

        def inputs(ci, stage):
            rows = _chunk_rows(ci, rev, nc)
            ub = u_ref[rows, :].astype(BF16)
            ub_ref[rows, :] = ub
            bu = _dg(ub, b_ref[...], NN)
            stage[...] = bu
            xs_ref[rows, :] = bu

        def first(stage, ci, carry):
            return _scan_chunk((stage, True), None, ar, ai, rev, nj, ci, carry)

        def first_pass(t, carry):
            inputs(2 * t + 1, stage1)
            carry = first(stage0, 2 * t, carry)
            inputs(2 * t + 2, stage0)
            return first(stage1, 2 * t + 1, carry)

        inputs(0, stage0)
        carry = lax.fori_loop(0, nc // 2 - 1, first_pass, (zero, zero))
        inputs(nc - 1, stage1)
        carry = first(stage0, nc - 2, carry)
        end_r, end_i = first(stage1, nc - 1, carry)
        init = _segment_inits(ar, ai, end_r, end_i, rev, nj)

        @pl.when(d == 0)
        def _():
            y_ref[...] = jnp.zeros_like(y_ref)

        def outputs(ci):
            rows = _chunk_rows(ci, rev, nc)
            y_ref[rows, :] += _dg(xs_ref[rows, :].astype(BF16), c_ref[...], NN)
            pltpu.make_async_copy(xs_ref.at[rows], keep_ref.at[d, k, rows], keep_sem).start()

        def second(ci, carry):
            return _scan_chunk((xs_ref, False), (xs_ref, False), ar, ai, rev, nj, ci, carry)

        def second_pass(ci, carry):
            outputs(ci - 1)
            return second(ci, carry)

        lax.fori_loop(1, nc, second_pass, second(0, init))
        outputs(nc - 1)
        pltpu.make_async_copy(xs_ref, keep_ref.at[d, k], keep_sem).wait()

    act, bmat, cmat, lam = _ssm_specs(l)
    return pl.pallas_call(
        body, grid=(SSM_CB, 2), in_specs=[act, bmat, cmat, lam, lam], out_specs=[act, act, ANY],
        out_shape=[jax.ShapeDtypeStruct((l, SSM_WIDTH), F32), jax.ShapeDtypeStruct((l, SSM_WIDTH), BF16),
                   jax.ShapeDtypeStruct((2, SSM_CB, l, 2 * SSM_ST), F32)],
        scratch_shapes=[pltpu.VMEM((l, 2 * SSM_ST), F32), pltpu.VMEM((SSM_RC, 2 * SSM_ST), F32),
                        pltpu.VMEM((SSM_RC, 2 * SSM_ST), F32), pltpu.SemaphoreType.DMA],
        name="ssm_fwd", compiler_params=_params(("parallel", "arbitrary"), vmem_mb=56),
    )(u_seg, bcat.astype(BF16), ccat.astype(BF16), lam_re, lam_im)


def _ssm_bwd(u_seg, dy_seg, states, bcat, ccat, lam_re, lam_im):
    l = u_seg.shape[0]
    nj = l // N_SEG
    nc = l // SSM_RC

    def body(u_ref, dy_ref, keep_ref, b_ref, c_ref, lr_ref, li_ref,
             du_ref, db_ref, dc_ref, dlr_ref, dli_ref, xs_ref, gs_ref, dyb_ref, stage0, stage1, keep_sem):
        k, d = pl.program_id(0), pl.program_id(1)
        rev = d == 1
        back = jnp.logical_not(rev)
        shape = (N_SEG, SSM_ST)
        ar, ai = jnp.broadcast_to(lr_ref[...], shape), -jnp.broadcast_to(li_ref[...], shape)
        zero = jnp.zeros(shape, F32)
        fetch = pltpu.make_async_copy(keep_ref.at[d, k], xs_ref, keep_sem)
        fetch.start()

        def inputs(ci, stage):
            rows = _chunk_rows(ci, back, nc)
            dyb = dy_ref[rows, :].astype(BF16)
            dyb_ref[rows, :] = dyb
            dx = _dg(dyb, c_ref[...], NT)
            stage[...] = dx
            gs_ref[rows, :] = dx

        def first(stage, ci, carry):
            return _scan_chunk((stage, True), None, ar, ai, back, nj, ci, carry)

        def first_pass(t, carry):
            inputs(2 * t + 1, stage1)
            carry = first(stage0, 2 * t, carry)
            inputs(2 * t + 2, stage0)
            return first(stage1, 2 * t + 1, carry)

        inputs(0, stage0)
        carry = lax.fori_loop(0, nc // 2 - 1, first_pass, (zero, zero))
        inputs(nc - 1, stage1)
        carry = first(stage0, nc - 2, carry)
        end_r, end_i = first(stage1, nc - 1, carry)
        init = _segment_inits(ar, ai, end_r, end_i, back, nj)
        fetch.wait()
        db_ref[...] = jnp.zeros_like(db_ref)
        dc_ref[...] = jnp.zeros_like(dc_ref)

        @pl.when(d == 0)
        def _():
            du_ref[...] = jnp.zeros_like(du_ref)

        def outputs(ci, stage):
            rows = _chunk_rows(ci, back, nc)
            g = stage[...].astype(BF16)
            dc_ref[...] += _dg(xs_ref[rows, :].astype(BF16), dyb_ref[rows, :], TN)
            db_ref[...] += _dg(u_ref[rows, :], g, TN)
            du_ref[rows, :] += _dg(g, b_ref[...], NT)

        def second(ci, stage, carry):
            return _scan_chunk((gs_ref, False), (stage, True), ar, ai, back, nj, ci, carry, prev_ref=xs_ref)

        def second_pass(t, carry):
            outputs(2 * t, stage0)
            carry = second(2 * t + 1, stage1, carry)
            outputs(2 * t + 1, stage1)
            return second(2 * t + 2, stage0, carry)

        carry = lax.fori_loop(0, nc // 2 - 1, second_pass, second(0, stage0, init + (zero, zero)))
        outputs(nc - 2, stage0)
        gr, gi, acc_r, acc_i = second(nc - 1, stage1, carry)
        outputs(nc - 1, stage1)

        seg = lax.broadcasted_iota(jnp.int32, shape, 0)
        jb = jnp.where(rev, nj - 1, 0)
        erow = pl.ds(pl.multiple_of((nj - 1 - jb) * N_SEG, N_SEG), N_SEG)

        def before(t):
            up = jnp.where(seg >= 1, pltpu.roll(t, 1, 0), 0.0)
            down = jnp.where(seg <= N_SEG - 2, pltpu.roll(t, N_SEG - 1, 0), 0.0)
            return jnp.where(rev, down, up)

        init_r, init_i = before(xs_ref[erow, _RE]), before(xs_ref[erow, _IM])
        acc_r = acc_r + gr * init_r + gi * init_i
        acc_i = acc_i + gi * init_r - gr * init_i
        dlr_ref[...] = jnp.sum(acc_r, axis=0, keepdims=True)
        dli_ref[...] = jnp.sum(acc_i, axis=0, keepdims=True)

    act, bmat, cmat, lam = _ssm_specs(l)
    return pl.pallas_call(
        body, grid=(SSM_CB, 2), in_specs=[act, act, ANY, bmat, cmat, lam, lam],
        out_specs=[act, bmat, cmat, lam, lam],
        out_shape=[jax.ShapeDtypeStruct((l, SSM_WIDTH), F32),
                   jax.ShapeDtypeStruct(bcat.shape, F32), jax.ShapeDtypeStruct(ccat.shape, F32),
                   jax.ShapeDtypeStruct(lam_re.shape, F32), jax.ShapeDtypeStruct(lam_im.shape, F32)],
        scratch_shapes=[pltpu.VMEM((l, 2 * SSM_ST), F32), pltpu.VMEM((l, 2 * SSM_ST), F32),
                        pltpu.VMEM((l, SSM_CH), BF16),
                        pltpu.VMEM((SSM_RC, 2 * SSM_ST), F32), pltpu.VMEM((SSM_RC, 2 * SSM_ST), F32),
                        pltpu.SemaphoreType.DMA],
        name="ssm_bwd", compiler_params=_params(("parallel", "arbitrary"), vmem_mb=58),
    )(u_seg, dy_seg, states, bcat.astype(BF16), ccat.astype(BF16), lam_re, lam_im)


def _glu_fwd(y_ssm, u, d_skip, w_glu):
    l, w = u.shape

    def body(y_ref, u_ref, d_ref, w_ref, pre_ref, s_ref, ys_ref):
        pre = y_ref[...] + d_ref[...] * u_ref[...]
        z = _gelu(pre)
        s = _dg(z.astype(BF16), w_ref[...], NN)
        pre_ref[...] = pre
        s_ref[...] = s
        ys_ref[...] = z * _sigmoid(s)

    row = pl.BlockSpec((TM_EW, w), lambda i: (i, 0))
    out = jax.ShapeDtypeStruct((l, w), F32)
    return pl.pallas_call(
        body, grid=(l // TM_EW,),
        in_specs=[row, row, pl.BlockSpec((1, w), lambda i: (0, 0)), pl.BlockSpec((w, w), lambda i: (0, 0))],
        out_specs=[row, row, row], out_shape=[out, out, out], name="glu_fwd",
        compiler_params=_params(("parallel",)),
    )(y_ssm, u, d_skip, w_glu)


def _glu_bwd(pre, s, dys, u, d_skip, w_glu):
    l, w = u.shape

    def body(pre_ref, s_ref, dys_ref, u_ref, d_ref, w_ref, dpre_ref, z_ref, ds_ref, dd_ref):
        pre, dys = pre_ref[...], dys_ref[...]
        z = _gelu(pre)
        sig = _sigmoid(s_ref[...])
        ds = (dys * z * sig * (1.0 - sig)).astype(BF16)
        dz = dys * sig + _dg(ds, w_ref[...], NT)
        dpre = dz * _gelu_grad(pre)
        dpre_ref[...] = dpre
        z_ref[...] = z.astype(BF16)
        ds_ref[...] = ds

        @pl.when(pl.program_id(0) == 0)
        def _():
            dd_ref[...] = jnp.zeros_like(dd_ref)

        dd_ref[...] += jnp.sum(dpre * u_ref[...], axis=0, keepdims=True)

    row = pl.BlockSpec((TM_EW, w), lambda i: (i, 0))
    vec = pl.BlockSpec((1, w), lambda i: (0, 0))
    return pl.pallas_call(
        body, grid=(l // TM_EW,),
        in_specs=[row, row, row, row, vec, pl.BlockSpec((w, w), lambda i: (0, 0))],
        out_specs=[row, row, row, vec],
        out_shape=[jax.ShapeDtypeStruct((l, w), F32), jax.ShapeDtypeStruct((l, w), BF16),
                   jax.ShapeDtypeStruct((l, w), BF16), jax.ShapeDtypeStruct((1, w), F32)],
        name="glu_bwd", compiler_params=_params(("arbitrary",)),
    )(pre, s, dys, u, d_skip, w_glu)


TM_CV = 512
TC_CV = 256
TM_CF = 256
TC_CF = D_FF // 2
HALO = SUBLANES


def _conv_specs(l, col0, tm=TM_CV, tc=TC_CV):
    per = tm // HALO
    nh = l // HALO
    off = col0 // tc
    return [
        pl.BlockSpec((HALO, tc), lambda j, i: (jnp.maximum(i * per - 1, 0), j + off)),
        pl.BlockSpec((tm, tc), lambda j, i: (i, j + off)),
        pl.BlockSpec((HALO, tc), lambda j, i: (jnp.minimum((i + 1) * per, nh - 1), j + off)),
    ]


def _ext(prev_ref, mid_ref, next_ref, first, last):
    p = jnp.where(first, 0.0, prev_ref[...])
    n = jnp.where(last, 0.0, next_ref[...])
    return jnp.concatenate([p, mid_ref[...], n], axis=0)


def _shift_dn(t):
    return pltpu.roll(t, 1, 0)


def _shift_up(t):
    return pltpu.roll(t, t.shape[0] - 1, 0)


def _conv3(e, w_ref, b_ref):
    return w_ref[0:1, :] * _shift_dn(e) + w_ref[1:2, :] * e + w_ref[2:3, :] * _shift_up(e) + b_ref[...]


def _convffn_fwd(up_pre, conv_w, conv_b):
    l = up_pre.shape[0]
    tm, tc = TM_CF, TC_CF
    ni = l // tm
    wspec = lambda off: pl.BlockSpec((3, tc), lambda j, i: (0, j + off))
    bspec = lambda off: pl.BlockSpec((1, tc), lambda j, i: (0, j + off))
    voff = D_FF // tc

    def body(gp, gm, gn, vp, vm, vn, wg, bg, wv, bv, o_ref):
        i = pl.program_id(1)
        first, last = i == 0, i == ni - 1
        gate = _conv3(_ext(gp, gm, gn, first, last), wg, bg)[HALO:HALO + tm]
        val = _conv3(_ext(vp, vm, vn, first, last), wv, bv)[HALO:HALO + tm]
        o_ref[...] = (gate * _sigmoid(gate) * val).astype(BF16)

    return pl.pallas_call(
        body, grid=(D_FF // tc, ni),
        in_specs=_conv_specs(l, 0, tm, tc) + _conv_specs(l, D_FF, tm, tc)
        + [wspec(0), bspec(0), wspec(voff), bspec(voff)],
        out_specs=pl.BlockSpec((tm, tc), lambda j, i: (i, j)),
        out_shape=jax.ShapeDtypeStruct((l, D_FF), BF16), name="convffn_fwd",
        compiler_params=_params(("parallel", "parallel")),
    )(up_pre, up_pre, up_pre, up_pre, up_pre, up_pre, conv_w, conv_b, conv_w, conv_b)


HALO_B = 2 * SUBLANES


def _convffn_bwd(up_pre, dx2b, w_down, conv_w, conv_b):
    l = up_pre.shape[0]
    ni = l // TM_CV
    d = dx2b.shape[1]
    wspec = lambda off: pl.BlockSpec((3, TC_CV), lambda i, j: (0, j + off))
    bspec = lambda off: pl.BlockSpec((1, TC_CV), lambda i, j: (0, j + off))
    voff = D_FF // TC_CV
    swap = lambda spec: pl.BlockSpec(spec.block_shape, lambda i, j, f=spec.index_map: f(j, i))
    per, nh = TM_CV // HALO_B, l // HALO_B
    dx_specs = [pl.BlockSpec((HALO_B, d), lambda i, j: (jnp.maximum(i * per - 1, 0), 0)),
                pl.BlockSpec((TM_CV, d), lambda i, j: (i, 0)),
                pl.BlockSpec((HALO_B, d), lambda i, j: (jnp.minimum((i + 1) * per, nh - 1), 0))]

    def body(gp, gm, gn, vp, vm, vn, xp, xm, xn, wd, wg, bg, wv, bv, dup_ref, pg_ref, pv_ref):
        i = pl.program_id(0)
        first, last = i == 0, i == ni - 1
        ge, ve = _ext(gp, gm, gn, first, last), _ext(vp, vm, vn, first, last)
        zero = jnp.zeros((HALO_B, d), BF16)
        dx = jnp.concatenate([jnp.where(first, zero, xp[...]), xm[...], jnp.where(last, zero, xn[...])], axis=0)
        de = _dg(dx, wd[...], NT)[HALO_B - HALO:HALO_B + TM_CV + HALO]
        taps = [(_shift_dn(e), e, _shift_up(e)) for e in (ge, ve)]
        conv = lambda t, w_ref, b_ref: w_ref[0:1, :] * t[0] + w_ref[1:2, :] * t[1] + w_ref[2:3, :] * t[2] + b_ref[...]
        gate, val = conv(taps[0], wg, bg), conv(taps[1], wv, bv)
        sig = _sigmoid(gate)
        silu = gate * sig
        dgate = de * val * (sig + silu * (1.0 - sig))
        dval = de * silu
        mid = slice(HALO, HALO + TM_CV)
        rid = lax.broadcasted_iota(jnp.int32, (SUBLANES, TC_CV), 0)
        for half, (dup, tap, w_ref, p_ref) in enumerate(((dgate, taps[0], wg, pg_ref), (dval, taps[1], wv, pv_ref))):
            dpre = w_ref[0:1, :] * _shift_up(dup) + w_ref[1:2, :] * dup + w_ref[2:3, :] * _shift_dn(dup)
            dup_ref[half] = dpre[mid].astype(BF16)
            dm_ = dup[mid]
            sums = [jnp.sum(dm_ * t[mid], axis=0, keepdims=True) for t in tap]
            sums.append(jnp.sum(dm_, axis=0, keepdims=True))
            acc = jnp.zeros((SUBLANES, TC_CV), F32)
            for k, sk in enumerate(sums):
                acc = jnp.where(rid == k, sk, acc)
            p_ref[...] = acc

    par = pl.BlockSpec((None, SUBLANES, TC_CV), lambda i, j: (i, 0, j))
    dup, pg, pv = pl.pallas_call(
        body, grid=(ni, D_FF // TC_CV),
        in_specs=[swap(s) for s in _conv_specs(l, 0) + _conv_specs(l, D_FF)] + dx_specs
        + [pl.BlockSpec((TC_CV, d), lambda i, j: (j, 0)), wspec(0), bspec(0), wspec(voff), bspec(voff)],
        out_specs=[pl.BlockSpec((2, TM_CV, TC_CV), lambda i, j: (0, i, j)), par, par],
        out_shape=[jax.ShapeDtypeStruct((2, l, D_FF), BF16),
                   jax.ShapeDtypeStruct((ni, SUBLANES, D_FF), F32), jax.ShapeDtypeStruct((ni, SUBLANES, D_FF), F32)],
        name="convffn_bwd", compiler_params=_params(("parallel", "parallel")),
    )(up_pre, up_pre, up_pre, up_pre, up_pre, up_pre, dx2b, dx2b, dx2b, w_down, conv_w, conv_b, conv_w, conv_b)
    return dup, jnp.concatenate([jnp.sum(pg, axis=0), jnp.sum(pv, axis=0)], axis=1)


def _local_step(x, target, wb, sp, mixer_weights=None, late_weights=None, grads_ready=None,
                grads_next=None):
    l = x.shape[0]
    tabs = _rope_tables(l)
    disc = _ssm_disc(sp["a_re"], sp["a_im"], sp["log_step"], sp["b_re"], sp["b_im"])
    bcat, ccat, lam_re, lam_im = _ssm_pack(*disc, sp["c_re"], sp["c_im"])
    d_skip = sp["d_skip"].reshape(1, SSM_WIDTH)

    h, qkv, u = _rms_mm_rope(x, sp["norm_mix_g"], wb["w_in"], tabs, "mm_in")
    attn, lse = _attn_fwd(qkv, sp["sink"])
    y_seg, u_seg, states = _ssm_fwd(_to_segments(u), bcat, ccat, lam_re, lam_im)
    y_ssm = _from_segments(y_seg)
    if mixer_weights is not None:
        wb = dict(wb, **mixer_weights(attn))
    pre, s_glu, ys = _glu_fwd(y_ssm, u, d_skip, wb["w_glu"])
    mixed, x1, h2 = _mix_mm_res_rms(attn, ys, sp["norm_attn_g"], sp["norm_ssm_g"], wb["w_out"], x,
                                    sp["norm_ffn_g"], "mm_out")
    if late_weights is not None:
        wb = dict(wb, **late_weights(h2))
    up_pre = _mm_nn_cols(h2, wb["w_up"], min(l, 1024), "mm_up")
    conv_w = wb["conv_w"]
    act = _convffn_fwd(up_pre, conv_w, sp["conv_b"])
    loss, dx2, dx2b, d_final_g = _mm_res_loss(act, wb["w_down"], x1, sp["norm_final_g"].reshape(1, D_MODEL), target)

    g = {"norm_final_g": d_final_g.reshape(D_MODEL)}
    g["w_down"] = _mm_tn(act, dx2b, D_FF // 2, 512, "mm_down_dw")
    dup_pre, conv_par = _convffn_bwd(up_pre, dx2b, wb["w_down"], conv_w, sp["conv_b"])
    g["conv_w"], g["conv_b"] = conv_par[0:3], conv_par[3:4]
    g["w_up"] = _mm_tn_cols(h2, dup_pre, wb["w_up"].shape[0], 512, "mm_up_dw")
    dx1, dx1b, g["norm_ffn_g"] = _mm_cols_rms_bwd(dup_pre, wb["w_up"], x1, sp["norm_ffn_g"], dx2, "mm_up_dx")
    g["w_out"] = _mm_tn(mixed, dx1b, 1024, 1024, "mm_out_dw")
    zero = grads_ready(g["w_up"], g["w_down"], g["w_out"]) if grads_ready is not None else 0.0
    dattn, dys, g["norm_attn_g"], g["norm_ssm_g"] = _mm_mix_bwd(
        dx1b, wb["w_out"], attn, ys, sp["norm_attn_g"] + zero, sp["norm_ssm_g"], "mm_out_dx")
    dpre, zb, dsb, dd = _glu_bwd(pre, s_glu, dys, u, d_skip, wb["w_glu"])
    g["d_skip"] = dd.reshape(N_SSM_GROUPS, SSM_GROUP)
    g["w_glu"] = _mm_tn(zb, dsb, 512, 512, "mm_glu_dw")
    zero = grads_next(g["w_glu"]) if grads_next is not None else 0.0
    du_seg, dbcat, dccat, dlam_re, dlam_im = _ssm_bwd(u_seg, _to_segments(dpre), states, bcat, ccat,
                                                      lam_re + zero, lam_im)
    dlb_re, dlb_im, dbb_re, dbb_im, g["c_re"], g["c_im"] = _ssm_unpack(dbcat, dccat, dlam_re, dlam_im)
    _, disc_vjp = jax.vjp(_ssm_disc, sp["a_re"], sp["a_im"], sp["log_step"], sp["b_re"], sp["b_im"])
    g["a_re"], g["a_im"], g["log_step"], g["b_re"], g["b_im"] = disc_vjp((dlb_re, dlb_im, dbb_re, dbb_im))
    dq, dkv, g["sink"] = _attn_bwd(qkv, attn, dattn, lse, sp["sink"])
    dproj = _rope_bwd(dq, dkv, _from_segments(du_seg), dpre, d_skip, tabs)
    g["w_in"] = _mm_tn(dproj, h, IN_WIDTH // 5, D_MODEL, "mm_in_dw")
    grad_x, _, g["norm_mix_g"] = _mm_nn_rms_bwd(dproj, wb["w_in"], x, sp["norm_mix_g"], dx1, "mm_in_dx")
    return loss, grad_x, g


MESH = pl.DeviceIdType.MESH
ANY = pl.BlockSpec(memory_space=pl.ANY)


def _place():
    x, y, c = lax.axis_index("x"), lax.axis_index("y"), lax.axis_index("c")
    chips = [(1 - x, y), (x, 1 - y), (1 - x, 1 - y)]
    return x, y, c, chips


def _chip_index(px, py):
    return 2 * px + py


CHUNK_BYTES = 256 * 1024
MAX_CHUNKS = 16


def _row_chunks(rows, row_bytes, align):
    n = max(1, min(MAX_CHUNKS, (rows * row_bytes) // CHUNK_BYTES))
    per = -(-rows // n)
    per = -(-per // align) * align
    return [(r0, min(per, rows - r0)) for r0 in range(0, rows, per)]


def _align_of(dtype):
    return SUBLANES * 4 // jnp.dtype(dtype).itemsize


def _remote(src, dst, send_sem, recv_sem, to):
    return pltpu.make_async_remote_copy(src_ref=src, dst_ref=dst, send_sem=send_sem, recv_sem=recv_sem,
                                        device_id=to, device_id_type=MESH)


CAST_ROWS = 64


def _gather_weights(shards, dtypes):
    nw = len(shards)

    def body(*refs):
        w_refs, o_refs = refs[:nw], refs[nw:2 * nw]
        send_sems, recv_sems, in_sems, out_sems = refs[2 * nw:2 * nw + 4]
        raw, cast = refs[2 * nw + 4:3 * nw + 4], refs[3 * nw + 4:]
        x, y, c, chips = _place()
        mine = _chip_index(x, y)
        sibling = (x, y, 1 - c)

        def rows_of(ref, chip, r0, nr):
            return ref.at[chip, pl.ds(r0, nr), :]

        def copy(wi, k, src, dst, to):
            return _remote(src, dst, send_sems.at[wi, k], recv_sems.at[wi, k], to)

        geo = []
        for wi in range(nw):
            rows, cols = w_refs[wi].shape
            row_bytes = cols * jnp.dtype(dtypes[wi]).itemsize
            geo.append((rows // 2, _row_chunks(rows // 2, row_bytes, _align_of(dtypes[wi]))))

        stage_in = [pltpu.make_async_copy(w_refs[wi], raw[wi], in_sems.at[wi]) for wi in range(nw)]
        for cp in stage_in:
            cp.start()
        staged = [raw[wi] if dtypes[wi] == w_refs[wi].dtype else cast[wi] for wi in range(nw)]
        stage_out = []
        for wi in range(nw):
            stage_in[wi].wait()
            if staged[wi] is not raw[wi]:
                def cast_rows(i, _, wi=wi):
                    rows = pl.ds(pl.multiple_of(i * CAST_ROWS, CAST_ROWS), CAST_ROWS)
                    cast[wi][rows, :] = raw[wi][rows, :].astype(dtypes[wi])
                    return 0

                lax.fori_loop(0, w_refs[wi].shape[0] // CAST_ROWS, cast_rows, 0)
            cp = pltpu.make_async_copy(staged[wi], o_refs[wi].at[mine], out_sems.at[wi])
            cp.start()
            stage_out.append(cp)

        for wi in range(nw):
            hr, half_chunks = geo[wi]
            for j, chip in enumerate(chips):
                for r0, nr in half_chunks:
                    copy(wi, j, staged[wi].at[pl.ds(c * hr + r0, nr), :],
                         rows_of(o_refs[wi], mine, c * hr + r0, nr), (*chip, c)).start()
        for wi in range(nw):
            hr, half_chunks = geo[wi]
            for j, chip in enumerate(chips):
                got = rows_of(o_refs[wi], _chip_index(*chip), c * hr, hr)
                copy(wi, j, got, got, (*chip, c)).wait_recv()
                for r0, nr in half_chunks:
                    piece = rows_of(o_refs[wi], _chip_index(*chip), c * hr + r0, nr)
                    copy(wi, 3 + j, piece, piece, sibling).start()
        for wi in range(nw):
            hr = geo[wi][0]
            for j, chip in enumerate(chips):
                got = rows_of(o_refs[wi], _chip_index(*chip), (1 - c) * hr, hr)
                copy(wi, 3 + j, got, got, sibling).wait_recv()
        for wi in range(nw):
            hr = geo[wi][0]
            sent = rows_of(o_refs[wi], mine, c * hr, hr)
            for k in range(6):
                copy(wi, k, sent, sent, sibling).wait_send()
            stage_out[wi].wait()

    return pl.pallas_call(
        body, in_specs=[ANY] * nw, out_specs=[ANY] * nw,
        out_shape=[jax.ShapeDtypeStruct((4, *s.shape), t) for s, t in zip(shards, dtypes)],
        scratch_shapes=[pltpu.SemaphoreType.DMA((nw, 6)), pltpu.SemaphoreType.DMA((nw, 6)),
                        pltpu.SemaphoreType.DMA((nw,)), pltpu.SemaphoreType.DMA((nw,))]
        + [pltpu.VMEM(s.shape, s.dtype) for s in shards] + [pltpu.VMEM(s.shape, t) for s, t in zip(shards, dtypes)],
        name="gather_weights", compiler_params=_params(vmem_mb=40),
    )(*shards)


HBM = pl.BlockSpec(memory_space=pltpu.HBM)
SEM = pl.BlockSpec(memory_space=pltpu.SEMAPHORE)
EFFECT = pltpu.SideEffectType.DATAFLOW_SIDE_EFFECTING


def _cast_place(w, place, dtype, after, name):
    rows, cols = w.shape
    tr = _row_tile(rows, cols, _align_of(dtype))

    def body(p_ref, w_ref, after_ref, o_ref):
        del p_ref, after_ref
        o_ref[...] = w_ref[...].astype(dtype)

    grid_spec = pltpu.PrefetchScalarGridSpec(
        num_scalar_prefetch=1, grid=(rows // tr,),
        in_specs=[pl.BlockSpec((tr, cols), lambda i, p: (i, 0)), ANY],
        out_specs=pl.BlockSpec((None, tr, cols), lambda i, p: (p[1], i, 0)))
    return pl.pallas_call(body, grid_spec=grid_spec, out_shape=jax.ShapeDtypeStruct((4, rows, cols), dtype),
                          name=name, compiler_params=_params(("parallel",)))(place, w, after)


def _split_start(name, arrays, n_pairs, issue):
    n = len(arrays)

    def body(*refs):
        issue(refs[:n], refs[n:n + n_pairs], refs[n + n_pairs:n + 2 * n_pairs])
        token = refs[2 * n + 2 * n_pairs]
        token[...] = jnp.zeros_like(token)

    dma = pltpu.SemaphoreType.DMA(())
    outs = pl.pallas_call(
        body, name=name,
        out_shape=[dma] * (2 * n_pairs) + [pltpu.HBM(t.shape, t.dtype) for t in arrays]
        + [jax.ShapeDtypeStruct((SUBLANES, LANES), F32)],
        in_specs=[HBM] * n, out_specs=[SEM] * (2 * n_pairs) + [HBM] * n + [pl.BlockSpec(memory_space=pltpu.VMEM)],
        input_output_aliases={a: 2 * n_pairs + a for a in range(n)},
        compiler_params=pltpu.CompilerParams(has_side_effects=EFFECT),
    )(*[pltpu.with_memory_space_constraint(t, pltpu.HBM) for t in arrays])
    return outs[:n_pairs], outs[n_pairs:2 * n_pairs], outs[2 * n_pairs:2 * n_pairs + n], outs[-1]


def _split_wait(name, send_sems, recv_sems, flying, sizes, after):
    n, n_pairs = len(flying), len(send_sems)

    def body(*refs):
        x, y, c, _ = _place()
        for k, ref in enumerate(sizes(refs[:n])):
            cp = _remote(ref, ref, refs[n + k], refs[n + n_pairs + k], (x, y, 1 - c))
            cp.wait_send()
            cp.wait_recv()

    return pl.pallas_call(
        body, name=name, out_shape=[pltpu.HBM(t.shape, t.dtype) for t in flying],
        in_specs=[HBM] * n + [SEM] * (2 * n_pairs) + [ANY], out_specs=[HBM] * n,
        input_output_aliases={a: a for a in range(n)},
        compiler_params=pltpu.CompilerParams(has_side_effects=EFFECT),
    )(*flying, *send_sems, *recv_sems, after)


def _spread_start(lands, name):
    def issue(land_refs, send_sems, recv_sems):
        x, y, c, chips = _place()
        mine = _chip_index(x, y)
        for a, land in enumerate(land_refs):
            _, rows, cols = land.shape
            hr = rows // 2
            row_bytes = cols * jnp.dtype(land.dtype).itemsize
            for r0, nr in _row_chunks(hr, row_bytes, _align_of(land.dtype)):
                piece = land.at[mine, pl.ds(c * hr + r0, nr), :]
                for chip in chips:
                    for core in (0, 1):
                        _remote(piece, piece, send_sems[a], recv_sems[a], (*chip, core)).start()

    return _split_start(name, lands, len(lands), issue)


def _spread_wait(send_sems, recv_sems, flying, after, name):
    return _split_wait(name, send_sems, recv_sems, flying, lambda refs: [r.at[pl.ds(0, 3)] for r in refs], after)


def _pair_start(grads):
    n = len(grads)
    zones = [lax.empty((4, g.shape[1] // 2, g.shape[2]), F32) for g in grads]

    def issue(refs, send_sems, recv_sems):
        x, y, c, _ = _place()
        for a in range(n):
            g_ref, z_ref = refs[a], refs[n + a]
            _, rows, cols = g_ref.shape
            hr = rows // 2
            for k in range(4):
                for r0, nr in _row_chunks(hr, cols * 4, SUBLANES):
                    _remote(g_ref.at[k, pl.ds((1 - c) * hr + r0, nr), :], z_ref.at[k, pl.ds(r0, nr), :],
                            send_sems[a], recv_sems[a], (x, y, 1 - c)).start()

    return _split_start("pair_start", list(grads) + zones, n, issue)


def _pair_wait(send_sems, recv_sems, flying, after):
    n = len(flying) // 2
    out = _split_wait("pair_wait", send_sems, recv_sems, flying, lambda refs: list(refs[n:]), after)
    return out[:n], out[n:]


def _chip_start(sums):
    n = len(sums)
    zones = [lax.empty((3, *s.shape[1:]), s.dtype) for s in sums]

    def issue(refs, send_sems, recv_sems):
        x, y, c, chips = _place()
        for a in range(n):
            s_ref, z_ref = refs[a], refs[n + a]
            _, rows, cols = s_ref.shape
            row_bytes = cols * jnp.dtype(s_ref.dtype).itemsize
            for r0, nr in _row_chunks(rows, row_bytes, _align_of(s_ref.dtype)):
                for j, chip in enumerate(chips):
                    _remote(s_ref.at[_chip_index(*chip), pl.ds(r0, nr), :], z_ref.at[j, pl.ds(r0, nr), :],
                            send_sems[a], recv_sems[a], (*chip, c)).start()

    return _split_start("chip_start", list(sums) + zones, n, issue)


def _chip_wait(send_sems, recv_sems, flying, after):
    n = len(flying) // 2
    return _split_wait("chip_wait", send_sems, recv_sems, flying, lambda refs: list(refs[n:]), after)[n:]


def _pair_exchange(grads):
    na = len(grads)

    def body(*refs):
        g_refs, o_refs = refs[:na], refs[na:2 * na]
        send_sems, recv_sems = refs[2 * na:]
        x, y, c, _ = _place()
        sibling = (x, y, 1 - c)
        for ai in range(na):
            _, rows, cols = g_refs[ai].shape
            hr = rows // 2
            for k in range(4):
                for r0, nr in _row_chunks(hr, cols * 4, SUBLANES):
                    _remote(g_refs[ai].at[k, pl.ds((1 - c) * hr + r0, nr), :], o_refs[ai].at[k, pl.ds(r0, nr), :],
                            send_sems.at[ai], recv_sems.at[ai], sibling).start()
        for ai in range(na):
            _remote(o_refs[ai], o_refs[ai], send_sems.at[ai], recv_sems.at[ai], sibling).wait()

    return pl.pallas_call(
        body, in_specs=[ANY] * na, out_specs=[ANY] * na,
        out_shape=[jax.ShapeDtypeStruct((4, g.shape[1] // 2, g.shape[2]), F32) for g in grads],
        scratch_shapes=[pltpu.SemaphoreType.DMA((na,)), pltpu.SemaphoreType.DMA((na,))],
        name="pair_exchange",
    )(*grads)


def _row_tile(rows, cols, align, elems=256 * 1024):
    best = align
    for cand in range(align, rows + 1, align):
        if rows % cand == 0 and cand * cols <= elems:
            best = cand
    return best


def _pair_sum(g, got, place, transit, name):
    _, rows, cols = g.shape
    hr = rows // 2
    tr = _row_tile(hr, cols, _align_of(transit), 512 * 1024)
    nt = hr // tr

    def body(p_ref, g_ref, r_ref, s_ref, own_ref):
        total = g_ref[...] + r_ref[...]
        s_ref[...] = total.astype(transit)

        @pl.when(pl.program_id(1) == p_ref[1])
        def _():
            own_ref[...] = total

    grid_spec = pltpu.PrefetchScalarGridSpec(
        num_scalar_prefetch=1, grid=(nt, 4),
        in_specs=[pl.BlockSpec((None, tr, cols), lambda i, k, p: (k, p[0] * nt + i, 0)),
                  pl.BlockSpec((None, tr, cols), lambda i, k, p: (k, i, 0))],
        out_specs=[pl.BlockSpec((None, tr, cols), lambda i, k, p: (k, i, 0)),
                   pl.BlockSpec((tr, cols), lambda i, k, p: (i, 0))])
    return pl.pallas_call(
        body, grid_spec=grid_spec,
        out_shape=[jax.ShapeDtypeStruct((4, hr, cols), transit), jax.ShapeDtypeStruct((hr, cols), F32)],
        name=name, compiler_params=_params(("parallel", "arbitrary")),
    )(place, g, got)


def _chip_exchange(sums):
    na = len(sums)

    def body(*refs):
        s_refs, o_refs = refs[:na], refs[na:2 * na]
        send_sems, recv_sems = refs[2 * na:]
        x, y, c, chips = _place()
        for ai in range(na):
            _, rows, cols = s_refs[ai].shape
            row_bytes = cols * jnp.dtype(s_refs[ai].dtype).itemsize
            for r0, nr in _row_chunks(rows, row_bytes, _align_of(s_refs[ai].dtype)):
                for j, chip in enumerate(chips):
                    _remote(s_refs[ai].at[_chip_index(*chip), pl.ds(r0, nr), :], o_refs[ai].at[j, pl.ds(r0, nr), :],
                            send_sems.at[ai, j], recv_sems.at[ai, j], (*chip, c)).start()
        for ai in range(na):
            for j, chip in enumerate(chips):
                _remote(o_refs[ai].at[j], o_refs[ai].at[j], send_sems.at[ai, j], recv_sems.at[ai, j],
                        (*chip, c)).wait()

    return pl.pallas_call(
        body, in_specs=[ANY] * na, out_specs=[ANY] * na,
        out_shape=[jax.ShapeDtypeStruct((3, *s.shape[1:]), s.dtype) for s in sums],
        scratch_shapes=[pltpu.SemaphoreType.DMA((na, 3)), pltpu.SemaphoreType.DMA((na, 3))],
        name="chip_exchange",
    )(*sums)


def _chip_sum(own, landed, name):
    hr, cols = own.shape
    tr = _row_tile(hr, cols, _align_of(landed.dtype))

    def body(o_ref, l_ref, f_ref):
        acc = o_ref[...]
        for j in range(3):
            acc = acc + l_ref[j].astype(F32)
        f_ref[...] = acc

    return pl.pallas_call(
        body, grid=(hr // tr,),
        in_specs=[pl.BlockSpec((tr, cols), lambda i: (i, 0)), pl.BlockSpec((3, tr, cols), lambda i: (0, i, 0))],
        out_specs=pl.BlockSpec((tr, cols), lambda i: (i, 0)),
        out_shape=jax.ShapeDtypeStruct((hr, cols), F32), name=name,
        compiler_params=_params(("parallel",)),
    )(own, landed)


def _final_exchange(halves, small):
    nh = len(halves)

    def body(*refs):
        h_refs, s_ref = refs[:nh], refs[nh]
        o_refs, so_ref = refs[nh + 1:2 * nh + 1], refs[2 * nh + 1]
        send_sems, recv_sems, local_sem, ssend_sems, srecv_sems = refs[2 * nh + 2:]
        x, y, c, _ = _place()
        me = 4 * x + 2 * y + c
        sibling = (x, y, 1 - c)
        for hi in range(nh):
            hr, cols = h_refs[hi].shape
            for r0, nr in _row_chunks(hr, cols * 4, SUBLANES):
                _remote(h_refs[hi].at[pl.ds(r0, nr), :], o_refs[hi].at[pl.ds(r0, nr), :],
                        send_sems.at[hi], recv_sems.at[hi], sibling).start()
        small_cps = [pltpu.make_async_copy(s_ref, so_ref.at[me], local_sem)]
        for r in range(1, 8):
            fx, fy, fc = (r >> 2) & 1, (r >> 1) & 1, r & 1
            peer = (1 - x if fx else x, 1 - y if fy else y, 1 - c if fc else c)
            small_cps.append(_remote(s_ref, so_ref.at[me], ssend_sems.at[r - 1], srecv_sems.at[r - 1], peer))
        for cp in small_cps:
            cp.start()
        for hi in range(nh):
            _remote(h_refs[hi], o_refs[hi], send_sems.at[hi], recv_sems.at[hi], sibling).wait()
        for cp in small_cps:
            cp.wait()

    return pl.pallas_call(
        body, in_specs=[ANY] * (nh + 1), out_specs=[ANY] * (nh + 1),
        out_shape=[jax.ShapeDtypeStruct(h.shape, F32) for h in halves]
        + [jax.ShapeDtypeStruct((8, *small.shape), F32)],
        scratch_shapes=[pltpu.SemaphoreType.DMA((nh,)), pltpu.SemaphoreType.DMA((nh,)),
                        pltpu.SemaphoreType.DMA, pltpu.SemaphoreType.DMA((7,)), pltpu.SemaphoreType.DMA((7,))],
        name="final_exchange",
    )(*halves, small)


def _adamw_halves(w, own, other, m, v, place, name):
    r, c = w.shape
    hr = r // 2
    tr = _row_tile(hr, c, SUBLANES, 384 * 1024)
    nt = hr // tr
    c1 = 1.0 - ADAM_B1 ** ADAM_STEP
    c2 = 1.0 - ADAM_B2 ** ADAM_STEP

    def body(p_ref, w_ref, own_ref, other_ref, m_ref, v_ref, g_ref, d_ref, nm_ref, nv_ref):
        mine = pl.program_id(0) // nt == p_ref[0]
        gv = jnp.where(mine, own_ref[...], other_ref[...])
        nm = ADAM_B1 * m_ref[...] + (1.0 - ADAM_B1) * gv
        nv = ADAM_B2 * v_ref[...] + (1.0 - ADAM_B2) * (gv * gv)
        g_ref[...] = gv
        d_ref[...] = -ADAM_LR * ((nm / c1) / (jnp.sqrt(nv / c2) + ADAM_EPS) + ADAM_WD * w_ref[...])
        nm_ref[...] = nm
        nv_ref[...] = nv

    full = pl.BlockSpec((tr, c), lambda i, p: (i, 0))
    own_half = pl.BlockSpec((tr, c), lambda i, p: (jnp.where(i // nt == p[0], i % nt, 0), 0))
    other_half = pl.BlockSpec((tr, c), lambda i, p: (jnp.where(i // nt == p[0], 0, i % nt), 0))
    out = jax.ShapeDtypeStruct((r, c), F32)
    grid_spec = pltpu.PrefetchScalarGridSpec(num_scalar_prefetch=1, grid=(2 * nt,),
                                             in_specs=[full, own_half, other_half, full, full],
                                             out_specs=[full] * 4)
    return pl.pallas_call(body, grid_spec=grid_spec, out_shape=[out] * 4, name=name,
                          compiler_params=_params(("parallel",)))(place, w, own, other, m, v)


def _adamw_many(ws, gs, ms, vs, name):
    n = len(ws)
    c1 = 1.0 - ADAM_B1 ** ADAM_STEP
    c2 = 1.0 - ADAM_B2 ** ADAM_STEP

    def body(*refs):
        w_refs, g_refs, m_refs, v_refs = (refs[k * n:(k + 1) * n] for k in range(4))
        d_refs, nm_refs, nv_refs = (refs[(4 + k) * n:(5 + k) * n] for k in range(3))
        for i in range(n):
            gv = g_refs[i][...]
            nm = ADAM_B1 * m_refs[i][...] + (1.0 - ADAM_B1) * gv
            nv = ADAM_B2 * v_refs[i][...] + (1.0 - ADAM_B2) * (gv * gv)
            d_refs[i][...] = -ADAM_LR * ((nm / c1) / (jnp.sqrt(nv / c2) + ADAM_EPS) + ADAM_WD * w_refs[i][...])
            nm_refs[i][...] = nm
            nv_refs[i][...] = nv

    vmem = pl.BlockSpec(memory_space=pltpu.VMEM)
    shapes = [jax.ShapeDtypeStruct(t.shape, F32) for t in ws]
    outs = pl.pallas_call(body, in_specs=[vmem] * (4 * n), out_specs=[vmem] * (3 * n), out_shape=shapes * 3,
                          name=name, compiler_params=_params(vmem_mb=56))(*ws, *gs, *ms, *vs)
    return outs[:n], outs[n:2 * n], outs[2 * n:]


BIG = ("w_in", "w_glu", "w_out", "w_up", "w_down")
WEIGHTS = ("norm_mix_g", "w_in", "a_re", "a_im", "log_step", "b_re", "b_im", "c_re", "c_im", "d_skip", "w_glu",
           "sink", "norm_attn_g", "norm_ssm_g", "w_out", "norm_ffn_g", "w_up", "conv_w", "conv_b", "w_down",
           "norm_final_g")
SMALL = ("norm_mix_g", "a_re", "a_im", "log_step", "b_re", "b_im", "c_re", "c_im", "d_skip", "sink",
         "norm_attn_g", "norm_ssm_g", "norm_ffn_g", "conv_w", "conv_b", "norm_final_g")
SMALL_ROWS = 48
N_DEV = 8


def _tile_rows(size):
    return -(-size // (SUBLANES * D_MODEL)) * SUBLANES


def _by_owner(name, g):
    if name == "w_up":
        return g
    return g.reshape(4, g.shape[0] // 4, g.shape[1])


def _view(name, t):
    if name == "w_in":
        return jnp.swapaxes(t[0], 0, 1)
    if name in ("b_re", "b_im"):
        return jnp.swapaxes(t, -1, -2)
    return t


def _unview(name, t):
    if name == "w_in":
        return jnp.swapaxes(t, 0, 1)[None]
    if name in ("b_re", "b_im"):
        return jnp.swapaxes(t, -1, -2)
    return t


def kernel(x, norm_mix_g, w_in, a_re, a_im, log_step, b_re, b_im, c_re, c_im, d_skip, w_glu, sink, norm_attn_g, norm_ssm_g, w_out, norm_ffn_g, w_up, conv_w, conv_b, w_down, norm_final_g, loss_target, m_norm_mix_g, m_w_in, m_a_re, m_a_im, m_log_step, m_b_re, m_b_im, m_c_re, m_c_im, m_d_skip, m_w_glu, m_sink, m_norm_attn_g, m_norm_ssm_g, m_w_out, m_norm_ffn_g, m_w_up, m_conv_w, m_conv_b, m_w_down, m_norm_final_g, v_norm_mix_g, v_w_in, v_a_re, v_a_im, v_log_step, v_b_re, v_b_im, v_c_re, v_c_im, v_d_skip, v_w_glu, v_sink, v_norm_attn_g, v_norm_ssm_g, v_w_out, v_norm_ffn_g, v_w_up, v_conv_w, v_conv_b, v_w_down, v_norm_final_g):
    given = dict(locals())
    w = {n: given[n] for n in WEIGHTS}
    m = {n: given["m_" + n] for n in WEIGHTS}
    v = {n: given["v_" + n] for n in WEIGHTS}
    xy = 2 * lax.axis_index("x") + lax.axis_index("y")

    core = lax.axis_index("c")
    place = jnp.stack([core, xy]).astype(jnp.int32)

    conv_rows = jnp.pad(w["conv_w"][0], ((0, 2 * SUBLANES - 3), (0, 0)))
    rows = lambda t: t.reshape(4 * t.shape[1], t.shape[2])
    (w_in_all,) = _gather_weights([_view("w_in", w["w_in"])], [BF16])
    wb = {"w_in": rows(w_in_all)}
    mixer = [_cast_place(w[n][0], place, BF16, w_in_all, "cast_" + n) for n in ("w_glu", "w_out")]
    mixer.append(_cast_place(conv_rows, place, F32, w_in_all, "cast_conv_w"))
    *mixer_flight, mixer_token = _spread_start(mixer, "spread_mixer_start")
    late = ("w_up", "w_down")
    *late_flight, token = _spread_start(
        [_cast_place(w[n][0], place, BF16, mixer_token, "cast_" + n) for n in late], "spread_ffn_start")

    def mixer_weights(after):
        w_glu4, w_out4, conv4 = _spread_wait(*mixer_flight, after, "spread_mixer_wait")
        return {"w_glu": rows(w_glu4), "w_out": rows(w_out4),
                "conv_w": conv4[:, :3].transpose(1, 0, 2).reshape(3, 2 * D_FF)}

    def late_weights(after):
        w_up4, w_down4 = _spread_wait(*late_flight, after, "spread_ffn_wait")
        return {"w_up": w_up4, "w_down": rows(w_down4)}

    sp = {n: w[n][0] for n in ("a_re", "a_im", "log_step", "b_re", "b_im", "c_re", "c_im", "d_skip",
                               "norm_mix_g", "norm_attn_g", "norm_ssm_g", "norm_ffn_g", "sink", "conv_b")}
    for n in ("norm_mix_g", "norm_attn_g", "norm_ssm_g", "norm_ffn_g", "sink", "conv_b"):
        sp[n] = sp[n].reshape(1, -1)
    sp["norm_mix_g"] = sp["norm_mix_g"] + token[:1, :1]
    sp["norm_final_g"] = w["norm_final_g"]
    early, tail = late + ("w_out",), ("w_in", "w_glu")
    flight = {}

    def grads_ready(dw_up, dw_down, dw_out):
        *flight["pair"], token = _pair_start([dw_up, _by_owner("w_down", dw_down), _by_owner("w_out", dw_out)])
        return token[:1, :1]

    def grads_next(after):
        mine, got = _pair_wait(*flight["pair"], after)
        sums, flight["own"] = zip(*[_pair_sum(a, b, place, BF16, "pair_sum_" + n) for n, a, b in zip(early, mine, got)])
        *flight["chip"], token = _chip_start(list(sums))
        return token[:1, :1]

    loss, grad_x, g = _local_step(x[0], loss_target[0], wb, sp, mixer_weights, late_weights, grads_ready,
                                  grads_next)

    def as_rows(t):
        rows = _tile_rows(t.size)
        return jnp.pad(t.reshape(-1), (0, rows * D_MODEL - t.size)).reshape(rows, D_MODEL)

    pieces = [as_rows(g[n]) for n in SMALL] + [as_rows(loss)]
    spare = N_DEV * SMALL_ROWS - sum(p.shape[0] for p in pieces)
    small = jnp.concatenate(pieces + [jnp.zeros((spare, D_MODEL), F32)]).reshape(4, 2 * SMALL_ROWS, D_MODEL)
    by_owner = [_by_owner(n, g[n]) for n in tail] + [small]
    got = _pair_exchange(by_owner)
    transit = [BF16] * len(tail) + [F32]
    chip_sums, own_sums = zip(*[_pair_sum(a, b, place, t, "pair_sum_" + n)
                                for n, a, b, t in zip(tail + ("small",), by_owner, got, transit)])
    landed = _chip_exchange(list(chip_sums))
    halves = {n: _chip_sum(o, t, "chip_sum_" + n) for n, o, t in zip(tail + ("small",), own_sums, landed)}
    early_landed = _chip_wait(*flight["chip"], grad_x)
    for n, o, t in zip(early, flight["own"], early_landed):
        halves[n] = _chip_sum(o, t, "chip_sum_" + n)
    *others, small_all = _final_exchange([halves[n] for n in BIG], halves["small"])
    small_all = small_all.reshape(N_DEV * SMALL_ROWS, D_MODEL)
    grads, row = {}, 0
    for n in SMALL:
        shape = (3, 4 * w[n].shape[-1]) if n == "conv_w" else w[n].shape[1:] if n != "norm_final_g" else w[n].shape
        size = math.prod(shape)
        grads[n] = small_all[row:row + _tile_rows(size)].reshape(-1)[:size].reshape(shape)
        row += _tile_rows(size)
    loss = small_all[row, 0]
    cw = w["conv_w"].shape[-1]
    grads["conv_w"] = lax.dynamic_slice_in_dim(grads["conv_w"], xy * cw, cw, axis=1)
    grads = {n: _view(n, grads[n].reshape(w[n].shape)) for n in SMALL}
    wv, mv, vv = ({n: _view(n, t[n]) for n in WEIGHTS} for t in (w, m, v))

    delta, new_m, new_v = {}, {}, {}
    for n, other in zip(BIG, others):
        two_d = lambda t: t.reshape(t.shape[-2:])
        grads[n], delta[n], new_m[n], new_v[n] = _adamw_halves(
            two_d(wv[n]), halves[n], other, two_d(mv[n]), two_d(vv[n]), place, "adamw_" + n)
    for group, name in ((("b_re", "b_im"), "adamw_b"), (tuple(n for n in SMALL if n not in ("b_re", "b_im")), "adamw_small")):
        row = lambda t: t.reshape(1, -1) if t.ndim == 1 else t
        d_, m_, v_ = _adamw_many(*[[row(t[n]) for n in group] for t in (wv, grads, mv, vv)], name)
        for n, dn, mn, vn in zip(group, d_, m_, v_):
            delta[n], new_m[n], new_v[n] = (t.reshape(wv[n].shape) for t in (dn, mn, vn))
    natural = lambda t: [_unview(n, t[n].reshape(wv[n].shape)) for n in WEIGHTS]
    return (loss, grad_x[None], *natural(grads), *natural(delta), *natural(new_m), *natural(new_v))
```

```python
import functools
import math

import jax
import jax.numpy as jnp
import numpy as np
from jax import lax
from jax.experimental import pallas as pl
from jax.experimental.pallas import tpu as pltpu

F32 = jnp.float32
BF16 = jnp.bfloat16

D_MODEL = 1024
N_Q_HEADS = 8
N_KV_HEADS = 2
HEAD_DIM = 64
ATTN_WIDTH = 512
KV_WIDTH = 128
QKV_WIDTH = ATTN_WIDTH + 2 * KV_WIDTH
WINDOW = 128
BLOCK = 128
ROPE_DIM = 16
ROPE_THETA = 500000.0
SCORE_SCALE = HEAD_DIM ** -0.5
SSM_WIDTH = 512
SSM_GROUP = 16
N_SSM_GROUPS = 32
SSM_STATE = 64
IN_WIDTH = 1280
D_FF = 2816
EPS = 1e-6
ADAM_LR = 0.001
ADAM_B1 = 0.9
ADAM_B2 = 0.999
ADAM_EPS = 1e-08
ADAM_WD = 0.01
ADAM_STEP = 10

VMEM_BYTES_V7X = 64 * 1024 * 1024
SUBLANES = 8
LANES = 128
SSM_CB = 4
SSM_CH = 128
SSM_ST = 512
N_SEG = SUBLANES

NN = (((1,), (0,)), ((), ()))
NT = (((1,), (1,)), ((), ()))
TN = (((0,), (0,)), ((), ()))


def _params(sem=None, vmem_mb=48):
    limit = vmem_mb * 1024 * 1024
    assert limit < VMEM_BYTES_V7X
    return pltpu.CompilerParams(dimension_semantics=sem, vmem_limit_bytes=limit)


def _dg(a, b, dims):
    return lax.dot_general(a, b, dims, preferred_element_type=F32)


def _sigmoid(x):
    return 1.0 / (1.0 + jnp.exp(-x))


_SQRT_HALF = 0.7071067811865476
_INV_SQRT_2PI = 0.3989422804014327


def _gelu(x):
    return 0.5 * x * (1.0 + lax.erf(x * _SQRT_HALF))


def _gelu_grad(x):
    return 0.5 * (1.0 + lax.erf(x * _SQRT_HALF)) + x * (_INV_SQRT_2PI * jnp.exp(-0.5 * x * x))


def _mm_tn(a, b, tm, tn, name):
    k, m = a.shape
    n = b.shape[1]

    def body(a_ref, b_ref, o_ref):
        o_ref[...] = _dg(a_ref[...], b_ref[...], TN)

    return pl.pallas_call(
        body, grid=(m // tm, n // tn),
        in_specs=[pl.BlockSpec((k, tm), lambda i, j: (0, i)), pl.BlockSpec((k, tn), lambda i, j: (0, j))],
        out_specs=pl.BlockSpec((tm, tn), lambda i, j: (i, j)),
        out_shape=jax.ShapeDtypeStruct((m, n), F32), name=name,
        compiler_params=_params(("parallel", "parallel")),
    )(a, b)


def _mm_nn_cols(a, b4, tm, name):
    m, k = a.shape
    s, _, n = b4.shape

    def body(a_ref, b_ref, o_ref):
        o_ref[...] = _dg(a_ref[...], b_ref[...], NN)

    return pl.pallas_call(
        body, grid=(m // tm, s),
        in_specs=[pl.BlockSpec((tm, k), lambda i, j: (i, 0)), pl.BlockSpec((None, k, n), lambda i, j: (j, 0, 0))],
        out_specs=pl.BlockSpec((tm, n), lambda i, j: (i, j)),
        out_shape=jax.ShapeDtypeStruct((m, s * n), F32), name=name,
        compiler_params=_params(("parallel", "parallel")),
    )(a, b4)


def _mm_tn_cols(a, b2, s, tm, name):
    k, m = a.shape
    h, _, wide = b2.shape
    per = s // h
    n = wide // per

    def body(a_ref, b_ref, o_ref):
        o_ref[...] = _dg(a_ref[...], b_ref[...], TN)

    return pl.pallas_call(
        body, grid=(s, m // tm),
        in_specs=[pl.BlockSpec((k, tm), lambda j, i: (0, i)),
                  pl.BlockSpec((None, k, n), lambda j, i: (j // per, 0, j % per))],
        out_specs=pl.BlockSpec((None, tm, n), lambda j, i: (j, i, 0)),
        out_shape=jax.ShapeDtypeStruct((s, m, n), F32), name=name,
        compiler_params=_params(("parallel", "parallel")),
    )(a, b2)


TM_EW = 512


def _rms_bwd_vals(xv, gv, dy):
    r = lax.rsqrt(jnp.mean(xv * xv, axis=-1, keepdims=True) + EPS)
    xh = xv * r
    dxh = dy * gv
    dx = r * (dxh - xh * jnp.mean(dxh * xh, axis=-1, keepdims=True))
    return dx, dy * xh


TM_FUSED = 512
TM_LOSS = 256


def _rms_vals(xv, gv):
    return xv * lax.rsqrt(jnp.mean(xv * xv, axis=-1, keepdims=True) + EPS) * gv


def _rope_blocks(src, dst, c, lo, hi):
    nq = ATTN_WIDTH // LANES
    for blk in range(nq + 1):
        t = src[:, blk * LANES:(blk + 1) * LANES]
        rot = t * c + pltpu.roll(t, LANES - 8, 1) * lo + pltpu.roll(t, 8, 1) * hi
        dst[:, blk * LANES:(blk + 1) * LANES] = (rot * SCORE_SCALE if blk < nq else rot).astype(BF16)
    dst[:, (nq + 1) * LANES:] = src[:, (nq + 1) * LANES:].astype(BF16)


def _rms_mm_rope(x, g, wt, tabs, name):
    l, d = x.shape
    n = wt.shape[0]

    def body(x_ref, g_ref, w_ref, c_ref, lo_ref, hi_ref, h_ref, qkv_ref, u_ref):
        h = _rms_vals(x_ref[...], g_ref[...]).astype(BF16)
        h_ref[...] = h
        out = _dg(h, w_ref[...], NT)
        _rope_blocks(out[:, :QKV_WIDTH], qkv_ref, c_ref[...], lo_ref[...], hi_ref[...])
        u_ref[...] = out[:, QKV_WIDTH:]

    row = lambda width: pl.BlockSpec((TM_FUSED, width), lambda i: (i, 0))
    return pl.pallas_call(
        body, grid=(l // TM_FUSED,),
        in_specs=[row(d), pl.BlockSpec((1, d), lambda i: (0, 0)), pl.BlockSpec((n, d), lambda i: (0, 0)),
                  row(LANES), row(LANES), row(LANES)],
        out_specs=[row(d), row(QKV_WIDTH), row(n - QKV_WIDTH)],
        out_shape=[jax.ShapeDtypeStruct((l, d), BF16), jax.ShapeDtypeStruct((l, QKV_WIDTH), BF16),
                   jax.ShapeDtypeStruct((l, n - QKV_WIDTH), F32)],
        name=name, compiler_params=_params(("parallel",)),
    )(x, g, wt, *tabs)


def _mix_mm_res_rms(attn, ys, g_attn, g_ssm, b, res, g, name):
    l, w = attn.shape
    d = b.shape[1]

    def body(a_ref, y_ref, ga_ref, gs_ref, b_ref, r_ref, g_ref, m_ref, x_ref, h_ref):
        m_ref[:, :w] = _rms_vals(a_ref[...], ga_ref[...]).astype(BF16)
        m_ref[:, w:] = _rms_vals(y_ref[...], gs_ref[...]).astype(BF16)
        xv = r_ref[...] + _dg(m_ref[...], b_ref[...], NN)
        x_ref[...] = xv
        h_ref[...] = _rms_vals(xv, g_ref[...]).astype(BF16)

    row = lambda width: pl.BlockSpec((TM_FUSED, width), lambda i: (i, 0))
    vec = lambda width: pl.BlockSpec((1, width), lambda i: (0, 0))
    return pl.pallas_call(
        body, grid=(l // TM_FUSED,),
        in_specs=[row(w), row(w), vec(w), vec(w), pl.BlockSpec((2 * w, d), lambda i: (0, 0)), row(d), vec(d)],
        out_specs=[row(2 * w), row(d), row(d)],
        out_shape=[jax.ShapeDtypeStruct((l, 2 * w), BF16), jax.ShapeDtypeStruct((l, d), F32),
                   jax.ShapeDtypeStruct((l, d), BF16)],
        name=name, compiler_params=_params(("parallel",)),
    )(attn, ys, g_attn, g_ssm, b, res, g)


def _mm_res_loss(a, b, res, g, target):
    l, k = a.shape
    d = b.shape[1]

    def body(a_ref, b_ref, r_ref, g_ref, t_ref, loss_ref, dx_ref, dxb_ref, dg_ref):
        xv = r_ref[...] + _dg(a_ref[...], b_ref[...], NN)
        gv = g_ref[...]
        r = lax.rsqrt(jnp.mean(xv * xv, axis=-1, keepdims=True) + EPS)
        xh = xv * r
        e = xh * gv - t_ref[...]
        part = jnp.sum(jnp.sum(e * e, axis=1, keepdims=True), axis=0, keepdims=True) * (0.5 / d)
        dy = e * (1.0 / d)
        dxh = dy * gv
        dx = r * (dxh - xh * jnp.mean(dxh * xh, axis=-1, keepdims=True))
        dx_ref[...] = dx
        dxb_ref[...] = dx.astype(BF16)

        @pl.when(pl.program_id(0) == 0)
        def _():
            dg_ref[...] = jnp.zeros_like(dg_ref)
            loss_ref[...] = jnp.zeros_like(loss_ref)

        dg_ref[...] += jnp.sum(dy * xh, axis=0, keepdims=True)
        loss_ref[...] += part

    row = lambda width: pl.BlockSpec((TM_LOSS, width), lambda i: (i, 0))
    vec = pl.BlockSpec((1, d), lambda i: (0, 0))
    return pl.pallas_call(
        body, grid=(l // TM_LOSS,),
        in_specs=[row(k), pl.BlockSpec((k, d), lambda i: (0, 0)), row(d), vec, row(d)],
        out_specs=[pl.BlockSpec((1, 1), lambda i: (0, 0)), row(d), row(d), vec],
        out_shape=[jax.ShapeDtypeStruct((1, 1), F32), jax.ShapeDtypeStruct((l, d), F32),
                   jax.ShapeDtypeStruct((l, d), BF16), jax.ShapeDtypeStruct((1, d), F32)],
        name="mm_down_loss", compiler_params=_params(("arbitrary",)),
    )(a, b, res, g, target)


def _mm_rms_bwd(a, b, a_spec, b_spec, matmul, x, g, res, name):
    l, d = x.shape

    def body(a_ref, b_ref, x_ref, g_ref, res_ref, dx_ref, dxb_ref, dg_ref):
        dx, dgr = _rms_bwd_vals(x_ref[...], g_ref[...], matmul(a_ref, b_ref))
        dx = dx + res_ref[...]
        dx_ref[...] = dx
        dxb_ref[...] = dx.astype(BF16)

        @pl.when(pl.program_id(0) == 0)
        def _():
            dg_ref[...] = jnp.zeros_like(dg_ref)

        dg_ref[...] += jnp.sum(dgr, axis=0, keepdims=True)

    row = pl.BlockSpec((TM_FUSED, d), lambda i: (i, 0))
    vec = pl.BlockSpec((1, d), lambda i: (0, 0))
    return pl.pallas_call(
        body, grid=(l // TM_FUSED,), in_specs=[a_spec, b_spec, row, vec, row], out_specs=[row, row, vec],
        out_shape=[jax.ShapeDtypeStruct((l, d), F32), jax.ShapeDtypeStruct((l, d), BF16),
                   jax.ShapeDtypeStruct((1, d), F32)],
        name=name, compiler_params=_params(("arbitrary",)),
    )(a, b, x, g, res)


def _mm_nn_rms_bwd(a, b, x, g, res, name):
    return _mm_rms_bwd(a, b, pl.BlockSpec((TM_FUSED, a.shape[1]), lambda i: (i, 0)),
                       pl.BlockSpec(b.shape, lambda i: (0, 0)),
                       lambda a_ref, b_ref: _dg(a_ref[...], b_ref[...], NN), x, g, res, name)


def _mm_cols_rms_bwd(a2, b4, x, g, res, name):
    h, _, wide = a2.shape
    s, _, n = b4.shape
    per = s // h

    def matmul(a_ref, b_ref):
        acc = None
        for j in range(s):
            part = _dg(a_ref[j // per, :, (j % per) * n:(j % per + 1) * n], b_ref[j], NT)
            acc = part if acc is None else acc + part
        return acc

    return _mm_rms_bwd(a2, b4, pl.BlockSpec((h, TM_FUSED, wide), lambda i: (0, i, 0)),
                       pl.BlockSpec(b4.shape, lambda i: (0, 0, 0), pipeline_mode=pl.Buffered(1)),
                       matmul, x, g, res, name)


def _mm_mix_bwd(dx, b, attn, ys, g_attn, g_ssm, name):
    l, w = attn.shape
    d = dx.shape[1]

    def body(dx_ref, b_ref, a_ref, y_ref, ga_ref, gs_ref, da_ref, dy_ref, dga_ref, dgs_ref):
        @pl.when(pl.program_id(0) == 0)
        def _():
            dga_ref[...] = jnp.zeros_like(dga_ref)
            dgs_ref[...] = jnp.zeros_like(dgs_ref)

        dm = _dg(dx_ref[...], b_ref[...], NT)
        for src, gr, off, dst, dgr in ((a_ref, ga_ref, 0, da_ref, dga_ref), (y_ref, gs_ref, w, dy_ref, dgs_ref)):
            dxv, dg_rows = _rms_bwd_vals(src[...], gr[...], dm[:, off:off + w])
            dst[...] = dxv
            dgr[...] += jnp.sum(dg_rows, axis=0, keepdims=True)

    row = lambda width: pl.BlockSpec((TM_FUSED, width), lambda i: (i, 0))
    vec = pl.BlockSpec((1, w), lambda i: (0, 0))
    return pl.pallas_call(
        body, grid=(l // TM_FUSED,),
        in_specs=[row(d), pl.BlockSpec((2 * w, d), lambda i: (0, 0)), row(w), row(w), vec, vec],
        out_specs=[row(w), row(w), vec, vec],
        out_shape=[jax.ShapeDtypeStruct((l, w), F32), jax.ShapeDtypeStruct((l, w), F32),
                   jax.ShapeDtypeStruct((1, w), F32), jax.ShapeDtypeStruct((1, w), F32)],
        name=name, compiler_params=_params(("arbitrary",)),
    )(dx, b, attn, ys, g_attn, g_ssm)


def _rope_tables(l):
    half = ROPE_DIM // 2
    f32 = np.float32
    inv_freq = np.power(f32(ROPE_THETA), -np.arange(half, dtype=f32) / f32(half))
    ang = np.arange(l, dtype=f32)[:, None] * inv_freq[None, :]
    cos, sin = np.cos(ang), np.sin(ang)
    ones = np.ones((l, HEAD_DIM - ROPE_DIM), f32)
    zeros = np.zeros((l, HEAD_DIM - ROPE_DIM), f32)
    zh = np.zeros((l, half), f32)
    c = np.concatenate([cos, cos, ones], axis=1)
    s_lo = np.concatenate([-sin, zh, zeros], axis=1)
    s_hi = np.concatenate([zh, sin, zeros], axis=1)
    return tuple(jnp.asarray(np.tile(t, (1, LANES // HEAD_DIM)), F32) for t in (c, s_lo, s_hi))


def _rope_bwd(dq, dkv, du_ssm, dpre, d_skip, tabs):
    l = dq.shape[0]
    nq = ATTN_WIDTH // LANES

    def body(dq_ref, dkv_ref, du_ref, dpre_ref, ds_ref, c_ref, lo_ref, hi_ref, o_ref):
        c, lo, hi = c_ref[...], lo_ref[...], hi_ref[...]
        for blk in range(nq + 1):
            t = dq_ref[:, blk * LANES:(blk + 1) * LANES] if blk < nq else dkv_ref[:, :KV_WIDTH]
            g = t * c + pltpu.roll(t * lo, 8, 1) + pltpu.roll(t * hi, LANES - 8, 1)
            o_ref[:, blk * LANES:(blk + 1) * LANES] = g.astype(BF16)
        o_ref[:, (nq + 1) * LANES:QKV_WIDTH] = dkv_ref[:, KV_WIDTH:].astype(BF16)
        o_ref[:, QKV_WIDTH:] = (du_ref[...] + dpre_ref[...] * ds_ref[...]).astype(BF16)

    tab = pl.BlockSpec((TM_EW, LANES), lambda i: (i, 0))
    wide = pl.BlockSpec((TM_EW, SSM_WIDTH), lambda i: (i, 0))
    return pl.pallas_call(
        body, grid=(l // TM_EW,),
        in_specs=[wide, pl.BlockSpec((TM_EW, 2 * KV_WIDTH), lambda i: (i, 0)), wide, wide,
                  pl.BlockSpec((1, SSM_WIDTH), lambda i: (0, 0)), tab, tab, tab],
        out_specs=pl.BlockSpec((TM_EW, IN_WIDTH), lambda i: (i, 0)),
        out_shape=jax.ShapeDtypeStruct((l, IN_WIDTH), BF16), name="rope_bwd",
        compiler_params=_params(("parallel",)),
    )(dq, dkv, du_ssm, dpre, d_skip, *tabs)


_Q_COLS = ATTN_WIDTH // LANES
_NEG = -1e30


def _window_specs(nb, width, col):
    return [
        pl.BlockSpec((BLOCK, width), lambda n: (jnp.maximum(n - 1, 0), col)),
        pl.BlockSpec((BLOCK, width), lambda n: (n, col)),
        pl.BlockSpec((BLOCK, width), lambda n: (jnp.minimum(n + 1, nb - 1), col)),
    ]


def _stacked_sink(sink_ref, heads):
    rid = lax.broadcasted_iota(jnp.int32, (len(heads) * BLOCK, 1), 0)
    sk = jnp.full(rid.shape, sink_ref[0, heads[-1]], F32)
    for g in range(len(heads) - 2, -1, -1):
        sk = jnp.where(rid < (g + 1) * BLOCK, sink_ref[0, heads[g]], sk)
    return sk


def _attn_fwd(qkv, sink):
    l = qkv.shape[0]
    nb = l // BLOCK
    grp = N_Q_HEADS // N_KV_HEADS

    def body(sink_ref, q_ref, k0, k1, k2, v0, v1, v2, o_ref, lse_ref):
        n = pl.program_id(0)
        q = q_ref[...]
        kw = jnp.concatenate([k0[...], k1[...], k2[...]], axis=0)
        vw = jnp.concatenate([v0[...], v1[...], v2[...]], axis=0)
        row = lax.broadcasted_iota(jnp.int32, (grp * BLOCK, 3 * BLOCK), 0)
        col = lax.broadcasted_iota(jnp.int32, (grp * BLOCK, 3 * BLOCK), 1)
        valid = jnp.abs(col - BLOCK - (row & (BLOCK - 1))) <= WINDOW
        valid &= jnp.logical_not((n == 0) & (col < BLOCK))
        valid &= jnp.logical_not((n == nb - 1) & (col >= 2 * BLOCK))
        for hk in range(N_KV_HEADS):
            heads = range(hk * grp, (hk + 1) * grp)
            qs = jnp.concatenate([q[:, h * HEAD_DIM:(h + 1) * HEAD_DIM] for h in heads], axis=0)
            kh = kw[:, hk * HEAD_DIM:(hk + 1) * HEAD_DIM]
            vh = vw[:, hk * HEAD_DIM:(hk + 1) * HEAD_DIM]
            s = jnp.where(valid, _dg(qs, kh, NT), _NEG)
            sk = _stacked_sink(sink_ref, heads)
            m = jnp.maximum(jnp.max(s, axis=1, keepdims=True), sk)
            p = jnp.exp(s - m)
            denom = jnp.sum(p, axis=1, keepdims=True) + jnp.exp(sk - m)
            o = _dg((p / denom).astype(BF16), vh, NN)
            lse = m + jnp.log(denom)
            for g, h in enumerate(heads):
                o_ref[:, h * HEAD_DIM:(h + 1) * HEAD_DIM] = o[g * BLOCK:(g + 1) * BLOCK]
                lse_ref[:, h:h + 1] = lse[g * BLOCK:(g + 1) * BLOCK]

    return pl.pallas_call(
        body, grid=(nb,),
        in_specs=[pl.BlockSpec(memory_space=pltpu.SMEM),
                  pl.BlockSpec((BLOCK, ATTN_WIDTH), lambda n: (n, 0))]
        + _window_specs(nb, KV_WIDTH, _Q_COLS) + _window_specs(nb, KV_WIDTH, _Q_COLS + 1),
        out_specs=[pl.BlockSpec((BLOCK, ATTN_WIDTH), lambda n: (n, 0)),
                   pl.BlockSpec((BLOCK, N_Q_HEADS), lambda n: (n, 0))],
        out_shape=[jax.ShapeDtypeStruct((l, ATTN_WIDTH), F32), jax.ShapeDtypeStruct((l, N_Q_HEADS), F32)],
        name="attn_fwd", compiler_params=_params(("parallel",)),
    )(sink, qkv, qkv, qkv, qkv, qkv, qkv, qkv)


def _attn_bwd(qkv, attn, dattn, lse, sink):
    l = qkv.shape[0]
    nb = l // BLOCK
    grp = N_Q_HEADS // N_KV_HEADS
    win = 3 * BLOCK

    def body(sink_ref, q_ref, k0, k1, k2, v0, v1, v2, o_ref, d_ref, l_ref, dq_ref, dkv_ref, dsink_ref, ring_ref):
        n = pl.program_id(0)

        @pl.when(n == 0)
        def _():
            dsink_ref[...] = jnp.zeros_like(dsink_ref)
            ring_ref[...] = jnp.zeros_like(ring_ref)

        @pl.when(n < nb)
        def _():
            first, last = n == 0, n == nb - 1
            cat = lambda a, b, c: jnp.concatenate([a[...], b[...], c[...]], axis=0)
            q, kw, vw = q_ref[...], cat(k0, k1, k2), cat(v0, v1, v2)
            dov = d_ref[...]
            prod = o_ref[...] * dov
            dob = dov.astype(BF16)
            lse = l_ref[...]
            row = lax.broadcasted_iota(jnp.int32, (grp * BLOCK, win), 0)
            col = lax.broadcasted_iota(jnp.int32, (grp * BLOCK, win), 1)
            valid = jnp.abs(col - BLOCK - (row & (BLOCK - 1))) <= WINDOW
            valid &= jnp.logical_not(first & (col < BLOCK))
            valid &= jnp.logical_not(last & (col >= 2 * BLOCK))

            dsink_parts, dks, dvs = [], [], []
            for hk in range(N_KV_HEADS):
                heads = range(hk * grp, (hk + 1) * grp)
                ksl = slice(hk * HEAD_DIM, (hk + 1) * HEAD_DIM)
                hsl = [slice(h * HEAD_DIM, (h + 1) * HEAD_DIM) for h in heads]
                stack = lambda parts: jnp.concatenate(parts, axis=0)
                qs = stack([q[:, s_] for s_ in hsl])
                dos = stack([dob[:, s_] for s_ in hsl])
                deltas = stack([jnp.sum(prod[:, s_], axis=1, keepdims=True) for s_ in hsl])
                lses = stack([lse[:, h:h + 1] for h in heads])
                kh, vh = kw[:, ksl], vw[:, ksl]
                s = jnp.where(valid, _dg(qs, kh, NT), _NEG)
                p = jnp.exp(s - lses)
                dp = _dg(dos, vh, NT)
                ds = (p * (dp - deltas)).astype(BF16)
                dq = _dg(ds, kh, NN) * SCORE_SCALE
                sink_rows = jnp.exp(_stacked_sink(sink_ref, heads) - lses) * deltas
                for g in range(grp):
                    dq_ref[:, hsl[g]] = dq[g * BLOCK:(g + 1) * BLOCK]
                    dsink_parts.append(jnp.sum(sink_rows[g * BLOCK:(g + 1) * BLOCK], axis=0, keepdims=True))
                dks.append(_dg(ds, qs, TN))
                dvs.append(_dg(p.astype(BF16), dos, TN))
            dsink_ref[...] -= jnp.concatenate(dsink_parts, axis=1)
            part = jnp.concatenate(dks + dvs, axis=1)
            ring_ref[(n + 2) % 3] += part[0:BLOCK]
            ring_ref[n % 3] += part[BLOCK:2 * BLOCK]
            ring_ref[(n + 1) % 3] = part[2 * BLOCK:]

        @pl.when(n >= 1)
        def _():
            dkv_ref[...] = ring_ref[(n + 2) % 3]

    centre = lambda n: jnp.minimum(n, nb - 1)
    window = lambda width, col: [
        pl.BlockSpec((BLOCK, width), lambda n: (jnp.maximum(centre(n) - 1, 0), col)),
        pl.BlockSpec((BLOCK, width), lambda n: (centre(n), col)),
        pl.BlockSpec((BLOCK, width), lambda n: (jnp.minimum(centre(n) + 1, nb - 1), col))]
    own = lambda width: pl.BlockSpec((BLOCK, width), lambda n: (centre(n), 0))
    return pl.pallas_call(
        body, grid=(nb + 1,),
        in_specs=[pl.BlockSpec(memory_space=pltpu.SMEM), own(ATTN_WIDTH)]
        + window(KV_WIDTH, _Q_COLS) + window(KV_WIDTH, _Q_COLS + 1)
        + [own(ATTN_WIDTH), own(ATTN_WIDTH), own(N_Q_HEADS)],
        out_specs=[own(ATTN_WIDTH), pl.BlockSpec((BLOCK, 2 * KV_WIDTH), lambda n: (jnp.maximum(n - 1, 0), 0)),
                   pl.BlockSpec((1, N_Q_HEADS), lambda n: (0, 0))],
        out_shape=[jax.ShapeDtypeStruct((l, ATTN_WIDTH), F32), jax.ShapeDtypeStruct((l, 2 * KV_WIDTH), F32),
                   jax.ShapeDtypeStruct((1, N_Q_HEADS), F32)],
        scratch_shapes=[pltpu.VMEM((3, BLOCK, 2 * KV_WIDTH), F32)],
        name="attn_bwd", compiler_params=_params(("arbitrary",)),
    )(sink, qkv, qkv, qkv, qkv, qkv, qkv, qkv, attn, dattn, lse)


def _ssm_disc(a_re, a_im, log_step, b_re, b_im):
    step = jnp.exp(log_step)[..., None]
    mag = jnp.exp(a_re * step)
    lb_re, lb_im = mag * jnp.cos(a_im * step), mag * jnp.sin(a_im * step)
    nr, ni = lb_re - 1.0, lb_im
    den = a_re * a_re + a_im * a_im
    f_re = ((nr * a_re + ni * a_im) / den)[..., None]
    f_im = ((ni * a_re - nr * a_im) / den)[..., None]
    return lb_re, lb_im, f_re * b_re - f_im * b_im, f_re * b_im + f_im * b_re


def _ssm_pack(lb_re, lb_im, bb_re, bb_im, c_re, c_im):
    eye = jnp.eye(SSM_CH // SSM_GROUP, dtype=F32)
    ng = SSM_CH // SSM_GROUP

    def diag_b(bb):
        t = bb.reshape(2, SSM_CB, ng, SSM_STATE, SSM_GROUP)
        return jnp.einsum('dkgpc,gh->dkgchp', t, eye).reshape(2, SSM_CB, SSM_CH, SSM_ST)

    def diag_c(cc):
        t = cc.reshape(2, SSM_CB, ng, SSM_GROUP, SSM_STATE)
        return jnp.einsum('dkgcp,gh->dkhpgc', t, eye).reshape(2, SSM_CB, SSM_ST, SSM_CH)

    bcat = jnp.concatenate([diag_b(bb_re), diag_b(bb_im)], axis=-1)
    ccat = jnp.concatenate([diag_c(c_re), -diag_c(c_im)], axis=-2)
    lam_re = lb_re.reshape(2, SSM_CB, 1, SSM_ST)
    lam_im = lb_im.reshape(2, SSM_CB, 1, SSM_ST)
    return bcat, ccat, lam_re, lam_im


def _ssm_unpack(dbcat, dccat, dlam_re, dlam_im):
    ng = SSM_CH // SSM_GROUP
    eye = jnp.eye(ng, dtype=F32)

    def undiag_b(t):
        t = t.reshape(2, SSM_CB, ng, SSM_GROUP, ng, SSM_STATE)
        return jnp.einsum('dkgchp,gh->dkgpc', t, eye).reshape(2, N_SSM_GROUPS, SSM_STATE, SSM_GROUP)

    def undiag_c(t):
        t = t.reshape(2, SSM_CB, ng, SSM_STATE, ng, SSM_GROUP)
        return jnp.einsum('dkhpgc,gh->dkgcp', t, eye).reshape(2, N_SSM_GROUPS, SSM_GROUP, SSM_STATE)

    dbb_re, dbb_im = undiag_b(dbcat[..., :SSM_ST]), undiag_b(dbcat[..., SSM_ST:])
    dc_re, dc_im = undiag_c(dccat[:, :, :SSM_ST]), -undiag_c(dccat[:, :, SSM_ST:])
    shape = (2, N_SSM_GROUPS, SSM_STATE)
    return dlam_re.reshape(shape), dlam_im.reshape(shape), dbb_re, dbb_im, dc_re, dc_im


def _to_segments(t):
    l, w = t.shape
    return t.reshape(N_SEG, l // N_SEG, w).transpose(1, 0, 2).reshape(l, w)


def _from_segments(t):
    l, w = t.shape
    return t.reshape(l // N_SEG, N_SEG, w).transpose(1, 0, 2).reshape(l, w)


SSM_RC = 256
SSM_JC = SSM_RC // N_SEG
_RE, _IM = pl.ds(0, SSM_ST), pl.ds(SSM_ST, SSM_ST)


def _cfma(ar, ai, xr, xi, br, bi):
    return ar * xr - ai * xi + br, ar * xi + ai * xr + bi


def _chunk_rows(ci, rev, nc):
    start = jnp.where(rev, (nc - 1 - ci) * SSM_RC, ci * SSM_RC)
    return pl.ds(pl.multiple_of(start, SSM_RC), SSM_RC)


def _scan_chunk(src, dst, ar, ai, rev, nj, ci, carry, prev_ref=None):
    def rows_of(staged, j, k):
        at = jnp.where(rev, SSM_JC - 1 - k, k) if staged else j
        return pl.ds(pl.multiple_of(at * N_SEG, N_SEG), N_SEG)

    for k in range(SSM_JC):
        jj = ci * SSM_JC + k
        j = jnp.where(rev, nj - 1 - jj, jj)
        rows = rows_of(src[1], j, k)
        nr, ni = _cfma(ar, ai, carry[0], carry[1], src[0][rows, _RE], src[0][rows, _IM])
        if dst is not None:
            rows = rows_of(dst[1], j, k)
            dst[0][rows, _RE] = nr
            dst[0][rows, _IM] = ni
        if prev_ref is None:
            carry = (nr, ni)
            continue
        jp = jnp.where(rev, j - 1, j + 1)
        if k == SSM_JC - 1:
            inside = jnp.where((jp >= 0) & (jp < nj), 1.0, 0.0)
            jp = jnp.clip(jp, 0, nj - 1)
        prow = pl.ds(pl.multiple_of(jp * N_SEG, N_SEG), N_SEG)
        xr, xi = prev_ref[prow, _RE], prev_ref[prow, _IM]
        sr, si = nr * xr + ni * xi, ni * xr - nr * xi
        if k == SSM_JC - 1:
            sr, si = inside * sr, inside * si
        carry = (nr, ni, carry[2] + sr, carry[3] + si)
    return carry


def _segment_inits(ar, ai, end_r, end_i, rev, nj):
    pr, pi = ar, ai
    for _ in range(int(math.log2(nj))):
        pr, pi = pr * pr - pi * pi, 2.0 * pr * pi
    seg = lax.broadcasted_iota(jnp.int32, end_r.shape, 0)
    zero = jnp.zeros_like(end_r)

    def chain(shift, keep):
        ir, ii = zero, zero
        for _ in range(N_SEG - 1):
            tr, ti = _cfma(pr, pi, ir, ii, end_r, end_i)
            ir = jnp.where(keep, pltpu.roll(tr, shift, 0), 0.0)
            ii = jnp.where(keep, pltpu.roll(ti, shift, 0), 0.0)
        return ir, ii

    up_r, up_i = chain(1, seg >= 1)
    dn_r, dn_i = chain(N_SEG - 1, seg <= N_SEG - 2)
    return jnp.where(rev, dn_r, up_r), jnp.where(rev, dn_i, up_i)


def _ssm_specs(l):
    act = pl.BlockSpec((l, SSM_CH), lambda k, d: (0, k))
    bmat = pl.BlockSpec((None, None, SSM_CH, 2 * SSM_ST), lambda k, d: (d, k, 0, 0))
    cmat = pl.BlockSpec((None, None, 2 * SSM_ST, SSM_CH), lambda k, d: (d, k, 0, 0))
    lam = pl.BlockSpec((None, None, 1, SSM_ST), lambda k, d: (d, k, 0, 0))
    return act, bmat, cmat, lam


def _ssm_fwd(u_seg, bcat, ccat, lam_re, lam_im):
    l = u_seg.shape[0]
    nj = l // N_SEG
    nc = l // SSM_RC

    def body(u_ref, b_ref, c_ref, lr_ref, li_ref, y_ref, ub_ref, keep_ref, xs_ref, stage0, stage1, keep_sem):
        k, d = pl.program_id(0), pl.program_id(1)
        rev = d == 1
        shape = (N_SEG, SSM_ST)
        ar, ai = jnp.broadcast_to(lr_ref[...], shape), jnp.broadcast_to(li_ref[...], shape)
        zero = jnp.zeros(shape, F32)

        def inputs(ci, stage):
            rows = _chunk_rows(ci, rev, nc)
            ub = u_ref[rows, :].astype(BF16)
            ub_ref[rows, :] = ub
            bu = _dg(ub, b_ref[...], NN)
            stage[...] = bu
            xs_ref[rows, :] = bu

        def first(stage, ci, carry):
            return _scan_chunk((stage, True), None, ar, ai, rev, nj, ci, carry)

        def first_pass(t, carry):
            inputs(2 * t + 1, stage1)
            carry = first(stage0, 2 * t, carry)
            inputs(2 * t + 2, stage0)
            return first(stage1, 2 * t + 1, carry)

        inputs(0, stage0)
        carry = lax.fori_loop(0, nc // 2 - 1, first_pass, (zero, zero))
        inputs(nc - 1, stage1)
        carry = first(stage0, nc - 2, carry)
        end_r, end_i = first(stage1, nc - 1, carry)
        init = _segment_inits(ar, ai, end_r, end_i, rev, nj)

        @pl.when(d == 0)
        def _():
            y_ref[...] = jnp.zeros_like(y_ref)

        def outputs(ci):
            rows = _chunk_rows(ci, rev, nc)
            y_ref[rows, :] += _dg(xs_ref[rows, :].astype(BF16), c_ref[...], NN)
            pltpu.make_async_copy(xs_ref.at[rows], keep_ref.at[d, k, rows], keep_sem).start()

        def second(ci, carry):
            return _scan_chunk((xs_ref, False), (xs_ref, False), ar, ai, rev, nj, ci, carry)

        def second_pass(ci, carry):
            outputs(ci - 1)
            return second(ci, carry)

        lax.fori_loop(1, nc, second_pass, second(0, init))
        outputs(nc - 1)
        pltpu.make_async_copy(xs_ref, keep_ref.at[d, k], keep_sem).wait()

    act, bmat, cmat, lam = _ssm_specs(l)
    return pl.pallas_call(
        body, grid=(SSM_CB, 2), in_specs=[act, bmat, cmat, lam, lam], out_specs=[act, act, ANY],
        out_shape=[jax.ShapeDtypeStruct((l, SSM_WIDTH), F32), jax.ShapeDtypeStruct((l, SSM_WIDTH), BF16),
                   jax.ShapeDtypeStruct((2, SSM_CB, l, 2 * SSM_ST), F32)],
        scratch_shapes=[pltpu.VMEM((l, 2 * SSM_ST), F32), pltpu.VMEM((SSM_RC, 2 * SSM_ST), F32),
                        pltpu.VMEM((SSM_RC, 2 * SSM_ST), F32), pltpu.SemaphoreType.DMA],
        name="ssm_fwd", compiler_params=_params(("parallel", "arbitrary"), vmem_mb=56),
    )(u_seg, bcat.astype(BF16), ccat.astype(BF16), lam_re, lam_im)


def _ssm_bwd(u_seg, dy_seg, states, bcat, ccat, lam_re, lam_im):
    l = u_seg.shape[0]
    nj = l // N_SEG
    nc = l // SSM_RC

    def body(u_ref, dy_ref, keep_ref, b_ref, c_ref, lr_ref, li_ref,
             du_ref, db_ref, dc_ref, dlr_ref, dli_ref, xs_ref, gs_ref, dyb_ref, stage0, stage1, keep_sem):
        k, d = pl.program_id(0), pl.program_id(1)
        rev = d == 1
        back = jnp.logical_not(rev)
        shape = (N_SEG, SSM_ST)
        ar, ai = jnp.broadcast_to(lr_ref[...], shape), -jnp.broadcast_to(li_ref[...], shape)
        zero = jnp.zeros(shape, F32)
        fetch = pltpu.make_async_copy(keep_ref.at[d, k], xs_ref, keep_sem)
        fetch.start()

        def inputs(ci, stage):
            rows = _chunk_rows(ci, back, nc)
            dyb = dy_ref[rows, :].astype(BF16)
            dyb_ref[rows, :] = dyb
            dx = _dg(dyb, c_ref[...], NT)
            stage[...] = dx
            gs_ref[rows, :] = dx

        def first(stage, ci, carry):
            return _scan_chunk((stage, True), None, ar, ai, back, nj, ci, carry)

        def first_pass(t, carry):
            inputs(2 * t + 1, stage1)
            carry = first(stage0, 2 * t, carry)
            inputs(2 * t + 2, stage0)
            return first(stage1, 2 * t + 1, carry)

        inputs(0, stage0)
        carry = lax.fori_loop(0, nc // 2 - 1, first_pass, (zero, zero))
        inputs(nc - 1, stage1)
        carry = first(stage0, nc - 2, carry)
        end_r, end_i = first(stage1, nc - 1, carry)
        init = _segment_inits(ar, ai, end_r, end_i, back, nj)
        fetch.wait()
        db_ref[...] = jnp.zeros_like(db_ref)
        dc_ref[...] = jnp.zeros_like(dc_ref)

        @pl.when(d == 0)
        def _():
            du_ref[...] = jnp.zeros_like(du_ref)

        def outputs(ci, stage):
            rows = _chunk_rows(ci, back, nc)
            g = stage[...].astype(BF16)
            dc_ref[...] += _dg(xs_ref[rows, :].astype(BF16), dyb_ref[rows, :], TN)
            db_ref[...] += _dg(u_ref[rows, :], g, TN)
            du_ref[rows, :] += _dg(g, b_ref[...], NT)

        def second(ci, stage, carry):
            return _scan_chunk((gs_ref, False), (stage, True), ar, ai, back, nj, ci, carry, prev_ref=xs_ref)

        def second_pass(t, carry):
            outputs(2 * t, stage0)
            carry = second(2 * t + 1, stage1, carry)
            outputs(2 * t + 1, stage1)
            return second(2 * t + 2, stage0, carry)

        carry = lax.fori_loop(0, nc // 2 - 1, second_pass, second(0, stage0, init + (zero, zero)))
        outputs(nc - 2, stage0)
        gr, gi, acc_r, acc_i = second(nc - 1, stage1, carry)
        outputs(nc - 1, stage1)

        seg = lax.broadcasted_iota(jnp.int32, shape, 0)
        jb = jnp.where(rev, nj - 1, 0)
        erow = pl.ds(pl.multiple_of((nj - 1 - jb) * N_SEG, N_SEG), N_SEG)

        def before(t):
            up = jnp.where(seg >= 1, pltpu.roll(t, 1, 0), 0.0)
            down = jnp.where(seg <= N_SEG - 2, pltpu.roll(t, N_SEG - 1, 0), 0.0)
            return jnp.where(rev, down, up)

        init_r, init_i = before(xs_ref[erow, _RE]), before(xs_ref[erow, _IM])
        acc_r = acc_r + gr * init_r + gi * init_i
        acc_i = acc_i + gi * init_r - gr * init_i
        dlr_ref[...] = jnp.sum(acc_r, axis=0, keepdims=True)
        dli_ref[...] = jnp.sum(acc_i, axis=0, keepdims=True)

    act, bmat, cmat, lam = _ssm_specs(l)
    return pl.pallas_call(
        body, grid=(SSM_CB, 2), in_specs=[act, act, ANY, bmat, cmat, lam, lam],
        out_specs=[act, bmat, cmat, lam, lam],
        out_shape=[jax.ShapeDtypeStruct((l, SSM_WIDTH), F32),
                   jax.ShapeDtypeStruct(bcat.shape, F32), jax.ShapeDtypeStruct(ccat.shape, F32),
                   jax.ShapeDtypeStruct(lam_re.shape, F32), jax.ShapeDtypeStruct(lam_im.shape, F32)],
        scratch_shapes=[pltpu.VMEM((l, 2 * SSM_ST), F32), pltpu.VMEM((l, 2 * SSM_ST), F32),
                        pltpu.VMEM((l, SSM_CH), BF16),
                        pltpu.VMEM((SSM_RC, 2 * SSM_ST), F32), pltpu.VMEM((SSM_RC, 2 * SSM_ST), F32),
                        pltpu.SemaphoreType.DMA],
        name="ssm_bwd", compiler_params=_params(("parallel", "arbitrary"), vmem_mb=58),
    )(u_seg, dy_seg, states, bcat.astype(BF16), ccat.astype(BF16), lam_re, lam_im)


def _glu_fwd(y_ssm, u, d_skip, w_glu):
    l, w = u.shape

    def body(y_ref, u_ref, d_ref, w_ref, pre_ref, s_ref, ys_ref):
        pre = y_ref[...] + d_ref[...] * u_ref[...]
        z = _gelu(pre)
        s = _dg(z.astype(BF16), w_ref[...], NN)
        pre_ref[...] = pre
        s_ref[...] = s
        ys_ref[...] = z * _sigmoid(s)

    row = pl.BlockSpec((TM_EW, w), lambda i: (i, 0))
    out = jax.ShapeDtypeStruct((l, w), F32)
    return pl.pallas_call(
        body, grid=(l // TM_EW,),
        in_specs=[row, row, pl.BlockSpec((1, w), lambda i: (0, 0)), pl.BlockSpec((w, w), lambda i: (0, 0))],
        out_specs=[row, row, row], out_shape=[out, out, out], name="glu_fwd",
        compiler_params=_params(("parallel",)),
    )(y_ssm, u, d_skip, w_glu)


def _glu_bwd(pre, s, dys, u, d_skip, w_glu):
    l, w = u.shape

    def body(pre_ref, s_ref, dys_ref, u_ref, d_ref, w_ref, dpre_ref, z_ref, ds_ref, dd_ref):
        pre, dys = pre_ref[...], dys_ref[...]
        z = _gelu(pre)
        sig = _sigmoid(s_ref[...])
        ds = (dys * z * sig * (1.0 - sig)).astype(BF16)
        dz = dys * sig + _dg(ds, w_ref[...], NT)
        dpre = dz * _gelu_grad(pre)
        dpre_ref[...] = dpre
        z_ref[...] = z.astype(BF16)
        ds_ref[...] = ds

        @pl.when(pl.program_id(0) == 0)
        def _():
            dd_ref[...] = jnp.zeros_like(dd_ref)

        dd_ref[...] += jnp.sum(dpre * u_ref[...], axis=0, keepdims=True)

    row = pl.BlockSpec((TM_EW, w), lambda i: (i, 0))
    vec = pl.BlockSpec((1, w), lambda i: (0, 0))
    return pl.pallas_call(
        body, grid=(l // TM_EW,),
        in_specs=[row, row, row, row, vec, pl.BlockSpec((w, w), lambda i: (0, 0))],
        out_specs=[row, row, row, vec],
        out_shape=[jax.ShapeDtypeStruct((l, w), F32), jax.ShapeDtypeStruct((l, w), BF16),
                   jax.ShapeDtypeStruct((l, w), BF16), jax.ShapeDtypeStruct((1, w), F32)],
        name="glu_bwd", compiler_params=_params(("arbitrary",)),
    )(pre, s, dys, u, d_skip, w_glu)


TM_CV = 1024
TC_CV = 256
TM_CF = 256
TC_CF = D_FF // 2
HALO = SUBLANES


def _conv_specs(l, col0, tm=TM_CV, tc=TC_CV):
    per = tm // HALO
    nh = l // HALO
    off = col0 // tc
    return [
        pl.BlockSpec((HALO, tc), lambda j, i: (jnp.maximum(i * per - 1, 0), j + off)),
        pl.BlockSpec((tm, tc), lambda j, i: (i, j + off)),
        pl.BlockSpec((HALO, tc), lambda j, i: (jnp.minimum((i + 1) * per, nh - 1), j + off)),
    ]


def _ext(prev_ref, mid_ref, next_ref, first, last):
    p = jnp.where(first, 0.0, prev_ref[...])
    n = jnp.where(last, 0.0, next_ref[...])
    return jnp.concatenate([p, mid_ref[...], n], axis=0)


def _shift_dn(t):
    return pltpu.roll(t, 1, 0)


def _shift_up(t):
    return pltpu.roll(t, t.shape[0] - 1, 0)


def _conv3(e, w_ref, b_ref):
    return w_ref[0:1, :] * _shift_dn(e) + w_ref[1:2, :] * e + w_ref[2:3, :] * _shift_up(e) + b_ref[...]


def _convffn_fwd(up_pre, conv_w, conv_b):
    l = up_pre.shape[0]
    tm, tc = TM_CF, TC_CF
    ni = l // tm
    wspec = lambda off: pl.BlockSpec((3, tc), lambda j, i: (0, j + off))
    bspec = lambda off: pl.BlockSpec((1, tc), lambda j, i: (0, j + off))
    voff = D_FF // tc

    def body(gp, gm, gn, vp, vm, vn, wg, bg, wv, bv, o_ref):
        i = pl.program_id(1)
        first, last = i == 0, i == ni - 1
        gate = _conv3(_ext(gp, gm, gn, first, last), wg, bg)[HALO:HALO + tm]
        val = _conv3(_ext(vp, vm, vn, first, last), wv, bv)[HALO:HALO + tm]
        o_ref[...] = (gate * _sigmoid(gate) * val).astype(BF16)

    return pl.pallas_call(
        body, grid=(D_FF // tc, ni),
        in_specs=_conv_specs(l, 0, tm, tc) + _conv_specs(l, D_FF, tm, tc)
        + [wspec(0), bspec(0), wspec(voff), bspec(voff)],
        out_specs=pl.BlockSpec((tm, tc), lambda j, i: (i, j)),
        out_shape=jax.ShapeDtypeStruct((l, D_FF), BF16), name="convffn_fwd",
        compiler_params=_params(("parallel", "parallel")),
    )(up_pre, up_pre, up_pre, up_pre, up_pre, up_pre, conv_w, conv_b, conv_w, conv_b)


HALO_B = 2 * SUBLANES


def _convffn_bwd(up_pre, dx2b, w_down, conv_w, conv_b):
    l = up_pre.shape[0]
    ni = l // TM_CV
    d = dx2b.shape[1]
    wspec = lambda off: pl.BlockSpec((3, TC_CV), lambda i, j: (0, j + off))
    bspec = lambda off: pl.BlockSpec((1, TC_CV), lambda i, j: (0, j + off))
    voff = D_FF // TC_CV
    swap = lambda spec: pl.BlockSpec(spec.block_shape, lambda i, j, f=spec.index_map: f(j, i))
    per, nh = TM_CV // HALO_B, l // HALO_B
    dx_specs = [pl.BlockSpec((HALO_B, d), lambda i, j: (jnp.maximum(i * per - 1, 0), 0)),
                pl.BlockSpec((TM_CV, d), lambda i, j: (i, 0)),
                pl.BlockSpec((HALO_B, d), lambda i, j: (jnp.minimum((i + 1) * per, nh - 1), 0))]

    def body(gp, gm, gn, vp, vm, vn, xp, xm, xn, wd, wg, bg, wv, bv, dup_ref, pg_ref, pv_ref):
        i = pl.program_id(0)
        first, last = i == 0, i == ni - 1
        ge, ve = _ext(gp, gm, gn, first, last), _ext(vp, vm, vn, first, last)
        zero = jnp.zeros((HALO_B, d), BF16)
        dx = jnp.concatenate([jnp.where(first, zero, xp[...]), xm[...], jnp.where(last, zero, xn[...])], axis=0)
        de = _dg(dx, wd[...], NT)[HALO_B - HALO:HALO_B + TM_CV + HALO]
        taps = [(_shift_dn(e), e, _shift_up(e)) for e in (ge, ve)]
        conv = lambda t, w_ref, b_ref: w_ref[0:1, :] * t[0] + w_ref[1:2, :] * t[1] + w_ref[2:3, :] * t[2] + b_ref[...]
        gate, val = conv(taps[0], wg, bg), conv(taps[1], wv, bv)
        sig = _sigmoid(gate)
        silu = gate * sig
        dgate = de * val * (sig + silu * (1.0 - sig))
        dval = de * silu
        mid = slice(HALO, HALO + TM_CV)
        rid = lax.broadcasted_iota(jnp.int32, (SUBLANES, TC_CV), 0)
        for half, (dup, tap, w_ref, p_ref) in enumerate(((dgate, taps[0], wg, pg_ref), (dval, taps[1], wv, pv_ref))):
            dpre = w_ref[0:1, :] * _shift_up(dup) + w_ref[1:2, :] * dup + w_ref[2:3, :] * _shift_dn(dup)
            dup_ref[half] = dpre[mid].astype(BF16)
            dm_ = dup[mid]
            sums = [jnp.sum(dm_ * t[mid], axis=0, keepdims=True) for t in tap]
            sums.append(jnp.sum(dm_, axis=0, keepdims=True))
            acc = jnp.zeros((SUBLANES, TC_CV), F32)
            for k, sk in enumerate(sums):
                acc = jnp.where(rid == k, sk, acc)
            p_ref[...] = acc

    par = pl.BlockSpec((None, SUBLANES, TC_CV), lambda i, j: (i, 0, j))
    dup, pg, pv = pl.pallas_call(
        body, grid=(ni, D_FF // TC_CV),
        in_specs=[swap(s) for s in _conv_specs(l, 0) + _conv_specs(l, D_FF)] + dx_specs
        + [pl.BlockSpec((TC_CV, d), lambda i, j: (j, 0)), wspec(0), bspec(0), wspec(voff), bspec(voff)],
        out_specs=[pl.BlockSpec((2, TM_CV, TC_CV), lambda i, j: (0, i, j)), par, par],
        out_shape=[jax.ShapeDtypeStruct((2, l, D_FF), BF16),
                   jax.ShapeDtypeStruct((ni, SUBLANES, D_FF), F32), jax.ShapeDtypeStruct((ni, SUBLANES, D_FF), F32)],
        name="convffn_bwd", compiler_params=_params(("parallel", "parallel")),
    )(up_pre, up_pre, up_pre, up_pre, up_pre, up_pre, dx2b, dx2b, dx2b, w_down, conv_w, conv_b, conv_w, conv_b)
    return dup, jnp.concatenate([jnp.sum(pg, axis=0), jnp.sum(pv, axis=0)], axis=1)


def _local_step(x, target, wb, sp, mixer_weights=None, late_weights=None, grads_ready=None,
                grads_next=None):
    l = x.shape[0]
    tabs = _rope_tables(l)
    disc = _ssm_disc(sp["a_re"], sp["a_im"], sp["log_step"], sp["b_re"], sp["b_im"])
    bcat, ccat, lam_re, lam_im = _ssm_pack(*disc, sp["c_re"], sp["c_im"])
    d_skip = sp["d_skip"].reshape(1, SSM_WIDTH)

    h, qkv, u = _rms_mm_rope(x, sp["norm_mix_g"], wb["w_in"], tabs, "mm_in")
    attn, lse = _attn_fwd(qkv, sp["sink"])
    y_seg, u_seg, states = _ssm_fwd(_to_segments(u), bcat, ccat, lam_re, lam_im)
    y_ssm = _from_segments(y_seg)
    if mixer_weights is not None:
        wb = dict(wb, **mixer_weights(attn))
    pre, s_glu, ys = _glu_fwd(y_ssm, u, d_skip, wb["w_glu"])
    mixed, x1, h2 = _mix_mm_res_rms(attn, ys, sp["norm_attn_g"], sp["norm_ssm_g"], wb["w_out"], x,
                                    sp["norm_ffn_g"], "mm_out")
    if late_weights is not None:
        wb = dict(wb, **late_weights(h2))
    up_pre = _mm_nn_cols(h2, wb["w_up"], min(l, 1024), "mm_up")
    conv_w = wb["conv_w"]
    act = _convffn_fwd(up_pre, conv_w, sp["conv_b"])
    loss, dx2, dx2b, d_final_g = _mm_res_loss(act, wb["w_down"], x1, sp["norm_final_g"].reshape(1, D_MODEL), target)

    g = {"norm_final_g": d_final_g.reshape(D_MODEL)}
    g["w_down"] = _mm_tn(act, dx2b, D_FF // 2, 512, "mm_down_dw")
    dup_pre, conv_par = _convffn_bwd(up_pre, dx2b, wb["w_down"], conv_w, sp["conv_b"])
    g["conv_w"], g["conv_b"] = conv_par[0:3], conv_par[3:4]
    g["w_up"] = _mm_tn_cols(h2, dup_pre, wb["w_up"].shape[0], 512, "mm_up_dw")
    dx1, dx1b, g["norm_ffn_g"] = _mm_cols_rms_bwd(dup_pre, wb["w_up"], x1, sp["norm_ffn_g"], dx2, "mm_up_dx")
    g["w_out"] = _mm_tn(mixed, dx1b, 1024, 1024, "mm_out_dw")
    zero = grads_ready(g["w_up"], g["w_down"], g["w_out"]) if grads_ready is not None else 0.0
    dattn, dys, g["norm_attn_g"], g["norm_ssm_g"] = _mm_mix_bwd(
        dx1b, wb["w_out"], attn, ys, sp["norm_attn_g"] + zero, sp["norm_ssm_g"], "mm_out_dx")
    dpre, zb, dsb, dd = _glu_bwd(pre, s_glu, dys, u, d_skip, wb["w_glu"])
    g["d_skip"] = dd.reshape(N_SSM_GROUPS, SSM_GROUP)
    g["w_glu"] = _mm_tn(zb, dsb, 512, 512, "mm_glu_dw")
    zero = grads_next(g["w_glu"]) if grads_next is not None else 0.0
    du_seg, dbcat, dccat, dlam_re, dlam_im = _ssm_bwd(u_seg, _to_segments(dpre), states, bcat, ccat,
                                                      lam_re + zero, lam_im)
    dlb_re, dlb_im, dbb_re, dbb_im, g["c_re"], g["c_im"] = _ssm_unpack(dbcat, dccat, dlam_re, dlam_im)
    _, disc_vjp = jax.vjp(_ssm_disc, sp["a_re"], sp["a_im"], sp["log_step"], sp["b_re"], sp["b_im"])
    g["a_re"], g["a_im"], g["log_step"], g["b_re"], g["b_im"] = disc_vjp((dlb_re, dlb_im, dbb_re, dbb_im))
    dq, dkv, g["sink"] = _attn_bwd(qkv, attn, dattn, lse, sp["sink"])
    dproj = _rope_bwd(dq, dkv, _from_segments(du_seg), dpre, d_skip, tabs)
    g["w_in"] = _mm_tn(dproj, h, IN_WIDTH // 5, D_MODEL, "mm_in_dw")
    grad_x, _, g["norm_mix_g"] = _mm_nn_rms_bwd(dproj, wb["w_in"], x, sp["norm_mix_g"], dx1, "mm_in_dx")
    return loss, grad_x, g


MESH = pl.DeviceIdType.MESH
ANY = pl.BlockSpec(memory_space=pl.ANY)


def _place():
    x, y, c = lax.axis_index("x"), lax.axis_index("y"), lax.axis_index("c")
    chips = [(1 - x, y), (x, 1 - y), (1 - x, 1 - y)]
    return x, y, c, chips


def _chip_index(px, py):
    return 2 * px + py


CHUNK_BYTES = 256 * 1024
MAX_CHUNKS = 16


def _row_chunks(rows, row_bytes, align):
    n = max(1, min(MAX_CHUNKS, (rows * row_bytes) // CHUNK_BYTES))
    per = -(-rows // n)
    per = -(-per // align) * align
    return [(r0, min(per, rows - r0)) for r0 in range(0, rows, per)]


def _align_of(dtype):
    return SUBLANES * 4 // jnp.dtype(dtype).itemsize


def _remote(src, dst, send_sem, recv_sem, to):
    return pltpu.make_async_remote_copy(src_ref=src, dst_ref=dst, send_sem=send_sem, recv_sem=recv_sem,
                                        device_id=to, device_id_type=MESH)


CAST_ROWS = 64


def _gather_weights(shards, dtypes):
    nw = len(shards)

    def body(*refs):
        w_refs, o_refs = refs[:nw], refs[nw:2 * nw]
        send_sems, recv_sems, in_sems, out_sems = refs[2 * nw:2 * nw + 4]
        raw, cast = refs[2 * nw + 4:3 * nw + 4], refs[3 * nw + 4:]
        x, y, c, chips = _place()
        mine = _chip_index(x, y)
        sibling = (x, y, 1 - c)

        def rows_of(ref, chip, r0, nr):
            return ref.at[chip, pl.ds(r0, nr), :]

        def copy(wi, k, src, dst, to):
            return _remote(src, dst, send_sems.at[wi, k], recv_sems.at[wi, k], to)

        geo = []
        for wi in range(nw):
            rows, cols = w_refs[wi].shape
            row_bytes = cols * jnp.dtype(dtypes[wi]).itemsize
            geo.append((rows // 2, _row_chunks(rows // 2, row_bytes, _align_of(dtypes[wi]))))

        stage_in = [pltpu.make_async_copy(w_refs[wi], raw[wi], in_sems.at[wi]) for wi in range(nw)]
        for cp in stage_in:
            cp.start()
        staged = [raw[wi] if dtypes[wi] == w_refs[wi].dtype else cast[wi] for wi in range(nw)]
        stage_out = []
        for wi in range(nw):
            stage_in[wi].wait()
            if staged[wi] is not raw[wi]:
                def cast_rows(i, _, wi=wi):
                    rows = pl.ds(pl.multiple_of(i * CAST_ROWS, CAST_ROWS), CAST_ROWS)
                    cast[wi][rows, :] = raw[wi][rows, :].astype(dtypes[wi])
                    return 0

                lax.fori_loop(0, w_refs[wi].shape[0] // CAST_ROWS, cast_rows, 0)
            cp = pltpu.make_async_copy(staged[wi], o_refs[wi].at[mine], out_sems.at[wi])
            cp.start()
            stage_out.append(cp)

        for wi in range(nw):
            hr, half_chunks = geo[wi]
            for j, chip in enumerate(chips):
                for r0, nr in half_chunks:
                    copy(wi, j, staged[wi].at[pl.ds(c * hr + r0, nr), :],
                         rows_of(o_refs[wi], mine, c * hr + r0, nr), (*chip, c)).start()
        for wi in range(nw):
            hr, half_chunks = geo[wi]
            for j, chip in enumerate(chips):
                got = rows_of(o_refs[wi], _chip_index(*chip), c * hr, hr)
                copy(wi, j, got, got, (*chip, c)).wait_recv()
                for r0, nr in half_chunks:
                    piece = rows_of(o_refs[wi], _chip_index(*chip), c * hr + r0, nr)
                    copy(wi, 3 + j, piece, piece, sibling).start()
        for wi in range(nw):
            hr = geo[wi][0]
            for j, chip in enumerate(chips):
                got = rows_of(o_refs[wi], _chip_index(*chip), (1 - c) * hr, hr)
                copy(wi, 3 + j, got, got, sibling).wait_recv()
        for wi in range(nw):
            hr = geo[wi][0]
            sent = rows_of(o_refs[wi], mine, c * hr, hr)
            for k in range(6):
                copy(wi, k, sent, sent, sibling).wait_send()
            stage_out[wi].wait()

    return pl.pallas_call(
        body, in_specs=[ANY] * nw, out_specs=[ANY] * nw,
        out_shape=[jax.ShapeDtypeStruct((4, *s.shape), t) for s, t in zip(shards, dtypes)],
        scratch_shapes=[pltpu.SemaphoreType.DMA((nw, 6)), pltpu.SemaphoreType.DMA((nw, 6)),
                        pltpu.SemaphoreType.DMA((nw,)), pltpu.SemaphoreType.DMA((nw,))]
        + [pltpu.VMEM(s.shape, s.dtype) for s in shards] + [pltpu.VMEM(s.shape, t) for s, t in zip(shards, dtypes)],
        name="gather_weights", compiler_params=_params(vmem_mb=40),
    )(*shards)


HBM = pl.BlockSpec(memory_space=pltpu.HBM)
SEM = pl.BlockSpec(memory_space=pltpu.SEMAPHORE)
EFFECT = pltpu.SideEffectType.DATAFLOW_SIDE_EFFECTING


def _cast_place(w, place, dtype, after, name):
    rows, cols = w.shape
    tr = _row_tile(rows, cols, _align_of(dtype))

    def body(p_ref, w_ref, after_ref, o_ref):
        del p_ref, after_ref
        o_ref[...] = w_ref[...].astype(dtype)

    grid_spec = pltpu.PrefetchScalarGridSpec(
        num_scalar_prefetch=1, grid=(rows // tr,),
        in_specs=[pl.BlockSpec((tr, cols), lambda i, p: (i, 0)), ANY],
        out_specs=pl.BlockSpec((None, tr, cols), lambda i, p: (p[1], i, 0)))
    return pl.pallas_call(body, grid_spec=grid_spec, out_shape=jax.ShapeDtypeStruct((4, rows, cols), dtype),
                          name=name, compiler_params=_params(("parallel",)))(place, w, after)


def _split_start(name, arrays, n_pairs, issue):
    n = len(arrays)

    def body(*refs):
        issue(refs[:n], refs[n:n + n_pairs], refs[n + n_pairs:n + 2 * n_pairs])
        token = refs[2 * n + 2 * n_pairs]
        token[...] = jnp.zeros_like(token)

    dma = pltpu.SemaphoreType.DMA(())
    outs = pl.pallas_call(
        body, name=name,
        out_shape=[dma] * (2 * n_pairs) + [pltpu.HBM(t.shape, t.dtype) for t in arrays]
        + [jax.ShapeDtypeStruct((SUBLANES, LANES), F32)],
        in_specs=[HBM] * n, out_specs=[SEM] * (2 * n_pairs) + [HBM] * n + [pl.BlockSpec(memory_space=pltpu.VMEM)],
        input_output_aliases={a: 2 * n_pairs + a for a in range(n)},
        compiler_params=pltpu.CompilerParams(has_side_effects=EFFECT),
    )(*[pltpu.with_memory_space_constraint(t, pltpu.HBM) for t in arrays])
    return outs[:n_pairs], outs[n_pairs:2 * n_pairs], outs[2 * n_pairs:2 * n_pairs + n], outs[-1]


def _split_wait(name, send_sems, recv_sems, flying, sizes, after):
    n, n_pairs = len(flying), len(send_sems)

    def body(*refs):
        x, y, c, _ = _place()
        for k, ref in enumerate(sizes(refs[:n])):
            cp = _remote(ref, ref, refs[n + k], refs[n + n_pairs + k], (x, y, 1 - c))
            cp.wait_send()
            cp.wait_recv()

    return pl.pallas_call(
        body, name=name, out_shape=[pltpu.HBM(t.shape, t.dtype) for t in flying],
        in_specs=[HBM] * n + [SEM] * (2 * n_pairs) + [ANY], out_specs=[HBM] * n,
        input_output_aliases={a: a for a in range(n)},
        compiler_params=pltpu.CompilerParams(has_side_effects=EFFECT),
    )(*flying, *send_sems, *recv_sems, after)


def _spread_start(lands, name):
    def issue(land_refs, send_sems, recv_sems):
        x, y, c, chips = _place()
        mine = _chip_index(x, y)
        for a, land in enumerate(land_refs):
            _, rows, cols = land.shape
            hr = rows // 2
            row_bytes = cols * jnp.dtype(land.dtype).itemsize
            for r0, nr in _row_chunks(hr, row_bytes, _align_of(land.dtype)):
                piece = land.at[mine, pl.ds(c * hr + r0, nr), :]
                for chip in chips:
                    for core in (0, 1):
                        _remote(piece, piece, send_sems[a], recv_sems[a], (*chip, core)).start()

    return _split_start(name, lands, len(lands), issue)


def _spread_wait(send_sems, recv_sems, flying, after, name):
    return _split_wait(name, send_sems, recv_sems, flying, lambda refs: [r.at[pl.ds(0, 3)] for r in refs], after)


def _pair_start(grads):
    n = len(grads)
    zones = [lax.empty((4, g.shape[1] // 2, g.shape[2]), F32) for g in grads]

    def issue(refs, send_sems, recv_sems):
        x, y, c, _ = _place()
        for a in range(n):
            g_ref, z_ref = refs[a], refs[n + a]
            _, rows, cols = g_ref.shape
            hr = rows // 2
            for k in range(4):
                for r0, nr in _row_chunks(hr, cols * 4, SUBLANES):
                    _remote(g_ref.at[k, pl.ds((1 - c) * hr + r0, nr), :], z_ref.at[k, pl.ds(r0, nr), :],
                            send_sems[a], recv_sems[a], (x, y, 1 - c)).start()

    return _split_start("pair_start", list(grads) + zones, n, issue)


def _pair_wait(send_sems, recv_sems, flying, after):
    n = len(flying) // 2
    out = _split_wait("pair_wait", send_sems, recv_sems, flying, lambda refs: list(refs[n:]), after)
    return out[:n], out[n:]


def _chip_start(sums):
    n = len(sums)
    zones = [lax.empty((3, *s.shape[1:]), s.dtype) for s in sums]

    def issue(refs, send_sems, recv_sems):
        x, y, c, chips = _place()
        for a in range(n):
            s_ref, z_ref = refs[a], refs[n + a]
            _, rows, cols = s_ref.shape
            row_bytes = cols * jnp.dtype(s_ref.dtype).itemsize
            for r0, nr in _row_chunks(rows, row_bytes, _align_of(s_ref.dtype)):
                for j, chip in enumerate(chips):
                    _remote(s_ref.at[_chip_index(*chip), pl.ds(r0, nr), :], z_ref.at[j, pl.ds(r0, nr), :],
                            send_sems[a], recv_sems[a], (*chip, c)).start()

    return _split_start("chip_start", list(sums) + zones, n, issue)


def _chip_wait(send_sems, recv_sems, flying, after):
    n = len(flying) // 2
    return _split_wait("chip_wait", send_sems, recv_sems, flying, lambda refs: list(refs[n:]), after)[n:]


def _pair_exchange(grads):
    na = len(grads)

    def body(*refs):
        g_refs, o_refs = refs[:na], refs[na:2 * na]
        send_sems, recv_sems = refs[2 * na:]
        x, y, c, _ = _place()
        sibling = (x, y, 1 - c)
        for ai in range(na):
            _, rows, cols = g_refs[ai].shape
            hr = rows // 2
            for k in range(4):
                for r0, nr in _row_chunks(hr, cols * 4, SUBLANES):
                    _remote(g_refs[ai].at[k, pl.ds((1 - c) * hr + r0, nr), :], o_refs[ai].at[k, pl.ds(r0, nr), :],
                            send_sems.at[ai], recv_sems.at[ai], sibling).start()
        for ai in range(na):
            _remote(o_refs[ai], o_refs[ai], send_sems.at[ai], recv_sems.at[ai], sibling).wait()

    return pl.pallas_call(
        body, in_specs=[ANY] * na, out_specs=[ANY] * na,
        out_shape=[jax.ShapeDtypeStruct((4, g.shape[1] // 2, g.shape[2]), F32) for g in grads],
        scratch_shapes=[pltpu.SemaphoreType.DMA((na,)), pltpu.SemaphoreType.DMA((na,))],
        name="pair_exchange",
    )(*grads)


def _row_tile(rows, cols, align, elems=256 * 1024):
    best = align
    for cand in range(align, rows + 1, align):
        if rows % cand == 0 and cand * cols <= elems:
            best = cand
    return best


def _pair_sum(g, got, place, transit, name):
    _, rows, cols = g.shape
    hr = rows // 2
    tr = _row_tile(hr, cols, _align_of(transit), 512 * 1024)
    nt = hr // tr

    def body(p_ref, g_ref, r_ref, s_ref, own_ref):
        total = g_ref[...] + r_ref[...]
        s_ref[...] = total.astype(transit)

        @pl.when(pl.program_id(1) == p_ref[1])
        def _():
            own_ref[...] = total

    grid_spec = pltpu.PrefetchScalarGridSpec(
        num_scalar_prefetch=1, grid=(nt, 4),
        in_specs=[pl.BlockSpec((None, tr, cols), lambda i, k, p: (k, p[0] * nt + i, 0)),
                  pl.BlockSpec((None, tr, cols), lambda i, k, p: (k, i, 0))],
        out_specs=[pl.BlockSpec((None, tr, cols), lambda i, k, p: (k, i, 0)),
                   pl.BlockSpec((tr, cols), lambda i, k, p: (i, 0))])
    return pl.pallas_call(
        body, grid_spec=grid_spec,
        out_shape=[jax.ShapeDtypeStruct((4, hr, cols), transit), jax.ShapeDtypeStruct((hr, cols), F32)],
        name=name, compiler_params=_params(("parallel", "arbitrary")),
    )(place, g, got)


def _chip_exchange(sums):
    na = len(sums)

    def body(*refs):
        s_refs, o_refs = refs[:na], refs[na:2 * na]
        send_sems, recv_sems = refs[2 * na:]
        x, y, c, chips = _place()
        for ai in range(na):
            _, rows, cols = s_refs[ai].shape
            row_bytes = cols * jnp.dtype(s_refs[ai].dtype).itemsize
            for r0, nr in _row_chunks(rows, row_bytes, _align_of(s_refs[ai].dtype)):
                for j, chip in enumerate(chips):
                    _remote(s_refs[ai].at[_chip_index(*chip), pl.ds(r0, nr), :], o_refs[ai].at[j, pl.ds(r0, nr), :],
                            send_sems.at[ai, j], recv_sems.at[ai, j], (*chip, c)).start()
        for ai in range(na):
            for j, chip in enumerate(chips):
                _remote(o_refs[ai].at[j], o_refs[ai].at[j], send_sems.at[ai, j], recv_sems.at[ai, j],
                        (*chip, c)).wait()

    return pl.pallas_call(
        body, in_specs=[ANY] * na, out_specs=[ANY] * na,
        out_shape=[jax.ShapeDtypeStruct((3, *s.shape[1:]), s.dtype) for s in sums],
        scratch_shapes=[pltpu.SemaphoreType.DMA((na, 3)), pltpu.SemaphoreType.DMA((na, 3))],
        name="chip_exchange",
    )(*sums)


def _chip_sum(own, landed, name):
    hr, cols = own.shape
    tr = _row_tile(hr, cols, _align_of(landed.dtype))

    def body(o_ref, l_ref, f_ref):
        acc = o_ref[...]
        for j in range(3):
            acc = acc + l_ref[j].astype(F32)
        f_ref[...] = acc

    return pl.pallas_call(
        body, grid=(hr // tr,),
        in_specs=[pl.BlockSpec((tr, cols), lambda i: (i, 0)), pl.BlockSpec((3, tr, cols), lambda i: (0, i, 0))],
        out_specs=pl.BlockSpec((tr, cols), lambda i: (i, 0)),
        out_shape=jax.ShapeDtypeStruct((hr, cols), F32), name=name,
        compiler_params=_params(("parallel",)),
    )(own, landed)


def _final_exchange(halves, small):
    nh = len(halves)

    def body(*refs):
        h_refs, s_ref = refs[:nh], refs[nh]
        o_refs, so_ref = refs[nh + 1:2 * nh + 1], refs[2 * nh + 1]
        send_sems, recv_sems, local_sem, ssend_sems, srecv_sems = refs[2 * nh + 2:]
        x, y, c, _ = _place()
        me = 4 * x + 2 * y + c
        sibling = (x, y, 1 - c)
        for hi in range(nh):
            hr, cols = h_refs[hi].shape
            for r0, nr in _row_chunks(hr, cols * 4, SUBLANES):
                _remote(h_refs[hi].at[pl.ds(r0, nr), :], o_refs[hi].at[pl.ds(r0, nr), :],
                        send_sems.at[hi], recv_sems.at[hi], sibling).start()
        small_cps = [pltpu.make_async_copy(s_ref, so_ref.at[me], local_sem)]
        for r in range(1, 8):
            fx, fy, fc = (r >> 2) & 1, (r >> 1) & 1, r & 1
            peer = (1 - x if fx else x, 1 - y if fy else y, 1 - c if fc else c)
            small_cps.append(_remote(s_ref, so_ref.at[me], ssend_sems.at[r - 1], srecv_sems.at[r - 1], peer))
        for cp in small_cps:
            cp.start()
        for hi in range(nh):
            _remote(h_refs[hi], o_refs[hi], send_sems.at[hi], recv_sems.at[hi], sibling).wait()
        for cp in small_cps:
            cp.wait()

    return pl.pallas_call(
        body, in_specs=[ANY] * (nh + 1), out_specs=[ANY] * (nh + 1),
        out_shape=[jax.ShapeDtypeStruct(h.shape, F32) for h in halves]
        + [jax.ShapeDtypeStruct((8, *small.shape), F32)],
        scratch_shapes=[pltpu.SemaphoreType.DMA((nh,)), pltpu.SemaphoreType.DMA((nh,)),
                        pltpu.SemaphoreType.DMA, pltpu.SemaphoreType.DMA((7,)), pltpu.SemaphoreType.DMA((7,))],
        name="final_exchange",
    )(*halves, small)


def _adamw_halves(w, own, other, m, v, place, name):
    r, c = w.shape
    hr = r // 2
    tr = _row_tile(hr, c, SUBLANES, 384 * 1024)
    nt = hr // tr
    c1 = 1.0 - ADAM_B1 ** ADAM_STEP
    c2 = 1.0 - ADAM_B2 ** ADAM_STEP

    def body(p_ref, w_ref, own_ref, other_ref, m_ref, v_ref, g_ref, d_ref, nm_ref, nv_ref):
        mine = pl.program_id(0) // nt == p_ref[0]
        gv = jnp.where(mine, own_ref[...], other_ref[...])
        nm = ADAM_B1 * m_ref[...] + (1.0 - ADAM_B1) * gv
        nv = ADAM_B2 * v_ref[...] + (1.0 - ADAM_B2) * (gv * gv)
        g_ref[...] = gv
        d_ref[...] = -ADAM_LR * ((nm / c1) / (jnp.sqrt(nv / c2) + ADAM_EPS) + ADAM_WD * w_ref[...])
        nm_ref[...] = nm
        nv_ref[...] = nv

    full = pl.BlockSpec((tr, c), lambda i, p: (i, 0))
    own_half = pl.BlockSpec((tr, c), lambda i, p: (jnp.where(i // nt == p[0], i % nt, 0), 0))
    other_half = pl.BlockSpec((tr, c), lambda i, p: (jnp.where(i // nt == p[0], 0, i % nt), 0))
    out = jax.ShapeDtypeStruct((r, c), F32)
    grid_spec = pltpu.PrefetchScalarGridSpec(num_scalar_prefetch=1, grid=(2 * nt,),
                                             in_specs=[full, own_half, other_half, full, full],
                                             out_specs=[full] * 4)
    return pl.pallas_call(body, grid_spec=grid_spec, out_shape=[out] * 4, name=name,
                          compiler_params=_params(("parallel",)))(place, w, own, other, m, v)


def _adamw_many(ws, gs, ms, vs, name):
    n = len(ws)
    c1 = 1.0 - ADAM_B1 ** ADAM_STEP
    c2 = 1.0 - ADAM_B2 ** ADAM_STEP

    def body(*refs):
        w_refs, g_refs, m_refs, v_refs = (refs[k * n:(k + 1) * n] for k in range(4))
        d_refs, nm_refs, nv_refs = (refs[(4 + k) * n:(5 + k) * n] for k in range(3))
        for i in range(n):
            gv = g_refs[i][...]
            nm = ADAM_B1 * m_refs[i][...] + (1.0 - ADAM_B1) * gv
            nv = ADAM_B2 * v_refs[i][...] + (1.0 - ADAM_B2) * (gv * gv)
            d_refs[i][...] = -ADAM_LR * ((nm / c1) / (jnp.sqrt(nv / c2) + ADAM_EPS) + ADAM_WD * w_refs[i][...])
            nm_refs[i][...] = nm
            nv_refs[i][...] = nv

    vmem = pl.BlockSpec(memory_space=pltpu.VMEM)
    shapes = [jax.ShapeDtypeStruct(t.shape, F32) for t in ws]
    outs = pl.pallas_call(body, in_specs=[vmem] * (4 * n), out_specs=[vmem] * (3 * n), out_shape=shapes * 3,
                          name=name, compiler_params=_params(vmem_mb=56))(*ws, *gs, *ms, *vs)
    return outs[:n], outs[n:2 * n], outs[2 * n:]


BIG = ("w_in", "w_glu", "w_out", "w_up", "w_down")
WEIGHTS = ("norm_mix_g", "w_in", "a_re", "a_im", "log_step", "b_re", "b_im", "c_re", "c_im", "d_skip", "w_glu",
           "sink", "norm_attn_g", "norm_ssm_g", "w_out", "norm_ffn_g", "w_up", "conv_w", "conv_b", "w_down",
           "norm_final_g")
SMALL = ("norm_mix_g", "a_re", "a_im", "log_step", "b_re", "b_im", "c_re", "c_im", "d_skip", "sink",
         "norm_attn_g", "norm_ssm_g", "norm_ffn_g", "conv_w", "conv_b", "norm_final_g")
SMALL_ROWS = 48
N_DEV = 8


def _tile_rows(size):
    return -(-size // (SUBLANES * D_MODEL)) * SUBLANES


def _by_owner(name, g):
    if name == "w_up":
        return g
    return g.reshape(4, g.shape[0] // 4, g.shape[1])


def _view(name, t):
    if name == "w_in":
        return jnp.swapaxes(t[0], 0, 1)
    if name in ("b_re", "b_im"):
        return jnp.swapaxes(t, -1, -2)
    return t


def _unview(name, t):
    if name == "w_in":
        return jnp.swapaxes(t, 0, 1)[None]
    if name in ("b_re", "b_im"):
        return jnp.swapaxes(t, -1, -2)
    return t


def kernel(x, norm_mix_g, w_in, a_re, a_im, log_step, b_re, b_im, c_re, c_im, d_skip, w_glu, sink, norm_attn_g, norm_ssm_g, w_out, norm_ffn_g, w_up, conv_w, conv_b, w_down, norm_final_g, loss_target, m_norm_mix_g, m_w_in, m_a_re, m_a_im, m_log_step, m_b_re, m_b_im, m_c_re, m_c_im, m_d_skip, m_w_glu, m_sink, m_norm_attn_g, m_norm_ssm_g, m_w_out, m_norm_ffn_g, m_w_up, m_conv_w, m_conv_b, m_w_down, m_norm_final_g, v_norm_mix_g, v_w_in, v_a_re, v_a_im, v_log_step, v_b_re, v_b_im, v_c_re, v_c_im, v_d_skip, v_w_glu, v_sink, v_norm_attn_g, v_norm_ssm_g, v_w_out, v_norm_ffn_g, v_w_up, v_conv_w, v_conv_b, v_w_down, v_norm_final_g):
    given = dict(locals())
    w = {n: given[n] for n in WEIGHTS}
    m = {n: given["m_" + n] for n in WEIGHTS}
    v = {n: given["v_" + n] for n in WEIGHTS}
    xy = 2 * lax.axis_index("x") + lax.axis_index("y")

    core = lax.axis_index("c")
    place = jnp.stack([core, xy]).astype(jnp.int32)

    conv_rows = jnp.pad(w["conv_w"][0], ((0, 2 * SUBLANES - 3), (0, 0)))
    rows = lambda t: t.reshape(4 * t.shape[1], t.shape[2])
    (w_in_all,) = _gather_weights([_view("w_in", w["w_in"])], [BF16])
    wb = {"w_in": rows(w_in_all)}
    mixer = [_cast_place(w[n][0], place, BF16, w_in_all, "cast_" + n) for n in ("w_glu", "w_out")]
    mixer.append(_cast_place(conv_rows, place, F32, w_in_all, "cast_conv_w"))
    *mixer_flight, mixer_token = _spread_start(mixer, "spread_mixer_start")
    late = ("w_up", "w_down")
    *late_flight, token = _spread_start(
        [_cast_place(w[n][0], place, BF16, mixer_token, "cast_" + n) for n in late], "spread_ffn_start")

    def mixer_weights(after):
        w_glu4, w_out4, conv4 = _spread_wait(*mixer_flight, after, "spread_mixer_wait")
        return {"w_glu": rows(w_glu4), "w_out": rows(w_out4),
                "conv_w": conv4[:, :3].transpose(1, 0, 2).reshape(3, 2 * D_FF)}

    def late_weights(after):
        w_up4, w_down4 = _spread_wait(*late_flight, after, "spread_ffn_wait")
        return {"w_up": w_up4, "w_down": rows(w_down4)}

    sp = {n: w[n][0] for n in ("a_re", "a_im", "log_step", "b_re", "b_im", "c_re", "c_im", "d_skip",
                               "norm_mix_g", "norm_attn_g", "norm_ssm_g", "norm_ffn_g", "sink", "conv_b")}
    for n in ("norm_mix_g", "norm_attn_g", "norm_ssm_g", "norm_ffn_g", "sink", "conv_b"):
        sp[n] = sp[n].reshape(1, -1)
    sp["norm_mix_g"] = sp["norm_mix_g"] + token[:1, :1]
    sp["norm_final_g"] = w["norm_final_g"]
    early, tail = late + ("w_out",), ("w_in", "w_glu")
    flight = {}

    def grads_ready(dw_up, dw_down, dw_out):
        *flight["pair"], token = _pair_start([dw_up, _by_owner("w_down", dw_down), _by_owner("w_out", dw_out)])
        return token[:1, :1]

    def grads_next(after):
        mine, got = _pair_wait(*flight["pair"], after)
        sums, flight["own"] = zip(*[_pair_sum(a, b, place, BF16, "pair_sum_" + n) for n, a, b in zip(early, mine, got)])
        *flight["chip"], token = _chip_start(list(sums))
        return token[:1, :1]

    loss, grad_x, g = _local_step(x[0], loss_target[0], wb, sp, mixer_weights, late_weights, grads_ready,
                                  grads_next)

    def as_rows(t):
        rows = _tile_rows(t.size)
        return jnp.pad(t.reshape(-1), (0, rows * D_MODEL - t.size)).reshape(rows, D_MODEL)

    pieces = [as_rows(g[n]) for n in SMALL] + [as_rows(loss)]
    spare = N_DEV * SMALL_ROWS - sum(p.shape[0] for p in pieces)
    small = jnp.concatenate(pieces + [jnp.zeros((spare, D_MODEL), F32)]).reshape(4, 2 * SMALL_ROWS, D_MODEL)
    by_owner = [_by_owner(n, g[n]) for n in tail] + [small]
    got = _pair_exchange(by_owner)
    transit = [BF16] * len(tail) + [F32]
    chip_sums, own_sums = zip(*[_pair_sum(a, b, place, t, "pair_sum_" + n)
                                for n, a, b, t in zip(tail + ("small",), by_owner, got, transit)])
    landed = _chip_exchange(list(chip_sums))
    halves = {n: _chip_sum(o, t, "chip_sum_" + n) for n, o, t in zip(tail + ("small",), own_sums, landed)}
    early_landed = _chip_wait(*flight["chip"], grad_x)
    for n, o, t in zip(early, flight["own"], early_landed):
        halves[n] = _chip_sum(o, t, "chip_sum_" + n)
    *others, small_all = _final_exchange([halves[n] for n in BIG], halves["small"])
    small_all = small_all.reshape(N_DEV * SMALL_ROWS, D_MODEL)
    grads, row = {}, 0
    for n in SMALL:
        shape = (3, 4 * w[n].shape[-1]) if n == "conv_w" else w[n].shape[1:] if n != "norm_final_g" else w[n].shape
        size = math.prod(shape)
        grads[n] = small_all[row:row + _tile_rows(size)].reshape(-1)[:size].reshape(shape)
        row += _tile_rows(size)
    loss = small_all[row, 0]
    cw = w["conv_w"].shape[-1]
    grads["conv_w"] = lax.dynamic_slice_in_dim(grads["conv_w"], xy * cw, cw, axis=1)
    grads = {n: _view(n, grads[n].reshape(w[n].shape)) for n in SMALL}
    wv, mv, vv = ({n: _view(n, t[n]) for n in WEIGHTS} for t in (w, m, v))

    delta, new_m, new_v = {}, {}, {}
    for n, other in zip(BIG, others):
        two_d = lambda t: t.reshape(t.shape[-2:])
        grads[n], delta[n], new_m[n], new_v[n] = _adamw_halves(
            two_d(wv[n]), halves[n], other, two_d(mv[n]), two_d(vv[n]), place, "adamw_" + n)
    for group, name in ((("b_re", "b_im"), "adamw_b"), (tuple(n for n in SMALL if n not in ("b_re", "b_im")), "adamw_small")):
        row = lambda t: t.reshape(1, -1) if t.ndim == 1 else t
        d_, m_, v_ = _adamw_many(*[[row(t[n]) for n in group] for t in (wv, grads, mv, vv)], name)
        for n, dn, mn, vn in zip(group, d_, m_, v_):
            delta[n], new_m[n], new_v[n] = (t.reshape(wv[n].shape) for t in (dn, mn, vn))
    natural = lambda t: [_unview(n, t[n].reshape(wv[n].shape)) for n in WEIGHTS]
    return (loss, grad_x[None], *natural(grads), *natural(delta), *natural(new_m), *natural(new_v))
```

```python
import functools
import math

import jax
import jax.numpy as jnp
import numpy as np
from jax import lax
from jax.experimental import pallas as pl
from jax.experimental.pallas import tpu as pltpu

F32 = jnp.float32
BF16 = jnp.bfloat16

D_MODEL = 1024
N_Q_HEADS = 8
N_KV_HEADS = 2
HEAD_DIM = 64
ATTN_WIDTH = 512
KV_WIDTH = 128
QKV_WIDTH = ATTN_WIDTH + 2 * KV_WIDTH
WINDOW = 128
BLOCK = 128
ROPE_DIM = 16
ROPE_THETA = 500000.0
SCORE_SCALE = HEAD_DIM ** -0.5
SSM_WIDTH = 512
SSM_GROUP = 16
N_SSM_GROUPS = 32
SSM_STATE = 64
IN_WIDTH = 1280
D_FF = 2816
EPS = 1e-6
ADAM_LR = 0.001
ADAM_B1 = 0.9
ADAM_B2 = 0.999
ADAM_EPS = 1e-08
ADAM_WD = 0.01
ADAM_STEP = 10

VMEM_BYTES_V7X = 64 * 1024 * 1024
SUBLANES = 8
LANES = 128
SSM_CB = 4
SSM_CH = 128
SSM_ST = 512
N_SEG = SUBLANES

NN = (((1,), (0,)), ((), ()))
NT = (((1,), (1,)), ((), ()))
TN = (((0,), (0,)), ((), ()))


def _params(sem=None, vmem_mb=48):
    limit = vmem_mb * 1024 * 1024
    assert limit < VMEM_BYTES_V7X
    return pltpu.CompilerParams(dimension_semantics=sem, vmem_limit_bytes=limit)


def _dg(a, b, dims):
    return lax.dot_general(a, b, dims, preferred_element_type=F32)


def _sigmoid(x):
    return 1.0 / (1.0 + jnp.exp(-x))


_SQRT_HALF = 0.7071067811865476
_INV_SQRT_2PI = 0.3989422804014327


def _gelu(x):
    return 0.5 * x * (1.0 + lax.erf(x * _SQRT_HALF))


def _gelu_grad(x):
    return 0.5 * (1.0 + lax.erf(x * _SQRT_HALF)) + x * (_INV_SQRT_2PI * jnp.exp(-0.5 * x * x))


def _mm_tn(a, b, tm, tn, name):
    k, m = a.shape
    n = b.shape[1]

    def body(a_ref, b_ref, o_ref):
        o_ref[...] = _dg(a_ref[...], b_ref[...], TN)

    return pl.pallas_call(
        body, grid=(m // tm, n // tn),
        in_specs=[pl.BlockSpec((k, tm), lambda i, j: (0, i)), pl.BlockSpec((k, tn), lambda i, j: (0, j))],
        out_specs=pl.BlockSpec((tm, tn), lambda i, j: (i, j)),
        out_shape=jax.ShapeDtypeStruct((m, n), F32), name=name,
        compiler_params=_params(("parallel", "parallel")),
    )(a, b)


def _mm_nn_cols(a, b4, tm, name):
    m, k = a.shape
    s, _, n = b4.shape

    def body(a_ref, b_ref, o_ref):
        o_ref[...] = _dg(a_ref[...], b_ref[...], NN)

    return pl.pallas_call(
        body, grid=(m // tm, s),
        in_specs=[pl.BlockSpec((tm, k), lambda i, j: (i, 0)), pl.BlockSpec((None, k, n), lambda i, j: (j, 0, 0))],
        out_specs=pl.BlockSpec((tm, n), lambda i, j: (i, j)),
        out_shape=jax.ShapeDtypeStruct((m, s * n), F32), name=name,
        compiler_params=_params(("parallel", "parallel")),
    )(a, b4)


def _mm_tn_cols(a, b2, s, tm, name):
    k, m = a.shape
    h, _, wide = b2.shape
    per = s // h
    n = wide // per

    def body(a_ref, b_ref, o_ref):
        o_ref[...] = _dg(a_ref[...], b_ref[...], TN)

    return pl.pallas_call(
        body, grid=(s, m // tm),
        in_specs=[pl.BlockSpec((k, tm), lambda j, i: (0, i)),
                  pl.BlockSpec((None, k, n), lambda j, i: (j // per, 0, j % per))],
        out_specs=pl.BlockSpec((None, tm, n), lambda j, i: (j, i, 0)),
        out_shape=jax.ShapeDtypeStruct((s, m, n), F32), name=name,
        compiler_params=_params(("parallel", "parallel")),
    )(a, b2)


TM_EW = 512


def _rms_bwd_vals(xv, gv, dy):
    r = lax.rsqrt(jnp.mean(xv * xv, axis=-1, keepdims=True) + EPS)
    xh = xv * r
    dxh = dy * gv
    dx = r * (dxh - xh * jnp.mean(dxh * xh, axis=-1, keepdims=True))
    return dx, dy * xh


TM_FUSED = 512
TM_LOSS = 256


def _rms_vals(xv, gv):
    return xv * lax.rsqrt(jnp.mean(xv * xv, axis=-1, keepdims=True) + EPS) * gv


def _rope_blocks(src, dst, c, lo, hi):
    nq = ATTN_WIDTH // LANES
    for blk in range(nq + 1):
        t = src[:, blk * LANES:(blk + 1) * LANES]
        rot = t * c + pltpu.roll(t, LANES - 8, 1) * lo + pltpu.roll(t, 8, 1) * hi
        dst[:, blk * LANES:(blk + 1) * LANES] = (rot * SCORE_SCALE if blk < nq else rot).astype(BF16)
    dst[:, (nq + 1) * LANES:] = src[:, (nq + 1) * LANES:].astype(BF16)


def _rms_mm_rope(x, g, wt, tabs, name):
    l, d = x.shape
    n = wt.shape[0]

    def body(x_ref, g_ref, w_ref, c_ref, lo_ref, hi_ref, h_ref, qkv_ref, u_ref):
        h = _rms_vals(x_ref[...], g_ref[...]).astype(BF16)
        h_ref[...] = h
        out = _dg(h, w_ref[...], NT)
        _rope_blocks(out[:, :QKV_WIDTH], qkv_ref, c_ref[...], lo_ref[...], hi_ref[...])
        u_ref[...] = out[:, QKV_WIDTH:]

    row = lambda width: pl.BlockSpec((TM_FUSED, width), lambda i: (i, 0))
    return pl.pallas_call(
        body, grid=(l // TM_FUSED,),
        in_specs=[row(d), pl.BlockSpec((1, d), lambda i: (0, 0)), pl.BlockSpec((n, d), lambda i: (0, 0)),
                  row(LANES), row(LANES), row(LANES)],
        out_specs=[row(d), row(QKV_WIDTH), row(n - QKV_WIDTH)],
        out_shape=[jax.ShapeDtypeStruct((l, d), BF16), jax.ShapeDtypeStruct((l, QKV_WIDTH), BF16),
                   jax.ShapeDtypeStruct((l, n - QKV_WIDTH), F32)],
        name=name, compiler_params=_params(("parallel",)),
    )(x, g, wt, *tabs)


def _mix_mm_res_rms(attn, ys, g_attn, g_ssm, b, res, g, name):
    l, w = attn.shape
    d = b.shape[1]

    def body(a_ref, y_ref, ga_ref, gs_ref, b_ref, r_ref, g_ref, m_ref, x_ref, h_ref):
        m_ref[:, :w] = _rms_vals(a_ref[...], ga_ref[...]).astype(BF16)
        m_ref[:, w:] = _rms_vals(y_ref[...], gs_ref[...]).astype(BF16)
        xv = r_ref[...] + _dg(m_ref[...], b_ref[...], NN)
        x_ref[...] = xv
        h_ref[...] = _rms_vals(xv, g_ref[...]).astype(BF16)

    row = lambda width: pl.BlockSpec((TM_FUSED, width), lambda i: (i, 0))
    vec = lambda width: pl.BlockSpec((1, width), lambda i: (0, 0))
    return pl.pallas_call(
        body, grid=(l // TM_FUSED,),
        in_specs=[row(w), row(w), vec(w), vec(w), pl.BlockSpec((2 * w, d), lambda i: (0, 0)), row(d), vec(d)],
        out_specs=[row(2 * w), row(d), row(d)],
        out_shape=[jax.ShapeDtypeStruct((l, 2 * w), BF16), jax.ShapeDtypeStruct((l, d), F32),
                   jax.ShapeDtypeStruct((l, d), BF16)],
        name=name, compiler_params=_params(("parallel",)),
    )(attn, ys, g_attn, g_ssm, b, res, g)


def _mm_res_loss(a, b, res, g, target):
    l, k = a.shape
    d = b.shape[1]

    def body(a_ref, b_ref, r_ref, g_ref, t_ref, loss_ref, dx_ref, dxb_ref, dg_ref):
        xv = r_ref[...] + _dg(a_ref[...], b_ref[...], NN)
        gv = g_ref[...]
        r = lax.rsqrt(jnp.mean(xv * xv, axis=-1, keepdims=True) + EPS)
        xh = xv * r
        e = xh * gv - t_ref[...]
        part = jnp.sum(jnp.sum(e * e, axis=1, keepdims=True), axis=0, keepdims=True) * (0.5 / d)
        dy = e * (1.0 / d)
        dxh = dy * gv
        dx = r * (dxh - xh * jnp.mean(dxh * xh, axis=-1, keepdims=True))
        dx_ref[...] = dx
        dxb_ref[...] = dx.astype(BF16)

        @pl.when(pl.program_id(0) == 0)
        def _():
            dg_ref[...] = jnp.zeros_like(dg_ref)
            loss_ref[...] = jnp.zeros_like(loss_ref)

        dg_ref[...] += jnp.sum(dy * xh, axis=0, keepdims=True)
        loss_ref[...] += part

    row = lambda width: pl.BlockSpec((TM_LOSS, width), lambda i: (i, 0))
    vec = pl.BlockSpec((1, d), lambda i: (0, 0))
    return pl.pallas_call(
        body, grid=(l // TM_LOSS,),
        in_specs=[row(k), pl.BlockSpec((k, d), lambda i: (0, 0)), row(d), vec, row(d)],
        out_specs=[pl.BlockSpec((1, 1), lambda i: (0, 0)), row(d), row(d), vec],
        out_shape=[jax.ShapeDtypeStruct((1, 1), F32), jax.ShapeDtypeStruct((l, d), F32),
                   jax.ShapeDtypeStruct((l, d), BF16), jax.ShapeDtypeStruct((1, d), F32)],
        name="mm_down_loss", compiler_params=_params(("arbitrary",)),
    )(a, b, res, g, target)


def _mm_rms_bwd(a, b, a_spec, b_spec, matmul, x, g, res, name):
    l, d = x.shape

    def body(a_ref, b_ref, x_ref, g_ref, res_ref, dx_ref, dxb_ref, dg_ref):
        dx, dgr = _rms_bwd_vals(x_ref[...], g_ref[...], matmul(a_ref, b_ref))
        dx = dx + res_ref[...]
        dx_ref[...] = dx
        dxb_ref[...] = dx.astype(BF16)

        @pl.when(pl.program_id(0) == 0)
        def _():
            dg_ref[...] = jnp.zeros_like(dg_ref)

        dg_ref[...] += jnp.sum(dgr, axis=0, keepdims=True)

    row = pl.BlockSpec((TM_FUSED, d), lambda i: (i, 0))
    vec = pl.BlockSpec((1, d), lambda i: (0, 0))
    return pl.pallas_call(
        body, grid=(l // TM_FUSED,), in_specs=[a_spec, b_spec, row, vec, row], out_specs=[row, row, vec],
        out_shape=[jax.ShapeDtypeStruct((l, d), F32), jax.ShapeDtypeStruct((l, d), BF16),
                   jax.ShapeDtypeStruct((1, d), F32)],
        name=name, compiler_params=_params(("arbitrary",)),
    )(a, b, x, g, res)


def _mm_nn_rms_bwd(a, b, x, g, res, name):
    return _mm_rms_bwd(a, b, pl.BlockSpec((TM_FUSED, a.shape[1]), lambda i: (i, 0)),
                       pl.BlockSpec(b.shape, lambda i: (0, 0)),
                       lambda a_ref, b_ref: _dg(a_ref[...], b_ref[...], NN), x, g, res, name)


def _mm_cols_rms_bwd(a2, b4, x, g, res, name):
    h, _, wide = a2.shape
    s, _, n = b4.shape
    per = s // h

    def matmul(a_ref, b_ref):
        acc = None
        for j in range(s):
            part = _dg(a_ref[j // per, :, (j % per) * n:(j % per + 1) * n], b_ref[j], NT)
            acc = part if acc is None else acc + part
        return acc

    return _mm_rms_bwd(a2, b4, pl.BlockSpec((h, TM_FUSED, wide), lambda i: (0, i, 0)),
                       pl.BlockSpec(b4.shape, lambda i: (0, 0, 0), pipeline_mode=pl.Buffered(1)),
                       matmul, x, g, res, name)


def _mm_mix_bwd(dx, b, attn, ys, g_attn, g_ssm, name):
    l, w = attn.shape
    d = dx.shape[1]

    def body(dx_ref, b_ref, a_ref, y_ref, ga_ref, gs_ref, da_ref, dy_ref, dga_ref, dgs_ref):
        @pl.when(pl.program_id(0) == 0)
        def _():
            dga_ref[...] = jnp.zeros_like(dga_ref)
            dgs_ref[...] = jnp.zeros_like(dgs_ref)

        dm = _dg(dx_ref[...], b_ref[...], NT)
        for src, gr, off, dst, dgr in ((a_ref, ga_ref, 0, da_ref, dga_ref), (y_ref, gs_ref, w, dy_ref, dgs_ref)):
            dxv, dg_rows = _rms_bwd_vals(src[...], gr[...], dm[:, off:off + w])
            dst[...] = dxv
            dgr[...] += jnp.sum(dg_rows, axis=0, keepdims=True)

    row = lambda width: pl.BlockSpec((TM_FUSED, width), lambda i: (i, 0))
    vec = pl.BlockSpec((1, w), lambda i: (0, 0))
    return pl.pallas_call(
        body, grid=(l // TM_FUSED,),
        in_specs=[row(d), pl.BlockSpec((2 * w, d), lambda i: (0, 0)), row(w), row(w), vec, vec],
        out_specs=[row(w), row(w), vec, vec],
        out_shape=[jax.ShapeDtypeStruct((l, w), F32), jax.ShapeDtypeStruct((l, w), F32),
                   jax.ShapeDtypeStruct((1, w), F32), jax.ShapeDtypeStruct((1, w), F32)],
        name=name, compiler_params=_params(("arbitrary",)),
    )(dx, b, attn, ys, g_attn, g_ssm)


def _rope_tables(l):
    half = ROPE_DIM // 2
    f32 = np.float32
    inv_freq = np.power(f32(ROPE_THETA), -np.arange(half, dtype=f32) / f32(half))
    ang = np.arange(l, dtype=f32)[:, None] * inv_freq[None, :]
    cos, sin = np.cos(ang), np.sin(ang)
    ones = np.ones((l, HEAD_DIM - ROPE_DIM), f32)
    zeros = np.zeros((l, HEAD_DIM - ROPE_DIM), f32)
    zh = np.zeros((l, half), f32)
    c = np.concatenate([cos, cos, ones], axis=1)
    s_lo = np.concatenate([-sin, zh, zeros], axis=1)
    s_hi = np.concatenate([zh, sin, zeros], axis=1)
    return tuple(jnp.asarray(np.tile(t, (1, LANES // HEAD_DIM)), F32) for t in (c, s_lo, s_hi))


def _rope_bwd(dq, dkv, du_ssm, dpre, d_skip, tabs):
    l = dq.shape[0]
    nq = ATTN_WIDTH // LANES

    def body(dq_ref, dkv_ref, du_ref, dpre_ref, ds_ref, c_ref, lo_ref, hi_ref, o_ref):
        c, lo, hi = c_ref[...], lo_ref[...], hi_ref[...]
        for blk in range(nq + 1):
            t = dq_ref[:, blk * LANES:(blk + 1) * LANES] if blk < nq else dkv_ref[:, :KV_WIDTH]
            g = t * c + pltpu.roll(t * lo, 8, 1) + pltpu.roll(t * hi, LANES - 8, 1)
            o_ref[:, blk * LANES:(blk + 1) * LANES] = g.astype(BF16)
        o_ref[:, (nq + 1) * LANES:QKV_WIDTH] = dkv_ref[:, KV_WIDTH:].astype(BF16)
        o_ref[:, QKV_WIDTH:] = (du_ref[...] + dpre_ref[...] * ds_ref[...]).astype(BF16)

    tab = pl.BlockSpec((TM_EW, LANES), lambda i: (i, 0))
    wide = pl.BlockSpec((TM_EW, SSM_WIDTH), lambda i: (i, 0))
    return pl.pallas_call(
        body, grid=(l // TM_EW,),
        in_specs=[wide, pl.BlockSpec((TM_EW, 2 * KV_WIDTH), lambda i: (i, 0)), wide, wide,
                  pl.BlockSpec((1, SSM_WIDTH), lambda i: (0, 0)), tab, tab, tab],
        out_specs=pl.BlockSpec((TM_EW, IN_WIDTH), lambda i: (i, 0)),
        out_shape=jax.ShapeDtypeStruct((l, IN_WIDTH), BF16), name="rope_bwd",
        compiler_params=_params(("parallel",)),
    )(dq, dkv, du_ssm, dpre, d_skip, *tabs)


_Q_COLS = ATTN_WIDTH // LANES
_NEG = -1e30


def _window_specs(nb, width, col):
    return [
        pl.BlockSpec((BLOCK, width), lambda n: (jnp.maximum(n - 1, 0), col)),
        pl.BlockSpec((BLOCK, width), lambda n: (n, col)),
        pl.BlockSpec((BLOCK, width), lambda n: (jnp.minimum(n + 1, nb - 1), col)),
    ]


def _stacked_sink(sink_ref, heads):
    rid = lax.broadcasted_iota(jnp.int32, (len(heads) * BLOCK, 1), 0)
    sk = jnp.full(rid.shape, sink_ref[0, heads[-1]], F32)
    for g in range(len(heads) - 2, -1, -1):
        sk = jnp.where(rid < (g + 1) * BLOCK, sink_ref[0, heads[g]], sk)
    return sk


def _attn_fwd(qkv, sink):
    l = qkv.shape[0]
    nb = l // BLOCK
    grp = N_Q_HEADS // N_KV_HEADS

    def body(sink_ref, q_ref, k0, k1, k2, v0, v1, v2, o_ref, lse_ref):
        n = pl.program_id(0)
        q = q_ref[...]
        kw = jnp.concatenate([k0[...], k1[...], k2[...]], axis=0)
        vw = jnp.concatenate([v0[...], v1[...], v2[...]], axis=0)
        row = lax.broadcasted_iota(jnp.int32, (grp * BLOCK, 3 * BLOCK), 0)
        col = lax.broadcasted_iota(jnp.int32, (grp * BLOCK, 3 * BLOCK), 1)
        valid = jnp.abs(col - BLOCK - (row & (BLOCK - 1))) <= WINDOW
        valid &= jnp.logical_not((n == 0) & (col < BLOCK))
        valid &= jnp.logical_not((n == nb - 1) & (col >= 2 * BLOCK))
        for hk in range(N_KV_HEADS):
            heads = range(hk * grp, (hk + 1) * grp)
            qs = jnp.concatenate([q[:, h * HEAD_DIM:(h + 1) * HEAD_DIM] for h in heads], axis=0)
            kh = kw[:, hk * HEAD_DIM:(hk + 1) * HEAD_DIM]
            vh = vw[:, hk * HEAD_DIM:(hk + 1) * HEAD_DIM]
            s = jnp.where(valid, _dg(qs, kh, NT), _NEG)
            sk = _stacked_sink(sink_ref, heads)
            m = jnp.maximum(jnp.max(s, axis=1, keepdims=True), sk)
            p = jnp.exp(s - m)
            denom = jnp.sum(p, axis=1, keepdims=True) + jnp.exp(sk - m)
            o = _dg((p / denom).astype(BF16), vh, NN)
            lse = m + jnp.log(denom)
            for g, h in enumerate(heads):
                o_ref[:, h * HEAD_DIM:(h + 1) * HEAD_DIM] = o[g * BLOCK:(g + 1) * BLOCK]
                lse_ref[:, h:h + 1] = lse[g * BLOCK:(g + 1) * BLOCK]

    return pl.pallas_call(
        body, grid=(nb,),
        in_specs=[pl.BlockSpec(memory_space=pltpu.SMEM),
                  pl.BlockSpec((BLOCK, ATTN_WIDTH), lambda n: (n, 0))]
        + _window_specs(nb, KV_WIDTH, _Q_COLS) + _window_specs(nb, KV_WIDTH, _Q_COLS + 1),
        out_specs=[pl.BlockSpec((BLOCK, ATTN_WIDTH), lambda n: (n, 0)),
                   pl.BlockSpec((BLOCK, N_Q_HEADS), lambda n: (n, 0))],
        out_shape=[jax.ShapeDtypeStruct((l, ATTN_WIDTH), F32), jax.ShapeDtypeStruct((l, N_Q_HEADS), F32)],
        name="attn_fwd", compiler_params=_params(("parallel",)),
    )(sink, qkv, qkv, qkv, qkv, qkv, qkv, qkv)


def _attn_bwd(qkv, attn, dattn, lse, sink):
    l = qkv.shape[0]
    nb = l // BLOCK
    grp = N_Q_HEADS // N_KV_HEADS
    win = 3 * BLOCK

    def body(sink_ref, q_ref, k0, k1, k2, v0, v1, v2, o_ref, d_ref, l_ref, dq_ref, dkv_ref, dsink_ref, ring_ref):
        n = pl.program_id(0)

        @pl.when(n == 0)
        def _():
            dsink_ref[...] = jnp.zeros_like(dsink_ref)
            ring_ref[...] = jnp.zeros_like(ring_ref)

        @pl.when(n < nb)
        def _():
            first, last = n == 0, n == nb - 1
            cat = lambda a, b, c: jnp.concatenate([a[...], b[...], c[...]], axis=0)
            q, kw, vw = q_ref[...], cat(k0, k1, k2), cat(v0, v1, v2)
            dov = d_ref[...]
            prod = o_ref[...] * dov
            dob = dov.astype(BF16)
            lse = l_ref[...]
            row = lax.broadcasted_iota(jnp.int32, (grp * BLOCK, win), 0)
            col = lax.broadcasted_iota(jnp.int32, (grp * BLOCK, win), 1)
            valid = jnp.abs(col - BLOCK - (row & (BLOCK - 1))) <= WINDOW
            valid &= jnp.logical_not(first & (col < BLOCK))
            valid &= jnp.logical_not(last & (col >= 2 * BLOCK))

            dsink_parts, dks, dvs = [], [], []
            for hk in range(N_KV_HEADS):
                heads = range(hk * grp, (hk + 1) * grp)
                ksl = slice(hk * HEAD_DIM, (hk + 1) * HEAD_DIM)
                hsl = [slice(h * HEAD_DIM, (h + 1) * HEAD_DIM) for h in heads]
                stack = lambda parts: jnp.concatenate(parts, axis=0)
                qs = stack([q[:, s_] for s_ in hsl])
                dos = stack([dob[:, s_] for s_ in hsl])
                deltas = stack([jnp.sum(prod[:, s_], axis=1, keepdims=True) for s_ in hsl])
                lses = stack([lse[:, h:h + 1] for h in heads])
                kh, vh = kw[:, ksl], vw[:, ksl]
                s = jnp.where(valid, _dg(qs, kh, NT), _NEG)
                p = jnp.exp(s - lses)
                dp = _dg(dos, vh, NT)
                ds = (p * (dp - deltas)).astype(BF16)
                dq = _dg(ds, kh, NN) * SCORE_SCALE
                sink_rows = jnp.exp(_stacked_sink(sink_ref, heads) - lses) * deltas
                for g in range(grp):
                    dq_ref[:, hsl[g]] = dq[g * BLOCK:(g + 1) * BLOCK]
                    dsink_parts.append(jnp.sum(sink_rows[g * BLOCK:(g + 1) * BLOCK], axis=0, keepdims=True))
                dks.append(_dg(ds, qs, TN))
                dvs.append(_dg(p.astype(BF16), dos, TN))
            dsink_ref[...] -= jnp.concatenate(dsink_parts, axis=1)
            part = jnp.concatenate(dks + dvs, axis=1)
            ring_ref[(n + 2) % 3] += part[0:BLOCK]
            ring_ref[n % 3] += part[BLOCK:2 * BLOCK]
            ring_ref[(n + 1) % 3] = part[2 * BLOCK:]

        @pl.when(n >= 1)
        def _():
            dkv_ref[...] = ring_ref[(n + 2) % 3]

    centre = lambda n: jnp.minimum(n, nb - 1)
    window = lambda width, col: [
        pl.BlockSpec((BLOCK, width), lambda n: (jnp.maximum(centre(n) - 1, 0), col)),
        pl.BlockSpec((BLOCK, width), lambda n: (centre(n), col)),
        pl.BlockSpec((BLOCK, width), lambda n: (jnp.minimum(centre(n) + 1, nb - 1), col))]
    own = lambda width: pl.BlockSpec((BLOCK, width), lambda n: (centre(n), 0))
    return pl.pallas_call(
        body, grid=(nb + 1,),
        in_specs=[pl.BlockSpec(memory_space=pltpu.SMEM), own(ATTN_WIDTH)]
        + window(KV_WIDTH, _Q_COLS) + window(KV_WIDTH, _Q_COLS + 1)
        + [own(ATTN_WIDTH), own(ATTN_WIDTH), own(N_Q_HEADS)],
        out_specs=[own(ATTN_WIDTH), pl.BlockSpec((BLOCK, 2 * KV_WIDTH), lambda n: (jnp.maximum(n - 1, 0), 0)),
                   pl.BlockSpec((1, N_Q_HEADS), lambda n: (0, 0))],
        out_shape=[jax.ShapeDtypeStruct((l, ATTN_WIDTH), F32), jax.ShapeDtypeStruct((l, 2 * KV_WIDTH), F32),
                   jax.ShapeDtypeStruct((1, N_Q_HEADS), F32)],
        scratch_shapes=[pltpu.VMEM((3, BLOCK, 2 * KV_WIDTH), F32)],
        name="attn_bwd", compiler_params=_params(("arbitrary",)),
    )(sink, qkv, qkv, qkv, qkv, qkv, qkv, qkv, attn, dattn, lse)


def _ssm_disc(a_re, a_im, log_step, b_re, b_im):
    step = jnp.exp(log_step)[..., None]
    mag = jnp.exp(a_re * step)
    lb_re, lb_im = mag * jnp.cos(a_im * step), mag * jnp.sin(a_im * step)
    nr, ni = lb_re - 1.0, lb_im
    den = a_re * a_re + a_im * a_im
    f_re = ((nr * a_re + ni * a_im) / den)[..., None]
    f_im = ((ni * a_re - nr * a_im) / den)[..., None]
    return lb_re, lb_im, f_re * b_re - f_im * b_im, f_re * b_im + f_im * b_re


def _ssm_pack(lb_re, lb_im, bb_re, bb_im, c_re, c_im):
    eye = jnp.eye(SSM_CH // SSM_GROUP, dtype=F32)
    ng = SSM_CH // SSM_GROUP

    def diag_b(bb):
        t = bb.reshape(2, SSM_CB, ng, SSM_STATE, SSM_GROUP)
        return jnp.einsum('dkgpc,gh->dkgchp', t, eye).reshape(2, SSM_CB, SSM_CH, SSM_ST)

    def diag_c(cc):
        t = cc.reshape(2, SSM_CB, ng, SSM_GROUP, SSM_STATE)
        return jnp.einsum('dkgcp,gh->dkhpgc', t, eye).reshape(2, SSM_CB, SSM_ST, SSM_CH)

    bcat = jnp.concatenate([diag_b(bb_re), diag_b(bb_im)], axis=-1)
    ccat = jnp.concatenate([diag_c(c_re), -diag_c(c_im)], axis=-2)
    lam_re = lb_re.reshape(2, SSM_CB, 1, SSM_ST)
    lam_im = lb_im.reshape(2, SSM_CB, 1, SSM_ST)
    return bcat, ccat, lam_re, lam_im


def _ssm_unpack(dbcat, dccat, dlam_re, dlam_im):
    ng = SSM_CH // SSM_GROUP
    eye = jnp.eye(ng, dtype=F32)

    def undiag_b(t):
        t = t.reshape(2, SSM_CB, ng, SSM_GROUP, ng, SSM_STATE)
        return jnp.einsum('dkgchp,gh->dkgpc', t, eye).reshape(2, N_SSM_GROUPS, SSM_STATE, SSM_GROUP)

    def undiag_c(t):
        t = t.reshape(2, SSM_CB, ng, SSM_STATE, ng, SSM_GROUP)
        return jnp.einsum('dkhpgc,gh->dkgcp', t, eye).reshape(2, N_SSM_GROUPS, SSM_GROUP, SSM_STATE)

    dbb_re, dbb_im = undiag_b(dbcat[..., :SSM_ST]), undiag_b(dbcat[..., SSM_ST:])
    dc_re, dc_im = undiag_c(dccat[:, :, :SSM_ST]), -undiag_c(dccat[:, :, SSM_ST:])
    shape = (2, N_SSM_GROUPS, SSM_STATE)
    return dlam_re.reshape(shape), dlam_im.reshape(shape), dbb_re, dbb_im, dc_re, dc_im


def _to_segments(t):
    l, w = t.shape
    return t.reshape(N_SEG, l // N_SEG, w).transpose(1, 0, 2).reshape(l, w)


def _from_segments(t):
    l, w = t.shape
    return t.reshape(l // N_SEG, N_SEG, w).transpose(1, 0, 2).reshape(l, w)


SSM_RC = 256
SSM_JC = SSM_RC // N_SEG
_RE, _IM = pl.ds(0, SSM_ST), pl.ds(SSM_ST, SSM_ST)


def _cfma(ar, ai, xr, xi, br, bi):
    return ar * xr - ai * xi + br, ar * xi + ai * xr + bi


def _chunk_rows(ci, rev, nc):
    start = jnp.where(rev, (nc - 1 - ci) * SSM_RC, ci * SSM_RC)
    return pl.ds(pl.multiple_of(start, SSM_RC), SSM_RC)


def _scan_chunk(src, dst, ar, ai, rev, nj, ci, carry, prev_ref=None):
    def rows_of(staged, j, k):
        at = jnp.where(rev, SSM_JC - 1 - k, k) if staged else j
        return pl.ds(pl.multiple_of(at * N_SEG, N_SEG), N_SEG)

    for k in range(SSM_JC):
        jj = ci * SSM_JC + k
        j = jnp.where(rev, nj - 1 - jj, jj)
        rows = rows_of(src[1], j, k)
        nr, ni = _cfma(ar, ai, carry[0], carry[1], src[0][rows, _RE], src[0][rows, _IM])
        if dst is not None:
            rows = rows_of(dst[1], j, k)
            dst[0][rows, _RE] = nr
            dst[0][rows, _IM] = ni
        if prev_ref is None:
            carry = (nr, ni)
            continue
        jp = jnp.where(rev, j - 1, j + 1)
        if k == SSM_JC - 1:
            inside = jnp.where((jp >= 0) & (jp < nj), 1.0, 0.0)
            jp = jnp.clip(jp, 0, nj - 1)
        prow = pl.ds(pl.multiple_of(jp * N_SEG, N_SEG), N_SEG)
        xr, xi = prev_ref[prow, _RE], prev_ref[prow, _IM]
        sr, si = nr * xr + ni * xi, ni * xr - nr * xi
        if k == SSM_JC - 1:
            sr, si = inside * sr, inside * si
        carry = (nr, ni, carry[2] + sr, carry[3] + si)
    return carry


def _segment_inits(ar, ai, end_r, end_i, rev, nj):
    pr, pi = ar, ai
    for _ in range(int(math.log2(nj))):
        pr, pi = pr * pr - pi * pi, 2.0 * pr * pi
    seg = lax.broadcasted_iota(jnp.int32, end_r.shape, 0)
    zero = jnp.zeros_like(end_r)

    def chain(shift, keep):
        ir, ii = zero, zero
        for _ in range(N_SEG - 1):
            tr, ti = _cfma(pr, pi, ir, ii, end_r, end_i)
            ir = jnp.where(keep, pltpu.roll(tr, shift, 0), 0.0)
            ii = jnp.where(keep, pltpu.roll(ti, shift, 0), 0.0)
        return ir, ii

    up_r, up_i = chain(1, seg >= 1)
    dn_r, dn_i = chain(N_SEG - 1, seg <= N_SEG - 2)
    return jnp.where(rev, dn_r, up_r), jnp.where(rev, dn_i, up_i)


def _ssm_specs(l):
    act = pl.BlockSpec((l, SSM_CH), lambda k, d: (0, k))
    bmat = pl.BlockSpec((None, None, SSM_CH, 2 * SSM_ST), lambda k, d: (d, k, 0, 0))
    cmat = pl.BlockSpec((None, None, 2 * SSM_ST, SSM_CH), lambda k, d: (d, k, 0, 0))
    lam = pl.BlockSpec((None, None, 1, SSM_ST), lambda k, d: (d, k, 0, 0))
    return act, bmat, cmat, lam


def _ssm_fwd(u_seg, bcat, ccat, lam_re, lam_im):
    l = u_seg.shape[0]
    nj = l // N_SEG
    nc = l // SSM_RC

    def body(u_ref, b_ref, c_ref, lr_ref, li_ref, y_ref, ub_ref, keep_ref, xs_ref, stage0, stage1, keep_sem):
        k, d = pl.program_id(0), pl.program_id(1)
        rev = d == 1
        shape = (N_SEG, SSM_ST)
        ar, ai = jnp.broadcast_to(lr_ref[...], shape), jnp.broadcast_to(li_ref[...], shape)
        zero = jnp.zeros(shape, F32)

        def inputs(ci, stage):
            rows = _chunk_rows(ci, rev, nc)
            ub = u_ref[rows, :].astype(BF16)
            ub_ref[rows, :] = ub
            bu = _dg(ub, b_ref[...], NN)
            stage[...] = bu
            xs_ref[rows, :] = bu

        def first(stage, ci, carry):
            return _scan_chunk((stage, True), None, ar, ai, rev, nj, ci, carry)

        def first_pass(t, carry):
            inputs(2 * t + 1, stage1)
            carry = first(stage0, 2 * t, carry)
            inputs(2 * t + 2, stage0)
            return first(stage1, 2 * t + 1, carry)

        inputs(0, stage0)
        carry = lax.fori_loop(0, nc // 2 - 1, first_pass, (zero, zero))
        inputs(nc - 1, stage1)
        carry = first(stage0, nc - 2, carry)
        end_r, end_i = first(stage1, nc - 1, carry)
        init = _segment_inits(ar, ai, end_r, end_i, rev, nj)

        @pl.when(d == 0)
        def _():
            y_ref[...] = jnp.zeros_like(y_ref)

        def outputs(ci):
            rows = _chunk_rows(ci, rev, nc)
            y_ref[rows, :] += _dg(xs_ref[rows, :].astype(BF16), c_ref[...], NN)
            pltpu.make_async_copy(xs_ref.at[rows], keep_ref.at[d, k, rows], keep_sem).start()

        def second(ci, carry):
            return _scan_chunk((xs_ref, False), (xs_ref, False), ar, ai, rev, nj, ci, carry)

        def second_pass(ci, carry):
            outputs(ci - 1)
            return second(ci, carry)

        lax.fori_loop(1, nc, second_pass, second(0, init))
        outputs(nc - 1)
        pltpu.make_async_copy(xs_ref, keep_ref.at[d, k], keep_sem).wait()

    act, bmat, cmat, lam = _ssm_specs(l)
    return pl.pallas_call(
        body, grid=(SSM_CB, 2), in_specs=[act, bmat, cmat, lam, lam], out_specs=[act, act, ANY],
        out_shape=[jax.ShapeDtypeStruct((l, SSM_WIDTH), F32), jax.ShapeDtypeStruct((l, SSM_WIDTH), BF16),
                   jax.ShapeDtypeStruct((2, SSM_CB, l, 2 * SSM_ST), F32)],
        scratch_shapes=[pltpu.VMEM((l, 2 * SSM_ST), F32), pltpu.VMEM((SSM_RC, 2 * SSM_ST), F32),
                        pltpu.VMEM((SSM_RC, 2 * SSM_ST), F32), pltpu.SemaphoreType.DMA],
        name="ssm_fwd", compiler_params=_params(("parallel", "arbitrary"), vmem_mb=56),
    )(u_seg, bcat.astype(BF16), ccat.astype(BF16), lam_re, lam_im)


def _ssm_bwd(u_seg, dy_seg, states, bcat, ccat, lam_re, lam_im):
    l = u_seg.shape[0]
    nj = l // N_SEG
    nc = l // SSM_RC

    def body(u_ref, dy_ref, keep_ref, b_ref, c_ref, lr_ref, li_ref,
             du_ref, db_ref, dc_ref, dlr_ref, dli_ref, xs_ref, gs_ref, dyb_ref, stage0, stage1, keep_sem):
        k, d = pl.program_id(0), pl.program_id(1)
        rev = d == 1
        back = jnp.logical_not(rev)
        shape = (N_SEG, SSM_ST)
        ar, ai = jnp.broadcast_to(lr_ref[...], shape), -jnp.broadcast_to(li_ref[...], shape)
        zero = jnp.zeros(shape, F32)
        fetch = pltpu.make_async_copy(keep_ref.at[d, k], xs_ref, keep_sem)
        fetch.start()

        def inputs(ci, stage):
            rows = _chunk_rows(ci, back, nc)
            dyb = dy_ref[rows, :].astype(BF16)
            dyb_ref[rows, :] = dyb
            dx = _dg(dyb, c_ref[...], NT)
            stage[...] = dx
            gs_ref[rows, :] = dx

        def first(stage, ci, carry):
            return _scan_chunk((stage, True), None, ar, ai, back, nj, ci, carry)

        def first_pass(t, carry):
            inputs(2 * t + 1, stage1)
            carry = first(stage0, 2 * t, carry)
            inputs(2 * t + 2, stage0)
            return first(stage1, 2 * t + 1, carry)

        inputs(0, stage0)
        carry = lax.fori_loop(0, nc // 2 - 1, first_pass, (zero, zero))
        inputs(nc - 1, stage1)
        carry = first(stage0, nc - 2, carry)
        end_r, end_i = first(stage1, nc - 1, carry)
        init = _segment_inits(ar, ai, end_r, end_i, back, nj)
        fetch.wait()
        db_ref[...] = jnp.zeros_like(db_ref)
        dc_ref[...] = jnp.zeros_like(dc_ref)

        @pl.when(d == 0)
        def _():
            du_ref[...] = jnp.zeros_like(du_ref)

        def outputs(ci, stage):
            rows = _chunk_rows(ci, back, nc)
            g = stage[...].astype(BF16)
            dc_ref[...] += _dg(xs_ref[rows, :].astype(BF16), dyb_ref[rows, :], TN)
            db_ref[...] += _dg(u_ref[rows, :], g, TN)
            du_ref[rows, :] += _dg(g, b_ref[...], NT)

        def second(ci, stage, carry):
            return _scan_chunk((gs_ref, False), (stage, True), ar, ai, back, nj, ci, carry, prev_ref=xs_ref)

        def second_pass(t, carry):
            outputs(2 * t, stage0)
            carry = second(2 * t + 1, stage1, carry)
            outputs(2 * t + 1, stage1)
            return second(2 * t + 2, stage0, carry)

        carry = lax.fori_loop(0, nc // 2 - 1, second_pass, second(0, stage0, init + (zero, zero)))
        outputs(nc - 2, stage0)
        gr, gi, acc_r, acc_i = second(nc - 1, stage1, carry)
        outputs(nc - 1, stage1)

        seg = lax.broadcasted_iota(jnp.int32, shape, 0)
        jb = jnp.where(rev, nj - 1, 0)
        erow = pl.ds(pl.multiple_of((nj - 1 - jb) * N_SEG, N_SEG), N_SEG)

        def before(t):
            up = jnp.where(seg >= 1, pltpu.roll(t, 1, 0), 0.0)
            down = jnp.where(seg <= N_SEG - 2, pltpu.roll(t, N_SEG - 1, 0), 0.0)
            return jnp.where(rev, down, up)

        init_r, init_i = before(xs_ref[erow, _RE]), before(xs_ref[erow, _IM])
        acc_r = acc_r + gr * init_r + gi * init_i
        acc_i = acc_i + gi * init_r - gr * init_i
        dlr_ref[...] = jnp.sum(acc_r, axis=0, keepdims=True)
        dli_ref[...] = jnp.sum(acc_i, axis=0, keepdims=True)

    act, bmat, cmat, lam = _ssm_specs(l)
    return pl.pallas_call(
        body, grid=(SSM_CB, 2), in_specs=[act, act, ANY, bmat, cmat, lam, lam],
        out_specs=[act, bmat, cmat, lam, lam],
        out_shape=[jax.ShapeDtypeStruct((l, SSM_WIDTH), F32),
                   jax.ShapeDtypeStruct(bcat.shape, F32), jax.ShapeDtypeStruct(ccat.shape, F32),
                   jax.ShapeDtypeStruct(lam_re.shape, F32), jax.ShapeDtypeStruct(lam_im.shape, F32)],
        scratch_shapes=[pltpu.VMEM((l, 2 * SSM_ST), F32), pltpu.VMEM((l, 2 * SSM_ST), F32),
                        pltpu.VMEM((l, SSM_CH), BF16),
                        pltpu.VMEM((SSM_RC, 2 * SSM_ST), F32), pltpu.VMEM((SSM_RC, 2 * SSM_ST), F32),
                        pltpu.SemaphoreType.DMA],
        name="ssm_bwd", compiler_params=_params(("parallel", "arbitrary"), vmem_mb=58),
    )(u_seg, dy_seg, states, bcat.astype(BF16), ccat.astype(BF16), lam_re, lam_im)


def _glu_fwd(y_ssm, u, d_skip, w_glu):
    l, w = u.shape

    def body(y_ref, u_ref, d_ref, w_ref, pre_ref, s_ref, ys_ref):
        pre = y_ref[...] + d_ref[...] * u_ref[...]
        z = _gelu(pre)
        s = _dg(z.astype(BF16), w_ref[...], NN)
        pre_ref[...] = pre
        s_ref[...] = s
        ys_ref[...] = z * _sigmoid(s)

    row = pl.BlockSpec((TM_EW, w), lambda i: (i, 0))
    out = jax.ShapeDtypeStruct((l, w), F32)
    return pl.pallas_call(
        body, grid=(l // TM_EW,),
        in_specs=[row, row, pl.BlockSpec((1, w), lambda i: (0, 0)), pl.BlockSpec((w, w), lambda i: (0, 0))],
        out_specs=[row, row, row], out_shape=[out, out, out], name="glu_fwd",
        compiler_params=_params(("parallel",)),
    )(y_ssm, u, d_skip, w_glu)


def _glu_bwd(pre, s, dys, u, d_skip, w_glu):
    l, w = u.shape

    def body(pre_ref, s_ref, dys_ref, u_ref, d_ref, w_ref, dpre_ref, z_ref, ds_ref, dd_ref):
        pre, dys = pre_ref[...], dys_ref[...]
        z = _gelu(pre)
        sig = _sigmoid(s_ref[...])
        ds = (dys * z * sig * (1.0 - sig)).astype(BF16)
        dz = dys * sig + _dg(ds, w_ref[...], NT)
        dpre = dz * _gelu_grad(pre)
        dpre_ref[...] = dpre
        z_ref[...] = z.astype(BF16)
        ds_ref[...] = ds

        @pl.when(pl.program_id(0) == 0)
        def _():
            dd_ref[...] = jnp.zeros_like(dd_ref)

        dd_ref[...] += jnp.sum(dpre * u_ref[...], axis=0, keepdims=True)

    row = pl.BlockSpec((TM_EW, w), lambda i: (i, 0))
    vec = pl.BlockSpec((1, w), lambda i: (0, 0))
    return pl.pallas_call(
        body, grid=(l // TM_EW,),
        in_specs=[row, row, row, row, vec, pl.BlockSpec((w, w), lambda i: (0, 0))],
        out_specs=[row, row, row, vec],
        out_shape=[jax.ShapeDtypeStruct((l, w), F32), jax.ShapeDtypeStruct((l, w), BF16),
                   jax.ShapeDtypeStruct((l, w), BF16), jax.ShapeDtypeStruct((1, w), F32)],
        name="glu_bwd", compiler_params=_params(("arbitrary",)),
    )(pre, s, dys, u, d_skip, w_glu)


TM_CV = 1024
TC_CV = 256
TM_CF = 512
TC_CF = D_FF // 2
HALO = SUBLANES


def _conv_specs(l, col0, tm=TM_CV, tc=TC_CV):
    per = tm // HALO
    nh = l // HALO
    off = col0 // tc
    return [
        pl.BlockSpec((HALO, tc), lambda j, i: (jnp.maximum(i * per - 1, 0), j + off)),
        pl.BlockSpec((tm, tc), lambda j, i: (i, j + off)),
        pl.BlockSpec((HALO, tc), lambda j, i: (jnp.minimum((i + 1) * per, nh - 1), j + off)),
    ]


def _ext(prev_ref, mid_ref, next_ref, first, last):
    p = jnp.where(first, 0.0, prev_ref[...])
    n = jnp.where(last, 0.0, next_ref[...])
    return jnp.concatenate([p, mid_ref[...], n], axis=0)


def _shift_dn(t):
    return pltpu.roll(t, 1, 0)


def _shift_up(t):
    return pltpu.roll(t, t.shape[0] - 1, 0)


def _conv3(e, w_ref, b_ref):
    return w_ref[0:1, :] * _shift_dn(e) + w_ref[1:2, :] * e + w_ref[2:3, :] * _shift_up(e) + b_ref[...]


def _convffn_fwd(up_pre, conv_w, conv_b):
    l = up_pre.shape[0]
    tm, tc = TM_CF, TC_CF
    ni = l // tm
    wspec = lambda off: pl.BlockSpec((3, tc), lambda j, i: (0, j + off))
    bspec = lambda off: pl.BlockSpec((1, tc), lambda j, i: (0, j + off))
    voff = D_FF // tc

    def body(gp, gm, gn, vp, vm, vn, wg, bg, wv, bv, o_ref):
        i = pl.program_id(1)
        first, last = i == 0, i == ni - 1
        gate = _conv3(_ext(gp, gm, gn, first, last), wg, bg)[HALO:HALO + tm]
        val = _conv3(_ext(vp, vm, vn, first, last), wv, bv)[HALO:HALO + tm]
        o_ref[...] = (gate * _sigmoid(gate) * val).astype(BF16)

    return pl.pallas_call(
        body, grid=(D_FF // tc, ni),
        in_specs=_conv_specs(l, 0, tm, tc) + _conv_specs(l, D_FF, tm, tc)
        + [wspec(0), bspec(0), wspec(voff), bspec(voff)],
        out_specs=pl.BlockSpec((tm, tc), lambda j, i: (i, j)),
        out_shape=jax.ShapeDtypeStruct((l, D_FF), BF16), name="convffn_fwd",
        compiler_params=_params(("parallel", "parallel")),
    )(up_pre, up_pre, up_pre, up_pre, up_pre, up_pre, conv_w, conv_b, conv_w, conv_b)


HALO_B = 2 * SUBLANES


def _convffn_bwd(up_pre, dx2b, w_down, conv_w, conv_b):
    l = up_pre.shape[0]
    ni = l // TM_CV
    d = dx2b.shape[1]
    wspec = lambda off: pl.BlockSpec((3, TC_CV), lambda i, j: (0, j + off))
    bspec = lambda off: pl.BlockSpec((1, TC_CV), lambda i, j: (0, j + off))
    voff = D_FF // TC_CV
    swap = lambda spec: pl.BlockSpec(spec.block_shape, lambda i, j, f=spec.index_map: f(j, i))
    per, nh = TM_CV // HALO_B, l // HALO_B
    dx_specs = [pl.BlockSpec((HALO_B, d), lambda i, j: (jnp.maximum(i * per - 1, 0), 0)),
                pl.BlockSpec((TM_CV, d), lambda i, j: (i, 0)),
                pl.BlockSpec((HALO_B, d), lambda i, j: (jnp.minimum((i + 1) * per, nh - 1), 0))]

    def body(gp, gm, gn, vp, vm, vn, xp, xm, xn, wd, wg, bg, wv, bv, dup_ref, pg_ref, pv_ref):
        i = pl.program_id(0)
        first, last = i == 0, i == ni - 1
        ge, ve = _ext(gp, gm, gn, first, last), _ext(vp, vm, vn, first, last)
        zero = jnp.zeros((HALO_B, d), BF16)
        dx = jnp.concatenate([jnp.where(first, zero, xp[...]), xm[...], jnp.where(last, zero, xn[...])], axis=0)
        de = _dg(dx, wd[...], NT)[HALO_B - HALO:HALO_B + TM_CV + HALO]
        taps = [(_shift_dn(e), e, _shift_up(e)) for e in (ge, ve)]
        conv = lambda t, w_ref, b_ref: w_ref[0:1, :] * t[0] + w_ref[1:2, :] * t[1] + w_ref[2:3, :] * t[2] + b_ref[...]
        gate, val = conv(taps[0], wg, bg), conv(taps[1], wv, bv)
        sig = _sigmoid(gate)
        silu = gate * sig
        dgate = de * val * (sig + silu * (1.0 - sig))
        dval = de * silu
        mid = slice(HALO, HALO + TM_CV)
        rid = lax.broadcasted_iota(jnp.int32, (SUBLANES, TC_CV), 0)
        for half, (dup, tap, w_ref, p_ref) in enumerate(((dgate, taps[0], wg, pg_ref), (dval, taps[1], wv, pv_ref))):
            dpre = w_ref[0:1, :] * _shift_up(dup) + w_ref[1:2, :] * dup + w_ref[2:3, :] * _shift_dn(dup)
            dup_ref[half] = dpre[mid].astype(BF16)
            dm_ = dup[mid]
            sums = [jnp.sum(dm_ * t[mid], axis=0, keepdims=True) for t in tap]
            sums.append(jnp.sum(dm_, axis=0, keepdims=True))
            acc = jnp.zeros((SUBLANES, TC_CV), F32)
            for k, sk in enumerate(sums):
                acc = jnp.where(rid == k, sk, acc)
            p_ref[...] = acc

    par = pl.BlockSpec((None, SUBLANES, TC_CV), lambda i, j: (i, 0, j))
    dup, pg, pv = pl.pallas_call(
        body, grid=(ni, D_FF // TC_CV),
        in_specs=[swap(s) for s in _conv_specs(l, 0) + _conv_specs(l, D_FF)] + dx_specs
        + [pl.BlockSpec((TC_CV, d), lambda i, j: (j, 0)), wspec(0), bspec(0), wspec(voff), bspec(voff)],
        out_specs=[pl.BlockSpec((2, TM_CV, TC_CV), lambda i, j: (0, i, j)), par, par],
        out_shape=[jax.ShapeDtypeStruct((2, l, D_FF), BF16),
                   jax.ShapeDtypeStruct((ni, SUBLANES, D_FF), F32), jax.ShapeDtypeStruct((ni, SUBLANES, D_FF), F32)],
        name="convffn_bwd", compiler_params=_params(("parallel", "parallel")),
    )(up_pre, up_pre, up_pre, up_pre, up_pre, up_pre, dx2b, dx2b, dx2b, w_down, conv_w, conv_b, conv_w, conv_b)
    return dup, jnp.concatenate([jnp.sum(pg, axis=0), jnp.sum(pv, axis=0)], axis=1)


def _local_step(x, target, wb, sp, mixer_weights=None, late_weights=None, grads_ready=None,
                grads_next=None):
    l = x.shape[0]
    tabs = _rope_tables(l)
    disc = _ssm_disc(sp["a_re"], sp["a_im"], sp["log_step"], sp["b_re"], sp["b_im"])
    bcat, ccat, lam_re, lam_im = _ssm_pack(*disc, sp["c_re"], sp["c_im"])
    d_skip = sp["d_skip"].reshape(1, SSM_WIDTH)

    h, qkv, u = _rms_mm_rope(x, sp["norm_mix_g"], wb["w_in"], tabs, "mm_in")
    attn, lse = _attn_fwd(qkv, sp["sink"])
    y_seg, u_seg, states = _ssm_fwd(_to_segments(u), bcat, ccat, lam_re, lam_im)
    y_ssm = _from_segments(y_seg)
    if mixer_weights is not None:
        wb = dict(wb, **mixer_weights(attn))
    pre, s_glu, ys = _glu_fwd(y_ssm, u, d_skip, wb["w_glu"])
    mixed, x1, h2 = _mix_mm_res_rms(attn, ys, sp["norm_attn_g"], sp["norm_ssm_g"], wb["w_out"], x,
                                    sp["norm_ffn_g"], "mm_out")
    if late_weights is not None:
        wb = dict(wb, **late_weights(h2))
    up_pre = _mm_nn_cols(h2, wb["w_up"], min(l, 1024), "mm_up")
    conv_w = wb["conv_w"]
    act = _convffn_fwd(up_pre, conv_w, sp["conv_b"])
    loss, dx2, dx2b, d_final_g = _mm_res_loss(act, wb["w_down"], x1, sp["norm_final_g"].reshape(1, D_MODEL), target)

    g = {"norm_final_g": d_final_g.reshape(D_MODEL)}
    g["w_down"] = _mm_tn(act, dx2b, D_FF // 2, 512, "mm_down_dw")
    dup_pre, conv_par = _convffn_bwd(up_pre, dx2b, wb["w_down"], conv_w, sp["conv_b"])
    g["conv_w"], g["conv_b"] = conv_par[0:3], conv_par[3:4]
    g["w_up"] = _mm_tn_cols(h2, dup_pre, wb["w_up"].shape[0], 512, "mm_up_dw")
    dx1, dx1b, g["norm_ffn_g"] = _mm_cols_rms_bwd(dup_pre, wb["w_up"], x1, sp["norm_ffn_g"], dx2, "mm_up_dx")
    g["w_out"] = _mm_tn(mixed, dx1b, 1024, 1024, "mm_out_dw")
    zero = grads_ready(g["w_up"], g["w_down"], g["w_out"]) if grads_ready is not None else 0.0
    dattn, dys, g["norm_attn_g"], g["norm_ssm_g"] = _mm_mix_bwd(
        dx1b, wb["w_out"], attn, ys, sp["norm_attn_g"] + zero, sp["norm_ssm_g"], "mm_out_dx")
    dpre, zb, dsb, dd = _glu_bwd(pre, s_glu, dys, u, d_skip, wb["w_glu"])
    g["d_skip"] = dd.reshape(N_SSM_GROUPS, SSM_GROUP)
    g["w_glu"] = _mm_tn(zb, dsb, 512, 512, "mm_glu_dw")
    zero = grads_next(g["w_glu"]) if grads_next is not None else 0.0
    du_seg, dbcat, dccat, dlam_re, dlam_im = _ssm_bwd(u_seg, _to_segments(dpre), states, bcat, ccat,
                                                      lam_re + zero, lam_im)
    dlb_re, dlb_im, dbb_re, dbb_im, g["c_re"], g["c_im"] = _ssm_unpack(dbcat, dccat, dlam_re, dlam_im)
    _, disc_vjp = jax.vjp(_ssm_disc, sp["a_re"], sp["a_im"], sp["log_step"], sp["b_re"], sp["b_im"])
    g["a_re"], g["a_im"], g["log_step"], g["b_re"], g["b_im"] = disc_vjp((dlb_re, dlb_im, dbb_re, dbb_im))
    dq, dkv, g["sink"] = _attn_bwd(qkv, attn, dattn, lse, sp["sink"])
    dproj = _rope_bwd(dq, dkv, _from_segments(du_seg), dpre, d_skip, tabs)
    g["w_in"] = _mm_tn(dproj, h, IN_WIDTH // 5, D_MODEL, "mm_in_dw")
    grad_x, _, g["norm_mix_g"] = _mm_nn_rms_bwd(dproj, wb["w_in"], x, sp["norm_mix_g"], dx1, "mm_in_dx")
    return loss, grad_x, g


MESH = pl.DeviceIdType.MESH
ANY = pl.BlockSpec(memory_space=pl.ANY)


def _place():
    x, y, c = lax.axis_index("x"), lax.axis_index("y"), lax.axis_index("c")
    chips = [(1 - x, y), (x, 1 - y), (1 - x, 1 - y)]
    return x, y, c, chips


def _chip_index(px, py):
    return 2 * px + py


CHUNK_BYTES = 256 * 1024
MAX_CHUNKS = 16


def _row_chunks(rows, row_bytes, align):
    n = max(1, min(MAX_CHUNKS, (rows * row_bytes) // CHUNK_BYTES))
    per = -(-rows // n)
    per = -(-per // align) * align
    return [(r0, min(per, rows - r0)) for r0 in range(0, rows, per)]


def _align_of(dtype):
    return SUBLANES * 4 // jnp.dtype(dtype).itemsize


def _remote(src, dst, send_sem, recv_sem, to):
    return pltpu.make_async_remote_copy(src_ref=src, dst_ref=dst, send_sem=send_sem, recv_sem=recv_sem,
                                        device_id=to, device_id_type=MESH)


CAST_ROWS = 64


def _gather_weights(shards, dtypes):
    nw = len(shards)

    def body(*refs):
        w_refs, o_refs = refs[:nw], refs[nw:2 * nw]
        send_sems, recv_sems, in_sems, out_sems = refs[2 * nw:2 * nw + 4]
        raw, cast = refs[2 * nw + 4:3 * nw + 4], refs[3 * nw + 4:]
        x, y, c, chips = _place()
        mine = _chip_index(x, y)
        sibling = (x, y, 1 - c)

        def rows_of(ref, chip, r0, nr):
            return ref.at[chip, pl.ds(r0, nr), :]

        def copy(wi, k, src, dst, to):
            return _remote(src, dst, send_sems.at[wi, k], recv_sems.at[wi, k], to)

        geo = []
        for wi in range(nw):
            rows, cols = w_refs[wi].shape
            row_bytes = cols * jnp.dtype(dtypes[wi]).itemsize
            geo.append((rows // 2, _row_chunks(rows // 2, row_bytes, _align_of(dtypes[wi]))))

        stage_in = [pltpu.make_async_copy(w_refs[wi], raw[wi], in_sems.at[wi]) for wi in range(nw)]
        for cp in stage_in:
            cp.start()
        staged = [raw[wi] if dtypes[wi] == w_refs[wi].dtype else cast[wi] for wi in range(nw)]
        stage_out = []
        for wi in range(nw):
            stage_in[wi].wait()
            if staged[wi] is not raw[wi]:
                def cast_rows(i, _, wi=wi):
                    rows = pl.ds(pl.multiple_of(i * CAST_ROWS, CAST_ROWS), CAST_ROWS)
                    cast[wi][rows, :] = raw[wi][rows, :].astype(dtypes[wi])
                    return 0

                lax.fori_loop(0, w_refs[wi].shape[0] // CAST_ROWS, cast_rows, 0)
            cp = pltpu.make_async_copy(staged[wi], o_refs[wi].at[mine], out_sems.at[wi])
            cp.start()
            stage_out.append(cp)

        for wi in range(nw):
            hr, half_chunks = geo[wi]
            for j, chip in enumerate(chips):
                for r0, nr in half_chunks:
                    copy(wi, j, staged[wi].at[pl.ds(c * hr + r0, nr), :],
                         rows_of(o_refs[wi], mine, c * hr + r0, nr), (*chip, c)).start()
        for wi in range(nw):
            hr, half_chunks = geo[wi]
            for j, chip in enumerate(chips):
                got = rows_of(o_refs[wi], _chip_index(*chip), c * hr, hr)
                copy(wi, j, got, got, (*chip, c)).wait_recv()
                for r0, nr in half_chunks:
                    piece = rows_of(o_refs[wi], _chip_index(*chip), c * hr + r0, nr)
                    copy(wi, 3 + j, piece, piece, sibling).start()
        for wi in range(nw):
            hr = geo[wi][0]
            for j, chip in enumerate(chips):
                got = rows_of(o_refs[wi], _chip_index(*chip), (1 - c) * hr, hr)
                copy(wi, 3 + j, got, got, sibling).wait_recv()
        for wi in range(nw):
            hr = geo[wi][0]
            sent = rows_of(o_refs[wi], mine, c * hr, hr)
            for k in range(6):
                copy(wi, k, sent, sent, sibling).wait_send()
            stage_out[wi].wait()

    return pl.pallas_call(
        body, in_specs=[ANY] * nw, out_specs=[ANY] * nw,
        out_shape=[jax.ShapeDtypeStruct((4, *s.shape), t) for s, t in zip(shards, dtypes)],
        scratch_shapes=[pltpu.SemaphoreType.DMA((nw, 6)), pltpu.SemaphoreType.DMA((nw, 6)),
                        pltpu.SemaphoreType.DMA((nw,)), pltpu.SemaphoreType.DMA((nw,))]
        + [pltpu.VMEM(s.shape, s.dtype) for s in shards] + [pltpu.VMEM(s.shape, t) for s, t in zip(shards, dtypes)],
        name="gather_weights", compiler_params=_params(vmem_mb=40),
    )(*shards)


HBM = pl.BlockSpec(memory_space=pltpu.HBM)
SEM = pl.BlockSpec(memory_space=pltpu.SEMAPHORE)
EFFECT = pltpu.SideEffectType.DATAFLOW_SIDE_EFFECTING


def _cast_place(w, place, dtype, after, name):
    rows, cols = w.shape
    tr = _row_tile(rows, cols, _align_of(dtype))

    def body(p_ref, w_ref, after_ref, o_ref):
        del p_ref, after_ref
        o_ref[...] = w_ref[...].astype(dtype)

    grid_spec = pltpu.PrefetchScalarGridSpec(
        num_scalar_prefetch=1, grid=(rows // tr,),
        in_specs=[pl.BlockSpec((tr, cols), lambda i, p: (i, 0)), ANY],
        out_specs=pl.BlockSpec((None, tr, cols), lambda i, p: (p[1], i, 0)))
    return pl.pallas_call(body, grid_spec=grid_spec, out_shape=jax.ShapeDtypeStruct((4, rows, cols), dtype),
                          name=name, compiler_params=_params(("parallel",)))(place, w, after)


def _split_start(name, arrays, n_pairs, issue):
    n = len(arrays)

    def body(*refs):
        issue(refs[:n], refs[n:n + n_pairs], refs[n + n_pairs:n + 2 * n_pairs])
        token = refs[2 * n + 2 * n_pairs]
        token[...] = jnp.zeros_like(token)

    dma = pltpu.SemaphoreType.DMA(())
    outs = pl.pallas_call(
        body, name=name,
        out_shape=[dma] * (2 * n_pairs) + [pltpu.HBM(t.shape, t.dtype) for t in arrays]
        + [jax.ShapeDtypeStruct((SUBLANES, LANES), F32)],
        in_specs=[HBM] * n, out_specs=[SEM] * (2 * n_pairs) + [HBM] * n + [pl.BlockSpec(memory_space=pltpu.VMEM)],
        input_output_aliases={a: 2 * n_pairs + a for a in range(n)},
        compiler_params=pltpu.CompilerParams(has_side_effects=EFFECT),
    )(*[pltpu.with_memory_space_constraint(t, pltpu.HBM) for t in arrays])
    return outs[:n_pairs], outs[n_pairs:2 * n_pairs], outs[2 * n_pairs:2 * n_pairs + n], outs[-1]


def _split_wait(name, send_sems, recv_sems, flying, sizes, after):
    n, n_pairs = len(flying), len(send_sems)

    def body(*refs):
        x, y, c, _ = _place()
        for k, ref in enumerate(sizes(refs[:n])):
            cp = _remote(ref, ref, refs[n + k], refs[n + n_pairs + k], (x, y, 1 - c))
            cp.wait_send()
            cp.wait_recv()

    return pl.pallas_call(
        body, name=name, out_shape=[pltpu.HBM(t.shape, t.dtype) for t in flying],
        in_specs=[HBM] * n + [SEM] * (2 * n_pairs) + [ANY], out_specs=[HBM] * n,
        input_output_aliases={a: a for a in range(n)},
        compiler_params=pltpu.CompilerParams(has_side_effects=EFFECT),
    )(*flying, *send_sems, *recv_sems, after)


def _spread_start(lands, name):
    def issue(land_refs, send_sems, recv_sems):
        x, y, c, chips = _place()
        mine = _chip_index(x, y)
        for a, land in enumerate(land_refs):
            _, rows, cols = land.shape
            hr = rows // 2
            row_bytes = cols * jnp.dtype(land.dtype).itemsize
            for r0, nr in _row_chunks(hr, row_bytes, _align_of(land.dtype)):
                piece = land.at[mine, pl.ds(c * hr + r0, nr), :]
                for chip in chips:
                    for core in (0, 1):
                        _remote(piece, piece, send_sems[a], recv_sems[a], (*chip, core)).start()

    return _split_start(name, lands, len(lands), issue)


def _spread_wait(send_sems, recv_sems, flying, after, name):
    return _split_wait(name, send_sems, recv_sems, flying, lambda refs: [r.at[pl.ds(0, 3)] for r in refs], after)


def _pair_start(grads):
    n = len(grads)
    zones = [lax.empty((4, g.shape[1] // 2, g.shape[2]), F32) for g in grads]

    def issue(refs, send_sems, recv_sems):
        x, y, c, _ = _place()
        for a in range(n):
            g_ref, z_ref = refs[a], refs[n + a]
            _, rows, cols = g_ref.shape
            hr = rows // 2
            for k in range(4):
                for r0, nr in _row_chunks(hr, cols * 4, SUBLANES):
                    _remote(g_ref.at[k, pl.ds((1 - c) * hr + r0, nr), :], z_ref.at[k, pl.ds(r0, nr), :],
                            send_sems[a], recv_sems[a], (x, y, 1 - c)).start()

    return _split_start("pair_start", list(grads) + zones, n, issue)


def _pair_wait(send_sems, recv_sems, flying, after):
    n = len(flying) // 2
    out = _split_wait("pair_wait", send_sems, recv_sems, flying, lambda refs: list(refs[n:]), after)
    return out[:n], out[n:]


def _chip_start(sums):
    n = len(sums)
    zones = [lax.empty((3, *s.shape[1:]), s.dtype) for s in sums]

    def issue(refs, send_sems, recv_sems):
        x, y, c, chips = _place()
        for a in range(n):
            s_ref, z_ref = refs[a], refs[n + a]
            _, rows, cols = s_ref.shape
            row_bytes = cols * jnp.dtype(s_ref.dtype).itemsize
            for r0, nr in _row_chunks(rows, row_bytes, _align_of(s_ref.dtype)):
                for j, chip in enumerate(chips):
                    _remote(s_ref.at[_chip_index(*chip), pl.ds(r0, nr), :], z_ref.at[j, pl.ds(r0, nr), :],
                            send_sems[a], recv_sems[a], (*chip, c)).start()

    return _split_start("chip_start", list(sums) + zones, n, issue)


def _chip_wait(send_sems, recv_sems, flying, after):
    n = len(flying) // 2
    return _split_wait("chip_wait", send_sems, recv_sems, flying, lambda refs: list(refs[n:]), after)[n:]


def _pair_exchange(grads):
    na = len(grads)

    def body(*refs):
        g_refs, o_refs = refs[:na], refs[na:2 * na]
        send_sems, recv_sems = refs[2 * na:]
        x, y, c, _ = _place()
        sibling = (x, y, 1 - c)
        for ai in range(na):
            _, rows, cols = g_refs[ai].shape
            hr = rows // 2
            for k in range(4):
                for r0, nr in _row_chunks(hr, cols * 4, SUBLANES):
                    _remote(g_refs[ai].at[k, pl.ds((1 - c) * hr + r0, nr), :], o_refs[ai].at[k, pl.ds(r0, nr), :],
                            send_sems.at[ai], recv_sems.at[ai], sibling).start()
        for ai in range(na):
            _remote(o_refs[ai], o_refs[ai], send_sems.at[ai], recv_sems.at[ai], sibling).wait()

    return pl.pallas_call(
        body, in_specs=[ANY] * na, out_specs=[ANY] * na,
        out_shape=[jax.ShapeDtypeStruct((4, g.shape[1] // 2, g.shape[2]), F32) for g in grads],
        scratch_shapes=[pltpu.SemaphoreType.DMA((na,)), pltpu.SemaphoreType.DMA((na,))],
        name="pair_exchange",
    )(*grads)


def _row_tile(rows, cols, align, elems=256 * 1024):
    best = align
    for cand in range(align, rows + 1, align):
        if rows % cand == 0 and cand * cols <= elems:
            best = cand
    return best


def _pair_sum(g, got, place, transit, name):
    _, rows, cols = g.shape
    hr = rows // 2
    tr = _row_tile(hr, cols, _align_of(transit), 512 * 1024)
    nt = hr // tr

    def body(p_ref, g_ref, r_ref, s_ref, own_ref):
        total = g_ref[...] + r_ref[...]
        s_ref[...] = total.astype(transit)

        @pl.when(pl.program_id(1) == p_ref[1])
        def _():
            own_ref[...] = total

    grid_spec = pltpu.PrefetchScalarGridSpec(
        num_scalar_prefetch=1, grid=(nt, 4),
        in_specs=[pl.BlockSpec((None, tr, cols), lambda i, k, p: (k, p[0] * nt + i, 0)),
                  pl.BlockSpec((None, tr, cols), lambda i, k, p: (k, i, 0))],
        out_specs=[pl.BlockSpec((None, tr, cols), lambda i, k, p: (k, i, 0)),
                   pl.BlockSpec((tr, cols), lambda i, k, p: (i, 0))])
    return pl.pallas_call(
        body, grid_spec=grid_spec,
        out_shape=[jax.ShapeDtypeStruct((4, hr, cols), transit), jax.ShapeDtypeStruct((hr, cols), F32)],
        name=name, compiler_params=_params(("parallel", "arbitrary")),
    )(place, g, got)


def _chip_exchange(sums):
    na = len(sums)

    def body(*refs):
        s_refs, o_refs = refs[:na], refs[na:2 * na]
        send_sems, recv_sems = refs[2 * na:]
        x, y, c, chips = _place()
        for ai in range(na):
            _, rows, cols = s_refs[ai].shape
            row_bytes = cols * jnp.dtype(s_refs[ai].dtype).itemsize
            for r0, nr in _row_chunks(rows, row_bytes, _align_of(s_refs[ai].dtype)):
                for j, chip in enumerate(chips):
                    _remote(s_refs[ai].at[_chip_index(*chip), pl.ds(r0, nr), :], o_refs[ai].at[j, pl.ds(r0, nr), :],
                            send_sems.at[ai, j], recv_sems.at[ai, j], (*chip, c)).start()
        for ai in range(na):
            for j, chip in enumerate(chips):
                _remote(o_refs[ai].at[j], o_refs[ai].at[j], send_sems.at[ai, j], recv_sems.at[ai, j],
                        (*chip, c)).wait()

    return pl.pallas_call(
        body, in_specs=[ANY] * na, out_specs=[ANY] * na,
        out_shape=[jax.ShapeDtypeStruct((3, *s.shape[1:]), s.dtype) for s in sums],
        scratch_shapes=[pltpu.SemaphoreType.DMA((na, 3)), pltpu.SemaphoreType.DMA((na, 3))],
        name="chip_exchange",
    )(*sums)


def _chip_sum(own, landed, name):
    hr, cols = own.shape
    tr = _row_tile(hr, cols, _align_of(landed.dtype))

    def body(o_ref, l_ref, f_ref):
        acc = o_ref[...]
        for j in range(3):
            acc = acc + l_ref[j].astype(F32)
        f_ref[...] = acc

    return pl.pallas_call(
        body, grid=(hr // tr,),
        in_specs=[pl.BlockSpec((tr, cols), lambda i: (i, 0)), pl.BlockSpec((3, tr, cols), lambda i: (0, i, 0))],
        out_specs=pl.BlockSpec((tr, cols), lambda i: (i, 0)),
        out_shape=jax.ShapeDtypeStruct((hr, cols), F32), name=name,
        compiler_params=_params(("parallel",)),
    )(own, landed)


def _final_exchange(halves, small):
    nh = len(halves)

    def body(*refs):
        h_refs, s_ref = refs[:nh], refs[nh]
        o_refs, so_ref = refs[nh + 1:2 * nh + 1], refs[2 * nh + 1]
        send_sems, recv_sems, local_sem, ssend_sems, srecv_sems = refs[2 * nh + 2:]
        x, y, c, _ = _place()
        me = 4 * x + 2 * y + c
        sibling = (x, y, 1 - c)
        for hi in range(nh):
            hr, cols = h_refs[hi].shape
            for r0, nr in _row_chunks(hr, cols * 4, SUBLANES):
                _remote(h_refs[hi].at[pl.ds(r0, nr), :], o_refs[hi].at[pl.ds(r0, nr), :],
                        send_sems.at[hi], recv_sems.at[hi], sibling).start()
        small_cps = [pltpu.make_async_copy(s_ref, so_ref.at[me], local_sem)]
        for r in range(1, 8):
            fx, fy, fc = (r >> 2) & 1, (r >> 1) & 1, r & 1
            peer = (1 - x if fx else x, 1 - y if fy else y, 1 - c if fc else c)
            small_cps.append(_remote(s_ref, so_ref.at[me], ssend_sems.at[r - 1], srecv_sems.at[r - 1], peer))
        for cp in small_cps:
            cp.start()
        for hi in range(nh):
            _remote(h_refs[hi], o_refs[hi], send_sems.at[hi], recv_sems.at[hi], sibling).wait()
        for cp in small_cps:
            cp.wait()

    return pl.pallas_call(
        body, in_specs=[ANY] * (nh + 1), out_specs=[ANY] * (nh + 1),
        out_shape=[jax.ShapeDtypeStruct(h.shape, F32) for h in halves]
        + [jax.ShapeDtypeStruct((8, *small.shape), F32)],
        scratch_shapes=[pltpu.SemaphoreType.DMA((nh,)), pltpu.SemaphoreType.DMA((nh,)),
                        pltpu.SemaphoreType.DMA, pltpu.SemaphoreType.DMA((7,)), pltpu.SemaphoreType.DMA((7,))],
        name="final_exchange",
    )(*halves, small)


def _adamw_halves(w, own, other, m, v, place, name):
    r, c = w.shape
    hr = r // 2
    tr = _row_tile(hr, c, SUBLANES, 384 * 1024)
    nt = hr // tr
    c1 = 1.0 - ADAM_B1 ** ADAM_STEP
    c2 = 1.0 - ADAM_B2 ** ADAM_STEP

    def body(p_ref, w_ref, own_ref, other_ref, m_ref, v_ref, g_ref, d_ref, nm_ref, nv_ref):
        mine = pl.program_id(0) // nt == p_ref[0]
        gv = jnp.where(mine, own_ref[...], other_ref[...])
        nm = ADAM_B1 * m_ref[...] + (1.0 - ADAM_B1) * gv
        nv = ADAM_B2 * v_ref[...] + (1.0 - ADAM_B2) * (gv * gv)
        g_ref[...] = gv
        d_ref[...] = -ADAM_LR * ((nm / c1) / (jnp.sqrt(nv / c2) + ADAM_EPS) + ADAM_WD * w_ref[...])
        nm_ref[...] = nm
        nv_ref[...] = nv

    full = pl.BlockSpec((tr, c), lambda i, p: (i, 0))
    own_half = pl.BlockSpec((tr, c), lambda i, p: (jnp.where(i // nt == p[0], i % nt, 0), 0))
    other_half = pl.BlockSpec((tr, c), lambda i, p: (jnp.where(i // nt == p[0], 0, i % nt), 0))
    out = jax.ShapeDtypeStruct((r, c), F32)
    grid_spec = pltpu.PrefetchScalarGridSpec(num_scalar_prefetch=1, grid=(2 * nt,),
                                             in_specs=[full, own_half, other_half, full, full],
                                             out_specs=[full] * 4)
    return pl.pallas_call(body, grid_spec=grid_spec, out_shape=[out] * 4, name=name,
                          compiler_params=_params(("parallel",)))(place, w, own, other, m, v)


def _adamw_many(ws, gs, ms, vs, name):
    n = len(ws)
    c1 = 1.0 - ADAM_B1 ** ADAM_STEP
    c2 = 1.0 - ADAM_B2 ** ADAM_STEP

    def body(*refs):
        w_refs, g_refs, m_refs, v_refs = (refs[k * n:(k + 1) * n] for k in range(4))
        d_refs, nm_refs, nv_refs = (refs[(4 + k) * n:(5 + k) * n] for k in range(3))
        for i in range(n):
            gv = g_refs[i][...]
            nm = ADAM_B1 * m_refs[i][...] + (1.0 - ADAM_B1) * gv
            nv = ADAM_B2 * v_refs[i][...] + (1.0 - ADAM_B2) * (gv * gv)
            d_refs[i][...] = -ADAM_LR * ((nm / c1) / (jnp.sqrt(nv / c2) + ADAM_EPS) + ADAM_WD * w_refs[i][...])
            nm_refs[i][...] = nm
            nv_refs[i][...] = nv

    vmem = pl.BlockSpec(memory_space=pltpu.VMEM)
    shapes = [jax.ShapeDtypeStruct(t.shape, F32) for t in ws]
    outs = pl.pallas_call(body, in_specs=[vmem] * (4 * n), out_specs=[vmem] * (3 * n), out_shape=shapes * 3,
                          name=name, compiler_params=_params(vmem_mb=56))(*ws, *gs, *ms, *vs)
    return outs[:n], outs[n:2 * n], outs[2 * n:]


BIG = ("w_in", "w_glu", "w_out", "w_up", "w_down")
WEIGHTS = ("norm_mix_g", "w_in", "a_re", "a_im", "log_step", "b_re", "b_im", "c_re", "c_im", "d_skip", "w_glu",
           "sink", "norm_attn_g", "norm_ssm_g", "w_out", "norm_ffn_g", "w_up", "conv_w", "conv_b", "w_down",
           "norm_final_g")
SMALL = ("norm_mix_g", "a_re", "a_im", "log_step", "b_re", "b_im", "c_re", "c_im", "d_skip", "sink",
         "norm_attn_g", "norm_ssm_g", "norm_ffn_g", "conv_w", "conv_b", "norm_final_g")
SMALL_ROWS = 48
N_DEV = 8


def _tile_rows(size):
    return -(-size // (SUBLANES * D_MODEL)) * SUBLANES


def _by_owner(name, g):
    if name == "w_up":
        return g
    return g.reshape(4, g.shape[0] // 4, g.shape[1])


def _view(name, t):
    if name == "w_in":
        return jnp.swapaxes(t[0], 0, 1)
    if name in ("b_re", "b_im"):
        return jnp.swapaxes(t, -1, -2)
    return t


def _unview(name, t):
    if name == "w_in":
        return jnp.swapaxes(t, 0, 1)[None]
    if name in ("b_re", "b_im"):
        return jnp.swapaxes(t, -1, -2)
    return t


def kernel(x, norm_mix_g, w_in, a_re, a_im, log_step, b_re, b_im, c_re, c_im, d_skip, w_glu, sink, norm_attn_g, norm_ssm_g, w_out, norm_ffn_g, w_up, conv_w, conv_b, w_down, norm_final_g, loss_target, m_norm_mix_g, m_w_in, m_a_re, m_a_im, m_log_step, m_b_re, m_b_im, m_c_re, m_c_im, m_d_skip, m_w_glu, m_sink, m_norm_attn_g, m_norm_ssm_g, m_w_out, m_norm_ffn_g, m_w_up, m_conv_w, m_conv_b, m_w_down, m_norm_final_g, v_norm_mix_g, v_w_in, v_a_re, v_a_im, v_log_step, v_b_re, v_b_im, v_c_re, v_c_im, v_d_skip, v_w_glu, v_sink, v_norm_attn_g, v_norm_ssm_g, v_w_out, v_norm_ffn_g, v_w_up, v_conv_w, v_conv_b, v_w_down, v_norm_final_g):
    given = dict(locals())
    w = {n: given[n] for n in WEIGHTS}
    m = {n: given["m_" + n] for n in WEIGHTS}
    v = {n: given["v_" + n] for n in WEIGHTS}
    xy = 2 * lax.axis_index("x") + lax.axis_index("y")

    core = lax.axis_index("c")
    place = jnp.stack([core, xy]).astype(jnp.int32)

    conv_rows = jnp.pad(w["conv_w"][0], ((0, 2 * SUBLANES - 3), (0, 0)))
    rows = lambda t: t.reshape(4 * t.shape[1], t.shape[2])
    (w_in_all,) = _gather_weights([_view("w_in", w["w_in"])], [BF16])
    wb = {"w_in": rows(w_in_all)}
    mixer = [_cast_place(w[n][0], place, BF16, w_in_all, "cast_" + n) for n in ("w_glu", "w_out")]
    mixer.append(_cast_place(conv_rows, place, F32, w_in_all, "cast_conv_w"))
    *mixer_flight, mixer_token = _spread_start(mixer, "spread_mixer_start")
    late = ("w_up", "w_down")
    *late_flight, token = _spread_start(
        [_cast_place(w[n][0], place, BF16, mixer_token, "cast_" + n) for n in late], "spread_ffn_start")

    def mixer_weights(after):
        w_glu4, w_out4, conv4 = _spread_wait(*mixer_flight, after, "spread_mixer_wait")
        return {"w_glu": rows(w_glu4), "w_out": rows(w_out4),
                "conv_w": conv4[:, :3].transpose(1, 0, 2).reshape(3, 2 * D_FF)}

    def late_weights(after):
        w_up4, w_down4 = _spread_wait(*late_flight, after, "spread_ffn_wait")
        return {"w_up": w_up4, "w_down": rows(w_down4)}

    sp = {n: w[n][0] for n in ("a_re", "a_im", "log_step", "b_re", "b_im", "c_re", "c_im", "d_skip",
                               "norm_mix_g", "norm_attn_g", "norm_ssm_g", "norm_ffn_g", "sink", "conv_b")}
    for n in ("norm_mix_g", "norm_attn_g", "norm_ssm_g", "norm_ffn_g", "sink", "conv_b"):
        sp[n] = sp[n].reshape(1, -1)
    sp["norm_mix_g"] = sp["norm_mix_g"] + token[:1, :1]
    sp["norm_final_g"] = w["norm_final_g"]
    early, tail = late + ("w_out",), ("w_in", "w_glu")
    flight = {}

    def grads_ready(dw_up, dw_down, dw_out):
        *flight["pair"], token = _pair_start([dw_up, _by_owner("w_down", dw_down), _by_owner("w_out", dw_out)])
        return token[:1, :1]

    def grads_next(after):
        mine, got = _pair_wait(*flight["pair"], after)
        sums, flight["own"] = zip(*[_pair_sum(a, b, place, BF16, "pair_sum_" + n) for n, a, b in zip(early, mine, got)])
        *flight["chip"], token = _chip_start(list(sums))
        return token[:1, :1]

    loss, grad_x, g = _local_step(x[0], loss_target[0], wb, sp, mixer_weights, late_weights, grads_ready,
                                  grads_next)

    def as_rows(t):
        rows = _tile_rows(t.size)
        return jnp.pad(t.reshape(-1), (0, rows * D_MODEL - t.size)).reshape(rows, D_MODEL)

    pieces = [as_rows(g[n]) for n in SMALL] + [as_rows(loss)]
    spare = N_DEV * SMALL_ROWS - sum(p.shape[0] for p in pieces)
    small = jnp.concatenate(pieces + [jnp.zeros((spare, D_MODEL), F32)]).reshape(4, 2 * SMALL_ROWS, D_MODEL)
    by_owner = [_by_owner(n, g[n]) for n in tail] + [small]
    got = _pair_exchange(by_owner)
    transit = [BF16] * len(tail) + [F32]
    chip_sums, own_sums = zip(*[_pair_sum(a, b, place, t, "pair_sum_" + n)
                                for n, a, b, t in zip(tail + ("small",), by_owner, got, transit)])
    landed = _chip_exchange(list(chip_sums))
    halves = {n: _chip_sum(o, t, "chip_sum_" + n) for n, o, t in zip(tail + ("small",), own_sums, landed)}
    early_landed = _chip_wait(*flight["chip"], grad_x)
    for n, o, t in zip(early, flight["own"], early_landed):
        halves[n] = _chip_sum(o, t, "chip_sum_" + n)
    *others, small_all = _final_exchange([halves[n] for n in BIG], halves["small"])
    small_all = small_all.reshape(N_DEV * SMALL_ROWS, D_MODEL)
    grads, row = {}, 0
    for n in SMALL:
        shape = (3, 4 * w[n].shape[-1]) if n == "conv_w" else w[n].shape[1:] if n != "norm_final_g" else w[n].shape
        size = math.prod(shape)
        grads[n] = small_all[row:row + _tile_rows(size)].reshape(-1)[:size].reshape(shape)
        row += _tile_rows(size)
    loss = small_all[row, 0]
    cw = w["conv_w"].shape[-1]
    grads["conv_w"] = lax.dynamic_slice_in_dim(grads["conv_w"], xy * cw, cw, axis=1)
    grads = {n: _view(n, grads[n].reshape(w[n].shape)) for n in SMALL}
    wv, mv, vv = ({n: _view(n, t[n]) for n in WEIGHTS} for t in (w, m, v))

    delta, new_m, new_v = {}, {}, {}
    for n, other in zip(BIG, others):
        two_d = lambda t: t.reshape(t.shape[-2:])
        grads[n], delta[n], new_m[n], new_v[n] = _adamw_halves(
            two_d(wv[n]), halves[n], other, two_d(mv[n]), two_d(vv[n]), place, "adamw_" + n)
    for group, name in ((("b_re", "b_im"), "adamw_b"), (tuple(n for n in SMALL if n not in ("b_re", "b_im")), "adamw_small")):
        row = lambda t: t.reshape(1, -1) if t.ndim == 1 else t
        d_, m_, v_ = _adamw_many(*[[row(t[n]) for n in group] for t in (wv, grads, mv, vv)], name)
        for n, dn, mn, vn in zip(group, d_, m_, v_):
            delta[n], new_m[n], new_v[n] = (t.reshape(wv[n].shape) for t in (dn, mn, vn))
    natural = lambda t: [_unview(n, t[n].reshape(wv[n].shape)) for n in WEIGHTS]
    return (loss, grad_x[None], *natural(grads), *natural(delta), *natural(new_m), *natural(new_v))
```

```python
import functools
import math

import jax
import jax.numpy as jnp
import numpy as np
from jax import lax
from jax.experimental import pallas as pl
from jax.experimental.pallas import tpu as pltpu

F32 = jnp.float32
BF16 = jnp.bfloat16

D_MODEL = 1024
N_Q_HEADS = 8
N_KV_HEADS = 2
HEAD_DIM = 64
ATTN_WIDTH = 512
KV_WIDTH = 128
QKV_WIDTH = ATTN_WIDTH + 2 * KV_WIDTH
WINDOW = 128
BLOCK = 128
ROPE_DIM = 16
ROPE_THETA = 500000.0
SCORE_SCALE = HEAD_DIM ** -0.5
SSM_WIDTH = 512
SSM_GROUP = 16
N_SSM_GROUPS = 32
SSM_STATE = 64
IN_WIDTH = 1280
D_FF = 2816
EPS = 1e-6
ADAM_LR = 0.001
ADAM_B1 = 0.9
ADAM_B2 = 0.999
ADAM_EPS = 1e-08
ADAM_WD = 0.01
ADAM_STEP = 10

VMEM_BYTES_V7X = 64 * 1024 * 1024
SUBLANES = 8
LANES = 128
SSM_CB = 4
SSM_CH = 128
SSM_ST = 512
N_SEG = SUBLANES

NN = (((1,), (0,)), ((), ()))
NT = (((1,), (1,)), ((), ()))
TN = (((0,), (0,)), ((), ()))


def _params(sem=None, vmem_mb=48):
    limit = vmem_mb * 1024 * 1024
    assert limit < VMEM_BYTES_V7X
    return pltpu.CompilerParams(dimension_semantics=sem, vmem_limit_bytes=limit)


def _dg(a, b, dims):
    return lax.dot_general(a, b, dims, preferred_element_type=F32)


def _sigmoid(x):
    return 1.0 / (1.0 + jnp.exp(-x))


_SQRT_HALF = 0.7071067811865476
_INV_SQRT_2PI = 0.3989422804014327


def _gelu(x):
    return 0.5 * x * (1.0 + lax.erf(x * _SQRT_HALF))


def _gelu_grad(x):
    return 0.5 * (1.0 + lax.erf(x * _SQRT_HALF)) + x * (_INV_SQRT_2PI * jnp.exp(-0.5 * x * x))


def _mm_tn(a, b, tm, tn, name):
    k, m = a.shape
    n = b.shape[1]

    def body(a_ref, b_ref, o_ref):
        o_ref[...] = _dg(a_ref[...], b_ref[...], TN)

    return pl.pallas_call(
        body, grid=(m // tm, n // tn),
        in_specs=[pl.BlockSpec((k, tm), lambda i, j: (0, i)), pl.BlockSpec((k, tn), lambda i, j: (0, j))],
        out_specs=pl.BlockSpec((tm, tn), lambda i, j: (i, j)),
        out_shape=jax.ShapeDtypeStruct((m, n), F32), name=name,
        compiler_params=_params(("parallel", "parallel")),
    )(a, b)


def _mm_nn_cols(a, b4, tm, name):
    m, k = a.shape
    s, _, n = b4.shape

    def body(a_ref, b_ref, o_ref):
        o_ref[...] = _dg(a_ref[...], b_ref[...], NN)

    return pl.pallas_call(
        body, grid=(m // tm, s),
        in_specs=[pl.BlockSpec((tm, k), lambda i, j: (i, 0)), pl.BlockSpec((None, k, n), lambda i, j: (j, 0, 0))],
        out_specs=pl.BlockSpec((tm, n), lambda i, j: (i, j)),
        out_shape=jax.ShapeDtypeStruct((m, s * n), F32), name=name,
        compiler_params=_params(("parallel", "parallel")),
    )(a, b4)


def _mm_tn_cols(a, b2, s, tm, name):
    k, m = a.shape
    h, _, wide = b2.shape
    per = s // h
    n = wide // per

    def body(a_ref, b_ref, o_ref):
        o_ref[...] = _dg(a_ref[...], b_ref[...], TN)

    return pl.pallas_call(
        body, grid=(s, m // tm),
        in_specs=[pl.BlockSpec((k, tm), lambda j, i: (0, i)),
                  pl.BlockSpec((None, k, n), lambda j, i: (j // per, 0, j % per))],
        out_specs=pl.BlockSpec((None, tm, n), lambda j, i: (j, i, 0)),
        out_shape=jax.ShapeDtypeStruct((s, m, n), F32), name=name,
        compiler_params=_params(("parallel", "parallel")),
    )(a, b2)


TM_EW = 512


def _rms_bwd_vals(xv, gv, dy):
    r = lax.rsqrt(jnp.mean(xv * xv, axis=-1, keepdims=True) + EPS)
    xh = xv * r
    dxh = dy * gv
    dx = r * (dxh - xh * jnp.mean(dxh * xh, axis=-1, keepdims=True))
    return dx, dy * xh


TM_FUSED = 512
TM_LOSS = 256


def _rms_vals(xv, gv):
    return xv * lax.rsqrt(jnp.mean(xv * xv, axis=-1, keepdims=True) + EPS) * gv


def _rope_blocks(src, dst, c, lo, hi):
    nq = ATTN_WIDTH // LANES
    for blk in range(nq + 1):
        t = src[:, blk * LANES:(blk + 1) * LANES]
        rot = t * c + pltpu.roll(t, LANES - 8, 1) * lo + pltpu.roll(t, 8, 1) * hi
        dst[:, blk * LANES:(blk + 1) * LANES] = (rot * SCORE_SCALE if blk < nq else rot).astype(BF16)
    dst[:, (nq + 1) * LANES:] = src[:, (nq + 1) * LANES:].astype(BF16)


def _rms_mm_rope(x, g, wt, tabs, name):
    l, d = x.shape
    n = wt.shape[0]

    def body(x_ref, g_ref, w_ref, c_ref, lo_ref, hi_ref, h_ref, qkv_ref, u_ref):
        h = _rms_vals(x_ref[...], g_ref[...]).astype(BF16)
        h_ref[...] = h
        out = _dg(h, w_ref[...], NT)
        _rope_blocks(out[:, :QKV_WIDTH], qkv_ref, c_ref[...], lo_ref[...], hi_ref[...])
        u_ref[...] = out[:, QKV_WIDTH:]

    row = lambda width: pl.BlockSpec((TM_FUSED, width), lambda i: (i, 0))
    return pl.pallas_call(
        body, grid=(l // TM_FUSED,),
        in_specs=[row(d), pl.BlockSpec((1, d), lambda i: (0, 0)), pl.BlockSpec((n, d), lambda i: (0, 0)),
                  row(LANES), row(LANES), row(LANES)],
        out_specs=[row(d), row(QKV_WIDTH), row(n - QKV_WIDTH)],
        out_shape=[jax.ShapeDtypeStruct((l, d), BF16), jax.ShapeDtypeStruct((l, QKV_WIDTH), BF16),
                   jax.ShapeDtypeStruct((l, n - QKV_WIDTH), F32)],
        name=name, compiler_params=_params(("parallel",)),
    )(x, g, wt, *tabs)


def _mix_mm_res_rms(attn, ys, g_attn, g_ssm, b, res, g, name):
    l, w = attn.shape
    d = b.shape[1]

    def body(a_ref, y_ref, ga_ref, gs_ref, b_ref, r_ref, g_ref, m_ref, x_ref, h_ref):
        m_ref[:, :w] = _rms_vals(a_ref[...], ga_ref[...]).astype(BF16)
        m_ref[:, w:] = _rms_vals(y_ref[...], gs_ref[...]).astype(BF16)
        xv = r_ref[...] + _dg(m_ref[...], b_ref[...], NN)
        x_ref[...] = xv
        h_ref[...] = _rms_vals(xv, g_ref[...]).astype(BF16)

    row = lambda width: pl.BlockSpec((TM_FUSED, width), lambda i: (i, 0))
    vec = lambda width: pl.BlockSpec((1, width), lambda i: (0, 0))
    return pl.pallas_call(
        body, grid=(l // TM_FUSED,),
        in_specs=[row(w), row(w), vec(w), vec(w), pl.BlockSpec((2 * w, d), lambda i: (0, 0)), row(d), vec(d)],
        out_specs=[row(2 * w), row(d), row(d)],
        out_shape=[jax.ShapeDtypeStruct((l, 2 * w), BF16), jax.ShapeDtypeStruct((l, d), F32),
                   jax.ShapeDtypeStruct((l, d), BF16)],
        name=name, compiler_params=_params(("parallel",)),
    )(attn, ys, g_attn, g_ssm, b, res, g)


def _mm_res_loss(a, b, res, g, target):
    l, k = a.shape
    d = b.shape[1]

    def body(a_ref, b_ref, r_ref, g_ref, t_ref, loss_ref, dx_ref, dxb_ref, dg_ref):
        xv = r_ref[...] + _dg(a_ref[...], b_ref[...], NN)
        gv = g_ref[...]
        r = lax.rsqrt(jnp.mean(xv * xv, axis=-1, keepdims=True) + EPS)
        xh = xv * r
        e = xh * gv - t_ref[...]
        part = jnp.sum(jnp.sum(e * e, axis=1, keepdims=True), axis=0, keepdims=True) * (0.5 / d)
        dy = e * (1.0 / d)
        dxh = dy * gv
        dx = r * (dxh - xh * jnp.mean(dxh * xh, axis=-1, keepdims=True))
        dx_ref[...] = dx
        dxb_ref[...] = dx.astype(BF16)

        @pl.when(pl.program_id(0) == 0)
        def _():
            dg_ref[...] = jnp.zeros_like(dg_ref)
            loss_ref[...] = jnp.zeros_like(loss_ref)

        dg_ref[...] += jnp.sum(dy * xh, axis=0, keepdims=True)
        loss_ref[...] += part

    row = lambda width: pl.BlockSpec((TM_LOSS, width), lambda i: (i, 0))
    vec = pl.BlockSpec((1, d), lambda i: (0, 0))
    return pl.pallas_call(
        body, grid=(l // TM_LOSS,),
        in_specs=[row(k), pl.BlockSpec((k, d), lambda i: (0, 0)), row(d), vec, row(d)],
        out_specs=[pl.BlockSpec((1, 1), lambda i: (0, 0)), row(d), row(d), vec],
        out_shape=[jax.ShapeDtypeStruct((1, 1), F32), jax.ShapeDtypeStruct((l, d), F32),
                   jax.ShapeDtypeStruct((l, d), BF16), jax.ShapeDtypeStruct((1, d), F32)],
        name="mm_down_loss", compiler_params=_params(("arbitrary",)),
    )(a, b, res, g, target)


def _mm_rms_bwd(a, b, a_spec, b_spec, matmul, x, g, res, name):
    l, d = x.shape

    def body(a_ref, b_ref, x_ref, g_ref, res_ref, dx_ref, dxb_ref, dg_ref):
        dx, dgr = _rms_bwd_vals(x_ref[...], g_ref[...], matmul(a_ref, b_ref))
        dx = dx + res_ref[...]
        dx_ref[...] = dx
        dxb_ref[...] = dx.astype(BF16)

        @pl.when(pl.program_id(0) == 0)
        def _():
            dg_ref[...] = jnp.zeros_like(dg_ref)

        dg_ref[...] += jnp.sum(dgr, axis=0, keepdims=True)

    row = pl.BlockSpec((TM_FUSED, d), lambda i: (i, 0))
    vec = pl.BlockSpec((1, d), lambda i: (0, 0))
    return pl.pallas_call(
        body, grid=(l // TM_FUSED,), in_specs=[a_spec, b_spec, row, vec, row], out_specs=[row, row, vec],
        out_shape=[jax.ShapeDtypeStruct((l, d), F32), jax.ShapeDtypeStruct((l, d), BF16),
                   jax.ShapeDtypeStruct((1, d), F32)],
        name=name, compiler_params=_params(("arbitrary",)),
    )(a, b, x, g, res)


def _mm_nn_rms_bwd(a, b, x, g, res, name):
    return _mm_rms_bwd(a, b, pl.BlockSpec((TM_FUSED, a.shape[1]), lambda i: (i, 0)),
                       pl.BlockSpec(b.shape, lambda i: (0, 0)),
                       lambda a_ref, b_ref: _dg(a_ref[...], b_ref[...], NN), x, g, res, name)


def _mm_cols_rms_bwd(a2, b4, x, g, res, name):
    h, _, wide = a2.shape
    s, _, n = b4.shape
    per = s // h

    def matmul(a_ref, b_ref):
        acc = None
        for j in range(s):
            part = _dg(a_ref[j // per, :, (j % per) * n:(j % per + 1) * n], b_ref[j], NT)
            acc = part if acc is None else acc + part
        return acc

    return _mm_rms_bwd(a2, b4, pl.BlockSpec((h, TM_FUSED, wide), lambda i: (0, i, 0)),
                       pl.BlockSpec(b4.shape, lambda i: (0, 0, 0), pipeline_mode=pl.Buffered(1)),
                       matmul, x, g, res, name)


def _mm_mix_bwd(dx, b, attn, ys, g_attn, g_ssm, name):
    l, w = attn.shape
    d = dx.shape[1]

    def body(dx_ref, b_ref, a_ref, y_ref, ga_ref, gs_ref, da_ref, dy_ref, dga_ref, dgs_ref):
        @pl.when(pl.program_id(0) == 0)
        def _():
            dga_ref[...] = jnp.zeros_like(dga_ref)
            dgs_ref[...] = jnp.zeros_like(dgs_ref)

        dm = _dg(dx_ref[...], b_ref[...], NT)
        for src, gr, off, dst, dgr in ((a_ref, ga_ref, 0, da_ref, dga_ref), (y_ref, gs_ref, w, dy_ref, dgs_ref)):
            dxv, dg_rows = _rms_bwd_vals(src[...], gr[...], dm[:, off:off + w])
            dst[...] = dxv
            dgr[...] += jnp.sum(dg_rows, axis=0, keepdims=True)

    row = lambda width: pl.BlockSpec((TM_FUSED, width), lambda i: (i, 0))
    vec = pl.BlockSpec((1, w), lambda i: (0, 0))
    return pl.pallas_call(
        body, grid=(l // TM_FUSED,),
        in_specs=[row(d), pl.BlockSpec((2 * w, d), lambda i: (0, 0)), row(w), row(w), vec, vec],
        out_specs=[row(w), row(w), vec, vec],
        out_shape=[jax.ShapeDtypeStruct((l, w), F32), jax.ShapeDtypeStruct((l, w), F32),
                   jax.ShapeDtypeStruct((1, w), F32), jax.ShapeDtypeStruct((1, w), F32)],
        name=name, compiler_params=_params(("arbitrary",)),
    )(dx, b, attn, ys, g_attn, g_ssm)


def _rope_tables(l):
    half = ROPE_DIM // 2
    f32 = np.float32
    inv_freq = np.power(f32(ROPE_THETA), -np.arange(half, dtype=f32) / f32(half))
    ang = np.arange(l, dtype=f32)[:, None] * inv_freq[None, :]
    cos, sin = np.cos(ang), np.sin(ang)
    ones = np.ones((l, HEAD_DIM - ROPE_DIM), f32)
    zeros = np.zeros((l, HEAD_DIM - ROPE_DIM), f32)
    zh = np.zeros((l, half), f32)
    c = np.concatenate([cos, cos, ones], axis=1)
    s_lo = np.concatenate([-sin, zh, zeros], axis=1)
    s_hi = np.concatenate([zh, sin, zeros], axis=1)
    return tuple(jnp.asarray(np.tile(t, (1, LANES // HEAD_DIM)), F32) for t in (c, s_lo, s_hi))


def _rope_bwd(dq, dkv, du_ssm, dpre, d_skip, tabs):
    l = dq.shape[0]
    nq = ATTN_WIDTH // LANES

    def body(dq_ref, dkv_ref, du_ref, dpre_ref, ds_ref, c_ref, lo_ref, hi_ref, o_ref):
        c, lo, hi = c_ref[...], lo_ref[...], hi_ref[...]
        for blk in range(nq + 1):
            t = dq_ref[:, blk * LANES:(blk + 1) * LANES] if blk < nq else dkv_ref[:, :KV_WIDTH]
            g = t * c + pltpu.roll(t * lo, 8, 1) + pltpu.roll(t * hi, LANES - 8, 1)
            o_ref[:, blk * LANES:(blk + 1) * LANES] = g.astype(BF16)
        o_ref[:, (nq + 1) * LANES:QKV_WIDTH] = dkv_ref[:, KV_WIDTH:].astype(BF16)
        o_ref[:, QKV_WIDTH:] = (du_ref[...] + dpre_ref[...] * ds_ref[...]).astype(BF16)

    tab = pl.BlockSpec((TM_EW, LANES), lambda i: (i, 0))
    wide = pl.BlockSpec((TM_EW, SSM_WIDTH), lambda i: (i, 0))
    return pl.pallas_call(
        body, grid=(l // TM_EW,),
        in_specs=[wide, pl.BlockSpec((TM_EW, 2 * KV_WIDTH), lambda i: (i, 0)), wide, wide,
                  pl.BlockSpec((1, SSM_WIDTH), lambda i: (0, 0)), tab, tab, tab],
        out_specs=pl.BlockSpec((TM_EW, IN_WIDTH), lambda i: (i, 0)),
        out_shape=jax.ShapeDtypeStruct((l, IN_WIDTH), BF16), name="rope_bwd",
        compiler_params=_params(("parallel",)),
    )(dq, dkv, du_ssm, dpre, d_skip, *tabs)


_Q_COLS = ATTN_WIDTH // LANES
_NEG = -1e30


def _window_specs(nb, width, col):
    return [
        pl.BlockSpec((BLOCK, width), lambda n: (jnp.maximum(n - 1, 0), col)),
        pl.BlockSpec((BLOCK, width), lambda n: (n, col)),
        pl.BlockSpec((BLOCK, width), lambda n: (jnp.minimum(n + 1, nb - 1), col)),
    ]


def _stacked_sink(sink_ref, heads):
    rid = lax.broadcasted_iota(jnp.int32, (len(heads) * BLOCK, 1), 0)
    sk = jnp.full(rid.shape, sink_ref[0, heads[-1]], F32)
    for g in range(len(heads) - 2, -1, -1):
        sk = jnp.where(rid < (g + 1) * BLOCK, sink_ref[0, heads[g]], sk)
    return sk


def _attn_fwd(qkv, sink):
    l = qkv.shape[0]
    nb = l // BLOCK
    grp = N_Q_HEADS // N_KV_HEADS

    def body(sink_ref, q_ref, k0, k1, k2, v0, v1, v2, o_ref, lse_ref):
        n = pl.program_id(0)
        q = q_ref[...]
        kw = jnp.concatenate([k0[...], k1[...], k2[...]], axis=0)
        vw = jnp.concatenate([v0[...], v1[...], v2[...]], axis=0)
        row = lax.broadcasted_iota(jnp.int32, (grp * BLOCK, 3 * BLOCK), 0)
        col = lax.broadcasted_iota(jnp.int32, (grp * BLOCK, 3 * BLOCK), 1)
        valid = jnp.abs(col - BLOCK - (row & (BLOCK - 1))) <= WINDOW
        valid &= jnp.logical_not((n == 0) & (col < BLOCK))
        valid &= jnp.logical_not((n == nb - 1) & (col >= 2 * BLOCK))
        for hk in range(N_KV_HEADS):
            heads = range(hk * grp, (hk + 1) * grp)
            qs = jnp.concatenate([q[:, h * HEAD_DIM:(h + 1) * HEAD_DIM] for h in heads], axis=0)
            kh = kw[:, hk * HEAD_DIM:(hk + 1) * HEAD_DIM]
            vh = vw[:, hk * HEAD_DIM:(hk + 1) * HEAD_DIM]
            s = jnp.where(valid, _dg(qs, kh, NT), _NEG)
            sk = _stacked_sink(sink_ref, heads)
            m = jnp.maximum(jnp.max(s, axis=1, keepdims=True), sk)
            p = jnp.exp(s - m)
            denom = jnp.sum(p, axis=1, keepdims=True) + jnp.exp(sk - m)
            o = _dg((p / denom).astype(BF16), vh, NN)
            lse = m + jnp.log(denom)
            for g, h in enumerate(heads):
                o_ref[:, h * HEAD_DIM:(h + 1) * HEAD_DIM] = o[g * BLOCK:(g + 1) * BLOCK]
                lse_ref[:, h:h + 1] = lse[g * BLOCK:(g + 1) * BLOCK]

    return pl.pallas_call(
        body, grid=(nb,),
        in_specs=[pl.BlockSpec(memory_space=pltpu.SMEM),
                  pl.BlockSpec((BLOCK, ATTN_WIDTH), lambda n: (n, 0))]
        + _window_specs(nb, KV_WIDTH, _Q_COLS) + _window_specs(nb, KV_WIDTH, _Q_COLS + 1),
        out_specs=[pl.BlockSpec((BLOCK, ATTN_WIDTH), lambda n: (n, 0)),
                   pl.BlockSpec((BLOCK, N_Q_HEADS), lambda n: (n, 0))],
        out_shape=[jax.ShapeDtypeStruct((l, ATTN_WIDTH), F32), jax.ShapeDtypeStruct((l, N_Q_HEADS), F32)],
        name="attn_fwd", compiler_params=_params(("parallel",)),
    )(sink, qkv, qkv, qkv, qkv, qkv, qkv, qkv)


def _attn_bwd(qkv, attn, dattn, lse, sink):
    l = qkv.shape[0]
    nb = l // BLOCK
    grp = N_Q_HEADS // N_KV_HEADS
    win = 3 * BLOCK

    def body(sink_ref, q_ref, k0, k1, k2, v0, v1, v2, o_ref, d_ref, l_ref, dq_ref, dkv_ref, dsink_ref, ring_ref):
        n = pl.program_id(0)

        @pl.when(n == 0)
        def _():
            dsink_ref[...] = jnp.zeros_like(dsink_ref)
            ring_ref[...] = jnp.zeros_like(ring_ref)

        @pl.when(n < nb)
        def _():
            first, last = n == 0, n == nb - 1
            cat = lambda a, b, c: jnp.concatenate([a[...], b[...], c[...]], axis=0)
            q, kw, vw = q_ref[...], cat(k0, k1, k2), cat(v0, v1, v2)
            dov = d_ref[...]
            prod = o_ref[...] * dov
            dob = dov.astype(BF16)
            lse = l_ref[...]
            row = lax.broadcasted_iota(jnp.int32, (grp * BLOCK, win), 0)
            col = lax.broadcasted_iota(jnp.int32, (grp * BLOCK, win), 1)
            valid = jnp.abs(col - BLOCK - (row & (BLOCK - 1))) <= WINDOW
            valid &= jnp.logical_not(first & (col < BLOCK))
            valid &= jnp.logical_not(last & (col >= 2 * BLOCK))

            dsink_parts, dks, dvs = [], [], []
            for hk in range(N_KV_HEADS):
                heads = range(hk * grp, (hk + 1) * grp)
                ksl = slice(hk * HEAD_DIM, (hk + 1) * HEAD_DIM)
                hsl = [slice(h * HEAD_DIM, (h + 1) * HEAD_DIM) for h in heads]
                stack = lambda parts: jnp.concatenate(parts, axis=0)
                qs = stack([q[:, s_] for s_ in hsl])
                dos = stack([dob[:, s_] for s_ in hsl])
                deltas = stack([jnp.sum(prod[:, s_], axis=1, keepdims=True) for s_ in hsl])
                lses = stack([lse[:, h:h + 1] for h in heads])
                kh, vh = kw[:, ksl], vw[:, ksl]
                s = jnp.where(valid, _dg(qs, kh, NT), _NEG)
                p = jnp.exp(s - lses)
                dp = _dg(dos, vh, NT)
                ds = (p * (dp - deltas)).astype(BF16)
                dq = _dg(ds, kh, NN) * SCORE_SCALE
                sink_rows = jnp.exp(_stacked_sink(sink_ref, heads) - lses) * deltas
                for g in range(grp):
                    dq_ref[:, hsl[g]] = dq[g * BLOCK:(g + 1) * BLOCK]
                    dsink_parts.append(jnp.sum(sink_rows[g * BLOCK:(g + 1) * BLOCK], axis=0, keepdims=True))
                dks.append(_dg(ds, qs, TN))
                dvs.append(_dg(p.astype(BF16), dos, TN))
            dsink_ref[...] -= jnp.concatenate(dsink_parts, axis=1)
            part = jnp.concatenate(dks + dvs, axis=1)
            ring_ref[(n + 2) % 3] += part[0:BLOCK]
            ring_ref[n % 3] += part[BLOCK:2 * BLOCK]
            ring_ref[(n + 1) % 3] = part[2 * BLOCK:]

        @pl.when(n >= 1)
        def _():
            dkv_ref[...] = ring_ref[(n + 2) % 3]

    centre = lambda n: jnp.minimum(n, nb - 1)
    window = lambda width, col: [
        pl.BlockSpec((BLOCK, width), lambda n: (jnp.maximum(centre(n) - 1, 0), col)),
        pl.BlockSpec((BLOCK, width), lambda n: (centre(n), col)),
        pl.BlockSpec((BLOCK, width), lambda n: (jnp.minimum(centre(n) + 1, nb - 1), col))]
    own = lambda width: pl.BlockSpec((BLOCK, width), lambda n: (centre(n), 0))
    return pl.pallas_call(
        body, grid=(nb + 1,),
        in_specs=[pl.BlockSpec(memory_space=pltpu.SMEM), own(ATTN_WIDTH)]
        + window(KV_WIDTH, _Q_COLS) + window(KV_WIDTH, _Q_COLS + 1)
        + [own(ATTN_WIDTH), own(ATTN_WIDTH), own(N_Q_HEADS)],
        out_specs=[own(ATTN_WIDTH), pl.BlockSpec((BLOCK, 2 * KV_WIDTH), lambda n: (jnp.maximum(n - 1, 0), 0)),
                   pl.BlockSpec((1, N_Q_HEADS), lambda n: (0, 0))],
        out_shape=[jax.ShapeDtypeStruct((l, ATTN_WIDTH), F32), jax.ShapeDtypeStruct((l, 2 * KV_WIDTH), F32),
                   jax.ShapeDtypeStruct((1, N_Q_HEADS), F32)],
        scratch_shapes=[pltpu.VMEM((3, BLOCK, 2 * KV_WIDTH), F32)],
        name="attn_bwd", compiler_params=_params(("arbitrary",)),
    )(sink, qkv, qkv, qkv, qkv, qkv, qkv, qkv, attn, dattn, lse)


def _ssm_disc(a_re, a_im, log_step, b_re, b_im):
    step = jnp.exp(log_step)[..., None]
    mag = jnp.exp(a_re * step)
    lb_re, lb_im = mag * jnp.cos(a_im * step), mag * jnp.sin(a_im * step)
    nr, ni = lb_re - 1.0, lb_im
    den = a_re * a_re + a_im * a_im
    f_re = ((nr * a_re + ni * a_im) / den)[..., None]
    f_im = ((ni * a_re - nr * a_im) / den)[..., None]
    return lb_re, lb_im, f_re * b_re - f_im * b_im, f_re * b_im + f_im * b_re


def _ssm_pack(lb_re, lb_im, bb_re, bb_im, c_re, c_im):
    eye = jnp.eye(SSM_CH // SSM_GROUP, dtype=F32)
    ng = SSM_CH // SSM_GROUP

    def diag_b(bb):
        t = bb.reshape(2, SSM_CB, ng, SSM_STATE, SSM_GROUP)
        return jnp.einsum('dkgpc,gh->dkgchp', t, eye).reshape(2, SSM_CB, SSM_CH, SSM_ST)

    def diag_c(cc):
        t = cc.reshape(2, SSM_CB, ng, SSM_GROUP, SSM_STATE)
        return jnp.einsum('dkgcp,gh->dkhpgc', t, eye).reshape(2, SSM_CB, SSM_ST, SSM_CH)

    bcat = jnp.concatenate([diag_b(bb_re), diag_b(bb_im)], axis=-1)
    ccat = jnp.concatenate([diag_c(c_re), -diag_c(c_im)], axis=-2)
    lam_re = lb_re.reshape(2, SSM_CB, 1, SSM_ST)
    lam_im = lb_im.reshape(2, SSM_CB, 1, SSM_ST)
    return bcat, ccat, lam_re, lam_im


def _ssm_unpack(dbcat, dccat, dlam_re, dlam_im):
    ng = SSM_CH // SSM_GROUP
    eye = jnp.eye(ng, dtype=F32)

    def undiag_b(t):
        t = t.reshape(2, SSM_CB, ng, SSM_GROUP, ng, SSM_STATE)
        return jnp.einsum('dkgchp,gh->dkgpc', t, eye).reshape(2, N_SSM_GROUPS, SSM_STATE, SSM_GROUP)

    def undiag_c(t):
        t = t.reshape(2, SSM_CB, ng, SSM_STATE, ng, SSM_GROUP)
        return jnp.einsum('dkhpgc,gh->dkgcp', t, eye).reshape(2, N_SSM_GROUPS, SSM_GROUP, SSM_STATE)

    dbb_re, dbb_im = undiag_b(dbcat[..., :SSM_ST]), undiag_b(dbcat[..., SSM_ST:])
    dc_re, dc_im = undiag_c(dccat[:, :, :SSM_ST]), -undiag_c(dccat[:, :, SSM_ST:])
    shape = (2, N_SSM_GROUPS, SSM_STATE)
    return dlam_re.reshape(shape), dlam_im.reshape(shape), dbb_re, dbb_im, dc_re, dc_im


def _to_segments(t):
    l, w = t.shape
    return t.reshape(N_SEG, l // N_SEG, w).transpose(1, 0, 2).reshape(l, w)


def _from_segments(t):
    l, w = t.shape
    return t.reshape(l // N_SEG, N_SEG, w).transpose(1, 0, 2).reshape(l, w)


SSM_RC = 256
SSM_JC = SSM_RC // N_SEG
_RE, _IM = pl.ds(0, SSM_ST), pl.ds(SSM_ST, SSM_ST)


def _cfma(ar, ai, xr, xi, br, bi):
    return ar * xr - ai * xi + br, ar * xi + ai * xr + bi


def _chunk_rows(ci, rev, nc):
    start = jnp.where(rev, (nc - 1 - ci) * SSM_RC, ci * SSM_RC)
    return pl.ds(pl.multiple_of(start, SSM_RC), SSM_RC)


def _scan_chunk(src, dst, ar, ai, rev, nj, ci, carry, prev_ref=None):
    def rows_of(staged, j, k):
        at = jnp.where(rev, SSM_JC - 1 - k, k) if staged else j
        return pl.ds(pl.multiple_of(at * N_SEG, N_SEG), N_SEG)

    for k in range(SSM_JC):
        jj = ci * SSM_JC + k
        j = jnp.where(rev, nj - 1 - jj, jj)
        rows = rows_of(src[1], j, k)
        nr, ni = _cfma(ar, ai, carry[0], carry[1], src[0][rows, _RE], src[0][rows, _IM])
        if dst is not None:
            rows = rows_of(dst[1], j, k)
            dst[0][rows, _RE] = nr
            dst[0][rows, _IM] = ni
        if prev_ref is None:
            carry = (nr, ni)
            continue
        jp = jnp.where(rev, j - 1, j + 1)
        if k == SSM_JC - 1:
            inside = jnp.where((jp >= 0) & (jp < nj), 1.0, 0.0)
            jp = jnp.clip(jp, 0, nj - 1)
        prow = pl.ds(pl.multiple_of(jp * N_SEG, N_SEG), N_SEG)
        xr, xi = prev_ref[prow, _RE], prev_ref[prow, _IM]
        sr, si = nr * xr + ni * xi, ni * xr - nr * xi
        if k == SSM_JC - 1:
            sr, si = inside * sr, inside * si
        carry = (nr, ni, carry[2] + sr, carry[3] + si)
    return carry


def _segment_inits(ar, ai, end_r, end_i, rev, nj):
    pr, pi = ar, ai
    for _ in range(int(math.log2(nj))):
        pr, pi = pr * pr - pi * pi, 2.0 * pr * pi
    seg = lax.broadcasted_iota(jnp.int32, end_r.shape, 0)
    zero = jnp.zeros_like(end_r)

    def chain(shift, keep):
        ir, ii = zero, zero
        for _ in range(N_SEG - 1):
            tr, ti = _cfma(pr, pi, ir, ii, end_r, end_i)
            ir = jnp.where(keep, pltpu.roll(tr, shift, 0), 0.0)
            ii = jnp.where(keep, pltpu.roll(ti, shift, 0), 0.0)
        return ir, ii

    up_r, up_i = chain(1, seg >= 1)
    dn_r, dn_i = chain(N_SEG - 1, seg <= N_SEG - 2)
    return jnp.where(rev, dn_r, up_r), jnp.where(rev, dn_i, up_i)


def _ssm_specs(l):
    act = pl.BlockSpec((l, SSM_CH), lambda k, d: (0, k))
    bmat = pl.BlockSpec((None, None, SSM_CH, 2 * SSM_ST), lambda k, d: (d, k, 0, 0))
    cmat = pl.BlockSpec((None, None, 2 * SSM_ST, SSM_CH), lambda k, d: (d, k, 0, 0))
    lam = pl.BlockSpec((None, None, 1, SSM_ST), lambda k, d: (d, k, 0, 0))
    return act, bmat, cmat, lam


def _ssm_fwd(u_seg, bcat, ccat, lam_re, lam_im):
    l = u_seg.shape[0]
    nj = l // N_SEG
    nc = l // SSM_RC

    def body(u_ref, b_ref, c_ref, lr_ref, li_ref, y_ref, ub_ref, keep_ref, xs_ref, stage0, stage1, keep_sem):
        k, d = pl.program_id(0), pl.program_id(1)
        rev = d == 1
        shape = (N_SEG, SSM_ST)
        ar, ai = jnp.broadcast_to(lr_ref[...], shape), jnp.broadcast_to(li_ref[...], shape)
        zero = jnp.zeros(shape, F32)

        def inputs(ci, stage):
            rows = _chunk_rows(ci, rev, nc)
            ub = u_ref[rows, :].astype(BF16)
            ub_ref[rows, :] = ub
            bu = _dg(ub, b_ref[...], NN)
            stage[...] = bu
            xs_ref[rows, :] = bu

        def first(stage, ci, carry):
            return _scan_chunk((stage, True), None, ar, ai, rev, nj, ci, carry)

        def first_pass(t, carry):
            inputs(2 * t + 1, stage1)
            carry = first(stage0, 2 * t, carry)
            inputs(2 * t + 2, stage0)
            return first(stage1, 2 * t + 1, carry)

        inputs(0, stage0)
        carry = lax.fori_loop(0, nc // 2 - 1, first_pass, (zero, zero))
        inputs(nc - 1, stage1)
        carry = first(stage0, nc - 2, carry)
        end_r, end_i = first(stage1, nc - 1, carry)
        init = _segment_inits(ar, ai, end_r, end_i, rev, nj)

        @pl.when(d == 0)
        def _():
            y_ref[...] = jnp.zeros_like(y_ref)

        def outputs(ci):
            rows = _chunk_rows(ci, rev, nc)
            y_ref[rows, :] += _dg(xs_ref[rows, :].astype(BF16), c_ref[...], NN)
            pltpu.make_async_copy(xs_ref.at[rows], keep_ref.at[d, k, rows], keep_sem).start()

        def second(ci, carry):
            return _scan_chunk((xs_ref, False), (xs_ref, False), ar, ai, rev, nj, ci, carry)

        def second_pass(ci, carry):
            outputs(ci - 1)
            return second(ci, carry)

        lax.fori_loop(1, nc, second_pass, second(0, init))
        outputs(nc - 1)
        pltpu.make_async_copy(xs_ref, keep_ref.at[d, k], keep_sem).wait()

    act, bmat, cmat, lam = _ssm_specs(l)
    return pl.pallas_call(
        body, grid=(SSM_CB, 2), in_specs=[act, bmat, cmat, lam, lam], out_specs=[act, act, ANY],
        out_shape=[jax.ShapeDtypeStruct((l, SSM_WIDTH), F32), jax.ShapeDtypeStruct((l, SSM_WIDTH), BF16),
                   jax.ShapeDtypeStruct((2, SSM_CB, l, 2 * SSM_ST), F32)],
        scratch_shapes=[pltpu.VMEM((l, 2 * SSM_ST), F32), pltpu.VMEM((SSM_RC, 2 * SSM_ST), F32),
                        pltpu.VMEM((SSM_RC, 2 * SSM_ST), F32), pltpu.SemaphoreType.DMA],
        name="ssm_fwd", compiler_params=_params(("parallel", "arbitrary"), vmem_mb=56),
    )(u_seg, bcat.astype(BF16), ccat.astype(BF16), lam_re, lam_im)


def _ssm_bwd(u_seg, dy_seg, states, bcat, ccat, lam_re, lam_im):
    l = u_seg.shape[0]
    nj = l // N_SEG
    nc = l // SSM_RC

    def body(u_ref, dy_ref, keep_ref, b_ref, c_ref, lr_ref, li_ref,
             du_ref, db_ref, dc_ref, dlr_ref, dli_ref, xs_ref, gs_ref, dyb_ref, stage0, stage1, keep_sem):
        k, d = pl.program_id(0), pl.program_id(1)
        rev = d == 1
        back = jnp.logical_not(rev)
        shape = (N_SEG, SSM_ST)
        ar, ai = jnp.broadcast_to(lr_ref[...], shape), -jnp.broadcast_to(li_ref[...], shape)
        zero = jnp.zeros(shape, F32)
        fetch = pltpu.make_async_copy(keep_ref.at[d, k], xs_ref, keep_sem)
        fetch.start()

        def inputs(ci, stage):
            rows = _chunk_rows(ci, back, nc)
            dyb = dy_ref[rows, :].astype(BF16)
            dyb_ref[rows, :] = dyb
            dx = _dg(dyb, c_ref[...], NT)
            stage[...] = dx
            gs_ref[rows, :] = dx

        def first(stage, ci, carry):
            return _scan_chunk((stage, True), None, ar, ai, back, nj, ci, carry)

        def first_pass(t, carry):
            inputs(2 * t + 1, stage1)
            carry = first(stage0, 2 * t, carry)
            inputs(2 * t + 2, stage0)
            return first(stage1, 2 * t + 1, carry)

        inputs(0, stage0)
        carry = lax.fori_loop(0, nc // 2 - 1, first_pass, (zero, zero))
        inputs(nc - 1, stage1)
        carry = first(stage0, nc - 2, carry)
        end_r, end_i = first(stage1, nc - 1, carry)
        init = _segment_inits(ar, ai, end_r, end_i, back, nj)
        fetch.wait()
        db_ref[...] = jnp.zeros_like(db_ref)
        dc_ref[...] = jnp.zeros_like(dc_ref)

        @pl.when(d == 0)
        def _():
            du_ref[...] = jnp.zeros_like(du_ref)

        def outputs(ci, stage):
            rows = _chunk_rows(ci, back, nc)
            g = stage[...].astype(BF16)
            dc_ref[...] += _dg(xs_ref[rows, :].astype(BF16), dyb_ref[rows, :], TN)
            db_ref[...] += _dg(u_ref[rows, :], g, TN)
            du_ref[rows, :] += _dg(g, b_ref[...], NT)

        def second(ci, stage, carry):
            return _scan_chunk((gs_ref, False), (stage, True), ar, ai, back, nj, ci, carry, prev_ref=xs_ref)

        def second_pass(t, carry):
            outputs(2 * t, stage0)
            carry = second(2 * t + 1, stage1, carry)
            outputs(2 * t + 1, stage1)
            return second(2 * t + 2, stage0, carry)

        carry = lax.fori_loop(0, nc // 2 - 1, second_pass, second(0, stage0, init + (zero, zero)))
        outputs(nc - 2, stage0)
        gr, gi, acc_r, acc_i = second(nc - 1, stage1, carry)
        outputs(nc - 1, stage1)

        seg = lax.broadcasted_iota(jnp.int32, shape, 0)
        jb = jnp.where(rev, nj - 1, 0)
        erow = pl.ds(pl.multiple_of((nj - 1 - jb) * N_SEG, N_SEG), N_SEG)

        def before(t):
            up = jnp.where(seg >= 1, pltpu.roll(t, 1, 0), 0.0)
            down = jnp.where(seg <= N_SEG - 2, pltpu.roll(t, N_SEG - 1, 0), 0.0)
            return jnp.where(rev, down, up)

        init_r, init_i = before(xs_ref[erow, _RE]), before(xs_ref[erow, _IM])
        acc_r = acc_r + gr * init_r + gi * init_i
        acc_i = acc_i + gi * init_r - gr * init_i
        dlr_ref[...] = jnp.sum(acc_r, axis=0, keepdims=True)
        dli_ref[...] = jnp.sum(acc_i, axis=0, keepdims=True)

    act, bmat, cmat, lam = _ssm_specs(l)
    return pl.pallas_call(
        body, grid=(SSM_CB, 2), in_specs=[act, act, ANY, bmat, cmat, lam, lam],
        out_specs=[act, bmat, cmat, lam, lam],
        out_shape=[jax.ShapeDtypeStruct((l, SSM_WIDTH), F32),
                   jax.ShapeDtypeStruct(bcat.shape, F32), jax.ShapeDtypeStruct(ccat.shape, F32),
                   jax.ShapeDtypeStruct(lam_re.shape, F32), jax.ShapeDtypeStruct(lam_im.shape, F32)],
        scratch_shapes=[pltpu.VMEM((l, 2 * SSM_ST), F32), pltpu.VMEM((l, 2 * SSM_ST), F32),
                        pltpu.VMEM((l, SSM_CH), BF16),
                        pltpu.VMEM((SSM_RC, 2 * SSM_ST), F32), pltpu.VMEM((SSM_RC, 2 * SSM_ST), F32),
                        pltpu.SemaphoreType.DMA],
        name="ssm_bwd", compiler_params=_params(("parallel", "arbitrary"), vmem_mb=58),
    )(u_seg, dy_seg, states, bcat.astype(BF16), ccat.astype(BF16), lam_re, lam_im)


def _glu_fwd(y_ssm, u, d_skip, w_glu):
    l, w = u.shape

    def body(y_ref, u_ref, d_ref, w_ref, pre_ref, s_ref, ys_ref):
        pre = y_ref[...] + d_ref[...] * u_ref[...]
        z = _gelu(pre)
        s = _dg(z.astype(BF16), w_ref[...], NN)
        pre_ref[...] = pre
        s_ref[...] = s
        ys_ref[...] = z * _sigmoid(s)

    row = pl.BlockSpec((TM_EW, w), lambda i: (i, 0))
    out = jax.ShapeDtypeStruct((l, w), F32)
    return pl.pallas_call(
        body, grid=(l // TM_EW,),
        in_specs=[row, row, pl.BlockSpec((1, w), lambda i: (0, 0)), pl.BlockSpec((w, w), lambda i: (0, 0))],
        out_specs=[row, row, row], out_shape=[out, out, out], name="glu_fwd",
        compiler_params=_params(("parallel",)),
    )(y_ssm, u, d_skip, w_glu)


def _glu_bwd(pre, s, dys, u, d_skip, w_glu):
    l, w = u.shape

    def body(pre_ref, s_ref, dys_ref, u_ref, d_ref, w_ref, dpre_ref, z_ref, ds_ref, dd_ref):
        pre, dys = pre_ref[...], dys_ref[...]
        z = _gelu(pre)
        sig = _sigmoid(s_ref[...])
        ds = (dys * z * sig * (1.0 - sig)).astype(BF16)
        dz = dys * sig + _dg(ds, w_ref[...], NT)
        dpre = dz * _gelu_grad(pre)
        dpre_ref[...] = dpre
        z_ref[...] = z.astype(BF16)
        ds_ref[...] = ds

        @pl.when(pl.program_id(0) == 0)
        def _():
            dd_ref[...] = jnp.zeros_like(dd_ref)

        dd_ref[...] += jnp.sum(dpre * u_ref[...], axis=0, keepdims=True)

    row = pl.BlockSpec((TM_EW, w), lambda i: (i, 0))
    vec = pl.BlockSpec((1, w), lambda i: (0, 0))
    return pl.pallas_call(
        body, grid=(l // TM_EW,),
        in_specs=[row, row, row, row, vec, pl.BlockSpec((w, w), lambda i: (0, 0))],
        out_specs=[row, row, row, vec],
        out_shape=[jax.ShapeDtypeStruct((l, w), F32), jax.ShapeDtypeStruct((l, w), BF16),
                   jax.ShapeDtypeStruct((l, w), BF16), jax.ShapeDtypeStruct((1, w), F32)],
        name="glu_bwd", compiler_params=_params(("arbitrary",)),
    )(pre, s, dys, u, d_skip, w_glu)


TM_CV = 1024
TC_CV = 256
TM_CF = 1024
TC_CF = D_FF // 2
HALO = SUBLANES


def _conv_specs(l, col0, tm=TM_CV, tc=TC_CV):
    per = tm // HALO
    nh = l // HALO
    off = col0 // tc
    return [
        pl.BlockSpec((HALO, tc), lambda j, i: (jnp.maximum(i * per - 1, 0), j + off)),
        pl.BlockSpec((tm, tc), lambda j, i: (i, j + off)),
        pl.BlockSpec((HALO, tc), lambda j, i: (jnp.minimum((i + 1) * per, nh - 1), j + off)),
    ]


def _ext(prev_ref, mid_ref, next_ref, first, last):
    p = jnp.where(first, 0.0, prev_ref[...])
    n = jnp.where(last, 0.0, next_ref[...])
    return jnp.concatenate([p, mid_ref[...], n], axis=0)


def _shift_dn(t):
    return pltpu.roll(t, 1, 0)


def _shift_up(t):
    return pltpu.roll(t, t.shape[0] - 1, 0)


def _conv3(e, w_ref, b_ref):
    return w_ref[0:1, :] * _shift_dn(e) + w_ref[1:2, :] * e + w_ref[2:3, :] * _shift_up(e) + b_ref[...]


def _convffn_fwd(up_pre, conv_w, conv_b):
    l = up_pre.shape[0]
    tm, tc = TM_CF, TC_CF
    ni = l // tm
    wspec = lambda off: pl.BlockSpec((3, tc), lambda j, i: (0, j + off))
    bspec = lambda off: pl.BlockSpec((1, tc), lambda j, i: (0, j + off))
    voff = D_FF // tc

    def body(gp, gm, gn, vp, vm, vn, wg, bg, wv, bv, o_ref):
        i = pl.program_id(1)
        first, last = i == 0, i == ni - 1
        gate = _conv3(_ext(gp, gm, gn, first, last), wg, bg)[HALO:HALO + tm]
        val = _conv3(_ext(vp, vm, vn, first, last), wv, bv)[HALO:HALO + tm]
        o_ref[...] = (gate * _sigmoid(gate) * val).astype(BF16)

    return pl.pallas_call(
        body, grid=(D_FF // tc, ni),
        in_specs=_conv_specs(l, 0, tm, tc) + _conv_specs(l, D_FF, tm, tc)
        + [wspec(0), bspec(0), wspec(voff), bspec(voff)],
        out_specs=pl.BlockSpec((tm, tc), lambda j, i: (i, j)),
        out_shape=jax.ShapeDtypeStruct((l, D_FF), BF16), name="convffn_fwd",
        compiler_params=_params(("parallel", "parallel")),
    )(up_pre, up_pre, up_pre, up_pre, up_pre, up_pre, conv_w, conv_b, conv_w, conv_b)


HALO_B = 2 * SUBLANES


def _convffn_bwd(up_pre, dx2b, w_down, conv_w, conv_b):
    l = up_pre.shape[0]
    ni = l // TM_CV
    d = dx2b.shape[1]
    wspec = lambda off: pl.BlockSpec((3, TC_CV), lambda i, j: (0, j + off))
    bspec = lambda off: pl.BlockSpec((1, TC_CV), lambda i, j: (0, j + off))
    voff = D_FF // TC_CV
    swap = lambda spec: pl.BlockSpec(spec.block_shape, lambda i, j, f=spec.index_map: f(j, i))
    per, nh = TM_CV // HALO_B, l // HALO_B
    dx_specs = [pl.BlockSpec((HALO_B, d), lambda i, j: (jnp.maximum(i * per - 1, 0), 0)),
                pl.BlockSpec((TM_CV, d), lambda i, j: (i, 0)),
                pl.BlockSpec((HALO_B, d), lambda i, j: (jnp.minimum((i + 1) * per, nh - 1), 0))]

    def body(gp, gm, gn, vp, vm, vn, xp, xm, xn, wd, wg, bg, wv, bv, dup_ref, pg_ref, pv_ref):
        i = pl.program_id(0)
        first, last = i == 0, i == ni - 1
        ge, ve = _ext(gp, gm, gn, first, last), _ext(vp, vm, vn, first, last)
        zero = jnp.zeros((HALO_B, d), BF16)
        dx = jnp.concatenate([jnp.where(first, zero, xp[...]), xm[...], jnp.where(last, zero, xn[...])], axis=0)
        de = _dg(dx, wd[...], NT)[HALO_B - HALO:HALO_B + TM_CV + HALO]
        taps = [(_shift_dn(e), e, _shift_up(e)) for e in (ge, ve)]
        conv = lambda t, w_ref, b_ref: w_ref[0:1, :] * t[0] + w_ref[1:2, :] * t[1] + w_ref[2:3, :] * t[2] + b_ref[...]
        gate, val = conv(taps[0], wg, bg), conv(taps[1], wv, bv)
        sig = _sigmoid(gate)
        silu = gate * sig
        dgate = de * val * (sig + silu * (1.0 - sig))
        dval = de * silu
        mid = slice(HALO, HALO + TM_CV)
        rid = lax.broadcasted_iota(jnp.int32, (SUBLANES, TC_CV), 0)
        for half, (dup, tap, w_ref, p_ref) in enumerate(((dgate, taps[0], wg, pg_ref), (dval, taps[1], wv, pv_ref))):
            dpre = w_ref[0:1, :] * _shift_up(dup) + w_ref[1:2, :] * dup + w_ref[2:3, :] * _shift_dn(dup)
            dup_ref[half] = dpre[mid].astype(BF16)
            dm_ = dup[mid]
            sums = [jnp.sum(dm_ * t[mid], axis=0, keepdims=True) for t in tap]
            sums.append(jnp.sum(dm_, axis=0, keepdims=True))
            acc = jnp.zeros((SUBLANES, TC_CV), F32)
            for k, sk in enumerate(sums):
                acc = jnp.where(rid == k, sk, acc)
            p_ref[...] = acc

    par = pl.BlockSpec((None, SUBLANES, TC_CV), lambda i, j: (i, 0, j))
    dup, pg, pv = pl.pallas_call(
        body, grid=(ni, D_FF // TC_CV),
        in_specs=[swap(s) for s in _conv_specs(l, 0) + _conv_specs(l, D_FF)] + dx_specs
        + [pl.BlockSpec((TC_CV, d), lambda i, j: (j, 0)), wspec(0), bspec(0), wspec(voff), bspec(voff)],
        out_specs=[pl.BlockSpec((2, TM_CV, TC_CV), lambda i, j: (0, i, j)), par, par],
        out_shape=[jax.ShapeDtypeStruct((2, l, D_FF), BF16),
                   jax.ShapeDtypeStruct((ni, SUBLANES, D_FF), F32), jax.ShapeDtypeStruct((ni, SUBLANES, D_FF), F32)],
        name="convffn_bwd", compiler_params=_params(("parallel", "parallel")),
    )(up_pre, up_pre, up_pre, up_pre, up_pre, up_pre, dx2b, dx2b, dx2b, w_down, conv_w, conv_b, conv_w, conv_b)
    return dup, jnp.concatenate([jnp.sum(pg, axis=0), jnp.sum(pv, axis=0)], axis=1)


def _local_step(x, target, wb, sp, mixer_weights=None, late_weights=None, grads_ready=None,
                grads_next=None):
    l = x.shape[0]
    tabs = _rope_tables(l)
    disc = _ssm_disc(sp["a_re"], sp["a_im"], sp["log_step"], sp["b_re"], sp["b_im"])
    bcat, ccat, lam_re, lam_im = _ssm_pack(*disc, sp["c_re"], sp["c_im"])
    d_skip = sp["d_skip"].reshape(1, SSM_WIDTH)

    h, qkv, u = _rms_mm_rope(x, sp["norm_mix_g"], wb["w_in"], tabs, "mm_in")
    attn, lse = _attn_fwd(qkv, sp["sink"])
    y_seg, u_seg, states = _ssm_fwd(_to_segments(u), bcat, ccat, lam_re, lam_im)
    y_ssm = _from_segments(y_seg)
    if mixer_weights is not None:
        wb = dict(wb, **mixer_weights(attn))
    pre, s_glu, ys = _glu_fwd(y_ssm, u, d_skip, wb["w_glu"])
    mixed, x1, h2 = _mix_mm_res_rms(attn, ys, sp["norm_attn_g"], sp["norm_ssm_g"], wb["w_out"], x,
                                    sp["norm_ffn_g"], "mm_out")
    if late_weights is not None:
        wb = dict(wb, **late_weights(h2))
    up_pre = _mm_nn_cols(h2, wb["w_up"], min(l, 1024), "mm_up")
    conv_w = wb["conv_w"]
    act = _convffn_fwd(up_pre, conv_w, sp["conv_b"])
    loss, dx2, dx2b, d_final_g = _mm_res_loss(act, wb["w_down"], x1, sp["norm_final_g"].reshape(1, D_MODEL), target)

    g = {"norm_final_g": d_final_g.reshape(D_MODEL)}
    g["w_down"] = _mm_tn(act, dx2b, D_FF // 2, 512, "mm_down_dw")
    dup_pre, conv_par = _convffn_bwd(up_pre, dx2b, wb["w_down"], conv_w, sp["conv_b"])
    g["conv_w"], g["conv_b"] = conv_par[0:3], conv_par[3:4]
    g["w_up"] = _mm_tn_cols(h2, dup_pre, wb["w_up"].shape[0], 512, "mm_up_dw")
    dx1, dx1b, g["norm_ffn_g"] = _mm_cols_rms_bwd(dup_pre, wb["w_up"], x1, sp["norm_ffn_g"], dx2, "mm_up_dx")
    g["w_out"] = _mm_tn(mixed, dx1b, 1024, 1024, "mm_out_dw")
    zero = grads_ready(g["w_up"], g["w_down"], g["w_out"]) if grads_ready is not None else 0.0
    dattn, dys, g["norm_attn_g"], g["norm_ssm_g"] = _mm_mix_bwd(
        dx1b, wb["w_out"], attn, ys, sp["norm_attn_g"] + zero, sp["norm_ssm_g"], "mm_out_dx")
    dpre, zb, dsb, dd = _glu_bwd(pre, s_glu, dys, u, d_skip, wb["w_glu"])
    g["d_skip"] = dd.reshape(N_SSM_GROUPS, SSM_GROUP)
    g["w_glu"] = _mm_tn(zb, dsb, 512, 512, "mm_glu_dw")
    zero = grads_next(g["w_glu"]) if grads_next is not None else 0.0
    du_seg, dbcat, dccat, dlam_re, dlam_im = _ssm_bwd(u_seg, _to_segments(dpre), states, bcat, ccat,
                                                      lam_re + zero, lam_im)
    dlb_re, dlb_im, dbb_re, dbb_im, g["c_re"], g["c_im"] = _ssm_unpack(dbcat, dccat, dlam_re, dlam_im)
    _, disc_vjp = jax.vjp(_ssm_disc, sp["a_re"], sp["a_im"], sp["log_step"], sp["b_re"], sp["b_im"])
    g["a_re"], g["a_im"], g["log_step"], g["b_re"], g["b_im"] = disc_vjp((dlb_re, dlb_im, dbb_re, dbb_im))
    dq, dkv, g["sink"] = _attn_bwd(qkv, attn, dattn, lse, sp["sink"])
    dproj = _rope_bwd(dq, dkv, _from_segments(du_seg), dpre, d_skip, tabs)
    g["w_in"] = _mm_tn(dproj, h, IN_WIDTH // 5, D_MODEL, "mm_in_dw")
    grad_x, _, g["norm_mix_g"] = _mm_nn_rms_bwd(dproj, wb["w_in"], x, sp["norm_mix_g"], dx1, "mm_in_dx")
    return loss, grad_x, g


MESH = pl.DeviceIdType.MESH
ANY = pl.BlockSpec(memory_space=pl.ANY)


def _place():
    x, y, c = lax.axis_index("x"), lax.axis_index("y"), lax.axis_index("c")
    chips = [(1 - x, y), (x, 1 - y), (1 - x, 1 - y)]
    return x, y, c, chips


def _chip_index(px, py):
    return 2 * px + py


CHUNK_BYTES = 256 * 1024
MAX_CHUNKS = 16


def _row_chunks(rows, row_bytes, align):
    n = max(1, min(MAX_CHUNKS, (rows * row_bytes) // CHUNK_BYTES))
    per = -(-rows // n)
    per = -(-per // align) * align
    return [(r0, min(per, rows - r0)) for r0 in range(0, rows, per)]


def _align_of(dtype):
    return SUBLANES * 4 // jnp.dtype(dtype).itemsize


def _remote(src, dst, send_sem, recv_sem, to):
    return pltpu.make_async_remote_copy(src_ref=src, dst_ref=dst, send_sem=send_sem, recv_sem=recv_sem,
                                        device_id=to, device_id_type=MESH)


CAST_ROWS = 64


def _gather_weights(shards, dtypes):
    nw = len(shards)

    def body(*refs):
        w_refs, o_refs = refs[:nw], refs[nw:2 * nw]
        send_sems, recv_sems, in_sems, out_sems = refs[2 * nw:2 * nw + 4]
        raw, cast = refs[2 * nw + 4:3 * nw + 4], refs[3 * nw + 4:]
        x, y, c, chips = _place()
        mine = _chip_index(x, y)
        sibling = (x, y, 1 - c)

        def rows_of(ref, chip, r0, nr):
            return ref.at[chip, pl.ds(r0, nr), :]

        def copy(wi, k, src, dst, to):
            return _remote(src, dst, send_sems.at[wi, k], recv_sems.at[wi, k], to)

        geo = []
        for wi in range(nw):
            rows, cols = w_refs[wi].shape
            row_bytes = cols * jnp.dtype(dtypes[wi]).itemsize
            geo.append((rows // 2, _row_chunks(rows // 2, row_bytes, _align_of(dtypes[wi]))))

        stage_in = [pltpu.make_async_copy(w_refs[wi], raw[wi], in_sems.at[wi]) for wi in range(nw)]
        for cp in stage_in:
            cp.start()
        staged = [raw[wi] if dtypes[wi] == w_refs[wi].dtype else cast[wi] for wi in range(nw)]
        stage_out = []
        for wi in range(nw):
            stage_in[wi].wait()
            if staged[wi] is not raw[wi]:
                def cast_rows(i, _, wi=wi):
                    rows = pl.ds(pl.multiple_of(i * CAST_ROWS, CAST_ROWS), CAST_ROWS)
                    cast[wi][rows, :] = raw[wi][rows, :].astype(dtypes[wi])
                    return 0

                lax.fori_loop(0, w_refs[wi].shape[0] // CAST_ROWS, cast_rows, 0)
            cp = pltpu.make_async_copy(staged[wi], o_refs[wi].at[mine], out_sems.at[wi])
            cp.start()
            stage_out.append(cp)

        for wi in range(nw):
            hr, half_chunks = geo[wi]
            for j, chip in enumerate(chips):
                for r0, nr in half_chunks:
                    copy(wi, j, staged[wi].at[pl.ds(c * hr + r0, nr), :],
                         rows_of(o_refs[wi], mine, c * hr + r0, nr), (*chip, c)).start()
        for wi in range(nw):
            hr, half_chunks = geo[wi]
            for j, chip in enumerate(chips):
                got = rows_of(o_refs[wi], _chip_index(*chip), c * hr, hr)
                copy(wi, j, got, got, (*chip, c)).wait_recv()
                for r0, nr in half_chunks:
                    piece = rows_of(o_refs[wi], _chip_index(*chip), c * hr + r0, nr)
                    copy(wi, 3 + j, piece, piece, sibling).start()
        for wi in range(nw):
            hr = geo[wi][0]
            for j, chip in enumerate(chips):
                got = rows_of(o_refs[wi], _chip_index(*chip), (1 - c) * hr, hr)
                copy(wi, 3 + j, got, got, sibling).wait_recv()
        for wi in range(nw):
            hr = geo[wi][0]
            sent = rows_of(o_refs[wi], mine, c * hr, hr)
            for k in range(6):
                copy(wi, k, sent, sent, sibling).wait_send()
            stage_out[wi].wait()

    return pl.pallas_call(
        body, in_specs=[ANY] * nw, out_specs=[ANY] * nw,
        out_shape=[jax.ShapeDtypeStruct((4, *s.shape), t) for s, t in zip(shards, dtypes)],
        scratch_shapes=[pltpu.SemaphoreType.DMA((nw, 6)), pltpu.SemaphoreType.DMA((nw, 6)),
                        pltpu.SemaphoreType.DMA((nw,)), pltpu.SemaphoreType.DMA((nw,))]
        + [pltpu.VMEM(s.shape, s.dtype) for s in shards] + [pltpu.VMEM(s.shape, t) for s, t in zip(shards, dtypes)],
        name="gather_weights", compiler_params=_params(vmem_mb=40),
    )(*shards)


HBM = pl.BlockSpec(memory_space=pltpu.HBM)
SEM = pl.BlockSpec(memory_space=pltpu.SEMAPHORE)
EFFECT = pltpu.SideEffectType.DATAFLOW_SIDE_EFFECTING


def _cast_place(w, place, dtype, after, name):
    rows, cols = w.shape
    tr = _row_tile(rows, cols, _align_of(dtype))

    def body(p_ref, w_ref, after_ref, o_ref):
        del p_ref, after_ref
        o_ref[...] = w_ref[...].astype(dtype)

    grid_spec = pltpu.PrefetchScalarGridSpec(
        num_scalar_prefetch=1, grid=(rows // tr,),
        in_specs=[pl.BlockSpec((tr, cols), lambda i, p: (i, 0)), ANY],
        out_specs=pl.BlockSpec((None, tr, cols), lambda i, p: (p[1], i, 0)))
    return pl.pallas_call(body, grid_spec=grid_spec, out_shape=jax.ShapeDtypeStruct((4, rows, cols), dtype),
                          name=name, compiler_params=_params(("parallel",)))(place, w, after)


def _split_start(name, arrays, n_pairs, issue):
    n = len(arrays)

    def body(*refs):
        issue(refs[:n], refs[n:n + n_pairs], refs[n + n_pairs:n + 2 * n_pairs])
        token = refs[2 * n + 2 * n_pairs]
        token[...] = jnp.zeros_like(token)

    dma = pltpu.SemaphoreType.DMA(())
    outs = pl.pallas_call(
        body, name=name,
        out_shape=[dma] * (2 * n_pairs) + [pltpu.HBM(t.shape, t.dtype) for t in arrays]
        + [jax.ShapeDtypeStruct((SUBLANES, LANES), F32)],
        in_specs=[HBM] * n, out_specs=[SEM] * (2 * n_pairs) + [HBM] * n + [pl.BlockSpec(memory_space=pltpu.VMEM)],
        input_output_aliases={a: 2 * n_pairs + a for a in range(n)},
        compiler_params=pltpu.CompilerParams(has_side_effects=EFFECT),
    )(*[pltpu.with_memory_space_constraint(t, pltpu.HBM) for t in arrays])
    return outs[:n_pairs], outs[n_pairs:2 * n_pairs], outs[2 * n_pairs:2 * n_pairs + n], outs[-1]


def _split_wait(name, send_sems, recv_sems, flying, sizes, after):
    n, n_pairs = len(flying), len(send_sems)

    def body(*refs):
        x, y, c, _ = _place()
        for k, ref in enumerate(sizes(refs[:n])):
            cp = _remote(ref, ref, refs[n + k], refs[n + n_pairs + k], (x, y, 1 - c))
            cp.wait_send()
            cp.wait_recv()

    return pl.pallas_call(
        body, name=name, out_shape=[pltpu.HBM(t.shape, t.dtype) for t in flying],
        in_specs=[HBM] * n + [SEM] * (2 * n_pairs) + [ANY], out_specs=[HBM] * n,
        input_output_aliases={a: a for a in range(n)},
        compiler_params=pltpu.CompilerParams(has_side_effects=EFFECT),
    )(*flying, *send_sems, *recv_sems, after)


def _spread_start(lands, name):
    def issue(land_refs, send_sems, recv_sems):
        x, y, c, chips = _place()
        mine = _chip_index(x, y)
        for a, land in enumerate(land_refs):
            _, rows, cols = land.shape
            hr = rows // 2
            row_bytes = cols * jnp.dtype(land.dtype).itemsize
            for r0, nr in _row_chunks(hr, row_bytes, _align_of(land.dtype)):
                piece = land.at[mine, pl.ds(c * hr + r0, nr), :]
                for chip in chips:
                    for core in (0, 1):
                        _remote(piece, piece, send_sems[a], recv_sems[a], (*chip, core)).start()

    return _split_start(name, lands, len(lands), issue)


def _spread_wait(send_sems, recv_sems, flying, after, name):
    return _split_wait(name, send_sems, recv_sems, flying, lambda refs: [r.at[pl.ds(0, 3)] for r in refs], after)


def _pair_start(grads):
    n = len(grads)
    zones = [lax.empty((4, g.shape[1] // 2, g.shape[2]), F32) for g in grads]

    def issue(refs, send_sems, recv_sems):
        x, y, c, _ = _place()
        for a in range(n):
            g_ref, z_ref = refs[a], refs[n + a]
            _, rows, cols = g_ref.shape
            hr = rows // 2
            for k in range(4):
                for r0, nr in _row_chunks(hr, cols * 4, SUBLANES):
                    _remote(g_ref.at[k, pl.ds((1 - c) * hr + r0, nr), :], z_ref.at[k, pl.ds(r0, nr), :],
                            send_sems[a], recv_sems[a], (x, y, 1 - c)).start()

    return _split_start("pair_start", list(grads) + zones, n, issue)


def _pair_wait(send_sems, recv_sems, flying, after):
    n = len(flying) // 2
    out = _split_wait("pair_wait", send_sems, recv_sems, flying, lambda refs: list(refs[n:]), after)
    return out[:n], out[n:]


def _chip_start(sums):
    n = len(sums)
    zones = [lax.empty((3, *s.shape[1:]), s.dtype) for s in sums]

    def issue(refs, send_sems, recv_sems):
        x, y, c, chips = _place()
        for a in range(n):
            s_ref, z_ref = refs[a], refs[n + a]
            _, rows, cols = s_ref.shape
            row_bytes = cols * jnp.dtype(s_ref.dtype).itemsize
            for r0, nr in _row_chunks(rows, row_bytes, _align_of(s_ref.dtype)):
                for j, chip in enumerate(chips):
                    _remote(s_ref.at[_chip_index(*chip), pl.ds(r0, nr), :], z_ref.at[j, pl.ds(r0, nr), :],
                            send_sems[a], recv_sems[a], (*chip, c)).start()

    return _split_start("chip_start", list(sums) + zones, n, issue)


def _chip_wait(send_sems, recv_sems, flying, after):
    n = len(flying) // 2
    return _split_wait("chip_wait", send_sems, recv_sems, flying, lambda refs: list(refs[n:]), after)[n:]


def _pair_exchange(grads):
    na = len(grads)

    def body(*refs):
        g_refs, o_refs = refs[:na], refs[na:2 * na]
        send_sems, recv_sems = refs[2 * na:]
        x, y, c, _ = _place()
        sibling = (x, y, 1 - c)
        for ai in range(na):
            _, rows, cols = g_refs[ai].shape
            hr = rows // 2
            for k in range(4):
                for r0, nr in _row_chunks(hr, cols * 4, SUBLANES):
                    _remote(g_refs[ai].at[k, pl.ds((1 - c) * hr + r0, nr), :], o_refs[ai].at[k, pl.ds(r0, nr), :],
                            send_sems.at[ai], recv_sems.at[ai], sibling).start()
        for ai in range(na):
            _remote(o_refs[ai], o_refs[ai], send_sems.at[ai], recv_sems.at[ai], sibling).wait()

    return pl.pallas_call(
        body, in_specs=[ANY] * na, out_specs=[ANY] * na,
        out_shape=[jax.ShapeDtypeStruct((4, g.shape[1] // 2, g.shape[2]), F32) for g in grads],
        scratch_shapes=[pltpu.SemaphoreType.DMA((na,)), pltpu.SemaphoreType.DMA((na,))],
        name="pair_exchange",
    )(*grads)


def _row_tile(rows, cols, align, elems=256 * 1024):
    best = align
    for cand in range(align, rows + 1, align):
        if rows % cand == 0 and cand * cols <= elems:
            best = cand
    return best


def _pair_sum(g, got, place, transit, name):
    _, rows, cols = g.shape
    hr = rows // 2
    tr = _row_tile(hr, cols, _align_of(transit), 512 * 1024)
    nt = hr // tr

    def body(p_ref, g_ref, r_ref, s_ref, own_ref):
        total = g_ref[...] + r_ref[...]
        s_ref[...] = total.astype(transit)

        @pl.when(pl.program_id(1) == p_ref[1])
        def _():
            own_ref[...] = total

    grid_spec = pltpu.PrefetchScalarGridSpec(
        num_scalar_prefetch=1, grid=(nt, 4),
        in_specs=[pl.BlockSpec((None, tr, cols), lambda i, k, p: (k, p[0] * nt + i, 0)),
                  pl.BlockSpec((None, tr, cols), lambda i, k, p: (k, i, 0))],
        out_specs=[pl.BlockSpec((None, tr, cols), lambda i, k, p: (k, i, 0)),
                   pl.BlockSpec((tr, cols), lambda i, k, p: (i, 0))])
    return pl.pallas_call(
        body, grid_spec=grid_spec,
        out_shape=[jax.ShapeDtypeStruct((4, hr, cols), transit), jax.ShapeDtypeStruct((hr, cols), F32)],
        name=name, compiler_params=_params(("parallel", "arbitrary")),
    )(place, g, got)


def _chip_exchange(sums):
    na = len(sums)

    def body(*refs):
        s_refs, o_refs = refs[:na], refs[na:2 * na]
        send_sems, recv_sems = refs[2 * na:]
        x, y, c, chips = _place()
        for ai in range(na):
            _, rows, cols = s_refs[ai].shape
            row_bytes = cols * jnp.dtype(s_refs[ai].dtype).itemsize
            for r0, nr in _row_chunks(rows, row_bytes, _align_of(s_refs[ai].dtype)):
                for j, chip in enumerate(chips):
                    _remote(s_refs[ai].at[_chip_index(*chip), pl.ds(r0, nr), :], o_refs[ai].at[j, pl.ds(r0, nr), :],
                            send_sems.at[ai, j], recv_sems.at[ai, j], (*chip, c)).start()
        for ai in range(na):
            for j, chip in enumerate(chips):
                _remote(o_refs[ai].at[j], o_refs[ai].at[j], send_sems.at[ai, j], recv_sems.at[ai, j],
                        (*chip, c)).wait()

    return pl.pallas_call(
        body, in_specs=[ANY] * na, out_specs=[ANY] * na,
        out_shape=[jax.ShapeDtypeStruct((3, *s.shape[1:]), s.dtype) for s in sums],
        scratch_shapes=[pltpu.SemaphoreType.DMA((na, 3)), pltpu.SemaphoreType.DMA((na, 3))],
        name="chip_exchange",
    )(*sums)


def _chip_sum(own, landed, name):
    hr, cols = own.shape
    tr = _row_tile(hr, cols, _align_of(landed.dtype))

    def body(o_ref, l_ref, f_ref):
        acc = o_ref[...]
        for j in range(3):
            acc = acc + l_ref[j].astype(F32)
        f_ref[...] = acc

    return pl.pallas_call(
        body, grid=(hr // tr,),
        in_specs=[pl.BlockSpec((tr, cols), lambda i: (i, 0)), pl.BlockSpec((3, tr, cols), lambda i: (0, i, 0))],
        out_specs=pl.BlockSpec((tr, cols), lambda i: (i, 0)),
        out_shape=jax.ShapeDtypeStruct((hr, cols), F32), name=name,
        compiler_params=_params(("parallel",)),
    )(own, landed)


def _final_exchange(halves, small):
    nh = len(halves)

    def body(*refs):
        h_refs, s_ref = refs[:nh], refs[nh]
        o_refs, so_ref = refs[nh + 1:2 * nh + 1], refs[2 * nh + 1]
        send_sems, recv_sems, local_sem, ssend_sems, srecv_sems = refs[2 * nh + 2:]
        x, y, c, _ = _place()
        me = 4 * x + 2 * y + c
        sibling = (x, y, 1 - c)
        for hi in range(nh):
            hr, cols = h_refs[hi].shape
            for r0, nr in _row_chunks(hr, cols * 4, SUBLANES):
                _remote(h_refs[hi].at[pl.ds(r0, nr), :], o_refs[hi].at[pl.ds(r0, nr), :],
                        send_sems.at[hi], recv_sems.at[hi], sibling).start()
        small_cps = [pltpu.make_async_copy(s_ref, so_ref.at[me], local_sem)]
        for r in range(1, 8):
            fx, fy, fc = (r >> 2) & 1, (r >> 1) & 1, r & 1
            peer = (1 - x if fx else x, 1 - y if fy else y, 1 - c if fc else c)
            small_cps.append(_remote(s_ref, so_ref.at[me], ssend_sems.at[r - 1], srecv_sems.at[r - 1], peer))
        for cp in small_cps:
            cp.start()
        for hi in range(nh):
            _remote(h_refs[hi], o_refs[hi], send_sems.at[hi], recv_sems.at[hi], sibling).wait()
        for cp in small_cps:
            cp.wait()

    return pl.pallas_call(
        body, in_specs=[ANY] * (nh + 1), out_specs=[ANY] * (nh + 1),
        out_shape=[jax.ShapeDtypeStruct(h.shape, F32) for h in halves]
        + [jax.ShapeDtypeStruct((8, *small.shape), F32)],
        scratch_shapes=[pltpu.SemaphoreType.DMA((nh,)), pltpu.SemaphoreType.DMA((nh,)),
                        pltpu.SemaphoreType.DMA, pltpu.SemaphoreType.DMA((7,)), pltpu.SemaphoreType.DMA((7,))],
        name="final_exchange",
    )(*halves, small)


def _adamw_halves(w, own, other, m, v, place, name):
    r, c = w.shape
    hr = r // 2
    tr = _row_tile(hr, c, SUBLANES, 384 * 1024)
    nt = hr // tr
    c1 = 1.0 - ADAM_B1 ** ADAM_STEP
    c2 = 1.0 - ADAM_B2 ** ADAM_STEP

    def body(p_ref, w_ref, own_ref, other_ref, m_ref, v_ref, g_ref, d_ref, nm_ref, nv_ref):
        mine = pl.program_id(0) // nt == p_ref[0]
        gv = jnp.where(mine, own_ref[...], other_ref[...])
        nm = ADAM_B1 * m_ref[...] + (1.0 - ADAM_B1) * gv
        nv = ADAM_B2 * v_ref[...] + (1.0 - ADAM_B2) * (gv * gv)
        g_ref[...] = gv
        d_ref[...] = -ADAM_LR * ((nm / c1) / (jnp.sqrt(nv / c2) + ADAM_EPS) + ADAM_WD * w_ref[...])
        nm_ref[...] = nm
        nv_ref[...] = nv

    full = pl.BlockSpec((tr, c), lambda i, p: (i, 0))
    own_half = pl.BlockSpec((tr, c), lambda i, p: (jnp.where(i // nt == p[0], i % nt, 0), 0))
    other_half = pl.BlockSpec((tr, c), lambda i, p: (jnp.where(i // nt == p[0], 0, i % nt), 0))
    out = jax.ShapeDtypeStruct((r, c), F32)
    grid_spec = pltpu.PrefetchScalarGridSpec(num_scalar_prefetch=1, grid=(2 * nt,),
                                             in_specs=[full, own_half, other_half, full, full],
                                             out_specs=[full] * 4)
    return pl.pallas_call(body, grid_spec=grid_spec, out_shape=[out] * 4, name=name,
                          compiler_params=_params(("parallel",)))(place, w, own, other, m, v)


def _adamw_many(ws, gs, ms, vs, name):
    n = len(ws)
    c1 = 1.0 - ADAM_B1 ** ADAM_STEP
    c2 = 1.0 - ADAM_B2 ** ADAM_STEP

    def body(*refs):
        w_refs, g_refs, m_refs, v_refs = (refs[k * n:(k + 1) * n] for k in range(4))
        d_refs, nm_refs, nv_refs = (refs[(4 + k) * n:(5 + k) * n] for k in range(3))
        for i in range(n):
            gv = g_refs[i][...]
            nm = ADAM_B1 * m_refs[i][...] + (1.0 - ADAM_B1) * gv
            nv = ADAM_B2 * v_refs[i][...] + (1.0 - ADAM_B2) * (gv * gv)
            d_refs[i][...] = -ADAM_LR * ((nm / c1) / (jnp.sqrt(nv / c2) + ADAM_EPS) + ADAM_WD * w_refs[i][...])
            nm_refs[i][...] = nm
            nv_refs[i][...] = nv

    vmem = pl.BlockSpec(memory_space=pltpu.VMEM)
    shapes = [jax.ShapeDtypeStruct(t.shape, F32) for t in ws]
    outs = pl.pallas_call(body, in_specs=[vmem] * (4 * n), out_specs=[vmem] * (3 * n), out_shape=shapes * 3,
                          name=name, compiler_params=_params(vmem_mb=56))(*ws, *gs, *ms, *vs)
    return outs[:n], outs[n:2 * n], outs[2 * n:]


BIG = ("w_in", "w_glu", "w_out", "w_up", "w_down")
WEIGHTS = ("norm_mix_g", "w_in", "a_re", "a_im", "log_step", "b_re", "b_im", "c_re", "c_im", "d_skip", "w_glu",
           "sink", "norm_attn_g", "norm_ssm_g", "w_out", "norm_ffn_g", "w_up", "conv_w", "conv_b", "w_down",
           "norm_final_g")
SMALL = ("norm_mix_g", "a_re", "a_im", "log_step", "b_re", "b_im", "c_re", "c_im", "d_skip", "sink",
         "norm_attn_g", "norm_ssm_g", "norm_ffn_g", "conv_w", "conv_b", "norm_final_g")
SMALL_ROWS = 48
N_DEV = 8


def _tile_rows(size):
    return -(-size // (SUBLANES * D_MODEL)) * SUBLANES


def _by_owner(name, g):
    if name == "w_up":
        return g
    return g.reshape(4, g.shape[0] // 4, g.shape[1])


def _view(name, t):
    if name == "w_in":
        return jnp.swapaxes(t[0], 0, 1)
    if name in ("b_re", "b_im"):
        return jnp.swapaxes(t, -1, -2)
    return t


def _unview(name, t):
    if name == "w_in":
        return jnp.swapaxes(t, 0, 1)[None]
    if name in ("b_re", "b_im"):
        return jnp.swapaxes(t, -1, -2)
    return t


def kernel(x, norm_mix_g, w_in, a_re, a_im, log_step, b_re, b_im, c_re, c_im, d_skip, w_glu, sink, norm_attn_g, norm_ssm_g, w_out, norm_ffn_g, w_up, conv_w, conv_b, w_down, norm_final_g, loss_target, m_norm_mix_g, m_w_in, m_a_re, m_a_im, m_log_step, m_b_re, m_b_im, m_c_re, m_c_im, m_d_skip, m_w_glu, m_sink, m_norm_attn_g, m_norm_ssm_g, m_w_out, m_norm_ffn_g, m_w_up, m_conv_w, m_conv_b, m_w_down, m_norm_final_g, v_norm_mix_g, v_w_in, v_a_re, v_a_im, v_log_step, v_b_re, v_b_im, v_c_re, v_c_im, v_d_skip, v_w_glu, v_sink, v_norm_attn_g, v_norm_ssm_g, v_w_out, v_norm_ffn_g, v_w_up, v_conv_w, v_conv_b, v_w_down, v_norm_final_g):
    given = dict(locals())
    w = {n: given[n] for n in WEIGHTS}
    m = {n: given["m_" + n] for n in WEIGHTS}
    v = {n: given["v_" + n] for n in WEIGHTS}
    xy = 2 * lax.axis_index("x") + lax.axis_index("y")

    core = lax.axis_index("c")
    place = jnp.stack([core, xy]).astype(jnp.int32)

    conv_rows = jnp.pad(w["conv_w"][0], ((0, 2 * SUBLANES - 3), (0, 0)))
    rows = lambda t: t.reshape(4 * t.shape[1], t.shape[2])
    (w_in_all,) = _gather_weights([_view("w_in", w["w_in"])], [BF16])
    wb = {"w_in": rows(w_in_all)}
    mixer = [_cast_place(w[n][0], place, BF16, w_in_all, "cast_" + n) for n in ("w_glu", "w_out")]
    mixer.append(_cast_place(conv_rows, place, F32, w_in_all, "cast_conv_w"))
    *mixer_flight, mixer_token = _spread_start(mixer, "spread_mixer_start")
    late = ("w_up", "w_down")
    *late_flight, token = _spread_start(
        [_cast_place(w[n][0], place, BF16, mixer_token, "cast_" + n) for n in late], "spread_ffn_start")

    def mixer_weights(after):
        w_glu4, w_out4, conv4 = _spread_wait(*mixer_flight, after, "spread_mixer_wait")
        return {"w_glu": rows(w_glu4), "w_out": rows(w_out4),
                "conv_w": conv4[:, :3].transpose(1, 0, 2).reshape(3, 2 * D_FF)}

    def late_weights(after):
        w_up4, w_down4 = _spread_wait(*late_flight, after, "spread_ffn_wait")
        return {"w_up": w_up4, "w_down": rows(w_down4)}

    sp = {n: w[n][0] for n in ("a_re", "a_im", "log_step", "b_re", "b_im", "c_re", "c_im", "d_skip",
                               "norm_mix_g", "norm_attn_g", "norm_ssm_g", "norm_ffn_g", "sink", "conv_b")}
    for n in ("norm_mix_g", "norm_attn_g", "norm_ssm_g", "norm_ffn_g", "sink", "conv_b"):
        sp[n] = sp[n].reshape(1, -1)
    sp["norm_mix_g"] = sp["norm_mix_g"] + token[:1, :1]
    sp["norm_final_g"] = w["norm_final_g"]
    early, tail = late + ("w_out",), ("w_in", "w_glu")
    flight = {}

    def grads_ready(dw_up, dw_down, dw_out):
        *flight["pair"], token = _pair_start([dw_up, _by_owner("w_down", dw_down), _by_owner("w_out", dw_out)])
        return token[:1, :1]

    def grads_next(after):
        mine, got = _pair_wait(*flight["pair"], after)
        sums, flight["own"] = zip(*[_pair_sum(a, b, place, BF16, "pair_sum_" + n) for n, a, b in zip(early, mine, got)])
        *flight["chip"], token = _chip_start(list(sums))
        return token[:1, :1]

    loss, grad_x, g = _local_step(x[0], loss_target[0], wb, sp, mixer_weights, late_weights, grads_ready,
                                  grads_next)

    def as_rows(t):
        rows = _tile_rows(t.size)
        return jnp.pad(t.reshape(-1), (0, rows * D_MODEL - t.size)).reshape(rows, D_MODEL)

    pieces = [as_rows(g[n]) for n in SMALL] + [as_rows(loss)]
    spare = N_DEV * SMALL_ROWS - sum(p.shape[0] for p in pieces)
    small = jnp.concatenate(pieces + [jnp.zeros((spare, D_MODEL), F32)]).reshape(4, 2 * SMALL_ROWS, D_MODEL)
    by_owner = [_by_owner(n, g[n]) for n in tail] + [small]
    got = _pair_exchange(by_owner)
    transit = [BF16] * len(tail) + [F32]
    chip_sums, own_sums = zip(*[_pair_sum(a, b, place, t, "pair_sum_" + n)
                                for n, a, b, t in zip(tail + ("small",), by_owner, got, transit)])
    landed = _chip_exchange(list(chip_sums))
    halves = {n: _chip_sum(o, t, "chip_sum_" + n) for n, o, t in zip(tail + ("small",), own_sums, landed)}
    early_landed = _chip_wait(*flight["chip"], grad_x)
    for n, o, t in zip(early, flight["own"], early_landed):
        halves[n] = _chip_sum(o, t, "chip_sum_" + n)
    *others, small_all = _final_exchange([halves[n] for n in BIG], halves["small"])
    small_all = small_all.reshape(N_DEV * SMALL_ROWS, D_MODEL)
    grads, row = {}, 0
    for n in SMALL:
        shape = (3, 4 * w[n].shape[-1]) if n == "conv_w" else w[n].shape[1:] if n != "norm_final_g" else w[n].shape
        size = math.prod(shape)
        grads[n] = small_all[row:row + _tile_rows(size)].reshape(-1)[:size].reshape(shape)
        row += _tile_rows(size)
    loss = small_all[row, 0]
    cw = w["conv_w"].shape[-1]
    grads["conv_w"] = lax.dynamic_slice_in_dim(grads["conv_w"], xy * cw, cw, axis=1)
    grads = {n: _view(n, grads[n].reshape(w[n].shape)) for n in SMALL}
    wv, mv, vv = ({n: _view(n, t[n]) for n in WEIGHTS} for t in (w, m, v))

    delta, new_m, new_v = {}, {}, {}
    for n, other in zip(BIG, others):
        two_d = lambda t: t.reshape(t.shape[-2:])
        grads[n], delta[n], new_m[n], new_v[n] = _adamw_halves(
            two_d(wv[n]), halves[n], other, two_d(mv[n]), two_d(vv[n]), place, "adamw_" + n)
    for group, name in ((("b_re", "b_im"), "adamw_b"), (tuple(n for n in SMALL if n not in ("b_re", "b_im")), "adamw_small")):
        row = lambda t: t.reshape(1, -1) if t.ndim == 1 else t
        d_, m_, v_ = _adamw_many(*[[row(t[n]) for n in group] for t in (wv, grads, mv, vv)], name)
        for n, dn, mn, vn in zip(group, d_, m_, v_):
            delta[n], new_m[n], new_v[n] = (t.reshape(wv[n].shape) for t in (dn, mn, vn))
    natural = lambda t: [_unview(n, t[n].reshape(wv[n].shape)) for n in WEIGHTS]
    return (loss, grad_x[None], *natural(grads), *natural(delta), *natural(new_m), *natural(new_v))
```
